```python
import math
import jax, jax.numpy as jnp
from jax import lax
import numpy as np

D_MODEL = 1024
BATCH = 8
SEQ = 2048
DEPTH = 1

D_MIX = D_MODEL
HEAD_DIM = 64
SB_HEADS = 8
SB_WIDTH = SB_HEADS * HEAD_DIM
SG_HEADS = 8
SG_WIDTH = SG_HEADS * HEAD_DIM
D_IN = 3 * SB_WIDTH + 2 * SG_WIDTH
CHUNK = 128
Q_BLOCK = 128
N_GROUPS = 4
EXPERTS_PER_GROUP = 8
TOP_K = 2
D_EXPERT = 512
EPS = 1e-6

kernel_name = "hymba_style_stickbreak_sgmlp_hiermoe"


def rmsnorm(x, g):
    xf = x.astype(jnp.float32)
    y = xf * lax.rsqrt(jnp.mean(xf * xf, axis=-1, keepdims=True) + EPS)
    return (y * g.astype(jnp.float32)).astype(x.dtype)


def stick_breaking_attention(q, k, v):
    B, S, H, Dh = q.shape
    nb = S // Q_BLOCK
    scale = 1.0 / math.sqrt(Dh)
    k = k.transpose(0, 2, 1, 3)
    v = v.transpose(0, 2, 1, 3)
    q_blocks = q.transpose(0, 2, 1, 3).reshape(B, H, nb, Q_BLOCK, Dh).transpose(2, 0, 1, 3, 4)
    key_pos = jnp.arange(S)

    def one_block(args):
        qb, blk = args
        z = jnp.einsum('bhqd,bhkd->bhqk', qb, k).astype(jnp.float32) * scale
        q_pos = blk * Q_BLOCK + jnp.arange(Q_BLOCK)
        mask = key_pos[None, :] < q_pos[:, None]
        log_beta = jax.nn.log_sigmoid(z)
        log_1m_beta = jnp.where(mask, jax.nn.log_sigmoid(-z), 0.0)
        after = lax.cumsum(log_1m_beta, axis=log_1m_beta.ndim - 1, reverse=True) - log_1m_beta
        a = jnp.where(mask, jnp.exp(log_beta + after), 0.0)
        return jnp.einsum('bhqk,bhkd->bhqd', a.astype(v.dtype), v)

    out = lax.map(one_block, (q_blocks, jnp.arange(nb)))
    return out.transpose(1, 0, 3, 2, 4).reshape(B, S, H * Dh)


def chunked_spatial_gating(u, vg, sg_norm_g, w_spatial, b_spatial):
    B, S, _ = u.shape
    nc = S // CHUNK
    u = jax.nn.gelu(u)
    vg = rmsnorm(jax.nn.gelu(vg), sg_norm_g)
    vg = vg.reshape(B, nc, CHUNK, SG_HEADS, HEAD_DIM)
    causal = jnp.tril(jnp.ones((CHUNK, CHUNK), dtype=w_spatial.dtype))
    w_masked = w_spatial * causal[None]
    mixed = jnp.einsum('hts,bcshd->bcthd', w_masked, vg) + b_spatial.T[None, None, :, :, None]
    return u * mixed.reshape(B, S, SG_WIDTH)


def hierarchical_moe(x, w_router_group, b_router_group, w_router_expert, b_router_expert,
                     w_gate, w_up, w_down):
    B, S, D = x.shape
    xt = x.reshape(-1, D)
    n = xt.shape[0]
    g_logits = (xt @ w_router_group).astype(jnp.float32) + b_router_group.astype(jnp.float32)
    g_probs = jax.nn.softmax(g_logits, axis=-1)
    g_idx = jnp.argmax(g_logits, axis=-1)
    g_onehot = jax.nn.one_hot(g_idx, N_GROUPS, dtype=jnp.float32)
    g_weight = jnp.sum(g_probs * g_onehot, axis=-1)
    e_logits_all = jnp.einsum('nd,gde->nge', xt, w_router_expert).astype(jnp.float32) \
        + b_router_expert.astype(jnp.float32)[None]
    e_logits = jnp.sum(e_logits_all * g_onehot[:, :, None], axis=1)
    e_probs = jax.nn.softmax(e_logits, axis=-1)
    top_p, top_i = lax.top_k(e_probs, TOP_K)
    top_p = top_p / jnp.sum(top_p, axis=-1, keepdims=True)
    e_w = jnp.sum(jax.nn.one_hot(top_i, EXPERTS_PER_GROUP, dtype=jnp.float32) * top_p[..., None], axis=1)
    combine = (g_onehot[:, :, None] * (g_weight[:, None] * e_w)[:, None, :]).astype(x.dtype)
    out = jnp.zeros((n, D), dtype=x.dtype)
    for g in range(N_GROUPS):
        h = jax.nn.silu(jnp.einsum('nd,edf->nef', xt, w_gate[g])) * jnp.einsum('nd,edf->nef', xt, w_up[g])
        out = out + jnp.einsum('nef,efd->nd', h * combine[:, g, :, None], w_down[g])
    return out.reshape(B, S, D)


def setup_inputs(seed: int = 0) -> dict:
    key = jax.random.key(seed)
    ks = jax.random.split(key, 20)
    f32 = jnp.float32
    nrm = lambda k, shape, s: jax.random.normal(k, shape, f32) * s
    gain = lambda k, shape: 1.0 + 0.02 * jax.random.normal(k, shape, f32)
    L, G, E = DEPTH, N_GROUPS, EXPERTS_PER_GROUP
    return {
        "x": jax.random.normal(ks[0], (BATCH, SEQ, D_MODEL), f32),
        "attn_norm_g": gain(ks[1], (L, D_MODEL)),
        "w_in": nrm(ks[2], (L, D_MODEL, D_IN), D_MODEL ** -0.5),
        "sg_norm_g": gain(ks[3], (L, SG_WIDTH)),
        "w_spatial": nrm(ks[4], (L, SG_HEADS, CHUNK, CHUNK), CHUNK ** -0.5),
        "b_spatial": gain(ks[5], (L, SG_HEADS, CHUNK)),
        "sb_out_norm_g": gain(ks[6], (L, SB_WIDTH)),
        "sg_out_norm_g": gain(ks[7], (L, SG_WIDTH)),
        "w_out": nrm(ks[8], (L, D_MIX, D_MODEL), D_MIX ** -0.5),
        "ffn_norm_g": gain(ks[9], (L, D_MODEL)),
        "w_router_group": nrm(ks[10], (L, D_MODEL, G), D_MODEL ** -0.5),
        "b_router_group": nrm(ks[11], (L, G), 0.01),
        "w_router_expert": nrm(ks[12], (L, G, D_MODEL, E), D_MODEL ** -0.5),
        "b_router_expert": nrm(ks[13], (L, G, E), 0.01),
        "w_gate": nrm(ks[14], (L, G, E, D_MODEL, D_EXPERT), D_MODEL ** -0.5),
        "w_up": nrm(ks[15], (L, G, E, D_MODEL, D_EXPERT), D_MODEL ** -0.5),
        "w_down": nrm(ks[16], (L, G, E, D_EXPERT, D_MODEL), D_EXPERT ** -0.5),
        "final_norm_g": gain(ks[17], (D_MODEL,)),
    }


def reference(x, attn_norm_g, w_in, sg_norm_g, w_spatial, b_spatial, sb_out_norm_g, sg_out_norm_g,
              w_out, ffn_norm_g, w_router_group, b_router_group, w_router_expert, b_router_expert,
              w_gate, w_up, w_down, final_norm_g):
    B, S, _ = x.shape
    h = x
    for layer in range(DEPTH):
        hn = rmsnorm(h, attn_norm_g[layer])
        proj = hn @ w_in[layer]
        q, k, v, u, vg = jnp.split(proj, [SB_WIDTH, 2 * SB_WIDTH, 3 * SB_WIDTH, 3 * SB_WIDTH + SG_WIDTH], axis=-1)
        shp = (B, S, SB_HEADS, HEAD_DIM)
        sb_out = stick_breaking_attention(q.reshape(shp), k.reshape(shp), v.reshape(shp))
        sg_out = chunked_spatial_gating(u, vg, sg_norm_g[layer], w_spatial[layer], b_spatial[layer])
        mixed = jnp.concatenate([rmsnorm(sb_out, sb_out_norm_g[layer]),
                                 rmsnorm(sg_out, sg_out_norm_g[layer])], axis=-1)
        h = h + mixed @ w_out[layer]
        hn = rmsnorm(h, ffn_norm_g[layer])
        h = h + hierarchical_moe(hn, w_router_group[layer], b_router_group[layer],
                                 w_router_expert[layer], b_router_expert[layer],
                                 w_gate[layer], w_up[layer], w_down[layer])
    return rmsnorm(h, final_norm_g)
```

```python
import functools
import math

import jax
import jax.numpy as jnp
from jax import lax
from jax.experimental import pallas as pl
from jax.experimental.pallas import tpu as pltpu

D_MODEL = 1024
HEAD_DIM = 64
SB_WIDTH = 512
SG_WIDTH = 512
SG_HEADS = 8
D_IN = 3 * SB_WIDTH + 2 * SG_WIDTH
CHUNK = 128
N_GROUPS = 4
EXPERTS_PER_GROUP = 8
N_EXPERTS = N_GROUPS * EXPERTS_PER_GROUP
D_EXPERT = 512
EPS = 1e-6

LANES = 128
ROW_TILE = D_MODEL // LANES
assert ROW_TILE == 8
HEAD_PAIR = 2 * HEAD_DIM
ROUTER_LANE0 = N_GROUPS

TM_PROJ = 512
TQ_ATTN = 256
TM_MIX = 512
TM_DISPATCH = 512
TM_EXPERT = 256
TM_COMBINE = 256
VMEM_LIMIT = 48 * 1024 * 1024

F32 = jnp.float32
BF16 = jnp.bfloat16


def _rms(x, g):
    return x * lax.rsqrt(jnp.mean(x * x, axis=-1, keepdims=True) + EPS) * g


def _gelu(x):
    c = math.sqrt(2.0 / math.pi)
    return x * (0.5 * (1.0 + jnp.tanh(c * (x + 0.044715 * (x * x * x)))))


def _softplus(z):
    return jnp.maximum(z, 0.0) + jnp.log1p(jnp.exp(-jnp.abs(z)))


def _dot(a, b):
    return jnp.dot(a, b, preferred_element_type=F32)


def _rows_to_tiles(ref, x):
    m = x.shape[0]
    for k in range(ROW_TILE):
        ref[pl.ds(k, m, stride=ROW_TILE), :] = x[:, k * LANES:(k + 1) * LANES]


def _tiles_to_rows(ref, m):
    return jnp.concatenate([ref[pl.ds(k, m, stride=ROW_TILE), :] for k in range(ROW_TILE)], axis=1)


def _token_rows(ref, first_token, n_tokens):
    return ref.at[pl.ds(pl.multiple_of(first_token * ROW_TILE, ROW_TILE), n_tokens * ROW_TILE)]


def _split_bf16(x):
    hi = x.astype(BF16)
    lo = (x - hi.astype(F32)).astype(BF16)
    return hi, lo


def _inproj_kernel(x_ref, g_ref, w_ref, sgg_ref, qkv_ref, gu_ref, vgn_ref):
    hb = _rms(x_ref[...], g_ref[...]).astype(BF16)
    q = _dot(hb, w_ref[:, 0:SB_WIDTH]) * (1.0 / math.sqrt(HEAD_DIM))
    qkv_ref[:, 0:SB_WIDTH] = q.astype(BF16)
    qkv_ref[:, SB_WIDTH:3 * SB_WIDTH] = _dot(hb, w_ref[:, SB_WIDTH:3 * SB_WIDTH]).astype(BF16)
    gu_ref[...] = _gelu(_dot(hb, w_ref[:, 3 * SB_WIDTH:3 * SB_WIDTH + SG_WIDTH]))
    gv = _gelu(_dot(hb, w_ref[:, 3 * SB_WIDTH + SG_WIDTH:D_IN]))
    vgn_ref[...] = _rms(gv, sgg_ref[...]).astype(BF16)


def _inproj(x2, attn_g, w_in_b, sg_g):
    n = x2.shape[0]
    row = lambda i: (i, 0)
    const = lambda i: (0, 0)
    return pl.pallas_call(
        _inproj_kernel,
        grid=(n // TM_PROJ,),
        in_specs=[pl.BlockSpec((TM_PROJ, D_MODEL), row),
                  pl.BlockSpec((1, D_MODEL), const),
                  pl.BlockSpec((D_MODEL, D_IN), const),
                  pl.BlockSpec((1, SG_WIDTH), const)],
        out_specs=[pl.BlockSpec((TM_PROJ, 3 * SB_WIDTH), row),
                   pl.BlockSpec((TM_PROJ, SG_WIDTH), row),
                   pl.BlockSpec((TM_PROJ, SG_WIDTH), row)],
        out_shape=[jax.ShapeDtypeStruct((n, 3 * SB_WIDTH), BF16),
                   jax.ShapeDtypeStruct((n, SG_WIDTH), F32),
                   jax.ShapeDtypeStruct((n, SG_WIDTH), BF16)],
        compiler_params=pltpu.CompilerParams(dimension_semantics=("arbitrary",),
                                             vmem_limit_bytes=VMEM_LIMIT),
        name="inproj",
    )(x2, attn_g, w_in_b, sg_g)


def _attn_kernel(q_ref, k_ref, v_ref, o_ref, acc_ref, carry_ref):
    t = TQ_ATTN
    qi = pl.program_id(2)
    q = q_ref[0]
    lane = lax.broadcasted_iota(jnp.int32, (1, HEAD_PAIR), 1)
    head_lanes = (lane < HEAD_DIM, lane >= HEAD_DIM)
    zero = jnp.zeros((), BF16)
    q_heads = tuple(jnp.where(m, q, zero) for m in head_lanes)
    r_idx = lax.broadcasted_iota(jnp.int32, (t, t), 0)
    c_idx = lax.broadcasted_iota(jnp.int32, (t, t), 1)
    suffix = (r_idx > c_idx).astype(BF16)
    causal = c_idx < r_idx

    acc_ref[...] = jnp.zeros_like(acc_ref)
    carry_ref[...] = jnp.zeros_like(carry_ref)

    def block(j, diag):
        start = pl.multiple_of(j * t, t)
        kb = k_ref[0, pl.ds(start, t), :]
        vb = v_ref[0, pl.ds(start, t), :]
        acc = acc_ref[...]
        for h in range(2):
            z = lax.dot_general(q_heads[h], kb, (((1,), (1,)), ((), ())), preferred_element_type=F32)
            sp = _softplus(z)
            lm = -sp
            if diag:
                lm = jnp.where(causal, lm, 0.0)
            hi, lo = _split_bf16(lm)
            after = _dot(hi, suffix) + _dot(lo, suffix)
            carry = carry_ref[h]
            a = jnp.exp((z - sp) + after + carry)
            if diag:
                a = jnp.where(causal, a, 0.0)
            vh = jnp.where(head_lanes[h], vb, zero)
            acc = acc + _dot(a.astype(BF16), vh)
            carry_ref[h] = carry + after[:, 0:1] + lm[:, 0:1]
        acc_ref[...] = acc

    block(qi, True)

    def body(it, c):
        block(qi - 1 - it, False)
        return c

    lax.fori_loop(0, qi, body, 0)
    o_ref[0] = acc_ref[...]


def _attention(qkv, batch, seq):
    qkv3 = qkv.reshape(batch, seq, 3 * SB_WIDTH)
    n_pairs = SB_WIDTH // HEAD_PAIR
    return pl.pallas_call(
        _attn_kernel,
        grid=(batch, n_pairs, seq // TQ_ATTN),
        in_specs=[pl.BlockSpec((1, TQ_ATTN, HEAD_PAIR), lambda b, p, i: (b, i, p)),
                  pl.BlockSpec((1, seq, HEAD_PAIR), lambda b, p, i: (b, 0, n_pairs + p)),
                  pl.BlockSpec((1, seq, HEAD_PAIR), lambda b, p, i: (b, 0, 2 * n_pairs + p))],
        out_specs=pl.BlockSpec((1, TQ_ATTN, HEAD_PAIR), lambda b, p, i: (b, i, p)),
        out_shape=jax.ShapeDtypeStruct((batch, seq, SB_WIDTH), F32),
        scratch_shapes=[pltpu.VMEM((TQ_ATTN, HEAD_PAIR), F32),
                        pltpu.VMEM((2, TQ_ATTN, 1), F32)],
        compiler_params=pltpu.CompilerParams(dimension_semantics=("arbitrary",) * 3,
                                             vmem_limit_bytes=VMEM_LIMIT),
        name="sb_attention",
    )(qkv3, qkv3, qkv3)


def _mix_kernel(sb_ref, gu_ref, vgn_ref, x_ref, wsp_ref, bsp_ref, sbg_ref, sgg_ref, wout_ref,
                ffng_ref, wrh_ref, wrl_ref, br_ref,
                h_ref, hn_ref, ri_ref, rw_ref, cnt_ref, count_ref, sg_ref):
    tm = TM_MIX
    i = pl.program_id(0)

    @pl.when(i == 0)
    def _():
        count_ref[...] = jnp.zeros_like(count_ref)

    lane = lax.broadcasted_iota(jnp.int32, (1, LANES), 1)
    first = lane < HEAD_DIM
    zero = jnp.zeros((), BF16)
    r_c = lax.broadcasted_iota(jnp.int32, (CHUNK, CHUNK), 0)
    c_c = lax.broadcasted_iota(jnp.int32, (CHUNK, CHUNK), 1)
    tril = r_c >= c_c
    n_pairs = SG_WIDTH // HEAD_PAIR
    w_pairs = []
    for p in range(n_pairs):
        w0 = jnp.where(tril, wsp_ref[2 * p], 0.0).astype(BF16)
        w1 = jnp.where(tril, wsp_ref[2 * p + 1], 0.0).astype(BF16)
        w_pairs.append(jnp.concatenate([w0, w1], axis=1))
    bsp = bsp_ref[...]
    for c in range(tm // CHUNK):
        rows = slice(c * CHUNK, (c + 1) * CHUNK)
        for p in range(n_pairs):
            cols = slice(p * HEAD_PAIR, (p + 1) * HEAD_PAIR)
            vg = vgn_ref[rows, cols]
            rhs = jnp.concatenate([jnp.where(first, vg, zero), jnp.where(first, zero, vg)], axis=0)
            mixed = _dot(w_pairs[p], rhs) + bsp[:, cols]
            sg_ref[rows, cols] = gu_ref[rows, cols] * mixed
    sgn = _rms(sg_ref[...], sgg_ref[...]).astype(BF16)
    sbn = _rms(sb_ref[...], sbg_ref[...]).astype(BF16)
    h = x_ref[...] + _dot(sbn, wout_ref[0:SB_WIDTH, :]) + _dot(sgn, wout_ref[SB_WIDTH:, :])
    h_ref[...] = h
    hn = _rms(h, ffng_ref[...])
    _rows_to_tiles(hn_ref, hn)

    hn_hi, hn_lo = _split_bf16(hn)
    wrh = wrh_ref[...]
    logits = _dot(hn_hi, wrh) + _dot(hn_lo, wrh) + _dot(hn_hi, wrl_ref[...]) + br_ref[...]

    lane_t = lax.broadcasted_iota(jnp.int32, (tm, LANES), 1)
    neg = jnp.float32(-jnp.inf)
    gl = jnp.where(lane_t < N_GROUPS, logits, neg)
    gmax = jnp.max(gl, axis=-1, keepdims=True)
    gidx = jnp.min(jnp.where(gl == gmax, lane_t, LANES), axis=-1, keepdims=True)
    gsum = jnp.sum(jnp.exp(gl - gmax), axis=-1, keepdims=True)
    gweight = 1.0 / gsum
    lo_lane = ROUTER_LANE0 + EXPERTS_PER_GROUP * gidx
    el = jnp.where((lane_t >= lo_lane) & (lane_t < lo_lane + EXPERTS_PER_GROUP), logits, neg)
    m1 = jnp.max(el, axis=-1, keepdims=True)
    i1 = jnp.min(jnp.where(el == m1, lane_t, LANES), axis=-1, keepdims=True)
    el2 = jnp.where(lane_t == i1, neg, el)
    m2 = jnp.max(el2, axis=-1, keepdims=True)
    i2 = jnp.min(jnp.where(el2 == m2, lane_t, LANES), axis=-1, keepdims=True)
    t21 = jnp.exp(m2 - m1)
    w1 = gweight / (1.0 + t21)
    w2 = gweight * t21 / (1.0 + t21)

    sel1 = lane_t == i1
    sel2 = lane_t == i2
    onehot = jnp.where(sel1 | sel2, 1.0, 0.0)
    r_t = lax.broadcasted_iota(jnp.int32, (tm, tm), 0)
    c_t = lax.broadcasted_iota(jnp.int32, (tm, tm), 1)
    before = (r_t > c_t).astype(BF16)
    running = count_ref[0:1, :] + _dot(before, onehot.astype(BF16))
    rank1 = jnp.sum(jnp.where(sel1, running, 0.0), axis=-1, keepdims=True)
    rank2 = jnp.sum(jnp.where(sel2, running, 0.0), axis=-1, keepdims=True)
    new_count = count_ref[0:1, :] + jnp.sum(onehot, axis=0, keepdims=True)
    count_ref[...] = jnp.broadcast_to(new_count, count_ref.shape)
    cnt_ref[...] = jnp.broadcast_to(new_count, cnt_ref.shape)

    e1 = i1 - ROUTER_LANE0
    e2 = i2 - ROUTER_LANE0
    ri = jnp.where(lane_t == 0, e1, jnp.where(lane_t == 1, e2, jnp.where(
        lane_t == 2, rank1.astype(jnp.int32), jnp.where(lane_t == 3, rank2.astype(jnp.int32), 0))))
    ri_ref[...] = ri
    rw_ref[...] = jnp.where(lane_t == 0, w1, jnp.where(lane_t == 1, w2, 0.0))


def _mix(sb, gu, vgn, x2, wsp, bsp_full, sb_g, sg_g, w_out_b, ffn_g, wr_hi, wr_lo, br):
    n = x2.shape[0]
    row = lambda i: (i, 0)
    const = lambda i: (0, 0)
    return pl.pallas_call(
        _mix_kernel,
        grid=(n // TM_MIX,),
        in_specs=[pl.BlockSpec((TM_MIX, SB_WIDTH), row),
                  pl.BlockSpec((TM_MIX, SG_WIDTH), row),
                  pl.BlockSpec((TM_MIX, SG_WIDTH), row),
                  pl.BlockSpec((TM_MIX, D_MODEL), row),
                  pl.BlockSpec((SG_HEADS, CHUNK, CHUNK), lambda i: (0, 0, 0)),
                  pl.BlockSpec((CHUNK, SG_WIDTH), const),
                  pl.BlockSpec((1, SB_WIDTH), const),
                  pl.BlockSpec((1, SG_WIDTH), const),
                  pl.BlockSpec((D_MODEL, D_MODEL), const),
                  pl.BlockSpec((1, D_MODEL), const),
                  pl.BlockSpec((D_MODEL, LANES), const),
                  pl.BlockSpec((D_MODEL, LANES), const),
                  pl.BlockSpec((1, LANES), const)],
        out_specs=[pl.BlockSpec((TM_MIX, D_MODEL), row),
                   pl.BlockSpec((TM_MIX * ROW_TILE, LANES), row),
                   pl.BlockSpec((TM_MIX, LANES), row),
                   pl.BlockSpec((TM_MIX, LANES), row),
                   pl.BlockSpec((8, LANES), const)],
        out_shape=[jax.ShapeDtypeStruct((n, D_MODEL), F32),
                   jax.ShapeDtypeStruct((n * ROW_TILE, LANES), F32),
                   jax.ShapeDtypeStruct((n, LANES), jnp.int32),
                   jax.ShapeDtypeStruct((n, LANES), F32),
                   jax.ShapeDtypeStruct((8, LANES), F32)],
        scratch_shapes=[pltpu.VMEM((8, LANES), F32),
                        pltpu.VMEM((TM_MIX, SG_WIDTH), F32)],
        compiler_params=pltpu.CompilerParams(dimension_semantics=("arbitrary",),
                                             vmem_limit_bytes=VMEM_LIMIT),
        name="mix_router",
    )(sb, gu, vgn, x2, wsp, bsp_full, sb_g, sg_g, w_out_b, ffn_g, wr_hi, wr_lo, br)


_PAD_BITS = tuple(1 << b for b in reversed(range(TM_EXPERT.bit_length() - 1)))


def _dispatch_kernel(dest_ref, pad_start_ref, pad_count_ref, nt_ref, hn_ref, zeros_ref, xs_ref, sem, zsem):
    tm = TM_DISPATCH
    i = pl.program_id(0)
    base = i * (2 * tm)
    n_tiles_max = xs_ref.shape[0] // (TM_EXPERT * ROW_TILE)

    def pad_copies(do):
        for e in range(N_EXPERTS):
            start = pad_start_ref[e]
            count = pad_count_ref[e]
            for bit in _PAD_BITS:
                @pl.when((count & bit) != 0)
                def _(start=start, bit=bit):
                    do(pltpu.make_async_copy(_token_rows(zeros_ref, 0, bit),
                                             _token_rows(xs_ref, start, bit), zsem))
                start = start + (count & bit)
        for k in range(N_EXPERTS):
            tile = nt_ref[0] + k

            @pl.when(tile < n_tiles_max)
            def _(tile=tile):
                do(pltpu.make_async_copy(zeros_ref, _token_rows(xs_ref, tile * TM_EXPERT, TM_EXPERT), zsem))

    @pl.when(i == 0)
    def _():
        pad_copies(lambda cp: cp.start())

    def body(r, c):
        src = _token_rows(hn_ref, i * tm + r, 1)
        for s in range(2):
            pltpu.make_async_copy(src, _token_rows(xs_ref, dest_ref[base + 2 * r + s], 1), sem).start()
        return c

    lax.fori_loop(0, tm, body, 0, unroll=8)
    for _ in range(2):
        pltpu.make_async_copy(_token_rows(hn_ref, 0, tm), _token_rows(xs_ref, 0, tm), sem).wait()

    @pl.when(i == 0)
    def _():
        pad_copies(lambda cp: cp.wait())


def _dispatch(dest, pad_start, pad_count, n_tiles, hn_tiles, n_rows):
    n = hn_tiles.shape[0] // ROW_TILE
    zeros = jnp.zeros((TM_EXPERT * ROW_TILE, LANES), F32)
    return pl.pallas_call(
        _dispatch_kernel,
        grid_spec=pltpu.PrefetchScalarGridSpec(
            num_scalar_prefetch=4,
            grid=(n // TM_DISPATCH,),
            in_specs=[pl.BlockSpec(memory_space=pl.ANY),
                      pl.BlockSpec(memory_space=pl.ANY)],
            out_specs=pl.BlockSpec(memory_space=pl.ANY),
            scratch_shapes=[pltpu.SemaphoreType.DMA, pltpu.SemaphoreType.DMA]),
        out_shape=jax.ShapeDtypeStruct((n_rows * ROW_TILE, LANES), F32),
        compiler_params=pltpu.CompilerParams(dimension_semantics=("arbitrary",),
                                             vmem_limit_bytes=VMEM_LIMIT),
        name="dispatch",
    )(dest, pad_start, pad_count, n_tiles, hn_tiles, zeros)


def _expert_kernel(te_ref, tf_ref, nt_ref, x_ref, wg_ref, wu_ref, wd_ref, y_ref, wgb, wub, wdb):
    t = pl.program_id(0)

    @pl.when(t < nt_ref[0])
    def _():
        @pl.when(tf_ref[t] != 0)
        def _():
            wgb[...] = wg_ref[0].astype(BF16)
            wub[...] = wu_ref[0].astype(BF16)
            wdb[...] = wd_ref[0].astype(BF16)

        x = _tiles_to_rows(x_ref, TM_EXPERT).astype(BF16)
        g = _dot(x, wgb[...])
        u = _dot(x, wub[...])
        hidden = (g * jax.nn.sigmoid(g)) * u
        _rows_to_tiles(y_ref, _dot(hidden.astype(BF16), wdb[...]))

    @pl.when(t >= nt_ref[0])
    def _():
        y_ref[...] = jnp.zeros_like(y_ref)


def _experts(tile_expert, tile_first, n_tiles, xs, wg, wu, wd):
    n_rows = xs.shape[0] // ROW_TILE
    last = lambda t, nt: jnp.minimum(t, nt[0] - 1)
    return pl.pallas_call(
        _expert_kernel,
        grid_spec=pltpu.PrefetchScalarGridSpec(
            num_scalar_prefetch=3,
            grid=(n_rows // TM_EXPERT,),
            in_specs=[pl.BlockSpec((TM_EXPERT * ROW_TILE, LANES), lambda t, te, tf, nt: (last(t, nt), 0)),
                      pl.BlockSpec((1, D_MODEL, D_EXPERT), lambda t, te, tf, nt: (te[t], 0, 0)),
                      pl.BlockSpec((1, D_MODEL, D_EXPERT), lambda t, te, tf, nt: (te[t], 0, 0)),
                      pl.BlockSpec((1, D_EXPERT, D_MODEL), lambda t, te, tf, nt: (te[t], 0, 0))],
            out_specs=pl.BlockSpec((TM_EXPERT * ROW_TILE, LANES), lambda t, te, tf, nt: (t, 0)),
            scratch_shapes=[pltpu.VMEM((D_MODEL, D_EXPERT), BF16),
                            pltpu.VMEM((D_MODEL, D_EXPERT), BF16),
                            pltpu.VMEM((D_EXPERT, D_MODEL), BF16)]),
        out_shape=jax.ShapeDtypeStruct((n_rows * ROW_TILE, LANES), F32),
        compiler_params=pltpu.CompilerParams(dimension_semantics=("arbitrary",),
                                             vmem_limit_bytes=VMEM_LIMIT),
        name="expert_mlp",
    )(tile_expert, tile_first, n_tiles, xs, wg, wu, wd)


def _combine_kernel(dest_ref, h_ref, rw_ref, fg_ref, y_ref, o_ref, buf, sem):
    tm = TM_COMBINE
    i = pl.program_id(0)
    base = i * (2 * tm)

    def body(r, c):
        for s in range(2):
            pltpu.make_async_copy(_token_rows(y_ref, dest_ref[base + 2 * r + s], 1),
                                  _token_rows(buf.at[s], r, 1), sem).start()
        return c

    lax.fori_loop(0, tm, body, 0, unroll=8)
    for s in range(2):
        pltpu.make_async_copy(_token_rows(y_ref, 0, tm), buf.at[s], sem).wait()
    rw = rw_ref[...]
    out = (h_ref[...] + rw[:, 0:1] * _tiles_to_rows(buf.at[0], tm)
           + rw[:, 1:2] * _tiles_to_rows(buf.at[1], tm))
    o_ref[...] = _rms(out, fg_ref[...])


def _combine(dest, h, rw, final_g, ys):
    n = h.shape[0]
    return pl.pallas_call(
        _combine_kernel,
        grid_spec=pltpu.PrefetchScalarGridSpec(
            num_scalar_prefetch=1,
            grid=(n // TM_COMBINE,),
            in_specs=[pl.BlockSpec((TM_COMBINE, D_MODEL), lambda i, d: (i, 0)),
                      pl.BlockSpec((TM_COMBINE, LANES), lambda i, d: (i, 0)),
                      pl.BlockSpec((1, D_MODEL), lambda i, d: (0, 0)),
                      pl.BlockSpec(memory_space=pl.ANY)],
            out_specs=pl.BlockSpec((TM_COMBINE, D_MODEL), lambda i, d: (i, 0)),
            scratch_shapes=[pltpu.VMEM((2, TM_COMBINE * ROW_TILE, LANES), F32),
                            pltpu.SemaphoreType.DMA]),
        out_shape=jax.ShapeDtypeStruct((n, D_MODEL), F32),
        compiler_params=pltpu.CompilerParams(dimension_semantics=("arbitrary",),
                                             vmem_limit_bytes=VMEM_LIMIT),
        name="combine",
    )(dest, h, rw, final_g, ys)


def _schedule(counts, n_tiles_max):
    tiles = (counts + TM_EXPERT - 1) // TM_EXPERT
    tile_end = jnp.cumsum(tiles)
    offsets = (tile_end - tiles) * TM_EXPERT
    n_tiles = tile_end[-1]
    t = jnp.arange(n_tiles_max, dtype=jnp.int32)
    t_clamped = jnp.minimum(t, n_tiles - 1)
    tile_expert = jnp.sum(t_clamped[:, None] >= tile_end[None, :], axis=1).astype(jnp.int32)
    tile_first = (t_clamped == (tile_end - tiles)[tile_expert]).astype(jnp.int32)
    return offsets, tile_expert, tile_first, n_tiles.reshape(1).astype(jnp.int32)


def _layer(x, attn_g, w_in, sg_g, w_sp, b_sp, sb_g, sg_out_g, w_out, ffn_g,
           w_rg, b_rg, w_re, b_re, w_gate, w_up, w_down):
    batch, seq, _ = x.shape
    n = batch * seq
    x2 = x.reshape(n, D_MODEL)
    row = lambda v: v.reshape(1, -1)

    qkv, gu, vgn = _inproj(x2, row(attn_g), w_in.astype(BF16), row(sg_g))
    sb = _attention(qkv, batch, seq).reshape(n, SB_WIDTH)

    w_r = jnp.concatenate([w_rg, jnp.transpose(w_re, (1, 0, 2)).reshape(D_MODEL, N_EXPERTS)], axis=1)
    w_r = jnp.pad(w_r, ((0, 0), (0, LANES - w_r.shape[1])))
    wr_hi = w_r.astype(BF16)
    wr_lo = (w_r - wr_hi.astype(F32)).astype(BF16)
    b_r = jnp.pad(jnp.concatenate([b_rg, b_re.reshape(-1)]), (0, LANES - N_GROUPS - N_EXPERTS))
    bsp_full = jnp.repeat(b_sp.T, HEAD_DIM, axis=1)

    h, hn, ri, rw, cnt = _mix(sb, gu, vgn, x2, w_sp, bsp_full, row(sb_g), row(sg_out_g),
                              w_out.astype(BF16), row(ffn_g), wr_hi, wr_lo, row(b_r))

    counts = cnt[0, ROUTER_LANE0:ROUTER_LANE0 + N_EXPERTS].astype(jnp.int32)
    n_rows = 2 * n + N_EXPERTS * TM_EXPERT
    offsets, tile_expert, tile_first, n_tiles = _schedule(counts, n_rows // TM_EXPERT)
    dest = (offsets[ri[:, 0:2]] + ri[:, 2:4]).reshape(-1).astype(jnp.int32)
    pad_start = (offsets + counts).astype(jnp.int32)
    pad_count = ((-counts) % TM_EXPERT).astype(jnp.int32)

    xs = _dispatch(dest, pad_start, pad_count, n_tiles, hn, n_rows)
    ys = _experts(tile_expert, tile_first, n_tiles, xs,
                  w_gate.reshape(N_EXPERTS, D_MODEL, D_EXPERT),
                  w_up.reshape(N_EXPERTS, D_MODEL, D_EXPERT),
                  w_down.reshape(N_EXPERTS, D_EXPERT, D_MODEL))
    return dest, h, rw, ys


def kernel(x, attn_norm_g, w_in, sg_norm_g, w_spatial, b_spatial, sb_out_norm_g, sg_out_norm_g,
           w_out, ffn_norm_g, w_router_group, b_router_group, w_router_expert, b_router_expert,
           w_gate, w_up, w_down, final_norm_g):
    assert attn_norm_g.shape[0] == 1, "single-layer problem"
    batch, seq, _ = x.shape
    dest, h, rw, ys = _layer(x, attn_norm_g[0], w_in[0], sg_norm_g[0], w_spatial[0], b_spatial[0],
                             sb_out_norm_g[0], sg_out_norm_g[0], w_out[0], ffn_norm_g[0],
                             w_router_group[0], b_router_group[0], w_router_expert[0],
                             b_router_expert[0], w_gate[0], w_up[0], w_down[0])
    out = _combine(dest, h, rw, final_norm_g.reshape(1, -1), ys)
    return out.reshape(batch, seq, D_MODEL)
```

```python
import functools
import math

import jax
import jax.numpy as jnp
from jax import lax
from jax.experimental import pallas as pl
from jax.experimental.pallas import tpu as pltpu

D_MODEL = 1024
HEAD_DIM = 64
SB_WIDTH = 512
SG_WIDTH = 512
SG_HEADS = 8
D_IN = 3 * SB_WIDTH + 2 * SG_WIDTH
CHUNK = 128
N_GROUPS = 4
EXPERTS_PER_GROUP = 8
N_EXPERTS = N_GROUPS * EXPERTS_PER_GROUP
D_EXPERT = 512
EPS = 1e-6

LANES = 128
ROW_TILE = D_MODEL // LANES
assert ROW_TILE == 8
HEAD_PAIR = 2 * HEAD_DIM
ROUTER_LANE0 = N_GROUPS

TM_PROJ = 512
TQ_ATTN = 256
TM_MIX = 512
TM_DISPATCH = 512
TM_EXPERT = 256
TM_COMBINE = 256
VMEM_LIMIT = 48 * 1024 * 1024

F32 = jnp.float32
BF16 = jnp.bfloat16


def _rms(x, g):
    return x * lax.rsqrt(jnp.mean(x * x, axis=-1, keepdims=True) + EPS) * g


def _gelu(x):
    c = math.sqrt(2.0 / math.pi)
    return x * (0.5 * (1.0 + jnp.tanh(c * (x + 0.044715 * (x * x * x)))))


def _softplus(z):
    return jnp.maximum(z, 0.0) + jnp.log1p(jnp.exp(-jnp.abs(z)))


def _dot(a, b):
    return jnp.dot(a, b, preferred_element_type=F32)


def _rows_to_tiles(ref, x):
    m = x.shape[0]
    for k in range(ROW_TILE):
        ref[pl.ds(k, m, stride=ROW_TILE), :] = x[:, k * LANES:(k + 1) * LANES]


def _tiles_to_rows(ref, m):
    return jnp.concatenate([ref[pl.ds(k, m, stride=ROW_TILE), :] for k in range(ROW_TILE)], axis=1)


def _token_rows(ref, first_token, n_tokens):
    return ref.at[pl.ds(pl.multiple_of(first_token * ROW_TILE, ROW_TILE), n_tokens * ROW_TILE)]


def _split_bf16(x):
    hi = x.astype(BF16)
    lo = (x - hi.astype(F32)).astype(BF16)
    return hi, lo


def _inproj_kernel(x_ref, g_ref, w_ref, sgg_ref, qkv_ref, gu_ref, vgn_ref):
    hb = _rms(x_ref[...], g_ref[...]).astype(BF16)
    q = _dot(hb, w_ref[:, 0:SB_WIDTH]) * (1.0 / math.sqrt(HEAD_DIM))
    qkv_ref[:, 0:SB_WIDTH] = q.astype(BF16)
    qkv_ref[:, SB_WIDTH:3 * SB_WIDTH] = _dot(hb, w_ref[:, SB_WIDTH:3 * SB_WIDTH]).astype(BF16)
    gu_ref[...] = _gelu(_dot(hb, w_ref[:, 3 * SB_WIDTH:3 * SB_WIDTH + SG_WIDTH]))
    gv = _gelu(_dot(hb, w_ref[:, 3 * SB_WIDTH + SG_WIDTH:D_IN]))
    vgn_ref[...] = _rms(gv, sgg_ref[...]).astype(BF16)


def _inproj(x2, attn_g, w_in_b, sg_g):
    n = x2.shape[0]
    row = lambda i: (i, 0)
    const = lambda i: (0, 0)
    return pl.pallas_call(
        _inproj_kernel,
        grid=(n // TM_PROJ,),
        in_specs=[pl.BlockSpec((TM_PROJ, D_MODEL), row),
                  pl.BlockSpec((1, D_MODEL), const),
                  pl.BlockSpec((D_MODEL, D_IN), const),
                  pl.BlockSpec((1, SG_WIDTH), const)],
        out_specs=[pl.BlockSpec((TM_PROJ, 3 * SB_WIDTH), row),
                   pl.BlockSpec((TM_PROJ, SG_WIDTH), row),
                   pl.BlockSpec((TM_PROJ, SG_WIDTH), row)],
        out_shape=[jax.ShapeDtypeStruct((n, 3 * SB_WIDTH), BF16),
                   jax.ShapeDtypeStruct((n, SG_WIDTH), F32),
                   jax.ShapeDtypeStruct((n, SG_WIDTH), BF16)],
        compiler_params=pltpu.CompilerParams(dimension_semantics=("arbitrary",),
                                             vmem_limit_bytes=VMEM_LIMIT),
        name="inproj",
    )(x2, attn_g, w_in_b, sg_g)


def _attn_kernel(q_ref, k_ref, v_ref, o_ref, acc_ref, carry_ref):
    t = TQ_ATTN
    qi = pl.program_id(2)
    q = q_ref[0]
    lane = lax.broadcasted_iota(jnp.int32, (1, HEAD_PAIR), 1)
    head_lanes = (lane < HEAD_DIM, lane >= HEAD_DIM)
    zero = jnp.zeros((), BF16)
    q_heads = tuple(jnp.where(m, q, zero) for m in head_lanes)
    r_idx = lax.broadcasted_iota(jnp.int32, (t, t), 0)
    c_idx = lax.broadcasted_iota(jnp.int32, (t, t), 1)
    suffix = (r_idx > c_idx).astype(BF16)
    causal = c_idx < r_idx

    acc_ref[...] = jnp.zeros_like(acc_ref)
    carry_ref[...] = jnp.zeros_like(carry_ref)

    def block(j, diag):
        start = pl.multiple_of(j * t, t)
        kb = k_ref[0, pl.ds(start, t), :]
        vb = v_ref[0, pl.ds(start, t), :]
        acc = acc_ref[...]
        for h in range(2):
            z = lax.dot_general(q_heads[h], kb, (((1,), (1,)), ((), ())), preferred_element_type=F32)
            sp = _softplus(z)
            lm = -sp
            if diag:
                lm = jnp.where(causal, lm, 0.0)
            hi, lo = _split_bf16(lm)
            after = _dot(hi, suffix) + _dot(lo, suffix)
            carry = carry_ref[h]
            a = jnp.exp((z - sp) + after + carry)
            if diag:
                a = jnp.where(causal, a, 0.0)
            vh = jnp.where(head_lanes[h], vb, zero)
            acc = acc + _dot(a.astype(BF16), vh)
            carry_ref[h] = carry + after[:, 0:1] + lm[:, 0:1]
        acc_ref[...] = acc

    block(qi, True)

    def body(it, c):
        block(qi - 1 - it, False)
        return c

    lax.fori_loop(0, qi, body, 0)
    o_ref[0] = acc_ref[...]


def _attention(qkv, batch, seq):
    qkv3 = qkv.reshape(batch, seq, 3 * SB_WIDTH)
    n_pairs = SB_WIDTH // HEAD_PAIR
    return pl.pallas_call(
        _attn_kernel,
        grid=(batch, n_pairs, seq // TQ_ATTN),
        in_specs=[pl.BlockSpec((1, TQ_ATTN, HEAD_PAIR), lambda b, p, i: (b, i, p)),
                  pl.BlockSpec((1, seq, HEAD_PAIR), lambda b, p, i: (b, 0, n_pairs + p)),
                  pl.BlockSpec((1, seq, HEAD_PAIR), lambda b, p, i: (b, 0, 2 * n_pairs + p))],
        out_specs=pl.BlockSpec((1, TQ_ATTN, HEAD_PAIR), lambda b, p, i: (b, i, p)),
        out_shape=jax.ShapeDtypeStruct((batch, seq, SB_WIDTH), F32),
        scratch_shapes=[pltpu.VMEM((TQ_ATTN, HEAD_PAIR), F32),
                        pltpu.VMEM((2, TQ_ATTN, 1), F32)],
        compiler_params=pltpu.CompilerParams(dimension_semantics=("arbitrary",) * 3,
                                             vmem_limit_bytes=VMEM_LIMIT),
        name="sb_attention",
    )(qkv3, qkv3, qkv3)


def _mix_kernel(sb_ref, gu_ref, vgn_ref, x_ref, wsp_ref, bsp_ref, sbg_ref, sgg_ref, wout_ref,
                ffng_ref, wrh_ref, wrl_ref, br_ref,
                h_ref, hn_ref, ri_ref, rw_ref, cnt_ref, count_ref, sg_ref):
    tm = TM_MIX
    i = pl.program_id(0)

    @pl.when(i == 0)
    def _():
        count_ref[...] = jnp.zeros_like(count_ref)

    lane = lax.broadcasted_iota(jnp.int32, (1, LANES), 1)
    first = lane < HEAD_DIM
    zero = jnp.zeros((), BF16)
    r_c = lax.broadcasted_iota(jnp.int32, (CHUNK, CHUNK), 0)
    c_c = lax.broadcasted_iota(jnp.int32, (CHUNK, CHUNK), 1)
    tril = r_c >= c_c
    n_pairs = SG_WIDTH // HEAD_PAIR
    w_pairs = []
    for p in range(n_pairs):
        w0 = jnp.where(tril, wsp_ref[2 * p], 0.0).astype(BF16)
        w1 = jnp.where(tril, wsp_ref[2 * p + 1], 0.0).astype(BF16)
        w_pairs.append(jnp.concatenate([w0, w1], axis=1))
    bsp = bsp_ref[...]
    for c in range(tm // CHUNK):
        rows = slice(c * CHUNK, (c + 1) * CHUNK)
        for p in range(n_pairs):
            cols = slice(p * HEAD_PAIR, (p + 1) * HEAD_PAIR)
            vg = vgn_ref[rows, cols]
            rhs = jnp.concatenate([jnp.where(first, vg, zero), jnp.where(first, zero, vg)], axis=0)
            mixed = _dot(w_pairs[p], rhs) + bsp[:, cols]
            sg_ref[rows, cols] = gu_ref[rows, cols] * mixed
    sgn = _rms(sg_ref[...], sgg_ref[...]).astype(BF16)
    sbn = _rms(sb_ref[...], sbg_ref[...]).astype(BF16)
    h = x_ref[...] + _dot(sbn, wout_ref[0:SB_WIDTH, :]) + _dot(sgn, wout_ref[SB_WIDTH:, :])
    h_ref[...] = h
    hn = _rms(h, ffng_ref[...])
    _rows_to_tiles(hn_ref, hn)

    hn_hi, hn_lo = _split_bf16(hn)
    wrh = wrh_ref[...]
    logits = _dot(hn_hi, wrh) + _dot(hn_lo, wrh) + _dot(hn_hi, wrl_ref[...]) + br_ref[...]

    lane_t = lax.broadcasted_iota(jnp.int32, (tm, LANES), 1)
    neg = jnp.float32(-jnp.inf)
    gl = jnp.where(lane_t < N_GROUPS, logits, neg)
    gmax = jnp.max(gl, axis=-1, keepdims=True)
    gidx = jnp.min(jnp.where(gl == gmax, lane_t, LANES), axis=-1, keepdims=True)
    gsum = jnp.sum(jnp.exp(gl - gmax), axis=-1, keepdims=True)
    gweight = 1.0 / gsum
    lo_lane = ROUTER_LANE0 + EXPERTS_PER_GROUP * gidx
    el = jnp.where((lane_t >= lo_lane) & (lane_t < lo_lane + EXPERTS_PER_GROUP), logits, neg)
    m1 = jnp.max(el, axis=-1, keepdims=True)
    i1 = jnp.min(jnp.where(el == m1, lane_t, LANES), axis=-1, keepdims=True)
    el2 = jnp.where(lane_t == i1, neg, el)
    m2 = jnp.max(el2, axis=-1, keepdims=True)
    i2 = jnp.min(jnp.where(el2 == m2, lane_t, LANES), axis=-1, keepdims=True)
    t21 = jnp.exp(m2 - m1)
    w1 = gweight / (1.0 + t21)
    w2 = gweight * t21 / (1.0 + t21)

    sel1 = lane_t == i1
    sel2 = lane_t == i2
    onehot = jnp.where(sel1 | sel2, 1.0, 0.0)
    r_t = lax.broadcasted_iota(jnp.int32, (tm, tm), 0)
    c_t = lax.broadcasted_iota(jnp.int32, (tm, tm), 1)
    before = (r_t > c_t).astype(BF16)
    running = count_ref[0:1, :] + _dot(before, onehot.astype(BF16))
    rank1 = jnp.sum(jnp.where(sel1, running, 0.0), axis=-1, keepdims=True)
    rank2 = jnp.sum(jnp.where(sel2, running, 0.0), axis=-1, keepdims=True)
    new_count = count_ref[0:1, :] + jnp.sum(onehot, axis=0, keepdims=True)
    count_ref[...] = jnp.broadcast_to(new_count, count_ref.shape)
    cnt_ref[...] = jnp.broadcast_to(new_count, cnt_ref.shape)

    e1 = i1 - ROUTER_LANE0
    e2 = i2 - ROUTER_LANE0
    ri = jnp.where(lane_t == 0, e1, jnp.where(lane_t == 1, e2, jnp.where(
        lane_t == 2, rank1.astype(jnp.int32), jnp.where(lane_t == 3, rank2.astype(jnp.int32), 0))))
    ri_ref[...] = ri
    rw_ref[...] = jnp.where(lane_t == 0, w1, jnp.where(lane_t == 1, w2, 0.0))


def _mix(sb, gu, vgn, x2, wsp, bsp_full, sb_g, sg_g, w_out_b, ffn_g, wr_hi, wr_lo, br):
    n = x2.shape[0]
    row = lambda i: (i, 0)
    const = lambda i: (0, 0)
    return pl.pallas_call(
        _mix_kernel,
        grid=(n // TM_MIX,),
        in_specs=[pl.BlockSpec((TM_MIX, SB_WIDTH), row),
                  pl.BlockSpec((TM_MIX, SG_WIDTH), row),
                  pl.BlockSpec((TM_MIX, SG_WIDTH), row),
                  pl.BlockSpec((TM_MIX, D_MODEL), row),
                  pl.BlockSpec((SG_HEADS, CHUNK, CHUNK), lambda i: (0, 0, 0)),
                  pl.BlockSpec((CHUNK, SG_WIDTH), const),
                  pl.BlockSpec((1, SB_WIDTH), const),
                  pl.BlockSpec((1, SG_WIDTH), const),
                  pl.BlockSpec((D_MODEL, D_MODEL), const),
                  pl.BlockSpec((1, D_MODEL), const),
                  pl.BlockSpec((D_MODEL, LANES), const),
                  pl.BlockSpec((D_MODEL, LANES), const),
                  pl.BlockSpec((1, LANES), const)],
        out_specs=[pl.BlockSpec((TM_MIX, D_MODEL), row),
                   pl.BlockSpec((TM_MIX * ROW_TILE, LANES), row),
                   pl.BlockSpec((TM_MIX, LANES), row),
                   pl.BlockSpec((TM_MIX, LANES), row),
                   pl.BlockSpec((8, LANES), const)],
        out_shape=[jax.ShapeDtypeStruct((n, D_MODEL), F32),
                   jax.ShapeDtypeStruct((n * ROW_TILE, LANES), F32),
                   jax.ShapeDtypeStruct((n, LANES), jnp.int32),
                   jax.ShapeDtypeStruct((n, LANES), F32),
                   jax.ShapeDtypeStruct((8, LANES), F32)],
        scratch_shapes=[pltpu.VMEM((8, LANES), F32),
                        pltpu.VMEM((TM_MIX, SG_WIDTH), F32)],
        compiler_params=pltpu.CompilerParams(dimension_semantics=("arbitrary",),
                                             vmem_limit_bytes=VMEM_LIMIT),
        name="mix_router",
    )(sb, gu, vgn, x2, wsp, bsp_full, sb_g, sg_g, w_out_b, ffn_g, wr_hi, wr_lo, br)


_PAD_BITS = tuple(1 << b for b in reversed(range(TM_EXPERT.bit_length() - 1)))


def _dispatch_kernel(dest_ref, pad_start_ref, pad_count_ref, nt_ref, hn_ref, zeros_ref, xs_ref, sem, zsem):
    tm = TM_DISPATCH
    i = pl.program_id(0)
    base = i * (2 * tm)
    n_tiles_max = xs_ref.shape[0] // (TM_EXPERT * ROW_TILE)

    def pad_copies(do):
        for e in range(N_EXPERTS):
            start = pad_start_ref[e]
            count = pad_count_ref[e]
            for bit in _PAD_BITS:
                @pl.when((count & bit) != 0)
                def _(start=start, bit=bit):
                    do(pltpu.make_async_copy(_token_rows(zeros_ref, 0, bit),
                                             _token_rows(xs_ref, start, bit), zsem))
                start = start + (count & bit)
        for k in range(N_EXPERTS):
            tile = nt_ref[0] + k

            @pl.when(tile < n_tiles_max)
            def _(tile=tile):
                do(pltpu.make_async_copy(zeros_ref, _token_rows(xs_ref, tile * TM_EXPERT, TM_EXPERT), zsem))

    @pl.when(i == 0)
    def _():
        pad_copies(lambda cp: cp.start())

    def body(r, c):
        src = _token_rows(hn_ref, r, 1)
        for s in range(2):
            pltpu.make_async_copy(src, _token_rows(xs_ref, dest_ref[base + 2 * r + s], 1), sem).start()
        return c

    lax.fori_loop(0, tm, body, 0, unroll=8)
    for _ in range(2):
        pltpu.make_async_copy(hn_ref, _token_rows(xs_ref, 0, tm), sem).wait()

    @pl.when(i == 0)
    def _():
        pad_copies(lambda cp: cp.wait())


def _dispatch(dest, pad_start, pad_count, n_tiles, hn_tiles, n_rows):
    n = hn_tiles.shape[0] // ROW_TILE
    zeros = jnp.zeros((TM_EXPERT * ROW_TILE, LANES), F32)
    return pl.pallas_call(
        _dispatch_kernel,
        grid_spec=pltpu.PrefetchScalarGridSpec(
            num_scalar_prefetch=4,
            grid=(n // TM_DISPATCH,),
            in_specs=[pl.BlockSpec((TM_DISPATCH * ROW_TILE, LANES), lambda i, *_: (i, 0)),
                      pl.BlockSpec(memory_space=pl.ANY)],
            out_specs=pl.BlockSpec(memory_space=pl.ANY),
            scratch_shapes=[pltpu.SemaphoreType.DMA, pltpu.SemaphoreType.DMA]),
        out_shape=jax.ShapeDtypeStruct((n_rows * ROW_TILE, LANES), F32),
        compiler_params=pltpu.CompilerParams(dimension_semantics=("arbitrary",),
                                             vmem_limit_bytes=VMEM_LIMIT),
        name="dispatch",
    )(dest, pad_start, pad_count, n_tiles, hn_tiles, zeros)


def _expert_kernel(te_ref, tf_ref, nt_ref, x_ref, wg_ref, wu_ref, wd_ref, y_ref, wgb, wub, wdb):
    t = pl.program_id(0)

    @pl.when(t < nt_ref[0])
    def _():
        @pl.when(tf_ref[t] != 0)
        def _():
            wgb[...] = wg_ref[0].astype(BF16)
            wub[...] = wu_ref[0].astype(BF16)
            wdb[...] = wd_ref[0].astype(BF16)

        x = _tiles_to_rows(x_ref, TM_EXPERT).astype(BF16)
        g = _dot(x, wgb[...])
        u = _dot(x, wub[...])
        hidden = (g * jax.nn.sigmoid(g)) * u
        _rows_to_tiles(y_ref, _dot(hidden.astype(BF16), wdb[...]))

    @pl.when(t >= nt_ref[0])
    def _():
        y_ref[...] = jnp.zeros_like(y_ref)


def _experts(tile_expert, tile_first, n_tiles, xs, wg, wu, wd):
    n_rows = xs.shape[0] // ROW_TILE
    last = lambda t, nt: jnp.minimum(t, nt[0] - 1)
    return pl.pallas_call(
        _expert_kernel,
        grid_spec=pltpu.PrefetchScalarGridSpec(
            num_scalar_prefetch=3,
            grid=(n_rows // TM_EXPERT,),
            in_specs=[pl.BlockSpec((TM_EXPERT * ROW_TILE, LANES), lambda t, te, tf, nt: (last(t, nt), 0)),
                      pl.BlockSpec((1, D_MODEL, D_EXPERT), lambda t, te, tf, nt: (te[t], 0, 0)),
                      pl.BlockSpec((1, D_MODEL, D_EXPERT), lambda t, te, tf, nt: (te[t], 0, 0)),
                      pl.BlockSpec((1, D_EXPERT, D_MODEL), lambda t, te, tf, nt: (te[t], 0, 0))],
            out_specs=pl.BlockSpec((TM_EXPERT * ROW_TILE, LANES), lambda t, te, tf, nt: (t, 0)),
            scratch_shapes=[pltpu.VMEM((D_MODEL, D_EXPERT), BF16),
                            pltpu.VMEM((D_MODEL, D_EXPERT), BF16),
                            pltpu.VMEM((D_EXPERT, D_MODEL), BF16)]),
        out_shape=jax.ShapeDtypeStruct((n_rows * ROW_TILE, LANES), F32),
        compiler_params=pltpu.CompilerParams(dimension_semantics=("arbitrary",),
                                             vmem_limit_bytes=VMEM_LIMIT),
        name="expert_mlp",
    )(tile_expert, tile_first, n_tiles, xs, wg, wu, wd)


def _combine_kernel(dest_ref, h_ref, rw_ref, fg_ref, y_ref, o_ref, buf, sem):
    tm = TM_COMBINE
    i = pl.program_id(0)
    base = i * (2 * tm)

    def body(r, c):
        for s in range(2):
            pltpu.make_async_copy(_token_rows(y_ref, dest_ref[base + 2 * r + s], 1),
                                  _token_rows(buf.at[s], r, 1), sem).start()
        return c

    lax.fori_loop(0, tm, body, 0, unroll=8)
    for s in range(2):
        pltpu.make_async_copy(_token_rows(y_ref, 0, tm), buf.at[s], sem).wait()
    rw = rw_ref[...]
    out = (h_ref[...] + rw[:, 0:1] * _tiles_to_rows(buf.at[0], tm)
           + rw[:, 1:2] * _tiles_to_rows(buf.at[1], tm))
    o_ref[...] = _rms(out, fg_ref[...])


def _combine(dest, h, rw, final_g, ys):
    n = h.shape[0]
    return pl.pallas_call(
        _combine_kernel,
        grid_spec=pltpu.PrefetchScalarGridSpec(
            num_scalar_prefetch=1,
            grid=(n // TM_COMBINE,),
            in_specs=[pl.BlockSpec((TM_COMBINE, D_MODEL), lambda i, d: (i, 0)),
                      pl.BlockSpec((TM_COMBINE, LANES), lambda i, d: (i, 0)),
                      pl.BlockSpec((1, D_MODEL), lambda i, d: (0, 0)),
                      pl.BlockSpec(memory_space=pl.ANY)],
            out_specs=pl.BlockSpec((TM_COMBINE, D_MODEL), lambda i, d: (i, 0)),
            scratch_shapes=[pltpu.VMEM((2, TM_COMBINE * ROW_TILE, LANES), F32),
                            pltpu.SemaphoreType.DMA]),
        out_shape=jax.ShapeDtypeStruct((n, D_MODEL), F32),
        compiler_params=pltpu.CompilerParams(dimension_semantics=("arbitrary",),
                                             vmem_limit_bytes=VMEM_LIMIT),
        name="combine",
    )(dest, h, rw, final_g, ys)


def _schedule(counts, n_tiles_max):
    tiles = (counts + TM_EXPERT - 1) // TM_EXPERT
    tile_end = jnp.cumsum(tiles)
    offsets = (tile_end - tiles) * TM_EXPERT
    n_tiles = tile_end[-1]
    t = jnp.arange(n_tiles_max, dtype=jnp.int32)
    t_clamped = jnp.minimum(t, n_tiles - 1)
    tile_expert = jnp.sum(t_clamped[:, None] >= tile_end[None, :], axis=1).astype(jnp.int32)
    tile_first = (t_clamped == (tile_end - tiles)[tile_expert]).astype(jnp.int32)
    return offsets, tile_expert, tile_first, n_tiles.reshape(1).astype(jnp.int32)


def _layer(x, attn_g, w_in, sg_g, w_sp, b_sp, sb_g, sg_out_g, w_out, ffn_g,
           w_rg, b_rg, w_re, b_re, w_gate, w_up, w_down):
    batch, seq, _ = x.shape
    n = batch * seq
    x2 = x.reshape(n, D_MODEL)
    row = lambda v: v.reshape(1, -1)

    qkv, gu, vgn = _inproj(x2, row(attn_g), w_in.astype(BF16), row(sg_g))
    sb = _attention(qkv, batch, seq).reshape(n, SB_WIDTH)

    w_r = jnp.concatenate([w_rg, jnp.transpose(w_re, (1, 0, 2)).reshape(D_MODEL, N_EXPERTS)], axis=1)
    w_r = jnp.pad(w_r, ((0, 0), (0, LANES - w_r.shape[1])))
    wr_hi = w_r.astype(BF16)
    wr_lo = (w_r - wr_hi.astype(F32)).astype(BF16)
    b_r = jnp.pad(jnp.concatenate([b_rg, b_re.reshape(-1)]), (0, LANES - N_GROUPS - N_EXPERTS))
    bsp_full = jnp.repeat(b_sp.T, HEAD_DIM, axis=1)

    h, hn, ri, rw, cnt = _mix(sb, gu, vgn, x2, w_sp, bsp_full, row(sb_g), row(sg_out_g),
                              w_out.astype(BF16), row(ffn_g), wr_hi, wr_lo, row(b_r))

    counts = cnt[0, ROUTER_LANE0:ROUTER_LANE0 + N_EXPERTS].astype(jnp.int32)
    n_rows = 2 * n + N_EXPERTS * TM_EXPERT
    offsets, tile_expert, tile_first, n_tiles = _schedule(counts, n_rows // TM_EXPERT)
    dest = (offsets[ri[:, 0:2]] + ri[:, 2:4]).reshape(-1).astype(jnp.int32)
    pad_start = (offsets + counts).astype(jnp.int32)
    pad_count = ((-counts) % TM_EXPERT).astype(jnp.int32)

    xs = _dispatch(dest, pad_start, pad_count, n_tiles, hn, n_rows)
    ys = _experts(tile_expert, tile_first, n_tiles, xs,
                  w_gate.reshape(N_EXPERTS, D_MODEL, D_EXPERT),
                  w_up.reshape(N_EXPERTS, D_MODEL, D_EXPERT),
                  w_down.reshape(N_EXPERTS, D_EXPERT, D_MODEL))
    return dest, h, rw, ys


def kernel(x, attn_norm_g, w_in, sg_norm_g, w_spatial, b_spatial, sb_out_norm_g, sg_out_norm_g,
           w_out, ffn_norm_g, w_router_group, b_router_group, w_router_expert, b_router_expert,
           w_gate, w_up, w_down, final_norm_g):
    assert attn_norm_g.shape[0] == 1, "single-layer problem"
    batch, seq, _ = x.shape
    dest, h, rw, ys = _layer(x, attn_norm_g[0], w_in[0], sg_norm_g[0], w_spatial[0], b_spatial[0],
                             sb_out_norm_g[0], sg_out_norm_g[0], w_out[0], ffn_norm_g[0],
                             w_router_group[0], b_router_group[0], w_router_expert[0],
                             b_router_expert[0], w_gate[0], w_up[0], w_down[0])
    out = _combine(dest, h, rw, final_norm_g.reshape(1, -1), ys)
    return out.reshape(batch, seq, D_MODEL)
```

```python
import functools
import math

import jax
import jax.numpy as jnp
from jax import lax
from jax.experimental import pallas as pl
from jax.experimental.pallas import tpu as pltpu

D_MODEL = 1024
HEAD_DIM = 64
SB_WIDTH = 512
SG_WIDTH = 512
SG_HEADS = 8
D_IN = 3 * SB_WIDTH + 2 * SG_WIDTH
CHUNK = 128
N_GROUPS = 4
EXPERTS_PER_GROUP = 8
N_EXPERTS = N_GROUPS * EXPERTS_PER_GROUP
D_EXPERT = 512
EPS = 1e-6
F32_EXP_UNDERFLOW = 110.0

LANES = 128
ROW_TILE = D_MODEL // LANES
assert ROW_TILE == 8
HEAD_PAIR = 2 * HEAD_DIM
ROUTER_LANE0 = N_GROUPS

TM_PROJ = 512
TQ_ATTN = 256
TM_MIX = 512
TM_DISPATCH = 512
TM_EXPERT = 256
TM_COMBINE = 256
VMEM_LIMIT = 48 * 1024 * 1024

F32 = jnp.float32
BF16 = jnp.bfloat16


def _rms(x, g):
    return x * lax.rsqrt(jnp.mean(x * x, axis=-1, keepdims=True) + EPS) * g


def _gelu(x):
    c = math.sqrt(2.0 / math.pi)
    return x * (0.5 * (1.0 + jnp.tanh(c * (x + 0.044715 * (x * x * x)))))


def _softplus(z):
    return jnp.maximum(z, 0.0) + jnp.log(1.0 + jnp.exp(-jnp.abs(z)))


def _dot(a, b):
    return jnp.dot(a, b, preferred_element_type=F32)


def _rows_to_tiles(ref, x):
    m = x.shape[0]
    for k in range(ROW_TILE):
        ref[pl.ds(k, m, stride=ROW_TILE), :] = x[:, k * LANES:(k + 1) * LANES]


def _tiles_to_rows(ref, m):
    return jnp.concatenate([ref[pl.ds(k, m, stride=ROW_TILE), :] for k in range(ROW_TILE)], axis=1)


def _token_rows(ref, first_token, n_tokens):
    return ref.at[pl.ds(pl.multiple_of(first_token * ROW_TILE, ROW_TILE), n_tokens * ROW_TILE)]


def _split_bf16(x):
    hi = x.astype(BF16)
    lo = (x - hi.astype(F32)).astype(BF16)
    return hi, lo


def _inproj_kernel(x_ref, g_ref, w_ref, sgg_ref, qkv_ref, gu_ref, vgn_ref):
    hb = _rms(x_ref[...], g_ref[...]).astype(BF16)
    q = _dot(hb, w_ref[:, 0:SB_WIDTH]) * (1.0 / math.sqrt(HEAD_DIM))
    qkv_ref[:, 0:SB_WIDTH] = q.astype(BF16)
    qkv_ref[:, SB_WIDTH:3 * SB_WIDTH] = _dot(hb, w_ref[:, SB_WIDTH:3 * SB_WIDTH]).astype(BF16)
    gu_ref[...] = _gelu(_dot(hb, w_ref[:, 3 * SB_WIDTH:3 * SB_WIDTH + SG_WIDTH]))
    gv = _gelu(_dot(hb, w_ref[:, 3 * SB_WIDTH + SG_WIDTH:D_IN]))
    vgn_ref[...] = _rms(gv, sgg_ref[...]).astype(BF16)


def _inproj(x2, attn_g, w_in_b, sg_g):
    n = x2.shape[0]
    row = lambda i: (i, 0)
    const = lambda i: (0, 0)
    return pl.pallas_call(
        _inproj_kernel,
        grid=(n // TM_PROJ,),
        in_specs=[pl.BlockSpec((TM_PROJ, D_MODEL), row),
                  pl.BlockSpec((1, D_MODEL), const),
                  pl.BlockSpec((D_MODEL, D_IN), const),
                  pl.BlockSpec((1, SG_WIDTH), const)],
        out_specs=[pl.BlockSpec((TM_PROJ, 3 * SB_WIDTH), row),
                   pl.BlockSpec((TM_PROJ, SG_WIDTH), row),
                   pl.BlockSpec((TM_PROJ, SG_WIDTH), row)],
        out_shape=[jax.ShapeDtypeStruct((n, 3 * SB_WIDTH), BF16),
                   jax.ShapeDtypeStruct((n, SG_WIDTH), F32),
                   jax.ShapeDtypeStruct((n, SG_WIDTH), BF16)],
        compiler_params=pltpu.CompilerParams(dimension_semantics=("arbitrary",),
                                             vmem_limit_bytes=VMEM_LIMIT),
        name="inproj",
    )(x2, attn_g, w_in_b, sg_g)


def _attn_kernel(q_ref, k_ref, v_ref, o_ref, acc_ref, carry_ref):
    t = TQ_ATTN
    qi = pl.program_id(2)
    q = q_ref[0]
    lane = lax.broadcasted_iota(jnp.int32, (1, HEAD_PAIR), 1)
    head_lanes = (lane < HEAD_DIM, lane >= HEAD_DIM)
    zero = jnp.zeros((), BF16)
    q_heads = tuple(jnp.where(m, q, zero) for m in head_lanes)
    r_idx = lax.broadcasted_iota(jnp.int32, (t, t), 0)
    c_idx = lax.broadcasted_iota(jnp.int32, (t, t), 1)
    suffix = (r_idx > c_idx).astype(BF16)
    causal = c_idx < r_idx

    acc_ref[...] = jnp.zeros_like(acc_ref)
    carry_ref[...] = jnp.zeros_like(carry_ref)

    def block(j, diag):
        start = pl.multiple_of(j * t, t)
        kb = k_ref[0, pl.ds(start, t), :]
        vb = v_ref[0, pl.ds(start, t), :]
        acc = acc_ref[...]
        for h in range(2):
            z = lax.dot_general(q_heads[h], kb, (((1,), (1,)), ((), ())), preferred_element_type=F32)
            sp = _softplus(z)
            nl = jnp.where(causal, sp, 0.0) if diag else sp
            hi, lo = _split_bf16(nl)
            after = _dot(hi, suffix) + _dot(lo, suffix)
            carry = carry_ref[h]
            a = jnp.exp(z - sp - after - carry)
            if diag:
                a = jnp.where(causal, a, 0.0)
            vh = jnp.where(head_lanes[h], vb, zero)
            acc = acc + _dot(a.astype(BF16), vh)
            carry_ref[h] = carry + after[:, 0:1] + nl[:, 0:1]
        acc_ref[...] = acc

    def live():
        return jnp.min(carry_ref[...]) < F32_EXP_UNDERFLOW

    block(qi, True)

    def body(state):
        it, _ = state
        block(qi - 1 - it, False)
        return it + 1, live()

    lax.while_loop(lambda s: (s[0] < qi) & s[1], body, (jnp.int32(0), live()))
    o_ref[0] = acc_ref[...]


def _attention(qkv, batch, seq):
    qkv3 = qkv.reshape(batch, seq, 3 * SB_WIDTH)
    n_pairs = SB_WIDTH // HEAD_PAIR
    return pl.pallas_call(
        _attn_kernel,
        grid=(batch, n_pairs, seq // TQ_ATTN),
        in_specs=[pl.BlockSpec((1, TQ_ATTN, HEAD_PAIR), lambda b, p, i: (b, i, p)),
                  pl.BlockSpec((1, seq, HEAD_PAIR), lambda b, p, i: (b, 0, n_pairs + p)),
                  pl.BlockSpec((1, seq, HEAD_PAIR), lambda b, p, i: (b, 0, 2 * n_pairs + p))],
        out_specs=pl.BlockSpec((1, TQ_ATTN, HEAD_PAIR), lambda b, p, i: (b, i, p)),
        out_shape=jax.ShapeDtypeStruct((batch, seq, SB_WIDTH), F32),
        scratch_shapes=[pltpu.VMEM((TQ_ATTN, HEAD_PAIR), F32),
                        pltpu.VMEM((2, TQ_ATTN, 1), F32)],
        compiler_params=pltpu.CompilerParams(dimension_semantics=("arbitrary",) * 3,
                                             vmem_limit_bytes=VMEM_LIMIT),
        name="sb_attention",
    )(qkv3, qkv3, qkv3)


def _mix_kernel(sb_ref, gu_ref, vgn_ref, x_ref, wsp_ref, bsp_ref, sbg_ref, sgg_ref, wout_ref,
                ffng_ref, wrh_ref, wrl_ref, br_ref,
                h_ref, hn_ref, ri_ref, rw_ref, cnt_ref, count_ref, sg_ref):
    tm = TM_MIX
    i = pl.program_id(0)

    @pl.when(i == 0)
    def _():
        count_ref[...] = jnp.zeros_like(count_ref)

    lane = lax.broadcasted_iota(jnp.int32, (1, LANES), 1)
    first = lane < HEAD_DIM
    zero = jnp.zeros((), BF16)
    r_c = lax.broadcasted_iota(jnp.int32, (CHUNK, CHUNK), 0)
    c_c = lax.broadcasted_iota(jnp.int32, (CHUNK, CHUNK), 1)
    tril = r_c >= c_c
    n_pairs = SG_WIDTH // HEAD_PAIR
    w_pairs = []
    for p in range(n_pairs):
        w0 = jnp.where(tril, wsp_ref[2 * p], 0.0).astype(BF16)
        w1 = jnp.where(tril, wsp_ref[2 * p + 1], 0.0).astype(BF16)
        w_pairs.append(jnp.concatenate([w0, w1], axis=1))
    bsp = bsp_ref[...]
    for c in range(tm // CHUNK):
        rows = slice(c * CHUNK, (c + 1) * CHUNK)
        for p in range(n_pairs):
            cols = slice(p * HEAD_PAIR, (p + 1) * HEAD_PAIR)
            vg = vgn_ref[rows, cols]
            rhs = jnp.concatenate([jnp.where(first, vg, zero), jnp.where(first, zero, vg)], axis=0)
            mixed = _dot(w_pairs[p], rhs) + bsp[:, cols]
            sg_ref[rows, cols] = gu_ref[rows, cols] * mixed
    sgn = _rms(sg_ref[...], sgg_ref[...]).astype(BF16)
    sbn = _rms(sb_ref[...], sbg_ref[...]).astype(BF16)
    h = x_ref[...] + _dot(sbn, wout_ref[0:SB_WIDTH, :]) + _dot(sgn, wout_ref[SB_WIDTH:, :])
    h_ref[...] = h
    hn = _rms(h, ffng_ref[...])
    _rows_to_tiles(hn_ref, hn)

    hn_hi, hn_lo = _split_bf16(hn)
    wrh = wrh_ref[...]
    logits = _dot(hn_hi, wrh) + _dot(hn_lo, wrh) + _dot(hn_hi, wrl_ref[...]) + br_ref[...]

    lane_t = lax.broadcasted_iota(jnp.int32, (tm, LANES), 1)
    neg = jnp.float32(-jnp.inf)
    gl = jnp.where(lane_t < N_GROUPS, logits, neg)
    gmax = jnp.max(gl, axis=-1, keepdims=True)
    gidx = jnp.min(jnp.where(gl == gmax, lane_t, LANES), axis=-1, keepdims=True)
    gsum = jnp.sum(jnp.exp(gl - gmax), axis=-1, keepdims=True)
    gweight = 1.0 / gsum
    lo_lane = ROUTER_LANE0 + EXPERTS_PER_GROUP * gidx
    el = jnp.where((lane_t >= lo_lane) & (lane_t < lo_lane + EXPERTS_PER_GROUP), logits, neg)
    m1 = jnp.max(el, axis=-1, keepdims=True)
    i1 = jnp.min(jnp.where(el == m1, lane_t, LANES), axis=-1, keepdims=True)
    el2 = jnp.where(lane_t == i1, neg, el)
    m2 = jnp.max(el2, axis=-1, keepdims=True)
    i2 = jnp.min(jnp.where(el2 == m2, lane_t, LANES), axis=-1, keepdims=True)
    t21 = jnp.exp(m2 - m1)
    w1 = gweight / (1.0 + t21)
    w2 = gweight * t21 / (1.0 + t21)

    sel1 = lane_t == i1
    sel2 = lane_t == i2
    onehot = jnp.where(sel1 | sel2, 1.0, 0.0)
    r_t = lax.broadcasted_iota(jnp.int32, (tm, tm), 0)
    c_t = lax.broadcasted_iota(jnp.int32, (tm, tm), 1)
    before = (r_t > c_t).astype(BF16)
    running = count_ref[0:1, :] + _dot(before, onehot.astype(BF16))
    rank1 = jnp.sum(jnp.where(sel1, running, 0.0), axis=-1, keepdims=True)
    rank2 = jnp.sum(jnp.where(sel2, running, 0.0), axis=-1, keepdims=True)
    new_count = count_ref[0:1, :] + jnp.sum(onehot, axis=0, keepdims=True)
    count_ref[...] = jnp.broadcast_to(new_count, count_ref.shape)
    cnt_ref[...] = jnp.broadcast_to(new_count, cnt_ref.shape)

    e1 = i1 - ROUTER_LANE0
    e2 = i2 - ROUTER_LANE0
    ri = jnp.where(lane_t == 0, e1, jnp.where(lane_t == 1, e2, jnp.where(
        lane_t == 2, rank1.astype(jnp.int32), jnp.where(lane_t == 3, rank2.astype(jnp.int32), 0))))
    ri_ref[...] = ri
    rw_ref[...] = jnp.where(lane_t == 0, w1, jnp.where(lane_t == 1, w2, 0.0))


def _mix(sb, gu, vgn, x2, wsp, bsp_full, sb_g, sg_g, w_out_b, ffn_g, wr_hi, wr_lo, br):
    n = x2.shape[0]
    row = lambda i: (i, 0)
    const = lambda i: (0, 0)
    return pl.pallas_call(
        _mix_kernel,
        grid=(n // TM_MIX,),
        in_specs=[pl.BlockSpec((TM_MIX, SB_WIDTH), row),
                  pl.BlockSpec((TM_MIX, SG_WIDTH), row),
                  pl.BlockSpec((TM_MIX, SG_WIDTH), row),
                  pl.BlockSpec((TM_MIX, D_MODEL), row),
                  pl.BlockSpec((SG_HEADS, CHUNK, CHUNK), lambda i: (0, 0, 0)),
                  pl.BlockSpec((CHUNK, SG_WIDTH), const),
                  pl.BlockSpec((1, SB_WIDTH), const),
                  pl.BlockSpec((1, SG_WIDTH), const),
                  pl.BlockSpec((D_MODEL, D_MODEL), const),
                  pl.BlockSpec((1, D_MODEL), const),
                  pl.BlockSpec((D_MODEL, LANES), const),
                  pl.BlockSpec((D_MODEL, LANES), const),
                  pl.BlockSpec((1, LANES), const)],
        out_specs=[pl.BlockSpec((TM_MIX, D_MODEL), row),
                   pl.BlockSpec((TM_MIX * ROW_TILE, LANES), row),
                   pl.BlockSpec((TM_MIX, LANES), row),
                   pl.BlockSpec((TM_MIX, LANES), row),
                   pl.BlockSpec((8, LANES), const)],
        out_shape=[jax.ShapeDtypeStruct((n, D_MODEL), F32),
                   jax.ShapeDtypeStruct((n * ROW_TILE, LANES), F32),
                   jax.ShapeDtypeStruct((n, LANES), jnp.int32),
                   jax.ShapeDtypeStruct((n, LANES), F32),
                   jax.ShapeDtypeStruct((8, LANES), F32)],
        scratch_shapes=[pltpu.VMEM((8, LANES), F32),
                        pltpu.VMEM((TM_MIX, SG_WIDTH), F32)],
        compiler_params=pltpu.CompilerParams(dimension_semantics=("arbitrary",),
                                             vmem_limit_bytes=VMEM_LIMIT),
        name="mix_router",
    )(sb, gu, vgn, x2, wsp, bsp_full, sb_g, sg_g, w_out_b, ffn_g, wr_hi, wr_lo, br)


_PAD_BITS = tuple(1 << b for b in reversed(range(TM_EXPERT.bit_length() - 1)))


def _dispatch_kernel(dest_ref, pad_start_ref, pad_count_ref, nt_ref, hn_ref, zeros_ref, xs_ref, sem, zsem):
    tm = TM_DISPATCH
    i = pl.program_id(0)
    base = i * (2 * tm)
    n_tiles_max = xs_ref.shape[0] // (TM_EXPERT * ROW_TILE)

    def pad_copies(do):
        for e in range(N_EXPERTS):
            start = pad_start_ref[e]
            count = pad_count_ref[e]
            for bit in _PAD_BITS:
                @pl.when((count & bit) != 0)
                def _(start=start, bit=bit):
                    do(pltpu.make_async_copy(_token_rows(zeros_ref, 0, bit),
                                             _token_rows(xs_ref, start, bit), zsem))
                start = start + (count & bit)
        for k in range(N_EXPERTS):
            tile = nt_ref[0] + k

            @pl.when(tile < n_tiles_max)
            def _(tile=tile):
                do(pltpu.make_async_copy(zeros_ref, _token_rows(xs_ref, tile * TM_EXPERT, TM_EXPERT), zsem))

    @pl.when(i == 0)
    def _():
        pad_copies(lambda cp: cp.start())

    def body(r, c):
        src = _token_rows(hn_ref, r, 1)
        for s in range(2):
            pltpu.make_async_copy(src, _token_rows(xs_ref, dest_ref[base + 2 * r + s], 1), sem).start()
        return c

    lax.fori_loop(0, tm, body, 0, unroll=8)
    for _ in range(2):
        pltpu.make_async_copy(hn_ref, _token_rows(xs_ref, 0, tm), sem).wait()

    @pl.when(i == 0)
    def _():
        pad_copies(lambda cp: cp.wait())


def _dispatch(dest, pad_start, pad_count, n_tiles, hn_tiles, n_rows):
    n = hn_tiles.shape[0] // ROW_TILE
    zeros = jnp.zeros((TM_EXPERT * ROW_TILE, LANES), F32)
    return pl.pallas_call(
        _dispatch_kernel,
        grid_spec=pltpu.PrefetchScalarGridSpec(
            num_scalar_prefetch=4,
            grid=(n // TM_DISPATCH,),
            in_specs=[pl.BlockSpec((TM_DISPATCH * ROW_TILE, LANES), lambda i, *_: (i, 0)),
                      pl.BlockSpec(memory_space=pl.ANY)],
            out_specs=pl.BlockSpec(memory_space=pl.ANY),
            scratch_shapes=[pltpu.SemaphoreType.DMA, pltpu.SemaphoreType.DMA]),
        out_shape=jax.ShapeDtypeStruct((n_rows * ROW_TILE, LANES), F32),
        compiler_params=pltpu.CompilerParams(dimension_semantics=("arbitrary",),
                                             vmem_limit_bytes=VMEM_LIMIT),
        name="dispatch",
    )(dest, pad_start, pad_count, n_tiles, hn_tiles, zeros)


def _expert_kernel(te_ref, tf_ref, nt_ref, x_ref, wg_ref, wu_ref, wd_ref, y_ref, wgb, wub, wdb):
    t = pl.program_id(0)

    @pl.when(t < nt_ref[0])
    def _():
        @pl.when(tf_ref[t] != 0)
        def _():
            wgb[...] = wg_ref[0].astype(BF16)
            wub[...] = wu_ref[0].astype(BF16)
            wdb[...] = wd_ref[0].astype(BF16)

        x = _tiles_to_rows(x_ref, TM_EXPERT).astype(BF16)
        g = _dot(x, wgb[...])
        u = _dot(x, wub[...])
        hidden = (g * jax.nn.sigmoid(g)) * u
        _rows_to_tiles(y_ref, _dot(hidden.astype(BF16), wdb[...]))

    @pl.when(t >= nt_ref[0])
    def _():
        y_ref[...] = jnp.zeros_like(y_ref)


def _experts(tile_expert, tile_first, n_tiles, xs, wg, wu, wd):
    n_rows = xs.shape[0] // ROW_TILE
    last = lambda t, nt: jnp.minimum(t, nt[0] - 1)
    return pl.pallas_call(
        _expert_kernel,
        grid_spec=pltpu.PrefetchScalarGridSpec(
            num_scalar_prefetch=3,
            grid=(n_rows // TM_EXPERT,),
            in_specs=[pl.BlockSpec((TM_EXPERT * ROW_TILE, LANES), lambda t, te, tf, nt: (last(t, nt), 0)),
                      pl.BlockSpec((1, D_MODEL, D_EXPERT), lambda t, te, tf, nt: (te[t], 0, 0)),
                      pl.BlockSpec((1, D_MODEL, D_EXPERT), lambda t, te, tf, nt: (te[t], 0, 0)),
                      pl.BlockSpec((1, D_EXPERT, D_MODEL), lambda t, te, tf, nt: (te[t], 0, 0))],
            out_specs=pl.BlockSpec((TM_EXPERT * ROW_TILE, LANES), lambda t, te, tf, nt: (t, 0)),
            scratch_shapes=[pltpu.VMEM((D_MODEL, D_EXPERT), BF16),
                            pltpu.VMEM((D_MODEL, D_EXPERT), BF16),
                            pltpu.VMEM((D_EXPERT, D_MODEL), BF16)]),
        out_shape=jax.ShapeDtypeStruct((n_rows * ROW_TILE, LANES), F32),
        compiler_params=pltpu.CompilerParams(dimension_semantics=("arbitrary",),
                                             vmem_limit_bytes=VMEM_LIMIT),
        name="expert_mlp",
    )(tile_expert, tile_first, n_tiles, xs, wg, wu, wd)


def _combine_kernel(dest_ref, h_ref, rw_ref, fg_ref, y_ref, o_ref, buf, sem):
    tm = TM_COMBINE
    i = pl.program_id(0)
    base = i * (2 * tm)

    def body(r, c):
        for s in range(2):
            pltpu.make_async_copy(_token_rows(y_ref, dest_ref[base + 2 * r + s], 1),
                                  _token_rows(buf.at[s], r, 1), sem).start()
        return c

    lax.fori_loop(0, tm, body, 0, unroll=8)
    for s in range(2):
        pltpu.make_async_copy(_token_rows(y_ref, 0, tm), buf.at[s], sem).wait()
    rw = rw_ref[...]
    out = (h_ref[...] + rw[:, 0:1] * _tiles_to_rows(buf.at[0], tm)
           + rw[:, 1:2] * _tiles_to_rows(buf.at[1], tm))
    o_ref[...] = _rms(out, fg_ref[...])


def _combine(dest, h, rw, final_g, ys):
    n = h.shape[0]
    return pl.pallas_call(
        _combine_kernel,
        grid_spec=pltpu.PrefetchScalarGridSpec(
            num_scalar_prefetch=1,
            grid=(n // TM_COMBINE,),
            in_specs=[pl.BlockSpec((TM_COMBINE, D_MODEL), lambda i, d: (i, 0)),
                      pl.BlockSpec((TM_COMBINE, LANES), lambda i, d: (i, 0)),
                      pl.BlockSpec((1, D_MODEL), lambda i, d: (0, 0)),
                      pl.BlockSpec(memory_space=pl.ANY)],
            out_specs=pl.BlockSpec((TM_COMBINE, D_MODEL), lambda i, d: (i, 0)),
            scratch_shapes=[pltpu.VMEM((2, TM_COMBINE * ROW_TILE, LANES), F32),
                            pltpu.SemaphoreType.DMA]),
        out_shape=jax.ShapeDtypeStruct((n, D_MODEL), F32),
        compiler_params=pltpu.CompilerParams(dimension_semantics=("arbitrary",),
                                             vmem_limit_bytes=VMEM_LIMIT),
        name="combine",
    )(dest, h, rw, final_g, ys)


def _schedule(counts, n_tiles_max):
    tiles = (counts + TM_EXPERT - 1) // TM_EXPERT
    tile_end = jnp.cumsum(tiles)
    offsets = (tile_end - tiles) * TM_EXPERT
    n_tiles = tile_end[-1]
    t = jnp.arange(n_tiles_max, dtype=jnp.int32)
    t_clamped = jnp.minimum(t, n_tiles - 1)
    tile_expert = jnp.sum(t_clamped[:, None] >= tile_end[None, :], axis=1).astype(jnp.int32)
    tile_first = (t_clamped == (tile_end - tiles)[tile_expert]).astype(jnp.int32)
    return offsets, tile_expert, tile_first, n_tiles.reshape(1).astype(jnp.int32)


def _layer(x, attn_g, w_in, sg_g, w_sp, b_sp, sb_g, sg_out_g, w_out, ffn_g,
           w_rg, b_rg, w_re, b_re, w_gate, w_up, w_down):
    batch, seq, _ = x.shape
    n = batch * seq
    x2 = x.reshape(n, D_MODEL)
    row = lambda v: v.reshape(1, -1)

    qkv, gu, vgn = _inproj(x2, row(attn_g), w_in.astype(BF16), row(sg_g))
    sb = _attention(qkv, batch, seq).reshape(n, SB_WIDTH)

    w_r = jnp.concatenate([w_rg, jnp.transpose(w_re, (1, 0, 2)).reshape(D_MODEL, N_EXPERTS)], axis=1)
    w_r = jnp.pad(w_r, ((0, 0), (0, LANES - w_r.shape[1])))
    wr_hi = w_r.astype(BF16)
    wr_lo = (w_r - wr_hi.astype(F32)).astype(BF16)
    b_r = jnp.pad(jnp.concatenate([b_rg, b_re.reshape(-1)]), (0, LANES - N_GROUPS - N_EXPERTS))
    bsp_full = jnp.repeat(b_sp.T, HEAD_DIM, axis=1)

    h, hn, ri, rw, cnt = _mix(sb, gu, vgn, x2, w_sp, bsp_full, row(sb_g), row(sg_out_g),
                              w_out.astype(BF16), row(ffn_g), wr_hi, wr_lo, row(b_r))

    counts = cnt[0, ROUTER_LANE0:ROUTER_LANE0 + N_EXPERTS].astype(jnp.int32)
    n_rows = 2 * n + N_EXPERTS * TM_EXPERT
    offsets, tile_expert, tile_first, n_tiles = _schedule(counts, n_rows // TM_EXPERT)
    dest = (offsets[ri[:, 0:2]] + ri[:, 2:4]).reshape(-1).astype(jnp.int32)
    pad_start = (offsets + counts).astype(jnp.int32)
    pad_count = ((-counts) % TM_EXPERT).astype(jnp.int32)

    xs = _dispatch(dest, pad_start, pad_count, n_tiles, hn, n_rows)
    ys = _experts(tile_expert, tile_first, n_tiles, xs,
                  w_gate.reshape(N_EXPERTS, D_MODEL, D_EXPERT),
                  w_up.reshape(N_EXPERTS, D_MODEL, D_EXPERT),
                  w_down.reshape(N_EXPERTS, D_EXPERT, D_MODEL))
    return dest, h, rw, ys


def kernel(x, attn_norm_g, w_in, sg_norm_g, w_spatial, b_spatial, sb_out_norm_g, sg_out_norm_g,
           w_out, ffn_norm_g, w_router_group, b_router_group, w_router_expert, b_router_expert,
           w_gate, w_up, w_down, final_norm_g):
    assert attn_norm_g.shape[0] == 1, "single-layer problem"
    batch, seq, _ = x.shape
    dest, h, rw, ys = _layer(x, attn_norm_g[0], w_in[0], sg_norm_g[0], w_spatial[0], b_spatial[0],
                             sb_out_norm_g[0], sg_out_norm_g[0], w_out[0], ffn_norm_g[0],
                             w_router_group[0], b_router_group[0], w_router_expert[0],
                             b_router_expert[0], w_gate[0], w_up[0], w_down[0])
    out = _combine(dest, h, rw, final_norm_g.reshape(1, -1), ys)
    return out.reshape(batch, seq, D_MODEL)
```

```python
import functools
import math

import jax
import jax.numpy as jnp
from jax import lax
from jax.experimental import pallas as pl
from jax.experimental.pallas import tpu as pltpu

D_MODEL = 1024
HEAD_DIM = 64
SB_WIDTH = 512
SG_WIDTH = 512
SG_HEADS = 8
D_IN = 3 * SB_WIDTH + 2 * SG_WIDTH
CHUNK = 128
N_GROUPS = 4
EXPERTS_PER_GROUP = 8
N_EXPERTS = N_GROUPS * EXPERTS_PER_GROUP
D_EXPERT = 512
EPS = 1e-6
F32_EXP_UNDERFLOW = 110.0

LANES = 128
ROW_TILE = D_MODEL // LANES
assert ROW_TILE == 8
HEAD_PAIR = 2 * HEAD_DIM
ROUTER_LANE0 = N_GROUPS

TM_PROJ = 512
TQ_ATTN = 256
TM_MIX = 512
TM_DISPATCH = 512
TM_EXPERT = 256
TM_COMBINE = 256
VMEM_LIMIT = 48 * 1024 * 1024

F32 = jnp.float32
BF16 = jnp.bfloat16


def _rms(x, g):
    return x * lax.rsqrt(jnp.mean(x * x, axis=-1, keepdims=True) + EPS) * g


def _gelu(x):
    c = math.sqrt(2.0 / math.pi)
    return x * (0.5 * (1.0 + jnp.tanh(c * (x + 0.044715 * (x * x * x)))))


def _softplus(z):
    return jnp.maximum(z, 0.0) + jnp.log(1.0 + jnp.exp(-jnp.abs(z)))


def _dot(a, b):
    return jnp.dot(a, b, preferred_element_type=F32)


def _rows_to_tiles(ref, x):
    m = x.shape[0]
    for k in range(ROW_TILE):
        ref[pl.ds(k, m, stride=ROW_TILE), :] = x[:, k * LANES:(k + 1) * LANES]


def _tiles_to_rows(ref, m):
    return jnp.concatenate([ref[pl.ds(k, m, stride=ROW_TILE), :] for k in range(ROW_TILE)], axis=1)


def _token_rows(ref, first_token, n_tokens):
    return ref.at[pl.ds(pl.multiple_of(first_token * ROW_TILE, ROW_TILE), n_tokens * ROW_TILE)]


def _split_bf16(x):
    hi = x.astype(BF16)
    lo = (x - hi.astype(F32)).astype(BF16)
    return hi, lo


def _inproj_kernel(x_ref, g_ref, w_ref, sgg_ref, qkv_ref, gu_ref, vgn_ref):
    hb = _rms(x_ref[...], g_ref[...]).astype(BF16)
    q = _dot(hb, w_ref[:, 0:SB_WIDTH]) * (1.0 / math.sqrt(HEAD_DIM))
    qkv_ref[:, 0:SB_WIDTH] = q.astype(BF16)
    qkv_ref[:, SB_WIDTH:3 * SB_WIDTH] = _dot(hb, w_ref[:, SB_WIDTH:3 * SB_WIDTH]).astype(BF16)
    gu_ref[...] = _gelu(_dot(hb, w_ref[:, 3 * SB_WIDTH:3 * SB_WIDTH + SG_WIDTH]))
    gv = _gelu(_dot(hb, w_ref[:, 3 * SB_WIDTH + SG_WIDTH:D_IN]))
    vgn_ref[...] = _rms(gv, sgg_ref[...]).astype(BF16)


def _inproj(x2, attn_g, w_in_b, sg_g):
    n = x2.shape[0]
    row = lambda i: (i, 0)
    const = lambda i: (0, 0)
    return pl.pallas_call(
        _inproj_kernel,
        grid=(n // TM_PROJ,),
        in_specs=[pl.BlockSpec((TM_PROJ, D_MODEL), row),
                  pl.BlockSpec((1, D_MODEL), const),
                  pl.BlockSpec((D_MODEL, D_IN), const),
                  pl.BlockSpec((1, SG_WIDTH), const)],
        out_specs=[pl.BlockSpec((TM_PROJ, 3 * SB_WIDTH), row),
                   pl.BlockSpec((TM_PROJ, SG_WIDTH), row),
                   pl.BlockSpec((TM_PROJ, SG_WIDTH), row)],
        out_shape=[jax.ShapeDtypeStruct((n, 3 * SB_WIDTH), BF16),
                   jax.ShapeDtypeStruct((n, SG_WIDTH), F32),
                   jax.ShapeDtypeStruct((n, SG_WIDTH), BF16)],
        compiler_params=pltpu.CompilerParams(dimension_semantics=("arbitrary",),
                                             vmem_limit_bytes=VMEM_LIMIT),
        name="inproj",
    )(x2, attn_g, w_in_b, sg_g)


def _attn_kernel(q_ref, k_ref, v_ref, o_ref, q2_ref, carry_ref):
    t = TQ_ATTN
    n_pairs = SB_WIDTH // HEAD_PAIR
    qi = pl.program_id(1)
    lane = lax.broadcasted_iota(jnp.int32, (1, HEAD_PAIR), 1)
    head_lanes = (lane < HEAD_DIM, lane >= HEAD_DIM)
    zero = jnp.zeros((), BF16)
    for p in range(n_pairs):
        qp = q_ref[0, :, p * HEAD_PAIR:(p + 1) * HEAD_PAIR]
        for h in range(2):
            q2_ref[(2 * p + h) * t:(2 * p + h + 1) * t, :] = jnp.where(head_lanes[h], qp, zero)
    r_idx = lax.broadcasted_iota(jnp.int32, (t, t), 0)
    c_idx = lax.broadcasted_iota(jnp.int32, (t, t), 1)
    suffix = (r_idx > c_idx).astype(BF16)
    suffix2 = jnp.concatenate([suffix, suffix], axis=0)
    causal = c_idx < r_idx

    o_ref[...] = jnp.zeros_like(o_ref)
    carry_ref[...] = jnp.zeros_like(carry_ref)

    def block(j, diag):
        start = pl.multiple_of(j * t, t)
        for p in range(n_pairs):
            cols = slice(p * HEAD_PAIR, (p + 1) * HEAD_PAIR)
            rows = slice(2 * p * t, (2 * p + 2) * t)
            kb = k_ref[0, pl.ds(start, t), cols]
            vb = v_ref[0, pl.ds(start, t), cols]
            z = lax.dot_general(q2_ref[rows, :], kb, (((1,), (1,)), ((), ())),
                                preferred_element_type=F32)
            sp = _softplus(z)
            if diag:
                mask2 = jnp.concatenate([causal, causal], axis=0)
                nl = jnp.where(mask2, sp, 0.0)
            else:
                nl = sp
            hi, lo = _split_bf16(nl)
            hl = jnp.concatenate([hi, lo], axis=1)
            after = jnp.concatenate([_dot(hl[0:t], suffix2), _dot(hl[t:2 * t], suffix2)], axis=0)
            carry = carry_ref[rows, :]
            a = jnp.exp(z - sp - after - carry)
            if diag:
                a = jnp.where(mask2, a, 0.0)
            a = a.astype(BF16)
            a2 = jnp.concatenate([a[0:t], a[t:2 * t]], axis=1)
            v2 = jnp.concatenate([jnp.where(head_lanes[0], vb, zero),
                                  jnp.where(head_lanes[1], vb, zero)], axis=0)
            o_ref[0, :, cols] += _dot(a2, v2)
            carry_ref[rows, :] = carry + after[:, 0:1] + nl[:, 0:1]

    def live():
        return jnp.min(carry_ref[...]) < F32_EXP_UNDERFLOW

    block(qi, True)

    def body(state):
        it, _ = state
        block(qi - 1 - it, False)
        return it + 1, live()

    lax.while_loop(lambda s: (s[0] < qi) & s[1], body, (jnp.int32(0), live()))


def _attention(qkv, batch, seq):
    qkv3 = qkv.reshape(batch, seq, 3 * SB_WIDTH)
    n_heads = SB_WIDTH // HEAD_DIM
    return pl.pallas_call(
        _attn_kernel,
        grid=(batch, seq // TQ_ATTN),
        in_specs=[pl.BlockSpec((1, TQ_ATTN, SB_WIDTH), lambda b, i: (b, i, 0)),
                  pl.BlockSpec((1, seq, SB_WIDTH), lambda b, i: (b, 0, 1)),
                  pl.BlockSpec((1, seq, SB_WIDTH), lambda b, i: (b, 0, 2))],
        out_specs=pl.BlockSpec((1, TQ_ATTN, SB_WIDTH), lambda b, i: (b, i, 0)),
        out_shape=jax.ShapeDtypeStruct((batch, seq, SB_WIDTH), F32),
        scratch_shapes=[pltpu.VMEM((n_heads * TQ_ATTN, HEAD_PAIR), BF16),
                        pltpu.VMEM((n_heads * TQ_ATTN, 1), F32)],
        compiler_params=pltpu.CompilerParams(dimension_semantics=("arbitrary",) * 2,
                                             vmem_limit_bytes=VMEM_LIMIT),
        name="sb_attention",
    )(qkv3, qkv3, qkv3)


def _mix_kernel(sb_ref, gu_ref, vgn_ref, x_ref, wsp_ref, bsp_ref, sbg_ref, sgg_ref, wout_ref,
                ffng_ref, wrh_ref, wrl_ref, br_ref,
                h_ref, hn_ref, ri_ref, rw_ref, cnt_ref, count_ref, sg_ref):
    tm = TM_MIX
    i = pl.program_id(0)

    @pl.when(i == 0)
    def _():
        count_ref[...] = jnp.zeros_like(count_ref)

    lane = lax.broadcasted_iota(jnp.int32, (1, LANES), 1)
    first = lane < HEAD_DIM
    zero = jnp.zeros((), BF16)
    r_c = lax.broadcasted_iota(jnp.int32, (CHUNK, CHUNK), 0)
    c_c = lax.broadcasted_iota(jnp.int32, (CHUNK, CHUNK), 1)
    tril = r_c >= c_c
    n_pairs = SG_WIDTH // HEAD_PAIR
    w_pairs = []
    for p in range(n_pairs):
        w0 = jnp.where(tril, wsp_ref[2 * p], 0.0).astype(BF16)
        w1 = jnp.where(tril, wsp_ref[2 * p + 1], 0.0).astype(BF16)
        w_pairs.append(jnp.concatenate([w0, w1], axis=1))
    bsp = bsp_ref[...]
    for c in range(tm // CHUNK):
        rows = slice(c * CHUNK, (c + 1) * CHUNK)
        for p in range(n_pairs):
            cols = slice(p * HEAD_PAIR, (p + 1) * HEAD_PAIR)
            vg = vgn_ref[rows, cols]
            rhs = jnp.concatenate([jnp.where(first, vg, zero), jnp.where(first, zero, vg)], axis=0)
            mixed = _dot(w_pairs[p], rhs) + bsp[:, cols]
            sg_ref[rows, cols] = gu_ref[rows, cols] * mixed
    sgn = _rms(sg_ref[...], sgg_ref[...]).astype(BF16)
    sbn = _rms(sb_ref[...], sbg_ref[...]).astype(BF16)
    h = x_ref[...] + _dot(sbn, wout_ref[0:SB_WIDTH, :]) + _dot(sgn, wout_ref[SB_WIDTH:, :])
    h_ref[...] = h
    hn = _rms(h, ffng_ref[...])
    _rows_to_tiles(hn_ref, hn)

    hn_hi, hn_lo = _split_bf16(hn)
    wrh = wrh_ref[...]
    logits = _dot(hn_hi, wrh) + _dot(hn_lo, wrh) + _dot(hn_hi, wrl_ref[...]) + br_ref[...]

    lane_t = lax.broadcasted_iota(jnp.int32, (tm, LANES), 1)
    neg = jnp.float32(-jnp.inf)
    gl = jnp.where(lane_t < N_GROUPS, logits, neg)
    gmax = jnp.max(gl, axis=-1, keepdims=True)
    gidx = jnp.min(jnp.where(gl == gmax, lane_t, LANES), axis=-1, keepdims=True)
    gsum = jnp.sum(jnp.exp(gl - gmax), axis=-1, keepdims=True)
    gweight = 1.0 / gsum
    lo_lane = ROUTER_LANE0 + EXPERTS_PER_GROUP * gidx
    el = jnp.where((lane_t >= lo_lane) & (lane_t < lo_lane + EXPERTS_PER_GROUP), logits, neg)
    m1 = jnp.max(el, axis=-1, keepdims=True)
    i1 = jnp.min(jnp.where(el == m1, lane_t, LANES), axis=-1, keepdims=True)
    el2 = jnp.where(lane_t == i1, neg, el)
    m2 = jnp.max(el2, axis=-1, keepdims=True)
    i2 = jnp.min(jnp.where(el2 == m2, lane_t, LANES), axis=-1, keepdims=True)
    t21 = jnp.exp(m2 - m1)
    w1 = gweight / (1.0 + t21)
    w2 = gweight * t21 / (1.0 + t21)

    sel1 = lane_t == i1
    sel2 = lane_t == i2
    onehot = jnp.where(sel1 | sel2, 1.0, 0.0)
    r_t = lax.broadcasted_iota(jnp.int32, (tm, tm), 0)
    c_t = lax.broadcasted_iota(jnp.int32, (tm, tm), 1)
    before = (r_t > c_t).astype(BF16)
    running = count_ref[0:1, :] + _dot(before, onehot.astype(BF16))
    rank1 = jnp.sum(jnp.where(sel1, running, 0.0), axis=-1, keepdims=True)
    rank2 = jnp.sum(jnp.where(sel2, running, 0.0), axis=-1, keepdims=True)
    new_count = count_ref[0:1, :] + jnp.sum(onehot, axis=0, keepdims=True)
    count_ref[...] = jnp.broadcast_to(new_count, count_ref.shape)
    cnt_ref[...] = jnp.broadcast_to(new_count, cnt_ref.shape)

    e1 = i1 - ROUTER_LANE0
    e2 = i2 - ROUTER_LANE0
    ri = jnp.where(lane_t == 0, e1, jnp.where(lane_t == 1, e2, jnp.where(
        lane_t == 2, rank1.astype(jnp.int32), jnp.where(lane_t == 3, rank2.astype(jnp.int32), 0))))
    ri_ref[...] = ri
    rw_ref[...] = jnp.where(lane_t == 0, w1, jnp.where(lane_t == 1, w2, 0.0))


def _mix(sb, gu, vgn, x2, wsp, bsp_full, sb_g, sg_g, w_out_b, ffn_g, wr_hi, wr_lo, br):
    n = x2.shape[0]
    row = lambda i: (i, 0)
    const = lambda i: (0, 0)
    return pl.pallas_call(
        _mix_kernel,
        grid=(n // TM_MIX,),
        in_specs=[pl.BlockSpec((TM_MIX, SB_WIDTH), row),
                  pl.BlockSpec((TM_MIX, SG_WIDTH), row),
                  pl.BlockSpec((TM_MIX, SG_WIDTH), row),
                  pl.BlockSpec((TM_MIX, D_MODEL), row),
                  pl.BlockSpec((SG_HEADS, CHUNK, CHUNK), lambda i: (0, 0, 0)),
                  pl.BlockSpec((CHUNK, SG_WIDTH), const),
                  pl.BlockSpec((1, SB_WIDTH), const),
                  pl.BlockSpec((1, SG_WIDTH), const),
                  pl.BlockSpec((D_MODEL, D_MODEL), const),
                  pl.BlockSpec((1, D_MODEL), const),
                  pl.BlockSpec((D_MODEL, LANES), const),
                  pl.BlockSpec((D_MODEL, LANES), const),
                  pl.BlockSpec((1, LANES), const)],
        out_specs=[pl.BlockSpec((TM_MIX, D_MODEL), row),
                   pl.BlockSpec((TM_MIX * ROW_TILE, LANES), row),
                   pl.BlockSpec((TM_MIX, LANES), row),
                   pl.BlockSpec((TM_MIX, LANES), row),
                   pl.BlockSpec((8, LANES), const)],
        out_shape=[jax.ShapeDtypeStruct((n, D_MODEL), F32),
                   jax.ShapeDtypeStruct((n * ROW_TILE, LANES), F32),
                   jax.ShapeDtypeStruct((n, LANES), jnp.int32),
                   jax.ShapeDtypeStruct((n, LANES), F32),
                   jax.ShapeDtypeStruct((8, LANES), F32)],
        scratch_shapes=[pltpu.VMEM((8, LANES), F32),
                        pltpu.VMEM((TM_MIX, SG_WIDTH), F32)],
        compiler_params=pltpu.CompilerParams(dimension_semantics=("arbitrary",),
                                             vmem_limit_bytes=VMEM_LIMIT),
        name="mix_router",
    )(sb, gu, vgn, x2, wsp, bsp_full, sb_g, sg_g, w_out_b, ffn_g, wr_hi, wr_lo, br)


_PAD_BITS = tuple(1 << b for b in reversed(range(TM_EXPERT.bit_length() - 1)))


def _dispatch_kernel(dest_ref, pad_start_ref, pad_count_ref, nt_ref, hn_ref, zeros_ref, xs_ref, sem, zsem):
    tm = TM_DISPATCH
    i = pl.program_id(0)
    base = i * (2 * tm)
    n_tiles_max = xs_ref.shape[0] // (TM_EXPERT * ROW_TILE)

    def pad_copies(do):
        for e in range(N_EXPERTS):
            start = pad_start_ref[e]
            count = pad_count_ref[e]
            for bit in _PAD_BITS:
                @pl.when((count & bit) != 0)
                def _(start=start, bit=bit):
                    do(pltpu.make_async_copy(_token_rows(zeros_ref, 0, bit),
                                             _token_rows(xs_ref, start, bit), zsem))
                start = start + (count & bit)
        for k in range(N_EXPERTS):
            tile = nt_ref[0] + k

            @pl.when(tile < n_tiles_max)
            def _(tile=tile):
                do(pltpu.make_async_copy(zeros_ref, _token_rows(xs_ref, tile * TM_EXPERT, TM_EXPERT), zsem))

    @pl.when(i == 0)
    def _():
        pad_copies(lambda cp: cp.start())

    def body(r, c):
        src = _token_rows(hn_ref, r, 1)
        for s in range(2):
            pltpu.make_async_copy(src, _token_rows(xs_ref, dest_ref[base + 2 * r + s], 1),
                                  sem).start(priority=s)
        return c

    lax.fori_loop(0, tm, body, 0, unroll=8)
    for _ in range(2):
        pltpu.make_async_copy(hn_ref, _token_rows(xs_ref, 0, tm), sem).wait()

    @pl.when(i == 0)
    def _():
        pad_copies(lambda cp: cp.wait())


def _dispatch(dest, pad_start, pad_count, n_tiles, hn_tiles, n_rows):
    n = hn_tiles.shape[0] // ROW_TILE
    zeros = jnp.zeros((TM_EXPERT * ROW_TILE, LANES), F32)
    return pl.pallas_call(
        _dispatch_kernel,
        grid_spec=pltpu.PrefetchScalarGridSpec(
            num_scalar_prefetch=4,
            grid=(n // TM_DISPATCH,),
            in_specs=[pl.BlockSpec((TM_DISPATCH * ROW_TILE, LANES), lambda i, *_: (i, 0)),
                      pl.BlockSpec(memory_space=pl.ANY)],
            out_specs=pl.BlockSpec(memory_space=pl.ANY),
            scratch_shapes=[pltpu.SemaphoreType.DMA, pltpu.SemaphoreType.DMA]),
        out_shape=jax.ShapeDtypeStruct((n_rows * ROW_TILE, LANES), F32),
        compiler_params=pltpu.CompilerParams(dimension_semantics=("arbitrary",),
                                             vmem_limit_bytes=VMEM_LIMIT),
        name="dispatch",
    )(dest, pad_start, pad_count, n_tiles, hn_tiles, zeros)


def _expert_kernel(te_ref, tf_ref, nt_ref, x_ref, wg_ref, wu_ref, wd_ref, y_ref, wgb, wub, wdb):
    t = pl.program_id(0)

    @pl.when(t < nt_ref[0])
    def _():
        @pl.when(tf_ref[t] != 0)
        def _():
            wgb[...] = wg_ref[0].astype(BF16)
            wub[...] = wu_ref[0].astype(BF16)
            wdb[...] = wd_ref[0].astype(BF16)

        x = _tiles_to_rows(x_ref, TM_EXPERT).astype(BF16)
        g = _dot(x, wgb[...])
        u = _dot(x, wub[...])
        hidden = (g * jax.nn.sigmoid(g)) * u
        _rows_to_tiles(y_ref, _dot(hidden.astype(BF16), wdb[...]))

    @pl.when(t >= nt_ref[0])
    def _():
        y_ref[...] = jnp.zeros_like(y_ref)


def _experts(tile_expert, tile_first, n_tiles, xs, wg, wu, wd):
    n_rows = xs.shape[0] // ROW_TILE
    last = lambda t, nt: jnp.minimum(t, nt[0] - 1)
    return pl.pallas_call(
        _expert_kernel,
        grid_spec=pltpu.PrefetchScalarGridSpec(
            num_scalar_prefetch=3,
            grid=(n_rows // TM_EXPERT,),
            in_specs=[pl.BlockSpec((TM_EXPERT * ROW_TILE, LANES), lambda t, te, tf, nt: (last(t, nt), 0)),
                      pl.BlockSpec((1, D_MODEL, D_EXPERT), lambda t, te, tf, nt: (te[t], 0, 0)),
                      pl.BlockSpec((1, D_MODEL, D_EXPERT), lambda t, te, tf, nt: (te[t], 0, 0)),
                      pl.BlockSpec((1, D_EXPERT, D_MODEL), lambda t, te, tf, nt: (te[t], 0, 0))],
            out_specs=pl.BlockSpec((TM_EXPERT * ROW_TILE, LANES), lambda t, te, tf, nt: (t, 0)),
            scratch_shapes=[pltpu.VMEM((D_MODEL, D_EXPERT), BF16),
                            pltpu.VMEM((D_MODEL, D_EXPERT), BF16),
                            pltpu.VMEM((D_EXPERT, D_MODEL), BF16)]),
        out_shape=jax.ShapeDtypeStruct((n_rows * ROW_TILE, LANES), F32),
        compiler_params=pltpu.CompilerParams(dimension_semantics=("arbitrary",),
                                             vmem_limit_bytes=VMEM_LIMIT),
        name="expert_mlp",
    )(tile_expert, tile_first, n_tiles, xs, wg, wu, wd)


def _combine_kernel(dest_ref, h_ref, rw_ref, fg_ref, y_ref, o_ref, buf, sem):
    tm = TM_COMBINE
    i = pl.program_id(0)
    base = i * (2 * tm)

    def body(r, c):
        for s in range(2):
            pltpu.make_async_copy(_token_rows(y_ref, dest_ref[base + 2 * r + s], 1),
                                  _token_rows(buf.at[s], r, 1), sem).start(priority=s)
        return c

    lax.fori_loop(0, tm, body, 0, unroll=8)
    for s in range(2):
        pltpu.make_async_copy(_token_rows(y_ref, 0, tm), buf.at[s], sem).wait()
    rw = rw_ref[...]
    out = (h_ref[...] + rw[:, 0:1] * _tiles_to_rows(buf.at[0], tm)
           + rw[:, 1:2] * _tiles_to_rows(buf.at[1], tm))
    o_ref[...] = _rms(out, fg_ref[...])


def _combine(dest, h, rw, final_g, ys):
    n = h.shape[0]
    return pl.pallas_call(
        _combine_kernel,
        grid_spec=pltpu.PrefetchScalarGridSpec(
            num_scalar_prefetch=1,
            grid=(n // TM_COMBINE,),
            in_specs=[pl.BlockSpec((TM_COMBINE, D_MODEL), lambda i, d: (i, 0)),
                      pl.BlockSpec((TM_COMBINE, LANES), lambda i, d: (i, 0)),
                      pl.BlockSpec((1, D_MODEL), lambda i, d: (0, 0)),
                      pl.BlockSpec(memory_space=pl.ANY)],
            out_specs=pl.BlockSpec((TM_COMBINE, D_MODEL), lambda i, d: (i, 0)),
            scratch_shapes=[pltpu.VMEM((2, TM_COMBINE * ROW_TILE, LANES), F32),
                            pltpu.SemaphoreType.DMA]),
        out_shape=jax.ShapeDtypeStruct((n, D_MODEL), F32),
        compiler_params=pltpu.CompilerParams(dimension_semantics=("arbitrary",),
                                             vmem_limit_bytes=VMEM_LIMIT),
        name="combine",
    )(dest, h, rw, final_g, ys)


def _schedule(counts, n_tiles_max):
    tiles = (counts + TM_EXPERT - 1) // TM_EXPERT
    tile_end = jnp.cumsum(tiles)
    offsets = (tile_end - tiles) * TM_EXPERT
    n_tiles = tile_end[-1]
    t = jnp.arange(n_tiles_max, dtype=jnp.int32)
    t_clamped = jnp.minimum(t, n_tiles - 1)
    tile_expert = jnp.sum(t_clamped[:, None] >= tile_end[None, :], axis=1).astype(jnp.int32)
    tile_first = (t_clamped == (tile_end - tiles)[tile_expert]).astype(jnp.int32)
    return offsets, tile_expert, tile_first, n_tiles.reshape(1).astype(jnp.int32)


def _layer(x, attn_g, w_in, sg_g, w_sp, b_sp, sb_g, sg_out_g, w_out, ffn_g,
           w_rg, b_rg, w_re, b_re, w_gate, w_up, w_down):
    batch, seq, _ = x.shape
    n = batch * seq
    x2 = x.reshape(n, D_MODEL)
    row = lambda v: v.reshape(1, -1)

    qkv, gu, vgn = _inproj(x2, row(attn_g), w_in.astype(BF16), row(sg_g))
    sb = _attention(qkv, batch, seq).reshape(n, SB_WIDTH)

    w_r = jnp.concatenate([w_rg, jnp.transpose(w_re, (1, 0, 2)).reshape(D_MODEL, N_EXPERTS)], axis=1)
    w_r = jnp.pad(w_r, ((0, 0), (0, LANES - w_r.shape[1])))
    wr_hi = w_r.astype(BF16)
    wr_lo = (w_r - wr_hi.astype(F32)).astype(BF16)
    b_r = jnp.pad(jnp.concatenate([b_rg, b_re.reshape(-1)]), (0, LANES - N_GROUPS - N_EXPERTS))
    bsp_full = jnp.repeat(b_sp.T, HEAD_DIM, axis=1)

    h, hn, ri, rw, cnt = _mix(sb, gu, vgn, x2, w_sp, bsp_full, row(sb_g), row(sg_out_g),
                              w_out.astype(BF16), row(ffn_g), wr_hi, wr_lo, row(b_r))

    counts = cnt[0, ROUTER_LANE0:ROUTER_LANE0 + N_EXPERTS].astype(jnp.int32)
    n_rows = 2 * n + N_EXPERTS * TM_EXPERT
    offsets, tile_expert, tile_first, n_tiles = _schedule(counts, n_rows // TM_EXPERT)
    dest = (offsets[ri[:, 0:2]] + ri[:, 2:4]).reshape(-1).astype(jnp.int32)
    pad_start = (offsets + counts).astype(jnp.int32)
    pad_count = ((-counts) % TM_EXPERT).astype(jnp.int32)

    xs = _dispatch(dest, pad_start, pad_count, n_tiles, hn, n_rows)
    ys = _experts(tile_expert, tile_first, n_tiles, xs,
                  w_gate.reshape(N_EXPERTS, D_MODEL, D_EXPERT),
                  w_up.reshape(N_EXPERTS, D_MODEL, D_EXPERT),
                  w_down.reshape(N_EXPERTS, D_EXPERT, D_MODEL))
    return dest, h, rw, ys


def kernel(x, attn_norm_g, w_in, sg_norm_g, w_spatial, b_spatial, sb_out_norm_g, sg_out_norm_g,
           w_out, ffn_norm_g, w_router_group, b_router_group, w_router_expert, b_router_expert,
           w_gate, w_up, w_down, final_norm_g):
    assert attn_norm_g.shape[0] == 1, "single-layer problem"
    batch, seq, _ = x.shape
    dest, h, rw, ys = _layer(x, attn_norm_g[0], w_in[0], sg_norm_g[0], w_spatial[0], b_spatial[0],
                             sb_out_norm_g[0], sg_out_norm_g[0], w_out[0], ffn_norm_g[0],
                             w_router_group[0], b_router_group[0], w_router_expert[0],
                             b_router_expert[0], w_gate[0], w_up[0], w_down[0])
    out = _combine(dest, h, rw, final_norm_g.reshape(1, -1), ys)
    return out.reshape(batch, seq, D_MODEL)
```

```python
import functools
import math

import jax
import jax.numpy as jnp
from jax import lax
from jax.experimental import pallas as pl
from jax.experimental.pallas import tpu as pltpu

D_MODEL = 1024
HEAD_DIM = 64
SB_WIDTH = 512
SG_WIDTH = 512
SG_HEADS = 8
D_IN = 3 * SB_WIDTH + 2 * SG_WIDTH
CHUNK = 128
N_GROUPS = 4
EXPERTS_PER_GROUP = 8
N_EXPERTS = N_GROUPS * EXPERTS_PER_GROUP
D_EXPERT = 512
EPS = 1e-6
F32_EXP_UNDERFLOW = 110.0

LANES = 128
ROW_TILE = D_MODEL // LANES
assert ROW_TILE == 8
HEAD_PAIR = 2 * HEAD_DIM
ROUTER_LANE0 = N_GROUPS

TM_PROJ = 512
TQ_ATTN = 256
TM_MIX = 512
TM_DISPATCH = 512
TM_EXPERT = 256
TM_COMBINE = 256
VMEM_LIMIT = 48 * 1024 * 1024

F32 = jnp.float32
BF16 = jnp.bfloat16


def _rms(x, g):
    return x * lax.rsqrt(jnp.mean(x * x, axis=-1, keepdims=True) + EPS) * g


def _gelu(x):
    c = math.sqrt(2.0 / math.pi)
    return x * (0.5 * (1.0 + jnp.tanh(c * (x + 0.044715 * (x * x * x)))))


def _softplus(z):
    return jnp.maximum(z, 0.0) + jnp.log(1.0 + jnp.exp(-jnp.abs(z)))


def _dot(a, b):
    return jnp.dot(a, b, preferred_element_type=F32)


def _rows_to_tiles(ref, x):
    m = x.shape[0]
    for k in range(ROW_TILE):
        ref[pl.ds(k, m, stride=ROW_TILE), :] = x[:, k * LANES:(k + 1) * LANES]


def _tiles_to_rows(ref, m):
    return jnp.concatenate([ref[pl.ds(k, m, stride=ROW_TILE), :] for k in range(ROW_TILE)], axis=1)


def _token_rows(ref, first_token, n_tokens):
    return ref.at[pl.ds(pl.multiple_of(first_token * ROW_TILE, ROW_TILE), n_tokens * ROW_TILE)]


def _split_bf16(x):
    hi = x.astype(BF16)
    lo = (x - hi.astype(F32)).astype(BF16)
    return hi, lo


def _inproj_kernel(x_ref, g_ref, w_ref, sgg_ref, qkv_ref, gu_ref, vgn_ref):
    hb = _rms(x_ref[...], g_ref[...]).astype(BF16)
    q = _dot(hb, w_ref[:, 0:SB_WIDTH]) * (1.0 / math.sqrt(HEAD_DIM))
    qkv_ref[:, 0:SB_WIDTH] = q.astype(BF16)
    qkv_ref[:, SB_WIDTH:3 * SB_WIDTH] = _dot(hb, w_ref[:, SB_WIDTH:3 * SB_WIDTH]).astype(BF16)
    gu_ref[...] = _gelu(_dot(hb, w_ref[:, 3 * SB_WIDTH:3 * SB_WIDTH + SG_WIDTH]))
    gv = _gelu(_dot(hb, w_ref[:, 3 * SB_WIDTH + SG_WIDTH:D_IN]))
    vgn_ref[...] = _rms(gv, sgg_ref[...]).astype(BF16)


def _inproj(x2, attn_g, w_in_b, sg_g):
    n = x2.shape[0]
    row = lambda i: (i, 0)
    const = lambda i: (0, 0)
    return pl.pallas_call(
        _inproj_kernel,
        grid=(n // TM_PROJ,),
        in_specs=[pl.BlockSpec((TM_PROJ, D_MODEL), row),
                  pl.BlockSpec((1, D_MODEL), const),
                  pl.BlockSpec((D_MODEL, D_IN), const),
                  pl.BlockSpec((1, SG_WIDTH), const)],
        out_specs=[pl.BlockSpec((TM_PROJ, 3 * SB_WIDTH), row),
                   pl.BlockSpec((TM_PROJ, SG_WIDTH), row),
                   pl.BlockSpec((TM_PROJ, SG_WIDTH), row)],
        out_shape=[jax.ShapeDtypeStruct((n, 3 * SB_WIDTH), BF16),
                   jax.ShapeDtypeStruct((n, SG_WIDTH), F32),
                   jax.ShapeDtypeStruct((n, SG_WIDTH), BF16)],
        compiler_params=pltpu.CompilerParams(dimension_semantics=("arbitrary",),
                                             vmem_limit_bytes=VMEM_LIMIT),
        name="inproj",
    )(x2, attn_g, w_in_b, sg_g)


def _attn_kernel(q_ref, k_ref, v_ref, o_ref, q2_ref, carry_ref):
    t = TQ_ATTN
    n_pairs = SB_WIDTH // HEAD_PAIR
    qi = pl.program_id(1)
    lane = lax.broadcasted_iota(jnp.int32, (1, HEAD_PAIR), 1)
    head_lanes = (lane < HEAD_DIM, lane >= HEAD_DIM)
    zero = jnp.zeros((), BF16)
    for p in range(n_pairs):
        qp = q_ref[0, :, p * HEAD_PAIR:(p + 1) * HEAD_PAIR]
        for h in range(2):
            q2_ref[(2 * p + h) * t:(2 * p + h + 1) * t, :] = jnp.where(head_lanes[h], qp, zero)
    r_idx = lax.broadcasted_iota(jnp.int32, (t, t), 0)
    c_idx = lax.broadcasted_iota(jnp.int32, (t, t), 1)
    suffix = (r_idx > c_idx).astype(BF16)
    suffix2 = jnp.concatenate([suffix, suffix], axis=0)
    causal = c_idx < r_idx

    o_ref[...] = jnp.zeros_like(o_ref)
    carry_ref[...] = jnp.zeros_like(carry_ref)

    def block(j, diag):
        start = pl.multiple_of(j * t, t)
        for p in range(n_pairs):
            cols = slice(p * HEAD_PAIR, (p + 1) * HEAD_PAIR)
            rows = slice(2 * p * t, (2 * p + 2) * t)
            kb = k_ref[0, pl.ds(start, t), cols]
            vb = v_ref[0, pl.ds(start, t), cols]
            z = lax.dot_general(q2_ref[rows, :], kb, (((1,), (1,)), ((), ())),
                                preferred_element_type=F32)
            sp = _softplus(z)
            if diag:
                mask2 = jnp.concatenate([causal, causal], axis=0)
                nl = jnp.where(mask2, sp, 0.0)
            else:
                nl = sp
            hi, lo = _split_bf16(nl)
            hl = jnp.concatenate([hi, lo], axis=1)
            after = jnp.concatenate([_dot(hl[0:t], suffix2), _dot(hl[t:2 * t], suffix2)], axis=0)
            carry = carry_ref[rows, :]
            a = jnp.exp(z - sp - after - carry)
            if diag:
                a = jnp.where(mask2, a, 0.0)
            a = a.astype(BF16)
            a2 = jnp.concatenate([a[0:t], a[t:2 * t]], axis=1)
            v2 = jnp.concatenate([jnp.where(head_lanes[0], vb, zero),
                                  jnp.where(head_lanes[1], vb, zero)], axis=0)
            o_ref[0, :, cols] += _dot(a2, v2)
            carry_ref[rows, :] = carry + after[:, 0:1] + nl[:, 0:1]

    def live():
        return jnp.min(carry_ref[...]) < F32_EXP_UNDERFLOW

    block(qi, True)

    def body(state):
        it, _ = state
        block(qi - 1 - it, False)
        return it + 1, live()

    lax.while_loop(lambda s: (s[0] < qi) & s[1], body, (jnp.int32(0), live()))


def _attention(qkv, batch, seq):
    qkv3 = qkv.reshape(batch, seq, 3 * SB_WIDTH)
    n_heads = SB_WIDTH // HEAD_DIM
    return pl.pallas_call(
        _attn_kernel,
        grid=(batch, seq // TQ_ATTN),
        in_specs=[pl.BlockSpec((1, TQ_ATTN, SB_WIDTH), lambda b, i: (b, i, 0)),
                  pl.BlockSpec((1, seq, SB_WIDTH), lambda b, i: (b, 0, 1)),
                  pl.BlockSpec((1, seq, SB_WIDTH), lambda b, i: (b, 0, 2))],
        out_specs=pl.BlockSpec((1, TQ_ATTN, SB_WIDTH), lambda b, i: (b, i, 0)),
        out_shape=jax.ShapeDtypeStruct((batch, seq, SB_WIDTH), F32),
        scratch_shapes=[pltpu.VMEM((n_heads * TQ_ATTN, HEAD_PAIR), BF16),
                        pltpu.VMEM((n_heads * TQ_ATTN, 1), F32)],
        compiler_params=pltpu.CompilerParams(dimension_semantics=("arbitrary",) * 2,
                                             vmem_limit_bytes=VMEM_LIMIT),
        name="sb_attention",
    )(qkv3, qkv3, qkv3)


def _mix_kernel(sb_ref, gu_ref, vgn_ref, x_ref, wsp_ref, bsp_ref, sbg_ref, sgg_ref, wout_ref,
                ffng_ref, wrh_ref, wrl_ref, br_ref,
                h_ref, hn_ref, ri_ref, rw_ref, cnt_ref, count_ref, sg_ref):
    tm = TM_MIX
    i = pl.program_id(0)

    @pl.when(i == 0)
    def _():
        count_ref[...] = jnp.zeros_like(count_ref)

    lane = lax.broadcasted_iota(jnp.int32, (1, LANES), 1)
    first = lane < HEAD_DIM
    zero = jnp.zeros((), BF16)
    r_c = lax.broadcasted_iota(jnp.int32, (CHUNK, CHUNK), 0)
    c_c = lax.broadcasted_iota(jnp.int32, (CHUNK, CHUNK), 1)
    tril = r_c >= c_c
    n_pairs = SG_WIDTH // HEAD_PAIR
    w_pairs = []
    for p in range(n_pairs):
        w0 = jnp.where(tril, wsp_ref[2 * p], 0.0).astype(BF16)
        w1 = jnp.where(tril, wsp_ref[2 * p + 1], 0.0).astype(BF16)
        w_pairs.append(jnp.concatenate([w0, w1], axis=1))
    bsp = bsp_ref[...]
    for c in range(tm // CHUNK):
        rows = slice(c * CHUNK, (c + 1) * CHUNK)
        for p in range(n_pairs):
            cols = slice(p * HEAD_PAIR, (p + 1) * HEAD_PAIR)
            vg = vgn_ref[rows, cols]
            rhs = jnp.concatenate([jnp.where(first, vg, zero), jnp.where(first, zero, vg)], axis=0)
            mixed = _dot(w_pairs[p], rhs) + bsp[:, cols]
            sg_ref[rows, cols] = gu_ref[rows, cols] * mixed
    sgn = _rms(sg_ref[...], sgg_ref[...]).astype(BF16)
    sbn = _rms(sb_ref[...], sbg_ref[...]).astype(BF16)
    h = x_ref[...] + _dot(sbn, wout_ref[0:SB_WIDTH, :]) + _dot(sgn, wout_ref[SB_WIDTH:, :])
    h_ref[...] = h
    hn = _rms(h, ffng_ref[...])
    _rows_to_tiles(hn_ref, hn)

    hn_hi, hn_lo = _split_bf16(hn)
    wrh = wrh_ref[...]
    logits = _dot(hn_hi, wrh) + _dot(hn_lo, wrh) + _dot(hn_hi, wrl_ref[...]) + br_ref[...]

    lane_t = lax.broadcasted_iota(jnp.int32, (tm, LANES), 1)
    neg = jnp.float32(-jnp.inf)
    gl = jnp.where(lane_t < N_GROUPS, logits, neg)
    gmax = jnp.max(gl, axis=-1, keepdims=True)
    gidx = jnp.min(jnp.where(gl == gmax, lane_t, LANES), axis=-1, keepdims=True)
    gsum = jnp.sum(jnp.exp(gl - gmax), axis=-1, keepdims=True)
    gweight = 1.0 / gsum
    lo_lane = ROUTER_LANE0 + EXPERTS_PER_GROUP * gidx
    el = jnp.where((lane_t >= lo_lane) & (lane_t < lo_lane + EXPERTS_PER_GROUP), logits, neg)
    m1 = jnp.max(el, axis=-1, keepdims=True)
    i1 = jnp.min(jnp.where(el == m1, lane_t, LANES), axis=-1, keepdims=True)
    el2 = jnp.where(lane_t == i1, neg, el)
    m2 = jnp.max(el2, axis=-1, keepdims=True)
    i2 = jnp.min(jnp.where(el2 == m2, lane_t, LANES), axis=-1, keepdims=True)
    t21 = jnp.exp(m2 - m1)
    w1 = gweight / (1.0 + t21)
    w2 = gweight * t21 / (1.0 + t21)

    sel1 = lane_t == i1
    sel2 = lane_t == i2
    onehot = jnp.where(sel1 | sel2, 1.0, 0.0)
    r_t = lax.broadcasted_iota(jnp.int32, (tm, tm), 0)
    c_t = lax.broadcasted_iota(jnp.int32, (tm, tm), 1)
    before = (r_t > c_t).astype(BF16)
    running = count_ref[0:1, :] + _dot(before, onehot.astype(BF16))
    rank1 = jnp.sum(jnp.where(sel1, running, 0.0), axis=-1, keepdims=True)
    rank2 = jnp.sum(jnp.where(sel2, running, 0.0), axis=-1, keepdims=True)
    new_count = count_ref[0:1, :] + jnp.sum(onehot, axis=0, keepdims=True)
    count_ref[...] = jnp.broadcast_to(new_count, count_ref.shape)
    cnt_ref[...] = jnp.broadcast_to(new_count, cnt_ref.shape)

    e1 = i1 - ROUTER_LANE0
    e2 = i2 - ROUTER_LANE0
    ri = jnp.where(lane_t == 0, e1.astype(F32), jnp.where(lane_t == 1, e2.astype(F32), jnp.where(
        lane_t == 2, rank1, jnp.where(lane_t == 3, rank2, 0.0))))
    ri_ref[...] = ri.T[0:8, :].astype(jnp.int32)
    rw_ref[...] = jnp.where(lane_t == 0, w1, jnp.where(lane_t == 1, w2, 0.0))


def _mix(sb, gu, vgn, x2, wsp, bsp_full, sb_g, sg_g, w_out_b, ffn_g, wr_hi, wr_lo, br):
    n = x2.shape[0]
    row = lambda i: (i, 0)
    const = lambda i: (0, 0)
    return pl.pallas_call(
        _mix_kernel,
        grid=(n // TM_MIX,),
        in_specs=[pl.BlockSpec((TM_MIX, SB_WIDTH), row),
                  pl.BlockSpec((TM_MIX, SG_WIDTH), row),
                  pl.BlockSpec((TM_MIX, SG_WIDTH), row),
                  pl.BlockSpec((TM_MIX, D_MODEL), row),
                  pl.BlockSpec((SG_HEADS, CHUNK, CHUNK), lambda i: (0, 0, 0)),
                  pl.BlockSpec((CHUNK, SG_WIDTH), const),
                  pl.BlockSpec((1, SB_WIDTH), const),
                  pl.BlockSpec((1, SG_WIDTH), const),
                  pl.BlockSpec((D_MODEL, D_MODEL), const),
                  pl.BlockSpec((1, D_MODEL), const),
                  pl.BlockSpec((D_MODEL, LANES), const),
                  pl.BlockSpec((D_MODEL, LANES), const),
                  pl.BlockSpec((1, LANES), const)],
        out_specs=[pl.BlockSpec((TM_MIX, D_MODEL), row),
                   pl.BlockSpec((TM_MIX * ROW_TILE, LANES), row),
                   pl.BlockSpec((8, TM_MIX), lambda i: (0, i)),
                   pl.BlockSpec((TM_MIX, LANES), row),
                   pl.BlockSpec((8, LANES), const)],
        out_shape=[jax.ShapeDtypeStruct((n, D_MODEL), F32),
                   jax.ShapeDtypeStruct((n * ROW_TILE, LANES), F32),
                   jax.ShapeDtypeStruct((8, n), jnp.int32),
                   jax.ShapeDtypeStruct((n, LANES), F32),
                   jax.ShapeDtypeStruct((8, LANES), F32)],
        scratch_shapes=[pltpu.VMEM((8, LANES), F32),
                        pltpu.VMEM((TM_MIX, SG_WIDTH), F32)],
        compiler_params=pltpu.CompilerParams(dimension_semantics=("arbitrary",),
                                             vmem_limit_bytes=VMEM_LIMIT),
        name="mix_router",
    )(sb, gu, vgn, x2, wsp, bsp_full, sb_g, sg_g, w_out_b, ffn_g, wr_hi, wr_lo, br)


_PAD_BITS = tuple(1 << b for b in reversed(range(TM_EXPERT.bit_length() - 1)))


def _dispatch_kernel(dest_ref, pad_start_ref, pad_count_ref, nt_ref, hn_ref, zeros_ref, xs_ref, sem, zsem):
    tm = TM_DISPATCH
    i = pl.program_id(0)
    n = pl.num_programs(0) * tm
    base = i * tm
    n_tiles_max = xs_ref.shape[0] // (TM_EXPERT * ROW_TILE)

    def pad_copies(do):
        for e in range(N_EXPERTS):
            start = pad_start_ref[e]
            count = pad_count_ref[e]
            for bit in _PAD_BITS:
                @pl.when((count & bit) != 0)
                def _(start=start, bit=bit):
                    do(pltpu.make_async_copy(_token_rows(zeros_ref, 0, bit),
                                             _token_rows(xs_ref, start, bit), zsem))
                start = start + (count & bit)
        for k in range(N_EXPERTS):
            tile = nt_ref[0] + k

            @pl.when(tile < n_tiles_max)
            def _(tile=tile):
                do(pltpu.make_async_copy(zeros_ref, _token_rows(xs_ref, tile * TM_EXPERT, TM_EXPERT), zsem))

    @pl.when(i == 0)
    def _():
        pad_copies(lambda cp: cp.start())

    def body(r, c):
        src = _token_rows(hn_ref, r, 1)
        for s in range(2):
            pltpu.make_async_copy(src, _token_rows(xs_ref, dest_ref[s * n + base + r], 1),
                                  sem).start(priority=s)
        return c

    lax.fori_loop(0, tm, body, 0, unroll=8)
    for _ in range(2):
        pltpu.make_async_copy(hn_ref, _token_rows(xs_ref, 0, tm), sem).wait()

    @pl.when(i == 0)
    def _():
        pad_copies(lambda cp: cp.wait())


def _dispatch(dest, pad_start, pad_count, n_tiles, hn_tiles, n_rows):
    n = hn_tiles.shape[0] // ROW_TILE
    zeros = jnp.zeros((TM_EXPERT * ROW_TILE, LANES), F32)
    return pl.pallas_call(
        _dispatch_kernel,
        grid_spec=pltpu.PrefetchScalarGridSpec(
            num_scalar_prefetch=4,
            grid=(n // TM_DISPATCH,),
            in_specs=[pl.BlockSpec((TM_DISPATCH * ROW_TILE, LANES), lambda i, *_: (i, 0)),
                      pl.BlockSpec(memory_space=pl.ANY)],
            out_specs=pl.BlockSpec(memory_space=pl.ANY),
            scratch_shapes=[pltpu.SemaphoreType.DMA, pltpu.SemaphoreType.DMA]),
        out_shape=jax.ShapeDtypeStruct((n_rows * ROW_TILE, LANES), F32),
        compiler_params=pltpu.CompilerParams(dimension_semantics=("arbitrary",),
                                             vmem_limit_bytes=VMEM_LIMIT),
        name="dispatch",
    )(dest, pad_start, pad_count, n_tiles, hn_tiles, zeros)


def _expert_kernel(tiles_ref, nt_ref, x_ref, wg_ref, wu_ref, wd_ref, y_ref,
                   sg_buf, su_buf, sd_buf, wgb, wub, wdb, state, sems):
    t = pl.program_id(0)

    def weight_copies(e, slot):
        return (pltpu.make_async_copy(wg_ref.at[e], sg_buf.at[slot], sems.at[slot]),
                pltpu.make_async_copy(wu_ref.at[e], su_buf.at[slot], sems.at[slot]),
                pltpu.make_async_copy(wd_ref.at[e], sd_buf.at[slot], sems.at[slot]))

    def next_with_rows(e):
        return lax.while_loop(lambda k: (k < N_EXPERTS) & (tiles_ref[jnp.minimum(k, N_EXPERTS - 1)] == 0),
                              lambda k: k + 1, e + 1)

    @pl.when(t == 0)
    def _():
        first = next_with_rows(jnp.int32(-1))
        state[0] = jnp.int32(-1)
        state[1] = jnp.int32(0)
        state[2] = jnp.int32(1)
        state[3] = first
        for cp in weight_copies(first, 0):
            cp.start()

    @pl.when(t < nt_ref[0])
    def _():
        @pl.when(state[1] == 0)
        def _():
            e = state[3]
            slot = 1 - state[2]
            nxt = next_with_rows(e)
            state[0] = e
            state[1] = tiles_ref[e]
            state[2] = slot
            state[3] = nxt
            for cp in weight_copies(e, slot):
                cp.wait()

            @pl.when(nxt < N_EXPERTS)
            def _():
                for cp in weight_copies(nxt, 1 - slot):
                    cp.start()

            wgb[...] = sg_buf[slot].astype(BF16)
            wub[...] = su_buf[slot].astype(BF16)
            wdb[...] = sd_buf[slot].astype(BF16)

        state[1] = state[1] - 1
        x = _tiles_to_rows(x_ref, TM_EXPERT).astype(BF16)
        g = _dot(x, wgb[...])
        u = _dot(x, wub[...])
        hidden = (g * jax.nn.sigmoid(g)) * u
        _rows_to_tiles(y_ref, _dot(hidden.astype(BF16), wdb[...]))

    @pl.when(t >= nt_ref[0])
    def _():
        y_ref[...] = jnp.zeros_like(y_ref)


def _experts(tiles, n_tiles, xs, wg, wu, wd):
    n_rows = xs.shape[0] // ROW_TILE
    return pl.pallas_call(
        _expert_kernel,
        grid_spec=pltpu.PrefetchScalarGridSpec(
            num_scalar_prefetch=2,
            grid=(n_rows // TM_EXPERT,),
            in_specs=[pl.BlockSpec((TM_EXPERT * ROW_TILE, LANES),
                                   lambda t, tiles, nt: (jnp.minimum(t, nt[0] - 1), 0)),
                      pl.BlockSpec(memory_space=pl.ANY),
                      pl.BlockSpec(memory_space=pl.ANY),
                      pl.BlockSpec(memory_space=pl.ANY)],
            out_specs=pl.BlockSpec((TM_EXPERT * ROW_TILE, LANES), lambda t, tiles, nt: (t, 0)),
            scratch_shapes=[pltpu.VMEM((2, D_MODEL, D_EXPERT), F32),
                            pltpu.VMEM((2, D_MODEL, D_EXPERT), F32),
                            pltpu.VMEM((2, D_EXPERT, D_MODEL), F32),
                            pltpu.VMEM((D_MODEL, D_EXPERT), BF16),
                            pltpu.VMEM((D_MODEL, D_EXPERT), BF16),
                            pltpu.VMEM((D_EXPERT, D_MODEL), BF16),
                            pltpu.SMEM((4,), jnp.int32),
                            pltpu.SemaphoreType.DMA((2,))]),
        out_shape=jax.ShapeDtypeStruct((n_rows * ROW_TILE, LANES), F32),
        compiler_params=pltpu.CompilerParams(dimension_semantics=("arbitrary",),
                                             vmem_limit_bytes=VMEM_LIMIT),
        name="expert_mlp",
    )(tiles, n_tiles, xs, wg, wu, wd)


def _combine_kernel(dest_ref, h_ref, rw_ref, fg_ref, y_ref, o_ref, buf, sems):
    tm = TM_COMBINE
    i = pl.program_id(0)
    n_steps = pl.num_programs(0)
    n = n_steps * tm
    cur = i % 2

    def fetch(step, half):
        def body(r, c):
            for s in range(2):
                pltpu.make_async_copy(_token_rows(y_ref, dest_ref[s * n + step * tm + r], 1),
                                      _token_rows(buf.at[half, s], r, 1),
                                      sems.at[half]).start(priority=s)
            return c

        lax.fori_loop(0, tm, body, 0, unroll=8)

    @pl.when(i == 0)
    def _():
        fetch(0, 0)

    @pl.when(i + 1 < n_steps)
    def _():
        fetch(i + 1, 1 - cur)

    for s in range(2):
        pltpu.make_async_copy(_token_rows(y_ref, 0, tm), buf.at[cur, s], sems.at[cur]).wait()
    rw = rw_ref[...]
    out = (h_ref[...] + rw[:, 0:1] * _tiles_to_rows(buf.at[cur, 0], tm)
           + rw[:, 1:2] * _tiles_to_rows(buf.at[cur, 1], tm))
    o_ref[...] = _rms(out, fg_ref[...])


def _combine(dest, h, rw, final_g, ys):
    n = h.shape[0]
    return pl.pallas_call(
        _combine_kernel,
        grid_spec=pltpu.PrefetchScalarGridSpec(
            num_scalar_prefetch=1,
            grid=(n // TM_COMBINE,),
            in_specs=[pl.BlockSpec((TM_COMBINE, D_MODEL), lambda i, d: (i, 0)),
                      pl.BlockSpec((TM_COMBINE, LANES), lambda i, d: (i, 0)),
                      pl.BlockSpec((1, D_MODEL), lambda i, d: (0, 0)),
                      pl.BlockSpec(memory_space=pl.ANY)],
            out_specs=pl.BlockSpec((TM_COMBINE, D_MODEL), lambda i, d: (i, 0)),
            scratch_shapes=[pltpu.VMEM((2, 2, TM_COMBINE * ROW_TILE, LANES), F32),
                            pltpu.SemaphoreType.DMA((2,))]),
        out_shape=jax.ShapeDtypeStruct((n, D_MODEL), F32),
        compiler_params=pltpu.CompilerParams(dimension_semantics=("arbitrary",),
                                             vmem_limit_bytes=VMEM_LIMIT),
        name="combine",
    )(dest, h, rw, final_g, ys)


def _schedule(counts):
    tiles = (counts + TM_EXPERT - 1) // TM_EXPERT
    tile_end = jnp.cumsum(tiles)
    offsets = (tile_end - tiles) * TM_EXPERT
    return tiles, offsets, tile_end[-1:]


def _layer(x, attn_g, w_in, sg_g, w_sp, b_sp, sb_g, sg_out_g, w_out, ffn_g,
           w_rg, b_rg, w_re, b_re, w_gate, w_up, w_down):
    batch, seq, _ = x.shape
    n = batch * seq
    x2 = x.reshape(n, D_MODEL)
    row = lambda v: v.reshape(1, -1)

    qkv, gu, vgn = _inproj(x2, row(attn_g), w_in.astype(BF16), row(sg_g))
    sb = _attention(qkv, batch, seq).reshape(n, SB_WIDTH)

    w_r = jnp.concatenate([w_rg, jnp.transpose(w_re, (1, 0, 2)).reshape(D_MODEL, N_EXPERTS)], axis=1)
    w_r = jnp.pad(w_r, ((0, 0), (0, LANES - w_r.shape[1])))
    wr_hi = w_r.astype(BF16)
    wr_lo = (w_r - wr_hi.astype(F32)).astype(BF16)
    b_r = jnp.pad(jnp.concatenate([b_rg, b_re.reshape(-1)]), (0, LANES - N_GROUPS - N_EXPERTS))
    bsp_full = jnp.repeat(b_sp.T, HEAD_DIM, axis=1)

    h, hn, ri, rw, cnt = _mix(sb, gu, vgn, x2, w_sp, bsp_full, row(sb_g), row(sg_out_g),
                              w_out.astype(BF16), row(ffn_g), wr_hi, wr_lo, row(b_r))

    counts = cnt[0, ROUTER_LANE0:ROUTER_LANE0 + N_EXPERTS].astype(jnp.int32)
    n_rows = 2 * n + N_EXPERTS * TM_EXPERT
    tiles, offsets, n_tiles = _schedule(counts)
    expert, rank = ri[0:2], ri[2:4]
    is_e = expert[None] == jnp.arange(N_EXPERTS, dtype=jnp.int32)[:, None, None]
    dest = (jnp.sum(jnp.where(is_e, offsets[:, None, None], 0), axis=0) + rank).reshape(-1)
    pad_start = offsets + counts
    pad_count = (-counts) % TM_EXPERT

    xs = _dispatch(dest, pad_start, pad_count, n_tiles, hn, n_rows)
    ys = _experts(tiles, n_tiles, xs,
                  w_gate.reshape(N_EXPERTS, D_MODEL, D_EXPERT),
                  w_up.reshape(N_EXPERTS, D_MODEL, D_EXPERT),
                  w_down.reshape(N_EXPERTS, D_EXPERT, D_MODEL))
    return dest, h, rw, ys


def kernel(x, attn_norm_g, w_in, sg_norm_g, w_spatial, b_spatial, sb_out_norm_g, sg_out_norm_g,
           w_out, ffn_norm_g, w_router_group, b_router_group, w_router_expert, b_router_expert,
           w_gate, w_up, w_down, final_norm_g):
    assert attn_norm_g.shape[0] == 1, "single-layer problem"
    batch, seq, _ = x.shape
    dest, h, rw, ys = _layer(x, attn_norm_g[0], w_in[0], sg_norm_g[0], w_spatial[0], b_spatial[0],
                             sb_out_norm_g[0], sg_out_norm_g[0], w_out[0], ffn_norm_g[0],
                             w_router_group[0], b_router_group[0], w_router_expert[0],
                             b_router_expert[0], w_gate[0], w_up[0], w_down[0])
    out = _combine(dest, h, rw, final_norm_g.reshape(1, -1), ys)
    return out.reshape(batch, seq, D_MODEL)
```

```python
import functools
import math

import jax
import jax.numpy as jnp
from jax import lax
from jax.experimental import pallas as pl
from jax.experimental.pallas import tpu as pltpu

D_MODEL = 1024
HEAD_DIM = 64
SB_WIDTH = 512
SG_WIDTH = 512
SG_HEADS = 8
D_IN = 3 * SB_WIDTH + 2 * SG_WIDTH
CHUNK = 128
N_GROUPS = 4
EXPERTS_PER_GROUP = 8
N_EXPERTS = N_GROUPS * EXPERTS_PER_GROUP
D_EXPERT = 512
EPS = 1e-6
F32_EXP_UNDERFLOW = 110.0

LANES = 128
ROW_TILE = D_MODEL // LANES
assert ROW_TILE == 8
HEAD_PAIR = 2 * HEAD_DIM
ROUTER_LANE0 = N_GROUPS

TM_PROJ = 512
TQ_ATTN = 256
TM_MIX = 512
TM_EXPERT = 256
TM_COMBINE = 256
VMEM_LIMIT = 48 * 1024 * 1024

F32 = jnp.float32
BF16 = jnp.bfloat16


def _rms(x, g):
    return x * lax.rsqrt(jnp.mean(x * x, axis=-1, keepdims=True) + EPS) * g


def _gelu(x):
    c = math.sqrt(2.0 / math.pi)
    return x * (0.5 * (1.0 + jnp.tanh(c * (x + 0.044715 * (x * x * x)))))


def _softplus(z):
    return jnp.maximum(z, 0.0) + jnp.log(1.0 + jnp.exp(-jnp.abs(z)))


def _dot(a, b):
    return jnp.dot(a, b, preferred_element_type=F32)


def _rows_to_tiles(ref, x):
    m = x.shape[0]
    for k in range(ROW_TILE):
        ref[pl.ds(k, m, stride=ROW_TILE), :] = x[:, k * LANES:(k + 1) * LANES]


def _tiles_to_rows(ref, m):
    return jnp.concatenate([ref[pl.ds(k, m, stride=ROW_TILE), :] for k in range(ROW_TILE)], axis=1)


def _token_rows(ref, first_token, n_tokens):
    return ref.at[pl.ds(pl.multiple_of(first_token * ROW_TILE, ROW_TILE), n_tokens * ROW_TILE)]


def _split_bf16(x):
    hi = x.astype(BF16)
    lo = (x - hi.astype(F32)).astype(BF16)
    return hi, lo


def _inproj_kernel(x_ref, g_ref, w_ref, sgg_ref, qkv_ref, gu_ref, vgn_ref):
    hb = _rms(x_ref[...], g_ref[...]).astype(BF16)
    q = _dot(hb, w_ref[:, 0:SB_WIDTH]) * (1.0 / math.sqrt(HEAD_DIM))
    qkv_ref[:, 0:SB_WIDTH] = q.astype(BF16)
    qkv_ref[:, SB_WIDTH:3 * SB_WIDTH] = _dot(hb, w_ref[:, SB_WIDTH:3 * SB_WIDTH]).astype(BF16)
    gu_ref[...] = _gelu(_dot(hb, w_ref[:, 3 * SB_WIDTH:3 * SB_WIDTH + SG_WIDTH]))
    gv = _gelu(_dot(hb, w_ref[:, 3 * SB_WIDTH + SG_WIDTH:D_IN]))
    vgn_ref[...] = _rms(gv, sgg_ref[...]).astype(BF16)


def _inproj(x2, attn_g, w_in_b, sg_g):
    n = x2.shape[0]
    row = lambda i: (i, 0)
    const = lambda i: (0, 0)
    return pl.pallas_call(
        _inproj_kernel,
        grid=(n // TM_PROJ,),
        in_specs=[pl.BlockSpec((TM_PROJ, D_MODEL), row),
                  pl.BlockSpec((1, D_MODEL), const),
                  pl.BlockSpec((D_MODEL, D_IN), const),
                  pl.BlockSpec((1, SG_WIDTH), const)],
        out_specs=[pl.BlockSpec((TM_PROJ, 3 * SB_WIDTH), row),
                   pl.BlockSpec((TM_PROJ, SG_WIDTH), row),
                   pl.BlockSpec((TM_PROJ, SG_WIDTH), row)],
        out_shape=[jax.ShapeDtypeStruct((n, 3 * SB_WIDTH), BF16),
                   jax.ShapeDtypeStruct((n, SG_WIDTH), F32),
                   jax.ShapeDtypeStruct((n, SG_WIDTH), BF16)],
        compiler_params=pltpu.CompilerParams(dimension_semantics=("arbitrary",),
                                             vmem_limit_bytes=VMEM_LIMIT),
        name="inproj",
    )(x2, attn_g, w_in_b, sg_g)


def _attn_kernel(q_ref, k_ref, v_ref, o_ref, q2_ref, carry_ref):
    t = TQ_ATTN
    n_pairs = SB_WIDTH // HEAD_PAIR
    qi = pl.program_id(1)
    lane = lax.broadcasted_iota(jnp.int32, (1, HEAD_PAIR), 1)
    head_lanes = (lane < HEAD_DIM, lane >= HEAD_DIM)
    zero = jnp.zeros((), BF16)
    for p in range(n_pairs):
        qp = q_ref[0, :, p * HEAD_PAIR:(p + 1) * HEAD_PAIR]
        for h in range(2):
            q2_ref[(2 * p + h) * t:(2 * p + h + 1) * t, :] = jnp.where(head_lanes[h], qp, zero)
    r_idx = lax.broadcasted_iota(jnp.int32, (t, t), 0)
    c_idx = lax.broadcasted_iota(jnp.int32, (t, t), 1)
    suffix = (r_idx > c_idx).astype(BF16)
    suffix2 = jnp.concatenate([suffix, suffix], axis=0)
    causal = c_idx < r_idx

    o_ref[...] = jnp.zeros_like(o_ref)
    carry_ref[...] = jnp.zeros_like(carry_ref)

    def block(j, diag):
        start = pl.multiple_of(j * t, t)
        for p in range(n_pairs):
            cols = slice(p * HEAD_PAIR, (p + 1) * HEAD_PAIR)
            rows = slice(2 * p * t, (2 * p + 2) * t)
            kb = k_ref[0, pl.ds(start, t), cols]
            vb = v_ref[0, pl.ds(start, t), cols]
            z = lax.dot_general(q2_ref[rows, :], kb, (((1,), (1,)), ((), ())),
                                preferred_element_type=F32)
            sp = _softplus(z)
            if diag:
                mask2 = jnp.concatenate([causal, causal], axis=0)
                nl = jnp.where(mask2, sp, 0.0)
            else:
                nl = sp
            hi, lo = _split_bf16(nl)
            hl = jnp.concatenate([hi, lo], axis=1)
            after = jnp.concatenate([_dot(hl[0:t], suffix2), _dot(hl[t:2 * t], suffix2)], axis=0)
            carry = carry_ref[rows, :]
            a = jnp.exp(z - sp - after - carry)
            if diag:
                a = jnp.where(mask2, a, 0.0)
            a = a.astype(BF16)
            a2 = jnp.concatenate([a[0:t], a[t:2 * t]], axis=1)
            v2 = jnp.concatenate([jnp.where(head_lanes[0], vb, zero),
                                  jnp.where(head_lanes[1], vb, zero)], axis=0)
            o_ref[0, :, cols] += _dot(a2, v2)
            carry_ref[rows, :] = carry + after[:, 0:1] + nl[:, 0:1]

    def live():
        return jnp.min(carry_ref[...]) < F32_EXP_UNDERFLOW

    block(qi, True)

    def body(state):
        it, _ = state
        block(qi - 1 - it, False)
        return it + 1, live()

    lax.while_loop(lambda s: (s[0] < qi) & s[1], body, (jnp.int32(0), live()))


def _attention(qkv, batch, seq):
    qkv3 = qkv.reshape(batch, seq, 3 * SB_WIDTH)
    n_heads = SB_WIDTH // HEAD_DIM
    return pl.pallas_call(
        _attn_kernel,
        grid=(batch, seq // TQ_ATTN),
        in_specs=[pl.BlockSpec((1, TQ_ATTN, SB_WIDTH), lambda b, i: (b, i, 0)),
                  pl.BlockSpec((1, seq, SB_WIDTH), lambda b, i: (b, 0, 1)),
                  pl.BlockSpec((1, seq, SB_WIDTH), lambda b, i: (b, 0, 2))],
        out_specs=pl.BlockSpec((1, TQ_ATTN, SB_WIDTH), lambda b, i: (b, i, 0)),
        out_shape=jax.ShapeDtypeStruct((batch, seq, SB_WIDTH), F32),
        scratch_shapes=[pltpu.VMEM((n_heads * TQ_ATTN, HEAD_PAIR), BF16),
                        pltpu.VMEM((n_heads * TQ_ATTN, 1), F32)],
        compiler_params=pltpu.CompilerParams(dimension_semantics=("arbitrary",) * 2,
                                             vmem_limit_bytes=VMEM_LIMIT),
        name="sb_attention",
    )(qkv3, qkv3, qkv3)


def _mix_kernel(sb_ref, gu_ref, vgn_ref, x_ref, wsp_ref, bsp_ref, sbg_ref, sgg_ref, wout_ref,
                ffng_ref, wrh_ref, wrl_ref, br_ref,
                h_ref, hn_ref, ri_ref, rw_ref, cnt_ref, count_ref, sg_ref):
    tm = TM_MIX
    i = pl.program_id(0)

    @pl.when(i == 0)
    def _():
        count_ref[...] = jnp.zeros_like(count_ref)

    lane = lax.broadcasted_iota(jnp.int32, (1, LANES), 1)
    first = lane < HEAD_DIM
    zero = jnp.zeros((), BF16)
    r_c = lax.broadcasted_iota(jnp.int32, (CHUNK, CHUNK), 0)
    c_c = lax.broadcasted_iota(jnp.int32, (CHUNK, CHUNK), 1)
    tril = r_c >= c_c
    n_pairs = SG_WIDTH // HEAD_PAIR
    w_pairs = []
    for p in range(n_pairs):
        w0 = jnp.where(tril, wsp_ref[2 * p], 0.0).astype(BF16)
        w1 = jnp.where(tril, wsp_ref[2 * p + 1], 0.0).astype(BF16)
        w_pairs.append(jnp.concatenate([w0, w1], axis=1))
    bsp = bsp_ref[...]
    for c in range(tm // CHUNK):
        rows = slice(c * CHUNK, (c + 1) * CHUNK)
        for p in range(n_pairs):
            cols = slice(p * HEAD_PAIR, (p + 1) * HEAD_PAIR)
            vg = vgn_ref[rows, cols]
            rhs = jnp.concatenate([jnp.where(first, vg, zero), jnp.where(first, zero, vg)], axis=0)
            mixed = _dot(w_pairs[p], rhs) + bsp[:, cols]
            sg_ref[rows, cols] = gu_ref[rows, cols] * mixed
    sgn = _rms(sg_ref[...], sgg_ref[...]).astype(BF16)
    sbn = _rms(sb_ref[...], sbg_ref[...]).astype(BF16)
    h = x_ref[...] + _dot(sbn, wout_ref[0:SB_WIDTH, :]) + _dot(sgn, wout_ref[SB_WIDTH:, :])
    h_ref[...] = h
    hn = _rms(h, ffng_ref[...])
    _rows_to_tiles(hn_ref, hn)

    hn_hi, hn_lo = _split_bf16(hn)
    wrh = wrh_ref[...]
    logits = _dot(hn_hi, wrh) + _dot(hn_lo, wrh) + _dot(hn_hi, wrl_ref[...]) + br_ref[...]

    lane_t = lax.broadcasted_iota(jnp.int32, (tm, LANES), 1)
    neg = jnp.float32(-jnp.inf)
    gl = jnp.where(lane_t < N_GROUPS, logits, neg)
    gmax = jnp.max(gl, axis=-1, keepdims=True)
    gidx = jnp.min(jnp.where(gl == gmax, lane_t, LANES), axis=-1, keepdims=True)
    gsum = jnp.sum(jnp.exp(gl - gmax), axis=-1, keepdims=True)
    gweight = 1.0 / gsum
    lo_lane = ROUTER_LANE0 + EXPERTS_PER_GROUP * gidx
    el = jnp.where((lane_t >= lo_lane) & (lane_t < lo_lane + EXPERTS_PER_GROUP), logits, neg)
    m1 = jnp.max(el, axis=-1, keepdims=True)
    i1 = jnp.min(jnp.where(el == m1, lane_t, LANES), axis=-1, keepdims=True)
    el2 = jnp.where(lane_t == i1, neg, el)
    m2 = jnp.max(el2, axis=-1, keepdims=True)
    i2 = jnp.min(jnp.where(el2 == m2, lane_t, LANES), axis=-1, keepdims=True)
    t21 = jnp.exp(m2 - m1)
    w1 = gweight / (1.0 + t21)
    w2 = gweight * t21 / (1.0 + t21)

    sel1 = lane_t == i1
    sel2 = lane_t == i2
    onehot = jnp.where(sel1 | sel2, 1.0, 0.0)
    r_t = lax.broadcasted_iota(jnp.int32, (tm, tm), 0)
    c_t = lax.broadcasted_iota(jnp.int32, (tm, tm), 1)
    before = (r_t > c_t).astype(BF16)
    running = count_ref[0:1, :] + _dot(before, onehot.astype(BF16))
    rank1 = jnp.sum(jnp.where(sel1, running, 0.0), axis=-1, keepdims=True)
    rank2 = jnp.sum(jnp.where(sel2, running, 0.0), axis=-1, keepdims=True)
    new_count = count_ref[0:1, :] + jnp.sum(onehot, axis=0, keepdims=True)
    count_ref[...] = jnp.broadcast_to(new_count, count_ref.shape)
    cnt_ref[...] = jnp.broadcast_to(new_count, cnt_ref.shape)

    e1 = i1 - ROUTER_LANE0
    e2 = i2 - ROUTER_LANE0
    ri = jnp.where(lane_t == 0, e1.astype(F32), jnp.where(lane_t == 1, e2.astype(F32), jnp.where(
        lane_t == 2, rank1, jnp.where(lane_t == 3, rank2, 0.0))))
    ri_ref[...] = ri.T[0:8, :].astype(jnp.int32)
    rw_ref[...] = jnp.where(lane_t == 0, w1, jnp.where(lane_t == 1, w2, 0.0))


def _mix(sb, gu, vgn, x2, wsp, bsp_full, sb_g, sg_g, w_out_b, ffn_g, wr_hi, wr_lo, br):
    n = x2.shape[0]
    row = lambda i: (i, 0)
    const = lambda i: (0, 0)
    return pl.pallas_call(
        _mix_kernel,
        grid=(n // TM_MIX,),
        in_specs=[pl.BlockSpec((TM_MIX, SB_WIDTH), row),
                  pl.BlockSpec((TM_MIX, SG_WIDTH), row),
                  pl.BlockSpec((TM_MIX, SG_WIDTH), row),
                  pl.BlockSpec((TM_MIX, D_MODEL), row),
                  pl.BlockSpec((SG_HEADS, CHUNK, CHUNK), lambda i: (0, 0, 0)),
                  pl.BlockSpec((CHUNK, SG_WIDTH), const),
                  pl.BlockSpec((1, SB_WIDTH), const),
                  pl.BlockSpec((1, SG_WIDTH), const),
                  pl.BlockSpec((D_MODEL, D_MODEL), const),
                  pl.BlockSpec((1, D_MODEL), const),
                  pl.BlockSpec((D_MODEL, LANES), const),
                  pl.BlockSpec((D_MODEL, LANES), const),
                  pl.BlockSpec((1, LANES), const)],
        out_specs=[pl.BlockSpec((TM_MIX, D_MODEL), row),
                   pl.BlockSpec((TM_MIX * ROW_TILE, LANES), row),
                   pl.BlockSpec((8, TM_MIX), lambda i: (0, i)),
                   pl.BlockSpec((TM_MIX, LANES), row),
                   pl.BlockSpec((8, LANES), const)],
        out_shape=[jax.ShapeDtypeStruct((n, D_MODEL), F32),
                   jax.ShapeDtypeStruct((n * ROW_TILE, LANES), F32),
                   jax.ShapeDtypeStruct((8, n), jnp.int32),
                   jax.ShapeDtypeStruct((n, LANES), F32),
                   jax.ShapeDtypeStruct((8, LANES), F32)],
        scratch_shapes=[pltpu.VMEM((8, LANES), F32),
                        pltpu.VMEM((TM_MIX, SG_WIDTH), F32)],
        compiler_params=pltpu.CompilerParams(dimension_semantics=("arbitrary",),
                                             vmem_limit_bytes=VMEM_LIMIT),
        name="mix_router",
    )(sb, gu, vgn, x2, wsp, bsp_full, sb_g, sg_g, w_out_b, ffn_g, wr_hi, wr_lo, br)


X_SLOTS = 3


def _expert_kernel(tiles_ref, nt_ref, src_ref, hn_ref, wg_ref, wu_ref, wd_ref, y_ref,
                   x_buf, sg_buf, su_buf, sd_buf, wgb, wub, wdb, state, w_sems, x_sems):
    tm = TM_EXPERT
    t = pl.program_id(0)
    nt = nt_ref[0]

    def gather(tile):
        slot = lax.rem(tile, X_SLOTS)

        def body(j, c):
            for u in range(2):
                r = 2 * j + u
                pltpu.make_async_copy(_token_rows(hn_ref, src_ref[tile * tm + r], 1),
                                      _token_rows(x_buf.at[slot], r, 1),
                                      x_sems.at[slot]).start(priority=u)
            return c

        lax.fori_loop(0, tm // 2, body, 0, unroll=4)

    def weight_copies(e, slot):
        return (pltpu.make_async_copy(wg_ref.at[e], sg_buf.at[slot], w_sems.at[slot]),
                pltpu.make_async_copy(wu_ref.at[e], su_buf.at[slot], w_sems.at[slot]),
                pltpu.make_async_copy(wd_ref.at[e], sd_buf.at[slot], w_sems.at[slot]))

    def next_with_rows(e):
        return lax.while_loop(lambda k: (k < N_EXPERTS) & (tiles_ref[jnp.minimum(k, N_EXPERTS - 1)] == 0),
                              lambda k: k + 1, e + 1)

    @pl.when(t == 0)
    def _():
        first = next_with_rows(jnp.int32(-1))
        state[0] = jnp.int32(-1)
        state[1] = jnp.int32(0)
        state[2] = jnp.int32(1)
        state[3] = first
        for cp in weight_copies(first, 0):
            cp.start()
        gather(0)

        @pl.when(nt > 1)
        def _():
            gather(1)

    @pl.when(t + 2 < nt)
    def _():
        gather(t + 2)

    @pl.when(t < nt)
    def _():
        @pl.when(state[1] == 0)
        def _():
            e = state[3]
            slot = 1 - state[2]
            nxt = next_with_rows(e)
            state[0] = e
            state[1] = tiles_ref[e]
            state[2] = slot
            state[3] = nxt
            for cp in weight_copies(e, slot):
                cp.wait()

            @pl.when(nxt < N_EXPERTS)
            def _():
                for cp in weight_copies(nxt, 1 - slot):
                    cp.start()

            wgb[...] = sg_buf[slot].astype(BF16)
            wub[...] = su_buf[slot].astype(BF16)
            wdb[...] = sd_buf[slot].astype(BF16)

        state[1] = state[1] - 1
        x_slot = x_buf.at[lax.rem(t, X_SLOTS)]
        pltpu.make_async_copy(_token_rows(hn_ref, 0, tm), x_slot, x_sems.at[lax.rem(t, X_SLOTS)]).wait()
        x = _tiles_to_rows(x_slot, tm).astype(BF16)
        g = _dot(x, wgb[...])
        u = _dot(x, wub[...])
        hidden = (g * jax.nn.sigmoid(g)) * u
        _rows_to_tiles(y_ref, _dot(hidden.astype(BF16), wdb[...]))

    @pl.when(t >= nt)
    def _():
        y_ref[...] = jnp.zeros_like(y_ref)


def _experts(tiles, n_tiles, src, hn_tiles, wg, wu, wd):
    n_rows = src.shape[0]
    any_spec = pl.BlockSpec(memory_space=pl.ANY)
    return pl.pallas_call(
        _expert_kernel,
        grid_spec=pltpu.PrefetchScalarGridSpec(
            num_scalar_prefetch=3,
            grid=(n_rows // TM_EXPERT,),
            in_specs=[any_spec, any_spec, any_spec, any_spec],
            out_specs=pl.BlockSpec((TM_EXPERT * ROW_TILE, LANES), lambda t, *_: (t, 0)),
            scratch_shapes=[pltpu.VMEM((X_SLOTS, TM_EXPERT * ROW_TILE, LANES), F32),
                            pltpu.VMEM((2, D_MODEL, D_EXPERT), F32),
                            pltpu.VMEM((2, D_MODEL, D_EXPERT), F32),
                            pltpu.VMEM((2, D_EXPERT, D_MODEL), F32),
                            pltpu.VMEM((D_MODEL, D_EXPERT), BF16),
                            pltpu.VMEM((D_MODEL, D_EXPERT), BF16),
                            pltpu.VMEM((D_EXPERT, D_MODEL), BF16),
                            pltpu.SMEM((4,), jnp.int32),
                            pltpu.SemaphoreType.DMA((2,)),
                            pltpu.SemaphoreType.DMA((X_SLOTS,))]),
        out_shape=jax.ShapeDtypeStruct((n_rows * ROW_TILE, LANES), F32),
        compiler_params=pltpu.CompilerParams(dimension_semantics=("arbitrary",),
                                             vmem_limit_bytes=VMEM_LIMIT),
        name="expert_mlp",
    )(tiles, n_tiles, src, hn_tiles, wg, wu, wd)


def _combine_kernel(dest_ref, h_ref, rw_ref, fg_ref, y_ref, o_ref, buf, sems):
    tm = TM_COMBINE
    i = pl.program_id(0)
    n_steps = pl.num_programs(0)
    n = n_steps * tm
    cur = i % 2

    def fetch(step, half):
        def body(r, c):
            for s in range(2):
                pltpu.make_async_copy(_token_rows(y_ref, dest_ref[s * n + step * tm + r], 1),
                                      _token_rows(buf.at[half, s], r, 1),
                                      sems.at[half]).start(priority=s)
            return c

        lax.fori_loop(0, tm, body, 0, unroll=8)

    @pl.when(i == 0)
    def _():
        fetch(0, 0)

    @pl.when(i + 1 < n_steps)
    def _():
        fetch(i + 1, 1 - cur)

    for s in range(2):
        pltpu.make_async_copy(_token_rows(y_ref, 0, tm), buf.at[cur, s], sems.at[cur]).wait()
    rw = rw_ref[...]
    out = (h_ref[...] + rw[:, 0:1] * _tiles_to_rows(buf.at[cur, 0], tm)
           + rw[:, 1:2] * _tiles_to_rows(buf.at[cur, 1], tm))
    o_ref[...] = _rms(out, fg_ref[...])


def _combine(dest, h, rw, final_g, ys):
    n = h.shape[0]
    return pl.pallas_call(
        _combine_kernel,
        grid_spec=pltpu.PrefetchScalarGridSpec(
            num_scalar_prefetch=1,
            grid=(n // TM_COMBINE,),
            in_specs=[pl.BlockSpec((TM_COMBINE, D_MODEL), lambda i, d: (i, 0)),
                      pl.BlockSpec((TM_COMBINE, LANES), lambda i, d: (i, 0)),
                      pl.BlockSpec((1, D_MODEL), lambda i, d: (0, 0)),
                      pl.BlockSpec(memory_space=pl.ANY)],
            out_specs=pl.BlockSpec((TM_COMBINE, D_MODEL), lambda i, d: (i, 0)),
            scratch_shapes=[pltpu.VMEM((2, 2, TM_COMBINE * ROW_TILE, LANES), F32),
                            pltpu.SemaphoreType.DMA((2,))]),
        out_shape=jax.ShapeDtypeStruct((n, D_MODEL), F32),
        compiler_params=pltpu.CompilerParams(dimension_semantics=("arbitrary",),
                                             vmem_limit_bytes=VMEM_LIMIT),
        name="combine",
    )(dest, h, rw, final_g, ys)


def _schedule(counts):
    tiles = (counts + TM_EXPERT - 1) // TM_EXPERT
    tile_end = jnp.cumsum(tiles)
    offsets = (tile_end - tiles) * TM_EXPERT
    return tiles, offsets, tile_end[-1:]


def _layer(x, attn_g, w_in, sg_g, w_sp, b_sp, sb_g, sg_out_g, w_out, ffn_g,
           w_rg, b_rg, w_re, b_re, w_gate, w_up, w_down):
    batch, seq, _ = x.shape
    n = batch * seq
    x2 = x.reshape(n, D_MODEL)
    row = lambda v: v.reshape(1, -1)

    qkv, gu, vgn = _inproj(x2, row(attn_g), w_in.astype(BF16), row(sg_g))
    sb = _attention(qkv, batch, seq).reshape(n, SB_WIDTH)

    w_r = jnp.concatenate([w_rg, jnp.transpose(w_re, (1, 0, 2)).reshape(D_MODEL, N_EXPERTS)], axis=1)
    w_r = jnp.pad(w_r, ((0, 0), (0, LANES - w_r.shape[1])))
    wr_hi = w_r.astype(BF16)
    wr_lo = (w_r - wr_hi.astype(F32)).astype(BF16)
    b_r = jnp.pad(jnp.concatenate([b_rg, b_re.reshape(-1)]), (0, LANES - N_GROUPS - N_EXPERTS))
    bsp_full = jnp.repeat(b_sp.T, HEAD_DIM, axis=1)

    h, hn, ri, rw, cnt = _mix(sb, gu, vgn, x2, w_sp, bsp_full, row(sb_g), row(sg_out_g),
                              w_out.astype(BF16), row(ffn_g), wr_hi, wr_lo, row(b_r))

    counts = cnt[0, ROUTER_LANE0:ROUTER_LANE0 + N_EXPERTS].astype(jnp.int32)
    n_rows = 2 * n + N_EXPERTS * TM_EXPERT
    tiles, offsets, n_tiles = _schedule(counts)
    expert, rank = ri[0:2], ri[2:4]
    is_e = expert[None] == jnp.arange(N_EXPERTS, dtype=jnp.int32)[:, None, None]
    dest = (jnp.sum(jnp.where(is_e, offsets[:, None, None], 0), axis=0) + rank).reshape(-1)
    token = jnp.tile(jnp.arange(n, dtype=jnp.int32), 2)
    src = jnp.zeros((n_rows,), jnp.int32).at[dest].set(token, unique_indices=True)

    ys = _experts(tiles, n_tiles, src, hn,
                  w_gate.reshape(N_EXPERTS, D_MODEL, D_EXPERT),
                  w_up.reshape(N_EXPERTS, D_MODEL, D_EXPERT),
                  w_down.reshape(N_EXPERTS, D_EXPERT, D_MODEL))
    return dest, h, rw, ys


def kernel(x, attn_norm_g, w_in, sg_norm_g, w_spatial, b_spatial, sb_out_norm_g, sg_out_norm_g,
           w_out, ffn_norm_g, w_router_group, b_router_group, w_router_expert, b_router_expert,
           w_gate, w_up, w_down, final_norm_g):
    assert attn_norm_g.shape[0] == 1, "single-layer problem"
    batch, seq, _ = x.shape
    dest, h, rw, ys = _layer(x, attn_norm_g[0], w_in[0], sg_norm_g[0], w_spatial[0], b_spatial[0],
                             sb_out_norm_g[0], sg_out_norm_g[0], w_out[0], ffn_norm_g[0],
                             w_router_group[0], b_router_group[0], w_router_expert[0],
                             b_router_expert[0], w_gate[0], w_up[0], w_down[0])
    out = _combine(dest, h, rw, final_norm_g.reshape(1, -1), ys)
    return out.reshape(batch, seq, D_MODEL)
```

```python
import functools
import math

import jax
import jax.numpy as jnp
from jax import lax
from jax.experimental import pallas as pl
from jax.experimental.pallas import tpu as pltpu

D_MODEL = 1024
HEAD_DIM = 64
SB_WIDTH = 512
SG_WIDTH = 512
SG_HEADS = 8
D_IN = 3 * SB_WIDTH + 2 * SG_WIDTH
CHUNK = 128
N_GROUPS = 4
EXPERTS_PER_GROUP = 8
N_EXPERTS = N_GROUPS * EXPERTS_PER_GROUP
D_EXPERT = 512
EPS = 1e-6
F32_EXP_UNDERFLOW = 110.0

LANES = 128
ROW_TILE = D_MODEL // LANES
assert ROW_TILE == 8
HEAD_PAIR = 2 * HEAD_DIM
ROUTER_LANE0 = N_GROUPS

TM_PROJ = 512
TQ_ATTN = 256
TM_MIX = 512
TM_DISPATCH = 512
TM_EXPERT = 256
TM_COMBINE = 256
VMEM_LIMIT = 48 * 1024 * 1024

F32 = jnp.float32
BF16 = jnp.bfloat16


def _rms(x, g):
    return x * lax.rsqrt(jnp.mean(x * x, axis=-1, keepdims=True) + EPS) * g


def _gelu(x):
    c = math.sqrt(2.0 / math.pi)
    return x * (0.5 * (1.0 + jnp.tanh(c * (x + 0.044715 * (x * x * x)))))


def _softplus(z):
    return jnp.maximum(z, 0.0) + jnp.log(1.0 + jnp.exp(-jnp.abs(z)))


def _dot(a, b):
    return jnp.dot(a, b, preferred_element_type=F32)


def _rows_to_tiles(ref, x):
    m = x.shape[0]
    for k in range(ROW_TILE):
        ref[pl.ds(k, m, stride=ROW_TILE), :] = x[:, k * LANES:(k + 1) * LANES]


def _tiles_to_rows(ref, m):
    return jnp.concatenate([ref[pl.ds(k, m, stride=ROW_TILE), :] for k in range(ROW_TILE)], axis=1)


def _token_rows(ref, first_token, n_tokens):
    return ref.at[pl.ds(pl.multiple_of(first_token * ROW_TILE, ROW_TILE), n_tokens * ROW_TILE)]


def _split_bf16(x):
    hi = x.astype(BF16)
    lo = (x - hi.astype(F32)).astype(BF16)
    return hi, lo


def _inproj_kernel(x_ref, g_ref, w_ref, sgg_ref, qkv_ref, gu_ref, vgn_ref):
    hb = _rms(x_ref[...], g_ref[...]).astype(BF16)
    q = _dot(hb, w_ref[:, 0:SB_WIDTH]) * (1.0 / math.sqrt(HEAD_DIM))
    qkv_ref[:, 0:SB_WIDTH] = q.astype(BF16)
    qkv_ref[:, SB_WIDTH:3 * SB_WIDTH] = _dot(hb, w_ref[:, SB_WIDTH:3 * SB_WIDTH]).astype(BF16)
    gu_ref[...] = _gelu(_dot(hb, w_ref[:, 3 * SB_WIDTH:3 * SB_WIDTH + SG_WIDTH]))
    gv = _gelu(_dot(hb, w_ref[:, 3 * SB_WIDTH + SG_WIDTH:D_IN]))
    vgn_ref[...] = _rms(gv, sgg_ref[...]).astype(BF16)


def _inproj(x2, attn_g, w_in_b, sg_g):
    n = x2.shape[0]
    row = lambda i: (i, 0)
    const = lambda i: (0, 0)
    return pl.pallas_call(
        _inproj_kernel,
        grid=(n // TM_PROJ,),
        in_specs=[pl.BlockSpec((TM_PROJ, D_MODEL), row),
                  pl.BlockSpec((1, D_MODEL), const),
                  pl.BlockSpec((D_MODEL, D_IN), const),
                  pl.BlockSpec((1, SG_WIDTH), const)],
        out_specs=[pl.BlockSpec((TM_PROJ, 3 * SB_WIDTH), row),
                   pl.BlockSpec((TM_PROJ, SG_WIDTH), row),
                   pl.BlockSpec((TM_PROJ, SG_WIDTH), row)],
        out_shape=[jax.ShapeDtypeStruct((n, 3 * SB_WIDTH), BF16),
                   jax.ShapeDtypeStruct((n, SG_WIDTH), F32),
                   jax.ShapeDtypeStruct((n, SG_WIDTH), BF16)],
        compiler_params=pltpu.CompilerParams(dimension_semantics=("arbitrary",),
                                             vmem_limit_bytes=VMEM_LIMIT),
        name="inproj",
    )(x2, attn_g, w_in_b, sg_g)


def _attn_kernel(q_ref, k_ref, v_ref, o_ref, q2_ref, carry_ref):
    t = TQ_ATTN
    n_pairs = SB_WIDTH // HEAD_PAIR
    qi = pl.program_id(1)
    lane = lax.broadcasted_iota(jnp.int32, (1, HEAD_PAIR), 1)
    head_lanes = (lane < HEAD_DIM, lane >= HEAD_DIM)
    zero = jnp.zeros((), BF16)
    for p in range(n_pairs):
        qp = q_ref[0, :, p * HEAD_PAIR:(p + 1) * HEAD_PAIR]
        for h in range(2):
            q2_ref[(2 * p + h) * t:(2 * p + h + 1) * t, :] = jnp.where(head_lanes[h], qp, zero)
    r_idx = lax.broadcasted_iota(jnp.int32, (t, t), 0)
    c_idx = lax.broadcasted_iota(jnp.int32, (t, t), 1)
    suffix = (r_idx > c_idx).astype(BF16)
    suffix2 = jnp.concatenate([suffix, suffix], axis=0)
    causal = c_idx < r_idx

    o_ref[...] = jnp.zeros_like(o_ref)
    carry_ref[...] = jnp.zeros_like(carry_ref)

    def block(j, diag):
        start = pl.multiple_of(j * t, t)
        for p in range(n_pairs):
            cols = slice(p * HEAD_PAIR, (p + 1) * HEAD_PAIR)
            rows = slice(2 * p * t, (2 * p + 2) * t)
            kb = k_ref[0, pl.ds(start, t), cols]
            vb = v_ref[0, pl.ds(start, t), cols]
            z = lax.dot_general(q2_ref[rows, :], kb, (((1,), (1,)), ((), ())),
                                preferred_element_type=F32)
            sp = _softplus(z)
            if diag:
                mask2 = jnp.concatenate([causal, causal], axis=0)
                nl = jnp.where(mask2, sp, 0.0)
            else:
                nl = sp
            hi, lo = _split_bf16(nl)
            hl = jnp.concatenate([hi, lo], axis=1)
            after = jnp.concatenate([_dot(hl[0:t], suffix2), _dot(hl[t:2 * t], suffix2)], axis=0)
            carry = carry_ref[rows, :]
            a = jnp.exp(z - sp - after - carry)
            if diag:
                a = jnp.where(mask2, a, 0.0)
            a = a.astype(BF16)
            a2 = jnp.concatenate([a[0:t], a[t:2 * t]], axis=1)
            v2 = jnp.concatenate([jnp.where(head_lanes[0], vb, zero),
                                  jnp.where(head_lanes[1], vb, zero)], axis=0)
            o_ref[0, :, cols] += _dot(a2, v2)
            carry_ref[rows, :] = carry + after[:, 0:1] + nl[:, 0:1]

    def live():
        return jnp.min(carry_ref[...]) < F32_EXP_UNDERFLOW

    block(qi, True)

    def body(state):
        it, _ = state
        block(qi - 1 - it, False)
        return it + 1, live()

    lax.while_loop(lambda s: (s[0] < qi) & s[1], body, (jnp.int32(0), live()))


def _attention(qkv, batch, seq):
    qkv3 = qkv.reshape(batch, seq, 3 * SB_WIDTH)
    n_heads = SB_WIDTH // HEAD_DIM
    return pl.pallas_call(
        _attn_kernel,
        grid=(batch, seq // TQ_ATTN),
        in_specs=[pl.BlockSpec((1, TQ_ATTN, SB_WIDTH), lambda b, i: (b, i, 0)),
                  pl.BlockSpec((1, seq, SB_WIDTH), lambda b, i: (b, 0, 1)),
                  pl.BlockSpec((1, seq, SB_WIDTH), lambda b, i: (b, 0, 2))],
        out_specs=pl.BlockSpec((1, TQ_ATTN, SB_WIDTH), lambda b, i: (b, i, 0)),
        out_shape=jax.ShapeDtypeStruct((batch, seq, SB_WIDTH), F32),
        scratch_shapes=[pltpu.VMEM((n_heads * TQ_ATTN, HEAD_PAIR), BF16),
                        pltpu.VMEM((n_heads * TQ_ATTN, 1), F32)],
        compiler_params=pltpu.CompilerParams(dimension_semantics=("arbitrary",) * 2,
                                             vmem_limit_bytes=VMEM_LIMIT),
        name="sb_attention",
    )(qkv3, qkv3, qkv3)


def _mix_kernel(sb_ref, gu_ref, vgn_ref, x_ref, wsp_ref, bsp_ref, sbg_ref, sgg_ref, wout_ref,
                ffng_ref, wrh_ref, wrl_ref, br_ref,
                h_ref, hn_ref, ri_ref, rw_ref, cnt_ref, count_ref, sg_ref):
    tm = TM_MIX
    i = pl.program_id(0)

    @pl.when(i == 0)
    def _():
        count_ref[...] = jnp.zeros_like(count_ref)

    lane = lax.broadcasted_iota(jnp.int32, (1, LANES), 1)
    first = lane < HEAD_DIM
    zero = jnp.zeros((), BF16)
    r_c = lax.broadcasted_iota(jnp.int32, (CHUNK, CHUNK), 0)
    c_c = lax.broadcasted_iota(jnp.int32, (CHUNK, CHUNK), 1)
    tril = r_c >= c_c
    n_pairs = SG_WIDTH // HEAD_PAIR
    w_pairs = []
    for p in range(n_pairs):
        w0 = jnp.where(tril, wsp_ref[2 * p], 0.0).astype(BF16)
        w1 = jnp.where(tril, wsp_ref[2 * p + 1], 0.0).astype(BF16)
        w_pairs.append(jnp.concatenate([w0, w1], axis=1))
    bsp = bsp_ref[...]
    for c in range(tm // CHUNK):
        rows = slice(c * CHUNK, (c + 1) * CHUNK)
        for p in range(n_pairs):
            cols = slice(p * HEAD_PAIR, (p + 1) * HEAD_PAIR)
            vg = vgn_ref[rows, cols]
            rhs = jnp.concatenate([jnp.where(first, vg, zero), jnp.where(first, zero, vg)], axis=0)
            mixed = _dot(w_pairs[p], rhs) + bsp[:, cols]
            sg_ref[rows, cols] = gu_ref[rows, cols] * mixed
    sgn = _rms(sg_ref[...], sgg_ref[...]).astype(BF16)
    sbn = _rms(sb_ref[...], sbg_ref[...]).astype(BF16)
    h = x_ref[...] + _dot(sbn, wout_ref[0:SB_WIDTH, :]) + _dot(sgn, wout_ref[SB_WIDTH:, :])
    h_ref[...] = h
    hn = _rms(h, ffng_ref[...])
    _rows_to_tiles(hn_ref, hn)

    hn_hi, hn_lo = _split_bf16(hn)
    wrh = wrh_ref[...]
    logits = _dot(hn_hi, wrh) + _dot(hn_lo, wrh) + _dot(hn_hi, wrl_ref[...]) + br_ref[...]

    lane_t = lax.broadcasted_iota(jnp.int32, (tm, LANES), 1)
    neg = jnp.float32(-jnp.inf)
    gl = jnp.where(lane_t < N_GROUPS, logits, neg)
    gmax = jnp.max(gl, axis=-1, keepdims=True)
    gidx = jnp.min(jnp.where(gl == gmax, lane_t, LANES), axis=-1, keepdims=True)
    gsum = jnp.sum(jnp.exp(gl - gmax), axis=-1, keepdims=True)
    gweight = 1.0 / gsum
    lo_lane = ROUTER_LANE0 + EXPERTS_PER_GROUP * gidx
    el = jnp.where((lane_t >= lo_lane) & (lane_t < lo_lane + EXPERTS_PER_GROUP), logits, neg)
    m1 = jnp.max(el, axis=-1, keepdims=True)
    i1 = jnp.min(jnp.where(el == m1, lane_t, LANES), axis=-1, keepdims=True)
    el2 = jnp.where(lane_t == i1, neg, el)
    m2 = jnp.max(el2, axis=-1, keepdims=True)
    i2 = jnp.min(jnp.where(el2 == m2, lane_t, LANES), axis=-1, keepdims=True)
    t21 = jnp.exp(m2 - m1)
    w1 = gweight / (1.0 + t21)
    w2 = gweight * t21 / (1.0 + t21)

    sel1 = lane_t == i1
    sel2 = lane_t == i2
    onehot = jnp.where(sel1 | sel2, 1.0, 0.0)
    r_t = lax.broadcasted_iota(jnp.int32, (tm, tm), 0)
    c_t = lax.broadcasted_iota(jnp.int32, (tm, tm), 1)
    before = (r_t > c_t).astype(BF16)
    running = count_ref[0:1, :] + _dot(before, onehot.astype(BF16))
    rank1 = jnp.sum(jnp.where(sel1, running, 0.0), axis=-1, keepdims=True)
    rank2 = jnp.sum(jnp.where(sel2, running, 0.0), axis=-1, keepdims=True)
    new_count = count_ref[0:1, :] + jnp.sum(onehot, axis=0, keepdims=True)
    count_ref[...] = jnp.broadcast_to(new_count, count_ref.shape)
    cnt_ref[...] = jnp.broadcast_to(new_count, cnt_ref.shape)

    e1 = i1 - ROUTER_LANE0
    e2 = i2 - ROUTER_LANE0
    ri = jnp.where(lane_t == 0, e1.astype(F32), jnp.where(lane_t == 1, e2.astype(F32), jnp.where(
        lane_t == 2, rank1, jnp.where(lane_t == 3, rank2, 0.0))))
    ri_ref[...] = ri.T[0:8, :].astype(jnp.int32)
    rw_ref[...] = jnp.where(lane_t == 0, w1, jnp.where(lane_t == 1, w2, 0.0))


def _mix(sb, gu, vgn, x2, wsp, bsp_full, sb_g, sg_g, w_out_b, ffn_g, wr_hi, wr_lo, br):
    n = x2.shape[0]
    row = lambda i: (i, 0)
    const = lambda i: (0, 0)
    return pl.pallas_call(
        _mix_kernel,
        grid=(n // TM_MIX,),
        in_specs=[pl.BlockSpec((TM_MIX, SB_WIDTH), row),
                  pl.BlockSpec((TM_MIX, SG_WIDTH), row),
                  pl.BlockSpec((TM_MIX, SG_WIDTH), row),
                  pl.BlockSpec((TM_MIX, D_MODEL), row),
                  pl.BlockSpec((SG_HEADS, CHUNK, CHUNK), lambda i: (0, 0, 0)),
                  pl.BlockSpec((CHUNK, SG_WIDTH), const),
                  pl.BlockSpec((1, SB_WIDTH), const),
                  pl.BlockSpec((1, SG_WIDTH), const),
                  pl.BlockSpec((D_MODEL, D_MODEL), const),
                  pl.BlockSpec((1, D_MODEL), const),
                  pl.BlockSpec((D_MODEL, LANES), const),
                  pl.BlockSpec((D_MODEL, LANES), const),
                  pl.BlockSpec((1, LANES), const)],
        out_specs=[pl.BlockSpec((TM_MIX, D_MODEL), row),
                   pl.BlockSpec((TM_MIX * ROW_TILE, LANES), row),
                   pl.BlockSpec((8, TM_MIX), lambda i: (0, i)),
                   pl.BlockSpec((TM_MIX, LANES), row),
                   pl.BlockSpec((8, LANES), const)],
        out_shape=[jax.ShapeDtypeStruct((n, D_MODEL), F32),
                   jax.ShapeDtypeStruct((n * ROW_TILE, LANES), F32),
                   jax.ShapeDtypeStruct((8, n), jnp.int32),
                   jax.ShapeDtypeStruct((n, LANES), F32),
                   jax.ShapeDtypeStruct((8, LANES), F32)],
        scratch_shapes=[pltpu.VMEM((8, LANES), F32),
                        pltpu.VMEM((TM_MIX, SG_WIDTH), F32)],
        compiler_params=pltpu.CompilerParams(dimension_semantics=("arbitrary",),
                                             vmem_limit_bytes=VMEM_LIMIT),
        name="mix_router",
    )(sb, gu, vgn, x2, wsp, bsp_full, sb_g, sg_g, w_out_b, ffn_g, wr_hi, wr_lo, br)


_PAD_BITS = tuple(1 << b for b in reversed(range(TM_EXPERT.bit_length() - 1)))


def _dispatch_kernel(dest_ref, pad_start_ref, pad_count_ref, nt_ref, hn_ref, zeros_ref, xs_ref, sem, zsem):
    tm = TM_DISPATCH
    i = pl.program_id(0)
    n = pl.num_programs(0) * tm
    base = i * tm
    n_tiles_max = xs_ref.shape[0] // (TM_EXPERT * ROW_TILE)

    def pad_copies(do):
        for e in range(N_EXPERTS):
            start = pad_start_ref[e]
            count = pad_count_ref[e]
            for bit in _PAD_BITS:
                @pl.when((count & bit) != 0)
                def _(start=start, bit=bit):
                    do(pltpu.make_async_copy(_token_rows(zeros_ref, 0, bit),
                                             _token_rows(xs_ref, start, bit), zsem))
                start = start + (count & bit)
        for k in range(N_EXPERTS):
            tile = nt_ref[0] + k

            @pl.when(tile < n_tiles_max)
            def _(tile=tile):
                do(pltpu.make_async_copy(zeros_ref, _token_rows(xs_ref, tile * TM_EXPERT, TM_EXPERT), zsem))

    @pl.when(i == 0)
    def _():
        pad_copies(lambda cp: cp.start())

    def body(r, c):
        src = _token_rows(hn_ref, r, 1)
        for s in range(2):
            pltpu.make_async_copy(src, _token_rows(xs_ref, dest_ref[s * n + base + r], 1),
                                  sem).start(priority=s)
        return c

    lax.fori_loop(0, tm, body, 0, unroll=8)
    for _ in range(2):
        pltpu.make_async_copy(hn_ref, _token_rows(xs_ref, 0, tm), sem).wait()

    @pl.when(i == 0)
    def _():
        pad_copies(lambda cp: cp.wait())


def _dispatch(dest, pad_start, pad_count, n_tiles, hn_tiles, n_rows):
    n = hn_tiles.shape[0] // ROW_TILE
    zeros = jnp.zeros((TM_EXPERT * ROW_TILE, LANES), F32)
    return pl.pallas_call(
        _dispatch_kernel,
        grid_spec=pltpu.PrefetchScalarGridSpec(
            num_scalar_prefetch=4,
            grid=(n // TM_DISPATCH,),
            in_specs=[pl.BlockSpec((TM_DISPATCH * ROW_TILE, LANES), lambda i, *_: (i, 0)),
                      pl.BlockSpec(memory_space=pl.ANY)],
            out_specs=pl.BlockSpec(memory_space=pl.ANY),
            scratch_shapes=[pltpu.SemaphoreType.DMA, pltpu.SemaphoreType.DMA]),
        out_shape=jax.ShapeDtypeStruct((n_rows * ROW_TILE, LANES), F32),
        compiler_params=pltpu.CompilerParams(dimension_semantics=("arbitrary",),
                                             vmem_limit_bytes=VMEM_LIMIT),
        name="dispatch",
    )(dest, pad_start, pad_count, n_tiles, hn_tiles, zeros)


X_SLOTS = 3


def _expert_kernel(tiles_ref, nt_ref, xs_ref, wg_ref, wu_ref, wd_ref, y_ref,
                   x_buf, sg_buf, su_buf, sd_buf, wgb, wub, wdb, state, w_sems, x_sems):
    tm = TM_EXPERT
    t = pl.program_id(0)
    nt = nt_ref[0]

    def x_copy(tile):
        slot = lax.rem(tile, X_SLOTS)
        return pltpu.make_async_copy(_token_rows(xs_ref, tile * tm, tm), x_buf.at[slot], x_sems.at[slot])

    def weight_copies(e, slot):
        return (pltpu.make_async_copy(wg_ref.at[e], sg_buf.at[slot], w_sems.at[slot]),
                pltpu.make_async_copy(wu_ref.at[e], su_buf.at[slot], w_sems.at[slot]),
                pltpu.make_async_copy(wd_ref.at[e], sd_buf.at[slot], w_sems.at[slot]))

    def next_with_rows(e):
        return lax.while_loop(lambda k: (k < N_EXPERTS) & (tiles_ref[jnp.minimum(k, N_EXPERTS - 1)] == 0),
                              lambda k: k + 1, e + 1)

    @pl.when(t == 0)
    def _():
        first = next_with_rows(jnp.int32(-1))
        state[0] = jnp.int32(-1)
        state[1] = jnp.int32(0)
        state[2] = jnp.int32(1)
        state[3] = first
        for cp in weight_copies(first, 0):
            cp.start()
        x_copy(0).start()

        @pl.when(nt > 1)
        def _():
            x_copy(1).start()

    @pl.when(t + 2 < nt)
    def _():
        x_copy(t + 2).start()

    @pl.when(t < nt)
    def _():
        @pl.when(state[1] == 0)
        def _():
            e = state[3]
            slot = 1 - state[2]
            nxt = next_with_rows(e)
            state[0] = e
            state[1] = tiles_ref[e]
            state[2] = slot
            state[3] = nxt
            for cp in weight_copies(e, slot):
                cp.wait()

            @pl.when(nxt < N_EXPERTS)
            def _():
                for cp in weight_copies(nxt, 1 - slot):
                    cp.start()

            wgb[...] = sg_buf[slot].astype(BF16)
            wub[...] = su_buf[slot].astype(BF16)
            wdb[...] = sd_buf[slot].astype(BF16)

        state[1] = state[1] - 1
        x_copy(t).wait()
        x = _tiles_to_rows(x_buf.at[lax.rem(t, X_SLOTS)], tm).astype(BF16)
        g = _dot(x, wgb[...])
        u = _dot(x, wub[...])
        hidden = (g * jax.nn.sigmoid(g)) * u
        _rows_to_tiles(y_ref, _dot(hidden.astype(BF16), wdb[...]))

    @pl.when(t >= nt)
    def _():
        y_ref[...] = jnp.zeros_like(y_ref)


def _experts(tiles, n_tiles, xs, wg, wu, wd):
    n_rows = xs.shape[0] // ROW_TILE
    any_spec = pl.BlockSpec(memory_space=pl.ANY)
    return pl.pallas_call(
        _expert_kernel,
        grid_spec=pltpu.PrefetchScalarGridSpec(
            num_scalar_prefetch=2,
            grid=(n_rows // TM_EXPERT,),
            in_specs=[any_spec, any_spec, any_spec, any_spec],
            out_specs=pl.BlockSpec((TM_EXPERT * ROW_TILE, LANES), lambda t, *_: (t, 0)),
            scratch_shapes=[pltpu.VMEM((X_SLOTS, TM_EXPERT * ROW_TILE, LANES), F32),
                            pltpu.VMEM((2, D_MODEL, D_EXPERT), F32),
                            pltpu.VMEM((2, D_MODEL, D_EXPERT), F32),
                            pltpu.VMEM((2, D_EXPERT, D_MODEL), F32),
                            pltpu.VMEM((D_MODEL, D_EXPERT), BF16),
                            pltpu.VMEM((D_MODEL, D_EXPERT), BF16),
                            pltpu.VMEM((D_EXPERT, D_MODEL), BF16),
                            pltpu.SMEM((4,), jnp.int32),
                            pltpu.SemaphoreType.DMA((2,)),
                            pltpu.SemaphoreType.DMA((X_SLOTS,))]),
        out_shape=jax.ShapeDtypeStruct((n_rows * ROW_TILE, LANES), F32),
        compiler_params=pltpu.CompilerParams(dimension_semantics=("arbitrary",),
                                             vmem_limit_bytes=VMEM_LIMIT),
        name="expert_mlp",
    )(tiles, n_tiles, xs, wg, wu, wd)


def _combine_kernel(dest_ref, h_ref, rw_ref, fg_ref, y_ref, o_ref, buf, sems):
    tm = TM_COMBINE
    i = pl.program_id(0)
    n_steps = pl.num_programs(0)
    n = n_steps * tm
    cur = i % 2

    def fetch(step, half):
        def body(r, c):
            for s in range(2):
                pltpu.make_async_copy(_token_rows(y_ref, dest_ref[s * n + step * tm + r], 1),
                                      _token_rows(buf.at[half, s], r, 1),
                                      sems.at[half]).start(priority=s)
            return c

        lax.fori_loop(0, tm, body, 0, unroll=8)

    @pl.when(i == 0)
    def _():
        fetch(0, 0)

    @pl.when(i + 1 < n_steps)
    def _():
        fetch(i + 1, 1 - cur)

    for s in range(2):
        pltpu.make_async_copy(_token_rows(y_ref, 0, tm), buf.at[cur, s], sems.at[cur]).wait()
    rw = rw_ref[...]
    out = (h_ref[...] + rw[:, 0:1] * _tiles_to_rows(buf.at[cur, 0], tm)
           + rw[:, 1:2] * _tiles_to_rows(buf.at[cur, 1], tm))
    o_ref[...] = _rms(out, fg_ref[...])


def _combine(dest, h, rw, final_g, ys):
    n = h.shape[0]
    return pl.pallas_call(
        _combine_kernel,
        grid_spec=pltpu.PrefetchScalarGridSpec(
            num_scalar_prefetch=1,
            grid=(n // TM_COMBINE,),
            in_specs=[pl.BlockSpec((TM_COMBINE, D_MODEL), lambda i, d: (i, 0)),
                      pl.BlockSpec((TM_COMBINE, LANES), lambda i, d: (i, 0)),
                      pl.BlockSpec((1, D_MODEL), lambda i, d: (0, 0)),
                      pl.BlockSpec(memory_space=pl.ANY)],
            out_specs=pl.BlockSpec((TM_COMBINE, D_MODEL), lambda i, d: (i, 0)),
            scratch_shapes=[pltpu.VMEM((2, 2, TM_COMBINE * ROW_TILE, LANES), F32),
                            pltpu.SemaphoreType.DMA((2,))]),
        out_shape=jax.ShapeDtypeStruct((n, D_MODEL), F32),
        compiler_params=pltpu.CompilerParams(dimension_semantics=("arbitrary",),
                                             vmem_limit_bytes=VMEM_LIMIT),
        name="combine",
    )(dest, h, rw, final_g, ys)


def _schedule(counts):
    tiles = (counts + TM_EXPERT - 1) // TM_EXPERT
    tile_end = jnp.cumsum(tiles)
    offsets = (tile_end - tiles) * TM_EXPERT
    return tiles, offsets, tile_end[-1:]


def _layer(x, attn_g, w_in, sg_g, w_sp, b_sp, sb_g, sg_out_g, w_out, ffn_g,
           w_rg, b_rg, w_re, b_re, w_gate, w_up, w_down):
    batch, seq, _ = x.shape
    n = batch * seq
    x2 = x.reshape(n, D_MODEL)
    row = lambda v: v.reshape(1, -1)

    qkv, gu, vgn = _inproj(x2, row(attn_g), w_in.astype(BF16), row(sg_g))
    sb = _attention(qkv, batch, seq).reshape(n, SB_WIDTH)

    w_r = jnp.concatenate([w_rg, jnp.transpose(w_re, (1, 0, 2)).reshape(D_MODEL, N_EXPERTS)], axis=1)
    w_r = jnp.pad(w_r, ((0, 0), (0, LANES - w_r.shape[1])))
    wr_hi = w_r.astype(BF16)
    wr_lo = (w_r - wr_hi.astype(F32)).astype(BF16)
    b_r = jnp.pad(jnp.concatenate([b_rg, b_re.reshape(-1)]), (0, LANES - N_GROUPS - N_EXPERTS))
    bsp_full = jnp.repeat(b_sp.T, HEAD_DIM, axis=1)

    h, hn, ri, rw, cnt = _mix(sb, gu, vgn, x2, w_sp, bsp_full, row(sb_g), row(sg_out_g),
                              w_out.astype(BF16), row(ffn_g), wr_hi, wr_lo, row(b_r))

    counts = cnt[0, ROUTER_LANE0:ROUTER_LANE0 + N_EXPERTS].astype(jnp.int32)
    n_rows = 2 * n + N_EXPERTS * TM_EXPERT
    tiles, offsets, n_tiles = _schedule(counts)
    expert, rank = ri[0:2], ri[2:4]
    is_e = expert[None] == jnp.arange(N_EXPERTS, dtype=jnp.int32)[:, None, None]
    dest = (jnp.sum(jnp.where(is_e, offsets[:, None, None], 0), axis=0) + rank).reshape(-1)
    pad_start = offsets + counts
    pad_count = (-counts) % TM_EXPERT

    xs = _dispatch(dest, pad_start, pad_count, n_tiles, hn, n_rows)
    ys = _experts(tiles, n_tiles, xs,
                  w_gate.reshape(N_EXPERTS, D_MODEL, D_EXPERT),
                  w_up.reshape(N_EXPERTS, D_MODEL, D_EXPERT),
                  w_down.reshape(N_EXPERTS, D_EXPERT, D_MODEL))
    return dest, h, rw, ys


def kernel(x, attn_norm_g, w_in, sg_norm_g, w_spatial, b_spatial, sb_out_norm_g, sg_out_norm_g,
           w_out, ffn_norm_g, w_router_group, b_router_group, w_router_expert, b_router_expert,
           w_gate, w_up, w_down, final_norm_g):
    assert attn_norm_g.shape[0] == 1, "single-layer problem"
    batch, seq, _ = x.shape
    dest, h, rw, ys = _layer(x, attn_norm_g[0], w_in[0], sg_norm_g[0], w_spatial[0], b_spatial[0],
                             sb_out_norm_g[0], sg_out_norm_g[0], w_out[0], ffn_norm_g[0],
                             w_router_group[0], b_router_group[0], w_router_expert[0],
                             b_router_expert[0], w_gate[0], w_up[0], w_down[0])
    out = _combine(dest, h, rw, final_norm_g.reshape(1, -1), ys)
    return out.reshape(batch, seq, D_MODEL)
```

```python
import functools
import math

import jax
import jax.numpy as jnp
from jax import lax
from jax.experimental import pallas as pl
from jax.experimental.pallas import tpu as pltpu

D_MODEL = 1024
HEAD_DIM = 64
SB_WIDTH = 512
SG_WIDTH = 512
SG_HEADS = 8
D_IN = 3 * SB_WIDTH + 2 * SG_WIDTH
CHUNK = 128
N_GROUPS = 4
EXPERTS_PER_GROUP = 8
N_EXPERTS = N_GROUPS * EXPERTS_PER_GROUP
D_EXPERT = 512
EPS = 1e-6
F32_EXP_UNDERFLOW = 110.0

LANES = 128
ROW_TILE = D_MODEL // LANES
assert ROW_TILE == 8
HEAD_PAIR = 2 * HEAD_DIM
ROUTER_LANE0 = N_GROUPS

TM_PROJ = 512
TQ_ATTN = 256
TM_MIX = 512
TM_DISPATCH = 512
TM_EXPERT = 256
TM_COMBINE = 256
VMEM_LIMIT = 48 * 1024 * 1024

F32 = jnp.float32
BF16 = jnp.bfloat16


def _rms(x, g):
    return x * lax.rsqrt(jnp.mean(x * x, axis=-1, keepdims=True) + EPS) * g


def _gelu(x):
    c = math.sqrt(2.0 / math.pi)
    return x * (0.5 * (1.0 + jnp.tanh(c * (x + 0.044715 * (x * x * x)))))


def _softplus(z):
    return jnp.maximum(z, 0.0) + jnp.log(1.0 + jnp.exp(-jnp.abs(z)))


def _dot(a, b):
    return jnp.dot(a, b, preferred_element_type=F32)


def _rows_to_tiles(ref, x):
    m = x.shape[0]
    for k in range(ROW_TILE):
        ref[pl.ds(k, m, stride=ROW_TILE), :] = x[:, k * LANES:(k + 1) * LANES]


def _tiles_to_rows(ref, m):
    return jnp.concatenate([ref[pl.ds(k, m, stride=ROW_TILE), :] for k in range(ROW_TILE)], axis=1)


def _token_rows(ref, first_token, n_tokens):
    return ref.at[pl.ds(pl.multiple_of(first_token * ROW_TILE, ROW_TILE), n_tokens * ROW_TILE)]


def _split_bf16(x):
    hi = x.astype(BF16)
    lo = (x - hi.astype(F32)).astype(BF16)
    return hi, lo


def _inproj_kernel(x_ref, g_ref, w_ref, sgg_ref, qkv_ref, gu_ref, vgn_ref):
    hb = _rms(x_ref[...], g_ref[...]).astype(BF16)
    q = _dot(hb, w_ref[:, 0:SB_WIDTH]) * (1.0 / math.sqrt(HEAD_DIM))
    qkv_ref[:, 0:SB_WIDTH] = q.astype(BF16)
    qkv_ref[:, SB_WIDTH:3 * SB_WIDTH] = _dot(hb, w_ref[:, SB_WIDTH:3 * SB_WIDTH]).astype(BF16)
    gu_ref[...] = _gelu(_dot(hb, w_ref[:, 3 * SB_WIDTH:3 * SB_WIDTH + SG_WIDTH]))
    gv = _gelu(_dot(hb, w_ref[:, 3 * SB_WIDTH + SG_WIDTH:D_IN]))
    vgn_ref[...] = _rms(gv, sgg_ref[...]).astype(BF16)


def _inproj(x2, attn_g, w_in_b, sg_g):
    n = x2.shape[0]
    row = lambda i: (i, 0)
    const = lambda i: (0, 0)
    return pl.pallas_call(
        _inproj_kernel,
        grid=(n // TM_PROJ,),
        in_specs=[pl.BlockSpec((TM_PROJ, D_MODEL), row),
                  pl.BlockSpec((1, D_MODEL), const),
                  pl.BlockSpec((D_MODEL, D_IN), const),
                  pl.BlockSpec((1, SG_WIDTH), const)],
        out_specs=[pl.BlockSpec((TM_PROJ, 3 * SB_WIDTH), row),
                   pl.BlockSpec((TM_PROJ, SG_WIDTH), row),
                   pl.BlockSpec((TM_PROJ, SG_WIDTH), row)],
        out_shape=[jax.ShapeDtypeStruct((n, 3 * SB_WIDTH), BF16),
                   jax.ShapeDtypeStruct((n, SG_WIDTH), F32),
                   jax.ShapeDtypeStruct((n, SG_WIDTH), BF16)],
        compiler_params=pltpu.CompilerParams(dimension_semantics=("arbitrary",),
                                             vmem_limit_bytes=VMEM_LIMIT),
        name="inproj",
    )(x2, attn_g, w_in_b, sg_g)


def _attn_kernel(q_ref, k_ref, v_ref, o_ref, q2_ref, carry_ref):
    t = TQ_ATTN
    n_pairs = SB_WIDTH // HEAD_PAIR
    qi = pl.program_id(1)
    lane = lax.broadcasted_iota(jnp.int32, (1, HEAD_PAIR), 1)
    head_lanes = (lane < HEAD_DIM, lane >= HEAD_DIM)
    zero = jnp.zeros((), BF16)
    for p in range(n_pairs):
        qp = q_ref[0, :, p * HEAD_PAIR:(p + 1) * HEAD_PAIR]
        for h in range(2):
            q2_ref[(2 * p + h) * t:(2 * p + h + 1) * t, :] = jnp.where(head_lanes[h], qp, zero)
    r_idx = lax.broadcasted_iota(jnp.int32, (t, t), 0)
    c_idx = lax.broadcasted_iota(jnp.int32, (t, t), 1)
    suffix = (r_idx > c_idx).astype(BF16)
    suffix2 = jnp.concatenate([suffix, suffix], axis=0)
    causal = c_idx < r_idx

    o_ref[...] = jnp.zeros_like(o_ref)
    carry_ref[...] = jnp.zeros_like(carry_ref)

    def block(j, diag):
        start = pl.multiple_of(j * t, t)
        for p in range(n_pairs):
            cols = slice(p * HEAD_PAIR, (p + 1) * HEAD_PAIR)
            rows = slice(2 * p * t, (2 * p + 2) * t)
            kb = k_ref[0, pl.ds(start, t), cols]
            vb = v_ref[0, pl.ds(start, t), cols]
            z = lax.dot_general(q2_ref[rows, :], kb, (((1,), (1,)), ((), ())),
                                preferred_element_type=F32)
            sp = _softplus(z)
            if diag:
                mask2 = jnp.concatenate([causal, causal], axis=0)
                nl = jnp.where(mask2, sp, 0.0)
            else:
                nl = sp
            hi, lo = _split_bf16(nl)
            hl = jnp.concatenate([hi, lo], axis=1)
            after = jnp.concatenate([_dot(hl[0:t], suffix2), _dot(hl[t:2 * t], suffix2)], axis=0)
            carry = carry_ref[rows, :]
            a = jnp.exp(z - sp - after - carry)
            if diag:
                a = jnp.where(mask2, a, 0.0)
            a = a.astype(BF16)
            a2 = jnp.concatenate([a[0:t], a[t:2 * t]], axis=1)
            v2 = jnp.concatenate([jnp.where(head_lanes[0], vb, zero),
                                  jnp.where(head_lanes[1], vb, zero)], axis=0)
            o_ref[0, :, cols] += _dot(a2, v2)
            carry_ref[rows, :] = carry + after[:, 0:1] + nl[:, 0:1]

    def live():
        return jnp.min(carry_ref[...]) < F32_EXP_UNDERFLOW

    block(qi, True)

    def body(state):
        it, _ = state
        block(qi - 1 - it, False)
        return it + 1, live()

    lax.while_loop(lambda s: (s[0] < qi) & s[1], body, (jnp.int32(0), live()))


def _attention(qkv, batch, seq):
    qkv3 = qkv.reshape(batch, seq, 3 * SB_WIDTH)
    n_heads = SB_WIDTH // HEAD_DIM
    return pl.pallas_call(
        _attn_kernel,
        grid=(batch, seq // TQ_ATTN),
        in_specs=[pl.BlockSpec((1, TQ_ATTN, SB_WIDTH), lambda b, i: (b, i, 0)),
                  pl.BlockSpec((1, seq, SB_WIDTH), lambda b, i: (b, 0, 1)),
                  pl.BlockSpec((1, seq, SB_WIDTH), lambda b, i: (b, 0, 2))],
        out_specs=pl.BlockSpec((1, TQ_ATTN, SB_WIDTH), lambda b, i: (b, i, 0)),
        out_shape=jax.ShapeDtypeStruct((batch, seq, SB_WIDTH), F32),
        scratch_shapes=[pltpu.VMEM((n_heads * TQ_ATTN, HEAD_PAIR), BF16),
                        pltpu.VMEM((n_heads * TQ_ATTN, 1), F32)],
        compiler_params=pltpu.CompilerParams(dimension_semantics=("arbitrary",) * 2,
                                             vmem_limit_bytes=VMEM_LIMIT),
        name="sb_attention",
    )(qkv3, qkv3, qkv3)


def _mix_kernel(sb_ref, gu_ref, vgn_ref, x_ref, wsp_ref, bsp_ref, sbg_ref, sgg_ref, wout_ref,
                ffng_ref, wrh_ref, wrl_ref, br_ref,
                h_ref, hn_ref, ri_ref, rw_ref, cnt_ref, count_ref, sg_ref):
    tm = TM_MIX
    i = pl.program_id(0)

    @pl.when(i == 0)
    def _():
        count_ref[...] = jnp.zeros_like(count_ref)

    lane = lax.broadcasted_iota(jnp.int32, (1, LANES), 1)
    first = lane < HEAD_DIM
    zero = jnp.zeros((), BF16)
    r_c = lax.broadcasted_iota(jnp.int32, (CHUNK, CHUNK), 0)
    c_c = lax.broadcasted_iota(jnp.int32, (CHUNK, CHUNK), 1)
    tril = r_c >= c_c
    n_pairs = SG_WIDTH // HEAD_PAIR
    w_pairs = []
    for p in range(n_pairs):
        w0 = jnp.where(tril, wsp_ref[2 * p], 0.0).astype(BF16)
        w1 = jnp.where(tril, wsp_ref[2 * p + 1], 0.0).astype(BF16)
        w_pairs.append(jnp.concatenate([w0, w1], axis=1))
    bsp = bsp_ref[...]
    for c in range(tm // CHUNK):
        rows = slice(c * CHUNK, (c + 1) * CHUNK)
        for p in range(n_pairs):
            cols = slice(p * HEAD_PAIR, (p + 1) * HEAD_PAIR)
            vg = vgn_ref[rows, cols]
            rhs = jnp.concatenate([jnp.where(first, vg, zero), jnp.where(first, zero, vg)], axis=0)
            mixed = _dot(w_pairs[p], rhs) + bsp[:, cols]
            sg_ref[rows, cols] = gu_ref[rows, cols] * mixed
    sgn = _rms(sg_ref[...], sgg_ref[...]).astype(BF16)
    sbn = _rms(sb_ref[...], sbg_ref[...]).astype(BF16)
    h = x_ref[...] + _dot(sbn, wout_ref[0:SB_WIDTH, :]) + _dot(sgn, wout_ref[SB_WIDTH:, :])
    h_ref[...] = h
    hn = _rms(h, ffng_ref[...])
    _rows_to_tiles(hn_ref, hn)

    hn_hi, hn_lo = _split_bf16(hn)
    wrh = wrh_ref[...]
    logits = _dot(hn_hi, wrh) + _dot(hn_lo, wrh) + _dot(hn_hi, wrl_ref[...]) + br_ref[...]

    lane_t = lax.broadcasted_iota(jnp.int32, (tm, LANES), 1)
    neg = jnp.float32(-jnp.inf)
    gl = jnp.where(lane_t < N_GROUPS, logits, neg)
    gmax = jnp.max(gl, axis=-1, keepdims=True)
    gidx = jnp.min(jnp.where(gl == gmax, lane_t, LANES), axis=-1, keepdims=True)
    gsum = jnp.sum(jnp.exp(gl - gmax), axis=-1, keepdims=True)
    gweight = 1.0 / gsum
    lo_lane = ROUTER_LANE0 + EXPERTS_PER_GROUP * gidx
    el = jnp.where((lane_t >= lo_lane) & (lane_t < lo_lane + EXPERTS_PER_GROUP), logits, neg)
    m1 = jnp.max(el, axis=-1, keepdims=True)
    i1 = jnp.min(jnp.where(el == m1, lane_t, LANES), axis=-1, keepdims=True)
    el2 = jnp.where(lane_t == i1, neg, el)
    m2 = jnp.max(el2, axis=-1, keepdims=True)
    i2 = jnp.min(jnp.where(el2 == m2, lane_t, LANES), axis=-1, keepdims=True)
    t21 = jnp.exp(m2 - m1)
    w1 = gweight / (1.0 + t21)
    w2 = gweight * t21 / (1.0 + t21)

    sel1 = lane_t == i1
    sel2 = lane_t == i2
    onehot = jnp.where(sel1 | sel2, 1.0, 0.0)
    r_t = lax.broadcasted_iota(jnp.int32, (tm, tm), 0)
    c_t = lax.broadcasted_iota(jnp.int32, (tm, tm), 1)
    before = (r_t > c_t).astype(BF16)
    running = count_ref[0:1, :] + _dot(before, onehot.astype(BF16))
    rank1 = jnp.sum(jnp.where(sel1, running, 0.0), axis=-1, keepdims=True)
    rank2 = jnp.sum(jnp.where(sel2, running, 0.0), axis=-1, keepdims=True)
    new_count = count_ref[0:1, :] + jnp.sum(onehot, axis=0, keepdims=True)
    count_ref[...] = jnp.broadcast_to(new_count, count_ref.shape)
    cnt_ref[...] = jnp.broadcast_to(new_count, cnt_ref.shape)

    e1 = i1 - ROUTER_LANE0
    e2 = i2 - ROUTER_LANE0
    ri = jnp.where(lane_t == 0, e1.astype(F32), jnp.where(lane_t == 1, e2.astype(F32), jnp.where(
        lane_t == 2, rank1, jnp.where(lane_t == 3, rank2, 0.0))))
    ri_ref[...] = ri.T[0:8, :].astype(jnp.int32)
    rw_ref[...] = jnp.where(lane_t == 0, w1, jnp.where(lane_t == 1, w2, 0.0))


def _mix(sb, gu, vgn, x2, wsp, bsp_full, sb_g, sg_g, w_out_b, ffn_g, wr_hi, wr_lo, br):
    n = x2.shape[0]
    row = lambda i: (i, 0)
    const = lambda i: (0, 0)
    return pl.pallas_call(
        _mix_kernel,
        grid=(n // TM_MIX,),
        in_specs=[pl.BlockSpec((TM_MIX, SB_WIDTH), row),
                  pl.BlockSpec((TM_MIX, SG_WIDTH), row),
                  pl.BlockSpec((TM_MIX, SG_WIDTH), row),
                  pl.BlockSpec((TM_MIX, D_MODEL), row),
                  pl.BlockSpec((SG_HEADS, CHUNK, CHUNK), lambda i: (0, 0, 0)),
                  pl.BlockSpec((CHUNK, SG_WIDTH), const),
                  pl.BlockSpec((1, SB_WIDTH), const),
                  pl.BlockSpec((1, SG_WIDTH), const),
                  pl.BlockSpec((D_MODEL, D_MODEL), const),
                  pl.BlockSpec((1, D_MODEL), const),
                  pl.BlockSpec((D_MODEL, LANES), const),
                  pl.BlockSpec((D_MODEL, LANES), const),
                  pl.BlockSpec((1, LANES), const)],
        out_specs=[pl.BlockSpec((TM_MIX, D_MODEL), row),
                   pl.BlockSpec((TM_MIX * ROW_TILE, LANES), row),
                   pl.BlockSpec((8, TM_MIX), lambda i: (0, i)),
                   pl.BlockSpec((TM_MIX, LANES), row),
                   pl.BlockSpec((8, LANES), const)],
        out_shape=[jax.ShapeDtypeStruct((n, D_MODEL), F32),
                   jax.ShapeDtypeStruct((n * ROW_TILE, LANES), F32),
                   jax.ShapeDtypeStruct((8, n), jnp.int32),
                   jax.ShapeDtypeStruct((n, LANES), F32),
                   jax.ShapeDtypeStruct((8, LANES), F32)],
        scratch_shapes=[pltpu.VMEM((8, LANES), F32),
                        pltpu.VMEM((TM_MIX, SG_WIDTH), F32)],
        compiler_params=pltpu.CompilerParams(dimension_semantics=("arbitrary",),
                                             vmem_limit_bytes=VMEM_LIMIT),
        name="mix_router",
    )(sb, gu, vgn, x2, wsp, bsp_full, sb_g, sg_g, w_out_b, ffn_g, wr_hi, wr_lo, br)


_PAD_BITS = tuple(1 << b for b in reversed(range(TM_EXPERT.bit_length() - 1)))


def _dispatch_kernel(dest_ref, pad_start_ref, pad_count_ref, nt_ref, hn_ref, zeros_ref, xs_ref, sem, zsem):
    tm = TM_DISPATCH
    i = pl.program_id(0)
    n = pl.num_programs(0) * tm
    base = i * tm
    n_tiles_max = xs_ref.shape[0] // (TM_EXPERT * ROW_TILE)

    def pad_copies(do):
        for e in range(N_EXPERTS):
            start = pad_start_ref[e]
            count = pad_count_ref[e]
            for bit in _PAD_BITS:
                @pl.when((count & bit) != 0)
                def _(start=start, bit=bit):
                    do(pltpu.make_async_copy(_token_rows(zeros_ref, 0, bit),
                                             _token_rows(xs_ref, start, bit), zsem))
                start = start + (count & bit)
        for k in range(N_EXPERTS):
            tile = nt_ref[0] + k

            @pl.when(tile < n_tiles_max)
            def _(tile=tile):
                do(pltpu.make_async_copy(zeros_ref, _token_rows(xs_ref, tile * TM_EXPERT, TM_EXPERT), zsem))

    @pl.when(i == 0)
    def _():
        pad_copies(lambda cp: cp.start())

    def body(r, c):
        src = _token_rows(hn_ref, r, 1)
        for s in range(2):
            pltpu.make_async_copy(src, _token_rows(xs_ref, dest_ref[s * n + base + r], 1),
                                  sem).start(priority=s)
        return c

    lax.fori_loop(0, tm, body, 0, unroll=8)
    for _ in range(2):
        pltpu.make_async_copy(hn_ref, _token_rows(xs_ref, 0, tm), sem).wait()

    @pl.when(i == 0)
    def _():
        pad_copies(lambda cp: cp.wait())


def _dispatch(dest, pad_start, pad_count, n_tiles, hn_tiles, n_rows):
    n = hn_tiles.shape[0] // ROW_TILE
    zeros = jnp.zeros((TM_EXPERT * ROW_TILE, LANES), F32)
    return pl.pallas_call(
        _dispatch_kernel,
        grid_spec=pltpu.PrefetchScalarGridSpec(
            num_scalar_prefetch=4,
            grid=(n // TM_DISPATCH,),
            in_specs=[pl.BlockSpec((TM_DISPATCH * ROW_TILE, LANES), lambda i, *_: (i, 0)),
                      pl.BlockSpec(memory_space=pl.ANY)],
            out_specs=pl.BlockSpec(memory_space=pl.ANY),
            scratch_shapes=[pltpu.SemaphoreType.DMA, pltpu.SemaphoreType.DMA]),
        out_shape=jax.ShapeDtypeStruct((n_rows * ROW_TILE, LANES), F32),
        compiler_params=pltpu.CompilerParams(dimension_semantics=("arbitrary",),
                                             vmem_limit_bytes=VMEM_LIMIT),
        name="dispatch",
    )(dest, pad_start, pad_count, n_tiles, hn_tiles, zeros)


X_SLOTS = 3


def _expert_kernel(tiles_ref, nt_ref, xs_ref, wg_ref, wu_ref, wd_ref, y_ref,
                   x_buf, sg_buf, su_buf, sd_buf, wgb, wub, wdb, state, w_sems, x_sems):
    tm = TM_EXPERT
    t = pl.program_id(0)
    nt = nt_ref[0]

    def x_copy(tile):
        slot = lax.rem(tile, X_SLOTS)
        return pltpu.make_async_copy(_token_rows(xs_ref, tile * tm, tm), x_buf.at[slot], x_sems.at[slot])

    def weight_copies(e, slot):
        return (pltpu.make_async_copy(wg_ref.at[e], sg_buf.at[slot], w_sems.at[slot]),
                pltpu.make_async_copy(wu_ref.at[e], su_buf.at[slot], w_sems.at[slot]),
                pltpu.make_async_copy(wd_ref.at[e], sd_buf.at[slot], w_sems.at[slot]))

    def next_with_rows(e):
        return lax.while_loop(lambda k: (k < N_EXPERTS) & (tiles_ref[jnp.minimum(k, N_EXPERTS - 1)] == 0),
                              lambda k: k + 1, e + 1)

    @pl.when(t == 0)
    def _():
        first = next_with_rows(jnp.int32(-1))
        state[0] = jnp.int32(-1)
        state[1] = jnp.int32(0)
        state[2] = jnp.int32(1)
        state[3] = first
        for cp in weight_copies(first, 0):
            cp.start()
        x_copy(0).start()

        @pl.when(nt > 1)
        def _():
            x_copy(1).start()

    @pl.when(t + 2 < nt)
    def _():
        x_copy(t + 2).start()

    @pl.when(t < nt)
    def _():
        @pl.when(state[1] == 0)
        def _():
            e = state[3]
            slot = 1 - state[2]
            nxt = next_with_rows(e)
            state[0] = e
            state[1] = tiles_ref[e]
            state[2] = slot
            state[3] = nxt
            for cp in weight_copies(e, slot):
                cp.wait()

            @pl.when(nxt < N_EXPERTS)
            def _():
                for cp in weight_copies(nxt, 1 - slot):
                    cp.start()

            wgb[...] = sg_buf[slot].astype(BF16)
            wub[...] = su_buf[slot].astype(BF16)
            wdb[...] = sd_buf[slot].astype(BF16)

        state[1] = state[1] - 1
        x_copy(t).wait()
        x = _tiles_to_rows(x_buf.at[lax.rem(t, X_SLOTS)], tm).astype(BF16)
        g = _dot(x, wgb[...])
        u = _dot(x, wub[...])
        hidden = (g * jax.nn.sigmoid(g)) * u
        _rows_to_tiles(y_ref, _dot(hidden.astype(BF16), wdb[...]))

    @pl.when(t >= nt)
    def _():
        y_ref[...] = jnp.zeros_like(y_ref)


def _experts(tiles, n_tiles, xs, wg, wu, wd):
    n_rows = xs.shape[0] // ROW_TILE
    any_spec = pl.BlockSpec(memory_space=pl.ANY)
    return pl.pallas_call(
        _expert_kernel,
        grid_spec=pltpu.PrefetchScalarGridSpec(
            num_scalar_prefetch=2,
            grid=(n_rows // TM_EXPERT,),
            in_specs=[any_spec, any_spec, any_spec, any_spec],
            out_specs=pl.BlockSpec((TM_EXPERT * ROW_TILE, LANES), lambda t, *_: (t, 0)),
            scratch_shapes=[pltpu.VMEM((X_SLOTS, TM_EXPERT * ROW_TILE, LANES), F32),
                            pltpu.VMEM((2, D_MODEL, D_EXPERT), F32),
                            pltpu.VMEM((2, D_MODEL, D_EXPERT), F32),
                            pltpu.VMEM((2, D_EXPERT, D_MODEL), F32),
                            pltpu.VMEM((D_MODEL, D_EXPERT), BF16),
                            pltpu.VMEM((D_MODEL, D_EXPERT), BF16),
                            pltpu.VMEM((D_EXPERT, D_MODEL), BF16),
                            pltpu.SMEM((4,), jnp.int32),
                            pltpu.SemaphoreType.DMA((2,)),
                            pltpu.SemaphoreType.DMA((X_SLOTS,))]),
        out_shape=jax.ShapeDtypeStruct((n_rows * ROW_TILE, LANES), F32),
        compiler_params=pltpu.CompilerParams(dimension_semantics=("arbitrary",),
                                             vmem_limit_bytes=VMEM_LIMIT),
        name="expert_mlp",
    )(tiles, n_tiles, xs, wg, wu, wd)


def _combine_kernel(dest_ref, h_ref, rw_ref, fg_ref, y_ref, o_ref, buf_a, buf_b, sems):
    tm = TM_COMBINE
    i = pl.program_id(0)
    n_tiles = 2 * pl.num_programs(0)
    n = n_tiles * tm

    def row_copy(tile, r, s, buf, sem):
        return pltpu.make_async_copy(_token_rows(y_ref, dest_ref[s * n + tile * tm + r], 1),
                                     _token_rows(buf.at[s], r, 1), sem)

    def fetch(tile, buf, sem):
        for r in range(tm):
            for s in range(2):
                row_copy(tile, r, s, buf, sem).start(priority=s)

    def wait(buf, sem):
        for s in range(2):
            pltpu.make_async_copy(_token_rows(y_ref, 0, tm), buf.at[s], sem).wait()

    def combine(half, buf):
        rows = slice(half * tm, (half + 1) * tm)
        rw = rw_ref[rows, :]
        out = (h_ref[rows, :] + rw[:, 0:1] * _tiles_to_rows(buf.at[0], tm)
               + rw[:, 1:2] * _tiles_to_rows(buf.at[1], tm))
        o_ref[rows, :] = _rms(out, fg_ref[...])

    @pl.when(i == 0)
    def _():
        def body(r, c):
            for s in range(2):
                row_copy(0, r, s, buf_a, sems.at[0]).start(priority=s)
            return c

        lax.fori_loop(0, tm, body, 0, unroll=8)

    wait(buf_a, sems.at[0])
    fetch(2 * i + 1, buf_b, sems.at[1])
    combine(0, buf_a)
    wait(buf_b, sems.at[1])
    fetch(jnp.minimum(2 * i + 2, n_tiles - 1), buf_a, sems.at[0])
    combine(1, buf_b)

    @pl.when(i == pl.num_programs(0) - 1)
    def _():
        wait(buf_a, sems.at[0])


def _combine(dest, h, rw, final_g, ys):
    n = h.shape[0]
    tm2 = 2 * TM_COMBINE
    return pl.pallas_call(
        _combine_kernel,
        grid_spec=pltpu.PrefetchScalarGridSpec(
            num_scalar_prefetch=1,
            grid=(n // tm2,),
            in_specs=[pl.BlockSpec((tm2, D_MODEL), lambda i, d: (i, 0)),
                      pl.BlockSpec((tm2, LANES), lambda i, d: (i, 0)),
                      pl.BlockSpec((1, D_MODEL), lambda i, d: (0, 0)),
                      pl.BlockSpec(memory_space=pl.ANY)],
            out_specs=pl.BlockSpec((tm2, D_MODEL), lambda i, d: (i, 0)),
            scratch_shapes=[pltpu.VMEM((2, TM_COMBINE * ROW_TILE, LANES), F32),
                            pltpu.VMEM((2, TM_COMBINE * ROW_TILE, LANES), F32),
                            pltpu.SemaphoreType.DMA((2,))]),
        out_shape=jax.ShapeDtypeStruct((n, D_MODEL), F32),
        compiler_params=pltpu.CompilerParams(dimension_semantics=("arbitrary",),
                                             vmem_limit_bytes=VMEM_LIMIT),
        name="combine",
    )(dest, h, rw, final_g, ys)


def _schedule(counts):
    tiles = (counts + TM_EXPERT - 1) // TM_EXPERT
    tile_end = jnp.cumsum(tiles)
    offsets = (tile_end - tiles) * TM_EXPERT
    return tiles, offsets, tile_end[-1:]


def _layer(x, attn_g, w_in, sg_g, w_sp, b_sp, sb_g, sg_out_g, w_out, ffn_g,
           w_rg, b_rg, w_re, b_re, w_gate, w_up, w_down):
    batch, seq, _ = x.shape
    n = batch * seq
    x2 = x.reshape(n, D_MODEL)
    row = lambda v: v.reshape(1, -1)

    qkv, gu, vgn = _inproj(x2, row(attn_g), w_in.astype(BF16), row(sg_g))
    sb = _attention(qkv, batch, seq).reshape(n, SB_WIDTH)

    w_r = jnp.concatenate([w_rg, jnp.transpose(w_re, (1, 0, 2)).reshape(D_MODEL, N_EXPERTS)], axis=1)
    w_r = jnp.pad(w_r, ((0, 0), (0, LANES - w_r.shape[1])))
    wr_hi = w_r.astype(BF16)
    wr_lo = (w_r - wr_hi.astype(F32)).astype(BF16)
    b_r = jnp.pad(jnp.concatenate([b_rg, b_re.reshape(-1)]), (0, LANES - N_GROUPS - N_EXPERTS))
    bsp_full = jnp.repeat(b_sp.T, HEAD_DIM, axis=1)

    h, hn, ri, rw, cnt = _mix(sb, gu, vgn, x2, w_sp, bsp_full, row(sb_g), row(sg_out_g),
                              w_out.astype(BF16), row(ffn_g), wr_hi, wr_lo, row(b_r))

    counts = cnt[0, ROUTER_LANE0:ROUTER_LANE0 + N_EXPERTS].astype(jnp.int32)
    n_rows = 2 * n + N_EXPERTS * TM_EXPERT
    tiles, offsets, n_tiles = _schedule(counts)
    expert, rank = ri[0:2], ri[2:4]
    is_e = expert[None] == jnp.arange(N_EXPERTS, dtype=jnp.int32)[:, None, None]
    dest = (jnp.sum(jnp.where(is_e, offsets[:, None, None], 0), axis=0) + rank).reshape(-1)
    pad_start = offsets + counts
    pad_count = (-counts) % TM_EXPERT

    xs = _dispatch(dest, pad_start, pad_count, n_tiles, hn, n_rows)
    ys = _experts(tiles, n_tiles, xs,
                  w_gate.reshape(N_EXPERTS, D_MODEL, D_EXPERT),
                  w_up.reshape(N_EXPERTS, D_MODEL, D_EXPERT),
                  w_down.reshape(N_EXPERTS, D_EXPERT, D_MODEL))
    return dest, h, rw, ys


def kernel(x, attn_norm_g, w_in, sg_norm_g, w_spatial, b_spatial, sb_out_norm_g, sg_out_norm_g,
           w_out, ffn_norm_g, w_router_group, b_router_group, w_router_expert, b_router_expert,
           w_gate, w_up, w_down, final_norm_g):
    assert attn_norm_g.shape[0] == 1, "single-layer problem"
    batch, seq, _ = x.shape
    dest, h, rw, ys = _layer(x, attn_norm_g[0], w_in[0], sg_norm_g[0], w_spatial[0], b_spatial[0],
                             sb_out_norm_g[0], sg_out_norm_g[0], w_out[0], ffn_norm_g[0],
                             w_router_group[0], b_router_group[0], w_router_expert[0],
                             b_router_expert[0], w_gate[0], w_up[0], w_down[0])
    out = _combine(dest, h, rw, final_norm_g.reshape(1, -1), ys)
    return out.reshape(batch, seq, D_MODEL)
```

```python
import functools
import math

import jax
import jax.numpy as jnp
from jax import lax
from jax.experimental import pallas as pl
from jax.experimental.pallas import tpu as pltpu

D_MODEL = 1024
HEAD_DIM = 64
SB_WIDTH = 512
SG_WIDTH = 512
SG_HEADS = 8
D_IN = 3 * SB_WIDTH + 2 * SG_WIDTH
CHUNK = 128
N_GROUPS = 4
EXPERTS_PER_GROUP = 8
N_EXPERTS = N_GROUPS * EXPERTS_PER_GROUP
D_EXPERT = 512
EPS = 1e-6
F32_EXP_UNDERFLOW = 110.0

LANES = 128
ROW_TILE = D_MODEL // LANES
assert ROW_TILE == 8
HEAD_PAIR = 2 * HEAD_DIM
ROUTER_LANE0 = 8
ROUTER_ROWS = ROUTER_LANE0 + N_EXPERTS
assert EXPERTS_PER_GROUP == 8 and N_GROUPS <= ROUTER_LANE0

TM_PROJ = 512
TQ_ATTN = 256
TM_MIX = 512
TM_ROUTE = 1024
TM_DISPATCH = 512
TM_EXPERT = 256
TM_COMBINE = 256
VMEM_LIMIT = 48 * 1024 * 1024

F32 = jnp.float32
BF16 = jnp.bfloat16


def _rms(x, g):
    return x * lax.rsqrt(jnp.mean(x * x, axis=-1, keepdims=True) + EPS) * g


def _gelu(x):
    c = math.sqrt(2.0 / math.pi)
    return x * (0.5 * (1.0 + jnp.tanh(c * (x + 0.044715 * (x * x * x)))))


def _softplus(z):
    return jnp.maximum(z, 0.0) + jnp.log(1.0 + jnp.exp(-jnp.abs(z)))


def _dot(a, b):
    return jnp.dot(a, b, preferred_element_type=F32)


def _rows_to_tiles(ref, x):
    m = x.shape[0]
    for k in range(ROW_TILE):
        ref[pl.ds(k, m, stride=ROW_TILE), :] = x[:, k * LANES:(k + 1) * LANES]


def _tiles_to_rows(ref, m):
    return jnp.concatenate([ref[pl.ds(k, m, stride=ROW_TILE), :] for k in range(ROW_TILE)], axis=1)


def _token_rows(ref, first_token, n_tokens):
    return ref.at[pl.ds(pl.multiple_of(first_token * ROW_TILE, ROW_TILE), n_tokens * ROW_TILE)]


def _split_bf16(x):
    hi = x.astype(BF16)
    lo = (x - hi.astype(F32)).astype(BF16)
    return hi, lo


def _inproj_kernel(x_ref, g_ref, w_ref, sgg_ref, qkv_ref, gu_ref, vgn_ref):
    hb = _rms(x_ref[...], g_ref[...]).astype(BF16)
    q = _dot(hb, w_ref[:, 0:SB_WIDTH]) * (1.0 / math.sqrt(HEAD_DIM))
    qkv_ref[:, 0:SB_WIDTH] = q.astype(BF16)
    qkv_ref[:, SB_WIDTH:3 * SB_WIDTH] = _dot(hb, w_ref[:, SB_WIDTH:3 * SB_WIDTH]).astype(BF16)
    gu_ref[...] = _gelu(_dot(hb, w_ref[:, 3 * SB_WIDTH:3 * SB_WIDTH + SG_WIDTH]))
    gv = _gelu(_dot(hb, w_ref[:, 3 * SB_WIDTH + SG_WIDTH:D_IN]))
    vgn_ref[...] = _rms(gv, sgg_ref[...]).astype(BF16)


def _inproj(x2, attn_g, w_in_b, sg_g):
    n = x2.shape[0]
    row = lambda i: (i, 0)
    const = lambda i: (0, 0)
    return pl.pallas_call(
        _inproj_kernel,
        grid=(n // TM_PROJ,),
        in_specs=[pl.BlockSpec((TM_PROJ, D_MODEL), row),
                  pl.BlockSpec((1, D_MODEL), const),
                  pl.BlockSpec((D_MODEL, D_IN), const),
                  pl.BlockSpec((1, SG_WIDTH), const)],
        out_specs=[pl.BlockSpec((TM_PROJ, 3 * SB_WIDTH), row),
                   pl.BlockSpec((TM_PROJ, SG_WIDTH), row),
                   pl.BlockSpec((TM_PROJ, SG_WIDTH), row)],
        out_shape=[jax.ShapeDtypeStruct((n, 3 * SB_WIDTH), BF16),
                   jax.ShapeDtypeStruct((n, SG_WIDTH), F32),
                   jax.ShapeDtypeStruct((n, SG_WIDTH), BF16)],
        compiler_params=pltpu.CompilerParams(dimension_semantics=("arbitrary",),
                                             vmem_limit_bytes=VMEM_LIMIT),
        name="inproj",
    )(x2, attn_g, w_in_b, sg_g)


def _attn_kernel(q_ref, k_ref, v_ref, o_ref, q2_ref, carry_ref):
    t = TQ_ATTN
    n_pairs = SB_WIDTH // HEAD_PAIR
    qi = pl.program_id(1)
    lane = lax.broadcasted_iota(jnp.int32, (1, HEAD_PAIR), 1)
    head_lanes = (lane < HEAD_DIM, lane >= HEAD_DIM)
    zero = jnp.zeros((), BF16)
    for p in range(n_pairs):
        qp = q_ref[0, :, p * HEAD_PAIR:(p + 1) * HEAD_PAIR]
        for h in range(2):
            q2_ref[(2 * p + h) * t:(2 * p + h + 1) * t, :] = jnp.where(head_lanes[h], qp, zero)
    r_idx = lax.broadcasted_iota(jnp.int32, (t, t), 0)
    c_idx = lax.broadcasted_iota(jnp.int32, (t, t), 1)
    suffix = (r_idx > c_idx).astype(BF16)
    suffix2 = jnp.concatenate([suffix, suffix], axis=0)
    causal = c_idx < r_idx

    o_ref[...] = jnp.zeros_like(o_ref)
    carry_ref[...] = jnp.zeros_like(carry_ref)

    def block(j, diag):
        start = pl.multiple_of(j * t, t)
        for p in range(n_pairs):
            cols = slice(p * HEAD_PAIR, (p + 1) * HEAD_PAIR)
            rows = slice(2 * p * t, (2 * p + 2) * t)
            kb = k_ref[0, pl.ds(start, t), cols]
            vb = v_ref[0, pl.ds(start, t), cols]
            z = lax.dot_general(q2_ref[rows, :], kb, (((1,), (1,)), ((), ())),
                                preferred_element_type=F32)
            sp = _softplus(z)
            if diag:
                mask2 = jnp.concatenate([causal, causal], axis=0)
                nl = jnp.where(mask2, sp, 0.0)
            else:
                nl = sp
            hi, lo = _split_bf16(nl)
            hl = jnp.concatenate([hi, lo], axis=1)
            after = jnp.concatenate([_dot(hl[0:t], suffix2), _dot(hl[t:2 * t], suffix2)], axis=0)
            carry = carry_ref[rows, :]
            a = jnp.exp(z - sp - after - carry)
            if diag:
                a = jnp.where(mask2, a, 0.0)
            a = a.astype(BF16)
            a2 = jnp.concatenate([a[0:t], a[t:2 * t]], axis=1)
            v2 = jnp.concatenate([jnp.where(head_lanes[0], vb, zero),
                                  jnp.where(head_lanes[1], vb, zero)], axis=0)
            o_ref[0, :, cols] += _dot(a2, v2)
            carry_ref[rows, :] = carry + after[:, 0:1] + nl[:, 0:1]

    def live():
        return jnp.min(carry_ref[...]) < F32_EXP_UNDERFLOW

    @pl.when(qi == 0)
    def _():
        block(qi, True)

    @pl.when(qi > 0)
    def _():
        block(qi, True)
        block(qi - 1, False)

    def body(state):
        it, _ = state
        block(qi - 1 - it, False)
        return it + 1, live()

    lax.while_loop(lambda s: (s[0] < qi) & s[1], body, (jnp.int32(1), live()))


def _attention(qkv, batch, seq):
    qkv3 = qkv.reshape(batch, seq, 3 * SB_WIDTH)
    n_heads = SB_WIDTH // HEAD_DIM
    return pl.pallas_call(
        _attn_kernel,
        grid=(batch, seq // TQ_ATTN),
        in_specs=[pl.BlockSpec((1, TQ_ATTN, SB_WIDTH), lambda b, i: (b, i, 0)),
                  pl.BlockSpec((1, seq, SB_WIDTH), lambda b, i: (b, 0, 1)),
                  pl.BlockSpec((1, seq, SB_WIDTH), lambda b, i: (b, 0, 2))],
        out_specs=pl.BlockSpec((1, TQ_ATTN, SB_WIDTH), lambda b, i: (b, i, 0)),
        out_shape=jax.ShapeDtypeStruct((batch, seq, SB_WIDTH), F32),
        scratch_shapes=[pltpu.VMEM((n_heads * TQ_ATTN, HEAD_PAIR), BF16),
                        pltpu.VMEM((n_heads * TQ_ATTN, 1), F32)],
        compiler_params=pltpu.CompilerParams(dimension_semantics=("arbitrary",) * 2,
                                             vmem_limit_bytes=VMEM_LIMIT),
        name="sb_attention",
    )(qkv3, qkv3, qkv3)


def _mix_kernel(sb_ref, gu_ref, vgn_ref, x_ref, wsp_ref, bsp_ref, sbg_ref, sgg_ref, wout_ref,
                ffng_ref, wr2_ref, br_ref, h_ref, hn_ref, lg_ref, sg_ref):
    tm = TM_MIX

    lane = lax.broadcasted_iota(jnp.int32, (1, LANES), 1)
    first = lane < HEAD_DIM
    zero = jnp.zeros((), BF16)
    r_c = lax.broadcasted_iota(jnp.int32, (CHUNK, CHUNK), 0)
    c_c = lax.broadcasted_iota(jnp.int32, (CHUNK, CHUNK), 1)
    tril = r_c >= c_c
    n_pairs = SG_WIDTH // HEAD_PAIR
    w_pairs = []
    for p in range(n_pairs):
        w0 = jnp.where(tril, wsp_ref[2 * p], 0.0).astype(BF16)
        w1 = jnp.where(tril, wsp_ref[2 * p + 1], 0.0).astype(BF16)
        w_pairs.append(jnp.concatenate([w0, w1], axis=1))
    bsp = bsp_ref[...]
    for c in range(tm // CHUNK):
        rows = slice(c * CHUNK, (c + 1) * CHUNK)
        for p in range(n_pairs):
            cols = slice(p * HEAD_PAIR, (p + 1) * HEAD_PAIR)
            vg = vgn_ref[rows, cols]
            rhs = jnp.concatenate([jnp.where(first, vg, zero), jnp.where(first, zero, vg)], axis=0)
            mixed = _dot(w_pairs[p], rhs) + bsp[:, cols]
            sg_ref[rows, cols] = gu_ref[rows, cols] * mixed
    sgn = _rms(sg_ref[...], sgg_ref[...]).astype(BF16)
    sbn = _rms(sb_ref[...], sbg_ref[...]).astype(BF16)
    h = x_ref[...] + _dot(sbn, wout_ref[0:SB_WIDTH, :]) + _dot(sgn, wout_ref[SB_WIDTH:, :])
    h_ref[...] = h
    hn = _rms(h, ffng_ref[...])
    _rows_to_tiles(hn_ref, hn)

    hn_hi, hn_lo = _split_bf16(hn)
    both = _dot(hn_hi, wr2_ref[...])
    logits = both[:, 0:LANES] + both[:, LANES:] + _dot(hn_lo, wr2_ref[:, 0:LANES]) + br_ref[...]
    lg_ref[...] = logits.T[0:ROUTER_ROWS, :]


def _route_kernel(lg_ref, ri_ref, rw_ref, cnt_ref, count_ref):
    tr = TM_ROUTE
    i = pl.program_id(0)

    @pl.when(i == 0)
    def _():
        count_ref[...] = jnp.zeros_like(count_ref)

    neg = jnp.float32(-jnp.inf)
    row8 = lax.broadcasted_iota(jnp.int32, (8, tr), 0)

    def top(v):
        m = jnp.max(v, axis=0, keepdims=True)
        return m, jnp.min(jnp.where(v == m, row8, 8), axis=0, keepdims=True)

    gl = jnp.where(row8 < N_GROUPS, lg_ref[0:8, :], neg)
    gmax, gidx = top(gl)
    gweight = 1.0 / jnp.sum(jnp.exp(gl - gmax), axis=0, keepdims=True)
    el = lg_ref[8:16, :]
    for g in range(1, N_GROUPS):
        el = jnp.where(gidx == g, lg_ref[8 + 8 * g:16 + 8 * g, :], el)
    m1, i1 = top(el)
    m2, i2 = top(jnp.where(row8 == i1, neg, el))
    t21 = jnp.exp(m2 - m1)
    w1 = gweight / (1.0 + t21)
    w2 = gweight * t21 / (1.0 + t21)
    e1 = gidx * EXPERTS_PER_GROUP + i1
    e2 = gidx * EXPERTS_PER_GROUP + i2

    row_e = lax.broadcasted_iota(jnp.int32, (N_EXPERTS, tr), 0)
    sel1 = row_e == e1
    sel2 = row_e == e2
    onehot = jnp.where(sel1 | sel2, 1.0, 0.0)
    r_t = lax.broadcasted_iota(jnp.int32, (tr, tr), 0)
    c_t = lax.broadcasted_iota(jnp.int32, (tr, tr), 1)
    before = (r_t < c_t).astype(BF16)
    running = count_ref[:, 0:1] + _dot(onehot.astype(BF16), before)
    rank1 = jnp.sum(jnp.where(sel1, running, 0.0), axis=0, keepdims=True)
    rank2 = jnp.sum(jnp.where(sel2, running, 0.0), axis=0, keepdims=True)
    new_count = count_ref[:, 0:1] + jnp.sum(onehot, axis=1, keepdims=True)
    count_ref[...] = jnp.broadcast_to(new_count, count_ref.shape)
    cnt_ref[...] = jnp.broadcast_to(new_count, cnt_ref.shape)

    ri_ref[...] = jnp.where(row8 == 0, e1, jnp.where(row8 == 1, e2, jnp.where(
        row8 == 2, rank1.astype(jnp.int32), jnp.where(row8 == 3, rank2.astype(jnp.int32), 0))))
    row128 = lax.broadcasted_iota(jnp.int32, (LANES, tr), 0)
    rw_ref[...] = jnp.where(row128 == 0, w1, jnp.where(row128 == 1, w2, 0.0)).T


def _route(lg):
    n = lg.shape[1]
    return pl.pallas_call(
        _route_kernel,
        grid=(n // TM_ROUTE,),
        in_specs=[pl.BlockSpec((ROUTER_ROWS, TM_ROUTE), lambda i: (0, i))],
        out_specs=[pl.BlockSpec((8, TM_ROUTE), lambda i: (0, i)),
                   pl.BlockSpec((TM_ROUTE, LANES), lambda i: (i, 0)),
                   pl.BlockSpec((N_EXPERTS, LANES), lambda i: (0, 0))],
        out_shape=[jax.ShapeDtypeStruct((8, n), jnp.int32),
                   jax.ShapeDtypeStruct((n, LANES), F32),
                   jax.ShapeDtypeStruct((N_EXPERTS, LANES), F32)],
        scratch_shapes=[pltpu.VMEM((N_EXPERTS, LANES), F32)],
        compiler_params=pltpu.CompilerParams(dimension_semantics=("arbitrary",),
                                             vmem_limit_bytes=VMEM_LIMIT),
        name="route",
    )(lg)


def _mix(sb, gu, vgn, x2, wsp, bsp_full, sb_g, sg_g, w_out_b, ffn_g, wr2, br):
    n = x2.shape[0]
    row = lambda i: (i, 0)
    const = lambda i: (0, 0)
    return pl.pallas_call(
        _mix_kernel,
        grid=(n // TM_MIX,),
        in_specs=[pl.BlockSpec((TM_MIX, SB_WIDTH), row),
                  pl.BlockSpec((TM_MIX, SG_WIDTH), row),
                  pl.BlockSpec((TM_MIX, SG_WIDTH), row),
                  pl.BlockSpec((TM_MIX, D_MODEL), row),
                  pl.BlockSpec((SG_HEADS, CHUNK, CHUNK), lambda i: (0, 0, 0)),
                  pl.BlockSpec((CHUNK, SG_WIDTH), const),
                  pl.BlockSpec((1, SB_WIDTH), const),
                  pl.BlockSpec((1, SG_WIDTH), const),
                  pl.BlockSpec((D_MODEL, D_MODEL), const),
                  pl.BlockSpec((1, D_MODEL), const),
                  pl.BlockSpec((D_MODEL, 2 * LANES), const),
                  pl.BlockSpec((1, LANES), const)],
        out_specs=[pl.BlockSpec((TM_MIX, D_MODEL), row),
                   pl.BlockSpec((TM_MIX * ROW_TILE, LANES), row),
                   pl.BlockSpec((ROUTER_ROWS, TM_MIX), lambda i: (0, i))],
        out_shape=[jax.ShapeDtypeStruct((n, D_MODEL), F32),
                   jax.ShapeDtypeStruct((n * ROW_TILE, LANES), F32),
                   jax.ShapeDtypeStruct((ROUTER_ROWS, n), F32)],
        scratch_shapes=[pltpu.VMEM((TM_MIX, SG_WIDTH), F32)],
        compiler_params=pltpu.CompilerParams(dimension_semantics=("arbitrary",),
                                             vmem_limit_bytes=VMEM_LIMIT),
        name="mix_router",
    )(sb, gu, vgn, x2, wsp, bsp_full, sb_g, sg_g, w_out_b, ffn_g, wr2, br)


_PAD_BITS = tuple(1 << b for b in reversed(range(TM_EXPERT.bit_length() - 1)))


def _dispatch_kernel(dest_ref, pad_start_ref, pad_count_ref, nt_ref, hn_ref, zeros_ref, xs_ref, sem, zsem):
    tm = TM_DISPATCH
    i = pl.program_id(0)
    n = pl.num_programs(0) * tm
    base = i * tm
    n_tiles_max = xs_ref.shape[0] // (TM_EXPERT * ROW_TILE)

    def pad_copies(do):
        for e in range(N_EXPERTS):
            start = pad_start_ref[e]
            count = pad_count_ref[e]
            for bit in _PAD_BITS:
                @pl.when((count & bit) != 0)
                def _(start=start, bit=bit):
                    do(pltpu.make_async_copy(_token_rows(zeros_ref, 0, bit),
                                             _token_rows(xs_ref, start, bit), zsem))
                start = start + (count & bit)
        for k in range(N_EXPERTS):
            tile = nt_ref[0] + k

            @pl.when(tile < n_tiles_max)
            def _(tile=tile):
                do(pltpu.make_async_copy(zeros_ref, _token_rows(xs_ref, tile * TM_EXPERT, TM_EXPERT), zsem))

    @pl.when(i == 0)
    def _():
        pad_copies(lambda cp: cp.start())

    def body(r, c):
        src = _token_rows(hn_ref, r, 1)
        for s in range(2):
            pltpu.make_async_copy(src, _token_rows(xs_ref, dest_ref[s * n + base + r], 1),
                                  sem).start(priority=s)
        return c

    lax.fori_loop(0, tm, body, 0, unroll=8)
    for _ in range(2):
        pltpu.make_async_copy(hn_ref, _token_rows(xs_ref, 0, tm), sem).wait()

    @pl.when(i == 0)
    def _():
        pad_copies(lambda cp: cp.wait())


def _dispatch(dest, pad_start, pad_count, n_tiles, hn_tiles, n_rows):
    n = hn_tiles.shape[0] // ROW_TILE
    zeros = jnp.zeros((TM_EXPERT * ROW_TILE, LANES), F32)
    return pl.pallas_call(
        _dispatch_kernel,
        grid_spec=pltpu.PrefetchScalarGridSpec(
            num_scalar_prefetch=4,
            grid=(n // TM_DISPATCH,),
            in_specs=[pl.BlockSpec((TM_DISPATCH * ROW_TILE, LANES), lambda i, *_: (i, 0)),
                      pl.BlockSpec(memory_space=pl.ANY)],
            out_specs=pl.BlockSpec(memory_space=pl.ANY),
            scratch_shapes=[pltpu.SemaphoreType.DMA, pltpu.SemaphoreType.DMA]),
        out_shape=jax.ShapeDtypeStruct((n_rows * ROW_TILE, LANES), F32),
        compiler_params=pltpu.CompilerParams(dimension_semantics=("arbitrary",),
                                             vmem_limit_bytes=VMEM_LIMIT),
        name="dispatch",
    )(dest, pad_start, pad_count, n_tiles, hn_tiles, zeros)


X_SLOTS = 3


def _expert_kernel(tiles_ref, nt_ref, xs_ref, wg_ref, wu_ref, wd_ref, y_ref,
                   x_buf, sg_buf, su_buf, sd_buf, wgb, wub, wdb, state, w_sems, x_sems):
    tm = TM_EXPERT
    t = pl.program_id(0)
    nt = nt_ref[0]

    def x_copy(tile):
        slot = lax.rem(tile, X_SLOTS)
        return pltpu.make_async_copy(_token_rows(xs_ref, tile * tm, tm), x_buf.at[slot], x_sems.at[slot])

    def weight_copies(e, slot):
        return (pltpu.make_async_copy(wg_ref.at[e], sg_buf.at[slot], w_sems.at[slot]),
                pltpu.make_async_copy(wu_ref.at[e], su_buf.at[slot], w_sems.at[slot]),
                pltpu.make_async_copy(wd_ref.at[e], sd_buf.at[slot], w_sems.at[slot]))

    def next_with_rows(e):
        return lax.while_loop(lambda k: (k < N_EXPERTS) & (tiles_ref[jnp.minimum(k, N_EXPERTS - 1)] == 0),
                              lambda k: k + 1, e + 1)

    @pl.when(t == 0)
    def _():
        first = next_with_rows(jnp.int32(-1))
        state[0] = jnp.int32(-1)
        state[1] = jnp.int32(0)
        state[2] = jnp.int32(1)
        state[3] = first
        for cp in weight_copies(first, 0):
            cp.start()
        x_copy(0).start()

        @pl.when(nt > 1)
        def _():
            x_copy(1).start()

    @pl.when(t + 2 < nt)
    def _():
        x_copy(t + 2).start()

    @pl.when(t < nt)
    def _():
        @pl.when(state[1] == 0)
        def _():
            e = state[3]
            slot = 1 - state[2]
            nxt = next_with_rows(e)
            state[0] = e
            state[1] = tiles_ref[e]
            state[2] = slot
            state[3] = nxt
            for cp in weight_copies(e, slot):
                cp.wait()

            @pl.when(nxt < N_EXPERTS)
            def _():
                for cp in weight_copies(nxt, 1 - slot):
                    cp.start()

            wgb[...] = sg_buf[slot].astype(BF16)
            wub[...] = su_buf[slot].astype(BF16)
            wdb[...] = sd_buf[slot].astype(BF16)

        state[1] = state[1] - 1
        x_copy(t).wait()
        x = _tiles_to_rows(x_buf.at[lax.rem(t, X_SLOTS)], tm).astype(BF16)
        g = _dot(x, wgb[...])
        u = _dot(x, wub[...])
        hidden = (g * jax.nn.sigmoid(g)) * u
        _rows_to_tiles(y_ref, _dot(hidden.astype(BF16), wdb[...]))

    @pl.when(t >= nt)
    def _():
        y_ref[...] = jnp.zeros_like(y_ref)


def _experts(tiles, n_tiles, xs, wg, wu, wd):
    n_rows = xs.shape[0] // ROW_TILE
    any_spec = pl.BlockSpec(memory_space=pl.ANY)
    return pl.pallas_call(
        _expert_kernel,
        grid_spec=pltpu.PrefetchScalarGridSpec(
            num_scalar_prefetch=2,
            grid=(n_rows // TM_EXPERT,),
            in_specs=[any_spec, any_spec, any_spec, any_spec],
            out_specs=pl.BlockSpec((TM_EXPERT * ROW_TILE, LANES), lambda t, *_: (t, 0)),
            scratch_shapes=[pltpu.VMEM((X_SLOTS, TM_EXPERT * ROW_TILE, LANES), F32),
                            pltpu.VMEM((2, D_MODEL, D_EXPERT), F32),
                            pltpu.VMEM((2, D_MODEL, D_EXPERT), F32),
                            pltpu.VMEM((2, D_EXPERT, D_MODEL), F32),
                            pltpu.VMEM((D_MODEL, D_EXPERT), BF16),
                            pltpu.VMEM((D_MODEL, D_EXPERT), BF16),
                            pltpu.VMEM((D_EXPERT, D_MODEL), BF16),
                            pltpu.SMEM((4,), jnp.int32),
                            pltpu.SemaphoreType.DMA((2,)),
                            pltpu.SemaphoreType.DMA((X_SLOTS,))]),
        out_shape=jax.ShapeDtypeStruct((n_rows * ROW_TILE, LANES), F32),
        compiler_params=pltpu.CompilerParams(dimension_semantics=("arbitrary",),
                                             vmem_limit_bytes=VMEM_LIMIT),
        name="expert_mlp",
    )(tiles, n_tiles, xs, wg, wu, wd)


def _combine_kernel(dest_ref, h_ref, rw_ref, fg_ref, y_ref, o_ref, buf, sems):
    tm = TM_COMBINE
    i = pl.program_id(0)
    n_steps = pl.num_programs(0)
    n = n_steps * tm
    cur = i % 2

    def fetch(step, half):
        def body(r, c):
            for s in range(2):
                pltpu.make_async_copy(_token_rows(y_ref, dest_ref[s * n + step * tm + r], 1),
                                      _token_rows(buf.at[half, s], r, 1),
                                      sems.at[half]).start(priority=s)
            return c

        lax.fori_loop(0, tm, body, 0, unroll=8)

    @pl.when(i == 0)
    def _():
        fetch(0, 0)

    @pl.when(i + 1 < n_steps)
    def _():
        fetch(i + 1, 1 - cur)

    for s in range(2):
        pltpu.make_async_copy(_token_rows(y_ref, 0, tm), buf.at[cur, s], sems.at[cur]).wait()
    rw = rw_ref[...]
    out = (h_ref[...] + rw[:, 0:1] * _tiles_to_rows(buf.at[cur, 0], tm)
           + rw[:, 1:2] * _tiles_to_rows(buf.at[cur, 1], tm))
    o_ref[...] = _rms(out, fg_ref[...])


def _combine(dest, h, rw, final_g, ys):
    n = h.shape[0]
    return pl.pallas_call(
        _combine_kernel,
        grid_spec=pltpu.PrefetchScalarGridSpec(
            num_scalar_prefetch=1,
            grid=(n // TM_COMBINE,),
            in_specs=[pl.BlockSpec((TM_COMBINE, D_MODEL), lambda i, d: (i, 0)),
                      pl.BlockSpec((TM_COMBINE, LANES), lambda i, d: (i, 0)),
                      pl.BlockSpec((1, D_MODEL), lambda i, d: (0, 0)),
                      pl.BlockSpec(memory_space=pl.ANY)],
            out_specs=pl.BlockSpec((TM_COMBINE, D_MODEL), lambda i, d: (i, 0)),
            scratch_shapes=[pltpu.VMEM((2, 2, TM_COMBINE * ROW_TILE, LANES), F32),
                            pltpu.SemaphoreType.DMA((2,))]),
        out_shape=jax.ShapeDtypeStruct((n, D_MODEL), F32),
        compiler_params=pltpu.CompilerParams(dimension_semantics=("arbitrary",),
                                             vmem_limit_bytes=VMEM_LIMIT),
        name="combine",
    )(dest, h, rw, final_g, ys)


def _schedule(counts):
    tiles = (counts + TM_EXPERT - 1) // TM_EXPERT
    tile_end = jnp.cumsum(tiles)
    offsets = (tile_end - tiles) * TM_EXPERT
    return tiles, offsets, tile_end[-1:]


def _layer(x, attn_g, w_in, sg_g, w_sp, b_sp, sb_g, sg_out_g, w_out, ffn_g,
           w_rg, b_rg, w_re, b_re, w_gate, w_up, w_down):
    batch, seq, _ = x.shape
    n = batch * seq
    x2 = x.reshape(n, D_MODEL)
    row = lambda v: v.reshape(1, -1)

    qkv, gu, vgn = _inproj(x2, row(attn_g), w_in.astype(BF16), row(sg_g))
    sb = _attention(qkv, batch, seq).reshape(n, SB_WIDTH)

    pad_lanes = lambda v, width: jnp.pad(v, [(0, 0)] * (v.ndim - 1) + [(0, width - v.shape[-1])])
    w_r = jnp.concatenate([pad_lanes(w_rg, ROUTER_LANE0),
                           jnp.transpose(w_re, (1, 0, 2)).reshape(D_MODEL, N_EXPERTS)], axis=1)
    w_r = pad_lanes(w_r, LANES)
    wr_hi = w_r.astype(BF16)
    wr_lo = (w_r - wr_hi.astype(F32)).astype(BF16)
    wr2 = jnp.concatenate([wr_hi, wr_lo], axis=1)
    b_r = pad_lanes(jnp.concatenate([pad_lanes(b_rg, ROUTER_LANE0), b_re.reshape(-1)]), LANES)
    bsp_full = jnp.repeat(b_sp.T, HEAD_DIM, axis=1)

    h, hn, lg = _mix(sb, gu, vgn, x2, w_sp, bsp_full, row(sb_g), row(sg_out_g),
                     w_out.astype(BF16), row(ffn_g), wr2, row(b_r))
    ri, rw, cnt = _route(lg)

    counts = cnt[:, 0].astype(jnp.int32)
    n_rows = 2 * n + N_EXPERTS * TM_EXPERT
    tiles, offsets, n_tiles = _schedule(counts)
    expert, rank = ri[0:2], ri[2:4]
    is_e = expert[None] == jnp.arange(N_EXPERTS, dtype=jnp.int32)[:, None, None]
    dest = (jnp.sum(jnp.where(is_e, offsets[:, None, None], 0), axis=0) + rank).reshape(-1)
    pad_start = offsets + counts
    pad_count = (-counts) % TM_EXPERT

    xs = _dispatch(dest, pad_start, pad_count, n_tiles, hn, n_rows)
    ys = _experts(tiles, n_tiles, xs,
                  w_gate.reshape(N_EXPERTS, D_MODEL, D_EXPERT),
                  w_up.reshape(N_EXPERTS, D_MODEL, D_EXPERT),
                  w_down.reshape(N_EXPERTS, D_EXPERT, D_MODEL))
    return dest, h, rw, ys


def kernel(x, attn_norm_g, w_in, sg_norm_g, w_spatial, b_spatial, sb_out_norm_g, sg_out_norm_g,
           w_out, ffn_norm_g, w_router_group, b_router_group, w_router_expert, b_router_expert,
           w_gate, w_up, w_down, final_norm_g):
    assert attn_norm_g.shape[0] == 1, "single-layer problem"
    batch, seq, _ = x.shape
    dest, h, rw, ys = _layer(x, attn_norm_g[0], w_in[0], sg_norm_g[0], w_spatial[0], b_spatial[0],
                             sb_out_norm_g[0], sg_out_norm_g[0], w_out[0], ffn_norm_g[0],
                             w_router_group[0], b_router_group[0], w_router_expert[0],
                             b_router_expert[0], w_gate[0], w_up[0], w_down[0])
    out = _combine(dest, h, rw, final_norm_g.reshape(1, -1), ys)
    return out.reshape(batch, seq, D_MODEL)
```

```python
import functools
import math

import jax
import jax.numpy as jnp
from jax import lax
from jax.experimental import pallas as pl
from jax.experimental.pallas import tpu as pltpu

D_MODEL = 1024
HEAD_DIM = 64
SB_WIDTH = 512
SG_WIDTH = 512
SG_HEADS = 8
D_IN = 3 * SB_WIDTH + 2 * SG_WIDTH
CHUNK = 128
N_GROUPS = 4
EXPERTS_PER_GROUP = 8
N_EXPERTS = N_GROUPS * EXPERTS_PER_GROUP
D_EXPERT = 512
EPS = 1e-6
F32_EXP_UNDERFLOW = 110.0

LANES = 128
ROW_TILE = D_MODEL // LANES
assert ROW_TILE == 8
HEAD_PAIR = 2 * HEAD_DIM
ROUTER_LANE0 = 8
ROUTER_ROWS = ROUTER_LANE0 + N_EXPERTS
assert EXPERTS_PER_GROUP == 8 and N_GROUPS <= ROUTER_LANE0

TM_PROJ = 512
TQ_ATTN = 256
TM_MIX = 512
TM_ROUTE = 1024
TM_DISPATCH = 512
TM_EXPERT = 256
TM_COMBINE = 256
VMEM_LIMIT = 48 * 1024 * 1024

F32 = jnp.float32
BF16 = jnp.bfloat16


def _rms(x, g):
    return x * lax.rsqrt(jnp.mean(x * x, axis=-1, keepdims=True) + EPS) * g


def _gelu(x):
    c = math.sqrt(2.0 / math.pi)
    return x * (0.5 * (1.0 + jnp.tanh(c * (x + 0.044715 * (x * x * x)))))


def _softplus(z):
    return jnp.maximum(z, 0.0) + jnp.log(1.0 + jnp.exp(-jnp.abs(z)))


def _dot(a, b):
    return jnp.dot(a, b, preferred_element_type=F32)


def _rows_to_tiles(ref, x):
    m = x.shape[0]
    for k in range(ROW_TILE):
        ref[pl.ds(k, m, stride=ROW_TILE), :] = x[:, k * LANES:(k + 1) * LANES]


def _tiles_to_rows(ref, m):
    return jnp.concatenate([ref[pl.ds(k, m, stride=ROW_TILE), :] for k in range(ROW_TILE)], axis=1)


def _token_rows(ref, first_token, n_tokens):
    return ref.at[pl.ds(pl.multiple_of(first_token * ROW_TILE, ROW_TILE), n_tokens * ROW_TILE)]


def _split_bf16(x):
    hi = x.astype(BF16)
    lo = (x - hi.astype(F32)).astype(BF16)
    return hi, lo


def _inproj_kernel(x_ref, g_ref, w_ref, sgg_ref, qkv_ref, gu_ref, vgn_ref):
    hb = _rms(x_ref[...], g_ref[...]).astype(BF16)
    q = _dot(hb, w_ref[:, 0:SB_WIDTH]) * (1.0 / math.sqrt(HEAD_DIM))
    qkv_ref[:, 0:SB_WIDTH] = q.astype(BF16)
    qkv_ref[:, SB_WIDTH:3 * SB_WIDTH] = _dot(hb, w_ref[:, SB_WIDTH:3 * SB_WIDTH]).astype(BF16)
    gu_ref[...] = _gelu(_dot(hb, w_ref[:, 3 * SB_WIDTH:3 * SB_WIDTH + SG_WIDTH]))
    gv = _gelu(_dot(hb, w_ref[:, 3 * SB_WIDTH + SG_WIDTH:D_IN]))
    vgn_ref[...] = _rms(gv, sgg_ref[...]).astype(BF16)


def _inproj(x2, attn_g, w_in_b, sg_g):
    n = x2.shape[0]
    row = lambda i: (i, 0)
    const = lambda i: (0, 0)
    return pl.pallas_call(
        _inproj_kernel,
        grid=(n // TM_PROJ,),
        in_specs=[pl.BlockSpec((TM_PROJ, D_MODEL), row),
                  pl.BlockSpec((1, D_MODEL), const),
                  pl.BlockSpec((D_MODEL, D_IN), const),
                  pl.BlockSpec((1, SG_WIDTH), const)],
        out_specs=[pl.BlockSpec((TM_PROJ, 3 * SB_WIDTH), row),
                   pl.BlockSpec((TM_PROJ, SG_WIDTH), row),
                   pl.BlockSpec((TM_PROJ, SG_WIDTH), row)],
        out_shape=[jax.ShapeDtypeStruct((n, 3 * SB_WIDTH), BF16),
                   jax.ShapeDtypeStruct((n, SG_WIDTH), F32),
                   jax.ShapeDtypeStruct((n, SG_WIDTH), BF16)],
        compiler_params=pltpu.CompilerParams(dimension_semantics=("arbitrary",),
                                             vmem_limit_bytes=VMEM_LIMIT),
        name="inproj",
    )(x2, attn_g, w_in_b, sg_g)


def _attn_kernel(q_ref, k_ref, v_ref, o_ref, q2_ref, carry_ref):
    t = TQ_ATTN
    n_pairs = SB_WIDTH // HEAD_PAIR
    qi = pl.program_id(1)
    lane = lax.broadcasted_iota(jnp.int32, (1, HEAD_PAIR), 1)
    head_lanes = (lane < HEAD_DIM, lane >= HEAD_DIM)
    zero = jnp.zeros((), BF16)
    for p in range(n_pairs):
        qp = q_ref[0, :, p * HEAD_PAIR:(p + 1) * HEAD_PAIR]
        for h in range(2):
            q2_ref[(2 * p + h) * t:(2 * p + h + 1) * t, :] = jnp.where(head_lanes[h], qp, zero)
    r_idx = lax.broadcasted_iota(jnp.int32, (t, t), 0)
    c_idx = lax.broadcasted_iota(jnp.int32, (t, t), 1)
    suffix = (r_idx > c_idx).astype(BF16)
    suffix2 = jnp.concatenate([suffix, suffix], axis=0)
    causal = c_idx < r_idx

    o_ref[...] = jnp.zeros_like(o_ref)
    carry_ref[...] = jnp.zeros_like(carry_ref)

    def block(j, diag):
        start = pl.multiple_of(j * t, t)
        for p in range(n_pairs):
            cols = slice(p * HEAD_PAIR, (p + 1) * HEAD_PAIR)
            rows = slice(2 * p * t, (2 * p + 2) * t)
            kb = k_ref[0, pl.ds(start, t), cols]
            vb = v_ref[0, pl.ds(start, t), cols]
            z = lax.dot_general(q2_ref[rows, :], kb, (((1,), (1,)), ((), ())),
                                preferred_element_type=F32)
            sp = _softplus(z)
            if diag:
                mask2 = jnp.concatenate([causal, causal], axis=0)
                nl = jnp.where(mask2, sp, 0.0)
            else:
                nl = sp
            hi, lo = _split_bf16(nl)
            hl = jnp.concatenate([hi, lo], axis=1)
            after = jnp.concatenate([_dot(hl[0:t], suffix2), _dot(hl[t:2 * t], suffix2)], axis=0)
            carry = carry_ref[rows, :]
            a = jnp.exp(z - sp - after - carry)
            if diag:
                a = jnp.where(mask2, a, 0.0)
            a = a.astype(BF16)
            a2 = jnp.concatenate([a[0:t], a[t:2 * t]], axis=1)
            v2 = jnp.concatenate([jnp.where(head_lanes[0], vb, zero),
                                  jnp.where(head_lanes[1], vb, zero)], axis=0)
            o_ref[0, :, cols] += _dot(a2, v2)
            carry_ref[rows, :] = carry + after[:, 0:1] + nl[:, 0:1]

    def live():
        return jnp.min(carry_ref[...]) < F32_EXP_UNDERFLOW

    block(qi, True)

    def body(state):
        it, _ = state
        block(qi - 1 - it, False)
        return it + 1, live()

    lax.while_loop(lambda s: (s[0] < qi) & s[1], body, (jnp.int32(0), live()))


def _attention(qkv, batch, seq):
    qkv3 = qkv.reshape(batch, seq, 3 * SB_WIDTH)
    n_heads = SB_WIDTH // HEAD_DIM
    return pl.pallas_call(
        _attn_kernel,
        grid=(batch, seq // TQ_ATTN),
        in_specs=[pl.BlockSpec((1, TQ_ATTN, SB_WIDTH), lambda b, i: (b, i, 0)),
                  pl.BlockSpec((1, seq, SB_WIDTH), lambda b, i: (b, 0, 1)),
                  pl.BlockSpec((1, seq, SB_WIDTH), lambda b, i: (b, 0, 2))],
        out_specs=pl.BlockSpec((1, TQ_ATTN, SB_WIDTH), lambda b, i: (b, i, 0)),
        out_shape=jax.ShapeDtypeStruct((batch, seq, SB_WIDTH), F32),
        scratch_shapes=[pltpu.VMEM((n_heads * TQ_ATTN, HEAD_PAIR), BF16),
                        pltpu.VMEM((n_heads * TQ_ATTN, 1), F32)],
        compiler_params=pltpu.CompilerParams(dimension_semantics=("arbitrary",) * 2,
                                             vmem_limit_bytes=VMEM_LIMIT),
        name="sb_attention",
    )(qkv3, qkv3, qkv3)


def _mix_kernel(sb_ref, gu_ref, vgn_ref, x_ref, wsp_ref, bsp_ref, sbg_ref, sgg_ref, wout_ref,
                ffng_ref, wr2_ref, br_ref, h_ref, hn_ref, lg_ref, sg_ref):
    tm = TM_MIX

    lane = lax.broadcasted_iota(jnp.int32, (1, LANES), 1)
    first = lane < HEAD_DIM
    zero = jnp.zeros((), BF16)
    r_c = lax.broadcasted_iota(jnp.int32, (CHUNK, CHUNK), 0)
    c_c = lax.broadcasted_iota(jnp.int32, (CHUNK, CHUNK), 1)
    tril = r_c >= c_c
    n_pairs = SG_WIDTH // HEAD_PAIR
    w_pairs = []
    for p in range(n_pairs):
        w0 = jnp.where(tril, wsp_ref[2 * p], 0.0).astype(BF16)
        w1 = jnp.where(tril, wsp_ref[2 * p + 1], 0.0).astype(BF16)
        w_pairs.append(jnp.concatenate([w0, w1], axis=1))
    bsp = bsp_ref[...]
    for c in range(tm // CHUNK):
        rows = slice(c * CHUNK, (c + 1) * CHUNK)
        for p in range(n_pairs):
            cols = slice(p * HEAD_PAIR, (p + 1) * HEAD_PAIR)
            vg = vgn_ref[rows, cols]
            rhs = jnp.concatenate([jnp.where(first, vg, zero), jnp.where(first, zero, vg)], axis=0)
            mixed = _dot(w_pairs[p], rhs) + bsp[:, cols]
            sg_ref[rows, cols] = gu_ref[rows, cols] * mixed
    sgn = _rms(sg_ref[...], sgg_ref[...]).astype(BF16)
    sbn = _rms(sb_ref[...], sbg_ref[...]).astype(BF16)
    h = x_ref[...] + _dot(sbn, wout_ref[0:SB_WIDTH, :]) + _dot(sgn, wout_ref[SB_WIDTH:, :])
    h_ref[...] = h
    hn = _rms(h, ffng_ref[...])
    _rows_to_tiles(hn_ref, hn)

    hn_hi, hn_lo = _split_bf16(hn)
    both = _dot(hn_hi, wr2_ref[...])
    logits = both[:, 0:LANES] + both[:, LANES:] + _dot(hn_lo, wr2_ref[:, 0:LANES]) + br_ref[...]
    lg_ref[...] = logits.T[0:ROUTER_ROWS, :]


def _route_kernel(lg_ref, ri_ref, rw_ref, cnt_ref, count_ref):
    tr = TM_ROUTE
    i = pl.program_id(0)

    @pl.when(i == 0)
    def _():
        count_ref[...] = jnp.zeros_like(count_ref)

    neg = jnp.float32(-jnp.inf)
    row8 = lax.broadcasted_iota(jnp.int32, (8, tr), 0)

    def top(v):
        m = jnp.max(v, axis=0, keepdims=True)
        return m, jnp.min(jnp.where(v == m, row8, 8), axis=0, keepdims=True)

    gl = jnp.where(row8 < N_GROUPS, lg_ref[0:8, :], neg)
    gmax, gidx = top(gl)
    gweight = 1.0 / jnp.sum(jnp.exp(gl - gmax), axis=0, keepdims=True)
    el = lg_ref[8:16, :]
    for g in range(1, N_GROUPS):
        el = jnp.where(gidx == g, lg_ref[8 + 8 * g:16 + 8 * g, :], el)
    m1, i1 = top(el)
    m2, i2 = top(jnp.where(row8 == i1, neg, el))
    t21 = jnp.exp(m2 - m1)
    w1 = gweight / (1.0 + t21)
    w2 = gweight * t21 / (1.0 + t21)
    e1 = gidx * EXPERTS_PER_GROUP + i1
    e2 = gidx * EXPERTS_PER_GROUP + i2

    row_e = lax.broadcasted_iota(jnp.int32, (N_EXPERTS, tr), 0)
    sel1 = row_e == e1
    sel2 = row_e == e2
    onehot = jnp.where(sel1 | sel2, 1.0, 0.0)
    r_t = lax.broadcasted_iota(jnp.int32, (tr, tr), 0)
    c_t = lax.broadcasted_iota(jnp.int32, (tr, tr), 1)
    before = (r_t < c_t).astype(BF16)
    running = count_ref[:, 0:1] + _dot(onehot.astype(BF16), before)
    rank1 = jnp.sum(jnp.where(sel1, running, 0.0), axis=0, keepdims=True)
    rank2 = jnp.sum(jnp.where(sel2, running, 0.0), axis=0, keepdims=True)
    new_count = count_ref[:, 0:1] + jnp.sum(onehot, axis=1, keepdims=True)
    count_ref[...] = jnp.broadcast_to(new_count, count_ref.shape)
    cnt_ref[...] = jnp.broadcast_to(new_count, cnt_ref.shape)

    ri_ref[...] = jnp.where(row8 == 0, e1, jnp.where(row8 == 1, e2, jnp.where(
        row8 == 2, rank1.astype(jnp.int32), jnp.where(row8 == 3, rank2.astype(jnp.int32), 0))))
    row128 = lax.broadcasted_iota(jnp.int32, (LANES, tr), 0)
    rw_ref[...] = jnp.where(row128 == 0, w1, jnp.where(row128 == 1, w2, 0.0)).T


def _route(lg):
    n = lg.shape[1]
    return pl.pallas_call(
        _route_kernel,
        grid=(n // TM_ROUTE,),
        in_specs=[pl.BlockSpec((ROUTER_ROWS, TM_ROUTE), lambda i: (0, i))],
        out_specs=[pl.BlockSpec((8, TM_ROUTE), lambda i: (0, i)),
                   pl.BlockSpec((TM_ROUTE, LANES), lambda i: (i, 0)),
                   pl.BlockSpec((N_EXPERTS, LANES), lambda i: (0, 0))],
        out_shape=[jax.ShapeDtypeStruct((8, n), jnp.int32),
                   jax.ShapeDtypeStruct((n, LANES), F32),
                   jax.ShapeDtypeStruct((N_EXPERTS, LANES), F32)],
        scratch_shapes=[pltpu.VMEM((N_EXPERTS, LANES), F32)],
        compiler_params=pltpu.CompilerParams(dimension_semantics=("arbitrary",),
                                             vmem_limit_bytes=VMEM_LIMIT),
        name="route",
    )(lg)


def _mix(sb, gu, vgn, x2, wsp, bsp_full, sb_g, sg_g, w_out_b, ffn_g, wr2, br):
    n = x2.shape[0]
    row = lambda i: (i, 0)
    const = lambda i: (0, 0)
    return pl.pallas_call(
        _mix_kernel,
        grid=(n // TM_MIX,),
        in_specs=[pl.BlockSpec((TM_MIX, SB_WIDTH), row),
                  pl.BlockSpec((TM_MIX, SG_WIDTH), row),
                  pl.BlockSpec((TM_MIX, SG_WIDTH), row),
                  pl.BlockSpec((TM_MIX, D_MODEL), row),
                  pl.BlockSpec((SG_HEADS, CHUNK, CHUNK), lambda i: (0, 0, 0)),
                  pl.BlockSpec((CHUNK, SG_WIDTH), const),
                  pl.BlockSpec((1, SB_WIDTH), const),
                  pl.BlockSpec((1, SG_WIDTH), const),
                  pl.BlockSpec((D_MODEL, D_MODEL), const),
                  pl.BlockSpec((1, D_MODEL), const),
                  pl.BlockSpec((D_MODEL, 2 * LANES), const),
                  pl.BlockSpec((1, LANES), const)],
        out_specs=[pl.BlockSpec((TM_MIX, D_MODEL), row),
                   pl.BlockSpec((TM_MIX * ROW_TILE, LANES), row),
                   pl.BlockSpec((ROUTER_ROWS, TM_MIX), lambda i: (0, i))],
        out_shape=[jax.ShapeDtypeStruct((n, D_MODEL), F32),
                   jax.ShapeDtypeStruct((n * ROW_TILE, LANES), F32),
                   jax.ShapeDtypeStruct((ROUTER_ROWS, n), F32)],
        scratch_shapes=[pltpu.VMEM((TM_MIX, SG_WIDTH), F32)],
        compiler_params=pltpu.CompilerParams(dimension_semantics=("arbitrary",),
                                             vmem_limit_bytes=VMEM_LIMIT),
        name="mix_router",
    )(sb, gu, vgn, x2, wsp, bsp_full, sb_g, sg_g, w_out_b, ffn_g, wr2, br)


_PAD_BITS = tuple(1 << b for b in reversed(range(TM_EXPERT.bit_length() - 1)))


def _dispatch_kernel(dest_ref, pad_start_ref, pad_count_ref, nt_ref, hn_ref, zeros_ref, xs_ref, sem, zsem):
    tm = TM_DISPATCH
    i = pl.program_id(0)
    n = pl.num_programs(0) * tm
    base = i * tm
    n_tiles_max = xs_ref.shape[0] // (TM_EXPERT * ROW_TILE)

    def pad_copies(do):
        for e in range(N_EXPERTS):
            start = pad_start_ref[e]
            count = pad_count_ref[e]
            for bit in _PAD_BITS:
                @pl.when((count & bit) != 0)
                def _(start=start, bit=bit):
                    do(pltpu.make_async_copy(_token_rows(zeros_ref, 0, bit),
                                             _token_rows(xs_ref, start, bit), zsem))
                start = start + (count & bit)
        for k in range(N_EXPERTS):
            tile = nt_ref[0] + k

            @pl.when(tile < n_tiles_max)
            def _(tile=tile):
                do(pltpu.make_async_copy(zeros_ref, _token_rows(xs_ref, tile * TM_EXPERT, TM_EXPERT), zsem))

    @pl.when(i == 0)
    def _():
        pad_copies(lambda cp: cp.start())

    def body(r, c):
        src = _token_rows(hn_ref, r, 1)
        for s in range(2):
            pltpu.make_async_copy(src, _token_rows(xs_ref, dest_ref[s * n + base + r], 1),
                                  sem).start(priority=s)
        return c

    lax.fori_loop(0, tm, body, 0, unroll=8)
    for _ in range(2):
        pltpu.make_async_copy(hn_ref, _token_rows(xs_ref, 0, tm), sem).wait()

    @pl.when(i == 0)
    def _():
        pad_copies(lambda cp: cp.wait())


def _dispatch(dest, pad_start, pad_count, n_tiles, hn_tiles, n_rows):
    n = hn_tiles.shape[0] // ROW_TILE
    zeros = jnp.zeros((TM_EXPERT * ROW_TILE, LANES), F32)
    return pl.pallas_call(
        _dispatch_kernel,
        grid_spec=pltpu.PrefetchScalarGridSpec(
            num_scalar_prefetch=4,
            grid=(n // TM_DISPATCH,),
            in_specs=[pl.BlockSpec((TM_DISPATCH * ROW_TILE, LANES), lambda i, *_: (i, 0)),
                      pl.BlockSpec(memory_space=pl.ANY)],
            out_specs=pl.BlockSpec(memory_space=pl.ANY),
            scratch_shapes=[pltpu.SemaphoreType.DMA, pltpu.SemaphoreType.DMA]),
        out_shape=jax.ShapeDtypeStruct((n_rows * ROW_TILE, LANES), F32),
        compiler_params=pltpu.CompilerParams(dimension_semantics=("arbitrary",),
                                             vmem_limit_bytes=VMEM_LIMIT),
        name="dispatch",
    )(dest, pad_start, pad_count, n_tiles, hn_tiles, zeros)


X_SLOTS = 3


def _expert_kernel(tiles_ref, nt_ref, xs_ref, wg_ref, wu_ref, wd_ref, y_ref,
                   x_buf, sg_buf, su_buf, sd_buf, wgb, wub, wdb, state, w_sems, x_sems):
    tm = TM_EXPERT
    t = pl.program_id(0)
    nt = nt_ref[0]

    def x_copy(tile):
        slot = lax.rem(tile, X_SLOTS)
        return pltpu.make_async_copy(_token_rows(xs_ref, tile * tm, tm), x_buf.at[slot], x_sems.at[slot])

    def weight_copies(e, slot):
        return (pltpu.make_async_copy(wg_ref.at[e], sg_buf.at[slot], w_sems.at[slot]),
                pltpu.make_async_copy(wu_ref.at[e], su_buf.at[slot], w_sems.at[slot]),
                pltpu.make_async_copy(wd_ref.at[e], sd_buf.at[slot], w_sems.at[slot]))

    def next_with_rows(e):
        return lax.while_loop(lambda k: (k < N_EXPERTS) & (tiles_ref[jnp.minimum(k, N_EXPERTS - 1)] == 0),
                              lambda k: k + 1, e + 1)

    @pl.when(t == 0)
    def _():
        first = next_with_rows(jnp.int32(-1))
        state[0] = jnp.int32(-1)
        state[1] = jnp.int32(0)
        state[2] = jnp.int32(1)
        state[3] = first
        for cp in weight_copies(first, 0):
            cp.start()
        x_copy(0).start()

        @pl.when(nt > 1)
        def _():
            x_copy(1).start()

    @pl.when(t + 2 < nt)
    def _():
        x_copy(t + 2).start()

    @pl.when(t < nt)
    def _():
        @pl.when(state[1] == 0)
        def _():
            e = state[3]
            slot = 1 - state[2]
            nxt = next_with_rows(e)
            state[0] = e
            state[1] = tiles_ref[e]
            state[2] = slot
            state[3] = nxt
            for cp in weight_copies(e, slot):
                cp.wait()

            @pl.when(nxt < N_EXPERTS)
            def _():
                for cp in weight_copies(nxt, 1 - slot):
                    cp.start()

            wgb[...] = sg_buf[slot].astype(BF16)
            wub[...] = su_buf[slot].astype(BF16)
            wdb[...] = sd_buf[slot].astype(BF16)

        state[1] = state[1] - 1
        x_copy(t).wait()
        x = _tiles_to_rows(x_buf.at[lax.rem(t, X_SLOTS)], tm).astype(BF16)
        g = _dot(x, wgb[...])
        u = _dot(x, wub[...])
        hidden = (g * jax.nn.sigmoid(g)) * u
        _rows_to_tiles(y_ref, _dot(hidden.astype(BF16), wdb[...]))

    @pl.when(t >= nt)
    def _():
        y_ref[...] = jnp.zeros_like(y_ref)


def _experts(tiles, n_tiles, xs, wg, wu, wd):
    n_rows = xs.shape[0] // ROW_TILE
    any_spec = pl.BlockSpec(memory_space=pl.ANY)
    return pl.pallas_call(
        _expert_kernel,
        grid_spec=pltpu.PrefetchScalarGridSpec(
            num_scalar_prefetch=2,
            grid=(n_rows // TM_EXPERT,),
            in_specs=[any_spec, any_spec, any_spec, any_spec],
            out_specs=pl.BlockSpec((TM_EXPERT * ROW_TILE, LANES), lambda t, *_: (t, 0)),
            scratch_shapes=[pltpu.VMEM((X_SLOTS, TM_EXPERT * ROW_TILE, LANES), F32),
                            pltpu.VMEM((2, D_MODEL, D_EXPERT), F32),
                            pltpu.VMEM((2, D_MODEL, D_EXPERT), F32),
                            pltpu.VMEM((2, D_EXPERT, D_MODEL), F32),
                            pltpu.VMEM((D_MODEL, D_EXPERT), BF16),
                            pltpu.VMEM((D_MODEL, D_EXPERT), BF16),
                            pltpu.VMEM((D_EXPERT, D_MODEL), BF16),
                            pltpu.SMEM((4,), jnp.int32),
                            pltpu.SemaphoreType.DMA((2,)),
                            pltpu.SemaphoreType.DMA((X_SLOTS,))]),
        out_shape=jax.ShapeDtypeStruct((n_rows * ROW_TILE, LANES), F32),
        compiler_params=pltpu.CompilerParams(dimension_semantics=("arbitrary",),
                                             vmem_limit_bytes=VMEM_LIMIT),
        name="expert_mlp",
    )(tiles, n_tiles, xs, wg, wu, wd)


def _combine_kernel(dest_ref, h_ref, rw_ref, fg_ref, y_ref, o_ref, buf, sems):
    tm = TM_COMBINE
    i = pl.program_id(0)
    n_steps = pl.num_programs(0)
    n = n_steps * tm
    cur = i % 2

    def fetch(step, half):
        def body(r, c):
            for s in range(2):
                pltpu.make_async_copy(_token_rows(y_ref, dest_ref[s * n + step * tm + r], 1),
                                      _token_rows(buf.at[half, s], r, 1),
                                      sems.at[half]).start(priority=s)
            return c

        lax.fori_loop(0, tm, body, 0, unroll=8)

    @pl.when(i == 0)
    def _():
        fetch(0, 0)

    @pl.when(i + 1 < n_steps)
    def _():
        fetch(i + 1, 1 - cur)

    for s in range(2):
        pltpu.make_async_copy(_token_rows(y_ref, 0, tm), buf.at[cur, s], sems.at[cur]).wait()
    rw = rw_ref[...]
    out = (h_ref[...] + rw[:, 0:1] * _tiles_to_rows(buf.at[cur, 0], tm)
           + rw[:, 1:2] * _tiles_to_rows(buf.at[cur, 1], tm))
    o_ref[...] = _rms(out, fg_ref[...])


def _combine(dest, h, rw, final_g, ys):
    n = h.shape[0]
    return pl.pallas_call(
        _combine_kernel,
        grid_spec=pltpu.PrefetchScalarGridSpec(
            num_scalar_prefetch=1,
            grid=(n // TM_COMBINE,),
            in_specs=[pl.BlockSpec((TM_COMBINE, D_MODEL), lambda i, d: (i, 0)),
                      pl.BlockSpec((TM_COMBINE, LANES), lambda i, d: (i, 0)),
                      pl.BlockSpec((1, D_MODEL), lambda i, d: (0, 0)),
                      pl.BlockSpec(memory_space=pl.ANY)],
            out_specs=pl.BlockSpec((TM_COMBINE, D_MODEL), lambda i, d: (i, 0)),
            scratch_shapes=[pltpu.VMEM((2, 2, TM_COMBINE * ROW_TILE, LANES), F32),
                            pltpu.SemaphoreType.DMA((2,))]),
        out_shape=jax.ShapeDtypeStruct((n, D_MODEL), F32),
        compiler_params=pltpu.CompilerParams(dimension_semantics=("arbitrary",),
                                             vmem_limit_bytes=VMEM_LIMIT),
        name="combine",
    )(dest, h, rw, final_g, ys)


def _schedule(counts):
    tiles = (counts + TM_EXPERT - 1) // TM_EXPERT
    tile_end = jnp.cumsum(tiles)
    offsets = (tile_end - tiles) * TM_EXPERT
    return tiles, offsets, tile_end[-1:]


def _layer(x, attn_g, w_in, sg_g, w_sp, b_sp, sb_g, sg_out_g, w_out, ffn_g,
           w_rg, b_rg, w_re, b_re, w_gate, w_up, w_down):
    batch, seq, _ = x.shape
    n = batch * seq
    x2 = x.reshape(n, D_MODEL)
    row = lambda v: v.reshape(1, -1)

    qkv, gu, vgn = _inproj(x2, row(attn_g), w_in.astype(BF16), row(sg_g))
    sb = _attention(qkv, batch, seq).reshape(n, SB_WIDTH)

    pad_lanes = lambda v, width: jnp.pad(v, [(0, 0)] * (v.ndim - 1) + [(0, width - v.shape[-1])])
    w_r = jnp.concatenate([pad_lanes(w_rg, ROUTER_LANE0),
                           jnp.transpose(w_re, (1, 0, 2)).reshape(D_MODEL, N_EXPERTS)], axis=1)
    w_r = pad_lanes(w_r, LANES)
    wr_hi = w_r.astype(BF16)
    wr_lo = (w_r - wr_hi.astype(F32)).astype(BF16)
    wr2 = jnp.concatenate([wr_hi, wr_lo], axis=1)
    b_r = pad_lanes(jnp.concatenate([pad_lanes(b_rg, ROUTER_LANE0), b_re.reshape(-1)]), LANES)
    bsp_full = jnp.repeat(b_sp.T, HEAD_DIM, axis=1)

    h, hn, lg = _mix(sb, gu, vgn, x2, w_sp, bsp_full, row(sb_g), row(sg_out_g),
                     w_out.astype(BF16), row(ffn_g), wr2, row(b_r))
    ri, rw, cnt = _route(lg)

    counts = cnt[:, 0].astype(jnp.int32)
    n_rows = 2 * n + N_EXPERTS * TM_EXPERT
    tiles, offsets, n_tiles = _schedule(counts)
    expert, rank = ri[0:2], ri[2:4]
    is_e = expert[None] == jnp.arange(N_EXPERTS, dtype=jnp.int32)[:, None, None]
    dest = (jnp.sum(jnp.where(is_e, offsets[:, None, None], 0), axis=0) + rank).reshape(-1)
    pad_start = offsets + counts
    pad_count = (-counts) % TM_EXPERT

    xs = _dispatch(dest, pad_start, pad_count, n_tiles, hn, n_rows)
    ys = _experts(tiles, n_tiles, xs,
                  w_gate.reshape(N_EXPERTS, D_MODEL, D_EXPERT),
                  w_up.reshape(N_EXPERTS, D_MODEL, D_EXPERT),
                  w_down.reshape(N_EXPERTS, D_EXPERT, D_MODEL))
    return dest, h, rw, ys


def kernel(x, attn_norm_g, w_in, sg_norm_g, w_spatial, b_spatial, sb_out_norm_g, sg_out_norm_g,
           w_out, ffn_norm_g, w_router_group, b_router_group, w_router_expert, b_router_expert,
           w_gate, w_up, w_down, final_norm_g):
    assert attn_norm_g.shape[0] == 1, "single-layer problem"
    batch, seq, _ = x.shape
    dest, h, rw, ys = _layer(x, attn_norm_g[0], w_in[0], sg_norm_g[0], w_spatial[0], b_spatial[0],
                             sb_out_norm_g[0], sg_out_norm_g[0], w_out[0], ffn_norm_g[0],
                             w_router_group[0], b_router_group[0], w_router_expert[0],
                             b_router_expert[0], w_gate[0], w_up[0], w_down[0])
    out = _combine(dest, h, rw, final_norm_g.reshape(1, -1), ys)
    return out.reshape(batch, seq, D_MODEL)
```

```python
import functools
import math

import jax
import jax.numpy as jnp
from jax import lax
from jax.experimental import pallas as pl
from jax.experimental.pallas import tpu as pltpu

D_MODEL = 1024
HEAD_DIM = 64
SB_WIDTH = 512
SG_WIDTH = 512
SG_HEADS = 8
D_IN = 3 * SB_WIDTH + 2 * SG_WIDTH
CHUNK = 128
N_GROUPS = 4
EXPERTS_PER_GROUP = 8
N_EXPERTS = N_GROUPS * EXPERTS_PER_GROUP
D_EXPERT = 512
EPS = 1e-6
F32_EXP_UNDERFLOW = 110.0

LANES = 128
ROW_TILE = D_MODEL // LANES
assert ROW_TILE == 8
HEAD_PAIR = 2 * HEAD_DIM
ROUTER_LANE0 = 8
ROUTER_ROWS = ROUTER_LANE0 + N_EXPERTS
assert EXPERTS_PER_GROUP == 8 and N_GROUPS <= ROUTER_LANE0

TM_PROJ = 1024
TQ_ATTN = 256
TM_MIX = 512
TM_ROUTE = 1024
TM_DISPATCH = 512
TM_EXPERT = 256
TM_COMBINE = 256
VMEM_LIMIT = 48 * 1024 * 1024

F32 = jnp.float32
BF16 = jnp.bfloat16


def _rms(x, g):
    return x * lax.rsqrt(jnp.mean(x * x, axis=-1, keepdims=True) + EPS) * g


def _gelu(x):
    c = math.sqrt(2.0 / math.pi)
    return x * (0.5 * (1.0 + jnp.tanh(c * (x + 0.044715 * (x * x * x)))))


def _softplus(z):
    return jnp.maximum(z, 0.0) + jnp.log(1.0 + jnp.exp(-jnp.abs(z)))


def _dot(a, b):
    return jnp.dot(a, b, preferred_element_type=F32)


def _rows_to_tiles(ref, x):
    m = x.shape[0]
    for k in range(ROW_TILE):
        ref[pl.ds(k, m, stride=ROW_TILE), :] = x[:, k * LANES:(k + 1) * LANES]


def _tiles_to_rows(ref, m):
    return jnp.concatenate([ref[pl.ds(k, m, stride=ROW_TILE), :] for k in range(ROW_TILE)], axis=1)


def _token_rows(ref, first_token, n_tokens):
    return ref.at[pl.ds(pl.multiple_of(first_token * ROW_TILE, ROW_TILE), n_tokens * ROW_TILE)]


def _split_bf16(x):
    hi = x.astype(BF16)
    lo = (x - hi.astype(F32)).astype(BF16)
    return hi, lo


def _inproj_kernel(x_ref, g_ref, w_ref, sgg_ref, wsp_ref, bsp_ref, sgog_ref, qkv_ref, sgn_ref,
                   gu_ref, vgn_ref, sg_ref):
    tm = TM_PROJ
    hb = _rms(x_ref[...], g_ref[...]).astype(BF16)
    q = _dot(hb, w_ref[:, 0:SB_WIDTH]) * (1.0 / math.sqrt(HEAD_DIM))
    qkv_ref[:, 0:SB_WIDTH] = q.astype(BF16)
    qkv_ref[:, SB_WIDTH:3 * SB_WIDTH] = _dot(hb, w_ref[:, SB_WIDTH:3 * SB_WIDTH]).astype(BF16)
    gu_ref[...] = _gelu(_dot(hb, w_ref[:, 3 * SB_WIDTH:3 * SB_WIDTH + SG_WIDTH]))
    gv = _gelu(_dot(hb, w_ref[:, 3 * SB_WIDTH + SG_WIDTH:D_IN]))
    vgn_ref[...] = _rms(gv, sgg_ref[...]).astype(BF16)

    lane = lax.broadcasted_iota(jnp.int32, (1, LANES), 1)
    first = lane < HEAD_DIM
    zero = jnp.zeros((), BF16)
    r_c = lax.broadcasted_iota(jnp.int32, (CHUNK, CHUNK), 0)
    c_c = lax.broadcasted_iota(jnp.int32, (CHUNK, CHUNK), 1)
    tril = r_c >= c_c
    n_pairs = SG_WIDTH // HEAD_PAIR
    w_pairs = []
    for p in range(n_pairs):
        w0 = jnp.where(tril, wsp_ref[2 * p], 0.0).astype(BF16)
        w1 = jnp.where(tril, wsp_ref[2 * p + 1], 0.0).astype(BF16)
        w_pairs.append(jnp.concatenate([w0, w1], axis=1))
    bsp = bsp_ref[...]
    for c in range(tm // CHUNK):
        rows = slice(c * CHUNK, (c + 1) * CHUNK)
        for p in range(n_pairs):
            cols = slice(p * HEAD_PAIR, (p + 1) * HEAD_PAIR)
            vg = vgn_ref[rows, cols]
            rhs = jnp.concatenate([jnp.where(first, vg, zero), jnp.where(first, zero, vg)], axis=0)
            mixed = _dot(w_pairs[p], rhs) + bsp[:, cols]
            sg_ref[rows, cols] = gu_ref[rows, cols] * mixed
    sgn_ref[...] = _rms(sg_ref[...], sgog_ref[...]).astype(BF16)


def _inproj(x2, attn_g, w_in_b, sg_g, wsp, bsp_full, sg_out_g):
    n = x2.shape[0]
    row = lambda i: (i, 0)
    const = lambda i: (0, 0)
    return pl.pallas_call(
        _inproj_kernel,
        grid=(n // TM_PROJ,),
        in_specs=[pl.BlockSpec((TM_PROJ, D_MODEL), row),
                  pl.BlockSpec((1, D_MODEL), const),
                  pl.BlockSpec((D_MODEL, D_IN), const),
                  pl.BlockSpec((1, SG_WIDTH), const),
                  pl.BlockSpec((SG_HEADS, CHUNK, CHUNK), lambda i: (0, 0, 0)),
                  pl.BlockSpec((CHUNK, SG_WIDTH), const),
                  pl.BlockSpec((1, SG_WIDTH), const)],
        out_specs=[pl.BlockSpec((TM_PROJ, 3 * SB_WIDTH), row),
                   pl.BlockSpec((TM_PROJ, SG_WIDTH), row)],
        out_shape=[jax.ShapeDtypeStruct((n, 3 * SB_WIDTH), BF16),
                   jax.ShapeDtypeStruct((n, SG_WIDTH), BF16)],
        scratch_shapes=[pltpu.VMEM((TM_PROJ, SG_WIDTH), F32),
                        pltpu.VMEM((TM_PROJ, SG_WIDTH), BF16),
                        pltpu.VMEM((TM_PROJ, SG_WIDTH), F32)],
        compiler_params=pltpu.CompilerParams(dimension_semantics=("arbitrary",),
                                             vmem_limit_bytes=VMEM_LIMIT),
        name="inproj",
    )(x2, attn_g, w_in_b, sg_g, wsp, bsp_full, sg_out_g)


def _attn_kernel(q_ref, k_ref, v_ref, o_ref, q2_ref, carry_ref):
    t = TQ_ATTN
    n_pairs = SB_WIDTH // HEAD_PAIR
    qi = pl.program_id(1)
    lane = lax.broadcasted_iota(jnp.int32, (1, HEAD_PAIR), 1)
    head_lanes = (lane < HEAD_DIM, lane >= HEAD_DIM)
    zero = jnp.zeros((), BF16)
    for p in range(n_pairs):
        qp = q_ref[0, :, p * HEAD_PAIR:(p + 1) * HEAD_PAIR]
        for h in range(2):
            q2_ref[(2 * p + h) * t:(2 * p + h + 1) * t, :] = jnp.where(head_lanes[h], qp, zero)
    r_idx = lax.broadcasted_iota(jnp.int32, (t, t), 0)
    c_idx = lax.broadcasted_iota(jnp.int32, (t, t), 1)
    suffix = (r_idx > c_idx).astype(BF16)
    suffix2 = jnp.concatenate([suffix, suffix], axis=0)
    causal = c_idx < r_idx

    o_ref[...] = jnp.zeros_like(o_ref)
    carry_ref[...] = jnp.zeros_like(carry_ref)

    def block(j, diag):
        start = pl.multiple_of(j * t, t)
        for p in range(n_pairs):
            cols = slice(p * HEAD_PAIR, (p + 1) * HEAD_PAIR)
            rows = slice(2 * p * t, (2 * p + 2) * t)
            kb = k_ref[0, pl.ds(start, t), cols]
            vb = v_ref[0, pl.ds(start, t), cols]
            z = lax.dot_general(q2_ref[rows, :], kb, (((1,), (1,)), ((), ())),
                                preferred_element_type=F32)
            sp = _softplus(z)
            if diag:
                mask2 = jnp.concatenate([causal, causal], axis=0)
                nl = jnp.where(mask2, sp, 0.0)
            else:
                nl = sp
            hi, lo = _split_bf16(nl)
            hl = jnp.concatenate([hi, lo], axis=1)
            after = jnp.concatenate([_dot(hl[0:t], suffix2), _dot(hl[t:2 * t], suffix2)], axis=0)
            carry = carry_ref[rows, :]
            a = jnp.exp(z - sp - after - carry)
            if diag:
                a = jnp.where(mask2, a, 0.0)
            a = a.astype(BF16)
            a2 = jnp.concatenate([a[0:t], a[t:2 * t]], axis=1)
            v2 = jnp.concatenate([jnp.where(head_lanes[0], vb, zero),
                                  jnp.where(head_lanes[1], vb, zero)], axis=0)
            o_ref[0, :, cols] += _dot(a2, v2)
            carry_ref[rows, :] = carry + after[:, 0:1] + nl[:, 0:1]

    def live():
        return jnp.min(carry_ref[...]) < F32_EXP_UNDERFLOW

    block(qi, True)

    def body(state):
        it, _ = state
        block(qi - 1 - it, False)
        return it + 1, live()

    lax.while_loop(lambda s: (s[0] < qi) & s[1], body, (jnp.int32(0), live()))


def _attention(qkv, batch, seq):
    qkv3 = qkv.reshape(batch, seq, 3 * SB_WIDTH)
    n_heads = SB_WIDTH // HEAD_DIM
    return pl.pallas_call(
        _attn_kernel,
        grid=(batch, seq // TQ_ATTN),
        in_specs=[pl.BlockSpec((1, TQ_ATTN, SB_WIDTH), lambda b, i: (b, i, 0)),
                  pl.BlockSpec((1, seq, SB_WIDTH), lambda b, i: (b, 0, 1)),
                  pl.BlockSpec((1, seq, SB_WIDTH), lambda b, i: (b, 0, 2))],
        out_specs=pl.BlockSpec((1, TQ_ATTN, SB_WIDTH), lambda b, i: (b, i, 0)),
        out_shape=jax.ShapeDtypeStruct((batch, seq, SB_WIDTH), F32),
        scratch_shapes=[pltpu.VMEM((n_heads * TQ_ATTN, HEAD_PAIR), BF16),
                        pltpu.VMEM((n_heads * TQ_ATTN, 1), F32)],
        compiler_params=pltpu.CompilerParams(dimension_semantics=("arbitrary",) * 2,
                                             vmem_limit_bytes=VMEM_LIMIT),
        name="sb_attention",
    )(qkv3, qkv3, qkv3)


def _mix_kernel(sb_ref, sgn_ref, x_ref, sbg_ref, wout_ref, ffng_ref, wr2_ref, br_ref,
                h_ref, hn_ref, lg_ref):
    sbn = _rms(sb_ref[...], sbg_ref[...]).astype(BF16)
    h = x_ref[...] + _dot(sbn, wout_ref[0:SB_WIDTH, :]) + _dot(sgn_ref[...], wout_ref[SB_WIDTH:, :])
    h_ref[...] = h
    hn = _rms(h, ffng_ref[...])
    _rows_to_tiles(hn_ref, hn)

    hn_hi, hn_lo = _split_bf16(hn)
    both = _dot(hn_hi, wr2_ref[...])
    logits = both[:, 0:LANES] + both[:, LANES:] + _dot(hn_lo, wr2_ref[:, 0:LANES]) + br_ref[...]
    lg_ref[...] = logits.T[0:ROUTER_ROWS, :]


def _route_kernel(lg_ref, ri_ref, rw_ref, cnt_ref, count_ref):
    tr = TM_ROUTE
    i = pl.program_id(0)

    @pl.when(i == 0)
    def _():
        count_ref[...] = jnp.zeros_like(count_ref)

    neg = jnp.float32(-jnp.inf)
    row8 = lax.broadcasted_iota(jnp.int32, (8, tr), 0)

    def top(v):
        m = jnp.max(v, axis=0, keepdims=True)
        return m, jnp.min(jnp.where(v == m, row8, 8), axis=0, keepdims=True)

    gl = jnp.where(row8 < N_GROUPS, lg_ref[0:8, :], neg)
    gmax, gidx = top(gl)
    gweight = 1.0 / jnp.sum(jnp.exp(gl - gmax), axis=0, keepdims=True)
    el = lg_ref[8:16, :]
    for g in range(1, N_GROUPS):
        el = jnp.where(gidx == g, lg_ref[8 + 8 * g:16 + 8 * g, :], el)
    m1, i1 = top(el)
    m2, i2 = top(jnp.where(row8 == i1, neg, el))
    t21 = jnp.exp(m2 - m1)
    w1 = gweight / (1.0 + t21)
    w2 = gweight * t21 / (1.0 + t21)
    e1 = gidx * EXPERTS_PER_GROUP + i1
    e2 = gidx * EXPERTS_PER_GROUP + i2

    row_e = lax.broadcasted_iota(jnp.int32, (N_EXPERTS, tr), 0)
    sel1 = row_e == e1
    sel2 = row_e == e2
    onehot = jnp.where(sel1 | sel2, 1.0, 0.0)
    r_t = lax.broadcasted_iota(jnp.int32, (tr, tr), 0)
    c_t = lax.broadcasted_iota(jnp.int32, (tr, tr), 1)
    before = (r_t < c_t).astype(BF16)
    running = count_ref[:, 0:1] + _dot(onehot.astype(BF16), before)
    rank1 = jnp.sum(jnp.where(sel1, running, 0.0), axis=0, keepdims=True)
    rank2 = jnp.sum(jnp.where(sel2, running, 0.0), axis=0, keepdims=True)
    new_count = count_ref[:, 0:1] + jnp.sum(onehot, axis=1, keepdims=True)
    count_ref[...] = jnp.broadcast_to(new_count, count_ref.shape)
    cnt_ref[...] = jnp.broadcast_to(new_count, cnt_ref.shape)

    ri_ref[...] = jnp.where(row8 == 0, e1, jnp.where(row8 == 1, e2, jnp.where(
        row8 == 2, rank1.astype(jnp.int32), jnp.where(row8 == 3, rank2.astype(jnp.int32), 0))))
    row128 = lax.broadcasted_iota(jnp.int32, (LANES, tr), 0)
    rw_ref[...] = jnp.where(row128 == 0, w1, jnp.where(row128 == 1, w2, 0.0)).T


def _route(lg):
    n = lg.shape[1]
    return pl.pallas_call(
        _route_kernel,
        grid=(n // TM_ROUTE,),
        in_specs=[pl.BlockSpec((ROUTER_ROWS, TM_ROUTE), lambda i: (0, i))],
        out_specs=[pl.BlockSpec((8, TM_ROUTE), lambda i: (0, i)),
                   pl.BlockSpec((TM_ROUTE, LANES), lambda i: (i, 0)),
                   pl.BlockSpec((N_EXPERTS, LANES), lambda i: (0, 0))],
        out_shape=[jax.ShapeDtypeStruct((8, n), jnp.int32),
                   jax.ShapeDtypeStruct((n, LANES), F32),
                   jax.ShapeDtypeStruct((N_EXPERTS, LANES), F32)],
        scratch_shapes=[pltpu.VMEM((N_EXPERTS, LANES), F32)],
        compiler_params=pltpu.CompilerParams(dimension_semantics=("arbitrary",),
                                             vmem_limit_bytes=VMEM_LIMIT),
        name="route",
    )(lg)


def _mix(sb, sgn, x2, sb_g, w_out_b, ffn_g, wr2, br):
    n = x2.shape[0]
    row = lambda i: (i, 0)
    const = lambda i: (0, 0)
    return pl.pallas_call(
        _mix_kernel,
        grid=(n // TM_MIX,),
        in_specs=[pl.BlockSpec((TM_MIX, SB_WIDTH), row),
                  pl.BlockSpec((TM_MIX, SG_WIDTH), row),
                  pl.BlockSpec((TM_MIX, D_MODEL), row),
                  pl.BlockSpec((1, SB_WIDTH), const),
                  pl.BlockSpec((D_MODEL, D_MODEL), const),
                  pl.BlockSpec((1, D_MODEL), const),
                  pl.BlockSpec((D_MODEL, 2 * LANES), const),
                  pl.BlockSpec((1, LANES), const)],
        out_specs=[pl.BlockSpec((TM_MIX, D_MODEL), row),
                   pl.BlockSpec((TM_MIX * ROW_TILE, LANES), row),
                   pl.BlockSpec((ROUTER_ROWS, TM_MIX), lambda i: (0, i))],
        out_shape=[jax.ShapeDtypeStruct((n, D_MODEL), F32),
                   jax.ShapeDtypeStruct((n * ROW_TILE, LANES), F32),
                   jax.ShapeDtypeStruct((ROUTER_ROWS, n), F32)],
        compiler_params=pltpu.CompilerParams(dimension_semantics=("arbitrary",),
                                             vmem_limit_bytes=VMEM_LIMIT),
        name="mix_router",
    )(sb, sgn, x2, sb_g, w_out_b, ffn_g, wr2, br)


_PAD_BITS = tuple(1 << b for b in reversed(range(TM_EXPERT.bit_length() - 1)))


def _dispatch_kernel(dest_ref, pad_start_ref, pad_count_ref, nt_ref, hn_ref, zeros_ref, xs_ref, sem, zsem):
    tm = TM_DISPATCH
    i = pl.program_id(0)
    n = pl.num_programs(0) * tm
    base = i * tm
    n_tiles_max = xs_ref.shape[0] // (TM_EXPERT * ROW_TILE)

    def pad_copies(do):
        for e in range(N_EXPERTS):
            start = pad_start_ref[e]
            count = pad_count_ref[e]
            for bit in _PAD_BITS:
                @pl.when((count & bit) != 0)
                def _(start=start, bit=bit):
                    do(pltpu.make_async_copy(_token_rows(zeros_ref, 0, bit),
                                             _token_rows(xs_ref, start, bit), zsem))
                start = start + (count & bit)
        for k in range(N_EXPERTS):
            tile = nt_ref[0] + k

            @pl.when(tile < n_tiles_max)
            def _(tile=tile):
                do(pltpu.make_async_copy(zeros_ref, _token_rows(xs_ref, tile * TM_EXPERT, TM_EXPERT), zsem))

    @pl.when(i == 0)
    def _():
        pad_copies(lambda cp: cp.start())

    def body(r, c):
        src = _token_rows(hn_ref, r, 1)
        for s in range(2):
            pltpu.make_async_copy(src, _token_rows(xs_ref, dest_ref[s * n + base + r], 1),
                                  sem).start(priority=s)
        return c

    lax.fori_loop(0, tm, body, 0, unroll=8)
    for _ in range(2):
        pltpu.make_async_copy(hn_ref, _token_rows(xs_ref, 0, tm), sem).wait()

    @pl.when(i == 0)
    def _():
        pad_copies(lambda cp: cp.wait())


def _dispatch(dest, pad_start, pad_count, n_tiles, hn_tiles, n_rows):
    n = hn_tiles.shape[0] // ROW_TILE
    zeros = jnp.zeros((TM_EXPERT * ROW_TILE, LANES), F32)
    return pl.pallas_call(
        _dispatch_kernel,
        grid_spec=pltpu.PrefetchScalarGridSpec(
            num_scalar_prefetch=4,
            grid=(n // TM_DISPATCH,),
            in_specs=[pl.BlockSpec((TM_DISPATCH * ROW_TILE, LANES), lambda i, *_: (i, 0)),
                      pl.BlockSpec(memory_space=pl.ANY)],
            out_specs=pl.BlockSpec(memory_space=pl.ANY),
            scratch_shapes=[pltpu.SemaphoreType.DMA, pltpu.SemaphoreType.DMA]),
        out_shape=jax.ShapeDtypeStruct((n_rows * ROW_TILE, LANES), F32),
        compiler_params=pltpu.CompilerParams(dimension_semantics=("arbitrary",),
                                             vmem_limit_bytes=VMEM_LIMIT),
        name="dispatch",
    )(dest, pad_start, pad_count, n_tiles, hn_tiles, zeros)


X_SLOTS = 3


def _expert_kernel(tiles_ref, nt_ref, xs_ref, wg_ref, wu_ref, wd_ref, y_ref,
                   x_buf, sg_buf, su_buf, sd_buf, wgb, wub, wdb, state, w_sems, x_sems):
    tm = TM_EXPERT
    t = pl.program_id(0)
    nt = nt_ref[0]

    def x_copy(tile):
        slot = lax.rem(tile, X_SLOTS)
        return pltpu.make_async_copy(_token_rows(xs_ref, tile * tm, tm), x_buf.at[slot], x_sems.at[slot])

    def weight_copies(e, slot):
        return (pltpu.make_async_copy(wg_ref.at[e], sg_buf.at[slot], w_sems.at[slot]),
                pltpu.make_async_copy(wu_ref.at[e], su_buf.at[slot], w_sems.at[slot]),
                pltpu.make_async_copy(wd_ref.at[e], sd_buf.at[slot], w_sems.at[slot]))

    def next_with_rows(e):
        return lax.while_loop(lambda k: (k < N_EXPERTS) & (tiles_ref[jnp.minimum(k, N_EXPERTS - 1)] == 0),
                              lambda k: k + 1, e + 1)

    @pl.when(t == 0)
    def _():
        first = next_with_rows(jnp.int32(-1))
        state[0] = jnp.int32(-1)
        state[1] = jnp.int32(0)
        state[2] = jnp.int32(1)
        state[3] = first
        for cp in weight_copies(first, 0):
            cp.start()
        x_copy(0).start()

        @pl.when(nt > 1)
        def _():
            x_copy(1).start()

    @pl.when(t + 2 < nt)
    def _():
        x_copy(t + 2).start()

    @pl.when(t < nt)
    def _():
        @pl.when(state[1] == 0)
        def _():
            e = state[3]
            slot = 1 - state[2]
            nxt = next_with_rows(e)
            state[0] = e
            state[1] = tiles_ref[e]
            state[2] = slot
            state[3] = nxt
            for cp in weight_copies(e, slot):
                cp.wait()

            @pl.when(nxt < N_EXPERTS)
            def _():
                for cp in weight_copies(nxt, 1 - slot):
                    cp.start()

            wgb[...] = sg_buf[slot].astype(BF16)
            wub[...] = su_buf[slot].astype(BF16)
            wdb[...] = sd_buf[slot].astype(BF16)

        state[1] = state[1] - 1
        x_copy(t).wait()
        x = _tiles_to_rows(x_buf.at[lax.rem(t, X_SLOTS)], tm).astype(BF16)
        g = _dot(x, wgb[...])
        u = _dot(x, wub[...])
        hidden = (g * jax.nn.sigmoid(g)) * u
        _rows_to_tiles(y_ref, _dot(hidden.astype(BF16), wdb[...]))

    @pl.when(t >= nt)
    def _():
        y_ref[...] = jnp.zeros_like(y_ref)


def _experts(tiles, n_tiles, xs, wg, wu, wd):
    n_rows = xs.shape[0] // ROW_TILE
    any_spec = pl.BlockSpec(memory_space=pl.ANY)
    return pl.pallas_call(
        _expert_kernel,
        grid_spec=pltpu.PrefetchScalarGridSpec(
            num_scalar_prefetch=2,
            grid=(n_rows // TM_EXPERT,),
            in_specs=[any_spec, any_spec, any_spec, any_spec],
            out_specs=pl.BlockSpec((TM_EXPERT * ROW_TILE, LANES), lambda t, *_: (t, 0)),
            scratch_shapes=[pltpu.VMEM((X_SLOTS, TM_EXPERT * ROW_TILE, LANES), F32),
                            pltpu.VMEM((2, D_MODEL, D_EXPERT), F32),
                            pltpu.VMEM((2, D_MODEL, D_EXPERT), F32),
                            pltpu.VMEM((2, D_EXPERT, D_MODEL), F32),
                            pltpu.VMEM((D_MODEL, D_EXPERT), BF16),
                            pltpu.VMEM((D_MODEL, D_EXPERT), BF16),
                            pltpu.VMEM((D_EXPERT, D_MODEL), BF16),
                            pltpu.SMEM((4,), jnp.int32),
                            pltpu.SemaphoreType.DMA((2,)),
                            pltpu.SemaphoreType.DMA((X_SLOTS,))]),
        out_shape=jax.ShapeDtypeStruct((n_rows * ROW_TILE, LANES), F32),
        compiler_params=pltpu.CompilerParams(dimension_semantics=("arbitrary",),
                                             vmem_limit_bytes=VMEM_LIMIT),
        name="expert_mlp",
    )(tiles, n_tiles, xs, wg, wu, wd)


def _combine_kernel(dest_ref, h_ref, rw_ref, fg_ref, y_ref, o_ref, buf, sems):
    tm = TM_COMBINE
    i = pl.program_id(0)
    n_steps = pl.num_programs(0)
    n = n_steps * tm
    cur = i % 2

    def fetch(step, half):
        def body(r, c):
            for s in range(2):
                pltpu.make_async_copy(_token_rows(y_ref, dest_ref[s * n + step * tm + r], 1),
                                      _token_rows(buf.at[half, s], r, 1),
                                      sems.at[half]).start(priority=s)
            return c

        lax.fori_loop(0, tm, body, 0, unroll=8)

    @pl.when(i == 0)
    def _():
        fetch(0, 0)

    @pl.when(i + 1 < n_steps)
    def _():
        fetch(i + 1, 1 - cur)

    for s in range(2):
        pltpu.make_async_copy(_token_rows(y_ref, 0, tm), buf.at[cur, s], sems.at[cur]).wait()
    rw = rw_ref[...]
    out = (h_ref[...] + rw[:, 0:1] * _tiles_to_rows(buf.at[cur, 0], tm)
           + rw[:, 1:2] * _tiles_to_rows(buf.at[cur, 1], tm))
    o_ref[...] = _rms(out, fg_ref[...])


def _combine(dest, h, rw, final_g, ys):
    n = h.shape[0]
    return pl.pallas_call(
        _combine_kernel,
        grid_spec=pltpu.PrefetchScalarGridSpec(
            num_scalar_prefetch=1,
            grid=(n // TM_COMBINE,),
            in_specs=[pl.BlockSpec((TM_COMBINE, D_MODEL), lambda i, d: (i, 0)),
                      pl.BlockSpec((TM_COMBINE, LANES), lambda i, d: (i, 0)),
                      pl.BlockSpec((1, D_MODEL), lambda i, d: (0, 0)),
                      pl.BlockSpec(memory_space=pl.ANY)],
            out_specs=pl.BlockSpec((TM_COMBINE, D_MODEL), lambda i, d: (i, 0)),
            scratch_shapes=[pltpu.VMEM((2, 2, TM_COMBINE * ROW_TILE, LANES), F32),
                            pltpu.SemaphoreType.DMA((2,))]),
        out_shape=jax.ShapeDtypeStruct((n, D_MODEL), F32),
        compiler_params=pltpu.CompilerParams(dimension_semantics=("arbitrary",),
                                             vmem_limit_bytes=VMEM_LIMIT),
        name="combine",
    )(dest, h, rw, final_g, ys)


def _schedule(counts):
    tiles = (counts + TM_EXPERT - 1) // TM_EXPERT
    tile_end = jnp.cumsum(tiles)
    offsets = (tile_end - tiles) * TM_EXPERT
    return tiles, offsets, tile_end[-1:]


def _layer(x, attn_g, w_in, sg_g, w_sp, b_sp, sb_g, sg_out_g, w_out, ffn_g,
           w_rg, b_rg, w_re, b_re, w_gate, w_up, w_down):
    batch, seq, _ = x.shape
    n = batch * seq
    x2 = x.reshape(n, D_MODEL)
    row = lambda v: v.reshape(1, -1)

    bsp_full = jnp.repeat(b_sp.T, HEAD_DIM, axis=1)
    qkv, sgn = _inproj(x2, row(attn_g), w_in.astype(BF16), row(sg_g), w_sp, bsp_full, row(sg_out_g))
    sb = _attention(qkv, batch, seq).reshape(n, SB_WIDTH)

    pad_lanes = lambda v, width: jnp.pad(v, [(0, 0)] * (v.ndim - 1) + [(0, width - v.shape[-1])])
    w_r = jnp.concatenate([pad_lanes(w_rg, ROUTER_LANE0),
                           jnp.transpose(w_re, (1, 0, 2)).reshape(D_MODEL, N_EXPERTS)], axis=1)
    w_r = pad_lanes(w_r, LANES)
    wr_hi = w_r.astype(BF16)
    wr_lo = (w_r - wr_hi.astype(F32)).astype(BF16)
    wr2 = jnp.concatenate([wr_hi, wr_lo], axis=1)
    b_r = pad_lanes(jnp.concatenate([pad_lanes(b_rg, ROUTER_LANE0), b_re.reshape(-1)]), LANES)

    h, hn, lg = _mix(sb, sgn, x2, row(sb_g), w_out.astype(BF16), row(ffn_g), wr2, row(b_r))
    ri, rw, cnt = _route(lg)

    counts = cnt[:, 0].astype(jnp.int32)
    n_rows = 2 * n + N_EXPERTS * TM_EXPERT
    tiles, offsets, n_tiles = _schedule(counts)
    expert, rank = ri[0:2], ri[2:4]
    is_e = expert[None] == jnp.arange(N_EXPERTS, dtype=jnp.int32)[:, None, None]
    dest = (jnp.sum(jnp.where(is_e, offsets[:, None, None], 0), axis=0) + rank).reshape(-1)
    pad_start = offsets + counts
    pad_count = (-counts) % TM_EXPERT

    xs = _dispatch(dest, pad_start, pad_count, n_tiles, hn, n_rows)
    ys = _experts(tiles, n_tiles, xs,
                  w_gate.reshape(N_EXPERTS, D_MODEL, D_EXPERT),
                  w_up.reshape(N_EXPERTS, D_MODEL, D_EXPERT),
                  w_down.reshape(N_EXPERTS, D_EXPERT, D_MODEL))
    return dest, h, rw, ys


def kernel(x, attn_norm_g, w_in, sg_norm_g, w_spatial, b_spatial, sb_out_norm_g, sg_out_norm_g,
           w_out, ffn_norm_g, w_router_group, b_router_group, w_router_expert, b_router_expert,
           w_gate, w_up, w_down, final_norm_g):
    assert attn_norm_g.shape[0] == 1, "single-layer problem"
    batch, seq, _ = x.shape
    dest, h, rw, ys = _layer(x, attn_norm_g[0], w_in[0], sg_norm_g[0], w_spatial[0], b_spatial[0],
                             sb_out_norm_g[0], sg_out_norm_g[0], w_out[0], ffn_norm_g[0],
                             w_router_group[0], b_router_group[0], w_router_expert[0],
                             b_router_expert[0], w_gate[0], w_up[0], w_down[0])
    out = _combine(dest, h, rw, final_norm_g.reshape(1, -1), ys)
    return out.reshape(batch, seq, D_MODEL)
```

```python
import functools
import math

import jax
import jax.numpy as jnp
from jax import lax
from jax.experimental import pallas as pl
from jax.experimental.pallas import tpu as pltpu

D_MODEL = 1024
HEAD_DIM = 64
SB_WIDTH = 512
SG_WIDTH = 512
SG_HEADS = 8
D_IN = 3 * SB_WIDTH + 2 * SG_WIDTH
CHUNK = 128
N_GROUPS = 4
EXPERTS_PER_GROUP = 8
N_EXPERTS = N_GROUPS * EXPERTS_PER_GROUP
D_EXPERT = 512
EPS = 1e-6
F32_EXP_UNDERFLOW = 110.0

LANES = 128
ROW_TILE = D_MODEL // LANES
assert ROW_TILE == 8
HEAD_PAIR = 2 * HEAD_DIM
ROUTER_LANE0 = 8
ROUTER_ROWS = ROUTER_LANE0 + N_EXPERTS
assert EXPERTS_PER_GROUP == 8 and N_GROUPS <= ROUTER_LANE0

TM_PROJ = 1024
TQ_ATTN = 256
TM_MIX = 1024
TM_ROUTE = 1024
TM_DISPATCH = 512
TM_EXPERT = 256
TM_COMBINE = 256
VMEM_LIMIT = 48 * 1024 * 1024

F32 = jnp.float32
BF16 = jnp.bfloat16


def _rms(x, g):
    return x * lax.rsqrt(jnp.mean(x * x, axis=-1, keepdims=True) + EPS) * g


def _gelu(x):
    c = math.sqrt(2.0 / math.pi)
    return x * (0.5 * (1.0 + jnp.tanh(c * (x + 0.044715 * (x * x * x)))))


def _softplus(z):
    return jnp.maximum(z, 0.0) + jnp.log(1.0 + jnp.exp(-jnp.abs(z)))


def _dot(a, b):
    return jnp.dot(a, b, preferred_element_type=F32)


def _rows_to_tiles(ref, x):
    m = x.shape[0]
    for k in range(ROW_TILE):
        ref[pl.ds(k, m, stride=ROW_TILE), :] = x[:, k * LANES:(k + 1) * LANES]


def _tiles_to_rows(ref, m):
    return jnp.concatenate([ref[pl.ds(k, m, stride=ROW_TILE), :] for k in range(ROW_TILE)], axis=1)


def _token_rows(ref, first_token, n_tokens):
    return ref.at[pl.ds(pl.multiple_of(first_token * ROW_TILE, ROW_TILE), n_tokens * ROW_TILE)]


def _split_bf16(x):
    hi = x.astype(BF16)
    lo = (x - hi.astype(F32)).astype(BF16)
    return hi, lo


def _inproj_kernel(x_ref, g_ref, w_ref, sgg_ref, wsp_ref, bsp_ref, sgog_ref, qkv_ref, sgn_ref,
                   gu_ref, vgn_ref, sg_ref):
    tm = TM_PROJ
    hb = _rms(x_ref[...], g_ref[...]).astype(BF16)
    q = _dot(hb, w_ref[:, 0:SB_WIDTH]) * (1.0 / math.sqrt(HEAD_DIM))
    qkv_ref[:, 0:SB_WIDTH] = q.astype(BF16)
    qkv_ref[:, SB_WIDTH:3 * SB_WIDTH] = _dot(hb, w_ref[:, SB_WIDTH:3 * SB_WIDTH]).astype(BF16)
    gu_ref[...] = _gelu(_dot(hb, w_ref[:, 3 * SB_WIDTH:3 * SB_WIDTH + SG_WIDTH]))
    gv = _gelu(_dot(hb, w_ref[:, 3 * SB_WIDTH + SG_WIDTH:D_IN]))
    vgn_ref[...] = _rms(gv, sgg_ref[...]).astype(BF16)

    lane = lax.broadcasted_iota(jnp.int32, (1, LANES), 1)
    first = lane < HEAD_DIM
    zero = jnp.zeros((), BF16)
    r_c = lax.broadcasted_iota(jnp.int32, (CHUNK, CHUNK), 0)
    c_c = lax.broadcasted_iota(jnp.int32, (CHUNK, CHUNK), 1)
    tril = r_c >= c_c
    n_pairs = SG_WIDTH // HEAD_PAIR
    w_pairs = []
    for p in range(n_pairs):
        w0 = jnp.where(tril, wsp_ref[2 * p], 0.0).astype(BF16)
        w1 = jnp.where(tril, wsp_ref[2 * p + 1], 0.0).astype(BF16)
        w_pairs.append(jnp.concatenate([w0, w1], axis=1))
    bsp = bsp_ref[...]
    for c in range(tm // CHUNK):
        rows = slice(c * CHUNK, (c + 1) * CHUNK)
        for p in range(n_pairs):
            cols = slice(p * HEAD_PAIR, (p + 1) * HEAD_PAIR)
            vg = vgn_ref[rows, cols]
            rhs = jnp.concatenate([jnp.where(first, vg, zero), jnp.where(first, zero, vg)], axis=0)
            mixed = _dot(w_pairs[p], rhs) + bsp[:, cols]
            sg_ref[rows, cols] = gu_ref[rows, cols] * mixed
    sgn_ref[...] = _rms(sg_ref[...], sgog_ref[...]).astype(BF16)


def _inproj(x2, attn_g, w_in_b, sg_g, wsp, bsp_full, sg_out_g):
    n = x2.shape[0]
    row = lambda i: (i, 0)
    const = lambda i: (0, 0)
    return pl.pallas_call(
        _inproj_kernel,
        grid=(n // TM_PROJ,),
        in_specs=[pl.BlockSpec((TM_PROJ, D_MODEL), row),
                  pl.BlockSpec((1, D_MODEL), const),
                  pl.BlockSpec((D_MODEL, D_IN), const),
                  pl.BlockSpec((1, SG_WIDTH), const),
                  pl.BlockSpec((SG_HEADS, CHUNK, CHUNK), lambda i: (0, 0, 0)),
                  pl.BlockSpec((CHUNK, SG_WIDTH), const),
                  pl.BlockSpec((1, SG_WIDTH), const)],
        out_specs=[pl.BlockSpec((TM_PROJ, 3 * SB_WIDTH), row),
                   pl.BlockSpec((TM_PROJ, SG_WIDTH), row)],
        out_shape=[jax.ShapeDtypeStruct((n, 3 * SB_WIDTH), BF16),
                   jax.ShapeDtypeStruct((n, SG_WIDTH), BF16)],
        scratch_shapes=[pltpu.VMEM((TM_PROJ, SG_WIDTH), F32),
                        pltpu.VMEM((TM_PROJ, SG_WIDTH), BF16),
                        pltpu.VMEM((TM_PROJ, SG_WIDTH), F32)],
        compiler_params=pltpu.CompilerParams(dimension_semantics=("arbitrary",),
                                             vmem_limit_bytes=VMEM_LIMIT),
        name="inproj",
    )(x2, attn_g, w_in_b, sg_g, wsp, bsp_full, sg_out_g)


def _attn_kernel(q_ref, k_ref, v_ref, o_ref, q2_ref, carry_ref):
    t = TQ_ATTN
    n_pairs = SB_WIDTH // HEAD_PAIR
    qi = pl.program_id(1)
    lane = lax.broadcasted_iota(jnp.int32, (1, HEAD_PAIR), 1)
    head_lanes = (lane < HEAD_DIM, lane >= HEAD_DIM)
    zero = jnp.zeros((), BF16)
    for p in range(n_pairs):
        qp = q_ref[0, :, p * HEAD_PAIR:(p + 1) * HEAD_PAIR]
        for h in range(2):
            q2_ref[(2 * p + h) * t:(2 * p + h + 1) * t, :] = jnp.where(head_lanes[h], qp, zero)
    r_idx = lax.broadcasted_iota(jnp.int32, (t, t), 0)
    c_idx = lax.broadcasted_iota(jnp.int32, (t, t), 1)
    suffix = (r_idx > c_idx).astype(BF16)
    suffix2 = jnp.concatenate([suffix, suffix], axis=0)
    causal = c_idx < r_idx

    o_ref[...] = jnp.zeros_like(o_ref)
    carry_ref[...] = jnp.zeros_like(carry_ref)

    def block(j, diag):
        start = pl.multiple_of(j * t, t)
        for p in range(n_pairs):
            cols = slice(p * HEAD_PAIR, (p + 1) * HEAD_PAIR)
            rows = slice(2 * p * t, (2 * p + 2) * t)
            kb = k_ref[0, pl.ds(start, t), cols]
            vb = v_ref[0, pl.ds(start, t), cols]
            z = lax.dot_general(q2_ref[rows, :], kb, (((1,), (1,)), ((), ())),
                                preferred_element_type=F32)
            sp = _softplus(z)
            if diag:
                mask2 = jnp.concatenate([causal, causal], axis=0)
                nl = jnp.where(mask2, sp, 0.0)
            else:
                nl = sp
            hi, lo = _split_bf16(nl)
            hl = jnp.concatenate([hi, lo], axis=1)
            after = jnp.concatenate([_dot(hl[0:t], suffix2), _dot(hl[t:2 * t], suffix2)], axis=0)
            carry = carry_ref[rows, :]
            a = jnp.exp(z - sp - after - carry)
            if diag:
                a = jnp.where(mask2, a, 0.0)
            a = a.astype(BF16)
            a2 = jnp.concatenate([a[0:t], a[t:2 * t]], axis=1)
            v2 = jnp.concatenate([jnp.where(head_lanes[0], vb, zero),
                                  jnp.where(head_lanes[1], vb, zero)], axis=0)
            o_ref[0, :, cols] += _dot(a2, v2)
            carry_ref[rows, :] = carry + after[:, 0:1] + nl[:, 0:1]

    def live():
        return jnp.min(carry_ref[...]) < F32_EXP_UNDERFLOW

    block(qi, True)

    def body(state):
        it, _ = state
        block(qi - 1 - it, False)
        return it + 1, live()

    lax.while_loop(lambda s: (s[0] < qi) & s[1], body, (jnp.int32(0), live()))


def _attention(qkv, batch, seq):
    qkv3 = qkv.reshape(batch, seq, 3 * SB_WIDTH)
    n_heads = SB_WIDTH // HEAD_DIM
    return pl.pallas_call(
        _attn_kernel,
        grid=(batch, seq // TQ_ATTN),
        in_specs=[pl.BlockSpec((1, TQ_ATTN, SB_WIDTH), lambda b, i: (b, i, 0)),
                  pl.BlockSpec((1, seq, SB_WIDTH), lambda b, i: (b, 0, 1)),
                  pl.BlockSpec((1, seq, SB_WIDTH), lambda b, i: (b, 0, 2))],
        out_specs=pl.BlockSpec((1, TQ_ATTN, SB_WIDTH), lambda b, i: (b, i, 0)),
        out_shape=jax.ShapeDtypeStruct((batch, seq, SB_WIDTH), F32),
        scratch_shapes=[pltpu.VMEM((n_heads * TQ_ATTN, HEAD_PAIR), BF16),
                        pltpu.VMEM((n_heads * TQ_ATTN, 1), F32)],
        compiler_params=pltpu.CompilerParams(dimension_semantics=("arbitrary",) * 2,
                                             vmem_limit_bytes=VMEM_LIMIT),
        name="sb_attention",
    )(qkv3, qkv3, qkv3)


def _mix_kernel(sb_ref, sgn_ref, x_ref, sbg_ref, wout_ref, ffng_ref, wr2_ref, br_ref,
                h_ref, hn_ref, lg_ref):
    sbn = _rms(sb_ref[...], sbg_ref[...]).astype(BF16)
    h = x_ref[...] + _dot(sbn, wout_ref[0:SB_WIDTH, :]) + _dot(sgn_ref[...], wout_ref[SB_WIDTH:, :])
    h_ref[...] = h
    hn = _rms(h, ffng_ref[...])
    _rows_to_tiles(hn_ref, hn)

    hn_hi, hn_lo = _split_bf16(hn)
    both = _dot(hn_hi, wr2_ref[...])
    logits = both[:, 0:LANES] + both[:, LANES:] + _dot(hn_lo, wr2_ref[:, 0:LANES]) + br_ref[...]
    lg_ref[...] = logits.T[0:ROUTER_ROWS, :]


def _route_kernel(lg_ref, ri_ref, rw_ref, cnt_ref, count_ref):
    tr = TM_ROUTE
    i = pl.program_id(0)

    @pl.when(i == 0)
    def _():
        count_ref[...] = jnp.zeros_like(count_ref)

    neg = jnp.float32(-jnp.inf)
    row8 = lax.broadcasted_iota(jnp.int32, (8, tr), 0)

    def top(v):
        m = jnp.max(v, axis=0, keepdims=True)
        return m, jnp.min(jnp.where(v == m, row8, 8), axis=0, keepdims=True)

    gl = jnp.where(row8 < N_GROUPS, lg_ref[0:8, :], neg)
    gmax, gidx = top(gl)
    gweight = 1.0 / jnp.sum(jnp.exp(gl - gmax), axis=0, keepdims=True)
    el = lg_ref[8:16, :]
    for g in range(1, N_GROUPS):
        el = jnp.where(gidx == g, lg_ref[8 + 8 * g:16 + 8 * g, :], el)
    m1, i1 = top(el)
    m2, i2 = top(jnp.where(row8 == i1, neg, el))
    t21 = jnp.exp(m2 - m1)
    w1 = gweight / (1.0 + t21)
    w2 = gweight * t21 / (1.0 + t21)
    e1 = gidx * EXPERTS_PER_GROUP + i1
    e2 = gidx * EXPERTS_PER_GROUP + i2

    row_e = lax.broadcasted_iota(jnp.int32, (N_EXPERTS, tr), 0)
    sel1 = row_e == e1
    sel2 = row_e == e2
    onehot = jnp.where(sel1 | sel2, 1.0, 0.0)
    r_t = lax.broadcasted_iota(jnp.int32, (tr, tr), 0)
    c_t = lax.broadcasted_iota(jnp.int32, (tr, tr), 1)
    before = (r_t < c_t).astype(BF16)
    running = count_ref[:, 0:1] + _dot(onehot.astype(BF16), before)
    rank1 = jnp.sum(jnp.where(sel1, running, 0.0), axis=0, keepdims=True)
    rank2 = jnp.sum(jnp.where(sel2, running, 0.0), axis=0, keepdims=True)
    new_count = count_ref[:, 0:1] + jnp.sum(onehot, axis=1, keepdims=True)
    count_ref[...] = jnp.broadcast_to(new_count, count_ref.shape)
    cnt_ref[...] = jnp.broadcast_to(new_count, cnt_ref.shape)

    ri_ref[...] = jnp.where(row8 == 0, e1, jnp.where(row8 == 1, e2, jnp.where(
        row8 == 2, rank1.astype(jnp.int32), jnp.where(row8 == 3, rank2.astype(jnp.int32), 0))))
    row128 = lax.broadcasted_iota(jnp.int32, (LANES, tr), 0)
    rw_ref[...] = jnp.where(row128 == 0, w1, jnp.where(row128 == 1, w2, 0.0)).T


def _route(lg):
    n = lg.shape[1]
    return pl.pallas_call(
        _route_kernel,
        grid=(n // TM_ROUTE,),
        in_specs=[pl.BlockSpec((ROUTER_ROWS, TM_ROUTE), lambda i: (0, i))],
        out_specs=[pl.BlockSpec((8, TM_ROUTE), lambda i: (0, i)),
                   pl.BlockSpec((TM_ROUTE, LANES), lambda i: (i, 0)),
                   pl.BlockSpec((N_EXPERTS, LANES), lambda i: (0, 0))],
        out_shape=[jax.ShapeDtypeStruct((8, n), jnp.int32),
                   jax.ShapeDtypeStruct((n, LANES), F32),
                   jax.ShapeDtypeStruct((N_EXPERTS, LANES), F32)],
        scratch_shapes=[pltpu.VMEM((N_EXPERTS, LANES), F32)],
        compiler_params=pltpu.CompilerParams(dimension_semantics=("arbitrary",),
                                             vmem_limit_bytes=VMEM_LIMIT),
        name="route",
    )(lg)


def _mix(sb, sgn, x2, sb_g, w_out_b, ffn_g, wr2, br):
    n = x2.shape[0]
    row = lambda i: (i, 0)
    const = lambda i: (0, 0)
    return pl.pallas_call(
        _mix_kernel,
        grid=(n // TM_MIX,),
        in_specs=[pl.BlockSpec((TM_MIX, SB_WIDTH), row),
                  pl.BlockSpec((TM_MIX, SG_WIDTH), row),
                  pl.BlockSpec((TM_MIX, D_MODEL), row),
                  pl.BlockSpec((1, SB_WIDTH), const),
                  pl.BlockSpec((D_MODEL, D_MODEL), const),
                  pl.BlockSpec((1, D_MODEL), const),
                  pl.BlockSpec((D_MODEL, 2 * LANES), const),
                  pl.BlockSpec((1, LANES), const)],
        out_specs=[pl.BlockSpec((TM_MIX, D_MODEL), row),
                   pl.BlockSpec((TM_MIX * ROW_TILE, LANES), row),
                   pl.BlockSpec((ROUTER_ROWS, TM_MIX), lambda i: (0, i))],
        out_shape=[jax.ShapeDtypeStruct((n, D_MODEL), F32),
                   jax.ShapeDtypeStruct((n * ROW_TILE, LANES), F32),
                   jax.ShapeDtypeStruct((ROUTER_ROWS, n), F32)],
        compiler_params=pltpu.CompilerParams(dimension_semantics=("arbitrary",),
                                             vmem_limit_bytes=VMEM_LIMIT),
        name="mix_router",
    )(sb, sgn, x2, sb_g, w_out_b, ffn_g, wr2, br)


_PAD_BITS = tuple(1 << b for b in reversed(range(TM_EXPERT.bit_length() - 1)))


def _dispatch_kernel(dest_ref, pad_start_ref, pad_count_ref, nt_ref, hn_ref, zeros_ref, xs_ref, sem, zsem):
    tm = TM_DISPATCH
    i = pl.program_id(0)
    n = pl.num_programs(0) * tm
    base = i * tm
    n_tiles_max = xs_ref.shape[0] // (TM_EXPERT * ROW_TILE)

    def pad_copies(do):
        for e in range(N_EXPERTS):
            start = pad_start_ref[e]
            count = pad_count_ref[e]
            for bit in _PAD_BITS:
                @pl.when((count & bit) != 0)
                def _(start=start, bit=bit):
                    do(pltpu.make_async_copy(_token_rows(zeros_ref, 0, bit),
                                             _token_rows(xs_ref, start, bit), zsem))
                start = start + (count & bit)
        for k in range(N_EXPERTS):
            tile = nt_ref[0] + k

            @pl.when(tile < n_tiles_max)
            def _(tile=tile):
                do(pltpu.make_async_copy(zeros_ref, _token_rows(xs_ref, tile * TM_EXPERT, TM_EXPERT), zsem))

    @pl.when(i == 0)
    def _():
        pad_copies(lambda cp: cp.start())

    def body(r, c):
        src = _token_rows(hn_ref, r, 1)
        for s in range(2):
            pltpu.make_async_copy(src, _token_rows(xs_ref, dest_ref[s * n + base + r], 1),
                                  sem).start(priority=s)
        return c

    lax.fori_loop(0, tm, body, 0, unroll=8)
    for _ in range(2):
        pltpu.make_async_copy(hn_ref, _token_rows(xs_ref, 0, tm), sem).wait()

    @pl.when(i == 0)
    def _():
        pad_copies(lambda cp: cp.wait())


def _dispatch(dest, pad_start, pad_count, n_tiles, hn_tiles, n_rows):
    n = hn_tiles.shape[0] // ROW_TILE
    zeros = jnp.zeros((TM_EXPERT * ROW_TILE, LANES), F32)
    return pl.pallas_call(
        _dispatch_kernel,
        grid_spec=pltpu.PrefetchScalarGridSpec(
            num_scalar_prefetch=4,
            grid=(n // TM_DISPATCH,),
            in_specs=[pl.BlockSpec((TM_DISPATCH * ROW_TILE, LANES), lambda i, *_: (i, 0)),
                      pl.BlockSpec(memory_space=pl.ANY)],
            out_specs=pl.BlockSpec(memory_space=pl.ANY),
            scratch_shapes=[pltpu.SemaphoreType.DMA, pltpu.SemaphoreType.DMA]),
        out_shape=jax.ShapeDtypeStruct((n_rows * ROW_TILE, LANES), F32),
        compiler_params=pltpu.CompilerParams(dimension_semantics=("arbitrary",),
                                             vmem_limit_bytes=VMEM_LIMIT),
        name="dispatch",
    )(dest, pad_start, pad_count, n_tiles, hn_tiles, zeros)


X_SLOTS = 3


def _expert_kernel(tiles_ref, nt_ref, xs_ref, wg_ref, wu_ref, wd_ref, y_ref,
                   x_buf, sg_buf, su_buf, sd_buf, wgb, wub, wdb, state, w_sems, x_sems):
    tm = TM_EXPERT
    t = pl.program_id(0)
    nt = nt_ref[0]

    def x_copy(tile):
        slot = lax.rem(tile, X_SLOTS)
        return pltpu.make_async_copy(_token_rows(xs_ref, tile * tm, tm), x_buf.at[slot], x_sems.at[slot])

    def weight_copies(e, slot):
        return (pltpu.make_async_copy(wg_ref.at[e], sg_buf.at[slot], w_sems.at[slot]),
                pltpu.make_async_copy(wu_ref.at[e], su_buf.at[slot], w_sems.at[slot]),
                pltpu.make_async_copy(wd_ref.at[e], sd_buf.at[slot], w_sems.at[slot]))

    def next_with_rows(e):
        return lax.while_loop(lambda k: (k < N_EXPERTS) & (tiles_ref[jnp.minimum(k, N_EXPERTS - 1)] == 0),
                              lambda k: k + 1, e + 1)

    @pl.when(t == 0)
    def _():
        first = next_with_rows(jnp.int32(-1))
        state[0] = jnp.int32(-1)
        state[1] = jnp.int32(0)
        state[2] = jnp.int32(1)
        state[3] = first
        for cp in weight_copies(first, 0):
            cp.start()
        x_copy(0).start()

        @pl.when(nt > 1)
        def _():
            x_copy(1).start()

    @pl.when(t + 2 < nt)
    def _():
        x_copy(t + 2).start()

    @pl.when(t < nt)
    def _():
        @pl.when(state[1] == 0)
        def _():
            e = state[3]
            slot = 1 - state[2]
            nxt = next_with_rows(e)
            state[0] = e
            state[1] = tiles_ref[e]
            state[2] = slot
            state[3] = nxt
            for cp in weight_copies(e, slot):
                cp.wait()

            @pl.when(nxt < N_EXPERTS)
            def _():
                for cp in weight_copies(nxt, 1 - slot):
                    cp.start()

            wgb[...] = sg_buf[slot].astype(BF16)
            wub[...] = su_buf[slot].astype(BF16)
            wdb[...] = sd_buf[slot].astype(BF16)

        state[1] = state[1] - 1
        x_copy(t).wait()
        x = _tiles_to_rows(x_buf.at[lax.rem(t, X_SLOTS)], tm).astype(BF16)
        g = _dot(x, wgb[...])
        u = _dot(x, wub[...])
        hidden = (g * jax.nn.sigmoid(g)) * u
        _rows_to_tiles(y_ref, _dot(hidden.astype(BF16), wdb[...]))

    @pl.when(t >= nt)
    def _():
        y_ref[...] = jnp.zeros_like(y_ref)


def _experts(tiles, n_tiles, xs, wg, wu, wd):
    n_rows = xs.shape[0] // ROW_TILE
    any_spec = pl.BlockSpec(memory_space=pl.ANY)
    return pl.pallas_call(
        _expert_kernel,
        grid_spec=pltpu.PrefetchScalarGridSpec(
            num_scalar_prefetch=2,
            grid=(n_rows // TM_EXPERT,),
            in_specs=[any_spec, any_spec, any_spec, any_spec],
            out_specs=pl.BlockSpec((TM_EXPERT * ROW_TILE, LANES), lambda t, *_: (t, 0)),
            scratch_shapes=[pltpu.VMEM((X_SLOTS, TM_EXPERT * ROW_TILE, LANES), F32),
                            pltpu.VMEM((2, D_MODEL, D_EXPERT), F32),
                            pltpu.VMEM((2, D_MODEL, D_EXPERT), F32),
                            pltpu.VMEM((2, D_EXPERT, D_MODEL), F32),
                            pltpu.VMEM((D_MODEL, D_EXPERT), BF16),
                            pltpu.VMEM((D_MODEL, D_EXPERT), BF16),
                            pltpu.VMEM((D_EXPERT, D_MODEL), BF16),
                            pltpu.SMEM((4,), jnp.int32),
                            pltpu.SemaphoreType.DMA((2,)),
                            pltpu.SemaphoreType.DMA((X_SLOTS,))]),
        out_shape=jax.ShapeDtypeStruct((n_rows * ROW_TILE, LANES), F32),
        compiler_params=pltpu.CompilerParams(dimension_semantics=("arbitrary",),
                                             vmem_limit_bytes=VMEM_LIMIT),
        name="expert_mlp",
    )(tiles, n_tiles, xs, wg, wu, wd)


def _combine_kernel(dest_ref, h_ref, rw_ref, fg_ref, y_ref, o_ref, buf, sems):
    tm = TM_COMBINE
    i = pl.program_id(0)
    n_steps = pl.num_programs(0)
    n = n_steps * tm
    cur = i % 2

    def fetch(step, half):
        def body(r, c):
            for s in range(2):
                pltpu.make_async_copy(_token_rows(y_ref, dest_ref[s * n + step * tm + r], 1),
                                      _token_rows(buf.at[half, s], r, 1),
                                      sems.at[half]).start(priority=s)
            return c

        lax.fori_loop(0, tm, body, 0, unroll=8)

    @pl.when(i == 0)
    def _():
        fetch(0, 0)

    @pl.when(i + 1 < n_steps)
    def _():
        fetch(i + 1, 1 - cur)

    for s in range(2):
        pltpu.make_async_copy(_token_rows(y_ref, 0, tm), buf.at[cur, s], sems.at[cur]).wait()
    rw = rw_ref[...]
    out = (h_ref[...] + rw[:, 0:1] * _tiles_to_rows(buf.at[cur, 0], tm)
           + rw[:, 1:2] * _tiles_to_rows(buf.at[cur, 1], tm))
    o_ref[...] = _rms(out, fg_ref[...])


def _combine(dest, h, rw, final_g, ys):
    n = h.shape[0]
    return pl.pallas_call(
        _combine_kernel,
        grid_spec=pltpu.PrefetchScalarGridSpec(
            num_scalar_prefetch=1,
            grid=(n // TM_COMBINE,),
            in_specs=[pl.BlockSpec((TM_COMBINE, D_MODEL), lambda i, d: (i, 0)),
                      pl.BlockSpec((TM_COMBINE, LANES), lambda i, d: (i, 0)),
                      pl.BlockSpec((1, D_MODEL), lambda i, d: (0, 0)),
                      pl.BlockSpec(memory_space=pl.ANY)],
            out_specs=pl.BlockSpec((TM_COMBINE, D_MODEL), lambda i, d: (i, 0)),
            scratch_shapes=[pltpu.VMEM((2, 2, TM_COMBINE * ROW_TILE, LANES), F32),
                            pltpu.SemaphoreType.DMA((2,))]),
        out_shape=jax.ShapeDtypeStruct((n, D_MODEL), F32),
        compiler_params=pltpu.CompilerParams(dimension_semantics=("arbitrary",),
                                             vmem_limit_bytes=VMEM_LIMIT),
        name="combine",
    )(dest, h, rw, final_g, ys)


def _schedule(counts):
    tiles = (counts + TM_EXPERT - 1) // TM_EXPERT
    tile_end = jnp.cumsum(tiles)
    offsets = (tile_end - tiles) * TM_EXPERT
    return tiles, offsets, tile_end[-1:]


def _layer(x, attn_g, w_in, sg_g, w_sp, b_sp, sb_g, sg_out_g, w_out, ffn_g,
           w_rg, b_rg, w_re, b_re, w_gate, w_up, w_down):
    batch, seq, _ = x.shape
    n = batch * seq
    x2 = x.reshape(n, D_MODEL)
    row = lambda v: v.reshape(1, -1)

    bsp_full = jnp.repeat(b_sp.T, HEAD_DIM, axis=1)
    qkv, sgn = _inproj(x2, row(attn_g), w_in.astype(BF16), row(sg_g), w_sp, bsp_full, row(sg_out_g))
    sb = _attention(qkv, batch, seq).reshape(n, SB_WIDTH)

    pad_lanes = lambda v, width: jnp.pad(v, [(0, 0)] * (v.ndim - 1) + [(0, width - v.shape[-1])])
    w_r = jnp.concatenate([pad_lanes(w_rg, ROUTER_LANE0),
                           jnp.transpose(w_re, (1, 0, 2)).reshape(D_MODEL, N_EXPERTS)], axis=1)
    w_r = pad_lanes(w_r, LANES)
    wr_hi = w_r.astype(BF16)
    wr_lo = (w_r - wr_hi.astype(F32)).astype(BF16)
    wr2 = jnp.concatenate([wr_hi, wr_lo], axis=1)
    b_r = pad_lanes(jnp.concatenate([pad_lanes(b_rg, ROUTER_LANE0), b_re.reshape(-1)]), LANES)

    h, hn, lg = _mix(sb, sgn, x2, row(sb_g), w_out.astype(BF16), row(ffn_g), wr2, row(b_r))
    ri, rw, cnt = _route(lg)

    counts = cnt[:, 0].astype(jnp.int32)
    n_rows = 2 * n + N_EXPERTS * TM_EXPERT
    tiles, offsets, n_tiles = _schedule(counts)
    expert, rank = ri[0:2], ri[2:4]
    is_e = expert[None] == jnp.arange(N_EXPERTS, dtype=jnp.int32)[:, None, None]
    dest = (jnp.sum(jnp.where(is_e, offsets[:, None, None], 0), axis=0) + rank).reshape(-1)
    pad_start = offsets + counts
    pad_count = (-counts) % TM_EXPERT

    xs = _dispatch(dest, pad_start, pad_count, n_tiles, hn, n_rows)
    ys = _experts(tiles, n_tiles, xs,
                  w_gate.reshape(N_EXPERTS, D_MODEL, D_EXPERT),
                  w_up.reshape(N_EXPERTS, D_MODEL, D_EXPERT),
                  w_down.reshape(N_EXPERTS, D_EXPERT, D_MODEL))
    return dest, h, rw, ys


def kernel(x, attn_norm_g, w_in, sg_norm_g, w_spatial, b_spatial, sb_out_norm_g, sg_out_norm_g,
           w_out, ffn_norm_g, w_router_group, b_router_group, w_router_expert, b_router_expert,
           w_gate, w_up, w_down, final_norm_g):
    assert attn_norm_g.shape[0] == 1, "single-layer problem"
    batch, seq, _ = x.shape
    dest, h, rw, ys = _layer(x, attn_norm_g[0], w_in[0], sg_norm_g[0], w_spatial[0], b_spatial[0],
                             sb_out_norm_g[0], sg_out_norm_g[0], w_out[0], ffn_norm_g[0],
                             w_router_group[0], b_router_group[0], w_router_expert[0],
                             b_router_expert[0], w_gate[0], w_up[0], w_down[0])
    out = _combine(dest, h, rw, final_norm_g.reshape(1, -1), ys)
    return out.reshape(batch, seq, D_MODEL)
```

```python
import functools
import math

import jax
import jax.numpy as jnp
from jax import lax
from jax.experimental import pallas as pl
from jax.experimental.pallas import tpu as pltpu

D_MODEL = 1024
HEAD_DIM = 64
SB_WIDTH = 512
SG_WIDTH = 512
SG_HEADS = 8
D_IN = 3 * SB_WIDTH + 2 * SG_WIDTH
CHUNK = 128
N_GROUPS = 4
EXPERTS_PER_GROUP = 8
N_EXPERTS = N_GROUPS * EXPERTS_PER_GROUP
D_EXPERT = 512
EPS = 1e-6
F32_EXP_UNDERFLOW = 110.0

LANES = 128
ROW_TILE = D_MODEL // LANES
assert ROW_TILE == 8
HEAD_PAIR = 2 * HEAD_DIM
ROUTER_LANE0 = 8
ROUTER_ROWS = ROUTER_LANE0 + N_EXPERTS
assert EXPERTS_PER_GROUP == 8 and N_GROUPS <= ROUTER_LANE0

TM_PROJ = 1024
TQ_ATTN = 256
TM_MIX = 1024
TM_ROUTE = 1024
TM_DISPATCH = 512
TM_EXPERT = 256
TM_COMBINE = 256
VMEM_LIMIT = 48 * 1024 * 1024

F32 = jnp.float32
BF16 = jnp.bfloat16


def _rms(x, g):
    return x * lax.rsqrt(jnp.mean(x * x, axis=-1, keepdims=True) + EPS) * g


def _gelu(x):
    c = math.sqrt(2.0 / math.pi)
    return x * (0.5 * (1.0 + jnp.tanh(c * (x + 0.044715 * (x * x * x)))))


def _softplus(z):
    return jnp.maximum(z, 0.0) + jnp.log(1.0 + jnp.exp(-jnp.abs(z)))


def _dot(a, b):
    return jnp.dot(a, b, preferred_element_type=F32)


def _rows_to_tiles(ref, x):
    m = x.shape[0]
    for k in range(ROW_TILE):
        ref[pl.ds(k, m, stride=ROW_TILE), :] = x[:, k * LANES:(k + 1) * LANES]


def _tiles_to_rows(ref, m):
    return jnp.concatenate([ref[pl.ds(k, m, stride=ROW_TILE), :] for k in range(ROW_TILE)], axis=1)


def _token_rows(ref, first_token, n_tokens):
    return ref.at[pl.ds(pl.multiple_of(first_token * ROW_TILE, ROW_TILE), n_tokens * ROW_TILE)]


def _split_bf16(x):
    hi = x.astype(BF16)
    lo = (x - hi.astype(F32)).astype(BF16)
    return hi, lo


def _inproj_kernel(x_ref, g_ref, wf_ref, sgg_ref, wsp_ref, bsp_ref, sgog_ref, qkv_ref, sgn_ref,
                   w_ref, gu_ref, vgn_ref, sg_ref):
    tm = TM_PROJ

    @pl.when(pl.program_id(0) == 0)
    def _():
        w_ref[...] = wf_ref[...].astype(BF16)

    hb = _rms(x_ref[...], g_ref[...]).astype(BF16)
    q = _dot(hb, w_ref[:, 0:SB_WIDTH]) * (1.0 / math.sqrt(HEAD_DIM))
    qkv_ref[:, 0:SB_WIDTH] = q.astype(BF16)
    qkv_ref[:, SB_WIDTH:3 * SB_WIDTH] = _dot(hb, w_ref[:, SB_WIDTH:3 * SB_WIDTH]).astype(BF16)
    gu_ref[...] = _gelu(_dot(hb, w_ref[:, 3 * SB_WIDTH:3 * SB_WIDTH + SG_WIDTH]))
    gv = _gelu(_dot(hb, w_ref[:, 3 * SB_WIDTH + SG_WIDTH:D_IN]))
    vgn_ref[...] = _rms(gv, sgg_ref[...]).astype(BF16)

    lane = lax.broadcasted_iota(jnp.int32, (1, LANES), 1)
    first = lane < HEAD_DIM
    zero = jnp.zeros((), BF16)
    r_c = lax.broadcasted_iota(jnp.int32, (CHUNK, CHUNK), 0)
    c_c = lax.broadcasted_iota(jnp.int32, (CHUNK, CHUNK), 1)
    tril = r_c >= c_c
    n_pairs = SG_WIDTH // HEAD_PAIR
    w_pairs = []
    for p in range(n_pairs):
        w0 = jnp.where(tril, wsp_ref[2 * p], 0.0).astype(BF16)
        w1 = jnp.where(tril, wsp_ref[2 * p + 1], 0.0).astype(BF16)
        w_pairs.append(jnp.concatenate([w0, w1], axis=1))
    bsp = bsp_ref[...]
    for c in range(tm // CHUNK):
        rows = slice(c * CHUNK, (c + 1) * CHUNK)
        for p in range(n_pairs):
            cols = slice(p * HEAD_PAIR, (p + 1) * HEAD_PAIR)
            vg = vgn_ref[rows, cols]
            rhs = jnp.concatenate([jnp.where(first, vg, zero), jnp.where(first, zero, vg)], axis=0)
            mixed = _dot(w_pairs[p], rhs) + bsp[:, cols]
            sg_ref[rows, cols] = gu_ref[rows, cols] * mixed
    sgn_ref[...] = _rms(sg_ref[...], sgog_ref[...]).astype(BF16)


def _inproj(x2, attn_g, w_in, sg_g, wsp, bsp_full, sg_out_g):
    n = x2.shape[0]
    row = lambda i: (i, 0)
    const = lambda i: (0, 0)
    return pl.pallas_call(
        _inproj_kernel,
        grid=(n // TM_PROJ,),
        in_specs=[pl.BlockSpec((TM_PROJ, D_MODEL), row),
                  pl.BlockSpec((1, D_MODEL), const),
                  pl.BlockSpec((D_MODEL, D_IN), const, pipeline_mode=pl.Buffered(1)),
                  pl.BlockSpec((1, SG_WIDTH), const),
                  pl.BlockSpec((SG_HEADS, CHUNK, CHUNK), lambda i: (0, 0, 0)),
                  pl.BlockSpec((CHUNK, SG_WIDTH), const),
                  pl.BlockSpec((1, SG_WIDTH), const)],
        out_specs=[pl.BlockSpec((TM_PROJ, 3 * SB_WIDTH), row),
                   pl.BlockSpec((TM_PROJ, SG_WIDTH), row)],
        out_shape=[jax.ShapeDtypeStruct((n, 3 * SB_WIDTH), BF16),
                   jax.ShapeDtypeStruct((n, SG_WIDTH), BF16)],
        scratch_shapes=[pltpu.VMEM((D_MODEL, D_IN), BF16),
                        pltpu.VMEM((TM_PROJ, SG_WIDTH), F32),
                        pltpu.VMEM((TM_PROJ, SG_WIDTH), BF16),
                        pltpu.VMEM((TM_PROJ, SG_WIDTH), F32)],
        compiler_params=pltpu.CompilerParams(dimension_semantics=("arbitrary",),
                                             vmem_limit_bytes=VMEM_LIMIT),
        name="inproj",
    )(x2, attn_g, w_in, sg_g, wsp, bsp_full, sg_out_g)


def _attn_kernel(q_ref, k_ref, v_ref, o_ref, q2_ref, carry_ref):
    t = TQ_ATTN
    n_pairs = SB_WIDTH // HEAD_PAIR
    qi = pl.program_id(1)
    lane = lax.broadcasted_iota(jnp.int32, (1, HEAD_PAIR), 1)
    head_lanes = (lane < HEAD_DIM, lane >= HEAD_DIM)
    zero = jnp.zeros((), BF16)
    for p in range(n_pairs):
        qp = q_ref[0, :, p * HEAD_PAIR:(p + 1) * HEAD_PAIR]
        for h in range(2):
            q2_ref[(2 * p + h) * t:(2 * p + h + 1) * t, :] = jnp.where(head_lanes[h], qp, zero)
    r_idx = lax.broadcasted_iota(jnp.int32, (t, t), 0)
    c_idx = lax.broadcasted_iota(jnp.int32, (t, t), 1)
    suffix = (r_idx > c_idx).astype(BF16)
    suffix2 = jnp.concatenate([suffix, suffix], axis=0)
    causal = c_idx < r_idx

    o_ref[...] = jnp.zeros_like(o_ref)
    carry_ref[...] = jnp.zeros_like(carry_ref)

    def block(j, diag):
        start = pl.multiple_of(j * t, t)
        for p in range(n_pairs):
            cols = slice(p * HEAD_PAIR, (p + 1) * HEAD_PAIR)
            rows = slice(2 * p * t, (2 * p + 2) * t)
            kb = k_ref[0, pl.ds(start, t), cols]
            vb = v_ref[0, pl.ds(start, t), cols]
            z = lax.dot_general(q2_ref[rows, :], kb, (((1,), (1,)), ((), ())),
                                preferred_element_type=F32)
            sp = _softplus(z)
            if diag:
                mask2 = jnp.concatenate([causal, causal], axis=0)
                nl = jnp.where(mask2, sp, 0.0)
            else:
                nl = sp
            hi, lo = _split_bf16(nl)
            hl = jnp.concatenate([hi, lo], axis=1)
            after = jnp.concatenate([_dot(hl[0:t], suffix2), _dot(hl[t:2 * t], suffix2)], axis=0)
            carry = carry_ref[rows, :]
            a = jnp.exp(z - sp - after - carry)
            if diag:
                a = jnp.where(mask2, a, 0.0)
            a = a.astype(BF16)
            a2 = jnp.concatenate([a[0:t], a[t:2 * t]], axis=1)
            v2 = jnp.concatenate([jnp.where(head_lanes[0], vb, zero),
                                  jnp.where(head_lanes[1], vb, zero)], axis=0)
            o_ref[0, :, cols] += _dot(a2, v2)
            carry_ref[rows, :] = carry + after[:, 0:1] + nl[:, 0:1]

    def live():
        return jnp.min(carry_ref[...]) < F32_EXP_UNDERFLOW

    block(qi, True)

    def body(state):
        it, _ = state
        block(qi - 1 - it, False)
        return it + 1, live()

    lax.while_loop(lambda s: (s[0] < qi) & s[1], body, (jnp.int32(0), live()))


def _attention(qkv, batch, seq):
    qkv3 = qkv.reshape(batch, seq, 3 * SB_WIDTH)
    n_heads = SB_WIDTH // HEAD_DIM
    return pl.pallas_call(
        _attn_kernel,
        grid=(batch, seq // TQ_ATTN),
        in_specs=[pl.BlockSpec((1, TQ_ATTN, SB_WIDTH), lambda b, i: (b, i, 0)),
                  pl.BlockSpec((1, seq, SB_WIDTH), lambda b, i: (b, 0, 1)),
                  pl.BlockSpec((1, seq, SB_WIDTH), lambda b, i: (b, 0, 2))],
        out_specs=pl.BlockSpec((1, TQ_ATTN, SB_WIDTH), lambda b, i: (b, i, 0)),
        out_shape=jax.ShapeDtypeStruct((batch, seq, SB_WIDTH), F32),
        scratch_shapes=[pltpu.VMEM((n_heads * TQ_ATTN, HEAD_PAIR), BF16),
                        pltpu.VMEM((n_heads * TQ_ATTN, 1), F32)],
        compiler_params=pltpu.CompilerParams(dimension_semantics=("arbitrary",) * 2,
                                             vmem_limit_bytes=VMEM_LIMIT),
        name="sb_attention",
    )(qkv3, qkv3, qkv3)


def _mix_kernel(sb_ref, sgn_ref, x_ref, sbg_ref, woutf_ref, ffng_ref, wr2_ref, br_ref,
                h_ref, hn_ref, lg_ref, wout_ref):
    @pl.when(pl.program_id(0) == 0)
    def _():
        wout_ref[...] = woutf_ref[...].astype(BF16)

    sbn = _rms(sb_ref[...], sbg_ref[...]).astype(BF16)
    h = x_ref[...] + _dot(sbn, wout_ref[0:SB_WIDTH, :]) + _dot(sgn_ref[...], wout_ref[SB_WIDTH:, :])
    h_ref[...] = h
    hn = _rms(h, ffng_ref[...])
    _rows_to_tiles(hn_ref, hn)

    hn_hi, hn_lo = _split_bf16(hn)
    both = _dot(hn_hi, wr2_ref[...])
    logits = both[:, 0:LANES] + both[:, LANES:] + _dot(hn_lo, wr2_ref[:, 0:LANES]) + br_ref[...]
    lg_ref[...] = logits.T[0:ROUTER_ROWS, :]


def _route_kernel(lg_ref, ri_ref, rw_ref, cnt_ref, count_ref):
    tr = TM_ROUTE
    i = pl.program_id(0)

    @pl.when(i == 0)
    def _():
        count_ref[...] = jnp.zeros_like(count_ref)

    neg = jnp.float32(-jnp.inf)
    row8 = lax.broadcasted_iota(jnp.int32, (8, tr), 0)

    def top(v):
        m = jnp.max(v, axis=0, keepdims=True)
        return m, jnp.min(jnp.where(v == m, row8, 8), axis=0, keepdims=True)

    gl = jnp.where(row8 < N_GROUPS, lg_ref[0:8, :], neg)
    gmax, gidx = top(gl)
    gweight = 1.0 / jnp.sum(jnp.exp(gl - gmax), axis=0, keepdims=True)
    el = lg_ref[8:16, :]
    for g in range(1, N_GROUPS):
        el = jnp.where(gidx == g, lg_ref[8 + 8 * g:16 + 8 * g, :], el)
    m1, i1 = top(el)
    m2, i2 = top(jnp.where(row8 == i1, neg, el))
    t21 = jnp.exp(m2 - m1)
    w1 = gweight / (1.0 + t21)
    w2 = gweight * t21 / (1.0 + t21)
    e1 = gidx * EXPERTS_PER_GROUP + i1
    e2 = gidx * EXPERTS_PER_GROUP + i2

    row_e = lax.broadcasted_iota(jnp.int32, (N_EXPERTS, tr), 0)
    sel1 = row_e == e1
    sel2 = row_e == e2
    onehot = jnp.where(sel1 | sel2, 1.0, 0.0)
    r_t = lax.broadcasted_iota(jnp.int32, (tr, tr), 0)
    c_t = lax.broadcasted_iota(jnp.int32, (tr, tr), 1)
    before = (r_t < c_t).astype(BF16)
    running = count_ref[:, 0:1] + _dot(onehot.astype(BF16), before)
    rank1 = jnp.sum(jnp.where(sel1, running, 0.0), axis=0, keepdims=True)
    rank2 = jnp.sum(jnp.where(sel2, running, 0.0), axis=0, keepdims=True)
    new_count = count_ref[:, 0:1] + jnp.sum(onehot, axis=1, keepdims=True)
    count_ref[...] = jnp.broadcast_to(new_count, count_ref.shape)
    cnt_ref[...] = jnp.broadcast_to(new_count, cnt_ref.shape)

    ri_ref[...] = jnp.where(row8 == 0, e1, jnp.where(row8 == 1, e2, jnp.where(
        row8 == 2, rank1.astype(jnp.int32), jnp.where(row8 == 3, rank2.astype(jnp.int32), 0))))
    row128 = lax.broadcasted_iota(jnp.int32, (LANES, tr), 0)
    rw_ref[...] = jnp.where(row128 == 0, w1, jnp.where(row128 == 1, w2, 0.0)).T


def _route(lg):
    n = lg.shape[1]
    return pl.pallas_call(
        _route_kernel,
        grid=(n // TM_ROUTE,),
        in_specs=[pl.BlockSpec((ROUTER_ROWS, TM_ROUTE), lambda i: (0, i))],
        out_specs=[pl.BlockSpec((8, TM_ROUTE), lambda i: (0, i)),
                   pl.BlockSpec((TM_ROUTE, LANES), lambda i: (i, 0)),
                   pl.BlockSpec((N_EXPERTS, LANES), lambda i: (0, 0))],
        out_shape=[jax.ShapeDtypeStruct((8, n), jnp.int32),
                   jax.ShapeDtypeStruct((n, LANES), F32),
                   jax.ShapeDtypeStruct((N_EXPERTS, LANES), F32)],
        scratch_shapes=[pltpu.VMEM((N_EXPERTS, LANES), F32)],
        compiler_params=pltpu.CompilerParams(dimension_semantics=("arbitrary",),
                                             vmem_limit_bytes=VMEM_LIMIT),
        name="route",
    )(lg)


def _mix(sb, sgn, x2, sb_g, w_out, ffn_g, wr2, br):
    n = x2.shape[0]
    row = lambda i: (i, 0)
    const = lambda i: (0, 0)
    return pl.pallas_call(
        _mix_kernel,
        grid=(n // TM_MIX,),
        in_specs=[pl.BlockSpec((TM_MIX, SB_WIDTH), row),
                  pl.BlockSpec((TM_MIX, SG_WIDTH), row),
                  pl.BlockSpec((TM_MIX, D_MODEL), row),
                  pl.BlockSpec((1, SB_WIDTH), const),
                  pl.BlockSpec((D_MODEL, D_MODEL), const, pipeline_mode=pl.Buffered(1)),
                  pl.BlockSpec((1, D_MODEL), const),
                  pl.BlockSpec((D_MODEL, 2 * LANES), const),
                  pl.BlockSpec((1, LANES), const)],
        out_specs=[pl.BlockSpec((TM_MIX, D_MODEL), row),
                   pl.BlockSpec((TM_MIX * ROW_TILE, LANES), row),
                   pl.BlockSpec((ROUTER_ROWS, TM_MIX), lambda i: (0, i))],
        out_shape=[jax.ShapeDtypeStruct((n, D_MODEL), F32),
                   jax.ShapeDtypeStruct((n * ROW_TILE, LANES), F32),
                   jax.ShapeDtypeStruct((ROUTER_ROWS, n), F32)],
        scratch_shapes=[pltpu.VMEM((D_MODEL, D_MODEL), BF16)],
        compiler_params=pltpu.CompilerParams(dimension_semantics=("arbitrary",),
                                             vmem_limit_bytes=VMEM_LIMIT),
        name="mix_router",
    )(sb, sgn, x2, sb_g, w_out, ffn_g, wr2, br)


_PAD_BITS = tuple(1 << b for b in reversed(range(TM_EXPERT.bit_length() - 1)))


def _dispatch_kernel(dest_ref, pad_start_ref, pad_count_ref, nt_ref, hn_ref, zeros_ref, xs_ref, sem, zsem):
    tm = TM_DISPATCH
    i = pl.program_id(0)
    n = pl.num_programs(0) * tm
    base = i * tm
    n_tiles_max = xs_ref.shape[0] // (TM_EXPERT * ROW_TILE)

    def pad_copies(do):
        for e in range(N_EXPERTS):
            start = pad_start_ref[e]
            count = pad_count_ref[e]
            for bit in _PAD_BITS:
                @pl.when((count & bit) != 0)
                def _(start=start, bit=bit):
                    do(pltpu.make_async_copy(_token_rows(zeros_ref, 0, bit),
                                             _token_rows(xs_ref, start, bit), zsem))
                start = start + (count & bit)
        for k in range(N_EXPERTS):
            tile = nt_ref[0] + k

            @pl.when(tile < n_tiles_max)
            def _(tile=tile):
                do(pltpu.make_async_copy(zeros_ref, _token_rows(xs_ref, tile * TM_EXPERT, TM_EXPERT), zsem))

    @pl.when(i == 0)
    def _():
        pad_copies(lambda cp: cp.start())

    def body(r, c):
        src = _token_rows(hn_ref, r, 1)
        for s in range(2):
            pltpu.make_async_copy(src, _token_rows(xs_ref, dest_ref[s * n + base + r], 1),
                                  sem).start(priority=s)
        return c

    lax.fori_loop(0, tm, body, 0, unroll=8)
    for _ in range(2):
        pltpu.make_async_copy(hn_ref, _token_rows(xs_ref, 0, tm), sem).wait()

    @pl.when(i == 0)
    def _():
        pad_copies(lambda cp: cp.wait())


def _dispatch(dest, pad_start, pad_count, n_tiles, hn_tiles, n_rows):
    n = hn_tiles.shape[0] // ROW_TILE
    zeros = jnp.zeros((TM_EXPERT * ROW_TILE, LANES), F32)
    return pl.pallas_call(
        _dispatch_kernel,
        grid_spec=pltpu.PrefetchScalarGridSpec(
            num_scalar_prefetch=4,
            grid=(n // TM_DISPATCH,),
            in_specs=[pl.BlockSpec((TM_DISPATCH * ROW_TILE, LANES), lambda i, *_: (i, 0)),
                      pl.BlockSpec(memory_space=pl.ANY)],
            out_specs=pl.BlockSpec(memory_space=pl.ANY),
            scratch_shapes=[pltpu.SemaphoreType.DMA, pltpu.SemaphoreType.DMA]),
        out_shape=jax.ShapeDtypeStruct((n_rows * ROW_TILE, LANES), F32),
        compiler_params=pltpu.CompilerParams(dimension_semantics=("arbitrary",),
                                             vmem_limit_bytes=VMEM_LIMIT),
        name="dispatch",
    )(dest, pad_start, pad_count, n_tiles, hn_tiles, zeros)


X_SLOTS = 3


def _expert_kernel(tiles_ref, nt_ref, xs_ref, wg_ref, wu_ref, wd_ref, y_ref,
                   x_buf, sg_buf, su_buf, sd_buf, wgb, wub, wdb, state, w_sems, x_sems):
    tm = TM_EXPERT
    t = pl.program_id(0)
    nt = nt_ref[0]

    def x_copy(tile):
        slot = lax.rem(tile, X_SLOTS)
        return pltpu.make_async_copy(_token_rows(xs_ref, tile * tm, tm), x_buf.at[slot], x_sems.at[slot])

    def weight_copies(e, slot):
        return (pltpu.make_async_copy(wg_ref.at[e], sg_buf.at[slot], w_sems.at[slot]),
                pltpu.make_async_copy(wu_ref.at[e], su_buf.at[slot], w_sems.at[slot]),
                pltpu.make_async_copy(wd_ref.at[e], sd_buf.at[slot], w_sems.at[slot]))

    def next_with_rows(e):
        return lax.while_loop(lambda k: (k < N_EXPERTS) & (tiles_ref[jnp.minimum(k, N_EXPERTS - 1)] == 0),
                              lambda k: k + 1, e + 1)

    @pl.when(t == 0)
    def _():
        first = next_with_rows(jnp.int32(-1))
        state[0] = jnp.int32(-1)
        state[1] = jnp.int32(0)
        state[2] = jnp.int32(1)
        state[3] = first
        for cp in weight_copies(first, 0):
            cp.start()
        x_copy(0).start()

        @pl.when(nt > 1)
        def _():
            x_copy(1).start()

    @pl.when(t + 2 < nt)
    def _():
        x_copy(t + 2).start()

    @pl.when(t < nt)
    def _():
        @pl.when(state[1] == 0)
        def _():
            e = state[3]
            slot = 1 - state[2]
            nxt = next_with_rows(e)
            state[0] = e
            state[1] = tiles_ref[e]
            state[2] = slot
            state[3] = nxt
            for cp in weight_copies(e, slot):
                cp.wait()

            @pl.when(nxt < N_EXPERTS)
            def _():
                for cp in weight_copies(nxt, 1 - slot):
                    cp.start()

            wgb[...] = sg_buf[slot].astype(BF16)
            wub[...] = su_buf[slot].astype(BF16)
            wdb[...] = sd_buf[slot].astype(BF16)

        state[1] = state[1] - 1
        x_copy(t).wait()
        x = _tiles_to_rows(x_buf.at[lax.rem(t, X_SLOTS)], tm).astype(BF16)
        g = _dot(x, wgb[...])
        u = _dot(x, wub[...])
        hidden = (g * jax.nn.sigmoid(g)) * u
        _rows_to_tiles(y_ref, _dot(hidden.astype(BF16), wdb[...]))

    @pl.when(t >= nt)
    def _():
        y_ref[...] = jnp.zeros_like(y_ref)


def _experts(tiles, n_tiles, xs, wg, wu, wd):
    n_rows = xs.shape[0] // ROW_TILE
    any_spec = pl.BlockSpec(memory_space=pl.ANY)
    return pl.pallas_call(
        _expert_kernel,
        grid_spec=pltpu.PrefetchScalarGridSpec(
            num_scalar_prefetch=2,
            grid=(n_rows // TM_EXPERT,),
            in_specs=[any_spec, any_spec, any_spec, any_spec],
            out_specs=pl.BlockSpec((TM_EXPERT * ROW_TILE, LANES), lambda t, *_: (t, 0)),
            scratch_shapes=[pltpu.VMEM((X_SLOTS, TM_EXPERT * ROW_TILE, LANES), F32),
                            pltpu.VMEM((2, D_MODEL, D_EXPERT), F32),
                            pltpu.VMEM((2, D_MODEL, D_EXPERT), F32),
                            pltpu.VMEM((2, D_EXPERT, D_MODEL), F32),
                            pltpu.VMEM((D_MODEL, D_EXPERT), BF16),
                            pltpu.VMEM((D_MODEL, D_EXPERT), BF16),
                            pltpu.VMEM((D_EXPERT, D_MODEL), BF16),
                            pltpu.SMEM((4,), jnp.int32),
                            pltpu.SemaphoreType.DMA((2,)),
                            pltpu.SemaphoreType.DMA((X_SLOTS,))]),
        out_shape=jax.ShapeDtypeStruct((n_rows * ROW_TILE, LANES), F32),
        compiler_params=pltpu.CompilerParams(dimension_semantics=("arbitrary",),
                                             vmem_limit_bytes=VMEM_LIMIT),
        name="expert_mlp",
    )(tiles, n_tiles, xs, wg, wu, wd)


def _combine_kernel(dest_ref, h_ref, rw_ref, fg_ref, y_ref, o_ref, buf, sems):
    tm = TM_COMBINE
    i = pl.program_id(0)
    n_steps = pl.num_programs(0)
    n = n_steps * tm
    cur = i % 2

    def fetch(step, half):
        def body(r, c):
            for s in range(2):
                pltpu.make_async_copy(_token_rows(y_ref, dest_ref[s * n + step * tm + r], 1),
                                      _token_rows(buf.at[half, s], r, 1),
                                      sems.at[half]).start(priority=s)
            return c

        lax.fori_loop(0, tm, body, 0, unroll=8)

    @pl.when(i == 0)
    def _():
        fetch(0, 0)

    @pl.when(i + 1 < n_steps)
    def _():
        fetch(i + 1, 1 - cur)

    for s in range(2):
        pltpu.make_async_copy(_token_rows(y_ref, 0, tm), buf.at[cur, s], sems.at[cur]).wait()
    rw = rw_ref[...]
    out = (h_ref[...] + rw[:, 0:1] * _tiles_to_rows(buf.at[cur, 0], tm)
           + rw[:, 1:2] * _tiles_to_rows(buf.at[cur, 1], tm))
    o_ref[...] = _rms(out, fg_ref[...])


def _combine(dest, h, rw, final_g, ys):
    n = h.shape[0]
    return pl.pallas_call(
        _combine_kernel,
        grid_spec=pltpu.PrefetchScalarGridSpec(
            num_scalar_prefetch=1,
            grid=(n // TM_COMBINE,),
            in_specs=[pl.BlockSpec((TM_COMBINE, D_MODEL), lambda i, d: (i, 0)),
                      pl.BlockSpec((TM_COMBINE, LANES), lambda i, d: (i, 0)),
                      pl.BlockSpec((1, D_MODEL), lambda i, d: (0, 0)),
                      pl.BlockSpec(memory_space=pl.ANY)],
            out_specs=pl.BlockSpec((TM_COMBINE, D_MODEL), lambda i, d: (i, 0)),
            scratch_shapes=[pltpu.VMEM((2, 2, TM_COMBINE * ROW_TILE, LANES), F32),
                            pltpu.SemaphoreType.DMA((2,))]),
        out_shape=jax.ShapeDtypeStruct((n, D_MODEL), F32),
        compiler_params=pltpu.CompilerParams(dimension_semantics=("arbitrary",),
                                             vmem_limit_bytes=VMEM_LIMIT),
        name="combine",
    )(dest, h, rw, final_g, ys)


def _schedule(counts):
    tiles = (counts + TM_EXPERT - 1) // TM_EXPERT
    tile_end = jnp.cumsum(tiles)
    offsets = (tile_end - tiles) * TM_EXPERT
    return tiles, offsets, tile_end[-1:]


def _layer(x, attn_g, w_in, sg_g, w_sp, b_sp, sb_g, sg_out_g, w_out, ffn_g,
           w_rg, b_rg, w_re, b_re, w_gate, w_up, w_down):
    batch, seq, _ = x.shape
    n = batch * seq
    x2 = x.reshape(n, D_MODEL)
    row = lambda v: v.reshape(1, -1)

    bsp_full = jnp.repeat(b_sp.T, HEAD_DIM, axis=1)
    qkv, sgn = _inproj(x2, row(attn_g), w_in, row(sg_g), w_sp, bsp_full, row(sg_out_g))
    sb = _attention(qkv, batch, seq).reshape(n, SB_WIDTH)

    pad_lanes = lambda v, width: jnp.pad(v, [(0, 0)] * (v.ndim - 1) + [(0, width - v.shape[-1])])
    w_r = jnp.concatenate([pad_lanes(w_rg, ROUTER_LANE0),
                           jnp.transpose(w_re, (1, 0, 2)).reshape(D_MODEL, N_EXPERTS)], axis=1)
    w_r = pad_lanes(w_r, LANES)
    wr_hi = w_r.astype(BF16)
    wr_lo = (w_r - wr_hi.astype(F32)).astype(BF16)
    wr2 = jnp.concatenate([wr_hi, wr_lo], axis=1)
    b_r = pad_lanes(jnp.concatenate([pad_lanes(b_rg, ROUTER_LANE0), b_re.reshape(-1)]), LANES)

    h, hn, lg = _mix(sb, sgn, x2, row(sb_g), w_out, row(ffn_g), wr2, row(b_r))
    ri, rw, cnt = _route(lg)

    counts = cnt[:, 0].astype(jnp.int32)
    n_rows = 2 * n + N_EXPERTS * TM_EXPERT
    tiles, offsets, n_tiles = _schedule(counts)
    expert, rank = ri[0:2], ri[2:4]
    is_e = expert[None] == jnp.arange(N_EXPERTS, dtype=jnp.int32)[:, None, None]
    dest = (jnp.sum(jnp.where(is_e, offsets[:, None, None], 0), axis=0) + rank).reshape(-1)
    pad_start = offsets + counts
    pad_count = (-counts) % TM_EXPERT

    xs = _dispatch(dest, pad_start, pad_count, n_tiles, hn, n_rows)
    ys = _experts(tiles, n_tiles, xs,
                  w_gate.reshape(N_EXPERTS, D_MODEL, D_EXPERT),
                  w_up.reshape(N_EXPERTS, D_MODEL, D_EXPERT),
                  w_down.reshape(N_EXPERTS, D_EXPERT, D_MODEL))
    return dest, h, rw, ys


def kernel(x, attn_norm_g, w_in, sg_norm_g, w_spatial, b_spatial, sb_out_norm_g, sg_out_norm_g,
           w_out, ffn_norm_g, w_router_group, b_router_group, w_router_expert, b_router_expert,
           w_gate, w_up, w_down, final_norm_g):
    assert attn_norm_g.shape[0] == 1, "single-layer problem"
    batch, seq, _ = x.shape
    dest, h, rw, ys = _layer(x, attn_norm_g[0], w_in[0], sg_norm_g[0], w_spatial[0], b_spatial[0],
                             sb_out_norm_g[0], sg_out_norm_g[0], w_out[0], ffn_norm_g[0],
                             w_router_group[0], b_router_group[0], w_router_expert[0],
                             b_router_expert[0], w_gate[0], w_up[0], w_down[0])
    out = _combine(dest, h, rw, final_norm_g.reshape(1, -1), ys)
    return out.reshape(batch, seq, D_MODEL)
```

```python
import functools
import math

import jax
import jax.numpy as jnp
from jax import lax
from jax.experimental import pallas as pl
from jax.experimental.pallas import tpu as pltpu

D_MODEL = 1024
HEAD_DIM = 64
SB_WIDTH = 512
SG_WIDTH = 512
SG_HEADS = 8
D_IN = 3 * SB_WIDTH + 2 * SG_WIDTH
CHUNK = 128
N_GROUPS = 4
EXPERTS_PER_GROUP = 8
N_EXPERTS = N_GROUPS * EXPERTS_PER_GROUP
D_EXPERT = 512
EPS = 1e-6
F32_EXP_UNDERFLOW = 110.0

LANES = 128
ROW_TILE = D_MODEL // LANES
assert ROW_TILE == 8
HEAD_PAIR = 2 * HEAD_DIM
ROUTER_LANE0 = 8
ROUTER_ROWS = ROUTER_LANE0 + N_EXPERTS
assert EXPERTS_PER_GROUP == 8 and N_GROUPS <= ROUTER_LANE0

TM_PROJ = 1024
TQ_ATTN = 256
TM_MIX = 1024
TM_ROUTE = 1024
TM_DISPATCH = 512
TM_EXPERT = 512
TM_COMBINE = 256
VMEM_LIMIT = 48 * 1024 * 1024

F32 = jnp.float32
BF16 = jnp.bfloat16


def _rms(x, g):
    return x * lax.rsqrt(jnp.mean(x * x, axis=-1, keepdims=True) + EPS) * g


def _gelu(x):
    c = math.sqrt(2.0 / math.pi)
    return x * (0.5 * (1.0 + jnp.tanh(c * (x + 0.044715 * (x * x * x)))))


def _softplus(z):
    return jnp.maximum(z, 0.0) + jnp.log(1.0 + jnp.exp(-jnp.abs(z)))


def _dot(a, b):
    return jnp.dot(a, b, preferred_element_type=F32)


def _rows_to_tiles(ref, x):
    m = x.shape[0]
    for k in range(ROW_TILE):
        ref[pl.ds(k, m, stride=ROW_TILE), :] = x[:, k * LANES:(k + 1) * LANES]


def _tiles_to_rows(ref, m):
    return jnp.concatenate([ref[pl.ds(k, m, stride=ROW_TILE), :] for k in range(ROW_TILE)], axis=1)


def _token_rows(ref, first_token, n_tokens):
    return ref.at[pl.ds(pl.multiple_of(first_token * ROW_TILE, ROW_TILE), n_tokens * ROW_TILE)]


def _split_bf16(x):
    hi = x.astype(BF16)
    lo = (x - hi.astype(F32)).astype(BF16)
    return hi, lo


def _inproj_kernel(x_ref, g_ref, w_ref, sgg_ref, wsp_ref, bsp_ref, sgog_ref, qkv_ref, sgn_ref,
                   gu_ref, vgn_ref, sg_ref):
    tm = TM_PROJ
    hb = _rms(x_ref[...], g_ref[...]).astype(BF16)
    q = _dot(hb, w_ref[:, 0:SB_WIDTH]) * (1.0 / math.sqrt(HEAD_DIM))
    qkv_ref[:, 0:SB_WIDTH] = q.astype(BF16)
    qkv_ref[:, SB_WIDTH:3 * SB_WIDTH] = _dot(hb, w_ref[:, SB_WIDTH:3 * SB_WIDTH]).astype(BF16)
    gu_ref[...] = _gelu(_dot(hb, w_ref[:, 3 * SB_WIDTH:3 * SB_WIDTH + SG_WIDTH]))
    gv = _gelu(_dot(hb, w_ref[:, 3 * SB_WIDTH + SG_WIDTH:D_IN]))
    vgn_ref[...] = _rms(gv, sgg_ref[...]).astype(BF16)

    lane = lax.broadcasted_iota(jnp.int32, (1, LANES), 1)
    first = lane < HEAD_DIM
    zero = jnp.zeros((), BF16)
    r_c = lax.broadcasted_iota(jnp.int32, (CHUNK, CHUNK), 0)
    c_c = lax.broadcasted_iota(jnp.int32, (CHUNK, CHUNK), 1)
    tril = r_c >= c_c
    n_pairs = SG_WIDTH // HEAD_PAIR
    w_pairs = []
    for p in range(n_pairs):
        w0 = jnp.where(tril, wsp_ref[2 * p], 0.0).astype(BF16)
        w1 = jnp.where(tril, wsp_ref[2 * p + 1], 0.0).astype(BF16)
        w_pairs.append(jnp.concatenate([w0, w1], axis=1))
    bsp = bsp_ref[...]
    for c in range(tm // CHUNK):
        rows = slice(c * CHUNK, (c + 1) * CHUNK)
        for p in range(n_pairs):
            cols = slice(p * HEAD_PAIR, (p + 1) * HEAD_PAIR)
            vg = vgn_ref[rows, cols]
            rhs = jnp.concatenate([jnp.where(first, vg, zero), jnp.where(first, zero, vg)], axis=0)
            mixed = _dot(w_pairs[p], rhs) + bsp[:, cols]
            sg_ref[rows, cols] = gu_ref[rows, cols] * mixed
    sgn_ref[...] = _rms(sg_ref[...], sgog_ref[...]).astype(BF16)


def _inproj(x2, attn_g, w_in_b, sg_g, wsp, bsp_full, sg_out_g):
    n = x2.shape[0]
    row = lambda i: (i, 0)
    const = lambda i: (0, 0)
    return pl.pallas_call(
        _inproj_kernel,
        grid=(n // TM_PROJ,),
        in_specs=[pl.BlockSpec((TM_PROJ, D_MODEL), row),
                  pl.BlockSpec((1, D_MODEL), const),
                  pl.BlockSpec((D_MODEL, D_IN), const),
                  pl.BlockSpec((1, SG_WIDTH), const),
                  pl.BlockSpec((SG_HEADS, CHUNK, CHUNK), lambda i: (0, 0, 0)),
                  pl.BlockSpec((CHUNK, SG_WIDTH), const),
                  pl.BlockSpec((1, SG_WIDTH), const)],
        out_specs=[pl.BlockSpec((TM_PROJ, 3 * SB_WIDTH), row),
                   pl.BlockSpec((TM_PROJ, SG_WIDTH), row)],
        out_shape=[jax.ShapeDtypeStruct((n, 3 * SB_WIDTH), BF16),
                   jax.ShapeDtypeStruct((n, SG_WIDTH), BF16)],
        scratch_shapes=[pltpu.VMEM((TM_PROJ, SG_WIDTH), F32),
                        pltpu.VMEM((TM_PROJ, SG_WIDTH), BF16),
                        pltpu.VMEM((TM_PROJ, SG_WIDTH), F32)],
        compiler_params=pltpu.CompilerParams(dimension_semantics=("arbitrary",),
                                             vmem_limit_bytes=VMEM_LIMIT),
        name="inproj",
    )(x2, attn_g, w_in_b, sg_g, wsp, bsp_full, sg_out_g)


def _attn_kernel(q_ref, k_ref, v_ref, o_ref, q2_ref, carry_ref):
    t = TQ_ATTN
    n_pairs = SB_WIDTH // HEAD_PAIR
    qi = pl.program_id(1)
    lane = lax.broadcasted_iota(jnp.int32, (1, HEAD_PAIR), 1)
    head_lanes = (lane < HEAD_DIM, lane >= HEAD_DIM)
    zero = jnp.zeros((), BF16)
    for p in range(n_pairs):
        qp = q_ref[0, :, p * HEAD_PAIR:(p + 1) * HEAD_PAIR]
        for h in range(2):
            q2_ref[(2 * p + h) * t:(2 * p + h + 1) * t, :] = jnp.where(head_lanes[h], qp, zero)
    r_idx = lax.broadcasted_iota(jnp.int32, (t, t), 0)
    c_idx = lax.broadcasted_iota(jnp.int32, (t, t), 1)
    suffix = (r_idx > c_idx).astype(BF16)
    suffix2 = jnp.concatenate([suffix, suffix], axis=0)
    causal = c_idx < r_idx

    o_ref[...] = jnp.zeros_like(o_ref)
    carry_ref[...] = jnp.zeros_like(carry_ref)

    def block(j, diag):
        start = pl.multiple_of(j * t, t)
        for p in range(n_pairs):
            cols = slice(p * HEAD_PAIR, (p + 1) * HEAD_PAIR)
            rows = slice(2 * p * t, (2 * p + 2) * t)
            kb = k_ref[0, pl.ds(start, t), cols]
            vb = v_ref[0, pl.ds(start, t), cols]
            z = lax.dot_general(q2_ref[rows, :], kb, (((1,), (1,)), ((), ())),
                                preferred_element_type=F32)
            sp = _softplus(z)
            if diag:
                mask2 = jnp.concatenate([causal, causal], axis=0)
                nl = jnp.where(mask2, sp, 0.0)
            else:
                nl = sp
            hi, lo = _split_bf16(nl)
            hl = jnp.concatenate([hi, lo], axis=1)
            after = jnp.concatenate([_dot(hl[0:t], suffix2), _dot(hl[t:2 * t], suffix2)], axis=0)
            carry = carry_ref[rows, :]
            a = jnp.exp(z - sp - after - carry)
            if diag:
                a = jnp.where(mask2, a, 0.0)
            a = a.astype(BF16)
            a2 = jnp.concatenate([a[0:t], a[t:2 * t]], axis=1)
            v2 = jnp.concatenate([jnp.where(head_lanes[0], vb, zero),
                                  jnp.where(head_lanes[1], vb, zero)], axis=0)
            o_ref[0, :, cols] += _dot(a2, v2)
            carry_ref[rows, :] = carry + after[:, 0:1] + nl[:, 0:1]

    def live():
        return jnp.min(carry_ref[...]) < F32_EXP_UNDERFLOW

    block(qi, True)

    def body(state):
        it, _ = state
        block(qi - 1 - it, False)
        return it + 1, live()

    lax.while_loop(lambda s: (s[0] < qi) & s[1], body, (jnp.int32(0), live()))


def _attention(qkv, batch, seq):
    qkv3 = qkv.reshape(batch, seq, 3 * SB_WIDTH)
    n_heads = SB_WIDTH // HEAD_DIM
    return pl.pallas_call(
        _attn_kernel,
        grid=(batch, seq // TQ_ATTN),
        in_specs=[pl.BlockSpec((1, TQ_ATTN, SB_WIDTH), lambda b, i: (b, i, 0)),
                  pl.BlockSpec((1, seq, SB_WIDTH), lambda b, i: (b, 0, 1)),
                  pl.BlockSpec((1, seq, SB_WIDTH), lambda b, i: (b, 0, 2))],
        out_specs=pl.BlockSpec((1, TQ_ATTN, SB_WIDTH), lambda b, i: (b, i, 0)),
        out_shape=jax.ShapeDtypeStruct((batch, seq, SB_WIDTH), F32),
        scratch_shapes=[pltpu.VMEM((n_heads * TQ_ATTN, HEAD_PAIR), BF16),
                        pltpu.VMEM((n_heads * TQ_ATTN, 1), F32)],
        compiler_params=pltpu.CompilerParams(dimension_semantics=("arbitrary",) * 2,
                                             vmem_limit_bytes=VMEM_LIMIT),
        name="sb_attention",
    )(qkv3, qkv3, qkv3)


def _mix_kernel(sb_ref, sgn_ref, x_ref, sbg_ref, wout_ref, ffng_ref, wr2_ref, br_ref,
                h_ref, hn_ref, lg_ref):
    sbn = _rms(sb_ref[...], sbg_ref[...]).astype(BF16)
    h = x_ref[...] + _dot(sbn, wout_ref[0:SB_WIDTH, :]) + _dot(sgn_ref[...], wout_ref[SB_WIDTH:, :])
    h_ref[...] = h
    hn = _rms(h, ffng_ref[...])
    _rows_to_tiles(hn_ref, hn)

    hn_hi, hn_lo = _split_bf16(hn)
    both = _dot(hn_hi, wr2_ref[...])
    logits = both[:, 0:LANES] + both[:, LANES:] + _dot(hn_lo, wr2_ref[:, 0:LANES]) + br_ref[...]
    lg_ref[...] = logits.T[0:ROUTER_ROWS, :]


def _route_kernel(lg_ref, ri_ref, rw_ref, cnt_ref, count_ref):
    tr = TM_ROUTE
    i = pl.program_id(0)

    @pl.when(i == 0)
    def _():
        count_ref[...] = jnp.zeros_like(count_ref)

    neg = jnp.float32(-jnp.inf)
    row8 = lax.broadcasted_iota(jnp.int32, (8, tr), 0)

    def top(v):
        m = jnp.max(v, axis=0, keepdims=True)
        return m, jnp.min(jnp.where(v == m, row8, 8), axis=0, keepdims=True)

    gl = jnp.where(row8 < N_GROUPS, lg_ref[0:8, :], neg)
    gmax, gidx = top(gl)
    gweight = 1.0 / jnp.sum(jnp.exp(gl - gmax), axis=0, keepdims=True)
    el = lg_ref[8:16, :]
    for g in range(1, N_GROUPS):
        el = jnp.where(gidx == g, lg_ref[8 + 8 * g:16 + 8 * g, :], el)
    m1, i1 = top(el)
    m2, i2 = top(jnp.where(row8 == i1, neg, el))
    t21 = jnp.exp(m2 - m1)
    w1 = gweight / (1.0 + t21)
    w2 = gweight * t21 / (1.0 + t21)
    e1 = gidx * EXPERTS_PER_GROUP + i1
    e2 = gidx * EXPERTS_PER_GROUP + i2

    row_e = lax.broadcasted_iota(jnp.int32, (N_EXPERTS, tr), 0)
    sel1 = row_e == e1
    sel2 = row_e == e2
    onehot = jnp.where(sel1 | sel2, 1.0, 0.0)
    r_t = lax.broadcasted_iota(jnp.int32, (tr, tr), 0)
    c_t = lax.broadcasted_iota(jnp.int32, (tr, tr), 1)
    before = (r_t < c_t).astype(BF16)
    running = count_ref[:, 0:1] + _dot(onehot.astype(BF16), before)
    rank1 = jnp.sum(jnp.where(sel1, running, 0.0), axis=0, keepdims=True)
    rank2 = jnp.sum(jnp.where(sel2, running, 0.0), axis=0, keepdims=True)
    new_count = count_ref[:, 0:1] + jnp.sum(onehot, axis=1, keepdims=True)
    count_ref[...] = jnp.broadcast_to(new_count, count_ref.shape)
    cnt_ref[...] = jnp.broadcast_to(new_count, cnt_ref.shape)

    ri_ref[...] = jnp.where(row8 == 0, e1, jnp.where(row8 == 1, e2, jnp.where(
        row8 == 2, rank1.astype(jnp.int32), jnp.where(row8 == 3, rank2.astype(jnp.int32), 0))))
    row128 = lax.broadcasted_iota(jnp.int32, (LANES, tr), 0)
    rw_ref[...] = jnp.where(row128 == 0, w1, jnp.where(row128 == 1, w2, 0.0)).T


def _route(lg):
    n = lg.shape[1]
    return pl.pallas_call(
        _route_kernel,
        grid=(n // TM_ROUTE,),
        in_specs=[pl.BlockSpec((ROUTER_ROWS, TM_ROUTE), lambda i: (0, i))],
        out_specs=[pl.BlockSpec((8, TM_ROUTE), lambda i: (0, i)),
                   pl.BlockSpec((TM_ROUTE, LANES), lambda i: (i, 0)),
                   pl.BlockSpec((N_EXPERTS, LANES), lambda i: (0, 0))],
        out_shape=[jax.ShapeDtypeStruct((8, n), jnp.int32),
                   jax.ShapeDtypeStruct((n, LANES), F32),
                   jax.ShapeDtypeStruct((N_EXPERTS, LANES), F32)],
        scratch_shapes=[pltpu.VMEM((N_EXPERTS, LANES), F32)],
        compiler_params=pltpu.CompilerParams(dimension_semantics=("arbitrary",),
                                             vmem_limit_bytes=VMEM_LIMIT),
        name="route",
    )(lg)


def _mix(sb, sgn, x2, sb_g, w_out_b, ffn_g, wr2, br):
    n = x2.shape[0]
    row = lambda i: (i, 0)
    const = lambda i: (0, 0)
    return pl.pallas_call(
        _mix_kernel,
        grid=(n // TM_MIX,),
        in_specs=[pl.BlockSpec((TM_MIX, SB_WIDTH), row),
                  pl.BlockSpec((TM_MIX, SG_WIDTH), row),
                  pl.BlockSpec((TM_MIX, D_MODEL), row),
                  pl.BlockSpec((1, SB_WIDTH), const),
                  pl.BlockSpec((D_MODEL, D_MODEL), const),
                  pl.BlockSpec((1, D_MODEL), const),
                  pl.BlockSpec((D_MODEL, 2 * LANES), const),
                  pl.BlockSpec((1, LANES), const)],
        out_specs=[pl.BlockSpec((TM_MIX, D_MODEL), row),
                   pl.BlockSpec((TM_MIX * ROW_TILE, LANES), row),
                   pl.BlockSpec((ROUTER_ROWS, TM_MIX), lambda i: (0, i))],
        out_shape=[jax.ShapeDtypeStruct((n, D_MODEL), F32),
                   jax.ShapeDtypeStruct((n * ROW_TILE, LANES), F32),
                   jax.ShapeDtypeStruct((ROUTER_ROWS, n), F32)],
        compiler_params=pltpu.CompilerParams(dimension_semantics=("arbitrary",),
                                             vmem_limit_bytes=VMEM_LIMIT),
        name="mix_router",
    )(sb, sgn, x2, sb_g, w_out_b, ffn_g, wr2, br)


_PAD_BITS = tuple(1 << b for b in reversed(range(TM_EXPERT.bit_length() - 1)))


def _dispatch_kernel(dest_ref, pad_start_ref, pad_count_ref, nt_ref, hn_ref, zeros_ref, xs_ref, sem, zsem):
    tm = TM_DISPATCH
    i = pl.program_id(0)
    n = pl.num_programs(0) * tm
    base = i * tm
    n_tiles_max = xs_ref.shape[0] // (TM_EXPERT * ROW_TILE)

    def pad_copies(do):
        for e in range(N_EXPERTS):
            start = pad_start_ref[e]
            count = pad_count_ref[e]
            for bit in _PAD_BITS:
                @pl.when((count & bit) != 0)
                def _(start=start, bit=bit):
                    do(pltpu.make_async_copy(_token_rows(zeros_ref, 0, bit),
                                             _token_rows(xs_ref, start, bit), zsem))
                start = start + (count & bit)
        for k in range(N_EXPERTS):
            tile = nt_ref[0] + k

            @pl.when(tile < n_tiles_max)
            def _(tile=tile):
                do(pltpu.make_async_copy(zeros_ref, _token_rows(xs_ref, tile * TM_EXPERT, TM_EXPERT), zsem))

    @pl.when(i == 0)
    def _():
        pad_copies(lambda cp: cp.start())

    def body(r, c):
        src = _token_rows(hn_ref, r, 1)
        for s in range(2):
            pltpu.make_async_copy(src, _token_rows(xs_ref, dest_ref[s * n + base + r], 1),
                                  sem).start(priority=s)
        return c

    lax.fori_loop(0, tm, body, 0, unroll=8)
    for _ in range(2):
        pltpu.make_async_copy(hn_ref, _token_rows(xs_ref, 0, tm), sem).wait()

    @pl.when(i == 0)
    def _():
        pad_copies(lambda cp: cp.wait())


def _dispatch(dest, pad_start, pad_count, n_tiles, hn_tiles, n_rows):
    n = hn_tiles.shape[0] // ROW_TILE
    zeros = jnp.zeros((TM_EXPERT * ROW_TILE, LANES), F32)
    return pl.pallas_call(
        _dispatch_kernel,
        grid_spec=pltpu.PrefetchScalarGridSpec(
            num_scalar_prefetch=4,
            grid=(n // TM_DISPATCH,),
            in_specs=[pl.BlockSpec((TM_DISPATCH * ROW_TILE, LANES), lambda i, *_: (i, 0)),
                      pl.BlockSpec(memory_space=pl.ANY)],
            out_specs=pl.BlockSpec(memory_space=pl.ANY),
            scratch_shapes=[pltpu.SemaphoreType.DMA, pltpu.SemaphoreType.DMA]),
        out_shape=jax.ShapeDtypeStruct((n_rows * ROW_TILE, LANES), F32),
        compiler_params=pltpu.CompilerParams(dimension_semantics=("arbitrary",),
                                             vmem_limit_bytes=VMEM_LIMIT),
        name="dispatch",
    )(dest, pad_start, pad_count, n_tiles, hn_tiles, zeros)


X_SLOTS = 3


def _expert_kernel(tiles_ref, nt_ref, xs_ref, wg_ref, wu_ref, wd_ref, y_ref,
                   x_buf, sg_buf, su_buf, sd_buf, wgb, wub, wdb, state, w_sems, x_sems):
    tm = TM_EXPERT
    t = pl.program_id(0)
    nt = nt_ref[0]

    def x_copy(tile):
        slot = lax.rem(tile, X_SLOTS)
        return pltpu.make_async_copy(_token_rows(xs_ref, tile * tm, tm), x_buf.at[slot], x_sems.at[slot])

    def weight_copies(e, slot):
        return (pltpu.make_async_copy(wg_ref.at[e], sg_buf.at[slot], w_sems.at[slot]),
                pltpu.make_async_copy(wu_ref.at[e], su_buf.at[slot], w_sems.at[slot]),
                pltpu.make_async_copy(wd_ref.at[e], sd_buf.at[slot], w_sems.at[slot]))

    def next_with_rows(e):
        return lax.while_loop(lambda k: (k < N_EXPERTS) & (tiles_ref[jnp.minimum(k, N_EXPERTS - 1)] == 0),
                              lambda k: k + 1, e + 1)

    @pl.when(t == 0)
    def _():
        first = next_with_rows(jnp.int32(-1))
        state[0] = jnp.int32(-1)
        state[1] = jnp.int32(0)
        state[2] = jnp.int32(1)
        state[3] = first
        for cp in weight_copies(first, 0):
            cp.start()
        x_copy(0).start()

        @pl.when(nt > 1)
        def _():
            x_copy(1).start()

    @pl.when(t + 2 < nt)
    def _():
        x_copy(t + 2).start()

    @pl.when(t < nt)
    def _():
        @pl.when(state[1] == 0)
        def _():
            e = state[3]
            slot = 1 - state[2]
            nxt = next_with_rows(e)
            state[0] = e
            state[1] = tiles_ref[e]
            state[2] = slot
            state[3] = nxt
            for cp in weight_copies(e, slot):
                cp.wait()

            @pl.when(nxt < N_EXPERTS)
            def _():
                for cp in weight_copies(nxt, 1 - slot):
                    cp.start()

            wgb[...] = sg_buf[slot].astype(BF16)
            wub[...] = su_buf[slot].astype(BF16)
            wdb[...] = sd_buf[slot].astype(BF16)

        state[1] = state[1] - 1
        x_copy(t).wait()
        x = _tiles_to_rows(x_buf.at[lax.rem(t, X_SLOTS)], tm).astype(BF16)
        g = _dot(x, wgb[...])
        u = _dot(x, wub[...])
        hidden = (g * jax.nn.sigmoid(g)) * u
        _rows_to_tiles(y_ref, _dot(hidden.astype(BF16), wdb[...]))

    @pl.when(t >= nt)
    def _():
        y_ref[...] = jnp.zeros_like(y_ref)


def _experts(tiles, n_tiles, xs, wg, wu, wd):
    n_rows = xs.shape[0] // ROW_TILE
    any_spec = pl.BlockSpec(memory_space=pl.ANY)
    return pl.pallas_call(
        _expert_kernel,
        grid_spec=pltpu.PrefetchScalarGridSpec(
            num_scalar_prefetch=2,
            grid=(n_rows // TM_EXPERT,),
            in_specs=[any_spec, any_spec, any_spec, any_spec],
            out_specs=pl.BlockSpec((TM_EXPERT * ROW_TILE, LANES), lambda t, *_: (t, 0)),
            scratch_shapes=[pltpu.VMEM((X_SLOTS, TM_EXPERT * ROW_TILE, LANES), F32),
                            pltpu.VMEM((2, D_MODEL, D_EXPERT), F32),
                            pltpu.VMEM((2, D_MODEL, D_EXPERT), F32),
                            pltpu.VMEM((2, D_EXPERT, D_MODEL), F32),
                            pltpu.VMEM((D_MODEL, D_EXPERT), BF16),
                            pltpu.VMEM((D_MODEL, D_EXPERT), BF16),
                            pltpu.VMEM((D_EXPERT, D_MODEL), BF16),
                            pltpu.SMEM((4,), jnp.int32),
                            pltpu.SemaphoreType.DMA((2,)),
                            pltpu.SemaphoreType.DMA((X_SLOTS,))]),
        out_shape=jax.ShapeDtypeStruct((n_rows * ROW_TILE, LANES), F32),
        compiler_params=pltpu.CompilerParams(dimension_semantics=("arbitrary",),
                                             vmem_limit_bytes=VMEM_LIMIT),
        name="expert_mlp",
    )(tiles, n_tiles, xs, wg, wu, wd)


def _combine_kernel(dest_ref, h_ref, rw_ref, fg_ref, y_ref, o_ref, buf, sems):
    tm = TM_COMBINE
    i = pl.program_id(0)
    n_steps = pl.num_programs(0)
    n = n_steps * tm
    cur = i % 2

    def fetch(step, half):
        def body(r, c):
            for s in range(2):
                pltpu.make_async_copy(_token_rows(y_ref, dest_ref[s * n + step * tm + r], 1),
                                      _token_rows(buf.at[half, s], r, 1),
                                      sems.at[half]).start(priority=s)
            return c

        lax.fori_loop(0, tm, body, 0, unroll=8)

    @pl.when(i == 0)
    def _():
        fetch(0, 0)

    @pl.when(i + 1 < n_steps)
    def _():
        fetch(i + 1, 1 - cur)

    for s in range(2):
        pltpu.make_async_copy(_token_rows(y_ref, 0, tm), buf.at[cur, s], sems.at[cur]).wait()
    rw = rw_ref[...]
    out = (h_ref[...] + rw[:, 0:1] * _tiles_to_rows(buf.at[cur, 0], tm)
           + rw[:, 1:2] * _tiles_to_rows(buf.at[cur, 1], tm))
    o_ref[...] = _rms(out, fg_ref[...])


def _combine(dest, h, rw, final_g, ys):
    n = h.shape[0]
    return pl.pallas_call(
        _combine_kernel,
        grid_spec=pltpu.PrefetchScalarGridSpec(
            num_scalar_prefetch=1,
            grid=(n // TM_COMBINE,),
            in_specs=[pl.BlockSpec((TM_COMBINE, D_MODEL), lambda i, d: (i, 0)),
                      pl.BlockSpec((TM_COMBINE, LANES), lambda i, d: (i, 0)),
                      pl.BlockSpec((1, D_MODEL), lambda i, d: (0, 0)),
                      pl.BlockSpec(memory_space=pl.ANY)],
            out_specs=pl.BlockSpec((TM_COMBINE, D_MODEL), lambda i, d: (i, 0)),
            scratch_shapes=[pltpu.VMEM((2, 2, TM_COMBINE * ROW_TILE, LANES), F32),
                            pltpu.SemaphoreType.DMA((2,))]),
        out_shape=jax.ShapeDtypeStruct((n, D_MODEL), F32),
        compiler_params=pltpu.CompilerParams(dimension_semantics=("arbitrary",),
                                             vmem_limit_bytes=VMEM_LIMIT),
        name="combine",
    )(dest, h, rw, final_g, ys)


def _schedule(counts):
    tiles = (counts + TM_EXPERT - 1) // TM_EXPERT
    tile_end = jnp.cumsum(tiles)
    offsets = (tile_end - tiles) * TM_EXPERT
    return tiles, offsets, tile_end[-1:]


def _layer(x, attn_g, w_in, sg_g, w_sp, b_sp, sb_g, sg_out_g, w_out, ffn_g,
           w_rg, b_rg, w_re, b_re, w_gate, w_up, w_down):
    batch, seq, _ = x.shape
    n = batch * seq
    x2 = x.reshape(n, D_MODEL)
    row = lambda v: v.reshape(1, -1)

    bsp_full = jnp.repeat(b_sp.T, HEAD_DIM, axis=1)
    qkv, sgn = _inproj(x2, row(attn_g), w_in.astype(BF16), row(sg_g), w_sp, bsp_full, row(sg_out_g))
    sb = _attention(qkv, batch, seq).reshape(n, SB_WIDTH)

    pad_lanes = lambda v, width: jnp.pad(v, [(0, 0)] * (v.ndim - 1) + [(0, width - v.shape[-1])])
    w_r = jnp.concatenate([pad_lanes(w_rg, ROUTER_LANE0),
                           jnp.transpose(w_re, (1, 0, 2)).reshape(D_MODEL, N_EXPERTS)], axis=1)
    w_r = pad_lanes(w_r, LANES)
    wr_hi = w_r.astype(BF16)
    wr_lo = (w_r - wr_hi.astype(F32)).astype(BF16)
    wr2 = jnp.concatenate([wr_hi, wr_lo], axis=1)
    b_r = pad_lanes(jnp.concatenate([pad_lanes(b_rg, ROUTER_LANE0), b_re.reshape(-1)]), LANES)

    h, hn, lg = _mix(sb, sgn, x2, row(sb_g), w_out.astype(BF16), row(ffn_g), wr2, row(b_r))
    ri, rw, cnt = _route(lg)

    counts = cnt[:, 0].astype(jnp.int32)
    n_rows = 2 * n + N_EXPERTS * TM_EXPERT
    tiles, offsets, n_tiles = _schedule(counts)
    expert, rank = ri[0:2], ri[2:4]
    is_e = expert[None] == jnp.arange(N_EXPERTS, dtype=jnp.int32)[:, None, None]
    dest = (jnp.sum(jnp.where(is_e, offsets[:, None, None], 0), axis=0) + rank).reshape(-1)
    pad_start = offsets + counts
    pad_count = (-counts) % TM_EXPERT

    xs = _dispatch(dest, pad_start, pad_count, n_tiles, hn, n_rows)
    ys = _experts(tiles, n_tiles, xs,
                  w_gate.reshape(N_EXPERTS, D_MODEL, D_EXPERT),
                  w_up.reshape(N_EXPERTS, D_MODEL, D_EXPERT),
                  w_down.reshape(N_EXPERTS, D_EXPERT, D_MODEL))
    return dest, h, rw, ys


def kernel(x, attn_norm_g, w_in, sg_norm_g, w_spatial, b_spatial, sb_out_norm_g, sg_out_norm_g,
           w_out, ffn_norm_g, w_router_group, b_router_group, w_router_expert, b_router_expert,
           w_gate, w_up, w_down, final_norm_g):
    assert attn_norm_g.shape[0] == 1, "single-layer problem"
    batch, seq, _ = x.shape
    dest, h, rw, ys = _layer(x, attn_norm_g[0], w_in[0], sg_norm_g[0], w_spatial[0], b_spatial[0],
                             sb_out_norm_g[0], sg_out_norm_g[0], w_out[0], ffn_norm_g[0],
                             w_router_group[0], b_router_group[0], w_router_expert[0],
                             b_router_expert[0], w_gate[0], w_up[0], w_down[0])
    out = _combine(dest, h, rw, final_norm_g.reshape(1, -1), ys)
    return out.reshape(batch, seq, D_MODEL)
```

```python
import functools
import math

import jax
import jax.numpy as jnp
from jax import lax
from jax.experimental import pallas as pl
from jax.experimental.pallas import tpu as pltpu

D_MODEL = 1024
HEAD_DIM = 64
SB_WIDTH = 512
SG_WIDTH = 512
SG_HEADS = 8
D_IN = 3 * SB_WIDTH + 2 * SG_WIDTH
CHUNK = 128
N_GROUPS = 4
EXPERTS_PER_GROUP = 8
N_EXPERTS = N_GROUPS * EXPERTS_PER_GROUP
D_EXPERT = 512
EPS = 1e-6
F32_EXP_UNDERFLOW = 110.0

LANES = 128
ROW_TILE = D_MODEL // LANES
assert ROW_TILE == 8
HEAD_PAIR = 2 * HEAD_DIM
ROUTER_LANE0 = 8
ROUTER_ROWS = ROUTER_LANE0 + N_EXPERTS
assert EXPERTS_PER_GROUP == 8 and N_GROUPS <= ROUTER_LANE0

TM_PROJ = 1024
TQ_ATTN = 256
TM_MIX = 1024
TM_ROUTE = 1024
TM_DISPATCH = 512
TM_EXPERT = 512
TM_COMBINE = 256
VMEM_LIMIT = 48 * 1024 * 1024

F32 = jnp.float32
BF16 = jnp.bfloat16


def _rms(x, g):
    return x * lax.rsqrt(jnp.mean(x * x, axis=-1, keepdims=True) + EPS) * g


def _gelu(x):
    c = math.sqrt(2.0 / math.pi)
    return x * (0.5 * (1.0 + jnp.tanh(c * (x + 0.044715 * (x * x * x)))))


def _softplus(z):
    return jnp.maximum(z, 0.0) + jnp.log(1.0 + jnp.exp(-jnp.abs(z)))


def _dot(a, b):
    return jnp.dot(a, b, preferred_element_type=F32)


def _rows_to_tiles(ref, x):
    m = x.shape[0]
    for k in range(ROW_TILE):
        ref[pl.ds(k, m, stride=ROW_TILE), :] = x[:, k * LANES:(k + 1) * LANES]


def _tiles_to_rows(ref, m):
    return jnp.concatenate([ref[pl.ds(k, m, stride=ROW_TILE), :] for k in range(ROW_TILE)], axis=1)


def _token_rows(ref, first_token, n_tokens):
    return ref.at[pl.ds(pl.multiple_of(first_token * ROW_TILE, ROW_TILE), n_tokens * ROW_TILE)]


def _split_bf16(x):
    hi = x.astype(BF16)
    lo = (x - hi.astype(F32)).astype(BF16)
    return hi, lo


def _inproj_kernel(x_ref, g_ref, w_ref, sgg_ref, wsp_ref, bsp_ref, sgog_ref, qkv_ref, sgn_ref,
                   gu_ref, vgn_ref, sg_ref):
    tm = TM_PROJ
    hb = _rms(x_ref[...], g_ref[...]).astype(BF16)
    q = _dot(hb, w_ref[:, 0:SB_WIDTH]) * (1.0 / math.sqrt(HEAD_DIM))
    qkv_ref[:, 0:SB_WIDTH] = q.astype(BF16)
    qkv_ref[:, SB_WIDTH:3 * SB_WIDTH] = _dot(hb, w_ref[:, SB_WIDTH:3 * SB_WIDTH]).astype(BF16)
    gu_ref[...] = _gelu(_dot(hb, w_ref[:, 3 * SB_WIDTH:3 * SB_WIDTH + SG_WIDTH]))
    gv = _gelu(_dot(hb, w_ref[:, 3 * SB_WIDTH + SG_WIDTH:D_IN]))
    vgn_ref[...] = _rms(gv, sgg_ref[...]).astype(BF16)

    lane = lax.broadcasted_iota(jnp.int32, (1, LANES), 1)
    first = lane < HEAD_DIM
    zero = jnp.zeros((), BF16)
    r_c = lax.broadcasted_iota(jnp.int32, (CHUNK, CHUNK), 0)
    c_c = lax.broadcasted_iota(jnp.int32, (CHUNK, CHUNK), 1)
    tril = r_c >= c_c
    n_pairs = SG_WIDTH // HEAD_PAIR
    w_pairs = []
    for p in range(n_pairs):
        w0 = jnp.where(tril, wsp_ref[2 * p], 0.0).astype(BF16)
        w1 = jnp.where(tril, wsp_ref[2 * p + 1], 0.0).astype(BF16)
        w_pairs.append(jnp.concatenate([w0, w1], axis=1))
    bsp = bsp_ref[...]
    for c in range(tm // CHUNK):
        rows = slice(c * CHUNK, (c + 1) * CHUNK)
        for p in range(n_pairs):
            cols = slice(p * HEAD_PAIR, (p + 1) * HEAD_PAIR)
            vg = vgn_ref[rows, cols]
            rhs = jnp.concatenate([jnp.where(first, vg, zero), jnp.where(first, zero, vg)], axis=0)
            mixed = _dot(w_pairs[p], rhs) + bsp[:, cols]
            sg_ref[rows, cols] = gu_ref[rows, cols] * mixed
    sgn_ref[...] = _rms(sg_ref[...], sgog_ref[...]).astype(BF16)


def _inproj(x2, attn_g, w_in_b, sg_g, wsp, bsp_full, sg_out_g):
    n = x2.shape[0]
    row = lambda i: (i, 0)
    const = lambda i: (0, 0)
    return pl.pallas_call(
        _inproj_kernel,
        grid=(n // TM_PROJ,),
        in_specs=[pl.BlockSpec((TM_PROJ, D_MODEL), row),
                  pl.BlockSpec((1, D_MODEL), const),
                  pl.BlockSpec((D_MODEL, D_IN), const),
                  pl.BlockSpec((1, SG_WIDTH), const),
                  pl.BlockSpec((SG_HEADS, CHUNK, CHUNK), lambda i: (0, 0, 0)),
                  pl.BlockSpec((CHUNK, SG_WIDTH), const),
                  pl.BlockSpec((1, SG_WIDTH), const)],
        out_specs=[pl.BlockSpec((TM_PROJ, 3 * SB_WIDTH), row),
                   pl.BlockSpec((TM_PROJ, SG_WIDTH), row)],
        out_shape=[jax.ShapeDtypeStruct((n, 3 * SB_WIDTH), BF16),
                   jax.ShapeDtypeStruct((n, SG_WIDTH), BF16)],
        scratch_shapes=[pltpu.VMEM((TM_PROJ, SG_WIDTH), F32),
                        pltpu.VMEM((TM_PROJ, SG_WIDTH), BF16),
                        pltpu.VMEM((TM_PROJ, SG_WIDTH), F32)],
        compiler_params=pltpu.CompilerParams(dimension_semantics=("arbitrary",),
                                             vmem_limit_bytes=VMEM_LIMIT),
        name="inproj",
    )(x2, attn_g, w_in_b, sg_g, wsp, bsp_full, sg_out_g)


def _attn_kernel(q_ref, k_ref, v_ref, o_ref, q2_ref, carry_ref):
    t = TQ_ATTN
    n_pairs = SB_WIDTH // HEAD_PAIR
    qi = pl.program_id(1)
    lane = lax.broadcasted_iota(jnp.int32, (1, HEAD_PAIR), 1)
    head_lanes = (lane < HEAD_DIM, lane >= HEAD_DIM)
    zero = jnp.zeros((), BF16)
    for p in range(n_pairs):
        qp = q_ref[0, :, p * HEAD_PAIR:(p + 1) * HEAD_PAIR]
        for h in range(2):
            q2_ref[(2 * p + h) * t:(2 * p + h + 1) * t, :] = jnp.where(head_lanes[h], qp, zero)
    r_idx = lax.broadcasted_iota(jnp.int32, (t, t), 0)
    c_idx = lax.broadcasted_iota(jnp.int32, (t, t), 1)
    suffix = (r_idx > c_idx).astype(BF16)
    suffix2 = jnp.concatenate([suffix, suffix], axis=0)
    causal = c_idx < r_idx

    o_ref[...] = jnp.zeros_like(o_ref)
    carry_ref[...] = jnp.zeros_like(carry_ref)

    def block(j, diag):
        start = pl.multiple_of(j * t, t)
        for p in range(n_pairs):
            cols = slice(p * HEAD_PAIR, (p + 1) * HEAD_PAIR)
            rows = slice(2 * p * t, (2 * p + 2) * t)
            kb = k_ref[0, pl.ds(start, t), cols]
            vb = v_ref[0, pl.ds(start, t), cols]
            z = lax.dot_general(q2_ref[rows, :], kb, (((1,), (1,)), ((), ())),
                                preferred_element_type=F32)
            sp = _softplus(z)
            if diag:
                mask2 = jnp.concatenate([causal, causal], axis=0)
                nl = jnp.where(mask2, sp, 0.0)
            else:
                nl = sp
            hi, lo = _split_bf16(nl)
            hl = jnp.concatenate([hi, lo], axis=1)
            after = jnp.concatenate([_dot(hl[0:t], suffix2), _dot(hl[t:2 * t], suffix2)], axis=0)
            carry = carry_ref[rows, :]
            a = jnp.exp(z - sp - after - carry)
            if diag:
                a = jnp.where(mask2, a, 0.0)
            a = a.astype(BF16)
            a2 = jnp.concatenate([a[0:t], a[t:2 * t]], axis=1)
            v2 = jnp.concatenate([jnp.where(head_lanes[0], vb, zero),
                                  jnp.where(head_lanes[1], vb, zero)], axis=0)
            o_ref[0, :, cols] += _dot(a2, v2)
            carry_ref[rows, :] = carry + after[:, 0:1] + nl[:, 0:1]

    def live():
        return jnp.min(carry_ref[...]) < F32_EXP_UNDERFLOW

    block(qi, True)

    def body(state):
        it, _ = state
        block(qi - 1 - it, False)
        return it + 1, live()

    lax.while_loop(lambda s: (s[0] < qi) & s[1], body, (jnp.int32(0), live()))


def _attention(qkv, batch, seq):
    qkv3 = qkv.reshape(batch, seq, 3 * SB_WIDTH)
    n_heads = SB_WIDTH // HEAD_DIM
    return pl.pallas_call(
        _attn_kernel,
        grid=(batch, seq // TQ_ATTN),
        in_specs=[pl.BlockSpec((1, TQ_ATTN, SB_WIDTH), lambda b, i: (b, i, 0)),
                  pl.BlockSpec((1, seq, SB_WIDTH), lambda b, i: (b, 0, 1)),
                  pl.BlockSpec((1, seq, SB_WIDTH), lambda b, i: (b, 0, 2))],
        out_specs=pl.BlockSpec((1, TQ_ATTN, SB_WIDTH), lambda b, i: (b, i, 0)),
        out_shape=jax.ShapeDtypeStruct((batch, seq, SB_WIDTH), F32),
        scratch_shapes=[pltpu.VMEM((n_heads * TQ_ATTN, HEAD_PAIR), BF16),
                        pltpu.VMEM((n_heads * TQ_ATTN, 1), F32)],
        compiler_params=pltpu.CompilerParams(dimension_semantics=("arbitrary",) * 2,
                                             vmem_limit_bytes=VMEM_LIMIT),
        name="sb_attention",
    )(qkv3, qkv3, qkv3)


def _mix_kernel(sb_ref, sgn_ref, x_ref, sbg_ref, wout_ref, ffng_ref, wr2_ref, br_ref,
                h_ref, hn_ref, lg_ref):
    sbn = _rms(sb_ref[...], sbg_ref[...]).astype(BF16)
    h = x_ref[...] + _dot(sbn, wout_ref[0:SB_WIDTH, :]) + _dot(sgn_ref[...], wout_ref[SB_WIDTH:, :])
    h_ref[...] = h
    hn = _rms(h, ffng_ref[...])
    _rows_to_tiles(hn_ref, hn)

    hn_hi, hn_lo = _split_bf16(hn)
    both = _dot(hn_hi, wr2_ref[...])
    logits = both[:, 0:LANES] + both[:, LANES:] + _dot(hn_lo, wr2_ref[:, 0:LANES]) + br_ref[...]
    lg_ref[...] = logits.T[0:ROUTER_ROWS, :]


def _route_kernel(lg_ref, ri_ref, rw_ref, cnt_ref, count_ref):
    tr = TM_ROUTE
    i = pl.program_id(0)

    @pl.when(i == 0)
    def _():
        count_ref[...] = jnp.zeros_like(count_ref)

    neg = jnp.float32(-jnp.inf)
    row8 = lax.broadcasted_iota(jnp.int32, (8, tr), 0)

    def top(v):
        m = jnp.max(v, axis=0, keepdims=True)
        return m, jnp.min(jnp.where(v == m, row8, 8), axis=0, keepdims=True)

    gl = jnp.where(row8 < N_GROUPS, lg_ref[0:8, :], neg)
    gmax, gidx = top(gl)
    gweight = 1.0 / jnp.sum(jnp.exp(gl - gmax), axis=0, keepdims=True)
    el = lg_ref[8:16, :]
    for g in range(1, N_GROUPS):
        el = jnp.where(gidx == g, lg_ref[8 + 8 * g:16 + 8 * g, :], el)
    m1, i1 = top(el)
    m2, i2 = top(jnp.where(row8 == i1, neg, el))
    t21 = jnp.exp(m2 - m1)
    w1 = gweight / (1.0 + t21)
    w2 = gweight * t21 / (1.0 + t21)
    e1 = gidx * EXPERTS_PER_GROUP + i1
    e2 = gidx * EXPERTS_PER_GROUP + i2

    row_e = lax.broadcasted_iota(jnp.int32, (N_EXPERTS, tr), 0)
    sel1 = row_e == e1
    sel2 = row_e == e2
    onehot = jnp.where(sel1 | sel2, 1.0, 0.0)
    r_t = lax.broadcasted_iota(jnp.int32, (tr, tr), 0)
    c_t = lax.broadcasted_iota(jnp.int32, (tr, tr), 1)
    before = (r_t < c_t).astype(BF16)
    running = count_ref[:, 0:1] + _dot(onehot.astype(BF16), before)
    rank1 = jnp.sum(jnp.where(sel1, running, 0.0), axis=0, keepdims=True)
    rank2 = jnp.sum(jnp.where(sel2, running, 0.0), axis=0, keepdims=True)
    new_count = count_ref[:, 0:1] + jnp.sum(onehot, axis=1, keepdims=True)
    count_ref[...] = jnp.broadcast_to(new_count, count_ref.shape)
    cnt_ref[...] = jnp.broadcast_to(new_count, cnt_ref.shape)

    ri_ref[...] = jnp.where(row8 == 0, e1, jnp.where(row8 == 1, e2, jnp.where(
        row8 == 2, rank1.astype(jnp.int32), jnp.where(row8 == 3, rank2.astype(jnp.int32), 0))))
    row128 = lax.broadcasted_iota(jnp.int32, (LANES, tr), 0)
    rw_ref[...] = jnp.where(row128 == 0, w1, jnp.where(row128 == 1, w2, 0.0)).T


def _route(lg):
    n = lg.shape[1]
    return pl.pallas_call(
        _route_kernel,
        grid=(n // TM_ROUTE,),
        in_specs=[pl.BlockSpec((ROUTER_ROWS, TM_ROUTE), lambda i: (0, i))],
        out_specs=[pl.BlockSpec((8, TM_ROUTE), lambda i: (0, i)),
                   pl.BlockSpec((TM_ROUTE, LANES), lambda i: (i, 0)),
                   pl.BlockSpec((N_EXPERTS, LANES), lambda i: (0, 0))],
        out_shape=[jax.ShapeDtypeStruct((8, n), jnp.int32),
                   jax.ShapeDtypeStruct((n, LANES), F32),
                   jax.ShapeDtypeStruct((N_EXPERTS, LANES), F32)],
        scratch_shapes=[pltpu.VMEM((N_EXPERTS, LANES), F32)],
        compiler_params=pltpu.CompilerParams(dimension_semantics=("arbitrary",),
                                             vmem_limit_bytes=VMEM_LIMIT),
        name="route",
    )(lg)


def _mix(sb, sgn, x2, sb_g, w_out_b, ffn_g, wr2, br):
    n = x2.shape[0]
    row = lambda i: (i, 0)
    const = lambda i: (0, 0)
    return pl.pallas_call(
        _mix_kernel,
        grid=(n // TM_MIX,),
        in_specs=[pl.BlockSpec((TM_MIX, SB_WIDTH), row),
                  pl.BlockSpec((TM_MIX, SG_WIDTH), row),
                  pl.BlockSpec((TM_MIX, D_MODEL), row),
                  pl.BlockSpec((1, SB_WIDTH), const),
                  pl.BlockSpec((D_MODEL, D_MODEL), const),
                  pl.BlockSpec((1, D_MODEL), const),
                  pl.BlockSpec((D_MODEL, 2 * LANES), const),
                  pl.BlockSpec((1, LANES), const)],
        out_specs=[pl.BlockSpec((TM_MIX, D_MODEL), row),
                   pl.BlockSpec((TM_MIX * ROW_TILE, LANES), row),
                   pl.BlockSpec((ROUTER_ROWS, TM_MIX), lambda i: (0, i))],
        out_shape=[jax.ShapeDtypeStruct((n, D_MODEL), F32),
                   jax.ShapeDtypeStruct((n * ROW_TILE, LANES), F32),
                   jax.ShapeDtypeStruct((ROUTER_ROWS, n), F32)],
        compiler_params=pltpu.CompilerParams(dimension_semantics=("arbitrary",),
                                             vmem_limit_bytes=VMEM_LIMIT),
        name="mix_router",
    )(sb, sgn, x2, sb_g, w_out_b, ffn_g, wr2, br)


_PAD_BITS = tuple(1 << b for b in reversed(range(TM_EXPERT.bit_length() - 1)))


def _dispatch_kernel(dest_ref, pad_start_ref, pad_count_ref, nt_ref, hn_ref, zeros_ref, xs_ref, sem, zsem):
    tm = TM_DISPATCH
    i = pl.program_id(0)
    n = pl.num_programs(0) * tm
    base = i * tm
    n_tiles_max = xs_ref.shape[0] // (TM_EXPERT * ROW_TILE)

    def pad_copies(do):
        for e in range(N_EXPERTS):
            start = pad_start_ref[e]
            count = pad_count_ref[e]
            for bit in _PAD_BITS:
                @pl.when((count & bit) != 0)
                def _(start=start, bit=bit):
                    do(pltpu.make_async_copy(_token_rows(zeros_ref, 0, bit),
                                             _token_rows(xs_ref, start, bit), zsem))
                start = start + (count & bit)
        for k in range(N_EXPERTS):
            tile = nt_ref[0] + k

            @pl.when(tile < n_tiles_max)
            def _(tile=tile):
                do(pltpu.make_async_copy(zeros_ref, _token_rows(xs_ref, tile * TM_EXPERT, TM_EXPERT), zsem))

    @pl.when(i == 0)
    def _():
        pad_copies(lambda cp: cp.start())

    def body(r, c):
        src = _token_rows(hn_ref, r, 1)
        for s in range(2):
            pltpu.make_async_copy(src, _token_rows(xs_ref, dest_ref[s * n + base + r], 1),
                                  sem).start(priority=s)
        return c

    lax.fori_loop(0, tm, body, 0, unroll=8)
    for _ in range(2):
        pltpu.make_async_copy(hn_ref, _token_rows(xs_ref, 0, tm), sem).wait()

    @pl.when(i == pl.num_programs(0) - 1)
    def _():
        pad_copies(lambda cp: cp.wait())


def _dispatch(dest, pad_start, pad_count, n_tiles, hn_tiles, n_rows):
    n = hn_tiles.shape[0] // ROW_TILE
    zeros = jnp.zeros((TM_EXPERT * ROW_TILE, LANES), F32)
    return pl.pallas_call(
        _dispatch_kernel,
        grid_spec=pltpu.PrefetchScalarGridSpec(
            num_scalar_prefetch=4,
            grid=(n // TM_DISPATCH,),
            in_specs=[pl.BlockSpec((TM_DISPATCH * ROW_TILE, LANES), lambda i, *_: (i, 0)),
                      pl.BlockSpec(memory_space=pl.ANY)],
            out_specs=pl.BlockSpec(memory_space=pl.ANY),
            scratch_shapes=[pltpu.SemaphoreType.DMA, pltpu.SemaphoreType.DMA]),
        out_shape=jax.ShapeDtypeStruct((n_rows * ROW_TILE, LANES), F32),
        compiler_params=pltpu.CompilerParams(dimension_semantics=("arbitrary",),
                                             vmem_limit_bytes=VMEM_LIMIT),
        name="dispatch",
    )(dest, pad_start, pad_count, n_tiles, hn_tiles, zeros)


X_SLOTS = 3


def _expert_kernel(tiles_ref, nt_ref, xs_ref, wg_ref, wu_ref, wd_ref, y_ref,
                   x_buf, sg_buf, su_buf, sd_buf, wgb, wub, wdb, state, w_sems, x_sems):
    tm = TM_EXPERT
    t = pl.program_id(0)
    nt = nt_ref[0]

    def x_copy(tile):
        slot = lax.rem(tile, X_SLOTS)
        return pltpu.make_async_copy(_token_rows(xs_ref, tile * tm, tm), x_buf.at[slot], x_sems.at[slot])

    def weight_copies(e, slot):
        return (pltpu.make_async_copy(wg_ref.at[e], sg_buf.at[slot], w_sems.at[slot]),
                pltpu.make_async_copy(wu_ref.at[e], su_buf.at[slot], w_sems.at[slot]),
                pltpu.make_async_copy(wd_ref.at[e], sd_buf.at[slot], w_sems.at[slot]))

    def next_with_rows(e):
        return lax.while_loop(lambda k: (k < N_EXPERTS) & (tiles_ref[jnp.minimum(k, N_EXPERTS - 1)] == 0),
                              lambda k: k + 1, e + 1)

    @pl.when(t == 0)
    def _():
        first = next_with_rows(jnp.int32(-1))
        state[0] = jnp.int32(-1)
        state[1] = jnp.int32(0)
        state[2] = jnp.int32(1)
        state[3] = first
        for cp in weight_copies(first, 0):
            cp.start()
        x_copy(0).start()

        @pl.when(nt > 1)
        def _():
            x_copy(1).start()

    @pl.when(t + 2 < nt)
    def _():
        x_copy(t + 2).start()

    @pl.when(t < nt)
    def _():
        @pl.when(state[1] == 0)
        def _():
            e = state[3]
            slot = 1 - state[2]
            nxt = next_with_rows(e)
            state[0] = e
            state[1] = tiles_ref[e]
            state[2] = slot
            state[3] = nxt
            for cp in weight_copies(e, slot):
                cp.wait()

            @pl.when(nxt < N_EXPERTS)
            def _():
                for cp in weight_copies(nxt, 1 - slot):
                    cp.start()

            wgb[...] = sg_buf[slot].astype(BF16)
            wub[...] = su_buf[slot].astype(BF16)
            wdb[...] = sd_buf[slot].astype(BF16)

        state[1] = state[1] - 1
        x_copy(t).wait()
        x = _tiles_to_rows(x_buf.at[lax.rem(t, X_SLOTS)], tm).astype(BF16)
        g = _dot(x, wgb[...])
        u = _dot(x, wub[...])
        hidden = (g * jax.nn.sigmoid(g)) * u
        _rows_to_tiles(y_ref, _dot(hidden.astype(BF16), wdb[...]))

    @pl.when(t >= nt)
    def _():
        y_ref[...] = jnp.zeros_like(y_ref)


def _experts(tiles, n_tiles, xs, wg, wu, wd):
    n_rows = xs.shape[0] // ROW_TILE
    any_spec = pl.BlockSpec(memory_space=pl.ANY)
    return pl.pallas_call(
        _expert_kernel,
        grid_spec=pltpu.PrefetchScalarGridSpec(
            num_scalar_prefetch=2,
            grid=(n_rows // TM_EXPERT,),
            in_specs=[any_spec, any_spec, any_spec, any_spec],
            out_specs=pl.BlockSpec((TM_EXPERT * ROW_TILE, LANES), lambda t, *_: (t, 0)),
            scratch_shapes=[pltpu.VMEM((X_SLOTS, TM_EXPERT * ROW_TILE, LANES), F32),
                            pltpu.VMEM((2, D_MODEL, D_EXPERT), F32),
                            pltpu.VMEM((2, D_MODEL, D_EXPERT), F32),
                            pltpu.VMEM((2, D_EXPERT, D_MODEL), F32),
                            pltpu.VMEM((D_MODEL, D_EXPERT), BF16),
                            pltpu.VMEM((D_MODEL, D_EXPERT), BF16),
                            pltpu.VMEM((D_EXPERT, D_MODEL), BF16),
                            pltpu.SMEM((4,), jnp.int32),
                            pltpu.SemaphoreType.DMA((2,)),
                            pltpu.SemaphoreType.DMA((X_SLOTS,))]),
        out_shape=jax.ShapeDtypeStruct((n_rows * ROW_TILE, LANES), F32),
        compiler_params=pltpu.CompilerParams(dimension_semantics=("arbitrary",),
                                             vmem_limit_bytes=VMEM_LIMIT),
        name="expert_mlp",
    )(tiles, n_tiles, xs, wg, wu, wd)


def _combine_kernel(dest_ref, h_ref, rw_ref, fg_ref, y_ref, o_ref, buf, sems):
    tm = TM_COMBINE
    i = pl.program_id(0)
    n_steps = pl.num_programs(0)
    n = n_steps * tm
    cur = i % 2

    def fetch(step, half):
        def body(r, c):
            for s in range(2):
                pltpu.make_async_copy(_token_rows(y_ref, dest_ref[s * n + step * tm + r], 1),
                                      _token_rows(buf.at[half, s], r, 1),
                                      sems.at[half]).start(priority=s)
            return c

        lax.fori_loop(0, tm, body, 0, unroll=8)

    @pl.when(i == 0)
    def _():
        fetch(0, 0)

    @pl.when(i + 1 < n_steps)
    def _():
        fetch(i + 1, 1 - cur)

    for s in range(2):
        pltpu.make_async_copy(_token_rows(y_ref, 0, tm), buf.at[cur, s], sems.at[cur]).wait()
    rw = rw_ref[...]
    out = (h_ref[...] + rw[:, 0:1] * _tiles_to_rows(buf.at[cur, 0], tm)
           + rw[:, 1:2] * _tiles_to_rows(buf.at[cur, 1], tm))
    o_ref[...] = _rms(out, fg_ref[...])


def _combine(dest, h, rw, final_g, ys):
    n = h.shape[0]
    return pl.pallas_call(
        _combine_kernel,
        grid_spec=pltpu.PrefetchScalarGridSpec(
            num_scalar_prefetch=1,
            grid=(n // TM_COMBINE,),
            in_specs=[pl.BlockSpec((TM_COMBINE, D_MODEL), lambda i, d: (i, 0)),
                      pl.BlockSpec((TM_COMBINE, LANES), lambda i, d: (i, 0)),
                      pl.BlockSpec((1, D_MODEL), lambda i, d: (0, 0)),
                      pl.BlockSpec(memory_space=pl.ANY)],
            out_specs=pl.BlockSpec((TM_COMBINE, D_MODEL), lambda i, d: (i, 0)),
            scratch_shapes=[pltpu.VMEM((2, 2, TM_COMBINE * ROW_TILE, LANES), F32),
                            pltpu.SemaphoreType.DMA((2,))]),
        out_shape=jax.ShapeDtypeStruct((n, D_MODEL), F32),
        compiler_params=pltpu.CompilerParams(dimension_semantics=("arbitrary",),
                                             vmem_limit_bytes=VMEM_LIMIT),
        name="combine",
    )(dest, h, rw, final_g, ys)


def _schedule(counts):
    tiles = (counts + TM_EXPERT - 1) // TM_EXPERT
    tile_end = jnp.cumsum(tiles)
    offsets = (tile_end - tiles) * TM_EXPERT
    return tiles, offsets, tile_end[-1:]


def _layer(x, attn_g, w_in, sg_g, w_sp, b_sp, sb_g, sg_out_g, w_out, ffn_g,
           w_rg, b_rg, w_re, b_re, w_gate, w_up, w_down):
    batch, seq, _ = x.shape
    n = batch * seq
    x2 = x.reshape(n, D_MODEL)
    row = lambda v: v.reshape(1, -1)

    bsp_full = jnp.repeat(b_sp.T, HEAD_DIM, axis=1)
    qkv, sgn = _inproj(x2, row(attn_g), w_in.astype(BF16), row(sg_g), w_sp, bsp_full, row(sg_out_g))
    sb = _attention(qkv, batch, seq).reshape(n, SB_WIDTH)

    pad_lanes = lambda v, width: jnp.pad(v, [(0, 0)] * (v.ndim - 1) + [(0, width - v.shape[-1])])
    w_r = jnp.concatenate([pad_lanes(w_rg, ROUTER_LANE0),
                           jnp.transpose(w_re, (1, 0, 2)).reshape(D_MODEL, N_EXPERTS)], axis=1)
    w_r = pad_lanes(w_r, LANES)
    wr_hi = w_r.astype(BF16)
    wr_lo = (w_r - wr_hi.astype(F32)).astype(BF16)
    wr2 = jnp.concatenate([wr_hi, wr_lo], axis=1)
    b_r = pad_lanes(jnp.concatenate([pad_lanes(b_rg, ROUTER_LANE0), b_re.reshape(-1)]), LANES)

    h, hn, lg = _mix(sb, sgn, x2, row(sb_g), w_out.astype(BF16), row(ffn_g), wr2, row(b_r))
    ri, rw, cnt = _route(lg)

    counts = cnt[:, 0].astype(jnp.int32)
    n_rows = 2 * n + N_EXPERTS * TM_EXPERT
    tiles, offsets, n_tiles = _schedule(counts)
    expert, rank = ri[0:2], ri[2:4]
    is_e = expert[None] == jnp.arange(N_EXPERTS, dtype=jnp.int32)[:, None, None]
    dest = (jnp.sum(jnp.where(is_e, offsets[:, None, None], 0), axis=0) + rank).reshape(-1)
    pad_start = offsets + counts
    pad_count = (-counts) % TM_EXPERT

    xs = _dispatch(dest, pad_start, pad_count, n_tiles, hn, n_rows)
    ys = _experts(tiles, n_tiles, xs,
                  w_gate.reshape(N_EXPERTS, D_MODEL, D_EXPERT),
                  w_up.reshape(N_EXPERTS, D_MODEL, D_EXPERT),
                  w_down.reshape(N_EXPERTS, D_EXPERT, D_MODEL))
    return dest, h, rw, ys


def kernel(x, attn_norm_g, w_in, sg_norm_g, w_spatial, b_spatial, sb_out_norm_g, sg_out_norm_g,
           w_out, ffn_norm_g, w_router_group, b_router_group, w_router_expert, b_router_expert,
           w_gate, w_up, w_down, final_norm_g):
    assert attn_norm_g.shape[0] == 1, "single-layer problem"
    batch, seq, _ = x.shape
    dest, h, rw, ys = _layer(x, attn_norm_g[0], w_in[0], sg_norm_g[0], w_spatial[0], b_spatial[0],
                             sb_out_norm_g[0], sg_out_norm_g[0], w_out[0], ffn_norm_g[0],
                             w_router_group[0], b_router_group[0], w_router_expert[0],
                             b_router_expert[0], w_gate[0], w_up[0], w_down[0])
    out = _combine(dest, h, rw, final_norm_g.reshape(1, -1), ys)
    return out.reshape(batch, seq, D_MODEL)
```

```python
import functools
import math

import jax
import jax.numpy as jnp
from jax import lax
from jax.experimental import pallas as pl
from jax.experimental.pallas import tpu as pltpu

D_MODEL = 1024
HEAD_DIM = 64
SB_WIDTH = 512
SG_WIDTH = 512
SG_HEADS = 8
D_IN = 3 * SB_WIDTH + 2 * SG_WIDTH
CHUNK = 128
N_GROUPS = 4
EXPERTS_PER_GROUP = 8
N_EXPERTS = N_GROUPS * EXPERTS_PER_GROUP
D_EXPERT = 512
EPS = 1e-6
F32_EXP_UNDERFLOW = 110.0

LANES = 128
SUBLANES = 8
ROW_TILE = D_MODEL // LANES
assert ROW_TILE == SUBLANES
HEAD_PAIR = 2 * HEAD_DIM
ROUTER_LANE0 = SUBLANES
ROUTER_ROWS = ROUTER_LANE0 + N_EXPERTS
assert EXPERTS_PER_GROUP == SUBLANES and N_GROUPS <= ROUTER_LANE0

TM_PROJ = 1024
TQ_ATTN = 256
TM_MIX = 1024
TM_ROUTE = 1024
TM_DISPATCH = 512
TM_EXPERT = 512
TM_COMBINE = 256
VMEM_LIMIT = 48 * 1024 * 1024

F32 = jnp.float32
BF16 = jnp.bfloat16


def _rms(x, g):
    return x * lax.rsqrt(jnp.mean(x * x, axis=-1, keepdims=True) + EPS) * g


def _gelu(x):
    c = math.sqrt(2.0 / math.pi)
    return x * (0.5 * (1.0 + jnp.tanh(c * (x + 0.044715 * (x * x * x)))))


def _softplus(z):
    return jnp.maximum(z, 0.0) + jnp.log(1.0 + jnp.exp(-jnp.abs(z)))


def _dot(a, b):
    return jnp.dot(a, b, preferred_element_type=F32)


def _rows_to_tiles(ref, x):
    m = x.shape[0]
    for k in range(ROW_TILE):
        ref[pl.ds(k, m, stride=ROW_TILE), :] = x[:, k * LANES:(k + 1) * LANES]


def _tiles_to_rows(ref, m):
    return jnp.concatenate([ref[pl.ds(k, m, stride=ROW_TILE), :] for k in range(ROW_TILE)], axis=1)


def _token_rows(ref, first_token, n_tokens):
    return ref.at[pl.ds(pl.multiple_of(first_token * ROW_TILE, ROW_TILE), n_tokens * ROW_TILE)]


def _split_bf16(x):
    hi = x.astype(BF16)
    lo = (x - hi.astype(F32)).astype(BF16)
    return hi, lo


def _inproj_kernel(x_ref, g_ref, w_ref, sgg_ref, wsp_ref, bsp_ref, sgog_ref, qkv_ref, sgn_ref,
                   gu_ref, vgn_ref, sg_ref):
    tm = TM_PROJ
    hb = _rms(x_ref[...], g_ref[...]).astype(BF16)
    q = _dot(hb, w_ref[:, 0:SB_WIDTH]) * (1.0 / math.sqrt(HEAD_DIM))
    qkv_ref[:, 0:SB_WIDTH] = q.astype(BF16)
    qkv_ref[:, SB_WIDTH:3 * SB_WIDTH] = _dot(hb, w_ref[:, SB_WIDTH:3 * SB_WIDTH]).astype(BF16)
    gu_ref[...] = _gelu(_dot(hb, w_ref[:, 3 * SB_WIDTH:3 * SB_WIDTH + SG_WIDTH]))
    gv = _gelu(_dot(hb, w_ref[:, 3 * SB_WIDTH + SG_WIDTH:D_IN]))
    vgn_ref[...] = _rms(gv, sgg_ref[...]).astype(BF16)

    lane = lax.broadcasted_iota(jnp.int32, (1, LANES), 1)
    first = lane < HEAD_DIM
    zero = jnp.zeros((), BF16)
    r_c = lax.broadcasted_iota(jnp.int32, (CHUNK, CHUNK), 0)
    c_c = lax.broadcasted_iota(jnp.int32, (CHUNK, CHUNK), 1)
    tril = r_c >= c_c
    n_pairs = SG_WIDTH // HEAD_PAIR
    w_pairs = []
    for p in range(n_pairs):
        w0 = jnp.where(tril, wsp_ref[2 * p], 0.0).astype(BF16)
        w1 = jnp.where(tril, wsp_ref[2 * p + 1], 0.0).astype(BF16)
        w_pairs.append(jnp.concatenate([w0, w1], axis=1))
    bsp = bsp_ref[...]
    for c in range(tm // CHUNK):
        rows = slice(c * CHUNK, (c + 1) * CHUNK)
        for p in range(n_pairs):
            cols = slice(p * HEAD_PAIR, (p + 1) * HEAD_PAIR)
            vg = vgn_ref[rows, cols]
            rhs = jnp.concatenate([jnp.where(first, vg, zero), jnp.where(first, zero, vg)], axis=0)
            mixed = _dot(w_pairs[p], rhs) + bsp[:, cols]
            sg_ref[rows, cols] = gu_ref[rows, cols] * mixed
    sgn_ref[...] = _rms(sg_ref[...], sgog_ref[...]).astype(BF16)


def _inproj(x2, attn_g, w_in_b, sg_g, wsp, bsp_full, sg_out_g):
    n = x2.shape[0]
    row = lambda i: (i, 0)
    const = lambda i: (0, 0)
    return pl.pallas_call(
        _inproj_kernel,
        grid=(n // TM_PROJ,),
        in_specs=[pl.BlockSpec((TM_PROJ, D_MODEL), row),
                  pl.BlockSpec((1, D_MODEL), const),
                  pl.BlockSpec((D_MODEL, D_IN), const),
                  pl.BlockSpec((1, SG_WIDTH), const),
                  pl.BlockSpec((SG_HEADS, CHUNK, CHUNK), lambda i: (0, 0, 0)),
                  pl.BlockSpec((CHUNK, SG_WIDTH), const),
                  pl.BlockSpec((1, SG_WIDTH), const)],
        out_specs=[pl.BlockSpec((TM_PROJ, 3 * SB_WIDTH), row),
                   pl.BlockSpec((TM_PROJ, SG_WIDTH), row)],
        out_shape=[jax.ShapeDtypeStruct((n, 3 * SB_WIDTH), BF16),
                   jax.ShapeDtypeStruct((n, SG_WIDTH), BF16)],
        scratch_shapes=[pltpu.VMEM((TM_PROJ, SG_WIDTH), F32),
                        pltpu.VMEM((TM_PROJ, SG_WIDTH), BF16),
                        pltpu.VMEM((TM_PROJ, SG_WIDTH), F32)],
        compiler_params=pltpu.CompilerParams(dimension_semantics=("arbitrary",),
                                             vmem_limit_bytes=VMEM_LIMIT),
        name="inproj",
    )(x2, attn_g, w_in_b, sg_g, wsp, bsp_full, sg_out_g)


def _attn_kernel(q_ref, k_ref, v_ref, o_ref, q2_ref, carry_ref):
    t = TQ_ATTN
    n_pairs = SB_WIDTH // HEAD_PAIR
    qi = pl.program_id(1)
    lane = lax.broadcasted_iota(jnp.int32, (1, HEAD_PAIR), 1)
    head_lanes = (lane < HEAD_DIM, lane >= HEAD_DIM)
    zero = jnp.zeros((), BF16)
    for p in range(n_pairs):
        qp = q_ref[0, :, p * HEAD_PAIR:(p + 1) * HEAD_PAIR]
        for h in range(2):
            q2_ref[(2 * p + h) * t:(2 * p + h + 1) * t, :] = jnp.where(head_lanes[h], qp, zero)
    r_idx = lax.broadcasted_iota(jnp.int32, (t, t), 0)
    c_idx = lax.broadcasted_iota(jnp.int32, (t, t), 1)
    suffix = (r_idx > c_idx).astype(BF16)
    suffix2 = jnp.concatenate([suffix, suffix], axis=0)
    causal = c_idx < r_idx

    o_ref[...] = jnp.zeros_like(o_ref)
    carry_ref[...] = jnp.zeros_like(carry_ref)

    def block(j, diag):
        start = pl.multiple_of(j * t, t)
        for p in range(n_pairs):
            cols = slice(p * HEAD_PAIR, (p + 1) * HEAD_PAIR)
            rows = slice(2 * p * t, (2 * p + 2) * t)
            kb = k_ref[0, pl.ds(start, t), cols]
            vb = v_ref[0, pl.ds(start, t), cols]
            z = lax.dot_general(q2_ref[rows, :], kb, (((1,), (1,)), ((), ())),
                                preferred_element_type=F32)
            sp = _softplus(z)
            if diag:
                mask2 = jnp.concatenate([causal, causal], axis=0)
                nl = jnp.where(mask2, sp, 0.0)
            else:
                nl = sp
            hi, lo = _split_bf16(nl)
            hl = jnp.concatenate([hi, lo], axis=1)
            after = jnp.concatenate([_dot(hl[0:t], suffix2), _dot(hl[t:2 * t], suffix2)], axis=0)
            carry = carry_ref[rows, :]
            a = jnp.exp(z - sp - after - carry)
            if diag:
                a = jnp.where(mask2, a, 0.0)
            a = a.astype(BF16)
            a2 = jnp.concatenate([a[0:t], a[t:2 * t]], axis=1)
            v2 = jnp.concatenate([jnp.where(head_lanes[0], vb, zero),
                                  jnp.where(head_lanes[1], vb, zero)], axis=0)
            o_ref[0, :, cols] += _dot(a2, v2)
            carry_ref[rows, :] = carry + after[:, 0:1] + nl[:, 0:1]

    def live():
        return jnp.min(carry_ref[...]) < F32_EXP_UNDERFLOW

    block(qi, True)

    def body(state):
        it, _ = state
        block(qi - 1 - it, False)
        return it + 1, live()

    lax.while_loop(lambda s: (s[0] < qi) & s[1], body, (jnp.int32(0), live()))


def _attention(qkv, batch, seq):
    qkv3 = qkv.reshape(batch, seq, 3 * SB_WIDTH)
    n_heads = SB_WIDTH // HEAD_DIM
    return pl.pallas_call(
        _attn_kernel,
        grid=(batch, seq // TQ_ATTN),
        in_specs=[pl.BlockSpec((1, TQ_ATTN, SB_WIDTH), lambda b, i: (b, i, 0)),
                  pl.BlockSpec((1, seq, SB_WIDTH), lambda b, i: (b, 0, 1)),
                  pl.BlockSpec((1, seq, SB_WIDTH), lambda b, i: (b, 0, 2))],
        out_specs=pl.BlockSpec((1, TQ_ATTN, SB_WIDTH), lambda b, i: (b, i, 0)),
        out_shape=jax.ShapeDtypeStruct((batch, seq, SB_WIDTH), F32),
        scratch_shapes=[pltpu.VMEM((n_heads * TQ_ATTN, HEAD_PAIR), BF16),
                        pltpu.VMEM((n_heads * TQ_ATTN, 1), F32)],
        compiler_params=pltpu.CompilerParams(dimension_semantics=("arbitrary",) * 2,
                                             vmem_limit_bytes=VMEM_LIMIT),
        name="sb_attention",
    )(qkv3, qkv3, qkv3)


def _mix_kernel(sb_ref, sgn_ref, x_ref, sbg_ref, wout_ref, ffng_ref, wr2_ref, br_ref,
                h_ref, lg_ref):
    sbn = _rms(sb_ref[...], sbg_ref[...]).astype(BF16)
    h = x_ref[...] + _dot(sbn, wout_ref[0:SB_WIDTH, :]) + _dot(sgn_ref[...], wout_ref[SB_WIDTH:, :])
    h_ref[...] = h
    hn = _rms(h, ffng_ref[...])

    hn_hi, hn_lo = _split_bf16(hn)
    both = _dot(hn_hi, wr2_ref[...])
    logits = both[:, 0:LANES] + both[:, LANES:] + _dot(hn_lo, wr2_ref[:, 0:LANES]) + br_ref[...]
    lg_ref[...] = logits.T[0:ROUTER_ROWS, :]


def _route_kernel(lg_ref, ri_ref, rw_ref, cnt_ref, count_ref):
    tr = TM_ROUTE
    i = pl.program_id(0)

    @pl.when(i == 0)
    def _():
        count_ref[...] = jnp.zeros_like(count_ref)

    neg = jnp.float32(-jnp.inf)
    row8 = lax.broadcasted_iota(jnp.int32, (SUBLANES, tr), 0)

    def top(v):
        m = jnp.max(v, axis=0, keepdims=True)
        return m, jnp.min(jnp.where(v == m, row8, SUBLANES), axis=0, keepdims=True)

    def group_rows(g):
        return lg_ref[ROUTER_LANE0 + g * EXPERTS_PER_GROUP:ROUTER_LANE0 + (g + 1) * EXPERTS_PER_GROUP, :]

    gl = jnp.where(row8 < N_GROUPS, lg_ref[0:SUBLANES, :], neg)
    gmax, gidx = top(gl)
    gweight = 1.0 / jnp.sum(jnp.exp(gl - gmax), axis=0, keepdims=True)
    el = group_rows(0)
    for g in range(1, N_GROUPS):
        el = jnp.where(gidx == g, group_rows(g), el)
    m1, i1 = top(el)
    m2, i2 = top(jnp.where(row8 == i1, neg, el))
    t21 = jnp.exp(m2 - m1)
    w1 = gweight / (1.0 + t21)
    w2 = gweight * t21 / (1.0 + t21)
    e1 = gidx * EXPERTS_PER_GROUP + i1
    e2 = gidx * EXPERTS_PER_GROUP + i2

    row_e = lax.broadcasted_iota(jnp.int32, (N_EXPERTS, tr), 0)
    sel1 = row_e == e1
    sel2 = row_e == e2
    onehot = jnp.where(sel1 | sel2, 1.0, 0.0)
    r_t = lax.broadcasted_iota(jnp.int32, (tr, tr), 0)
    c_t = lax.broadcasted_iota(jnp.int32, (tr, tr), 1)
    before = (r_t < c_t).astype(BF16)
    running = count_ref[:, 0:1] + _dot(onehot.astype(BF16), before)
    rank1 = jnp.sum(jnp.where(sel1, running, 0.0), axis=0, keepdims=True)
    rank2 = jnp.sum(jnp.where(sel2, running, 0.0), axis=0, keepdims=True)
    new_count = count_ref[:, 0:1] + jnp.sum(onehot, axis=1, keepdims=True)
    count_ref[...] = jnp.broadcast_to(new_count, count_ref.shape)
    cnt_ref[...] = jnp.broadcast_to(new_count, cnt_ref.shape)

    ri_ref[...] = jnp.where(row8 == 0, e1, jnp.where(row8 == 1, e2, jnp.where(
        row8 == 2, rank1.astype(jnp.int32), jnp.where(row8 == 3, rank2.astype(jnp.int32), 0))))
    row128 = lax.broadcasted_iota(jnp.int32, (LANES, tr), 0)
    rw_ref[...] = jnp.where(row128 == 0, w1, jnp.where(row128 == 1, w2, 0.0)).T


def _route(lg):
    n = lg.shape[1]
    return pl.pallas_call(
        _route_kernel,
        grid=(n // TM_ROUTE,),
        in_specs=[pl.BlockSpec((ROUTER_ROWS, TM_ROUTE), lambda i: (0, i))],
        out_specs=[pl.BlockSpec((SUBLANES, TM_ROUTE), lambda i: (0, i)),
                   pl.BlockSpec((TM_ROUTE, LANES), lambda i: (i, 0)),
                   pl.BlockSpec((N_EXPERTS, LANES), lambda i: (0, 0))],
        out_shape=[jax.ShapeDtypeStruct((SUBLANES, n), jnp.int32),
                   jax.ShapeDtypeStruct((n, LANES), F32),
                   jax.ShapeDtypeStruct((N_EXPERTS, LANES), F32)],
        scratch_shapes=[pltpu.VMEM((N_EXPERTS, LANES), F32)],
        compiler_params=pltpu.CompilerParams(dimension_semantics=("arbitrary",),
                                             vmem_limit_bytes=VMEM_LIMIT),
        name="route",
    )(lg)


def _mix(sb, sgn, x2, sb_g, w_out_b, ffn_g, wr2, br):
    n = x2.shape[0]
    row = lambda i: (i, 0)
    const = lambda i: (0, 0)
    return pl.pallas_call(
        _mix_kernel,
        grid=(n // TM_MIX,),
        in_specs=[pl.BlockSpec((TM_MIX, SB_WIDTH), row),
                  pl.BlockSpec((TM_MIX, SG_WIDTH), row),
                  pl.BlockSpec((TM_MIX, D_MODEL), row),
                  pl.BlockSpec((1, SB_WIDTH), const),
                  pl.BlockSpec((D_MODEL, D_MODEL), const),
                  pl.BlockSpec((1, D_MODEL), const),
                  pl.BlockSpec((D_MODEL, 2 * LANES), const),
                  pl.BlockSpec((1, LANES), const)],
        out_specs=[pl.BlockSpec((TM_MIX, D_MODEL), row),
                   pl.BlockSpec((ROUTER_ROWS, TM_MIX), lambda i: (0, i))],
        out_shape=[jax.ShapeDtypeStruct((n, D_MODEL), F32),
                   jax.ShapeDtypeStruct((ROUTER_ROWS, n), F32)],
        compiler_params=pltpu.CompilerParams(dimension_semantics=("arbitrary",),
                                             vmem_limit_bytes=VMEM_LIMIT),
        name="mix_router",
    )(sb, sgn, x2, sb_g, w_out_b, ffn_g, wr2, br)


_PAD_BITS = tuple(1 << b for b in reversed(range(TM_EXPERT.bit_length() - 1)))


def _dispatch_kernel(dest_ref, pad_start_ref, pad_count_ref, nt_ref, h_ref, g_ref, zeros_ref, xs_ref,
                     hn_ref, sem, zsem):
    tm = TM_DISPATCH
    i = pl.program_id(0)
    n = pl.num_programs(0) * tm
    base = i * tm
    n_tiles_max = xs_ref.shape[0] // (TM_EXPERT * ROW_TILE)
    _rows_to_tiles(hn_ref, _rms(h_ref[...], g_ref[...]))

    def pad_copies(do):
        for e in range(N_EXPERTS):
            start = pad_start_ref[e]
            count = pad_count_ref[e]
            for bit in _PAD_BITS:
                @pl.when((count & bit) != 0)
                def _(start=start, bit=bit):
                    do(pltpu.make_async_copy(_token_rows(zeros_ref, 0, bit),
                                             _token_rows(xs_ref, start, bit), zsem))
                start = start + (count & bit)
        for k in range(N_EXPERTS):
            tile = nt_ref[0] + k

            @pl.when(tile < n_tiles_max)
            def _(tile=tile):
                do(pltpu.make_async_copy(zeros_ref, _token_rows(xs_ref, tile * TM_EXPERT, TM_EXPERT), zsem))

    @pl.when(i == 0)
    def _():
        pad_copies(lambda cp: cp.start())

    def body(r, c):
        src = _token_rows(hn_ref, r, 1)
        for s in range(2):
            pltpu.make_async_copy(src, _token_rows(xs_ref, dest_ref[s * n + base + r], 1),
                                  sem).start(priority=s)
        return c

    lax.fori_loop(0, tm, body, 0, unroll=8)
    for _ in range(2):
        pltpu.make_async_copy(hn_ref, _token_rows(xs_ref, 0, tm), sem).wait()

    @pl.when(i == pl.num_programs(0) - 1)
    def _():
        pad_copies(lambda cp: cp.wait())


def _dispatch(dest, pad_start, pad_count, n_tiles, h, ffn_g, n_rows):
    n = h.shape[0]
    zeros = jnp.zeros((TM_EXPERT * ROW_TILE, LANES), F32)
    return pl.pallas_call(
        _dispatch_kernel,
        grid_spec=pltpu.PrefetchScalarGridSpec(
            num_scalar_prefetch=4,
            grid=(n // TM_DISPATCH,),
            in_specs=[pl.BlockSpec((TM_DISPATCH, D_MODEL), lambda i, *_: (i, 0)),
                      pl.BlockSpec((1, D_MODEL), lambda i, *_: (0, 0)),
                      pl.BlockSpec(memory_space=pl.ANY)],
            out_specs=pl.BlockSpec(memory_space=pl.ANY),
            scratch_shapes=[pltpu.VMEM((TM_DISPATCH * ROW_TILE, LANES), F32),
                            pltpu.SemaphoreType.DMA, pltpu.SemaphoreType.DMA]),
        out_shape=jax.ShapeDtypeStruct((n_rows * ROW_TILE, LANES), F32),
        compiler_params=pltpu.CompilerParams(dimension_semantics=("arbitrary",),
                                             vmem_limit_bytes=VMEM_LIMIT),
        name="dispatch",
    )(dest, pad_start, pad_count, n_tiles, h, ffn_g, zeros)


X_SLOTS = 3


def _expert_kernel(tiles_ref, nt_ref, xs_ref, wg_ref, wu_ref, wd_ref, y_ref,
                   x_buf, sg_buf, su_buf, sd_buf, wgb, wub, wdb, state, w_sems, x_sems):
    tm = TM_EXPERT
    t = pl.program_id(0)
    nt = nt_ref[0]

    def x_copy(tile):
        slot = lax.rem(tile, X_SLOTS)
        return pltpu.make_async_copy(_token_rows(xs_ref, tile * tm, tm), x_buf.at[slot], x_sems.at[slot])

    def weight_copies(e, slot):
        return (pltpu.make_async_copy(wg_ref.at[e], sg_buf.at[slot], w_sems.at[slot]),
                pltpu.make_async_copy(wu_ref.at[e], su_buf.at[slot], w_sems.at[slot]),
                pltpu.make_async_copy(wd_ref.at[e], sd_buf.at[slot], w_sems.at[slot]))

    def next_with_rows(e):
        return lax.while_loop(lambda k: (k < N_EXPERTS) & (tiles_ref[jnp.minimum(k, N_EXPERTS - 1)] == 0),
                              lambda k: k + 1, e + 1)

    @pl.when(t == 0)
    def _():
        first = next_with_rows(jnp.int32(-1))
        state[0] = jnp.int32(-1)
        state[1] = jnp.int32(0)
        state[2] = jnp.int32(1)
        state[3] = first
        for cp in weight_copies(first, 0):
            cp.start()
        x_copy(0).start()

        @pl.when(nt > 1)
        def _():
            x_copy(1).start()

    @pl.when(t + 2 < nt)
    def _():
        x_copy(t + 2).start()

    @pl.when(t < nt)
    def _():
        @pl.when(state[1] == 0)
        def _():
            e = state[3]
            slot = 1 - state[2]
            nxt = next_with_rows(e)
            state[0] = e
            state[1] = tiles_ref[e]
            state[2] = slot
            state[3] = nxt
            for cp in weight_copies(e, slot):
                cp.wait()

            @pl.when(nxt < N_EXPERTS)
            def _():
                for cp in weight_copies(nxt, 1 - slot):
                    cp.start()

            wgb[...] = sg_buf[slot].astype(BF16)
            wub[...] = su_buf[slot].astype(BF16)
            wdb[...] = sd_buf[slot].astype(BF16)

        state[1] = state[1] - 1
        x_copy(t).wait()
        x = _tiles_to_rows(x_buf.at[lax.rem(t, X_SLOTS)], tm).astype(BF16)
        g = _dot(x, wgb[...])
        u = _dot(x, wub[...])
        hidden = (g * jax.nn.sigmoid(g)) * u
        _rows_to_tiles(y_ref, _dot(hidden.astype(BF16), wdb[...]))

    @pl.when(t >= nt)
    def _():
        y_ref[...] = jnp.zeros_like(y_ref)


def _experts(tiles, n_tiles, xs, wg, wu, wd):
    n_rows = xs.shape[0] // ROW_TILE
    any_spec = pl.BlockSpec(memory_space=pl.ANY)
    return pl.pallas_call(
        _expert_kernel,
        grid_spec=pltpu.PrefetchScalarGridSpec(
            num_scalar_prefetch=2,
            grid=(n_rows // TM_EXPERT,),
            in_specs=[any_spec, any_spec, any_spec, any_spec],
            out_specs=pl.BlockSpec((TM_EXPERT * ROW_TILE, LANES), lambda t, *_: (t, 0)),
            scratch_shapes=[pltpu.VMEM((X_SLOTS, TM_EXPERT * ROW_TILE, LANES), F32),
                            pltpu.VMEM((2, D_MODEL, D_EXPERT), F32),
                            pltpu.VMEM((2, D_MODEL, D_EXPERT), F32),
                            pltpu.VMEM((2, D_EXPERT, D_MODEL), F32),
                            pltpu.VMEM((D_MODEL, D_EXPERT), BF16),
                            pltpu.VMEM((D_MODEL, D_EXPERT), BF16),
                            pltpu.VMEM((D_EXPERT, D_MODEL), BF16),
                            pltpu.SMEM((4,), jnp.int32),
                            pltpu.SemaphoreType.DMA((2,)),
                            pltpu.SemaphoreType.DMA((X_SLOTS,))]),
        out_shape=jax.ShapeDtypeStruct((n_rows * ROW_TILE, LANES), F32),
        compiler_params=pltpu.CompilerParams(dimension_semantics=("arbitrary",),
                                             vmem_limit_bytes=VMEM_LIMIT),
        name="expert_mlp",
    )(tiles, n_tiles, xs, wg, wu, wd)


def _combine_kernel(dest_ref, h_ref, rw_ref, fg_ref, y_ref, o_ref, buf, sems):
    tm = TM_COMBINE
    i = pl.program_id(0)
    n_steps = pl.num_programs(0)
    n = n_steps * tm
    cur = i % 2

    def fetch(step, half):
        def body(r, c):
            for s in range(2):
                pltpu.make_async_copy(_token_rows(y_ref, dest_ref[s * n + step * tm + r], 1),
                                      _token_rows(buf.at[half, s], r, 1),
                                      sems.at[half]).start(priority=s)
            return c

        lax.fori_loop(0, tm, body, 0, unroll=8)

    @pl.when(i == 0)
    def _():
        fetch(0, 0)

    @pl.when(i + 1 < n_steps)
    def _():
        fetch(i + 1, 1 - cur)

    for s in range(2):
        pltpu.make_async_copy(_token_rows(y_ref, 0, tm), buf.at[cur, s], sems.at[cur]).wait()
    rw = rw_ref[...]
    out = (h_ref[...] + rw[:, 0:1] * _tiles_to_rows(buf.at[cur, 0], tm)
           + rw[:, 1:2] * _tiles_to_rows(buf.at[cur, 1], tm))
    o_ref[...] = _rms(out, fg_ref[...])


def _combine(dest, h, rw, final_g, ys):
    n = h.shape[0]
    return pl.pallas_call(
        _combine_kernel,
        grid_spec=pltpu.PrefetchScalarGridSpec(
            num_scalar_prefetch=1,
            grid=(n // TM_COMBINE,),
            in_specs=[pl.BlockSpec((TM_COMBINE, D_MODEL), lambda i, d: (i, 0)),
                      pl.BlockSpec((TM_COMBINE, LANES), lambda i, d: (i, 0)),
                      pl.BlockSpec((1, D_MODEL), lambda i, d: (0, 0)),
                      pl.BlockSpec(memory_space=pl.ANY)],
            out_specs=pl.BlockSpec((TM_COMBINE, D_MODEL), lambda i, d: (i, 0)),
            scratch_shapes=[pltpu.VMEM((2, 2, TM_COMBINE * ROW_TILE, LANES), F32),
                            pltpu.SemaphoreType.DMA((2,))]),
        out_shape=jax.ShapeDtypeStruct((n, D_MODEL), F32),
        compiler_params=pltpu.CompilerParams(dimension_semantics=("arbitrary",),
                                             vmem_limit_bytes=VMEM_LIMIT),
        name="combine",
    )(dest, h, rw, final_g, ys)


def _schedule(counts):
    tiles = (counts + TM_EXPERT - 1) // TM_EXPERT
    tile_end = jnp.cumsum(tiles)
    offsets = (tile_end - tiles) * TM_EXPERT
    return tiles, offsets, tile_end[-1:]


def _layer(x, attn_g, w_in, sg_g, w_sp, b_sp, sb_g, sg_out_g, w_out, ffn_g,
           w_rg, b_rg, w_re, b_re, w_gate, w_up, w_down):
    batch, seq, _ = x.shape
    n = batch * seq
    x2 = x.reshape(n, D_MODEL)
    row = lambda v: v.reshape(1, -1)

    bsp_full = jnp.repeat(b_sp.T, HEAD_DIM, axis=1)
    qkv, sgn = _inproj(x2, row(attn_g), w_in.astype(BF16), row(sg_g), w_sp, bsp_full, row(sg_out_g))
    sb = _attention(qkv, batch, seq).reshape(n, SB_WIDTH)

    pad_lanes = lambda v, width: jnp.pad(v, [(0, 0)] * (v.ndim - 1) + [(0, width - v.shape[-1])])
    w_r = jnp.concatenate([pad_lanes(w_rg, ROUTER_LANE0),
                           jnp.transpose(w_re, (1, 0, 2)).reshape(D_MODEL, N_EXPERTS)], axis=1)
    w_r = pad_lanes(w_r, LANES)
    wr_hi = w_r.astype(BF16)
    wr_lo = (w_r - wr_hi.astype(F32)).astype(BF16)
    wr2 = jnp.concatenate([wr_hi, wr_lo], axis=1)
    b_r = pad_lanes(jnp.concatenate([pad_lanes(b_rg, ROUTER_LANE0), b_re.reshape(-1)]), LANES)

    h, lg = _mix(sb, sgn, x2, row(sb_g), w_out.astype(BF16), row(ffn_g), wr2, row(b_r))
    ri, rw, cnt = _route(lg)

    counts = cnt[:, 0].astype(jnp.int32)
    n_rows = 2 * n + N_EXPERTS * TM_EXPERT
    tiles, offsets, n_tiles = _schedule(counts)
    expert, rank = ri[0:2], ri[2:4]
    is_e = expert[None] == jnp.arange(N_EXPERTS, dtype=jnp.int32)[:, None, None]
    dest = (jnp.sum(jnp.where(is_e, offsets[:, None, None], 0), axis=0) + rank).reshape(-1)
    pad_start = offsets + counts
    pad_count = (-counts) % TM_EXPERT

    xs = _dispatch(dest, pad_start, pad_count, n_tiles, h, row(ffn_g), n_rows)
    ys = _experts(tiles, n_tiles, xs,
                  w_gate.reshape(N_EXPERTS, D_MODEL, D_EXPERT),
                  w_up.reshape(N_EXPERTS, D_MODEL, D_EXPERT),
                  w_down.reshape(N_EXPERTS, D_EXPERT, D_MODEL))
    return dest, h, rw, ys


def kernel(x, attn_norm_g, w_in, sg_norm_g, w_spatial, b_spatial, sb_out_norm_g, sg_out_norm_g,
           w_out, ffn_norm_g, w_router_group, b_router_group, w_router_expert, b_router_expert,
           w_gate, w_up, w_down, final_norm_g):
    assert attn_norm_g.shape[0] == 1, "single-layer problem"
    batch, seq, _ = x.shape
    dest, h, rw, ys = _layer(x, attn_norm_g[0], w_in[0], sg_norm_g[0], w_spatial[0], b_spatial[0],
                             sb_out_norm_g[0], sg_out_norm_g[0], w_out[0], ffn_norm_g[0],
                             w_router_group[0], b_router_group[0], w_router_expert[0],
                             b_router_expert[0], w_gate[0], w_up[0], w_down[0])
    out = _combine(dest, h, rw, final_norm_g.reshape(1, -1), ys)
    return out.reshape(batch, seq, D_MODEL)
```

```python
import functools
import math

import jax
import jax.numpy as jnp
from jax import lax
from jax.experimental import pallas as pl
from jax.experimental.pallas import tpu as pltpu

D_MODEL = 1024
HEAD_DIM = 64
SB_WIDTH = 512
SG_WIDTH = 512
SG_HEADS = 8
D_IN = 3 * SB_WIDTH + 2 * SG_WIDTH
CHUNK = 128
N_GROUPS = 4
EXPERTS_PER_GROUP = 8
N_EXPERTS = N_GROUPS * EXPERTS_PER_GROUP
D_EXPERT = 512
EPS = 1e-6
F32_EXP_UNDERFLOW = 110.0

LANES = 128
SUBLANES = 8
ROW_TILE = D_MODEL // LANES
assert ROW_TILE == SUBLANES
HEAD_PAIR = 2 * HEAD_DIM
ROUTER_LANE0 = SUBLANES
ROUTER_ROWS = ROUTER_LANE0 + N_EXPERTS
assert EXPERTS_PER_GROUP == SUBLANES and N_GROUPS <= ROUTER_LANE0

TM_PROJ = 1024
TQ_ATTN = 256
TM_MIX = 1024
TM_ROUTE = 1024
TM_DISPATCH = 512
TM_EXPERT = 512
TM_COMBINE = 256
VMEM_LIMIT = 48 * 1024 * 1024

F32 = jnp.float32
BF16 = jnp.bfloat16


def _rms(x, g):
    return x * lax.rsqrt(jnp.mean(x * x, axis=-1, keepdims=True) + EPS) * g


def _gelu(x):
    c = math.sqrt(2.0 / math.pi)
    return x * (0.5 * (1.0 + jnp.tanh(c * (x + 0.044715 * (x * x * x)))))


def _softplus(z):
    return jnp.maximum(z, 0.0) + jnp.log(1.0 + jnp.exp(-jnp.abs(z)))


def _dot(a, b):
    return jnp.dot(a, b, preferred_element_type=F32)


def _rows_to_tiles(ref, x):
    m = x.shape[0]
    for k in range(ROW_TILE):
        ref[pl.ds(k, m, stride=ROW_TILE), :] = x[:, k * LANES:(k + 1) * LANES]


def _tiles_to_rows(ref, m):
    return jnp.concatenate([ref[pl.ds(k, m, stride=ROW_TILE), :] for k in range(ROW_TILE)], axis=1)


def _token_rows(ref, first_token, n_tokens):
    return ref.at[pl.ds(pl.multiple_of(first_token * ROW_TILE, ROW_TILE), n_tokens * ROW_TILE)]


def _split_bf16(x):
    hi = x.astype(BF16)
    lo = (x - hi.astype(F32)).astype(BF16)
    return hi, lo


def _inproj_kernel(x_ref, g_ref, w_ref, sgg_ref, wsp_ref, bsp_ref, sgog_ref, qkv_ref, sgn_ref,
                   gu_ref, vgn_ref, sg_ref):
    tm = TM_PROJ
    hb = _rms(x_ref[...], g_ref[...]).astype(BF16)
    gv = _gelu(_dot(hb, w_ref[:, 3 * SB_WIDTH + SG_WIDTH:D_IN]))
    vgn_ref[...] = _rms(gv, sgg_ref[...]).astype(BF16)
    gu_ref[...] = _gelu(_dot(hb, w_ref[:, 3 * SB_WIDTH:3 * SB_WIDTH + SG_WIDTH]))
    q = _dot(hb, w_ref[:, 0:SB_WIDTH]) * (1.0 / math.sqrt(HEAD_DIM))
    qkv_ref[:, 0:SB_WIDTH] = q.astype(BF16)
    qkv_ref[:, SB_WIDTH:2 * SB_WIDTH] = _dot(hb, w_ref[:, SB_WIDTH:2 * SB_WIDTH]).astype(BF16)

    lane = lax.broadcasted_iota(jnp.int32, (1, LANES), 1)
    first = lane < HEAD_DIM
    zero = jnp.zeros((), BF16)
    r_c = lax.broadcasted_iota(jnp.int32, (CHUNK, CHUNK), 0)
    c_c = lax.broadcasted_iota(jnp.int32, (CHUNK, CHUNK), 1)
    tril = r_c >= c_c
    n_pairs = SG_WIDTH // HEAD_PAIR
    w_pairs = []
    for p in range(n_pairs):
        w0 = jnp.where(tril, wsp_ref[2 * p], 0.0).astype(BF16)
        w1 = jnp.where(tril, wsp_ref[2 * p + 1], 0.0).astype(BF16)
        w_pairs.append(jnp.concatenate([w0, w1], axis=1))
    bsp = bsp_ref[...]
    for c in range(tm // CHUNK):
        rows = slice(c * CHUNK, (c + 1) * CHUNK)
        for p in range(n_pairs):
            cols = slice(p * HEAD_PAIR, (p + 1) * HEAD_PAIR)
            vg = vgn_ref[rows, cols]
            rhs = jnp.concatenate([jnp.where(first, vg, zero), jnp.where(first, zero, vg)], axis=0)
            mixed = _dot(w_pairs[p], rhs) + bsp[:, cols]
            sg_ref[rows, cols] = gu_ref[rows, cols] * mixed
    qkv_ref[:, 2 * SB_WIDTH:3 * SB_WIDTH] = _dot(hb, w_ref[:, 2 * SB_WIDTH:3 * SB_WIDTH]).astype(BF16)
    sgn_ref[...] = _rms(sg_ref[...], sgog_ref[...]).astype(BF16)


def _inproj(x2, attn_g, w_in_b, sg_g, wsp, bsp_full, sg_out_g):
    n = x2.shape[0]
    row = lambda i: (i, 0)
    const = lambda i: (0, 0)
    return pl.pallas_call(
        _inproj_kernel,
        grid=(n // TM_PROJ,),
        in_specs=[pl.BlockSpec((TM_PROJ, D_MODEL), row),
                  pl.BlockSpec((1, D_MODEL), const),
                  pl.BlockSpec((D_MODEL, D_IN), const),
                  pl.BlockSpec((1, SG_WIDTH), const),
                  pl.BlockSpec((SG_HEADS, CHUNK, CHUNK), lambda i: (0, 0, 0)),
                  pl.BlockSpec((CHUNK, SG_WIDTH), const),
                  pl.BlockSpec((1, SG_WIDTH), const)],
        out_specs=[pl.BlockSpec((TM_PROJ, 3 * SB_WIDTH), row),
                   pl.BlockSpec((TM_PROJ, SG_WIDTH), row)],
        out_shape=[jax.ShapeDtypeStruct((n, 3 * SB_WIDTH), BF16),
                   jax.ShapeDtypeStruct((n, SG_WIDTH), BF16)],
        scratch_shapes=[pltpu.VMEM((TM_PROJ, SG_WIDTH), F32),
                        pltpu.VMEM((TM_PROJ, SG_WIDTH), BF16),
                        pltpu.VMEM((TM_PROJ, SG_WIDTH), F32)],
        compiler_params=pltpu.CompilerParams(dimension_semantics=("arbitrary",),
                                             vmem_limit_bytes=VMEM_LIMIT),
        name="inproj",
    )(x2, attn_g, w_in_b, sg_g, wsp, bsp_full, sg_out_g)


def _attn_kernel(q_ref, k_ref, v_ref, o_ref, q2_ref, carry_ref):
    t = TQ_ATTN
    n_pairs = SB_WIDTH // HEAD_PAIR
    qi = pl.program_id(1)
    lane = lax.broadcasted_iota(jnp.int32, (1, HEAD_PAIR), 1)
    head_lanes = (lane < HEAD_DIM, lane >= HEAD_DIM)
    zero = jnp.zeros((), BF16)
    for p in range(n_pairs):
        qp = q_ref[0, :, p * HEAD_PAIR:(p + 1) * HEAD_PAIR]
        for h in range(2):
            q2_ref[(2 * p + h) * t:(2 * p + h + 1) * t, :] = jnp.where(head_lanes[h], qp, zero)
    r_idx = lax.broadcasted_iota(jnp.int32, (t, t), 0)
    c_idx = lax.broadcasted_iota(jnp.int32, (t, t), 1)
    suffix = (r_idx > c_idx).astype(BF16)
    suffix2 = jnp.concatenate([suffix, suffix], axis=0)
    causal = c_idx < r_idx

    o_ref[...] = jnp.zeros_like(o_ref)
    carry_ref[...] = jnp.zeros_like(carry_ref)

    def block(j, diag):
        start = pl.multiple_of(j * t, t)
        for p in range(n_pairs):
            cols = slice(p * HEAD_PAIR, (p + 1) * HEAD_PAIR)
            rows = slice(2 * p * t, (2 * p + 2) * t)
            kb = k_ref[0, pl.ds(start, t), cols]
            vb = v_ref[0, pl.ds(start, t), cols]
            z = lax.dot_general(q2_ref[rows, :], kb, (((1,), (1,)), ((), ())),
                                preferred_element_type=F32)
            sp = _softplus(z)
            if diag:
                mask2 = jnp.concatenate([causal, causal], axis=0)
                nl = jnp.where(mask2, sp, 0.0)
            else:
                nl = sp
            hi, lo = _split_bf16(nl)
            hl = jnp.concatenate([hi, lo], axis=1)
            after = jnp.concatenate([_dot(hl[0:t], suffix2), _dot(hl[t:2 * t], suffix2)], axis=0)
            carry = carry_ref[rows, :]
            a = jnp.exp(z - sp - after - carry)
            if diag:
                a = jnp.where(mask2, a, 0.0)
            a = a.astype(BF16)
            a2 = jnp.concatenate([a[0:t], a[t:2 * t]], axis=1)
            v2 = jnp.concatenate([jnp.where(head_lanes[0], vb, zero),
                                  jnp.where(head_lanes[1], vb, zero)], axis=0)
            o_ref[0, :, cols] += _dot(a2, v2)
            carry_ref[rows, :] = carry + after[:, 0:1] + nl[:, 0:1]

    def live():
        return jnp.min(carry_ref[...]) < F32_EXP_UNDERFLOW

    block(qi, True)

    def body(state):
        it, _ = state
        block(qi - 1 - it, False)
        return it + 1, live()

    lax.while_loop(lambda s: (s[0] < qi) & s[1], body, (jnp.int32(0), live()))


def _attention(qkv, batch, seq):
    qkv3 = qkv.reshape(batch, seq, 3 * SB_WIDTH)
    n_heads = SB_WIDTH // HEAD_DIM
    return pl.pallas_call(
        _attn_kernel,
        grid=(batch, seq // TQ_ATTN),
        in_specs=[pl.BlockSpec((1, TQ_ATTN, SB_WIDTH), lambda b, i: (b, i, 0)),
                  pl.BlockSpec((1, seq, SB_WIDTH), lambda b, i: (b, 0, 1)),
                  pl.BlockSpec((1, seq, SB_WIDTH), lambda b, i: (b, 0, 2))],
        out_specs=pl.BlockSpec((1, TQ_ATTN, SB_WIDTH), lambda b, i: (b, i, 0)),
        out_shape=jax.ShapeDtypeStruct((batch, seq, SB_WIDTH), F32),
        scratch_shapes=[pltpu.VMEM((n_heads * TQ_ATTN, HEAD_PAIR), BF16),
                        pltpu.VMEM((n_heads * TQ_ATTN, 1), F32)],
        compiler_params=pltpu.CompilerParams(dimension_semantics=("arbitrary",) * 2,
                                             vmem_limit_bytes=VMEM_LIMIT),
        name="sb_attention",
    )(qkv3, qkv3, qkv3)


def _mix_kernel(sb_ref, sgn_ref, x_ref, sbg_ref, wout_ref, ffng_ref, wr2_ref, br_ref,
                h_ref, lg_ref):
    sbn = _rms(sb_ref[...], sbg_ref[...]).astype(BF16)
    h = x_ref[...] + _dot(sbn, wout_ref[0:SB_WIDTH, :]) + _dot(sgn_ref[...], wout_ref[SB_WIDTH:, :])
    h_ref[...] = h
    hn = _rms(h, ffng_ref[...])

    hn_hi, hn_lo = _split_bf16(hn)
    both = _dot(hn_hi, wr2_ref[...])
    logits = both[:, 0:LANES] + both[:, LANES:] + _dot(hn_lo, wr2_ref[:, 0:LANES]) + br_ref[...]
    lg_ref[...] = logits.T[0:ROUTER_ROWS, :]


def _route_kernel(lg_ref, ri_ref, rw_ref, cnt_ref, count_ref):
    tr = TM_ROUTE
    i = pl.program_id(0)

    @pl.when(i == 0)
    def _():
        count_ref[...] = jnp.zeros_like(count_ref)

    neg = jnp.float32(-jnp.inf)
    row8 = lax.broadcasted_iota(jnp.int32, (SUBLANES, tr), 0)

    def top(v):
        m = jnp.max(v, axis=0, keepdims=True)
        return m, jnp.min(jnp.where(v == m, row8, SUBLANES), axis=0, keepdims=True)

    def group_rows(g):
        return lg_ref[ROUTER_LANE0 + g * EXPERTS_PER_GROUP:ROUTER_LANE0 + (g + 1) * EXPERTS_PER_GROUP, :]

    gl = jnp.where(row8 < N_GROUPS, lg_ref[0:SUBLANES, :], neg)
    gmax, gidx = top(gl)
    gweight = 1.0 / jnp.sum(jnp.exp(gl - gmax), axis=0, keepdims=True)
    el = group_rows(0)
    for g in range(1, N_GROUPS):
        el = jnp.where(gidx == g, group_rows(g), el)
    m1, i1 = top(el)
    m2, i2 = top(jnp.where(row8 == i1, neg, el))
    t21 = jnp.exp(m2 - m1)
    w1 = gweight / (1.0 + t21)
    w2 = gweight * t21 / (1.0 + t21)
    e1 = gidx * EXPERTS_PER_GROUP + i1
    e2 = gidx * EXPERTS_PER_GROUP + i2

    row_e = lax.broadcasted_iota(jnp.int32, (N_EXPERTS, tr), 0)
    sel1 = row_e == e1
    sel2 = row_e == e2
    onehot = jnp.where(sel1 | sel2, 1.0, 0.0)
    r_t = lax.broadcasted_iota(jnp.int32, (tr, tr), 0)
    c_t = lax.broadcasted_iota(jnp.int32, (tr, tr), 1)
    before = (r_t < c_t).astype(BF16)
    running = count_ref[:, 0:1] + _dot(onehot.astype(BF16), before)
    rank1 = jnp.sum(jnp.where(sel1, running, 0.0), axis=0, keepdims=True)
    rank2 = jnp.sum(jnp.where(sel2, running, 0.0), axis=0, keepdims=True)
    new_count = count_ref[:, 0:1] + jnp.sum(onehot, axis=1, keepdims=True)
    count_ref[...] = jnp.broadcast_to(new_count, count_ref.shape)
    cnt_ref[...] = jnp.broadcast_to(new_count, cnt_ref.shape)

    ri_ref[...] = jnp.where(row8 == 0, e1, jnp.where(row8 == 1, e2, jnp.where(
        row8 == 2, rank1.astype(jnp.int32), jnp.where(row8 == 3, rank2.astype(jnp.int32), 0))))
    row128 = lax.broadcasted_iota(jnp.int32, (LANES, tr), 0)
    rw_ref[...] = jnp.where(row128 == 0, w1, jnp.where(row128 == 1, w2, 0.0)).T


def _route(lg):
    n = lg.shape[1]
    return pl.pallas_call(
        _route_kernel,
        grid=(n // TM_ROUTE,),
        in_specs=[pl.BlockSpec((ROUTER_ROWS, TM_ROUTE), lambda i: (0, i))],
        out_specs=[pl.BlockSpec((SUBLANES, TM_ROUTE), lambda i: (0, i)),
                   pl.BlockSpec((TM_ROUTE, LANES), lambda i: (i, 0)),
                   pl.BlockSpec((N_EXPERTS, LANES), lambda i: (0, 0))],
        out_shape=[jax.ShapeDtypeStruct((SUBLANES, n), jnp.int32),
                   jax.ShapeDtypeStruct((n, LANES), F32),
                   jax.ShapeDtypeStruct((N_EXPERTS, LANES), F32)],
        scratch_shapes=[pltpu.VMEM((N_EXPERTS, LANES), F32)],
        compiler_params=pltpu.CompilerParams(dimension_semantics=("arbitrary",),
                                             vmem_limit_bytes=VMEM_LIMIT),
        name="route",
    )(lg)


def _mix(sb, sgn, x2, sb_g, w_out_b, ffn_g, wr2, br):
    n = x2.shape[0]
    row = lambda i: (i, 0)
    const = lambda i: (0, 0)
    return pl.pallas_call(
        _mix_kernel,
        grid=(n // TM_MIX,),
        in_specs=[pl.BlockSpec((TM_MIX, SB_WIDTH), row),
                  pl.BlockSpec((TM_MIX, SG_WIDTH), row),
                  pl.BlockSpec((TM_MIX, D_MODEL), row),
                  pl.BlockSpec((1, SB_WIDTH), const),
                  pl.BlockSpec((D_MODEL, D_MODEL), const),
                  pl.BlockSpec((1, D_MODEL), const),
                  pl.BlockSpec((D_MODEL, 2 * LANES), const),
                  pl.BlockSpec((1, LANES), const)],
        out_specs=[pl.BlockSpec((TM_MIX, D_MODEL), row),
                   pl.BlockSpec((ROUTER_ROWS, TM_MIX), lambda i: (0, i))],
        out_shape=[jax.ShapeDtypeStruct((n, D_MODEL), F32),
                   jax.ShapeDtypeStruct((ROUTER_ROWS, n), F32)],
        compiler_params=pltpu.CompilerParams(dimension_semantics=("arbitrary",),
                                             vmem_limit_bytes=VMEM_LIMIT),
        name="mix_router",
    )(sb, sgn, x2, sb_g, w_out_b, ffn_g, wr2, br)


_PAD_BITS = tuple(1 << b for b in reversed(range(TM_EXPERT.bit_length() - 1)))


def _dispatch_kernel(dest_ref, pad_start_ref, pad_count_ref, nt_ref, h_ref, g_ref, zeros_ref, xs_ref,
                     hn_ref, sem, zsem):
    tm = TM_DISPATCH
    i = pl.program_id(0)
    n = pl.num_programs(0) * tm
    base = i * tm
    n_tiles_max = xs_ref.shape[0] // (TM_EXPERT * ROW_TILE)
    _rows_to_tiles(hn_ref, _rms(h_ref[...], g_ref[...]))

    def pad_copies(do):
        for e in range(N_EXPERTS):
            start = pad_start_ref[e]
            count = pad_count_ref[e]
            for bit in _PAD_BITS:
                @pl.when((count & bit) != 0)
                def _(start=start, bit=bit):
                    do(pltpu.make_async_copy(_token_rows(zeros_ref, 0, bit),
                                             _token_rows(xs_ref, start, bit), zsem))
                start = start + (count & bit)
        for k in range(N_EXPERTS):
            tile = nt_ref[0] + k

            @pl.when(tile < n_tiles_max)
            def _(tile=tile):
                do(pltpu.make_async_copy(zeros_ref, _token_rows(xs_ref, tile * TM_EXPERT, TM_EXPERT), zsem))

    @pl.when(i == 0)
    def _():
        pad_copies(lambda cp: cp.start())

    def body(r, c):
        src = _token_rows(hn_ref, r, 1)
        for s in range(2):
            pltpu.make_async_copy(src, _token_rows(xs_ref, dest_ref[s * n + base + r], 1),
                                  sem).start(priority=s)
        return c

    lax.fori_loop(0, tm, body, 0, unroll=8)
    for _ in range(2):
        pltpu.make_async_copy(hn_ref, _token_rows(xs_ref, 0, tm), sem).wait()

    @pl.when(i == pl.num_programs(0) - 1)
    def _():
        pad_copies(lambda cp: cp.wait())


def _dispatch(dest, pad_start, pad_count, n_tiles, h, ffn_g, n_rows):
    n = h.shape[0]
    zeros = jnp.zeros((TM_EXPERT * ROW_TILE, LANES), F32)
    return pl.pallas_call(
        _dispatch_kernel,
        grid_spec=pltpu.PrefetchScalarGridSpec(
            num_scalar_prefetch=4,
            grid=(n // TM_DISPATCH,),
            in_specs=[pl.BlockSpec((TM_DISPATCH, D_MODEL), lambda i, *_: (i, 0)),
                      pl.BlockSpec((1, D_MODEL), lambda i, *_: (0, 0)),
                      pl.BlockSpec(memory_space=pl.ANY)],
            out_specs=pl.BlockSpec(memory_space=pl.ANY),
            scratch_shapes=[pltpu.VMEM((TM_DISPATCH * ROW_TILE, LANES), F32),
                            pltpu.SemaphoreType.DMA, pltpu.SemaphoreType.DMA]),
        out_shape=jax.ShapeDtypeStruct((n_rows * ROW_TILE, LANES), F32),
        compiler_params=pltpu.CompilerParams(dimension_semantics=("arbitrary",),
                                             vmem_limit_bytes=VMEM_LIMIT),
        name="dispatch",
    )(dest, pad_start, pad_count, n_tiles, h, ffn_g, zeros)


X_SLOTS = 3


def _expert_kernel(tiles_ref, nt_ref, xs_ref, wg_ref, wu_ref, wd_ref, y_ref,
                   x_buf, sg_buf, su_buf, sd_buf, wgb, wub, wdb, state, w_sems, x_sems):
    tm = TM_EXPERT
    t = pl.program_id(0)
    nt = nt_ref[0]

    def x_copy(tile):
        slot = lax.rem(tile, X_SLOTS)
        return pltpu.make_async_copy(_token_rows(xs_ref, tile * tm, tm), x_buf.at[slot], x_sems.at[slot])

    def weight_copies(e, slot):
        return (pltpu.make_async_copy(wg_ref.at[e], sg_buf.at[slot], w_sems.at[slot]),
                pltpu.make_async_copy(wu_ref.at[e], su_buf.at[slot], w_sems.at[slot]),
                pltpu.make_async_copy(wd_ref.at[e], sd_buf.at[slot], w_sems.at[slot]))

    def next_with_rows(e):
        return lax.while_loop(lambda k: (k < N_EXPERTS) & (tiles_ref[jnp.minimum(k, N_EXPERTS - 1)] == 0),
                              lambda k: k + 1, e + 1)

    @pl.when(t == 0)
    def _():
        first = next_with_rows(jnp.int32(-1))
        state[0] = jnp.int32(-1)
        state[1] = jnp.int32(0)
        state[2] = jnp.int32(1)
        state[3] = first
        for cp in weight_copies(first, 0):
            cp.start()
        x_copy(0).start()

        @pl.when(nt > 1)
        def _():
            x_copy(1).start()

    @pl.when(t + 2 < nt)
    def _():
        x_copy(t + 2).start()

    @pl.when(t < nt)
    def _():
        @pl.when(state[1] == 0)
        def _():
            e = state[3]
            slot = 1 - state[2]
            nxt = next_with_rows(e)
            state[0] = e
            state[1] = tiles_ref[e]
            state[2] = slot
            state[3] = nxt
            for cp in weight_copies(e, slot):
                cp.wait()

            @pl.when(nxt < N_EXPERTS)
            def _():
                for cp in weight_copies(nxt, 1 - slot):
                    cp.start()

            wgb[...] = sg_buf[slot].astype(BF16)
            wub[...] = su_buf[slot].astype(BF16)
            wdb[...] = sd_buf[slot].astype(BF16)

        state[1] = state[1] - 1
        x_copy(t).wait()
        x = _tiles_to_rows(x_buf.at[lax.rem(t, X_SLOTS)], tm).astype(BF16)
        g = _dot(x, wgb[...])
        u = _dot(x, wub[...])
        hidden = (g * jax.nn.sigmoid(g)) * u
        _rows_to_tiles(y_ref, _dot(hidden.astype(BF16), wdb[...]))

    @pl.when(t >= nt)
    def _():
        y_ref[...] = jnp.zeros_like(y_ref)


def _experts(tiles, n_tiles, xs, wg, wu, wd):
    n_rows = xs.shape[0] // ROW_TILE
    any_spec = pl.BlockSpec(memory_space=pl.ANY)
    return pl.pallas_call(
        _expert_kernel,
        grid_spec=pltpu.PrefetchScalarGridSpec(
            num_scalar_prefetch=2,
            grid=(n_rows // TM_EXPERT,),
            in_specs=[any_spec, any_spec, any_spec, any_spec],
            out_specs=pl.BlockSpec((TM_EXPERT * ROW_TILE, LANES), lambda t, *_: (t, 0)),
            scratch_shapes=[pltpu.VMEM((X_SLOTS, TM_EXPERT * ROW_TILE, LANES), F32),
                            pltpu.VMEM((2, D_MODEL, D_EXPERT), F32),
                            pltpu.VMEM((2, D_MODEL, D_EXPERT), F32),
                            pltpu.VMEM((2, D_EXPERT, D_MODEL), F32),
                            pltpu.VMEM((D_MODEL, D_EXPERT), BF16),
                            pltpu.VMEM((D_MODEL, D_EXPERT), BF16),
                            pltpu.VMEM((D_EXPERT, D_MODEL), BF16),
                            pltpu.SMEM((4,), jnp.int32),
                            pltpu.SemaphoreType.DMA((2,)),
                            pltpu.SemaphoreType.DMA((X_SLOTS,))]),
        out_shape=jax.ShapeDtypeStruct((n_rows * ROW_TILE, LANES), F32),
        compiler_params=pltpu.CompilerParams(dimension_semantics=("arbitrary",),
                                             vmem_limit_bytes=VMEM_LIMIT),
        name="expert_mlp",
    )(tiles, n_tiles, xs, wg, wu, wd)


def _combine_kernel(dest_ref, h_ref, rw_ref, fg_ref, y_ref, o_ref, buf, sems):
    tm = TM_COMBINE
    i = pl.program_id(0)
    n_steps = pl.num_programs(0)
    n = n_steps * tm
    cur = i % 2

    def fetch(step, half):
        def body(r, c):
            for s in range(2):
                pltpu.make_async_copy(_token_rows(y_ref, dest_ref[s * n + step * tm + r], 1),
                                      _token_rows(buf.at[half, s], r, 1),
                                      sems.at[half]).start(priority=s)
            return c

        lax.fori_loop(0, tm, body, 0, unroll=8)

    @pl.when(i == 0)
    def _():
        fetch(0, 0)

    @pl.when(i + 1 < n_steps)
    def _():
        fetch(i + 1, 1 - cur)

    for s in range(2):
        pltpu.make_async_copy(_token_rows(y_ref, 0, tm), buf.at[cur, s], sems.at[cur]).wait()
    rw = rw_ref[...]
    out = (h_ref[...] + rw[:, 0:1] * _tiles_to_rows(buf.at[cur, 0], tm)
           + rw[:, 1:2] * _tiles_to_rows(buf.at[cur, 1], tm))
    o_ref[...] = _rms(out, fg_ref[...])


def _combine(dest, h, rw, final_g, ys):
    n = h.shape[0]
    return pl.pallas_call(
        _combine_kernel,
        grid_spec=pltpu.PrefetchScalarGridSpec(
            num_scalar_prefetch=1,
            grid=(n // TM_COMBINE,),
            in_specs=[pl.BlockSpec((TM_COMBINE, D_MODEL), lambda i, d: (i, 0)),
                      pl.BlockSpec((TM_COMBINE, LANES), lambda i, d: (i, 0)),
                      pl.BlockSpec((1, D_MODEL), lambda i, d: (0, 0)),
                      pl.BlockSpec(memory_space=pl.ANY)],
            out_specs=pl.BlockSpec((TM_COMBINE, D_MODEL), lambda i, d: (i, 0)),
            scratch_shapes=[pltpu.VMEM((2, 2, TM_COMBINE * ROW_TILE, LANES), F32),
                            pltpu.SemaphoreType.DMA((2,))]),
        out_shape=jax.ShapeDtypeStruct((n, D_MODEL), F32),
        compiler_params=pltpu.CompilerParams(dimension_semantics=("arbitrary",),
                                             vmem_limit_bytes=VMEM_LIMIT),
        name="combine",
    )(dest, h, rw, final_g, ys)


def _schedule(counts):
    tiles = (counts + TM_EXPERT - 1) // TM_EXPERT
    tile_end = jnp.cumsum(tiles)
    offsets = (tile_end - tiles) * TM_EXPERT
    return tiles, offsets, tile_end[-1:]


def _layer(x, attn_g, w_in, sg_g, w_sp, b_sp, sb_g, sg_out_g, w_out, ffn_g,
           w_rg, b_rg, w_re, b_re, w_gate, w_up, w_down):
    batch, seq, _ = x.shape
    n = batch * seq
    x2 = x.reshape(n, D_MODEL)
    row = lambda v: v.reshape(1, -1)

    bsp_full = jnp.repeat(b_sp.T, HEAD_DIM, axis=1)
    qkv, sgn = _inproj(x2, row(attn_g), w_in.astype(BF16), row(sg_g), w_sp, bsp_full, row(sg_out_g))
    sb = _attention(qkv, batch, seq).reshape(n, SB_WIDTH)

    pad_lanes = lambda v, width: jnp.pad(v, [(0, 0)] * (v.ndim - 1) + [(0, width - v.shape[-1])])
    w_r = jnp.concatenate([pad_lanes(w_rg, ROUTER_LANE0),
                           jnp.transpose(w_re, (1, 0, 2)).reshape(D_MODEL, N_EXPERTS)], axis=1)
    w_r = pad_lanes(w_r, LANES)
    wr_hi = w_r.astype(BF16)
    wr_lo = (w_r - wr_hi.astype(F32)).astype(BF16)
    wr2 = jnp.concatenate([wr_hi, wr_lo], axis=1)
    b_r = pad_lanes(jnp.concatenate([pad_lanes(b_rg, ROUTER_LANE0), b_re.reshape(-1)]), LANES)

    h, lg = _mix(sb, sgn, x2, row(sb_g), w_out.astype(BF16), row(ffn_g), wr2, row(b_r))
    ri, rw, cnt = _route(lg)

    counts = cnt[:, 0].astype(jnp.int32)
    n_rows = 2 * n + N_EXPERTS * TM_EXPERT
    tiles, offsets, n_tiles = _schedule(counts)
    expert, rank = ri[0:2], ri[2:4]
    is_e = expert[None] == jnp.arange(N_EXPERTS, dtype=jnp.int32)[:, None, None]
    dest = (jnp.sum(jnp.where(is_e, offsets[:, None, None], 0), axis=0) + rank).reshape(-1)
    pad_start = offsets + counts
    pad_count = (-counts) % TM_EXPERT

    xs = _dispatch(dest, pad_start, pad_count, n_tiles, h, row(ffn_g), n_rows)
    ys = _experts(tiles, n_tiles, xs,
                  w_gate.reshape(N_EXPERTS, D_MODEL, D_EXPERT),
                  w_up.reshape(N_EXPERTS, D_MODEL, D_EXPERT),
                  w_down.reshape(N_EXPERTS, D_EXPERT, D_MODEL))
    return dest, h, rw, ys


def kernel(x, attn_norm_g, w_in, sg_norm_g, w_spatial, b_spatial, sb_out_norm_g, sg_out_norm_g,
           w_out, ffn_norm_g, w_router_group, b_router_group, w_router_expert, b_router_expert,
           w_gate, w_up, w_down, final_norm_g):
    assert attn_norm_g.shape[0] == 1, "single-layer problem"
    batch, seq, _ = x.shape
    dest, h, rw, ys = _layer(x, attn_norm_g[0], w_in[0], sg_norm_g[0], w_spatial[0], b_spatial[0],
                             sb_out_norm_g[0], sg_out_norm_g[0], w_out[0], ffn_norm_g[0],
                             w_router_group[0], b_router_group[0], w_router_expert[0],
                             b_router_expert[0], w_gate[0], w_up[0], w_down[0])
    out = _combine(dest, h, rw, final_norm_g.reshape(1, -1), ys)
    return out.reshape(batch, seq, D_MODEL)
```

```python
import functools
import math

import jax
import jax.numpy as jnp
from jax import lax
from jax.experimental import pallas as pl
from jax.experimental.pallas import tpu as pltpu

D_MODEL = 1024
HEAD_DIM = 64
SB_WIDTH = 512
SG_WIDTH = 512
SG_HEADS = 8
D_IN = 3 * SB_WIDTH + 2 * SG_WIDTH
CHUNK = 128
N_GROUPS = 4
EXPERTS_PER_GROUP = 8
N_EXPERTS = N_GROUPS * EXPERTS_PER_GROUP
D_EXPERT = 512
EPS = 1e-6
F32_EXP_UNDERFLOW = 110.0

LANES = 128
SUBLANES = 8
ROW_TILE = D_MODEL // LANES
assert ROW_TILE == SUBLANES
HEAD_PAIR = 2 * HEAD_DIM
ROUTER_LANE0 = SUBLANES
ROUTER_ROWS = ROUTER_LANE0 + N_EXPERTS
assert EXPERTS_PER_GROUP == SUBLANES and N_GROUPS <= ROUTER_LANE0

TM_PROJ = 1024
TQ_ATTN = 256
TM_MIX = 1024
TM_ROUTE = 1024
TM_DISPATCH = 512
TM_EXPERT = 512
TM_COMBINE = 256
VMEM_LIMIT = 48 * 1024 * 1024

F32 = jnp.float32
BF16 = jnp.bfloat16


def _rms(x, g):
    return x * lax.rsqrt(jnp.mean(x * x, axis=-1, keepdims=True) + EPS) * g


def _gelu(x):
    c = math.sqrt(2.0 / math.pi)
    return x * (0.5 * (1.0 + jnp.tanh(c * (x + 0.044715 * (x * x * x)))))


def _softplus(z):
    return jnp.maximum(z, 0.0) + jnp.log(1.0 + jnp.exp(-jnp.abs(z)))


def _dot(a, b):
    return jnp.dot(a, b, preferred_element_type=F32)


def _rows_to_tiles(ref, x):
    m = x.shape[0]
    for k in range(ROW_TILE):
        ref[pl.ds(k, m, stride=ROW_TILE), :] = x[:, k * LANES:(k + 1) * LANES]


def _tiles_to_rows(ref, m):
    return jnp.concatenate([ref[pl.ds(k, m, stride=ROW_TILE), :] for k in range(ROW_TILE)], axis=1)


def _token_rows(ref, first_token, n_tokens):
    return ref.at[pl.ds(pl.multiple_of(first_token * ROW_TILE, ROW_TILE), n_tokens * ROW_TILE)]


def _split_bf16(x):
    hi = x.astype(BF16)
    lo = (x - hi.astype(F32)).astype(BF16)
    return hi, lo


def _inproj_kernel(x_ref, g_ref, w_ref, sgg_ref, wsp_ref, bsp_ref, sgog_ref, qkv_ref, sgn_ref,
                   gu_ref, vgn_ref, sg_ref):
    tm = TM_PROJ
    hb = _rms(x_ref[...], g_ref[...]).astype(BF16)
    gv = _gelu(_dot(hb, w_ref[:, 3 * SB_WIDTH + SG_WIDTH:D_IN]))
    vgn_ref[...] = _rms(gv, sgg_ref[...]).astype(BF16)
    gu_ref[...] = _gelu(_dot(hb, w_ref[:, 3 * SB_WIDTH:3 * SB_WIDTH + SG_WIDTH]))
    q = _dot(hb, w_ref[:, 0:SB_WIDTH]) * (1.0 / math.sqrt(HEAD_DIM))
    qkv_ref[:, 0:SB_WIDTH] = q.astype(BF16)
    qkv_ref[:, SB_WIDTH:2 * SB_WIDTH] = _dot(hb, w_ref[:, SB_WIDTH:2 * SB_WIDTH]).astype(BF16)

    lane = lax.broadcasted_iota(jnp.int32, (1, LANES), 1)
    first = lane < HEAD_DIM
    zero = jnp.zeros((), BF16)
    r_c = lax.broadcasted_iota(jnp.int32, (CHUNK, CHUNK), 0)
    c_c = lax.broadcasted_iota(jnp.int32, (CHUNK, CHUNK), 1)
    tril = r_c >= c_c
    n_pairs = SG_WIDTH // HEAD_PAIR
    w_pairs = []
    for p in range(n_pairs):
        w0 = jnp.where(tril, wsp_ref[2 * p], 0.0).astype(BF16)
        w1 = jnp.where(tril, wsp_ref[2 * p + 1], 0.0).astype(BF16)
        w_pairs.append(jnp.concatenate([w0, w1], axis=1))
    bsp = bsp_ref[...]
    for c in range(tm // CHUNK):
        rows = slice(c * CHUNK, (c + 1) * CHUNK)
        for p in range(n_pairs):
            cols = slice(p * HEAD_PAIR, (p + 1) * HEAD_PAIR)
            vg = vgn_ref[rows, cols]
            rhs = jnp.concatenate([jnp.where(first, vg, zero), jnp.where(first, zero, vg)], axis=0)
            mixed = _dot(w_pairs[p], rhs) + bsp[:, cols]
            sg_ref[rows, cols] = gu_ref[rows, cols] * mixed
    qkv_ref[:, 2 * SB_WIDTH:3 * SB_WIDTH] = _dot(hb, w_ref[:, 2 * SB_WIDTH:3 * SB_WIDTH]).astype(BF16)
    sgn_ref[...] = _rms(sg_ref[...], sgog_ref[...]).astype(BF16)


def _inproj(x2, attn_g, w_in_b, sg_g, wsp, bsp_full, sg_out_g):
    n = x2.shape[0]
    row = lambda i: (i, 0)
    const = lambda i: (0, 0)
    return pl.pallas_call(
        _inproj_kernel,
        grid=(n // TM_PROJ,),
        in_specs=[pl.BlockSpec((TM_PROJ, D_MODEL), row),
                  pl.BlockSpec((1, D_MODEL), const),
                  pl.BlockSpec((D_MODEL, D_IN), const),
                  pl.BlockSpec((1, SG_WIDTH), const),
                  pl.BlockSpec((SG_HEADS, CHUNK, CHUNK), lambda i: (0, 0, 0)),
                  pl.BlockSpec((CHUNK, SG_WIDTH), const),
                  pl.BlockSpec((1, SG_WIDTH), const)],
        out_specs=[pl.BlockSpec((TM_PROJ, 3 * SB_WIDTH), row),
                   pl.BlockSpec((TM_PROJ, SG_WIDTH), row)],
        out_shape=[jax.ShapeDtypeStruct((n, 3 * SB_WIDTH), BF16),
                   jax.ShapeDtypeStruct((n, SG_WIDTH), BF16)],
        scratch_shapes=[pltpu.VMEM((TM_PROJ, SG_WIDTH), F32),
                        pltpu.VMEM((TM_PROJ, SG_WIDTH), BF16),
                        pltpu.VMEM((TM_PROJ, SG_WIDTH), F32)],
        compiler_params=pltpu.CompilerParams(dimension_semantics=("arbitrary",),
                                             vmem_limit_bytes=VMEM_LIMIT),
        name="inproj",
    )(x2, attn_g, w_in_b, sg_g, wsp, bsp_full, sg_out_g)


def _attn_kernel(q_ref, k_ref, v_ref, o_ref, q2_ref, carry_ref):
    t = TQ_ATTN
    n_pairs = SB_WIDTH // HEAD_PAIR
    qi = pl.program_id(1)
    lane = lax.broadcasted_iota(jnp.int32, (1, HEAD_PAIR), 1)
    head_lanes = (lane < HEAD_DIM, lane >= HEAD_DIM)
    zero = jnp.zeros((), BF16)
    for p in range(n_pairs):
        qp = q_ref[0, :, p * HEAD_PAIR:(p + 1) * HEAD_PAIR]
        for h in range(2):
            q2_ref[(2 * p + h) * t:(2 * p + h + 1) * t, :] = jnp.where(head_lanes[h], qp, zero)
    r_idx = lax.broadcasted_iota(jnp.int32, (t, t), 0)
    c_idx = lax.broadcasted_iota(jnp.int32, (t, t), 1)
    suffix = (r_idx > c_idx).astype(BF16)
    suffix2 = jnp.concatenate([suffix, suffix], axis=0)
    causal = c_idx < r_idx

    o_ref[...] = jnp.zeros_like(o_ref)
    carry_ref[...] = jnp.zeros_like(carry_ref)

    def block(j, diag):
        start = pl.multiple_of(j * t, t)
        mask2 = jnp.concatenate([causal, causal], axis=0) if diag else None
        st = [dict() for _ in range(n_pairs)]

        def scores(p):
            d = st[p]
            d["cols"] = slice(p * HEAD_PAIR, (p + 1) * HEAD_PAIR)
            d["rows"] = slice(2 * p * t, (2 * p + 2) * t)
            kb = k_ref[0, pl.ds(start, t), d["cols"]]
            z = lax.dot_general(q2_ref[d["rows"], :], kb, (((1,), (1,)), ((), ())),
                                preferred_element_type=F32)
            sp = _softplus(z)
            nl = jnp.where(mask2, sp, 0.0) if diag else sp
            hi, lo = _split_bf16(nl)
            d["hl"] = jnp.concatenate([hi, lo], axis=1)
            d["log_beta"] = z - sp
            d["nl0"] = nl[:, 0:1]

        def weights(p):
            d = st[p]
            hl = d["hl"]
            after = jnp.concatenate([_dot(hl[0:t], suffix2), _dot(hl[t:2 * t], suffix2)], axis=0)
            carry = carry_ref[d["rows"], :]
            a = jnp.exp(d["log_beta"] - after - carry)
            if diag:
                a = jnp.where(mask2, a, 0.0)
            a = a.astype(BF16)
            d["a2"] = jnp.concatenate([a[0:t], a[t:2 * t]], axis=1)
            carry_ref[d["rows"], :] = carry + after[:, 0:1] + d["nl0"]

        def values(p):
            d = st[p]
            vb = v_ref[0, pl.ds(start, t), d["cols"]]
            v2 = jnp.concatenate([jnp.where(head_lanes[0], vb, zero),
                                  jnp.where(head_lanes[1], vb, zero)], axis=0)
            o_ref[0, :, d["cols"]] += _dot(d["a2"], v2)

        for step in range(n_pairs + 2):
            if step < n_pairs:
                scores(step)
            if 0 <= step - 1 < n_pairs:
                weights(step - 1)
            if 0 <= step - 2 < n_pairs:
                values(step - 2)

    def live():
        return jnp.min(carry_ref[...]) < F32_EXP_UNDERFLOW

    block(qi, True)

    def body(state):
        it, _ = state
        block(qi - 1 - it, False)
        return it + 1, live()

    lax.while_loop(lambda s: (s[0] < qi) & s[1], body, (jnp.int32(0), live()))


def _attention(qkv, batch, seq):
    qkv3 = qkv.reshape(batch, seq, 3 * SB_WIDTH)
    n_heads = SB_WIDTH // HEAD_DIM
    return pl.pallas_call(
        _attn_kernel,
        grid=(batch, seq // TQ_ATTN),
        in_specs=[pl.BlockSpec((1, TQ_ATTN, SB_WIDTH), lambda b, i: (b, i, 0)),
                  pl.BlockSpec((1, seq, SB_WIDTH), lambda b, i: (b, 0, 1)),
                  pl.BlockSpec((1, seq, SB_WIDTH), lambda b, i: (b, 0, 2))],
        out_specs=pl.BlockSpec((1, TQ_ATTN, SB_WIDTH), lambda b, i: (b, i, 0)),
        out_shape=jax.ShapeDtypeStruct((batch, seq, SB_WIDTH), F32),
        scratch_shapes=[pltpu.VMEM((n_heads * TQ_ATTN, HEAD_PAIR), BF16),
                        pltpu.VMEM((n_heads * TQ_ATTN, 1), F32)],
        compiler_params=pltpu.CompilerParams(dimension_semantics=("arbitrary",) * 2,
                                             vmem_limit_bytes=VMEM_LIMIT),
        name="sb_attention",
    )(qkv3, qkv3, qkv3)


def _mix_kernel(sb_ref, sgn_ref, x_ref, sbg_ref, wout_ref, ffng_ref, wr2_ref, br_ref,
                h_ref, lg_ref):
    sbn = _rms(sb_ref[...], sbg_ref[...]).astype(BF16)
    h = x_ref[...] + _dot(sbn, wout_ref[0:SB_WIDTH, :]) + _dot(sgn_ref[...], wout_ref[SB_WIDTH:, :])
    h_ref[...] = h
    hn = _rms(h, ffng_ref[...])

    hn_hi, hn_lo = _split_bf16(hn)
    both = _dot(hn_hi, wr2_ref[...])
    logits = both[:, 0:LANES] + both[:, LANES:] + _dot(hn_lo, wr2_ref[:, 0:LANES]) + br_ref[...]
    lg_ref[...] = logits.T[0:ROUTER_ROWS, :]


def _route_kernel(lg_ref, ri_ref, rw_ref, cnt_ref, count_ref):
    tr = TM_ROUTE
    i = pl.program_id(0)

    @pl.when(i == 0)
    def _():
        count_ref[...] = jnp.zeros_like(count_ref)

    neg = jnp.float32(-jnp.inf)
    row8 = lax.broadcasted_iota(jnp.int32, (SUBLANES, tr), 0)

    def top(v):
        m = jnp.max(v, axis=0, keepdims=True)
        return m, jnp.min(jnp.where(v == m, row8, SUBLANES), axis=0, keepdims=True)

    def group_rows(g):
        return lg_ref[ROUTER_LANE0 + g * EXPERTS_PER_GROUP:ROUTER_LANE0 + (g + 1) * EXPERTS_PER_GROUP, :]

    gl = jnp.where(row8 < N_GROUPS, lg_ref[0:SUBLANES, :], neg)
    gmax, gidx = top(gl)
    gweight = 1.0 / jnp.sum(jnp.exp(gl - gmax), axis=0, keepdims=True)
    el = group_rows(0)
    for g in range(1, N_GROUPS):
        el = jnp.where(gidx == g, group_rows(g), el)
    m1, i1 = top(el)
    m2, i2 = top(jnp.where(row8 == i1, neg, el))
    t21 = jnp.exp(m2 - m1)
    w1 = gweight / (1.0 + t21)
    w2 = gweight * t21 / (1.0 + t21)
    e1 = gidx * EXPERTS_PER_GROUP + i1
    e2 = gidx * EXPERTS_PER_GROUP + i2

    row_e = lax.broadcasted_iota(jnp.int32, (N_EXPERTS, tr), 0)
    sel1 = row_e == e1
    sel2 = row_e == e2
    onehot = jnp.where(sel1 | sel2, 1.0, 0.0)
    r_t = lax.broadcasted_iota(jnp.int32, (tr, tr), 0)
    c_t = lax.broadcasted_iota(jnp.int32, (tr, tr), 1)
    before = (r_t < c_t).astype(BF16)
    running = count_ref[:, 0:1] + _dot(onehot.astype(BF16), before)
    rank1 = jnp.sum(jnp.where(sel1, running, 0.0), axis=0, keepdims=True)
    rank2 = jnp.sum(jnp.where(sel2, running, 0.0), axis=0, keepdims=True)
    new_count = count_ref[:, 0:1] + jnp.sum(onehot, axis=1, keepdims=True)
    count_ref[...] = jnp.broadcast_to(new_count, count_ref.shape)
    cnt_ref[...] = jnp.broadcast_to(new_count, cnt_ref.shape)

    ri_ref[...] = jnp.where(row8 == 0, e1, jnp.where(row8 == 1, e2, jnp.where(
        row8 == 2, rank1.astype(jnp.int32), jnp.where(row8 == 3, rank2.astype(jnp.int32), 0))))
    row128 = lax.broadcasted_iota(jnp.int32, (LANES, tr), 0)
    rw_ref[...] = jnp.where(row128 == 0, w1, jnp.where(row128 == 1, w2, 0.0)).T


def _route(lg):
    n = lg.shape[1]
    return pl.pallas_call(
        _route_kernel,
        grid=(n // TM_ROUTE,),
        in_specs=[pl.BlockSpec((ROUTER_ROWS, TM_ROUTE), lambda i: (0, i))],
        out_specs=[pl.BlockSpec((SUBLANES, TM_ROUTE), lambda i: (0, i)),
                   pl.BlockSpec((TM_ROUTE, LANES), lambda i: (i, 0)),
                   pl.BlockSpec((N_EXPERTS, LANES), lambda i: (0, 0))],
        out_shape=[jax.ShapeDtypeStruct((SUBLANES, n), jnp.int32),
                   jax.ShapeDtypeStruct((n, LANES), F32),
                   jax.ShapeDtypeStruct((N_EXPERTS, LANES), F32)],
        scratch_shapes=[pltpu.VMEM((N_EXPERTS, LANES), F32)],
        compiler_params=pltpu.CompilerParams(dimension_semantics=("arbitrary",),
                                             vmem_limit_bytes=VMEM_LIMIT),
        name="route",
    )(lg)


def _mix(sb, sgn, x2, sb_g, w_out_b, ffn_g, wr2, br):
    n = x2.shape[0]
    row = lambda i: (i, 0)
    const = lambda i: (0, 0)
    return pl.pallas_call(
        _mix_kernel,
        grid=(n // TM_MIX,),
        in_specs=[pl.BlockSpec((TM_MIX, SB_WIDTH), row),
                  pl.BlockSpec((TM_MIX, SG_WIDTH), row),
                  pl.BlockSpec((TM_MIX, D_MODEL), row),
                  pl.BlockSpec((1, SB_WIDTH), const),
                  pl.BlockSpec((D_MODEL, D_MODEL), const),
                  pl.BlockSpec((1, D_MODEL), const),
                  pl.BlockSpec((D_MODEL, 2 * LANES), const),
                  pl.BlockSpec((1, LANES), const)],
        out_specs=[pl.BlockSpec((TM_MIX, D_MODEL), row),
                   pl.BlockSpec((ROUTER_ROWS, TM_MIX), lambda i: (0, i))],
        out_shape=[jax.ShapeDtypeStruct((n, D_MODEL), F32),
                   jax.ShapeDtypeStruct((ROUTER_ROWS, n), F32)],
        compiler_params=pltpu.CompilerParams(dimension_semantics=("arbitrary",),
                                             vmem_limit_bytes=VMEM_LIMIT),
        name="mix_router",
    )(sb, sgn, x2, sb_g, w_out_b, ffn_g, wr2, br)


_PAD_BITS = tuple(1 << b for b in reversed(range(TM_EXPERT.bit_length() - 1)))


def _dispatch_kernel(dest_ref, pad_start_ref, pad_count_ref, nt_ref, h_ref, g_ref, zeros_ref, xs_ref,
                     hn_ref, sem, zsem):
    tm = TM_DISPATCH
    i = pl.program_id(0)
    n = pl.num_programs(0) * tm
    base = i * tm
    n_tiles_max = xs_ref.shape[0] // (TM_EXPERT * ROW_TILE)
    _rows_to_tiles(hn_ref, _rms(h_ref[...], g_ref[...]))

    def pad_copies(do):
        for e in range(N_EXPERTS):
            start = pad_start_ref[e]
            count = pad_count_ref[e]
            for bit in _PAD_BITS:
                @pl.when((count & bit) != 0)
                def _(start=start, bit=bit):
                    do(pltpu.make_async_copy(_token_rows(zeros_ref, 0, bit),
                                             _token_rows(xs_ref, start, bit), zsem))
                start = start + (count & bit)
        for k in range(N_EXPERTS):
            tile = nt_ref[0] + k

            @pl.when(tile < n_tiles_max)
            def _(tile=tile):
                do(pltpu.make_async_copy(zeros_ref, _token_rows(xs_ref, tile * TM_EXPERT, TM_EXPERT), zsem))

    @pl.when(i == 0)
    def _():
        pad_copies(lambda cp: cp.start())

    def body(r, c):
        src = _token_rows(hn_ref, r, 1)
        for s in range(2):
            pltpu.make_async_copy(src, _token_rows(xs_ref, dest_ref[s * n + base + r], 1),
                                  sem).start(priority=s)
        return c

    lax.fori_loop(0, tm, body, 0, unroll=8)
    for _ in range(2):
        pltpu.make_async_copy(hn_ref, _token_rows(xs_ref, 0, tm), sem).wait()

    @pl.when(i == pl.num_programs(0) - 1)
    def _():
        pad_copies(lambda cp: cp.wait())


def _dispatch(dest, pad_start, pad_count, n_tiles, h, ffn_g, n_rows):
    n = h.shape[0]
    zeros = jnp.zeros((TM_EXPERT * ROW_TILE, LANES), F32)
    return pl.pallas_call(
        _dispatch_kernel,
        grid_spec=pltpu.PrefetchScalarGridSpec(
            num_scalar_prefetch=4,
            grid=(n // TM_DISPATCH,),
            in_specs=[pl.BlockSpec((TM_DISPATCH, D_MODEL), lambda i, *_: (i, 0)),
                      pl.BlockSpec((1, D_MODEL), lambda i, *_: (0, 0)),
                      pl.BlockSpec(memory_space=pl.ANY)],
            out_specs=pl.BlockSpec(memory_space=pl.ANY),
            scratch_shapes=[pltpu.VMEM((TM_DISPATCH * ROW_TILE, LANES), F32),
                            pltpu.SemaphoreType.DMA, pltpu.SemaphoreType.DMA]),
        out_shape=jax.ShapeDtypeStruct((n_rows * ROW_TILE, LANES), F32),
        compiler_params=pltpu.CompilerParams(dimension_semantics=("arbitrary",),
                                             vmem_limit_bytes=VMEM_LIMIT),
        name="dispatch",
    )(dest, pad_start, pad_count, n_tiles, h, ffn_g, zeros)


X_SLOTS = 3


def _expert_kernel(tiles_ref, nt_ref, xs_ref, wg_ref, wu_ref, wd_ref, y_ref,
                   x_buf, sg_buf, su_buf, sd_buf, wgb, wub, wdb, state, w_sems, x_sems):
    tm = TM_EXPERT
    t = pl.program_id(0)
    nt = nt_ref[0]

    def x_copy(tile):
        slot = lax.rem(tile, X_SLOTS)
        return pltpu.make_async_copy(_token_rows(xs_ref, tile * tm, tm), x_buf.at[slot], x_sems.at[slot])

    def weight_copies(e, slot):
        return (pltpu.make_async_copy(wg_ref.at[e], sg_buf.at[slot], w_sems.at[slot]),
                pltpu.make_async_copy(wu_ref.at[e], su_buf.at[slot], w_sems.at[slot]),
                pltpu.make_async_copy(wd_ref.at[e], sd_buf.at[slot], w_sems.at[slot]))

    def next_with_rows(e):
        return lax.while_loop(lambda k: (k < N_EXPERTS) & (tiles_ref[jnp.minimum(k, N_EXPERTS - 1)] == 0),
                              lambda k: k + 1, e + 1)

    @pl.when(t == 0)
    def _():
        first = next_with_rows(jnp.int32(-1))
        state[0] = jnp.int32(-1)
        state[1] = jnp.int32(0)
        state[2] = jnp.int32(1)
        state[3] = first
        for cp in weight_copies(first, 0):
            cp.start()
        x_copy(0).start()

        @pl.when(nt > 1)
        def _():
            x_copy(1).start()

    @pl.when(t + 2 < nt)
    def _():
        x_copy(t + 2).start()

    @pl.when(t < nt)
    def _():
        @pl.when(state[1] == 0)
        def _():
            e = state[3]
            slot = 1 - state[2]
            nxt = next_with_rows(e)
            state[0] = e
            state[1] = tiles_ref[e]
            state[2] = slot
            state[3] = nxt
            for cp in weight_copies(e, slot):
                cp.wait()

            @pl.when(nxt < N_EXPERTS)
            def _():
                for cp in weight_copies(nxt, 1 - slot):
                    cp.start()

            wgb[...] = sg_buf[slot].astype(BF16)
            wub[...] = su_buf[slot].astype(BF16)
            wdb[...] = sd_buf[slot].astype(BF16)

        state[1] = state[1] - 1
        x_copy(t).wait()
        x_slot = x_buf.at[lax.rem(t, X_SLOTS)]
        half = tm // 2
        hidden = []
        for k in range(2):
            x = _tiles_to_rows(_token_rows(x_slot, k * half, half), half).astype(BF16)
            g = _dot(x, wgb[...])
            u = _dot(x, wub[...])
            hidden.append(((g * jax.nn.sigmoid(g)) * u).astype(BF16))
        for k in range(2):
            _rows_to_tiles(_token_rows(y_ref, k * half, half), _dot(hidden[k], wdb[...]))

    @pl.when(t >= nt)
    def _():
        y_ref[...] = jnp.zeros_like(y_ref)


def _experts(tiles, n_tiles, xs, wg, wu, wd):
    n_rows = xs.shape[0] // ROW_TILE
    any_spec = pl.BlockSpec(memory_space=pl.ANY)
    return pl.pallas_call(
        _expert_kernel,
        grid_spec=pltpu.PrefetchScalarGridSpec(
            num_scalar_prefetch=2,
            grid=(n_rows // TM_EXPERT,),
            in_specs=[any_spec, any_spec, any_spec, any_spec],
            out_specs=pl.BlockSpec((TM_EXPERT * ROW_TILE, LANES), lambda t, *_: (t, 0)),
            scratch_shapes=[pltpu.VMEM((X_SLOTS, TM_EXPERT * ROW_TILE, LANES), F32),
                            pltpu.VMEM((2, D_MODEL, D_EXPERT), F32),
                            pltpu.VMEM((2, D_MODEL, D_EXPERT), F32),
                            pltpu.VMEM((2, D_EXPERT, D_MODEL), F32),
                            pltpu.VMEM((D_MODEL, D_EXPERT), BF16),
                            pltpu.VMEM((D_MODEL, D_EXPERT), BF16),
                            pltpu.VMEM((D_EXPERT, D_MODEL), BF16),
                            pltpu.SMEM((4,), jnp.int32),
                            pltpu.SemaphoreType.DMA((2,)),
                            pltpu.SemaphoreType.DMA((X_SLOTS,))]),
        out_shape=jax.ShapeDtypeStruct((n_rows * ROW_TILE, LANES), F32),
        compiler_params=pltpu.CompilerParams(dimension_semantics=("arbitrary",),
                                             vmem_limit_bytes=VMEM_LIMIT),
        name="expert_mlp",
    )(tiles, n_tiles, xs, wg, wu, wd)


def _combine_kernel(dest_ref, h_ref, rw_ref, fg_ref, y_ref, o_ref, buf, sems):
    tm = TM_COMBINE
    i = pl.program_id(0)
    n_steps = pl.num_programs(0)
    n = n_steps * tm
    cur = i % 2

    def fetch(step, half):
        def body(r, c):
            for s in range(2):
                pltpu.make_async_copy(_token_rows(y_ref, dest_ref[s * n + step * tm + r], 1),
                                      _token_rows(buf.at[half, s], r, 1),
                                      sems.at[half]).start(priority=s)
            return c

        lax.fori_loop(0, tm, body, 0, unroll=8)

    @pl.when(i == 0)
    def _():
        fetch(0, 0)

    @pl.when(i + 1 < n_steps)
    def _():
        fetch(i + 1, 1 - cur)

    for s in range(2):
        pltpu.make_async_copy(_token_rows(y_ref, 0, tm), buf.at[cur, s], sems.at[cur]).wait()
    rw = rw_ref[...]
    out = (h_ref[...] + rw[:, 0:1] * _tiles_to_rows(buf.at[cur, 0], tm)
           + rw[:, 1:2] * _tiles_to_rows(buf.at[cur, 1], tm))
    o_ref[...] = _rms(out, fg_ref[...])


def _combine(dest, h, rw, final_g, ys):
    n = h.shape[0]
    return pl.pallas_call(
        _combine_kernel,
        grid_spec=pltpu.PrefetchScalarGridSpec(
            num_scalar_prefetch=1,
            grid=(n // TM_COMBINE,),
            in_specs=[pl.BlockSpec((TM_COMBINE, D_MODEL), lambda i, d: (i, 0)),
                      pl.BlockSpec((TM_COMBINE, LANES), lambda i, d: (i, 0)),
                      pl.BlockSpec((1, D_MODEL), lambda i, d: (0, 0)),
                      pl.BlockSpec(memory_space=pl.ANY)],
            out_specs=pl.BlockSpec((TM_COMBINE, D_MODEL), lambda i, d: (i, 0)),
            scratch_shapes=[pltpu.VMEM((2, 2, TM_COMBINE * ROW_TILE, LANES), F32),
                            pltpu.SemaphoreType.DMA((2,))]),
        out_shape=jax.ShapeDtypeStruct((n, D_MODEL), F32),
        compiler_params=pltpu.CompilerParams(dimension_semantics=("arbitrary",),
                                             vmem_limit_bytes=VMEM_LIMIT),
        name="combine",
    )(dest, h, rw, final_g, ys)


def _schedule(counts):
    tiles = (counts + TM_EXPERT - 1) // TM_EXPERT
    tile_end = jnp.cumsum(tiles)
    offsets = (tile_end - tiles) * TM_EXPERT
    return tiles, offsets, tile_end[-1:]


def _layer(x, attn_g, w_in, sg_g, w_sp, b_sp, sb_g, sg_out_g, w_out, ffn_g,
           w_rg, b_rg, w_re, b_re, w_gate, w_up, w_down):
    batch, seq, _ = x.shape
    n = batch * seq
    x2 = x.reshape(n, D_MODEL)
    row = lambda v: v.reshape(1, -1)

    bsp_full = jnp.repeat(b_sp.T, HEAD_DIM, axis=1)
    qkv, sgn = _inproj(x2, row(attn_g), w_in.astype(BF16), row(sg_g), w_sp, bsp_full, row(sg_out_g))
    sb = _attention(qkv, batch, seq).reshape(n, SB_WIDTH)

    pad_lanes = lambda v, width: jnp.pad(v, [(0, 0)] * (v.ndim - 1) + [(0, width - v.shape[-1])])
    w_r = jnp.concatenate([pad_lanes(w_rg, ROUTER_LANE0),
                           jnp.transpose(w_re, (1, 0, 2)).reshape(D_MODEL, N_EXPERTS)], axis=1)
    w_r = pad_lanes(w_r, LANES)
    wr_hi = w_r.astype(BF16)
    wr_lo = (w_r - wr_hi.astype(F32)).astype(BF16)
    wr2 = jnp.concatenate([wr_hi, wr_lo], axis=1)
    b_r = pad_lanes(jnp.concatenate([pad_lanes(b_rg, ROUTER_LANE0), b_re.reshape(-1)]), LANES)

    h, lg = _mix(sb, sgn, x2, row(sb_g), w_out.astype(BF16), row(ffn_g), wr2, row(b_r))
    ri, rw, cnt = _route(lg)

    counts = cnt[:, 0].astype(jnp.int32)
    n_rows = 2 * n + N_EXPERTS * TM_EXPERT
    tiles, offsets, n_tiles = _schedule(counts)
    expert, rank = ri[0:2], ri[2:4]
    is_e = expert[None] == jnp.arange(N_EXPERTS, dtype=jnp.int32)[:, None, None]
    dest = (jnp.sum(jnp.where(is_e, offsets[:, None, None], 0), axis=0) + rank).reshape(-1)
    pad_start = offsets + counts
    pad_count = (-counts) % TM_EXPERT

    xs = _dispatch(dest, pad_start, pad_count, n_tiles, h, row(ffn_g), n_rows)
    ys = _experts(tiles, n_tiles, xs,
                  w_gate.reshape(N_EXPERTS, D_MODEL, D_EXPERT),
                  w_up.reshape(N_EXPERTS, D_MODEL, D_EXPERT),
                  w_down.reshape(N_EXPERTS, D_EXPERT, D_MODEL))
    return dest, h, rw, ys


def kernel(x, attn_norm_g, w_in, sg_norm_g, w_spatial, b_spatial, sb_out_norm_g, sg_out_norm_g,
           w_out, ffn_norm_g, w_router_group, b_router_group, w_router_expert, b_router_expert,
           w_gate, w_up, w_down, final_norm_g):
    assert attn_norm_g.shape[0] == 1, "single-layer problem"
    batch, seq, _ = x.shape
    dest, h, rw, ys = _layer(x, attn_norm_g[0], w_in[0], sg_norm_g[0], w_spatial[0], b_spatial[0],
                             sb_out_norm_g[0], sg_out_norm_g[0], w_out[0], ffn_norm_g[0],
                             w_router_group[0], b_router_group[0], w_router_expert[0],
                             b_router_expert[0], w_gate[0], w_up[0], w_down[0])
    out = _combine(dest, h, rw, final_norm_g.reshape(1, -1), ys)
    return out.reshape(batch, seq, D_MODEL)
```

```python
import functools
import math

import jax
import jax.numpy as jnp
from jax import lax
from jax.experimental import pallas as pl
from jax.experimental.pallas import tpu as pltpu

D_MODEL = 1024
HEAD_DIM = 64
SB_WIDTH = 512
SG_WIDTH = 512
SG_HEADS = 8
D_IN = 3 * SB_WIDTH + 2 * SG_WIDTH
CHUNK = 128
N_GROUPS = 4
EXPERTS_PER_GROUP = 8
N_EXPERTS = N_GROUPS * EXPERTS_PER_GROUP
D_EXPERT = 512
EPS = 1e-6
F32_EXP_UNDERFLOW = 110.0

LANES = 128
SUBLANES = 8
ROW_TILE = D_MODEL // LANES
assert ROW_TILE == SUBLANES
HEAD_PAIR = 2 * HEAD_DIM
ROUTER_LANE0 = SUBLANES
ROUTER_ROWS = ROUTER_LANE0 + N_EXPERTS
assert EXPERTS_PER_GROUP == SUBLANES and N_GROUPS <= ROUTER_LANE0

TM_PROJ = 1024
TQ_ATTN = 256
TM_MIX = 1024
TM_ROUTE = 1024
TM_DISPATCH = 512
TM_EXPERT = 512
TM_COMBINE = 256
VMEM_LIMIT = 48 * 1024 * 1024

F32 = jnp.float32
BF16 = jnp.bfloat16


def _rms(x, g):
    return x * lax.rsqrt(jnp.mean(x * x, axis=-1, keepdims=True) + EPS) * g


def _gelu(x):
    c = math.sqrt(2.0 / math.pi)
    return x * (0.5 * (1.0 + jnp.tanh(c * (x + 0.044715 * (x * x * x)))))


def _softplus(z):
    return jnp.maximum(z, 0.0) + jnp.log(1.0 + jnp.exp(-jnp.abs(z)))


def _dot(a, b):
    return jnp.dot(a, b, preferred_element_type=F32)


def _rows_to_tiles(ref, x):
    m = x.shape[0]
    for k in range(ROW_TILE):
        ref[pl.ds(k, m, stride=ROW_TILE), :] = x[:, k * LANES:(k + 1) * LANES]


def _tiles_to_rows(ref, m):
    return jnp.concatenate([ref[pl.ds(k, m, stride=ROW_TILE), :] for k in range(ROW_TILE)], axis=1)


def _token_rows(ref, first_token, n_tokens):
    return ref.at[pl.ds(pl.multiple_of(first_token * ROW_TILE, ROW_TILE), n_tokens * ROW_TILE)]


def _split_bf16(x):
    hi = x.astype(BF16)
    lo = (x - hi.astype(F32)).astype(BF16)
    return hi, lo


def _inproj_kernel(x_ref, g_ref, w_ref, sgg_ref, wsp_ref, bsp_ref, sgog_ref, qkv_ref, sgn_ref,
                   gu_ref, vgn_ref, sg_ref):
    tm = TM_PROJ
    hb = _rms(x_ref[...], g_ref[...]).astype(BF16)
    gv = _gelu(_dot(hb, w_ref[:, 3 * SB_WIDTH + SG_WIDTH:D_IN]))
    vgn_ref[...] = _rms(gv, sgg_ref[...]).astype(BF16)
    gu_ref[...] = _gelu(_dot(hb, w_ref[:, 3 * SB_WIDTH:3 * SB_WIDTH + SG_WIDTH]))
    q = _dot(hb, w_ref[:, 0:SB_WIDTH]) * (1.0 / math.sqrt(HEAD_DIM))
    qkv_ref[:, 0:SB_WIDTH] = q.astype(BF16)
    qkv_ref[:, SB_WIDTH:2 * SB_WIDTH] = _dot(hb, w_ref[:, SB_WIDTH:2 * SB_WIDTH]).astype(BF16)

    lane = lax.broadcasted_iota(jnp.int32, (1, LANES), 1)
    first = lane < HEAD_DIM
    zero = jnp.zeros((), BF16)
    r_c = lax.broadcasted_iota(jnp.int32, (CHUNK, CHUNK), 0)
    c_c = lax.broadcasted_iota(jnp.int32, (CHUNK, CHUNK), 1)
    tril = r_c >= c_c
    n_pairs = SG_WIDTH // HEAD_PAIR
    w_pairs = []
    for p in range(n_pairs):
        w0 = jnp.where(tril, wsp_ref[2 * p], 0.0).astype(BF16)
        w1 = jnp.where(tril, wsp_ref[2 * p + 1], 0.0).astype(BF16)
        w_pairs.append(jnp.concatenate([w0, w1], axis=1))
    bsp = bsp_ref[...]
    for c in range(tm // CHUNK):
        rows = slice(c * CHUNK, (c + 1) * CHUNK)
        for p in range(n_pairs):
            cols = slice(p * HEAD_PAIR, (p + 1) * HEAD_PAIR)
            vg = vgn_ref[rows, cols]
            rhs = jnp.concatenate([jnp.where(first, vg, zero), jnp.where(first, zero, vg)], axis=0)
            mixed = _dot(w_pairs[p], rhs) + bsp[:, cols]
            sg_ref[rows, cols] = gu_ref[rows, cols] * mixed
    qkv_ref[:, 2 * SB_WIDTH:3 * SB_WIDTH] = _dot(hb, w_ref[:, 2 * SB_WIDTH:3 * SB_WIDTH]).astype(BF16)
    sgn_ref[...] = _rms(sg_ref[...], sgog_ref[...]).astype(BF16)


def _inproj(x2, attn_g, w_in_b, sg_g, wsp, bsp_full, sg_out_g):
    n = x2.shape[0]
    row = lambda i: (i, 0)
    const = lambda i: (0, 0)
    return pl.pallas_call(
        _inproj_kernel,
        grid=(n // TM_PROJ,),
        in_specs=[pl.BlockSpec((TM_PROJ, D_MODEL), row),
                  pl.BlockSpec((1, D_MODEL), const),
                  pl.BlockSpec((D_MODEL, D_IN), const),
                  pl.BlockSpec((1, SG_WIDTH), const),
                  pl.BlockSpec((SG_HEADS, CHUNK, CHUNK), lambda i: (0, 0, 0)),
                  pl.BlockSpec((CHUNK, SG_WIDTH), const),
                  pl.BlockSpec((1, SG_WIDTH), const)],
        out_specs=[pl.BlockSpec((TM_PROJ, 3 * SB_WIDTH), row),
                   pl.BlockSpec((TM_PROJ, SG_WIDTH), row)],
        out_shape=[jax.ShapeDtypeStruct((n, 3 * SB_WIDTH), BF16),
                   jax.ShapeDtypeStruct((n, SG_WIDTH), BF16)],
        scratch_shapes=[pltpu.VMEM((TM_PROJ, SG_WIDTH), F32),
                        pltpu.VMEM((TM_PROJ, SG_WIDTH), BF16),
                        pltpu.VMEM((TM_PROJ, SG_WIDTH), F32)],
        compiler_params=pltpu.CompilerParams(dimension_semantics=("arbitrary",),
                                             vmem_limit_bytes=VMEM_LIMIT),
        name="inproj",
    )(x2, attn_g, w_in_b, sg_g, wsp, bsp_full, sg_out_g)


def _attn_kernel(q_ref, k_ref, v_ref, o_ref, q2_ref, carry_ref):
    t = TQ_ATTN
    n_pairs = SB_WIDTH // HEAD_PAIR
    qi = pl.program_id(1)
    lane = lax.broadcasted_iota(jnp.int32, (1, HEAD_PAIR), 1)
    head_lanes = (lane < HEAD_DIM, lane >= HEAD_DIM)
    zero = jnp.zeros((), BF16)
    for p in range(n_pairs):
        qp = q_ref[0, :, p * HEAD_PAIR:(p + 1) * HEAD_PAIR]
        for h in range(2):
            q2_ref[(2 * p + h) * t:(2 * p + h + 1) * t, :] = jnp.where(head_lanes[h], qp, zero)
    r_idx = lax.broadcasted_iota(jnp.int32, (t, t), 0)
    c_idx = lax.broadcasted_iota(jnp.int32, (t, t), 1)
    suffix = (r_idx > c_idx).astype(BF16)
    suffix2 = jnp.concatenate([suffix, suffix], axis=0)
    causal = c_idx < r_idx

    o_ref[...] = jnp.zeros_like(o_ref)
    carry_ref[...] = jnp.zeros_like(carry_ref)

    def block(j, diag):
        start = pl.multiple_of(j * t, t)
        mask2 = jnp.concatenate([causal, causal], axis=0) if diag else None
        st = [dict() for _ in range(n_pairs)]

        def scores(p):
            d = st[p]
            d["cols"] = slice(p * HEAD_PAIR, (p + 1) * HEAD_PAIR)
            d["rows"] = slice(2 * p * t, (2 * p + 2) * t)
            kb = k_ref[0, pl.ds(start, t), d["cols"]]
            z = lax.dot_general(q2_ref[d["rows"], :], kb, (((1,), (1,)), ((), ())),
                                preferred_element_type=F32)
            sp = _softplus(z)
            nl = jnp.where(mask2, sp, 0.0) if diag else sp
            hi, lo = _split_bf16(nl)
            d["hl"] = jnp.concatenate([hi, lo], axis=1)
            d["log_beta"] = z - sp
            d["nl0"] = nl[:, 0:1]

        def weights(p):
            d = st[p]
            hl = d["hl"]
            after = jnp.concatenate([_dot(hl[0:t], suffix2), _dot(hl[t:2 * t], suffix2)], axis=0)
            carry = carry_ref[d["rows"], :]
            a = jnp.exp(d["log_beta"] - after - carry)
            if diag:
                a = jnp.where(mask2, a, 0.0)
            a = a.astype(BF16)
            d["a2"] = jnp.concatenate([a[0:t], a[t:2 * t]], axis=1)
            carry_ref[d["rows"], :] = carry + after[:, 0:1] + d["nl0"]

        def values(p):
            d = st[p]
            vb = v_ref[0, pl.ds(start, t), d["cols"]]
            v2 = jnp.concatenate([jnp.where(head_lanes[0], vb, zero),
                                  jnp.where(head_lanes[1], vb, zero)], axis=0)
            o_ref[0, :, d["cols"]] += _dot(d["a2"], v2)

        for step in range(n_pairs + 2):
            if step < n_pairs:
                scores(step)
            if 0 <= step - 1 < n_pairs:
                weights(step - 1)
            if 0 <= step - 2 < n_pairs:
                values(step - 2)

    def live():
        return jnp.min(carry_ref[...]) < F32_EXP_UNDERFLOW

    block(qi, True)

    def body(state):
        it, _ = state
        block(qi - 1 - it, False)
        return it + 1, live()

    lax.while_loop(lambda s: (s[0] < qi) & s[1], body, (jnp.int32(0), live()))


def _attention(qkv, batch, seq):
    qkv3 = qkv.reshape(batch, seq, 3 * SB_WIDTH)
    n_heads = SB_WIDTH // HEAD_DIM
    return pl.pallas_call(
        _attn_kernel,
        grid=(batch, seq // TQ_ATTN),
        in_specs=[pl.BlockSpec((1, TQ_ATTN, SB_WIDTH), lambda b, i: (b, i, 0)),
                  pl.BlockSpec((1, seq, SB_WIDTH), lambda b, i: (b, 0, 1)),
                  pl.BlockSpec((1, seq, SB_WIDTH), lambda b, i: (b, 0, 2))],
        out_specs=pl.BlockSpec((1, TQ_ATTN, SB_WIDTH), lambda b, i: (b, i, 0)),
        out_shape=jax.ShapeDtypeStruct((batch, seq, SB_WIDTH), F32),
        scratch_shapes=[pltpu.VMEM((n_heads * TQ_ATTN, HEAD_PAIR), BF16),
                        pltpu.VMEM((n_heads * TQ_ATTN, 1), F32)],
        compiler_params=pltpu.CompilerParams(dimension_semantics=("arbitrary",) * 2,
                                             vmem_limit_bytes=VMEM_LIMIT),
        name="sb_attention",
    )(qkv3, qkv3, qkv3)


def _mix_kernel(sb_ref, sgn_ref, x_ref, sbg_ref, wout_ref, ffng_ref, wr2_ref, br_ref,
                h_ref, lg_ref):
    sbn = _rms(sb_ref[...], sbg_ref[...]).astype(BF16)
    h = x_ref[...] + _dot(sbn, wout_ref[0:SB_WIDTH, :]) + _dot(sgn_ref[...], wout_ref[SB_WIDTH:, :])
    h_ref[...] = h
    hn = _rms(h, ffng_ref[...])

    hn_hi, hn_lo = _split_bf16(hn)
    both = _dot(hn_hi, wr2_ref[...])
    logits = both[:, 0:LANES] + both[:, LANES:] + _dot(hn_lo, wr2_ref[:, 0:LANES]) + br_ref[...]
    lg_ref[...] = logits.T[0:ROUTER_ROWS, :]


def _route_kernel(lg_ref, ri_ref, rw_ref, cnt_ref, count_ref):
    tr = TM_ROUTE
    i = pl.program_id(0)

    @pl.when(i == 0)
    def _():
        count_ref[...] = jnp.zeros_like(count_ref)

    neg = jnp.float32(-jnp.inf)
    row8 = lax.broadcasted_iota(jnp.int32, (SUBLANES, tr), 0)

    def top(v):
        m = jnp.max(v, axis=0, keepdims=True)
        return m, jnp.min(jnp.where(v == m, row8, SUBLANES), axis=0, keepdims=True)

    def group_rows(g):
        return lg_ref[ROUTER_LANE0 + g * EXPERTS_PER_GROUP:ROUTER_LANE0 + (g + 1) * EXPERTS_PER_GROUP, :]

    gl = jnp.where(row8 < N_GROUPS, lg_ref[0:SUBLANES, :], neg)
    gmax, gidx = top(gl)
    gweight = 1.0 / jnp.sum(jnp.exp(gl - gmax), axis=0, keepdims=True)
    el = group_rows(0)
    for g in range(1, N_GROUPS):
        el = jnp.where(gidx == g, group_rows(g), el)
    m1, i1 = top(el)
    m2, i2 = top(jnp.where(row8 == i1, neg, el))
    t21 = jnp.exp(m2 - m1)
    w1 = gweight / (1.0 + t21)
    w2 = gweight * t21 / (1.0 + t21)
    e1 = gidx * EXPERTS_PER_GROUP + i1
    e2 = gidx * EXPERTS_PER_GROUP + i2

    row_e = lax.broadcasted_iota(jnp.int32, (N_EXPERTS, tr), 0)
    sel1 = row_e == e1
    sel2 = row_e == e2
    onehot = jnp.where(sel1 | sel2, 1.0, 0.0)
    r_t = lax.broadcasted_iota(jnp.int32, (tr, tr), 0)
    c_t = lax.broadcasted_iota(jnp.int32, (tr, tr), 1)
    before = (r_t < c_t).astype(BF16)
    running = count_ref[:, 0:1] + _dot(onehot.astype(BF16), before)
    rank1 = jnp.sum(jnp.where(sel1, running, 0.0), axis=0, keepdims=True)
    rank2 = jnp.sum(jnp.where(sel2, running, 0.0), axis=0, keepdims=True)
    new_count = count_ref[:, 0:1] + jnp.sum(onehot, axis=1, keepdims=True)
    count_ref[...] = jnp.broadcast_to(new_count, count_ref.shape)
    cnt_ref[...] = jnp.broadcast_to(new_count, cnt_ref.shape)

    ri_ref[...] = jnp.where(row8 == 0, e1, jnp.where(row8 == 1, e2, jnp.where(
        row8 == 2, rank1.astype(jnp.int32), jnp.where(row8 == 3, rank2.astype(jnp.int32), 0))))
    row128 = lax.broadcasted_iota(jnp.int32, (LANES, tr), 0)
    rw_ref[...] = jnp.where(row128 == 0, w1, jnp.where(row128 == 1, w2, 0.0)).T


def _route(lg):
    n = lg.shape[1]
    return pl.pallas_call(
        _route_kernel,
        grid=(n // TM_ROUTE,),
        in_specs=[pl.BlockSpec((ROUTER_ROWS, TM_ROUTE), lambda i: (0, i))],
        out_specs=[pl.BlockSpec((SUBLANES, TM_ROUTE), lambda i: (0, i)),
                   pl.BlockSpec((TM_ROUTE, LANES), lambda i: (i, 0)),
                   pl.BlockSpec((N_EXPERTS, LANES), lambda i: (0, 0))],
        out_shape=[jax.ShapeDtypeStruct((SUBLANES, n), jnp.int32),
                   jax.ShapeDtypeStruct((n, LANES), F32),
                   jax.ShapeDtypeStruct((N_EXPERTS, LANES), F32)],
        scratch_shapes=[pltpu.VMEM((N_EXPERTS, LANES), F32)],
        compiler_params=pltpu.CompilerParams(dimension_semantics=("arbitrary",),
                                             vmem_limit_bytes=VMEM_LIMIT),
        name="route",
    )(lg)


def _mix(sb, sgn, x2, sb_g, w_out_b, ffn_g, wr2, br):
    n = x2.shape[0]
    row = lambda i: (i, 0)
    const = lambda i: (0, 0)
    return pl.pallas_call(
        _mix_kernel,
        grid=(n // TM_MIX,),
        in_specs=[pl.BlockSpec((TM_MIX, SB_WIDTH), row),
                  pl.BlockSpec((TM_MIX, SG_WIDTH), row),
                  pl.BlockSpec((TM_MIX, D_MODEL), row),
                  pl.BlockSpec((1, SB_WIDTH), const),
                  pl.BlockSpec((D_MODEL, D_MODEL), const),
                  pl.BlockSpec((1, D_MODEL), const),
                  pl.BlockSpec((D_MODEL, 2 * LANES), const),
                  pl.BlockSpec((1, LANES), const)],
        out_specs=[pl.BlockSpec((TM_MIX, D_MODEL), row),
                   pl.BlockSpec((ROUTER_ROWS, TM_MIX), lambda i: (0, i))],
        out_shape=[jax.ShapeDtypeStruct((n, D_MODEL), F32),
                   jax.ShapeDtypeStruct((ROUTER_ROWS, n), F32)],
        compiler_params=pltpu.CompilerParams(dimension_semantics=("arbitrary",),
                                             vmem_limit_bytes=VMEM_LIMIT),
        name="mix_router",
    )(sb, sgn, x2, sb_g, w_out_b, ffn_g, wr2, br)


_PAD_BITS = tuple(1 << b for b in reversed(range(TM_EXPERT.bit_length() - 1)))


def _dispatch_kernel(dest_ref, pad_start_ref, pad_count_ref, nt_ref, h_ref, g_ref, zeros_ref, xs_ref,
                     hn_ref, sem, zsem):
    tm = TM_DISPATCH
    i = pl.program_id(0)
    n_steps = pl.num_programs(0) - 1
    n = n_steps * tm
    base = (i - 1) * tm
    prev = hn_ref.at[lax.rem(i + 1, 2)]
    n_tiles_max = xs_ref.shape[0] // (TM_EXPERT * ROW_TILE)

    def pad_copies(do):
        for e in range(N_EXPERTS):
            start = pad_start_ref[e]
            count = pad_count_ref[e]
            for bit in _PAD_BITS:
                @pl.when((count & bit) != 0)
                def _(start=start, bit=bit):
                    do(pltpu.make_async_copy(_token_rows(zeros_ref, 0, bit),
                                             _token_rows(xs_ref, start, bit), zsem))
                start = start + (count & bit)
        for k in range(N_EXPERTS):
            tile = nt_ref[0] + k

            @pl.when(tile < n_tiles_max)
            def _(tile=tile):
                do(pltpu.make_async_copy(zeros_ref, _token_rows(xs_ref, tile * TM_EXPERT, TM_EXPERT), zsem))

    @pl.when(i == 0)
    def _():
        pad_copies(lambda cp: cp.start())

    @pl.when(i > 0)
    def _():
        def body(r, c):
            src = _token_rows(prev, r, 1)
            for s in range(2):
                pltpu.make_async_copy(src, _token_rows(xs_ref, dest_ref[s * n + base + r], 1),
                                      sem).start(priority=s)
            return c

        lax.fori_loop(0, tm, body, 0, unroll=8)

    @pl.when(i < n_steps)
    def _():
        _rows_to_tiles(hn_ref.at[lax.rem(i, 2)], _rms(h_ref[...], g_ref[...]))

    @pl.when(i > 0)
    def _():
        for _ in range(2):
            pltpu.make_async_copy(prev, _token_rows(xs_ref, 0, tm), sem).wait()

    @pl.when(i == n_steps)
    def _():
        pad_copies(lambda cp: cp.wait())


def _dispatch(dest, pad_start, pad_count, n_tiles, h, ffn_g, n_rows):
    n_steps = h.shape[0] // TM_DISPATCH
    zeros = jnp.zeros((TM_EXPERT * ROW_TILE, LANES), F32)
    return pl.pallas_call(
        _dispatch_kernel,
        grid_spec=pltpu.PrefetchScalarGridSpec(
            num_scalar_prefetch=4,
            grid=(n_steps + 1,),
            in_specs=[pl.BlockSpec((TM_DISPATCH, D_MODEL), lambda i, *_: (jnp.minimum(i, n_steps - 1), 0)),
                      pl.BlockSpec((1, D_MODEL), lambda i, *_: (0, 0)),
                      pl.BlockSpec(memory_space=pl.ANY)],
            out_specs=pl.BlockSpec(memory_space=pl.ANY),
            scratch_shapes=[pltpu.VMEM((2, TM_DISPATCH * ROW_TILE, LANES), F32),
                            pltpu.SemaphoreType.DMA, pltpu.SemaphoreType.DMA]),
        out_shape=jax.ShapeDtypeStruct((n_rows * ROW_TILE, LANES), F32),
        compiler_params=pltpu.CompilerParams(dimension_semantics=("arbitrary",),
                                             vmem_limit_bytes=VMEM_LIMIT),
        name="dispatch",
    )(dest, pad_start, pad_count, n_tiles, h, ffn_g, zeros)


X_SLOTS = 3


def _expert_kernel(tiles_ref, nt_ref, xs_ref, wg_ref, wu_ref, wd_ref, y_ref,
                   x_buf, sg_buf, su_buf, sd_buf, wgb, wub, wdb, state, w_sems, x_sems):
    tm = TM_EXPERT
    t = pl.program_id(0)
    nt = nt_ref[0]

    def x_copy(tile):
        slot = lax.rem(tile, X_SLOTS)
        return pltpu.make_async_copy(_token_rows(xs_ref, tile * tm, tm), x_buf.at[slot], x_sems.at[slot])

    def weight_copies(e, slot):
        return (pltpu.make_async_copy(wg_ref.at[e], sg_buf.at[slot], w_sems.at[slot]),
                pltpu.make_async_copy(wu_ref.at[e], su_buf.at[slot], w_sems.at[slot]),
                pltpu.make_async_copy(wd_ref.at[e], sd_buf.at[slot], w_sems.at[slot]))

    def next_with_rows(e):
        return lax.while_loop(lambda k: (k < N_EXPERTS) & (tiles_ref[jnp.minimum(k, N_EXPERTS - 1)] == 0),
                              lambda k: k + 1, e + 1)

    @pl.when(t == 0)
    def _():
        first = next_with_rows(jnp.int32(-1))
        state[0] = jnp.int32(-1)
        state[1] = jnp.int32(0)
        state[2] = jnp.int32(1)
        state[3] = first
        for cp in weight_copies(first, 0):
            cp.start()
        x_copy(0).start()

        @pl.when(nt > 1)
        def _():
            x_copy(1).start()

    @pl.when(t + 2 < nt)
    def _():
        x_copy(t + 2).start()

    @pl.when(t < nt)
    def _():
        @pl.when(state[1] == 0)
        def _():
            e = state[3]
            slot = 1 - state[2]
            nxt = next_with_rows(e)
            state[0] = e
            state[1] = tiles_ref[e]
            state[2] = slot
            state[3] = nxt
            for cp in weight_copies(e, slot):
                cp.wait()

            @pl.when(nxt < N_EXPERTS)
            def _():
                for cp in weight_copies(nxt, 1 - slot):
                    cp.start()

            wgb[...] = sg_buf[slot].astype(BF16)
            wub[...] = su_buf[slot].astype(BF16)
            wdb[...] = sd_buf[slot].astype(BF16)

        state[1] = state[1] - 1
        x_copy(t).wait()
        x = _tiles_to_rows(x_buf.at[lax.rem(t, X_SLOTS)], tm).astype(BF16)
        g = _dot(x, wgb[...])
        u = _dot(x, wub[...])
        hidden = (g * jax.nn.sigmoid(g)) * u
        _rows_to_tiles(y_ref, _dot(hidden.astype(BF16), wdb[...]))

    @pl.when(t >= nt)
    def _():
        y_ref[...] = jnp.zeros_like(y_ref)


def _experts(tiles, n_tiles, xs, wg, wu, wd):
    n_rows = xs.shape[0] // ROW_TILE
    any_spec = pl.BlockSpec(memory_space=pl.ANY)
    return pl.pallas_call(
        _expert_kernel,
        grid_spec=pltpu.PrefetchScalarGridSpec(
            num_scalar_prefetch=2,
            grid=(n_rows // TM_EXPERT,),
            in_specs=[any_spec, any_spec, any_spec, any_spec],
            out_specs=pl.BlockSpec((TM_EXPERT * ROW_TILE, LANES), lambda t, *_: (t, 0)),
            scratch_shapes=[pltpu.VMEM((X_SLOTS, TM_EXPERT * ROW_TILE, LANES), F32),
                            pltpu.VMEM((2, D_MODEL, D_EXPERT), F32),
                            pltpu.VMEM((2, D_MODEL, D_EXPERT), F32),
                            pltpu.VMEM((2, D_EXPERT, D_MODEL), F32),
                            pltpu.VMEM((D_MODEL, D_EXPERT), BF16),
                            pltpu.VMEM((D_MODEL, D_EXPERT), BF16),
                            pltpu.VMEM((D_EXPERT, D_MODEL), BF16),
                            pltpu.SMEM((4,), jnp.int32),
                            pltpu.SemaphoreType.DMA((2,)),
                            pltpu.SemaphoreType.DMA((X_SLOTS,))]),
        out_shape=jax.ShapeDtypeStruct((n_rows * ROW_TILE, LANES), F32),
        compiler_params=pltpu.CompilerParams(dimension_semantics=("arbitrary",),
                                             vmem_limit_bytes=VMEM_LIMIT),
        name="expert_mlp",
    )(tiles, n_tiles, xs, wg, wu, wd)


def _combine_kernel(dest_ref, h_ref, rw_ref, fg_ref, y_ref, o_ref, buf, sems):
    tm = TM_COMBINE
    i = pl.program_id(0)
    n_steps = pl.num_programs(0)
    n = n_steps * tm
    cur = i % 2

    def fetch(step, half):
        def body(r, c):
            for s in range(2):
                pltpu.make_async_copy(_token_rows(y_ref, dest_ref[s * n + step * tm + r], 1),
                                      _token_rows(buf.at[half, s], r, 1),
                                      sems.at[half]).start(priority=s)
            return c

        lax.fori_loop(0, tm, body, 0, unroll=8)

    @pl.when(i == 0)
    def _():
        fetch(0, 0)

    @pl.when(i + 1 < n_steps)
    def _():
        fetch(i + 1, 1 - cur)

    for s in range(2):
        pltpu.make_async_copy(_token_rows(y_ref, 0, tm), buf.at[cur, s], sems.at[cur]).wait()
    rw = rw_ref[...]
    out = (h_ref[...] + rw[:, 0:1] * _tiles_to_rows(buf.at[cur, 0], tm)
           + rw[:, 1:2] * _tiles_to_rows(buf.at[cur, 1], tm))
    o_ref[...] = _rms(out, fg_ref[...])


def _combine(dest, h, rw, final_g, ys):
    n = h.shape[0]
    return pl.pallas_call(
        _combine_kernel,
        grid_spec=pltpu.PrefetchScalarGridSpec(
            num_scalar_prefetch=1,
            grid=(n // TM_COMBINE,),
            in_specs=[pl.BlockSpec((TM_COMBINE, D_MODEL), lambda i, d: (i, 0)),
                      pl.BlockSpec((TM_COMBINE, LANES), lambda i, d: (i, 0)),
                      pl.BlockSpec((1, D_MODEL), lambda i, d: (0, 0)),
                      pl.BlockSpec(memory_space=pl.ANY)],
            out_specs=pl.BlockSpec((TM_COMBINE, D_MODEL), lambda i, d: (i, 0)),
            scratch_shapes=[pltpu.VMEM((2, 2, TM_COMBINE * ROW_TILE, LANES), F32),
                            pltpu.SemaphoreType.DMA((2,))]),
        out_shape=jax.ShapeDtypeStruct((n, D_MODEL), F32),
        compiler_params=pltpu.CompilerParams(dimension_semantics=("arbitrary",),
                                             vmem_limit_bytes=VMEM_LIMIT),
        name="combine",
    )(dest, h, rw, final_g, ys)


def _schedule(counts):
    tiles = (counts + TM_EXPERT - 1) // TM_EXPERT
    tile_end = jnp.cumsum(tiles)
    offsets = (tile_end - tiles) * TM_EXPERT
    return tiles, offsets, tile_end[-1:]


def _layer(x, attn_g, w_in, sg_g, w_sp, b_sp, sb_g, sg_out_g, w_out, ffn_g,
           w_rg, b_rg, w_re, b_re, w_gate, w_up, w_down):
    batch, seq, _ = x.shape
    n = batch * seq
    x2 = x.reshape(n, D_MODEL)
    row = lambda v: v.reshape(1, -1)

    bsp_full = jnp.repeat(b_sp.T, HEAD_DIM, axis=1)
    qkv, sgn = _inproj(x2, row(attn_g), w_in.astype(BF16), row(sg_g), w_sp, bsp_full, row(sg_out_g))
    sb = _attention(qkv, batch, seq).reshape(n, SB_WIDTH)

    pad_lanes = lambda v, width: jnp.pad(v, [(0, 0)] * (v.ndim - 1) + [(0, width - v.shape[-1])])
    w_r = jnp.concatenate([pad_lanes(w_rg, ROUTER_LANE0),
                           jnp.transpose(w_re, (1, 0, 2)).reshape(D_MODEL, N_EXPERTS)], axis=1)
    w_r = pad_lanes(w_r, LANES)
    wr_hi = w_r.astype(BF16)
    wr_lo = (w_r - wr_hi.astype(F32)).astype(BF16)
    wr2 = jnp.concatenate([wr_hi, wr_lo], axis=1)
    b_r = pad_lanes(jnp.concatenate([pad_lanes(b_rg, ROUTER_LANE0), b_re.reshape(-1)]), LANES)

    h, lg = _mix(sb, sgn, x2, row(sb_g), w_out.astype(BF16), row(ffn_g), wr2, row(b_r))
    ri, rw, cnt = _route(lg)

    counts = cnt[:, 0].astype(jnp.int32)
    n_rows = 2 * n + N_EXPERTS * TM_EXPERT
    tiles, offsets, n_tiles = _schedule(counts)
    expert, rank = ri[0:2], ri[2:4]
    is_e = expert[None] == jnp.arange(N_EXPERTS, dtype=jnp.int32)[:, None, None]
    dest = (jnp.sum(jnp.where(is_e, offsets[:, None, None], 0), axis=0) + rank).reshape(-1)
    pad_start = offsets + counts
    pad_count = (-counts) % TM_EXPERT

    xs = _dispatch(dest, pad_start, pad_count, n_tiles, h, row(ffn_g), n_rows)
    ys = _experts(tiles, n_tiles, xs,
                  w_gate.reshape(N_EXPERTS, D_MODEL, D_EXPERT),
                  w_up.reshape(N_EXPERTS, D_MODEL, D_EXPERT),
                  w_down.reshape(N_EXPERTS, D_EXPERT, D_MODEL))
    return dest, h, rw, ys


def kernel(x, attn_norm_g, w_in, sg_norm_g, w_spatial, b_spatial, sb_out_norm_g, sg_out_norm_g,
           w_out, ffn_norm_g, w_router_group, b_router_group, w_router_expert, b_router_expert,
           w_gate, w_up, w_down, final_norm_g):
    assert attn_norm_g.shape[0] == 1, "single-layer problem"
    batch, seq, _ = x.shape
    dest, h, rw, ys = _layer(x, attn_norm_g[0], w_in[0], sg_norm_g[0], w_spatial[0], b_spatial[0],
                             sb_out_norm_g[0], sg_out_norm_g[0], w_out[0], ffn_norm_g[0],
                             w_router_group[0], b_router_group[0], w_router_expert[0],
                             b_router_expert[0], w_gate[0], w_up[0], w_down[0])
    out = _combine(dest, h, rw, final_norm_g.reshape(1, -1), ys)
    return out.reshape(batch, seq, D_MODEL)
```

```python
import functools
import math

import jax
import jax.numpy as jnp
from jax import lax
from jax.experimental import pallas as pl
from jax.experimental.pallas import tpu as pltpu

D_MODEL = 1024
HEAD_DIM = 64
SB_WIDTH = 512
SG_WIDTH = 512
SG_HEADS = 8
D_IN = 3 * SB_WIDTH + 2 * SG_WIDTH
CHUNK = 128
N_GROUPS = 4
EXPERTS_PER_GROUP = 8
N_EXPERTS = N_GROUPS * EXPERTS_PER_GROUP
D_EXPERT = 512
EPS = 1e-6
F32_EXP_UNDERFLOW = 110.0

LANES = 128
SUBLANES = 8
ROW_TILE = D_MODEL // LANES
assert ROW_TILE == SUBLANES
HEAD_PAIR = 2 * HEAD_DIM
ROUTER_LANE0 = SUBLANES
ROUTER_ROWS = ROUTER_LANE0 + N_EXPERTS
assert EXPERTS_PER_GROUP == SUBLANES and N_GROUPS <= ROUTER_LANE0

TM_PROJ = 1024
TQ_ATTN = 256
ATTN_TOP_ROWS = 176
TM_MIX = 1024
TM_ROUTE = 1024
TM_DISPATCH = 512
TM_EXPERT = 512
TM_COMBINE = 256
VMEM_LIMIT = 48 * 1024 * 1024

F32 = jnp.float32
BF16 = jnp.bfloat16


def _rms(x, g):
    return x * lax.rsqrt(jnp.mean(x * x, axis=-1, keepdims=True) + EPS) * g


def _gelu(x):
    c = math.sqrt(2.0 / math.pi)
    return x * (0.5 * (1.0 + jnp.tanh(c * (x + 0.044715 * (x * x * x)))))


def _softplus(z):
    return jnp.maximum(z, 0.0) + jnp.log(1.0 + jnp.exp(-jnp.abs(z)))


def _dot(a, b):
    return jnp.dot(a, b, preferred_element_type=F32)


def _rows_to_tiles(ref, x):
    m = x.shape[0]
    for k in range(ROW_TILE):
        ref[pl.ds(k, m, stride=ROW_TILE), :] = x[:, k * LANES:(k + 1) * LANES]


def _tiles_to_rows(ref, m):
    return jnp.concatenate([ref[pl.ds(k, m, stride=ROW_TILE), :] for k in range(ROW_TILE)], axis=1)


def _token_rows(ref, first_token, n_tokens):
    return ref.at[pl.ds(pl.multiple_of(first_token * ROW_TILE, ROW_TILE), n_tokens * ROW_TILE)]


def _split_bf16(x):
    hi = x.astype(BF16)
    lo = (x - hi.astype(F32)).astype(BF16)
    return hi, lo


def _inproj_kernel(x_ref, g_ref, w_ref, sgg_ref, wsp_ref, bsp_ref, sgog_ref, qkv_ref, sgn_ref,
                   gu_ref, vgn_ref, sg_ref):
    tm = TM_PROJ
    hb = _rms(x_ref[...], g_ref[...]).astype(BF16)
    gv = _gelu(_dot(hb, w_ref[:, 3 * SB_WIDTH + SG_WIDTH:D_IN]))
    vgn_ref[...] = _rms(gv, sgg_ref[...]).astype(BF16)
    gu_ref[...] = _gelu(_dot(hb, w_ref[:, 3 * SB_WIDTH:3 * SB_WIDTH + SG_WIDTH]))
    q = _dot(hb, w_ref[:, 0:SB_WIDTH]) * (1.0 / math.sqrt(HEAD_DIM))
    qkv_ref[:, 0:SB_WIDTH] = q.astype(BF16)
    qkv_ref[:, SB_WIDTH:2 * SB_WIDTH] = _dot(hb, w_ref[:, SB_WIDTH:2 * SB_WIDTH]).astype(BF16)

    lane = lax.broadcasted_iota(jnp.int32, (1, LANES), 1)
    first = lane < HEAD_DIM
    zero = jnp.zeros((), BF16)
    r_c = lax.broadcasted_iota(jnp.int32, (CHUNK, CHUNK), 0)
    c_c = lax.broadcasted_iota(jnp.int32, (CHUNK, CHUNK), 1)
    tril = r_c >= c_c
    n_pairs = SG_WIDTH // HEAD_PAIR
    w_pairs = []
    for p in range(n_pairs):
        w0 = jnp.where(tril, wsp_ref[2 * p], 0.0).astype(BF16)
        w1 = jnp.where(tril, wsp_ref[2 * p + 1], 0.0).astype(BF16)
        w_pairs.append(jnp.concatenate([w0, w1], axis=1))
    bsp = bsp_ref[...]
    for c in range(tm // CHUNK):
        rows = slice(c * CHUNK, (c + 1) * CHUNK)
        for p in range(n_pairs):
            cols = slice(p * HEAD_PAIR, (p + 1) * HEAD_PAIR)
            vg = vgn_ref[rows, cols]
            rhs = jnp.concatenate([jnp.where(first, vg, zero), jnp.where(first, zero, vg)], axis=0)
            mixed = _dot(w_pairs[p], rhs) + bsp[:, cols]
            sg_ref[rows, cols] = gu_ref[rows, cols] * mixed
    qkv_ref[:, 2 * SB_WIDTH:3 * SB_WIDTH] = _dot(hb, w_ref[:, 2 * SB_WIDTH:3 * SB_WIDTH]).astype(BF16)
    sgn_ref[...] = _rms(sg_ref[...], sgog_ref[...]).astype(BF16)


def _inproj(x2, attn_g, w_in_b, sg_g, wsp, bsp_full, sg_out_g):
    n = x2.shape[0]
    row = lambda i: (i, 0)
    const = lambda i: (0, 0)
    return pl.pallas_call(
        _inproj_kernel,
        grid=(n // TM_PROJ,),
        in_specs=[pl.BlockSpec((TM_PROJ, D_MODEL), row),
                  pl.BlockSpec((1, D_MODEL), const),
                  pl.BlockSpec((D_MODEL, D_IN), const),
                  pl.BlockSpec((1, SG_WIDTH), const),
                  pl.BlockSpec((SG_HEADS, CHUNK, CHUNK), lambda i: (0, 0, 0)),
                  pl.BlockSpec((CHUNK, SG_WIDTH), const),
                  pl.BlockSpec((1, SG_WIDTH), const)],
        out_specs=[pl.BlockSpec((TM_PROJ, 3 * SB_WIDTH), row),
                   pl.BlockSpec((TM_PROJ, SG_WIDTH), row)],
        out_shape=[jax.ShapeDtypeStruct((n, 3 * SB_WIDTH), BF16),
                   jax.ShapeDtypeStruct((n, SG_WIDTH), BF16)],
        scratch_shapes=[pltpu.VMEM((TM_PROJ, SG_WIDTH), F32),
                        pltpu.VMEM((TM_PROJ, SG_WIDTH), BF16),
                        pltpu.VMEM((TM_PROJ, SG_WIDTH), F32)],
        compiler_params=pltpu.CompilerParams(dimension_semantics=("arbitrary",),
                                             vmem_limit_bytes=VMEM_LIMIT),
        name="inproj",
    )(x2, attn_g, w_in_b, sg_g, wsp, bsp_full, sg_out_g)


def _attn_kernel(q_ref, k_ref, v_ref, o_ref, q2_ref, carry_ref):
    t = TQ_ATTN
    n_pairs = SB_WIDTH // HEAD_PAIR
    qi = pl.program_id(1)
    lane = lax.broadcasted_iota(jnp.int32, (1, HEAD_PAIR), 1)
    head_lanes = (lane < HEAD_DIM, lane >= HEAD_DIM)
    zero = jnp.zeros((), BF16)
    for p in range(n_pairs):
        qp = q_ref[0, :, p * HEAD_PAIR:(p + 1) * HEAD_PAIR]
        for h in range(2):
            q2_ref[(2 * p + h) * t:(2 * p + h + 1) * t, :] = jnp.where(head_lanes[h], qp, zero)
    r_idx = lax.broadcasted_iota(jnp.int32, (t, t), 0)
    c_idx = lax.broadcasted_iota(jnp.int32, (t, t), 1)
    suffix = (r_idx > c_idx).astype(BF16)
    suffix2 = jnp.concatenate([suffix, suffix], axis=0)
    causal = c_idx < r_idx

    o_ref[...] = jnp.zeros_like(o_ref)
    carry_ref[...] = jnp.zeros_like(carry_ref)

    def block(j, diag, m):
        start = pl.multiple_of(j * t, t)
        mask2 = jnp.concatenate([causal, causal], axis=0) if diag else None
        st = [dict() for _ in range(n_pairs)]

        def head_rows(p):
            return [slice((2 * p + h) * t, (2 * p + h) * t + m) for h in range(2)]

        def scores(p):
            d = st[p]
            d["cols"] = slice(p * HEAD_PAIR, (p + 1) * HEAD_PAIR)
            kb = k_ref[0, pl.ds(start, t), d["cols"]]
            q2 = jnp.concatenate([q2_ref[r, :] for r in head_rows(p)], axis=0)
            z = lax.dot_general(q2, kb, (((1,), (1,)), ((), ())),
                                preferred_element_type=F32)
            sp = _softplus(z)
            nl = jnp.where(mask2, sp, 0.0) if diag else sp
            hi, lo = _split_bf16(nl)
            d["hl"] = jnp.concatenate([hi, lo], axis=1)
            d["log_beta"] = z - sp
            d["nl0"] = nl[:, 0:1]

        def weights(p):
            d = st[p]
            hl = d["hl"]
            after = jnp.concatenate([_dot(hl[0:m], suffix2), _dot(hl[m:2 * m], suffix2)], axis=0)
            carry = jnp.concatenate([carry_ref[r, :] for r in head_rows(p)], axis=0)
            a = jnp.exp(d["log_beta"] - after - carry)
            if diag:
                a = jnp.where(mask2, a, 0.0)
            a = a.astype(BF16)
            d["a2"] = jnp.concatenate([a[0:m], a[m:2 * m]], axis=1)
            new_carry = carry + after[:, 0:1] + d["nl0"]
            for h, r in enumerate(head_rows(p)):
                carry_ref[r, :] = new_carry[h * m:(h + 1) * m]

        def values(p):
            d = st[p]
            vb = v_ref[0, pl.ds(start, t), d["cols"]]
            v2 = jnp.concatenate([jnp.where(head_lanes[0], vb, zero),
                                  jnp.where(head_lanes[1], vb, zero)], axis=0)
            o_ref[0, 0:m, d["cols"]] += _dot(d["a2"], v2)

        for step in range(n_pairs + 2):
            if step < n_pairs:
                scores(step)
            if 0 <= step - 1 < n_pairs:
                weights(step - 1)
            if 0 <= step - 2 < n_pairs:
                values(step - 2)

    top = ATTN_TOP_ROWS

    def flags():
        bottom = jnp.concatenate([carry_ref[hh * t + top:(hh + 1) * t, :] for hh in range(2 * n_pairs)], axis=0)
        return (jnp.min(carry_ref[...]) < F32_EXP_UNDERFLOW, jnp.min(bottom) >= F32_EXP_UNDERFLOW)

    block(qi, True, t)

    def body(state):
        it, _, bottom_done = state
        j = qi - 1 - it

        @pl.when(bottom_done)
        def _():
            block(j, False, top)

        @pl.when(jnp.logical_not(bottom_done))
        def _():
            block(j, False, t)

        return (it + 1,) + flags()

    lax.while_loop(lambda s: (s[0] < qi) & s[1], body, (jnp.int32(0),) + flags())


def _attention(qkv, batch, seq):
    qkv3 = qkv.reshape(batch, seq, 3 * SB_WIDTH)
    n_heads = SB_WIDTH // HEAD_DIM
    return pl.pallas_call(
        _attn_kernel,
        grid=(batch, seq // TQ_ATTN),
        in_specs=[pl.BlockSpec((1, TQ_ATTN, SB_WIDTH), lambda b, i: (b, i, 0)),
                  pl.BlockSpec((1, seq, SB_WIDTH), lambda b, i: (b, 0, 1)),
                  pl.BlockSpec((1, seq, SB_WIDTH), lambda b, i: (b, 0, 2))],
        out_specs=pl.BlockSpec((1, TQ_ATTN, SB_WIDTH), lambda b, i: (b, i, 0)),
        out_shape=jax.ShapeDtypeStruct((batch, seq, SB_WIDTH), F32),
        scratch_shapes=[pltpu.VMEM((n_heads * TQ_ATTN, HEAD_PAIR), BF16),
                        pltpu.VMEM((n_heads * TQ_ATTN, 1), F32)],
        compiler_params=pltpu.CompilerParams(dimension_semantics=("arbitrary",) * 2,
                                             vmem_limit_bytes=VMEM_LIMIT),
        name="sb_attention",
    )(qkv3, qkv3, qkv3)


def _mix_kernel(sb_ref, sgn_ref, x_ref, sbg_ref, wout_ref, ffng_ref, wr2_ref, br_ref,
                h_ref, lg_ref):
    sbn = _rms(sb_ref[...], sbg_ref[...]).astype(BF16)
    h = x_ref[...] + _dot(sbn, wout_ref[0:SB_WIDTH, :]) + _dot(sgn_ref[...], wout_ref[SB_WIDTH:, :])
    h_ref[...] = h
    hn = _rms(h, ffng_ref[...])

    hn_hi, hn_lo = _split_bf16(hn)
    both = _dot(hn_hi, wr2_ref[...])
    logits = both[:, 0:LANES] + both[:, LANES:] + _dot(hn_lo, wr2_ref[:, 0:LANES]) + br_ref[...]
    lg_ref[...] = logits.T[0:ROUTER_ROWS, :]


def _route_kernel(lg_ref, ri_ref, rw_ref, cnt_ref, count_ref):
    tr = TM_ROUTE
    i = pl.program_id(0)

    @pl.when(i == 0)
    def _():
        count_ref[...] = jnp.zeros_like(count_ref)

    neg = jnp.float32(-jnp.inf)
    row8 = lax.broadcasted_iota(jnp.int32, (SUBLANES, tr), 0)

    def top(v):
        m = jnp.max(v, axis=0, keepdims=True)
        return m, jnp.min(jnp.where(v == m, row8, SUBLANES), axis=0, keepdims=True)

    def group_rows(g):
        return lg_ref[ROUTER_LANE0 + g * EXPERTS_PER_GROUP:ROUTER_LANE0 + (g + 1) * EXPERTS_PER_GROUP, :]

    gl = jnp.where(row8 < N_GROUPS, lg_ref[0:SUBLANES, :], neg)
    gmax, gidx = top(gl)
    gweight = 1.0 / jnp.sum(jnp.exp(gl - gmax), axis=0, keepdims=True)
    el = group_rows(0)
    for g in range(1, N_GROUPS):
        el = jnp.where(gidx == g, group_rows(g), el)
    m1, i1 = top(el)
    m2, i2 = top(jnp.where(row8 == i1, neg, el))
    t21 = jnp.exp(m2 - m1)
    w1 = gweight / (1.0 + t21)
    w2 = gweight * t21 / (1.0 + t21)
    e1 = gidx * EXPERTS_PER_GROUP + i1
    e2 = gidx * EXPERTS_PER_GROUP + i2

    row_e = lax.broadcasted_iota(jnp.int32, (N_EXPERTS, tr), 0)
    sel1 = row_e == e1
    sel2 = row_e == e2
    onehot = jnp.where(sel1 | sel2, 1.0, 0.0)
    r_t = lax.broadcasted_iota(jnp.int32, (tr, tr), 0)
    c_t = lax.broadcasted_iota(jnp.int32, (tr, tr), 1)
    before = (r_t < c_t).astype(BF16)
    running = count_ref[:, 0:1] + _dot(onehot.astype(BF16), before)
    rank1 = jnp.sum(jnp.where(sel1, running, 0.0), axis=0, keepdims=True)
    rank2 = jnp.sum(jnp.where(sel2, running, 0.0), axis=0, keepdims=True)
    new_count = count_ref[:, 0:1] + jnp.sum(onehot, axis=1, keepdims=True)
    count_ref[...] = jnp.broadcast_to(new_count, count_ref.shape)
    cnt_ref[...] = jnp.broadcast_to(new_count, cnt_ref.shape)

    ri_ref[...] = jnp.where(row8 == 0, e1, jnp.where(row8 == 1, e2, jnp.where(
        row8 == 2, rank1.astype(jnp.int32), jnp.where(row8 == 3, rank2.astype(jnp.int32), 0))))
    row128 = lax.broadcasted_iota(jnp.int32, (LANES, tr), 0)
    rw_ref[...] = jnp.where(row128 == 0, w1, jnp.where(row128 == 1, w2, 0.0)).T


def _route(lg):
    n = lg.shape[1]
    return pl.pallas_call(
        _route_kernel,
        grid=(n // TM_ROUTE,),
        in_specs=[pl.BlockSpec((ROUTER_ROWS, TM_ROUTE), lambda i: (0, i))],
        out_specs=[pl.BlockSpec((SUBLANES, TM_ROUTE), lambda i: (0, i)),
                   pl.BlockSpec((TM_ROUTE, LANES), lambda i: (i, 0)),
                   pl.BlockSpec((N_EXPERTS, LANES), lambda i: (0, 0))],
        out_shape=[jax.ShapeDtypeStruct((SUBLANES, n), jnp.int32),
                   jax.ShapeDtypeStruct((n, LANES), F32),
                   jax.ShapeDtypeStruct((N_EXPERTS, LANES), F32)],
        scratch_shapes=[pltpu.VMEM((N_EXPERTS, LANES), F32)],
        compiler_params=pltpu.CompilerParams(dimension_semantics=("arbitrary",),
                                             vmem_limit_bytes=VMEM_LIMIT),
        name="route",
    )(lg)


def _mix(sb, sgn, x2, sb_g, w_out_b, ffn_g, wr2, br):
    n = x2.shape[0]
    row = lambda i: (i, 0)
    const = lambda i: (0, 0)
    return pl.pallas_call(
        _mix_kernel,
        grid=(n // TM_MIX,),
        in_specs=[pl.BlockSpec((TM_MIX, SB_WIDTH), row),
                  pl.BlockSpec((TM_MIX, SG_WIDTH), row),
                  pl.BlockSpec((TM_MIX, D_MODEL), row),
                  pl.BlockSpec((1, SB_WIDTH), const),
                  pl.BlockSpec((D_MODEL, D_MODEL), const),
                  pl.BlockSpec((1, D_MODEL), const),
                  pl.BlockSpec((D_MODEL, 2 * LANES), const),
                  pl.BlockSpec((1, LANES), const)],
        out_specs=[pl.BlockSpec((TM_MIX, D_MODEL), row),
                   pl.BlockSpec((ROUTER_ROWS, TM_MIX), lambda i: (0, i))],
        out_shape=[jax.ShapeDtypeStruct((n, D_MODEL), F32),
                   jax.ShapeDtypeStruct((ROUTER_ROWS, n), F32)],
        compiler_params=pltpu.CompilerParams(dimension_semantics=("arbitrary",),
                                             vmem_limit_bytes=VMEM_LIMIT),
        name="mix_router",
    )(sb, sgn, x2, sb_g, w_out_b, ffn_g, wr2, br)


_PAD_BITS = tuple(1 << b for b in reversed(range(TM_EXPERT.bit_length() - 1)))


def _dispatch_kernel(dest_ref, pad_start_ref, pad_count_ref, nt_ref, h_ref, g_ref, zeros_ref, xs_ref,
                     hn_ref, sem, zsem):
    tm = TM_DISPATCH
    i = pl.program_id(0)
    n_steps = pl.num_programs(0) - 1
    n = n_steps * tm
    base = (i - 1) * tm
    prev = hn_ref.at[lax.rem(i + 1, 2)]
    n_tiles_max = xs_ref.shape[0] // (TM_EXPERT * ROW_TILE)

    def pad_copies(do):
        for e in range(N_EXPERTS):
            start = pad_start_ref[e]
            count = pad_count_ref[e]
            for bit in _PAD_BITS:
                @pl.when((count & bit) != 0)
                def _(start=start, bit=bit):
                    do(pltpu.make_async_copy(_token_rows(zeros_ref, 0, bit),
                                             _token_rows(xs_ref, start, bit), zsem))
                start = start + (count & bit)
        for k in range(N_EXPERTS):
            tile = nt_ref[0] + k

            @pl.when(tile < n_tiles_max)
            def _(tile=tile):
                do(pltpu.make_async_copy(zeros_ref, _token_rows(xs_ref, tile * TM_EXPERT, TM_EXPERT), zsem))

    @pl.when(i == 0)
    def _():
        pad_copies(lambda cp: cp.start())

    @pl.when(i > 0)
    def _():
        def body(r, c):
            src = _token_rows(prev, r, 1)
            for s in range(2):
                pltpu.make_async_copy(src, _token_rows(xs_ref, dest_ref[s * n + base + r], 1),
                                      sem).start(priority=s)
            return c

        lax.fori_loop(0, tm, body, 0, unroll=8)

    @pl.when(i < n_steps)
    def _():
        _rows_to_tiles(hn_ref.at[lax.rem(i, 2)], _rms(h_ref[...], g_ref[...]))

    @pl.when(i > 0)
    def _():
        for _ in range(2):
            pltpu.make_async_copy(prev, _token_rows(xs_ref, 0, tm), sem).wait()

    @pl.when(i == n_steps)
    def _():
        pad_copies(lambda cp: cp.wait())


def _dispatch(dest, pad_start, pad_count, n_tiles, h, ffn_g, n_rows):
    n_steps = h.shape[0] // TM_DISPATCH
    zeros = jnp.zeros((TM_EXPERT * ROW_TILE, LANES), F32)
    return pl.pallas_call(
        _dispatch_kernel,
        grid_spec=pltpu.PrefetchScalarGridSpec(
            num_scalar_prefetch=4,
            grid=(n_steps + 1,),
            in_specs=[pl.BlockSpec((TM_DISPATCH, D_MODEL), lambda i, *_: (jnp.minimum(i, n_steps - 1), 0)),
                      pl.BlockSpec((1, D_MODEL), lambda i, *_: (0, 0)),
                      pl.BlockSpec(memory_space=pl.ANY)],
            out_specs=pl.BlockSpec(memory_space=pl.ANY),
            scratch_shapes=[pltpu.VMEM((2, TM_DISPATCH * ROW_TILE, LANES), F32),
                            pltpu.SemaphoreType.DMA, pltpu.SemaphoreType.DMA]),
        out_shape=jax.ShapeDtypeStruct((n_rows * ROW_TILE, LANES), F32),
        compiler_params=pltpu.CompilerParams(dimension_semantics=("arbitrary",),
                                             vmem_limit_bytes=VMEM_LIMIT),
        name="dispatch",
    )(dest, pad_start, pad_count, n_tiles, h, ffn_g, zeros)


X_SLOTS = 3


def _expert_kernel(tiles_ref, nt_ref, xs_ref, wg_ref, wu_ref, wd_ref, y_ref,
                   x_buf, sg_buf, su_buf, sd_buf, wgb, wub, wdb, state, w_sems, x_sems):
    tm = TM_EXPERT
    t = pl.program_id(0)
    nt = nt_ref[0]

    def x_copy(tile):
        slot = lax.rem(tile, X_SLOTS)
        return pltpu.make_async_copy(_token_rows(xs_ref, tile * tm, tm), x_buf.at[slot], x_sems.at[slot])

    def weight_copies(e, slot):
        return (pltpu.make_async_copy(wg_ref.at[e], sg_buf.at[slot], w_sems.at[slot]),
                pltpu.make_async_copy(wu_ref.at[e], su_buf.at[slot], w_sems.at[slot]),
                pltpu.make_async_copy(wd_ref.at[e], sd_buf.at[slot], w_sems.at[slot]))

    def next_with_rows(e):
        return lax.while_loop(lambda k: (k < N_EXPERTS) & (tiles_ref[jnp.minimum(k, N_EXPERTS - 1)] == 0),
                              lambda k: k + 1, e + 1)

    @pl.when(t == 0)
    def _():
        first = next_with_rows(jnp.int32(-1))
        state[0] = jnp.int32(-1)
        state[1] = jnp.int32(0)
        state[2] = jnp.int32(1)
        state[3] = first
        for cp in weight_copies(first, 0):
            cp.start()
        x_copy(0).start()

        @pl.when(nt > 1)
        def _():
            x_copy(1).start()

    @pl.when(t + 2 < nt)
    def _():
        x_copy(t + 2).start()

    @pl.when(t < nt)
    def _():
        @pl.when(state[1] == 0)
        def _():
            e = state[3]
            slot = 1 - state[2]
            nxt = next_with_rows(e)
            state[0] = e
            state[1] = tiles_ref[e]
            state[2] = slot
            state[3] = nxt
            for cp in weight_copies(e, slot):
                cp.wait()

            @pl.when(nxt < N_EXPERTS)
            def _():
                for cp in weight_copies(nxt, 1 - slot):
                    cp.start()

            wgb[...] = sg_buf[slot].astype(BF16)
            wub[...] = su_buf[slot].astype(BF16)
            wdb[...] = sd_buf[slot].astype(BF16)

        state[1] = state[1] - 1
        x_copy(t).wait()
        x = _tiles_to_rows(x_buf.at[lax.rem(t, X_SLOTS)], tm).astype(BF16)
        g = _dot(x, wgb[...])
        u = _dot(x, wub[...])
        hidden = (g * jax.nn.sigmoid(g)) * u
        _rows_to_tiles(y_ref, _dot(hidden.astype(BF16), wdb[...]))

    @pl.when(t >= nt)
    def _():
        y_ref[...] = jnp.zeros_like(y_ref)


def _experts(tiles, n_tiles, xs, wg, wu, wd):
    n_rows = xs.shape[0] // ROW_TILE
    any_spec = pl.BlockSpec(memory_space=pl.ANY)
    return pl.pallas_call(
        _expert_kernel,
        grid_spec=pltpu.PrefetchScalarGridSpec(
            num_scalar_prefetch=2,
            grid=(n_rows // TM_EXPERT,),
            in_specs=[any_spec, any_spec, any_spec, any_spec],
            out_specs=pl.BlockSpec((TM_EXPERT * ROW_TILE, LANES), lambda t, *_: (t, 0)),
            scratch_shapes=[pltpu.VMEM((X_SLOTS, TM_EXPERT * ROW_TILE, LANES), F32),
                            pltpu.VMEM((2, D_MODEL, D_EXPERT), F32),
                            pltpu.VMEM((2, D_MODEL, D_EXPERT), F32),
                            pltpu.VMEM((2, D_EXPERT, D_MODEL), F32),
                            pltpu.VMEM((D_MODEL, D_EXPERT), BF16),
                            pltpu.VMEM((D_MODEL, D_EXPERT), BF16),
                            pltpu.VMEM((D_EXPERT, D_MODEL), BF16),
                            pltpu.SMEM((4,), jnp.int32),
                            pltpu.SemaphoreType.DMA((2,)),
                            pltpu.SemaphoreType.DMA((X_SLOTS,))]),
        out_shape=jax.ShapeDtypeStruct((n_rows * ROW_TILE, LANES), F32),
        compiler_params=pltpu.CompilerParams(dimension_semantics=("arbitrary",),
                                             vmem_limit_bytes=VMEM_LIMIT),
        name="expert_mlp",
    )(tiles, n_tiles, xs, wg, wu, wd)


def _combine_kernel(dest_ref, h_ref, rw_ref, fg_ref, y_ref, o_ref, buf, sems):
    tm = TM_COMBINE
    i = pl.program_id(0)
    n_steps = pl.num_programs(0)
    n = n_steps * tm
    cur = i % 2

    def fetch(step, half):
        def body(r, c):
            for s in range(2):
                pltpu.make_async_copy(_token_rows(y_ref, dest_ref[s * n + step * tm + r], 1),
                                      _token_rows(buf.at[half, s], r, 1),
                                      sems.at[half]).start(priority=s)
            return c

        lax.fori_loop(0, tm, body, 0, unroll=8)

    @pl.when(i == 0)
    def _():
        fetch(0, 0)

    @pl.when(i + 1 < n_steps)
    def _():
        fetch(i + 1, 1 - cur)

    for s in range(2):
        pltpu.make_async_copy(_token_rows(y_ref, 0, tm), buf.at[cur, s], sems.at[cur]).wait()
    rw = rw_ref[...]
    out = (h_ref[...] + rw[:, 0:1] * _tiles_to_rows(buf.at[cur, 0], tm)
           + rw[:, 1:2] * _tiles_to_rows(buf.at[cur, 1], tm))
    o_ref[...] = _rms(out, fg_ref[...])


def _combine(dest, h, rw, final_g, ys):
    n = h.shape[0]
    return pl.pallas_call(
        _combine_kernel,
        grid_spec=pltpu.PrefetchScalarGridSpec(
            num_scalar_prefetch=1,
            grid=(n // TM_COMBINE,),
            in_specs=[pl.BlockSpec((TM_COMBINE, D_MODEL), lambda i, d: (i, 0)),
                      pl.BlockSpec((TM_COMBINE, LANES), lambda i, d: (i, 0)),
                      pl.BlockSpec((1, D_MODEL), lambda i, d: (0, 0)),
                      pl.BlockSpec(memory_space=pl.ANY)],
            out_specs=pl.BlockSpec((TM_COMBINE, D_MODEL), lambda i, d: (i, 0)),
            scratch_shapes=[pltpu.VMEM((2, 2, TM_COMBINE * ROW_TILE, LANES), F32),
                            pltpu.SemaphoreType.DMA((2,))]),
        out_shape=jax.ShapeDtypeStruct((n, D_MODEL), F32),
        compiler_params=pltpu.CompilerParams(dimension_semantics=("arbitrary",),
                                             vmem_limit_bytes=VMEM_LIMIT),
        name="combine",
    )(dest, h, rw, final_g, ys)


def _schedule(counts):
    tiles = (counts + TM_EXPERT - 1) // TM_EXPERT
    tile_end = jnp.cumsum(tiles)
    offsets = (tile_end - tiles) * TM_EXPERT
    return tiles, offsets, tile_end[-1:]


def _layer(x, attn_g, w_in, sg_g, w_sp, b_sp, sb_g, sg_out_g, w_out, ffn_g,
           w_rg, b_rg, w_re, b_re, w_gate, w_up, w_down):
    batch, seq, _ = x.shape
    n = batch * seq
    x2 = x.reshape(n, D_MODEL)
    row = lambda v: v.reshape(1, -1)

    bsp_full = jnp.repeat(b_sp.T, HEAD_DIM, axis=1)
    qkv, sgn = _inproj(x2, row(attn_g), w_in.astype(BF16), row(sg_g), w_sp, bsp_full, row(sg_out_g))
    sb = _attention(qkv, batch, seq).reshape(n, SB_WIDTH)

    pad_lanes = lambda v, width: jnp.pad(v, [(0, 0)] * (v.ndim - 1) + [(0, width - v.shape[-1])])
    w_r = jnp.concatenate([pad_lanes(w_rg, ROUTER_LANE0),
                           jnp.transpose(w_re, (1, 0, 2)).reshape(D_MODEL, N_EXPERTS)], axis=1)
    w_r = pad_lanes(w_r, LANES)
    wr_hi = w_r.astype(BF16)
    wr_lo = (w_r - wr_hi.astype(F32)).astype(BF16)
    wr2 = jnp.concatenate([wr_hi, wr_lo], axis=1)
    b_r = pad_lanes(jnp.concatenate([pad_lanes(b_rg, ROUTER_LANE0), b_re.reshape(-1)]), LANES)

    h, lg = _mix(sb, sgn, x2, row(sb_g), w_out.astype(BF16), row(ffn_g), wr2, row(b_r))
    ri, rw, cnt = _route(lg)

    counts = cnt[:, 0].astype(jnp.int32)
    n_rows = 2 * n + N_EXPERTS * TM_EXPERT
    tiles, offsets, n_tiles = _schedule(counts)
    expert, rank = ri[0:2], ri[2:4]
    is_e = expert[None] == jnp.arange(N_EXPERTS, dtype=jnp.int32)[:, None, None]
    dest = (jnp.sum(jnp.where(is_e, offsets[:, None, None], 0), axis=0) + rank).reshape(-1)
    pad_start = offsets + counts
    pad_count = (-counts) % TM_EXPERT

    xs = _dispatch(dest, pad_start, pad_count, n_tiles, h, row(ffn_g), n_rows)
    ys = _experts(tiles, n_tiles, xs,
                  w_gate.reshape(N_EXPERTS, D_MODEL, D_EXPERT),
                  w_up.reshape(N_EXPERTS, D_MODEL, D_EXPERT),
                  w_down.reshape(N_EXPERTS, D_EXPERT, D_MODEL))
    return dest, h, rw, ys


def kernel(x, attn_norm_g, w_in, sg_norm_g, w_spatial, b_spatial, sb_out_norm_g, sg_out_norm_g,
           w_out, ffn_norm_g, w_router_group, b_router_group, w_router_expert, b_router_expert,
           w_gate, w_up, w_down, final_norm_g):
    assert attn_norm_g.shape[0] == 1, "single-layer problem"
    batch, seq, _ = x.shape
    dest, h, rw, ys = _layer(x, attn_norm_g[0], w_in[0], sg_norm_g[0], w_spatial[0], b_spatial[0],
                             sb_out_norm_g[0], sg_out_norm_g[0], w_out[0], ffn_norm_g[0],
                             w_router_group[0], b_router_group[0], w_router_expert[0],
                             b_router_expert[0], w_gate[0], w_up[0], w_down[0])
    out = _combine(dest, h, rw, final_norm_g.reshape(1, -1), ys)
    return out.reshape(batch, seq, D_MODEL)
```

```python
import functools
import math

import jax
import jax.numpy as jnp
from jax import lax
from jax.experimental import pallas as pl
from jax.experimental.pallas import tpu as pltpu

D_MODEL = 1024
HEAD_DIM = 64
SB_WIDTH = 512
SG_WIDTH = 512
SG_HEADS = 8
D_IN = 3 * SB_WIDTH + 2 * SG_WIDTH
CHUNK = 128
N_GROUPS = 4
EXPERTS_PER_GROUP = 8
N_EXPERTS = N_GROUPS * EXPERTS_PER_GROUP
D_EXPERT = 512
EPS = 1e-6
F32_EXP_UNDERFLOW = 110.0

LANES = 128
SUBLANES = 8
ROW_TILE = D_MODEL // LANES
assert ROW_TILE == SUBLANES
HEAD_PAIR = 2 * HEAD_DIM
ROUTER_LANE0 = SUBLANES
ROUTER_ROWS = ROUTER_LANE0 + N_EXPERTS
assert EXPERTS_PER_GROUP == SUBLANES and N_GROUPS <= ROUTER_LANE0

TM_PROJ = 1024
TQ_ATTN = 256
ATTN_TOP_ROWS = 160
TM_MIX = 1024
TM_ROUTE = 1024
TM_DISPATCH = 512
TM_EXPERT = 512
TM_COMBINE = 256
VMEM_LIMIT = 48 * 1024 * 1024

F32 = jnp.float32
BF16 = jnp.bfloat16


def _rms(x, g):
    return x * lax.rsqrt(jnp.mean(x * x, axis=-1, keepdims=True) + EPS) * g


def _gelu(x):
    c = math.sqrt(2.0 / math.pi)
    return x * (0.5 * (1.0 + jnp.tanh(c * (x + 0.044715 * (x * x * x)))))


def _softplus(z):
    return jnp.maximum(z, 0.0) + jnp.log(1.0 + jnp.exp(-jnp.abs(z)))


def _dot(a, b):
    return jnp.dot(a, b, preferred_element_type=F32)


def _rows_to_tiles(ref, x):
    m = x.shape[0]
    for k in range(ROW_TILE):
        ref[pl.ds(k, m, stride=ROW_TILE), :] = x[:, k * LANES:(k + 1) * LANES]


def _tiles_to_rows(ref, m):
    return jnp.concatenate([ref[pl.ds(k, m, stride=ROW_TILE), :] for k in range(ROW_TILE)], axis=1)


def _token_rows(ref, first_token, n_tokens):
    return ref.at[pl.ds(pl.multiple_of(first_token * ROW_TILE, ROW_TILE), n_tokens * ROW_TILE)]


def _split_bf16(x):
    hi = x.astype(BF16)
    lo = (x - hi.astype(F32)).astype(BF16)
    return hi, lo


def _inproj_kernel(x_ref, g_ref, w_ref, sgg_ref, wsp_ref, bsp_ref, sgog_ref, qkv_ref, sgn_ref,
                   gu_ref, vgn_ref, sg_ref):
    tm = TM_PROJ
    hb = _rms(x_ref[...], g_ref[...]).astype(BF16)
    gv = _gelu(_dot(hb, w_ref[:, 3 * SB_WIDTH + SG_WIDTH:D_IN]))
    vgn_ref[...] = _rms(gv, sgg_ref[...]).astype(BF16)
    gu_ref[...] = _gelu(_dot(hb, w_ref[:, 3 * SB_WIDTH:3 * SB_WIDTH + SG_WIDTH]))
    q = _dot(hb, w_ref[:, 0:SB_WIDTH]) * (1.0 / math.sqrt(HEAD_DIM))
    qkv_ref[:, 0:SB_WIDTH] = q.astype(BF16)
    qkv_ref[:, SB_WIDTH:2 * SB_WIDTH] = _dot(hb, w_ref[:, SB_WIDTH:2 * SB_WIDTH]).astype(BF16)

    lane = lax.broadcasted_iota(jnp.int32, (1, LANES), 1)
    first = lane < HEAD_DIM
    zero = jnp.zeros((), BF16)
    r_c = lax.broadcasted_iota(jnp.int32, (CHUNK, CHUNK), 0)
    c_c = lax.broadcasted_iota(jnp.int32, (CHUNK, CHUNK), 1)
    tril = r_c >= c_c
    n_pairs = SG_WIDTH // HEAD_PAIR
    w_pairs = []
    for p in range(n_pairs):
        w0 = jnp.where(tril, wsp_ref[2 * p], 0.0).astype(BF16)
        w1 = jnp.where(tril, wsp_ref[2 * p + 1], 0.0).astype(BF16)
        w_pairs.append(jnp.concatenate([w0, w1], axis=1))
    bsp = bsp_ref[...]
    for c in range(tm // CHUNK):
        rows = slice(c * CHUNK, (c + 1) * CHUNK)
        for p in range(n_pairs):
            cols = slice(p * HEAD_PAIR, (p + 1) * HEAD_PAIR)
            vg = vgn_ref[rows, cols]
            rhs = jnp.concatenate([jnp.where(first, vg, zero), jnp.where(first, zero, vg)], axis=0)
            mixed = _dot(w_pairs[p], rhs) + bsp[:, cols]
            sg_ref[rows, cols] = gu_ref[rows, cols] * mixed
    qkv_ref[:, 2 * SB_WIDTH:3 * SB_WIDTH] = _dot(hb, w_ref[:, 2 * SB_WIDTH:3 * SB_WIDTH]).astype(BF16)
    sgn_ref[...] = _rms(sg_ref[...], sgog_ref[...]).astype(BF16)


def _inproj(x2, attn_g, w_in_b, sg_g, wsp, bsp_full, sg_out_g):
    n = x2.shape[0]
    row = lambda i: (i, 0)
    const = lambda i: (0, 0)
    return pl.pallas_call(
        _inproj_kernel,
        grid=(n // TM_PROJ,),
        in_specs=[pl.BlockSpec((TM_PROJ, D_MODEL), row),
                  pl.BlockSpec((1, D_MODEL), const),
                  pl.BlockSpec((D_MODEL, D_IN), const),
                  pl.BlockSpec((1, SG_WIDTH), const),
                  pl.BlockSpec((SG_HEADS, CHUNK, CHUNK), lambda i: (0, 0, 0)),
                  pl.BlockSpec((CHUNK, SG_WIDTH), const),
                  pl.BlockSpec((1, SG_WIDTH), const)],
        out_specs=[pl.BlockSpec((TM_PROJ, 3 * SB_WIDTH), row),
                   pl.BlockSpec((TM_PROJ, SG_WIDTH), row)],
        out_shape=[jax.ShapeDtypeStruct((n, 3 * SB_WIDTH), BF16),
                   jax.ShapeDtypeStruct((n, SG_WIDTH), BF16)],
        scratch_shapes=[pltpu.VMEM((TM_PROJ, SG_WIDTH), F32),
                        pltpu.VMEM((TM_PROJ, SG_WIDTH), BF16),
                        pltpu.VMEM((TM_PROJ, SG_WIDTH), F32)],
        compiler_params=pltpu.CompilerParams(dimension_semantics=("arbitrary",),
                                             vmem_limit_bytes=VMEM_LIMIT),
        name="inproj",
    )(x2, attn_g, w_in_b, sg_g, wsp, bsp_full, sg_out_g)


def _attn_kernel(q_ref, k_ref, v_ref, o_ref, q2_ref, carry_ref):
    t = TQ_ATTN
    n_pairs = SB_WIDTH // HEAD_PAIR
    qi = pl.program_id(1)
    lane = lax.broadcasted_iota(jnp.int32, (1, HEAD_PAIR), 1)
    head_lanes = (lane < HEAD_DIM, lane >= HEAD_DIM)
    zero = jnp.zeros((), BF16)
    for p in range(n_pairs):
        qp = q_ref[0, :, p * HEAD_PAIR:(p + 1) * HEAD_PAIR]
        for h in range(2):
            q2_ref[(2 * p + h) * t:(2 * p + h + 1) * t, :] = jnp.where(head_lanes[h], qp, zero)
    r_idx = lax.broadcasted_iota(jnp.int32, (t, t), 0)
    c_idx = lax.broadcasted_iota(jnp.int32, (t, t), 1)
    suffix = (r_idx > c_idx).astype(BF16)
    suffix2 = jnp.concatenate([suffix, suffix], axis=0)
    causal = c_idx < r_idx

    o_ref[...] = jnp.zeros_like(o_ref)
    carry_ref[...] = jnp.zeros_like(carry_ref)

    def block(j, diag, m):
        start = pl.multiple_of(j * t, t)
        mask2 = jnp.concatenate([causal, causal], axis=0) if diag else None
        st = [dict() for _ in range(n_pairs)]

        def head_rows(p):
            return [slice((2 * p + h) * t, (2 * p + h) * t + m) for h in range(2)]

        def scores(p):
            d = st[p]
            d["cols"] = slice(p * HEAD_PAIR, (p + 1) * HEAD_PAIR)
            kb = k_ref[0, pl.ds(start, t), d["cols"]]
            q2 = jnp.concatenate([q2_ref[r, :] for r in head_rows(p)], axis=0)
            z = lax.dot_general(q2, kb, (((1,), (1,)), ((), ())),
                                preferred_element_type=F32)
            sp = _softplus(z)
            nl = jnp.where(mask2, sp, 0.0) if diag else sp
            hi, lo = _split_bf16(nl)
            d["hl"] = jnp.concatenate([hi, lo], axis=1)
            d["log_beta"] = z - sp
            d["nl0"] = nl[:, 0:1]

        def weights(p):
            d = st[p]
            hl = d["hl"]
            after = jnp.concatenate([_dot(hl[0:m], suffix2), _dot(hl[m:2 * m], suffix2)], axis=0)
            carry = jnp.concatenate([carry_ref[r, :] for r in head_rows(p)], axis=0)
            a = jnp.exp(d["log_beta"] - after - carry)
            if diag:
                a = jnp.where(mask2, a, 0.0)
            a = a.astype(BF16)
            d["a2"] = jnp.concatenate([a[0:m], a[m:2 * m]], axis=1)
            new_carry = carry + after[:, 0:1] + d["nl0"]
            for h, r in enumerate(head_rows(p)):
                carry_ref[r, :] = new_carry[h * m:(h + 1) * m]

        def values(p):
            d = st[p]
            vb = v_ref[0, pl.ds(start, t), d["cols"]]
            v2 = jnp.concatenate([jnp.where(head_lanes[0], vb, zero),
                                  jnp.where(head_lanes[1], vb, zero)], axis=0)
            o_ref[0, 0:m, d["cols"]] += _dot(d["a2"], v2)

        for step in range(n_pairs + 2):
            if step < n_pairs:
                scores(step)
            if 0 <= step - 1 < n_pairs:
                weights(step - 1)
            if 0 <= step - 2 < n_pairs:
                values(step - 2)

    top = ATTN_TOP_ROWS

    def flags():
        bottom = jnp.concatenate([carry_ref[hh * t + top:(hh + 1) * t, :] for hh in range(2 * n_pairs)], axis=0)
        return (jnp.min(carry_ref[...]) < F32_EXP_UNDERFLOW, jnp.min(bottom) >= F32_EXP_UNDERFLOW)

    block(qi, True, t)

    def body(state):
        it, _, bottom_done = state
        j = qi - 1 - it

        @pl.when(bottom_done)
        def _():
            block(j, False, top)

        @pl.when(jnp.logical_not(bottom_done))
        def _():
            block(j, False, t)

        return (it + 1,) + flags()

    lax.while_loop(lambda s: (s[0] < qi) & s[1], body, (jnp.int32(0),) + flags())


def _attention(qkv, batch, seq):
    qkv3 = qkv.reshape(batch, seq, 3 * SB_WIDTH)
    n_heads = SB_WIDTH // HEAD_DIM
    return pl.pallas_call(
        _attn_kernel,
        grid=(batch, seq // TQ_ATTN),
        in_specs=[pl.BlockSpec((1, TQ_ATTN, SB_WIDTH), lambda b, i: (b, i, 0)),
                  pl.BlockSpec((1, seq, SB_WIDTH), lambda b, i: (b, 0, 1)),
                  pl.BlockSpec((1, seq, SB_WIDTH), lambda b, i: (b, 0, 2))],
        out_specs=pl.BlockSpec((1, TQ_ATTN, SB_WIDTH), lambda b, i: (b, i, 0)),
        out_shape=jax.ShapeDtypeStruct((batch, seq, SB_WIDTH), F32),
        scratch_shapes=[pltpu.VMEM((n_heads * TQ_ATTN, HEAD_PAIR), BF16),
                        pltpu.VMEM((n_heads * TQ_ATTN, 1), F32)],
        compiler_params=pltpu.CompilerParams(dimension_semantics=("arbitrary",) * 2,
                                             vmem_limit_bytes=VMEM_LIMIT),
        name="sb_attention",
    )(qkv3, qkv3, qkv3)


def _mix_kernel(sb_ref, sgn_ref, x_ref, sbg_ref, wout_ref, ffng_ref, wr2_ref, br_ref,
                h_ref, lg_ref):
    sbn = _rms(sb_ref[...], sbg_ref[...]).astype(BF16)
    h = x_ref[...] + _dot(sbn, wout_ref[0:SB_WIDTH, :]) + _dot(sgn_ref[...], wout_ref[SB_WIDTH:, :])
    h_ref[...] = h
    hn = _rms(h, ffng_ref[...])

    hn_hi, hn_lo = _split_bf16(hn)
    both = _dot(hn_hi, wr2_ref[...])
    logits = both[:, 0:LANES] + both[:, LANES:] + _dot(hn_lo, wr2_ref[:, 0:LANES]) + br_ref[...]
    lg_ref[...] = logits.T[0:ROUTER_ROWS, :]


def _route_kernel(lg_ref, ri_ref, rw_ref, cnt_ref, count_ref):
    tr = TM_ROUTE
    i = pl.program_id(0)

    @pl.when(i == 0)
    def _():
        count_ref[...] = jnp.zeros_like(count_ref)

    neg = jnp.float32(-jnp.inf)
    row8 = lax.broadcasted_iota(jnp.int32, (SUBLANES, tr), 0)

    def top(v):
        m = jnp.max(v, axis=0, keepdims=True)
        return m, jnp.min(jnp.where(v == m, row8, SUBLANES), axis=0, keepdims=True)

    def group_rows(g):
        return lg_ref[ROUTER_LANE0 + g * EXPERTS_PER_GROUP:ROUTER_LANE0 + (g + 1) * EXPERTS_PER_GROUP, :]

    gl = jnp.where(row8 < N_GROUPS, lg_ref[0:SUBLANES, :], neg)
    gmax, gidx = top(gl)
    gweight = 1.0 / jnp.sum(jnp.exp(gl - gmax), axis=0, keepdims=True)
    el = group_rows(0)
    for g in range(1, N_GROUPS):
        el = jnp.where(gidx == g, group_rows(g), el)
    m1, i1 = top(el)
    m2, i2 = top(jnp.where(row8 == i1, neg, el))
    t21 = jnp.exp(m2 - m1)
    w1 = gweight / (1.0 + t21)
    w2 = gweight * t21 / (1.0 + t21)
    e1 = gidx * EXPERTS_PER_GROUP + i1
    e2 = gidx * EXPERTS_PER_GROUP + i2

    row_e = lax.broadcasted_iota(jnp.int32, (N_EXPERTS, tr), 0)
    sel1 = row_e == e1
    sel2 = row_e == e2
    onehot = jnp.where(sel1 | sel2, 1.0, 0.0)
    r_t = lax.broadcasted_iota(jnp.int32, (tr, tr), 0)
    c_t = lax.broadcasted_iota(jnp.int32, (tr, tr), 1)
    before = (r_t < c_t).astype(BF16)
    running = count_ref[:, 0:1] + _dot(onehot.astype(BF16), before)
    rank1 = jnp.sum(jnp.where(sel1, running, 0.0), axis=0, keepdims=True)
    rank2 = jnp.sum(jnp.where(sel2, running, 0.0), axis=0, keepdims=True)
    new_count = count_ref[:, 0:1] + jnp.sum(onehot, axis=1, keepdims=True)
    count_ref[...] = jnp.broadcast_to(new_count, count_ref.shape)
    cnt_ref[...] = jnp.broadcast_to(new_count, cnt_ref.shape)

    ri_ref[...] = jnp.where(row8 == 0, e1, jnp.where(row8 == 1, e2, jnp.where(
        row8 == 2, rank1.astype(jnp.int32), jnp.where(row8 == 3, rank2.astype(jnp.int32), 0))))
    row128 = lax.broadcasted_iota(jnp.int32, (LANES, tr), 0)
    rw_ref[...] = jnp.where(row128 == 0, w1, jnp.where(row128 == 1, w2, 0.0)).T


def _route(lg):
    n = lg.shape[1]
    return pl.pallas_call(
        _route_kernel,
        grid=(n // TM_ROUTE,),
        in_specs=[pl.BlockSpec((ROUTER_ROWS, TM_ROUTE), lambda i: (0, i))],
        out_specs=[pl.BlockSpec((SUBLANES, TM_ROUTE), lambda i: (0, i)),
                   pl.BlockSpec((TM_ROUTE, LANES), lambda i: (i, 0)),
                   pl.BlockSpec((N_EXPERTS, LANES), lambda i: (0, 0))],
        out_shape=[jax.ShapeDtypeStruct((SUBLANES, n), jnp.int32),
                   jax.ShapeDtypeStruct((n, LANES), F32),
                   jax.ShapeDtypeStruct((N_EXPERTS, LANES), F32)],
        scratch_shapes=[pltpu.VMEM((N_EXPERTS, LANES), F32)],
        compiler_params=pltpu.CompilerParams(dimension_semantics=("arbitrary",),
                                             vmem_limit_bytes=VMEM_LIMIT),
        name="route",
    )(lg)


def _mix(sb, sgn, x2, sb_g, w_out_b, ffn_g, wr2, br):
    n = x2.shape[0]
    row = lambda i: (i, 0)
    const = lambda i: (0, 0)
    return pl.pallas_call(
        _mix_kernel,
        grid=(n // TM_MIX,),
        in_specs=[pl.BlockSpec((TM_MIX, SB_WIDTH), row),
                  pl.BlockSpec((TM_MIX, SG_WIDTH), row),
                  pl.BlockSpec((TM_MIX, D_MODEL), row),
                  pl.BlockSpec((1, SB_WIDTH), const),
                  pl.BlockSpec((D_MODEL, D_MODEL), const),
                  pl.BlockSpec((1, D_MODEL), const),
                  pl.BlockSpec((D_MODEL, 2 * LANES), const),
                  pl.BlockSpec((1, LANES), const)],
        out_specs=[pl.BlockSpec((TM_MIX, D_MODEL), row),
                   pl.BlockSpec((ROUTER_ROWS, TM_MIX), lambda i: (0, i))],
        out_shape=[jax.ShapeDtypeStruct((n, D_MODEL), F32),
                   jax.ShapeDtypeStruct((ROUTER_ROWS, n), F32)],
        compiler_params=pltpu.CompilerParams(dimension_semantics=("arbitrary",),
                                             vmem_limit_bytes=VMEM_LIMIT),
        name="mix_router",
    )(sb, sgn, x2, sb_g, w_out_b, ffn_g, wr2, br)


_PAD_BITS = tuple(1 << b for b in reversed(range(TM_EXPERT.bit_length() - 1)))


def _dispatch_kernel(dest_ref, pad_start_ref, pad_count_ref, nt_ref, h_ref, g_ref, zeros_ref, xs_ref,
                     hn_ref, sem, zsem):
    tm = TM_DISPATCH
    i = pl.program_id(0)
    n_steps = pl.num_programs(0) - 1
    n = n_steps * tm
    base = (i - 1) * tm
    prev = hn_ref.at[lax.rem(i + 1, 2)]
    n_tiles_max = xs_ref.shape[0] // (TM_EXPERT * ROW_TILE)

    def pad_copies(do):
        for e in range(N_EXPERTS):
            start = pad_start_ref[e]
            count = pad_count_ref[e]
            for bit in _PAD_BITS:
                @pl.when((count & bit) != 0)
                def _(start=start, bit=bit):
                    do(pltpu.make_async_copy(_token_rows(zeros_ref, 0, bit),
                                             _token_rows(xs_ref, start, bit), zsem))
                start = start + (count & bit)
        for k in range(N_EXPERTS):
            tile = nt_ref[0] + k

            @pl.when(tile < n_tiles_max)
            def _(tile=tile):
                do(pltpu.make_async_copy(zeros_ref, _token_rows(xs_ref, tile * TM_EXPERT, TM_EXPERT), zsem))

    @pl.when(i == 0)
    def _():
        pad_copies(lambda cp: cp.start())

    @pl.when(i > 0)
    def _():
        def body(r, c):
            src = _token_rows(prev, r, 1)
            for s in range(2):
                pltpu.make_async_copy(src, _token_rows(xs_ref, dest_ref[s * n + base + r], 1),
                                      sem).start(priority=s)
            return c

        lax.fori_loop(0, tm, body, 0, unroll=8)

    @pl.when(i < n_steps)
    def _():
        _rows_to_tiles(hn_ref.at[lax.rem(i, 2)], _rms(h_ref[...], g_ref[...]))

    @pl.when(i > 0)
    def _():
        for _ in range(2):
            pltpu.make_async_copy(prev, _token_rows(xs_ref, 0, tm), sem).wait()

    @pl.when(i == n_steps)
    def _():
        pad_copies(lambda cp: cp.wait())


def _dispatch(dest, pad_start, pad_count, n_tiles, h, ffn_g, n_rows):
    n_steps = h.shape[0] // TM_DISPATCH
    zeros = jnp.zeros((TM_EXPERT * ROW_TILE, LANES), F32)
    return pl.pallas_call(
        _dispatch_kernel,
        grid_spec=pltpu.PrefetchScalarGridSpec(
            num_scalar_prefetch=4,
            grid=(n_steps + 1,),
            in_specs=[pl.BlockSpec((TM_DISPATCH, D_MODEL), lambda i, *_: (jnp.minimum(i, n_steps - 1), 0)),
                      pl.BlockSpec((1, D_MODEL), lambda i, *_: (0, 0)),
                      pl.BlockSpec(memory_space=pl.ANY)],
            out_specs=pl.BlockSpec(memory_space=pl.ANY),
            scratch_shapes=[pltpu.VMEM((2, TM_DISPATCH * ROW_TILE, LANES), F32),
                            pltpu.SemaphoreType.DMA, pltpu.SemaphoreType.DMA]),
        out_shape=jax.ShapeDtypeStruct((n_rows * ROW_TILE, LANES), F32),
        compiler_params=pltpu.CompilerParams(dimension_semantics=("arbitrary",),
                                             vmem_limit_bytes=VMEM_LIMIT),
        name="dispatch",
    )(dest, pad_start, pad_count, n_tiles, h, ffn_g, zeros)


X_SLOTS = 3


def _expert_kernel(tiles_ref, nt_ref, xs_ref, wg_ref, wu_ref, wd_ref, y_ref,
                   x_buf, sg_buf, su_buf, sd_buf, wgb, wub, wdb, state, w_sems, x_sems):
    tm = TM_EXPERT
    t = pl.program_id(0)
    nt = nt_ref[0]

    def x_copy(tile):
        slot = lax.rem(tile, X_SLOTS)
        return pltpu.make_async_copy(_token_rows(xs_ref, tile * tm, tm), x_buf.at[slot], x_sems.at[slot])

    def weight_copies(e, slot):
        return (pltpu.make_async_copy(wg_ref.at[e], sg_buf.at[slot], w_sems.at[slot]),
                pltpu.make_async_copy(wu_ref.at[e], su_buf.at[slot], w_sems.at[slot]),
                pltpu.make_async_copy(wd_ref.at[e], sd_buf.at[slot], w_sems.at[slot]))

    def next_with_rows(e):
        return lax.while_loop(lambda k: (k < N_EXPERTS) & (tiles_ref[jnp.minimum(k, N_EXPERTS - 1)] == 0),
                              lambda k: k + 1, e + 1)

    @pl.when(t == 0)
    def _():
        first = next_with_rows(jnp.int32(-1))
        state[0] = jnp.int32(-1)
        state[1] = jnp.int32(0)
        state[2] = jnp.int32(1)
        state[3] = first
        for cp in weight_copies(first, 0):
            cp.start()
        x_copy(0).start()

        @pl.when(nt > 1)
        def _():
            x_copy(1).start()

    @pl.when(t + 2 < nt)
    def _():
        x_copy(t + 2).start()

    @pl.when(t < nt)
    def _():
        @pl.when(state[1] == 0)
        def _():
            e = state[3]
            slot = 1 - state[2]
            nxt = next_with_rows(e)
            state[0] = e
            state[1] = tiles_ref[e]
            state[2] = slot
            state[3] = nxt
            for cp in weight_copies(e, slot):
                cp.wait()

            @pl.when(nxt < N_EXPERTS)
            def _():
                for cp in weight_copies(nxt, 1 - slot):
                    cp.start()

            wgb[...] = sg_buf[slot].astype(BF16)
            wub[...] = su_buf[slot].astype(BF16)
            wdb[...] = sd_buf[slot].astype(BF16)

        state[1] = state[1] - 1
        x_copy(t).wait()
        x = _tiles_to_rows(x_buf.at[lax.rem(t, X_SLOTS)], tm).astype(BF16)
        g = _dot(x, wgb[...])
        u = _dot(x, wub[...])
        hidden = (g * jax.nn.sigmoid(g)) * u
        _rows_to_tiles(y_ref, _dot(hidden.astype(BF16), wdb[...]))

    @pl.when(t >= nt)
    def _():
        y_ref[...] = jnp.zeros_like(y_ref)


def _experts(tiles, n_tiles, xs, wg, wu, wd):
    n_rows = xs.shape[0] // ROW_TILE
    any_spec = pl.BlockSpec(memory_space=pl.ANY)
    return pl.pallas_call(
        _expert_kernel,
        grid_spec=pltpu.PrefetchScalarGridSpec(
            num_scalar_prefetch=2,
            grid=(n_rows // TM_EXPERT,),
            in_specs=[any_spec, any_spec, any_spec, any_spec],
            out_specs=pl.BlockSpec((TM_EXPERT * ROW_TILE, LANES), lambda t, *_: (t, 0)),
            scratch_shapes=[pltpu.VMEM((X_SLOTS, TM_EXPERT * ROW_TILE, LANES), F32),
                            pltpu.VMEM((2, D_MODEL, D_EXPERT), F32),
                            pltpu.VMEM((2, D_MODEL, D_EXPERT), F32),
                            pltpu.VMEM((2, D_EXPERT, D_MODEL), F32),
                            pltpu.VMEM((D_MODEL, D_EXPERT), BF16),
                            pltpu.VMEM((D_MODEL, D_EXPERT), BF16),
                            pltpu.VMEM((D_EXPERT, D_MODEL), BF16),
                            pltpu.SMEM((4,), jnp.int32),
                            pltpu.SemaphoreType.DMA((2,)),
                            pltpu.SemaphoreType.DMA((X_SLOTS,))]),
        out_shape=jax.ShapeDtypeStruct((n_rows * ROW_TILE, LANES), F32),
        compiler_params=pltpu.CompilerParams(dimension_semantics=("arbitrary",),
                                             vmem_limit_bytes=VMEM_LIMIT),
        name="expert_mlp",
    )(tiles, n_tiles, xs, wg, wu, wd)


def _combine_kernel(dest_ref, h_ref, rw_ref, fg_ref, y_ref, o_ref, buf, sems):
    tm = TM_COMBINE
    i = pl.program_id(0)
    n_steps = pl.num_programs(0)
    n = n_steps * tm
    cur = i % 2

    def fetch(step, half):
        def body(r, c):
            for s in range(2):
                pltpu.make_async_copy(_token_rows(y_ref, dest_ref[s * n + step * tm + r], 1),
                                      _token_rows(buf.at[half, s], r, 1),
                                      sems.at[half]).start(priority=s)
            return c

        lax.fori_loop(0, tm, body, 0, unroll=8)

    @pl.when(i == 0)
    def _():
        fetch(0, 0)

    @pl.when(i + 1 < n_steps)
    def _():
        fetch(i + 1, 1 - cur)

    for s in range(2):
        pltpu.make_async_copy(_token_rows(y_ref, 0, tm), buf.at[cur, s], sems.at[cur]).wait()
    rw = rw_ref[...]
    out = (h_ref[...] + rw[:, 0:1] * _tiles_to_rows(buf.at[cur, 0], tm)
           + rw[:, 1:2] * _tiles_to_rows(buf.at[cur, 1], tm))
    o_ref[...] = _rms(out, fg_ref[...])


def _combine(dest, h, rw, final_g, ys):
    n = h.shape[0]
    return pl.pallas_call(
        _combine_kernel,
        grid_spec=pltpu.PrefetchScalarGridSpec(
            num_scalar_prefetch=1,
            grid=(n // TM_COMBINE,),
            in_specs=[pl.BlockSpec((TM_COMBINE, D_MODEL), lambda i, d: (i, 0)),
                      pl.BlockSpec((TM_COMBINE, LANES), lambda i, d: (i, 0)),
                      pl.BlockSpec((1, D_MODEL), lambda i, d: (0, 0)),
                      pl.BlockSpec(memory_space=pl.ANY)],
            out_specs=pl.BlockSpec((TM_COMBINE, D_MODEL), lambda i, d: (i, 0)),
            scratch_shapes=[pltpu.VMEM((2, 2, TM_COMBINE * ROW_TILE, LANES), F32),
                            pltpu.SemaphoreType.DMA((2,))]),
        out_shape=jax.ShapeDtypeStruct((n, D_MODEL), F32),
        compiler_params=pltpu.CompilerParams(dimension_semantics=("arbitrary",),
                                             vmem_limit_bytes=VMEM_LIMIT),
        name="combine",
    )(dest, h, rw, final_g, ys)


def _schedule(counts):
    tiles = (counts + TM_EXPERT - 1) // TM_EXPERT
    tile_end = jnp.cumsum(tiles)
    offsets = (tile_end - tiles) * TM_EXPERT
    return tiles, offsets, tile_end[-1:]


def _layer(x, attn_g, w_in, sg_g, w_sp, b_sp, sb_g, sg_out_g, w_out, ffn_g,
           w_rg, b_rg, w_re, b_re, w_gate, w_up, w_down):
    batch, seq, _ = x.shape
    n = batch * seq
    x2 = x.reshape(n, D_MODEL)
    row = lambda v: v.reshape(1, -1)

    bsp_full = jnp.repeat(b_sp.T, HEAD_DIM, axis=1)
    qkv, sgn = _inproj(x2, row(attn_g), w_in.astype(BF16), row(sg_g), w_sp, bsp_full, row(sg_out_g))
    sb = _attention(qkv, batch, seq).reshape(n, SB_WIDTH)

    pad_lanes = lambda v, width: jnp.pad(v, [(0, 0)] * (v.ndim - 1) + [(0, width - v.shape[-1])])
    w_r = jnp.concatenate([pad_lanes(w_rg, ROUTER_LANE0),
                           jnp.transpose(w_re, (1, 0, 2)).reshape(D_MODEL, N_EXPERTS)], axis=1)
    w_r = pad_lanes(w_r, LANES)
    wr_hi = w_r.astype(BF16)
    wr_lo = (w_r - wr_hi.astype(F32)).astype(BF16)
    wr2 = jnp.concatenate([wr_hi, wr_lo], axis=1)
    b_r = pad_lanes(jnp.concatenate([pad_lanes(b_rg, ROUTER_LANE0), b_re.reshape(-1)]), LANES)

    h, lg = _mix(sb, sgn, x2, row(sb_g), w_out.astype(BF16), row(ffn_g), wr2, row(b_r))
    ri, rw, cnt = _route(lg)

    counts = cnt[:, 0].astype(jnp.int32)
    n_rows = 2 * n + N_EXPERTS * TM_EXPERT
    tiles, offsets, n_tiles = _schedule(counts)
    expert, rank = ri[0:2], ri[2:4]
    is_e = expert[None] == jnp.arange(N_EXPERTS, dtype=jnp.int32)[:, None, None]
    dest = (jnp.sum(jnp.where(is_e, offsets[:, None, None], 0), axis=0) + rank).reshape(-1)
    pad_start = offsets + counts
    pad_count = (-counts) % TM_EXPERT

    xs = _dispatch(dest, pad_start, pad_count, n_tiles, h, row(ffn_g), n_rows)
    ys = _experts(tiles, n_tiles, xs,
                  w_gate.reshape(N_EXPERTS, D_MODEL, D_EXPERT),
                  w_up.reshape(N_EXPERTS, D_MODEL, D_EXPERT),
                  w_down.reshape(N_EXPERTS, D_EXPERT, D_MODEL))
    return dest, h, rw, ys


def kernel(x, attn_norm_g, w_in, sg_norm_g, w_spatial, b_spatial, sb_out_norm_g, sg_out_norm_g,
           w_out, ffn_norm_g, w_router_group, b_router_group, w_router_expert, b_router_expert,
           w_gate, w_up, w_down, final_norm_g):
    assert attn_norm_g.shape[0] == 1, "single-layer problem"
    batch, seq, _ = x.shape
    dest, h, rw, ys = _layer(x, attn_norm_g[0], w_in[0], sg_norm_g[0], w_spatial[0], b_spatial[0],
                             sb_out_norm_g[0], sg_out_norm_g[0], w_out[0], ffn_norm_g[0],
                             w_router_group[0], b_router_group[0], w_router_expert[0],
                             b_router_expert[0], w_gate[0], w_up[0], w_down[0])
    out = _combine(dest, h, rw, final_norm_g.reshape(1, -1), ys)
    return out.reshape(batch, seq, D_MODEL)
```

```python
import functools
import math

import jax
import jax.numpy as jnp
from jax import lax
from jax.experimental import pallas as pl
from jax.experimental.pallas import tpu as pltpu

D_MODEL = 1024
HEAD_DIM = 64
SB_WIDTH = 512
SG_WIDTH = 512
SG_HEADS = 8
D_IN = 3 * SB_WIDTH + 2 * SG_WIDTH
CHUNK = 128
N_GROUPS = 4
EXPERTS_PER_GROUP = 8
N_EXPERTS = N_GROUPS * EXPERTS_PER_GROUP
D_EXPERT = 512
EPS = 1e-6
F32_EXP_UNDERFLOW = 110.0

LANES = 128
SUBLANES = 8
ROW_TILE = D_MODEL // LANES
assert ROW_TILE == SUBLANES
HEAD_PAIR = 2 * HEAD_DIM
ROUTER_LANE0 = SUBLANES
ROUTER_ROWS = ROUTER_LANE0 + N_EXPERTS
assert EXPERTS_PER_GROUP == SUBLANES and N_GROUPS <= ROUTER_LANE0

TM_PROJ = 1024
TQ_ATTN = 256
ATTN_TOP_ROWS = 160
TM_MIX = 1024
TM_ROUTE = 1024
TM_DISPATCH = 1024
TM_EXPERT = 512
TM_COMBINE = 512
VMEM_LIMIT = 48 * 1024 * 1024

F32 = jnp.float32
BF16 = jnp.bfloat16


def _rms(x, g):
    return x * lax.rsqrt(jnp.mean(x * x, axis=-1, keepdims=True) + EPS) * g


def _gelu(x):
    c = math.sqrt(2.0 / math.pi)
    return x * (0.5 * (1.0 + jnp.tanh(c * (x + 0.044715 * (x * x * x)))))


def _softplus(z):
    return jnp.maximum(z, 0.0) + jnp.log(1.0 + jnp.exp(-jnp.abs(z)))


def _dot(a, b):
    return jnp.dot(a, b, preferred_element_type=F32)


def _rows_to_tiles(ref, x):
    m = x.shape[0]
    for k in range(ROW_TILE):
        ref[pl.ds(k, m, stride=ROW_TILE), :] = x[:, k * LANES:(k + 1) * LANES]


def _tiles_to_rows(ref, m):
    return jnp.concatenate([ref[pl.ds(k, m, stride=ROW_TILE), :] for k in range(ROW_TILE)], axis=1)


def _token_rows(ref, first_token, n_tokens):
    return ref.at[pl.ds(pl.multiple_of(first_token * ROW_TILE, ROW_TILE), n_tokens * ROW_TILE)]


def _split_bf16(x):
    hi = x.astype(BF16)
    lo = (x - hi.astype(F32)).astype(BF16)
    return hi, lo


def _inproj_kernel(x_ref, g_ref, w_ref, sgg_ref, wsp_ref, bsp_ref, sgog_ref, qkv_ref, sgn_ref,
                   gu_ref, vgn_ref, sg_ref):
    tm = TM_PROJ
    hb = _rms(x_ref[...], g_ref[...]).astype(BF16)
    gv = _gelu(_dot(hb, w_ref[:, 3 * SB_WIDTH + SG_WIDTH:D_IN]))
    vgn_ref[...] = _rms(gv, sgg_ref[...]).astype(BF16)
    gu_ref[...] = _gelu(_dot(hb, w_ref[:, 3 * SB_WIDTH:3 * SB_WIDTH + SG_WIDTH]))
    q = _dot(hb, w_ref[:, 0:SB_WIDTH]) * (1.0 / math.sqrt(HEAD_DIM))
    qkv_ref[:, 0:SB_WIDTH] = q.astype(BF16)
    qkv_ref[:, SB_WIDTH:2 * SB_WIDTH] = _dot(hb, w_ref[:, SB_WIDTH:2 * SB_WIDTH]).astype(BF16)

    lane = lax.broadcasted_iota(jnp.int32, (1, LANES), 1)
    first = lane < HEAD_DIM
    zero = jnp.zeros((), BF16)
    r_c = lax.broadcasted_iota(jnp.int32, (CHUNK, CHUNK), 0)
    c_c = lax.broadcasted_iota(jnp.int32, (CHUNK, CHUNK), 1)
    tril = r_c >= c_c
    n_pairs = SG_WIDTH // HEAD_PAIR
    w_pairs = []
    for p in range(n_pairs):
        w0 = jnp.where(tril, wsp_ref[2 * p], 0.0).astype(BF16)
        w1 = jnp.where(tril, wsp_ref[2 * p + 1], 0.0).astype(BF16)
        w_pairs.append(jnp.concatenate([w0, w1], axis=1))
    bsp = bsp_ref[...]
    for c in range(tm // CHUNK):
        rows = slice(c * CHUNK, (c + 1) * CHUNK)
        for p in range(n_pairs):
            cols = slice(p * HEAD_PAIR, (p + 1) * HEAD_PAIR)
            vg = vgn_ref[rows, cols]
            rhs = jnp.concatenate([jnp.where(first, vg, zero), jnp.where(first, zero, vg)], axis=0)
            mixed = _dot(w_pairs[p], rhs) + bsp[:, cols]
            sg_ref[rows, cols] = gu_ref[rows, cols] * mixed
    qkv_ref[:, 2 * SB_WIDTH:3 * SB_WIDTH] = _dot(hb, w_ref[:, 2 * SB_WIDTH:3 * SB_WIDTH]).astype(BF16)
    sgn_ref[...] = _rms(sg_ref[...], sgog_ref[...]).astype(BF16)


def _inproj(x2, attn_g, w_in_b, sg_g, wsp, bsp_full, sg_out_g):
    n = x2.shape[0]
    row = lambda i: (i, 0)
    const = lambda i: (0, 0)
    return pl.pallas_call(
        _inproj_kernel,
        grid=(n // TM_PROJ,),
        in_specs=[pl.BlockSpec((TM_PROJ, D_MODEL), row),
                  pl.BlockSpec((1, D_MODEL), const),
                  pl.BlockSpec((D_MODEL, D_IN), const),
                  pl.BlockSpec((1, SG_WIDTH), const),
                  pl.BlockSpec((SG_HEADS, CHUNK, CHUNK), lambda i: (0, 0, 0)),
                  pl.BlockSpec((CHUNK, SG_WIDTH), const),
                  pl.BlockSpec((1, SG_WIDTH), const)],
        out_specs=[pl.BlockSpec((TM_PROJ, 3 * SB_WIDTH), row),
                   pl.BlockSpec((TM_PROJ, SG_WIDTH), row)],
        out_shape=[jax.ShapeDtypeStruct((n, 3 * SB_WIDTH), BF16),
                   jax.ShapeDtypeStruct((n, SG_WIDTH), BF16)],
        scratch_shapes=[pltpu.VMEM((TM_PROJ, SG_WIDTH), F32),
                        pltpu.VMEM((TM_PROJ, SG_WIDTH), BF16),
                        pltpu.VMEM((TM_PROJ, SG_WIDTH), F32)],
        compiler_params=pltpu.CompilerParams(dimension_semantics=("arbitrary",),
                                             vmem_limit_bytes=VMEM_LIMIT),
        name="inproj",
    )(x2, attn_g, w_in_b, sg_g, wsp, bsp_full, sg_out_g)


def _attn_kernel(q_ref, k_ref, v_ref, o_ref, q2_ref, carry_ref):
    t = TQ_ATTN
    n_pairs = SB_WIDTH // HEAD_PAIR
    qi = pl.program_id(1)
    lane = lax.broadcasted_iota(jnp.int32, (1, HEAD_PAIR), 1)
    head_lanes = (lane < HEAD_DIM, lane >= HEAD_DIM)
    zero = jnp.zeros((), BF16)
    for p in range(n_pairs):
        qp = q_ref[0, :, p * HEAD_PAIR:(p + 1) * HEAD_PAIR]
        for h in range(2):
            q2_ref[(2 * p + h) * t:(2 * p + h + 1) * t, :] = jnp.where(head_lanes[h], qp, zero)
    r_idx = lax.broadcasted_iota(jnp.int32, (t, t), 0)
    c_idx = lax.broadcasted_iota(jnp.int32, (t, t), 1)
    suffix = (r_idx > c_idx).astype(BF16)
    suffix2 = jnp.concatenate([suffix, suffix], axis=0)
    causal = c_idx < r_idx

    o_ref[...] = jnp.zeros_like(o_ref)
    carry_ref[...] = jnp.zeros_like(carry_ref)

    def block(j, diag, m):
        start = pl.multiple_of(j * t, t)
        mask2 = jnp.concatenate([causal, causal], axis=0) if diag else None
        st = [dict() for _ in range(n_pairs)]

        def head_rows(p):
            return [slice((2 * p + h) * t, (2 * p + h) * t + m) for h in range(2)]

        def scores(p):
            d = st[p]
            d["cols"] = slice(p * HEAD_PAIR, (p + 1) * HEAD_PAIR)
            kb = k_ref[0, pl.ds(start, t), d["cols"]]
            q2 = jnp.concatenate([q2_ref[r, :] for r in head_rows(p)], axis=0)
            z = lax.dot_general(q2, kb, (((1,), (1,)), ((), ())),
                                preferred_element_type=F32)
            sp = _softplus(z)
            nl = jnp.where(mask2, sp, 0.0) if diag else sp
            hi, lo = _split_bf16(nl)
            d["hl"] = jnp.concatenate([hi, lo], axis=1)
            d["log_beta"] = z - sp
            d["nl0"] = nl[:, 0:1]

        def weights(p):
            d = st[p]
            hl = d["hl"]
            after = jnp.concatenate([_dot(hl[0:m], suffix2), _dot(hl[m:2 * m], suffix2)], axis=0)
            carry = jnp.concatenate([carry_ref[r, :] for r in head_rows(p)], axis=0)
            a = jnp.exp(d["log_beta"] - after - carry)
            if diag:
                a = jnp.where(mask2, a, 0.0)
            a = a.astype(BF16)
            d["a2"] = jnp.concatenate([a[0:m], a[m:2 * m]], axis=1)
            new_carry = carry + after[:, 0:1] + d["nl0"]
            for h, r in enumerate(head_rows(p)):
                carry_ref[r, :] = new_carry[h * m:(h + 1) * m]

        def values(p):
            d = st[p]
            vb = v_ref[0, pl.ds(start, t), d["cols"]]
            v2 = jnp.concatenate([jnp.where(head_lanes[0], vb, zero),
                                  jnp.where(head_lanes[1], vb, zero)], axis=0)
            o_ref[0, 0:m, d["cols"]] += _dot(d["a2"], v2)

        for step in range(n_pairs + 2):
            if step < n_pairs:
                scores(step)
            if 0 <= step - 1 < n_pairs:
                weights(step - 1)
            if 0 <= step - 2 < n_pairs:
                values(step - 2)

    top = ATTN_TOP_ROWS

    def flags():
        bottom = jnp.concatenate([carry_ref[hh * t + top:(hh + 1) * t, :] for hh in range(2 * n_pairs)], axis=0)
        return (jnp.min(carry_ref[...]) < F32_EXP_UNDERFLOW, jnp.min(bottom) >= F32_EXP_UNDERFLOW)

    block(qi, True, t)

    def body(state):
        it, _, bottom_done = state
        j = qi - 1 - it

        @pl.when(bottom_done)
        def _():
            block(j, False, top)

        @pl.when(jnp.logical_not(bottom_done))
        def _():
            block(j, False, t)

        return (it + 1,) + flags()

    lax.while_loop(lambda s: (s[0] < qi) & s[1], body, (jnp.int32(0),) + flags())


def _attention(qkv, batch, seq):
    qkv3 = qkv.reshape(batch, seq, 3 * SB_WIDTH)
    n_heads = SB_WIDTH // HEAD_DIM
    return pl.pallas_call(
        _attn_kernel,
        grid=(batch, seq // TQ_ATTN),
        in_specs=[pl.BlockSpec((1, TQ_ATTN, SB_WIDTH), lambda b, i: (b, i, 0)),
                  pl.BlockSpec((1, seq, SB_WIDTH), lambda b, i: (b, 0, 1)),
                  pl.BlockSpec((1, seq, SB_WIDTH), lambda b, i: (b, 0, 2))],
        out_specs=pl.BlockSpec((1, TQ_ATTN, SB_WIDTH), lambda b, i: (b, i, 0)),
        out_shape=jax.ShapeDtypeStruct((batch, seq, SB_WIDTH), F32),
        scratch_shapes=[pltpu.VMEM((n_heads * TQ_ATTN, HEAD_PAIR), BF16),
                        pltpu.VMEM((n_heads * TQ_ATTN, 1), F32)],
        compiler_params=pltpu.CompilerParams(dimension_semantics=("arbitrary",) * 2,
                                             vmem_limit_bytes=VMEM_LIMIT),
        name="sb_attention",
    )(qkv3, qkv3, qkv3)


def _mix_kernel(sb_ref, sgn_ref, x_ref, sbg_ref, wout_ref, ffng_ref, wr2_ref, br_ref,
                h_ref, lg_ref):
    sbn = _rms(sb_ref[...], sbg_ref[...]).astype(BF16)
    h = x_ref[...] + _dot(sbn, wout_ref[0:SB_WIDTH, :]) + _dot(sgn_ref[...], wout_ref[SB_WIDTH:, :])
    h_ref[...] = h
    hn = _rms(h, ffng_ref[...])

    hn_hi, hn_lo = _split_bf16(hn)
    both = _dot(hn_hi, wr2_ref[...])
    logits = both[:, 0:LANES] + both[:, LANES:] + _dot(hn_lo, wr2_ref[:, 0:LANES]) + br_ref[...]
    lg_ref[...] = logits.T[0:ROUTER_ROWS, :]


def _route_kernel(lg_ref, ri_ref, rw_ref, cnt_ref, count_ref):
    tr = TM_ROUTE
    i = pl.program_id(0)

    @pl.when(i == 0)
    def _():
        count_ref[...] = jnp.zeros_like(count_ref)

    neg = jnp.float32(-jnp.inf)
    row8 = lax.broadcasted_iota(jnp.int32, (SUBLANES, tr), 0)

    def top(v):
        m = jnp.max(v, axis=0, keepdims=True)
        return m, jnp.min(jnp.where(v == m, row8, SUBLANES), axis=0, keepdims=True)

    def group_rows(g):
        return lg_ref[ROUTER_LANE0 + g * EXPERTS_PER_GROUP:ROUTER_LANE0 + (g + 1) * EXPERTS_PER_GROUP, :]

    gl = jnp.where(row8 < N_GROUPS, lg_ref[0:SUBLANES, :], neg)
    gmax, gidx = top(gl)
    gweight = 1.0 / jnp.sum(jnp.exp(gl - gmax), axis=0, keepdims=True)
    el = group_rows(0)
    for g in range(1, N_GROUPS):
        el = jnp.where(gidx == g, group_rows(g), el)
    m1, i1 = top(el)
    m2, i2 = top(jnp.where(row8 == i1, neg, el))
    t21 = jnp.exp(m2 - m1)
    w1 = gweight / (1.0 + t21)
    w2 = gweight * t21 / (1.0 + t21)
    e1 = gidx * EXPERTS_PER_GROUP + i1
    e2 = gidx * EXPERTS_PER_GROUP + i2

    row_e = lax.broadcasted_iota(jnp.int32, (N_EXPERTS, tr), 0)
    sel1 = row_e == e1
    sel2 = row_e == e2
    onehot = jnp.where(sel1 | sel2, 1.0, 0.0)
    r_t = lax.broadcasted_iota(jnp.int32, (tr, tr), 0)
    c_t = lax.broadcasted_iota(jnp.int32, (tr, tr), 1)
    before = (r_t < c_t).astype(BF16)
    running = count_ref[:, 0:1] + _dot(onehot.astype(BF16), before)
    rank1 = jnp.sum(jnp.where(sel1, running, 0.0), axis=0, keepdims=True)
    rank2 = jnp.sum(jnp.where(sel2, running, 0.0), axis=0, keepdims=True)
    new_count = count_ref[:, 0:1] + jnp.sum(onehot, axis=1, keepdims=True)
    count_ref[...] = jnp.broadcast_to(new_count, count_ref.shape)
    cnt_ref[...] = jnp.broadcast_to(new_count, cnt_ref.shape)

    ri_ref[...] = jnp.where(row8 == 0, e1, jnp.where(row8 == 1, e2, jnp.where(
        row8 == 2, rank1.astype(jnp.int32), jnp.where(row8 == 3, rank2.astype(jnp.int32), 0))))
    row128 = lax.broadcasted_iota(jnp.int32, (LANES, tr), 0)
    rw_ref[...] = jnp.where(row128 == 0, w1, jnp.where(row128 == 1, w2, 0.0)).T


def _route(lg):
    n = lg.shape[1]
    return pl.pallas_call(
        _route_kernel,
        grid=(n // TM_ROUTE,),
        in_specs=[pl.BlockSpec((ROUTER_ROWS, TM_ROUTE), lambda i: (0, i))],
        out_specs=[pl.BlockSpec((SUBLANES, TM_ROUTE), lambda i: (0, i)),
                   pl.BlockSpec((TM_ROUTE, LANES), lambda i: (i, 0)),
                   pl.BlockSpec((N_EXPERTS, LANES), lambda i: (0, 0))],
        out_shape=[jax.ShapeDtypeStruct((SUBLANES, n), jnp.int32),
                   jax.ShapeDtypeStruct((n, LANES), F32),
                   jax.ShapeDtypeStruct((N_EXPERTS, LANES), F32)],
        scratch_shapes=[pltpu.VMEM((N_EXPERTS, LANES), F32)],
        compiler_params=pltpu.CompilerParams(dimension_semantics=("arbitrary",),
                                             vmem_limit_bytes=VMEM_LIMIT),
        name="route",
    )(lg)


def _mix(sb, sgn, x2, sb_g, w_out_b, ffn_g, wr2, br):
    n = x2.shape[0]
    row = lambda i: (i, 0)
    const = lambda i: (0, 0)
    return pl.pallas_call(
        _mix_kernel,
        grid=(n // TM_MIX,),
        in_specs=[pl.BlockSpec((TM_MIX, SB_WIDTH), row),
                  pl.BlockSpec((TM_MIX, SG_WIDTH), row),
                  pl.BlockSpec((TM_MIX, D_MODEL), row),
                  pl.BlockSpec((1, SB_WIDTH), const),
                  pl.BlockSpec((D_MODEL, D_MODEL), const),
                  pl.BlockSpec((1, D_MODEL), const),
                  pl.BlockSpec((D_MODEL, 2 * LANES), const),
                  pl.BlockSpec((1, LANES), const)],
        out_specs=[pl.BlockSpec((TM_MIX, D_MODEL), row),
                   pl.BlockSpec((ROUTER_ROWS, TM_MIX), lambda i: (0, i))],
        out_shape=[jax.ShapeDtypeStruct((n, D_MODEL), F32),
                   jax.ShapeDtypeStruct((ROUTER_ROWS, n), F32)],
        compiler_params=pltpu.CompilerParams(dimension_semantics=("arbitrary",),
                                             vmem_limit_bytes=VMEM_LIMIT),
        name="mix_router",
    )(sb, sgn, x2, sb_g, w_out_b, ffn_g, wr2, br)


_PAD_BITS = tuple(1 << b for b in reversed(range(TM_EXPERT.bit_length() - 1)))


def _dispatch_kernel(dest_ref, pad_start_ref, pad_count_ref, nt_ref, h_ref, g_ref, zeros_ref, xs_ref,
                     hn_ref, sem, zsem):
    tm = TM_DISPATCH
    i = pl.program_id(0)
    n_steps = pl.num_programs(0) - 1
    n = n_steps * tm
    base = (i - 1) * tm
    prev = hn_ref.at[lax.rem(i + 1, 2)]
    n_tiles_max = xs_ref.shape[0] // (TM_EXPERT * ROW_TILE)

    def pad_copies(do):
        for e in range(N_EXPERTS):
            start = pad_start_ref[e]
            count = pad_count_ref[e]
            for bit in _PAD_BITS:
                @pl.when((count & bit) != 0)
                def _(start=start, bit=bit):
                    do(pltpu.make_async_copy(_token_rows(zeros_ref, 0, bit),
                                             _token_rows(xs_ref, start, bit), zsem))
                start = start + (count & bit)
        for k in range(N_EXPERTS):
            tile = nt_ref[0] + k

            @pl.when(tile < n_tiles_max)
            def _(tile=tile):
                do(pltpu.make_async_copy(zeros_ref, _token_rows(xs_ref, tile * TM_EXPERT, TM_EXPERT), zsem))

    @pl.when(i == 0)
    def _():
        pad_copies(lambda cp: cp.start())

    @pl.when(i > 0)
    def _():
        def body(r, c):
            src = _token_rows(prev, r, 1)
            for s in range(2):
                pltpu.make_async_copy(src, _token_rows(xs_ref, dest_ref[s * n + base + r], 1),
                                      sem).start(priority=s)
            return c

        lax.fori_loop(0, tm, body, 0, unroll=8)

    @pl.when(i < n_steps)
    def _():
        _rows_to_tiles(hn_ref.at[lax.rem(i, 2)], _rms(h_ref[...], g_ref[...]))

    @pl.when(i > 0)
    def _():
        for _ in range(2):
            pltpu.make_async_copy(prev, _token_rows(xs_ref, 0, tm), sem).wait()

    @pl.when(i == n_steps)
    def _():
        pad_copies(lambda cp: cp.wait())


def _dispatch(dest, pad_start, pad_count, n_tiles, h, ffn_g, n_rows):
    n_steps = h.shape[0] // TM_DISPATCH
    zeros = jnp.zeros((TM_EXPERT * ROW_TILE, LANES), F32)
    return pl.pallas_call(
        _dispatch_kernel,
        grid_spec=pltpu.PrefetchScalarGridSpec(
            num_scalar_prefetch=4,
            grid=(n_steps + 1,),
            in_specs=[pl.BlockSpec((TM_DISPATCH, D_MODEL), lambda i, *_: (jnp.minimum(i, n_steps - 1), 0)),
                      pl.BlockSpec((1, D_MODEL), lambda i, *_: (0, 0)),
                      pl.BlockSpec(memory_space=pl.ANY)],
            out_specs=pl.BlockSpec(memory_space=pl.ANY),
            scratch_shapes=[pltpu.VMEM((2, TM_DISPATCH * ROW_TILE, LANES), F32),
                            pltpu.SemaphoreType.DMA, pltpu.SemaphoreType.DMA]),
        out_shape=jax.ShapeDtypeStruct((n_rows * ROW_TILE, LANES), F32),
        compiler_params=pltpu.CompilerParams(dimension_semantics=("arbitrary",),
                                             vmem_limit_bytes=VMEM_LIMIT),
        name="dispatch",
    )(dest, pad_start, pad_count, n_tiles, h, ffn_g, zeros)


X_SLOTS = 3


def _expert_kernel(tiles_ref, nt_ref, xs_ref, wg_ref, wu_ref, wd_ref, y_ref,
                   x_buf, sg_buf, su_buf, sd_buf, wgb, wub, wdb, state, w_sems, x_sems):
    tm = TM_EXPERT
    t = pl.program_id(0)
    nt = nt_ref[0]

    def x_copy(tile):
        slot = lax.rem(tile, X_SLOTS)
        return pltpu.make_async_copy(_token_rows(xs_ref, tile * tm, tm), x_buf.at[slot], x_sems.at[slot])

    def weight_copies(e, slot):
        return (pltpu.make_async_copy(wg_ref.at[e], sg_buf.at[slot], w_sems.at[slot]),
                pltpu.make_async_copy(wu_ref.at[e], su_buf.at[slot], w_sems.at[slot]),
                pltpu.make_async_copy(wd_ref.at[e], sd_buf.at[slot], w_sems.at[slot]))

    def next_with_rows(e):
        return lax.while_loop(lambda k: (k < N_EXPERTS) & (tiles_ref[jnp.minimum(k, N_EXPERTS - 1)] == 0),
                              lambda k: k + 1, e + 1)

    @pl.when(t == 0)
    def _():
        first = next_with_rows(jnp.int32(-1))
        state[0] = jnp.int32(-1)
        state[1] = jnp.int32(0)
        state[2] = jnp.int32(1)
        state[3] = first
        for cp in weight_copies(first, 0):
            cp.start()
        x_copy(0).start()

        @pl.when(nt > 1)
        def _():
            x_copy(1).start()

    @pl.when(t + 2 < nt)
    def _():
        x_copy(t + 2).start()

    @pl.when(t < nt)
    def _():
        @pl.when(state[1] == 0)
        def _():
            e = state[3]
            slot = 1 - state[2]
            nxt = next_with_rows(e)
            state[0] = e
            state[1] = tiles_ref[e]
            state[2] = slot
            state[3] = nxt
            for cp in weight_copies(e, slot):
                cp.wait()

            @pl.when(nxt < N_EXPERTS)
            def _():
                for cp in weight_copies(nxt, 1 - slot):
                    cp.start()

            wgb[...] = sg_buf[slot].astype(BF16)
            wub[...] = su_buf[slot].astype(BF16)
            wdb[...] = sd_buf[slot].astype(BF16)

        state[1] = state[1] - 1
        x_copy(t).wait()
        x = _tiles_to_rows(x_buf.at[lax.rem(t, X_SLOTS)], tm).astype(BF16)
        g = _dot(x, wgb[...])
        u = _dot(x, wub[...])
        hidden = (g * jax.nn.sigmoid(g)) * u
        _rows_to_tiles(y_ref, _dot(hidden.astype(BF16), wdb[...]))

    @pl.when(t >= nt)
    def _():
        y_ref[...] = jnp.zeros_like(y_ref)


def _experts(tiles, n_tiles, xs, wg, wu, wd):
    n_rows = xs.shape[0] // ROW_TILE
    any_spec = pl.BlockSpec(memory_space=pl.ANY)
    return pl.pallas_call(
        _expert_kernel,
        grid_spec=pltpu.PrefetchScalarGridSpec(
            num_scalar_prefetch=2,
            grid=(n_rows // TM_EXPERT,),
            in_specs=[any_spec, any_spec, any_spec, any_spec],
            out_specs=pl.BlockSpec((TM_EXPERT * ROW_TILE, LANES), lambda t, *_: (t, 0)),
            scratch_shapes=[pltpu.VMEM((X_SLOTS, TM_EXPERT * ROW_TILE, LANES), F32),
                            pltpu.VMEM((2, D_MODEL, D_EXPERT), F32),
                            pltpu.VMEM((2, D_MODEL, D_EXPERT), F32),
                            pltpu.VMEM((2, D_EXPERT, D_MODEL), F32),
                            pltpu.VMEM((D_MODEL, D_EXPERT), BF16),
                            pltpu.VMEM((D_MODEL, D_EXPERT), BF16),
                            pltpu.VMEM((D_EXPERT, D_MODEL), BF16),
                            pltpu.SMEM((4,), jnp.int32),
                            pltpu.SemaphoreType.DMA((2,)),
                            pltpu.SemaphoreType.DMA((X_SLOTS,))]),
        out_shape=jax.ShapeDtypeStruct((n_rows * ROW_TILE, LANES), F32),
        compiler_params=pltpu.CompilerParams(dimension_semantics=("arbitrary",),
                                             vmem_limit_bytes=VMEM_LIMIT),
        name="expert_mlp",
    )(tiles, n_tiles, xs, wg, wu, wd)


def _combine_kernel(dest_ref, h_ref, rw_ref, fg_ref, y_ref, o_ref, buf, sems):
    tm = TM_COMBINE
    i = pl.program_id(0)
    n_steps = pl.num_programs(0)
    n = n_steps * tm
    cur = i % 2

    def fetch(step, half):
        def body(r, c):
            for s in range(2):
                pltpu.make_async_copy(_token_rows(y_ref, dest_ref[s * n + step * tm + r], 1),
                                      _token_rows(buf.at[half, s], r, 1),
                                      sems.at[half]).start(priority=s)
            return c

        lax.fori_loop(0, tm, body, 0, unroll=8)

    @pl.when(i == 0)
    def _():
        fetch(0, 0)

    @pl.when(i + 1 < n_steps)
    def _():
        fetch(i + 1, 1 - cur)

    for s in range(2):
        pltpu.make_async_copy(_token_rows(y_ref, 0, tm), buf.at[cur, s], sems.at[cur]).wait()
    rw = rw_ref[...]
    out = (h_ref[...] + rw[:, 0:1] * _tiles_to_rows(buf.at[cur, 0], tm)
           + rw[:, 1:2] * _tiles_to_rows(buf.at[cur, 1], tm))
    o_ref[...] = _rms(out, fg_ref[...])


def _combine(dest, h, rw, final_g, ys):
    n = h.shape[0]
    return pl.pallas_call(
        _combine_kernel,
        grid_spec=pltpu.PrefetchScalarGridSpec(
            num_scalar_prefetch=1,
            grid=(n // TM_COMBINE,),
            in_specs=[pl.BlockSpec((TM_COMBINE, D_MODEL), lambda i, d: (i, 0)),
                      pl.BlockSpec((TM_COMBINE, LANES), lambda i, d: (i, 0)),
                      pl.BlockSpec((1, D_MODEL), lambda i, d: (0, 0)),
                      pl.BlockSpec(memory_space=pl.ANY)],
            out_specs=pl.BlockSpec((TM_COMBINE, D_MODEL), lambda i, d: (i, 0)),
            scratch_shapes=[pltpu.VMEM((2, 2, TM_COMBINE * ROW_TILE, LANES), F32),
                            pltpu.SemaphoreType.DMA((2,))]),
        out_shape=jax.ShapeDtypeStruct((n, D_MODEL), F32),
        compiler_params=pltpu.CompilerParams(dimension_semantics=("arbitrary",),
                                             vmem_limit_bytes=VMEM_LIMIT),
        name="combine",
    )(dest, h, rw, final_g, ys)


def _schedule(counts):
    tiles = (counts + TM_EXPERT - 1) // TM_EXPERT
    tile_end = jnp.cumsum(tiles)
    offsets = (tile_end - tiles) * TM_EXPERT
    return tiles, offsets, tile_end[-1:]


def _layer(x, attn_g, w_in, sg_g, w_sp, b_sp, sb_g, sg_out_g, w_out, ffn_g,
           w_rg, b_rg, w_re, b_re, w_gate, w_up, w_down):
    batch, seq, _ = x.shape
    n = batch * seq
    x2 = x.reshape(n, D_MODEL)
    row = lambda v: v.reshape(1, -1)

    bsp_full = jnp.repeat(b_sp.T, HEAD_DIM, axis=1)
    qkv, sgn = _inproj(x2, row(attn_g), w_in.astype(BF16), row(sg_g), w_sp, bsp_full, row(sg_out_g))
    sb = _attention(qkv, batch, seq).reshape(n, SB_WIDTH)

    pad_lanes = lambda v, width: jnp.pad(v, [(0, 0)] * (v.ndim - 1) + [(0, width - v.shape[-1])])
    w_r = jnp.concatenate([pad_lanes(w_rg, ROUTER_LANE0),
                           jnp.transpose(w_re, (1, 0, 2)).reshape(D_MODEL, N_EXPERTS)], axis=1)
    w_r = pad_lanes(w_r, LANES)
    wr_hi = w_r.astype(BF16)
    wr_lo = (w_r - wr_hi.astype(F32)).astype(BF16)
    wr2 = jnp.concatenate([wr_hi, wr_lo], axis=1)
    b_r = pad_lanes(jnp.concatenate([pad_lanes(b_rg, ROUTER_LANE0), b_re.reshape(-1)]), LANES)

    h, lg = _mix(sb, sgn, x2, row(sb_g), w_out.astype(BF16), row(ffn_g), wr2, row(b_r))
    ri, rw, cnt = _route(lg)

    counts = cnt[:, 0].astype(jnp.int32)
    n_rows = 2 * n + N_EXPERTS * TM_EXPERT
    tiles, offsets, n_tiles = _schedule(counts)
    expert, rank = ri[0:2], ri[2:4]
    is_e = expert[None] == jnp.arange(N_EXPERTS, dtype=jnp.int32)[:, None, None]
    dest = (jnp.sum(jnp.where(is_e, offsets[:, None, None], 0), axis=0) + rank).reshape(-1)
    pad_start = offsets + counts
    pad_count = (-counts) % TM_EXPERT

    xs = _dispatch(dest, pad_start, pad_count, n_tiles, h, row(ffn_g), n_rows)
    ys = _experts(tiles, n_tiles, xs,
                  w_gate.reshape(N_EXPERTS, D_MODEL, D_EXPERT),
                  w_up.reshape(N_EXPERTS, D_MODEL, D_EXPERT),
                  w_down.reshape(N_EXPERTS, D_EXPERT, D_MODEL))
    return dest, h, rw, ys


def kernel(x, attn_norm_g, w_in, sg_norm_g, w_spatial, b_spatial, sb_out_norm_g, sg_out_norm_g,
           w_out, ffn_norm_g, w_router_group, b_router_group, w_router_expert, b_router_expert,
           w_gate, w_up, w_down, final_norm_g):
    assert attn_norm_g.shape[0] == 1, "single-layer problem"
    batch, seq, _ = x.shape
    dest, h, rw, ys = _layer(x, attn_norm_g[0], w_in[0], sg_norm_g[0], w_spatial[0], b_spatial[0],
                             sb_out_norm_g[0], sg_out_norm_g[0], w_out[0], ffn_norm_g[0],
                             w_router_group[0], b_router_group[0], w_router_expert[0],
                             b_router_expert[0], w_gate[0], w_up[0], w_down[0])
    out = _combine(dest, h, rw, final_norm_g.reshape(1, -1), ys)
    return out.reshape(batch, seq, D_MODEL)
```

```python
import functools
import math

import jax
import jax.numpy as jnp
from jax import lax
from jax.experimental import pallas as pl
from jax.experimental.pallas import tpu as pltpu

D_MODEL = 1024
HEAD_DIM = 64
SB_WIDTH = 512
SG_WIDTH = 512
SG_HEADS = 8
D_IN = 3 * SB_WIDTH + 2 * SG_WIDTH
CHUNK = 128
N_GROUPS = 4
EXPERTS_PER_GROUP = 8
N_EXPERTS = N_GROUPS * EXPERTS_PER_GROUP
D_EXPERT = 512
EPS = 1e-6
F32_EXP_UNDERFLOW = 110.0

LANES = 128
SUBLANES = 8
ROW_TILE = D_MODEL // LANES
assert ROW_TILE == SUBLANES
HEAD_PAIR = 2 * HEAD_DIM
ROUTER_LANE0 = SUBLANES
ROUTER_ROWS = ROUTER_LANE0 + N_EXPERTS
assert EXPERTS_PER_GROUP == SUBLANES and N_GROUPS <= ROUTER_LANE0

TM_PROJ = 1024
TQ_ATTN = 256
ATTN_TOP_ROWS = 160
TM_MIX = 1024
TM_ROUTE = 1024
TM_DISPATCH = 2048
TM_EXPERT = 512
TM_COMBINE = 512
VMEM_LIMIT = 48 * 1024 * 1024

F32 = jnp.float32
BF16 = jnp.bfloat16


def _rms(x, g):
    return x * lax.rsqrt(jnp.mean(x * x, axis=-1, keepdims=True) + EPS) * g


def _gelu(x):
    c = math.sqrt(2.0 / math.pi)
    return x * (0.5 * (1.0 + jnp.tanh(c * (x + 0.044715 * (x * x * x)))))


def _softplus(z):
    return jnp.maximum(z, 0.0) + jnp.log(1.0 + jnp.exp(-jnp.abs(z)))


def _dot(a, b):
    return jnp.dot(a, b, preferred_element_type=F32)


def _rows_to_tiles(ref, x):
    m = x.shape[0]
    for k in range(ROW_TILE):
        ref[pl.ds(k, m, stride=ROW_TILE), :] = x[:, k * LANES:(k + 1) * LANES]


def _tiles_to_rows(ref, m):
    return jnp.concatenate([ref[pl.ds(k, m, stride=ROW_TILE), :] for k in range(ROW_TILE)], axis=1)


def _token_rows(ref, first_token, n_tokens):
    return ref.at[pl.ds(pl.multiple_of(first_token * ROW_TILE, ROW_TILE), n_tokens * ROW_TILE)]


def _split_bf16(x):
    hi = x.astype(BF16)
    lo = (x - hi.astype(F32)).astype(BF16)
    return hi, lo


def _inproj_kernel(x_ref, g_ref, w_ref, sgg_ref, wsp_ref, bsp_ref, sgog_ref, qkv_ref, sgn_ref,
                   gu_ref, vgn_ref, sg_ref):
    tm = TM_PROJ
    hb = _rms(x_ref[...], g_ref[...]).astype(BF16)
    gv = _gelu(_dot(hb, w_ref[:, 3 * SB_WIDTH + SG_WIDTH:D_IN]))
    vgn_ref[...] = _rms(gv, sgg_ref[...]).astype(BF16)
    gu_ref[...] = _gelu(_dot(hb, w_ref[:, 3 * SB_WIDTH:3 * SB_WIDTH + SG_WIDTH]))
    q = _dot(hb, w_ref[:, 0:SB_WIDTH]) * (1.0 / math.sqrt(HEAD_DIM))
    qkv_ref[:, 0:SB_WIDTH] = q.astype(BF16)
    qkv_ref[:, SB_WIDTH:2 * SB_WIDTH] = _dot(hb, w_ref[:, SB_WIDTH:2 * SB_WIDTH]).astype(BF16)

    lane = lax.broadcasted_iota(jnp.int32, (1, LANES), 1)
    first = lane < HEAD_DIM
    zero = jnp.zeros((), BF16)
    r_c = lax.broadcasted_iota(jnp.int32, (CHUNK, CHUNK), 0)
    c_c = lax.broadcasted_iota(jnp.int32, (CHUNK, CHUNK), 1)
    tril = r_c >= c_c
    n_pairs = SG_WIDTH // HEAD_PAIR
    w_pairs = []
    for p in range(n_pairs):
        w0 = jnp.where(tril, wsp_ref[2 * p], 0.0).astype(BF16)
        w1 = jnp.where(tril, wsp_ref[2 * p + 1], 0.0).astype(BF16)
        w_pairs.append(jnp.concatenate([w0, w1], axis=1))
    bsp = bsp_ref[...]
    for c in range(tm // CHUNK):
        rows = slice(c * CHUNK, (c + 1) * CHUNK)
        for p in range(n_pairs):
            cols = slice(p * HEAD_PAIR, (p + 1) * HEAD_PAIR)
            vg = vgn_ref[rows, cols]
            rhs = jnp.concatenate([jnp.where(first, vg, zero), jnp.where(first, zero, vg)], axis=0)
            mixed = _dot(w_pairs[p], rhs) + bsp[:, cols]
            sg_ref[rows, cols] = gu_ref[rows, cols] * mixed
    qkv_ref[:, 2 * SB_WIDTH:3 * SB_WIDTH] = _dot(hb, w_ref[:, 2 * SB_WIDTH:3 * SB_WIDTH]).astype(BF16)
    sgn_ref[...] = _rms(sg_ref[...], sgog_ref[...]).astype(BF16)


def _inproj(x2, attn_g, w_in_b, sg_g, wsp, bsp_full, sg_out_g):
    n = x2.shape[0]
    row = lambda i: (i, 0)
    const = lambda i: (0, 0)
    return pl.pallas_call(
        _inproj_kernel,
        grid=(n // TM_PROJ,),
        in_specs=[pl.BlockSpec((TM_PROJ, D_MODEL), row),
                  pl.BlockSpec((1, D_MODEL), const),
                  pl.BlockSpec((D_MODEL, D_IN), const),
                  pl.BlockSpec((1, SG_WIDTH), const),
                  pl.BlockSpec((SG_HEADS, CHUNK, CHUNK), lambda i: (0, 0, 0)),
                  pl.BlockSpec((CHUNK, SG_WIDTH), const),
                  pl.BlockSpec((1, SG_WIDTH), const)],
        out_specs=[pl.BlockSpec((TM_PROJ, 3 * SB_WIDTH), row),
                   pl.BlockSpec((TM_PROJ, SG_WIDTH), row)],
        out_shape=[jax.ShapeDtypeStruct((n, 3 * SB_WIDTH), BF16),
                   jax.ShapeDtypeStruct((n, SG_WIDTH), BF16)],
        scratch_shapes=[pltpu.VMEM((TM_PROJ, SG_WIDTH), F32),
                        pltpu.VMEM((TM_PROJ, SG_WIDTH), BF16),
                        pltpu.VMEM((TM_PROJ, SG_WIDTH), F32)],
        compiler_params=pltpu.CompilerParams(dimension_semantics=("arbitrary",),
                                             vmem_limit_bytes=VMEM_LIMIT),
        name="inproj",
    )(x2, attn_g, w_in_b, sg_g, wsp, bsp_full, sg_out_g)


def _attn_kernel(q_ref, k_ref, v_ref, o_ref, q2_ref, carry_ref):
    t = TQ_ATTN
    n_pairs = SB_WIDTH // HEAD_PAIR
    qi = pl.program_id(1)
    lane = lax.broadcasted_iota(jnp.int32, (1, HEAD_PAIR), 1)
    head_lanes = (lane < HEAD_DIM, lane >= HEAD_DIM)
    zero = jnp.zeros((), BF16)
    for p in range(n_pairs):
        qp = q_ref[0, :, p * HEAD_PAIR:(p + 1) * HEAD_PAIR]
        for h in range(2):
            q2_ref[(2 * p + h) * t:(2 * p + h + 1) * t, :] = jnp.where(head_lanes[h], qp, zero)
    r_idx = lax.broadcasted_iota(jnp.int32, (t, t), 0)
    c_idx = lax.broadcasted_iota(jnp.int32, (t, t), 1)
    suffix = (r_idx > c_idx).astype(BF16)
    suffix2 = jnp.concatenate([suffix, suffix], axis=0)
    causal = c_idx < r_idx

    o_ref[...] = jnp.zeros_like(o_ref)
    carry_ref[...] = jnp.zeros_like(carry_ref)

    def block(j, diag, m):
        start = pl.multiple_of(j * t, t)
        mask2 = jnp.concatenate([causal, causal], axis=0) if diag else None
        st = [dict() for _ in range(n_pairs)]

        def head_rows(p):
            return [slice((2 * p + h) * t, (2 * p + h) * t + m) for h in range(2)]

        def scores(p):
            d = st[p]
            d["cols"] = slice(p * HEAD_PAIR, (p + 1) * HEAD_PAIR)
            kb = k_ref[0, pl.ds(start, t), d["cols"]]
            q2 = jnp.concatenate([q2_ref[r, :] for r in head_rows(p)], axis=0)
            z = lax.dot_general(q2, kb, (((1,), (1,)), ((), ())),
                                preferred_element_type=F32)
            sp = _softplus(z)
            nl = jnp.where(mask2, sp, 0.0) if diag else sp
            hi, lo = _split_bf16(nl)
            d["hl"] = jnp.concatenate([hi, lo], axis=1)
            d["log_beta"] = z - sp
            d["nl0"] = nl[:, 0:1]

        def weights(p):
            d = st[p]
            hl = d["hl"]
            after = jnp.concatenate([_dot(hl[0:m], suffix2), _dot(hl[m:2 * m], suffix2)], axis=0)
            carry = jnp.concatenate([carry_ref[r, :] for r in head_rows(p)], axis=0)
            a = jnp.exp(d["log_beta"] - after - carry)
            if diag:
                a = jnp.where(mask2, a, 0.0)
            a = a.astype(BF16)
            d["a2"] = jnp.concatenate([a[0:m], a[m:2 * m]], axis=1)
            new_carry = carry + after[:, 0:1] + d["nl0"]
            for h, r in enumerate(head_rows(p)):
                carry_ref[r, :] = new_carry[h * m:(h + 1) * m]

        def values(p):
            d = st[p]
            vb = v_ref[0, pl.ds(start, t), d["cols"]]
            v2 = jnp.concatenate([jnp.where(head_lanes[0], vb, zero),
                                  jnp.where(head_lanes[1], vb, zero)], axis=0)
            o_ref[0, 0:m, d["cols"]] += _dot(d["a2"], v2)

        for step in range(n_pairs + 2):
            if step < n_pairs:
                scores(step)
            if 0 <= step - 1 < n_pairs:
                weights(step - 1)
            if 0 <= step - 2 < n_pairs:
                values(step - 2)

    top = ATTN_TOP_ROWS

    def flags():
        bottom = jnp.concatenate([carry_ref[hh * t + top:(hh + 1) * t, :] for hh in range(2 * n_pairs)], axis=0)
        return (jnp.min(carry_ref[...]) < F32_EXP_UNDERFLOW, jnp.min(bottom) >= F32_EXP_UNDERFLOW)

    block(qi, True, t)

    def body(state):
        it, _, bottom_done = state
        j = qi - 1 - it

        @pl.when(bottom_done)
        def _():
            block(j, False, top)

        @pl.when(jnp.logical_not(bottom_done))
        def _():
            block(j, False, t)

        return (it + 1,) + flags()

    lax.while_loop(lambda s: (s[0] < qi) & s[1], body, (jnp.int32(0),) + flags())


def _attention(qkv, batch, seq):
    qkv3 = qkv.reshape(batch, seq, 3 * SB_WIDTH)
    n_heads = SB_WIDTH // HEAD_DIM
    return pl.pallas_call(
        _attn_kernel,
        grid=(batch, seq // TQ_ATTN),
        in_specs=[pl.BlockSpec((1, TQ_ATTN, SB_WIDTH), lambda b, i: (b, i, 0)),
                  pl.BlockSpec((1, seq, SB_WIDTH), lambda b, i: (b, 0, 1)),
                  pl.BlockSpec((1, seq, SB_WIDTH), lambda b, i: (b, 0, 2))],
        out_specs=pl.BlockSpec((1, TQ_ATTN, SB_WIDTH), lambda b, i: (b, i, 0)),
        out_shape=jax.ShapeDtypeStruct((batch, seq, SB_WIDTH), F32),
        scratch_shapes=[pltpu.VMEM((n_heads * TQ_ATTN, HEAD_PAIR), BF16),
                        pltpu.VMEM((n_heads * TQ_ATTN, 1), F32)],
        compiler_params=pltpu.CompilerParams(dimension_semantics=("arbitrary",) * 2,
                                             vmem_limit_bytes=VMEM_LIMIT),
        name="sb_attention",
    )(qkv3, qkv3, qkv3)


def _mix_kernel(sb_ref, sgn_ref, x_ref, sbg_ref, wout_ref, ffng_ref, wr2_ref, br_ref,
                h_ref, lg_ref):
    sbn = _rms(sb_ref[...], sbg_ref[...]).astype(BF16)
    h = x_ref[...] + _dot(sbn, wout_ref[0:SB_WIDTH, :]) + _dot(sgn_ref[...], wout_ref[SB_WIDTH:, :])
    h_ref[...] = h
    hn = _rms(h, ffng_ref[...])

    hn_hi, hn_lo = _split_bf16(hn)
    both = _dot(hn_hi, wr2_ref[...])
    logits = both[:, 0:LANES] + both[:, LANES:] + _dot(hn_lo, wr2_ref[:, 0:LANES]) + br_ref[...]
    lg_ref[...] = logits.T[0:ROUTER_ROWS, :]


def _route_kernel(lg_ref, ri_ref, rw_ref, cnt_ref, count_ref):
    tr = TM_ROUTE
    i = pl.program_id(0)

    @pl.when(i == 0)
    def _():
        count_ref[...] = jnp.zeros_like(count_ref)

    neg = jnp.float32(-jnp.inf)
    row8 = lax.broadcasted_iota(jnp.int32, (SUBLANES, tr), 0)

    def top(v):
        m = jnp.max(v, axis=0, keepdims=True)
        return m, jnp.min(jnp.where(v == m, row8, SUBLANES), axis=0, keepdims=True)

    def group_rows(g):
        return lg_ref[ROUTER_LANE0 + g * EXPERTS_PER_GROUP:ROUTER_LANE0 + (g + 1) * EXPERTS_PER_GROUP, :]

    gl = jnp.where(row8 < N_GROUPS, lg_ref[0:SUBLANES, :], neg)
    gmax, gidx = top(gl)
    gweight = 1.0 / jnp.sum(jnp.exp(gl - gmax), axis=0, keepdims=True)
    el = group_rows(0)
    for g in range(1, N_GROUPS):
        el = jnp.where(gidx == g, group_rows(g), el)
    m1, i1 = top(el)
    m2, i2 = top(jnp.where(row8 == i1, neg, el))
    t21 = jnp.exp(m2 - m1)
    w1 = gweight / (1.0 + t21)
    w2 = gweight * t21 / (1.0 + t21)
    e1 = gidx * EXPERTS_PER_GROUP + i1
    e2 = gidx * EXPERTS_PER_GROUP + i2

    row_e = lax.broadcasted_iota(jnp.int32, (N_EXPERTS, tr), 0)
    sel1 = row_e == e1
    sel2 = row_e == e2
    onehot = jnp.where(sel1 | sel2, 1.0, 0.0)
    r_t = lax.broadcasted_iota(jnp.int32, (tr, tr), 0)
    c_t = lax.broadcasted_iota(jnp.int32, (tr, tr), 1)
    before = (r_t < c_t).astype(BF16)
    running = count_ref[:, 0:1] + _dot(onehot.astype(BF16), before)
    rank1 = jnp.sum(jnp.where(sel1, running, 0.0), axis=0, keepdims=True)
    rank2 = jnp.sum(jnp.where(sel2, running, 0.0), axis=0, keepdims=True)
    new_count = count_ref[:, 0:1] + jnp.sum(onehot, axis=1, keepdims=True)
    count_ref[...] = jnp.broadcast_to(new_count, count_ref.shape)
    cnt_ref[...] = jnp.broadcast_to(new_count, cnt_ref.shape)

    ri_ref[...] = jnp.where(row8 == 0, e1, jnp.where(row8 == 1, e2, jnp.where(
        row8 == 2, rank1.astype(jnp.int32), jnp.where(row8 == 3, rank2.astype(jnp.int32), 0))))
    row128 = lax.broadcasted_iota(jnp.int32, (LANES, tr), 0)
    rw_ref[...] = jnp.where(row128 == 0, w1, jnp.where(row128 == 1, w2, 0.0)).T


def _route(lg):
    n = lg.shape[1]
    return pl.pallas_call(
        _route_kernel,
        grid=(n // TM_ROUTE,),
        in_specs=[pl.BlockSpec((ROUTER_ROWS, TM_ROUTE), lambda i: (0, i))],
        out_specs=[pl.BlockSpec((SUBLANES, TM_ROUTE), lambda i: (0, i)),
                   pl.BlockSpec((TM_ROUTE, LANES), lambda i: (i, 0)),
                   pl.BlockSpec((N_EXPERTS, LANES), lambda i: (0, 0))],
        out_shape=[jax.ShapeDtypeStruct((SUBLANES, n), jnp.int32),
                   jax.ShapeDtypeStruct((n, LANES), F32),
                   jax.ShapeDtypeStruct((N_EXPERTS, LANES), F32)],
        scratch_shapes=[pltpu.VMEM((N_EXPERTS, LANES), F32)],
        compiler_params=pltpu.CompilerParams(dimension_semantics=("arbitrary",),
                                             vmem_limit_bytes=VMEM_LIMIT),
        name="route",
    )(lg)


def _mix(sb, sgn, x2, sb_g, w_out_b, ffn_g, wr2, br):
    n = x2.shape[0]
    row = lambda i: (i, 0)
    const = lambda i: (0, 0)
    return pl.pallas_call(
        _mix_kernel,
        grid=(n // TM_MIX,),
        in_specs=[pl.BlockSpec((TM_MIX, SB_WIDTH), row),
                  pl.BlockSpec((TM_MIX, SG_WIDTH), row),
                  pl.BlockSpec((TM_MIX, D_MODEL), row),
                  pl.BlockSpec((1, SB_WIDTH), const),
                  pl.BlockSpec((D_MODEL, D_MODEL), const),
                  pl.BlockSpec((1, D_MODEL), const),
                  pl.BlockSpec((D_MODEL, 2 * LANES), const),
                  pl.BlockSpec((1, LANES), const)],
        out_specs=[pl.BlockSpec((TM_MIX, D_MODEL), row),
                   pl.BlockSpec((ROUTER_ROWS, TM_MIX), lambda i: (0, i))],
        out_shape=[jax.ShapeDtypeStruct((n, D_MODEL), F32),
                   jax.ShapeDtypeStruct((ROUTER_ROWS, n), F32)],
        compiler_params=pltpu.CompilerParams(dimension_semantics=("arbitrary",),
                                             vmem_limit_bytes=VMEM_LIMIT),
        name="mix_router",
    )(sb, sgn, x2, sb_g, w_out_b, ffn_g, wr2, br)


_PAD_BITS = tuple(1 << b for b in reversed(range(TM_EXPERT.bit_length() - 1)))


def _dispatch_kernel(dest_ref, pad_start_ref, pad_count_ref, nt_ref, h_ref, g_ref, zeros_ref, xs_ref,
                     hn_ref, sem, zsem):
    tm = TM_DISPATCH
    i = pl.program_id(0)
    n_steps = pl.num_programs(0) - 1
    n = n_steps * tm
    base = (i - 1) * tm
    prev = hn_ref.at[lax.rem(i + 1, 2)]
    n_tiles_max = xs_ref.shape[0] // (TM_EXPERT * ROW_TILE)

    def pad_copies(do):
        for e in range(N_EXPERTS):
            start = pad_start_ref[e]
            count = pad_count_ref[e]
            for bit in _PAD_BITS:
                @pl.when((count & bit) != 0)
                def _(start=start, bit=bit):
                    do(pltpu.make_async_copy(_token_rows(zeros_ref, 0, bit),
                                             _token_rows(xs_ref, start, bit), zsem))
                start = start + (count & bit)
        for k in range(N_EXPERTS):
            tile = nt_ref[0] + k

            @pl.when(tile < n_tiles_max)
            def _(tile=tile):
                do(pltpu.make_async_copy(zeros_ref, _token_rows(xs_ref, tile * TM_EXPERT, TM_EXPERT), zsem))

    @pl.when(i == 0)
    def _():
        pad_copies(lambda cp: cp.start())

    @pl.when(i > 0)
    def _():
        def body(r, c):
            src = _token_rows(prev, r, 1)
            for s in range(2):
                pltpu.make_async_copy(src, _token_rows(xs_ref, dest_ref[s * n + base + r], 1),
                                      sem).start(priority=s)
            return c

        lax.fori_loop(0, tm, body, 0, unroll=8)

    @pl.when(i < n_steps)
    def _():
        _rows_to_tiles(hn_ref.at[lax.rem(i, 2)], _rms(h_ref[...], g_ref[...]))

    @pl.when(i > 0)
    def _():
        for _ in range(2):
            pltpu.make_async_copy(prev, _token_rows(xs_ref, 0, tm), sem).wait()

    @pl.when(i == n_steps)
    def _():
        pad_copies(lambda cp: cp.wait())


def _dispatch(dest, pad_start, pad_count, n_tiles, h, ffn_g, n_rows):
    n_steps = h.shape[0] // TM_DISPATCH
    zeros = jnp.zeros((TM_EXPERT * ROW_TILE, LANES), F32)
    return pl.pallas_call(
        _dispatch_kernel,
        grid_spec=pltpu.PrefetchScalarGridSpec(
            num_scalar_prefetch=4,
            grid=(n_steps + 1,),
            in_specs=[pl.BlockSpec((TM_DISPATCH, D_MODEL), lambda i, *_: (jnp.minimum(i, n_steps - 1), 0)),
                      pl.BlockSpec((1, D_MODEL), lambda i, *_: (0, 0)),
                      pl.BlockSpec(memory_space=pl.ANY)],
            out_specs=pl.BlockSpec(memory_space=pl.ANY),
            scratch_shapes=[pltpu.VMEM((2, TM_DISPATCH * ROW_TILE, LANES), F32),
                            pltpu.SemaphoreType.DMA, pltpu.SemaphoreType.DMA]),
        out_shape=jax.ShapeDtypeStruct((n_rows * ROW_TILE, LANES), F32),
        compiler_params=pltpu.CompilerParams(dimension_semantics=("arbitrary",),
                                             vmem_limit_bytes=VMEM_LIMIT),
        name="dispatch",
    )(dest, pad_start, pad_count, n_tiles, h, ffn_g, zeros)


X_SLOTS = 3


def _expert_kernel(tiles_ref, nt_ref, xs_ref, wg_ref, wu_ref, wd_ref, y_ref,
                   x_buf, sg_buf, su_buf, sd_buf, wgb, wub, wdb, state, w_sems, x_sems):
    tm = TM_EXPERT
    t = pl.program_id(0)
    nt = nt_ref[0]

    def x_copy(tile):
        slot = lax.rem(tile, X_SLOTS)
        return pltpu.make_async_copy(_token_rows(xs_ref, tile * tm, tm), x_buf.at[slot], x_sems.at[slot])

    def weight_copies(e, slot):
        return (pltpu.make_async_copy(wg_ref.at[e], sg_buf.at[slot], w_sems.at[slot]),
                pltpu.make_async_copy(wu_ref.at[e], su_buf.at[slot], w_sems.at[slot]),
                pltpu.make_async_copy(wd_ref.at[e], sd_buf.at[slot], w_sems.at[slot]))

    def next_with_rows(e):
        return lax.while_loop(lambda k: (k < N_EXPERTS) & (tiles_ref[jnp.minimum(k, N_EXPERTS - 1)] == 0),
                              lambda k: k + 1, e + 1)

    @pl.when(t == 0)
    def _():
        first = next_with_rows(jnp.int32(-1))
        state[0] = jnp.int32(-1)
        state[1] = jnp.int32(0)
        state[2] = jnp.int32(1)
        state[3] = first
        for cp in weight_copies(first, 0):
            cp.start()
        x_copy(0).start()

        @pl.when(nt > 1)
        def _():
            x_copy(1).start()

    @pl.when(t + 2 < nt)
    def _():
        x_copy(t + 2).start()

    @pl.when(t < nt)
    def _():
        @pl.when(state[1] == 0)
        def _():
            e = state[3]
            slot = 1 - state[2]
            nxt = next_with_rows(e)
            state[0] = e
            state[1] = tiles_ref[e]
            state[2] = slot
            state[3] = nxt
            for cp in weight_copies(e, slot):
                cp.wait()

            @pl.when(nxt < N_EXPERTS)
            def _():
                for cp in weight_copies(nxt, 1 - slot):
                    cp.start()

            wgb[...] = sg_buf[slot].astype(BF16)
            wub[...] = su_buf[slot].astype(BF16)
            wdb[...] = sd_buf[slot].astype(BF16)

        state[1] = state[1] - 1
        x_copy(t).wait()
        x = _tiles_to_rows(x_buf.at[lax.rem(t, X_SLOTS)], tm).astype(BF16)
        g = _dot(x, wgb[...])
        u = _dot(x, wub[...])
        hidden = (g * jax.nn.sigmoid(g)) * u
        _rows_to_tiles(y_ref, _dot(hidden.astype(BF16), wdb[...]))

    @pl.when(t >= nt)
    def _():
        y_ref[...] = jnp.zeros_like(y_ref)


def _experts(tiles, n_tiles, xs, wg, wu, wd):
    n_rows = xs.shape[0] // ROW_TILE
    any_spec = pl.BlockSpec(memory_space=pl.ANY)
    return pl.pallas_call(
        _expert_kernel,
        grid_spec=pltpu.PrefetchScalarGridSpec(
            num_scalar_prefetch=2,
            grid=(n_rows // TM_EXPERT,),
            in_specs=[any_spec, any_spec, any_spec, any_spec],
            out_specs=pl.BlockSpec((TM_EXPERT * ROW_TILE, LANES), lambda t, *_: (t, 0)),
            scratch_shapes=[pltpu.VMEM((X_SLOTS, TM_EXPERT * ROW_TILE, LANES), F32),
                            pltpu.VMEM((2, D_MODEL, D_EXPERT), F32),
                            pltpu.VMEM((2, D_MODEL, D_EXPERT), F32),
                            pltpu.VMEM((2, D_EXPERT, D_MODEL), F32),
                            pltpu.VMEM((D_MODEL, D_EXPERT), BF16),
                            pltpu.VMEM((D_MODEL, D_EXPERT), BF16),
                            pltpu.VMEM((D_EXPERT, D_MODEL), BF16),
                            pltpu.SMEM((4,), jnp.int32),
                            pltpu.SemaphoreType.DMA((2,)),
                            pltpu.SemaphoreType.DMA((X_SLOTS,))]),
        out_shape=jax.ShapeDtypeStruct((n_rows * ROW_TILE, LANES), F32),
        compiler_params=pltpu.CompilerParams(dimension_semantics=("arbitrary",),
                                             vmem_limit_bytes=VMEM_LIMIT),
        name="expert_mlp",
    )(tiles, n_tiles, xs, wg, wu, wd)


def _combine_kernel(dest_ref, h_ref, rw_ref, fg_ref, y_ref, o_ref, buf, sems):
    tm = TM_COMBINE
    i = pl.program_id(0)
    n_steps = pl.num_programs(0)
    n = n_steps * tm
    cur = i % 2

    def fetch(step, half):
        def body(r, c):
            for s in range(2):
                pltpu.make_async_copy(_token_rows(y_ref, dest_ref[s * n + step * tm + r], 1),
                                      _token_rows(buf.at[half, s], r, 1),
                                      sems.at[half]).start(priority=s)
            return c

        lax.fori_loop(0, tm, body, 0, unroll=8)

    @pl.when(i == 0)
    def _():
        fetch(0, 0)

    @pl.when(i + 1 < n_steps)
    def _():
        fetch(i + 1, 1 - cur)

    for s in range(2):
        pltpu.make_async_copy(_token_rows(y_ref, 0, tm), buf.at[cur, s], sems.at[cur]).wait()
    rw = rw_ref[...]
    out = (h_ref[...] + rw[:, 0:1] * _tiles_to_rows(buf.at[cur, 0], tm)
           + rw[:, 1:2] * _tiles_to_rows(buf.at[cur, 1], tm))
    o_ref[...] = _rms(out, fg_ref[...])


def _combine(dest, h, rw, final_g, ys):
    n = h.shape[0]
    return pl.pallas_call(
        _combine_kernel,
        grid_spec=pltpu.PrefetchScalarGridSpec(
            num_scalar_prefetch=1,
            grid=(n // TM_COMBINE,),
            in_specs=[pl.BlockSpec((TM_COMBINE, D_MODEL), lambda i, d: (i, 0)),
                      pl.BlockSpec((TM_COMBINE, LANES), lambda i, d: (i, 0)),
                      pl.BlockSpec((1, D_MODEL), lambda i, d: (0, 0)),
                      pl.BlockSpec(memory_space=pl.ANY)],
            out_specs=pl.BlockSpec((TM_COMBINE, D_MODEL), lambda i, d: (i, 0)),
            scratch_shapes=[pltpu.VMEM((2, 2, TM_COMBINE * ROW_TILE, LANES), F32),
                            pltpu.SemaphoreType.DMA((2,))]),
        out_shape=jax.ShapeDtypeStruct((n, D_MODEL), F32),
        compiler_params=pltpu.CompilerParams(dimension_semantics=("arbitrary",),
                                             vmem_limit_bytes=VMEM_LIMIT),
        name="combine",
    )(dest, h, rw, final_g, ys)


def _schedule(counts):
    tiles = (counts + TM_EXPERT - 1) // TM_EXPERT
    tile_end = jnp.cumsum(tiles)
    offsets = (tile_end - tiles) * TM_EXPERT
    return tiles, offsets, tile_end[-1:]


def _layer(x, attn_g, w_in, sg_g, w_sp, b_sp, sb_g, sg_out_g, w_out, ffn_g,
           w_rg, b_rg, w_re, b_re, w_gate, w_up, w_down):
    batch, seq, _ = x.shape
    n = batch * seq
    x2 = x.reshape(n, D_MODEL)
    row = lambda v: v.reshape(1, -1)

    bsp_full = jnp.repeat(b_sp.T, HEAD_DIM, axis=1)
    qkv, sgn = _inproj(x2, row(attn_g), w_in.astype(BF16), row(sg_g), w_sp, bsp_full, row(sg_out_g))
    sb = _attention(qkv, batch, seq).reshape(n, SB_WIDTH)

    pad_lanes = lambda v, width: jnp.pad(v, [(0, 0)] * (v.ndim - 1) + [(0, width - v.shape[-1])])
    w_r = jnp.concatenate([pad_lanes(w_rg, ROUTER_LANE0),
                           jnp.transpose(w_re, (1, 0, 2)).reshape(D_MODEL, N_EXPERTS)], axis=1)
    w_r = pad_lanes(w_r, LANES)
    wr_hi = w_r.astype(BF16)
    wr_lo = (w_r - wr_hi.astype(F32)).astype(BF16)
    wr2 = jnp.concatenate([wr_hi, wr_lo], axis=1)
    b_r = pad_lanes(jnp.concatenate([pad_lanes(b_rg, ROUTER_LANE0), b_re.reshape(-1)]), LANES)

    h, lg = _mix(sb, sgn, x2, row(sb_g), w_out.astype(BF16), row(ffn_g), wr2, row(b_r))
    ri, rw, cnt = _route(lg)

    counts = cnt[:, 0].astype(jnp.int32)
    n_rows = 2 * n + N_EXPERTS * TM_EXPERT
    tiles, offsets, n_tiles = _schedule(counts)
    expert, rank = ri[0:2], ri[2:4]
    is_e = expert[None] == jnp.arange(N_EXPERTS, dtype=jnp.int32)[:, None, None]
    dest = (jnp.sum(jnp.where(is_e, offsets[:, None, None], 0), axis=0) + rank).reshape(-1)
    pad_start = offsets + counts
    pad_count = (-counts) % TM_EXPERT

    xs = _dispatch(dest, pad_start, pad_count, n_tiles, h, row(ffn_g), n_rows)
    ys = _experts(tiles, n_tiles, xs,
                  w_gate.reshape(N_EXPERTS, D_MODEL, D_EXPERT),
                  w_up.reshape(N_EXPERTS, D_MODEL, D_EXPERT),
                  w_down.reshape(N_EXPERTS, D_EXPERT, D_MODEL))
    return dest, h, rw, ys


def kernel(x, attn_norm_g, w_in, sg_norm_g, w_spatial, b_spatial, sb_out_norm_g, sg_out_norm_g,
           w_out, ffn_norm_g, w_router_group, b_router_group, w_router_expert, b_router_expert,
           w_gate, w_up, w_down, final_norm_g):
    assert attn_norm_g.shape[0] == 1, "single-layer problem"
    batch, seq, _ = x.shape
    dest, h, rw, ys = _layer(x, attn_norm_g[0], w_in[0], sg_norm_g[0], w_spatial[0], b_spatial[0],
                             sb_out_norm_g[0], sg_out_norm_g[0], w_out[0], ffn_norm_g[0],
                             w_router_group[0], b_router_group[0], w_router_expert[0],
                             b_router_expert[0], w_gate[0], w_up[0], w_down[0])
    out = _combine(dest, h, rw, final_norm_g.reshape(1, -1), ys)
    return out.reshape(batch, seq, D_MODEL)
```

```python
import functools
import math

import jax
import jax.numpy as jnp
from jax import lax
from jax.experimental import pallas as pl
from jax.experimental.pallas import tpu as pltpu

D_MODEL = 1024
HEAD_DIM = 64
SB_WIDTH = 512
SG_WIDTH = 512
SG_HEADS = 8
D_IN = 3 * SB_WIDTH + 2 * SG_WIDTH
CHUNK = 128
N_GROUPS = 4
EXPERTS_PER_GROUP = 8
N_EXPERTS = N_GROUPS * EXPERTS_PER_GROUP
D_EXPERT = 512
EPS = 1e-6
F32_EXP_UNDERFLOW = 110.0

LANES = 128
SUBLANES = 8
ROW_TILE = D_MODEL // LANES
assert ROW_TILE == SUBLANES
HEAD_PAIR = 2 * HEAD_DIM
ROUTER_LANE0 = SUBLANES
ROUTER_ROWS = ROUTER_LANE0 + N_EXPERTS
assert EXPERTS_PER_GROUP == SUBLANES and N_GROUPS <= ROUTER_LANE0

TM_PROJ = 512
TQ_ATTN = 256
ATTN_TOP_ROWS = 160
TM_MIX = 1024
TM_ROUTE = 1024
TM_DISPATCH = 1024
TM_EXPERT = 512
TM_COMBINE = 512
VMEM_LIMIT = 48 * 1024 * 1024

F32 = jnp.float32
BF16 = jnp.bfloat16


def _rms(x, g):
    return x * lax.rsqrt(jnp.mean(x * x, axis=-1, keepdims=True) + EPS) * g


def _gelu(x):
    c = math.sqrt(2.0 / math.pi)
    return x * (0.5 * (1.0 + jnp.tanh(c * (x + 0.044715 * (x * x * x)))))


def _softplus(z):
    return jnp.maximum(z, 0.0) + jnp.log(1.0 + jnp.exp(-jnp.abs(z)))


def _dot(a, b):
    return jnp.dot(a, b, preferred_element_type=F32)


def _rows_to_tiles(ref, x):
    m = x.shape[0]
    for k in range(ROW_TILE):
        ref[pl.ds(k, m, stride=ROW_TILE), :] = x[:, k * LANES:(k + 1) * LANES]


def _tiles_to_rows(ref, m):
    return jnp.concatenate([ref[pl.ds(k, m, stride=ROW_TILE), :] for k in range(ROW_TILE)], axis=1)


def _token_rows(ref, first_token, n_tokens):
    return ref.at[pl.ds(pl.multiple_of(first_token * ROW_TILE, ROW_TILE), n_tokens * ROW_TILE)]


def _split_bf16(x):
    hi = x.astype(BF16)
    lo = (x - hi.astype(F32)).astype(BF16)
    return hi, lo


def _inproj_kernel(x_ref, g_ref, w_ref, sgg_ref, wsp_ref, bsp_ref, sgog_ref, qkv_ref, sgn_ref,
                   gu_ref, vgn_ref, sg_ref):
    tm = TM_PROJ
    hb = _rms(x_ref[...], g_ref[...]).astype(BF16)
    gv = _gelu(_dot(hb, w_ref[:, 3 * SB_WIDTH + SG_WIDTH:D_IN]))
    vgn_ref[...] = _rms(gv, sgg_ref[...]).astype(BF16)
    gu_ref[...] = _gelu(_dot(hb, w_ref[:, 3 * SB_WIDTH:3 * SB_WIDTH + SG_WIDTH]))
    q = _dot(hb, w_ref[:, 0:SB_WIDTH]) * (1.0 / math.sqrt(HEAD_DIM))
    qkv_ref[:, 0:SB_WIDTH] = q.astype(BF16)
    qkv_ref[:, SB_WIDTH:2 * SB_WIDTH] = _dot(hb, w_ref[:, SB_WIDTH:2 * SB_WIDTH]).astype(BF16)

    lane = lax.broadcasted_iota(jnp.int32, (1, LANES), 1)
    first = lane < HEAD_DIM
    zero = jnp.zeros((), BF16)
    r_c = lax.broadcasted_iota(jnp.int32, (CHUNK, CHUNK), 0)
    c_c = lax.broadcasted_iota(jnp.int32, (CHUNK, CHUNK), 1)
    tril = r_c >= c_c
    n_pairs = SG_WIDTH // HEAD_PAIR
    w_pairs = []
    for p in range(n_pairs):
        w0 = jnp.where(tril, wsp_ref[2 * p], 0.0).astype(BF16)
        w1 = jnp.where(tril, wsp_ref[2 * p + 1], 0.0).astype(BF16)
        w_pairs.append(jnp.concatenate([w0, w1], axis=1))
    bsp = bsp_ref[...]
    for c in range(tm // CHUNK):
        rows = slice(c * CHUNK, (c + 1) * CHUNK)
        for p in range(n_pairs):
            cols = slice(p * HEAD_PAIR, (p + 1) * HEAD_PAIR)
            vg = vgn_ref[rows, cols]
            rhs = jnp.concatenate([jnp.where(first, vg, zero), jnp.where(first, zero, vg)], axis=0)
            mixed = _dot(w_pairs[p], rhs) + bsp[:, cols]
            sg_ref[rows, cols] = gu_ref[rows, cols] * mixed
    qkv_ref[:, 2 * SB_WIDTH:3 * SB_WIDTH] = _dot(hb, w_ref[:, 2 * SB_WIDTH:3 * SB_WIDTH]).astype(BF16)
    sgn_ref[...] = _rms(sg_ref[...], sgog_ref[...]).astype(BF16)


def _inproj(x2, attn_g, w_in_b, sg_g, wsp, bsp_full, sg_out_g):
    n = x2.shape[0]
    row = lambda i: (i, 0)
    const = lambda i: (0, 0)
    return pl.pallas_call(
        _inproj_kernel,
        grid=(n // TM_PROJ,),
        in_specs=[pl.BlockSpec((TM_PROJ, D_MODEL), row),
                  pl.BlockSpec((1, D_MODEL), const),
                  pl.BlockSpec((D_MODEL, D_IN), const),
                  pl.BlockSpec((1, SG_WIDTH), const),
                  pl.BlockSpec((SG_HEADS, CHUNK, CHUNK), lambda i: (0, 0, 0)),
                  pl.BlockSpec((CHUNK, SG_WIDTH), const),
                  pl.BlockSpec((1, SG_WIDTH), const)],
        out_specs=[pl.BlockSpec((TM_PROJ, 3 * SB_WIDTH), row),
                   pl.BlockSpec((TM_PROJ, SG_WIDTH), row)],
        out_shape=[jax.ShapeDtypeStruct((n, 3 * SB_WIDTH), BF16),
                   jax.ShapeDtypeStruct((n, SG_WIDTH), BF16)],
        scratch_shapes=[pltpu.VMEM((TM_PROJ, SG_WIDTH), F32),
                        pltpu.VMEM((TM_PROJ, SG_WIDTH), BF16),
                        pltpu.VMEM((TM_PROJ, SG_WIDTH), F32)],
        compiler_params=pltpu.CompilerParams(dimension_semantics=("arbitrary",),
                                             vmem_limit_bytes=VMEM_LIMIT),
        name="inproj",
    )(x2, attn_g, w_in_b, sg_g, wsp, bsp_full, sg_out_g)


def _attn_kernel(q_ref, k_ref, v_ref, o_ref, q2_ref, carry_ref):
    t = TQ_ATTN
    n_pairs = SB_WIDTH // HEAD_PAIR
    qi = pl.program_id(1)
    lane = lax.broadcasted_iota(jnp.int32, (1, HEAD_PAIR), 1)
    head_lanes = (lane < HEAD_DIM, lane >= HEAD_DIM)
    zero = jnp.zeros((), BF16)
    for p in range(n_pairs):
        qp = q_ref[0, :, p * HEAD_PAIR:(p + 1) * HEAD_PAIR]
        for h in range(2):
            q2_ref[(2 * p + h) * t:(2 * p + h + 1) * t, :] = jnp.where(head_lanes[h], qp, zero)
    r_idx = lax.broadcasted_iota(jnp.int32, (t, t), 0)
    c_idx = lax.broadcasted_iota(jnp.int32, (t, t), 1)
    suffix = (r_idx > c_idx).astype(BF16)
    suffix2 = jnp.concatenate([suffix, suffix], axis=0)
    causal = c_idx < r_idx

    o_ref[...] = jnp.zeros_like(o_ref)
    carry_ref[...] = jnp.zeros_like(carry_ref)

    def block(j, diag, m):
        start = pl.multiple_of(j * t, t)
        mask2 = jnp.concatenate([causal, causal], axis=0) if diag else None
        st = [dict() for _ in range(n_pairs)]

        def head_rows(p):
            return [slice((2 * p + h) * t, (2 * p + h) * t + m) for h in range(2)]

        def scores(p):
            d = st[p]
            d["cols"] = slice(p * HEAD_PAIR, (p + 1) * HEAD_PAIR)
            kb = k_ref[0, pl.ds(start, t), d["cols"]]
            q2 = jnp.concatenate([q2_ref[r, :] for r in head_rows(p)], axis=0)
            z = lax.dot_general(q2, kb, (((1,), (1,)), ((), ())),
                                preferred_element_type=F32)
            sp = _softplus(z)
            nl = jnp.where(mask2, sp, 0.0) if diag else sp
            hi, lo = _split_bf16(nl)
            d["hl"] = jnp.concatenate([hi, lo], axis=1)
            d["log_beta"] = z - sp
            d["nl0"] = nl[:, 0:1]

        def weights(p):
            d = st[p]
            hl = d["hl"]
            after = jnp.concatenate([_dot(hl[0:m], suffix2), _dot(hl[m:2 * m], suffix2)], axis=0)
            carry = jnp.concatenate([carry_ref[r, :] for r in head_rows(p)], axis=0)
            a = jnp.exp(d["log_beta"] - after - carry)
            if diag:
                a = jnp.where(mask2, a, 0.0)
            a = a.astype(BF16)
            d["a2"] = jnp.concatenate([a[0:m], a[m:2 * m]], axis=1)
            new_carry = carry + after[:, 0:1] + d["nl0"]
            for h, r in enumerate(head_rows(p)):
                carry_ref[r, :] = new_carry[h * m:(h + 1) * m]

        def values(p):
            d = st[p]
            vb = v_ref[0, pl.ds(start, t), d["cols"]]
            v2 = jnp.concatenate([jnp.where(head_lanes[0], vb, zero),
                                  jnp.where(head_lanes[1], vb, zero)], axis=0)
            o_ref[0, 0:m, d["cols"]] += _dot(d["a2"], v2)

        for step in range(n_pairs + 2):
            if step < n_pairs:
                scores(step)
            if 0 <= step - 1 < n_pairs:
                weights(step - 1)
            if 0 <= step - 2 < n_pairs:
                values(step - 2)

    top = ATTN_TOP_ROWS

    def flags():
        bottom = jnp.concatenate([carry_ref[hh * t + top:(hh + 1) * t, :] for hh in range(2 * n_pairs)], axis=0)
        return (jnp.min(carry_ref[...]) < F32_EXP_UNDERFLOW, jnp.min(bottom) >= F32_EXP_UNDERFLOW)

    block(qi, True, t)

    def body(state):
        it, _, bottom_done = state
        j = qi - 1 - it

        @pl.when(bottom_done)
        def _():
            block(j, False, top)

        @pl.when(jnp.logical_not(bottom_done))
        def _():
            block(j, False, t)

        return (it + 1,) + flags()

    lax.while_loop(lambda s: (s[0] < qi) & s[1], body, (jnp.int32(0),) + flags())


def _attention(qkv, batch, seq):
    qkv3 = qkv.reshape(batch, seq, 3 * SB_WIDTH)
    n_heads = SB_WIDTH // HEAD_DIM
    return pl.pallas_call(
        _attn_kernel,
        grid=(batch, seq // TQ_ATTN),
        in_specs=[pl.BlockSpec((1, TQ_ATTN, SB_WIDTH), lambda b, i: (b, i, 0)),
                  pl.BlockSpec((1, seq, SB_WIDTH), lambda b, i: (b, 0, 1)),
                  pl.BlockSpec((1, seq, SB_WIDTH), lambda b, i: (b, 0, 2))],
        out_specs=pl.BlockSpec((1, TQ_ATTN, SB_WIDTH), lambda b, i: (b, i, 0)),
        out_shape=jax.ShapeDtypeStruct((batch, seq, SB_WIDTH), F32),
        scratch_shapes=[pltpu.VMEM((n_heads * TQ_ATTN, HEAD_PAIR), BF16),
                        pltpu.VMEM((n_heads * TQ_ATTN, 1), F32)],
        compiler_params=pltpu.CompilerParams(dimension_semantics=("arbitrary",) * 2,
                                             vmem_limit_bytes=VMEM_LIMIT),
        name="sb_attention",
    )(qkv3, qkv3, qkv3)


def _mix_kernel(sb_ref, sgn_ref, x_ref, sbg_ref, wout_ref, ffng_ref, wr2_ref, br_ref,
                h_ref, lg_ref):
    sbn = _rms(sb_ref[...], sbg_ref[...]).astype(BF16)
    h = x_ref[...] + _dot(sbn, wout_ref[0:SB_WIDTH, :]) + _dot(sgn_ref[...], wout_ref[SB_WIDTH:, :])
    h_ref[...] = h
    hn = _rms(h, ffng_ref[...])

    hn_hi, hn_lo = _split_bf16(hn)
    both = _dot(hn_hi, wr2_ref[...])
    logits = both[:, 0:LANES] + both[:, LANES:] + _dot(hn_lo, wr2_ref[:, 0:LANES]) + br_ref[...]
    lg_ref[...] = logits.T[0:ROUTER_ROWS, :]


def _route_kernel(lg_ref, ri_ref, rw_ref, cnt_ref, count_ref):
    tr = TM_ROUTE
    i = pl.program_id(0)

    @pl.when(i == 0)
    def _():
        count_ref[...] = jnp.zeros_like(count_ref)

    neg = jnp.float32(-jnp.inf)
    row8 = lax.broadcasted_iota(jnp.int32, (SUBLANES, tr), 0)

    def top(v):
        m = jnp.max(v, axis=0, keepdims=True)
        return m, jnp.min(jnp.where(v == m, row8, SUBLANES), axis=0, keepdims=True)

    def group_rows(g):
        return lg_ref[ROUTER_LANE0 + g * EXPERTS_PER_GROUP:ROUTER_LANE0 + (g + 1) * EXPERTS_PER_GROUP, :]

    gl = jnp.where(row8 < N_GROUPS, lg_ref[0:SUBLANES, :], neg)
    gmax, gidx = top(gl)
    gweight = 1.0 / jnp.sum(jnp.exp(gl - gmax), axis=0, keepdims=True)
    el = group_rows(0)
    for g in range(1, N_GROUPS):
        el = jnp.where(gidx == g, group_rows(g), el)
    m1, i1 = top(el)
    m2, i2 = top(jnp.where(row8 == i1, neg, el))
    t21 = jnp.exp(m2 - m1)
    w1 = gweight / (1.0 + t21)
    w2 = gweight * t21 / (1.0 + t21)
    e1 = gidx * EXPERTS_PER_GROUP + i1
    e2 = gidx * EXPERTS_PER_GROUP + i2

    row_e = lax.broadcasted_iota(jnp.int32, (N_EXPERTS, tr), 0)
    sel1 = row_e == e1
    sel2 = row_e == e2
    onehot = jnp.where(sel1 | sel2, 1.0, 0.0)
    r_t = lax.broadcasted_iota(jnp.int32, (tr, tr), 0)
    c_t = lax.broadcasted_iota(jnp.int32, (tr, tr), 1)
    before = (r_t < c_t).astype(BF16)
    running = count_ref[:, 0:1] + _dot(onehot.astype(BF16), before)
    rank1 = jnp.sum(jnp.where(sel1, running, 0.0), axis=0, keepdims=True)
    rank2 = jnp.sum(jnp.where(sel2, running, 0.0), axis=0, keepdims=True)
    new_count = count_ref[:, 0:1] + jnp.sum(onehot, axis=1, keepdims=True)
    count_ref[...] = jnp.broadcast_to(new_count, count_ref.shape)
    cnt_ref[...] = jnp.broadcast_to(new_count, cnt_ref.shape)

    ri_ref[...] = jnp.where(row8 == 0, e1, jnp.where(row8 == 1, e2, jnp.where(
        row8 == 2, rank1.astype(jnp.int32), jnp.where(row8 == 3, rank2.astype(jnp.int32), 0))))
    row128 = lax.broadcasted_iota(jnp.int32, (LANES, tr), 0)
    rw_ref[...] = jnp.where(row128 == 0, w1, jnp.where(row128 == 1, w2, 0.0)).T


def _route(lg):
    n = lg.shape[1]
    return pl.pallas_call(
        _route_kernel,
        grid=(n // TM_ROUTE,),
        in_specs=[pl.BlockSpec((ROUTER_ROWS, TM_ROUTE), lambda i: (0, i))],
        out_specs=[pl.BlockSpec((SUBLANES, TM_ROUTE), lambda i: (0, i)),
                   pl.BlockSpec((TM_ROUTE, LANES), lambda i: (i, 0)),
                   pl.BlockSpec((N_EXPERTS, LANES), lambda i: (0, 0))],
        out_shape=[jax.ShapeDtypeStruct((SUBLANES, n), jnp.int32),
                   jax.ShapeDtypeStruct((n, LANES), F32),
                   jax.ShapeDtypeStruct((N_EXPERTS, LANES), F32)],
        scratch_shapes=[pltpu.VMEM((N_EXPERTS, LANES), F32)],
        compiler_params=pltpu.CompilerParams(dimension_semantics=("arbitrary",),
                                             vmem_limit_bytes=VMEM_LIMIT),
        name="route",
    )(lg)


def _mix(sb, sgn, x2, sb_g, w_out_b, ffn_g, wr2, br):
    n = x2.shape[0]
    row = lambda i: (i, 0)
    const = lambda i: (0, 0)
    return pl.pallas_call(
        _mix_kernel,
        grid=(n // TM_MIX,),
        in_specs=[pl.BlockSpec((TM_MIX, SB_WIDTH), row),
                  pl.BlockSpec((TM_MIX, SG_WIDTH), row),
                  pl.BlockSpec((TM_MIX, D_MODEL), row),
                  pl.BlockSpec((1, SB_WIDTH), const),
                  pl.BlockSpec((D_MODEL, D_MODEL), const),
                  pl.BlockSpec((1, D_MODEL), const),
                  pl.BlockSpec((D_MODEL, 2 * LANES), const),
                  pl.BlockSpec((1, LANES), const)],
        out_specs=[pl.BlockSpec((TM_MIX, D_MODEL), row),
                   pl.BlockSpec((ROUTER_ROWS, TM_MIX), lambda i: (0, i))],
        out_shape=[jax.ShapeDtypeStruct((n, D_MODEL), F32),
                   jax.ShapeDtypeStruct((ROUTER_ROWS, n), F32)],
        compiler_params=pltpu.CompilerParams(dimension_semantics=("arbitrary",),
                                             vmem_limit_bytes=VMEM_LIMIT),
        name="mix_router",
    )(sb, sgn, x2, sb_g, w_out_b, ffn_g, wr2, br)


_PAD_BITS = tuple(1 << b for b in reversed(range(TM_EXPERT.bit_length() - 1)))


def _dispatch_kernel(dest_ref, pad_start_ref, pad_count_ref, nt_ref, h_ref, g_ref, zeros_ref, xs_ref,
                     hn_ref, sem, zsem):
    tm = TM_DISPATCH
    i = pl.program_id(0)
    n_steps = pl.num_programs(0) - 1
    n = n_steps * tm
    base = (i - 1) * tm
    prev = hn_ref.at[lax.rem(i + 1, 2)]
    n_tiles_max = xs_ref.shape[0] // (TM_EXPERT * ROW_TILE)

    def pad_copies(do):
        for e in range(N_EXPERTS):
            start = pad_start_ref[e]
            count = pad_count_ref[e]
            for bit in _PAD_BITS:
                @pl.when((count & bit) != 0)
                def _(start=start, bit=bit):
                    do(pltpu.make_async_copy(_token_rows(zeros_ref, 0, bit),
                                             _token_rows(xs_ref, start, bit), zsem))
                start = start + (count & bit)
        for k in range(N_EXPERTS):
            tile = nt_ref[0] + k

            @pl.when(tile < n_tiles_max)
            def _(tile=tile):
                do(pltpu.make_async_copy(zeros_ref, _token_rows(xs_ref, tile * TM_EXPERT, TM_EXPERT), zsem))

    @pl.when(i == 0)
    def _():
        pad_copies(lambda cp: cp.start())

    @pl.when(i > 0)
    def _():
        def body(r, c):
            src = _token_rows(prev, r, 1)
            for s in range(2):
                pltpu.make_async_copy(src, _token_rows(xs_ref, dest_ref[s * n + base + r], 1),
                                      sem).start(priority=s)
            return c

        lax.fori_loop(0, tm, body, 0, unroll=8)

    @pl.when(i < n_steps)
    def _():
        _rows_to_tiles(hn_ref.at[lax.rem(i, 2)], _rms(h_ref[...], g_ref[...]))

    @pl.when(i > 0)
    def _():
        for _ in range(2):
            pltpu.make_async_copy(prev, _token_rows(xs_ref, 0, tm), sem).wait()

    @pl.when(i == n_steps)
    def _():
        pad_copies(lambda cp: cp.wait())


def _dispatch(dest, pad_start, pad_count, n_tiles, h, ffn_g, n_rows):
    n_steps = h.shape[0] // TM_DISPATCH
    zeros = jnp.zeros((TM_EXPERT * ROW_TILE, LANES), F32)
    return pl.pallas_call(
        _dispatch_kernel,
        grid_spec=pltpu.PrefetchScalarGridSpec(
            num_scalar_prefetch=4,
            grid=(n_steps + 1,),
            in_specs=[pl.BlockSpec((TM_DISPATCH, D_MODEL), lambda i, *_: (jnp.minimum(i, n_steps - 1), 0)),
                      pl.BlockSpec((1, D_MODEL), lambda i, *_: (0, 0)),
                      pl.BlockSpec(memory_space=pl.ANY)],
            out_specs=pl.BlockSpec(memory_space=pl.ANY),
            scratch_shapes=[pltpu.VMEM((2, TM_DISPATCH * ROW_TILE, LANES), F32),
                            pltpu.SemaphoreType.DMA, pltpu.SemaphoreType.DMA]),
        out_shape=jax.ShapeDtypeStruct((n_rows * ROW_TILE, LANES), F32),
        compiler_params=pltpu.CompilerParams(dimension_semantics=("arbitrary",),
                                             vmem_limit_bytes=VMEM_LIMIT),
        name="dispatch",
    )(dest, pad_start, pad_count, n_tiles, h, ffn_g, zeros)


X_SLOTS = 3


def _expert_kernel(tiles_ref, nt_ref, xs_ref, wg_ref, wu_ref, wd_ref, y_ref,
                   x_buf, sg_buf, su_buf, sd_buf, wgb, wub, wdb, state, w_sems, x_sems):
    tm = TM_EXPERT
    t = pl.program_id(0)
    nt = nt_ref[0]

    def x_copy(tile):
        slot = lax.rem(tile, X_SLOTS)
        return pltpu.make_async_copy(_token_rows(xs_ref, tile * tm, tm), x_buf.at[slot], x_sems.at[slot])

    def weight_copies(e, slot):
        return (pltpu.make_async_copy(wg_ref.at[e], sg_buf.at[slot], w_sems.at[slot]),
                pltpu.make_async_copy(wu_ref.at[e], su_buf.at[slot], w_sems.at[slot]),
                pltpu.make_async_copy(wd_ref.at[e], sd_buf.at[slot], w_sems.at[slot]))

    def next_with_rows(e):
        return lax.while_loop(lambda k: (k < N_EXPERTS) & (tiles_ref[jnp.minimum(k, N_EXPERTS - 1)] == 0),
                              lambda k: k + 1, e + 1)

    @pl.when(t == 0)
    def _():
        first = next_with_rows(jnp.int32(-1))
        state[0] = jnp.int32(-1)
        state[1] = jnp.int32(0)
        state[2] = jnp.int32(1)
        state[3] = first
        for cp in weight_copies(first, 0):
            cp.start()
        x_copy(0).start()

        @pl.when(nt > 1)
        def _():
            x_copy(1).start()

    @pl.when(t + 2 < nt)
    def _():
        x_copy(t + 2).start()

    @pl.when(t < nt)
    def _():
        @pl.when(state[1] == 0)
        def _():
            e = state[3]
            slot = 1 - state[2]
            nxt = next_with_rows(e)
            state[0] = e
            state[1] = tiles_ref[e]
            state[2] = slot
            state[3] = nxt
            for cp in weight_copies(e, slot):
                cp.wait()

            @pl.when(nxt < N_EXPERTS)
            def _():
                for cp in weight_copies(nxt, 1 - slot):
                    cp.start()

            wgb[...] = sg_buf[slot].astype(BF16)
            wub[...] = su_buf[slot].astype(BF16)
            wdb[...] = sd_buf[slot].astype(BF16)

        state[1] = state[1] - 1
        x_copy(t).wait()
        x = _tiles_to_rows(x_buf.at[lax.rem(t, X_SLOTS)], tm).astype(BF16)
        g = _dot(x, wgb[...])
        u = _dot(x, wub[...])
        hidden = (g * jax.nn.sigmoid(g)) * u
        _rows_to_tiles(y_ref, _dot(hidden.astype(BF16), wdb[...]))

    @pl.when(t >= nt)
    def _():
        y_ref[...] = jnp.zeros_like(y_ref)


def _experts(tiles, n_tiles, xs, wg, wu, wd):
    n_rows = xs.shape[0] // ROW_TILE
    any_spec = pl.BlockSpec(memory_space=pl.ANY)
    return pl.pallas_call(
        _expert_kernel,
        grid_spec=pltpu.PrefetchScalarGridSpec(
            num_scalar_prefetch=2,
            grid=(n_rows // TM_EXPERT,),
            in_specs=[any_spec, any_spec, any_spec, any_spec],
            out_specs=pl.BlockSpec((TM_EXPERT * ROW_TILE, LANES), lambda t, *_: (t, 0)),
            scratch_shapes=[pltpu.VMEM((X_SLOTS, TM_EXPERT * ROW_TILE, LANES), F32),
                            pltpu.VMEM((2, D_MODEL, D_EXPERT), F32),
                            pltpu.VMEM((2, D_MODEL, D_EXPERT), F32),
                            pltpu.VMEM((2, D_EXPERT, D_MODEL), F32),
                            pltpu.VMEM((D_MODEL, D_EXPERT), BF16),
                            pltpu.VMEM((D_MODEL, D_EXPERT), BF16),
                            pltpu.VMEM((D_EXPERT, D_MODEL), BF16),
                            pltpu.SMEM((4,), jnp.int32),
                            pltpu.SemaphoreType.DMA((2,)),
                            pltpu.SemaphoreType.DMA((X_SLOTS,))]),
        out_shape=jax.ShapeDtypeStruct((n_rows * ROW_TILE, LANES), F32),
        compiler_params=pltpu.CompilerParams(dimension_semantics=("arbitrary",),
                                             vmem_limit_bytes=VMEM_LIMIT),
        name="expert_mlp",
    )(tiles, n_tiles, xs, wg, wu, wd)


def _combine_kernel(dest_ref, h_ref, rw_ref, fg_ref, y_ref, o_ref, buf, sems):
    tm = TM_COMBINE
    i = pl.program_id(0)
    n_steps = pl.num_programs(0)
    n = n_steps * tm
    cur = i % 2

    def fetch(step, half):
        def body(r, c):
            for s in range(2):
                pltpu.make_async_copy(_token_rows(y_ref, dest_ref[s * n + step * tm + r], 1),
                                      _token_rows(buf.at[half, s], r, 1),
                                      sems.at[half]).start(priority=s)
            return c

        lax.fori_loop(0, tm, body, 0, unroll=8)

    @pl.when(i == 0)
    def _():
        fetch(0, 0)

    @pl.when(i + 1 < n_steps)
    def _():
        fetch(i + 1, 1 - cur)

    for s in range(2):
        pltpu.make_async_copy(_token_rows(y_ref, 0, tm), buf.at[cur, s], sems.at[cur]).wait()
    rw = rw_ref[...]
    out = (h_ref[...] + rw[:, 0:1] * _tiles_to_rows(buf.at[cur, 0], tm)
           + rw[:, 1:2] * _tiles_to_rows(buf.at[cur, 1], tm))
    o_ref[...] = _rms(out, fg_ref[...])


def _combine(dest, h, rw, final_g, ys):
    n = h.shape[0]
    return pl.pallas_call(
        _combine_kernel,
        grid_spec=pltpu.PrefetchScalarGridSpec(
            num_scalar_prefetch=1,
            grid=(n // TM_COMBINE,),
            in_specs=[pl.BlockSpec((TM_COMBINE, D_MODEL), lambda i, d: (i, 0)),
                      pl.BlockSpec((TM_COMBINE, LANES), lambda i, d: (i, 0)),
                      pl.BlockSpec((1, D_MODEL), lambda i, d: (0, 0)),
                      pl.BlockSpec(memory_space=pl.ANY)],
            out_specs=pl.BlockSpec((TM_COMBINE, D_MODEL), lambda i, d: (i, 0)),
            scratch_shapes=[pltpu.VMEM((2, 2, TM_COMBINE * ROW_TILE, LANES), F32),
                            pltpu.SemaphoreType.DMA((2,))]),
        out_shape=jax.ShapeDtypeStruct((n, D_MODEL), F32),
        compiler_params=pltpu.CompilerParams(dimension_semantics=("arbitrary",),
                                             vmem_limit_bytes=VMEM_LIMIT),
        name="combine",
    )(dest, h, rw, final_g, ys)


def _schedule(counts):
    tiles = (counts + TM_EXPERT - 1) // TM_EXPERT
    tile_end = jnp.cumsum(tiles)
    offsets = (tile_end - tiles) * TM_EXPERT
    return tiles, offsets, tile_end[-1:]


def _layer(x, attn_g, w_in, sg_g, w_sp, b_sp, sb_g, sg_out_g, w_out, ffn_g,
           w_rg, b_rg, w_re, b_re, w_gate, w_up, w_down):
    batch, seq, _ = x.shape
    n = batch * seq
    x2 = x.reshape(n, D_MODEL)
    row = lambda v: v.reshape(1, -1)

    bsp_full = jnp.repeat(b_sp.T, HEAD_DIM, axis=1)
    qkv, sgn = _inproj(x2, row(attn_g), w_in.astype(BF16), row(sg_g), w_sp, bsp_full, row(sg_out_g))
    sb = _attention(qkv, batch, seq).reshape(n, SB_WIDTH)

    pad_lanes = lambda v, width: jnp.pad(v, [(0, 0)] * (v.ndim - 1) + [(0, width - v.shape[-1])])
    w_r = jnp.concatenate([pad_lanes(w_rg, ROUTER_LANE0),
                           jnp.transpose(w_re, (1, 0, 2)).reshape(D_MODEL, N_EXPERTS)], axis=1)
    w_r = pad_lanes(w_r, LANES)
    wr_hi = w_r.astype(BF16)
    wr_lo = (w_r - wr_hi.astype(F32)).astype(BF16)
    wr2 = jnp.concatenate([wr_hi, wr_lo], axis=1)
    b_r = pad_lanes(jnp.concatenate([pad_lanes(b_rg, ROUTER_LANE0), b_re.reshape(-1)]), LANES)

    h, lg = _mix(sb, sgn, x2, row(sb_g), w_out.astype(BF16), row(ffn_g), wr2, row(b_r))
    ri, rw, cnt = _route(lg)

    counts = cnt[:, 0].astype(jnp.int32)
    n_rows = 2 * n + N_EXPERTS * TM_EXPERT
    tiles, offsets, n_tiles = _schedule(counts)
    expert, rank = ri[0:2], ri[2:4]
    is_e = expert[None] == jnp.arange(N_EXPERTS, dtype=jnp.int32)[:, None, None]
    dest = (jnp.sum(jnp.where(is_e, offsets[:, None, None], 0), axis=0) + rank).reshape(-1)
    pad_start = offsets + counts
    pad_count = (-counts) % TM_EXPERT

    xs = _dispatch(dest, pad_start, pad_count, n_tiles, h, row(ffn_g), n_rows)
    ys = _experts(tiles, n_tiles, xs,
                  w_gate.reshape(N_EXPERTS, D_MODEL, D_EXPERT),
                  w_up.reshape(N_EXPERTS, D_MODEL, D_EXPERT),
                  w_down.reshape(N_EXPERTS, D_EXPERT, D_MODEL))
    return dest, h, rw, ys


def kernel(x, attn_norm_g, w_in, sg_norm_g, w_spatial, b_spatial, sb_out_norm_g, sg_out_norm_g,
           w_out, ffn_norm_g, w_router_group, b_router_group, w_router_expert, b_router_expert,
           w_gate, w_up, w_down, final_norm_g):
    assert attn_norm_g.shape[0] == 1, "single-layer problem"
    batch, seq, _ = x.shape
    dest, h, rw, ys = _layer(x, attn_norm_g[0], w_in[0], sg_norm_g[0], w_spatial[0], b_spatial[0],
                             sb_out_norm_g[0], sg_out_norm_g[0], w_out[0], ffn_norm_g[0],
                             w_router_group[0], b_router_group[0], w_router_expert[0],
                             b_router_expert[0], w_gate[0], w_up[0], w_down[0])
    out = _combine(dest, h, rw, final_norm_g.reshape(1, -1), ys)
    return out.reshape(batch, seq, D_MODEL)
```

```python
import functools
import math

import jax
import jax.numpy as jnp
from jax import lax
from jax.experimental import pallas as pl
from jax.experimental.pallas import tpu as pltpu

D_MODEL = 1024
HEAD_DIM = 64
SB_WIDTH = 512
SG_WIDTH = 512
SG_HEADS = 8
D_IN = 3 * SB_WIDTH + 2 * SG_WIDTH
CHUNK = 128
N_GROUPS = 4
EXPERTS_PER_GROUP = 8
N_EXPERTS = N_GROUPS * EXPERTS_PER_GROUP
D_EXPERT = 512
EPS = 1e-6
F32_EXP_UNDERFLOW = 110.0

LANES = 128
SUBLANES = 8
ROW_TILE = D_MODEL // LANES
assert ROW_TILE == SUBLANES
HEAD_PAIR = 2 * HEAD_DIM
ROUTER_LANE0 = SUBLANES
ROUTER_ROWS = ROUTER_LANE0 + N_EXPERTS
assert EXPERTS_PER_GROUP == SUBLANES and N_GROUPS <= ROUTER_LANE0

TM_PROJ = 1024
TQ_ATTN = 256
ATTN_BLOCKS_PER_STEP = 2
ATTN_TOP_ROWS = 160
TM_MIX = 1024
TM_ROUTE = 1024
TM_DISPATCH = 1024
TM_EXPERT = 512
TM_COMBINE = 512
VMEM_LIMIT = 48 * 1024 * 1024

F32 = jnp.float32
BF16 = jnp.bfloat16


def _rms(x, g):
    return x * lax.rsqrt(jnp.mean(x * x, axis=-1, keepdims=True) + EPS) * g


def _gelu(x):
    c = math.sqrt(2.0 / math.pi)
    return x * (0.5 * (1.0 + jnp.tanh(c * (x + 0.044715 * (x * x * x)))))


def _softplus(z):
    return jnp.maximum(z, 0.0) + jnp.log(1.0 + jnp.exp(-jnp.abs(z)))


def _dot(a, b):
    return jnp.dot(a, b, preferred_element_type=F32)


def _rows_to_tiles(ref, x):
    m = x.shape[0]
    for k in range(ROW_TILE):
        ref[pl.ds(k, m, stride=ROW_TILE), :] = x[:, k * LANES:(k + 1) * LANES]


def _tiles_to_rows(ref, m):
    return jnp.concatenate([ref[pl.ds(k, m, stride=ROW_TILE), :] for k in range(ROW_TILE)], axis=1)


def _token_rows(ref, first_token, n_tokens):
    return ref.at[pl.ds(pl.multiple_of(first_token * ROW_TILE, ROW_TILE), n_tokens * ROW_TILE)]


def _split_bf16(x):
    hi = x.astype(BF16)
    lo = (x - hi.astype(F32)).astype(BF16)
    return hi, lo


def _inproj_kernel(x_ref, g_ref, w_ref, sgg_ref, wsp_ref, bsp_ref, sgog_ref, qkv_ref, sgn_ref,
                   gu_ref, vgn_ref, sg_ref):
    tm = TM_PROJ
    hb = _rms(x_ref[...], g_ref[...]).astype(BF16)
    gv = _gelu(_dot(hb, w_ref[:, 3 * SB_WIDTH + SG_WIDTH:D_IN]))
    vgn_ref[...] = _rms(gv, sgg_ref[...]).astype(BF16)
    gu_ref[...] = _gelu(_dot(hb, w_ref[:, 3 * SB_WIDTH:3 * SB_WIDTH + SG_WIDTH]))
    q = _dot(hb, w_ref[:, 0:SB_WIDTH]) * (1.0 / math.sqrt(HEAD_DIM))
    qkv_ref[:, 0:SB_WIDTH] = q.astype(BF16)
    qkv_ref[:, SB_WIDTH:2 * SB_WIDTH] = _dot(hb, w_ref[:, SB_WIDTH:2 * SB_WIDTH]).astype(BF16)

    lane = lax.broadcasted_iota(jnp.int32, (1, LANES), 1)
    first = lane < HEAD_DIM
    zero = jnp.zeros((), BF16)
    r_c = lax.broadcasted_iota(jnp.int32, (CHUNK, CHUNK), 0)
    c_c = lax.broadcasted_iota(jnp.int32, (CHUNK, CHUNK), 1)
    tril = r_c >= c_c
    n_pairs = SG_WIDTH // HEAD_PAIR
    w_pairs = []
    for p in range(n_pairs):
        w0 = jnp.where(tril, wsp_ref[2 * p], 0.0).astype(BF16)
        w1 = jnp.where(tril, wsp_ref[2 * p + 1], 0.0).astype(BF16)
        w_pairs.append(jnp.concatenate([w0, w1], axis=1))
    bsp = bsp_ref[...]
    for c in range(tm // CHUNK):
        rows = slice(c * CHUNK, (c + 1) * CHUNK)
        for p in range(n_pairs):
            cols = slice(p * HEAD_PAIR, (p + 1) * HEAD_PAIR)
            vg = vgn_ref[rows, cols]
            rhs = jnp.concatenate([jnp.where(first, vg, zero), jnp.where(first, zero, vg)], axis=0)
            mixed = _dot(w_pairs[p], rhs) + bsp[:, cols]
            sg_ref[rows, cols] = gu_ref[rows, cols] * mixed
    qkv_ref[:, 2 * SB_WIDTH:3 * SB_WIDTH] = _dot(hb, w_ref[:, 2 * SB_WIDTH:3 * SB_WIDTH]).astype(BF16)
    sgn_ref[...] = _rms(sg_ref[...], sgog_ref[...]).astype(BF16)


def _inproj(x2, attn_g, w_in_b, sg_g, wsp, bsp_full, sg_out_g):
    n = x2.shape[0]
    row = lambda i: (i, 0)
    const = lambda i: (0, 0)
    return pl.pallas_call(
        _inproj_kernel,
        grid=(n // TM_PROJ,),
        in_specs=[pl.BlockSpec((TM_PROJ, D_MODEL), row),
                  pl.BlockSpec((1, D_MODEL), const),
                  pl.BlockSpec((D_MODEL, D_IN), const),
                  pl.BlockSpec((1, SG_WIDTH), const),
                  pl.BlockSpec((SG_HEADS, CHUNK, CHUNK), lambda i: (0, 0, 0)),
                  pl.BlockSpec((CHUNK, SG_WIDTH), const),
                  pl.BlockSpec((1, SG_WIDTH), const)],
        out_specs=[pl.BlockSpec((TM_PROJ, 3 * SB_WIDTH), row),
                   pl.BlockSpec((TM_PROJ, SG_WIDTH), row)],
        out_shape=[jax.ShapeDtypeStruct((n, 3 * SB_WIDTH), BF16),
                   jax.ShapeDtypeStruct((n, SG_WIDTH), BF16)],
        scratch_shapes=[pltpu.VMEM((TM_PROJ, SG_WIDTH), F32),
                        pltpu.VMEM((TM_PROJ, SG_WIDTH), BF16),
                        pltpu.VMEM((TM_PROJ, SG_WIDTH), F32)],
        compiler_params=pltpu.CompilerParams(dimension_semantics=("arbitrary",),
                                             vmem_limit_bytes=VMEM_LIMIT),
        name="inproj",
    )(x2, attn_g, w_in_b, sg_g, wsp, bsp_full, sg_out_g)


def _attn_kernel(q_ref, k_ref, v_ref, o_ref, q2_ref, carry_ref):
    t = TQ_ATTN
    n_pairs = SB_WIDTH // HEAD_PAIR
    lane = lax.broadcasted_iota(jnp.int32, (1, HEAD_PAIR), 1)
    head_lanes = (lane < HEAD_DIM, lane >= HEAD_DIM)
    zero = jnp.zeros((), BF16)
    r_idx = lax.broadcasted_iota(jnp.int32, (t, t), 0)
    c_idx = lax.broadcasted_iota(jnp.int32, (t, t), 1)
    suffix = (r_idx > c_idx).astype(BF16)
    suffix2 = jnp.concatenate([suffix, suffix], axis=0)
    causal = c_idx < r_idx

    def one_query_block(sub, c):
        qi = pl.program_id(1) * ATTN_BLOCKS_PER_STEP + sub
        row0 = pl.multiple_of(sub * t, t)
        for p in range(n_pairs):
            qp = q_ref[0, pl.ds(row0, t), p * HEAD_PAIR:(p + 1) * HEAD_PAIR]
            for h in range(2):
                q2_ref[(2 * p + h) * t:(2 * p + h + 1) * t, :] = jnp.where(head_lanes[h], qp, zero)
        o_ref[0, pl.ds(row0, t), :] = jnp.zeros((t, SB_WIDTH), F32)
        carry_ref[...] = jnp.zeros_like(carry_ref)

        def block(j, diag, m):
            start = pl.multiple_of(j * t, t)
            mask2 = jnp.concatenate([causal, causal], axis=0) if diag else None
            st = [dict() for _ in range(n_pairs)]

            def head_rows(p):
                return [slice((2 * p + h) * t, (2 * p + h) * t + m) for h in range(2)]

            def scores(p):
                d = st[p]
                d["cols"] = slice(p * HEAD_PAIR, (p + 1) * HEAD_PAIR)
                kb = k_ref[0, pl.ds(start, t), d["cols"]]
                q2 = jnp.concatenate([q2_ref[r, :] for r in head_rows(p)], axis=0)
                z = lax.dot_general(q2, kb, (((1,), (1,)), ((), ())),
                                    preferred_element_type=F32)
                sp = _softplus(z)
                nl = jnp.where(mask2, sp, 0.0) if diag else sp
                hi, lo = _split_bf16(nl)
                d["hl"] = jnp.concatenate([hi, lo], axis=1)
                d["log_beta"] = z - sp
                d["nl0"] = nl[:, 0:1]

            def weights(p):
                d = st[p]
                hl = d["hl"]
                after = jnp.concatenate([_dot(hl[0:m], suffix2), _dot(hl[m:2 * m], suffix2)], axis=0)
                carry = jnp.concatenate([carry_ref[r, :] for r in head_rows(p)], axis=0)
                a = jnp.exp(d["log_beta"] - after - carry)
                if diag:
                    a = jnp.where(mask2, a, 0.0)
                a = a.astype(BF16)
                d["a2"] = jnp.concatenate([a[0:m], a[m:2 * m]], axis=1)
                new_carry = carry + after[:, 0:1] + d["nl0"]
                for h, r in enumerate(head_rows(p)):
                    carry_ref[r, :] = new_carry[h * m:(h + 1) * m]

            def values(p):
                d = st[p]
                vb = v_ref[0, pl.ds(start, t), d["cols"]]
                v2 = jnp.concatenate([jnp.where(head_lanes[0], vb, zero),
                                      jnp.where(head_lanes[1], vb, zero)], axis=0)
                o_ref[0, pl.ds(row0, m), d["cols"]] += _dot(d["a2"], v2)

            for step in range(n_pairs + 2):
                if step < n_pairs:
                    scores(step)
                if 0 <= step - 1 < n_pairs:
                    weights(step - 1)
                if 0 <= step - 2 < n_pairs:
                    values(step - 2)

        top = ATTN_TOP_ROWS

        def flags():
            bottom = jnp.concatenate([carry_ref[hh * t + top:(hh + 1) * t, :] for hh in range(2 * n_pairs)], axis=0)
            return (jnp.min(carry_ref[...]) < F32_EXP_UNDERFLOW, jnp.min(bottom) >= F32_EXP_UNDERFLOW)

        block(qi, True, t)

        def body(state):
            it, _, bottom_done = state
            j = qi - 1 - it

            @pl.when(bottom_done)
            def _():
                block(j, False, top)

            @pl.when(jnp.logical_not(bottom_done))
            def _():
                block(j, False, t)

            return (it + 1,) + flags()

        lax.while_loop(lambda s: (s[0] < qi) & s[1], body, (jnp.int32(0),) + flags())
        return c

    lax.fori_loop(0, ATTN_BLOCKS_PER_STEP, one_query_block, 0)


def _attention(qkv, batch, seq):
    qkv3 = qkv.reshape(batch, seq, 3 * SB_WIDTH)
    n_heads = SB_WIDTH // HEAD_DIM
    return pl.pallas_call(
        _attn_kernel,
        grid=(batch, seq // (ATTN_BLOCKS_PER_STEP * TQ_ATTN)),
        in_specs=[pl.BlockSpec((1, ATTN_BLOCKS_PER_STEP * TQ_ATTN, SB_WIDTH), lambda b, i: (b, i, 0)),
                  pl.BlockSpec((1, seq, SB_WIDTH), lambda b, i: (b, 0, 1)),
                  pl.BlockSpec((1, seq, SB_WIDTH), lambda b, i: (b, 0, 2))],
        out_specs=pl.BlockSpec((1, ATTN_BLOCKS_PER_STEP * TQ_ATTN, SB_WIDTH), lambda b, i: (b, i, 0)),
        out_shape=jax.ShapeDtypeStruct((batch, seq, SB_WIDTH), F32),
        scratch_shapes=[pltpu.VMEM((n_heads * TQ_ATTN, HEAD_PAIR), BF16),
                        pltpu.VMEM((n_heads * TQ_ATTN, 1), F32)],
        compiler_params=pltpu.CompilerParams(dimension_semantics=("arbitrary",) * 2,
                                             vmem_limit_bytes=VMEM_LIMIT),
        name="sb_attention",
    )(qkv3, qkv3, qkv3)


def _mix_kernel(sb_ref, sgn_ref, x_ref, sbg_ref, wout_ref, ffng_ref, wr2_ref, br_ref,
                h_ref, lg_ref):
    sbn = _rms(sb_ref[...], sbg_ref[...]).astype(BF16)
    h = x_ref[...] + _dot(sbn, wout_ref[0:SB_WIDTH, :]) + _dot(sgn_ref[...], wout_ref[SB_WIDTH:, :])
    h_ref[...] = h
    hn = _rms(h, ffng_ref[...])

    hn_hi, hn_lo = _split_bf16(hn)
    both = _dot(hn_hi, wr2_ref[...])
    logits = both[:, 0:LANES] + both[:, LANES:] + _dot(hn_lo, wr2_ref[:, 0:LANES]) + br_ref[...]
    lg_ref[...] = logits.T[0:ROUTER_ROWS, :]


def _route_kernel(lg_ref, ri_ref, rw_ref, cnt_ref, count_ref):
    tr = TM_ROUTE
    i = pl.program_id(0)

    @pl.when(i == 0)
    def _():
        count_ref[...] = jnp.zeros_like(count_ref)

    neg = jnp.float32(-jnp.inf)
    row8 = lax.broadcasted_iota(jnp.int32, (SUBLANES, tr), 0)

    def top(v):
        m = jnp.max(v, axis=0, keepdims=True)
        return m, jnp.min(jnp.where(v == m, row8, SUBLANES), axis=0, keepdims=True)

    def group_rows(g):
        return lg_ref[ROUTER_LANE0 + g * EXPERTS_PER_GROUP:ROUTER_LANE0 + (g + 1) * EXPERTS_PER_GROUP, :]

    gl = jnp.where(row8 < N_GROUPS, lg_ref[0:SUBLANES, :], neg)
    gmax, gidx = top(gl)
    gweight = 1.0 / jnp.sum(jnp.exp(gl - gmax), axis=0, keepdims=True)
    el = group_rows(0)
    for g in range(1, N_GROUPS):
        el = jnp.where(gidx == g, group_rows(g), el)
    m1, i1 = top(el)
    m2, i2 = top(jnp.where(row8 == i1, neg, el))
    t21 = jnp.exp(m2 - m1)
    w1 = gweight / (1.0 + t21)
    w2 = gweight * t21 / (1.0 + t21)
    e1 = gidx * EXPERTS_PER_GROUP + i1
    e2 = gidx * EXPERTS_PER_GROUP + i2

    row_e = lax.broadcasted_iota(jnp.int32, (N_EXPERTS, tr), 0)
    sel1 = row_e == e1
    sel2 = row_e == e2
    onehot = jnp.where(sel1 | sel2, 1.0, 0.0)
    r_t = lax.broadcasted_iota(jnp.int32, (tr, tr), 0)
    c_t = lax.broadcasted_iota(jnp.int32, (tr, tr), 1)
    before = (r_t < c_t).astype(BF16)
    running = count_ref[:, 0:1] + _dot(onehot.astype(BF16), before)
    rank1 = jnp.sum(jnp.where(sel1, running, 0.0), axis=0, keepdims=True)
    rank2 = jnp.sum(jnp.where(sel2, running, 0.0), axis=0, keepdims=True)
    new_count = count_ref[:, 0:1] + jnp.sum(onehot, axis=1, keepdims=True)
    count_ref[...] = jnp.broadcast_to(new_count, count_ref.shape)
    cnt_ref[...] = jnp.broadcast_to(new_count, cnt_ref.shape)

    ri_ref[...] = jnp.where(row8 == 0, e1, jnp.where(row8 == 1, e2, jnp.where(
        row8 == 2, rank1.astype(jnp.int32), jnp.where(row8 == 3, rank2.astype(jnp.int32), 0))))
    row128 = lax.broadcasted_iota(jnp.int32, (LANES, tr), 0)
    rw_ref[...] = jnp.where(row128 == 0, w1, jnp.where(row128 == 1, w2, 0.0)).T


def _route(lg):
    n = lg.shape[1]
    return pl.pallas_call(
        _route_kernel,
        grid=(n // TM_ROUTE,),
        in_specs=[pl.BlockSpec((ROUTER_ROWS, TM_ROUTE), lambda i: (0, i))],
        out_specs=[pl.BlockSpec((SUBLANES, TM_ROUTE), lambda i: (0, i)),
                   pl.BlockSpec((TM_ROUTE, LANES), lambda i: (i, 0)),
                   pl.BlockSpec((N_EXPERTS, LANES), lambda i: (0, 0))],
        out_shape=[jax.ShapeDtypeStruct((SUBLANES, n), jnp.int32),
                   jax.ShapeDtypeStruct((n, LANES), F32),
                   jax.ShapeDtypeStruct((N_EXPERTS, LANES), F32)],
        scratch_shapes=[pltpu.VMEM((N_EXPERTS, LANES), F32)],
        compiler_params=pltpu.CompilerParams(dimension_semantics=("arbitrary",),
                                             vmem_limit_bytes=VMEM_LIMIT),
        name="route",
    )(lg)


def _mix(sb, sgn, x2, sb_g, w_out_b, ffn_g, wr2, br):
    n = x2.shape[0]
    row = lambda i: (i, 0)
    const = lambda i: (0, 0)
    return pl.pallas_call(
        _mix_kernel,
        grid=(n // TM_MIX,),
        in_specs=[pl.BlockSpec((TM_MIX, SB_WIDTH), row),
                  pl.BlockSpec((TM_MIX, SG_WIDTH), row),
                  pl.BlockSpec((TM_MIX, D_MODEL), row),
                  pl.BlockSpec((1, SB_WIDTH), const),
                  pl.BlockSpec((D_MODEL, D_MODEL), const),
                  pl.BlockSpec((1, D_MODEL), const),
                  pl.BlockSpec((D_MODEL, 2 * LANES), const),
                  pl.BlockSpec((1, LANES), const)],
        out_specs=[pl.BlockSpec((TM_MIX, D_MODEL), row),
                   pl.BlockSpec((ROUTER_ROWS, TM_MIX), lambda i: (0, i))],
        out_shape=[jax.ShapeDtypeStruct((n, D_MODEL), F32),
                   jax.ShapeDtypeStruct((ROUTER_ROWS, n), F32)],
        compiler_params=pltpu.CompilerParams(dimension_semantics=("arbitrary",),
                                             vmem_limit_bytes=VMEM_LIMIT),
        name="mix_router",
    )(sb, sgn, x2, sb_g, w_out_b, ffn_g, wr2, br)


_PAD_BITS = tuple(1 << b for b in reversed(range(TM_EXPERT.bit_length() - 1)))


def _dispatch_kernel(dest_ref, pad_start_ref, pad_count_ref, nt_ref, h_ref, g_ref, zeros_ref, xs_ref,
                     hn_ref, sem, zsem):
    tm = TM_DISPATCH
    i = pl.program_id(0)
    n_steps = pl.num_programs(0) - 1
    n = n_steps * tm
    base = (i - 1) * tm
    prev = hn_ref.at[lax.rem(i + 1, 2)]
    n_tiles_max = xs_ref.shape[0] // (TM_EXPERT * ROW_TILE)

    def pad_copies(do):
        for e in range(N_EXPERTS):
            start = pad_start_ref[e]
            count = pad_count_ref[e]
            for bit in _PAD_BITS:
                @pl.when((count & bit) != 0)
                def _(start=start, bit=bit):
                    do(pltpu.make_async_copy(_token_rows(zeros_ref, 0, bit),
                                             _token_rows(xs_ref, start, bit), zsem))
                start = start + (count & bit)
        for k in range(N_EXPERTS):
            tile = nt_ref[0] + k

            @pl.when(tile < n_tiles_max)
            def _(tile=tile):
                do(pltpu.make_async_copy(zeros_ref, _token_rows(xs_ref, tile * TM_EXPERT, TM_EXPERT), zsem))

    @pl.when(i == 0)
    def _():
        pad_copies(lambda cp: cp.start())

    @pl.when(i > 0)
    def _():
        def body(r, c):
            src = _token_rows(prev, r, 1)
            for s in range(2):
                pltpu.make_async_copy(src, _token_rows(xs_ref, dest_ref[s * n + base + r], 1),
                                      sem).start(priority=s)
            return c

        lax.fori_loop(0, tm, body, 0, unroll=8)

    @pl.when(i < n_steps)
    def _():
        _rows_to_tiles(hn_ref.at[lax.rem(i, 2)], _rms(h_ref[...], g_ref[...]))

    @pl.when(i > 0)
    def _():
        for _ in range(2):
            pltpu.make_async_copy(prev, _token_rows(xs_ref, 0, tm), sem).wait()

    @pl.when(i == n_steps)
    def _():
        pad_copies(lambda cp: cp.wait())


def _dispatch(dest, pad_start, pad_count, n_tiles, h, ffn_g, n_rows):
    n_steps = h.shape[0] // TM_DISPATCH
    zeros = jnp.zeros((TM_EXPERT * ROW_TILE, LANES), F32)
    return pl.pallas_call(
        _dispatch_kernel,
        grid_spec=pltpu.PrefetchScalarGridSpec(
            num_scalar_prefetch=4,
            grid=(n_steps + 1,),
            in_specs=[pl.BlockSpec((TM_DISPATCH, D_MODEL), lambda i, *_: (jnp.minimum(i, n_steps - 1), 0)),
                      pl.BlockSpec((1, D_MODEL), lambda i, *_: (0, 0)),
                      pl.BlockSpec(memory_space=pl.ANY)],
            out_specs=pl.BlockSpec(memory_space=pl.ANY),
            scratch_shapes=[pltpu.VMEM((2, TM_DISPATCH * ROW_TILE, LANES), F32),
                            pltpu.SemaphoreType.DMA, pltpu.SemaphoreType.DMA]),
        out_shape=jax.ShapeDtypeStruct((n_rows * ROW_TILE, LANES), F32),
        compiler_params=pltpu.CompilerParams(dimension_semantics=("arbitrary",),
                                             vmem_limit_bytes=VMEM_LIMIT),
        name="dispatch",
    )(dest, pad_start, pad_count, n_tiles, h, ffn_g, zeros)


X_SLOTS = 3


def _expert_kernel(tiles_ref, nt_ref, xs_ref, wg_ref, wu_ref, wd_ref, y_ref,
                   x_buf, sg_buf, su_buf, sd_buf, wgb, wub, wdb, state, w_sems, x_sems):
    tm = TM_EXPERT
    t = pl.program_id(0)
    nt = nt_ref[0]

    def x_copy(tile):
        slot = lax.rem(tile, X_SLOTS)
        return pltpu.make_async_copy(_token_rows(xs_ref, tile * tm, tm), x_buf.at[slot], x_sems.at[slot])

    def weight_copies(e, slot):
        return (pltpu.make_async_copy(wg_ref.at[e], sg_buf.at[slot], w_sems.at[slot]),
                pltpu.make_async_copy(wu_ref.at[e], su_buf.at[slot], w_sems.at[slot]),
                pltpu.make_async_copy(wd_ref.at[e], sd_buf.at[slot], w_sems.at[slot]))

    def next_with_rows(e):
        return lax.while_loop(lambda k: (k < N_EXPERTS) & (tiles_ref[jnp.minimum(k, N_EXPERTS - 1)] == 0),
                              lambda k: k + 1, e + 1)

    @pl.when(t == 0)
    def _():
        first = next_with_rows(jnp.int32(-1))
        state[0] = jnp.int32(-1)
        state[1] = jnp.int32(0)
        state[2] = jnp.int32(1)
        state[3] = first
        for cp in weight_copies(first, 0):
            cp.start()
        x_copy(0).start()

        @pl.when(nt > 1)
        def _():
            x_copy(1).start()

    @pl.when(t + 2 < nt)
    def _():
        x_copy(t + 2).start()

    @pl.when(t < nt)
    def _():
        @pl.when(state[1] == 0)
        def _():
            e = state[3]
            slot = 1 - state[2]
            nxt = next_with_rows(e)
            state[0] = e
            state[1] = tiles_ref[e]
            state[2] = slot
            state[3] = nxt
            for cp in weight_copies(e, slot):
                cp.wait()

            @pl.when(nxt < N_EXPERTS)
            def _():
                for cp in weight_copies(nxt, 1 - slot):
                    cp.start()

            wgb[...] = sg_buf[slot].astype(BF16)
            wub[...] = su_buf[slot].astype(BF16)
            wdb[...] = sd_buf[slot].astype(BF16)

        state[1] = state[1] - 1
        x_copy(t).wait()
        x = _tiles_to_rows(x_buf.at[lax.rem(t, X_SLOTS)], tm).astype(BF16)
        g = _dot(x, wgb[...])
        u = _dot(x, wub[...])
        hidden = (g * jax.nn.sigmoid(g)) * u
        _rows_to_tiles(y_ref, _dot(hidden.astype(BF16), wdb[...]))

    @pl.when(t >= nt)
    def _():
        y_ref[...] = jnp.zeros_like(y_ref)


def _experts(tiles, n_tiles, xs, wg, wu, wd):
    n_rows = xs.shape[0] // ROW_TILE
    any_spec = pl.BlockSpec(memory_space=pl.ANY)
    return pl.pallas_call(
        _expert_kernel,
        grid_spec=pltpu.PrefetchScalarGridSpec(
            num_scalar_prefetch=2,
            grid=(n_rows // TM_EXPERT,),
            in_specs=[any_spec, any_spec, any_spec, any_spec],
            out_specs=pl.BlockSpec((TM_EXPERT * ROW_TILE, LANES), lambda t, *_: (t, 0)),
            scratch_shapes=[pltpu.VMEM((X_SLOTS, TM_EXPERT * ROW_TILE, LANES), F32),
                            pltpu.VMEM((2, D_MODEL, D_EXPERT), F32),
                            pltpu.VMEM((2, D_MODEL, D_EXPERT), F32),
                            pltpu.VMEM((2, D_EXPERT, D_MODEL), F32),
                            pltpu.VMEM((D_MODEL, D_EXPERT), BF16),
                            pltpu.VMEM((D_MODEL, D_EXPERT), BF16),
                            pltpu.VMEM((D_EXPERT, D_MODEL), BF16),
                            pltpu.SMEM((4,), jnp.int32),
                            pltpu.SemaphoreType.DMA((2,)),
                            pltpu.SemaphoreType.DMA((X_SLOTS,))]),
        out_shape=jax.ShapeDtypeStruct((n_rows * ROW_TILE, LANES), F32),
        compiler_params=pltpu.CompilerParams(dimension_semantics=("arbitrary",),
                                             vmem_limit_bytes=VMEM_LIMIT),
        name="expert_mlp",
    )(tiles, n_tiles, xs, wg, wu, wd)


def _combine_kernel(dest_ref, h_ref, rw_ref, fg_ref, y_ref, o_ref, buf, sems):
    tm = TM_COMBINE
    i = pl.program_id(0)
    n_steps = pl.num_programs(0)
    n = n_steps * tm
    cur = i % 2

    def fetch(step, half):
        def body(r, c):
            for s in range(2):
                pltpu.make_async_copy(_token_rows(y_ref, dest_ref[s * n + step * tm + r], 1),
                                      _token_rows(buf.at[half, s], r, 1),
                                      sems.at[half]).start(priority=s)
            return c

        lax.fori_loop(0, tm, body, 0, unroll=8)

    @pl.when(i == 0)
    def _():
        fetch(0, 0)

    @pl.when(i + 1 < n_steps)
    def _():
        fetch(i + 1, 1 - cur)

    for s in range(2):
        pltpu.make_async_copy(_token_rows(y_ref, 0, tm), buf.at[cur, s], sems.at[cur]).wait()
    rw = rw_ref[...]
    out = (h_ref[...] + rw[:, 0:1] * _tiles_to_rows(buf.at[cur, 0], tm)
           + rw[:, 1:2] * _tiles_to_rows(buf.at[cur, 1], tm))
    o_ref[...] = _rms(out, fg_ref[...])


def _combine(dest, h, rw, final_g, ys):
    n = h.shape[0]
    return pl.pallas_call(
        _combine_kernel,
        grid_spec=pltpu.PrefetchScalarGridSpec(
            num_scalar_prefetch=1,
            grid=(n // TM_COMBINE,),
            in_specs=[pl.BlockSpec((TM_COMBINE, D_MODEL), lambda i, d: (i, 0)),
                      pl.BlockSpec((TM_COMBINE, LANES), lambda i, d: (i, 0)),
                      pl.BlockSpec((1, D_MODEL), lambda i, d: (0, 0)),
                      pl.BlockSpec(memory_space=pl.ANY)],
            out_specs=pl.BlockSpec((TM_COMBINE, D_MODEL), lambda i, d: (i, 0)),
            scratch_shapes=[pltpu.VMEM((2, 2, TM_COMBINE * ROW_TILE, LANES), F32),
                            pltpu.SemaphoreType.DMA((2,))]),
        out_shape=jax.ShapeDtypeStruct((n, D_MODEL), F32),
        compiler_params=pltpu.CompilerParams(dimension_semantics=("arbitrary",),
                                             vmem_limit_bytes=VMEM_LIMIT),
        name="combine",
    )(dest, h, rw, final_g, ys)


def _schedule(counts):
    tiles = (counts + TM_EXPERT - 1) // TM_EXPERT
    tile_end = jnp.cumsum(tiles)
    offsets = (tile_end - tiles) * TM_EXPERT
    return tiles, offsets, tile_end[-1:]


def _layer(x, attn_g, w_in, sg_g, w_sp, b_sp, sb_g, sg_out_g, w_out, ffn_g,
           w_rg, b_rg, w_re, b_re, w_gate, w_up, w_down):
    batch, seq, _ = x.shape
    n = batch * seq
    x2 = x.reshape(n, D_MODEL)
    row = lambda v: v.reshape(1, -1)

    bsp_full = jnp.repeat(b_sp.T, HEAD_DIM, axis=1)
    qkv, sgn = _inproj(x2, row(attn_g), w_in.astype(BF16), row(sg_g), w_sp, bsp_full, row(sg_out_g))
    sb = _attention(qkv, batch, seq).reshape(n, SB_WIDTH)

    pad_lanes = lambda v, width: jnp.pad(v, [(0, 0)] * (v.ndim - 1) + [(0, width - v.shape[-1])])
    w_r = jnp.concatenate([pad_lanes(w_rg, ROUTER_LANE0),
                           jnp.transpose(w_re, (1, 0, 2)).reshape(D_MODEL, N_EXPERTS)], axis=1)
    w_r = pad_lanes(w_r, LANES)
    wr_hi = w_r.astype(BF16)
    wr_lo = (w_r - wr_hi.astype(F32)).astype(BF16)
    wr2 = jnp.concatenate([wr_hi, wr_lo], axis=1)
    b_r = pad_lanes(jnp.concatenate([pad_lanes(b_rg, ROUTER_LANE0), b_re.reshape(-1)]), LANES)

    h, lg = _mix(sb, sgn, x2, row(sb_g), w_out.astype(BF16), row(ffn_g), wr2, row(b_r))
    ri, rw, cnt = _route(lg)

    counts = cnt[:, 0].astype(jnp.int32)
    n_rows = 2 * n + N_EXPERTS * TM_EXPERT
    tiles, offsets, n_tiles = _schedule(counts)
    expert, rank = ri[0:2], ri[2:4]
    is_e = expert[None] == jnp.arange(N_EXPERTS, dtype=jnp.int32)[:, None, None]
    dest = (jnp.sum(jnp.where(is_e, offsets[:, None, None], 0), axis=0) + rank).reshape(-1)
    pad_start = offsets + counts
    pad_count = (-counts) % TM_EXPERT

    xs = _dispatch(dest, pad_start, pad_count, n_tiles, h, row(ffn_g), n_rows)
    ys = _experts(tiles, n_tiles, xs,
                  w_gate.reshape(N_EXPERTS, D_MODEL, D_EXPERT),
                  w_up.reshape(N_EXPERTS, D_MODEL, D_EXPERT),
                  w_down.reshape(N_EXPERTS, D_EXPERT, D_MODEL))
    return dest, h, rw, ys


def kernel(x, attn_norm_g, w_in, sg_norm_g, w_spatial, b_spatial, sb_out_norm_g, sg_out_norm_g,
           w_out, ffn_norm_g, w_router_group, b_router_group, w_router_expert, b_router_expert,
           w_gate, w_up, w_down, final_norm_g):
    assert attn_norm_g.shape[0] == 1, "single-layer problem"
    batch, seq, _ = x.shape
    dest, h, rw, ys = _layer(x, attn_norm_g[0], w_in[0], sg_norm_g[0], w_spatial[0], b_spatial[0],
                             sb_out_norm_g[0], sg_out_norm_g[0], w_out[0], ffn_norm_g[0],
                             w_router_group[0], b_router_group[0], w_router_expert[0],
                             b_router_expert[0], w_gate[0], w_up[0], w_down[0])
    out = _combine(dest, h, rw, final_norm_g.reshape(1, -1), ys)
    return out.reshape(batch, seq, D_MODEL)
```

```python
import functools
import math

import jax
import jax.numpy as jnp
from jax import lax
from jax.experimental import pallas as pl
from jax.experimental.pallas import tpu as pltpu

D_MODEL = 1024
HEAD_DIM = 64
SB_WIDTH = 512
SG_WIDTH = 512
SG_HEADS = 8
D_IN = 3 * SB_WIDTH + 2 * SG_WIDTH
CHUNK = 128
N_GROUPS = 4
EXPERTS_PER_GROUP = 8
N_EXPERTS = N_GROUPS * EXPERTS_PER_GROUP
D_EXPERT = 512
EPS = 1e-6
F32_EXP_UNDERFLOW = 110.0

LANES = 128
SUBLANES = 8
ROW_TILE = D_MODEL // LANES
assert ROW_TILE == SUBLANES
HEAD_PAIR = 2 * HEAD_DIM
ROUTER_LANE0 = SUBLANES
ROUTER_ROWS = ROUTER_LANE0 + N_EXPERTS
assert EXPERTS_PER_GROUP == SUBLANES and N_GROUPS <= ROUTER_LANE0

TM_PROJ = 1024
TQ_ATTN = 256
ATTN_BLOCKS_PER_STEP = 2
ATTN_TOP_ROWS = 160
TM_MIX = 1024
TM_ROUTE = 1024
TM_DISPATCH = 1024
TM_EXPERT = 512
EXPERT_CHUNK = 128
TM_COMBINE = 512
VMEM_LIMIT = 48 * 1024 * 1024

F32 = jnp.float32
BF16 = jnp.bfloat16


def _rms(x, g):
    return x * lax.rsqrt(jnp.mean(x * x, axis=-1, keepdims=True) + EPS) * g


def _gelu(x):
    c = math.sqrt(2.0 / math.pi)
    return x * (0.5 * (1.0 + jnp.tanh(c * (x + 0.044715 * (x * x * x)))))


def _softplus(z):
    return jnp.maximum(z, 0.0) + jnp.log(1.0 + jnp.exp(-jnp.abs(z)))


def _dot(a, b):
    return jnp.dot(a, b, preferred_element_type=F32)


def _rows_to_tiles(ref, x):
    m = x.shape[0]
    for k in range(ROW_TILE):
        ref[pl.ds(k, m, stride=ROW_TILE), :] = x[:, k * LANES:(k + 1) * LANES]


def _tiles_to_rows(ref, m):
    return jnp.concatenate([ref[pl.ds(k, m, stride=ROW_TILE), :] for k in range(ROW_TILE)], axis=1)


def _token_rows(ref, first_token, n_tokens):
    return ref.at[pl.ds(pl.multiple_of(first_token * ROW_TILE, ROW_TILE), n_tokens * ROW_TILE)]


def _split_bf16(x):
    hi = x.astype(BF16)
    lo = (x - hi.astype(F32)).astype(BF16)
    return hi, lo


def _inproj_kernel(x_ref, g_ref, w_ref, sgg_ref, wsp_ref, bsp_ref, sgog_ref, qkv_ref, sgn_ref,
                   gu_ref, vgn_ref, sg_ref):
    tm = TM_PROJ
    hb = _rms(x_ref[...], g_ref[...]).astype(BF16)
    gv = _gelu(_dot(hb, w_ref[:, 3 * SB_WIDTH + SG_WIDTH:D_IN]))
    vgn_ref[...] = _rms(gv, sgg_ref[...]).astype(BF16)
    gu_ref[...] = _gelu(_dot(hb, w_ref[:, 3 * SB_WIDTH:3 * SB_WIDTH + SG_WIDTH]))
    q = _dot(hb, w_ref[:, 0:SB_WIDTH]) * (1.0 / math.sqrt(HEAD_DIM))
    qkv_ref[:, 0:SB_WIDTH] = q.astype(BF16)
    qkv_ref[:, SB_WIDTH:2 * SB_WIDTH] = _dot(hb, w_ref[:, SB_WIDTH:2 * SB_WIDTH]).astype(BF16)

    lane = lax.broadcasted_iota(jnp.int32, (1, LANES), 1)
    first = lane < HEAD_DIM
    zero = jnp.zeros((), BF16)
    r_c = lax.broadcasted_iota(jnp.int32, (CHUNK, CHUNK), 0)
    c_c = lax.broadcasted_iota(jnp.int32, (CHUNK, CHUNK), 1)
    tril = r_c >= c_c
    n_pairs = SG_WIDTH // HEAD_PAIR
    w_pairs = []
    for p in range(n_pairs):
        w0 = jnp.where(tril, wsp_ref[2 * p], 0.0).astype(BF16)
        w1 = jnp.where(tril, wsp_ref[2 * p + 1], 0.0).astype(BF16)
        w_pairs.append(jnp.concatenate([w0, w1], axis=1))
    bsp = bsp_ref[...]
    for c in range(tm // CHUNK):
        rows = slice(c * CHUNK, (c + 1) * CHUNK)
        for p in range(n_pairs):
            cols = slice(p * HEAD_PAIR, (p + 1) * HEAD_PAIR)
            vg = vgn_ref[rows, cols]
            rhs = jnp.concatenate([jnp.where(first, vg, zero), jnp.where(first, zero, vg)], axis=0)
            mixed = _dot(w_pairs[p], rhs) + bsp[:, cols]
            sg_ref[rows, cols] = gu_ref[rows, cols] * mixed
    qkv_ref[:, 2 * SB_WIDTH:3 * SB_WIDTH] = _dot(hb, w_ref[:, 2 * SB_WIDTH:3 * SB_WIDTH]).astype(BF16)
    sgn_ref[...] = _rms(sg_ref[...], sgog_ref[...]).astype(BF16)


def _inproj(x2, attn_g, w_in_b, sg_g, wsp, bsp_full, sg_out_g):
    n = x2.shape[0]
    row = lambda i: (i, 0)
    const = lambda i: (0, 0)
    return pl.pallas_call(
        _inproj_kernel,
        grid=(n // TM_PROJ,),
        in_specs=[pl.BlockSpec((TM_PROJ, D_MODEL), row),
                  pl.BlockSpec((1, D_MODEL), const),
                  pl.BlockSpec((D_MODEL, D_IN), const),
                  pl.BlockSpec((1, SG_WIDTH), const),
                  pl.BlockSpec((SG_HEADS, CHUNK, CHUNK), lambda i: (0, 0, 0)),
                  pl.BlockSpec((CHUNK, SG_WIDTH), const),
                  pl.BlockSpec((1, SG_WIDTH), const)],
        out_specs=[pl.BlockSpec((TM_PROJ, 3 * SB_WIDTH), row),
                   pl.BlockSpec((TM_PROJ, SG_WIDTH), row)],
        out_shape=[jax.ShapeDtypeStruct((n, 3 * SB_WIDTH), BF16),
                   jax.ShapeDtypeStruct((n, SG_WIDTH), BF16)],
        scratch_shapes=[pltpu.VMEM((TM_PROJ, SG_WIDTH), F32),
                        pltpu.VMEM((TM_PROJ, SG_WIDTH), BF16),
                        pltpu.VMEM((TM_PROJ, SG_WIDTH), F32)],
        compiler_params=pltpu.CompilerParams(dimension_semantics=("arbitrary",),
                                             vmem_limit_bytes=VMEM_LIMIT),
        name="inproj",
    )(x2, attn_g, w_in_b, sg_g, wsp, bsp_full, sg_out_g)


def _attn_kernel(q_ref, k_ref, v_ref, o_ref, q2_ref, carry_ref):
    t = TQ_ATTN
    n_pairs = SB_WIDTH // HEAD_PAIR
    lane = lax.broadcasted_iota(jnp.int32, (1, HEAD_PAIR), 1)
    head_lanes = (lane < HEAD_DIM, lane >= HEAD_DIM)
    zero = jnp.zeros((), BF16)
    r_idx = lax.broadcasted_iota(jnp.int32, (t, t), 0)
    c_idx = lax.broadcasted_iota(jnp.int32, (t, t), 1)
    suffix = (r_idx > c_idx).astype(BF16)
    suffix2 = jnp.concatenate([suffix, suffix], axis=0)
    causal = c_idx < r_idx

    def one_query_block(sub, c):
        qi = pl.program_id(1) * ATTN_BLOCKS_PER_STEP + sub
        row0 = pl.multiple_of(sub * t, t)
        for p in range(n_pairs):
            qp = q_ref[0, pl.ds(row0, t), p * HEAD_PAIR:(p + 1) * HEAD_PAIR]
            for h in range(2):
                q2_ref[(2 * p + h) * t:(2 * p + h + 1) * t, :] = jnp.where(head_lanes[h], qp, zero)
        o_ref[0, pl.ds(row0, t), :] = jnp.zeros((t, SB_WIDTH), F32)
        carry_ref[...] = jnp.zeros_like(carry_ref)

        def block(j, diag, m):
            start = pl.multiple_of(j * t, t)
            mask2 = jnp.concatenate([causal, causal], axis=0) if diag else None
            st = [dict() for _ in range(n_pairs)]

            def head_rows(p):
                return [slice((2 * p + h) * t, (2 * p + h) * t + m) for h in range(2)]

            def scores(p):
                d = st[p]
                d["cols"] = slice(p * HEAD_PAIR, (p + 1) * HEAD_PAIR)
                kb = k_ref[0, pl.ds(start, t), d["cols"]]
                q2 = jnp.concatenate([q2_ref[r, :] for r in head_rows(p)], axis=0)
                z = lax.dot_general(q2, kb, (((1,), (1,)), ((), ())),
                                    preferred_element_type=F32)
                sp = _softplus(z)
                nl = jnp.where(mask2, sp, 0.0) if diag else sp
                hi, lo = _split_bf16(nl)
                d["hl"] = jnp.concatenate([hi, lo], axis=1)
                d["log_beta"] = z - sp
                d["nl0"] = nl[:, 0:1]

            def weights(p):
                d = st[p]
                hl = d["hl"]
                after = jnp.concatenate([_dot(hl[0:m], suffix2), _dot(hl[m:2 * m], suffix2)], axis=0)
                carry = jnp.concatenate([carry_ref[r, :] for r in head_rows(p)], axis=0)
                a = jnp.exp(d["log_beta"] - after - carry)
                if diag:
                    a = jnp.where(mask2, a, 0.0)
                a = a.astype(BF16)
                d["a2"] = jnp.concatenate([a[0:m], a[m:2 * m]], axis=1)
                new_carry = carry + after[:, 0:1] + d["nl0"]
                for h, r in enumerate(head_rows(p)):
                    carry_ref[r, :] = new_carry[h * m:(h + 1) * m]

            def values(p):
                d = st[p]
                vb = v_ref[0, pl.ds(start, t), d["cols"]]
                v2 = jnp.concatenate([jnp.where(head_lanes[0], vb, zero),
                                      jnp.where(head_lanes[1], vb, zero)], axis=0)
                o_ref[0, pl.ds(row0, m), d["cols"]] += _dot(d["a2"], v2)

            for step in range(n_pairs + 2):
                if step < n_pairs:
                    scores(step)
                if 0 <= step - 1 < n_pairs:
                    weights(step - 1)
                if 0 <= step - 2 < n_pairs:
                    values(step - 2)

        top = ATTN_TOP_ROWS

        def flags():
            bottom = jnp.concatenate([carry_ref[hh * t + top:(hh + 1) * t, :] for hh in range(2 * n_pairs)], axis=0)
            return (jnp.min(carry_ref[...]) < F32_EXP_UNDERFLOW, jnp.min(bottom) >= F32_EXP_UNDERFLOW)

        block(qi, True, t)

        def body(state):
            it, _, bottom_done = state
            j = qi - 1 - it

            @pl.when(bottom_done)
            def _():
                block(j, False, top)

            @pl.when(jnp.logical_not(bottom_done))
            def _():
                block(j, False, t)

            return (it + 1,) + flags()

        lax.while_loop(lambda s: (s[0] < qi) & s[1], body, (jnp.int32(0),) + flags())
        return c

    lax.fori_loop(0, ATTN_BLOCKS_PER_STEP, one_query_block, 0)


def _attention(qkv, batch, seq):
    qkv3 = qkv.reshape(batch, seq, 3 * SB_WIDTH)
    n_heads = SB_WIDTH // HEAD_DIM
    return pl.pallas_call(
        _attn_kernel,
        grid=(batch, seq // (ATTN_BLOCKS_PER_STEP * TQ_ATTN)),
        in_specs=[pl.BlockSpec((1, ATTN_BLOCKS_PER_STEP * TQ_ATTN, SB_WIDTH), lambda b, i: (b, i, 0)),
                  pl.BlockSpec((1, seq, SB_WIDTH), lambda b, i: (b, 0, 1)),
                  pl.BlockSpec((1, seq, SB_WIDTH), lambda b, i: (b, 0, 2))],
        out_specs=pl.BlockSpec((1, ATTN_BLOCKS_PER_STEP * TQ_ATTN, SB_WIDTH), lambda b, i: (b, i, 0)),
        out_shape=jax.ShapeDtypeStruct((batch, seq, SB_WIDTH), F32),
        scratch_shapes=[pltpu.VMEM((n_heads * TQ_ATTN, HEAD_PAIR), BF16),
                        pltpu.VMEM((n_heads * TQ_ATTN, 1), F32)],
        compiler_params=pltpu.CompilerParams(dimension_semantics=("arbitrary",) * 2,
                                             vmem_limit_bytes=VMEM_LIMIT),
        name="sb_attention",
    )(qkv3, qkv3, qkv3)


def _mix_kernel(sb_ref, sgn_ref, x_ref, sbg_ref, wout_ref, ffng_ref, wr2_ref, br_ref,
                h_ref, lg_ref):
    sbn = _rms(sb_ref[...], sbg_ref[...]).astype(BF16)
    h = x_ref[...] + _dot(sbn, wout_ref[0:SB_WIDTH, :]) + _dot(sgn_ref[...], wout_ref[SB_WIDTH:, :])
    h_ref[...] = h
    hn = _rms(h, ffng_ref[...])

    hn_hi, hn_lo = _split_bf16(hn)
    both = _dot(hn_hi, wr2_ref[...])
    logits = both[:, 0:LANES] + both[:, LANES:] + _dot(hn_lo, wr2_ref[:, 0:LANES]) + br_ref[...]
    lg_ref[...] = logits.T[0:ROUTER_ROWS, :]


def _route_kernel(lg_ref, ri_ref, rw_ref, cnt_ref, count_ref):
    tr = TM_ROUTE
    i = pl.program_id(0)

    @pl.when(i == 0)
    def _():
        count_ref[...] = jnp.zeros_like(count_ref)

    neg = jnp.float32(-jnp.inf)
    row8 = lax.broadcasted_iota(jnp.int32, (SUBLANES, tr), 0)

    def top(v):
        m = jnp.max(v, axis=0, keepdims=True)
        return m, jnp.min(jnp.where(v == m, row8, SUBLANES), axis=0, keepdims=True)

    def group_rows(g):
        return lg_ref[ROUTER_LANE0 + g * EXPERTS_PER_GROUP:ROUTER_LANE0 + (g + 1) * EXPERTS_PER_GROUP, :]

    gl = jnp.where(row8 < N_GROUPS, lg_ref[0:SUBLANES, :], neg)
    gmax, gidx = top(gl)
    gweight = 1.0 / jnp.sum(jnp.exp(gl - gmax), axis=0, keepdims=True)
    el = group_rows(0)
    for g in range(1, N_GROUPS):
        el = jnp.where(gidx == g, group_rows(g), el)
    m1, i1 = top(el)
    m2, i2 = top(jnp.where(row8 == i1, neg, el))
    t21 = jnp.exp(m2 - m1)
    w1 = gweight / (1.0 + t21)
    w2 = gweight * t21 / (1.0 + t21)
    e1 = gidx * EXPERTS_PER_GROUP + i1
    e2 = gidx * EXPERTS_PER_GROUP + i2

    row_e = lax.broadcasted_iota(jnp.int32, (N_EXPERTS, tr), 0)
    sel1 = row_e == e1
    sel2 = row_e == e2
    onehot = jnp.where(sel1 | sel2, 1.0, 0.0)
    r_t = lax.broadcasted_iota(jnp.int32, (tr, tr), 0)
    c_t = lax.broadcasted_iota(jnp.int32, (tr, tr), 1)
    before = (r_t < c_t).astype(BF16)
    running = count_ref[:, 0:1] + _dot(onehot.astype(BF16), before)
    rank1 = jnp.sum(jnp.where(sel1, running, 0.0), axis=0, keepdims=True)
    rank2 = jnp.sum(jnp.where(sel2, running, 0.0), axis=0, keepdims=True)
    new_count = count_ref[:, 0:1] + jnp.sum(onehot, axis=1, keepdims=True)
    count_ref[...] = jnp.broadcast_to(new_count, count_ref.shape)
    cnt_ref[...] = jnp.broadcast_to(new_count, cnt_ref.shape)

    ri_ref[...] = jnp.where(row8 == 0, e1, jnp.where(row8 == 1, e2, jnp.where(
        row8 == 2, rank1.astype(jnp.int32), jnp.where(row8 == 3, rank2.astype(jnp.int32), 0))))
    row128 = lax.broadcasted_iota(jnp.int32, (LANES, tr), 0)
    rw_ref[...] = jnp.where(row128 == 0, w1, jnp.where(row128 == 1, w2, 0.0)).T


def _route(lg):
    n = lg.shape[1]
    return pl.pallas_call(
        _route_kernel,
        grid=(n // TM_ROUTE,),
        in_specs=[pl.BlockSpec((ROUTER_ROWS, TM_ROUTE), lambda i: (0, i))],
        out_specs=[pl.BlockSpec((SUBLANES, TM_ROUTE), lambda i: (0, i)),
                   pl.BlockSpec((TM_ROUTE, LANES), lambda i: (i, 0)),
                   pl.BlockSpec((N_EXPERTS, LANES), lambda i: (0, 0))],
        out_shape=[jax.ShapeDtypeStruct((SUBLANES, n), jnp.int32),
                   jax.ShapeDtypeStruct((n, LANES), F32),
                   jax.ShapeDtypeStruct((N_EXPERTS, LANES), F32)],
        scratch_shapes=[pltpu.VMEM((N_EXPERTS, LANES), F32)],
        compiler_params=pltpu.CompilerParams(dimension_semantics=("arbitrary",),
                                             vmem_limit_bytes=VMEM_LIMIT),
        name="route",
    )(lg)


def _mix(sb, sgn, x2, sb_g, w_out_b, ffn_g, wr2, br):
    n = x2.shape[0]
    row = lambda i: (i, 0)
    const = lambda i: (0, 0)
    return pl.pallas_call(
        _mix_kernel,
        grid=(n // TM_MIX,),
        in_specs=[pl.BlockSpec((TM_MIX, SB_WIDTH), row),
                  pl.BlockSpec((TM_MIX, SG_WIDTH), row),
                  pl.BlockSpec((TM_MIX, D_MODEL), row),
                  pl.BlockSpec((1, SB_WIDTH), const),
                  pl.BlockSpec((D_MODEL, D_MODEL), const),
                  pl.BlockSpec((1, D_MODEL), const),
                  pl.BlockSpec((D_MODEL, 2 * LANES), const),
                  pl.BlockSpec((1, LANES), const)],
        out_specs=[pl.BlockSpec((TM_MIX, D_MODEL), row),
                   pl.BlockSpec((ROUTER_ROWS, TM_MIX), lambda i: (0, i))],
        out_shape=[jax.ShapeDtypeStruct((n, D_MODEL), F32),
                   jax.ShapeDtypeStruct((ROUTER_ROWS, n), F32)],
        compiler_params=pltpu.CompilerParams(dimension_semantics=("arbitrary",),
                                             vmem_limit_bytes=VMEM_LIMIT),
        name="mix_router",
    )(sb, sgn, x2, sb_g, w_out_b, ffn_g, wr2, br)


_PAD_BITS = tuple(1 << b for b in reversed(range(EXPERT_CHUNK.bit_length() - 1)))


def _dispatch_kernel(dest_ref, pad_start_ref, pad_count_ref, used_ref, h_ref, g_ref, zeros_ref, xs_ref,
                     hn_ref, sem, zsem):
    tm = TM_DISPATCH
    i = pl.program_id(0)
    n_steps = pl.num_programs(0) - 1
    n = n_steps * tm
    base = (i - 1) * tm
    prev = hn_ref.at[lax.rem(i + 1, 2)]
    n_chunks = xs_ref.shape[0] // (EXPERT_CHUNK * ROW_TILE)

    def pad_copies(do):
        for e in range(N_EXPERTS):
            start = pad_start_ref[e]
            count = pad_count_ref[e]
            for bit in _PAD_BITS:
                @pl.when((count & bit) != 0)
                def _(start=start, bit=bit):
                    do(pltpu.make_async_copy(_token_rows(zeros_ref, 0, bit),
                                             _token_rows(xs_ref, start, bit), zsem))
                start = start + (count & bit)
        for k in range(N_EXPERTS):
            chunk = used_ref[0] + k

            @pl.when(chunk < n_chunks)
            def _(chunk=chunk):
                do(pltpu.make_async_copy(zeros_ref, _token_rows(xs_ref, chunk * EXPERT_CHUNK, EXPERT_CHUNK),
                                         zsem))

    @pl.when(i == 0)
    def _():
        pad_copies(lambda cp: cp.start())

    @pl.when(i > 0)
    def _():
        def body(r, c):
            src = _token_rows(prev, r, 1)
            for s in range(2):
                pltpu.make_async_copy(src, _token_rows(xs_ref, dest_ref[s * n + base + r], 1),
                                      sem).start(priority=s)
            return c

        lax.fori_loop(0, tm, body, 0, unroll=8)

    @pl.when(i < n_steps)
    def _():
        _rows_to_tiles(hn_ref.at[lax.rem(i, 2)], _rms(h_ref[...], g_ref[...]))

    @pl.when(i > 0)
    def _():
        for _ in range(2):
            pltpu.make_async_copy(prev, _token_rows(xs_ref, 0, tm), sem).wait()

    @pl.when(i == n_steps)
    def _():
        pad_copies(lambda cp: cp.wait())


def _dispatch(dest, pad_start, pad_count, used_chunks, h, ffn_g, n_rows):
    n_steps = h.shape[0] // TM_DISPATCH
    zeros = jnp.zeros((EXPERT_CHUNK * ROW_TILE, LANES), F32)
    return pl.pallas_call(
        _dispatch_kernel,
        grid_spec=pltpu.PrefetchScalarGridSpec(
            num_scalar_prefetch=4,
            grid=(n_steps + 1,),
            in_specs=[pl.BlockSpec((TM_DISPATCH, D_MODEL), lambda i, *_: (jnp.minimum(i, n_steps - 1), 0)),
                      pl.BlockSpec((1, D_MODEL), lambda i, *_: (0, 0)),
                      pl.BlockSpec(memory_space=pl.ANY)],
            out_specs=pl.BlockSpec(memory_space=pl.ANY),
            scratch_shapes=[pltpu.VMEM((2, TM_DISPATCH * ROW_TILE, LANES), F32),
                            pltpu.SemaphoreType.DMA, pltpu.SemaphoreType.DMA]),
        out_shape=jax.ShapeDtypeStruct((n_rows * ROW_TILE, LANES), F32),
        compiler_params=pltpu.CompilerParams(dimension_semantics=("arbitrary",),
                                             vmem_limit_bytes=VMEM_LIMIT),
        name="dispatch",
    )(dest, pad_start, pad_count, used_chunks, h, ffn_g, zeros)


X_SLOTS = 3
TILE_CHUNKS = TM_EXPERT // EXPERT_CHUNK


def _expert_kernel(tiles_ref, chunk0_ref, chunks_ref, nt_ref, used_ref, xs_ref, wg_ref, wu_ref, wd_ref,
                   zeros_ref, ys_ref, x_buf, y_buf, sg_buf, su_buf, sd_buf, wgb, wub, wdb, state,
                   w_sems, x_sems, y_sems, zsem):
    t = pl.program_id(0)
    last = pl.num_programs(0) - 1
    nt = nt_ref[0]
    n_chunks = ys_ref.shape[0] // (EXPERT_CHUNK * ROW_TILE)

    def tile_copies(tile, do, out):
        for c in range(TILE_CHUNKS):
            @pl.when(c < chunks_ref[tile])
            def _(c=c):
                first = (chunk0_ref[tile] + c) * EXPERT_CHUNK
                if out:
                    slot = lax.rem(tile, 2)
                    do(pltpu.make_async_copy(_token_rows(y_buf.at[slot], c * EXPERT_CHUNK, EXPERT_CHUNK),
                                             _token_rows(ys_ref, first, EXPERT_CHUNK), y_sems.at[slot]))
                else:
                    slot = lax.rem(tile, X_SLOTS)
                    do(pltpu.make_async_copy(_token_rows(xs_ref, first, EXPERT_CHUNK),
                                             _token_rows(x_buf.at[slot], c * EXPERT_CHUNK, EXPERT_CHUNK),
                                             x_sems.at[slot]))

    start = lambda cp: cp.start()
    wait = lambda cp: cp.wait()

    def tail_copies(do):
        for k in range(N_EXPERTS):
            chunk = used_ref[0] + k

            @pl.when(chunk < n_chunks)
            def _(chunk=chunk):
                do(pltpu.make_async_copy(zeros_ref, _token_rows(ys_ref, chunk * EXPERT_CHUNK, EXPERT_CHUNK),
                                         zsem))

    def weight_copies(e, slot):
        return (pltpu.make_async_copy(wg_ref.at[e], sg_buf.at[slot], w_sems.at[slot]),
                pltpu.make_async_copy(wu_ref.at[e], su_buf.at[slot], w_sems.at[slot]),
                pltpu.make_async_copy(wd_ref.at[e], sd_buf.at[slot], w_sems.at[slot]))

    def next_with_rows(e):
        return lax.while_loop(lambda k: (k < N_EXPERTS) & (tiles_ref[jnp.minimum(k, N_EXPERTS - 1)] == 0),
                              lambda k: k + 1, e + 1)

    @pl.when(t == 0)
    def _():
        first = next_with_rows(jnp.int32(-1))
        state[0] = jnp.int32(-1)
        state[1] = jnp.int32(0)
        state[2] = jnp.int32(1)
        state[3] = first
        for cp in weight_copies(first, 0):
            cp.start()
        tile_copies(0, start, False)

        @pl.when(nt > 1)
        def _():
            tile_copies(1, start, False)

        tail_copies(start)

    @pl.when(t + 2 < nt)
    def _():
        tile_copies(t + 2, start, False)

    @pl.when(t < nt)
    def _():
        @pl.when(state[1] == 0)
        def _():
            e = state[3]
            slot = 1 - state[2]
            nxt = next_with_rows(e)
            state[0] = e
            state[1] = tiles_ref[e]
            state[2] = slot
            state[3] = nxt
            for cp in weight_copies(e, slot):
                cp.wait()

            @pl.when(nxt < N_EXPERTS)
            def _():
                for cp in weight_copies(nxt, 1 - slot):
                    cp.start()

            wgb[...] = sg_buf[slot].astype(BF16)
            wub[...] = su_buf[slot].astype(BF16)
            wdb[...] = sd_buf[slot].astype(BF16)

        state[1] = state[1] - 1
        tile_copies(t, wait, False)

        @pl.when(t >= 2)
        def _():
            tile_copies(t - 2, wait, True)

        for n_chunks_here in range(1, TILE_CHUNKS + 1):
            @pl.when(chunks_ref[t] == n_chunks_here)
            def _(m=n_chunks_here * EXPERT_CHUNK):
                x = _tiles_to_rows(x_buf.at[lax.rem(t, X_SLOTS)], m).astype(BF16)
                g = _dot(x, wgb[...])
                u = _dot(x, wub[...])
                hidden = (g * jax.nn.sigmoid(g)) * u
                _rows_to_tiles(y_buf.at[lax.rem(t, 2)], _dot(hidden.astype(BF16), wdb[...]))

        tile_copies(t, start, True)

    @pl.when(t == last)
    def _():
        for back in (2, 1):
            @pl.when(nt >= back)
            def _(back=back):
                tile_copies(nt - back, wait, True)

        tail_copies(wait)


def _experts(tiles, chunk0, chunks, n_tiles, used_chunks, xs, wg, wu, wd):
    any_spec = pl.BlockSpec(memory_space=pl.ANY)
    zeros = jnp.zeros((EXPERT_CHUNK * ROW_TILE, LANES), F32)
    return pl.pallas_call(
        _expert_kernel,
        grid_spec=pltpu.PrefetchScalarGridSpec(
            num_scalar_prefetch=5,
            grid=(chunks.shape[0],),
            in_specs=[any_spec, any_spec, any_spec, any_spec, any_spec],
            out_specs=any_spec,
            scratch_shapes=[pltpu.VMEM((X_SLOTS, TM_EXPERT * ROW_TILE, LANES), F32),
                            pltpu.VMEM((2, TM_EXPERT * ROW_TILE, LANES), F32),
                            pltpu.VMEM((2, D_MODEL, D_EXPERT), F32),
                            pltpu.VMEM((2, D_MODEL, D_EXPERT), F32),
                            pltpu.VMEM((2, D_EXPERT, D_MODEL), F32),
                            pltpu.VMEM((D_MODEL, D_EXPERT), BF16),
                            pltpu.VMEM((D_MODEL, D_EXPERT), BF16),
                            pltpu.VMEM((D_EXPERT, D_MODEL), BF16),
                            pltpu.SMEM((4,), jnp.int32),
                            pltpu.SemaphoreType.DMA((2,)),
                            pltpu.SemaphoreType.DMA((X_SLOTS,)),
                            pltpu.SemaphoreType.DMA((2,)),
                            pltpu.SemaphoreType.DMA]),
        out_shape=jax.ShapeDtypeStruct(xs.shape, F32),
        compiler_params=pltpu.CompilerParams(dimension_semantics=("arbitrary",),
                                             vmem_limit_bytes=VMEM_LIMIT),
        name="expert_mlp",
    )(tiles, chunk0, chunks, n_tiles, used_chunks, xs, wg, wu, wd, zeros)


def _combine_kernel(dest_ref, h_ref, rw_ref, fg_ref, y_ref, o_ref, buf, sems):
    tm = TM_COMBINE
    i = pl.program_id(0)
    n_steps = pl.num_programs(0)
    n = n_steps * tm
    cur = i % 2

    def fetch(step, half):
        def body(r, c):
            for s in range(2):
                pltpu.make_async_copy(_token_rows(y_ref, dest_ref[s * n + step * tm + r], 1),
                                      _token_rows(buf.at[half, s], r, 1),
                                      sems.at[half]).start(priority=s)
            return c

        lax.fori_loop(0, tm, body, 0, unroll=8)

    @pl.when(i == 0)
    def _():
        fetch(0, 0)

    @pl.when(i + 1 < n_steps)
    def _():
        fetch(i + 1, 1 - cur)

    for s in range(2):
        pltpu.make_async_copy(_token_rows(y_ref, 0, tm), buf.at[cur, s], sems.at[cur]).wait()
    rw = rw_ref[...]
    out = (h_ref[...] + rw[:, 0:1] * _tiles_to_rows(buf.at[cur, 0], tm)
           + rw[:, 1:2] * _tiles_to_rows(buf.at[cur, 1], tm))
    o_ref[...] = _rms(out, fg_ref[...])


def _combine(dest, h, rw, final_g, ys):
    n = h.shape[0]
    return pl.pallas_call(
        _combine_kernel,
        grid_spec=pltpu.PrefetchScalarGridSpec(
            num_scalar_prefetch=1,
            grid=(n // TM_COMBINE,),
            in_specs=[pl.BlockSpec((TM_COMBINE, D_MODEL), lambda i, d: (i, 0)),
                      pl.BlockSpec((TM_COMBINE, LANES), lambda i, d: (i, 0)),
                      pl.BlockSpec((1, D_MODEL), lambda i, d: (0, 0)),
                      pl.BlockSpec(memory_space=pl.ANY)],
            out_specs=pl.BlockSpec((TM_COMBINE, D_MODEL), lambda i, d: (i, 0)),
            scratch_shapes=[pltpu.VMEM((2, 2, TM_COMBINE * ROW_TILE, LANES), F32),
                            pltpu.SemaphoreType.DMA((2,))]),
        out_shape=jax.ShapeDtypeStruct((n, D_MODEL), F32),
        compiler_params=pltpu.CompilerParams(dimension_semantics=("arbitrary",),
                                             vmem_limit_bytes=VMEM_LIMIT),
        name="combine",
    )(dest, h, rw, final_g, ys)


def _schedule(counts, max_tiles):
    chunks = (counts + EXPERT_CHUNK - 1) // EXPERT_CHUNK
    chunk_end = jnp.cumsum(chunks)
    chunk_start = chunk_end - chunks
    tiles = (chunks + TILE_CHUNKS - 1) // TILE_CHUNKS
    tile_end = jnp.cumsum(tiles)
    tile = jnp.arange(max_tiles, dtype=jnp.int32)
    owner = jnp.sum(tile[:, None] >= tile_end[None, :], axis=1)
    is_owner = owner[:, None] == jnp.arange(N_EXPERTS, dtype=jnp.int32)[None, :]
    of_owner = lambda v: jnp.sum(jnp.where(is_owner, v[None, :], 0), axis=1)
    done = (tile - of_owner(tile_end - tiles)) * TILE_CHUNKS
    tile_chunk0 = (of_owner(chunk_start) + done).astype(jnp.int32)
    tile_chunks = jnp.clip(of_owner(chunks) - done, 0, TILE_CHUNKS).astype(jnp.int32)
    return tiles, chunk_start * EXPERT_CHUNK, tile_chunk0, tile_chunks, tile_end[-1:], chunk_end[-1:]


def _layer(x, attn_g, w_in, sg_g, w_sp, b_sp, sb_g, sg_out_g, w_out, ffn_g,
           w_rg, b_rg, w_re, b_re, w_gate, w_up, w_down):
    batch, seq, _ = x.shape
    n = batch * seq
    x2 = x.reshape(n, D_MODEL)
    row = lambda v: v.reshape(1, -1)

    bsp_full = jnp.repeat(b_sp.T, HEAD_DIM, axis=1)
    qkv, sgn = _inproj(x2, row(attn_g), w_in.astype(BF16), row(sg_g), w_sp, bsp_full, row(sg_out_g))
    sb = _attention(qkv, batch, seq).reshape(n, SB_WIDTH)

    pad_lanes = lambda v, width: jnp.pad(v, [(0, 0)] * (v.ndim - 1) + [(0, width - v.shape[-1])])
    w_r = jnp.concatenate([pad_lanes(w_rg, ROUTER_LANE0),
                           jnp.transpose(w_re, (1, 0, 2)).reshape(D_MODEL, N_EXPERTS)], axis=1)
    w_r = pad_lanes(w_r, LANES)
    wr_hi = w_r.astype(BF16)
    wr_lo = (w_r - wr_hi.astype(F32)).astype(BF16)
    wr2 = jnp.concatenate([wr_hi, wr_lo], axis=1)
    b_r = pad_lanes(jnp.concatenate([pad_lanes(b_rg, ROUTER_LANE0), b_re.reshape(-1)]), LANES)

    h, lg = _mix(sb, sgn, x2, row(sb_g), w_out.astype(BF16), row(ffn_g), wr2, row(b_r))
    ri, rw, cnt = _route(lg)

    counts = cnt[:, 0].astype(jnp.int32)
    n_rows = 2 * n + N_EXPERTS * EXPERT_CHUNK
    tiles, offsets, tile_chunk0, tile_chunks, n_tiles, used_chunks = _schedule(
        counts, 2 * n // TM_EXPERT + N_EXPERTS)
    expert, rank = ri[0:2], ri[2:4]
    is_e = expert[None] == jnp.arange(N_EXPERTS, dtype=jnp.int32)[:, None, None]
    dest = (jnp.sum(jnp.where(is_e, offsets[:, None, None], 0), axis=0) + rank).reshape(-1)
    pad_start = offsets + counts
    pad_count = (-counts) % EXPERT_CHUNK

    xs = _dispatch(dest, pad_start, pad_count, used_chunks, h, row(ffn_g), n_rows)
    ys = _experts(tiles, tile_chunk0, tile_chunks, n_tiles, used_chunks, xs,
                  w_gate.reshape(N_EXPERTS, D_MODEL, D_EXPERT),
                  w_up.reshape(N_EXPERTS, D_MODEL, D_EXPERT),
                  w_down.reshape(N_EXPERTS, D_EXPERT, D_MODEL))
    return dest, h, rw, ys


def kernel(x, attn_norm_g, w_in, sg_norm_g, w_spatial, b_spatial, sb_out_norm_g, sg_out_norm_g,
           w_out, ffn_norm_g, w_router_group, b_router_group, w_router_expert, b_router_expert,
           w_gate, w_up, w_down, final_norm_g):
    assert attn_norm_g.shape[0] == 1, "single-layer problem"
    batch, seq, _ = x.shape
    dest, h, rw, ys = _layer(x, attn_norm_g[0], w_in[0], sg_norm_g[0], w_spatial[0], b_spatial[0],
                             sb_out_norm_g[0], sg_out_norm_g[0], w_out[0], ffn_norm_g[0],
                             w_router_group[0], b_router_group[0], w_router_expert[0],
                             b_router_expert[0], w_gate[0], w_up[0], w_down[0])
    out = _combine(dest, h, rw, final_norm_g.reshape(1, -1), ys)
    return out.reshape(batch, seq, D_MODEL)
```

```python
import functools
import math

import jax
import jax.numpy as jnp
from jax import lax
from jax.experimental import pallas as pl
from jax.experimental.pallas import tpu as pltpu

D_MODEL = 1024
HEAD_DIM = 64
SB_WIDTH = 512
SG_WIDTH = 512
SG_HEADS = 8
D_IN = 3 * SB_WIDTH + 2 * SG_WIDTH
CHUNK = 128
N_GROUPS = 4
EXPERTS_PER_GROUP = 8
N_EXPERTS = N_GROUPS * EXPERTS_PER_GROUP
D_EXPERT = 512
EPS = 1e-6
F32_EXP_UNDERFLOW = 110.0

LANES = 128
SUBLANES = 8
ROW_TILE = D_MODEL // LANES
assert ROW_TILE == SUBLANES
HEAD_PAIR = 2 * HEAD_DIM
ROUTER_LANE0 = SUBLANES
ROUTER_ROWS = ROUTER_LANE0 + N_EXPERTS
assert EXPERTS_PER_GROUP == SUBLANES and N_GROUPS <= ROUTER_LANE0

TM_PROJ = 1024
TQ_ATTN = 256
ATTN_BLOCKS_PER_STEP = 2
ATTN_TOP_ROWS = 160
TM_MIX = 1024
TM_ROUTE = 1024
TM_DISPATCH = 1024
TM_EXPERT = 1024
EXPERT_CHUNK = 128
TM_COMBINE = 512
VMEM_LIMIT = 48 * 1024 * 1024

F32 = jnp.float32
BF16 = jnp.bfloat16


def _rms(x, g):
    return x * lax.rsqrt(jnp.mean(x * x, axis=-1, keepdims=True) + EPS) * g


def _gelu(x):
    c = math.sqrt(2.0 / math.pi)
    return x * (0.5 * (1.0 + jnp.tanh(c * (x + 0.044715 * (x * x * x)))))


def _softplus(z):
    return jnp.maximum(z, 0.0) + jnp.log(1.0 + jnp.exp(-jnp.abs(z)))


def _dot(a, b):
    return jnp.dot(a, b, preferred_element_type=F32)


def _rows_to_tiles(ref, x):
    m = x.shape[0]
    for k in range(ROW_TILE):
        ref[pl.ds(k, m, stride=ROW_TILE), :] = x[:, k * LANES:(k + 1) * LANES]


def _tiles_to_rows(ref, m):
    return jnp.concatenate([ref[pl.ds(k, m, stride=ROW_TILE), :] for k in range(ROW_TILE)], axis=1)


def _token_rows(ref, first_token, n_tokens):
    return ref.at[pl.ds(pl.multiple_of(first_token * ROW_TILE, ROW_TILE), n_tokens * ROW_TILE)]


def _split_bf16(x):
    hi = x.astype(BF16)
    lo = (x - hi.astype(F32)).astype(BF16)
    return hi, lo


def _inproj_kernel(x_ref, g_ref, w_ref, sgg_ref, wsp_ref, bsp_ref, sgog_ref, qkv_ref, sgn_ref,
                   gu_ref, vgn_ref, sg_ref):
    tm = TM_PROJ
    hb = _rms(x_ref[...], g_ref[...]).astype(BF16)
    gv = _gelu(_dot(hb, w_ref[:, 3 * SB_WIDTH + SG_WIDTH:D_IN]))
    vgn_ref[...] = _rms(gv, sgg_ref[...]).astype(BF16)
    gu_ref[...] = _gelu(_dot(hb, w_ref[:, 3 * SB_WIDTH:3 * SB_WIDTH + SG_WIDTH]))
    q = _dot(hb, w_ref[:, 0:SB_WIDTH]) * (1.0 / math.sqrt(HEAD_DIM))
    qkv_ref[:, 0:SB_WIDTH] = q.astype(BF16)
    qkv_ref[:, SB_WIDTH:2 * SB_WIDTH] = _dot(hb, w_ref[:, SB_WIDTH:2 * SB_WIDTH]).astype(BF16)

    lane = lax.broadcasted_iota(jnp.int32, (1, LANES), 1)
    first = lane < HEAD_DIM
    zero = jnp.zeros((), BF16)
    r_c = lax.broadcasted_iota(jnp.int32, (CHUNK, CHUNK), 0)
    c_c = lax.broadcasted_iota(jnp.int32, (CHUNK, CHUNK), 1)
    tril = r_c >= c_c
    n_pairs = SG_WIDTH // HEAD_PAIR
    w_pairs = []
    for p in range(n_pairs):
        w0 = jnp.where(tril, wsp_ref[2 * p], 0.0).astype(BF16)
        w1 = jnp.where(tril, wsp_ref[2 * p + 1], 0.0).astype(BF16)
        w_pairs.append(jnp.concatenate([w0, w1], axis=1))
    bsp = bsp_ref[...]
    for c in range(tm // CHUNK):
        rows = slice(c * CHUNK, (c + 1) * CHUNK)
        for p in range(n_pairs):
            cols = slice(p * HEAD_PAIR, (p + 1) * HEAD_PAIR)
            vg = vgn_ref[rows, cols]
            rhs = jnp.concatenate([jnp.where(first, vg, zero), jnp.where(first, zero, vg)], axis=0)
            mixed = _dot(w_pairs[p], rhs) + bsp[:, cols]
            sg_ref[rows, cols] = gu_ref[rows, cols] * mixed
    qkv_ref[:, 2 * SB_WIDTH:3 * SB_WIDTH] = _dot(hb, w_ref[:, 2 * SB_WIDTH:3 * SB_WIDTH]).astype(BF16)
    sgn_ref[...] = _rms(sg_ref[...], sgog_ref[...]).astype(BF16)


def _inproj(x2, attn_g, w_in_b, sg_g, wsp, bsp_full, sg_out_g):
    n = x2.shape[0]
    row = lambda i: (i, 0)
    const = lambda i: (0, 0)
    return pl.pallas_call(
        _inproj_kernel,
        grid=(n // TM_PROJ,),
        in_specs=[pl.BlockSpec((TM_PROJ, D_MODEL), row),
                  pl.BlockSpec((1, D_MODEL), const),
                  pl.BlockSpec((D_MODEL, D_IN), const),
                  pl.BlockSpec((1, SG_WIDTH), const),
                  pl.BlockSpec((SG_HEADS, CHUNK, CHUNK), lambda i: (0, 0, 0)),
                  pl.BlockSpec((CHUNK, SG_WIDTH), const),
                  pl.BlockSpec((1, SG_WIDTH), const)],
        out_specs=[pl.BlockSpec((TM_PROJ, 3 * SB_WIDTH), row),
                   pl.BlockSpec((TM_PROJ, SG_WIDTH), row)],
        out_shape=[jax.ShapeDtypeStruct((n, 3 * SB_WIDTH), BF16),
                   jax.ShapeDtypeStruct((n, SG_WIDTH), BF16)],
        scratch_shapes=[pltpu.VMEM((TM_PROJ, SG_WIDTH), F32),
                        pltpu.VMEM((TM_PROJ, SG_WIDTH), BF16),
                        pltpu.VMEM((TM_PROJ, SG_WIDTH), F32)],
        compiler_params=pltpu.CompilerParams(dimension_semantics=("arbitrary",),
                                             vmem_limit_bytes=VMEM_LIMIT),
        name="inproj",
    )(x2, attn_g, w_in_b, sg_g, wsp, bsp_full, sg_out_g)


def _attn_kernel(q_ref, k_ref, v_ref, o_ref, q2_ref, carry_ref):
    t = TQ_ATTN
    n_pairs = SB_WIDTH // HEAD_PAIR
    lane = lax.broadcasted_iota(jnp.int32, (1, HEAD_PAIR), 1)
    head_lanes = (lane < HEAD_DIM, lane >= HEAD_DIM)
    zero = jnp.zeros((), BF16)
    r_idx = lax.broadcasted_iota(jnp.int32, (t, t), 0)
    c_idx = lax.broadcasted_iota(jnp.int32, (t, t), 1)
    suffix = (r_idx > c_idx).astype(BF16)
    suffix2 = jnp.concatenate([suffix, suffix], axis=0)
    causal = c_idx < r_idx

    def one_query_block(sub, c):
        qi = pl.program_id(1) * ATTN_BLOCKS_PER_STEP + sub
        row0 = pl.multiple_of(sub * t, t)
        for p in range(n_pairs):
            qp = q_ref[0, pl.ds(row0, t), p * HEAD_PAIR:(p + 1) * HEAD_PAIR]
            for h in range(2):
                q2_ref[(2 * p + h) * t:(2 * p + h + 1) * t, :] = jnp.where(head_lanes[h], qp, zero)
        o_ref[0, pl.ds(row0, t), :] = jnp.zeros((t, SB_WIDTH), F32)
        carry_ref[...] = jnp.zeros_like(carry_ref)

        def block(j, diag, m):
            start = pl.multiple_of(j * t, t)
            mask2 = jnp.concatenate([causal, causal], axis=0) if diag else None
            st = [dict() for _ in range(n_pairs)]

            def head_rows(p):
                return [slice((2 * p + h) * t, (2 * p + h) * t + m) for h in range(2)]

            def scores(p):
                d = st[p]
                d["cols"] = slice(p * HEAD_PAIR, (p + 1) * HEAD_PAIR)
                kb = k_ref[0, pl.ds(start, t), d["cols"]]
                q2 = jnp.concatenate([q2_ref[r, :] for r in head_rows(p)], axis=0)
                z = lax.dot_general(q2, kb, (((1,), (1,)), ((), ())),
                                    preferred_element_type=F32)
                sp = _softplus(z)
                nl = jnp.where(mask2, sp, 0.0) if diag else sp
                hi, lo = _split_bf16(nl)
                d["hl"] = jnp.concatenate([hi, lo], axis=1)
                d["log_beta"] = z - sp
                d["nl0"] = nl[:, 0:1]

            def weights(p):
                d = st[p]
                hl = d["hl"]
                after = jnp.concatenate([_dot(hl[0:m], suffix2), _dot(hl[m:2 * m], suffix2)], axis=0)
                carry = jnp.concatenate([carry_ref[r, :] for r in head_rows(p)], axis=0)
                a = jnp.exp(d["log_beta"] - after - carry)
                if diag:
                    a = jnp.where(mask2, a, 0.0)
                a = a.astype(BF16)
                d["a2"] = jnp.concatenate([a[0:m], a[m:2 * m]], axis=1)
                new_carry = carry + after[:, 0:1] + d["nl0"]
                for h, r in enumerate(head_rows(p)):
                    carry_ref[r, :] = new_carry[h * m:(h + 1) * m]

            def values(p):
                d = st[p]
                vb = v_ref[0, pl.ds(start, t), d["cols"]]
                v2 = jnp.concatenate([jnp.where(head_lanes[0], vb, zero),
                                      jnp.where(head_lanes[1], vb, zero)], axis=0)
                o_ref[0, pl.ds(row0, m), d["cols"]] += _dot(d["a2"], v2)

            for step in range(n_pairs + 2):
                if step < n_pairs:
                    scores(step)
                if 0 <= step - 1 < n_pairs:
                    weights(step - 1)
                if 0 <= step - 2 < n_pairs:
                    values(step - 2)

        top = ATTN_TOP_ROWS

        def flags():
            bottom = jnp.concatenate([carry_ref[hh * t + top:(hh + 1) * t, :] for hh in range(2 * n_pairs)], axis=0)
            return (jnp.min(carry_ref[...]) < F32_EXP_UNDERFLOW, jnp.min(bottom) >= F32_EXP_UNDERFLOW)

        block(qi, True, t)

        def body(state):
            it, _, bottom_done = state
            j = qi - 1 - it

            @pl.when(bottom_done)
            def _():
                block(j, False, top)

            @pl.when(jnp.logical_not(bottom_done))
            def _():
                block(j, False, t)

            return (it + 1,) + flags()

        lax.while_loop(lambda s: (s[0] < qi) & s[1], body, (jnp.int32(0),) + flags())
        return c

    lax.fori_loop(0, ATTN_BLOCKS_PER_STEP, one_query_block, 0)


def _attention(qkv, batch, seq):
    qkv3 = qkv.reshape(batch, seq, 3 * SB_WIDTH)
    n_heads = SB_WIDTH // HEAD_DIM
    return pl.pallas_call(
        _attn_kernel,
        grid=(batch, seq // (ATTN_BLOCKS_PER_STEP * TQ_ATTN)),
        in_specs=[pl.BlockSpec((1, ATTN_BLOCKS_PER_STEP * TQ_ATTN, SB_WIDTH), lambda b, i: (b, i, 0)),
                  pl.BlockSpec((1, seq, SB_WIDTH), lambda b, i: (b, 0, 1)),
                  pl.BlockSpec((1, seq, SB_WIDTH), lambda b, i: (b, 0, 2))],
        out_specs=pl.BlockSpec((1, ATTN_BLOCKS_PER_STEP * TQ_ATTN, SB_WIDTH), lambda b, i: (b, i, 0)),
        out_shape=jax.ShapeDtypeStruct((batch, seq, SB_WIDTH), F32),
        scratch_shapes=[pltpu.VMEM((n_heads * TQ_ATTN, HEAD_PAIR), BF16),
                        pltpu.VMEM((n_heads * TQ_ATTN, 1), F32)],
        compiler_params=pltpu.CompilerParams(dimension_semantics=("arbitrary",) * 2,
                                             vmem_limit_bytes=VMEM_LIMIT),
        name="sb_attention",
    )(qkv3, qkv3, qkv3)


def _mix_kernel(sb_ref, sgn_ref, x_ref, sbg_ref, wout_ref, ffng_ref, wr2_ref, br_ref,
                h_ref, lg_ref):
    sbn = _rms(sb_ref[...], sbg_ref[...]).astype(BF16)
    h = x_ref[...] + _dot(sbn, wout_ref[0:SB_WIDTH, :]) + _dot(sgn_ref[...], wout_ref[SB_WIDTH:, :])
    h_ref[...] = h
    hn = _rms(h, ffng_ref[...])

    hn_hi, hn_lo = _split_bf16(hn)
    both = _dot(hn_hi, wr2_ref[...])
    logits = both[:, 0:LANES] + both[:, LANES:] + _dot(hn_lo, wr2_ref[:, 0:LANES]) + br_ref[...]
    lg_ref[...] = logits.T[0:ROUTER_ROWS, :]


def _route_kernel(lg_ref, ri_ref, rw_ref, cnt_ref, count_ref):
    tr = TM_ROUTE
    i = pl.program_id(0)

    @pl.when(i == 0)
    def _():
        count_ref[...] = jnp.zeros_like(count_ref)

    neg = jnp.float32(-jnp.inf)
    row8 = lax.broadcasted_iota(jnp.int32, (SUBLANES, tr), 0)

    def top(v):
        m = jnp.max(v, axis=0, keepdims=True)
        return m, jnp.min(jnp.where(v == m, row8, SUBLANES), axis=0, keepdims=True)

    def group_rows(g):
        return lg_ref[ROUTER_LANE0 + g * EXPERTS_PER_GROUP:ROUTER_LANE0 + (g + 1) * EXPERTS_PER_GROUP, :]

    gl = jnp.where(row8 < N_GROUPS, lg_ref[0:SUBLANES, :], neg)
    gmax, gidx = top(gl)
    gweight = 1.0 / jnp.sum(jnp.exp(gl - gmax), axis=0, keepdims=True)
    el = group_rows(0)
    for g in range(1, N_GROUPS):
        el = jnp.where(gidx == g, group_rows(g), el)
    m1, i1 = top(el)
    m2, i2 = top(jnp.where(row8 == i1, neg, el))
    t21 = jnp.exp(m2 - m1)
    w1 = gweight / (1.0 + t21)
    w2 = gweight * t21 / (1.0 + t21)
    e1 = gidx * EXPERTS_PER_GROUP + i1
    e2 = gidx * EXPERTS_PER_GROUP + i2

    row_e = lax.broadcasted_iota(jnp.int32, (N_EXPERTS, tr), 0)
    sel1 = row_e == e1
    sel2 = row_e == e2
    onehot = jnp.where(sel1 | sel2, 1.0, 0.0)
    r_t = lax.broadcasted_iota(jnp.int32, (tr, tr), 0)
    c_t = lax.broadcasted_iota(jnp.int32, (tr, tr), 1)
    before = (r_t < c_t).astype(BF16)
    running = count_ref[:, 0:1] + _dot(onehot.astype(BF16), before)
    rank1 = jnp.sum(jnp.where(sel1, running, 0.0), axis=0, keepdims=True)
    rank2 = jnp.sum(jnp.where(sel2, running, 0.0), axis=0, keepdims=True)
    new_count = count_ref[:, 0:1] + jnp.sum(onehot, axis=1, keepdims=True)
    count_ref[...] = jnp.broadcast_to(new_count, count_ref.shape)
    cnt_ref[...] = jnp.broadcast_to(new_count, cnt_ref.shape)

    ri_ref[...] = jnp.where(row8 == 0, e1, jnp.where(row8 == 1, e2, jnp.where(
        row8 == 2, rank1.astype(jnp.int32), jnp.where(row8 == 3, rank2.astype(jnp.int32), 0))))
    row128 = lax.broadcasted_iota(jnp.int32, (LANES, tr), 0)
    rw_ref[...] = jnp.where(row128 == 0, w1, jnp.where(row128 == 1, w2, 0.0)).T


def _route(lg):
    n = lg.shape[1]
    return pl.pallas_call(
        _route_kernel,
        grid=(n // TM_ROUTE,),
        in_specs=[pl.BlockSpec((ROUTER_ROWS, TM_ROUTE), lambda i: (0, i))],
        out_specs=[pl.BlockSpec((SUBLANES, TM_ROUTE), lambda i: (0, i)),
                   pl.BlockSpec((TM_ROUTE, LANES), lambda i: (i, 0)),
                   pl.BlockSpec((N_EXPERTS, LANES), lambda i: (0, 0))],
        out_shape=[jax.ShapeDtypeStruct((SUBLANES, n), jnp.int32),
                   jax.ShapeDtypeStruct((n, LANES), F32),
                   jax.ShapeDtypeStruct((N_EXPERTS, LANES), F32)],
        scratch_shapes=[pltpu.VMEM((N_EXPERTS, LANES), F32)],
        compiler_params=pltpu.CompilerParams(dimension_semantics=("arbitrary",),
                                             vmem_limit_bytes=VMEM_LIMIT),
        name="route",
    )(lg)


def _mix(sb, sgn, x2, sb_g, w_out_b, ffn_g, wr2, br):
    n = x2.shape[0]
    row = lambda i: (i, 0)
    const = lambda i: (0, 0)
    return pl.pallas_call(
        _mix_kernel,
        grid=(n // TM_MIX,),
        in_specs=[pl.BlockSpec((TM_MIX, SB_WIDTH), row),
                  pl.BlockSpec((TM_MIX, SG_WIDTH), row),
                  pl.BlockSpec((TM_MIX, D_MODEL), row),
                  pl.BlockSpec((1, SB_WIDTH), const),
                  pl.BlockSpec((D_MODEL, D_MODEL), const),
                  pl.BlockSpec((1, D_MODEL), const),
                  pl.BlockSpec((D_MODEL, 2 * LANES), const),
                  pl.BlockSpec((1, LANES), const)],
        out_specs=[pl.BlockSpec((TM_MIX, D_MODEL), row),
                   pl.BlockSpec((ROUTER_ROWS, TM_MIX), lambda i: (0, i))],
        out_shape=[jax.ShapeDtypeStruct((n, D_MODEL), F32),
                   jax.ShapeDtypeStruct((ROUTER_ROWS, n), F32)],
        compiler_params=pltpu.CompilerParams(dimension_semantics=("arbitrary",),
                                             vmem_limit_bytes=VMEM_LIMIT),
        name="mix_router",
    )(sb, sgn, x2, sb_g, w_out_b, ffn_g, wr2, br)


_PAD_BITS = tuple(1 << b for b in reversed(range(EXPERT_CHUNK.bit_length() - 1)))


def _dispatch_kernel(dest_ref, pad_start_ref, pad_count_ref, used_ref, h_ref, g_ref, zeros_ref, xs_ref,
                     hn_ref, sem, zsem):
    tm = TM_DISPATCH
    i = pl.program_id(0)
    n_steps = pl.num_programs(0) - 1
    n = n_steps * tm
    base = (i - 1) * tm
    prev = hn_ref.at[lax.rem(i + 1, 2)]
    n_chunks = xs_ref.shape[0] // (EXPERT_CHUNK * ROW_TILE)

    def pad_copies(do):
        for e in range(N_EXPERTS):
            start = pad_start_ref[e]
            count = pad_count_ref[e]
            for bit in _PAD_BITS:
                @pl.when((count & bit) != 0)
                def _(start=start, bit=bit):
                    do(pltpu.make_async_copy(_token_rows(zeros_ref, 0, bit),
                                             _token_rows(xs_ref, start, bit), zsem))
                start = start + (count & bit)
        for k in range(N_EXPERTS):
            chunk = used_ref[0] + k

            @pl.when(chunk < n_chunks)
            def _(chunk=chunk):
                do(pltpu.make_async_copy(zeros_ref, _token_rows(xs_ref, chunk * EXPERT_CHUNK, EXPERT_CHUNK),
                                         zsem))

    @pl.when(i == 0)
    def _():
        pad_copies(lambda cp: cp.start())

    @pl.when(i > 0)
    def _():
        def body(r, c):
            src = _token_rows(prev, r, 1)
            for s in range(2):
                pltpu.make_async_copy(src, _token_rows(xs_ref, dest_ref[s * n + base + r], 1),
                                      sem).start(priority=s)
            return c

        lax.fori_loop(0, tm, body, 0, unroll=8)

    @pl.when(i < n_steps)
    def _():
        _rows_to_tiles(hn_ref.at[lax.rem(i, 2)], _rms(h_ref[...], g_ref[...]))

    @pl.when(i > 0)
    def _():
        for _ in range(2):
            pltpu.make_async_copy(prev, _token_rows(xs_ref, 0, tm), sem).wait()

    @pl.when(i == n_steps)
    def _():
        pad_copies(lambda cp: cp.wait())


def _dispatch(dest, pad_start, pad_count, used_chunks, h, ffn_g, n_rows):
    n_steps = h.shape[0] // TM_DISPATCH
    zeros = jnp.zeros((EXPERT_CHUNK * ROW_TILE, LANES), F32)
    return pl.pallas_call(
        _dispatch_kernel,
        grid_spec=pltpu.PrefetchScalarGridSpec(
            num_scalar_prefetch=4,
            grid=(n_steps + 1,),
            in_specs=[pl.BlockSpec((TM_DISPATCH, D_MODEL), lambda i, *_: (jnp.minimum(i, n_steps - 1), 0)),
                      pl.BlockSpec((1, D_MODEL), lambda i, *_: (0, 0)),
                      pl.BlockSpec(memory_space=pl.ANY)],
            out_specs=pl.BlockSpec(memory_space=pl.ANY),
            scratch_shapes=[pltpu.VMEM((2, TM_DISPATCH * ROW_TILE, LANES), F32),
                            pltpu.SemaphoreType.DMA, pltpu.SemaphoreType.DMA]),
        out_shape=jax.ShapeDtypeStruct((n_rows * ROW_TILE, LANES), F32),
        compiler_params=pltpu.CompilerParams(dimension_semantics=("arbitrary",),
                                             vmem_limit_bytes=VMEM_LIMIT),
        name="dispatch",
    )(dest, pad_start, pad_count, used_chunks, h, ffn_g, zeros)


X_SLOTS = 3
TILE_CHUNKS = TM_EXPERT // EXPERT_CHUNK


def _expert_kernel(tiles_ref, chunk0_ref, chunks_ref, nt_ref, used_ref, xs_ref, wg_ref, wu_ref, wd_ref,
                   zeros_ref, ys_ref, x_buf, y_buf, sg_buf, su_buf, sd_buf, wgb, wub, wdb, state,
                   w_sems, x_sems, y_sems, zsem):
    t = pl.program_id(0)
    last = pl.num_programs(0) - 1
    nt = nt_ref[0]
    n_chunks = ys_ref.shape[0] // (EXPERT_CHUNK * ROW_TILE)

    def tile_copies(tile, do, out):
        for c in range(TILE_CHUNKS):
            @pl.when(c < chunks_ref[tile])
            def _(c=c):
                first = (chunk0_ref[tile] + c) * EXPERT_CHUNK
                if out:
                    slot = lax.rem(tile, 2)
                    do(pltpu.make_async_copy(_token_rows(y_buf.at[slot], c * EXPERT_CHUNK, EXPERT_CHUNK),
                                             _token_rows(ys_ref, first, EXPERT_CHUNK), y_sems.at[slot]))
                else:
                    slot = lax.rem(tile, X_SLOTS)
                    do(pltpu.make_async_copy(_token_rows(xs_ref, first, EXPERT_CHUNK),
                                             _token_rows(x_buf.at[slot], c * EXPERT_CHUNK, EXPERT_CHUNK),
                                             x_sems.at[slot]))

    start = lambda cp: cp.start()
    wait = lambda cp: cp.wait()

    def tail_copies(do):
        for k in range(N_EXPERTS):
            chunk = used_ref[0] + k

            @pl.when(chunk < n_chunks)
            def _(chunk=chunk):
                do(pltpu.make_async_copy(zeros_ref, _token_rows(ys_ref, chunk * EXPERT_CHUNK, EXPERT_CHUNK),
                                         zsem))

    def weight_copies(e, slot):
        return (pltpu.make_async_copy(wg_ref.at[e], sg_buf.at[slot], w_sems.at[slot]),
                pltpu.make_async_copy(wu_ref.at[e], su_buf.at[slot], w_sems.at[slot]),
                pltpu.make_async_copy(wd_ref.at[e], sd_buf.at[slot], w_sems.at[slot]))

    def next_with_rows(e):
        return lax.while_loop(lambda k: (k < N_EXPERTS) & (tiles_ref[jnp.minimum(k, N_EXPERTS - 1)] == 0),
                              lambda k: k + 1, e + 1)

    @pl.when(t == 0)
    def _():
        first = next_with_rows(jnp.int32(-1))
        state[0] = jnp.int32(-1)
        state[1] = jnp.int32(0)
        state[2] = jnp.int32(1)
        state[3] = first
        for cp in weight_copies(first, 0):
            cp.start()
        tile_copies(0, start, False)

        @pl.when(nt > 1)
        def _():
            tile_copies(1, start, False)

        tail_copies(start)

    @pl.when(t + 2 < nt)
    def _():
        tile_copies(t + 2, start, False)

    @pl.when(t < nt)
    def _():
        @pl.when(state[1] == 0)
        def _():
            e = state[3]
            slot = 1 - state[2]
            nxt = next_with_rows(e)
            state[0] = e
            state[1] = tiles_ref[e]
            state[2] = slot
            state[3] = nxt
            for cp in weight_copies(e, slot):
                cp.wait()

            @pl.when(nxt < N_EXPERTS)
            def _():
                for cp in weight_copies(nxt, 1 - slot):
                    cp.start()

            wgb[...] = sg_buf[slot].astype(BF16)
            wub[...] = su_buf[slot].astype(BF16)
            wdb[...] = sd_buf[slot].astype(BF16)

        state[1] = state[1] - 1
        tile_copies(t, wait, False)

        @pl.when(t >= 2)
        def _():
            tile_copies(t - 2, wait, True)

        for n_chunks_here in range(1, TILE_CHUNKS + 1):
            @pl.when(chunks_ref[t] == n_chunks_here)
            def _(m=n_chunks_here * EXPERT_CHUNK):
                x = _tiles_to_rows(x_buf.at[lax.rem(t, X_SLOTS)], m).astype(BF16)
                g = _dot(x, wgb[...])
                u = _dot(x, wub[...])
                hidden = (g * jax.nn.sigmoid(g)) * u
                _rows_to_tiles(y_buf.at[lax.rem(t, 2)], _dot(hidden.astype(BF16), wdb[...]))

        tile_copies(t, start, True)

    @pl.when(t == last)
    def _():
        for back in (2, 1):
            @pl.when(nt >= back)
            def _(back=back):
                tile_copies(nt - back, wait, True)

        tail_copies(wait)


def _experts(tiles, chunk0, chunks, n_tiles, used_chunks, xs, wg, wu, wd):
    any_spec = pl.BlockSpec(memory_space=pl.ANY)
    zeros = jnp.zeros((EXPERT_CHUNK * ROW_TILE, LANES), F32)
    return pl.pallas_call(
        _expert_kernel,
        grid_spec=pltpu.PrefetchScalarGridSpec(
            num_scalar_prefetch=5,
            grid=(chunks.shape[0],),
            in_specs=[any_spec, any_spec, any_spec, any_spec, any_spec],
            out_specs=any_spec,
            scratch_shapes=[pltpu.VMEM((X_SLOTS, TM_EXPERT * ROW_TILE, LANES), F32),
                            pltpu.VMEM((2, TM_EXPERT * ROW_TILE, LANES), F32),
                            pltpu.VMEM((2, D_MODEL, D_EXPERT), F32),
                            pltpu.VMEM((2, D_MODEL, D_EXPERT), F32),
                            pltpu.VMEM((2, D_EXPERT, D_MODEL), F32),
                            pltpu.VMEM((D_MODEL, D_EXPERT), BF16),
                            pltpu.VMEM((D_MODEL, D_EXPERT), BF16),
                            pltpu.VMEM((D_EXPERT, D_MODEL), BF16),
                            pltpu.SMEM((4,), jnp.int32),
                            pltpu.SemaphoreType.DMA((2,)),
                            pltpu.SemaphoreType.DMA((X_SLOTS,)),
                            pltpu.SemaphoreType.DMA((2,)),
                            pltpu.SemaphoreType.DMA]),
        out_shape=jax.ShapeDtypeStruct(xs.shape, F32),
        compiler_params=pltpu.CompilerParams(dimension_semantics=("arbitrary",),
                                             vmem_limit_bytes=VMEM_LIMIT),
        name="expert_mlp",
    )(tiles, chunk0, chunks, n_tiles, used_chunks, xs, wg, wu, wd, zeros)


def _combine_kernel(dest_ref, h_ref, rw_ref, fg_ref, y_ref, o_ref, buf, sems):
    tm = TM_COMBINE
    i = pl.program_id(0)
    n_steps = pl.num_programs(0)
    n = n_steps * tm
    cur = i % 2

    def fetch(step, half):
        def body(r, c):
            for s in range(2):
                pltpu.make_async_copy(_token_rows(y_ref, dest_ref[s * n + step * tm + r], 1),
                                      _token_rows(buf.at[half, s], r, 1),
                                      sems.at[half]).start(priority=s)
            return c

        lax.fori_loop(0, tm, body, 0, unroll=8)

    @pl.when(i == 0)
    def _():
        fetch(0, 0)

    @pl.when(i + 1 < n_steps)
    def _():
        fetch(i + 1, 1 - cur)

    for s in range(2):
        pltpu.make_async_copy(_token_rows(y_ref, 0, tm), buf.at[cur, s], sems.at[cur]).wait()
    rw = rw_ref[...]
    out = (h_ref[...] + rw[:, 0:1] * _tiles_to_rows(buf.at[cur, 0], tm)
           + rw[:, 1:2] * _tiles_to_rows(buf.at[cur, 1], tm))
    o_ref[...] = _rms(out, fg_ref[...])


def _combine(dest, h, rw, final_g, ys):
    n = h.shape[0]
    return pl.pallas_call(
        _combine_kernel,
        grid_spec=pltpu.PrefetchScalarGridSpec(
            num_scalar_prefetch=1,
            grid=(n // TM_COMBINE,),
            in_specs=[pl.BlockSpec((TM_COMBINE, D_MODEL), lambda i, d: (i, 0)),
                      pl.BlockSpec((TM_COMBINE, LANES), lambda i, d: (i, 0)),
                      pl.BlockSpec((1, D_MODEL), lambda i, d: (0, 0)),
                      pl.BlockSpec(memory_space=pl.ANY)],
            out_specs=pl.BlockSpec((TM_COMBINE, D_MODEL), lambda i, d: (i, 0)),
            scratch_shapes=[pltpu.VMEM((2, 2, TM_COMBINE * ROW_TILE, LANES), F32),
                            pltpu.SemaphoreType.DMA((2,))]),
        out_shape=jax.ShapeDtypeStruct((n, D_MODEL), F32),
        compiler_params=pltpu.CompilerParams(dimension_semantics=("arbitrary",),
                                             vmem_limit_bytes=VMEM_LIMIT),
        name="combine",
    )(dest, h, rw, final_g, ys)


def _schedule(counts, max_tiles):
    chunks = (counts + EXPERT_CHUNK - 1) // EXPERT_CHUNK
    chunk_end = jnp.cumsum(chunks)
    chunk_start = chunk_end - chunks
    tiles = (chunks + TILE_CHUNKS - 1) // TILE_CHUNKS
    tile_end = jnp.cumsum(tiles)
    tile = jnp.arange(max_tiles, dtype=jnp.int32)
    owner = jnp.sum(tile[:, None] >= tile_end[None, :], axis=1)
    is_owner = owner[:, None] == jnp.arange(N_EXPERTS, dtype=jnp.int32)[None, :]
    of_owner = lambda v: jnp.sum(jnp.where(is_owner, v[None, :], 0), axis=1)
    done = (tile - of_owner(tile_end - tiles)) * TILE_CHUNKS
    tile_chunk0 = (of_owner(chunk_start) + done).astype(jnp.int32)
    tile_chunks = jnp.clip(of_owner(chunks) - done, 0, TILE_CHUNKS).astype(jnp.int32)
    return tiles, chunk_start * EXPERT_CHUNK, tile_chunk0, tile_chunks, tile_end[-1:], chunk_end[-1:]


def _layer(x, attn_g, w_in, sg_g, w_sp, b_sp, sb_g, sg_out_g, w_out, ffn_g,
           w_rg, b_rg, w_re, b_re, w_gate, w_up, w_down):
    batch, seq, _ = x.shape
    n = batch * seq
    x2 = x.reshape(n, D_MODEL)
    row = lambda v: v.reshape(1, -1)

    bsp_full = jnp.repeat(b_sp.T, HEAD_DIM, axis=1)
    qkv, sgn = _inproj(x2, row(attn_g), w_in.astype(BF16), row(sg_g), w_sp, bsp_full, row(sg_out_g))
    sb = _attention(qkv, batch, seq).reshape(n, SB_WIDTH)

    pad_lanes = lambda v, width: jnp.pad(v, [(0, 0)] * (v.ndim - 1) + [(0, width - v.shape[-1])])
    w_r = jnp.concatenate([pad_lanes(w_rg, ROUTER_LANE0),
                           jnp.transpose(w_re, (1, 0, 2)).reshape(D_MODEL, N_EXPERTS)], axis=1)
    w_r = pad_lanes(w_r, LANES)
    wr_hi = w_r.astype(BF16)
    wr_lo = (w_r - wr_hi.astype(F32)).astype(BF16)
    wr2 = jnp.concatenate([wr_hi, wr_lo], axis=1)
    b_r = pad_lanes(jnp.concatenate([pad_lanes(b_rg, ROUTER_LANE0), b_re.reshape(-1)]), LANES)

    h, lg = _mix(sb, sgn, x2, row(sb_g), w_out.astype(BF16), row(ffn_g), wr2, row(b_r))
    ri, rw, cnt = _route(lg)

    counts = cnt[:, 0].astype(jnp.int32)
    n_rows = 2 * n + N_EXPERTS * EXPERT_CHUNK
    tiles, offsets, tile_chunk0, tile_chunks, n_tiles, used_chunks = _schedule(
        counts, 2 * n // TM_EXPERT + N_EXPERTS)
    expert, rank = ri[0:2], ri[2:4]
    is_e = expert[None] == jnp.arange(N_EXPERTS, dtype=jnp.int32)[:, None, None]
    dest = (jnp.sum(jnp.where(is_e, offsets[:, None, None], 0), axis=0) + rank).reshape(-1)
    pad_start = offsets + counts
    pad_count = (-counts) % EXPERT_CHUNK

    xs = _dispatch(dest, pad_start, pad_count, used_chunks, h, row(ffn_g), n_rows)
    ys = _experts(tiles, tile_chunk0, tile_chunks, n_tiles, used_chunks, xs,
                  w_gate.reshape(N_EXPERTS, D_MODEL, D_EXPERT),
                  w_up.reshape(N_EXPERTS, D_MODEL, D_EXPERT),
                  w_down.reshape(N_EXPERTS, D_EXPERT, D_MODEL))
    return dest, h, rw, ys


def kernel(x, attn_norm_g, w_in, sg_norm_g, w_spatial, b_spatial, sb_out_norm_g, sg_out_norm_g,
           w_out, ffn_norm_g, w_router_group, b_router_group, w_router_expert, b_router_expert,
           w_gate, w_up, w_down, final_norm_g):
    assert attn_norm_g.shape[0] == 1, "single-layer problem"
    batch, seq, _ = x.shape
    dest, h, rw, ys = _layer(x, attn_norm_g[0], w_in[0], sg_norm_g[0], w_spatial[0], b_spatial[0],
                             sb_out_norm_g[0], sg_out_norm_g[0], w_out[0], ffn_norm_g[0],
                             w_router_group[0], b_router_group[0], w_router_expert[0],
                             b_router_expert[0], w_gate[0], w_up[0], w_down[0])
    out = _combine(dest, h, rw, final_norm_g.reshape(1, -1), ys)
    return out.reshape(batch, seq, D_MODEL)
```

```python
import functools
import math

import jax
import jax.numpy as jnp
from jax import lax
from jax.experimental import pallas as pl
from jax.experimental.pallas import tpu as pltpu

D_MODEL = 1024
HEAD_DIM = 64
SB_WIDTH = 512
SG_WIDTH = 512
SG_HEADS = 8
D_IN = 3 * SB_WIDTH + 2 * SG_WIDTH
CHUNK = 128
N_GROUPS = 4
EXPERTS_PER_GROUP = 8
N_EXPERTS = N_GROUPS * EXPERTS_PER_GROUP
D_EXPERT = 512
EPS = 1e-6
F32_EXP_UNDERFLOW = 110.0

LANES = 128
SUBLANES = 8
ROW_TILE = D_MODEL // LANES
assert ROW_TILE == SUBLANES
HEAD_PAIR = 2 * HEAD_DIM
ROUTER_LANE0 = SUBLANES
ROUTER_ROWS = ROUTER_LANE0 + N_EXPERTS
assert EXPERTS_PER_GROUP == SUBLANES and N_GROUPS <= ROUTER_LANE0

TM_PROJ = 1024
TQ_ATTN = 256
ATTN_BLOCKS_PER_STEP = 2
ATTN_TOP_ROWS = 160
TM_MIX = 1024
TM_ROUTE = 1024
TM_DISPATCH = 1024
TM_EXPERT = 512
EXPERT_CHUNK = 128
TM_COMBINE = 512
VMEM_LIMIT = 48 * 1024 * 1024

F32 = jnp.float32
BF16 = jnp.bfloat16


def _rms(x, g):
    return x * lax.rsqrt(jnp.mean(x * x, axis=-1, keepdims=True) + EPS) * g


def _gelu(x):
    c = math.sqrt(2.0 / math.pi)
    return x * (0.5 * (1.0 + jnp.tanh(c * (x + 0.044715 * (x * x * x)))))


def _softplus(z):
    return jnp.maximum(z, 0.0) + jnp.log(1.0 + jnp.exp(-jnp.abs(z)))


def _dot(a, b):
    return jnp.dot(a, b, preferred_element_type=F32)


def _rows_to_tiles(ref, x):
    m = x.shape[0]
    for k in range(ROW_TILE):
        ref[pl.ds(k, m, stride=ROW_TILE), :] = x[:, k * LANES:(k + 1) * LANES]


def _tiles_to_rows(ref, m):
    return jnp.concatenate([ref[pl.ds(k, m, stride=ROW_TILE), :] for k in range(ROW_TILE)], axis=1)


def _token_rows(ref, first_token, n_tokens):
    return ref.at[pl.ds(pl.multiple_of(first_token * ROW_TILE, ROW_TILE), n_tokens * ROW_TILE)]


def _split_bf16(x):
    hi = x.astype(BF16)
    lo = (x - hi.astype(F32)).astype(BF16)
    return hi, lo


def _inproj_kernel(x_ref, g_ref, w_ref, sgg_ref, wsp_ref, bsp_ref, sgog_ref, qkv_ref, sgn_ref,
                   gu_ref, vgn_ref, sg_ref):
    tm = TM_PROJ
    hb = _rms(x_ref[...], g_ref[...]).astype(BF16)
    gv = _gelu(_dot(hb, w_ref[:, 3 * SB_WIDTH + SG_WIDTH:D_IN]))
    vgn_ref[...] = _rms(gv, sgg_ref[...]).astype(BF16)
    gu_ref[...] = _gelu(_dot(hb, w_ref[:, 3 * SB_WIDTH:3 * SB_WIDTH + SG_WIDTH]))
    q = _dot(hb, w_ref[:, 0:SB_WIDTH]) * (1.0 / math.sqrt(HEAD_DIM))
    qkv_ref[:, 0:SB_WIDTH] = q.astype(BF16)
    qkv_ref[:, SB_WIDTH:2 * SB_WIDTH] = _dot(hb, w_ref[:, SB_WIDTH:2 * SB_WIDTH]).astype(BF16)

    lane = lax.broadcasted_iota(jnp.int32, (1, LANES), 1)
    first = lane < HEAD_DIM
    zero = jnp.zeros((), BF16)
    r_c = lax.broadcasted_iota(jnp.int32, (CHUNK, CHUNK), 0)
    c_c = lax.broadcasted_iota(jnp.int32, (CHUNK, CHUNK), 1)
    tril = r_c >= c_c
    n_pairs = SG_WIDTH // HEAD_PAIR
    w_pairs = []
    for p in range(n_pairs):
        w0 = jnp.where(tril, wsp_ref[2 * p], 0.0).astype(BF16)
        w1 = jnp.where(tril, wsp_ref[2 * p + 1], 0.0).astype(BF16)
        w_pairs.append(jnp.concatenate([w0, w1], axis=1))
    bsp = bsp_ref[...]
    for c in range(tm // CHUNK):
        rows = slice(c * CHUNK, (c + 1) * CHUNK)
        for p in range(n_pairs):
            cols = slice(p * HEAD_PAIR, (p + 1) * HEAD_PAIR)
            vg = vgn_ref[rows, cols]
            rhs = jnp.concatenate([jnp.where(first, vg, zero), jnp.where(first, zero, vg)], axis=0)
            mixed = _dot(w_pairs[p], rhs) + bsp[:, cols]
            sg_ref[rows, cols] = gu_ref[rows, cols] * mixed
    qkv_ref[:, 2 * SB_WIDTH:3 * SB_WIDTH] = _dot(hb, w_ref[:, 2 * SB_WIDTH:3 * SB_WIDTH]).astype(BF16)
    sgn_ref[...] = _rms(sg_ref[...], sgog_ref[...]).astype(BF16)


def _inproj(x2, attn_g, w_in_b, sg_g, wsp, bsp_full, sg_out_g):
    n = x2.shape[0]
    row = lambda i: (i, 0)
    const = lambda i: (0, 0)
    return pl.pallas_call(
        _inproj_kernel,
        grid=(n // TM_PROJ,),
        in_specs=[pl.BlockSpec((TM_PROJ, D_MODEL), row),
                  pl.BlockSpec((1, D_MODEL), const),
                  pl.BlockSpec((D_MODEL, D_IN), const),
                  pl.BlockSpec((1, SG_WIDTH), const),
                  pl.BlockSpec((SG_HEADS, CHUNK, CHUNK), lambda i: (0, 0, 0)),
                  pl.BlockSpec((CHUNK, SG_WIDTH), const),
                  pl.BlockSpec((1, SG_WIDTH), const)],
        out_specs=[pl.BlockSpec((TM_PROJ, 3 * SB_WIDTH), row),
                   pl.BlockSpec((TM_PROJ, SG_WIDTH), row)],
        out_shape=[jax.ShapeDtypeStruct((n, 3 * SB_WIDTH), BF16),
                   jax.ShapeDtypeStruct((n, SG_WIDTH), BF16)],
        scratch_shapes=[pltpu.VMEM((TM_PROJ, SG_WIDTH), F32),
                        pltpu.VMEM((TM_PROJ, SG_WIDTH), BF16),
                        pltpu.VMEM((TM_PROJ, SG_WIDTH), F32)],
        compiler_params=pltpu.CompilerParams(dimension_semantics=("arbitrary",),
                                             vmem_limit_bytes=VMEM_LIMIT),
        name="inproj",
    )(x2, attn_g, w_in_b, sg_g, wsp, bsp_full, sg_out_g)


def _attn_kernel(q_ref, k_ref, v_ref, g_ref, o_ref, q2_ref, carry_ref, acc_ref):
    t = TQ_ATTN
    n_pairs = SB_WIDTH // HEAD_PAIR
    lane = lax.broadcasted_iota(jnp.int32, (1, HEAD_PAIR), 1)
    head_lanes = (lane < HEAD_DIM, lane >= HEAD_DIM)
    zero = jnp.zeros((), BF16)
    r_idx = lax.broadcasted_iota(jnp.int32, (t, t), 0)
    c_idx = lax.broadcasted_iota(jnp.int32, (t, t), 1)
    suffix = (r_idx > c_idx).astype(BF16)
    suffix2 = jnp.concatenate([suffix, suffix], axis=0)
    causal = c_idx < r_idx

    def one_query_block(sub, c):
        qi = pl.program_id(1) * ATTN_BLOCKS_PER_STEP + sub
        row0 = pl.multiple_of(sub * t, t)
        for p in range(n_pairs):
            qp = q_ref[0, pl.ds(row0, t), p * HEAD_PAIR:(p + 1) * HEAD_PAIR]
            for h in range(2):
                q2_ref[(2 * p + h) * t:(2 * p + h + 1) * t, :] = jnp.where(head_lanes[h], qp, zero)
        acc_ref[...] = jnp.zeros_like(acc_ref)
        carry_ref[...] = jnp.zeros_like(carry_ref)

        def block(j, diag, m):
            start = pl.multiple_of(j * t, t)
            mask2 = jnp.concatenate([causal, causal], axis=0) if diag else None
            st = [dict() for _ in range(n_pairs)]

            def head_rows(p):
                return [slice((2 * p + h) * t, (2 * p + h) * t + m) for h in range(2)]

            def scores(p):
                d = st[p]
                d["cols"] = slice(p * HEAD_PAIR, (p + 1) * HEAD_PAIR)
                kb = k_ref[0, pl.ds(start, t), d["cols"]]
                q2 = jnp.concatenate([q2_ref[r, :] for r in head_rows(p)], axis=0)
                z = lax.dot_general(q2, kb, (((1,), (1,)), ((), ())),
                                    preferred_element_type=F32)
                sp = _softplus(z)
                nl = jnp.where(mask2, sp, 0.0) if diag else sp
                hi, lo = _split_bf16(nl)
                d["hl"] = jnp.concatenate([hi, lo], axis=1)
                d["log_beta"] = z - sp
                d["nl0"] = nl[:, 0:1]

            def weights(p):
                d = st[p]
                hl = d["hl"]
                after = jnp.concatenate([_dot(hl[0:m], suffix2), _dot(hl[m:2 * m], suffix2)], axis=0)
                carry = jnp.concatenate([carry_ref[r, :] for r in head_rows(p)], axis=0)
                a = jnp.exp(d["log_beta"] - after - carry)
                if diag:
                    a = jnp.where(mask2, a, 0.0)
                a = a.astype(BF16)
                d["a2"] = jnp.concatenate([a[0:m], a[m:2 * m]], axis=1)
                new_carry = carry + after[:, 0:1] + d["nl0"]
                for h, r in enumerate(head_rows(p)):
                    carry_ref[r, :] = new_carry[h * m:(h + 1) * m]

            def values(p):
                d = st[p]
                vb = v_ref[0, pl.ds(start, t), d["cols"]]
                v2 = jnp.concatenate([jnp.where(head_lanes[0], vb, zero),
                                      jnp.where(head_lanes[1], vb, zero)], axis=0)
                acc_ref[0:m, d["cols"]] += _dot(d["a2"], v2)

            for step in range(n_pairs + 2):
                if step < n_pairs:
                    scores(step)
                if 0 <= step - 1 < n_pairs:
                    weights(step - 1)
                if 0 <= step - 2 < n_pairs:
                    values(step - 2)

        top = ATTN_TOP_ROWS

        def flags():
            bottom = jnp.concatenate([carry_ref[hh * t + top:(hh + 1) * t, :] for hh in range(2 * n_pairs)], axis=0)
            return (jnp.min(carry_ref[...]) < F32_EXP_UNDERFLOW, jnp.min(bottom) >= F32_EXP_UNDERFLOW)

        block(qi, True, t)

        def body(state):
            it, _, bottom_done = state
            j = qi - 1 - it

            @pl.when(bottom_done)
            def _():
                block(j, False, top)

            @pl.when(jnp.logical_not(bottom_done))
            def _():
                block(j, False, t)

            return (it + 1,) + flags()

        lax.while_loop(lambda s: (s[0] < qi) & s[1], body, (jnp.int32(0),) + flags())
        o_ref[0, pl.ds(row0, t), :] = _rms(acc_ref[...], g_ref[...]).astype(BF16)
        return c

    lax.fori_loop(0, ATTN_BLOCKS_PER_STEP, one_query_block, 0)


def _attention(qkv, sb_g, batch, seq):
    qkv3 = qkv.reshape(batch, seq, 3 * SB_WIDTH)
    n_heads = SB_WIDTH // HEAD_DIM
    return pl.pallas_call(
        _attn_kernel,
        grid=(batch, seq // (ATTN_BLOCKS_PER_STEP * TQ_ATTN)),
        in_specs=[pl.BlockSpec((1, ATTN_BLOCKS_PER_STEP * TQ_ATTN, SB_WIDTH), lambda b, i: (b, i, 0)),
                  pl.BlockSpec((1, seq, SB_WIDTH), lambda b, i: (b, 0, 1)),
                  pl.BlockSpec((1, seq, SB_WIDTH), lambda b, i: (b, 0, 2)),
                  pl.BlockSpec((1, SB_WIDTH), lambda b, i: (0, 0))],
        out_specs=pl.BlockSpec((1, ATTN_BLOCKS_PER_STEP * TQ_ATTN, SB_WIDTH), lambda b, i: (b, i, 0)),
        out_shape=jax.ShapeDtypeStruct((batch, seq, SB_WIDTH), BF16),
        scratch_shapes=[pltpu.VMEM((n_heads * TQ_ATTN, HEAD_PAIR), BF16),
                        pltpu.VMEM((n_heads * TQ_ATTN, 1), F32),
                        pltpu.VMEM((TQ_ATTN, SB_WIDTH), F32)],
        compiler_params=pltpu.CompilerParams(dimension_semantics=("arbitrary",) * 2,
                                             vmem_limit_bytes=VMEM_LIMIT),
        name="sb_attention",
    )(qkv3, qkv3, qkv3, sb_g)


def _mix_kernel(sbn_ref, sgn_ref, x_ref, wout_ref, ffng_ref, wr2_ref, br_ref, h_ref, lg_ref):
    h = x_ref[...] + _dot(sbn_ref[...], wout_ref[0:SB_WIDTH, :]) + _dot(sgn_ref[...], wout_ref[SB_WIDTH:, :])
    h_ref[...] = h
    hn = _rms(h, ffng_ref[...])

    hn_hi, hn_lo = _split_bf16(hn)
    both = _dot(hn_hi, wr2_ref[...])
    logits = both[:, 0:LANES] + both[:, LANES:] + _dot(hn_lo, wr2_ref[:, 0:LANES]) + br_ref[...]
    lg_ref[...] = logits.T[0:ROUTER_ROWS, :]


def _route_kernel(lg_ref, ri_ref, rw_ref, cnt_ref, count_ref):
    tr = TM_ROUTE
    i = pl.program_id(0)

    @pl.when(i == 0)
    def _():
        count_ref[...] = jnp.zeros_like(count_ref)

    neg = jnp.float32(-jnp.inf)
    row8 = lax.broadcasted_iota(jnp.int32, (SUBLANES, tr), 0)

    def top(v):
        m = jnp.max(v, axis=0, keepdims=True)
        return m, jnp.min(jnp.where(v == m, row8, SUBLANES), axis=0, keepdims=True)

    def group_rows(g):
        return lg_ref[ROUTER_LANE0 + g * EXPERTS_PER_GROUP:ROUTER_LANE0 + (g + 1) * EXPERTS_PER_GROUP, :]

    gl = jnp.where(row8 < N_GROUPS, lg_ref[0:SUBLANES, :], neg)
    gmax, gidx = top(gl)
    gweight = 1.0 / jnp.sum(jnp.exp(gl - gmax), axis=0, keepdims=True)
    el = group_rows(0)
    for g in range(1, N_GROUPS):
        el = jnp.where(gidx == g, group_rows(g), el)
    m1, i1 = top(el)
    m2, i2 = top(jnp.where(row8 == i1, neg, el))
    t21 = jnp.exp(m2 - m1)
    w1 = gweight / (1.0 + t21)
    w2 = gweight * t21 / (1.0 + t21)
    e1 = gidx * EXPERTS_PER_GROUP + i1
    e2 = gidx * EXPERTS_PER_GROUP + i2

    row_e = lax.broadcasted_iota(jnp.int32, (N_EXPERTS, tr), 0)
    sel1 = row_e == e1
    sel2 = row_e == e2
    onehot = jnp.where(sel1 | sel2, 1.0, 0.0)
    r_t = lax.broadcasted_iota(jnp.int32, (tr, tr), 0)
    c_t = lax.broadcasted_iota(jnp.int32, (tr, tr), 1)
    before = (r_t < c_t).astype(BF16)
    running = count_ref[:, 0:1] + _dot(onehot.astype(BF16), before)
    rank1 = jnp.sum(jnp.where(sel1, running, 0.0), axis=0, keepdims=True)
    rank2 = jnp.sum(jnp.where(sel2, running, 0.0), axis=0, keepdims=True)
    new_count = count_ref[:, 0:1] + jnp.sum(onehot, axis=1, keepdims=True)
    count_ref[...] = jnp.broadcast_to(new_count, count_ref.shape)
    cnt_ref[...] = jnp.broadcast_to(new_count, cnt_ref.shape)

    ri_ref[...] = jnp.where(row8 == 0, e1, jnp.where(row8 == 1, e2, jnp.where(
        row8 == 2, rank1.astype(jnp.int32), jnp.where(row8 == 3, rank2.astype(jnp.int32), 0))))
    row128 = lax.broadcasted_iota(jnp.int32, (LANES, tr), 0)
    rw_ref[...] = jnp.where(row128 == 0, w1, jnp.where(row128 == 1, w2, 0.0)).T


def _route(lg):
    n = lg.shape[1]
    return pl.pallas_call(
        _route_kernel,
        grid=(n // TM_ROUTE,),
        in_specs=[pl.BlockSpec((ROUTER_ROWS, TM_ROUTE), lambda i: (0, i))],
        out_specs=[pl.BlockSpec((SUBLANES, TM_ROUTE), lambda i: (0, i)),
                   pl.BlockSpec((TM_ROUTE, LANES), lambda i: (i, 0)),
                   pl.BlockSpec((N_EXPERTS, LANES), lambda i: (0, 0))],
        out_shape=[jax.ShapeDtypeStruct((SUBLANES, n), jnp.int32),
                   jax.ShapeDtypeStruct((n, LANES), F32),
                   jax.ShapeDtypeStruct((N_EXPERTS, LANES), F32)],
        scratch_shapes=[pltpu.VMEM((N_EXPERTS, LANES), F32)],
        compiler_params=pltpu.CompilerParams(dimension_semantics=("arbitrary",),
                                             vmem_limit_bytes=VMEM_LIMIT),
        name="route",
    )(lg)


def _mix(sbn, sgn, x2, w_out_b, ffn_g, wr2, br):
    n = x2.shape[0]
    row = lambda i: (i, 0)
    const = lambda i: (0, 0)
    return pl.pallas_call(
        _mix_kernel,
        grid=(n // TM_MIX,),
        in_specs=[pl.BlockSpec((TM_MIX, SB_WIDTH), row),
                  pl.BlockSpec((TM_MIX, SG_WIDTH), row),
                  pl.BlockSpec((TM_MIX, D_MODEL), row),
                  pl.BlockSpec((D_MODEL, D_MODEL), const),
                  pl.BlockSpec((1, D_MODEL), const),
                  pl.BlockSpec((D_MODEL, 2 * LANES), const),
                  pl.BlockSpec((1, LANES), const)],
        out_specs=[pl.BlockSpec((TM_MIX, D_MODEL), row),
                   pl.BlockSpec((ROUTER_ROWS, TM_MIX), lambda i: (0, i))],
        out_shape=[jax.ShapeDtypeStruct((n, D_MODEL), F32),
                   jax.ShapeDtypeStruct((ROUTER_ROWS, n), F32)],
        compiler_params=pltpu.CompilerParams(dimension_semantics=("arbitrary",),
                                             vmem_limit_bytes=VMEM_LIMIT),
        name="mix_router",
    )(sbn, sgn, x2, w_out_b, ffn_g, wr2, br)


_PAD_BITS = tuple(1 << b for b in reversed(range(EXPERT_CHUNK.bit_length() - 1)))


def _dispatch_kernel(dest_ref, pad_start_ref, pad_count_ref, used_ref, h_ref, g_ref, zeros_ref, xs_ref,
                     hn_ref, sem, zsem):
    tm = TM_DISPATCH
    i = pl.program_id(0)
    n_steps = pl.num_programs(0) - 1
    n = n_steps * tm
    base = (i - 1) * tm
    prev = hn_ref.at[lax.rem(i + 1, 2)]
    n_chunks = xs_ref.shape[0] // (EXPERT_CHUNK * ROW_TILE)

    def pad_copies(do):
        for e in range(N_EXPERTS):
            start = pad_start_ref[e]
            count = pad_count_ref[e]
            for bit in _PAD_BITS:
                @pl.when((count & bit) != 0)
                def _(start=start, bit=bit):
                    do(pltpu.make_async_copy(_token_rows(zeros_ref, 0, bit),
                                             _token_rows(xs_ref, start, bit), zsem))
                start = start + (count & bit)
        for k in range(N_EXPERTS):
            chunk = used_ref[0] + k

            @pl.when(chunk < n_chunks)
            def _(chunk=chunk):
                do(pltpu.make_async_copy(zeros_ref, _token_rows(xs_ref, chunk * EXPERT_CHUNK, EXPERT_CHUNK),
                                         zsem))

    @pl.when(i == 0)
    def _():
        pad_copies(lambda cp: cp.start())

    @pl.when(i > 0)
    def _():
        def body(r, c):
            src = _token_rows(prev, r, 1)
            for s in range(2):
                pltpu.make_async_copy(src, _token_rows(xs_ref, dest_ref[s * n + base + r], 1),
                                      sem).start(priority=s)
            return c

        lax.fori_loop(0, tm, body, 0, unroll=8)

    @pl.when(i < n_steps)
    def _():
        _rows_to_tiles(hn_ref.at[lax.rem(i, 2)], _rms(h_ref[...], g_ref[...]))

    @pl.when(i > 0)
    def _():
        for _ in range(2):
            pltpu.make_async_copy(prev, _token_rows(xs_ref, 0, tm), sem).wait()

    @pl.when(i == n_steps)
    def _():
        pad_copies(lambda cp: cp.wait())


def _dispatch(dest, pad_start, pad_count, used_chunks, h, ffn_g, n_rows):
    n_steps = h.shape[0] // TM_DISPATCH
    zeros = jnp.zeros((EXPERT_CHUNK * ROW_TILE, LANES), F32)
    return pl.pallas_call(
        _dispatch_kernel,
        grid_spec=pltpu.PrefetchScalarGridSpec(
            num_scalar_prefetch=4,
            grid=(n_steps + 1,),
            in_specs=[pl.BlockSpec((TM_DISPATCH, D_MODEL), lambda i, *_: (jnp.minimum(i, n_steps - 1), 0)),
                      pl.BlockSpec((1, D_MODEL), lambda i, *_: (0, 0)),
                      pl.BlockSpec(memory_space=pl.ANY)],
            out_specs=pl.BlockSpec(memory_space=pl.ANY),
            scratch_shapes=[pltpu.VMEM((2, TM_DISPATCH * ROW_TILE, LANES), F32),
                            pltpu.SemaphoreType.DMA, pltpu.SemaphoreType.DMA]),
        out_shape=jax.ShapeDtypeStruct((n_rows * ROW_TILE, LANES), F32),
        compiler_params=pltpu.CompilerParams(dimension_semantics=("arbitrary",),
                                             vmem_limit_bytes=VMEM_LIMIT),
        name="dispatch",
    )(dest, pad_start, pad_count, used_chunks, h, ffn_g, zeros)


X_SLOTS = 3
TILE_CHUNKS = TM_EXPERT // EXPERT_CHUNK


def _expert_kernel(tiles_ref, chunk0_ref, chunks_ref, nt_ref, used_ref, xs_ref, wg_ref, wu_ref, wd_ref,
                   zeros_ref, ys_ref, x_buf, y_buf, sg_buf, su_buf, sd_buf, wgb, wub, wdb, state,
                   w_sems, x_sems, y_sems, zsem):
    t = pl.program_id(0)
    last = pl.num_programs(0) - 1
    nt = nt_ref[0]
    n_chunks = ys_ref.shape[0] // (EXPERT_CHUNK * ROW_TILE)

    def tile_copies(tile, do, out):
        for c in range(TILE_CHUNKS):
            @pl.when(c < chunks_ref[tile])
            def _(c=c):
                first = (chunk0_ref[tile] + c) * EXPERT_CHUNK
                if out:
                    slot = lax.rem(tile, 2)
                    do(pltpu.make_async_copy(_token_rows(y_buf.at[slot], c * EXPERT_CHUNK, EXPERT_CHUNK),
                                             _token_rows(ys_ref, first, EXPERT_CHUNK), y_sems.at[slot]))
                else:
                    slot = lax.rem(tile, X_SLOTS)
                    do(pltpu.make_async_copy(_token_rows(xs_ref, first, EXPERT_CHUNK),
                                             _token_rows(x_buf.at[slot], c * EXPERT_CHUNK, EXPERT_CHUNK),
                                             x_sems.at[slot]))

    start = lambda cp: cp.start()
    wait = lambda cp: cp.wait()

    def tail_copies(do):
        for k in range(N_EXPERTS):
            chunk = used_ref[0] + k

            @pl.when(chunk < n_chunks)
            def _(chunk=chunk):
                do(pltpu.make_async_copy(zeros_ref, _token_rows(ys_ref, chunk * EXPERT_CHUNK, EXPERT_CHUNK),
                                         zsem))

    def weight_copies(e, slot):
        return (pltpu.make_async_copy(wg_ref.at[e], sg_buf.at[slot], w_sems.at[slot]),
                pltpu.make_async_copy(wu_ref.at[e], su_buf.at[slot], w_sems.at[slot]),
                pltpu.make_async_copy(wd_ref.at[e], sd_buf.at[slot], w_sems.at[slot]))

    def next_with_rows(e):
        return lax.while_loop(lambda k: (k < N_EXPERTS) & (tiles_ref[jnp.minimum(k, N_EXPERTS - 1)] == 0),
                              lambda k: k + 1, e + 1)

    @pl.when(t == 0)
    def _():
        first = next_with_rows(jnp.int32(-1))
        state[0] = jnp.int32(-1)
        state[1] = jnp.int32(0)
        state[2] = jnp.int32(1)
        state[3] = first
        for cp in weight_copies(first, 0):
            cp.start()
        tile_copies(0, start, False)

        @pl.when(nt > 1)
        def _():
            tile_copies(1, start, False)

        tail_copies(start)

    @pl.when(t + 2 < nt)
    def _():
        tile_copies(t + 2, start, False)

    @pl.when(t < nt)
    def _():
        @pl.when(state[1] == 0)
        def _():
            e = state[3]
            slot = 1 - state[2]
            nxt = next_with_rows(e)
            state[0] = e
            state[1] = tiles_ref[e]
            state[2] = slot
            state[3] = nxt
            for cp in weight_copies(e, slot):
                cp.wait()

            @pl.when(nxt < N_EXPERTS)
            def _():
                for cp in weight_copies(nxt, 1 - slot):
                    cp.start()

            wgb[...] = sg_buf[slot].astype(BF16)
            wub[...] = su_buf[slot].astype(BF16)
            wdb[...] = sd_buf[slot].astype(BF16)

        state[1] = state[1] - 1
        tile_copies(t, wait, False)

        @pl.when(t >= 2)
        def _():
            tile_copies(t - 2, wait, True)

        for n_chunks_here in range(1, TILE_CHUNKS + 1):
            @pl.when(chunks_ref[t] == n_chunks_here)
            def _(m=n_chunks_here * EXPERT_CHUNK):
                x = _tiles_to_rows(x_buf.at[lax.rem(t, X_SLOTS)], m).astype(BF16)
                g = _dot(x, wgb[...])
                u = _dot(x, wub[...])
                hidden = (g * jax.nn.sigmoid(g)) * u
                _rows_to_tiles(y_buf.at[lax.rem(t, 2)], _dot(hidden.astype(BF16), wdb[...]))

        tile_copies(t, start, True)

    @pl.when(t == last)
    def _():
        for back in (2, 1):
            @pl.when(nt >= back)
            def _(back=back):
                tile_copies(nt - back, wait, True)

        tail_copies(wait)


def _experts(tiles, chunk0, chunks, n_tiles, used_chunks, xs, wg, wu, wd):
    any_spec = pl.BlockSpec(memory_space=pl.ANY)
    zeros = jnp.zeros((EXPERT_CHUNK * ROW_TILE, LANES), F32)
    return pl.pallas_call(
        _expert_kernel,
        grid_spec=pltpu.PrefetchScalarGridSpec(
            num_scalar_prefetch=5,
            grid=(chunks.shape[0],),
            in_specs=[any_spec, any_spec, any_spec, any_spec, any_spec],
            out_specs=any_spec,
            scratch_shapes=[pltpu.VMEM((X_SLOTS, TM_EXPERT * ROW_TILE, LANES), F32),
                            pltpu.VMEM((2, TM_EXPERT * ROW_TILE, LANES), F32),
                            pltpu.VMEM((2, D_MODEL, D_EXPERT), F32),
                            pltpu.VMEM((2, D_MODEL, D_EXPERT), F32),
                            pltpu.VMEM((2, D_EXPERT, D_MODEL), F32),
                            pltpu.VMEM((D_MODEL, D_EXPERT), BF16),
                            pltpu.VMEM((D_MODEL, D_EXPERT), BF16),
                            pltpu.VMEM((D_EXPERT, D_MODEL), BF16),
                            pltpu.SMEM((4,), jnp.int32),
                            pltpu.SemaphoreType.DMA((2,)),
                            pltpu.SemaphoreType.DMA((X_SLOTS,)),
                            pltpu.SemaphoreType.DMA((2,)),
                            pltpu.SemaphoreType.DMA]),
        out_shape=jax.ShapeDtypeStruct(xs.shape, F32),
        compiler_params=pltpu.CompilerParams(dimension_semantics=("arbitrary",),
                                             vmem_limit_bytes=VMEM_LIMIT),
        name="expert_mlp",
    )(tiles, chunk0, chunks, n_tiles, used_chunks, xs, wg, wu, wd, zeros)


def _combine_kernel(dest_ref, h_ref, rw_ref, fg_ref, y_ref, o_ref, buf, sems):
    tm = TM_COMBINE
    i = pl.program_id(0)
    n_steps = pl.num_programs(0)
    n = n_steps * tm
    cur = i % 2

    def fetch(step, half):
        def body(r, c):
            for s in range(2):
                pltpu.make_async_copy(_token_rows(y_ref, dest_ref[s * n + step * tm + r], 1),
                                      _token_rows(buf.at[half, s], r, 1),
                                      sems.at[half]).start(priority=s)
            return c

        lax.fori_loop(0, tm, body, 0, unroll=8)

    @pl.when(i == 0)
    def _():
        fetch(0, 0)

    @pl.when(i + 1 < n_steps)
    def _():
        fetch(i + 1, 1 - cur)

    for s in range(2):
        pltpu.make_async_copy(_token_rows(y_ref, 0, tm), buf.at[cur, s], sems.at[cur]).wait()
    rw = rw_ref[...]
    out = (h_ref[...] + rw[:, 0:1] * _tiles_to_rows(buf.at[cur, 0], tm)
           + rw[:, 1:2] * _tiles_to_rows(buf.at[cur, 1], tm))
    o_ref[...] = _rms(out, fg_ref[...])


def _combine(dest, h, rw, final_g, ys):
    n = h.shape[0]
    return pl.pallas_call(
        _combine_kernel,
        grid_spec=pltpu.PrefetchScalarGridSpec(
            num_scalar_prefetch=1,
            grid=(n // TM_COMBINE,),
            in_specs=[pl.BlockSpec((TM_COMBINE, D_MODEL), lambda i, d: (i, 0)),
                      pl.BlockSpec((TM_COMBINE, LANES), lambda i, d: (i, 0)),
                      pl.BlockSpec((1, D_MODEL), lambda i, d: (0, 0)),
                      pl.BlockSpec(memory_space=pl.ANY)],
            out_specs=pl.BlockSpec((TM_COMBINE, D_MODEL), lambda i, d: (i, 0)),
            scratch_shapes=[pltpu.VMEM((2, 2, TM_COMBINE * ROW_TILE, LANES), F32),
                            pltpu.SemaphoreType.DMA((2,))]),
        out_shape=jax.ShapeDtypeStruct((n, D_MODEL), F32),
        compiler_params=pltpu.CompilerParams(dimension_semantics=("arbitrary",),
                                             vmem_limit_bytes=VMEM_LIMIT),
        name="combine",
    )(dest, h, rw, final_g, ys)


def _schedule(counts, max_tiles):
    chunks = (counts + EXPERT_CHUNK - 1) // EXPERT_CHUNK
    chunk_end = jnp.cumsum(chunks)
    chunk_start = chunk_end - chunks
    tiles = (chunks + TILE_CHUNKS - 1) // TILE_CHUNKS
    tile_end = jnp.cumsum(tiles)
    tile = jnp.arange(max_tiles, dtype=jnp.int32)
    owner = jnp.sum(tile[:, None] >= tile_end[None, :], axis=1)
    is_owner = owner[:, None] == jnp.arange(N_EXPERTS, dtype=jnp.int32)[None, :]
    of_owner = lambda v: jnp.sum(jnp.where(is_owner, v[None, :], 0), axis=1)
    done = (tile - of_owner(tile_end - tiles)) * TILE_CHUNKS
    tile_chunk0 = (of_owner(chunk_start) + done).astype(jnp.int32)
    tile_chunks = jnp.clip(of_owner(chunks) - done, 0, TILE_CHUNKS).astype(jnp.int32)
    return tiles, chunk_start * EXPERT_CHUNK, tile_chunk0, tile_chunks, tile_end[-1:], chunk_end[-1:]


def _layer(x, attn_g, w_in, sg_g, w_sp, b_sp, sb_g, sg_out_g, w_out, ffn_g,
           w_rg, b_rg, w_re, b_re, w_gate, w_up, w_down):
    batch, seq, _ = x.shape
    n = batch * seq
    x2 = x.reshape(n, D_MODEL)
    row = lambda v: v.reshape(1, -1)

    bsp_full = jnp.repeat(b_sp.T, HEAD_DIM, axis=1)
    qkv, sgn = _inproj(x2, row(attn_g), w_in.astype(BF16), row(sg_g), w_sp, bsp_full, row(sg_out_g))
    sbn = _attention(qkv, row(sb_g), batch, seq).reshape(n, SB_WIDTH)

    pad_lanes = lambda v, width: jnp.pad(v, [(0, 0)] * (v.ndim - 1) + [(0, width - v.shape[-1])])
    w_r = jnp.concatenate([pad_lanes(w_rg, ROUTER_LANE0),
                           jnp.transpose(w_re, (1, 0, 2)).reshape(D_MODEL, N_EXPERTS)], axis=1)
    w_r = pad_lanes(w_r, LANES)
    wr_hi = w_r.astype(BF16)
    wr_lo = (w_r - wr_hi.astype(F32)).astype(BF16)
    wr2 = jnp.concatenate([wr_hi, wr_lo], axis=1)
    b_r = pad_lanes(jnp.concatenate([pad_lanes(b_rg, ROUTER_LANE0), b_re.reshape(-1)]), LANES)

    h, lg = _mix(sbn, sgn, x2, w_out.astype(BF16), row(ffn_g), wr2, row(b_r))
    ri, rw, cnt = _route(lg)

    counts = cnt[:, 0].astype(jnp.int32)
    n_rows = 2 * n + N_EXPERTS * EXPERT_CHUNK
    tiles, offsets, tile_chunk0, tile_chunks, n_tiles, used_chunks = _schedule(
        counts, 2 * n // TM_EXPERT + N_EXPERTS)
    expert, rank = ri[0:2], ri[2:4]
    is_e = expert[None] == jnp.arange(N_EXPERTS, dtype=jnp.int32)[:, None, None]
    dest = (jnp.sum(jnp.where(is_e, offsets[:, None, None], 0), axis=0) + rank).reshape(-1)
    pad_start = offsets + counts
    pad_count = (-counts) % EXPERT_CHUNK

    xs = _dispatch(dest, pad_start, pad_count, used_chunks, h, row(ffn_g), n_rows)
    ys = _experts(tiles, tile_chunk0, tile_chunks, n_tiles, used_chunks, xs,
                  w_gate.reshape(N_EXPERTS, D_MODEL, D_EXPERT),
                  w_up.reshape(N_EXPERTS, D_MODEL, D_EXPERT),
                  w_down.reshape(N_EXPERTS, D_EXPERT, D_MODEL))
    return dest, h, rw, ys


def kernel(x, attn_norm_g, w_in, sg_norm_g, w_spatial, b_spatial, sb_out_norm_g, sg_out_norm_g,
           w_out, ffn_norm_g, w_router_group, b_router_group, w_router_expert, b_router_expert,
           w_gate, w_up, w_down, final_norm_g):
    assert attn_norm_g.shape[0] == 1, "single-layer problem"
    batch, seq, _ = x.shape
    dest, h, rw, ys = _layer(x, attn_norm_g[0], w_in[0], sg_norm_g[0], w_spatial[0], b_spatial[0],
                             sb_out_norm_g[0], sg_out_norm_g[0], w_out[0], ffn_norm_g[0],
                             w_router_group[0], b_router_group[0], w_router_expert[0],
                             b_router_expert[0], w_gate[0], w_up[0], w_down[0])
    out = _combine(dest, h, rw, final_norm_g.reshape(1, -1), ys)
    return out.reshape(batch, seq, D_MODEL)
```

```python
import functools
import math

import jax
import jax.numpy as jnp
from jax import lax
from jax.experimental import pallas as pl
from jax.experimental.pallas import tpu as pltpu

D_MODEL = 1024
HEAD_DIM = 64
SB_WIDTH = 512
SG_WIDTH = 512
SG_HEADS = 8
D_IN = 3 * SB_WIDTH + 2 * SG_WIDTH
CHUNK = 128
N_GROUPS = 4
EXPERTS_PER_GROUP = 8
N_EXPERTS = N_GROUPS * EXPERTS_PER_GROUP
D_EXPERT = 512
EPS = 1e-6
F32_EXP_UNDERFLOW = 110.0

LANES = 128
SUBLANES = 8
ROW_TILE = D_MODEL // LANES
assert ROW_TILE == SUBLANES
HEAD_PAIR = 2 * HEAD_DIM
ROUTER_LANE0 = SUBLANES
ROUTER_ROWS = ROUTER_LANE0 + N_EXPERTS
assert EXPERTS_PER_GROUP == SUBLANES and N_GROUPS <= ROUTER_LANE0

TM_PROJ = 1024
TQ_ATTN = 256
ATTN_BLOCKS_PER_STEP = 2
ATTN_TOP_ROWS = 160
TM_MIX = 1024
TM_ROUTE = 1024
TM_DISPATCH = 1024
TM_EXPERT = 512
EXPERT_CHUNK = 64
TM_COMBINE = 512
VMEM_LIMIT = 48 * 1024 * 1024

F32 = jnp.float32
BF16 = jnp.bfloat16


def _rms(x, g):
    return x * lax.rsqrt(jnp.mean(x * x, axis=-1, keepdims=True) + EPS) * g


def _gelu(x):
    c = math.sqrt(2.0 / math.pi)
    return x * (0.5 * (1.0 + jnp.tanh(c * (x + 0.044715 * (x * x * x)))))


def _softplus(z):
    return jnp.maximum(z, 0.0) + jnp.log(1.0 + jnp.exp(-jnp.abs(z)))


def _dot(a, b):
    return jnp.dot(a, b, preferred_element_type=F32)


def _rows_to_tiles(ref, x):
    m = x.shape[0]
    for k in range(ROW_TILE):
        ref[pl.ds(k, m, stride=ROW_TILE), :] = x[:, k * LANES:(k + 1) * LANES]


def _tiles_to_rows(ref, m):
    return jnp.concatenate([ref[pl.ds(k, m, stride=ROW_TILE), :] for k in range(ROW_TILE)], axis=1)


def _token_rows(ref, first_token, n_tokens):
    return ref.at[pl.ds(pl.multiple_of(first_token * ROW_TILE, ROW_TILE), n_tokens * ROW_TILE)]


def _split_bf16(x):
    hi = x.astype(BF16)
    lo = (x - hi.astype(F32)).astype(BF16)
    return hi, lo


def _inproj_kernel(x_ref, g_ref, w_ref, sgg_ref, wsp_ref, bsp_ref, sgog_ref, qkv_ref, sgn_ref,
                   gu_ref, vgn_ref, sg_ref):
    tm = TM_PROJ
    hb = _rms(x_ref[...], g_ref[...]).astype(BF16)
    gv = _gelu(_dot(hb, w_ref[:, 3 * SB_WIDTH + SG_WIDTH:D_IN]))
    vgn_ref[...] = _rms(gv, sgg_ref[...]).astype(BF16)
    gu_ref[...] = _gelu(_dot(hb, w_ref[:, 3 * SB_WIDTH:3 * SB_WIDTH + SG_WIDTH]))
    q = _dot(hb, w_ref[:, 0:SB_WIDTH]) * (1.0 / math.sqrt(HEAD_DIM))
    qkv_ref[:, 0:SB_WIDTH] = q.astype(BF16)
    qkv_ref[:, SB_WIDTH:2 * SB_WIDTH] = _dot(hb, w_ref[:, SB_WIDTH:2 * SB_WIDTH]).astype(BF16)

    lane = lax.broadcasted_iota(jnp.int32, (1, LANES), 1)
    first = lane < HEAD_DIM
    zero = jnp.zeros((), BF16)
    r_c = lax.broadcasted_iota(jnp.int32, (CHUNK, CHUNK), 0)
    c_c = lax.broadcasted_iota(jnp.int32, (CHUNK, CHUNK), 1)
    tril = r_c >= c_c
    n_pairs = SG_WIDTH // HEAD_PAIR
    w_pairs = []
    for p in range(n_pairs):
        w0 = jnp.where(tril, wsp_ref[2 * p], 0.0).astype(BF16)
        w1 = jnp.where(tril, wsp_ref[2 * p + 1], 0.0).astype(BF16)
        w_pairs.append(jnp.concatenate([w0, w1], axis=1))
    bsp = bsp_ref[...]
    for c in range(tm // CHUNK):
        rows = slice(c * CHUNK, (c + 1) * CHUNK)
        for p in range(n_pairs):
            cols = slice(p * HEAD_PAIR, (p + 1) * HEAD_PAIR)
            vg = vgn_ref[rows, cols]
            rhs = jnp.concatenate([jnp.where(first, vg, zero), jnp.where(first, zero, vg)], axis=0)
            mixed = _dot(w_pairs[p], rhs) + bsp[:, cols]
            sg_ref[rows, cols] = gu_ref[rows, cols] * mixed
    qkv_ref[:, 2 * SB_WIDTH:3 * SB_WIDTH] = _dot(hb, w_ref[:, 2 * SB_WIDTH:3 * SB_WIDTH]).astype(BF16)
    sgn_ref[...] = _rms(sg_ref[...], sgog_ref[...]).astype(BF16)


def _inproj(x2, attn_g, w_in_b, sg_g, wsp, bsp_full, sg_out_g):
    n = x2.shape[0]
    row = lambda i: (i, 0)
    const = lambda i: (0, 0)
    return pl.pallas_call(
        _inproj_kernel,
        grid=(n // TM_PROJ,),
        in_specs=[pl.BlockSpec((TM_PROJ, D_MODEL), row),
                  pl.BlockSpec((1, D_MODEL), const),
                  pl.BlockSpec((D_MODEL, D_IN), const),
                  pl.BlockSpec((1, SG_WIDTH), const),
                  pl.BlockSpec((SG_HEADS, CHUNK, CHUNK), lambda i: (0, 0, 0)),
                  pl.BlockSpec((CHUNK, SG_WIDTH), const),
                  pl.BlockSpec((1, SG_WIDTH), const)],
        out_specs=[pl.BlockSpec((TM_PROJ, 3 * SB_WIDTH), row),
                   pl.BlockSpec((TM_PROJ, SG_WIDTH), row)],
        out_shape=[jax.ShapeDtypeStruct((n, 3 * SB_WIDTH), BF16),
                   jax.ShapeDtypeStruct((n, SG_WIDTH), BF16)],
        scratch_shapes=[pltpu.VMEM((TM_PROJ, SG_WIDTH), F32),
                        pltpu.VMEM((TM_PROJ, SG_WIDTH), BF16),
                        pltpu.VMEM((TM_PROJ, SG_WIDTH), F32)],
        compiler_params=pltpu.CompilerParams(dimension_semantics=("arbitrary",),
                                             vmem_limit_bytes=VMEM_LIMIT),
        name="inproj",
    )(x2, attn_g, w_in_b, sg_g, wsp, bsp_full, sg_out_g)


def _attn_kernel(q_ref, k_ref, v_ref, o_ref, q2_ref, carry_ref):
    t = TQ_ATTN
    n_pairs = SB_WIDTH // HEAD_PAIR
    lane = lax.broadcasted_iota(jnp.int32, (1, HEAD_PAIR), 1)
    head_lanes = (lane < HEAD_DIM, lane >= HEAD_DIM)
    zero = jnp.zeros((), BF16)
    r_idx = lax.broadcasted_iota(jnp.int32, (t, t), 0)
    c_idx = lax.broadcasted_iota(jnp.int32, (t, t), 1)
    suffix = (r_idx > c_idx).astype(BF16)
    suffix2 = jnp.concatenate([suffix, suffix], axis=0)
    causal = c_idx < r_idx

    def one_query_block(sub, c):
        qi = pl.program_id(1) * ATTN_BLOCKS_PER_STEP + sub
        row0 = pl.multiple_of(sub * t, t)
        for p in range(n_pairs):
            qp = q_ref[0, pl.ds(row0, t), p * HEAD_PAIR:(p + 1) * HEAD_PAIR]
            for h in range(2):
                q2_ref[(2 * p + h) * t:(2 * p + h + 1) * t, :] = jnp.where(head_lanes[h], qp, zero)
        o_ref[0, pl.ds(row0, t), :] = jnp.zeros((t, SB_WIDTH), F32)
        carry_ref[...] = jnp.zeros_like(carry_ref)

        def block(j, diag, m):
            start = pl.multiple_of(j * t, t)
            mask2 = jnp.concatenate([causal, causal], axis=0) if diag else None
            st = [dict() for _ in range(n_pairs)]

            def head_rows(p):
                return [slice((2 * p + h) * t, (2 * p + h) * t + m) for h in range(2)]

            def scores(p):
                d = st[p]
                d["cols"] = slice(p * HEAD_PAIR, (p + 1) * HEAD_PAIR)
                kb = k_ref[0, pl.ds(start, t), d["cols"]]
                q2 = jnp.concatenate([q2_ref[r, :] for r in head_rows(p)], axis=0)
                z = lax.dot_general(q2, kb, (((1,), (1,)), ((), ())),
                                    preferred_element_type=F32)
                sp = _softplus(z)
                nl = jnp.where(mask2, sp, 0.0) if diag else sp
                hi, lo = _split_bf16(nl)
                d["hl"] = jnp.concatenate([hi, lo], axis=1)
                d["log_beta"] = z - sp
                d["nl0"] = nl[:, 0:1]

            def weights(p):
                d = st[p]
                hl = d["hl"]
                after = jnp.concatenate([_dot(hl[0:m], suffix2), _dot(hl[m:2 * m], suffix2)], axis=0)
                carry = jnp.concatenate([carry_ref[r, :] for r in head_rows(p)], axis=0)
                a = jnp.exp(d["log_beta"] - after - carry)
                if diag:
                    a = jnp.where(mask2, a, 0.0)
                a = a.astype(BF16)
                d["a2"] = jnp.concatenate([a[0:m], a[m:2 * m]], axis=1)
                new_carry = carry + after[:, 0:1] + d["nl0"]
                for h, r in enumerate(head_rows(p)):
                    carry_ref[r, :] = new_carry[h * m:(h + 1) * m]

            def values(p):
                d = st[p]
                vb = v_ref[0, pl.ds(start, t), d["cols"]]
                v2 = jnp.concatenate([jnp.where(head_lanes[0], vb, zero),
                                      jnp.where(head_lanes[1], vb, zero)], axis=0)
                o_ref[0, pl.ds(row0, m), d["cols"]] += _dot(d["a2"], v2)

            for step in range(n_pairs + 2):
                if step < n_pairs:
                    scores(step)
                if 0 <= step - 1 < n_pairs:
                    weights(step - 1)
                if 0 <= step - 2 < n_pairs:
                    values(step - 2)

        top = ATTN_TOP_ROWS

        def flags():
            bottom = jnp.concatenate([carry_ref[hh * t + top:(hh + 1) * t, :] for hh in range(2 * n_pairs)], axis=0)
            return (jnp.min(carry_ref[...]) < F32_EXP_UNDERFLOW, jnp.min(bottom) >= F32_EXP_UNDERFLOW)

        block(qi, True, t)

        def body(state):
            it, _, bottom_done = state
            j = qi - 1 - it

            @pl.when(bottom_done)
            def _():
                block(j, False, top)

            @pl.when(jnp.logical_not(bottom_done))
            def _():
                block(j, False, t)

            return (it + 1,) + flags()

        lax.while_loop(lambda s: (s[0] < qi) & s[1], body, (jnp.int32(0),) + flags())
        return c

    lax.fori_loop(0, ATTN_BLOCKS_PER_STEP, one_query_block, 0)


def _attention(qkv, batch, seq):
    qkv3 = qkv.reshape(batch, seq, 3 * SB_WIDTH)
    n_heads = SB_WIDTH // HEAD_DIM
    return pl.pallas_call(
        _attn_kernel,
        grid=(batch, seq // (ATTN_BLOCKS_PER_STEP * TQ_ATTN)),
        in_specs=[pl.BlockSpec((1, ATTN_BLOCKS_PER_STEP * TQ_ATTN, SB_WIDTH), lambda b, i: (b, i, 0)),
                  pl.BlockSpec((1, seq, SB_WIDTH), lambda b, i: (b, 0, 1)),
                  pl.BlockSpec((1, seq, SB_WIDTH), lambda b, i: (b, 0, 2))],
        out_specs=pl.BlockSpec((1, ATTN_BLOCKS_PER_STEP * TQ_ATTN, SB_WIDTH), lambda b, i: (b, i, 0)),
        out_shape=jax.ShapeDtypeStruct((batch, seq, SB_WIDTH), F32),
        scratch_shapes=[pltpu.VMEM((n_heads * TQ_ATTN, HEAD_PAIR), BF16),
                        pltpu.VMEM((n_heads * TQ_ATTN, 1), F32)],
        compiler_params=pltpu.CompilerParams(dimension_semantics=("arbitrary",) * 2,
                                             vmem_limit_bytes=VMEM_LIMIT),
        name="sb_attention",
    )(qkv3, qkv3, qkv3)


def _mix_kernel(sb_ref, sgn_ref, x_ref, sbg_ref, wout_ref, ffng_ref, wr2_ref, br_ref,
                h_ref, lg_ref):
    sbn = _rms(sb_ref[...], sbg_ref[...]).astype(BF16)
    h = x_ref[...] + _dot(sbn, wout_ref[0:SB_WIDTH, :]) + _dot(sgn_ref[...], wout_ref[SB_WIDTH:, :])
    h_ref[...] = h
    hn = _rms(h, ffng_ref[...])

    hn_hi, hn_lo = _split_bf16(hn)
    both = _dot(hn_hi, wr2_ref[...])
    logits = both[:, 0:LANES] + both[:, LANES:] + _dot(hn_lo, wr2_ref[:, 0:LANES]) + br_ref[...]
    lg_ref[...] = logits.T[0:ROUTER_ROWS, :]


def _route_kernel(lg_ref, ri_ref, rw_ref, cnt_ref, count_ref):
    tr = TM_ROUTE
    i = pl.program_id(0)

    @pl.when(i == 0)
    def _():
        count_ref[...] = jnp.zeros_like(count_ref)

    neg = jnp.float32(-jnp.inf)
    row8 = lax.broadcasted_iota(jnp.int32, (SUBLANES, tr), 0)

    def top(v):
        m = jnp.max(v, axis=0, keepdims=True)
        return m, jnp.min(jnp.where(v == m, row8, SUBLANES), axis=0, keepdims=True)

    def group_rows(g):
        return lg_ref[ROUTER_LANE0 + g * EXPERTS_PER_GROUP:ROUTER_LANE0 + (g + 1) * EXPERTS_PER_GROUP, :]

    gl = jnp.where(row8 < N_GROUPS, lg_ref[0:SUBLANES, :], neg)
    gmax, gidx = top(gl)
    gweight = 1.0 / jnp.sum(jnp.exp(gl - gmax), axis=0, keepdims=True)
    el = group_rows(0)
    for g in range(1, N_GROUPS):
        el = jnp.where(gidx == g, group_rows(g), el)
    m1, i1 = top(el)
    m2, i2 = top(jnp.where(row8 == i1, neg, el))
    t21 = jnp.exp(m2 - m1)
    w1 = gweight / (1.0 + t21)
    w2 = gweight * t21 / (1.0 + t21)
    e1 = gidx * EXPERTS_PER_GROUP + i1
    e2 = gidx * EXPERTS_PER_GROUP + i2

    row_e = lax.broadcasted_iota(jnp.int32, (N_EXPERTS, tr), 0)
    sel1 = row_e == e1
    sel2 = row_e == e2
    onehot = jnp.where(sel1 | sel2, 1.0, 0.0)
    r_t = lax.broadcasted_iota(jnp.int32, (tr, tr), 0)
    c_t = lax.broadcasted_iota(jnp.int32, (tr, tr), 1)
    before = (r_t < c_t).astype(BF16)
    running = count_ref[:, 0:1] + _dot(onehot.astype(BF16), before)
    rank1 = jnp.sum(jnp.where(sel1, running, 0.0), axis=0, keepdims=True)
    rank2 = jnp.sum(jnp.where(sel2, running, 0.0), axis=0, keepdims=True)
    new_count = count_ref[:, 0:1] + jnp.sum(onehot, axis=1, keepdims=True)
    count_ref[...] = jnp.broadcast_to(new_count, count_ref.shape)
    cnt_ref[...] = jnp.broadcast_to(new_count, cnt_ref.shape)

    ri_ref[...] = jnp.where(row8 == 0, e1, jnp.where(row8 == 1, e2, jnp.where(
        row8 == 2, rank1.astype(jnp.int32), jnp.where(row8 == 3, rank2.astype(jnp.int32), 0))))
    row128 = lax.broadcasted_iota(jnp.int32, (LANES, tr), 0)
    rw_ref[...] = jnp.where(row128 == 0, w1, jnp.where(row128 == 1, w2, 0.0)).T


def _route(lg):
    n = lg.shape[1]
    return pl.pallas_call(
        _route_kernel,
        grid=(n // TM_ROUTE,),
        in_specs=[pl.BlockSpec((ROUTER_ROWS, TM_ROUTE), lambda i: (0, i))],
        out_specs=[pl.BlockSpec((SUBLANES, TM_ROUTE), lambda i: (0, i)),
                   pl.BlockSpec((TM_ROUTE, LANES), lambda i: (i, 0)),
                   pl.BlockSpec((N_EXPERTS, LANES), lambda i: (0, 0))],
        out_shape=[jax.ShapeDtypeStruct((SUBLANES, n), jnp.int32),
                   jax.ShapeDtypeStruct((n, LANES), F32),
                   jax.ShapeDtypeStruct((N_EXPERTS, LANES), F32)],
        scratch_shapes=[pltpu.VMEM((N_EXPERTS, LANES), F32)],
        compiler_params=pltpu.CompilerParams(dimension_semantics=("arbitrary",),
                                             vmem_limit_bytes=VMEM_LIMIT),
        name="route",
    )(lg)


def _mix(sb, sgn, x2, sb_g, w_out_b, ffn_g, wr2, br):
    n = x2.shape[0]
    row = lambda i: (i, 0)
    const = lambda i: (0, 0)
    return pl.pallas_call(
        _mix_kernel,
        grid=(n // TM_MIX,),
        in_specs=[pl.BlockSpec((TM_MIX, SB_WIDTH), row),
                  pl.BlockSpec((TM_MIX, SG_WIDTH), row),
                  pl.BlockSpec((TM_MIX, D_MODEL), row),
                  pl.BlockSpec((1, SB_WIDTH), const),
                  pl.BlockSpec((D_MODEL, D_MODEL), const),
                  pl.BlockSpec((1, D_MODEL), const),
                  pl.BlockSpec((D_MODEL, 2 * LANES), const),
                  pl.BlockSpec((1, LANES), const)],
        out_specs=[pl.BlockSpec((TM_MIX, D_MODEL), row),
                   pl.BlockSpec((ROUTER_ROWS, TM_MIX), lambda i: (0, i))],
        out_shape=[jax.ShapeDtypeStruct((n, D_MODEL), F32),
                   jax.ShapeDtypeStruct((ROUTER_ROWS, n), F32)],
        compiler_params=pltpu.CompilerParams(dimension_semantics=("arbitrary",),
                                             vmem_limit_bytes=VMEM_LIMIT),
        name="mix_router",
    )(sb, sgn, x2, sb_g, w_out_b, ffn_g, wr2, br)


_PAD_BITS = tuple(1 << b for b in reversed(range(EXPERT_CHUNK.bit_length() - 1)))


def _dispatch_kernel(dest_ref, pad_start_ref, pad_count_ref, used_ref, h_ref, g_ref, zeros_ref, xs_ref,
                     hn_ref, sem, zsem):
    tm = TM_DISPATCH
    i = pl.program_id(0)
    n_steps = pl.num_programs(0) - 1
    n = n_steps * tm
    base = (i - 1) * tm
    prev = hn_ref.at[lax.rem(i + 1, 2)]
    n_chunks = xs_ref.shape[0] // (EXPERT_CHUNK * ROW_TILE)

    def pad_copies(do):
        for e in range(N_EXPERTS):
            start = pad_start_ref[e]
            count = pad_count_ref[e]
            for bit in _PAD_BITS:
                @pl.when((count & bit) != 0)
                def _(start=start, bit=bit):
                    do(pltpu.make_async_copy(_token_rows(zeros_ref, 0, bit),
                                             _token_rows(xs_ref, start, bit), zsem))
                start = start + (count & bit)
        for k in range(N_EXPERTS):
            chunk = used_ref[0] + k

            @pl.when(chunk < n_chunks)
            def _(chunk=chunk):
                do(pltpu.make_async_copy(zeros_ref, _token_rows(xs_ref, chunk * EXPERT_CHUNK, EXPERT_CHUNK),
                                         zsem))

    @pl.when(i == 0)
    def _():
        pad_copies(lambda cp: cp.start())

    @pl.when(i > 0)
    def _():
        def body(r, c):
            src = _token_rows(prev, r, 1)
            for s in range(2):
                pltpu.make_async_copy(src, _token_rows(xs_ref, dest_ref[s * n + base + r], 1),
                                      sem).start(priority=s)
            return c

        lax.fori_loop(0, tm, body, 0, unroll=8)

    @pl.when(i < n_steps)
    def _():
        _rows_to_tiles(hn_ref.at[lax.rem(i, 2)], _rms(h_ref[...], g_ref[...]))

    @pl.when(i > 0)
    def _():
        for _ in range(2):
            pltpu.make_async_copy(prev, _token_rows(xs_ref, 0, tm), sem).wait()

    @pl.when(i == n_steps)
    def _():
        pad_copies(lambda cp: cp.wait())


def _dispatch(dest, pad_start, pad_count, used_chunks, h, ffn_g, n_rows):
    n_steps = h.shape[0] // TM_DISPATCH
    zeros = jnp.zeros((EXPERT_CHUNK * ROW_TILE, LANES), F32)
    return pl.pallas_call(
        _dispatch_kernel,
        grid_spec=pltpu.PrefetchScalarGridSpec(
            num_scalar_prefetch=4,
            grid=(n_steps + 1,),
            in_specs=[pl.BlockSpec((TM_DISPATCH, D_MODEL), lambda i, *_: (jnp.minimum(i, n_steps - 1), 0)),
                      pl.BlockSpec((1, D_MODEL), lambda i, *_: (0, 0)),
                      pl.BlockSpec(memory_space=pl.ANY)],
            out_specs=pl.BlockSpec(memory_space=pl.ANY),
            scratch_shapes=[pltpu.VMEM((2, TM_DISPATCH * ROW_TILE, LANES), F32),
                            pltpu.SemaphoreType.DMA, pltpu.SemaphoreType.DMA]),
        out_shape=jax.ShapeDtypeStruct((n_rows * ROW_TILE, LANES), F32),
        compiler_params=pltpu.CompilerParams(dimension_semantics=("arbitrary",),
                                             vmem_limit_bytes=VMEM_LIMIT),
        name="dispatch",
    )(dest, pad_start, pad_count, used_chunks, h, ffn_g, zeros)


X_SLOTS = 3
TILE_CHUNKS = TM_EXPERT // EXPERT_CHUNK


def _expert_kernel(tiles_ref, chunk0_ref, chunks_ref, nt_ref, used_ref, xs_ref, wg_ref, wu_ref, wd_ref,
                   zeros_ref, ys_ref, x_buf, y_buf, sg_buf, su_buf, sd_buf, wgb, wub, wdb, state,
                   w_sems, x_sems, y_sems, zsem):
    t = pl.program_id(0)
    last = pl.num_programs(0) - 1
    nt = nt_ref[0]
    n_chunks = ys_ref.shape[0] // (EXPERT_CHUNK * ROW_TILE)

    def tile_copies(tile, do, out):
        for c in range(TILE_CHUNKS):
            @pl.when(c < chunks_ref[tile])
            def _(c=c):
                first = (chunk0_ref[tile] + c) * EXPERT_CHUNK
                if out:
                    slot = lax.rem(tile, 2)
                    do(pltpu.make_async_copy(_token_rows(y_buf.at[slot], c * EXPERT_CHUNK, EXPERT_CHUNK),
                                             _token_rows(ys_ref, first, EXPERT_CHUNK), y_sems.at[slot]))
                else:
                    slot = lax.rem(tile, X_SLOTS)
                    do(pltpu.make_async_copy(_token_rows(xs_ref, first, EXPERT_CHUNK),
                                             _token_rows(x_buf.at[slot], c * EXPERT_CHUNK, EXPERT_CHUNK),
                                             x_sems.at[slot]))

    start = lambda cp: cp.start()
    wait = lambda cp: cp.wait()

    def tail_copies(do):
        for k in range(N_EXPERTS):
            chunk = used_ref[0] + k

            @pl.when(chunk < n_chunks)
            def _(chunk=chunk):
                do(pltpu.make_async_copy(zeros_ref, _token_rows(ys_ref, chunk * EXPERT_CHUNK, EXPERT_CHUNK),
                                         zsem))

    def weight_copies(e, slot):
        return (pltpu.make_async_copy(wg_ref.at[e], sg_buf.at[slot], w_sems.at[slot]),
                pltpu.make_async_copy(wu_ref.at[e], su_buf.at[slot], w_sems.at[slot]),
                pltpu.make_async_copy(wd_ref.at[e], sd_buf.at[slot], w_sems.at[slot]))

    def next_with_rows(e):
        return lax.while_loop(lambda k: (k < N_EXPERTS) & (tiles_ref[jnp.minimum(k, N_EXPERTS - 1)] == 0),
                              lambda k: k + 1, e + 1)

    @pl.when(t == 0)
    def _():
        first = next_with_rows(jnp.int32(-1))
        state[0] = jnp.int32(-1)
        state[1] = jnp.int32(0)
        state[2] = jnp.int32(1)
        state[3] = first
        for cp in weight_copies(first, 0):
            cp.start()
        tile_copies(0, start, False)

        @pl.when(nt > 1)
        def _():
            tile_copies(1, start, False)

        tail_copies(start)

    @pl.when(t + 2 < nt)
    def _():
        tile_copies(t + 2, start, False)

    @pl.when(t < nt)
    def _():
        @pl.when(state[1] == 0)
        def _():
            e = state[3]
            slot = 1 - state[2]
            nxt = next_with_rows(e)
            state[0] = e
            state[1] = tiles_ref[e]
            state[2] = slot
            state[3] = nxt
            for cp in weight_copies(e, slot):
                cp.wait()

            @pl.when(nxt < N_EXPERTS)
            def _():
                for cp in weight_copies(nxt, 1 - slot):
                    cp.start()

            wgb[...] = sg_buf[slot].astype(BF16)
            wub[...] = su_buf[slot].astype(BF16)
            wdb[...] = sd_buf[slot].astype(BF16)

        state[1] = state[1] - 1
        tile_copies(t, wait, False)

        @pl.when(t >= 2)
        def _():
            tile_copies(t - 2, wait, True)

        for n_chunks_here in range(1, TILE_CHUNKS + 1):
            @pl.when(chunks_ref[t] == n_chunks_here)
            def _(m=n_chunks_here * EXPERT_CHUNK):
                x = _tiles_to_rows(x_buf.at[lax.rem(t, X_SLOTS)], m).astype(BF16)
                g = _dot(x, wgb[...])
                u = _dot(x, wub[...])
                hidden = (g * jax.nn.sigmoid(g)) * u
                _rows_to_tiles(y_buf.at[lax.rem(t, 2)], _dot(hidden.astype(BF16), wdb[...]))

        tile_copies(t, start, True)

    @pl.when(t == last)
    def _():
        for back in (2, 1):
            @pl.when(nt >= back)
            def _(back=back):
                tile_copies(nt - back, wait, True)

        tail_copies(wait)


def _experts(tiles, chunk0, chunks, n_tiles, used_chunks, xs, wg, wu, wd):
    any_spec = pl.BlockSpec(memory_space=pl.ANY)
    zeros = jnp.zeros((EXPERT_CHUNK * ROW_TILE, LANES), F32)
    return pl.pallas_call(
        _expert_kernel,
        grid_spec=pltpu.PrefetchScalarGridSpec(
            num_scalar_prefetch=5,
            grid=(chunks.shape[0],),
            in_specs=[any_spec, any_spec, any_spec, any_spec, any_spec],
            out_specs=any_spec,
            scratch_shapes=[pltpu.VMEM((X_SLOTS, TM_EXPERT * ROW_TILE, LANES), F32),
                            pltpu.VMEM((2, TM_EXPERT * ROW_TILE, LANES), F32),
                            pltpu.VMEM((2, D_MODEL, D_EXPERT), F32),
                            pltpu.VMEM((2, D_MODEL, D_EXPERT), F32),
                            pltpu.VMEM((2, D_EXPERT, D_MODEL), F32),
                            pltpu.VMEM((D_MODEL, D_EXPERT), BF16),
                            pltpu.VMEM((D_MODEL, D_EXPERT), BF16),
                            pltpu.VMEM((D_EXPERT, D_MODEL), BF16),
                            pltpu.SMEM((4,), jnp.int32),
                            pltpu.SemaphoreType.DMA((2,)),
                            pltpu.SemaphoreType.DMA((X_SLOTS,)),
                            pltpu.SemaphoreType.DMA((2,)),
                            pltpu.SemaphoreType.DMA]),
        out_shape=jax.ShapeDtypeStruct(xs.shape, F32),
        compiler_params=pltpu.CompilerParams(dimension_semantics=("arbitrary",),
                                             vmem_limit_bytes=VMEM_LIMIT),
        name="expert_mlp",
    )(tiles, chunk0, chunks, n_tiles, used_chunks, xs, wg, wu, wd, zeros)


def _combine_kernel(dest_ref, h_ref, rw_ref, fg_ref, y_ref, o_ref, buf, sems):
    tm = TM_COMBINE
    i = pl.program_id(0)
    n_steps = pl.num_programs(0)
    n = n_steps * tm
    cur = i % 2

    def fetch(step, half):
        def body(r, c):
            for s in range(2):
                pltpu.make_async_copy(_token_rows(y_ref, dest_ref[s * n + step * tm + r], 1),
                                      _token_rows(buf.at[half, s], r, 1),
                                      sems.at[half]).start(priority=s)
            return c

        lax.fori_loop(0, tm, body, 0, unroll=8)

    @pl.when(i == 0)
    def _():
        fetch(0, 0)

    @pl.when(i + 1 < n_steps)
    def _():
        fetch(i + 1, 1 - cur)

    for s in range(2):
        pltpu.make_async_copy(_token_rows(y_ref, 0, tm), buf.at[cur, s], sems.at[cur]).wait()
    rw = rw_ref[...]
    out = (h_ref[...] + rw[:, 0:1] * _tiles_to_rows(buf.at[cur, 0], tm)
           + rw[:, 1:2] * _tiles_to_rows(buf.at[cur, 1], tm))
    o_ref[...] = _rms(out, fg_ref[...])


def _combine(dest, h, rw, final_g, ys):
    n = h.shape[0]
    return pl.pallas_call(
        _combine_kernel,
        grid_spec=pltpu.PrefetchScalarGridSpec(
            num_scalar_prefetch=1,
            grid=(n // TM_COMBINE,),
            in_specs=[pl.BlockSpec((TM_COMBINE, D_MODEL), lambda i, d: (i, 0)),
                      pl.BlockSpec((TM_COMBINE, LANES), lambda i, d: (i, 0)),
                      pl.BlockSpec((1, D_MODEL), lambda i, d: (0, 0)),
                      pl.BlockSpec(memory_space=pl.ANY)],
            out_specs=pl.BlockSpec((TM_COMBINE, D_MODEL), lambda i, d: (i, 0)),
            scratch_shapes=[pltpu.VMEM((2, 2, TM_COMBINE * ROW_TILE, LANES), F32),
                            pltpu.SemaphoreType.DMA((2,))]),
        out_shape=jax.ShapeDtypeStruct((n, D_MODEL), F32),
        compiler_params=pltpu.CompilerParams(dimension_semantics=("arbitrary",),
                                             vmem_limit_bytes=VMEM_LIMIT),
        name="combine",
    )(dest, h, rw, final_g, ys)


def _schedule(counts, max_tiles):
    chunks = (counts + EXPERT_CHUNK - 1) // EXPERT_CHUNK
    chunk_end = jnp.cumsum(chunks)
    chunk_start = chunk_end - chunks
    tiles = (chunks + TILE_CHUNKS - 1) // TILE_CHUNKS
    tile_end = jnp.cumsum(tiles)
    tile = jnp.arange(max_tiles, dtype=jnp.int32)
    owner = jnp.sum(tile[:, None] >= tile_end[None, :], axis=1)
    is_owner = owner[:, None] == jnp.arange(N_EXPERTS, dtype=jnp.int32)[None, :]
    of_owner = lambda v: jnp.sum(jnp.where(is_owner, v[None, :], 0), axis=1)
    done = (tile - of_owner(tile_end - tiles)) * TILE_CHUNKS
    tile_chunk0 = (of_owner(chunk_start) + done).astype(jnp.int32)
    tile_chunks = jnp.clip(of_owner(chunks) - done, 0, TILE_CHUNKS).astype(jnp.int32)
    return tiles, chunk_start * EXPERT_CHUNK, tile_chunk0, tile_chunks, tile_end[-1:], chunk_end[-1:]


def _layer(x, attn_g, w_in, sg_g, w_sp, b_sp, sb_g, sg_out_g, w_out, ffn_g,
           w_rg, b_rg, w_re, b_re, w_gate, w_up, w_down):
    batch, seq, _ = x.shape
    n = batch * seq
    x2 = x.reshape(n, D_MODEL)
    row = lambda v: v.reshape(1, -1)

    bsp_full = jnp.repeat(b_sp.T, HEAD_DIM, axis=1)
    qkv, sgn = _inproj(x2, row(attn_g), w_in.astype(BF16), row(sg_g), w_sp, bsp_full, row(sg_out_g))
    sb = _attention(qkv, batch, seq).reshape(n, SB_WIDTH)

    pad_lanes = lambda v, width: jnp.pad(v, [(0, 0)] * (v.ndim - 1) + [(0, width - v.shape[-1])])
    w_r = jnp.concatenate([pad_lanes(w_rg, ROUTER_LANE0),
                           jnp.transpose(w_re, (1, 0, 2)).reshape(D_MODEL, N_EXPERTS)], axis=1)
    w_r = pad_lanes(w_r, LANES)
    wr_hi = w_r.astype(BF16)
    wr_lo = (w_r - wr_hi.astype(F32)).astype(BF16)
    wr2 = jnp.concatenate([wr_hi, wr_lo], axis=1)
    b_r = pad_lanes(jnp.concatenate([pad_lanes(b_rg, ROUTER_LANE0), b_re.reshape(-1)]), LANES)

    h, lg = _mix(sb, sgn, x2, row(sb_g), w_out.astype(BF16), row(ffn_g), wr2, row(b_r))
    ri, rw, cnt = _route(lg)

    counts = cnt[:, 0].astype(jnp.int32)
    n_rows = 2 * n + N_EXPERTS * EXPERT_CHUNK
    tiles, offsets, tile_chunk0, tile_chunks, n_tiles, used_chunks = _schedule(
        counts, 2 * n // TM_EXPERT + N_EXPERTS)
    expert, rank = ri[0:2], ri[2:4]
    is_e = expert[None] == jnp.arange(N_EXPERTS, dtype=jnp.int32)[:, None, None]
    dest = (jnp.sum(jnp.where(is_e, offsets[:, None, None], 0), axis=0) + rank).reshape(-1)
    pad_start = offsets + counts
    pad_count = (-counts) % EXPERT_CHUNK

    xs = _dispatch(dest, pad_start, pad_count, used_chunks, h, row(ffn_g), n_rows)
    ys = _experts(tiles, tile_chunk0, tile_chunks, n_tiles, used_chunks, xs,
                  w_gate.reshape(N_EXPERTS, D_MODEL, D_EXPERT),
                  w_up.reshape(N_EXPERTS, D_MODEL, D_EXPERT),
                  w_down.reshape(N_EXPERTS, D_EXPERT, D_MODEL))
    return dest, h, rw, ys


def kernel(x, attn_norm_g, w_in, sg_norm_g, w_spatial, b_spatial, sb_out_norm_g, sg_out_norm_g,
           w_out, ffn_norm_g, w_router_group, b_router_group, w_router_expert, b_router_expert,
           w_gate, w_up, w_down, final_norm_g):
    assert attn_norm_g.shape[0] == 1, "single-layer problem"
    batch, seq, _ = x.shape
    dest, h, rw, ys = _layer(x, attn_norm_g[0], w_in[0], sg_norm_g[0], w_spatial[0], b_spatial[0],
                             sb_out_norm_g[0], sg_out_norm_g[0], w_out[0], ffn_norm_g[0],
                             w_router_group[0], b_router_group[0], w_router_expert[0],
                             b_router_expert[0], w_gate[0], w_up[0], w_down[0])
    out = _combine(dest, h, rw, final_norm_g.reshape(1, -1), ys)
    return out.reshape(batch, seq, D_MODEL)
```

```python
import functools
import math

import jax
import jax.numpy as jnp
from jax import lax
from jax.experimental import pallas as pl
from jax.experimental.pallas import tpu as pltpu

D_MODEL = 1024
HEAD_DIM = 64
SB_WIDTH = 512
SG_WIDTH = 512
SG_HEADS = 8
D_IN = 3 * SB_WIDTH + 2 * SG_WIDTH
CHUNK = 128
N_GROUPS = 4
EXPERTS_PER_GROUP = 8
N_EXPERTS = N_GROUPS * EXPERTS_PER_GROUP
D_EXPERT = 512
EPS = 1e-6
F32_EXP_UNDERFLOW = 110.0

LANES = 128
SUBLANES = 8
ROW_TILE = D_MODEL // LANES
assert ROW_TILE == SUBLANES
HEAD_PAIR = 2 * HEAD_DIM
ROUTER_LANE0 = SUBLANES
ROUTER_ROWS = ROUTER_LANE0 + N_EXPERTS
assert EXPERTS_PER_GROUP == SUBLANES and N_GROUPS <= ROUTER_LANE0

TM_PROJ = 1024
TQ_ATTN = 256
ATTN_BLOCKS_PER_STEP = 2
ATTN_TOP_ROWS = 160
TM_MIX = 1024
TM_ROUTE = 1024
TM_DISPATCH = 1024
TM_EXPERT = 512
EXPERT_CHUNK = 128
TM_COMBINE = 512
VMEM_LIMIT = 48 * 1024 * 1024

F32 = jnp.float32
BF16 = jnp.bfloat16


def _rms(x, g):
    return x * lax.rsqrt(jnp.mean(x * x, axis=-1, keepdims=True) + EPS) * g


def _gelu(x):
    c = math.sqrt(2.0 / math.pi)
    return x * (0.5 * (1.0 + jnp.tanh(c * (x + 0.044715 * (x * x * x)))))


def _softplus(z):
    return jnp.maximum(z, 0.0) + jnp.log(1.0 + jnp.exp(-jnp.abs(z)))


def _dot(a, b):
    return jnp.dot(a, b, preferred_element_type=F32)


def _rows_to_tiles(ref, x):
    m = x.shape[0]
    for k in range(ROW_TILE):
        ref[pl.ds(k, m, stride=ROW_TILE), :] = x[:, k * LANES:(k + 1) * LANES]


def _tiles_to_rows(ref, m):
    return jnp.concatenate([ref[pl.ds(k, m, stride=ROW_TILE), :] for k in range(ROW_TILE)], axis=1)


def _token_rows(ref, first_token, n_tokens):
    return ref.at[pl.ds(pl.multiple_of(first_token * ROW_TILE, ROW_TILE), n_tokens * ROW_TILE)]


def _split_bf16(x):
    hi = x.astype(BF16)
    lo = (x - hi.astype(F32)).astype(BF16)
    return hi, lo


def _inproj_kernel(x_ref, g_ref, w_ref, sgg_ref, wsp_ref, bsp_ref, sgog_ref, qkv_ref, sgn_ref,
                   gu_ref, vgn_ref, sg_ref):
    tm = TM_PROJ
    hb = _rms(x_ref[...], g_ref[...]).astype(BF16)
    gv = _gelu(_dot(hb, w_ref[:, 3 * SB_WIDTH + SG_WIDTH:D_IN]))
    vgn_ref[...] = _rms(gv, sgg_ref[...]).astype(BF16)
    gu_ref[...] = _gelu(_dot(hb, w_ref[:, 3 * SB_WIDTH:3 * SB_WIDTH + SG_WIDTH]))
    q = _dot(hb, w_ref[:, 0:SB_WIDTH]) * (1.0 / math.sqrt(HEAD_DIM))
    qkv_ref[:, 0:SB_WIDTH] = q.astype(BF16)
    qkv_ref[:, SB_WIDTH:2 * SB_WIDTH] = _dot(hb, w_ref[:, SB_WIDTH:2 * SB_WIDTH]).astype(BF16)

    lane = lax.broadcasted_iota(jnp.int32, (1, LANES), 1)
    first = lane < HEAD_DIM
    zero = jnp.zeros((), BF16)
    r_c = lax.broadcasted_iota(jnp.int32, (CHUNK, CHUNK), 0)
    c_c = lax.broadcasted_iota(jnp.int32, (CHUNK, CHUNK), 1)
    tril = r_c >= c_c
    n_pairs = SG_WIDTH // HEAD_PAIR
    w_pairs = []
    for p in range(n_pairs):
        w0 = jnp.where(tril, wsp_ref[2 * p], 0.0).astype(BF16)
        w1 = jnp.where(tril, wsp_ref[2 * p + 1], 0.0).astype(BF16)
        w_pairs.append(jnp.concatenate([w0, w1], axis=1))
    bsp = bsp_ref[...]
    for c in range(tm // CHUNK):
        rows = slice(c * CHUNK, (c + 1) * CHUNK)
        for p in range(n_pairs):
            cols = slice(p * HEAD_PAIR, (p + 1) * HEAD_PAIR)
            vg = vgn_ref[rows, cols]
            rhs = jnp.concatenate([jnp.where(first, vg, zero), jnp.where(first, zero, vg)], axis=0)
            mixed = _dot(w_pairs[p], rhs) + bsp[:, cols]
            sg_ref[rows, cols] = gu_ref[rows, cols] * mixed
    qkv_ref[:, 2 * SB_WIDTH:3 * SB_WIDTH] = _dot(hb, w_ref[:, 2 * SB_WIDTH:3 * SB_WIDTH]).astype(BF16)
    sgn_ref[...] = _rms(sg_ref[...], sgog_ref[...]).astype(BF16)


def _inproj(x2, attn_g, w_in_b, sg_g, wsp, bsp_full, sg_out_g):
    n = x2.shape[0]
    row = lambda i: (i, 0)
    const = lambda i: (0, 0)
    return pl.pallas_call(
        _inproj_kernel,
        grid=(n // TM_PROJ,),
        in_specs=[pl.BlockSpec((TM_PROJ, D_MODEL), row),
                  pl.BlockSpec((1, D_MODEL), const),
                  pl.BlockSpec((D_MODEL, D_IN), const),
                  pl.BlockSpec((1, SG_WIDTH), const),
                  pl.BlockSpec((SG_HEADS, CHUNK, CHUNK), lambda i: (0, 0, 0)),
                  pl.BlockSpec((CHUNK, SG_WIDTH), const),
                  pl.BlockSpec((1, SG_WIDTH), const)],
        out_specs=[pl.BlockSpec((TM_PROJ, 3 * SB_WIDTH), row),
                   pl.BlockSpec((TM_PROJ, SG_WIDTH), row)],
        out_shape=[jax.ShapeDtypeStruct((n, 3 * SB_WIDTH), BF16),
                   jax.ShapeDtypeStruct((n, SG_WIDTH), BF16)],
        scratch_shapes=[pltpu.VMEM((TM_PROJ, SG_WIDTH), F32),
                        pltpu.VMEM((TM_PROJ, SG_WIDTH), BF16),
                        pltpu.VMEM((TM_PROJ, SG_WIDTH), F32)],
        compiler_params=pltpu.CompilerParams(dimension_semantics=("arbitrary",),
                                             vmem_limit_bytes=VMEM_LIMIT),
        name="inproj",
    )(x2, attn_g, w_in_b, sg_g, wsp, bsp_full, sg_out_g)


def _attn_kernel(q_ref, k_ref, v_ref, o_ref, q2_ref, carry_ref):
    t = TQ_ATTN
    n_pairs = SB_WIDTH // HEAD_PAIR
    lane = lax.broadcasted_iota(jnp.int32, (1, HEAD_PAIR), 1)
    head_lanes = (lane < HEAD_DIM, lane >= HEAD_DIM)
    zero = jnp.zeros((), BF16)
    r_idx = lax.broadcasted_iota(jnp.int32, (t, t), 0)
    c_idx = lax.broadcasted_iota(jnp.int32, (t, t), 1)
    suffix = (r_idx > c_idx).astype(BF16)
    suffix2 = jnp.concatenate([suffix, suffix], axis=0)
    causal = c_idx < r_idx

    def one_query_block(sub, c):
        qi = pl.program_id(1) * ATTN_BLOCKS_PER_STEP + sub
        row0 = pl.multiple_of(sub * t, t)
        for p in range(n_pairs):
            qp = q_ref[0, pl.ds(row0, t), p * HEAD_PAIR:(p + 1) * HEAD_PAIR]
            for h in range(2):
                q2_ref[(2 * p + h) * t:(2 * p + h + 1) * t, :] = jnp.where(head_lanes[h], qp, zero)
        o_ref[0, pl.ds(row0, t), :] = jnp.zeros((t, SB_WIDTH), F32)
        carry_ref[...] = jnp.zeros_like(carry_ref)

        def block(j, diag, m):
            start = pl.multiple_of(j * t, t)
            mask2 = jnp.concatenate([causal, causal], axis=0) if diag else None
            st = [dict() for _ in range(n_pairs)]

            def head_rows(p):
                return [slice((2 * p + h) * t, (2 * p + h) * t + m) for h in range(2)]

            def scores(p):
                d = st[p]
                d["cols"] = slice(p * HEAD_PAIR, (p + 1) * HEAD_PAIR)
                kb = k_ref[0, pl.ds(start, t), d["cols"]]
                q2 = jnp.concatenate([q2_ref[r, :] for r in head_rows(p)], axis=0)
                z = lax.dot_general(q2, kb, (((1,), (1,)), ((), ())),
                                    preferred_element_type=F32)
                sp = _softplus(z)
                nl = jnp.where(mask2, sp, 0.0) if diag else sp
                hi, lo = _split_bf16(nl)
                d["hl"] = jnp.concatenate([hi, lo], axis=1)
                d["log_beta"] = z - sp
                d["nl0"] = nl[:, 0:1]

            def weights(p):
                d = st[p]
                hl = d["hl"]
                after = jnp.concatenate([_dot(hl[0:m], suffix2), _dot(hl[m:2 * m], suffix2)], axis=0)
                carry = jnp.concatenate([carry_ref[r, :] for r in head_rows(p)], axis=0)
                a = jnp.exp(d["log_beta"] - after - carry)
                if diag:
                    a = jnp.where(mask2, a, 0.0)
                a = a.astype(BF16)
                d["a2"] = jnp.concatenate([a[0:m], a[m:2 * m]], axis=1)
                new_carry = carry + after[:, 0:1] + d["nl0"]
                for h, r in enumerate(head_rows(p)):
                    carry_ref[r, :] = new_carry[h * m:(h + 1) * m]

            def values(p):
                d = st[p]
                vb = v_ref[0, pl.ds(start, t), d["cols"]]
                v2 = jnp.concatenate([jnp.where(head_lanes[0], vb, zero),
                                      jnp.where(head_lanes[1], vb, zero)], axis=0)
                o_ref[0, pl.ds(row0, m), d["cols"]] += _dot(d["a2"], v2)

            for step in range(n_pairs + 2):
                if step < n_pairs:
                    scores(step)
                if 0 <= step - 1 < n_pairs:
                    weights(step - 1)
                if 0 <= step - 2 < n_pairs:
                    values(step - 2)

        top = ATTN_TOP_ROWS

        def flags():
            bottom = jnp.concatenate([carry_ref[hh * t + top:(hh + 1) * t, :] for hh in range(2 * n_pairs)], axis=0)
            return (jnp.min(carry_ref[...]) < F32_EXP_UNDERFLOW, jnp.min(bottom) >= F32_EXP_UNDERFLOW)

        block(qi, True, t)

        def body(state):
            it, _, bottom_done = state
            j = qi - 1 - it

            @pl.when(bottom_done)
            def _():
                block(j, False, top)

            @pl.when(jnp.logical_not(bottom_done))
            def _():
                block(j, False, t)

            return (it + 1,) + flags()

        lax.while_loop(lambda s: (s[0] < qi) & s[1], body, (jnp.int32(0),) + flags())
        return c

    lax.fori_loop(0, ATTN_BLOCKS_PER_STEP, one_query_block, 0)


def _attention(qkv, batch, seq):
    qkv3 = qkv.reshape(batch, seq, 3 * SB_WIDTH)
    n_heads = SB_WIDTH // HEAD_DIM
    return pl.pallas_call(
        _attn_kernel,
        grid=(batch, seq // (ATTN_BLOCKS_PER_STEP * TQ_ATTN)),
        in_specs=[pl.BlockSpec((1, ATTN_BLOCKS_PER_STEP * TQ_ATTN, SB_WIDTH), lambda b, i: (b, i, 0)),
                  pl.BlockSpec((1, seq, SB_WIDTH), lambda b, i: (b, 0, 1)),
                  pl.BlockSpec((1, seq, SB_WIDTH), lambda b, i: (b, 0, 2))],
        out_specs=pl.BlockSpec((1, ATTN_BLOCKS_PER_STEP * TQ_ATTN, SB_WIDTH), lambda b, i: (b, i, 0)),
        out_shape=jax.ShapeDtypeStruct((batch, seq, SB_WIDTH), F32),
        scratch_shapes=[pltpu.VMEM((n_heads * TQ_ATTN, HEAD_PAIR), BF16),
                        pltpu.VMEM((n_heads * TQ_ATTN, 1), F32)],
        compiler_params=pltpu.CompilerParams(dimension_semantics=("arbitrary",) * 2,
                                             vmem_limit_bytes=VMEM_LIMIT),
        name="sb_attention",
    )(qkv3, qkv3, qkv3)


def _mix_kernel(sb_ref, sgn_ref, x_ref, sbg_ref, wout_ref, ffng_ref, wr2_ref, br_ref,
                h_ref, lg_ref):
    sbn = _rms(sb_ref[...], sbg_ref[...]).astype(BF16)
    h = x_ref[...] + _dot(sbn, wout_ref[0:SB_WIDTH, :]) + _dot(sgn_ref[...], wout_ref[SB_WIDTH:, :])
    h_ref[...] = h
    hn = _rms(h, ffng_ref[...])

    hn_hi, hn_lo = _split_bf16(hn)
    both = _dot(hn_hi, wr2_ref[...])
    logits = both[:, 0:LANES] + both[:, LANES:] + _dot(hn_lo, wr2_ref[:, 0:LANES]) + br_ref[...]
    lg_ref[...] = logits.T[0:ROUTER_ROWS, :]


def _route_kernel(lg_ref, ri_ref, rw_ref, cnt_ref, count_ref):
    tr = TM_ROUTE
    i = pl.program_id(0)

    @pl.when(i == 0)
    def _():
        count_ref[...] = jnp.zeros_like(count_ref)

    neg = jnp.float32(-jnp.inf)
    row8 = lax.broadcasted_iota(jnp.int32, (SUBLANES, tr), 0)

    def top(v):
        m = jnp.max(v, axis=0, keepdims=True)
        return m, jnp.min(jnp.where(v == m, row8, SUBLANES), axis=0, keepdims=True)

    def group_rows(g):
        return lg_ref[ROUTER_LANE0 + g * EXPERTS_PER_GROUP:ROUTER_LANE0 + (g + 1) * EXPERTS_PER_GROUP, :]

    gl = jnp.where(row8 < N_GROUPS, lg_ref[0:SUBLANES, :], neg)
    gmax, gidx = top(gl)
    gweight = 1.0 / jnp.sum(jnp.exp(gl - gmax), axis=0, keepdims=True)
    el = group_rows(0)
    for g in range(1, N_GROUPS):
        el = jnp.where(gidx == g, group_rows(g), el)
    m1, i1 = top(el)
    m2, i2 = top(jnp.where(row8 == i1, neg, el))
    t21 = jnp.exp(m2 - m1)
    w1 = gweight / (1.0 + t21)
    w2 = gweight * t21 / (1.0 + t21)
    e1 = gidx * EXPERTS_PER_GROUP + i1
    e2 = gidx * EXPERTS_PER_GROUP + i2

    row_e = lax.broadcasted_iota(jnp.int32, (N_EXPERTS, tr), 0)
    sel1 = row_e == e1
    sel2 = row_e == e2
    onehot = jnp.where(sel1 | sel2, 1.0, 0.0)
    r_t = lax.broadcasted_iota(jnp.int32, (tr, tr), 0)
    c_t = lax.broadcasted_iota(jnp.int32, (tr, tr), 1)
    before = (r_t < c_t).astype(BF16)
    running = count_ref[:, 0:1] + _dot(onehot.astype(BF16), before)
    rank1 = jnp.sum(jnp.where(sel1, running, 0.0), axis=0, keepdims=True)
    rank2 = jnp.sum(jnp.where(sel2, running, 0.0), axis=0, keepdims=True)
    new_count = count_ref[:, 0:1] + jnp.sum(onehot, axis=1, keepdims=True)
    count_ref[...] = jnp.broadcast_to(new_count, count_ref.shape)
    cnt_ref[...] = jnp.broadcast_to(new_count, cnt_ref.shape)

    ri_ref[...] = jnp.where(row8 == 0, e1, jnp.where(row8 == 1, e2, jnp.where(
        row8 == 2, rank1.astype(jnp.int32), jnp.where(row8 == 3, rank2.astype(jnp.int32), 0))))
    row128 = lax.broadcasted_iota(jnp.int32, (LANES, tr), 0)
    rw_ref[...] = jnp.where(row128 == 0, w1, jnp.where(row128 == 1, w2, 0.0)).T


def _route(lg):
    n = lg.shape[1]
    return pl.pallas_call(
        _route_kernel,
        grid=(n // TM_ROUTE,),
        in_specs=[pl.BlockSpec((ROUTER_ROWS, TM_ROUTE), lambda i: (0, i))],
        out_specs=[pl.BlockSpec((SUBLANES, TM_ROUTE), lambda i: (0, i)),
                   pl.BlockSpec((TM_ROUTE, LANES), lambda i: (i, 0)),
                   pl.BlockSpec((N_EXPERTS, LANES), lambda i: (0, 0))],
        out_shape=[jax.ShapeDtypeStruct((SUBLANES, n), jnp.int32),
                   jax.ShapeDtypeStruct((n, LANES), F32),
                   jax.ShapeDtypeStruct((N_EXPERTS, LANES), F32)],
        scratch_shapes=[pltpu.VMEM((N_EXPERTS, LANES), F32)],
        compiler_params=pltpu.CompilerParams(dimension_semantics=("arbitrary",),
                                             vmem_limit_bytes=VMEM_LIMIT),
        name="route",
    )(lg)


def _mix(sb, sgn, x2, sb_g, w_out_b, ffn_g, wr2, br):
    n = x2.shape[0]
    row = lambda i: (i, 0)
    const = lambda i: (0, 0)
    return pl.pallas_call(
        _mix_kernel,
        grid=(n // TM_MIX,),
        in_specs=[pl.BlockSpec((TM_MIX, SB_WIDTH), row),
                  pl.BlockSpec((TM_MIX, SG_WIDTH), row),
                  pl.BlockSpec((TM_MIX, D_MODEL), row),
                  pl.BlockSpec((1, SB_WIDTH), const),
                  pl.BlockSpec((D_MODEL, D_MODEL), const),
                  pl.BlockSpec((1, D_MODEL), const),
                  pl.BlockSpec((D_MODEL, 2 * LANES), const),
                  pl.BlockSpec((1, LANES), const)],
        out_specs=[pl.BlockSpec((TM_MIX, D_MODEL), row),
                   pl.BlockSpec((ROUTER_ROWS, TM_MIX), lambda i: (0, i))],
        out_shape=[jax.ShapeDtypeStruct((n, D_MODEL), F32),
                   jax.ShapeDtypeStruct((ROUTER_ROWS, n), F32)],
        compiler_params=pltpu.CompilerParams(dimension_semantics=("arbitrary",),
                                             vmem_limit_bytes=VMEM_LIMIT),
        name="mix_router",
    )(sb, sgn, x2, sb_g, w_out_b, ffn_g, wr2, br)


_PAD_BITS = tuple(1 << b for b in reversed(range(EXPERT_CHUNK.bit_length() - 1)))


def _dispatch_kernel(dest_ref, pad_start_ref, pad_count_ref, used_ref, h_ref, g_ref, zeros_ref, xs_ref,
                     hn_ref, sem, zsem):
    tm = TM_DISPATCH
    i = pl.program_id(0)
    n_steps = pl.num_programs(0) - 1
    n = n_steps * tm
    base = (i - 1) * tm
    prev = hn_ref.at[lax.rem(i + 1, 2)]
    n_chunks = xs_ref.shape[0] // (EXPERT_CHUNK * ROW_TILE)

    def pad_copies(do):
        for e in range(N_EXPERTS):
            start = pad_start_ref[e]
            count = pad_count_ref[e]
            for bit in _PAD_BITS:
                @pl.when((count & bit) != 0)
                def _(start=start, bit=bit):
                    do(pltpu.make_async_copy(_token_rows(zeros_ref, 0, bit),
                                             _token_rows(xs_ref, start, bit), zsem))
                start = start + (count & bit)
        for k in range(N_EXPERTS):
            chunk = used_ref[0] + k

            @pl.when(chunk < n_chunks)
            def _(chunk=chunk):
                do(pltpu.make_async_copy(zeros_ref, _token_rows(xs_ref, chunk * EXPERT_CHUNK, EXPERT_CHUNK),
                                         zsem))

    @pl.when(i == 0)
    def _():
        pad_copies(lambda cp: cp.start())

    @pl.when(i > 0)
    def _():
        def body(r, c):
            src = _token_rows(prev, r, 1)
            for s in range(2):
                pltpu.make_async_copy(src, _token_rows(xs_ref, dest_ref[s * n + base + r], 1),
                                      sem).start(priority=s)
            return c

        lax.fori_loop(0, tm, body, 0, unroll=8)

    @pl.when(i < n_steps)
    def _():
        _rows_to_tiles(hn_ref.at[lax.rem(i, 2)], _rms(h_ref[...], g_ref[...]))

    @pl.when(i > 0)
    def _():
        for _ in range(2):
            pltpu.make_async_copy(prev, _token_rows(xs_ref, 0, tm), sem).wait()

    @pl.when(i == n_steps)
    def _():
        pad_copies(lambda cp: cp.wait())


def _dispatch(dest, pad_start, pad_count, used_chunks, h, ffn_g, n_rows):
    n_steps = h.shape[0] // TM_DISPATCH
    zeros = jnp.zeros((EXPERT_CHUNK * ROW_TILE, LANES), F32)
    return pl.pallas_call(
        _dispatch_kernel,
        grid_spec=pltpu.PrefetchScalarGridSpec(
            num_scalar_prefetch=4,
            grid=(n_steps + 1,),
            in_specs=[pl.BlockSpec((TM_DISPATCH, D_MODEL), lambda i, *_: (jnp.minimum(i, n_steps - 1), 0)),
                      pl.BlockSpec((1, D_MODEL), lambda i, *_: (0, 0)),
                      pl.BlockSpec(memory_space=pl.ANY)],
            out_specs=pl.BlockSpec(memory_space=pl.ANY),
            scratch_shapes=[pltpu.VMEM((2, TM_DISPATCH * ROW_TILE, LANES), F32),
                            pltpu.SemaphoreType.DMA, pltpu.SemaphoreType.DMA]),
        out_shape=jax.ShapeDtypeStruct((n_rows * ROW_TILE, LANES), F32),
        compiler_params=pltpu.CompilerParams(dimension_semantics=("arbitrary",),
                                             vmem_limit_bytes=VMEM_LIMIT),
        name="dispatch",
    )(dest, pad_start, pad_count, used_chunks, h, ffn_g, zeros)


X_SLOTS = 3
TILE_CHUNKS = TM_EXPERT // EXPERT_CHUNK
W_SLOTS = 3


def _expert_kernel(tiles_ref, chunk0_ref, chunks_ref, nt_ref, used_ref, xs_ref, wg_ref, wu_ref, wd_ref,
                   zeros_ref, ys_ref, x_buf, y_buf, sg_buf, su_buf, sd_buf, wgb, wub, wdb, state,
                   w_sems, x_sems, y_sems, zsem):
    t = pl.program_id(0)
    last = pl.num_programs(0) - 1
    nt = nt_ref[0]
    n_chunks = ys_ref.shape[0] // (EXPERT_CHUNK * ROW_TILE)

    def tile_copies(tile, do, out):
        for c in range(TILE_CHUNKS):
            @pl.when(c < chunks_ref[tile])
            def _(c=c):
                first = (chunk0_ref[tile] + c) * EXPERT_CHUNK
                if out:
                    slot = lax.rem(tile, 2)
                    do(pltpu.make_async_copy(_token_rows(y_buf.at[slot], c * EXPERT_CHUNK, EXPERT_CHUNK),
                                             _token_rows(ys_ref, first, EXPERT_CHUNK), y_sems.at[slot]))
                else:
                    slot = lax.rem(tile, X_SLOTS)
                    do(pltpu.make_async_copy(_token_rows(xs_ref, first, EXPERT_CHUNK),
                                             _token_rows(x_buf.at[slot], c * EXPERT_CHUNK, EXPERT_CHUNK),
                                             x_sems.at[slot]))

    start = lambda cp: cp.start()
    wait = lambda cp: cp.wait()

    def tail_copies(do):
        for k in range(N_EXPERTS):
            chunk = used_ref[0] + k

            @pl.when(chunk < n_chunks)
            def _(chunk=chunk):
                do(pltpu.make_async_copy(zeros_ref, _token_rows(ys_ref, chunk * EXPERT_CHUNK, EXPERT_CHUNK),
                                         zsem))

    def weight_copies(e, slot):
        return (pltpu.make_async_copy(wg_ref.at[e], sg_buf.at[slot], w_sems.at[slot]),
                pltpu.make_async_copy(wu_ref.at[e], su_buf.at[slot], w_sems.at[slot]),
                pltpu.make_async_copy(wd_ref.at[e], sd_buf.at[slot], w_sems.at[slot]))

    def next_with_rows(e):
        return lax.while_loop(lambda k: (k < N_EXPERTS) & (tiles_ref[jnp.minimum(k, N_EXPERTS - 1)] == 0),
                              lambda k: k + 1, e + 1)

    @pl.when(t == 0)
    def _():
        first = next_with_rows(jnp.int32(-1))
        second = next_with_rows(first)
        state[0] = jnp.int32(-1)
        state[1] = jnp.int32(0)
        state[2] = jnp.int32(W_SLOTS - 1)
        state[3] = first
        state[4] = second
        for cp in weight_copies(first, 0):
            cp.start()

        @pl.when(second < N_EXPERTS)
        def _():
            for cp in weight_copies(second, 1):
                cp.start()

        tile_copies(0, start, False)

        @pl.when(nt > 1)
        def _():
            tile_copies(1, start, False)

        tail_copies(start)

    @pl.when(t + 2 < nt)
    def _():
        tile_copies(t + 2, start, False)

    @pl.when(t < nt)
    def _():
        @pl.when(state[1] == 0)
        def _():
            e = state[3]
            nxt = state[4]
            slot = lax.rem(state[2] + 1, W_SLOTS)
            after_next = next_with_rows(nxt)
            state[0] = e
            state[1] = tiles_ref[e]
            state[2] = slot
            state[3] = nxt
            state[4] = after_next
            for cp in weight_copies(e, slot):
                cp.wait()

            @pl.when(after_next < N_EXPERTS)
            def _():
                for cp in weight_copies(after_next, lax.rem(slot + 2, W_SLOTS)):
                    cp.start()

            wgb[...] = sg_buf[slot].astype(BF16)
            wub[...] = su_buf[slot].astype(BF16)
            wdb[...] = sd_buf[slot].astype(BF16)

        state[1] = state[1] - 1
        tile_copies(t, wait, False)

        @pl.when(t >= 2)
        def _():
            tile_copies(t - 2, wait, True)

        for n_chunks_here in range(1, TILE_CHUNKS + 1):
            @pl.when(chunks_ref[t] == n_chunks_here)
            def _(m=n_chunks_here * EXPERT_CHUNK):
                x = _tiles_to_rows(x_buf.at[lax.rem(t, X_SLOTS)], m).astype(BF16)
                g = _dot(x, wgb[...])
                u = _dot(x, wub[...])
                hidden = (g * jax.nn.sigmoid(g)) * u
                _rows_to_tiles(y_buf.at[lax.rem(t, 2)], _dot(hidden.astype(BF16), wdb[...]))

        tile_copies(t, start, True)

    @pl.when(t == last)
    def _():
        for back in (2, 1):
            @pl.when(nt >= back)
            def _(back=back):
                tile_copies(nt - back, wait, True)

        tail_copies(wait)


def _experts(tiles, chunk0, chunks, n_tiles, used_chunks, xs, wg, wu, wd):
    any_spec = pl.BlockSpec(memory_space=pl.ANY)
    zeros = jnp.zeros((EXPERT_CHUNK * ROW_TILE, LANES), F32)
    return pl.pallas_call(
        _expert_kernel,
        grid_spec=pltpu.PrefetchScalarGridSpec(
            num_scalar_prefetch=5,
            grid=(chunks.shape[0],),
            in_specs=[any_spec, any_spec, any_spec, any_spec, any_spec],
            out_specs=any_spec,
            scratch_shapes=[pltpu.VMEM((X_SLOTS, TM_EXPERT * ROW_TILE, LANES), F32),
                            pltpu.VMEM((2, TM_EXPERT * ROW_TILE, LANES), F32),
                            pltpu.VMEM((W_SLOTS, D_MODEL, D_EXPERT), F32),
                            pltpu.VMEM((W_SLOTS, D_MODEL, D_EXPERT), F32),
                            pltpu.VMEM((W_SLOTS, D_EXPERT, D_MODEL), F32),
                            pltpu.VMEM((D_MODEL, D_EXPERT), BF16),
                            pltpu.VMEM((D_MODEL, D_EXPERT), BF16),
                            pltpu.VMEM((D_EXPERT, D_MODEL), BF16),
                            pltpu.SMEM((5,), jnp.int32),
                            pltpu.SemaphoreType.DMA((W_SLOTS,)),
                            pltpu.SemaphoreType.DMA((X_SLOTS,)),
                            pltpu.SemaphoreType.DMA((2,)),
                            pltpu.SemaphoreType.DMA]),
        out_shape=jax.ShapeDtypeStruct(xs.shape, F32),
        compiler_params=pltpu.CompilerParams(dimension_semantics=("arbitrary",),
                                             vmem_limit_bytes=VMEM_LIMIT),
        name="expert_mlp",
    )(tiles, chunk0, chunks, n_tiles, used_chunks, xs, wg, wu, wd, zeros)


def _combine_kernel(dest_ref, h_ref, rw_ref, fg_ref, y_ref, o_ref, buf, sems):
    tm = TM_COMBINE
    i = pl.program_id(0)
    n_steps = pl.num_programs(0)
    n = n_steps * tm
    cur = i % 2

    def fetch(step, half):
        def body(r, c):
            for s in range(2):
                pltpu.make_async_copy(_token_rows(y_ref, dest_ref[s * n + step * tm + r], 1),
                                      _token_rows(buf.at[half, s], r, 1),
                                      sems.at[half]).start(priority=s)
            return c

        lax.fori_loop(0, tm, body, 0, unroll=8)

    @pl.when(i == 0)
    def _():
        fetch(0, 0)

    @pl.when(i + 1 < n_steps)
    def _():
        fetch(i + 1, 1 - cur)

    for s in range(2):
        pltpu.make_async_copy(_token_rows(y_ref, 0, tm), buf.at[cur, s], sems.at[cur]).wait()
    rw = rw_ref[...]
    out = (h_ref[...] + rw[:, 0:1] * _tiles_to_rows(buf.at[cur, 0], tm)
           + rw[:, 1:2] * _tiles_to_rows(buf.at[cur, 1], tm))
    o_ref[...] = _rms(out, fg_ref[...])


def _combine(dest, h, rw, final_g, ys):
    n = h.shape[0]
    return pl.pallas_call(
        _combine_kernel,
        grid_spec=pltpu.PrefetchScalarGridSpec(
            num_scalar_prefetch=1,
            grid=(n // TM_COMBINE,),
            in_specs=[pl.BlockSpec((TM_COMBINE, D_MODEL), lambda i, d: (i, 0)),
                      pl.BlockSpec((TM_COMBINE, LANES), lambda i, d: (i, 0)),
                      pl.BlockSpec((1, D_MODEL), lambda i, d: (0, 0)),
                      pl.BlockSpec(memory_space=pl.ANY)],
            out_specs=pl.BlockSpec((TM_COMBINE, D_MODEL), lambda i, d: (i, 0)),
            scratch_shapes=[pltpu.VMEM((2, 2, TM_COMBINE * ROW_TILE, LANES), F32),
                            pltpu.SemaphoreType.DMA((2,))]),
        out_shape=jax.ShapeDtypeStruct((n, D_MODEL), F32),
        compiler_params=pltpu.CompilerParams(dimension_semantics=("arbitrary",),
                                             vmem_limit_bytes=VMEM_LIMIT),
        name="combine",
    )(dest, h, rw, final_g, ys)


def _schedule(counts, max_tiles):
    chunks = (counts + EXPERT_CHUNK - 1) // EXPERT_CHUNK
    chunk_end = jnp.cumsum(chunks)
    chunk_start = chunk_end - chunks
    tiles = (chunks + TILE_CHUNKS - 1) // TILE_CHUNKS
    tile_end = jnp.cumsum(tiles)
    tile = jnp.arange(max_tiles, dtype=jnp.int32)
    owner = jnp.sum(tile[:, None] >= tile_end[None, :], axis=1)
    is_owner = owner[:, None] == jnp.arange(N_EXPERTS, dtype=jnp.int32)[None, :]
    of_owner = lambda v: jnp.sum(jnp.where(is_owner, v[None, :], 0), axis=1)
    done = (tile - of_owner(tile_end - tiles)) * TILE_CHUNKS
    tile_chunk0 = (of_owner(chunk_start) + done).astype(jnp.int32)
    tile_chunks = jnp.clip(of_owner(chunks) - done, 0, TILE_CHUNKS).astype(jnp.int32)
    return tiles, chunk_start * EXPERT_CHUNK, tile_chunk0, tile_chunks, tile_end[-1:], chunk_end[-1:]


def _layer(x, attn_g, w_in, sg_g, w_sp, b_sp, sb_g, sg_out_g, w_out, ffn_g,
           w_rg, b_rg, w_re, b_re, w_gate, w_up, w_down):
    batch, seq, _ = x.shape
    n = batch * seq
    x2 = x.reshape(n, D_MODEL)
    row = lambda v: v.reshape(1, -1)

    bsp_full = jnp.repeat(b_sp.T, HEAD_DIM, axis=1)
    qkv, sgn = _inproj(x2, row(attn_g), w_in.astype(BF16), row(sg_g), w_sp, bsp_full, row(sg_out_g))
    sb = _attention(qkv, batch, seq).reshape(n, SB_WIDTH)

    pad_lanes = lambda v, width: jnp.pad(v, [(0, 0)] * (v.ndim - 1) + [(0, width - v.shape[-1])])
    w_r = jnp.concatenate([pad_lanes(w_rg, ROUTER_LANE0),
                           jnp.transpose(w_re, (1, 0, 2)).reshape(D_MODEL, N_EXPERTS)], axis=1)
    w_r = pad_lanes(w_r, LANES)
    wr_hi = w_r.astype(BF16)
    wr_lo = (w_r - wr_hi.astype(F32)).astype(BF16)
    wr2 = jnp.concatenate([wr_hi, wr_lo], axis=1)
    b_r = pad_lanes(jnp.concatenate([pad_lanes(b_rg, ROUTER_LANE0), b_re.reshape(-1)]), LANES)

    h, lg = _mix(sb, sgn, x2, row(sb_g), w_out.astype(BF16), row(ffn_g), wr2, row(b_r))
    ri, rw, cnt = _route(lg)

    counts = cnt[:, 0].astype(jnp.int32)
    n_rows = 2 * n + N_EXPERTS * EXPERT_CHUNK
    tiles, offsets, tile_chunk0, tile_chunks, n_tiles, used_chunks = _schedule(
        counts, 2 * n // TM_EXPERT + N_EXPERTS)
    expert, rank = ri[0:2], ri[2:4]
    is_e = expert[None] == jnp.arange(N_EXPERTS, dtype=jnp.int32)[:, None, None]
    dest = (jnp.sum(jnp.where(is_e, offsets[:, None, None], 0), axis=0) + rank).reshape(-1)
    pad_start = offsets + counts
    pad_count = (-counts) % EXPERT_CHUNK

    xs = _dispatch(dest, pad_start, pad_count, used_chunks, h, row(ffn_g), n_rows)
    ys = _experts(tiles, tile_chunk0, tile_chunks, n_tiles, used_chunks, xs,
                  w_gate.reshape(N_EXPERTS, D_MODEL, D_EXPERT),
                  w_up.reshape(N_EXPERTS, D_MODEL, D_EXPERT),
                  w_down.reshape(N_EXPERTS, D_EXPERT, D_MODEL))
    return dest, h, rw, ys


def kernel(x, attn_norm_g, w_in, sg_norm_g, w_spatial, b_spatial, sb_out_norm_g, sg_out_norm_g,
           w_out, ffn_norm_g, w_router_group, b_router_group, w_router_expert, b_router_expert,
           w_gate, w_up, w_down, final_norm_g):
    assert attn_norm_g.shape[0] == 1, "single-layer problem"
    batch, seq, _ = x.shape
    dest, h, rw, ys = _layer(x, attn_norm_g[0], w_in[0], sg_norm_g[0], w_spatial[0], b_spatial[0],
                             sb_out_norm_g[0], sg_out_norm_g[0], w_out[0], ffn_norm_g[0],
                             w_router_group[0], b_router_group[0], w_router_expert[0],
                             b_router_expert[0], w_gate[0], w_up[0], w_down[0])
    out = _combine(dest, h, rw, final_norm_g.reshape(1, -1), ys)
    return out.reshape(batch, seq, D_MODEL)
```

```python
import functools
import math

import jax
import jax.numpy as jnp
from jax import lax
from jax.experimental import pallas as pl
from jax.experimental.pallas import tpu as pltpu

D_MODEL = 1024
HEAD_DIM = 64
SB_WIDTH = 512
SG_WIDTH = 512
SG_HEADS = 8
D_IN = 3 * SB_WIDTH + 2 * SG_WIDTH
CHUNK = 128
N_GROUPS = 4
EXPERTS_PER_GROUP = 8
N_EXPERTS = N_GROUPS * EXPERTS_PER_GROUP
D_EXPERT = 512
EPS = 1e-6
F32_EXP_UNDERFLOW = 110.0

LANES = 128
SUBLANES = 8
ROW_TILE = D_MODEL // LANES
assert ROW_TILE == SUBLANES
HEAD_PAIR = 2 * HEAD_DIM
ROUTER_LANE0 = SUBLANES
ROUTER_ROWS = ROUTER_LANE0 + N_EXPERTS
assert EXPERTS_PER_GROUP == SUBLANES and N_GROUPS <= ROUTER_LANE0

TM_PROJ = 1024
TQ_ATTN = 256
ATTN_BLOCKS_PER_STEP = 2
ATTN_TOP_ROWS = 160
TM_MIX = 1024
TM_ROUTE = 1024
TM_DISPATCH = 1024
TM_EXPERT = 512
EXPERT_CHUNK = 128
TM_COMBINE = 512
VMEM_LIMIT = 48 * 1024 * 1024

F32 = jnp.float32
BF16 = jnp.bfloat16


def _rms(x, g):
    return x * lax.rsqrt(jnp.mean(x * x, axis=-1, keepdims=True) + EPS) * g


def _gelu(x):
    c = math.sqrt(2.0 / math.pi)
    return x * (0.5 * (1.0 + jnp.tanh(c * (x + 0.044715 * (x * x * x)))))


def _softplus(z):
    return jnp.maximum(z, 0.0) + jnp.log(1.0 + jnp.exp(-jnp.abs(z)))


def _dot(a, b):
    return jnp.dot(a, b, preferred_element_type=F32)


def _rows_to_tiles(ref, x):
    m = x.shape[0]
    for k in range(ROW_TILE):
        ref[pl.ds(k, m, stride=ROW_TILE), :] = x[:, k * LANES:(k + 1) * LANES]


def _tiles_to_rows(ref, m):
    return jnp.concatenate([ref[pl.ds(k, m, stride=ROW_TILE), :] for k in range(ROW_TILE)], axis=1)


def _token_rows(ref, first_token, n_tokens):
    return ref.at[pl.ds(pl.multiple_of(first_token * ROW_TILE, ROW_TILE), n_tokens * ROW_TILE)]


def _split_bf16(x):
    hi = x.astype(BF16)
    lo = (x - hi.astype(F32)).astype(BF16)
    return hi, lo


def _inproj_kernel(x_ref, g_ref, w_ref, sgg_ref, wsp_ref, bsp_ref, sgog_ref, qkv_ref, sgn_ref,
                   gu_ref, vgn_ref, sg_ref):
    tm = TM_PROJ
    hb = _rms(x_ref[...], g_ref[...]).astype(BF16)
    gv = _gelu(_dot(hb, w_ref[:, 3 * SB_WIDTH + SG_WIDTH:D_IN]))
    vgn_ref[...] = _rms(gv, sgg_ref[...]).astype(BF16)
    gu_ref[...] = _gelu(_dot(hb, w_ref[:, 3 * SB_WIDTH:3 * SB_WIDTH + SG_WIDTH]))
    q = _dot(hb, w_ref[:, 0:SB_WIDTH]) * (1.0 / math.sqrt(HEAD_DIM))
    qkv_ref[:, 0:SB_WIDTH] = q.astype(BF16)
    qkv_ref[:, SB_WIDTH:2 * SB_WIDTH] = _dot(hb, w_ref[:, SB_WIDTH:2 * SB_WIDTH]).astype(BF16)

    lane = lax.broadcasted_iota(jnp.int32, (1, LANES), 1)
    first = lane < HEAD_DIM
    zero = jnp.zeros((), BF16)
    r_c = lax.broadcasted_iota(jnp.int32, (CHUNK, CHUNK), 0)
    c_c = lax.broadcasted_iota(jnp.int32, (CHUNK, CHUNK), 1)
    tril = r_c >= c_c
    n_pairs = SG_WIDTH // HEAD_PAIR
    w_pairs = []
    for p in range(n_pairs):
        w0 = jnp.where(tril, wsp_ref[2 * p], 0.0).astype(BF16)
        w1 = jnp.where(tril, wsp_ref[2 * p + 1], 0.0).astype(BF16)
        w_pairs.append(jnp.concatenate([w0, w1], axis=1))
    bsp = bsp_ref[...]
    for c in range(tm // CHUNK):
        rows = slice(c * CHUNK, (c + 1) * CHUNK)
        for p in range(n_pairs):
            cols = slice(p * HEAD_PAIR, (p + 1) * HEAD_PAIR)
            vg = vgn_ref[rows, cols]
            rhs = jnp.concatenate([jnp.where(first, vg, zero), jnp.where(first, zero, vg)], axis=0)
            mixed = _dot(w_pairs[p], rhs) + bsp[:, cols]
            sg_ref[rows, cols] = gu_ref[rows, cols] * mixed
    qkv_ref[:, 2 * SB_WIDTH:3 * SB_WIDTH] = _dot(hb, w_ref[:, 2 * SB_WIDTH:3 * SB_WIDTH]).astype(BF16)
    sgn_ref[...] = _rms(sg_ref[...], sgog_ref[...]).astype(BF16)


def _inproj(x2, attn_g, w_in_b, sg_g, wsp, bsp_full, sg_out_g):
    n = x2.shape[0]
    row = lambda i: (i, 0)
    const = lambda i: (0, 0)
    return pl.pallas_call(
        _inproj_kernel,
        grid=(n // TM_PROJ,),
        in_specs=[pl.BlockSpec((TM_PROJ, D_MODEL), row),
                  pl.BlockSpec((1, D_MODEL), const),
                  pl.BlockSpec((D_MODEL, D_IN), const),
                  pl.BlockSpec((1, SG_WIDTH), const),
                  pl.BlockSpec((SG_HEADS, CHUNK, CHUNK), lambda i: (0, 0, 0)),
                  pl.BlockSpec((CHUNK, SG_WIDTH), const),
                  pl.BlockSpec((1, SG_WIDTH), const)],
        out_specs=[pl.BlockSpec((TM_PROJ, 3 * SB_WIDTH), row),
                   pl.BlockSpec((TM_PROJ, SG_WIDTH), row)],
        out_shape=[jax.ShapeDtypeStruct((n, 3 * SB_WIDTH), BF16),
                   jax.ShapeDtypeStruct((n, SG_WIDTH), BF16)],
        scratch_shapes=[pltpu.VMEM((TM_PROJ, SG_WIDTH), F32),
                        pltpu.VMEM((TM_PROJ, SG_WIDTH), BF16),
                        pltpu.VMEM((TM_PROJ, SG_WIDTH), F32)],
        compiler_params=pltpu.CompilerParams(dimension_semantics=("arbitrary",),
                                             vmem_limit_bytes=VMEM_LIMIT),
        name="inproj",
    )(x2, attn_g, w_in_b, sg_g, wsp, bsp_full, sg_out_g)


def _attn_kernel(q_ref, k_ref, v_ref, o_ref, q2_ref, carry_ref):
    t = TQ_ATTN
    n_pairs = SB_WIDTH // HEAD_PAIR
    lane = lax.broadcasted_iota(jnp.int32, (1, HEAD_PAIR), 1)
    head_lanes = (lane < HEAD_DIM, lane >= HEAD_DIM)
    zero = jnp.zeros((), BF16)
    r_idx = lax.broadcasted_iota(jnp.int32, (t, t), 0)
    c_idx = lax.broadcasted_iota(jnp.int32, (t, t), 1)
    suffix = (r_idx > c_idx).astype(BF16)
    suffix2 = jnp.concatenate([suffix, suffix], axis=0)
    causal = c_idx < r_idx

    def one_query_block(sub, c):
        qi = pl.program_id(1) * ATTN_BLOCKS_PER_STEP + sub
        row0 = pl.multiple_of(sub * t, t)
        for p in range(n_pairs):
            qp = q_ref[0, pl.ds(row0, t), p * HEAD_PAIR:(p + 1) * HEAD_PAIR]
            for h in range(2):
                q2_ref[(2 * p + h) * t:(2 * p + h + 1) * t, :] = jnp.where(head_lanes[h], qp, zero)
        o_ref[0, pl.ds(row0, t), :] = jnp.zeros((t, SB_WIDTH), F32)
        carry_ref[...] = jnp.zeros_like(carry_ref)

        def block(j, diag, m):
            start = pl.multiple_of(j * t, t)
            mask2 = jnp.concatenate([causal, causal], axis=0) if diag else None
            st = [dict() for _ in range(n_pairs)]

            def head_rows(p):
                return [slice((2 * p + h) * t, (2 * p + h) * t + m) for h in range(2)]

            def scores(p):
                d = st[p]
                d["cols"] = slice(p * HEAD_PAIR, (p + 1) * HEAD_PAIR)
                kb = k_ref[0, pl.ds(start, t), d["cols"]]
                q2 = jnp.concatenate([q2_ref[r, :] for r in head_rows(p)], axis=0)
                z = lax.dot_general(q2, kb, (((1,), (1,)), ((), ())),
                                    preferred_element_type=F32)
                sp = _softplus(z)
                nl = jnp.where(mask2, sp, 0.0) if diag else sp
                hi, lo = _split_bf16(nl)
                d["hl"] = jnp.concatenate([hi, lo], axis=1)
                d["log_beta"] = z - sp
                d["nl0"] = nl[:, 0:1]

            def weights(p):
                d = st[p]
                hl = d["hl"]
                after = jnp.concatenate([_dot(hl[0:m], suffix2), _dot(hl[m:2 * m], suffix2)], axis=0)
                carry = jnp.concatenate([carry_ref[r, :] for r in head_rows(p)], axis=0)
                a = jnp.exp(d["log_beta"] - after - carry)
                if diag:
                    a = jnp.where(mask2, a, 0.0)
                a = a.astype(BF16)
                d["a2"] = jnp.concatenate([a[0:m], a[m:2 * m]], axis=1)
                new_carry = carry + after[:, 0:1] + d["nl0"]
                for h, r in enumerate(head_rows(p)):
                    carry_ref[r, :] = new_carry[h * m:(h + 1) * m]

            def values(p):
                d = st[p]
                vb = v_ref[0, pl.ds(start, t), d["cols"]]
                v2 = jnp.concatenate([jnp.where(head_lanes[0], vb, zero),
                                      jnp.where(head_lanes[1], vb, zero)], axis=0)
                o_ref[0, pl.ds(row0, m), d["cols"]] += _dot(d["a2"], v2)

            for step in range(n_pairs + 2):
                if step < n_pairs:
                    scores(step)
                if 0 <= step - 1 < n_pairs:
                    weights(step - 1)
                if 0 <= step - 2 < n_pairs:
                    values(step - 2)

        top = ATTN_TOP_ROWS

        def flags():
            bottom = jnp.concatenate([carry_ref[hh * t + top:(hh + 1) * t, :] for hh in range(2 * n_pairs)], axis=0)
            return (jnp.min(carry_ref[...]) < F32_EXP_UNDERFLOW, jnp.min(bottom) >= F32_EXP_UNDERFLOW)

        block(qi, True, t)

        def body(state):
            it, _, bottom_done = state
            j = qi - 1 - it

            @pl.when(bottom_done)
            def _():
                block(j, False, top)

            @pl.when(jnp.logical_not(bottom_done))
            def _():
                block(j, False, t)

            return (it + 1,) + flags()

        lax.while_loop(lambda s: (s[0] < qi) & s[1], body, (jnp.int32(0),) + flags())
        return c

    lax.fori_loop(0, ATTN_BLOCKS_PER_STEP, one_query_block, 0)


def _attention(qkv, batch, seq):
    qkv3 = qkv.reshape(batch, seq, 3 * SB_WIDTH)
    n_heads = SB_WIDTH // HEAD_DIM
    return pl.pallas_call(
        _attn_kernel,
        grid=(batch, seq // (ATTN_BLOCKS_PER_STEP * TQ_ATTN)),
        in_specs=[pl.BlockSpec((1, ATTN_BLOCKS_PER_STEP * TQ_ATTN, SB_WIDTH), lambda b, i: (b, i, 0)),
                  pl.BlockSpec((1, seq, SB_WIDTH), lambda b, i: (b, 0, 1)),
                  pl.BlockSpec((1, seq, SB_WIDTH), lambda b, i: (b, 0, 2))],
        out_specs=pl.BlockSpec((1, ATTN_BLOCKS_PER_STEP * TQ_ATTN, SB_WIDTH), lambda b, i: (b, i, 0)),
        out_shape=jax.ShapeDtypeStruct((batch, seq, SB_WIDTH), F32),
        scratch_shapes=[pltpu.VMEM((n_heads * TQ_ATTN, HEAD_PAIR), BF16),
                        pltpu.VMEM((n_heads * TQ_ATTN, 1), F32)],
        compiler_params=pltpu.CompilerParams(dimension_semantics=("arbitrary",) * 2,
                                             vmem_limit_bytes=VMEM_LIMIT),
        name="sb_attention",
    )(qkv3, qkv3, qkv3)


def _mix_kernel(sb_ref, sgn_ref, x_ref, sbg_ref, wout_ref, ffng_ref, wr2_ref, br_ref,
                h_ref, lg_ref):
    sbn = _rms(sb_ref[...], sbg_ref[...]).astype(BF16)
    h = x_ref[...] + _dot(sbn, wout_ref[0:SB_WIDTH, :]) + _dot(sgn_ref[...], wout_ref[SB_WIDTH:, :])
    h_ref[...] = h
    hn = _rms(h, ffng_ref[...])

    hn_hi, hn_lo = _split_bf16(hn)
    both = _dot(hn_hi, wr2_ref[...])
    logits = both[:, 0:LANES] + both[:, LANES:] + _dot(hn_lo, wr2_ref[:, 0:LANES]) + br_ref[...]
    lg_ref[...] = logits.T[0:ROUTER_ROWS, :]


def _route_kernel(lg_ref, ri_ref, rw_ref, cnt_ref, count_ref):
    tr = TM_ROUTE
    i = pl.program_id(0)

    @pl.when(i == 0)
    def _():
        count_ref[...] = jnp.zeros_like(count_ref)

    neg = jnp.float32(-jnp.inf)
    row8 = lax.broadcasted_iota(jnp.int32, (SUBLANES, tr), 0)

    def top(v):
        m = jnp.max(v, axis=0, keepdims=True)
        return m, jnp.min(jnp.where(v == m, row8, SUBLANES), axis=0, keepdims=True)

    def group_rows(g):
        return lg_ref[ROUTER_LANE0 + g * EXPERTS_PER_GROUP:ROUTER_LANE0 + (g + 1) * EXPERTS_PER_GROUP, :]

    gl = jnp.where(row8 < N_GROUPS, lg_ref[0:SUBLANES, :], neg)
    gmax, gidx = top(gl)
    gweight = 1.0 / jnp.sum(jnp.exp(gl - gmax), axis=0, keepdims=True)
    el = group_rows(0)
    for g in range(1, N_GROUPS):
        el = jnp.where(gidx == g, group_rows(g), el)
    m1, i1 = top(el)
    m2, i2 = top(jnp.where(row8 == i1, neg, el))
    t21 = jnp.exp(m2 - m1)
    w1 = gweight / (1.0 + t21)
    w2 = gweight * t21 / (1.0 + t21)
    e1 = gidx * EXPERTS_PER_GROUP + i1
    e2 = gidx * EXPERTS_PER_GROUP + i2

    row_e = lax.broadcasted_iota(jnp.int32, (N_EXPERTS, tr), 0)
    sel1 = row_e == e1
    sel2 = row_e == e2
    onehot = jnp.where(sel1 | sel2, 1.0, 0.0)
    r_t = lax.broadcasted_iota(jnp.int32, (tr, tr), 0)
    c_t = lax.broadcasted_iota(jnp.int32, (tr, tr), 1)
    before = (r_t < c_t).astype(BF16)
    running = count_ref[:, 0:1] + _dot(onehot.astype(BF16), before)
    rank1 = jnp.sum(jnp.where(sel1, running, 0.0), axis=0, keepdims=True)
    rank2 = jnp.sum(jnp.where(sel2, running, 0.0), axis=0, keepdims=True)
    new_count = count_ref[:, 0:1] + jnp.sum(onehot, axis=1, keepdims=True)
    count_ref[...] = jnp.broadcast_to(new_count, count_ref.shape)
    cnt_ref[...] = jnp.broadcast_to(new_count, cnt_ref.shape)

    ri_ref[...] = jnp.where(row8 == 0, e1, jnp.where(row8 == 1, e2, jnp.where(
        row8 == 2, rank1.astype(jnp.int32), jnp.where(row8 == 3, rank2.astype(jnp.int32), 0))))
    row128 = lax.broadcasted_iota(jnp.int32, (LANES, tr), 0)
    rw_ref[...] = jnp.where(row128 == 0, w1, jnp.where(row128 == 1, w2, 0.0)).T


def _route(lg):
    n = lg.shape[1]
    return pl.pallas_call(
        _route_kernel,
        grid=(n // TM_ROUTE,),
        in_specs=[pl.BlockSpec((ROUTER_ROWS, TM_ROUTE), lambda i: (0, i))],
        out_specs=[pl.BlockSpec((SUBLANES, TM_ROUTE), lambda i: (0, i)),
                   pl.BlockSpec((TM_ROUTE, LANES), lambda i: (i, 0)),
                   pl.BlockSpec((N_EXPERTS, LANES), lambda i: (0, 0))],
        out_shape=[jax.ShapeDtypeStruct((SUBLANES, n), jnp.int32),
                   jax.ShapeDtypeStruct((n, LANES), F32),
                   jax.ShapeDtypeStruct((N_EXPERTS, LANES), F32)],
        scratch_shapes=[pltpu.VMEM((N_EXPERTS, LANES), F32)],
        compiler_params=pltpu.CompilerParams(dimension_semantics=("arbitrary",),
                                             vmem_limit_bytes=VMEM_LIMIT),
        name="route",
    )(lg)


def _mix(sb, sgn, x2, sb_g, w_out_b, ffn_g, wr2, br):
    n = x2.shape[0]
    row = lambda i: (i, 0)
    const = lambda i: (0, 0)
    return pl.pallas_call(
        _mix_kernel,
        grid=(n // TM_MIX,),
        in_specs=[pl.BlockSpec((TM_MIX, SB_WIDTH), row),
                  pl.BlockSpec((TM_MIX, SG_WIDTH), row),
                  pl.BlockSpec((TM_MIX, D_MODEL), row),
                  pl.BlockSpec((1, SB_WIDTH), const),
                  pl.BlockSpec((D_MODEL, D_MODEL), const),
                  pl.BlockSpec((1, D_MODEL), const),
                  pl.BlockSpec((D_MODEL, 2 * LANES), const),
                  pl.BlockSpec((1, LANES), const)],
        out_specs=[pl.BlockSpec((TM_MIX, D_MODEL), row),
                   pl.BlockSpec((ROUTER_ROWS, TM_MIX), lambda i: (0, i))],
        out_shape=[jax.ShapeDtypeStruct((n, D_MODEL), F32),
                   jax.ShapeDtypeStruct((ROUTER_ROWS, n), F32)],
        compiler_params=pltpu.CompilerParams(dimension_semantics=("arbitrary",),
                                             vmem_limit_bytes=VMEM_LIMIT),
        name="mix_router",
    )(sb, sgn, x2, sb_g, w_out_b, ffn_g, wr2, br)


_PAD_BITS = tuple(1 << b for b in reversed(range(EXPERT_CHUNK.bit_length() - 1)))


def _dispatch_kernel(dest_ref, pad_start_ref, pad_count_ref, used_ref, h_ref, g_ref, zeros_ref, xs_ref,
                     hn_ref, sem, zsem):
    tm = TM_DISPATCH
    i = pl.program_id(0)
    n_steps = pl.num_programs(0) - 1
    n = n_steps * tm
    base = (i - 1) * tm
    prev = hn_ref.at[lax.rem(i + 1, 2)]
    n_chunks = xs_ref.shape[0] // (EXPERT_CHUNK * ROW_TILE)

    def pad_copies(do):
        for e in range(N_EXPERTS):
            start = pad_start_ref[e]
            count = pad_count_ref[e]
            for bit in _PAD_BITS:
                @pl.when((count & bit) != 0)
                def _(start=start, bit=bit):
                    do(pltpu.make_async_copy(_token_rows(zeros_ref, 0, bit),
                                             _token_rows(xs_ref, start, bit), zsem))
                start = start + (count & bit)
        for k in range(N_EXPERTS):
            chunk = used_ref[0] + k

            @pl.when(chunk < n_chunks)
            def _(chunk=chunk):
                do(pltpu.make_async_copy(zeros_ref, _token_rows(xs_ref, chunk * EXPERT_CHUNK, EXPERT_CHUNK),
                                         zsem))

    @pl.when(i == 0)
    def _():
        pad_copies(lambda cp: cp.start())

    @pl.when(i > 0)
    def _():
        def body(r, c):
            src = _token_rows(prev, r, 1)
            for s in range(2):
                pltpu.make_async_copy(src, _token_rows(xs_ref, dest_ref[s * n + base + r], 1),
                                      sem).start(priority=s)
            return c

        lax.fori_loop(0, tm, body, 0, unroll=8)

    @pl.when(i < n_steps)
    def _():
        _rows_to_tiles(hn_ref.at[lax.rem(i, 2)], _rms(h_ref[...], g_ref[...]))

    @pl.when(i > 0)
    def _():
        for _ in range(2):
            pltpu.make_async_copy(prev, _token_rows(xs_ref, 0, tm), sem).wait()

    @pl.when(i == n_steps)
    def _():
        pad_copies(lambda cp: cp.wait())


def _dispatch(dest, pad_start, pad_count, used_chunks, h, ffn_g, n_rows):
    n_steps = h.shape[0] // TM_DISPATCH
    zeros = jnp.zeros((EXPERT_CHUNK * ROW_TILE, LANES), F32)
    return pl.pallas_call(
        _dispatch_kernel,
        grid_spec=pltpu.PrefetchScalarGridSpec(
            num_scalar_prefetch=4,
            grid=(n_steps + 1,),
            in_specs=[pl.BlockSpec((TM_DISPATCH, D_MODEL), lambda i, *_: (jnp.minimum(i, n_steps - 1), 0)),
                      pl.BlockSpec((1, D_MODEL), lambda i, *_: (0, 0)),
                      pl.BlockSpec(memory_space=pl.ANY)],
            out_specs=pl.BlockSpec(memory_space=pl.ANY),
            scratch_shapes=[pltpu.VMEM((2, TM_DISPATCH * ROW_TILE, LANES), F32),
                            pltpu.SemaphoreType.DMA, pltpu.SemaphoreType.DMA]),
        out_shape=jax.ShapeDtypeStruct((n_rows * ROW_TILE, LANES), F32),
        compiler_params=pltpu.CompilerParams(dimension_semantics=("arbitrary",),
                                             vmem_limit_bytes=VMEM_LIMIT),
        name="dispatch",
    )(dest, pad_start, pad_count, used_chunks, h, ffn_g, zeros)


X_SLOTS = 4
TILE_CHUNKS = TM_EXPERT // EXPERT_CHUNK
W_SLOTS = 3


def _expert_kernel(tiles_ref, chunk0_ref, chunks_ref, nt_ref, used_ref, xs_ref, wg_ref, wu_ref, wd_ref,
                   zeros_ref, ys_ref, x_buf, y_buf, sg_buf, su_buf, sd_buf, wgb, wub, wdb, state,
                   w_sems, x_sems, y_sems, zsem):
    t = pl.program_id(0)
    last = pl.num_programs(0) - 1
    nt = nt_ref[0]
    n_chunks = ys_ref.shape[0] // (EXPERT_CHUNK * ROW_TILE)

    def tile_copies(tile, do, out):
        for c in range(TILE_CHUNKS):
            @pl.when(c < chunks_ref[tile])
            def _(c=c):
                first = (chunk0_ref[tile] + c) * EXPERT_CHUNK
                if out:
                    slot = lax.rem(tile, 2)
                    do(pltpu.make_async_copy(_token_rows(y_buf.at[slot], c * EXPERT_CHUNK, EXPERT_CHUNK),
                                             _token_rows(ys_ref, first, EXPERT_CHUNK), y_sems.at[slot]))
                else:
                    slot = lax.rem(tile, X_SLOTS)
                    do(pltpu.make_async_copy(_token_rows(xs_ref, first, EXPERT_CHUNK),
                                             _token_rows(x_buf.at[slot], c * EXPERT_CHUNK, EXPERT_CHUNK),
                                             x_sems.at[slot]))

    start = lambda cp: cp.start()
    wait = lambda cp: cp.wait()

    def tail_copies(do):
        for k in range(N_EXPERTS):
            chunk = used_ref[0] + k

            @pl.when(chunk < n_chunks)
            def _(chunk=chunk):
                do(pltpu.make_async_copy(zeros_ref, _token_rows(ys_ref, chunk * EXPERT_CHUNK, EXPERT_CHUNK),
                                         zsem))

    def weight_copies(e, slot):
        return (pltpu.make_async_copy(wg_ref.at[e], sg_buf.at[slot], w_sems.at[slot]),
                pltpu.make_async_copy(wu_ref.at[e], su_buf.at[slot], w_sems.at[slot]),
                pltpu.make_async_copy(wd_ref.at[e], sd_buf.at[slot], w_sems.at[slot]))

    def next_with_rows(e):
        return lax.while_loop(lambda k: (k < N_EXPERTS) & (tiles_ref[jnp.minimum(k, N_EXPERTS - 1)] == 0),
                              lambda k: k + 1, e + 1)

    @pl.when(t == 0)
    def _():
        first = next_with_rows(jnp.int32(-1))
        second = next_with_rows(first)
        state[0] = jnp.int32(-1)
        state[1] = jnp.int32(0)
        state[2] = jnp.int32(W_SLOTS - 1)
        state[3] = first
        state[4] = second
        for cp in weight_copies(first, 0):
            cp.start()

        @pl.when(second < N_EXPERTS)
        def _():
            for cp in weight_copies(second, 1):
                cp.start()

        for tile in range(X_SLOTS - 1):
            @pl.when(tile < nt)
            def _(tile=tile):
                tile_copies(tile, start, False)

        tail_copies(start)

    @pl.when(t + X_SLOTS - 1 < nt)
    def _():
        tile_copies(t + X_SLOTS - 1, start, False)

    @pl.when(t < nt)
    def _():
        @pl.when(state[1] == 0)
        def _():
            e = state[3]
            nxt = state[4]
            slot = lax.rem(state[2] + 1, W_SLOTS)
            after_next = next_with_rows(nxt)
            state[0] = e
            state[1] = tiles_ref[e]
            state[2] = slot
            state[3] = nxt
            state[4] = after_next
            for cp in weight_copies(e, slot):
                cp.wait()

            @pl.when(after_next < N_EXPERTS)
            def _():
                for cp in weight_copies(after_next, lax.rem(slot + 2, W_SLOTS)):
                    cp.start()

            wgb[...] = sg_buf[slot].astype(BF16)
            wub[...] = su_buf[slot].astype(BF16)
            wdb[...] = sd_buf[slot].astype(BF16)

        state[1] = state[1] - 1
        tile_copies(t, wait, False)

        @pl.when(t >= 2)
        def _():
            tile_copies(t - 2, wait, True)

        for n_chunks_here in range(1, TILE_CHUNKS + 1):
            @pl.when(chunks_ref[t] == n_chunks_here)
            def _(m=n_chunks_here * EXPERT_CHUNK):
                x = _tiles_to_rows(x_buf.at[lax.rem(t, X_SLOTS)], m).astype(BF16)
                g = _dot(x, wgb[...])
                u = _dot(x, wub[...])
                hidden = (g * jax.nn.sigmoid(g)) * u
                _rows_to_tiles(y_buf.at[lax.rem(t, 2)], _dot(hidden.astype(BF16), wdb[...]))

        tile_copies(t, start, True)

    @pl.when(t == last)
    def _():
        for back in (2, 1):
            @pl.when(nt >= back)
            def _(back=back):
                tile_copies(nt - back, wait, True)

        tail_copies(wait)


def _experts(tiles, chunk0, chunks, n_tiles, used_chunks, xs, wg, wu, wd):
    any_spec = pl.BlockSpec(memory_space=pl.ANY)
    zeros = jnp.zeros((EXPERT_CHUNK * ROW_TILE, LANES), F32)
    return pl.pallas_call(
        _expert_kernel,
        grid_spec=pltpu.PrefetchScalarGridSpec(
            num_scalar_prefetch=5,
            grid=(chunks.shape[0],),
            in_specs=[any_spec, any_spec, any_spec, any_spec, any_spec],
            out_specs=any_spec,
            scratch_shapes=[pltpu.VMEM((X_SLOTS, TM_EXPERT * ROW_TILE, LANES), F32),
                            pltpu.VMEM((2, TM_EXPERT * ROW_TILE, LANES), F32),
                            pltpu.VMEM((W_SLOTS, D_MODEL, D_EXPERT), F32),
                            pltpu.VMEM((W_SLOTS, D_MODEL, D_EXPERT), F32),
                            pltpu.VMEM((W_SLOTS, D_EXPERT, D_MODEL), F32),
                            pltpu.VMEM((D_MODEL, D_EXPERT), BF16),
                            pltpu.VMEM((D_MODEL, D_EXPERT), BF16),
                            pltpu.VMEM((D_EXPERT, D_MODEL), BF16),
                            pltpu.SMEM((5,), jnp.int32),
                            pltpu.SemaphoreType.DMA((W_SLOTS,)),
                            pltpu.SemaphoreType.DMA((X_SLOTS,)),
                            pltpu.SemaphoreType.DMA((2,)),
                            pltpu.SemaphoreType.DMA]),
        out_shape=jax.ShapeDtypeStruct(xs.shape, F32),
        compiler_params=pltpu.CompilerParams(dimension_semantics=("arbitrary",),
                                             vmem_limit_bytes=VMEM_LIMIT),
        name="expert_mlp",
    )(tiles, chunk0, chunks, n_tiles, used_chunks, xs, wg, wu, wd, zeros)


def _combine_kernel(dest_ref, h_ref, rw_ref, fg_ref, y_ref, o_ref, buf, sems):
    tm = TM_COMBINE
    i = pl.program_id(0)
    n_steps = pl.num_programs(0)
    n = n_steps * tm
    cur = i % 2

    def fetch(step, half):
        def body(r, c):
            for s in range(2):
                pltpu.make_async_copy(_token_rows(y_ref, dest_ref[s * n + step * tm + r], 1),
                                      _token_rows(buf.at[half, s], r, 1),
                                      sems.at[half]).start(priority=s)
            return c

        lax.fori_loop(0, tm, body, 0, unroll=8)

    @pl.when(i == 0)
    def _():
        fetch(0, 0)

    @pl.when(i + 1 < n_steps)
    def _():
        fetch(i + 1, 1 - cur)

    for s in range(2):
        pltpu.make_async_copy(_token_rows(y_ref, 0, tm), buf.at[cur, s], sems.at[cur]).wait()
    rw = rw_ref[...]
    out = (h_ref[...] + rw[:, 0:1] * _tiles_to_rows(buf.at[cur, 0], tm)
           + rw[:, 1:2] * _tiles_to_rows(buf.at[cur, 1], tm))
    o_ref[...] = _rms(out, fg_ref[...])


def _combine(dest, h, rw, final_g, ys):
    n = h.shape[0]
    return pl.pallas_call(
        _combine_kernel,
        grid_spec=pltpu.PrefetchScalarGridSpec(
            num_scalar_prefetch=1,
            grid=(n // TM_COMBINE,),
            in_specs=[pl.BlockSpec((TM_COMBINE, D_MODEL), lambda i, d: (i, 0)),
                      pl.BlockSpec((TM_COMBINE, LANES), lambda i, d: (i, 0)),
                      pl.BlockSpec((1, D_MODEL), lambda i, d: (0, 0)),
                      pl.BlockSpec(memory_space=pl.ANY)],
            out_specs=pl.BlockSpec((TM_COMBINE, D_MODEL), lambda i, d: (i, 0)),
            scratch_shapes=[pltpu.VMEM((2, 2, TM_COMBINE * ROW_TILE, LANES), F32),
                            pltpu.SemaphoreType.DMA((2,))]),
        out_shape=jax.ShapeDtypeStruct((n, D_MODEL), F32),
        compiler_params=pltpu.CompilerParams(dimension_semantics=("arbitrary",),
                                             vmem_limit_bytes=VMEM_LIMIT),
        name="combine",
    )(dest, h, rw, final_g, ys)


def _schedule(counts, max_tiles):
    chunks = (counts + EXPERT_CHUNK - 1) // EXPERT_CHUNK
    chunk_end = jnp.cumsum(chunks)
    chunk_start = chunk_end - chunks
    tiles = (chunks + TILE_CHUNKS - 1) // TILE_CHUNKS
    tile_end = jnp.cumsum(tiles)
    tile = jnp.arange(max_tiles, dtype=jnp.int32)
    owner = jnp.sum(tile[:, None] >= tile_end[None, :], axis=1)
    is_owner = owner[:, None] == jnp.arange(N_EXPERTS, dtype=jnp.int32)[None, :]
    of_owner = lambda v: jnp.sum(jnp.where(is_owner, v[None, :], 0), axis=1)
    done = (tile - of_owner(tile_end - tiles)) * TILE_CHUNKS
    tile_chunk0 = (of_owner(chunk_start) + done).astype(jnp.int32)
    tile_chunks = jnp.clip(of_owner(chunks) - done, 0, TILE_CHUNKS).astype(jnp.int32)
    return tiles, chunk_start * EXPERT_CHUNK, tile_chunk0, tile_chunks, tile_end[-1:], chunk_end[-1:]


def _layer(x, attn_g, w_in, sg_g, w_sp, b_sp, sb_g, sg_out_g, w_out, ffn_g,
           w_rg, b_rg, w_re, b_re, w_gate, w_up, w_down):
    batch, seq, _ = x.shape
    n = batch * seq
    x2 = x.reshape(n, D_MODEL)
    row = lambda v: v.reshape(1, -1)

    bsp_full = jnp.repeat(b_sp.T, HEAD_DIM, axis=1)
    qkv, sgn = _inproj(x2, row(attn_g), w_in.astype(BF16), row(sg_g), w_sp, bsp_full, row(sg_out_g))
    sb = _attention(qkv, batch, seq).reshape(n, SB_WIDTH)

    pad_lanes = lambda v, width: jnp.pad(v, [(0, 0)] * (v.ndim - 1) + [(0, width - v.shape[-1])])
    w_r = jnp.concatenate([pad_lanes(w_rg, ROUTER_LANE0),
                           jnp.transpose(w_re, (1, 0, 2)).reshape(D_MODEL, N_EXPERTS)], axis=1)
    w_r = pad_lanes(w_r, LANES)
    wr_hi = w_r.astype(BF16)
    wr_lo = (w_r - wr_hi.astype(F32)).astype(BF16)
    wr2 = jnp.concatenate([wr_hi, wr_lo], axis=1)
    b_r = pad_lanes(jnp.concatenate([pad_lanes(b_rg, ROUTER_LANE0), b_re.reshape(-1)]), LANES)

    h, lg = _mix(sb, sgn, x2, row(sb_g), w_out.astype(BF16), row(ffn_g), wr2, row(b_r))
    ri, rw, cnt = _route(lg)

    counts = cnt[:, 0].astype(jnp.int32)
    n_rows = 2 * n + N_EXPERTS * EXPERT_CHUNK
    tiles, offsets, tile_chunk0, tile_chunks, n_tiles, used_chunks = _schedule(
        counts, 2 * n // TM_EXPERT + N_EXPERTS)
    expert, rank = ri[0:2], ri[2:4]
    is_e = expert[None] == jnp.arange(N_EXPERTS, dtype=jnp.int32)[:, None, None]
    dest = (jnp.sum(jnp.where(is_e, offsets[:, None, None], 0), axis=0) + rank).reshape(-1)
    pad_start = offsets + counts
    pad_count = (-counts) % EXPERT_CHUNK

    xs = _dispatch(dest, pad_start, pad_count, used_chunks, h, row(ffn_g), n_rows)
    ys = _experts(tiles, tile_chunk0, tile_chunks, n_tiles, used_chunks, xs,
                  w_gate.reshape(N_EXPERTS, D_MODEL, D_EXPERT),
                  w_up.reshape(N_EXPERTS, D_MODEL, D_EXPERT),
                  w_down.reshape(N_EXPERTS, D_EXPERT, D_MODEL))
    return dest, h, rw, ys


def kernel(x, attn_norm_g, w_in, sg_norm_g, w_spatial, b_spatial, sb_out_norm_g, sg_out_norm_g,
           w_out, ffn_norm_g, w_router_group, b_router_group, w_router_expert, b_router_expert,
           w_gate, w_up, w_down, final_norm_g):
    assert attn_norm_g.shape[0] == 1, "single-layer problem"
    batch, seq, _ = x.shape
    dest, h, rw, ys = _layer(x, attn_norm_g[0], w_in[0], sg_norm_g[0], w_spatial[0], b_spatial[0],
                             sb_out_norm_g[0], sg_out_norm_g[0], w_out[0], ffn_norm_g[0],
                             w_router_group[0], b_router_group[0], w_router_expert[0],
                             b_router_expert[0], w_gate[0], w_up[0], w_down[0])
    out = _combine(dest, h, rw, final_norm_g.reshape(1, -1), ys)
    return out.reshape(batch, seq, D_MODEL)
```

```python
import functools
import math

import jax
import jax.numpy as jnp
from jax import lax
from jax.experimental import pallas as pl
from jax.experimental.pallas import tpu as pltpu

D_MODEL = 1024
HEAD_DIM = 64
SB_WIDTH = 512
SG_WIDTH = 512
SG_HEADS = 8
D_IN = 3 * SB_WIDTH + 2 * SG_WIDTH
CHUNK = 128
N_GROUPS = 4
EXPERTS_PER_GROUP = 8
N_EXPERTS = N_GROUPS * EXPERTS_PER_GROUP
D_EXPERT = 512
EPS = 1e-6
F32_EXP_UNDERFLOW = 110.0

LANES = 128
SUBLANES = 8
ROW_TILE = D_MODEL // LANES
assert ROW_TILE == SUBLANES
HEAD_PAIR = 2 * HEAD_DIM
ROUTER_LANE0 = SUBLANES
ROUTER_ROWS = ROUTER_LANE0 + N_EXPERTS
assert EXPERTS_PER_GROUP == SUBLANES and N_GROUPS <= ROUTER_LANE0

TM_PROJ = 1024
TQ_ATTN = 256
ATTN_BLOCKS_PER_STEP = 2
ATTN_TOP_ROWS = (160, 176)
TM_MIX = 1024
TM_ROUTE = 1024
TM_DISPATCH = 1024
TM_EXPERT = 512
EXPERT_CHUNK = 128
TM_COMBINE = 512
VMEM_LIMIT = 48 * 1024 * 1024

F32 = jnp.float32
BF16 = jnp.bfloat16


def _rms(x, g):
    return x * lax.rsqrt(jnp.mean(x * x, axis=-1, keepdims=True) + EPS) * g


def _gelu(x):
    c = math.sqrt(2.0 / math.pi)
    return x * (0.5 * (1.0 + jnp.tanh(c * (x + 0.044715 * (x * x * x)))))


def _softplus(z):
    return jnp.maximum(z, 0.0) + jnp.log(1.0 + jnp.exp(-jnp.abs(z)))


def _dot(a, b):
    return jnp.dot(a, b, preferred_element_type=F32)


def _rows_to_tiles(ref, x):
    m = x.shape[0]
    for k in range(ROW_TILE):
        ref[pl.ds(k, m, stride=ROW_TILE), :] = x[:, k * LANES:(k + 1) * LANES]


def _tiles_to_rows(ref, m):
    return jnp.concatenate([ref[pl.ds(k, m, stride=ROW_TILE), :] for k in range(ROW_TILE)], axis=1)


def _token_rows(ref, first_token, n_tokens):
    return ref.at[pl.ds(pl.multiple_of(first_token * ROW_TILE, ROW_TILE), n_tokens * ROW_TILE)]


def _split_bf16(x):
    hi = x.astype(BF16)
    lo = (x - hi.astype(F32)).astype(BF16)
    return hi, lo


def _inproj_kernel(x_ref, g_ref, w_ref, sgg_ref, wsp_ref, bsp_ref, sgog_ref, qkv_ref, sgn_ref,
                   gu_ref, vgn_ref, sg_ref):
    tm = TM_PROJ
    hb = _rms(x_ref[...], g_ref[...]).astype(BF16)
    gv = _gelu(_dot(hb, w_ref[:, 3 * SB_WIDTH + SG_WIDTH:D_IN]))
    vgn_ref[...] = _rms(gv, sgg_ref[...]).astype(BF16)
    gu_ref[...] = _gelu(_dot(hb, w_ref[:, 3 * SB_WIDTH:3 * SB_WIDTH + SG_WIDTH]))
    q = _dot(hb, w_ref[:, 0:SB_WIDTH]) * (1.0 / math.sqrt(HEAD_DIM))
    qkv_ref[:, 0:SB_WIDTH] = q.astype(BF16)
    qkv_ref[:, SB_WIDTH:2 * SB_WIDTH] = _dot(hb, w_ref[:, SB_WIDTH:2 * SB_WIDTH]).astype(BF16)

    lane = lax.broadcasted_iota(jnp.int32, (1, LANES), 1)
    first = lane < HEAD_DIM
    zero = jnp.zeros((), BF16)
    r_c = lax.broadcasted_iota(jnp.int32, (CHUNK, CHUNK), 0)
    c_c = lax.broadcasted_iota(jnp.int32, (CHUNK, CHUNK), 1)
    tril = r_c >= c_c
    n_pairs = SG_WIDTH // HEAD_PAIR
    w_pairs = []
    for p in range(n_pairs):
        w0 = jnp.where(tril, wsp_ref[2 * p], 0.0).astype(BF16)
        w1 = jnp.where(tril, wsp_ref[2 * p + 1], 0.0).astype(BF16)
        w_pairs.append(jnp.concatenate([w0, w1], axis=1))
    bsp = bsp_ref[...]
    for c in range(tm // CHUNK):
        rows = slice(c * CHUNK, (c + 1) * CHUNK)
        for p in range(n_pairs):
            cols = slice(p * HEAD_PAIR, (p + 1) * HEAD_PAIR)
            vg = vgn_ref[rows, cols]
            rhs = jnp.concatenate([jnp.where(first, vg, zero), jnp.where(first, zero, vg)], axis=0)
            mixed = _dot(w_pairs[p], rhs) + bsp[:, cols]
            sg_ref[rows, cols] = gu_ref[rows, cols] * mixed
    qkv_ref[:, 2 * SB_WIDTH:3 * SB_WIDTH] = _dot(hb, w_ref[:, 2 * SB_WIDTH:3 * SB_WIDTH]).astype(BF16)
    sgn_ref[...] = _rms(sg_ref[...], sgog_ref[...]).astype(BF16)


def _inproj(x2, attn_g, w_in_b, sg_g, wsp, bsp_full, sg_out_g):
    n = x2.shape[0]
    row = lambda i: (i, 0)
    const = lambda i: (0, 0)
    return pl.pallas_call(
        _inproj_kernel,
        grid=(n // TM_PROJ,),
        in_specs=[pl.BlockSpec((TM_PROJ, D_MODEL), row),
                  pl.BlockSpec((1, D_MODEL), const),
                  pl.BlockSpec((D_MODEL, D_IN), const),
                  pl.BlockSpec((1, SG_WIDTH), const),
                  pl.BlockSpec((SG_HEADS, CHUNK, CHUNK), lambda i: (0, 0, 0)),
                  pl.BlockSpec((CHUNK, SG_WIDTH), const),
                  pl.BlockSpec((1, SG_WIDTH), const)],
        out_specs=[pl.BlockSpec((TM_PROJ, 3 * SB_WIDTH), row),
                   pl.BlockSpec((TM_PROJ, SG_WIDTH), row)],
        out_shape=[jax.ShapeDtypeStruct((n, 3 * SB_WIDTH), BF16),
                   jax.ShapeDtypeStruct((n, SG_WIDTH), BF16)],
        scratch_shapes=[pltpu.VMEM((TM_PROJ, SG_WIDTH), F32),
                        pltpu.VMEM((TM_PROJ, SG_WIDTH), BF16),
                        pltpu.VMEM((TM_PROJ, SG_WIDTH), F32)],
        compiler_params=pltpu.CompilerParams(dimension_semantics=("arbitrary",),
                                             vmem_limit_bytes=VMEM_LIMIT),
        name="inproj",
    )(x2, attn_g, w_in_b, sg_g, wsp, bsp_full, sg_out_g)


def _attn_kernel(q_ref, k_ref, v_ref, o_ref, q2_ref, carry_ref):
    t = TQ_ATTN
    n_pairs = SB_WIDTH // HEAD_PAIR
    lane = lax.broadcasted_iota(jnp.int32, (1, HEAD_PAIR), 1)
    head_lanes = (lane < HEAD_DIM, lane >= HEAD_DIM)
    zero = jnp.zeros((), BF16)
    r_idx = lax.broadcasted_iota(jnp.int32, (t, t), 0)
    c_idx = lax.broadcasted_iota(jnp.int32, (t, t), 1)
    suffix = (r_idx > c_idx).astype(BF16)
    suffix2 = jnp.concatenate([suffix, suffix], axis=0)
    causal = c_idx < r_idx

    def one_query_block(sub, c):
        qi = pl.program_id(1) * ATTN_BLOCKS_PER_STEP + sub
        row0 = pl.multiple_of(sub * t, t)
        for p in range(n_pairs):
            qp = q_ref[0, pl.ds(row0, t), p * HEAD_PAIR:(p + 1) * HEAD_PAIR]
            for h in range(2):
                q2_ref[(2 * p + h) * t:(2 * p + h + 1) * t, :] = jnp.where(head_lanes[h], qp, zero)
        o_ref[0, pl.ds(row0, t), :] = jnp.zeros((t, SB_WIDTH), F32)
        carry_ref[...] = jnp.zeros_like(carry_ref)

        def block(j, diag, m):
            start = pl.multiple_of(j * t, t)
            mask2 = jnp.concatenate([causal, causal], axis=0) if diag else None
            st = [dict() for _ in range(n_pairs)]

            def head_rows(p):
                return [slice((2 * p + h) * t, (2 * p + h) * t + m) for h in range(2)]

            def scores(p):
                d = st[p]
                d["cols"] = slice(p * HEAD_PAIR, (p + 1) * HEAD_PAIR)
                kb = k_ref[0, pl.ds(start, t), d["cols"]]
                q2 = jnp.concatenate([q2_ref[r, :] for r in head_rows(p)], axis=0)
                z = lax.dot_general(q2, kb, (((1,), (1,)), ((), ())),
                                    preferred_element_type=F32)
                sp = _softplus(z)
                nl = jnp.where(mask2, sp, 0.0) if diag else sp
                hi, lo = _split_bf16(nl)
                d["hl"] = jnp.concatenate([hi, lo], axis=1)
                d["log_beta"] = z - sp
                d["nl0"] = nl[:, 0:1]

            def weights(p):
                d = st[p]
                hl = d["hl"]
                after = jnp.concatenate([_dot(hl[0:m], suffix2), _dot(hl[m:2 * m], suffix2)], axis=0)
                carry = jnp.concatenate([carry_ref[r, :] for r in head_rows(p)], axis=0)
                a = jnp.exp(d["log_beta"] - after - carry)
                if diag:
                    a = jnp.where(mask2, a, 0.0)
                a = a.astype(BF16)
                d["a2"] = jnp.concatenate([a[0:m], a[m:2 * m]], axis=1)
                new_carry = carry + after[:, 0:1] + d["nl0"]
                for h, r in enumerate(head_rows(p)):
                    carry_ref[r, :] = new_carry[h * m:(h + 1) * m]

            def values(p):
                d = st[p]
                vb = v_ref[0, pl.ds(start, t), d["cols"]]
                v2 = jnp.concatenate([jnp.where(head_lanes[0], vb, zero),
                                      jnp.where(head_lanes[1], vb, zero)], axis=0)
                o_ref[0, pl.ds(row0, m), d["cols"]] += _dot(d["a2"], v2)

            for step in range(n_pairs + 2):
                if step < n_pairs:
                    scores(step)
                if 0 <= step - 1 < n_pairs:
                    weights(step - 1)
                if 0 <= step - 2 < n_pairs:
                    values(step - 2)

        def flags():
            bounds = (0,) + ATTN_TOP_ROWS + (t,)
            lowest = [jnp.min(jnp.concatenate([carry_ref[hh * t + lo:hh * t + hi, :] for hh in range(2 * n_pairs)],
                                              axis=0))
                      for lo, hi in zip(bounds[:-1], bounds[1:])]
            below = [functools.reduce(jnp.minimum, lowest[k:]) for k in range(len(lowest))]
            return (below[0] < F32_EXP_UNDERFLOW,) + tuple(b >= F32_EXP_UNDERFLOW for b in below[1:])

        block(qi, True, t)

        def body(state):
            it, _, *done = state
            j = qi - 1 - it
            for k, m in enumerate(ATTN_TOP_ROWS + (t,)):
                use = done[k] if k < len(done) else True
                if k > 0:
                    use = jnp.logical_and(use, jnp.logical_not(done[k - 1]))

                @pl.when(use)
                def _(m=m):
                    block(j, False, m)

            return (it + 1,) + flags()

        lax.while_loop(lambda s: (s[0] < qi) & s[1], body, (jnp.int32(0),) + flags())
        return c

    lax.fori_loop(0, ATTN_BLOCKS_PER_STEP, one_query_block, 0)


def _attention(qkv, batch, seq):
    qkv3 = qkv.reshape(batch, seq, 3 * SB_WIDTH)
    n_heads = SB_WIDTH // HEAD_DIM
    return pl.pallas_call(
        _attn_kernel,
        grid=(batch, seq // (ATTN_BLOCKS_PER_STEP * TQ_ATTN)),
        in_specs=[pl.BlockSpec((1, ATTN_BLOCKS_PER_STEP * TQ_ATTN, SB_WIDTH), lambda b, i: (b, i, 0)),
                  pl.BlockSpec((1, seq, SB_WIDTH), lambda b, i: (b, 0, 1)),
                  pl.BlockSpec((1, seq, SB_WIDTH), lambda b, i: (b, 0, 2))],
        out_specs=pl.BlockSpec((1, ATTN_BLOCKS_PER_STEP * TQ_ATTN, SB_WIDTH), lambda b, i: (b, i, 0)),
        out_shape=jax.ShapeDtypeStruct((batch, seq, SB_WIDTH), F32),
        scratch_shapes=[pltpu.VMEM((n_heads * TQ_ATTN, HEAD_PAIR), BF16),
                        pltpu.VMEM((n_heads * TQ_ATTN, 1), F32)],
        compiler_params=pltpu.CompilerParams(dimension_semantics=("arbitrary",) * 2,
                                             vmem_limit_bytes=VMEM_LIMIT),
        name="sb_attention",
    )(qkv3, qkv3, qkv3)


def _mix_kernel(sb_ref, sgn_ref, x_ref, sbg_ref, wout_ref, ffng_ref, wr2_ref, br_ref,
                h_ref, lg_ref):
    sbn = _rms(sb_ref[...], sbg_ref[...]).astype(BF16)
    h = x_ref[...] + _dot(sbn, wout_ref[0:SB_WIDTH, :]) + _dot(sgn_ref[...], wout_ref[SB_WIDTH:, :])
    h_ref[...] = h
    hn = _rms(h, ffng_ref[...])

    hn_hi, hn_lo = _split_bf16(hn)
    both = _dot(hn_hi, wr2_ref[...])
    logits = both[:, 0:LANES] + both[:, LANES:] + _dot(hn_lo, wr2_ref[:, 0:LANES]) + br_ref[...]
    lg_ref[...] = logits.T[0:ROUTER_ROWS, :]


def _route_kernel(lg_ref, ri_ref, rw_ref, cnt_ref, count_ref):
    tr = TM_ROUTE
    i = pl.program_id(0)

    @pl.when(i == 0)
    def _():
        count_ref[...] = jnp.zeros_like(count_ref)

    neg = jnp.float32(-jnp.inf)
    row8 = lax.broadcasted_iota(jnp.int32, (SUBLANES, tr), 0)

    def top(v):
        m = jnp.max(v, axis=0, keepdims=True)
        return m, jnp.min(jnp.where(v == m, row8, SUBLANES), axis=0, keepdims=True)

    def group_rows(g):
        return lg_ref[ROUTER_LANE0 + g * EXPERTS_PER_GROUP:ROUTER_LANE0 + (g + 1) * EXPERTS_PER_GROUP, :]

    gl = jnp.where(row8 < N_GROUPS, lg_ref[0:SUBLANES, :], neg)
    gmax, gidx = top(gl)
    gweight = 1.0 / jnp.sum(jnp.exp(gl - gmax), axis=0, keepdims=True)
    el = group_rows(0)
    for g in range(1, N_GROUPS):
        el = jnp.where(gidx == g, group_rows(g), el)
    m1, i1 = top(el)
    m2, i2 = top(jnp.where(row8 == i1, neg, el))
    t21 = jnp.exp(m2 - m1)
    w1 = gweight / (1.0 + t21)
    w2 = gweight * t21 / (1.0 + t21)
    e1 = gidx * EXPERTS_PER_GROUP + i1
    e2 = gidx * EXPERTS_PER_GROUP + i2

    row_e = lax.broadcasted_iota(jnp.int32, (N_EXPERTS, tr), 0)
    sel1 = row_e == e1
    sel2 = row_e == e2
    onehot = jnp.where(sel1 | sel2, 1.0, 0.0)
    r_t = lax.broadcasted_iota(jnp.int32, (tr, tr), 0)
    c_t = lax.broadcasted_iota(jnp.int32, (tr, tr), 1)
    before = (r_t < c_t).astype(BF16)
    running = count_ref[:, 0:1] + _dot(onehot.astype(BF16), before)
    rank1 = jnp.sum(jnp.where(sel1, running, 0.0), axis=0, keepdims=True)
    rank2 = jnp.sum(jnp.where(sel2, running, 0.0), axis=0, keepdims=True)
    new_count = count_ref[:, 0:1] + jnp.sum(onehot, axis=1, keepdims=True)
    count_ref[...] = jnp.broadcast_to(new_count, count_ref.shape)
    cnt_ref[...] = jnp.broadcast_to(new_count, cnt_ref.shape)

    ri_ref[...] = jnp.where(row8 == 0, e1, jnp.where(row8 == 1, e2, jnp.where(
        row8 == 2, rank1.astype(jnp.int32), jnp.where(row8 == 3, rank2.astype(jnp.int32), 0))))
    row128 = lax.broadcasted_iota(jnp.int32, (LANES, tr), 0)
    rw_ref[...] = jnp.where(row128 == 0, w1, jnp.where(row128 == 1, w2, 0.0)).T


def _route(lg):
    n = lg.shape[1]
    return pl.pallas_call(
        _route_kernel,
        grid=(n // TM_ROUTE,),
        in_specs=[pl.BlockSpec((ROUTER_ROWS, TM_ROUTE), lambda i: (0, i))],
        out_specs=[pl.BlockSpec((SUBLANES, TM_ROUTE), lambda i: (0, i)),
                   pl.BlockSpec((TM_ROUTE, LANES), lambda i: (i, 0)),
                   pl.BlockSpec((N_EXPERTS, LANES), lambda i: (0, 0))],
        out_shape=[jax.ShapeDtypeStruct((SUBLANES, n), jnp.int32),
                   jax.ShapeDtypeStruct((n, LANES), F32),
                   jax.ShapeDtypeStruct((N_EXPERTS, LANES), F32)],
        scratch_shapes=[pltpu.VMEM((N_EXPERTS, LANES), F32)],
        compiler_params=pltpu.CompilerParams(dimension_semantics=("arbitrary",),
                                             vmem_limit_bytes=VMEM_LIMIT),
        name="route",
    )(lg)


def _mix(sb, sgn, x2, sb_g, w_out_b, ffn_g, wr2, br):
    n = x2.shape[0]
    row = lambda i: (i, 0)
    const = lambda i: (0, 0)
    return pl.pallas_call(
        _mix_kernel,
        grid=(n // TM_MIX,),
        in_specs=[pl.BlockSpec((TM_MIX, SB_WIDTH), row),
                  pl.BlockSpec((TM_MIX, SG_WIDTH), row),
                  pl.BlockSpec((TM_MIX, D_MODEL), row),
                  pl.BlockSpec((1, SB_WIDTH), const),
                  pl.BlockSpec((D_MODEL, D_MODEL), const),
                  pl.BlockSpec((1, D_MODEL), const),
                  pl.BlockSpec((D_MODEL, 2 * LANES), const),
                  pl.BlockSpec((1, LANES), const)],
        out_specs=[pl.BlockSpec((TM_MIX, D_MODEL), row),
                   pl.BlockSpec((ROUTER_ROWS, TM_MIX), lambda i: (0, i))],
        out_shape=[jax.ShapeDtypeStruct((n, D_MODEL), F32),
                   jax.ShapeDtypeStruct((ROUTER_ROWS, n), F32)],
        compiler_params=pltpu.CompilerParams(dimension_semantics=("arbitrary",),
                                             vmem_limit_bytes=VMEM_LIMIT),
        name="mix_router",
    )(sb, sgn, x2, sb_g, w_out_b, ffn_g, wr2, br)


_PAD_BITS = tuple(1 << b for b in reversed(range(EXPERT_CHUNK.bit_length() - 1)))


def _dispatch_kernel(dest_ref, pad_start_ref, pad_count_ref, used_ref, h_ref, g_ref, zeros_ref, xs_ref,
                     hn_ref, sem, zsem):
    tm = TM_DISPATCH
    i = pl.program_id(0)
    n_steps = pl.num_programs(0) - 1
    n = n_steps * tm
    base = (i - 1) * tm
    prev = hn_ref.at[lax.rem(i + 1, 2)]
    n_chunks = xs_ref.shape[0] // (EXPERT_CHUNK * ROW_TILE)

    def pad_copies(do):
        for e in range(N_EXPERTS):
            start = pad_start_ref[e]
            count = pad_count_ref[e]
            for bit in _PAD_BITS:
                @pl.when((count & bit) != 0)
                def _(start=start, bit=bit):
                    do(pltpu.make_async_copy(_token_rows(zeros_ref, 0, bit),
                                             _token_rows(xs_ref, start, bit), zsem))
                start = start + (count & bit)
        for k in range(N_EXPERTS):
            chunk = used_ref[0] + k

            @pl.when(chunk < n_chunks)
            def _(chunk=chunk):
                do(pltpu.make_async_copy(zeros_ref, _token_rows(xs_ref, chunk * EXPERT_CHUNK, EXPERT_CHUNK),
                                         zsem))

    @pl.when(i == 0)
    def _():
        pad_copies(lambda cp: cp.start())

    @pl.when(i > 0)
    def _():
        def body(r, c):
            src = _token_rows(prev, r, 1)
            for s in range(2):
                pltpu.make_async_copy(src, _token_rows(xs_ref, dest_ref[s * n + base + r], 1),
                                      sem).start(priority=s)
            return c

        lax.fori_loop(0, tm, body, 0, unroll=8)

    @pl.when(i < n_steps)
    def _():
        _rows_to_tiles(hn_ref.at[lax.rem(i, 2)], _rms(h_ref[...], g_ref[...]))

    @pl.when(i > 0)
    def _():
        for _ in range(2):
            pltpu.make_async_copy(prev, _token_rows(xs_ref, 0, tm), sem).wait()

    @pl.when(i == n_steps)
    def _():
        pad_copies(lambda cp: cp.wait())


def _dispatch(dest, pad_start, pad_count, used_chunks, h, ffn_g, n_rows):
    n_steps = h.shape[0] // TM_DISPATCH
    zeros = jnp.zeros((EXPERT_CHUNK * ROW_TILE, LANES), F32)
    return pl.pallas_call(
        _dispatch_kernel,
        grid_spec=pltpu.PrefetchScalarGridSpec(
            num_scalar_prefetch=4,
            grid=(n_steps + 1,),
            in_specs=[pl.BlockSpec((TM_DISPATCH, D_MODEL), lambda i, *_: (jnp.minimum(i, n_steps - 1), 0)),
                      pl.BlockSpec((1, D_MODEL), lambda i, *_: (0, 0)),
                      pl.BlockSpec(memory_space=pl.ANY)],
            out_specs=pl.BlockSpec(memory_space=pl.ANY),
            scratch_shapes=[pltpu.VMEM((2, TM_DISPATCH * ROW_TILE, LANES), F32),
                            pltpu.SemaphoreType.DMA, pltpu.SemaphoreType.DMA]),
        out_shape=jax.ShapeDtypeStruct((n_rows * ROW_TILE, LANES), F32),
        compiler_params=pltpu.CompilerParams(dimension_semantics=("arbitrary",),
                                             vmem_limit_bytes=VMEM_LIMIT),
        name="dispatch",
    )(dest, pad_start, pad_count, used_chunks, h, ffn_g, zeros)


X_SLOTS = 3
TILE_CHUNKS = TM_EXPERT // EXPERT_CHUNK
W_SLOTS = 3


def _expert_kernel(tiles_ref, chunk0_ref, chunks_ref, nt_ref, used_ref, xs_ref, wg_ref, wu_ref, wd_ref,
                   zeros_ref, ys_ref, x_buf, y_buf, sg_buf, su_buf, sd_buf, wgb, wub, wdb, state,
                   w_sems, x_sems, y_sems, zsem):
    t = pl.program_id(0)
    last = pl.num_programs(0) - 1
    nt = nt_ref[0]
    n_chunks = ys_ref.shape[0] // (EXPERT_CHUNK * ROW_TILE)

    def tile_copies(tile, do, out):
        for c in range(TILE_CHUNKS):
            @pl.when(c < chunks_ref[tile])
            def _(c=c):
                first = (chunk0_ref[tile] + c) * EXPERT_CHUNK
                if out:
                    slot = lax.rem(tile, 2)
                    do(pltpu.make_async_copy(_token_rows(y_buf.at[slot], c * EXPERT_CHUNK, EXPERT_CHUNK),
                                             _token_rows(ys_ref, first, EXPERT_CHUNK), y_sems.at[slot]))
                else:
                    slot = lax.rem(tile, X_SLOTS)
                    do(pltpu.make_async_copy(_token_rows(xs_ref, first, EXPERT_CHUNK),
                                             _token_rows(x_buf.at[slot], c * EXPERT_CHUNK, EXPERT_CHUNK),
                                             x_sems.at[slot]))

    start = lambda cp: cp.start()
    wait = lambda cp: cp.wait()

    def tail_copies(do):
        for k in range(N_EXPERTS):
            chunk = used_ref[0] + k

            @pl.when(chunk < n_chunks)
            def _(chunk=chunk):
                do(pltpu.make_async_copy(zeros_ref, _token_rows(ys_ref, chunk * EXPERT_CHUNK, EXPERT_CHUNK),
                                         zsem))

    def weight_copies(e, slot):
        return (pltpu.make_async_copy(wg_ref.at[e], sg_buf.at[slot], w_sems.at[slot]),
                pltpu.make_async_copy(wu_ref.at[e], su_buf.at[slot], w_sems.at[slot]),
                pltpu.make_async_copy(wd_ref.at[e], sd_buf.at[slot], w_sems.at[slot]))

    def next_with_rows(e):
        return lax.while_loop(lambda k: (k < N_EXPERTS) & (tiles_ref[jnp.minimum(k, N_EXPERTS - 1)] == 0),
                              lambda k: k + 1, e + 1)

    @pl.when(t == 0)
    def _():
        first = next_with_rows(jnp.int32(-1))
        second = next_with_rows(first)
        state[0] = jnp.int32(-1)
        state[1] = jnp.int32(0)
        state[2] = jnp.int32(W_SLOTS - 1)
        state[3] = first
        state[4] = second
        for cp in weight_copies(first, 0):
            cp.start()

        @pl.when(second < N_EXPERTS)
        def _():
            for cp in weight_copies(second, 1):
                cp.start()

        tile_copies(0, start, False)

        @pl.when(nt > 1)
        def _():
            tile_copies(1, start, False)

        tail_copies(start)

    @pl.when(t + 2 < nt)
    def _():
        tile_copies(t + 2, start, False)

    @pl.when(t < nt)
    def _():
        @pl.when(state[1] == 0)
        def _():
            e = state[3]
            nxt = state[4]
            slot = lax.rem(state[2] + 1, W_SLOTS)
            after_next = next_with_rows(nxt)
            state[0] = e
            state[1] = tiles_ref[e]
            state[2] = slot
            state[3] = nxt
            state[4] = after_next
            for cp in weight_copies(e, slot):
                cp.wait()

            @pl.when(after_next < N_EXPERTS)
            def _():
                for cp in weight_copies(after_next, lax.rem(slot + 2, W_SLOTS)):
                    cp.start()

            wgb[...] = sg_buf[slot].astype(BF16)
            wub[...] = su_buf[slot].astype(BF16)
            wdb[...] = sd_buf[slot].astype(BF16)

        state[1] = state[1] - 1
        tile_copies(t, wait, False)

        @pl.when(t >= 2)
        def _():
            tile_copies(t - 2, wait, True)

        for n_chunks_here in range(1, TILE_CHUNKS + 1):
            @pl.when(chunks_ref[t] == n_chunks_here)
            def _(m=n_chunks_here * EXPERT_CHUNK):
                x = _tiles_to_rows(x_buf.at[lax.rem(t, X_SLOTS)], m).astype(BF16)
                g = _dot(x, wgb[...])
                u = _dot(x, wub[...])
                hidden = (g * jax.nn.sigmoid(g)) * u
                _rows_to_tiles(y_buf.at[lax.rem(t, 2)], _dot(hidden.astype(BF16), wdb[...]))

        tile_copies(t, start, True)

    @pl.when(t == last)
    def _():
        for back in (2, 1):
            @pl.when(nt >= back)
            def _(back=back):
                tile_copies(nt - back, wait, True)

        tail_copies(wait)


def _experts(tiles, chunk0, chunks, n_tiles, used_chunks, xs, wg, wu, wd):
    any_spec = pl.BlockSpec(memory_space=pl.ANY)
    zeros = jnp.zeros((EXPERT_CHUNK * ROW_TILE, LANES), F32)
    return pl.pallas_call(
        _expert_kernel,
        grid_spec=pltpu.PrefetchScalarGridSpec(
            num_scalar_prefetch=5,
            grid=(chunks.shape[0],),
            in_specs=[any_spec, any_spec, any_spec, any_spec, any_spec],
            out_specs=any_spec,
            scratch_shapes=[pltpu.VMEM((X_SLOTS, TM_EXPERT * ROW_TILE, LANES), F32),
                            pltpu.VMEM((2, TM_EXPERT * ROW_TILE, LANES), F32),
                            pltpu.VMEM((W_SLOTS, D_MODEL, D_EXPERT), F32),
                            pltpu.VMEM((W_SLOTS, D_MODEL, D_EXPERT), F32),
                            pltpu.VMEM((W_SLOTS, D_EXPERT, D_MODEL), F32),
                            pltpu.VMEM((D_MODEL, D_EXPERT), BF16),
                            pltpu.VMEM((D_MODEL, D_EXPERT), BF16),
                            pltpu.VMEM((D_EXPERT, D_MODEL), BF16),
                            pltpu.SMEM((5,), jnp.int32),
                            pltpu.SemaphoreType.DMA((W_SLOTS,)),
                            pltpu.SemaphoreType.DMA((X_SLOTS,)),
                            pltpu.SemaphoreType.DMA((2,)),
                            pltpu.SemaphoreType.DMA]),
        out_shape=jax.ShapeDtypeStruct(xs.shape, F32),
        compiler_params=pltpu.CompilerParams(dimension_semantics=("arbitrary",),
                                             vmem_limit_bytes=VMEM_LIMIT),
        name="expert_mlp",
    )(tiles, chunk0, chunks, n_tiles, used_chunks, xs, wg, wu, wd, zeros)


def _combine_kernel(dest_ref, h_ref, rw_ref, fg_ref, y_ref, o_ref, buf, sems):
    tm = TM_COMBINE
    i = pl.program_id(0)
    n_steps = pl.num_programs(0)
    n = n_steps * tm
    cur = i % 2

    def fetch(step, half):
        def body(r, c):
            for s in range(2):
                pltpu.make_async_copy(_token_rows(y_ref, dest_ref[s * n + step * tm + r], 1),
                                      _token_rows(buf.at[half, s], r, 1),
                                      sems.at[half]).start(priority=s)
            return c

        lax.fori_loop(0, tm, body, 0, unroll=8)

    @pl.when(i == 0)
    def _():
        fetch(0, 0)

    @pl.when(i + 1 < n_steps)
    def _():
        fetch(i + 1, 1 - cur)

    for s in range(2):
        pltpu.make_async_copy(_token_rows(y_ref, 0, tm), buf.at[cur, s], sems.at[cur]).wait()
    rw = rw_ref[...]
    out = (h_ref[...] + rw[:, 0:1] * _tiles_to_rows(buf.at[cur, 0], tm)
           + rw[:, 1:2] * _tiles_to_rows(buf.at[cur, 1], tm))
    o_ref[...] = _rms(out, fg_ref[...])


def _combine(dest, h, rw, final_g, ys):
    n = h.shape[0]
    return pl.pallas_call(
        _combine_kernel,
        grid_spec=pltpu.PrefetchScalarGridSpec(
            num_scalar_prefetch=1,
            grid=(n // TM_COMBINE,),
            in_specs=[pl.BlockSpec((TM_COMBINE, D_MODEL), lambda i, d: (i, 0)),
                      pl.BlockSpec((TM_COMBINE, LANES), lambda i, d: (i, 0)),
                      pl.BlockSpec((1, D_MODEL), lambda i, d: (0, 0)),
                      pl.BlockSpec(memory_space=pl.ANY)],
            out_specs=pl.BlockSpec((TM_COMBINE, D_MODEL), lambda i, d: (i, 0)),
            scratch_shapes=[pltpu.VMEM((2, 2, TM_COMBINE * ROW_TILE, LANES), F32),
                            pltpu.SemaphoreType.DMA((2,))]),
        out_shape=jax.ShapeDtypeStruct((n, D_MODEL), F32),
        compiler_params=pltpu.CompilerParams(dimension_semantics=("arbitrary",),
                                             vmem_limit_bytes=VMEM_LIMIT),
        name="combine",
    )(dest, h, rw, final_g, ys)


def _schedule(counts, max_tiles):
    chunks = (counts + EXPERT_CHUNK - 1) // EXPERT_CHUNK
    chunk_end = jnp.cumsum(chunks)
    chunk_start = chunk_end - chunks
    tiles = (chunks + TILE_CHUNKS - 1) // TILE_CHUNKS
    tile_end = jnp.cumsum(tiles)
    tile = jnp.arange(max_tiles, dtype=jnp.int32)
    owner = jnp.sum(tile[:, None] >= tile_end[None, :], axis=1)
    is_owner = owner[:, None] == jnp.arange(N_EXPERTS, dtype=jnp.int32)[None, :]
    of_owner = lambda v: jnp.sum(jnp.where(is_owner, v[None, :], 0), axis=1)
    done = (tile - of_owner(tile_end - tiles)) * TILE_CHUNKS
    tile_chunk0 = (of_owner(chunk_start) + done).astype(jnp.int32)
    tile_chunks = jnp.clip(of_owner(chunks) - done, 0, TILE_CHUNKS).astype(jnp.int32)
    return tiles, chunk_start * EXPERT_CHUNK, tile_chunk0, tile_chunks, tile_end[-1:], chunk_end[-1:]


def _layer(x, attn_g, w_in, sg_g, w_sp, b_sp, sb_g, sg_out_g, w_out, ffn_g,
           w_rg, b_rg, w_re, b_re, w_gate, w_up, w_down):
    batch, seq, _ = x.shape
    n = batch * seq
    x2 = x.reshape(n, D_MODEL)
    row = lambda v: v.reshape(1, -1)

    bsp_full = jnp.repeat(b_sp.T, HEAD_DIM, axis=1)
    qkv, sgn = _inproj(x2, row(attn_g), w_in.astype(BF16), row(sg_g), w_sp, bsp_full, row(sg_out_g))
    sb = _attention(qkv, batch, seq).reshape(n, SB_WIDTH)

    pad_lanes = lambda v, width: jnp.pad(v, [(0, 0)] * (v.ndim - 1) + [(0, width - v.shape[-1])])
    w_r = jnp.concatenate([pad_lanes(w_rg, ROUTER_LANE0),
                           jnp.transpose(w_re, (1, 0, 2)).reshape(D_MODEL, N_EXPERTS)], axis=1)
    w_r = pad_lanes(w_r, LANES)
    wr_hi = w_r.astype(BF16)
    wr_lo = (w_r - wr_hi.astype(F32)).astype(BF16)
    wr2 = jnp.concatenate([wr_hi, wr_lo], axis=1)
    b_r = pad_lanes(jnp.concatenate([pad_lanes(b_rg, ROUTER_LANE0), b_re.reshape(-1)]), LANES)

    h, lg = _mix(sb, sgn, x2, row(sb_g), w_out.astype(BF16), row(ffn_g), wr2, row(b_r))
    ri, rw, cnt = _route(lg)

    counts = cnt[:, 0].astype(jnp.int32)
    n_rows = 2 * n + N_EXPERTS * EXPERT_CHUNK
    tiles, offsets, tile_chunk0, tile_chunks, n_tiles, used_chunks = _schedule(
        counts, 2 * n // TM_EXPERT + N_EXPERTS)
    expert, rank = ri[0:2], ri[2:4]
    is_e = expert[None] == jnp.arange(N_EXPERTS, dtype=jnp.int32)[:, None, None]
    dest = (jnp.sum(jnp.where(is_e, offsets[:, None, None], 0), axis=0) + rank).reshape(-1)
    pad_start = offsets + counts
    pad_count = (-counts) % EXPERT_CHUNK

    xs = _dispatch(dest, pad_start, pad_count, used_chunks, h, row(ffn_g), n_rows)
    ys = _experts(tiles, tile_chunk0, tile_chunks, n_tiles, used_chunks, xs,
                  w_gate.reshape(N_EXPERTS, D_MODEL, D_EXPERT),
                  w_up.reshape(N_EXPERTS, D_MODEL, D_EXPERT),
                  w_down.reshape(N_EXPERTS, D_EXPERT, D_MODEL))
    return dest, h, rw, ys


def kernel(x, attn_norm_g, w_in, sg_norm_g, w_spatial, b_spatial, sb_out_norm_g, sg_out_norm_g,
           w_out, ffn_norm_g, w_router_group, b_router_group, w_router_expert, b_router_expert,
           w_gate, w_up, w_down, final_norm_g):
    assert attn_norm_g.shape[0] == 1, "single-layer problem"
    batch, seq, _ = x.shape
    dest, h, rw, ys = _layer(x, attn_norm_g[0], w_in[0], sg_norm_g[0], w_spatial[0], b_spatial[0],
                             sb_out_norm_g[0], sg_out_norm_g[0], w_out[0], ffn_norm_g[0],
                             w_router_group[0], b_router_group[0], w_router_expert[0],
                             b_router_expert[0], w_gate[0], w_up[0], w_down[0])
    out = _combine(dest, h, rw, final_norm_g.reshape(1, -1), ys)
    return out.reshape(batch, seq, D_MODEL)
```

```python
import functools
import math

import jax
import jax.numpy as jnp
from jax import lax
from jax.experimental import pallas as pl
from jax.experimental.pallas import tpu as pltpu

D_MODEL = 1024
HEAD_DIM = 64
SB_WIDTH = 512
SG_WIDTH = 512
SG_HEADS = 8
D_IN = 3 * SB_WIDTH + 2 * SG_WIDTH
CHUNK = 128
N_GROUPS = 4
EXPERTS_PER_GROUP = 8
N_EXPERTS = N_GROUPS * EXPERTS_PER_GROUP
D_EXPERT = 512
EPS = 1e-6
F32_EXP_UNDERFLOW = 110.0

LANES = 128
SUBLANES = 8
ROW_TILE = D_MODEL // LANES
assert ROW_TILE == SUBLANES
HEAD_PAIR = 2 * HEAD_DIM
ROUTER_LANE0 = SUBLANES
ROUTER_ROWS = ROUTER_LANE0 + N_EXPERTS
assert EXPERTS_PER_GROUP == SUBLANES and N_GROUPS <= ROUTER_LANE0

TM_PROJ = 1024
TQ_ATTN = 256
ATTN_BLOCKS_PER_STEP = 2
ATTN_TOP_ROWS = (160, 176)
TM_MIX = 1024
TM_ROUTE = 1024
TM_DISPATCH = 1024
TM_EXPERT = 512
EXPERT_CHUNK = 128
TM_COMBINE = 512
VMEM_LIMIT = 48 * 1024 * 1024

F32 = jnp.float32
BF16 = jnp.bfloat16


def _rms(x, g):
    return x * lax.rsqrt(jnp.mean(x * x, axis=-1, keepdims=True) + EPS) * g


def _gelu(x):
    c = math.sqrt(2.0 / math.pi)
    return x * (0.5 * (1.0 + jnp.tanh(c * (x + 0.044715 * (x * x * x)))))


def _softplus(z):
    return jnp.maximum(z, 0.0) + jnp.log(1.0 + jnp.exp(-jnp.abs(z)))


def _dot(a, b):
    return jnp.dot(a, b, preferred_element_type=F32)


def _rows_to_tiles(ref, x):
    m = x.shape[0]
    for k in range(ROW_TILE):
        ref[pl.ds(k, m, stride=ROW_TILE), :] = x[:, k * LANES:(k + 1) * LANES]


def _tiles_to_rows(ref, m):
    return jnp.concatenate([ref[pl.ds(k, m, stride=ROW_TILE), :] for k in range(ROW_TILE)], axis=1)


def _token_rows(ref, first_token, n_tokens):
    return ref.at[pl.ds(pl.multiple_of(first_token * ROW_TILE, ROW_TILE), n_tokens * ROW_TILE)]


def _split_bf16(x):
    hi = x.astype(BF16)
    lo = (x - hi.astype(F32)).astype(BF16)
    return hi, lo


def _inproj_kernel(x_ref, g_ref, w_ref, sgg_ref, wsp_ref, bsp_ref, sgog_ref, qkv_ref, sgn_ref,
                   gu_ref, vgn_ref, sg_ref):
    tm = TM_PROJ
    hb = _rms(x_ref[...], g_ref[...]).astype(BF16)
    gv = _gelu(_dot(hb, w_ref[:, 3 * SB_WIDTH + SG_WIDTH:D_IN]))
    vgn_ref[...] = _rms(gv, sgg_ref[...]).astype(BF16)
    gu_ref[...] = _gelu(_dot(hb, w_ref[:, 3 * SB_WIDTH:3 * SB_WIDTH + SG_WIDTH]))
    q = _dot(hb, w_ref[:, 0:SB_WIDTH]) * (1.0 / math.sqrt(HEAD_DIM))
    qkv_ref[:, 0:SB_WIDTH] = q.astype(BF16)
    qkv_ref[:, SB_WIDTH:2 * SB_WIDTH] = _dot(hb, w_ref[:, SB_WIDTH:2 * SB_WIDTH]).astype(BF16)

    lane = lax.broadcasted_iota(jnp.int32, (1, LANES), 1)
    first = lane < HEAD_DIM
    zero = jnp.zeros((), BF16)
    r_c = lax.broadcasted_iota(jnp.int32, (CHUNK, CHUNK), 0)
    c_c = lax.broadcasted_iota(jnp.int32, (CHUNK, CHUNK), 1)
    tril = r_c >= c_c
    n_pairs = SG_WIDTH // HEAD_PAIR
    w_pairs = []
    for p in range(n_pairs):
        w0 = jnp.where(tril, wsp_ref[2 * p], 0.0).astype(BF16)
        w1 = jnp.where(tril, wsp_ref[2 * p + 1], 0.0).astype(BF16)
        w_pairs.append(jnp.concatenate([w0, w1], axis=1))
    bsp = bsp_ref[...]
    for c in range(tm // CHUNK):
        rows = slice(c * CHUNK, (c + 1) * CHUNK)
        for p in range(n_pairs):
            cols = slice(p * HEAD_PAIR, (p + 1) * HEAD_PAIR)
            vg = vgn_ref[rows, cols]
            rhs = jnp.concatenate([jnp.where(first, vg, zero), jnp.where(first, zero, vg)], axis=0)
            mixed = _dot(w_pairs[p], rhs) + bsp[:, cols]
            sg_ref[rows, cols] = gu_ref[rows, cols] * mixed
    qkv_ref[:, 2 * SB_WIDTH:3 * SB_WIDTH] = _dot(hb, w_ref[:, 2 * SB_WIDTH:3 * SB_WIDTH]).astype(BF16)
    sgn_ref[...] = _rms(sg_ref[...], sgog_ref[...]).astype(BF16)


def _inproj(x2, attn_g, w_in_b, sg_g, wsp, bsp_full, sg_out_g):
    n = x2.shape[0]
    row = lambda i: (i, 0)
    const = lambda i: (0, 0)
    return pl.pallas_call(
        _inproj_kernel,
        grid=(n // TM_PROJ,),
        in_specs=[pl.BlockSpec((TM_PROJ, D_MODEL), row),
                  pl.BlockSpec((1, D_MODEL), const),
                  pl.BlockSpec((D_MODEL, D_IN), const),
                  pl.BlockSpec((1, SG_WIDTH), const),
                  pl.BlockSpec((SG_HEADS, CHUNK, CHUNK), lambda i: (0, 0, 0)),
                  pl.BlockSpec((CHUNK, SG_WIDTH), const),
                  pl.BlockSpec((1, SG_WIDTH), const)],
        out_specs=[pl.BlockSpec((TM_PROJ, 3 * SB_WIDTH), row),
                   pl.BlockSpec((TM_PROJ, SG_WIDTH), row)],
        out_shape=[jax.ShapeDtypeStruct((n, 3 * SB_WIDTH), BF16),
                   jax.ShapeDtypeStruct((n, SG_WIDTH), BF16)],
        scratch_shapes=[pltpu.VMEM((TM_PROJ, SG_WIDTH), F32),
                        pltpu.VMEM((TM_PROJ, SG_WIDTH), BF16),
                        pltpu.VMEM((TM_PROJ, SG_WIDTH), F32)],
        compiler_params=pltpu.CompilerParams(dimension_semantics=("arbitrary",),
                                             vmem_limit_bytes=VMEM_LIMIT),
        name="inproj",
    )(x2, attn_g, w_in_b, sg_g, wsp, bsp_full, sg_out_g)


def _attn_kernel(q_ref, k_ref, v_ref, wg_ref, wu_ref, wd_ref, o_ref, wgb_ref, wub_ref, wdb_ref,
                 q2_ref, carry_ref):
    for w_ref, wb_ref in ((wg_ref, wgb_ref), (wu_ref, wub_ref), (wd_ref, wdb_ref)):
        wb_ref[...] = w_ref[...].astype(BF16)

    t = TQ_ATTN
    n_pairs = SB_WIDTH // HEAD_PAIR
    lane = lax.broadcasted_iota(jnp.int32, (1, HEAD_PAIR), 1)
    head_lanes = (lane < HEAD_DIM, lane >= HEAD_DIM)
    zero = jnp.zeros((), BF16)
    r_idx = lax.broadcasted_iota(jnp.int32, (t, t), 0)
    c_idx = lax.broadcasted_iota(jnp.int32, (t, t), 1)
    suffix = (r_idx > c_idx).astype(BF16)
    suffix2 = jnp.concatenate([suffix, suffix], axis=0)
    causal = c_idx < r_idx

    def one_query_block(sub, c):
        qi = pl.program_id(1) * ATTN_BLOCKS_PER_STEP + sub
        row0 = pl.multiple_of(sub * t, t)
        for p in range(n_pairs):
            qp = q_ref[0, pl.ds(row0, t), p * HEAD_PAIR:(p + 1) * HEAD_PAIR]
            for h in range(2):
                q2_ref[(2 * p + h) * t:(2 * p + h + 1) * t, :] = jnp.where(head_lanes[h], qp, zero)
        o_ref[0, pl.ds(row0, t), :] = jnp.zeros((t, SB_WIDTH), F32)
        carry_ref[...] = jnp.zeros_like(carry_ref)

        def block(j, diag, m):
            start = pl.multiple_of(j * t, t)
            mask2 = jnp.concatenate([causal, causal], axis=0) if diag else None
            st = [dict() for _ in range(n_pairs)]

            def head_rows(p):
                return [slice((2 * p + h) * t, (2 * p + h) * t + m) for h in range(2)]

            def scores(p):
                d = st[p]
                d["cols"] = slice(p * HEAD_PAIR, (p + 1) * HEAD_PAIR)
                kb = k_ref[0, pl.ds(start, t), d["cols"]]
                q2 = jnp.concatenate([q2_ref[r, :] for r in head_rows(p)], axis=0)
                z = lax.dot_general(q2, kb, (((1,), (1,)), ((), ())),
                                    preferred_element_type=F32)
                sp = _softplus(z)
                nl = jnp.where(mask2, sp, 0.0) if diag else sp
                hi, lo = _split_bf16(nl)
                d["hl"] = jnp.concatenate([hi, lo], axis=1)
                d["log_beta"] = z - sp
                d["nl0"] = nl[:, 0:1]

            def weights(p):
                d = st[p]
                hl = d["hl"]
                after = jnp.concatenate([_dot(hl[0:m], suffix2), _dot(hl[m:2 * m], suffix2)], axis=0)
                carry = jnp.concatenate([carry_ref[r, :] for r in head_rows(p)], axis=0)
                a = jnp.exp(d["log_beta"] - after - carry)
                if diag:
                    a = jnp.where(mask2, a, 0.0)
                a = a.astype(BF16)
                d["a2"] = jnp.concatenate([a[0:m], a[m:2 * m]], axis=1)
                new_carry = carry + after[:, 0:1] + d["nl0"]
                for h, r in enumerate(head_rows(p)):
                    carry_ref[r, :] = new_carry[h * m:(h + 1) * m]

            def values(p):
                d = st[p]
                vb = v_ref[0, pl.ds(start, t), d["cols"]]
                v2 = jnp.concatenate([jnp.where(head_lanes[0], vb, zero),
                                      jnp.where(head_lanes[1], vb, zero)], axis=0)
                o_ref[0, pl.ds(row0, m), d["cols"]] += _dot(d["a2"], v2)

            for step in range(n_pairs + 2):
                if step < n_pairs:
                    scores(step)
                if 0 <= step - 1 < n_pairs:
                    weights(step - 1)
                if 0 <= step - 2 < n_pairs:
                    values(step - 2)

        def flags():
            bounds = (0,) + ATTN_TOP_ROWS + (t,)
            lowest = [jnp.min(jnp.concatenate([carry_ref[hh * t + lo:hh * t + hi, :] for hh in range(2 * n_pairs)],
                                              axis=0))
                      for lo, hi in zip(bounds[:-1], bounds[1:])]
            below = [functools.reduce(jnp.minimum, lowest[k:]) for k in range(len(lowest))]
            return (below[0] < F32_EXP_UNDERFLOW,) + tuple(b >= F32_EXP_UNDERFLOW for b in below[1:])

        block(qi, True, t)

        def body(state):
            it, _, *done = state
            j = qi - 1 - it
            for k, m in enumerate(ATTN_TOP_ROWS + (t,)):
                use = done[k] if k < len(done) else True
                if k > 0:
                    use = jnp.logical_and(use, jnp.logical_not(done[k - 1]))

                @pl.when(use)
                def _(m=m):
                    block(j, False, m)

            return (it + 1,) + flags()

        lax.while_loop(lambda s: (s[0] < qi) & s[1], body, (jnp.int32(0),) + flags())
        return c

    lax.fori_loop(0, ATTN_BLOCKS_PER_STEP, one_query_block, 0)


def _attention(qkv, wg, wu, wd, batch, seq):
    qkv3 = qkv.reshape(batch, seq, 3 * SB_WIDTH)
    n_heads = SB_WIDTH // HEAD_DIM
    steps = seq // (ATTN_BLOCKS_PER_STEP * TQ_ATTN)
    assert N_EXPERTS % (batch * steps) == 0, "every grid step rounds a whole number of experts' weights"
    per_step = N_EXPERTS // (batch * steps)
    weight_spec = lambda w: pl.BlockSpec((per_step,) + w.shape[1:], lambda b, i: (b * steps + i, 0, 0))
    weights = (wg, wu, wd)
    return pl.pallas_call(
        _attn_kernel,
        grid=(batch, steps),
        in_specs=[pl.BlockSpec((1, ATTN_BLOCKS_PER_STEP * TQ_ATTN, SB_WIDTH), lambda b, i: (b, i, 0)),
                  pl.BlockSpec((1, seq, SB_WIDTH), lambda b, i: (b, 0, 1)),
                  pl.BlockSpec((1, seq, SB_WIDTH), lambda b, i: (b, 0, 2))] + [weight_spec(w) for w in weights],
        out_specs=[pl.BlockSpec((1, ATTN_BLOCKS_PER_STEP * TQ_ATTN, SB_WIDTH), lambda b, i: (b, i, 0))]
        + [weight_spec(w) for w in weights],
        out_shape=[jax.ShapeDtypeStruct((batch, seq, SB_WIDTH), F32)]
        + [jax.ShapeDtypeStruct(w.shape, BF16) for w in weights],
        scratch_shapes=[pltpu.VMEM((n_heads * TQ_ATTN, HEAD_PAIR), BF16),
                        pltpu.VMEM((n_heads * TQ_ATTN, 1), F32)],
        compiler_params=pltpu.CompilerParams(dimension_semantics=("arbitrary",) * 2,
                                             vmem_limit_bytes=VMEM_LIMIT),
        name="sb_attention",
    )(qkv3, qkv3, qkv3, *weights)


def _mix_kernel(sb_ref, sgn_ref, x_ref, sbg_ref, wout_ref, ffng_ref, wr2_ref, br_ref,
                h_ref, lg_ref):
    sbn = _rms(sb_ref[...], sbg_ref[...]).astype(BF16)
    h = x_ref[...] + _dot(sbn, wout_ref[0:SB_WIDTH, :]) + _dot(sgn_ref[...], wout_ref[SB_WIDTH:, :])
    h_ref[...] = h
    hn = _rms(h, ffng_ref[...])

    hn_hi, hn_lo = _split_bf16(hn)
    both = _dot(hn_hi, wr2_ref[...])
    logits = both[:, 0:LANES] + both[:, LANES:] + _dot(hn_lo, wr2_ref[:, 0:LANES]) + br_ref[...]
    lg_ref[...] = logits.T[0:ROUTER_ROWS, :]


def _route_kernel(lg_ref, ri_ref, rw_ref, cnt_ref, count_ref):
    tr = TM_ROUTE
    i = pl.program_id(0)

    @pl.when(i == 0)
    def _():
        count_ref[...] = jnp.zeros_like(count_ref)

    neg = jnp.float32(-jnp.inf)
    row8 = lax.broadcasted_iota(jnp.int32, (SUBLANES, tr), 0)

    def top(v):
        m = jnp.max(v, axis=0, keepdims=True)
        return m, jnp.min(jnp.where(v == m, row8, SUBLANES), axis=0, keepdims=True)

    def group_rows(g):
        return lg_ref[ROUTER_LANE0 + g * EXPERTS_PER_GROUP:ROUTER_LANE0 + (g + 1) * EXPERTS_PER_GROUP, :]

    gl = jnp.where(row8 < N_GROUPS, lg_ref[0:SUBLANES, :], neg)
    gmax, gidx = top(gl)
    gweight = 1.0 / jnp.sum(jnp.exp(gl - gmax), axis=0, keepdims=True)
    el = group_rows(0)
    for g in range(1, N_GROUPS):
        el = jnp.where(gidx == g, group_rows(g), el)
    m1, i1 = top(el)
    m2, i2 = top(jnp.where(row8 == i1, neg, el))
    t21 = jnp.exp(m2 - m1)
    w1 = gweight / (1.0 + t21)
    w2 = gweight * t21 / (1.0 + t21)
    e1 = gidx * EXPERTS_PER_GROUP + i1
    e2 = gidx * EXPERTS_PER_GROUP + i2

    row_e = lax.broadcasted_iota(jnp.int32, (N_EXPERTS, tr), 0)
    sel1 = row_e == e1
    sel2 = row_e == e2
    onehot = jnp.where(sel1 | sel2, 1.0, 0.0)
    r_t = lax.broadcasted_iota(jnp.int32, (tr, tr), 0)
    c_t = lax.broadcasted_iota(jnp.int32, (tr, tr), 1)
    before = (r_t < c_t).astype(BF16)
    running = count_ref[:, 0:1] + _dot(onehot.astype(BF16), before)
    rank1 = jnp.sum(jnp.where(sel1, running, 0.0), axis=0, keepdims=True)
    rank2 = jnp.sum(jnp.where(sel2, running, 0.0), axis=0, keepdims=True)
    new_count = count_ref[:, 0:1] + jnp.sum(onehot, axis=1, keepdims=True)
    count_ref[...] = jnp.broadcast_to(new_count, count_ref.shape)
    cnt_ref[...] = jnp.broadcast_to(new_count, cnt_ref.shape)

    ri_ref[...] = jnp.where(row8 == 0, e1, jnp.where(row8 == 1, e2, jnp.where(
        row8 == 2, rank1.astype(jnp.int32), jnp.where(row8 == 3, rank2.astype(jnp.int32), 0))))
    row128 = lax.broadcasted_iota(jnp.int32, (LANES, tr), 0)
    rw_ref[...] = jnp.where(row128 == 0, w1, jnp.where(row128 == 1, w2, 0.0)).T


def _route(lg):
    n = lg.shape[1]
    return pl.pallas_call(
        _route_kernel,
        grid=(n // TM_ROUTE,),
        in_specs=[pl.BlockSpec((ROUTER_ROWS, TM_ROUTE), lambda i: (0, i))],
        out_specs=[pl.BlockSpec((SUBLANES, TM_ROUTE), lambda i: (0, i)),
                   pl.BlockSpec((TM_ROUTE, LANES), lambda i: (i, 0)),
                   pl.BlockSpec((N_EXPERTS, LANES), lambda i: (0, 0))],
        out_shape=[jax.ShapeDtypeStruct((SUBLANES, n), jnp.int32),
                   jax.ShapeDtypeStruct((n, LANES), F32),
                   jax.ShapeDtypeStruct((N_EXPERTS, LANES), F32)],
        scratch_shapes=[pltpu.VMEM((N_EXPERTS, LANES), F32)],
        compiler_params=pltpu.CompilerParams(dimension_semantics=("arbitrary",),
                                             vmem_limit_bytes=VMEM_LIMIT),
        name="route",
    )(lg)


def _mix(sb, sgn, x2, sb_g, w_out_b, ffn_g, wr2, br):
    n = x2.shape[0]
    row = lambda i: (i, 0)
    const = lambda i: (0, 0)
    return pl.pallas_call(
        _mix_kernel,
        grid=(n // TM_MIX,),
        in_specs=[pl.BlockSpec((TM_MIX, SB_WIDTH), row),
                  pl.BlockSpec((TM_MIX, SG_WIDTH), row),
                  pl.BlockSpec((TM_MIX, D_MODEL), row),
                  pl.BlockSpec((1, SB_WIDTH), const),
                  pl.BlockSpec((D_MODEL, D_MODEL), const),
                  pl.BlockSpec((1, D_MODEL), const),
                  pl.BlockSpec((D_MODEL, 2 * LANES), const),
                  pl.BlockSpec((1, LANES), const)],
        out_specs=[pl.BlockSpec((TM_MIX, D_MODEL), row),
                   pl.BlockSpec((ROUTER_ROWS, TM_MIX), lambda i: (0, i))],
        out_shape=[jax.ShapeDtypeStruct((n, D_MODEL), F32),
                   jax.ShapeDtypeStruct((ROUTER_ROWS, n), F32)],
        compiler_params=pltpu.CompilerParams(dimension_semantics=("arbitrary",),
                                             vmem_limit_bytes=VMEM_LIMIT),
        name="mix_router",
    )(sb, sgn, x2, sb_g, w_out_b, ffn_g, wr2, br)


_PAD_BITS = tuple(1 << b for b in reversed(range(EXPERT_CHUNK.bit_length() - 1)))


def _dispatch_kernel(dest_ref, pad_start_ref, pad_count_ref, used_ref, h_ref, g_ref, zeros_ref, xs_ref,
                     hn_ref, sem, zsem):
    tm = TM_DISPATCH
    i = pl.program_id(0)
    n_steps = pl.num_programs(0) - 1
    n = n_steps * tm
    base = (i - 1) * tm
    prev = hn_ref.at[lax.rem(i + 1, 2)]
    n_chunks = xs_ref.shape[0] // (EXPERT_CHUNK * ROW_TILE)

    def pad_copies(do):
        for e in range(N_EXPERTS):
            start = pad_start_ref[e]
            count = pad_count_ref[e]
            for bit in _PAD_BITS:
                @pl.when((count & bit) != 0)
                def _(start=start, bit=bit):
                    do(pltpu.make_async_copy(_token_rows(zeros_ref, 0, bit),
                                             _token_rows(xs_ref, start, bit), zsem))
                start = start + (count & bit)
        for k in range(N_EXPERTS):
            chunk = used_ref[0] + k

            @pl.when(chunk < n_chunks)
            def _(chunk=chunk):
                do(pltpu.make_async_copy(zeros_ref, _token_rows(xs_ref, chunk * EXPERT_CHUNK, EXPERT_CHUNK),
                                         zsem))

    @pl.when(i == 0)
    def _():
        pad_copies(lambda cp: cp.start())

    @pl.when(i > 0)
    def _():
        def body(r, c):
            src = _token_rows(prev, r, 1)
            for s in range(2):
                pltpu.make_async_copy(src, _token_rows(xs_ref, dest_ref[s * n + base + r], 1),
                                      sem).start(priority=s)
            return c

        lax.fori_loop(0, tm, body, 0, unroll=8)

    @pl.when(i < n_steps)
    def _():
        _rows_to_tiles(hn_ref.at[lax.rem(i, 2)], _rms(h_ref[...], g_ref[...]))

    @pl.when(i > 0)
    def _():
        for _ in range(2):
            pltpu.make_async_copy(prev, _token_rows(xs_ref, 0, tm), sem).wait()

    @pl.when(i == n_steps)
    def _():
        pad_copies(lambda cp: cp.wait())


def _dispatch(dest, pad_start, pad_count, used_chunks, h, ffn_g, n_rows):
    n_steps = h.shape[0] // TM_DISPATCH
    zeros = jnp.zeros((EXPERT_CHUNK * ROW_TILE, LANES), F32)
    return pl.pallas_call(
        _dispatch_kernel,
        grid_spec=pltpu.PrefetchScalarGridSpec(
            num_scalar_prefetch=4,
            grid=(n_steps + 1,),
            in_specs=[pl.BlockSpec((TM_DISPATCH, D_MODEL), lambda i, *_: (jnp.minimum(i, n_steps - 1), 0)),
                      pl.BlockSpec((1, D_MODEL), lambda i, *_: (0, 0)),
                      pl.BlockSpec(memory_space=pl.ANY)],
            out_specs=pl.BlockSpec(memory_space=pl.ANY),
            scratch_shapes=[pltpu.VMEM((2, TM_DISPATCH * ROW_TILE, LANES), F32),
                            pltpu.SemaphoreType.DMA, pltpu.SemaphoreType.DMA]),
        out_shape=jax.ShapeDtypeStruct((n_rows * ROW_TILE, LANES), F32),
        compiler_params=pltpu.CompilerParams(dimension_semantics=("arbitrary",),
                                             vmem_limit_bytes=VMEM_LIMIT),
        name="dispatch",
    )(dest, pad_start, pad_count, used_chunks, h, ffn_g, zeros)


X_SLOTS = 3
TILE_CHUNKS = TM_EXPERT // EXPERT_CHUNK
W_SLOTS = 3


def _expert_kernel(tiles_ref, chunk0_ref, chunks_ref, nt_ref, used_ref, xs_ref, wg_ref, wu_ref, wd_ref,
                   zeros_ref, ys_ref, x_buf, y_buf, wg_buf, wu_buf, wd_buf, state,
                   w_sems, x_sems, y_sems, zsem):
    t = pl.program_id(0)
    last = pl.num_programs(0) - 1
    nt = nt_ref[0]
    n_chunks = ys_ref.shape[0] // (EXPERT_CHUNK * ROW_TILE)

    def tile_copies(tile, do, out):
        for c in range(TILE_CHUNKS):
            @pl.when(c < chunks_ref[tile])
            def _(c=c):
                first = (chunk0_ref[tile] + c) * EXPERT_CHUNK
                if out:
                    slot = lax.rem(tile, 2)
                    do(pltpu.make_async_copy(_token_rows(y_buf.at[slot], c * EXPERT_CHUNK, EXPERT_CHUNK),
                                             _token_rows(ys_ref, first, EXPERT_CHUNK), y_sems.at[slot]))
                else:
                    slot = lax.rem(tile, X_SLOTS)
                    do(pltpu.make_async_copy(_token_rows(xs_ref, first, EXPERT_CHUNK),
                                             _token_rows(x_buf.at[slot], c * EXPERT_CHUNK, EXPERT_CHUNK),
                                             x_sems.at[slot]))

    start = lambda cp: cp.start()
    wait = lambda cp: cp.wait()

    def tail_copies(do):
        for k in range(N_EXPERTS):
            chunk = used_ref[0] + k

            @pl.when(chunk < n_chunks)
            def _(chunk=chunk):
                do(pltpu.make_async_copy(zeros_ref, _token_rows(ys_ref, chunk * EXPERT_CHUNK, EXPERT_CHUNK),
                                         zsem))

    def weight_copies(e, slot):
        return (pltpu.make_async_copy(wg_ref.at[e], wg_buf.at[slot], w_sems.at[slot]),
                pltpu.make_async_copy(wu_ref.at[e], wu_buf.at[slot], w_sems.at[slot]),
                pltpu.make_async_copy(wd_ref.at[e], wd_buf.at[slot], w_sems.at[slot]))

    def next_with_rows(e):
        return lax.while_loop(lambda k: (k < N_EXPERTS) & (tiles_ref[jnp.minimum(k, N_EXPERTS - 1)] == 0),
                              lambda k: k + 1, e + 1)

    @pl.when(t == 0)
    def _():
        first = next_with_rows(jnp.int32(-1))
        second = next_with_rows(first)
        state[0] = jnp.int32(-1)
        state[1] = jnp.int32(0)
        state[2] = jnp.int32(W_SLOTS - 1)
        state[3] = first
        state[4] = second
        for cp in weight_copies(first, 0):
            cp.start()

        @pl.when(second < N_EXPERTS)
        def _():
            for cp in weight_copies(second, 1):
                cp.start()

        tile_copies(0, start, False)

        @pl.when(nt > 1)
        def _():
            tile_copies(1, start, False)

        tail_copies(start)

    @pl.when(t + 2 < nt)
    def _():
        tile_copies(t + 2, start, False)

    @pl.when(t < nt)
    def _():
        @pl.when(state[1] == 0)
        def _():
            e = state[3]
            nxt = state[4]
            slot = lax.rem(state[2] + 1, W_SLOTS)
            after_next = next_with_rows(nxt)
            state[0] = e
            state[1] = tiles_ref[e]
            state[2] = slot
            state[3] = nxt
            state[4] = after_next
            for cp in weight_copies(e, slot):
                cp.wait()

            @pl.when(after_next < N_EXPERTS)
            def _():
                for cp in weight_copies(after_next, lax.rem(slot + 2, W_SLOTS)):
                    cp.start()

        state[1] = state[1] - 1
        w_slot = state[2]
        tile_copies(t, wait, False)

        @pl.when(t >= 2)
        def _():
            tile_copies(t - 2, wait, True)

        for n_chunks_here in range(1, TILE_CHUNKS + 1):
            @pl.when(chunks_ref[t] == n_chunks_here)
            def _(m=n_chunks_here * EXPERT_CHUNK):
                x = _tiles_to_rows(x_buf.at[lax.rem(t, X_SLOTS)], m).astype(BF16)
                g = _dot(x, wg_buf[w_slot])
                u = _dot(x, wu_buf[w_slot])
                hidden = (g * jax.nn.sigmoid(g)) * u
                _rows_to_tiles(y_buf.at[lax.rem(t, 2)], _dot(hidden.astype(BF16), wd_buf[w_slot]))

        tile_copies(t, start, True)

    @pl.when(t == last)
    def _():
        for back in (2, 1):
            @pl.when(nt >= back)
            def _(back=back):
                tile_copies(nt - back, wait, True)

        tail_copies(wait)


def _experts(tiles, chunk0, chunks, n_tiles, used_chunks, xs, wg, wu, wd):
    any_spec = pl.BlockSpec(memory_space=pl.ANY)
    zeros = jnp.zeros((EXPERT_CHUNK * ROW_TILE, LANES), F32)
    return pl.pallas_call(
        _expert_kernel,
        grid_spec=pltpu.PrefetchScalarGridSpec(
            num_scalar_prefetch=5,
            grid=(chunks.shape[0],),
            in_specs=[any_spec, any_spec, any_spec, any_spec, any_spec],
            out_specs=any_spec,
            scratch_shapes=[pltpu.VMEM((X_SLOTS, TM_EXPERT * ROW_TILE, LANES), F32),
                            pltpu.VMEM((2, TM_EXPERT * ROW_TILE, LANES), F32),
                            pltpu.VMEM((W_SLOTS, D_MODEL, D_EXPERT), BF16),
                            pltpu.VMEM((W_SLOTS, D_MODEL, D_EXPERT), BF16),
                            pltpu.VMEM((W_SLOTS, D_EXPERT, D_MODEL), BF16),
                            pltpu.SMEM((5,), jnp.int32),
                            pltpu.SemaphoreType.DMA((W_SLOTS,)),
                            pltpu.SemaphoreType.DMA((X_SLOTS,)),
                            pltpu.SemaphoreType.DMA((2,)),
                            pltpu.SemaphoreType.DMA]),
        out_shape=jax.ShapeDtypeStruct(xs.shape, F32),
        compiler_params=pltpu.CompilerParams(dimension_semantics=("arbitrary",),
                                             vmem_limit_bytes=VMEM_LIMIT),
        name="expert_mlp",
    )(tiles, chunk0, chunks, n_tiles, used_chunks, xs, wg, wu, wd, zeros)


def _combine_kernel(dest_ref, h_ref, rw_ref, fg_ref, y_ref, o_ref, buf, sems):
    tm = TM_COMBINE
    i = pl.program_id(0)
    n_steps = pl.num_programs(0)
    n = n_steps * tm
    cur = i % 2

    def fetch(step, half):
        def body(r, c):
            for s in range(2):
                pltpu.make_async_copy(_token_rows(y_ref, dest_ref[s * n + step * tm + r], 1),
                                      _token_rows(buf.at[half, s], r, 1),
                                      sems.at[half]).start(priority=s)
            return c

        lax.fori_loop(0, tm, body, 0, unroll=8)

    @pl.when(i == 0)
    def _():
        fetch(0, 0)

    @pl.when(i + 1 < n_steps)
    def _():
        fetch(i + 1, 1 - cur)

    for s in range(2):
        pltpu.make_async_copy(_token_rows(y_ref, 0, tm), buf.at[cur, s], sems.at[cur]).wait()
    rw = rw_ref[...]
    out = (h_ref[...] + rw[:, 0:1] * _tiles_to_rows(buf.at[cur, 0], tm)
           + rw[:, 1:2] * _tiles_to_rows(buf.at[cur, 1], tm))
    o_ref[...] = _rms(out, fg_ref[...])


def _combine(dest, h, rw, final_g, ys):
    n = h.shape[0]
    return pl.pallas_call(
        _combine_kernel,
        grid_spec=pltpu.PrefetchScalarGridSpec(
            num_scalar_prefetch=1,
            grid=(n // TM_COMBINE,),
            in_specs=[pl.BlockSpec((TM_COMBINE, D_MODEL), lambda i, d: (i, 0)),
                      pl.BlockSpec((TM_COMBINE, LANES), lambda i, d: (i, 0)),
                      pl.BlockSpec((1, D_MODEL), lambda i, d: (0, 0)),
                      pl.BlockSpec(memory_space=pl.ANY)],
            out_specs=pl.BlockSpec((TM_COMBINE, D_MODEL), lambda i, d: (i, 0)),
            scratch_shapes=[pltpu.VMEM((2, 2, TM_COMBINE * ROW_TILE, LANES), F32),
                            pltpu.SemaphoreType.DMA((2,))]),
        out_shape=jax.ShapeDtypeStruct((n, D_MODEL), F32),
        compiler_params=pltpu.CompilerParams(dimension_semantics=("arbitrary",),
                                             vmem_limit_bytes=VMEM_LIMIT),
        name="combine",
    )(dest, h, rw, final_g, ys)


def _schedule(counts, max_tiles):
    chunks = (counts + EXPERT_CHUNK - 1) // EXPERT_CHUNK
    chunk_end = jnp.cumsum(chunks)
    chunk_start = chunk_end - chunks
    tiles = (chunks + TILE_CHUNKS - 1) // TILE_CHUNKS
    tile_end = jnp.cumsum(tiles)
    tile = jnp.arange(max_tiles, dtype=jnp.int32)
    owner = jnp.sum(tile[:, None] >= tile_end[None, :], axis=1)
    is_owner = owner[:, None] == jnp.arange(N_EXPERTS, dtype=jnp.int32)[None, :]
    of_owner = lambda v: jnp.sum(jnp.where(is_owner, v[None, :], 0), axis=1)
    done = (tile - of_owner(tile_end - tiles)) * TILE_CHUNKS
    tile_chunk0 = (of_owner(chunk_start) + done).astype(jnp.int32)
    tile_chunks = jnp.clip(of_owner(chunks) - done, 0, TILE_CHUNKS).astype(jnp.int32)
    return tiles, chunk_start * EXPERT_CHUNK, tile_chunk0, tile_chunks, tile_end[-1:], chunk_end[-1:]


def _layer(x, attn_g, w_in, sg_g, w_sp, b_sp, sb_g, sg_out_g, w_out, ffn_g,
           w_rg, b_rg, w_re, b_re, w_gate, w_up, w_down):
    batch, seq, _ = x.shape
    n = batch * seq
    x2 = x.reshape(n, D_MODEL)
    row = lambda v: v.reshape(1, -1)

    bsp_full = jnp.repeat(b_sp.T, HEAD_DIM, axis=1)
    qkv, sgn = _inproj(x2, row(attn_g), w_in.astype(BF16), row(sg_g), w_sp, bsp_full, row(sg_out_g))
    sb, wg_b, wu_b, wd_b = _attention(qkv, w_gate.reshape(N_EXPERTS, D_MODEL, D_EXPERT),
                                      w_up.reshape(N_EXPERTS, D_MODEL, D_EXPERT),
                                      w_down.reshape(N_EXPERTS, D_EXPERT, D_MODEL), batch, seq)
    sb = sb.reshape(n, SB_WIDTH)

    pad_lanes = lambda v, width: jnp.pad(v, [(0, 0)] * (v.ndim - 1) + [(0, width - v.shape[-1])])
    w_r = jnp.concatenate([pad_lanes(w_rg, ROUTER_LANE0),
                           jnp.transpose(w_re, (1, 0, 2)).reshape(D_MODEL, N_EXPERTS)], axis=1)
    w_r = pad_lanes(w_r, LANES)
    wr_hi = w_r.astype(BF16)
    wr_lo = (w_r - wr_hi.astype(F32)).astype(BF16)
    wr2 = jnp.concatenate([wr_hi, wr_lo], axis=1)
    b_r = pad_lanes(jnp.concatenate([pad_lanes(b_rg, ROUTER_LANE0), b_re.reshape(-1)]), LANES)

    h, lg = _mix(sb, sgn, x2, row(sb_g), w_out.astype(BF16), row(ffn_g), wr2, row(b_r))
    ri, rw, cnt = _route(lg)

    counts = cnt[:, 0].astype(jnp.int32)
    n_rows = 2 * n + N_EXPERTS * EXPERT_CHUNK
    tiles, offsets, tile_chunk0, tile_chunks, n_tiles, used_chunks = _schedule(
        counts, 2 * n // TM_EXPERT + N_EXPERTS)
    expert, rank = ri[0:2], ri[2:4]
    is_e = expert[None] == jnp.arange(N_EXPERTS, dtype=jnp.int32)[:, None, None]
    dest = (jnp.sum(jnp.where(is_e, offsets[:, None, None], 0), axis=0) + rank).reshape(-1)
    pad_start = offsets + counts
    pad_count = (-counts) % EXPERT_CHUNK

    xs = _dispatch(dest, pad_start, pad_count, used_chunks, h, row(ffn_g), n_rows)
    ys = _experts(tiles, tile_chunk0, tile_chunks, n_tiles, used_chunks, xs, wg_b, wu_b, wd_b)
    return dest, h, rw, ys


def kernel(x, attn_norm_g, w_in, sg_norm_g, w_spatial, b_spatial, sb_out_norm_g, sg_out_norm_g,
           w_out, ffn_norm_g, w_router_group, b_router_group, w_router_expert, b_router_expert,
           w_gate, w_up, w_down, final_norm_g):
    assert attn_norm_g.shape[0] == 1, "single-layer problem"
    batch, seq, _ = x.shape
    dest, h, rw, ys = _layer(x, attn_norm_g[0], w_in[0], sg_norm_g[0], w_spatial[0], b_spatial[0],
                             sb_out_norm_g[0], sg_out_norm_g[0], w_out[0], ffn_norm_g[0],
                             w_router_group[0], b_router_group[0], w_router_expert[0],
                             b_router_expert[0], w_gate[0], w_up[0], w_down[0])
    out = _combine(dest, h, rw, final_norm_g.reshape(1, -1), ys)
    return out.reshape(batch, seq, D_MODEL)
```

```python
import functools
import math

import jax
import jax.numpy as jnp
from jax import lax
from jax.experimental import pallas as pl
from jax.experimental.pallas import tpu as pltpu

D_MODEL = 1024
HEAD_DIM = 64
SB_WIDTH = 512
SG_WIDTH = 512
SG_HEADS = 8
D_IN = 3 * SB_WIDTH + 2 * SG_WIDTH
CHUNK = 128
N_GROUPS = 4
EXPERTS_PER_GROUP = 8
N_EXPERTS = N_GROUPS * EXPERTS_PER_GROUP
D_EXPERT = 512
EPS = 1e-6
F32_EXP_UNDERFLOW = 110.0

LANES = 128
SUBLANES = 8
ROW_TILE = D_MODEL // LANES
assert ROW_TILE == SUBLANES
HEAD_PAIR = 2 * HEAD_DIM
ROUTER_LANE0 = SUBLANES
ROUTER_ROWS = ROUTER_LANE0 + N_EXPERTS
assert EXPERTS_PER_GROUP == SUBLANES and N_GROUPS <= ROUTER_LANE0

TM_PROJ = 1024
TQ_ATTN = 256
ATTN_BLOCKS_PER_STEP = 2
ATTN_TOP_ROWS = (160, 176)
TM_MIX = 1024
TM_DISPATCH = 1024
TM_EXPERT = 512
EXPERT_CHUNK = 128
TM_COMBINE = 512
VMEM_LIMIT = 48 * 1024 * 1024

F32 = jnp.float32
BF16 = jnp.bfloat16


def _rms(x, g):
    return x * lax.rsqrt(jnp.mean(x * x, axis=-1, keepdims=True) + EPS) * g


def _gelu(x):
    c = math.sqrt(2.0 / math.pi)
    return x * (0.5 * (1.0 + jnp.tanh(c * (x + 0.044715 * (x * x * x)))))


def _softplus(z):
    return jnp.maximum(z, 0.0) + jnp.log(1.0 + jnp.exp(-jnp.abs(z)))


def _dot(a, b):
    return jnp.dot(a, b, preferred_element_type=F32)


def _rows_to_tiles(ref, x):
    m = x.shape[0]
    for k in range(ROW_TILE):
        ref[pl.ds(k, m, stride=ROW_TILE), :] = x[:, k * LANES:(k + 1) * LANES]


def _tiles_to_rows(ref, m):
    return jnp.concatenate([ref[pl.ds(k, m, stride=ROW_TILE), :] for k in range(ROW_TILE)], axis=1)


def _token_rows(ref, first_token, n_tokens):
    return ref.at[pl.ds(pl.multiple_of(first_token * ROW_TILE, ROW_TILE), n_tokens * ROW_TILE)]


def _split_bf16(x):
    hi = x.astype(BF16)
    lo = (x - hi.astype(F32)).astype(BF16)
    return hi, lo


def _inproj_kernel(x_ref, g_ref, w_ref, sgg_ref, wsp_ref, bsp_ref, sgog_ref, qkv_ref, sgn_ref,
                   gu_ref, vgn_ref, sg_ref):
    tm = TM_PROJ
    hb = _rms(x_ref[...], g_ref[...]).astype(BF16)
    gv = _gelu(_dot(hb, w_ref[:, 3 * SB_WIDTH + SG_WIDTH:D_IN]))
    vgn_ref[...] = _rms(gv, sgg_ref[...]).astype(BF16)
    gu_ref[...] = _gelu(_dot(hb, w_ref[:, 3 * SB_WIDTH:3 * SB_WIDTH + SG_WIDTH]))
    q = _dot(hb, w_ref[:, 0:SB_WIDTH]) * (1.0 / math.sqrt(HEAD_DIM))
    qkv_ref[:, 0:SB_WIDTH] = q.astype(BF16)
    qkv_ref[:, SB_WIDTH:2 * SB_WIDTH] = _dot(hb, w_ref[:, SB_WIDTH:2 * SB_WIDTH]).astype(BF16)

    lane = lax.broadcasted_iota(jnp.int32, (1, LANES), 1)
    first = lane < HEAD_DIM
    zero = jnp.zeros((), BF16)
    r_c = lax.broadcasted_iota(jnp.int32, (CHUNK, CHUNK), 0)
    c_c = lax.broadcasted_iota(jnp.int32, (CHUNK, CHUNK), 1)
    tril = r_c >= c_c
    n_pairs = SG_WIDTH // HEAD_PAIR
    w_pairs = []
    for p in range(n_pairs):
        w0 = jnp.where(tril, wsp_ref[2 * p], 0.0).astype(BF16)
        w1 = jnp.where(tril, wsp_ref[2 * p + 1], 0.0).astype(BF16)
        w_pairs.append(jnp.concatenate([w0, w1], axis=1))
    bsp = bsp_ref[...]
    for c in range(tm // CHUNK):
        rows = slice(c * CHUNK, (c + 1) * CHUNK)
        for p in range(n_pairs):
            cols = slice(p * HEAD_PAIR, (p + 1) * HEAD_PAIR)
            vg = vgn_ref[rows, cols]
            rhs = jnp.concatenate([jnp.where(first, vg, zero), jnp.where(first, zero, vg)], axis=0)
            mixed = _dot(w_pairs[p], rhs) + bsp[:, cols]
            sg_ref[rows, cols] = gu_ref[rows, cols] * mixed
    qkv_ref[:, 2 * SB_WIDTH:3 * SB_WIDTH] = _dot(hb, w_ref[:, 2 * SB_WIDTH:3 * SB_WIDTH]).astype(BF16)
    sgn_ref[...] = _rms(sg_ref[...], sgog_ref[...]).astype(BF16)


def _inproj(x2, attn_g, w_in_b, sg_g, wsp, bsp_full, sg_out_g):
    n = x2.shape[0]
    row = lambda i: (i, 0)
    const = lambda i: (0, 0)
    return pl.pallas_call(
        _inproj_kernel,
        grid=(n // TM_PROJ,),
        in_specs=[pl.BlockSpec((TM_PROJ, D_MODEL), row),
                  pl.BlockSpec((1, D_MODEL), const),
                  pl.BlockSpec((D_MODEL, D_IN), const),
                  pl.BlockSpec((1, SG_WIDTH), const),
                  pl.BlockSpec((SG_HEADS, CHUNK, CHUNK), lambda i: (0, 0, 0)),
                  pl.BlockSpec((CHUNK, SG_WIDTH), const),
                  pl.BlockSpec((1, SG_WIDTH), const)],
        out_specs=[pl.BlockSpec((TM_PROJ, 3 * SB_WIDTH), row),
                   pl.BlockSpec((TM_PROJ, SG_WIDTH), row)],
        out_shape=[jax.ShapeDtypeStruct((n, 3 * SB_WIDTH), BF16),
                   jax.ShapeDtypeStruct((n, SG_WIDTH), BF16)],
        scratch_shapes=[pltpu.VMEM((TM_PROJ, SG_WIDTH), F32),
                        pltpu.VMEM((TM_PROJ, SG_WIDTH), BF16),
                        pltpu.VMEM((TM_PROJ, SG_WIDTH), F32)],
        compiler_params=pltpu.CompilerParams(dimension_semantics=("arbitrary",),
                                             vmem_limit_bytes=VMEM_LIMIT),
        name="inproj",
    )(x2, attn_g, w_in_b, sg_g, wsp, bsp_full, sg_out_g)


def _attn_kernel(q_ref, k_ref, v_ref, o_ref, q2_ref, carry_ref):
    t = TQ_ATTN
    n_pairs = SB_WIDTH // HEAD_PAIR
    lane = lax.broadcasted_iota(jnp.int32, (1, HEAD_PAIR), 1)
    head_lanes = (lane < HEAD_DIM, lane >= HEAD_DIM)
    zero = jnp.zeros((), BF16)
    r_idx = lax.broadcasted_iota(jnp.int32, (t, t), 0)
    c_idx = lax.broadcasted_iota(jnp.int32, (t, t), 1)
    suffix = (r_idx > c_idx).astype(BF16)
    suffix2 = jnp.concatenate([suffix, suffix], axis=0)
    causal = c_idx < r_idx

    def one_query_block(sub, c):
        qi = pl.program_id(1) * ATTN_BLOCKS_PER_STEP + sub
        row0 = pl.multiple_of(sub * t, t)
        for p in range(n_pairs):
            qp = q_ref[0, pl.ds(row0, t), p * HEAD_PAIR:(p + 1) * HEAD_PAIR]
            for h in range(2):
                q2_ref[(2 * p + h) * t:(2 * p + h + 1) * t, :] = jnp.where(head_lanes[h], qp, zero)
        o_ref[0, pl.ds(row0, t), :] = jnp.zeros((t, SB_WIDTH), F32)
        carry_ref[...] = jnp.zeros_like(carry_ref)

        def block(j, diag, m):
            start = pl.multiple_of(j * t, t)
            mask2 = jnp.concatenate([causal, causal], axis=0) if diag else None
            st = [dict() for _ in range(n_pairs)]

            def head_rows(p):
                return [slice((2 * p + h) * t, (2 * p + h) * t + m) for h in range(2)]

            def scores(p):
                d = st[p]
                d["cols"] = slice(p * HEAD_PAIR, (p + 1) * HEAD_PAIR)
                kb = k_ref[0, pl.ds(start, t), d["cols"]]
                q2 = jnp.concatenate([q2_ref[r, :] for r in head_rows(p)], axis=0)
                z = lax.dot_general(q2, kb, (((1,), (1,)), ((), ())),
                                    preferred_element_type=F32)
                sp = _softplus(z)
                nl = jnp.where(mask2, sp, 0.0) if diag else sp
                hi, lo = _split_bf16(nl)
                d["hl"] = jnp.concatenate([hi, lo], axis=1)
                d["log_beta"] = z - sp
                d["nl0"] = nl[:, 0:1]

            def weights(p):
                d = st[p]
                hl = d["hl"]
                after = jnp.concatenate([_dot(hl[0:m], suffix2), _dot(hl[m:2 * m], suffix2)], axis=0)
                carry = jnp.concatenate([carry_ref[r, :] for r in head_rows(p)], axis=0)
                a = jnp.exp(d["log_beta"] - after - carry)
                if diag:
                    a = jnp.where(mask2, a, 0.0)
                a = a.astype(BF16)
                d["a2"] = jnp.concatenate([a[0:m], a[m:2 * m]], axis=1)
                new_carry = carry + after[:, 0:1] + d["nl0"]
                for h, r in enumerate(head_rows(p)):
                    carry_ref[r, :] = new_carry[h * m:(h + 1) * m]

            def values(p):
                d = st[p]
                vb = v_ref[0, pl.ds(start, t), d["cols"]]
                v2 = jnp.concatenate([jnp.where(head_lanes[0], vb, zero),
                                      jnp.where(head_lanes[1], vb, zero)], axis=0)
                o_ref[0, pl.ds(row0, m), d["cols"]] += _dot(d["a2"], v2)

            for step in range(n_pairs + 2):
                if step < n_pairs:
                    scores(step)
                if 0 <= step - 1 < n_pairs:
                    weights(step - 1)
                if 0 <= step - 2 < n_pairs:
                    values(step - 2)

        def flags():
            bounds = (0,) + ATTN_TOP_ROWS + (t,)
            lowest = [jnp.min(jnp.concatenate([carry_ref[hh * t + lo:hh * t + hi, :] for hh in range(2 * n_pairs)],
                                              axis=0))
                      for lo, hi in zip(bounds[:-1], bounds[1:])]
            below = [functools.reduce(jnp.minimum, lowest[k:]) for k in range(len(lowest))]
            return (below[0] < F32_EXP_UNDERFLOW,) + tuple(b >= F32_EXP_UNDERFLOW for b in below[1:])

        block(qi, True, t)

        def body(state):
            it, _, *done = state
            j = qi - 1 - it
            for k, m in enumerate(ATTN_TOP_ROWS + (t,)):
                use = done[k] if k < len(done) else True
                if k > 0:
                    use = jnp.logical_and(use, jnp.logical_not(done[k - 1]))

                @pl.when(use)
                def _(m=m):
                    block(j, False, m)

            return (it + 1,) + flags()

        lax.while_loop(lambda s: (s[0] < qi) & s[1], body, (jnp.int32(0),) + flags())
        return c

    lax.fori_loop(0, ATTN_BLOCKS_PER_STEP, one_query_block, 0)


def _attention(qkv, batch, seq):
    qkv3 = qkv.reshape(batch, seq, 3 * SB_WIDTH)
    n_heads = SB_WIDTH // HEAD_DIM
    return pl.pallas_call(
        _attn_kernel,
        grid=(batch, seq // (ATTN_BLOCKS_PER_STEP * TQ_ATTN)),
        in_specs=[pl.BlockSpec((1, ATTN_BLOCKS_PER_STEP * TQ_ATTN, SB_WIDTH), lambda b, i: (b, i, 0)),
                  pl.BlockSpec((1, seq, SB_WIDTH), lambda b, i: (b, 0, 1)),
                  pl.BlockSpec((1, seq, SB_WIDTH), lambda b, i: (b, 0, 2))],
        out_specs=pl.BlockSpec((1, ATTN_BLOCKS_PER_STEP * TQ_ATTN, SB_WIDTH), lambda b, i: (b, i, 0)),
        out_shape=jax.ShapeDtypeStruct((batch, seq, SB_WIDTH), F32),
        scratch_shapes=[pltpu.VMEM((n_heads * TQ_ATTN, HEAD_PAIR), BF16),
                        pltpu.VMEM((n_heads * TQ_ATTN, 1), F32)],
        compiler_params=pltpu.CompilerParams(dimension_semantics=("arbitrary",) * 2,
                                             vmem_limit_bytes=VMEM_LIMIT),
        name="sb_attention",
    )(qkv3, qkv3, qkv3)


def _mix_kernel(sb_ref, sgn_ref, x_ref, sbg_ref, wout_ref, ffng_ref, wr2_ref, br_ref,
                h_ref, ri_ref, rw_ref, cnt_ref, lg_ref, count_ref):
    sbn = _rms(sb_ref[...], sbg_ref[...]).astype(BF16)
    h = x_ref[...] + _dot(sbn, wout_ref[0:SB_WIDTH, :]) + _dot(sgn_ref[...], wout_ref[SB_WIDTH:, :])
    h_ref[...] = h
    hn = _rms(h, ffng_ref[...])

    hn_hi, hn_lo = _split_bf16(hn)
    both = _dot(hn_hi, wr2_ref[...])
    logits = both[:, 0:LANES] + both[:, LANES:] + _dot(hn_lo, wr2_ref[:, 0:LANES]) + br_ref[...]
    lg_ref[...] = logits.T[0:ROUTER_ROWS, :]
    _route_tile(lg_ref, ri_ref, rw_ref, cnt_ref, count_ref)


def _route_tile(lg_ref, ri_ref, rw_ref, cnt_ref, count_ref):
    tr = TM_MIX
    i = pl.program_id(0)

    @pl.when(i == 0)
    def _():
        count_ref[...] = jnp.zeros_like(count_ref)

    neg = jnp.float32(-jnp.inf)
    row8 = lax.broadcasted_iota(jnp.int32, (SUBLANES, tr), 0)

    def top(v):
        m = jnp.max(v, axis=0, keepdims=True)
        return m, jnp.min(jnp.where(v == m, row8, SUBLANES), axis=0, keepdims=True)

    def group_rows(g):
        return lg_ref[ROUTER_LANE0 + g * EXPERTS_PER_GROUP:ROUTER_LANE0 + (g + 1) * EXPERTS_PER_GROUP, :]

    gl = jnp.where(row8 < N_GROUPS, lg_ref[0:SUBLANES, :], neg)
    gmax, gidx = top(gl)
    gweight = 1.0 / jnp.sum(jnp.exp(gl - gmax), axis=0, keepdims=True)
    el = group_rows(0)
    for g in range(1, N_GROUPS):
        el = jnp.where(gidx == g, group_rows(g), el)
    m1, i1 = top(el)
    m2, i2 = top(jnp.where(row8 == i1, neg, el))
    t21 = jnp.exp(m2 - m1)
    w1 = gweight / (1.0 + t21)
    w2 = gweight * t21 / (1.0 + t21)
    e1 = gidx * EXPERTS_PER_GROUP + i1
    e2 = gidx * EXPERTS_PER_GROUP + i2

    row_e = lax.broadcasted_iota(jnp.int32, (N_EXPERTS, tr), 0)
    sel1 = row_e == e1
    sel2 = row_e == e2
    onehot = jnp.where(sel1 | sel2, 1.0, 0.0)
    r_t = lax.broadcasted_iota(jnp.int32, (tr, tr), 0)
    c_t = lax.broadcasted_iota(jnp.int32, (tr, tr), 1)
    before = (r_t < c_t).astype(BF16)
    running = count_ref[:, 0:1] + _dot(onehot.astype(BF16), before)
    rank1 = jnp.sum(jnp.where(sel1, running, 0.0), axis=0, keepdims=True)
    rank2 = jnp.sum(jnp.where(sel2, running, 0.0), axis=0, keepdims=True)
    new_count = count_ref[:, 0:1] + jnp.sum(onehot, axis=1, keepdims=True)
    count_ref[...] = jnp.broadcast_to(new_count, count_ref.shape)
    cnt_ref[...] = jnp.broadcast_to(new_count, cnt_ref.shape)

    ri_ref[...] = jnp.where(row8 == 0, e1, jnp.where(row8 == 1, e2, jnp.where(
        row8 == 2, rank1.astype(jnp.int32), jnp.where(row8 == 3, rank2.astype(jnp.int32), 0))))
    row128 = lax.broadcasted_iota(jnp.int32, (LANES, tr), 0)
    rw_ref[...] = jnp.where(row128 == 0, w1, jnp.where(row128 == 1, w2, 0.0)).T


def _mix(sb, sgn, x2, sb_g, w_out_b, ffn_g, wr2, br):
    n = x2.shape[0]
    row = lambda i: (i, 0)
    const = lambda i: (0, 0)
    return pl.pallas_call(
        _mix_kernel,
        grid=(n // TM_MIX,),
        in_specs=[pl.BlockSpec((TM_MIX, SB_WIDTH), row),
                  pl.BlockSpec((TM_MIX, SG_WIDTH), row),
                  pl.BlockSpec((TM_MIX, D_MODEL), row),
                  pl.BlockSpec((1, SB_WIDTH), const),
                  pl.BlockSpec((D_MODEL, D_MODEL), const),
                  pl.BlockSpec((1, D_MODEL), const),
                  pl.BlockSpec((D_MODEL, 2 * LANES), const),
                  pl.BlockSpec((1, LANES), const)],
        out_specs=[pl.BlockSpec((TM_MIX, D_MODEL), row),
                   pl.BlockSpec((SUBLANES, TM_MIX), lambda i: (0, i)),
                   pl.BlockSpec((TM_MIX, LANES), row),
                   pl.BlockSpec((N_EXPERTS, LANES), const)],
        out_shape=[jax.ShapeDtypeStruct((n, D_MODEL), F32),
                   jax.ShapeDtypeStruct((SUBLANES, n), jnp.int32),
                   jax.ShapeDtypeStruct((n, LANES), F32),
                   jax.ShapeDtypeStruct((N_EXPERTS, LANES), F32)],
        scratch_shapes=[pltpu.VMEM((ROUTER_ROWS, TM_MIX), F32),
                        pltpu.VMEM((N_EXPERTS, LANES), F32)],
        compiler_params=pltpu.CompilerParams(dimension_semantics=("arbitrary",),
                                             vmem_limit_bytes=VMEM_LIMIT),
        name="mix_router",
    )(sb, sgn, x2, sb_g, w_out_b, ffn_g, wr2, br)


_PAD_BITS = tuple(1 << b for b in reversed(range(EXPERT_CHUNK.bit_length() - 1)))


def _dispatch_kernel(dest_ref, pad_start_ref, pad_count_ref, used_ref, h_ref, g_ref, zeros_ref, xs_ref,
                     hn_ref, sem, zsem):
    tm = TM_DISPATCH
    i = pl.program_id(0)
    n_steps = pl.num_programs(0) - 1
    n = n_steps * tm
    base = (i - 1) * tm
    prev = hn_ref.at[lax.rem(i + 1, 2)]
    n_chunks = xs_ref.shape[0] // (EXPERT_CHUNK * ROW_TILE)

    def pad_copies(do):
        for e in range(N_EXPERTS):
            start = pad_start_ref[e]
            count = pad_count_ref[e]
            for bit in _PAD_BITS:
                @pl.when((count & bit) != 0)
                def _(start=start, bit=bit):
                    do(pltpu.make_async_copy(_token_rows(zeros_ref, 0, bit),
                                             _token_rows(xs_ref, start, bit), zsem))
                start = start + (count & bit)
        for k in range(N_EXPERTS):
            chunk = used_ref[0] + k

            @pl.when(chunk < n_chunks)
            def _(chunk=chunk):
                do(pltpu.make_async_copy(zeros_ref, _token_rows(xs_ref, chunk * EXPERT_CHUNK, EXPERT_CHUNK),
                                         zsem))

    @pl.when(i == 0)
    def _():
        pad_copies(lambda cp: cp.start())

    @pl.when(i > 0)
    def _():
        def body(r, c):
            src = _token_rows(prev, r, 1)
            for s in range(2):
                pltpu.make_async_copy(src, _token_rows(xs_ref, dest_ref[s * n + base + r], 1),
                                      sem).start(priority=s)
            return c

        lax.fori_loop(0, tm, body, 0, unroll=8)

    @pl.when(i < n_steps)
    def _():
        _rows_to_tiles(hn_ref.at[lax.rem(i, 2)], _rms(h_ref[...], g_ref[...]))

    @pl.when(i > 0)
    def _():
        for _ in range(2):
            pltpu.make_async_copy(prev, _token_rows(xs_ref, 0, tm), sem).wait()

    @pl.when(i == n_steps)
    def _():
        pad_copies(lambda cp: cp.wait())


def _dispatch(dest, pad_start, pad_count, used_chunks, h, ffn_g, n_rows):
    n_steps = h.shape[0] // TM_DISPATCH
    zeros = jnp.zeros((EXPERT_CHUNK * ROW_TILE, LANES), F32)
    return pl.pallas_call(
        _dispatch_kernel,
        grid_spec=pltpu.PrefetchScalarGridSpec(
            num_scalar_prefetch=4,
            grid=(n_steps + 1,),
            in_specs=[pl.BlockSpec((TM_DISPATCH, D_MODEL), lambda i, *_: (jnp.minimum(i, n_steps - 1), 0)),
                      pl.BlockSpec((1, D_MODEL), lambda i, *_: (0, 0)),
                      pl.BlockSpec(memory_space=pl.ANY)],
            out_specs=pl.BlockSpec(memory_space=pl.ANY),
            scratch_shapes=[pltpu.VMEM((2, TM_DISPATCH * ROW_TILE, LANES), F32),
                            pltpu.SemaphoreType.DMA, pltpu.SemaphoreType.DMA]),
        out_shape=jax.ShapeDtypeStruct((n_rows * ROW_TILE, LANES), F32),
        compiler_params=pltpu.CompilerParams(dimension_semantics=("arbitrary",),
                                             vmem_limit_bytes=VMEM_LIMIT),
        name="dispatch",
    )(dest, pad_start, pad_count, used_chunks, h, ffn_g, zeros)


X_SLOTS = 3
TILE_CHUNKS = TM_EXPERT // EXPERT_CHUNK
W_SLOTS = 3


def _expert_kernel(tiles_ref, chunk0_ref, chunks_ref, nt_ref, used_ref, xs_ref, wg_ref, wu_ref, wd_ref,
                   zeros_ref, ys_ref, x_buf, y_buf, sg_buf, su_buf, sd_buf, wgb, wub, wdb, state,
                   w_sems, x_sems, y_sems, zsem):
    t = pl.program_id(0)
    last = pl.num_programs(0) - 1
    nt = nt_ref[0]
    n_chunks = ys_ref.shape[0] // (EXPERT_CHUNK * ROW_TILE)

    def tile_copies(tile, do, out):
        for c in range(TILE_CHUNKS):
            @pl.when(c < chunks_ref[tile])
            def _(c=c):
                first = (chunk0_ref[tile] + c) * EXPERT_CHUNK
                if out:
                    slot = lax.rem(tile, 2)
                    do(pltpu.make_async_copy(_token_rows(y_buf.at[slot], c * EXPERT_CHUNK, EXPERT_CHUNK),
                                             _token_rows(ys_ref, first, EXPERT_CHUNK), y_sems.at[slot]))
                else:
                    slot = lax.rem(tile, X_SLOTS)
                    do(pltpu.make_async_copy(_token_rows(xs_ref, first, EXPERT_CHUNK),
                                             _token_rows(x_buf.at[slot], c * EXPERT_CHUNK, EXPERT_CHUNK),
                                             x_sems.at[slot]))

    start = lambda cp: cp.start()
    wait = lambda cp: cp.wait()

    def tail_copies(do):
        for k in range(N_EXPERTS):
            chunk = used_ref[0] + k

            @pl.when(chunk < n_chunks)
            def _(chunk=chunk):
                do(pltpu.make_async_copy(zeros_ref, _token_rows(ys_ref, chunk * EXPERT_CHUNK, EXPERT_CHUNK),
                                         zsem))

    def weight_copies(e, slot):
        return (pltpu.make_async_copy(wg_ref.at[e], sg_buf.at[slot], w_sems.at[slot]),
                pltpu.make_async_copy(wu_ref.at[e], su_buf.at[slot], w_sems.at[slot]),
                pltpu.make_async_copy(wd_ref.at[e], sd_buf.at[slot], w_sems.at[slot]))

    def next_with_rows(e):
        return lax.while_loop(lambda k: (k < N_EXPERTS) & (tiles_ref[jnp.minimum(k, N_EXPERTS - 1)] == 0),
                              lambda k: k + 1, e + 1)

    @pl.when(t == 0)
    def _():
        first = next_with_rows(jnp.int32(-1))
        second = next_with_rows(first)
        state[0] = jnp.int32(-1)
        state[1] = jnp.int32(0)
        state[2] = jnp.int32(W_SLOTS - 1)
        state[3] = first
        state[4] = second
        for cp in weight_copies(first, 0):
            cp.start()

        @pl.when(second < N_EXPERTS)
        def _():
            for cp in weight_copies(second, 1):
                cp.start()

        tile_copies(0, start, False)

        @pl.when(nt > 1)
        def _():
            tile_copies(1, start, False)

        tail_copies(start)

    @pl.when(t + 2 < nt)
    def _():
        tile_copies(t + 2, start, False)

    @pl.when(t < nt)
    def _():
        @pl.when(state[1] == 0)
        def _():
            e = state[3]
            nxt = state[4]
            slot = lax.rem(state[2] + 1, W_SLOTS)
            after_next = next_with_rows(nxt)
            state[0] = e
            state[1] = tiles_ref[e]
            state[2] = slot
            state[3] = nxt
            state[4] = after_next
            for cp in weight_copies(e, slot):
                cp.wait()

            @pl.when(after_next < N_EXPERTS)
            def _():
                for cp in weight_copies(after_next, lax.rem(slot + 2, W_SLOTS)):
                    cp.start()

            wgb[...] = sg_buf[slot].astype(BF16)
            wub[...] = su_buf[slot].astype(BF16)
            wdb[...] = sd_buf[slot].astype(BF16)

        state[1] = state[1] - 1
        tile_copies(t, wait, False)

        @pl.when(t >= 2)
        def _():
            tile_copies(t - 2, wait, True)

        for n_chunks_here in range(1, TILE_CHUNKS + 1):
            @pl.when(chunks_ref[t] == n_chunks_here)
            def _(m=n_chunks_here * EXPERT_CHUNK):
                x = _tiles_to_rows(x_buf.at[lax.rem(t, X_SLOTS)], m).astype(BF16)
                g = _dot(x, wgb[...])
                u = _dot(x, wub[...])
                hidden = (g * jax.nn.sigmoid(g)) * u
                _rows_to_tiles(y_buf.at[lax.rem(t, 2)], _dot(hidden.astype(BF16), wdb[...]))

        tile_copies(t, start, True)

    @pl.when(t == last)
    def _():
        for back in (2, 1):
            @pl.when(nt >= back)
            def _(back=back):
                tile_copies(nt - back, wait, True)

        tail_copies(wait)


def _experts(tiles, chunk0, chunks, n_tiles, used_chunks, xs, wg, wu, wd):
    any_spec = pl.BlockSpec(memory_space=pl.ANY)
    zeros = jnp.zeros((EXPERT_CHUNK * ROW_TILE, LANES), F32)
    return pl.pallas_call(
        _expert_kernel,
        grid_spec=pltpu.PrefetchScalarGridSpec(
            num_scalar_prefetch=5,
            grid=(chunks.shape[0],),
            in_specs=[any_spec, any_spec, any_spec, any_spec, any_spec],
            out_specs=any_spec,
            scratch_shapes=[pltpu.VMEM((X_SLOTS, TM_EXPERT * ROW_TILE, LANES), F32),
                            pltpu.VMEM((2, TM_EXPERT * ROW_TILE, LANES), F32),
                            pltpu.VMEM((W_SLOTS, D_MODEL, D_EXPERT), F32),
                            pltpu.VMEM((W_SLOTS, D_MODEL, D_EXPERT), F32),
                            pltpu.VMEM((W_SLOTS, D_EXPERT, D_MODEL), F32),
                            pltpu.VMEM((D_MODEL, D_EXPERT), BF16),
                            pltpu.VMEM((D_MODEL, D_EXPERT), BF16),
                            pltpu.VMEM((D_EXPERT, D_MODEL), BF16),
                            pltpu.SMEM((5,), jnp.int32),
                            pltpu.SemaphoreType.DMA((W_SLOTS,)),
                            pltpu.SemaphoreType.DMA((X_SLOTS,)),
                            pltpu.SemaphoreType.DMA((2,)),
                            pltpu.SemaphoreType.DMA]),
        out_shape=jax.ShapeDtypeStruct(xs.shape, F32),
        compiler_params=pltpu.CompilerParams(dimension_semantics=("arbitrary",),
                                             vmem_limit_bytes=VMEM_LIMIT),
        name="expert_mlp",
    )(tiles, chunk0, chunks, n_tiles, used_chunks, xs, wg, wu, wd, zeros)


def _combine_kernel(dest_ref, h_ref, rw_ref, fg_ref, y_ref, o_ref, buf, sems):
    tm = TM_COMBINE
    i = pl.program_id(0)
    n_steps = pl.num_programs(0)
    n = n_steps * tm
    cur = i % 2

    def fetch(step, half):
        def body(r, c):
            for s in range(2):
                pltpu.make_async_copy(_token_rows(y_ref, dest_ref[s * n + step * tm + r], 1),
                                      _token_rows(buf.at[half, s], r, 1),
                                      sems.at[half]).start(priority=s)
            return c

        lax.fori_loop(0, tm, body, 0, unroll=8)

    @pl.when(i == 0)
    def _():
        fetch(0, 0)

    @pl.when(i + 1 < n_steps)
    def _():
        fetch(i + 1, 1 - cur)

    for s in range(2):
        pltpu.make_async_copy(_token_rows(y_ref, 0, tm), buf.at[cur, s], sems.at[cur]).wait()
    rw = rw_ref[...]
    out = (h_ref[...] + rw[:, 0:1] * _tiles_to_rows(buf.at[cur, 0], tm)
           + rw[:, 1:2] * _tiles_to_rows(buf.at[cur, 1], tm))
    o_ref[...] = _rms(out, fg_ref[...])


def _combine(dest, h, rw, final_g, ys):
    n = h.shape[0]
    return pl.pallas_call(
        _combine_kernel,
        grid_spec=pltpu.PrefetchScalarGridSpec(
            num_scalar_prefetch=1,
            grid=(n // TM_COMBINE,),
            in_specs=[pl.BlockSpec((TM_COMBINE, D_MODEL), lambda i, d: (i, 0)),
                      pl.BlockSpec((TM_COMBINE, LANES), lambda i, d: (i, 0)),
                      pl.BlockSpec((1, D_MODEL), lambda i, d: (0, 0)),
                      pl.BlockSpec(memory_space=pl.ANY)],
            out_specs=pl.BlockSpec((TM_COMBINE, D_MODEL), lambda i, d: (i, 0)),
            scratch_shapes=[pltpu.VMEM((2, 2, TM_COMBINE * ROW_TILE, LANES), F32),
                            pltpu.SemaphoreType.DMA((2,))]),
        out_shape=jax.ShapeDtypeStruct((n, D_MODEL), F32),
        compiler_params=pltpu.CompilerParams(dimension_semantics=("arbitrary",),
                                             vmem_limit_bytes=VMEM_LIMIT),
        name="combine",
    )(dest, h, rw, final_g, ys)


def _schedule(counts, max_tiles):
    chunks = (counts + EXPERT_CHUNK - 1) // EXPERT_CHUNK
    chunk_end = jnp.cumsum(chunks)
    chunk_start = chunk_end - chunks
    tiles = (chunks + TILE_CHUNKS - 1) // TILE_CHUNKS
    tile_end = jnp.cumsum(tiles)
    tile = jnp.arange(max_tiles, dtype=jnp.int32)
    owner = jnp.sum(tile[:, None] >= tile_end[None, :], axis=1)
    is_owner = owner[:, None] == jnp.arange(N_EXPERTS, dtype=jnp.int32)[None, :]
    of_owner = lambda v: jnp.sum(jnp.where(is_owner, v[None, :], 0), axis=1)
    done = (tile - of_owner(tile_end - tiles)) * TILE_CHUNKS
    tile_chunk0 = (of_owner(chunk_start) + done).astype(jnp.int32)
    tile_chunks = jnp.clip(of_owner(chunks) - done, 0, TILE_CHUNKS).astype(jnp.int32)
    return tiles, chunk_start * EXPERT_CHUNK, tile_chunk0, tile_chunks, tile_end[-1:], chunk_end[-1:]


def _layer(x, attn_g, w_in, sg_g, w_sp, b_sp, sb_g, sg_out_g, w_out, ffn_g,
           w_rg, b_rg, w_re, b_re, w_gate, w_up, w_down):
    batch, seq, _ = x.shape
    n = batch * seq
    x2 = x.reshape(n, D_MODEL)
    row = lambda v: v.reshape(1, -1)

    bsp_full = jnp.repeat(b_sp.T, HEAD_DIM, axis=1)
    qkv, sgn = _inproj(x2, row(attn_g), w_in.astype(BF16), row(sg_g), w_sp, bsp_full, row(sg_out_g))
    sb = _attention(qkv, batch, seq).reshape(n, SB_WIDTH)

    pad_lanes = lambda v, width: jnp.pad(v, [(0, 0)] * (v.ndim - 1) + [(0, width - v.shape[-1])])
    w_r = jnp.concatenate([pad_lanes(w_rg, ROUTER_LANE0),
                           jnp.transpose(w_re, (1, 0, 2)).reshape(D_MODEL, N_EXPERTS)], axis=1)
    w_r = pad_lanes(w_r, LANES)
    wr_hi = w_r.astype(BF16)
    wr_lo = (w_r - wr_hi.astype(F32)).astype(BF16)
    wr2 = jnp.concatenate([wr_hi, wr_lo], axis=1)
    b_r = pad_lanes(jnp.concatenate([pad_lanes(b_rg, ROUTER_LANE0), b_re.reshape(-1)]), LANES)

    h, ri, rw, cnt = _mix(sb, sgn, x2, row(sb_g), w_out.astype(BF16), row(ffn_g), wr2, row(b_r))

    counts = cnt[:, 0].astype(jnp.int32)
    n_rows = 2 * n + N_EXPERTS * EXPERT_CHUNK
    tiles, offsets, tile_chunk0, tile_chunks, n_tiles, used_chunks = _schedule(
        counts, 2 * n // TM_EXPERT + N_EXPERTS)
    expert, rank = ri[0:2], ri[2:4]
    is_e = expert[None] == jnp.arange(N_EXPERTS, dtype=jnp.int32)[:, None, None]
    dest = (jnp.sum(jnp.where(is_e, offsets[:, None, None], 0), axis=0) + rank).reshape(-1)
    pad_start = offsets + counts
    pad_count = (-counts) % EXPERT_CHUNK

    xs = _dispatch(dest, pad_start, pad_count, used_chunks, h, row(ffn_g), n_rows)
    ys = _experts(tiles, tile_chunk0, tile_chunks, n_tiles, used_chunks, xs,
                  w_gate.reshape(N_EXPERTS, D_MODEL, D_EXPERT),
                  w_up.reshape(N_EXPERTS, D_MODEL, D_EXPERT),
                  w_down.reshape(N_EXPERTS, D_EXPERT, D_MODEL))
    return dest, h, rw, ys


def kernel(x, attn_norm_g, w_in, sg_norm_g, w_spatial, b_spatial, sb_out_norm_g, sg_out_norm_g,
           w_out, ffn_norm_g, w_router_group, b_router_group, w_router_expert, b_router_expert,
           w_gate, w_up, w_down, final_norm_g):
    assert attn_norm_g.shape[0] == 1, "single-layer problem"
    batch, seq, _ = x.shape
    dest, h, rw, ys = _layer(x, attn_norm_g[0], w_in[0], sg_norm_g[0], w_spatial[0], b_spatial[0],
                             sb_out_norm_g[0], sg_out_norm_g[0], w_out[0], ffn_norm_g[0],
                             w_router_group[0], b_router_group[0], w_router_expert[0],
                             b_router_expert[0], w_gate[0], w_up[0], w_down[0])
    out = _combine(dest, h, rw, final_norm_g.reshape(1, -1), ys)
    return out.reshape(batch, seq, D_MODEL)
```

```python
import functools
import math

import jax
import jax.numpy as jnp
from jax import lax
from jax.experimental import pallas as pl
from jax.experimental.pallas import tpu as pltpu

D_MODEL = 1024
HEAD_DIM = 64
SB_WIDTH = 512
SG_WIDTH = 512
SG_HEADS = 8
D_IN = 3 * SB_WIDTH + 2 * SG_WIDTH
CHUNK = 128
N_GROUPS = 4
EXPERTS_PER_GROUP = 8
N_EXPERTS = N_GROUPS * EXPERTS_PER_GROUP
D_EXPERT = 512
EPS = 1e-6
F32_EXP_UNDERFLOW = 110.0

LANES = 128
SUBLANES = 8
ROW_TILE = D_MODEL // LANES
assert ROW_TILE == SUBLANES
HEAD_PAIR = 2 * HEAD_DIM
ROUTER_LANE0 = SUBLANES
ROUTER_ROWS = ROUTER_LANE0 + N_EXPERTS
assert EXPERTS_PER_GROUP == SUBLANES and N_GROUPS <= ROUTER_LANE0

TM_PROJ = 1024
TQ_ATTN = 256
ATTN_BLOCKS_PER_STEP = 2
ATTN_TOP_ROWS = (160, 176)
TM_MIX = 1024
TM_ROUTE = 1024
TM_DISPATCH = 1024
TM_EXPERT = 512
EXPERT_CHUNK = 128
TM_COMBINE = 512
VMEM_LIMIT = 48 * 1024 * 1024

F32 = jnp.float32
BF16 = jnp.bfloat16


def _rms(x, g):
    return x * lax.rsqrt(jnp.mean(x * x, axis=-1, keepdims=True) + EPS) * g


def _gelu(x):
    c = math.sqrt(2.0 / math.pi)
    return x * (0.5 * (1.0 + jnp.tanh(c * (x + 0.044715 * (x * x * x)))))


def _softplus(z):
    return jnp.maximum(z, 0.0) + jnp.log(1.0 + jnp.exp(-jnp.abs(z)))


def _dot(a, b):
    return jnp.dot(a, b, preferred_element_type=F32)


def _rows_to_tiles(ref, x):
    m = x.shape[0]
    for k in range(ROW_TILE):
        ref[pl.ds(k, m, stride=ROW_TILE), :] = x[:, k * LANES:(k + 1) * LANES]


def _tiles_to_rows(ref, m):
    return jnp.concatenate([ref[pl.ds(k, m, stride=ROW_TILE), :] for k in range(ROW_TILE)], axis=1)


def _token_rows(ref, first_token, n_tokens):
    return ref.at[pl.ds(pl.multiple_of(first_token * ROW_TILE, ROW_TILE), n_tokens * ROW_TILE)]


def _split_bf16(x):
    hi = x.astype(BF16)
    lo = (x - hi.astype(F32)).astype(BF16)
    return hi, lo


def _inproj_kernel(x_ref, g_ref, w_ref, sgg_ref, wsp_ref, bsp_ref, sgog_ref, qkv_ref, sgn_ref,
                   gu_ref, vgn_ref, sg_ref):
    tm = TM_PROJ
    hb = _rms(x_ref[...], g_ref[...]).astype(BF16)
    gv = _gelu(_dot(hb, w_ref[:, 3 * SB_WIDTH + SG_WIDTH:D_IN]))
    vgn_ref[...] = _rms(gv, sgg_ref[...]).astype(BF16)
    gu_ref[...] = _gelu(_dot(hb, w_ref[:, 3 * SB_WIDTH:3 * SB_WIDTH + SG_WIDTH]))
    q = _dot(hb, w_ref[:, 0:SB_WIDTH]) * (1.0 / math.sqrt(HEAD_DIM))
    qkv_ref[:, 0:SB_WIDTH] = q.astype(BF16)
    qkv_ref[:, SB_WIDTH:2 * SB_WIDTH] = _dot(hb, w_ref[:, SB_WIDTH:2 * SB_WIDTH]).astype(BF16)

    lane = lax.broadcasted_iota(jnp.int32, (1, LANES), 1)
    first = lane < HEAD_DIM
    zero = jnp.zeros((), BF16)
    r_c = lax.broadcasted_iota(jnp.int32, (CHUNK, CHUNK), 0)
    c_c = lax.broadcasted_iota(jnp.int32, (CHUNK, CHUNK), 1)
    tril = r_c >= c_c
    n_pairs = SG_WIDTH // HEAD_PAIR
    w_pairs = []
    for p in range(n_pairs):
        w0 = jnp.where(tril, wsp_ref[2 * p], 0.0).astype(BF16)
        w1 = jnp.where(tril, wsp_ref[2 * p + 1], 0.0).astype(BF16)
        w_pairs.append(jnp.concatenate([w0, w1], axis=1))
    bsp = bsp_ref[...]
    for c in range(tm // CHUNK):
        rows = slice(c * CHUNK, (c + 1) * CHUNK)
        for p in range(n_pairs):
            cols = slice(p * HEAD_PAIR, (p + 1) * HEAD_PAIR)
            vg = vgn_ref[rows, cols]
            rhs = jnp.concatenate([jnp.where(first, vg, zero), jnp.where(first, zero, vg)], axis=0)
            mixed = _dot(w_pairs[p], rhs) + bsp[:, cols]
            sg_ref[rows, cols] = gu_ref[rows, cols] * mixed
    qkv_ref[:, 2 * SB_WIDTH:3 * SB_WIDTH] = _dot(hb, w_ref[:, 2 * SB_WIDTH:3 * SB_WIDTH]).astype(BF16)
    sgn_ref[...] = _rms(sg_ref[...], sgog_ref[...]).astype(BF16)


def _inproj(x2, attn_g, w_in_b, sg_g, wsp, bsp_full, sg_out_g):
    n = x2.shape[0]
    row = lambda i: (i, 0)
    const = lambda i: (0, 0)
    return pl.pallas_call(
        _inproj_kernel,
        grid=(n // TM_PROJ,),
        in_specs=[pl.BlockSpec((TM_PROJ, D_MODEL), row),
                  pl.BlockSpec((1, D_MODEL), const),
                  pl.BlockSpec((D_MODEL, D_IN), const),
                  pl.BlockSpec((1, SG_WIDTH), const),
                  pl.BlockSpec((SG_HEADS, CHUNK, CHUNK), lambda i: (0, 0, 0)),
                  pl.BlockSpec((CHUNK, SG_WIDTH), const),
                  pl.BlockSpec((1, SG_WIDTH), const)],
        out_specs=[pl.BlockSpec((TM_PROJ, 3 * SB_WIDTH), row),
                   pl.BlockSpec((TM_PROJ, SG_WIDTH), row)],
        out_shape=[jax.ShapeDtypeStruct((n, 3 * SB_WIDTH), BF16),
                   jax.ShapeDtypeStruct((n, SG_WIDTH), BF16)],
        scratch_shapes=[pltpu.VMEM((TM_PROJ, SG_WIDTH), F32),
                        pltpu.VMEM((TM_PROJ, SG_WIDTH), BF16),
                        pltpu.VMEM((TM_PROJ, SG_WIDTH), F32)],
        compiler_params=pltpu.CompilerParams(dimension_semantics=("arbitrary",),
                                             vmem_limit_bytes=VMEM_LIMIT),
        name="inproj",
    )(x2, attn_g, w_in_b, sg_g, wsp, bsp_full, sg_out_g)


def _attn_kernel(q_ref, k_ref, v_ref, o_ref, q2_ref, carry_ref):
    t = TQ_ATTN
    n_pairs = SB_WIDTH // HEAD_PAIR
    lane = lax.broadcasted_iota(jnp.int32, (1, HEAD_PAIR), 1)
    head_lanes = (lane < HEAD_DIM, lane >= HEAD_DIM)
    zero = jnp.zeros((), BF16)
    r_idx = lax.broadcasted_iota(jnp.int32, (t, t), 0)
    c_idx = lax.broadcasted_iota(jnp.int32, (t, t), 1)
    suffix = (r_idx > c_idx).astype(BF16)
    suffix2 = jnp.concatenate([suffix, suffix], axis=0)
    causal = c_idx < r_idx

    def one_query_block(sub, c):
        qi = pl.program_id(1) * ATTN_BLOCKS_PER_STEP + sub
        row0 = pl.multiple_of(sub * t, t)
        for p in range(n_pairs):
            qp = q_ref[0, pl.ds(row0, t), p * HEAD_PAIR:(p + 1) * HEAD_PAIR]
            for h in range(2):
                q2_ref[(2 * p + h) * t:(2 * p + h + 1) * t, :] = jnp.where(head_lanes[h], qp, zero)
        o_ref[0, pl.ds(row0, t), :] = jnp.zeros((t, SB_WIDTH), F32)
        carry_ref[...] = jnp.zeros_like(carry_ref)

        def block(j, diag, m):
            start = pl.multiple_of(j * t, t)
            mask2 = jnp.concatenate([causal, causal], axis=0) if diag else None
            st = [dict() for _ in range(n_pairs)]

            def head_rows(p):
                return [slice((2 * p + h) * t, (2 * p + h) * t + m) for h in range(2)]

            def scores(p):
                d = st[p]
                d["cols"] = slice(p * HEAD_PAIR, (p + 1) * HEAD_PAIR)
                kb = k_ref[0, pl.ds(start, t), d["cols"]]
                q2 = jnp.concatenate([q2_ref[r, :] for r in head_rows(p)], axis=0)
                z = lax.dot_general(q2, kb, (((1,), (1,)), ((), ())),
                                    preferred_element_type=F32)
                sp = _softplus(z)
                nl = jnp.where(mask2, sp, 0.0) if diag else sp
                hi, lo = _split_bf16(nl)
                d["hl"] = jnp.concatenate([hi, lo], axis=1)
                d["log_beta"] = z - sp
                d["nl0"] = nl[:, 0:1]

            def weights(p):
                d = st[p]
                hl = d["hl"]
                after = jnp.concatenate([_dot(hl[0:m], suffix2), _dot(hl[m:2 * m], suffix2)], axis=0)
                carry = jnp.concatenate([carry_ref[r, :] for r in head_rows(p)], axis=0)
                a = jnp.exp(d["log_beta"] - after - carry)
                if diag:
                    a = jnp.where(mask2, a, 0.0)
                a = a.astype(BF16)
                d["a2"] = jnp.concatenate([a[0:m], a[m:2 * m]], axis=1)
                new_carry = carry + after[:, 0:1] + d["nl0"]
                for h, r in enumerate(head_rows(p)):
                    carry_ref[r, :] = new_carry[h * m:(h + 1) * m]

            def values(p):
                d = st[p]
                vb = v_ref[0, pl.ds(start, t), d["cols"]]
                v2 = jnp.concatenate([jnp.where(head_lanes[0], vb, zero),
                                      jnp.where(head_lanes[1], vb, zero)], axis=0)
                o_ref[0, pl.ds(row0, m), d["cols"]] += _dot(d["a2"], v2)

            for step in range(n_pairs + 2):
                if step < n_pairs:
                    scores(step)
                if 0 <= step - 1 < n_pairs:
                    weights(step - 1)
                if 0 <= step - 2 < n_pairs:
                    values(step - 2)

        def flags():
            bounds = (0,) + ATTN_TOP_ROWS + (t,)
            lowest = [jnp.min(jnp.concatenate([carry_ref[hh * t + lo:hh * t + hi, :] for hh in range(2 * n_pairs)],
                                              axis=0))
                      for lo, hi in zip(bounds[:-1], bounds[1:])]
            below = [functools.reduce(jnp.minimum, lowest[k:]) for k in range(len(lowest))]
            return (below[0] < F32_EXP_UNDERFLOW,) + tuple(b >= F32_EXP_UNDERFLOW for b in below[1:])

        block(qi, True, t)

        def body(state):
            it, _, *done = state
            j = qi - 1 - it
            for k, m in enumerate(ATTN_TOP_ROWS + (t,)):
                use = done[k] if k < len(done) else True
                if k > 0:
                    use = jnp.logical_and(use, jnp.logical_not(done[k - 1]))

                @pl.when(use)
                def _(m=m):
                    block(j, False, m)

            return (it + 1,) + flags()

        lax.while_loop(lambda s: (s[0] < qi) & s[1], body, (jnp.int32(0),) + flags())
        return c

    lax.fori_loop(0, ATTN_BLOCKS_PER_STEP, one_query_block, 0)


def _attention(qkv, batch, seq):
    qkv3 = qkv.reshape(batch, seq, 3 * SB_WIDTH)
    n_heads = SB_WIDTH // HEAD_DIM
    return pl.pallas_call(
        _attn_kernel,
        grid=(batch, seq // (ATTN_BLOCKS_PER_STEP * TQ_ATTN)),
        in_specs=[pl.BlockSpec((1, ATTN_BLOCKS_PER_STEP * TQ_ATTN, SB_WIDTH), lambda b, i: (b, i, 0)),
                  pl.BlockSpec((1, seq, SB_WIDTH), lambda b, i: (b, 0, 1)),
                  pl.BlockSpec((1, seq, SB_WIDTH), lambda b, i: (b, 0, 2))],
        out_specs=pl.BlockSpec((1, ATTN_BLOCKS_PER_STEP * TQ_ATTN, SB_WIDTH), lambda b, i: (b, i, 0)),
        out_shape=jax.ShapeDtypeStruct((batch, seq, SB_WIDTH), F32),
        scratch_shapes=[pltpu.VMEM((n_heads * TQ_ATTN, HEAD_PAIR), BF16),
                        pltpu.VMEM((n_heads * TQ_ATTN, 1), F32)],
        compiler_params=pltpu.CompilerParams(dimension_semantics=("arbitrary",) * 2,
                                             vmem_limit_bytes=VMEM_LIMIT),
        name="sb_attention",
    )(qkv3, qkv3, qkv3)


def _mix_kernel(sb_ref, sgn_ref, x_ref, sbg_ref, wout_ref, ffng_ref, wr2_ref, br_ref,
                h_ref, lg_ref):
    sbn = _rms(sb_ref[...], sbg_ref[...]).astype(BF16)
    h = x_ref[...] + _dot(sbn, wout_ref[0:SB_WIDTH, :]) + _dot(sgn_ref[...], wout_ref[SB_WIDTH:, :])
    h_ref[...] = h
    hn = _rms(h, ffng_ref[...])

    hn_hi, hn_lo = _split_bf16(hn)
    both = _dot(hn_hi, wr2_ref[...])
    logits = both[:, 0:LANES] + both[:, LANES:] + _dot(hn_lo, wr2_ref[:, 0:LANES]) + br_ref[...]
    lg_ref[...] = logits.T[0:ROUTER_ROWS, :]


def _route_kernel(lg_ref, ri_ref, rw_ref, cnt_ref, count_ref):
    tr = TM_ROUTE
    i = pl.program_id(0)

    @pl.when(i == 0)
    def _():
        count_ref[...] = jnp.zeros_like(count_ref)

    neg = jnp.float32(-jnp.inf)
    row8 = lax.broadcasted_iota(jnp.int32, (SUBLANES, tr), 0)

    def top(v):
        m = jnp.max(v, axis=0, keepdims=True)
        return m, jnp.min(jnp.where(v == m, row8, SUBLANES), axis=0, keepdims=True)

    def group_rows(g):
        return lg_ref[ROUTER_LANE0 + g * EXPERTS_PER_GROUP:ROUTER_LANE0 + (g + 1) * EXPERTS_PER_GROUP, :]

    gl = jnp.where(row8 < N_GROUPS, lg_ref[0:SUBLANES, :], neg)
    gmax, gidx = top(gl)
    gweight = 1.0 / jnp.sum(jnp.exp(gl - gmax), axis=0, keepdims=True)
    el = group_rows(0)
    for g in range(1, N_GROUPS):
        el = jnp.where(gidx == g, group_rows(g), el)
    m1, i1 = top(el)
    m2, i2 = top(jnp.where(row8 == i1, neg, el))
    t21 = jnp.exp(m2 - m1)
    w1 = gweight / (1.0 + t21)
    w2 = gweight * t21 / (1.0 + t21)
    e1 = gidx * EXPERTS_PER_GROUP + i1
    e2 = gidx * EXPERTS_PER_GROUP + i2

    row_e = lax.broadcasted_iota(jnp.int32, (N_EXPERTS, tr), 0)
    sel1 = row_e == e1
    sel2 = row_e == e2
    onehot = jnp.where(sel1 | sel2, 1.0, 0.0)
    r_t = lax.broadcasted_iota(jnp.int32, (tr, tr), 0)
    c_t = lax.broadcasted_iota(jnp.int32, (tr, tr), 1)
    before = (r_t < c_t).astype(BF16)
    running = count_ref[:, 0:1] + _dot(onehot.astype(BF16), before)
    rank1 = jnp.sum(jnp.where(sel1, running, 0.0), axis=0, keepdims=True)
    rank2 = jnp.sum(jnp.where(sel2, running, 0.0), axis=0, keepdims=True)
    new_count = count_ref[:, 0:1] + jnp.sum(onehot, axis=1, keepdims=True)
    count_ref[...] = jnp.broadcast_to(new_count, count_ref.shape)
    cnt_ref[...] = jnp.broadcast_to(new_count, cnt_ref.shape)

    ri_ref[...] = jnp.where(row8 == 0, e1, jnp.where(row8 == 1, e2, jnp.where(
        row8 == 2, rank1.astype(jnp.int32), jnp.where(row8 == 3, rank2.astype(jnp.int32), 0))))
    row128 = lax.broadcasted_iota(jnp.int32, (LANES, tr), 0)
    rw_ref[...] = jnp.where(row128 == 0, w1, jnp.where(row128 == 1, w2, 0.0)).T


def _route(lg):
    n = lg.shape[1]
    return pl.pallas_call(
        _route_kernel,
        grid=(n // TM_ROUTE,),
        in_specs=[pl.BlockSpec((ROUTER_ROWS, TM_ROUTE), lambda i: (0, i))],
        out_specs=[pl.BlockSpec((SUBLANES, TM_ROUTE), lambda i: (0, i)),
                   pl.BlockSpec((TM_ROUTE, LANES), lambda i: (i, 0)),
                   pl.BlockSpec((N_EXPERTS, LANES), lambda i: (0, 0))],
        out_shape=[jax.ShapeDtypeStruct((SUBLANES, n), jnp.int32),
                   jax.ShapeDtypeStruct((n, LANES), F32),
                   jax.ShapeDtypeStruct((N_EXPERTS, LANES), F32)],
        scratch_shapes=[pltpu.VMEM((N_EXPERTS, LANES), F32)],
        compiler_params=pltpu.CompilerParams(dimension_semantics=("arbitrary",),
                                             vmem_limit_bytes=VMEM_LIMIT),
        name="route",
    )(lg)


def _mix(sb, sgn, x2, sb_g, w_out_b, ffn_g, wr2, br):
    n = x2.shape[0]
    row = lambda i: (i, 0)
    const = lambda i: (0, 0)
    return pl.pallas_call(
        _mix_kernel,
        grid=(n // TM_MIX,),
        in_specs=[pl.BlockSpec((TM_MIX, SB_WIDTH), row),
                  pl.BlockSpec((TM_MIX, SG_WIDTH), row),
                  pl.BlockSpec((TM_MIX, D_MODEL), row),
                  pl.BlockSpec((1, SB_WIDTH), const),
                  pl.BlockSpec((D_MODEL, D_MODEL), const),
                  pl.BlockSpec((1, D_MODEL), const),
                  pl.BlockSpec((D_MODEL, 2 * LANES), const),
                  pl.BlockSpec((1, LANES), const)],
        out_specs=[pl.BlockSpec((TM_MIX, D_MODEL), row),
                   pl.BlockSpec((ROUTER_ROWS, TM_MIX), lambda i: (0, i))],
        out_shape=[jax.ShapeDtypeStruct((n, D_MODEL), F32),
                   jax.ShapeDtypeStruct((ROUTER_ROWS, n), F32)],
        compiler_params=pltpu.CompilerParams(dimension_semantics=("arbitrary",),
                                             vmem_limit_bytes=VMEM_LIMIT),
        name="mix_router",
    )(sb, sgn, x2, sb_g, w_out_b, ffn_g, wr2, br)


_PAD_BITS = tuple(1 << b for b in reversed(range(EXPERT_CHUNK.bit_length() - 1)))


def _dispatch_kernel(dest_ref, pad_start_ref, pad_count_ref, used_ref, h_ref, g_ref, zeros_ref, xs_ref,
                     hn_ref, sem, zsem):
    tm = TM_DISPATCH
    i = pl.program_id(0)
    n_steps = pl.num_programs(0) - 1
    n = n_steps * tm
    base = (i - 1) * tm
    prev = hn_ref.at[lax.rem(i + 1, 2)]
    n_chunks = xs_ref.shape[0] // (EXPERT_CHUNK * ROW_TILE)

    def pad_copies(do):
        for e in range(N_EXPERTS):
            start = pad_start_ref[e]
            count = pad_count_ref[e]
            for bit in _PAD_BITS:
                @pl.when((count & bit) != 0)
                def _(start=start, bit=bit):
                    do(pltpu.make_async_copy(_token_rows(zeros_ref, 0, bit),
                                             _token_rows(xs_ref, start, bit), zsem))
                start = start + (count & bit)
        for k in range(N_EXPERTS):
            chunk = used_ref[0] + k

            @pl.when(chunk < n_chunks)
            def _(chunk=chunk):
                do(pltpu.make_async_copy(zeros_ref, _token_rows(xs_ref, chunk * EXPERT_CHUNK, EXPERT_CHUNK),
                                         zsem))

    @pl.when(i == 0)
    def _():
        pad_copies(lambda cp: cp.start())

    @pl.when(i > 0)
    def _():
        def body(r, c):
            src = _token_rows(prev, r, 1)
            for s in range(2):
                pltpu.make_async_copy(src, _token_rows(xs_ref, dest_ref[s * n + base + r], 1),
                                      sem).start(priority=s)
            return c

        lax.fori_loop(0, tm, body, 0, unroll=8)

    @pl.when(i < n_steps)
    def _():
        _rows_to_tiles(hn_ref.at[lax.rem(i, 2)], _rms(h_ref[...], g_ref[...]))

    @pl.when(i > 0)
    def _():
        for _ in range(2):
            pltpu.make_async_copy(prev, _token_rows(xs_ref, 0, tm), sem).wait()

    @pl.when(i == n_steps)
    def _():
        pad_copies(lambda cp: cp.wait())


def _dispatch(dest, pad_start, pad_count, used_chunks, h, ffn_g, n_rows):
    n_steps = h.shape[0] // TM_DISPATCH
    zeros = jnp.zeros((EXPERT_CHUNK * ROW_TILE, LANES), F32)
    return pl.pallas_call(
        _dispatch_kernel,
        grid_spec=pltpu.PrefetchScalarGridSpec(
            num_scalar_prefetch=4,
            grid=(n_steps + 1,),
            in_specs=[pl.BlockSpec((TM_DISPATCH, D_MODEL), lambda i, *_: (jnp.minimum(i, n_steps - 1), 0)),
                      pl.BlockSpec((1, D_MODEL), lambda i, *_: (0, 0)),
                      pl.BlockSpec(memory_space=pl.ANY)],
            out_specs=pl.BlockSpec(memory_space=pl.ANY),
            scratch_shapes=[pltpu.VMEM((2, TM_DISPATCH * ROW_TILE, LANES), F32),
                            pltpu.SemaphoreType.DMA, pltpu.SemaphoreType.DMA]),
        out_shape=jax.ShapeDtypeStruct((n_rows * ROW_TILE, LANES), F32),
        compiler_params=pltpu.CompilerParams(dimension_semantics=("arbitrary",),
                                             vmem_limit_bytes=VMEM_LIMIT),
        name="dispatch",
    )(dest, pad_start, pad_count, used_chunks, h, ffn_g, zeros)


X_SLOTS = 3
TILE_CHUNKS = TM_EXPERT // EXPERT_CHUNK
W_SLOTS = 3


def _expert_kernel(tiles_ref, chunk0_ref, chunks_ref, nt_ref, used_ref, xs_ref, wg_ref, wu_ref, wd_ref,
                   zeros_ref, ys_ref, x_buf, y_buf, sg_buf, su_buf, sd_buf, wgb, wub, wdb, state,
                   w_sems, x_sems, y_sems, zsem):
    t = pl.program_id(0)
    last = pl.num_programs(0) - 1
    nt = nt_ref[0]
    n_chunks = ys_ref.shape[0] // (EXPERT_CHUNK * ROW_TILE)

    def tile_copies(tile, do, out):
        for c in range(TILE_CHUNKS):
            @pl.when(c < chunks_ref[tile])
            def _(c=c):
                first = (chunk0_ref[tile] + c) * EXPERT_CHUNK
                if out:
                    slot = lax.rem(tile, 2)
                    do(pltpu.make_async_copy(_token_rows(y_buf.at[slot], c * EXPERT_CHUNK, EXPERT_CHUNK),
                                             _token_rows(ys_ref, first, EXPERT_CHUNK), y_sems.at[slot]))
                else:
                    slot = lax.rem(tile, X_SLOTS)
                    do(pltpu.make_async_copy(_token_rows(xs_ref, first, EXPERT_CHUNK),
                                             _token_rows(x_buf.at[slot], c * EXPERT_CHUNK, EXPERT_CHUNK),
                                             x_sems.at[slot]))

    start = lambda cp: cp.start()
    wait = lambda cp: cp.wait()

    def tail_copies(do):
        for k in range(N_EXPERTS):
            chunk = used_ref[0] + k

            @pl.when(chunk < n_chunks)
            def _(chunk=chunk):
                do(pltpu.make_async_copy(zeros_ref, _token_rows(ys_ref, chunk * EXPERT_CHUNK, EXPERT_CHUNK),
                                         zsem))

    def weight_copies(e, slot):
        return (pltpu.make_async_copy(wg_ref.at[e], sg_buf.at[slot], w_sems.at[slot]),
                pltpu.make_async_copy(wu_ref.at[e], su_buf.at[slot], w_sems.at[slot]),
                pltpu.make_async_copy(wd_ref.at[e], sd_buf.at[slot], w_sems.at[slot]))

    def next_with_rows(e):
        return lax.while_loop(lambda k: (k < N_EXPERTS) & (tiles_ref[jnp.minimum(k, N_EXPERTS - 1)] == 0),
                              lambda k: k + 1, e + 1)

    @pl.when(t == 0)
    def _():
        first = next_with_rows(jnp.int32(-1))
        second = next_with_rows(first)
        state[0] = jnp.int32(-1)
        state[1] = jnp.int32(0)
        state[2] = jnp.int32(W_SLOTS - 1)
        state[3] = first
        state[4] = second
        for cp in weight_copies(first, 0):
            cp.start()

        @pl.when(second < N_EXPERTS)
        def _():
            for cp in weight_copies(second, 1):
                cp.start(priority=1)

        tile_copies(0, start, False)

        @pl.when(nt > 1)
        def _():
            tile_copies(1, start, False)

        tail_copies(start)

    @pl.when(t + 2 < nt)
    def _():
        tile_copies(t + 2, start, False)

    @pl.when(t < nt)
    def _():
        @pl.when(state[1] == 0)
        def _():
            e = state[3]
            nxt = state[4]
            slot = lax.rem(state[2] + 1, W_SLOTS)
            after_next = next_with_rows(nxt)
            state[0] = e
            state[1] = tiles_ref[e]
            state[2] = slot
            state[3] = nxt
            state[4] = after_next
            for cp in weight_copies(e, slot):
                cp.wait()

            @pl.when(after_next < N_EXPERTS)
            def _():
                for cp in weight_copies(after_next, lax.rem(slot + 2, W_SLOTS)):
                    cp.start(priority=1)

            wgb[...] = sg_buf[slot].astype(BF16)
            wub[...] = su_buf[slot].astype(BF16)
            wdb[...] = sd_buf[slot].astype(BF16)

        state[1] = state[1] - 1
        tile_copies(t, wait, False)

        @pl.when(t >= 2)
        def _():
            tile_copies(t - 2, wait, True)

        for n_chunks_here in range(1, TILE_CHUNKS + 1):
            @pl.when(chunks_ref[t] == n_chunks_here)
            def _(m=n_chunks_here * EXPERT_CHUNK):
                x = _tiles_to_rows(x_buf.at[lax.rem(t, X_SLOTS)], m).astype(BF16)
                g = _dot(x, wgb[...])
                u = _dot(x, wub[...])
                hidden = (g * jax.nn.sigmoid(g)) * u
                _rows_to_tiles(y_buf.at[lax.rem(t, 2)], _dot(hidden.astype(BF16), wdb[...]))

        tile_copies(t, start, True)

    @pl.when(t == last)
    def _():
        for back in (2, 1):
            @pl.when(nt >= back)
            def _(back=back):
                tile_copies(nt - back, wait, True)

        tail_copies(wait)


def _experts(tiles, chunk0, chunks, n_tiles, used_chunks, xs, wg, wu, wd):
    any_spec = pl.BlockSpec(memory_space=pl.ANY)
    zeros = jnp.zeros((EXPERT_CHUNK * ROW_TILE, LANES), F32)
    return pl.pallas_call(
        _expert_kernel,
        grid_spec=pltpu.PrefetchScalarGridSpec(
            num_scalar_prefetch=5,
            grid=(chunks.shape[0],),
            in_specs=[any_spec, any_spec, any_spec, any_spec, any_spec],
            out_specs=any_spec,
            scratch_shapes=[pltpu.VMEM((X_SLOTS, TM_EXPERT * ROW_TILE, LANES), F32),
                            pltpu.VMEM((2, TM_EXPERT * ROW_TILE, LANES), F32),
                            pltpu.VMEM((W_SLOTS, D_MODEL, D_EXPERT), F32),
                            pltpu.VMEM((W_SLOTS, D_MODEL, D_EXPERT), F32),
                            pltpu.VMEM((W_SLOTS, D_EXPERT, D_MODEL), F32),
                            pltpu.VMEM((D_MODEL, D_EXPERT), BF16),
                            pltpu.VMEM((D_MODEL, D_EXPERT), BF16),
                            pltpu.VMEM((D_EXPERT, D_MODEL), BF16),
                            pltpu.SMEM((5,), jnp.int32),
                            pltpu.SemaphoreType.DMA((W_SLOTS,)),
                            pltpu.SemaphoreType.DMA((X_SLOTS,)),
                            pltpu.SemaphoreType.DMA((2,)),
                            pltpu.SemaphoreType.DMA]),
        out_shape=jax.ShapeDtypeStruct(xs.shape, F32),
        compiler_params=pltpu.CompilerParams(dimension_semantics=("arbitrary",),
                                             vmem_limit_bytes=VMEM_LIMIT),
        name="expert_mlp",
    )(tiles, chunk0, chunks, n_tiles, used_chunks, xs, wg, wu, wd, zeros)


def _combine_kernel(dest_ref, h_ref, rw_ref, fg_ref, y_ref, o_ref, buf, sems):
    tm = TM_COMBINE
    i = pl.program_id(0)
    n_steps = pl.num_programs(0)
    n = n_steps * tm
    cur = i % 2

    def fetch(step, half):
        def body(r, c):
            for s in range(2):
                pltpu.make_async_copy(_token_rows(y_ref, dest_ref[s * n + step * tm + r], 1),
                                      _token_rows(buf.at[half, s], r, 1),
                                      sems.at[half]).start(priority=s)
            return c

        lax.fori_loop(0, tm, body, 0, unroll=8)

    @pl.when(i == 0)
    def _():
        fetch(0, 0)

    @pl.when(i + 1 < n_steps)
    def _():
        fetch(i + 1, 1 - cur)

    for s in range(2):
        pltpu.make_async_copy(_token_rows(y_ref, 0, tm), buf.at[cur, s], sems.at[cur]).wait()
    rw = rw_ref[...]
    out = (h_ref[...] + rw[:, 0:1] * _tiles_to_rows(buf.at[cur, 0], tm)
           + rw[:, 1:2] * _tiles_to_rows(buf.at[cur, 1], tm))
    o_ref[...] = _rms(out, fg_ref[...])


def _combine(dest, h, rw, final_g, ys):
    n = h.shape[0]
    return pl.pallas_call(
        _combine_kernel,
        grid_spec=pltpu.PrefetchScalarGridSpec(
            num_scalar_prefetch=1,
            grid=(n // TM_COMBINE,),
            in_specs=[pl.BlockSpec((TM_COMBINE, D_MODEL), lambda i, d: (i, 0)),
                      pl.BlockSpec((TM_COMBINE, LANES), lambda i, d: (i, 0)),
                      pl.BlockSpec((1, D_MODEL), lambda i, d: (0, 0)),
                      pl.BlockSpec(memory_space=pl.ANY)],
            out_specs=pl.BlockSpec((TM_COMBINE, D_MODEL), lambda i, d: (i, 0)),
            scratch_shapes=[pltpu.VMEM((2, 2, TM_COMBINE * ROW_TILE, LANES), F32),
                            pltpu.SemaphoreType.DMA((2,))]),
        out_shape=jax.ShapeDtypeStruct((n, D_MODEL), F32),
        compiler_params=pltpu.CompilerParams(dimension_semantics=("arbitrary",),
                                             vmem_limit_bytes=VMEM_LIMIT),
        name="combine",
    )(dest, h, rw, final_g, ys)


def _schedule(counts, max_tiles):
    chunks = (counts + EXPERT_CHUNK - 1) // EXPERT_CHUNK
    chunk_end = jnp.cumsum(chunks)
    chunk_start = chunk_end - chunks
    tiles = (chunks + TILE_CHUNKS - 1) // TILE_CHUNKS
    tile_end = jnp.cumsum(tiles)
    tile = jnp.arange(max_tiles, dtype=jnp.int32)
    owner = jnp.sum(tile[:, None] >= tile_end[None, :], axis=1)
    is_owner = owner[:, None] == jnp.arange(N_EXPERTS, dtype=jnp.int32)[None, :]
    of_owner = lambda v: jnp.sum(jnp.where(is_owner, v[None, :], 0), axis=1)
    done = (tile - of_owner(tile_end - tiles)) * TILE_CHUNKS
    tile_chunk0 = (of_owner(chunk_start) + done).astype(jnp.int32)
    tile_chunks = jnp.clip(of_owner(chunks) - done, 0, TILE_CHUNKS).astype(jnp.int32)
    return tiles, chunk_start * EXPERT_CHUNK, tile_chunk0, tile_chunks, tile_end[-1:], chunk_end[-1:]


def _layer(x, attn_g, w_in, sg_g, w_sp, b_sp, sb_g, sg_out_g, w_out, ffn_g,
           w_rg, b_rg, w_re, b_re, w_gate, w_up, w_down):
    batch, seq, _ = x.shape
    n = batch * seq
    x2 = x.reshape(n, D_MODEL)
    row = lambda v: v.reshape(1, -1)

    bsp_full = jnp.repeat(b_sp.T, HEAD_DIM, axis=1)
    qkv, sgn = _inproj(x2, row(attn_g), w_in.astype(BF16), row(sg_g), w_sp, bsp_full, row(sg_out_g))
    sb = _attention(qkv, batch, seq).reshape(n, SB_WIDTH)

    pad_lanes = lambda v, width: jnp.pad(v, [(0, 0)] * (v.ndim - 1) + [(0, width - v.shape[-1])])
    w_r = jnp.concatenate([pad_lanes(w_rg, ROUTER_LANE0),
                           jnp.transpose(w_re, (1, 0, 2)).reshape(D_MODEL, N_EXPERTS)], axis=1)
    w_r = pad_lanes(w_r, LANES)
    wr_hi = w_r.astype(BF16)
    wr_lo = (w_r - wr_hi.astype(F32)).astype(BF16)
    wr2 = jnp.concatenate([wr_hi, wr_lo], axis=1)
    b_r = pad_lanes(jnp.concatenate([pad_lanes(b_rg, ROUTER_LANE0), b_re.reshape(-1)]), LANES)

    h, lg = _mix(sb, sgn, x2, row(sb_g), w_out.astype(BF16), row(ffn_g), wr2, row(b_r))
    ri, rw, cnt = _route(lg)

    counts = cnt[:, 0].astype(jnp.int32)
    n_rows = 2 * n + N_EXPERTS * EXPERT_CHUNK
    tiles, offsets, tile_chunk0, tile_chunks, n_tiles, used_chunks = _schedule(
        counts, 2 * n // TM_EXPERT + N_EXPERTS)
    expert, rank = ri[0:2], ri[2:4]
    is_e = expert[None] == jnp.arange(N_EXPERTS, dtype=jnp.int32)[:, None, None]
    dest = (jnp.sum(jnp.where(is_e, offsets[:, None, None], 0), axis=0) + rank).reshape(-1)
    pad_start = offsets + counts
    pad_count = (-counts) % EXPERT_CHUNK

    xs = _dispatch(dest, pad_start, pad_count, used_chunks, h, row(ffn_g), n_rows)
    ys = _experts(tiles, tile_chunk0, tile_chunks, n_tiles, used_chunks, xs,
                  w_gate.reshape(N_EXPERTS, D_MODEL, D_EXPERT),
                  w_up.reshape(N_EXPERTS, D_MODEL, D_EXPERT),
                  w_down.reshape(N_EXPERTS, D_EXPERT, D_MODEL))
    return dest, h, rw, ys


def kernel(x, attn_norm_g, w_in, sg_norm_g, w_spatial, b_spatial, sb_out_norm_g, sg_out_norm_g,
           w_out, ffn_norm_g, w_router_group, b_router_group, w_router_expert, b_router_expert,
           w_gate, w_up, w_down, final_norm_g):
    assert attn_norm_g.shape[0] == 1, "single-layer problem"
    batch, seq, _ = x.shape
    dest, h, rw, ys = _layer(x, attn_norm_g[0], w_in[0], sg_norm_g[0], w_spatial[0], b_spatial[0],
                             sb_out_norm_g[0], sg_out_norm_g[0], w_out[0], ffn_norm_g[0],
                             w_router_group[0], b_router_group[0], w_router_expert[0],
                             b_router_expert[0], w_gate[0], w_up[0], w_down[0])
    out = _combine(dest, h, rw, final_norm_g.reshape(1, -1), ys)
    return out.reshape(batch, seq, D_MODEL)
```

```python
import functools
import math

import jax
import jax.numpy as jnp
from jax import lax
from jax.experimental import pallas as pl
from jax.experimental.pallas import tpu as pltpu

D_MODEL = 1024
HEAD_DIM = 64
SB_WIDTH = 512
SG_WIDTH = 512
SG_HEADS = 8
D_IN = 3 * SB_WIDTH + 2 * SG_WIDTH
CHUNK = 128
N_GROUPS = 4
EXPERTS_PER_GROUP = 8
N_EXPERTS = N_GROUPS * EXPERTS_PER_GROUP
D_EXPERT = 512
EPS = 1e-6
F32_EXP_UNDERFLOW = 110.0

LANES = 128
SUBLANES = 8
ROW_TILE = D_MODEL // LANES
assert ROW_TILE == SUBLANES
HEAD_PAIR = 2 * HEAD_DIM
ROUTER_LANE0 = SUBLANES
ROUTER_ROWS = ROUTER_LANE0 + N_EXPERTS
assert EXPERTS_PER_GROUP == SUBLANES and N_GROUPS <= ROUTER_LANE0

TM_PROJ = 1024
TQ_ATTN = 256
ATTN_BLOCKS_PER_STEP = 2
ATTN_TOP_ROWS = (160, 176)
TM_MIX = 1024
TM_ROUTE = 1024
TM_DISPATCH = 1024
TM_EXPERT = 512
EXPERT_CHUNK = 128
TM_COMBINE = 512
VMEM_LIMIT = 48 * 1024 * 1024

F32 = jnp.float32
BF16 = jnp.bfloat16


def _rms(x, g):
    return x * lax.rsqrt(jnp.mean(x * x, axis=-1, keepdims=True) + EPS) * g


def _gelu(x):
    c = math.sqrt(2.0 / math.pi)
    return x * (0.5 * (1.0 + jnp.tanh(c * (x + 0.044715 * (x * x * x)))))


def _softplus(z):
    return jnp.maximum(z, 0.0) + jnp.log(1.0 + jnp.exp(-jnp.abs(z)))


def _dot(a, b):
    return jnp.dot(a, b, preferred_element_type=F32)


def _rows_to_tiles(ref, x):
    m = x.shape[0]
    for k in range(ROW_TILE):
        ref[pl.ds(k, m, stride=ROW_TILE), :] = x[:, k * LANES:(k + 1) * LANES]


def _tiles_to_rows(ref, m):
    return jnp.concatenate([ref[pl.ds(k, m, stride=ROW_TILE), :] for k in range(ROW_TILE)], axis=1)


def _token_rows(ref, first_token, n_tokens):
    return ref.at[pl.ds(pl.multiple_of(first_token * ROW_TILE, ROW_TILE), n_tokens * ROW_TILE)]


def _split_bf16(x):
    hi = x.astype(BF16)
    lo = (x - hi.astype(F32)).astype(BF16)
    return hi, lo


def _inproj_kernel(x_ref, g_ref, w_ref, sgg_ref, wsp_ref, bsp_ref, sgog_ref, qkv_ref, sgn_ref,
                   gu_ref, vgn_ref, sg_ref):
    tm = TM_PROJ
    hb = _rms(x_ref[...], g_ref[...]).astype(BF16)
    gv = _gelu(_dot(hb, w_ref[:, 3 * SB_WIDTH + SG_WIDTH:D_IN]))
    vgn_ref[...] = _rms(gv, sgg_ref[...]).astype(BF16)
    gu_ref[...] = _gelu(_dot(hb, w_ref[:, 3 * SB_WIDTH:3 * SB_WIDTH + SG_WIDTH]))
    q = _dot(hb, w_ref[:, 0:SB_WIDTH]) * (1.0 / math.sqrt(HEAD_DIM))
    qkv_ref[:, 0:SB_WIDTH] = q.astype(BF16)
    qkv_ref[:, SB_WIDTH:2 * SB_WIDTH] = _dot(hb, w_ref[:, SB_WIDTH:2 * SB_WIDTH]).astype(BF16)

    lane = lax.broadcasted_iota(jnp.int32, (1, LANES), 1)
    first = lane < HEAD_DIM
    zero = jnp.zeros((), BF16)
    r_c = lax.broadcasted_iota(jnp.int32, (CHUNK, CHUNK), 0)
    c_c = lax.broadcasted_iota(jnp.int32, (CHUNK, CHUNK), 1)
    tril = r_c >= c_c
    n_pairs = SG_WIDTH // HEAD_PAIR
    w_pairs = []
    for p in range(n_pairs):
        w0 = jnp.where(tril, wsp_ref[2 * p], 0.0).astype(BF16)
        w1 = jnp.where(tril, wsp_ref[2 * p + 1], 0.0).astype(BF16)
        w_pairs.append(jnp.concatenate([w0, w1], axis=1))
    bsp = bsp_ref[...]
    for c in range(tm // CHUNK):
        rows = slice(c * CHUNK, (c + 1) * CHUNK)
        for p in range(n_pairs):
            cols = slice(p * HEAD_PAIR, (p + 1) * HEAD_PAIR)
            vg = vgn_ref[rows, cols]
            rhs = jnp.concatenate([jnp.where(first, vg, zero), jnp.where(first, zero, vg)], axis=0)
            mixed = _dot(w_pairs[p], rhs) + bsp[:, cols]
            sg_ref[rows, cols] = gu_ref[rows, cols] * mixed
    qkv_ref[:, 2 * SB_WIDTH:3 * SB_WIDTH] = _dot(hb, w_ref[:, 2 * SB_WIDTH:3 * SB_WIDTH]).astype(BF16)
    sgn_ref[...] = _rms(sg_ref[...], sgog_ref[...]).astype(BF16)


def _inproj(x2, attn_g, w_in_b, sg_g, wsp, bsp_full, sg_out_g):
    n = x2.shape[0]
    row = lambda i: (i, 0)
    const = lambda i: (0, 0)
    return pl.pallas_call(
        _inproj_kernel,
        grid=(n // TM_PROJ,),
        in_specs=[pl.BlockSpec((TM_PROJ, D_MODEL), row),
                  pl.BlockSpec((1, D_MODEL), const),
                  pl.BlockSpec((D_MODEL, D_IN), const),
                  pl.BlockSpec((1, SG_WIDTH), const),
                  pl.BlockSpec((SG_HEADS, CHUNK, CHUNK), lambda i: (0, 0, 0)),
                  pl.BlockSpec((CHUNK, SG_WIDTH), const),
                  pl.BlockSpec((1, SG_WIDTH), const)],
        out_specs=[pl.BlockSpec((TM_PROJ, 3 * SB_WIDTH), row),
                   pl.BlockSpec((TM_PROJ, SG_WIDTH), row)],
        out_shape=[jax.ShapeDtypeStruct((n, 3 * SB_WIDTH), BF16),
                   jax.ShapeDtypeStruct((n, SG_WIDTH), BF16)],
        scratch_shapes=[pltpu.VMEM((TM_PROJ, SG_WIDTH), F32),
                        pltpu.VMEM((TM_PROJ, SG_WIDTH), BF16),
                        pltpu.VMEM((TM_PROJ, SG_WIDTH), F32)],
        compiler_params=pltpu.CompilerParams(dimension_semantics=("arbitrary",),
                                             vmem_limit_bytes=VMEM_LIMIT),
        name="inproj",
    )(x2, attn_g, w_in_b, sg_g, wsp, bsp_full, sg_out_g)


def _attn_kernel(q_ref, k_ref, v_ref, o_ref, q2_ref, carry_ref):
    t = TQ_ATTN
    n_pairs = SB_WIDTH // HEAD_PAIR
    lane = lax.broadcasted_iota(jnp.int32, (1, HEAD_PAIR), 1)
    head_lanes = (lane < HEAD_DIM, lane >= HEAD_DIM)
    zero = jnp.zeros((), BF16)
    r_idx = lax.broadcasted_iota(jnp.int32, (t, t), 0)
    c_idx = lax.broadcasted_iota(jnp.int32, (t, t), 1)
    suffix = (r_idx > c_idx).astype(BF16)
    suffix2 = jnp.concatenate([suffix, suffix], axis=0)
    causal = c_idx < r_idx

    def one_query_block(sub, c):
        qi = pl.program_id(1) * ATTN_BLOCKS_PER_STEP + sub
        row0 = pl.multiple_of(sub * t, t)
        for p in range(n_pairs):
            qp = q_ref[0, pl.ds(row0, t), p * HEAD_PAIR:(p + 1) * HEAD_PAIR]
            for h in range(2):
                q2_ref[(2 * p + h) * t:(2 * p + h + 1) * t, :] = jnp.where(head_lanes[h], qp, zero)
        o_ref[0, pl.ds(row0, t), :] = jnp.zeros((t, SB_WIDTH), F32)
        carry_ref[...] = jnp.zeros_like(carry_ref)

        def block(j, diag, m):
            start = pl.multiple_of(j * t, t)
            mask2 = jnp.concatenate([causal, causal], axis=0) if diag else None
            st = [dict() for _ in range(n_pairs)]

            def head_rows(p):
                return [slice((2 * p + h) * t, (2 * p + h) * t + m) for h in range(2)]

            def scores(p):
                d = st[p]
                d["cols"] = slice(p * HEAD_PAIR, (p + 1) * HEAD_PAIR)
                kb = k_ref[0, pl.ds(start, t), d["cols"]]
                q2 = jnp.concatenate([q2_ref[r, :] for r in head_rows(p)], axis=0)
                z = lax.dot_general(q2, kb, (((1,), (1,)), ((), ())),
                                    preferred_element_type=F32)
                sp = _softplus(z)
                nl = jnp.where(mask2, sp, 0.0) if diag else sp
                hi, lo = _split_bf16(nl)
                d["hl"] = jnp.concatenate([hi, lo], axis=1)
                d["log_beta"] = z - sp
                d["nl0"] = nl[:, 0:1]

            def weights(p):
                d = st[p]
                hl = d["hl"]
                after = jnp.concatenate([_dot(hl[0:m], suffix2), _dot(hl[m:2 * m], suffix2)], axis=0)
                carry = jnp.concatenate([carry_ref[r, :] for r in head_rows(p)], axis=0)
                a = jnp.exp(d["log_beta"] - after - carry)
                if diag:
                    a = jnp.where(mask2, a, 0.0)
                a = a.astype(BF16)
                d["a2"] = jnp.concatenate([a[0:m], a[m:2 * m]], axis=1)
                new_carry = carry + after[:, 0:1] + d["nl0"]
                for h, r in enumerate(head_rows(p)):
                    carry_ref[r, :] = new_carry[h * m:(h + 1) * m]

            def values(p):
                d = st[p]
                vb = v_ref[0, pl.ds(start, t), d["cols"]]
                v2 = jnp.concatenate([jnp.where(head_lanes[0], vb, zero),
                                      jnp.where(head_lanes[1], vb, zero)], axis=0)
                o_ref[0, pl.ds(row0, m), d["cols"]] += _dot(d["a2"], v2)

            for step in range(n_pairs + 2):
                if step < n_pairs:
                    scores(step)
                if 0 <= step - 1 < n_pairs:
                    weights(step - 1)
                if 0 <= step - 2 < n_pairs:
                    values(step - 2)

        def flags():
            bounds = (0,) + ATTN_TOP_ROWS + (t,)
            lowest = [jnp.min(jnp.concatenate([carry_ref[hh * t + lo:hh * t + hi, :] for hh in range(2 * n_pairs)],
                                              axis=0))
                      for lo, hi in zip(bounds[:-1], bounds[1:])]
            below = [functools.reduce(jnp.minimum, lowest[k:]) for k in range(len(lowest))]
            return (below[0] < F32_EXP_UNDERFLOW,) + tuple(b >= F32_EXP_UNDERFLOW for b in below[1:])

        block(qi, True, t)

        def body(state):
            it, _, *done = state
            j = qi - 1 - it
            for k, m in enumerate(ATTN_TOP_ROWS + (t,)):
                use = done[k] if k < len(done) else True
                if k > 0:
                    use = jnp.logical_and(use, jnp.logical_not(done[k - 1]))

                @pl.when(use)
                def _(m=m):
                    block(j, False, m)

            return (it + 1,) + flags()

        lax.while_loop(lambda s: (s[0] < qi) & s[1], body, (jnp.int32(0),) + flags())
        return c

    lax.fori_loop(0, ATTN_BLOCKS_PER_STEP, one_query_block, 0)


def _attention(qkv, batch, seq):
    qkv3 = qkv.reshape(batch, seq, 3 * SB_WIDTH)
    n_heads = SB_WIDTH // HEAD_DIM
    return pl.pallas_call(
        _attn_kernel,
        grid=(batch, seq // (ATTN_BLOCKS_PER_STEP * TQ_ATTN)),
        in_specs=[pl.BlockSpec((1, ATTN_BLOCKS_PER_STEP * TQ_ATTN, SB_WIDTH), lambda b, i: (b, i, 0)),
                  pl.BlockSpec((1, seq, SB_WIDTH), lambda b, i: (b, 0, 1)),
                  pl.BlockSpec((1, seq, SB_WIDTH), lambda b, i: (b, 0, 2))],
        out_specs=pl.BlockSpec((1, ATTN_BLOCKS_PER_STEP * TQ_ATTN, SB_WIDTH), lambda b, i: (b, i, 0)),
        out_shape=jax.ShapeDtypeStruct((batch, seq, SB_WIDTH), F32),
        scratch_shapes=[pltpu.VMEM((n_heads * TQ_ATTN, HEAD_PAIR), BF16),
                        pltpu.VMEM((n_heads * TQ_ATTN, 1), F32)],
        compiler_params=pltpu.CompilerParams(dimension_semantics=("arbitrary",) * 2,
                                             vmem_limit_bytes=VMEM_LIMIT),
        name="sb_attention",
    )(qkv3, qkv3, qkv3)


def _mix_kernel(sb_ref, sgn_ref, x_ref, sbg_ref, wout_ref, ffng_ref, wr2_ref, br_ref,
                h_ref, lg_ref):
    sbn = _rms(sb_ref[...], sbg_ref[...]).astype(BF16)
    h = x_ref[...] + _dot(sbn, wout_ref[0:SB_WIDTH, :]) + _dot(sgn_ref[...], wout_ref[SB_WIDTH:, :])
    h_ref[...] = h
    hn = _rms(h, ffng_ref[...])

    hn_hi, hn_lo = _split_bf16(hn)
    both = _dot(hn_hi, wr2_ref[...])
    logits = both[:, 0:LANES] + both[:, LANES:] + _dot(hn_lo, wr2_ref[:, 0:LANES]) + br_ref[...]
    lg_ref[...] = logits.T[0:ROUTER_ROWS, :]


def _route_kernel(lg_ref, ri_ref, rw_ref, cnt_ref, count_ref):
    tr = TM_ROUTE
    i = pl.program_id(0)

    @pl.when(i == 0)
    def _():
        count_ref[...] = jnp.zeros_like(count_ref)

    neg = jnp.float32(-jnp.inf)
    row8 = lax.broadcasted_iota(jnp.int32, (SUBLANES, tr), 0)

    def top(v):
        m = jnp.max(v, axis=0, keepdims=True)
        return m, jnp.min(jnp.where(v == m, row8, SUBLANES), axis=0, keepdims=True)

    def group_rows(g):
        return lg_ref[ROUTER_LANE0 + g * EXPERTS_PER_GROUP:ROUTER_LANE0 + (g + 1) * EXPERTS_PER_GROUP, :]

    gl = jnp.where(row8 < N_GROUPS, lg_ref[0:SUBLANES, :], neg)
    gmax, gidx = top(gl)
    gweight = 1.0 / jnp.sum(jnp.exp(gl - gmax), axis=0, keepdims=True)
    el = group_rows(0)
    for g in range(1, N_GROUPS):
        el = jnp.where(gidx == g, group_rows(g), el)
    m1, i1 = top(el)
    m2, i2 = top(jnp.where(row8 == i1, neg, el))
    t21 = jnp.exp(m2 - m1)
    w1 = gweight / (1.0 + t21)
    w2 = gweight * t21 / (1.0 + t21)
    e1 = gidx * EXPERTS_PER_GROUP + i1
    e2 = gidx * EXPERTS_PER_GROUP + i2

    row_e = lax.broadcasted_iota(jnp.int32, (N_EXPERTS, tr), 0)
    sel1 = row_e == e1
    sel2 = row_e == e2
    onehot = jnp.where(sel1 | sel2, 1.0, 0.0)
    r_t = lax.broadcasted_iota(jnp.int32, (tr, tr), 0)
    c_t = lax.broadcasted_iota(jnp.int32, (tr, tr), 1)
    before = (r_t < c_t).astype(BF16)
    running = count_ref[:, 0:1] + _dot(onehot.astype(BF16), before)
    rank1 = jnp.sum(jnp.where(sel1, running, 0.0), axis=0, keepdims=True)
    rank2 = jnp.sum(jnp.where(sel2, running, 0.0), axis=0, keepdims=True)
    new_count = count_ref[:, 0:1] + jnp.sum(onehot, axis=1, keepdims=True)
    count_ref[...] = jnp.broadcast_to(new_count, count_ref.shape)
    cnt_ref[...] = jnp.broadcast_to(new_count, cnt_ref.shape)

    ri_ref[...] = jnp.where(row8 == 0, e1, jnp.where(row8 == 1, e2, jnp.where(
        row8 == 2, rank1.astype(jnp.int32), jnp.where(row8 == 3, rank2.astype(jnp.int32), 0))))
    row128 = lax.broadcasted_iota(jnp.int32, (LANES, tr), 0)
    rw_ref[...] = jnp.where(row128 == 0, w1, jnp.where(row128 == 1, w2, 0.0)).T


def _route(lg):
    n = lg.shape[1]
    return pl.pallas_call(
        _route_kernel,
        grid=(n // TM_ROUTE,),
        in_specs=[pl.BlockSpec((ROUTER_ROWS, TM_ROUTE), lambda i: (0, i))],
        out_specs=[pl.BlockSpec((SUBLANES, TM_ROUTE), lambda i: (0, i)),
                   pl.BlockSpec((TM_ROUTE, LANES), lambda i: (i, 0)),
                   pl.BlockSpec((N_EXPERTS, LANES), lambda i: (0, 0))],
        out_shape=[jax.ShapeDtypeStruct((SUBLANES, n), jnp.int32),
                   jax.ShapeDtypeStruct((n, LANES), F32),
                   jax.ShapeDtypeStruct((N_EXPERTS, LANES), F32)],
        scratch_shapes=[pltpu.VMEM((N_EXPERTS, LANES), F32)],
        compiler_params=pltpu.CompilerParams(dimension_semantics=("arbitrary",),
                                             vmem_limit_bytes=VMEM_LIMIT),
        name="route",
    )(lg)


def _mix(sb, sgn, x2, sb_g, w_out_b, ffn_g, wr2, br):
    n = x2.shape[0]
    row = lambda i: (i, 0)
    const = lambda i: (0, 0)
    return pl.pallas_call(
        _mix_kernel,
        grid=(n // TM_MIX,),
        in_specs=[pl.BlockSpec((TM_MIX, SB_WIDTH), row),
                  pl.BlockSpec((TM_MIX, SG_WIDTH), row),
                  pl.BlockSpec((TM_MIX, D_MODEL), row),
                  pl.BlockSpec((1, SB_WIDTH), const),
                  pl.BlockSpec((D_MODEL, D_MODEL), const),
                  pl.BlockSpec((1, D_MODEL), const),
                  pl.BlockSpec((D_MODEL, 2 * LANES), const),
                  pl.BlockSpec((1, LANES), const)],
        out_specs=[pl.BlockSpec((TM_MIX, D_MODEL), row),
                   pl.BlockSpec((ROUTER_ROWS, TM_MIX), lambda i: (0, i))],
        out_shape=[jax.ShapeDtypeStruct((n, D_MODEL), F32),
                   jax.ShapeDtypeStruct((ROUTER_ROWS, n), F32)],
        compiler_params=pltpu.CompilerParams(dimension_semantics=("arbitrary",),
                                             vmem_limit_bytes=VMEM_LIMIT),
        name="mix_router",
    )(sb, sgn, x2, sb_g, w_out_b, ffn_g, wr2, br)


_PAD_BITS = tuple(1 << b for b in reversed(range(EXPERT_CHUNK.bit_length() - 1)))


def _dispatch_kernel(dest_ref, pad_start_ref, pad_count_ref, used_ref, h_ref, g_ref, xs_ref,
                     hn_ref, zeros_ref, sem, zsem):
    tm = TM_DISPATCH
    i = pl.program_id(0)
    n_steps = pl.num_programs(0) - 1
    n = n_steps * tm
    base = (i - 1) * tm
    prev = hn_ref.at[lax.rem(i + 1, 2)]
    n_chunks = xs_ref.shape[0] // (EXPERT_CHUNK * ROW_TILE)

    def pad_copies(do):
        for e in range(N_EXPERTS):
            start = pad_start_ref[e]
            count = pad_count_ref[e]
            for bit in _PAD_BITS:
                @pl.when((count & bit) != 0)
                def _(start=start, bit=bit):
                    do(pltpu.make_async_copy(_token_rows(zeros_ref, 0, bit),
                                             _token_rows(xs_ref, start, bit), zsem))
                start = start + (count & bit)
        for k in range(N_EXPERTS):
            chunk = used_ref[0] + k

            @pl.when(chunk < n_chunks)
            def _(chunk=chunk):
                do(pltpu.make_async_copy(zeros_ref, _token_rows(xs_ref, chunk * EXPERT_CHUNK, EXPERT_CHUNK),
                                         zsem))

    @pl.when(i == 0)
    def _():
        zeros_ref[...] = jnp.zeros_like(zeros_ref)
        pad_copies(lambda cp: cp.start())

    @pl.when(i > 0)
    def _():
        def body(r, c):
            src = _token_rows(prev, r, 1)
            for s in range(2):
                pltpu.make_async_copy(src, _token_rows(xs_ref, dest_ref[s * n + base + r], 1),
                                      sem).start(priority=s)
            return c

        lax.fori_loop(0, tm, body, 0, unroll=8)

    @pl.when(i < n_steps)
    def _():
        _rows_to_tiles(hn_ref.at[lax.rem(i, 2)], _rms(h_ref[...], g_ref[...]))

    @pl.when(i > 0)
    def _():
        for _ in range(2):
            pltpu.make_async_copy(prev, _token_rows(xs_ref, 0, tm), sem).wait()

    @pl.when(i == n_steps)
    def _():
        pad_copies(lambda cp: cp.wait())


def _dispatch(dest, pad_start, pad_count, used_chunks, h, ffn_g, n_rows):
    n_steps = h.shape[0] // TM_DISPATCH
    return pl.pallas_call(
        _dispatch_kernel,
        grid_spec=pltpu.PrefetchScalarGridSpec(
            num_scalar_prefetch=4,
            grid=(n_steps + 1,),
            in_specs=[pl.BlockSpec((TM_DISPATCH, D_MODEL), lambda i, *_: (jnp.minimum(i, n_steps - 1), 0)),
                      pl.BlockSpec((1, D_MODEL), lambda i, *_: (0, 0))],
            out_specs=pl.BlockSpec(memory_space=pl.ANY),
            scratch_shapes=[pltpu.VMEM((2, TM_DISPATCH * ROW_TILE, LANES), F32),
                            pltpu.VMEM((EXPERT_CHUNK * ROW_TILE, LANES), F32),
                            pltpu.SemaphoreType.DMA, pltpu.SemaphoreType.DMA]),
        out_shape=jax.ShapeDtypeStruct((n_rows * ROW_TILE, LANES), F32),
        compiler_params=pltpu.CompilerParams(dimension_semantics=("arbitrary",),
                                             vmem_limit_bytes=VMEM_LIMIT),
        name="dispatch",
    )(dest, pad_start, pad_count, used_chunks, h, ffn_g)


X_SLOTS = 3
TILE_CHUNKS = TM_EXPERT // EXPERT_CHUNK
W_SLOTS = 3


def _expert_kernel(tiles_ref, chunk0_ref, chunks_ref, nt_ref, used_ref, xs_ref, wg_ref, wu_ref, wd_ref,
                   zeros_ref, ys_ref, x_buf, y_buf, sg_buf, su_buf, sd_buf, wgb, wub, wdb, state,
                   w_sems, x_sems, y_sems, zsem):
    t = pl.program_id(0)
    last = pl.num_programs(0) - 1
    nt = nt_ref[0]
    n_chunks = ys_ref.shape[0] // (EXPERT_CHUNK * ROW_TILE)

    def tile_copies(tile, do, out):
        for c in range(TILE_CHUNKS):
            @pl.when(c < chunks_ref[tile])
            def _(c=c):
                first = (chunk0_ref[tile] + c) * EXPERT_CHUNK
                if out:
                    slot = lax.rem(tile, 2)
                    do(pltpu.make_async_copy(_token_rows(y_buf.at[slot], c * EXPERT_CHUNK, EXPERT_CHUNK),
                                             _token_rows(ys_ref, first, EXPERT_CHUNK), y_sems.at[slot]))
                else:
                    slot = lax.rem(tile, X_SLOTS)
                    do(pltpu.make_async_copy(_token_rows(xs_ref, first, EXPERT_CHUNK),
                                             _token_rows(x_buf.at[slot], c * EXPERT_CHUNK, EXPERT_CHUNK),
                                             x_sems.at[slot]))

    start = lambda cp: cp.start()
    wait = lambda cp: cp.wait()

    def tail_copies(do):
        for k in range(N_EXPERTS):
            chunk = used_ref[0] + k

            @pl.when(chunk < n_chunks)
            def _(chunk=chunk):
                do(pltpu.make_async_copy(zeros_ref, _token_rows(ys_ref, chunk * EXPERT_CHUNK, EXPERT_CHUNK),
                                         zsem))

    def weight_copies(e, slot):
        return (pltpu.make_async_copy(wg_ref.at[e], sg_buf.at[slot], w_sems.at[slot]),
                pltpu.make_async_copy(wu_ref.at[e], su_buf.at[slot], w_sems.at[slot]),
                pltpu.make_async_copy(wd_ref.at[e], sd_buf.at[slot], w_sems.at[slot]))

    def next_with_rows(e):
        return lax.while_loop(lambda k: (k < N_EXPERTS) & (tiles_ref[jnp.minimum(k, N_EXPERTS - 1)] == 0),
                              lambda k: k + 1, e + 1)

    @pl.when(t == 0)
    def _():
        first = next_with_rows(jnp.int32(-1))
        second = next_with_rows(first)
        state[0] = jnp.int32(-1)
        state[1] = jnp.int32(0)
        state[2] = jnp.int32(W_SLOTS - 1)
        state[3] = first
        state[4] = second
        for cp in weight_copies(first, 0):
            cp.start()

        @pl.when(second < N_EXPERTS)
        def _():
            for cp in weight_copies(second, 1):
                cp.start()

        tile_copies(0, start, False)

        @pl.when(nt > 1)
        def _():
            tile_copies(1, start, False)

        tail_copies(start)

    @pl.when(t + 2 < nt)
    def _():
        tile_copies(t + 2, start, False)

    @pl.when(t < nt)
    def _():
        @pl.when(state[1] == 0)
        def _():
            e = state[3]
            nxt = state[4]
            slot = lax.rem(state[2] + 1, W_SLOTS)
            after_next = next_with_rows(nxt)
            state[0] = e
            state[1] = tiles_ref[e]
            state[2] = slot
            state[3] = nxt
            state[4] = after_next
            for cp in weight_copies(e, slot):
                cp.wait()

            @pl.when(after_next < N_EXPERTS)
            def _():
                for cp in weight_copies(after_next, lax.rem(slot + 2, W_SLOTS)):
                    cp.start()

            wgb[...] = sg_buf[slot].astype(BF16)
            wub[...] = su_buf[slot].astype(BF16)
            wdb[...] = sd_buf[slot].astype(BF16)

        state[1] = state[1] - 1
        tile_copies(t, wait, False)

        @pl.when(t >= 2)
        def _():
            tile_copies(t - 2, wait, True)

        for n_chunks_here in range(1, TILE_CHUNKS + 1):
            @pl.when(chunks_ref[t] == n_chunks_here)
            def _(m=n_chunks_here * EXPERT_CHUNK):
                x = _tiles_to_rows(x_buf.at[lax.rem(t, X_SLOTS)], m).astype(BF16)
                g = _dot(x, wgb[...])
                u = _dot(x, wub[...])
                hidden = (g * jax.nn.sigmoid(g)) * u
                _rows_to_tiles(y_buf.at[lax.rem(t, 2)], _dot(hidden.astype(BF16), wdb[...]))

        tile_copies(t, start, True)

    @pl.when(t == last)
    def _():
        for back in (2, 1):
            @pl.when(nt >= back)
            def _(back=back):
                tile_copies(nt - back, wait, True)

        tail_copies(wait)


def _experts(tiles, chunk0, chunks, n_tiles, used_chunks, xs, wg, wu, wd):
    any_spec = pl.BlockSpec(memory_space=pl.ANY)
    zeros = jnp.zeros((EXPERT_CHUNK * ROW_TILE, LANES), F32)
    return pl.pallas_call(
        _expert_kernel,
        grid_spec=pltpu.PrefetchScalarGridSpec(
            num_scalar_prefetch=5,
            grid=(chunks.shape[0],),
            in_specs=[any_spec, any_spec, any_spec, any_spec, any_spec],
            out_specs=any_spec,
            scratch_shapes=[pltpu.VMEM((X_SLOTS, TM_EXPERT * ROW_TILE, LANES), F32),
                            pltpu.VMEM((2, TM_EXPERT * ROW_TILE, LANES), F32),
                            pltpu.VMEM((W_SLOTS, D_MODEL, D_EXPERT), F32),
                            pltpu.VMEM((W_SLOTS, D_MODEL, D_EXPERT), F32),
                            pltpu.VMEM((W_SLOTS, D_EXPERT, D_MODEL), F32),
                            pltpu.VMEM((D_MODEL, D_EXPERT), BF16),
                            pltpu.VMEM((D_MODEL, D_EXPERT), BF16),
                            pltpu.VMEM((D_EXPERT, D_MODEL), BF16),
                            pltpu.SMEM((5,), jnp.int32),
                            pltpu.SemaphoreType.DMA((W_SLOTS,)),
                            pltpu.SemaphoreType.DMA((X_SLOTS,)),
                            pltpu.SemaphoreType.DMA((2,)),
                            pltpu.SemaphoreType.DMA]),
        out_shape=jax.ShapeDtypeStruct(xs.shape, F32),
        compiler_params=pltpu.CompilerParams(dimension_semantics=("arbitrary",),
                                             vmem_limit_bytes=VMEM_LIMIT),
        name="expert_mlp",
    )(tiles, chunk0, chunks, n_tiles, used_chunks, xs, wg, wu, wd, zeros)


def _combine_kernel(dest_ref, h_ref, rw_ref, fg_ref, y_ref, o_ref, buf, sems):
    tm = TM_COMBINE
    i = pl.program_id(0)
    n_steps = pl.num_programs(0)
    n = n_steps * tm
    cur = i % 2

    def fetch(step, half):
        def body(r, c):
            for s in range(2):
                pltpu.make_async_copy(_token_rows(y_ref, dest_ref[s * n + step * tm + r], 1),
                                      _token_rows(buf.at[half, s], r, 1),
                                      sems.at[half]).start(priority=s)
            return c

        lax.fori_loop(0, tm, body, 0, unroll=8)

    @pl.when(i == 0)
    def _():
        fetch(0, 0)

    @pl.when(i + 1 < n_steps)
    def _():
        fetch(i + 1, 1 - cur)

    for s in range(2):
        pltpu.make_async_copy(_token_rows(y_ref, 0, tm), buf.at[cur, s], sems.at[cur]).wait()
    rw = rw_ref[...]
    out = (h_ref[...] + rw[:, 0:1] * _tiles_to_rows(buf.at[cur, 0], tm)
           + rw[:, 1:2] * _tiles_to_rows(buf.at[cur, 1], tm))
    o_ref[...] = _rms(out, fg_ref[...])


def _combine(dest, h, rw, final_g, ys):
    n = h.shape[0]
    return pl.pallas_call(
        _combine_kernel,
        grid_spec=pltpu.PrefetchScalarGridSpec(
            num_scalar_prefetch=1,
            grid=(n // TM_COMBINE,),
            in_specs=[pl.BlockSpec((TM_COMBINE, D_MODEL), lambda i, d: (i, 0)),
                      pl.BlockSpec((TM_COMBINE, LANES), lambda i, d: (i, 0)),
                      pl.BlockSpec((1, D_MODEL), lambda i, d: (0, 0)),
                      pl.BlockSpec(memory_space=pl.ANY)],
            out_specs=pl.BlockSpec((TM_COMBINE, D_MODEL), lambda i, d: (i, 0)),
            scratch_shapes=[pltpu.VMEM((2, 2, TM_COMBINE * ROW_TILE, LANES), F32),
                            pltpu.SemaphoreType.DMA((2,))]),
        out_shape=jax.ShapeDtypeStruct((n, D_MODEL), F32),
        compiler_params=pltpu.CompilerParams(dimension_semantics=("arbitrary",),
                                             vmem_limit_bytes=VMEM_LIMIT),
        name="combine",
    )(dest, h, rw, final_g, ys)


def _schedule(counts, max_tiles):
    chunks = (counts + EXPERT_CHUNK - 1) // EXPERT_CHUNK
    chunk_end = jnp.cumsum(chunks)
    chunk_start = chunk_end - chunks
    tiles = (chunks + TILE_CHUNKS - 1) // TILE_CHUNKS
    tile_end = jnp.cumsum(tiles)
    tile = jnp.arange(max_tiles, dtype=jnp.int32)
    owner = jnp.sum(tile[:, None] >= tile_end[None, :], axis=1)
    is_owner = owner[:, None] == jnp.arange(N_EXPERTS, dtype=jnp.int32)[None, :]
    of_owner = lambda v: jnp.sum(jnp.where(is_owner, v[None, :], 0), axis=1)
    done = (tile - of_owner(tile_end - tiles)) * TILE_CHUNKS
    tile_chunk0 = (of_owner(chunk_start) + done).astype(jnp.int32)
    tile_chunks = jnp.clip(of_owner(chunks) - done, 0, TILE_CHUNKS).astype(jnp.int32)
    return tiles, chunk_start * EXPERT_CHUNK, tile_chunk0, tile_chunks, tile_end[-1:], chunk_end[-1:]


def _layer(x, attn_g, w_in, sg_g, w_sp, b_sp, sb_g, sg_out_g, w_out, ffn_g,
           w_rg, b_rg, w_re, b_re, w_gate, w_up, w_down):
    batch, seq, _ = x.shape
    n = batch * seq
    x2 = x.reshape(n, D_MODEL)
    row = lambda v: v.reshape(1, -1)

    bsp_full = jnp.repeat(b_sp.T, HEAD_DIM, axis=1)
    qkv, sgn = _inproj(x2, row(attn_g), w_in.astype(BF16), row(sg_g), w_sp, bsp_full, row(sg_out_g))
    sb = _attention(qkv, batch, seq).reshape(n, SB_WIDTH)

    pad_lanes = lambda v, width: jnp.pad(v, [(0, 0)] * (v.ndim - 1) + [(0, width - v.shape[-1])])
    w_r = jnp.concatenate([pad_lanes(w_rg, ROUTER_LANE0),
                           jnp.transpose(w_re, (1, 0, 2)).reshape(D_MODEL, N_EXPERTS)], axis=1)
    w_r = pad_lanes(w_r, LANES)
    wr_hi = w_r.astype(BF16)
    wr_lo = (w_r - wr_hi.astype(F32)).astype(BF16)
    wr2 = jnp.concatenate([wr_hi, wr_lo], axis=1)
    b_r = pad_lanes(jnp.concatenate([pad_lanes(b_rg, ROUTER_LANE0), b_re.reshape(-1)]), LANES)

    h, lg = _mix(sb, sgn, x2, row(sb_g), w_out.astype(BF16), row(ffn_g), wr2, row(b_r))
    ri, rw, cnt = _route(lg)

    counts = cnt[:, 0].astype(jnp.int32)
    n_rows = 2 * n + N_EXPERTS * EXPERT_CHUNK
    tiles, offsets, tile_chunk0, tile_chunks, n_tiles, used_chunks = _schedule(
        counts, 2 * n // TM_EXPERT + N_EXPERTS)
    expert, rank = ri[0:2], ri[2:4]
    is_e = expert[None] == jnp.arange(N_EXPERTS, dtype=jnp.int32)[:, None, None]
    dest = (jnp.sum(jnp.where(is_e, offsets[:, None, None], 0), axis=0) + rank).reshape(-1)
    pad_start = offsets + counts
    pad_count = (-counts) % EXPERT_CHUNK

    xs = _dispatch(dest, pad_start, pad_count, used_chunks, h, row(ffn_g), n_rows)
    ys = _experts(tiles, tile_chunk0, tile_chunks, n_tiles, used_chunks, xs,
                  w_gate.reshape(N_EXPERTS, D_MODEL, D_EXPERT),
                  w_up.reshape(N_EXPERTS, D_MODEL, D_EXPERT),
                  w_down.reshape(N_EXPERTS, D_EXPERT, D_MODEL))
    return dest, h, rw, ys


def kernel(x, attn_norm_g, w_in, sg_norm_g, w_spatial, b_spatial, sb_out_norm_g, sg_out_norm_g,
           w_out, ffn_norm_g, w_router_group, b_router_group, w_router_expert, b_router_expert,
           w_gate, w_up, w_down, final_norm_g):
    assert attn_norm_g.shape[0] == 1, "single-layer problem"
    batch, seq, _ = x.shape
    dest, h, rw, ys = _layer(x, attn_norm_g[0], w_in[0], sg_norm_g[0], w_spatial[0], b_spatial[0],
                             sb_out_norm_g[0], sg_out_norm_g[0], w_out[0], ffn_norm_g[0],
                             w_router_group[0], b_router_group[0], w_router_expert[0],
                             b_router_expert[0], w_gate[0], w_up[0], w_down[0])
    out = _combine(dest, h, rw, final_norm_g.reshape(1, -1), ys)
    return out.reshape(batch, seq, D_MODEL)
```

```python
import functools
import math

import jax
import jax.numpy as jnp
from jax import lax
from jax.experimental import pallas as pl
from jax.experimental.pallas import tpu as pltpu

D_MODEL = 1024
HEAD_DIM = 64
SB_WIDTH = 512
SG_WIDTH = 512
SG_HEADS = 8
D_IN = 3 * SB_WIDTH + 2 * SG_WIDTH
CHUNK = 128
N_GROUPS = 4
EXPERTS_PER_GROUP = 8
N_EXPERTS = N_GROUPS * EXPERTS_PER_GROUP
D_EXPERT = 512
EPS = 1e-6
F32_EXP_UNDERFLOW = 110.0

LANES = 128
SUBLANES = 8
ROW_TILE = D_MODEL // LANES
assert ROW_TILE == SUBLANES
HEAD_PAIR = 2 * HEAD_DIM
ROUTER_LANE0 = SUBLANES
ROUTER_ROWS = ROUTER_LANE0 + N_EXPERTS
assert EXPERTS_PER_GROUP == SUBLANES and N_GROUPS <= ROUTER_LANE0

TM_PROJ = 1024
TQ_ATTN = 256
ATTN_BLOCKS_PER_STEP = 2
ATTN_TOP_ROWS = (160, 176)
TM_MIX = 1024
TM_ROUTE = 1024
TM_DISPATCH = 1024
TM_EXPERT = 512
EXPERT_CHUNK = 128
TM_COMBINE = 512
VMEM_LIMIT = 48 * 1024 * 1024

F32 = jnp.float32
BF16 = jnp.bfloat16


def _rms(x, g):
    return x * lax.rsqrt(jnp.mean(x * x, axis=-1, keepdims=True) + EPS) * g


def _gelu(x):
    c = math.sqrt(2.0 / math.pi)
    return x * (0.5 * (1.0 + jnp.tanh(c * (x + 0.044715 * (x * x * x)))))


def _softplus(z):
    return jnp.maximum(z, 0.0) + jnp.log(1.0 + jnp.exp(-jnp.abs(z)))


def _dot(a, b):
    return jnp.dot(a, b, preferred_element_type=F32)


def _rows_to_tiles(ref, x):
    m = x.shape[0]
    for k in range(ROW_TILE):
        ref[pl.ds(k, m, stride=ROW_TILE), :] = x[:, k * LANES:(k + 1) * LANES]


def _tiles_to_rows(ref, m):
    return jnp.concatenate([ref[pl.ds(k, m, stride=ROW_TILE), :] for k in range(ROW_TILE)], axis=1)


def _token_rows(ref, first_token, n_tokens):
    return ref.at[pl.ds(pl.multiple_of(first_token * ROW_TILE, ROW_TILE), n_tokens * ROW_TILE)]


def _split_bf16(x):
    hi = x.astype(BF16)
    lo = (x - hi.astype(F32)).astype(BF16)
    return hi, lo


def _inproj_kernel(x_ref, g_ref, w_ref, sgg_ref, wsp_ref, bsp_ref, sgog_ref, qkv_ref, sgn_ref,
                   gu_ref, vgn_ref, sg_ref):
    tm = TM_PROJ
    hb = _rms(x_ref[...], g_ref[...]).astype(BF16)
    gv = _gelu(_dot(hb, w_ref[:, 3 * SB_WIDTH + SG_WIDTH:D_IN]))
    vgn_ref[...] = _rms(gv, sgg_ref[...]).astype(BF16)
    gu_ref[...] = _gelu(_dot(hb, w_ref[:, 3 * SB_WIDTH:3 * SB_WIDTH + SG_WIDTH]))
    q = _dot(hb, w_ref[:, 0:SB_WIDTH]) * (1.0 / math.sqrt(HEAD_DIM))
    qkv_ref[:, 0:SB_WIDTH] = q.astype(BF16)
    qkv_ref[:, SB_WIDTH:2 * SB_WIDTH] = _dot(hb, w_ref[:, SB_WIDTH:2 * SB_WIDTH]).astype(BF16)

    lane = lax.broadcasted_iota(jnp.int32, (1, LANES), 1)
    first = lane < HEAD_DIM
    zero = jnp.zeros((), BF16)
    r_c = lax.broadcasted_iota(jnp.int32, (CHUNK, CHUNK), 0)
    c_c = lax.broadcasted_iota(jnp.int32, (CHUNK, CHUNK), 1)
    tril = r_c >= c_c
    n_pairs = SG_WIDTH // HEAD_PAIR
    w_pairs = []
    for p in range(n_pairs):
        w0 = jnp.where(tril, wsp_ref[2 * p], 0.0).astype(BF16)
        w1 = jnp.where(tril, wsp_ref[2 * p + 1], 0.0).astype(BF16)
        w_pairs.append(jnp.concatenate([w0, w1], axis=1))
    bsp = bsp_ref[...]
    for c in range(tm // CHUNK):
        rows = slice(c * CHUNK, (c + 1) * CHUNK)
        for p in range(n_pairs):
            cols = slice(p * HEAD_PAIR, (p + 1) * HEAD_PAIR)
            vg = vgn_ref[rows, cols]
            rhs = jnp.concatenate([jnp.where(first, vg, zero), jnp.where(first, zero, vg)], axis=0)
            mixed = _dot(w_pairs[p], rhs) + bsp[:, cols]
            sg_ref[rows, cols] = gu_ref[rows, cols] * mixed
    qkv_ref[:, 2 * SB_WIDTH:3 * SB_WIDTH] = _dot(hb, w_ref[:, 2 * SB_WIDTH:3 * SB_WIDTH]).astype(BF16)
    sgn_ref[...] = _rms(sg_ref[...], sgog_ref[...]).astype(BF16)


def _inproj(x2, attn_g, w_in_b, sg_g, wsp, bsp_full, sg_out_g):
    n = x2.shape[0]
    row = lambda i: (i, 0)
    const = lambda i: (0, 0)
    return pl.pallas_call(
        _inproj_kernel,
        grid=(n // TM_PROJ,),
        in_specs=[pl.BlockSpec((TM_PROJ, D_MODEL), row),
                  pl.BlockSpec((1, D_MODEL), const),
                  pl.BlockSpec((D_MODEL, D_IN), const),
                  pl.BlockSpec((1, SG_WIDTH), const),
                  pl.BlockSpec((SG_HEADS, CHUNK, CHUNK), lambda i: (0, 0, 0)),
                  pl.BlockSpec((CHUNK, SG_WIDTH), const),
                  pl.BlockSpec((1, SG_WIDTH), const)],
        out_specs=[pl.BlockSpec((TM_PROJ, 3 * SB_WIDTH), row),
                   pl.BlockSpec((TM_PROJ, SG_WIDTH), row)],
        out_shape=[jax.ShapeDtypeStruct((n, 3 * SB_WIDTH), BF16),
                   jax.ShapeDtypeStruct((n, SG_WIDTH), BF16)],
        scratch_shapes=[pltpu.VMEM((TM_PROJ, SG_WIDTH), F32),
                        pltpu.VMEM((TM_PROJ, SG_WIDTH), BF16),
                        pltpu.VMEM((TM_PROJ, SG_WIDTH), F32)],
        compiler_params=pltpu.CompilerParams(dimension_semantics=("arbitrary",),
                                             vmem_limit_bytes=VMEM_LIMIT),
        name="inproj",
    )(x2, attn_g, w_in_b, sg_g, wsp, bsp_full, sg_out_g)


def _attn_kernel(q_ref, k_ref, v_ref, o_ref, q2_ref, carry_ref):
    t = TQ_ATTN
    n_pairs = SB_WIDTH // HEAD_PAIR
    lane = lax.broadcasted_iota(jnp.int32, (1, HEAD_PAIR), 1)
    head_lanes = (lane < HEAD_DIM, lane >= HEAD_DIM)
    zero = jnp.zeros((), BF16)
    r_idx = lax.broadcasted_iota(jnp.int32, (t, t), 0)
    c_idx = lax.broadcasted_iota(jnp.int32, (t, t), 1)
    suffix = (r_idx > c_idx).astype(BF16)
    suffix2 = jnp.concatenate([suffix, suffix], axis=0)
    causal = c_idx < r_idx

    def one_query_block(sub, c):
        qi = pl.program_id(1) * ATTN_BLOCKS_PER_STEP + sub
        row0 = pl.multiple_of(sub * t, t)
        for p in range(n_pairs):
            qp = q_ref[0, pl.ds(row0, t), p * HEAD_PAIR:(p + 1) * HEAD_PAIR]
            for h in range(2):
                q2_ref[(2 * p + h) * t:(2 * p + h + 1) * t, :] = jnp.where(head_lanes[h], qp, zero)
        o_ref[0, pl.ds(row0, t), :] = jnp.zeros((t, SB_WIDTH), F32)
        carry_ref[...] = jnp.zeros_like(carry_ref)

        def block(j, diag, m):
            start = pl.multiple_of(j * t, t)
            mask2 = jnp.concatenate([causal, causal], axis=0) if diag else None
            st = [dict() for _ in range(n_pairs)]

            def head_rows(p):
                return [slice((2 * p + h) * t, (2 * p + h) * t + m) for h in range(2)]

            def scores(p):
                d = st[p]
                d["cols"] = slice(p * HEAD_PAIR, (p + 1) * HEAD_PAIR)
                kb = k_ref[0, pl.ds(start, t), d["cols"]]
                q2 = jnp.concatenate([q2_ref[r, :] for r in head_rows(p)], axis=0)
                z = lax.dot_general(q2, kb, (((1,), (1,)), ((), ())),
                                    preferred_element_type=F32)
                sp = _softplus(z)
                nl = jnp.where(mask2, sp, 0.0) if diag else sp
                hi, lo = _split_bf16(nl)
                d["hl"] = jnp.concatenate([hi, lo], axis=1)
                d["log_beta"] = z - sp
                d["nl0"] = nl[:, 0:1]

            def weights(p):
                d = st[p]
                hl = d["hl"]
                after = jnp.concatenate([_dot(hl[0:m], suffix2), _dot(hl[m:2 * m], suffix2)], axis=0)
                carry = jnp.concatenate([carry_ref[r, :] for r in head_rows(p)], axis=0)
                a = jnp.exp(d["log_beta"] - after - carry)
                if diag:
                    a = jnp.where(mask2, a, 0.0)
                a = a.astype(BF16)
                d["a2"] = jnp.concatenate([a[0:m], a[m:2 * m]], axis=1)
                new_carry = carry + after[:, 0:1] + d["nl0"]
                for h, r in enumerate(head_rows(p)):
                    carry_ref[r, :] = new_carry[h * m:(h + 1) * m]

            def values(p):
                d = st[p]
                vb = v_ref[0, pl.ds(start, t), d["cols"]]
                v2 = jnp.concatenate([jnp.where(head_lanes[0], vb, zero),
                                      jnp.where(head_lanes[1], vb, zero)], axis=0)
                o_ref[0, pl.ds(row0, m), d["cols"]] += _dot(d["a2"], v2)

            for step in range(n_pairs + 2):
                if step < n_pairs:
                    scores(step)
                if 0 <= step - 1 < n_pairs:
                    weights(step - 1)
                if 0 <= step - 2 < n_pairs:
                    values(step - 2)

        def flags():
            bounds = (0,) + ATTN_TOP_ROWS + (t,)
            lowest = [jnp.min(jnp.concatenate([carry_ref[hh * t + lo:hh * t + hi, :] for hh in range(2 * n_pairs)],
                                              axis=0))
                      for lo, hi in zip(bounds[:-1], bounds[1:])]
            below = [functools.reduce(jnp.minimum, lowest[k:]) for k in range(len(lowest))]
            return (below[0] < F32_EXP_UNDERFLOW,) + tuple(b >= F32_EXP_UNDERFLOW for b in below[1:])

        block(qi, True, t)

        def body(state):
            it, _, *done = state
            j = qi - 1 - it
            for k, m in enumerate(ATTN_TOP_ROWS + (t,)):
                use = done[k] if k < len(done) else True
                if k > 0:
                    use = jnp.logical_and(use, jnp.logical_not(done[k - 1]))

                @pl.when(use)
                def _(m=m):
                    block(j, False, m)

            return (it + 1,) + flags()

        lax.while_loop(lambda s: (s[0] < qi) & s[1], body, (jnp.int32(0),) + flags())
        return c

    lax.fori_loop(0, ATTN_BLOCKS_PER_STEP, one_query_block, 0)


def _attention(qkv, batch, seq):
    qkv3 = qkv.reshape(batch, seq, 3 * SB_WIDTH)
    n_heads = SB_WIDTH // HEAD_DIM
    return pl.pallas_call(
        _attn_kernel,
        grid=(batch, seq // (ATTN_BLOCKS_PER_STEP * TQ_ATTN)),
        in_specs=[pl.BlockSpec((1, ATTN_BLOCKS_PER_STEP * TQ_ATTN, SB_WIDTH), lambda b, i: (b, i, 0)),
                  pl.BlockSpec((1, seq, SB_WIDTH), lambda b, i: (b, 0, 1)),
                  pl.BlockSpec((1, seq, SB_WIDTH), lambda b, i: (b, 0, 2))],
        out_specs=pl.BlockSpec((1, ATTN_BLOCKS_PER_STEP * TQ_ATTN, SB_WIDTH), lambda b, i: (b, i, 0)),
        out_shape=jax.ShapeDtypeStruct((batch, seq, SB_WIDTH), F32),
        scratch_shapes=[pltpu.VMEM((n_heads * TQ_ATTN, HEAD_PAIR), BF16),
                        pltpu.VMEM((n_heads * TQ_ATTN, 1), F32)],
        compiler_params=pltpu.CompilerParams(dimension_semantics=("arbitrary",) * 2,
                                             vmem_limit_bytes=VMEM_LIMIT),
        name="sb_attention",
    )(qkv3, qkv3, qkv3)


def _mix_kernel(sb_ref, sgn_ref, x_ref, sbg_ref, wout_ref, ffng_ref, wr2_ref, br_ref,
                h_ref, lg_ref):
    sbn = _rms(sb_ref[...], sbg_ref[...]).astype(BF16)
    h = x_ref[...] + _dot(sbn, wout_ref[0:SB_WIDTH, :]) + _dot(sgn_ref[...], wout_ref[SB_WIDTH:, :])
    h_ref[...] = h
    hn = _rms(h, ffng_ref[...])

    hn_hi, hn_lo = _split_bf16(hn)
    both = _dot(hn_hi, wr2_ref[...])
    logits = both[:, 0:LANES] + both[:, LANES:] + _dot(hn_lo, wr2_ref[:, 0:LANES]) + br_ref[...]
    lg_ref[...] = logits.T[0:ROUTER_ROWS, :]


def _route_kernel(lg_ref, ri_ref, rw_ref, cnt_ref, count_ref):
    tr = TM_ROUTE
    i = pl.program_id(0)

    @pl.when(i == 0)
    def _():
        count_ref[...] = jnp.zeros_like(count_ref)

    neg = jnp.float32(-jnp.inf)
    row8 = lax.broadcasted_iota(jnp.int32, (SUBLANES, tr), 0)

    def top(v):
        m = jnp.max(v, axis=0, keepdims=True)
        return m, jnp.min(jnp.where(v == m, row8, SUBLANES), axis=0, keepdims=True)

    def group_rows(g):
        return lg_ref[ROUTER_LANE0 + g * EXPERTS_PER_GROUP:ROUTER_LANE0 + (g + 1) * EXPERTS_PER_GROUP, :]

    gl = jnp.where(row8 < N_GROUPS, lg_ref[0:SUBLANES, :], neg)
    gmax, gidx = top(gl)
    gweight = 1.0 / jnp.sum(jnp.exp(gl - gmax), axis=0, keepdims=True)
    el = group_rows(0)
    for g in range(1, N_GROUPS):
        el = jnp.where(gidx == g, group_rows(g), el)
    m1, i1 = top(el)
    m2, i2 = top(jnp.where(row8 == i1, neg, el))
    t21 = jnp.exp(m2 - m1)
    w1 = gweight / (1.0 + t21)
    w2 = gweight * t21 / (1.0 + t21)
    e1 = gidx * EXPERTS_PER_GROUP + i1
    e2 = gidx * EXPERTS_PER_GROUP + i2

    row_e = lax.broadcasted_iota(jnp.int32, (N_EXPERTS, tr), 0)
    sel1 = row_e == e1
    sel2 = row_e == e2
    onehot = jnp.where(sel1 | sel2, 1.0, 0.0)
    r_t = lax.broadcasted_iota(jnp.int32, (tr, tr), 0)
    c_t = lax.broadcasted_iota(jnp.int32, (tr, tr), 1)
    before = (r_t < c_t).astype(BF16)
    running = count_ref[:, 0:1] + _dot(onehot.astype(BF16), before)
    rank1 = jnp.sum(jnp.where(sel1, running, 0.0), axis=0, keepdims=True)
    rank2 = jnp.sum(jnp.where(sel2, running, 0.0), axis=0, keepdims=True)
    new_count = count_ref[:, 0:1] + jnp.sum(onehot, axis=1, keepdims=True)
    count_ref[...] = jnp.broadcast_to(new_count, count_ref.shape)
    cnt_ref[...] = jnp.broadcast_to(new_count, cnt_ref.shape)

    ri_ref[...] = jnp.where(row8 == 0, e1, jnp.where(row8 == 1, e2, jnp.where(
        row8 == 2, rank1.astype(jnp.int32), jnp.where(row8 == 3, rank2.astype(jnp.int32), 0))))
    row128 = lax.broadcasted_iota(jnp.int32, (LANES, tr), 0)
    rw_ref[...] = jnp.where(row128 == 0, w1, jnp.where(row128 == 1, w2, 0.0)).T


def _route(lg):
    n = lg.shape[1]
    return pl.pallas_call(
        _route_kernel,
        grid=(n // TM_ROUTE,),
        in_specs=[pl.BlockSpec((ROUTER_ROWS, TM_ROUTE), lambda i: (0, i))],
        out_specs=[pl.BlockSpec((SUBLANES, TM_ROUTE), lambda i: (0, i)),
                   pl.BlockSpec((TM_ROUTE, LANES), lambda i: (i, 0)),
                   pl.BlockSpec((N_EXPERTS, LANES), lambda i: (0, 0))],
        out_shape=[jax.ShapeDtypeStruct((SUBLANES, n), jnp.int32),
                   jax.ShapeDtypeStruct((n, LANES), F32),
                   jax.ShapeDtypeStruct((N_EXPERTS, LANES), F32)],
        scratch_shapes=[pltpu.VMEM((N_EXPERTS, LANES), F32)],
        compiler_params=pltpu.CompilerParams(dimension_semantics=("arbitrary",),
                                             vmem_limit_bytes=VMEM_LIMIT),
        name="route",
    )(lg)


def _mix(sb, sgn, x2, sb_g, w_out_b, ffn_g, wr2, br):
    n = x2.shape[0]
    row = lambda i: (i, 0)
    const = lambda i: (0, 0)
    return pl.pallas_call(
        _mix_kernel,
        grid=(n // TM_MIX,),
        in_specs=[pl.BlockSpec((TM_MIX, SB_WIDTH), row),
                  pl.BlockSpec((TM_MIX, SG_WIDTH), row),
                  pl.BlockSpec((TM_MIX, D_MODEL), row),
                  pl.BlockSpec((1, SB_WIDTH), const),
                  pl.BlockSpec((D_MODEL, D_MODEL), const),
                  pl.BlockSpec((1, D_MODEL), const),
                  pl.BlockSpec((D_MODEL, 2 * LANES), const),
                  pl.BlockSpec((1, LANES), const)],
        out_specs=[pl.BlockSpec((TM_MIX, D_MODEL), row),
                   pl.BlockSpec((ROUTER_ROWS, TM_MIX), lambda i: (0, i))],
        out_shape=[jax.ShapeDtypeStruct((n, D_MODEL), F32),
                   jax.ShapeDtypeStruct((ROUTER_ROWS, n), F32)],
        compiler_params=pltpu.CompilerParams(dimension_semantics=("arbitrary",),
                                             vmem_limit_bytes=VMEM_LIMIT),
        name="mix_router",
    )(sb, sgn, x2, sb_g, w_out_b, ffn_g, wr2, br)


_PAD_BITS = tuple(1 << b for b in reversed(range(EXPERT_CHUNK.bit_length() - 1)))


def _dispatch_kernel(dest_ref, pad_start_ref, pad_count_ref, used_ref, h_ref, g_ref, zeros_ref, xs_ref,
                     hn_ref, sem, zsem):
    tm = TM_DISPATCH
    i = pl.program_id(0)
    n_steps = pl.num_programs(0) - 1
    n = n_steps * tm
    base = (i - 1) * tm
    prev = hn_ref.at[lax.rem(i + 1, 2)]
    n_chunks = xs_ref.shape[0] // (EXPERT_CHUNK * ROW_TILE)

    def pad_copies(do):
        for e in range(N_EXPERTS):
            start = pad_start_ref[e]
            count = pad_count_ref[e]
            for bit in _PAD_BITS:
                @pl.when((count & bit) != 0)
                def _(start=start, bit=bit):
                    do(pltpu.make_async_copy(_token_rows(zeros_ref, 0, bit),
                                             _token_rows(xs_ref, start, bit), zsem))
                start = start + (count & bit)
        for k in range(N_EXPERTS):
            chunk = used_ref[0] + k

            @pl.when(chunk < n_chunks)
            def _(chunk=chunk):
                do(pltpu.make_async_copy(zeros_ref, _token_rows(xs_ref, chunk * EXPERT_CHUNK, EXPERT_CHUNK),
                                         zsem))

    @pl.when(i == 0)
    def _():
        pad_copies(lambda cp: cp.start())

    @pl.when(i > 0)
    def _():
        def body(r, c):
            src = _token_rows(prev, r, 1)
            for s in range(2):
                pltpu.make_async_copy(src, _token_rows(xs_ref, dest_ref[s * n + base + r], 1),
                                      sem).start(priority=s)
            return c

        lax.fori_loop(0, tm, body, 0, unroll=8)

    @pl.when(i < n_steps)
    def _():
        _rows_to_tiles(hn_ref.at[lax.rem(i, 2)], _rms(h_ref[...], g_ref[...]))

    @pl.when(i > 0)
    def _():
        for _ in range(2):
            pltpu.make_async_copy(prev, _token_rows(xs_ref, 0, tm), sem).wait()

    @pl.when(i == n_steps)
    def _():
        pad_copies(lambda cp: cp.wait())


def _dispatch(dest, pad_start, pad_count, used_chunks, h, ffn_g, n_rows):
    n_steps = h.shape[0] // TM_DISPATCH
    zeros = jnp.zeros((EXPERT_CHUNK * ROW_TILE, LANES), F32)
    return pl.pallas_call(
        _dispatch_kernel,
        grid_spec=pltpu.PrefetchScalarGridSpec(
            num_scalar_prefetch=4,
            grid=(n_steps + 1,),
            in_specs=[pl.BlockSpec((TM_DISPATCH, D_MODEL), lambda i, *_: (jnp.minimum(i, n_steps - 1), 0)),
                      pl.BlockSpec((1, D_MODEL), lambda i, *_: (0, 0)),
                      pl.BlockSpec(memory_space=pl.ANY)],
            out_specs=pl.BlockSpec(memory_space=pl.ANY),
            scratch_shapes=[pltpu.VMEM((2, TM_DISPATCH * ROW_TILE, LANES), F32),
                            pltpu.SemaphoreType.DMA, pltpu.SemaphoreType.DMA]),
        out_shape=jax.ShapeDtypeStruct((n_rows * ROW_TILE, LANES), F32),
        compiler_params=pltpu.CompilerParams(dimension_semantics=("arbitrary",),
                                             vmem_limit_bytes=VMEM_LIMIT),
        name="dispatch",
    )(dest, pad_start, pad_count, used_chunks, h, ffn_g, zeros)


X_SLOTS = 3
TILE_CHUNKS = TM_EXPERT // EXPERT_CHUNK
W_SLOTS = 3


def _expert_kernel(tiles_ref, chunk0_ref, chunks_ref, nt_ref, used_ref, xs_ref, wg_ref, wu_ref, wd_ref,
                   zeros_ref, ys_ref, x_buf, y_buf, sg_buf, su_buf, sd_buf, wgb, wub, wdb, state,
                   w_sems, x_sems, y_sems, zsem):
    nt = nt_ref[0]
    n_chunks = ys_ref.shape[0] // (EXPERT_CHUNK * ROW_TILE)

    def tile_copies(tile, do, out):
        for c in range(TILE_CHUNKS):
            @pl.when(c < chunks_ref[tile])
            def _(c=c):
                first = (chunk0_ref[tile] + c) * EXPERT_CHUNK
                if out:
                    slot = lax.rem(tile, 2)
                    do(pltpu.make_async_copy(_token_rows(y_buf.at[slot], c * EXPERT_CHUNK, EXPERT_CHUNK),
                                             _token_rows(ys_ref, first, EXPERT_CHUNK), y_sems.at[slot]))
                else:
                    slot = lax.rem(tile, X_SLOTS)
                    do(pltpu.make_async_copy(_token_rows(xs_ref, first, EXPERT_CHUNK),
                                             _token_rows(x_buf.at[slot], c * EXPERT_CHUNK, EXPERT_CHUNK),
                                             x_sems.at[slot]))

    start = lambda cp: cp.start()
    wait = lambda cp: cp.wait()

    def tail_copies(do):
        for k in range(N_EXPERTS):
            chunk = used_ref[0] + k

            @pl.when(chunk < n_chunks)
            def _(chunk=chunk):
                do(pltpu.make_async_copy(zeros_ref, _token_rows(ys_ref, chunk * EXPERT_CHUNK, EXPERT_CHUNK),
                                         zsem))

    def weight_copies(e, slot):
        return (pltpu.make_async_copy(wg_ref.at[e], sg_buf.at[slot], w_sems.at[slot]),
                pltpu.make_async_copy(wu_ref.at[e], su_buf.at[slot], w_sems.at[slot]),
                pltpu.make_async_copy(wd_ref.at[e], sd_buf.at[slot], w_sems.at[slot]))

    def next_with_rows(e):
        return lax.while_loop(lambda k: (k < N_EXPERTS) & (tiles_ref[jnp.minimum(k, N_EXPERTS - 1)] == 0),
                              lambda k: k + 1, e + 1)

    first = next_with_rows(jnp.int32(-1))
    second = next_with_rows(first)
    state[0] = jnp.int32(-1)
    state[1] = jnp.int32(0)
    state[2] = jnp.int32(W_SLOTS - 1)
    state[3] = first
    state[4] = second
    for cp in weight_copies(first, 0):
        cp.start()

    @pl.when(second < N_EXPERTS)
    def _():
        for cp in weight_copies(second, 1):
            cp.start()

    tile_copies(0, start, False)

    @pl.when(nt > 1)
    def _():
        tile_copies(1, start, False)

    tail_copies(start)

    def one_tile(t, carry):
        @pl.when(t + 2 < nt)
        def _():
            tile_copies(t + 2, start, False)

        @pl.when(state[1] == 0)
        def _():
            e = state[3]
            nxt = state[4]
            slot = lax.rem(state[2] + 1, W_SLOTS)
            after_next = next_with_rows(nxt)
            state[0] = e
            state[1] = tiles_ref[e]
            state[2] = slot
            state[3] = nxt
            state[4] = after_next
            for cp in weight_copies(e, slot):
                cp.wait()

            @pl.when(after_next < N_EXPERTS)
            def _():
                for cp in weight_copies(after_next, lax.rem(slot + 2, W_SLOTS)):
                    cp.start()

            wgb[...] = sg_buf[slot].astype(BF16)
            wub[...] = su_buf[slot].astype(BF16)
            wdb[...] = sd_buf[slot].astype(BF16)

        state[1] = state[1] - 1
        tile_copies(t, wait, False)

        @pl.when(t >= 2)
        def _():
            tile_copies(t - 2, wait, True)

        for n_chunks_here in range(1, TILE_CHUNKS + 1):
            @pl.when(chunks_ref[t] == n_chunks_here)
            def _(m=n_chunks_here * EXPERT_CHUNK):
                x = _tiles_to_rows(x_buf.at[lax.rem(t, X_SLOTS)], m).astype(BF16)
                g = _dot(x, wgb[...])
                u = _dot(x, wub[...])
                hidden = (g * jax.nn.sigmoid(g)) * u
                _rows_to_tiles(y_buf.at[lax.rem(t, 2)], _dot(hidden.astype(BF16), wdb[...]))

        tile_copies(t, start, True)
        return carry

    lax.fori_loop(0, nt, one_tile, 0)

    for back in (2, 1):
        @pl.when(nt >= back)
        def _(back=back):
            tile_copies(nt - back, wait, True)

    tail_copies(wait)


def _experts(tiles, chunk0, chunks, n_tiles, used_chunks, xs, wg, wu, wd):
    any_spec = pl.BlockSpec(memory_space=pl.ANY)
    zeros = jnp.zeros((EXPERT_CHUNK * ROW_TILE, LANES), F32)
    return pl.pallas_call(
        _expert_kernel,
        grid_spec=pltpu.PrefetchScalarGridSpec(
            num_scalar_prefetch=5,
            grid=(1,),
            in_specs=[any_spec, any_spec, any_spec, any_spec, any_spec],
            out_specs=any_spec,
            scratch_shapes=[pltpu.VMEM((X_SLOTS, TM_EXPERT * ROW_TILE, LANES), F32),
                            pltpu.VMEM((2, TM_EXPERT * ROW_TILE, LANES), F32),
                            pltpu.VMEM((W_SLOTS, D_MODEL, D_EXPERT), F32),
                            pltpu.VMEM((W_SLOTS, D_MODEL, D_EXPERT), F32),
                            pltpu.VMEM((W_SLOTS, D_EXPERT, D_MODEL), F32),
                            pltpu.VMEM((D_MODEL, D_EXPERT), BF16),
                            pltpu.VMEM((D_MODEL, D_EXPERT), BF16),
                            pltpu.VMEM((D_EXPERT, D_MODEL), BF16),
                            pltpu.SMEM((5,), jnp.int32),
                            pltpu.SemaphoreType.DMA((W_SLOTS,)),
                            pltpu.SemaphoreType.DMA((X_SLOTS,)),
                            pltpu.SemaphoreType.DMA((2,)),
                            pltpu.SemaphoreType.DMA]),
        out_shape=jax.ShapeDtypeStruct(xs.shape, F32),
        compiler_params=pltpu.CompilerParams(dimension_semantics=("arbitrary",),
                                             vmem_limit_bytes=VMEM_LIMIT),
        name="expert_mlp",
    )(tiles, chunk0, chunks, n_tiles, used_chunks, xs, wg, wu, wd, zeros)


def _combine_kernel(dest_ref, h_ref, rw_ref, fg_ref, y_ref, o_ref, buf, sems):
    tm = TM_COMBINE
    i = pl.program_id(0)
    n_steps = pl.num_programs(0)
    n = n_steps * tm
    cur = i % 2

    def fetch(step, half):
        def body(r, c):
            for s in range(2):
                pltpu.make_async_copy(_token_rows(y_ref, dest_ref[s * n + step * tm + r], 1),
                                      _token_rows(buf.at[half, s], r, 1),
                                      sems.at[half]).start(priority=s)
            return c

        lax.fori_loop(0, tm, body, 0, unroll=8)

    @pl.when(i == 0)
    def _():
        fetch(0, 0)

    @pl.when(i + 1 < n_steps)
    def _():
        fetch(i + 1, 1 - cur)

    for s in range(2):
        pltpu.make_async_copy(_token_rows(y_ref, 0, tm), buf.at[cur, s], sems.at[cur]).wait()
    rw = rw_ref[...]
    out = (h_ref[...] + rw[:, 0:1] * _tiles_to_rows(buf.at[cur, 0], tm)
           + rw[:, 1:2] * _tiles_to_rows(buf.at[cur, 1], tm))
    o_ref[...] = _rms(out, fg_ref[...])


def _combine(dest, h, rw, final_g, ys):
    n = h.shape[0]
    return pl.pallas_call(
        _combine_kernel,
        grid_spec=pltpu.PrefetchScalarGridSpec(
            num_scalar_prefetch=1,
            grid=(n // TM_COMBINE,),
            in_specs=[pl.BlockSpec((TM_COMBINE, D_MODEL), lambda i, d: (i, 0)),
                      pl.BlockSpec((TM_COMBINE, LANES), lambda i, d: (i, 0)),
                      pl.BlockSpec((1, D_MODEL), lambda i, d: (0, 0)),
                      pl.BlockSpec(memory_space=pl.ANY)],
            out_specs=pl.BlockSpec((TM_COMBINE, D_MODEL), lambda i, d: (i, 0)),
            scratch_shapes=[pltpu.VMEM((2, 2, TM_COMBINE * ROW_TILE, LANES), F32),
                            pltpu.SemaphoreType.DMA((2,))]),
        out_shape=jax.ShapeDtypeStruct((n, D_MODEL), F32),
        compiler_params=pltpu.CompilerParams(dimension_semantics=("arbitrary",),
                                             vmem_limit_bytes=VMEM_LIMIT),
        name="combine",
    )(dest, h, rw, final_g, ys)


def _schedule(counts, max_tiles):
    chunks = (counts + EXPERT_CHUNK - 1) // EXPERT_CHUNK
    chunk_end = jnp.cumsum(chunks)
    chunk_start = chunk_end - chunks
    tiles = (chunks + TILE_CHUNKS - 1) // TILE_CHUNKS
    tile_end = jnp.cumsum(tiles)
    tile = jnp.arange(max_tiles, dtype=jnp.int32)
    owner = jnp.sum(tile[:, None] >= tile_end[None, :], axis=1)
    is_owner = owner[:, None] == jnp.arange(N_EXPERTS, dtype=jnp.int32)[None, :]
    of_owner = lambda v: jnp.sum(jnp.where(is_owner, v[None, :], 0), axis=1)
    done = (tile - of_owner(tile_end - tiles)) * TILE_CHUNKS
    tile_chunk0 = (of_owner(chunk_start) + done).astype(jnp.int32)
    tile_chunks = jnp.clip(of_owner(chunks) - done, 0, TILE_CHUNKS).astype(jnp.int32)
    return tiles, chunk_start * EXPERT_CHUNK, tile_chunk0, tile_chunks, tile_end[-1:], chunk_end[-1:]


def _layer(x, attn_g, w_in, sg_g, w_sp, b_sp, sb_g, sg_out_g, w_out, ffn_g,
           w_rg, b_rg, w_re, b_re, w_gate, w_up, w_down):
    batch, seq, _ = x.shape
    n = batch * seq
    x2 = x.reshape(n, D_MODEL)
    row = lambda v: v.reshape(1, -1)

    bsp_full = jnp.repeat(b_sp.T, HEAD_DIM, axis=1)
    qkv, sgn = _inproj(x2, row(attn_g), w_in.astype(BF16), row(sg_g), w_sp, bsp_full, row(sg_out_g))
    sb = _attention(qkv, batch, seq).reshape(n, SB_WIDTH)

    pad_lanes = lambda v, width: jnp.pad(v, [(0, 0)] * (v.ndim - 1) + [(0, width - v.shape[-1])])
    w_r = jnp.concatenate([pad_lanes(w_rg, ROUTER_LANE0),
                           jnp.transpose(w_re, (1, 0, 2)).reshape(D_MODEL, N_EXPERTS)], axis=1)
    w_r = pad_lanes(w_r, LANES)
    wr_hi = w_r.astype(BF16)
    wr_lo = (w_r - wr_hi.astype(F32)).astype(BF16)
    wr2 = jnp.concatenate([wr_hi, wr_lo], axis=1)
    b_r = pad_lanes(jnp.concatenate([pad_lanes(b_rg, ROUTER_LANE0), b_re.reshape(-1)]), LANES)

    h, lg = _mix(sb, sgn, x2, row(sb_g), w_out.astype(BF16), row(ffn_g), wr2, row(b_r))
    ri, rw, cnt = _route(lg)

    counts = cnt[:, 0].astype(jnp.int32)
    n_rows = 2 * n + N_EXPERTS * EXPERT_CHUNK
    tiles, offsets, tile_chunk0, tile_chunks, n_tiles, used_chunks = _schedule(
        counts, 2 * n // TM_EXPERT + N_EXPERTS)
    expert, rank = ri[0:2], ri[2:4]
    is_e = expert[None] == jnp.arange(N_EXPERTS, dtype=jnp.int32)[:, None, None]
    dest = (jnp.sum(jnp.where(is_e, offsets[:, None, None], 0), axis=0) + rank).reshape(-1)
    pad_start = offsets + counts
    pad_count = (-counts) % EXPERT_CHUNK

    xs = _dispatch(dest, pad_start, pad_count, used_chunks, h, row(ffn_g), n_rows)
    ys = _experts(tiles, tile_chunk0, tile_chunks, n_tiles, used_chunks, xs,
                  w_gate.reshape(N_EXPERTS, D_MODEL, D_EXPERT),
                  w_up.reshape(N_EXPERTS, D_MODEL, D_EXPERT),
                  w_down.reshape(N_EXPERTS, D_EXPERT, D_MODEL))
    return dest, h, rw, ys


def kernel(x, attn_norm_g, w_in, sg_norm_g, w_spatial, b_spatial, sb_out_norm_g, sg_out_norm_g,
           w_out, ffn_norm_g, w_router_group, b_router_group, w_router_expert, b_router_expert,
           w_gate, w_up, w_down, final_norm_g):
    assert attn_norm_g.shape[0] == 1, "single-layer problem"
    batch, seq, _ = x.shape
    dest, h, rw, ys = _layer(x, attn_norm_g[0], w_in[0], sg_norm_g[0], w_spatial[0], b_spatial[0],
                             sb_out_norm_g[0], sg_out_norm_g[0], w_out[0], ffn_norm_g[0],
                             w_router_group[0], b_router_group[0], w_router_expert[0],
                             b_router_expert[0], w_gate[0], w_up[0], w_down[0])
    out = _combine(dest, h, rw, final_norm_g.reshape(1, -1), ys)
    return out.reshape(batch, seq, D_MODEL)
```

```python
import functools
import math

import jax
import jax.numpy as jnp
from jax import lax
from jax.experimental import pallas as pl
from jax.experimental.pallas import tpu as pltpu

D_MODEL = 1024
HEAD_DIM = 64
SB_WIDTH = 512
SG_WIDTH = 512
SG_HEADS = 8
D_IN = 3 * SB_WIDTH + 2 * SG_WIDTH
CHUNK = 128
N_GROUPS = 4
EXPERTS_PER_GROUP = 8
N_EXPERTS = N_GROUPS * EXPERTS_PER_GROUP
D_EXPERT = 512
EPS = 1e-6
F32_EXP_UNDERFLOW = 110.0

LANES = 128
SUBLANES = 8
ROW_TILE = D_MODEL // LANES
assert ROW_TILE == SUBLANES
HEAD_PAIR = 2 * HEAD_DIM
ROUTER_LANE0 = SUBLANES
ROUTER_ROWS = ROUTER_LANE0 + N_EXPERTS
assert EXPERTS_PER_GROUP == SUBLANES and N_GROUPS <= ROUTER_LANE0

TM_PROJ = 1024
TQ_ATTN = 256
ATTN_BLOCKS_PER_STEP = 2
ATTN_TOP_ROWS = (160, 176)
TM_MIX = 1024
TM_ROUTE = 1024
TM_DISPATCH = 1024
TM_EXPERT = 640
EXPERT_CHUNK = 128
TM_COMBINE = 512
VMEM_LIMIT = 48 * 1024 * 1024

F32 = jnp.float32
BF16 = jnp.bfloat16


def _rms(x, g):
    return x * lax.rsqrt(jnp.mean(x * x, axis=-1, keepdims=True) + EPS) * g


def _gelu(x):
    c = math.sqrt(2.0 / math.pi)
    return x * (0.5 * (1.0 + jnp.tanh(c * (x + 0.044715 * (x * x * x)))))


def _softplus(z):
    return jnp.maximum(z, 0.0) + jnp.log(1.0 + jnp.exp(-jnp.abs(z)))


def _dot(a, b):
    return jnp.dot(a, b, preferred_element_type=F32)


def _rows_to_tiles(ref, x):
    m = x.shape[0]
    for k in range(ROW_TILE):
        ref[pl.ds(k, m, stride=ROW_TILE), :] = x[:, k * LANES:(k + 1) * LANES]


def _tiles_to_rows(ref, m):
    return jnp.concatenate([ref[pl.ds(k, m, stride=ROW_TILE), :] for k in range(ROW_TILE)], axis=1)


def _token_rows(ref, first_token, n_tokens):
    return ref.at[pl.ds(pl.multiple_of(first_token * ROW_TILE, ROW_TILE), n_tokens * ROW_TILE)]


def _split_bf16(x):
    hi = x.astype(BF16)
    lo = (x - hi.astype(F32)).astype(BF16)
    return hi, lo


def _inproj_kernel(x_ref, g_ref, w_ref, sgg_ref, wsp_ref, bsp_ref, sgog_ref, qkv_ref, sgn_ref,
                   gu_ref, vgn_ref, sg_ref):
    tm = TM_PROJ
    hb = _rms(x_ref[...], g_ref[...]).astype(BF16)
    gv = _gelu(_dot(hb, w_ref[:, 3 * SB_WIDTH + SG_WIDTH:D_IN]))
    vgn_ref[...] = _rms(gv, sgg_ref[...]).astype(BF16)
    gu_ref[...] = _gelu(_dot(hb, w_ref[:, 3 * SB_WIDTH:3 * SB_WIDTH + SG_WIDTH]))
    q = _dot(hb, w_ref[:, 0:SB_WIDTH]) * (1.0 / math.sqrt(HEAD_DIM))
    qkv_ref[:, 0:SB_WIDTH] = q.astype(BF16)
    qkv_ref[:, SB_WIDTH:2 * SB_WIDTH] = _dot(hb, w_ref[:, SB_WIDTH:2 * SB_WIDTH]).astype(BF16)

    lane = lax.broadcasted_iota(jnp.int32, (1, LANES), 1)
    first = lane < HEAD_DIM
    zero = jnp.zeros((), BF16)
    r_c = lax.broadcasted_iota(jnp.int32, (CHUNK, CHUNK), 0)
    c_c = lax.broadcasted_iota(jnp.int32, (CHUNK, CHUNK), 1)
    tril = r_c >= c_c
    n_pairs = SG_WIDTH // HEAD_PAIR
    w_pairs = []
    for p in range(n_pairs):
        w0 = jnp.where(tril, wsp_ref[2 * p], 0.0).astype(BF16)
        w1 = jnp.where(tril, wsp_ref[2 * p + 1], 0.0).astype(BF16)
        w_pairs.append(jnp.concatenate([w0, w1], axis=1))
    bsp = bsp_ref[...]
    for c in range(tm // CHUNK):
        rows = slice(c * CHUNK, (c + 1) * CHUNK)
        for p in range(n_pairs):
            cols = slice(p * HEAD_PAIR, (p + 1) * HEAD_PAIR)
            vg = vgn_ref[rows, cols]
            rhs = jnp.concatenate([jnp.where(first, vg, zero), jnp.where(first, zero, vg)], axis=0)
            mixed = _dot(w_pairs[p], rhs) + bsp[:, cols]
            sg_ref[rows, cols] = gu_ref[rows, cols] * mixed
    qkv_ref[:, 2 * SB_WIDTH:3 * SB_WIDTH] = _dot(hb, w_ref[:, 2 * SB_WIDTH:3 * SB_WIDTH]).astype(BF16)
    sgn_ref[...] = _rms(sg_ref[...], sgog_ref[...]).astype(BF16)


def _inproj(x2, attn_g, w_in_b, sg_g, wsp, bsp_full, sg_out_g):
    n = x2.shape[0]
    row = lambda i: (i, 0)
    const = lambda i: (0, 0)
    return pl.pallas_call(
        _inproj_kernel,
        grid=(n // TM_PROJ,),
        in_specs=[pl.BlockSpec((TM_PROJ, D_MODEL), row),
                  pl.BlockSpec((1, D_MODEL), const),
                  pl.BlockSpec((D_MODEL, D_IN), const),
                  pl.BlockSpec((1, SG_WIDTH), const),
                  pl.BlockSpec((SG_HEADS, CHUNK, CHUNK), lambda i: (0, 0, 0)),
                  pl.BlockSpec((CHUNK, SG_WIDTH), const),
                  pl.BlockSpec((1, SG_WIDTH), const)],
        out_specs=[pl.BlockSpec((TM_PROJ, 3 * SB_WIDTH), row),
                   pl.BlockSpec((TM_PROJ, SG_WIDTH), row)],
        out_shape=[jax.ShapeDtypeStruct((n, 3 * SB_WIDTH), BF16),
                   jax.ShapeDtypeStruct((n, SG_WIDTH), BF16)],
        scratch_shapes=[pltpu.VMEM((TM_PROJ, SG_WIDTH), F32),
                        pltpu.VMEM((TM_PROJ, SG_WIDTH), BF16),
                        pltpu.VMEM((TM_PROJ, SG_WIDTH), F32)],
        compiler_params=pltpu.CompilerParams(dimension_semantics=("arbitrary",),
                                             vmem_limit_bytes=VMEM_LIMIT),
        name="inproj",
    )(x2, attn_g, w_in_b, sg_g, wsp, bsp_full, sg_out_g)


def _attn_kernel(q_ref, k_ref, v_ref, o_ref, q2_ref, carry_ref):
    t = TQ_ATTN
    n_pairs = SB_WIDTH // HEAD_PAIR
    lane = lax.broadcasted_iota(jnp.int32, (1, HEAD_PAIR), 1)
    head_lanes = (lane < HEAD_DIM, lane >= HEAD_DIM)
    zero = jnp.zeros((), BF16)
    r_idx = lax.broadcasted_iota(jnp.int32, (t, t), 0)
    c_idx = lax.broadcasted_iota(jnp.int32, (t, t), 1)
    suffix = (r_idx > c_idx).astype(BF16)
    suffix2 = jnp.concatenate([suffix, suffix], axis=0)
    causal = c_idx < r_idx

    def one_query_block(sub, c):
        qi = pl.program_id(1) * ATTN_BLOCKS_PER_STEP + sub
        row0 = pl.multiple_of(sub * t, t)
        for p in range(n_pairs):
            qp = q_ref[0, pl.ds(row0, t), p * HEAD_PAIR:(p + 1) * HEAD_PAIR]
            for h in range(2):
                q2_ref[(2 * p + h) * t:(2 * p + h + 1) * t, :] = jnp.where(head_lanes[h], qp, zero)
        o_ref[0, pl.ds(row0, t), :] = jnp.zeros((t, SB_WIDTH), F32)
        carry_ref[...] = jnp.zeros_like(carry_ref)

        def block(j, diag, m):
            start = pl.multiple_of(j * t, t)
            mask2 = jnp.concatenate([causal, causal], axis=0) if diag else None
            st = [dict() for _ in range(n_pairs)]

            def head_rows(p):
                return [slice((2 * p + h) * t, (2 * p + h) * t + m) for h in range(2)]

            def scores(p):
                d = st[p]
                d["cols"] = slice(p * HEAD_PAIR, (p + 1) * HEAD_PAIR)
                kb = k_ref[0, pl.ds(start, t), d["cols"]]
                q2 = jnp.concatenate([q2_ref[r, :] for r in head_rows(p)], axis=0)
                z = lax.dot_general(q2, kb, (((1,), (1,)), ((), ())),
                                    preferred_element_type=F32)
                sp = _softplus(z)
                nl = jnp.where(mask2, sp, 0.0) if diag else sp
                hi, lo = _split_bf16(nl)
                d["hl"] = jnp.concatenate([hi, lo], axis=1)
                d["log_beta"] = z - sp
                d["nl0"] = nl[:, 0:1]

            def weights(p):
                d = st[p]
                hl = d["hl"]
                after = jnp.concatenate([_dot(hl[0:m], suffix2), _dot(hl[m:2 * m], suffix2)], axis=0)
                carry = jnp.concatenate([carry_ref[r, :] for r in head_rows(p)], axis=0)
                a = jnp.exp(d["log_beta"] - after - carry)
                if diag:
                    a = jnp.where(mask2, a, 0.0)
                a = a.astype(BF16)
                d["a2"] = jnp.concatenate([a[0:m], a[m:2 * m]], axis=1)
                new_carry = carry + after[:, 0:1] + d["nl0"]
                for h, r in enumerate(head_rows(p)):
                    carry_ref[r, :] = new_carry[h * m:(h + 1) * m]

            def values(p):
                d = st[p]
                vb = v_ref[0, pl.ds(start, t), d["cols"]]
                v2 = jnp.concatenate([jnp.where(head_lanes[0], vb, zero),
                                      jnp.where(head_lanes[1], vb, zero)], axis=0)
                o_ref[0, pl.ds(row0, m), d["cols"]] += _dot(d["a2"], v2)

            for step in range(n_pairs + 2):
                if step < n_pairs:
                    scores(step)
                if 0 <= step - 1 < n_pairs:
                    weights(step - 1)
                if 0 <= step - 2 < n_pairs:
                    values(step - 2)

        def flags():
            bounds = (0,) + ATTN_TOP_ROWS + (t,)
            lowest = [jnp.min(jnp.concatenate([carry_ref[hh * t + lo:hh * t + hi, :] for hh in range(2 * n_pairs)],
                                              axis=0))
                      for lo, hi in zip(bounds[:-1], bounds[1:])]
            below = [functools.reduce(jnp.minimum, lowest[k:]) for k in range(len(lowest))]
            return (below[0] < F32_EXP_UNDERFLOW,) + tuple(b >= F32_EXP_UNDERFLOW for b in below[1:])

        block(qi, True, t)

        def body(state):
            it, _, *done = state
            j = qi - 1 - it
            for k, m in enumerate(ATTN_TOP_ROWS + (t,)):
                use = done[k] if k < len(done) else True
                if k > 0:
                    use = jnp.logical_and(use, jnp.logical_not(done[k - 1]))

                @pl.when(use)
                def _(m=m):
                    block(j, False, m)

            return (it + 1,) + flags()

        lax.while_loop(lambda s: (s[0] < qi) & s[1], body, (jnp.int32(0),) + flags())
        return c

    lax.fori_loop(0, ATTN_BLOCKS_PER_STEP, one_query_block, 0)


def _attention(qkv, batch, seq):
    qkv3 = qkv.reshape(batch, seq, 3 * SB_WIDTH)
    n_heads = SB_WIDTH // HEAD_DIM
    return pl.pallas_call(
        _attn_kernel,
        grid=(batch, seq // (ATTN_BLOCKS_PER_STEP * TQ_ATTN)),
        in_specs=[pl.BlockSpec((1, ATTN_BLOCKS_PER_STEP * TQ_ATTN, SB_WIDTH), lambda b, i: (b, i, 0)),
                  pl.BlockSpec((1, seq, SB_WIDTH), lambda b, i: (b, 0, 1)),
                  pl.BlockSpec((1, seq, SB_WIDTH), lambda b, i: (b, 0, 2))],
        out_specs=pl.BlockSpec((1, ATTN_BLOCKS_PER_STEP * TQ_ATTN, SB_WIDTH), lambda b, i: (b, i, 0)),
        out_shape=jax.ShapeDtypeStruct((batch, seq, SB_WIDTH), F32),
        scratch_shapes=[pltpu.VMEM((n_heads * TQ_ATTN, HEAD_PAIR), BF16),
                        pltpu.VMEM((n_heads * TQ_ATTN, 1), F32)],
        compiler_params=pltpu.CompilerParams(dimension_semantics=("arbitrary",) * 2,
                                             vmem_limit_bytes=VMEM_LIMIT),
        name="sb_attention",
    )(qkv3, qkv3, qkv3)


def _mix_kernel(sb_ref, sgn_ref, x_ref, sbg_ref, wout_ref, ffng_ref, wr2_ref, br_ref,
                h_ref, lg_ref):
    sbn = _rms(sb_ref[...], sbg_ref[...]).astype(BF16)
    h = x_ref[...] + _dot(sbn, wout_ref[0:SB_WIDTH, :]) + _dot(sgn_ref[...], wout_ref[SB_WIDTH:, :])
    h_ref[...] = h
    hn = _rms(h, ffng_ref[...])

    hn_hi, hn_lo = _split_bf16(hn)
    both = _dot(hn_hi, wr2_ref[...])
    logits = both[:, 0:LANES] + both[:, LANES:] + _dot(hn_lo, wr2_ref[:, 0:LANES]) + br_ref[...]
    lg_ref[...] = logits.T[0:ROUTER_ROWS, :]


def _route_kernel(lg_ref, ri_ref, rw_ref, cnt_ref, count_ref):
    tr = TM_ROUTE
    i = pl.program_id(0)

    @pl.when(i == 0)
    def _():
        count_ref[...] = jnp.zeros_like(count_ref)

    neg = jnp.float32(-jnp.inf)
    row8 = lax.broadcasted_iota(jnp.int32, (SUBLANES, tr), 0)

    def top(v):
        m = jnp.max(v, axis=0, keepdims=True)
        return m, jnp.min(jnp.where(v == m, row8, SUBLANES), axis=0, keepdims=True)

    def group_rows(g):
        return lg_ref[ROUTER_LANE0 + g * EXPERTS_PER_GROUP:ROUTER_LANE0 + (g + 1) * EXPERTS_PER_GROUP, :]

    gl = jnp.where(row8 < N_GROUPS, lg_ref[0:SUBLANES, :], neg)
    gmax, gidx = top(gl)
    gweight = 1.0 / jnp.sum(jnp.exp(gl - gmax), axis=0, keepdims=True)
    el = group_rows(0)
    for g in range(1, N_GROUPS):
        el = jnp.where(gidx == g, group_rows(g), el)
    m1, i1 = top(el)
    m2, i2 = top(jnp.where(row8 == i1, neg, el))
    t21 = jnp.exp(m2 - m1)
    w1 = gweight / (1.0 + t21)
    w2 = gweight * t21 / (1.0 + t21)
    e1 = gidx * EXPERTS_PER_GROUP + i1
    e2 = gidx * EXPERTS_PER_GROUP + i2

    row_e = lax.broadcasted_iota(jnp.int32, (N_EXPERTS, tr), 0)
    sel1 = row_e == e1
    sel2 = row_e == e2
    onehot = jnp.where(sel1 | sel2, 1.0, 0.0)
    r_t = lax.broadcasted_iota(jnp.int32, (tr, tr), 0)
    c_t = lax.broadcasted_iota(jnp.int32, (tr, tr), 1)
    before = (r_t < c_t).astype(BF16)
    running = count_ref[:, 0:1] + _dot(onehot.astype(BF16), before)
    rank1 = jnp.sum(jnp.where(sel1, running, 0.0), axis=0, keepdims=True)
    rank2 = jnp.sum(jnp.where(sel2, running, 0.0), axis=0, keepdims=True)
    new_count = count_ref[:, 0:1] + jnp.sum(onehot, axis=1, keepdims=True)
    count_ref[...] = jnp.broadcast_to(new_count, count_ref.shape)
    cnt_ref[...] = jnp.broadcast_to(new_count, cnt_ref.shape)

    ri_ref[...] = jnp.where(row8 == 0, e1, jnp.where(row8 == 1, e2, jnp.where(
        row8 == 2, rank1.astype(jnp.int32), jnp.where(row8 == 3, rank2.astype(jnp.int32), 0))))
    row128 = lax.broadcasted_iota(jnp.int32, (LANES, tr), 0)
    rw_ref[...] = jnp.where(row128 == 0, w1, jnp.where(row128 == 1, w2, 0.0)).T


def _route(lg):
    n = lg.shape[1]
    return pl.pallas_call(
        _route_kernel,
        grid=(n // TM_ROUTE,),
        in_specs=[pl.BlockSpec((ROUTER_ROWS, TM_ROUTE), lambda i: (0, i))],
        out_specs=[pl.BlockSpec((SUBLANES, TM_ROUTE), lambda i: (0, i)),
                   pl.BlockSpec((TM_ROUTE, LANES), lambda i: (i, 0)),
                   pl.BlockSpec((N_EXPERTS, LANES), lambda i: (0, 0))],
        out_shape=[jax.ShapeDtypeStruct((SUBLANES, n), jnp.int32),
                   jax.ShapeDtypeStruct((n, LANES), F32),
                   jax.ShapeDtypeStruct((N_EXPERTS, LANES), F32)],
        scratch_shapes=[pltpu.VMEM((N_EXPERTS, LANES), F32)],
        compiler_params=pltpu.CompilerParams(dimension_semantics=("arbitrary",),
                                             vmem_limit_bytes=VMEM_LIMIT),
        name="route",
    )(lg)


def _mix(sb, sgn, x2, sb_g, w_out_b, ffn_g, wr2, br):
    n = x2.shape[0]
    row = lambda i: (i, 0)
    const = lambda i: (0, 0)
    return pl.pallas_call(
        _mix_kernel,
        grid=(n // TM_MIX,),
        in_specs=[pl.BlockSpec((TM_MIX, SB_WIDTH), row),
                  pl.BlockSpec((TM_MIX, SG_WIDTH), row),
                  pl.BlockSpec((TM_MIX, D_MODEL), row),
                  pl.BlockSpec((1, SB_WIDTH), const),
                  pl.BlockSpec((D_MODEL, D_MODEL), const),
                  pl.BlockSpec((1, D_MODEL), const),
                  pl.BlockSpec((D_MODEL, 2 * LANES), const),
                  pl.BlockSpec((1, LANES), const)],
        out_specs=[pl.BlockSpec((TM_MIX, D_MODEL), row),
                   pl.BlockSpec((ROUTER_ROWS, TM_MIX), lambda i: (0, i))],
        out_shape=[jax.ShapeDtypeStruct((n, D_MODEL), F32),
                   jax.ShapeDtypeStruct((ROUTER_ROWS, n), F32)],
        compiler_params=pltpu.CompilerParams(dimension_semantics=("arbitrary",),
                                             vmem_limit_bytes=VMEM_LIMIT),
        name="mix_router",
    )(sb, sgn, x2, sb_g, w_out_b, ffn_g, wr2, br)


_PAD_BITS = tuple(1 << b for b in reversed(range(EXPERT_CHUNK.bit_length() - 1)))


def _dispatch_kernel(dest_ref, pad_start_ref, pad_count_ref, used_ref, h_ref, g_ref, zeros_ref, xs_ref,
                     hn_ref, sem, zsem):
    tm = TM_DISPATCH
    i = pl.program_id(0)
    n_steps = pl.num_programs(0) - 1
    n = n_steps * tm
    base = (i - 1) * tm
    prev = hn_ref.at[lax.rem(i + 1, 2)]
    n_chunks = xs_ref.shape[0] // (EXPERT_CHUNK * ROW_TILE)

    def pad_copies(do):
        for e in range(N_EXPERTS):
            start = pad_start_ref[e]
            count = pad_count_ref[e]
            for bit in _PAD_BITS:
                @pl.when((count & bit) != 0)
                def _(start=start, bit=bit):
                    do(pltpu.make_async_copy(_token_rows(zeros_ref, 0, bit),
                                             _token_rows(xs_ref, start, bit), zsem))
                start = start + (count & bit)
        for k in range(N_EXPERTS):
            chunk = used_ref[0] + k

            @pl.when(chunk < n_chunks)
            def _(chunk=chunk):
                do(pltpu.make_async_copy(zeros_ref, _token_rows(xs_ref, chunk * EXPERT_CHUNK, EXPERT_CHUNK),
                                         zsem))

    @pl.when(i == 0)
    def _():
        pad_copies(lambda cp: cp.start())

    @pl.when(i > 0)
    def _():
        def body(r, c):
            src = _token_rows(prev, r, 1)
            for s in range(2):
                pltpu.make_async_copy(src, _token_rows(xs_ref, dest_ref[s * n + base + r], 1),
                                      sem).start(priority=s)
            return c

        lax.fori_loop(0, tm, body, 0, unroll=8)

    @pl.when(i < n_steps)
    def _():
        _rows_to_tiles(hn_ref.at[lax.rem(i, 2)], _rms(h_ref[...], g_ref[...]))

    @pl.when(i > 0)
    def _():
        for _ in range(2):
            pltpu.make_async_copy(prev, _token_rows(xs_ref, 0, tm), sem).wait()

    @pl.when(i == n_steps)
    def _():
        pad_copies(lambda cp: cp.wait())


def _dispatch(dest, pad_start, pad_count, used_chunks, h, ffn_g, n_rows):
    n_steps = h.shape[0] // TM_DISPATCH
    zeros = jnp.zeros((EXPERT_CHUNK * ROW_TILE, LANES), F32)
    return pl.pallas_call(
        _dispatch_kernel,
        grid_spec=pltpu.PrefetchScalarGridSpec(
            num_scalar_prefetch=4,
            grid=(n_steps + 1,),
            in_specs=[pl.BlockSpec((TM_DISPATCH, D_MODEL), lambda i, *_: (jnp.minimum(i, n_steps - 1), 0)),
                      pl.BlockSpec((1, D_MODEL), lambda i, *_: (0, 0)),
                      pl.BlockSpec(memory_space=pl.ANY)],
            out_specs=pl.BlockSpec(memory_space=pl.ANY),
            scratch_shapes=[pltpu.VMEM((2, TM_DISPATCH * ROW_TILE, LANES), F32),
                            pltpu.SemaphoreType.DMA, pltpu.SemaphoreType.DMA]),
        out_shape=jax.ShapeDtypeStruct((n_rows * ROW_TILE, LANES), F32),
        compiler_params=pltpu.CompilerParams(dimension_semantics=("arbitrary",),
                                             vmem_limit_bytes=VMEM_LIMIT),
        name="dispatch",
    )(dest, pad_start, pad_count, used_chunks, h, ffn_g, zeros)


X_SLOTS = 3
TILE_CHUNKS = TM_EXPERT // EXPERT_CHUNK
W_SLOTS = 3


def _expert_kernel(tiles_ref, chunk0_ref, chunks_ref, nt_ref, used_ref, xs_ref, wg_ref, wu_ref, wd_ref,
                   zeros_ref, ys_ref, x_buf, y_buf, sg_buf, su_buf, sd_buf, wgb, wub, wdb, state,
                   w_sems, x_sems, y_sems, zsem):
    t = pl.program_id(0)
    last = pl.num_programs(0) - 1
    nt = nt_ref[0]
    n_chunks = ys_ref.shape[0] // (EXPERT_CHUNK * ROW_TILE)

    def tile_copies(tile, do, out):
        for c in range(TILE_CHUNKS):
            @pl.when(c < chunks_ref[tile])
            def _(c=c):
                first = (chunk0_ref[tile] + c) * EXPERT_CHUNK
                if out:
                    slot = lax.rem(tile, 2)
                    do(pltpu.make_async_copy(_token_rows(y_buf.at[slot], c * EXPERT_CHUNK, EXPERT_CHUNK),
                                             _token_rows(ys_ref, first, EXPERT_CHUNK), y_sems.at[slot]))
                else:
                    slot = lax.rem(tile, X_SLOTS)
                    do(pltpu.make_async_copy(_token_rows(xs_ref, first, EXPERT_CHUNK),
                                             _token_rows(x_buf.at[slot], c * EXPERT_CHUNK, EXPERT_CHUNK),
                                             x_sems.at[slot]))

    start = lambda cp: cp.start()
    wait = lambda cp: cp.wait()

    def tail_copies(do):
        for k in range(N_EXPERTS):
            chunk = used_ref[0] + k

            @pl.when(chunk < n_chunks)
            def _(chunk=chunk):
                do(pltpu.make_async_copy(zeros_ref, _token_rows(ys_ref, chunk * EXPERT_CHUNK, EXPERT_CHUNK),
                                         zsem))

    def weight_copies(e, slot):
        return (pltpu.make_async_copy(wg_ref.at[e], sg_buf.at[slot], w_sems.at[slot]),
                pltpu.make_async_copy(wu_ref.at[e], su_buf.at[slot], w_sems.at[slot]),
                pltpu.make_async_copy(wd_ref.at[e], sd_buf.at[slot], w_sems.at[slot]))

    def next_with_rows(e):
        return lax.while_loop(lambda k: (k < N_EXPERTS) & (tiles_ref[jnp.minimum(k, N_EXPERTS - 1)] == 0),
                              lambda k: k + 1, e + 1)

    @pl.when(t == 0)
    def _():
        first = next_with_rows(jnp.int32(-1))
        second = next_with_rows(first)
        state[0] = jnp.int32(-1)
        state[1] = jnp.int32(0)
        state[2] = jnp.int32(W_SLOTS - 1)
        state[3] = first
        state[4] = second
        for cp in weight_copies(first, 0):
            cp.start()

        @pl.when(second < N_EXPERTS)
        def _():
            for cp in weight_copies(second, 1):
                cp.start()

        tile_copies(0, start, False)

        @pl.when(nt > 1)
        def _():
            tile_copies(1, start, False)

        tail_copies(start)

    @pl.when(t + 2 < nt)
    def _():
        tile_copies(t + 2, start, False)

    @pl.when(t < nt)
    def _():
        @pl.when(state[1] == 0)
        def _():
            e = state[3]
            nxt = state[4]
            slot = lax.rem(state[2] + 1, W_SLOTS)
            after_next = next_with_rows(nxt)
            state[0] = e
            state[1] = tiles_ref[e]
            state[2] = slot
            state[3] = nxt
            state[4] = after_next
            for cp in weight_copies(e, slot):
                cp.wait()

            @pl.when(after_next < N_EXPERTS)
            def _():
                for cp in weight_copies(after_next, lax.rem(slot + 2, W_SLOTS)):
                    cp.start()

            wgb[...] = sg_buf[slot].astype(BF16)
            wub[...] = su_buf[slot].astype(BF16)
            wdb[...] = sd_buf[slot].astype(BF16)

        state[1] = state[1] - 1
        tile_copies(t, wait, False)

        @pl.when(t >= 2)
        def _():
            tile_copies(t - 2, wait, True)

        for n_chunks_here in range(1, TILE_CHUNKS + 1):
            @pl.when(chunks_ref[t] == n_chunks_here)
            def _(m=n_chunks_here * EXPERT_CHUNK):
                x = _tiles_to_rows(x_buf.at[lax.rem(t, X_SLOTS)], m).astype(BF16)
                g = _dot(x, wgb[...])
                u = _dot(x, wub[...])
                hidden = (g * jax.nn.sigmoid(g)) * u
                _rows_to_tiles(y_buf.at[lax.rem(t, 2)], _dot(hidden.astype(BF16), wdb[...]))

        tile_copies(t, start, True)

    @pl.when(t == last)
    def _():
        for back in (2, 1):
            @pl.when(nt >= back)
            def _(back=back):
                tile_copies(nt - back, wait, True)

        tail_copies(wait)


def _experts(tiles, chunk0, chunks, n_tiles, used_chunks, xs, wg, wu, wd):
    any_spec = pl.BlockSpec(memory_space=pl.ANY)
    zeros = jnp.zeros((EXPERT_CHUNK * ROW_TILE, LANES), F32)
    return pl.pallas_call(
        _expert_kernel,
        grid_spec=pltpu.PrefetchScalarGridSpec(
            num_scalar_prefetch=5,
            grid=(chunks.shape[0],),
            in_specs=[any_spec, any_spec, any_spec, any_spec, any_spec],
            out_specs=any_spec,
            scratch_shapes=[pltpu.VMEM((X_SLOTS, TM_EXPERT * ROW_TILE, LANES), F32),
                            pltpu.VMEM((2, TM_EXPERT * ROW_TILE, LANES), F32),
                            pltpu.VMEM((W_SLOTS, D_MODEL, D_EXPERT), F32),
                            pltpu.VMEM((W_SLOTS, D_MODEL, D_EXPERT), F32),
                            pltpu.VMEM((W_SLOTS, D_EXPERT, D_MODEL), F32),
                            pltpu.VMEM((D_MODEL, D_EXPERT), BF16),
                            pltpu.VMEM((D_MODEL, D_EXPERT), BF16),
                            pltpu.VMEM((D_EXPERT, D_MODEL), BF16),
                            pltpu.SMEM((5,), jnp.int32),
                            pltpu.SemaphoreType.DMA((W_SLOTS,)),
                            pltpu.SemaphoreType.DMA((X_SLOTS,)),
                            pltpu.SemaphoreType.DMA((2,)),
                            pltpu.SemaphoreType.DMA]),
        out_shape=jax.ShapeDtypeStruct(xs.shape, F32),
        compiler_params=pltpu.CompilerParams(dimension_semantics=("arbitrary",),
                                             vmem_limit_bytes=VMEM_LIMIT),
        name="expert_mlp",
    )(tiles, chunk0, chunks, n_tiles, used_chunks, xs, wg, wu, wd, zeros)


def _combine_kernel(dest_ref, h_ref, rw_ref, fg_ref, y_ref, o_ref, buf, sems):
    tm = TM_COMBINE
    i = pl.program_id(0)
    n_steps = pl.num_programs(0)
    n = n_steps * tm
    cur = i % 2

    def fetch(step, half):
        def body(r, c):
            for s in range(2):
                pltpu.make_async_copy(_token_rows(y_ref, dest_ref[s * n + step * tm + r], 1),
                                      _token_rows(buf.at[half, s], r, 1),
                                      sems.at[half]).start(priority=s)
            return c

        lax.fori_loop(0, tm, body, 0, unroll=8)

    @pl.when(i == 0)
    def _():
        fetch(0, 0)

    @pl.when(i + 1 < n_steps)
    def _():
        fetch(i + 1, 1 - cur)

    for s in range(2):
        pltpu.make_async_copy(_token_rows(y_ref, 0, tm), buf.at[cur, s], sems.at[cur]).wait()
    rw = rw_ref[...]
    out = (h_ref[...] + rw[:, 0:1] * _tiles_to_rows(buf.at[cur, 0], tm)
           + rw[:, 1:2] * _tiles_to_rows(buf.at[cur, 1], tm))
    o_ref[...] = _rms(out, fg_ref[...])


def _combine(dest, h, rw, final_g, ys):
    n = h.shape[0]
    return pl.pallas_call(
        _combine_kernel,
        grid_spec=pltpu.PrefetchScalarGridSpec(
            num_scalar_prefetch=1,
            grid=(n // TM_COMBINE,),
            in_specs=[pl.BlockSpec((TM_COMBINE, D_MODEL), lambda i, d: (i, 0)),
                      pl.BlockSpec((TM_COMBINE, LANES), lambda i, d: (i, 0)),
                      pl.BlockSpec((1, D_MODEL), lambda i, d: (0, 0)),
                      pl.BlockSpec(memory_space=pl.ANY)],
            out_specs=pl.BlockSpec((TM_COMBINE, D_MODEL), lambda i, d: (i, 0)),
            scratch_shapes=[pltpu.VMEM((2, 2, TM_COMBINE * ROW_TILE, LANES), F32),
                            pltpu.SemaphoreType.DMA((2,))]),
        out_shape=jax.ShapeDtypeStruct((n, D_MODEL), F32),
        compiler_params=pltpu.CompilerParams(dimension_semantics=("arbitrary",),
                                             vmem_limit_bytes=VMEM_LIMIT),
        name="combine",
    )(dest, h, rw, final_g, ys)


def _schedule(counts, max_tiles):
    chunks = (counts + EXPERT_CHUNK - 1) // EXPERT_CHUNK
    chunk_end = jnp.cumsum(chunks)
    chunk_start = chunk_end - chunks
    tiles = (chunks + TILE_CHUNKS - 1) // TILE_CHUNKS
    tile_end = jnp.cumsum(tiles)
    tile = jnp.arange(max_tiles, dtype=jnp.int32)
    owner = jnp.sum(tile[:, None] >= tile_end[None, :], axis=1)
    is_owner = owner[:, None] == jnp.arange(N_EXPERTS, dtype=jnp.int32)[None, :]
    of_owner = lambda v: jnp.sum(jnp.where(is_owner, v[None, :], 0), axis=1)
    done = (tile - of_owner(tile_end - tiles)) * TILE_CHUNKS
    tile_chunk0 = (of_owner(chunk_start) + done).astype(jnp.int32)
    tile_chunks = jnp.clip(of_owner(chunks) - done, 0, TILE_CHUNKS).astype(jnp.int32)
    return tiles, chunk_start * EXPERT_CHUNK, tile_chunk0, tile_chunks, tile_end[-1:], chunk_end[-1:]


def _layer(x, attn_g, w_in, sg_g, w_sp, b_sp, sb_g, sg_out_g, w_out, ffn_g,
           w_rg, b_rg, w_re, b_re, w_gate, w_up, w_down):
    batch, seq, _ = x.shape
    n = batch * seq
    x2 = x.reshape(n, D_MODEL)
    row = lambda v: v.reshape(1, -1)

    bsp_full = jnp.repeat(b_sp.T, HEAD_DIM, axis=1)
    qkv, sgn = _inproj(x2, row(attn_g), w_in.astype(BF16), row(sg_g), w_sp, bsp_full, row(sg_out_g))
    sb = _attention(qkv, batch, seq).reshape(n, SB_WIDTH)

    pad_lanes = lambda v, width: jnp.pad(v, [(0, 0)] * (v.ndim - 1) + [(0, width - v.shape[-1])])
    w_r = jnp.concatenate([pad_lanes(w_rg, ROUTER_LANE0),
                           jnp.transpose(w_re, (1, 0, 2)).reshape(D_MODEL, N_EXPERTS)], axis=1)
    w_r = pad_lanes(w_r, LANES)
    wr_hi = w_r.astype(BF16)
    wr_lo = (w_r - wr_hi.astype(F32)).astype(BF16)
    wr2 = jnp.concatenate([wr_hi, wr_lo], axis=1)
    b_r = pad_lanes(jnp.concatenate([pad_lanes(b_rg, ROUTER_LANE0), b_re.reshape(-1)]), LANES)

    h, lg = _mix(sb, sgn, x2, row(sb_g), w_out.astype(BF16), row(ffn_g), wr2, row(b_r))
    ri, rw, cnt = _route(lg)

    counts = cnt[:, 0].astype(jnp.int32)
    n_rows = 2 * n + N_EXPERTS * EXPERT_CHUNK
    tiles, offsets, tile_chunk0, tile_chunks, n_tiles, used_chunks = _schedule(
        counts, 2 * n // TM_EXPERT + N_EXPERTS)
    expert, rank = ri[0:2], ri[2:4]
    is_e = expert[None] == jnp.arange(N_EXPERTS, dtype=jnp.int32)[:, None, None]
    dest = (jnp.sum(jnp.where(is_e, offsets[:, None, None], 0), axis=0) + rank).reshape(-1)
    pad_start = offsets + counts
    pad_count = (-counts) % EXPERT_CHUNK

    xs = _dispatch(dest, pad_start, pad_count, used_chunks, h, row(ffn_g), n_rows)
    ys = _experts(tiles, tile_chunk0, tile_chunks, n_tiles, used_chunks, xs,
                  w_gate.reshape(N_EXPERTS, D_MODEL, D_EXPERT),
                  w_up.reshape(N_EXPERTS, D_MODEL, D_EXPERT),
                  w_down.reshape(N_EXPERTS, D_EXPERT, D_MODEL))
    return dest, h, rw, ys


def kernel(x, attn_norm_g, w_in, sg_norm_g, w_spatial, b_spatial, sb_out_norm_g, sg_out_norm_g,
           w_out, ffn_norm_g, w_router_group, b_router_group, w_router_expert, b_router_expert,
           w_gate, w_up, w_down, final_norm_g):
    assert attn_norm_g.shape[0] == 1, "single-layer problem"
    batch, seq, _ = x.shape
    dest, h, rw, ys = _layer(x, attn_norm_g[0], w_in[0], sg_norm_g[0], w_spatial[0], b_spatial[0],
                             sb_out_norm_g[0], sg_out_norm_g[0], w_out[0], ffn_norm_g[0],
                             w_router_group[0], b_router_group[0], w_router_expert[0],
                             b_router_expert[0], w_gate[0], w_up[0], w_down[0])
    out = _combine(dest, h, rw, final_norm_g.reshape(1, -1), ys)
    return out.reshape(batch, seq, D_MODEL)
```

```python
import functools
import math

import jax
import jax.numpy as jnp
from jax import lax
from jax.experimental import pallas as pl
from jax.experimental.pallas import tpu as pltpu

D_MODEL = 1024
HEAD_DIM = 64
SB_WIDTH = 512
SG_WIDTH = 512
SG_HEADS = 8
D_IN = 3 * SB_WIDTH + 2 * SG_WIDTH
CHUNK = 128
N_GROUPS = 4
EXPERTS_PER_GROUP = 8
N_EXPERTS = N_GROUPS * EXPERTS_PER_GROUP
D_EXPERT = 512
EPS = 1e-6
F32_EXP_UNDERFLOW = 110.0

LANES = 128
SUBLANES = 8
ROW_TILE = D_MODEL // LANES
assert ROW_TILE == SUBLANES
HEAD_PAIR = 2 * HEAD_DIM
ROUTER_LANE0 = SUBLANES
ROUTER_ROWS = ROUTER_LANE0 + N_EXPERTS
assert EXPERTS_PER_GROUP == SUBLANES and N_GROUPS <= ROUTER_LANE0

TM_PROJ = 1024
TQ_ATTN = 256
ATTN_BLOCKS_PER_STEP = 2
ATTN_TOP_ROWS = (160, 176)
TM_MIX = 1024
TM_ROUTE = 1024
TM_DISPATCH = 1024
TM_EXPERT = 768
EXPERT_CHUNK = 128
TM_COMBINE = 512
VMEM_LIMIT = 48 * 1024 * 1024

F32 = jnp.float32
BF16 = jnp.bfloat16


def _rms(x, g):
    return x * lax.rsqrt(jnp.mean(x * x, axis=-1, keepdims=True) + EPS) * g


def _gelu(x):
    c = math.sqrt(2.0 / math.pi)
    return x * (0.5 * (1.0 + jnp.tanh(c * (x + 0.044715 * (x * x * x)))))


def _softplus(z):
    return jnp.maximum(z, 0.0) + jnp.log(1.0 + jnp.exp(-jnp.abs(z)))


def _dot(a, b):
    return jnp.dot(a, b, preferred_element_type=F32)


def _rows_to_tiles(ref, x):
    m = x.shape[0]
    for k in range(ROW_TILE):
        ref[pl.ds(k, m, stride=ROW_TILE), :] = x[:, k * LANES:(k + 1) * LANES]


def _tiles_to_rows(ref, m):
    return jnp.concatenate([ref[pl.ds(k, m, stride=ROW_TILE), :] for k in range(ROW_TILE)], axis=1)


def _token_rows(ref, first_token, n_tokens):
    return ref.at[pl.ds(pl.multiple_of(first_token * ROW_TILE, ROW_TILE), n_tokens * ROW_TILE)]


def _split_bf16(x):
    hi = x.astype(BF16)
    lo = (x - hi.astype(F32)).astype(BF16)
    return hi, lo


def _inproj_kernel(x_ref, g_ref, w_ref, sgg_ref, wsp_ref, bsp_ref, sgog_ref, qkv_ref, sgn_ref,
                   gu_ref, vgn_ref, sg_ref):
    tm = TM_PROJ
    hb = _rms(x_ref[...], g_ref[...]).astype(BF16)
    gv = _gelu(_dot(hb, w_ref[:, 3 * SB_WIDTH + SG_WIDTH:D_IN]))
    vgn_ref[...] = _rms(gv, sgg_ref[...]).astype(BF16)
    gu_ref[...] = _gelu(_dot(hb, w_ref[:, 3 * SB_WIDTH:3 * SB_WIDTH + SG_WIDTH]))
    q = _dot(hb, w_ref[:, 0:SB_WIDTH]) * (1.0 / math.sqrt(HEAD_DIM))
    qkv_ref[:, 0:SB_WIDTH] = q.astype(BF16)
    qkv_ref[:, SB_WIDTH:2 * SB_WIDTH] = _dot(hb, w_ref[:, SB_WIDTH:2 * SB_WIDTH]).astype(BF16)

    lane = lax.broadcasted_iota(jnp.int32, (1, LANES), 1)
    first = lane < HEAD_DIM
    zero = jnp.zeros((), BF16)
    r_c = lax.broadcasted_iota(jnp.int32, (CHUNK, CHUNK), 0)
    c_c = lax.broadcasted_iota(jnp.int32, (CHUNK, CHUNK), 1)
    tril = r_c >= c_c
    n_pairs = SG_WIDTH // HEAD_PAIR
    w_pairs = []
    for p in range(n_pairs):
        w0 = jnp.where(tril, wsp_ref[2 * p], 0.0).astype(BF16)
        w1 = jnp.where(tril, wsp_ref[2 * p + 1], 0.0).astype(BF16)
        w_pairs.append(jnp.concatenate([w0, w1], axis=1))
    bsp = bsp_ref[...]
    for c in range(tm // CHUNK):
        rows = slice(c * CHUNK, (c + 1) * CHUNK)
        for p in range(n_pairs):
            cols = slice(p * HEAD_PAIR, (p + 1) * HEAD_PAIR)
            vg = vgn_ref[rows, cols]
            rhs = jnp.concatenate([jnp.where(first, vg, zero), jnp.where(first, zero, vg)], axis=0)
            mixed = _dot(w_pairs[p], rhs) + bsp[:, cols]
            sg_ref[rows, cols] = gu_ref[rows, cols] * mixed
    qkv_ref[:, 2 * SB_WIDTH:3 * SB_WIDTH] = _dot(hb, w_ref[:, 2 * SB_WIDTH:3 * SB_WIDTH]).astype(BF16)
    sgn_ref[...] = _rms(sg_ref[...], sgog_ref[...]).astype(BF16)


def _inproj(x2, attn_g, w_in_b, sg_g, wsp, bsp_full, sg_out_g):
    n = x2.shape[0]
    row = lambda i: (i, 0)
    const = lambda i: (0, 0)
    return pl.pallas_call(
        _inproj_kernel,
        grid=(n // TM_PROJ,),
        in_specs=[pl.BlockSpec((TM_PROJ, D_MODEL), row),
                  pl.BlockSpec((1, D_MODEL), const),
                  pl.BlockSpec((D_MODEL, D_IN), const),
                  pl.BlockSpec((1, SG_WIDTH), const),
                  pl.BlockSpec((SG_HEADS, CHUNK, CHUNK), lambda i: (0, 0, 0)),
                  pl.BlockSpec((CHUNK, SG_WIDTH), const),
                  pl.BlockSpec((1, SG_WIDTH), const)],
        out_specs=[pl.BlockSpec((TM_PROJ, 3 * SB_WIDTH), row),
                   pl.BlockSpec((TM_PROJ, SG_WIDTH), row)],
        out_shape=[jax.ShapeDtypeStruct((n, 3 * SB_WIDTH), BF16),
                   jax.ShapeDtypeStruct((n, SG_WIDTH), BF16)],
        scratch_shapes=[pltpu.VMEM((TM_PROJ, SG_WIDTH), F32),
                        pltpu.VMEM((TM_PROJ, SG_WIDTH), BF16),
                        pltpu.VMEM((TM_PROJ, SG_WIDTH), F32)],
        compiler_params=pltpu.CompilerParams(dimension_semantics=("arbitrary",),
                                             vmem_limit_bytes=VMEM_LIMIT),
        name="inproj",
    )(x2, attn_g, w_in_b, sg_g, wsp, bsp_full, sg_out_g)


def _attn_kernel(q_ref, k_ref, v_ref, o_ref, q2_ref, carry_ref):
    t = TQ_ATTN
    n_pairs = SB_WIDTH // HEAD_PAIR
    lane = lax.broadcasted_iota(jnp.int32, (1, HEAD_PAIR), 1)
    head_lanes = (lane < HEAD_DIM, lane >= HEAD_DIM)
    zero = jnp.zeros((), BF16)
    r_idx = lax.broadcasted_iota(jnp.int32, (t, t), 0)
    c_idx = lax.broadcasted_iota(jnp.int32, (t, t), 1)
    suffix = (r_idx > c_idx).astype(BF16)
    suffix2 = jnp.concatenate([suffix, suffix], axis=0)
    causal = c_idx < r_idx

    def one_query_block(sub, c):
        qi = pl.program_id(1) * ATTN_BLOCKS_PER_STEP + sub
        row0 = pl.multiple_of(sub * t, t)
        for p in range(n_pairs):
            qp = q_ref[0, pl.ds(row0, t), p * HEAD_PAIR:(p + 1) * HEAD_PAIR]
            for h in range(2):
                q2_ref[(2 * p + h) * t:(2 * p + h + 1) * t, :] = jnp.where(head_lanes[h], qp, zero)
        o_ref[0, pl.ds(row0, t), :] = jnp.zeros((t, SB_WIDTH), F32)
        carry_ref[...] = jnp.zeros_like(carry_ref)

        def block(j, diag, m):
            start = pl.multiple_of(j * t, t)
            mask2 = jnp.concatenate([causal, causal], axis=0) if diag else None
            st = [dict() for _ in range(n_pairs)]

            def head_rows(p):
                return [slice((2 * p + h) * t, (2 * p + h) * t + m) for h in range(2)]

            def scores(p):
                d = st[p]
                d["cols"] = slice(p * HEAD_PAIR, (p + 1) * HEAD_PAIR)
                kb = k_ref[0, pl.ds(start, t), d["cols"]]
                q2 = jnp.concatenate([q2_ref[r, :] for r in head_rows(p)], axis=0)
                z = lax.dot_general(q2, kb, (((1,), (1,)), ((), ())),
                                    preferred_element_type=F32)
                sp = _softplus(z)
                nl = jnp.where(mask2, sp, 0.0) if diag else sp
                hi, lo = _split_bf16(nl)
                d["hl"] = jnp.concatenate([hi, lo], axis=1)
                d["log_beta"] = z - sp
                d["nl0"] = nl[:, 0:1]

            def weights(p):
                d = st[p]
                hl = d["hl"]
                after = jnp.concatenate([_dot(hl[0:m], suffix2), _dot(hl[m:2 * m], suffix2)], axis=0)
                carry = jnp.concatenate([carry_ref[r, :] for r in head_rows(p)], axis=0)
                a = jnp.exp(d["log_beta"] - after - carry)
                if diag:
                    a = jnp.where(mask2, a, 0.0)
                a = a.astype(BF16)
                d["a2"] = jnp.concatenate([a[0:m], a[m:2 * m]], axis=1)
                new_carry = carry + after[:, 0:1] + d["nl0"]
                for h, r in enumerate(head_rows(p)):
                    carry_ref[r, :] = new_carry[h * m:(h + 1) * m]

            def values(p):
                d = st[p]
                vb = v_ref[0, pl.ds(start, t), d["cols"]]
                v2 = jnp.concatenate([jnp.where(head_lanes[0], vb, zero),
                                      jnp.where(head_lanes[1], vb, zero)], axis=0)
                o_ref[0, pl.ds(row0, m), d["cols"]] += _dot(d["a2"], v2)

            for step in range(n_pairs + 2):
                if step < n_pairs:
                    scores(step)
                if 0 <= step - 1 < n_pairs:
                    weights(step - 1)
                if 0 <= step - 2 < n_pairs:
                    values(step - 2)

        def flags():
            bounds = (0,) + ATTN_TOP_ROWS + (t,)
            lowest = [jnp.min(jnp.concatenate([carry_ref[hh * t + lo:hh * t + hi, :] for hh in range(2 * n_pairs)],
                                              axis=0))
                      for lo, hi in zip(bounds[:-1], bounds[1:])]
            below = [functools.reduce(jnp.minimum, lowest[k:]) for k in range(len(lowest))]
            return (below[0] < F32_EXP_UNDERFLOW,) + tuple(b >= F32_EXP_UNDERFLOW for b in below[1:])

        block(qi, True, t)

        def body(state):
            it, _, *done = state
            j = qi - 1 - it
            for k, m in enumerate(ATTN_TOP_ROWS + (t,)):
                use = done[k] if k < len(done) else True
                if k > 0:
                    use = jnp.logical_and(use, jnp.logical_not(done[k - 1]))

                @pl.when(use)
                def _(m=m):
                    block(j, False, m)

            return (it + 1,) + flags()

        lax.while_loop(lambda s: (s[0] < qi) & s[1], body, (jnp.int32(0),) + flags())
        return c

    lax.fori_loop(0, ATTN_BLOCKS_PER_STEP, one_query_block, 0)


def _attention(qkv, batch, seq):
    qkv3 = qkv.reshape(batch, seq, 3 * SB_WIDTH)
    n_heads = SB_WIDTH // HEAD_DIM
    return pl.pallas_call(
        _attn_kernel,
        grid=(batch, seq // (ATTN_BLOCKS_PER_STEP * TQ_ATTN)),
        in_specs=[pl.BlockSpec((1, ATTN_BLOCKS_PER_STEP * TQ_ATTN, SB_WIDTH), lambda b, i: (b, i, 0)),
                  pl.BlockSpec((1, seq, SB_WIDTH), lambda b, i: (b, 0, 1)),
                  pl.BlockSpec((1, seq, SB_WIDTH), lambda b, i: (b, 0, 2))],
        out_specs=pl.BlockSpec((1, ATTN_BLOCKS_PER_STEP * TQ_ATTN, SB_WIDTH), lambda b, i: (b, i, 0)),
        out_shape=jax.ShapeDtypeStruct((batch, seq, SB_WIDTH), F32),
        scratch_shapes=[pltpu.VMEM((n_heads * TQ_ATTN, HEAD_PAIR), BF16),
                        pltpu.VMEM((n_heads * TQ_ATTN, 1), F32)],
        compiler_params=pltpu.CompilerParams(dimension_semantics=("arbitrary",) * 2,
                                             vmem_limit_bytes=VMEM_LIMIT),
        name="sb_attention",
    )(qkv3, qkv3, qkv3)


def _mix_kernel(sb_ref, sgn_ref, x_ref, sbg_ref, wout_ref, ffng_ref, wr2_ref, br_ref,
                h_ref, lg_ref):
    sbn = _rms(sb_ref[...], sbg_ref[...]).astype(BF16)
    h = x_ref[...] + _dot(sbn, wout_ref[0:SB_WIDTH, :]) + _dot(sgn_ref[...], wout_ref[SB_WIDTH:, :])
    h_ref[...] = h
    hn = _rms(h, ffng_ref[...])

    hn_hi, hn_lo = _split_bf16(hn)
    both = _dot(hn_hi, wr2_ref[...])
    logits = both[:, 0:LANES] + both[:, LANES:] + _dot(hn_lo, wr2_ref[:, 0:LANES]) + br_ref[...]
    lg_ref[...] = logits.T[0:ROUTER_ROWS, :]


def _route_kernel(lg_ref, ri_ref, rw_ref, cnt_ref, count_ref):
    tr = TM_ROUTE
    i = pl.program_id(0)

    @pl.when(i == 0)
    def _():
        count_ref[...] = jnp.zeros_like(count_ref)

    neg = jnp.float32(-jnp.inf)
    row8 = lax.broadcasted_iota(jnp.int32, (SUBLANES, tr), 0)

    def top(v):
        m = jnp.max(v, axis=0, keepdims=True)
        return m, jnp.min(jnp.where(v == m, row8, SUBLANES), axis=0, keepdims=True)

    def group_rows(g):
        return lg_ref[ROUTER_LANE0 + g * EXPERTS_PER_GROUP:ROUTER_LANE0 + (g + 1) * EXPERTS_PER_GROUP, :]

    gl = jnp.where(row8 < N_GROUPS, lg_ref[0:SUBLANES, :], neg)
    gmax, gidx = top(gl)
    gweight = 1.0 / jnp.sum(jnp.exp(gl - gmax), axis=0, keepdims=True)
    el = group_rows(0)
    for g in range(1, N_GROUPS):
        el = jnp.where(gidx == g, group_rows(g), el)
    m1, i1 = top(el)
    m2, i2 = top(jnp.where(row8 == i1, neg, el))
    t21 = jnp.exp(m2 - m1)
    w1 = gweight / (1.0 + t21)
    w2 = gweight * t21 / (1.0 + t21)
    e1 = gidx * EXPERTS_PER_GROUP + i1
    e2 = gidx * EXPERTS_PER_GROUP + i2

    row_e = lax.broadcasted_iota(jnp.int32, (N_EXPERTS, tr), 0)
    sel1 = row_e == e1
    sel2 = row_e == e2
    onehot = jnp.where(sel1 | sel2, 1.0, 0.0)
    r_t = lax.broadcasted_iota(jnp.int32, (tr, tr), 0)
    c_t = lax.broadcasted_iota(jnp.int32, (tr, tr), 1)
    before = (r_t < c_t).astype(BF16)
    running = count_ref[:, 0:1] + _dot(onehot.astype(BF16), before)
    rank1 = jnp.sum(jnp.where(sel1, running, 0.0), axis=0, keepdims=True)
    rank2 = jnp.sum(jnp.where(sel2, running, 0.0), axis=0, keepdims=True)
    new_count = count_ref[:, 0:1] + jnp.sum(onehot, axis=1, keepdims=True)
    count_ref[...] = jnp.broadcast_to(new_count, count_ref.shape)
    cnt_ref[...] = jnp.broadcast_to(new_count, cnt_ref.shape)

    ri_ref[...] = jnp.where(row8 == 0, e1, jnp.where(row8 == 1, e2, jnp.where(
        row8 == 2, rank1.astype(jnp.int32), jnp.where(row8 == 3, rank2.astype(jnp.int32), 0))))
    row128 = lax.broadcasted_iota(jnp.int32, (LANES, tr), 0)
    rw_ref[...] = jnp.where(row128 == 0, w1, jnp.where(row128 == 1, w2, 0.0)).T


def _route(lg):
    n = lg.shape[1]
    return pl.pallas_call(
        _route_kernel,
        grid=(n // TM_ROUTE,),
        in_specs=[pl.BlockSpec((ROUTER_ROWS, TM_ROUTE), lambda i: (0, i))],
        out_specs=[pl.BlockSpec((SUBLANES, TM_ROUTE), lambda i: (0, i)),
                   pl.BlockSpec((TM_ROUTE, LANES), lambda i: (i, 0)),
                   pl.BlockSpec((N_EXPERTS, LANES), lambda i: (0, 0))],
        out_shape=[jax.ShapeDtypeStruct((SUBLANES, n), jnp.int32),
                   jax.ShapeDtypeStruct((n, LANES), F32),
                   jax.ShapeDtypeStruct((N_EXPERTS, LANES), F32)],
        scratch_shapes=[pltpu.VMEM((N_EXPERTS, LANES), F32)],
        compiler_params=pltpu.CompilerParams(dimension_semantics=("arbitrary",),
                                             vmem_limit_bytes=VMEM_LIMIT),
        name="route",
    )(lg)


def _mix(sb, sgn, x2, sb_g, w_out_b, ffn_g, wr2, br):
    n = x2.shape[0]
    row = lambda i: (i, 0)
    const = lambda i: (0, 0)
    return pl.pallas_call(
        _mix_kernel,
        grid=(n // TM_MIX,),
        in_specs=[pl.BlockSpec((TM_MIX, SB_WIDTH), row),
                  pl.BlockSpec((TM_MIX, SG_WIDTH), row),
                  pl.BlockSpec((TM_MIX, D_MODEL), row),
                  pl.BlockSpec((1, SB_WIDTH), const),
                  pl.BlockSpec((D_MODEL, D_MODEL), const),
                  pl.BlockSpec((1, D_MODEL), const),
                  pl.BlockSpec((D_MODEL, 2 * LANES), const),
                  pl.BlockSpec((1, LANES), const)],
        out_specs=[pl.BlockSpec((TM_MIX, D_MODEL), row),
                   pl.BlockSpec((ROUTER_ROWS, TM_MIX), lambda i: (0, i))],
        out_shape=[jax.ShapeDtypeStruct((n, D_MODEL), F32),
                   jax.ShapeDtypeStruct((ROUTER_ROWS, n), F32)],
        compiler_params=pltpu.CompilerParams(dimension_semantics=("arbitrary",),
                                             vmem_limit_bytes=VMEM_LIMIT),
        name="mix_router",
    )(sb, sgn, x2, sb_g, w_out_b, ffn_g, wr2, br)


_PAD_BITS = tuple(1 << b for b in reversed(range(EXPERT_CHUNK.bit_length() - 1)))


def _dispatch_kernel(dest_ref, pad_start_ref, pad_count_ref, used_ref, h_ref, g_ref, zeros_ref, xs_ref,
                     hn_ref, sem, zsem):
    tm = TM_DISPATCH
    i = pl.program_id(0)
    n_steps = pl.num_programs(0) - 1
    n = n_steps * tm
    base = (i - 1) * tm
    prev = hn_ref.at[lax.rem(i + 1, 2)]
    n_chunks = xs_ref.shape[0] // (EXPERT_CHUNK * ROW_TILE)

    def pad_copies(do):
        for e in range(N_EXPERTS):
            start = pad_start_ref[e]
            count = pad_count_ref[e]
            for bit in _PAD_BITS:
                @pl.when((count & bit) != 0)
                def _(start=start, bit=bit):
                    do(pltpu.make_async_copy(_token_rows(zeros_ref, 0, bit),
                                             _token_rows(xs_ref, start, bit), zsem))
                start = start + (count & bit)
        for k in range(N_EXPERTS):
            chunk = used_ref[0] + k

            @pl.when(chunk < n_chunks)
            def _(chunk=chunk):
                do(pltpu.make_async_copy(zeros_ref, _token_rows(xs_ref, chunk * EXPERT_CHUNK, EXPERT_CHUNK),
                                         zsem))

    @pl.when(i == 0)
    def _():
        pad_copies(lambda cp: cp.start())

    @pl.when(i > 0)
    def _():
        def body(r, c):
            src = _token_rows(prev, r, 1)
            for s in range(2):
                pltpu.make_async_copy(src, _token_rows(xs_ref, dest_ref[s * n + base + r], 1),
                                      sem).start(priority=s)
            return c

        lax.fori_loop(0, tm, body, 0, unroll=8)

    @pl.when(i < n_steps)
    def _():
        _rows_to_tiles(hn_ref.at[lax.rem(i, 2)], _rms(h_ref[...], g_ref[...]))

    @pl.when(i > 0)
    def _():
        for _ in range(2):
            pltpu.make_async_copy(prev, _token_rows(xs_ref, 0, tm), sem).wait()

    @pl.when(i == n_steps)
    def _():
        pad_copies(lambda cp: cp.wait())


def _dispatch(dest, pad_start, pad_count, used_chunks, h, ffn_g, n_rows):
    n_steps = h.shape[0] // TM_DISPATCH
    zeros = jnp.zeros((EXPERT_CHUNK * ROW_TILE, LANES), F32)
    return pl.pallas_call(
        _dispatch_kernel,
        grid_spec=pltpu.PrefetchScalarGridSpec(
            num_scalar_prefetch=4,
            grid=(n_steps + 1,),
            in_specs=[pl.BlockSpec((TM_DISPATCH, D_MODEL), lambda i, *_: (jnp.minimum(i, n_steps - 1), 0)),
                      pl.BlockSpec((1, D_MODEL), lambda i, *_: (0, 0)),
                      pl.BlockSpec(memory_space=pl.ANY)],
            out_specs=pl.BlockSpec(memory_space=pl.ANY),
            scratch_shapes=[pltpu.VMEM((2, TM_DISPATCH * ROW_TILE, LANES), F32),
                            pltpu.SemaphoreType.DMA, pltpu.SemaphoreType.DMA]),
        out_shape=jax.ShapeDtypeStruct((n_rows * ROW_TILE, LANES), F32),
        compiler_params=pltpu.CompilerParams(dimension_semantics=("arbitrary",),
                                             vmem_limit_bytes=VMEM_LIMIT),
        name="dispatch",
    )(dest, pad_start, pad_count, used_chunks, h, ffn_g, zeros)


X_SLOTS = 3
TILE_CHUNKS = TM_EXPERT // EXPERT_CHUNK
W_SLOTS = 3


def _expert_kernel(tiles_ref, chunk0_ref, chunks_ref, nt_ref, used_ref, xs_ref, wg_ref, wu_ref, wd_ref,
                   zeros_ref, ys_ref, x_buf, y_buf, sg_buf, su_buf, sd_buf, wgb, wub, wdb, state,
                   w_sems, x_sems, y_sems, zsem):
    t = pl.program_id(0)
    last = pl.num_programs(0) - 1
    nt = nt_ref[0]
    n_chunks = ys_ref.shape[0] // (EXPERT_CHUNK * ROW_TILE)

    def tile_copies(tile, do, out):
        for c in range(TILE_CHUNKS):
            @pl.when(c < chunks_ref[tile])
            def _(c=c):
                first = (chunk0_ref[tile] + c) * EXPERT_CHUNK
                if out:
                    slot = lax.rem(tile, 2)
                    do(pltpu.make_async_copy(_token_rows(y_buf.at[slot], c * EXPERT_CHUNK, EXPERT_CHUNK),
                                             _token_rows(ys_ref, first, EXPERT_CHUNK), y_sems.at[slot]))
                else:
                    slot = lax.rem(tile, X_SLOTS)
                    do(pltpu.make_async_copy(_token_rows(xs_ref, first, EXPERT_CHUNK),
                                             _token_rows(x_buf.at[slot], c * EXPERT_CHUNK, EXPERT_CHUNK),
                                             x_sems.at[slot]))

    start = lambda cp: cp.start()
    wait = lambda cp: cp.wait()

    def tail_copies(do):
        for k in range(N_EXPERTS):
            chunk = used_ref[0] + k

            @pl.when(chunk < n_chunks)
            def _(chunk=chunk):
                do(pltpu.make_async_copy(zeros_ref, _token_rows(ys_ref, chunk * EXPERT_CHUNK, EXPERT_CHUNK),
                                         zsem))

    def weight_copies(e, slot):
        return (pltpu.make_async_copy(wg_ref.at[e], sg_buf.at[slot], w_sems.at[slot]),
                pltpu.make_async_copy(wu_ref.at[e], su_buf.at[slot], w_sems.at[slot]),
                pltpu.make_async_copy(wd_ref.at[e], sd_buf.at[slot], w_sems.at[slot]))

    def next_with_rows(e):
        return lax.while_loop(lambda k: (k < N_EXPERTS) & (tiles_ref[jnp.minimum(k, N_EXPERTS - 1)] == 0),
                              lambda k: k + 1, e + 1)

    @pl.when(t == 0)
    def _():
        first = next_with_rows(jnp.int32(-1))
        second = next_with_rows(first)
        state[0] = jnp.int32(-1)
        state[1] = jnp.int32(0)
        state[2] = jnp.int32(W_SLOTS - 1)
        state[3] = first
        state[4] = second
        for cp in weight_copies(first, 0):
            cp.start()

        @pl.when(second < N_EXPERTS)
        def _():
            for cp in weight_copies(second, 1):
                cp.start()

        tile_copies(0, start, False)

        @pl.when(nt > 1)
        def _():
            tile_copies(1, start, False)

        tail_copies(start)

    @pl.when(t + 2 < nt)
    def _():
        tile_copies(t + 2, start, False)

    @pl.when(t < nt)
    def _():
        @pl.when(state[1] == 0)
        def _():
            e = state[3]
            nxt = state[4]
            slot = lax.rem(state[2] + 1, W_SLOTS)
            after_next = next_with_rows(nxt)
            state[0] = e
            state[1] = tiles_ref[e]
            state[2] = slot
            state[3] = nxt
            state[4] = after_next
            for cp in weight_copies(e, slot):
                cp.wait()

            @pl.when(after_next < N_EXPERTS)
            def _():
                for cp in weight_copies(after_next, lax.rem(slot + 2, W_SLOTS)):
                    cp.start()

            wgb[...] = sg_buf[slot].astype(BF16)
            wub[...] = su_buf[slot].astype(BF16)
            wdb[...] = sd_buf[slot].astype(BF16)

        state[1] = state[1] - 1
        tile_copies(t, wait, False)

        @pl.when(t >= 2)
        def _():
            tile_copies(t - 2, wait, True)

        for n_chunks_here in range(1, TILE_CHUNKS + 1):
            @pl.when(chunks_ref[t] == n_chunks_here)
            def _(m=n_chunks_here * EXPERT_CHUNK):
                x = _tiles_to_rows(x_buf.at[lax.rem(t, X_SLOTS)], m).astype(BF16)
                g = _dot(x, wgb[...])
                u = _dot(x, wub[...])
                hidden = (g * jax.nn.sigmoid(g)) * u
                _rows_to_tiles(y_buf.at[lax.rem(t, 2)], _dot(hidden.astype(BF16), wdb[...]))

        tile_copies(t, start, True)

    @pl.when(t == last)
    def _():
        for back in (2, 1):
            @pl.when(nt >= back)
            def _(back=back):
                tile_copies(nt - back, wait, True)

        tail_copies(wait)


def _experts(tiles, chunk0, chunks, n_tiles, used_chunks, xs, wg, wu, wd):
    any_spec = pl.BlockSpec(memory_space=pl.ANY)
    zeros = jnp.zeros((EXPERT_CHUNK * ROW_TILE, LANES), F32)
    return pl.pallas_call(
        _expert_kernel,
        grid_spec=pltpu.PrefetchScalarGridSpec(
            num_scalar_prefetch=5,
            grid=(chunks.shape[0],),
            in_specs=[any_spec, any_spec, any_spec, any_spec, any_spec],
            out_specs=any_spec,
            scratch_shapes=[pltpu.VMEM((X_SLOTS, TM_EXPERT * ROW_TILE, LANES), F32),
                            pltpu.VMEM((2, TM_EXPERT * ROW_TILE, LANES), F32),
                            pltpu.VMEM((W_SLOTS, D_MODEL, D_EXPERT), F32),
                            pltpu.VMEM((W_SLOTS, D_MODEL, D_EXPERT), F32),
                            pltpu.VMEM((W_SLOTS, D_EXPERT, D_MODEL), F32),
                            pltpu.VMEM((D_MODEL, D_EXPERT), BF16),
                            pltpu.VMEM((D_MODEL, D_EXPERT), BF16),
                            pltpu.VMEM((D_EXPERT, D_MODEL), BF16),
                            pltpu.SMEM((5,), jnp.int32),
                            pltpu.SemaphoreType.DMA((W_SLOTS,)),
                            pltpu.SemaphoreType.DMA((X_SLOTS,)),
                            pltpu.SemaphoreType.DMA((2,)),
                            pltpu.SemaphoreType.DMA]),
        out_shape=jax.ShapeDtypeStruct(xs.shape, F32),
        compiler_params=pltpu.CompilerParams(dimension_semantics=("arbitrary",),
                                             vmem_limit_bytes=VMEM_LIMIT),
        name="expert_mlp",
    )(tiles, chunk0, chunks, n_tiles, used_chunks, xs, wg, wu, wd, zeros)


def _combine_kernel(dest_ref, h_ref, rw_ref, fg_ref, y_ref, o_ref, buf, sems):
    tm = TM_COMBINE
    i = pl.program_id(0)
    n_steps = pl.num_programs(0)
    n = n_steps * tm
    cur = i % 2

    def fetch(step, half):
        def body(r, c):
            for s in range(2):
                pltpu.make_async_copy(_token_rows(y_ref, dest_ref[s * n + step * tm + r], 1),
                                      _token_rows(buf.at[half, s], r, 1),
                                      sems.at[half]).start(priority=s)
            return c

        lax.fori_loop(0, tm, body, 0, unroll=8)

    @pl.when(i == 0)
    def _():
        fetch(0, 0)

    @pl.when(i + 1 < n_steps)
    def _():
        fetch(i + 1, 1 - cur)

    for s in range(2):
        pltpu.make_async_copy(_token_rows(y_ref, 0, tm), buf.at[cur, s], sems.at[cur]).wait()
    rw = rw_ref[...]
    out = (h_ref[...] + rw[:, 0:1] * _tiles_to_rows(buf.at[cur, 0], tm)
           + rw[:, 1:2] * _tiles_to_rows(buf.at[cur, 1], tm))
    o_ref[...] = _rms(out, fg_ref[...])


def _combine(dest, h, rw, final_g, ys):
    n = h.shape[0]
    return pl.pallas_call(
        _combine_kernel,
        grid_spec=pltpu.PrefetchScalarGridSpec(
            num_scalar_prefetch=1,
            grid=(n // TM_COMBINE,),
            in_specs=[pl.BlockSpec((TM_COMBINE, D_MODEL), lambda i, d: (i, 0)),
                      pl.BlockSpec((TM_COMBINE, LANES), lambda i, d: (i, 0)),
                      pl.BlockSpec((1, D_MODEL), lambda i, d: (0, 0)),
                      pl.BlockSpec(memory_space=pl.ANY)],
            out_specs=pl.BlockSpec((TM_COMBINE, D_MODEL), lambda i, d: (i, 0)),
            scratch_shapes=[pltpu.VMEM((2, 2, TM_COMBINE * ROW_TILE, LANES), F32),
                            pltpu.SemaphoreType.DMA((2,))]),
        out_shape=jax.ShapeDtypeStruct((n, D_MODEL), F32),
        compiler_params=pltpu.CompilerParams(dimension_semantics=("arbitrary",),
                                             vmem_limit_bytes=VMEM_LIMIT),
        name="combine",
    )(dest, h, rw, final_g, ys)


def _schedule(counts, max_tiles):
    chunks = (counts + EXPERT_CHUNK - 1) // EXPERT_CHUNK
    chunk_end = jnp.cumsum(chunks)
    chunk_start = chunk_end - chunks
    tiles = (chunks + TILE_CHUNKS - 1) // TILE_CHUNKS
    tile_end = jnp.cumsum(tiles)
    tile = jnp.arange(max_tiles, dtype=jnp.int32)
    owner = jnp.sum(tile[:, None] >= tile_end[None, :], axis=1)
    is_owner = owner[:, None] == jnp.arange(N_EXPERTS, dtype=jnp.int32)[None, :]
    of_owner = lambda v: jnp.sum(jnp.where(is_owner, v[None, :], 0), axis=1)
    done = (tile - of_owner(tile_end - tiles)) * TILE_CHUNKS
    tile_chunk0 = (of_owner(chunk_start) + done).astype(jnp.int32)
    tile_chunks = jnp.clip(of_owner(chunks) - done, 0, TILE_CHUNKS).astype(jnp.int32)
    return tiles, chunk_start * EXPERT_CHUNK, tile_chunk0, tile_chunks, tile_end[-1:], chunk_end[-1:]


def _layer(x, attn_g, w_in, sg_g, w_sp, b_sp, sb_g, sg_out_g, w_out, ffn_g,
           w_rg, b_rg, w_re, b_re, w_gate, w_up, w_down):
    batch, seq, _ = x.shape
    n = batch * seq
    x2 = x.reshape(n, D_MODEL)
    row = lambda v: v.reshape(1, -1)

    bsp_full = jnp.repeat(b_sp.T, HEAD_DIM, axis=1)
    qkv, sgn = _inproj(x2, row(attn_g), w_in.astype(BF16), row(sg_g), w_sp, bsp_full, row(sg_out_g))
    sb = _attention(qkv, batch, seq).reshape(n, SB_WIDTH)

    pad_lanes = lambda v, width: jnp.pad(v, [(0, 0)] * (v.ndim - 1) + [(0, width - v.shape[-1])])
    w_r = jnp.concatenate([pad_lanes(w_rg, ROUTER_LANE0),
                           jnp.transpose(w_re, (1, 0, 2)).reshape(D_MODEL, N_EXPERTS)], axis=1)
    w_r = pad_lanes(w_r, LANES)
    wr_hi = w_r.astype(BF16)
    wr_lo = (w_r - wr_hi.astype(F32)).astype(BF16)
    wr2 = jnp.concatenate([wr_hi, wr_lo], axis=1)
    b_r = pad_lanes(jnp.concatenate([pad_lanes(b_rg, ROUTER_LANE0), b_re.reshape(-1)]), LANES)

    h, lg = _mix(sb, sgn, x2, row(sb_g), w_out.astype(BF16), row(ffn_g), wr2, row(b_r))
    ri, rw, cnt = _route(lg)

    counts = cnt[:, 0].astype(jnp.int32)
    n_rows = 2 * n + N_EXPERTS * EXPERT_CHUNK
    tiles, offsets, tile_chunk0, tile_chunks, n_tiles, used_chunks = _schedule(
        counts, 2 * n // TM_EXPERT + N_EXPERTS)
    expert, rank = ri[0:2], ri[2:4]
    is_e = expert[None] == jnp.arange(N_EXPERTS, dtype=jnp.int32)[:, None, None]
    dest = (jnp.sum(jnp.where(is_e, offsets[:, None, None], 0), axis=0) + rank).reshape(-1)
    pad_start = offsets + counts
    pad_count = (-counts) % EXPERT_CHUNK

    xs = _dispatch(dest, pad_start, pad_count, used_chunks, h, row(ffn_g), n_rows)
    ys = _experts(tiles, tile_chunk0, tile_chunks, n_tiles, used_chunks, xs,
                  w_gate.reshape(N_EXPERTS, D_MODEL, D_EXPERT),
                  w_up.reshape(N_EXPERTS, D_MODEL, D_EXPERT),
                  w_down.reshape(N_EXPERTS, D_EXPERT, D_MODEL))
    return dest, h, rw, ys


def kernel(x, attn_norm_g, w_in, sg_norm_g, w_spatial, b_spatial, sb_out_norm_g, sg_out_norm_g,
           w_out, ffn_norm_g, w_router_group, b_router_group, w_router_expert, b_router_expert,
           w_gate, w_up, w_down, final_norm_g):
    assert attn_norm_g.shape[0] == 1, "single-layer problem"
    batch, seq, _ = x.shape
    dest, h, rw, ys = _layer(x, attn_norm_g[0], w_in[0], sg_norm_g[0], w_spatial[0], b_spatial[0],
                             sb_out_norm_g[0], sg_out_norm_g[0], w_out[0], ffn_norm_g[0],
                             w_router_group[0], b_router_group[0], w_router_expert[0],
                             b_router_expert[0], w_gate[0], w_up[0], w_down[0])
    out = _combine(dest, h, rw, final_norm_g.reshape(1, -1), ys)
    return out.reshape(batch, seq, D_MODEL)
```

```python
import functools
import math

import jax
import jax.numpy as jnp
from jax import lax
from jax.experimental import pallas as pl
from jax.experimental.pallas import tpu as pltpu

D_MODEL = 1024
HEAD_DIM = 64
SB_WIDTH = 512
SG_WIDTH = 512
SG_HEADS = 8
D_IN = 3 * SB_WIDTH + 2 * SG_WIDTH
CHUNK = 128
N_GROUPS = 4
EXPERTS_PER_GROUP = 8
N_EXPERTS = N_GROUPS * EXPERTS_PER_GROUP
D_EXPERT = 512
EPS = 1e-6
F32_EXP_UNDERFLOW = 110.0

LANES = 128
SUBLANES = 8
ROW_TILE = D_MODEL // LANES
assert ROW_TILE == SUBLANES
HEAD_PAIR = 2 * HEAD_DIM
ROUTER_LANE0 = SUBLANES
ROUTER_ROWS = ROUTER_LANE0 + N_EXPERTS
assert EXPERTS_PER_GROUP == SUBLANES and N_GROUPS <= ROUTER_LANE0

TM_PROJ = 1024
TQ_ATTN = 256
ATTN_BLOCKS_PER_STEP = 2
ATTN_TOP_ROWS = (160, 176)
TM_MIX = 1024
TM_ROUTE = 1024
TM_DISPATCH = 1024
TM_EXPERT = 640
EXPERT_CHUNK = 128
TM_COMBINE = 512
VMEM_LIMIT = 48 * 1024 * 1024

F32 = jnp.float32
BF16 = jnp.bfloat16


def _rms(x, g):
    return x * lax.rsqrt(jnp.mean(x * x, axis=-1, keepdims=True) + EPS) * g


def _gelu(x):
    c = math.sqrt(2.0 / math.pi)
    return x * (0.5 * (1.0 + jnp.tanh(c * (x + 0.044715 * (x * x * x)))))


def _softplus(z):
    return jnp.maximum(z, 0.0) + jnp.log(1.0 + jnp.exp(-jnp.abs(z)))


def _dot(a, b):
    return jnp.dot(a, b, preferred_element_type=F32)


def _rows_to_tiles(ref, x):
    m = x.shape[0]
    for k in range(ROW_TILE):
        ref[pl.ds(k, m, stride=ROW_TILE), :] = x[:, k * LANES:(k + 1) * LANES]


def _tiles_to_rows(ref, m):
    return jnp.concatenate([ref[pl.ds(k, m, stride=ROW_TILE), :] for k in range(ROW_TILE)], axis=1)


def _token_rows(ref, first_token, n_tokens):
    return ref.at[pl.ds(pl.multiple_of(first_token * ROW_TILE, ROW_TILE), n_tokens * ROW_TILE)]


def _split_bf16(x):
    hi = x.astype(BF16)
    lo = (x - hi.astype(F32)).astype(BF16)
    return hi, lo


def _inproj_kernel(x_ref, g_ref, w_ref, sgg_ref, wsp_ref, bsp_ref, sgog_ref, qkv_ref, sgn_ref,
                   gu_ref, vgn_ref, sg_ref):
    tm = TM_PROJ
    hb = _rms(x_ref[...], g_ref[...]).astype(BF16)
    gv = _gelu(_dot(hb, w_ref[:, 3 * SB_WIDTH + SG_WIDTH:D_IN]))
    vgn_ref[...] = _rms(gv, sgg_ref[...]).astype(BF16)
    gu_ref[...] = _gelu(_dot(hb, w_ref[:, 3 * SB_WIDTH:3 * SB_WIDTH + SG_WIDTH]))
    q = _dot(hb, w_ref[:, 0:SB_WIDTH]) * (1.0 / math.sqrt(HEAD_DIM))
    qkv_ref[:, 0:SB_WIDTH] = q.astype(BF16)
    qkv_ref[:, SB_WIDTH:2 * SB_WIDTH] = _dot(hb, w_ref[:, SB_WIDTH:2 * SB_WIDTH]).astype(BF16)

    lane = lax.broadcasted_iota(jnp.int32, (1, LANES), 1)
    first = lane < HEAD_DIM
    zero = jnp.zeros((), BF16)
    r_c = lax.broadcasted_iota(jnp.int32, (CHUNK, CHUNK), 0)
    c_c = lax.broadcasted_iota(jnp.int32, (CHUNK, CHUNK), 1)
    tril = r_c >= c_c
    n_pairs = SG_WIDTH // HEAD_PAIR
    w_pairs = []
    for p in range(n_pairs):
        w0 = jnp.where(tril, wsp_ref[2 * p], 0.0).astype(BF16)
        w1 = jnp.where(tril, wsp_ref[2 * p + 1], 0.0).astype(BF16)
        w_pairs.append(jnp.concatenate([w0, w1], axis=1))
    bsp = bsp_ref[...]
    for c in range(tm // CHUNK):
        rows = slice(c * CHUNK, (c + 1) * CHUNK)
        for p in range(n_pairs):
            cols = slice(p * HEAD_PAIR, (p + 1) * HEAD_PAIR)
            vg = vgn_ref[rows, cols]
            rhs = jnp.concatenate([jnp.where(first, vg, zero), jnp.where(first, zero, vg)], axis=0)
            mixed = _dot(w_pairs[p], rhs) + bsp[:, cols]
            sg_ref[rows, cols] = gu_ref[rows, cols] * mixed
    qkv_ref[:, 2 * SB_WIDTH:3 * SB_WIDTH] = _dot(hb, w_ref[:, 2 * SB_WIDTH:3 * SB_WIDTH]).astype(BF16)
    sgn_ref[...] = _rms(sg_ref[...], sgog_ref[...]).astype(BF16)


def _inproj(x2, attn_g, w_in_b, sg_g, wsp, bsp_full, sg_out_g):
    n = x2.shape[0]
    row = lambda i: (i, 0)
    const = lambda i: (0, 0)
    return pl.pallas_call(
        _inproj_kernel,
        grid=(n // TM_PROJ,),
        in_specs=[pl.BlockSpec((TM_PROJ, D_MODEL), row),
                  pl.BlockSpec((1, D_MODEL), const),
                  pl.BlockSpec((D_MODEL, D_IN), const),
                  pl.BlockSpec((1, SG_WIDTH), const),
                  pl.BlockSpec((SG_HEADS, CHUNK, CHUNK), lambda i: (0, 0, 0)),
                  pl.BlockSpec((CHUNK, SG_WIDTH), const),
                  pl.BlockSpec((1, SG_WIDTH), const)],
        out_specs=[pl.BlockSpec((TM_PROJ, 3 * SB_WIDTH), row),
                   pl.BlockSpec((TM_PROJ, SG_WIDTH), row)],
        out_shape=[jax.ShapeDtypeStruct((n, 3 * SB_WIDTH), BF16),
                   jax.ShapeDtypeStruct((n, SG_WIDTH), BF16)],
        scratch_shapes=[pltpu.VMEM((TM_PROJ, SG_WIDTH), F32),
                        pltpu.VMEM((TM_PROJ, SG_WIDTH), BF16),
                        pltpu.VMEM((TM_PROJ, SG_WIDTH), F32)],
        compiler_params=pltpu.CompilerParams(dimension_semantics=("arbitrary",),
                                             vmem_limit_bytes=VMEM_LIMIT),
        name="inproj",
    )(x2, attn_g, w_in_b, sg_g, wsp, bsp_full, sg_out_g)


def _attn_kernel(q_ref, k_ref, v_ref, o_ref, q2_ref, carry_ref):
    t = TQ_ATTN
    n_pairs = SB_WIDTH // HEAD_PAIR
    lane = lax.broadcasted_iota(jnp.int32, (1, HEAD_PAIR), 1)
    head_lanes = (lane < HEAD_DIM, lane >= HEAD_DIM)
    zero = jnp.zeros((), BF16)
    r_idx = lax.broadcasted_iota(jnp.int32, (t, t), 0)
    c_idx = lax.broadcasted_iota(jnp.int32, (t, t), 1)
    suffix = (r_idx > c_idx).astype(BF16)
    suffix2 = jnp.concatenate([suffix, suffix], axis=0)
    causal = c_idx < r_idx

    def one_query_block(sub, c):
        qi = pl.program_id(1) * ATTN_BLOCKS_PER_STEP + sub
        row0 = pl.multiple_of(sub * t, t)
        for p in range(n_pairs):
            qp = q_ref[0, pl.ds(row0, t), p * HEAD_PAIR:(p + 1) * HEAD_PAIR]
            for h in range(2):
                q2_ref[(2 * p + h) * t:(2 * p + h + 1) * t, :] = jnp.where(head_lanes[h], qp, zero)
        o_ref[0, pl.ds(row0, t), :] = jnp.zeros((t, SB_WIDTH), F32)
        carry_ref[...] = jnp.zeros_like(carry_ref)

        def block(j, diag, m):
            start = pl.multiple_of(j * t, t)
            mask2 = jnp.concatenate([causal, causal], axis=0) if diag else None
            st = [dict() for _ in range(n_pairs)]

            def head_rows(p):
                return [slice((2 * p + h) * t, (2 * p + h) * t + m) for h in range(2)]

            def scores(p):
                d = st[p]
                d["cols"] = slice(p * HEAD_PAIR, (p + 1) * HEAD_PAIR)
                kb = k_ref[0, pl.ds(start, t), d["cols"]]
                q2 = jnp.concatenate([q2_ref[r, :] for r in head_rows(p)], axis=0)
                z = lax.dot_general(q2, kb, (((1,), (1,)), ((), ())),
                                    preferred_element_type=F32)
                sp = _softplus(z)
                nl = jnp.where(mask2, sp, 0.0) if diag else sp
                hi, lo = _split_bf16(nl)
                d["hl"] = jnp.concatenate([hi, lo], axis=1)
                d["log_beta"] = z - sp
                d["nl0"] = nl[:, 0:1]

            def weights(p):
                d = st[p]
                hl = d["hl"]
                after = jnp.concatenate([_dot(hl[0:m], suffix2), _dot(hl[m:2 * m], suffix2)], axis=0)
                carry = jnp.concatenate([carry_ref[r, :] for r in head_rows(p)], axis=0)
                a = jnp.exp(d["log_beta"] - after - carry)
                if diag:
                    a = jnp.where(mask2, a, 0.0)
                a = a.astype(BF16)
                d["a2"] = jnp.concatenate([a[0:m], a[m:2 * m]], axis=1)
                new_carry = carry + after[:, 0:1] + d["nl0"]
                for h, r in enumerate(head_rows(p)):
                    carry_ref[r, :] = new_carry[h * m:(h + 1) * m]

            def values(p):
                d = st[p]
                vb = v_ref[0, pl.ds(start, t), d["cols"]]
                v2 = jnp.concatenate([jnp.where(head_lanes[0], vb, zero),
                                      jnp.where(head_lanes[1], vb, zero)], axis=0)
                o_ref[0, pl.ds(row0, m), d["cols"]] += _dot(d["a2"], v2)

            for step in range(n_pairs + 2):
                if step < n_pairs:
                    scores(step)
                if 0 <= step - 1 < n_pairs:
                    weights(step - 1)
                if 0 <= step - 2 < n_pairs:
                    values(step - 2)

        def flags():
            bounds = (0,) + ATTN_TOP_ROWS + (t,)
            lowest = [jnp.min(jnp.concatenate([carry_ref[hh * t + lo:hh * t + hi, :] for hh in range(2 * n_pairs)],
                                              axis=0))
                      for lo, hi in zip(bounds[:-1], bounds[1:])]
            below = [functools.reduce(jnp.minimum, lowest[k:]) for k in range(len(lowest))]
            return (below[0] < F32_EXP_UNDERFLOW,) + tuple(b >= F32_EXP_UNDERFLOW for b in below[1:])

        block(qi, True, t)

        def body(state):
            it, _, *done = state
            j = qi - 1 - it
            for k, m in enumerate(ATTN_TOP_ROWS + (t,)):
                use = done[k] if k < len(done) else True
                if k > 0:
                    use = jnp.logical_and(use, jnp.logical_not(done[k - 1]))

                @pl.when(use)
                def _(m=m):
                    block(j, False, m)

            return (it + 1,) + flags()

        lax.while_loop(lambda s: (s[0] < qi) & s[1], body, (jnp.int32(0),) + flags())
        return c

    lax.fori_loop(0, ATTN_BLOCKS_PER_STEP, one_query_block, 0)


def _attention(qkv, batch, seq):
    qkv3 = qkv.reshape(batch, seq, 3 * SB_WIDTH)
    n_heads = SB_WIDTH // HEAD_DIM
    return pl.pallas_call(
        _attn_kernel,
        grid=(batch, seq // (ATTN_BLOCKS_PER_STEP * TQ_ATTN)),
        in_specs=[pl.BlockSpec((1, ATTN_BLOCKS_PER_STEP * TQ_ATTN, SB_WIDTH), lambda b, i: (b, i, 0)),
                  pl.BlockSpec((1, seq, SB_WIDTH), lambda b, i: (b, 0, 1)),
                  pl.BlockSpec((1, seq, SB_WIDTH), lambda b, i: (b, 0, 2))],
        out_specs=pl.BlockSpec((1, ATTN_BLOCKS_PER_STEP * TQ_ATTN, SB_WIDTH), lambda b, i: (b, i, 0)),
        out_shape=jax.ShapeDtypeStruct((batch, seq, SB_WIDTH), F32),
        scratch_shapes=[pltpu.VMEM((n_heads * TQ_ATTN, HEAD_PAIR), BF16),
                        pltpu.VMEM((n_heads * TQ_ATTN, 1), F32)],
        compiler_params=pltpu.CompilerParams(dimension_semantics=("arbitrary",) * 2,
                                             vmem_limit_bytes=VMEM_LIMIT),
        name="sb_attention",
    )(qkv3, qkv3, qkv3)


def _mix_kernel(sb_ref, sgn_ref, x_ref, sbg_ref, wout_ref, ffng_ref, wr2_ref, br_ref,
                h_ref, lg_ref):
    sbn = _rms(sb_ref[...], sbg_ref[...]).astype(BF16)
    h = x_ref[...] + _dot(sbn, wout_ref[0:SB_WIDTH, :]) + _dot(sgn_ref[...], wout_ref[SB_WIDTH:, :])
    h_ref[...] = h
    hn = _rms(h, ffng_ref[...])

    hn_hi, hn_lo = _split_bf16(hn)
    both = _dot(hn_hi, wr2_ref[...])
    logits = both[:, 0:LANES] + both[:, LANES:] + _dot(hn_lo, wr2_ref[:, 0:LANES]) + br_ref[...]
    lg_ref[...] = logits.T[0:ROUTER_ROWS, :]


def _route_kernel(lg_ref, ri_ref, rw_ref, cnt_ref, count_ref):
    tr = TM_ROUTE
    i = pl.program_id(0)

    @pl.when(i == 0)
    def _():
        count_ref[...] = jnp.zeros_like(count_ref)

    neg = jnp.float32(-jnp.inf)
    row8 = lax.broadcasted_iota(jnp.int32, (SUBLANES, tr), 0)

    def top(v):
        m = jnp.max(v, axis=0, keepdims=True)
        return m, jnp.min(jnp.where(v == m, row8, SUBLANES), axis=0, keepdims=True)

    def group_rows(g):
        return lg_ref[ROUTER_LANE0 + g * EXPERTS_PER_GROUP:ROUTER_LANE0 + (g + 1) * EXPERTS_PER_GROUP, :]

    gl = jnp.where(row8 < N_GROUPS, lg_ref[0:SUBLANES, :], neg)
    gmax, gidx = top(gl)
    gweight = 1.0 / jnp.sum(jnp.exp(gl - gmax), axis=0, keepdims=True)
    el = group_rows(0)
    for g in range(1, N_GROUPS):
        el = jnp.where(gidx == g, group_rows(g), el)
    m1, i1 = top(el)
    m2, i2 = top(jnp.where(row8 == i1, neg, el))
    t21 = jnp.exp(m2 - m1)
    w1 = gweight / (1.0 + t21)
    w2 = gweight * t21 / (1.0 + t21)
    e1 = gidx * EXPERTS_PER_GROUP + i1
    e2 = gidx * EXPERTS_PER_GROUP + i2

    row_e = lax.broadcasted_iota(jnp.int32, (N_EXPERTS, tr), 0)
    sel1 = row_e == e1
    sel2 = row_e == e2
    onehot = jnp.where(sel1 | sel2, 1.0, 0.0)
    r_t = lax.broadcasted_iota(jnp.int32, (tr, tr), 0)
    c_t = lax.broadcasted_iota(jnp.int32, (tr, tr), 1)
    before = (r_t < c_t).astype(BF16)
    running = count_ref[:, 0:1] + _dot(onehot.astype(BF16), before)
    rank1 = jnp.sum(jnp.where(sel1, running, 0.0), axis=0, keepdims=True)
    rank2 = jnp.sum(jnp.where(sel2, running, 0.0), axis=0, keepdims=True)
    new_count = count_ref[:, 0:1] + jnp.sum(onehot, axis=1, keepdims=True)
    count_ref[...] = jnp.broadcast_to(new_count, count_ref.shape)
    cnt_ref[...] = jnp.broadcast_to(new_count, cnt_ref.shape)

    ri_ref[...] = jnp.where(row8 == 0, e1, jnp.where(row8 == 1, e2, jnp.where(
        row8 == 2, rank1.astype(jnp.int32), jnp.where(row8 == 3, rank2.astype(jnp.int32), 0))))
    row128 = lax.broadcasted_iota(jnp.int32, (LANES, tr), 0)
    rw_ref[...] = jnp.where(row128 == 0, w1, jnp.where(row128 == 1, w2, 0.0)).T


def _route(lg):
    n = lg.shape[1]
    return pl.pallas_call(
        _route_kernel,
        grid=(n // TM_ROUTE,),
        in_specs=[pl.BlockSpec((ROUTER_ROWS, TM_ROUTE), lambda i: (0, i))],
        out_specs=[pl.BlockSpec((SUBLANES, TM_ROUTE), lambda i: (0, i)),
                   pl.BlockSpec((TM_ROUTE, LANES), lambda i: (i, 0)),
                   pl.BlockSpec((N_EXPERTS, LANES), lambda i: (0, 0))],
        out_shape=[jax.ShapeDtypeStruct((SUBLANES, n), jnp.int32),
                   jax.ShapeDtypeStruct((n, LANES), F32),
                   jax.ShapeDtypeStruct((N_EXPERTS, LANES), F32)],
        scratch_shapes=[pltpu.VMEM((N_EXPERTS, LANES), F32)],
        compiler_params=pltpu.CompilerParams(dimension_semantics=("arbitrary",),
                                             vmem_limit_bytes=VMEM_LIMIT),
        name="route",
    )(lg)


def _mix(sb, sgn, x2, sb_g, w_out_b, ffn_g, wr2, br):
    n = x2.shape[0]
    row = lambda i: (i, 0)
    const = lambda i: (0, 0)
    return pl.pallas_call(
        _mix_kernel,
        grid=(n // TM_MIX,),
        in_specs=[pl.BlockSpec((TM_MIX, SB_WIDTH), row),
                  pl.BlockSpec((TM_MIX, SG_WIDTH), row),
                  pl.BlockSpec((TM_MIX, D_MODEL), row),
                  pl.BlockSpec((1, SB_WIDTH), const),
                  pl.BlockSpec((D_MODEL, D_MODEL), const),
                  pl.BlockSpec((1, D_MODEL), const),
                  pl.BlockSpec((D_MODEL, 2 * LANES), const),
                  pl.BlockSpec((1, LANES), const)],
        out_specs=[pl.BlockSpec((TM_MIX, D_MODEL), row),
                   pl.BlockSpec((ROUTER_ROWS, TM_MIX), lambda i: (0, i))],
        out_shape=[jax.ShapeDtypeStruct((n, D_MODEL), F32),
                   jax.ShapeDtypeStruct((ROUTER_ROWS, n), F32)],
        compiler_params=pltpu.CompilerParams(dimension_semantics=("arbitrary",),
                                             vmem_limit_bytes=VMEM_LIMIT),
        name="mix_router",
    )(sb, sgn, x2, sb_g, w_out_b, ffn_g, wr2, br)


_PAD_BITS = tuple(1 << b for b in reversed(range(EXPERT_CHUNK.bit_length() - 1)))


def _dispatch_kernel(dest_ref, pad_start_ref, pad_count_ref, used_ref, h_ref, g_ref, zeros_ref, xs_ref,
                     hn_ref, sem, zsem):
    tm = TM_DISPATCH
    i = pl.program_id(0)
    n_steps = pl.num_programs(0) - 1
    n = n_steps * tm
    base = (i - 1) * tm
    prev = hn_ref.at[lax.rem(i + 1, 2)]
    n_chunks = xs_ref.shape[0] // (EXPERT_CHUNK * ROW_TILE)

    def pad_copies(do):
        for e in range(N_EXPERTS):
            start = pad_start_ref[e]
            count = pad_count_ref[e]
            for bit in _PAD_BITS:
                @pl.when((count & bit) != 0)
                def _(start=start, bit=bit):
                    do(pltpu.make_async_copy(_token_rows(zeros_ref, 0, bit),
                                             _token_rows(xs_ref, start, bit), zsem))
                start = start + (count & bit)
        for k in range(N_EXPERTS):
            chunk = used_ref[0] + k

            @pl.when(chunk < n_chunks)
            def _(chunk=chunk):
                do(pltpu.make_async_copy(zeros_ref, _token_rows(xs_ref, chunk * EXPERT_CHUNK, EXPERT_CHUNK),
                                         zsem))

    @pl.when(i == 0)
    def _():
        pad_copies(lambda cp: cp.start())

    @pl.when(i > 0)
    def _():
        def body(r, c):
            src = _token_rows(prev, r, 1)
            for s in range(2):
                pltpu.make_async_copy(src, _token_rows(xs_ref, dest_ref[s * n + base + r], 1),
                                      sem).start(priority=s)
            return c

        lax.fori_loop(0, tm, body, 0, unroll=8)

    @pl.when(i < n_steps)
    def _():
        _rows_to_tiles(hn_ref.at[lax.rem(i, 2)], _rms(h_ref[...], g_ref[...]))

    @pl.when(i > 0)
    def _():
        for _ in range(2):
            pltpu.make_async_copy(prev, _token_rows(xs_ref, 0, tm), sem).wait()

    @pl.when(i == n_steps)
    def _():
        pad_copies(lambda cp: cp.wait())


def _dispatch(dest, pad_start, pad_count, used_chunks, h, ffn_g, n_rows):
    n_steps = h.shape[0] // TM_DISPATCH
    zeros = jnp.zeros((EXPERT_CHUNK * ROW_TILE, LANES), F32)
    return pl.pallas_call(
        _dispatch_kernel,
        grid_spec=pltpu.PrefetchScalarGridSpec(
            num_scalar_prefetch=4,
            grid=(n_steps + 1,),
            in_specs=[pl.BlockSpec((TM_DISPATCH, D_MODEL), lambda i, *_: (jnp.minimum(i, n_steps - 1), 0)),
                      pl.BlockSpec((1, D_MODEL), lambda i, *_: (0, 0)),
                      pl.BlockSpec(memory_space=pl.ANY)],
            out_specs=pl.BlockSpec(memory_space=pl.ANY),
            scratch_shapes=[pltpu.VMEM((2, TM_DISPATCH * ROW_TILE, LANES), F32),
                            pltpu.SemaphoreType.DMA, pltpu.SemaphoreType.DMA]),
        out_shape=jax.ShapeDtypeStruct((n_rows * ROW_TILE, LANES), F32),
        compiler_params=pltpu.CompilerParams(dimension_semantics=("arbitrary",),
                                             vmem_limit_bytes=VMEM_LIMIT),
        name="dispatch",
    )(dest, pad_start, pad_count, used_chunks, h, ffn_g, zeros)


X_SLOTS = 3
TILE_CHUNKS = TM_EXPERT // EXPERT_CHUNK
W_SLOTS = 3


def _expert_kernel(tiles_ref, chunk0_ref, chunks_ref, nt_ref, used_ref, xs_ref, wg_ref, wu_ref, wd_ref,
                   zeros_ref, ys_ref, x_buf, y_buf, sg_buf, su_buf, sd_buf, wgb, wub, wdb, state,
                   w_sems, x_sems, y_sems, zsem):
    t = pl.program_id(0)
    last = pl.num_programs(0) - 1
    nt = nt_ref[0]
    n_chunks = ys_ref.shape[0] // (EXPERT_CHUNK * ROW_TILE)

    def tile_copies(tile, do, out):
        for c in range(TILE_CHUNKS):
            @pl.when(c < chunks_ref[tile])
            def _(c=c):
                first = (chunk0_ref[tile] + c) * EXPERT_CHUNK
                if out:
                    slot = lax.rem(tile, 2)
                    do(pltpu.make_async_copy(_token_rows(y_buf.at[slot], c * EXPERT_CHUNK, EXPERT_CHUNK),
                                             _token_rows(ys_ref, first, EXPERT_CHUNK), y_sems.at[slot]))
                else:
                    slot = lax.rem(tile, X_SLOTS)
                    do(pltpu.make_async_copy(_token_rows(xs_ref, first, EXPERT_CHUNK),
                                             _token_rows(x_buf.at[slot], c * EXPERT_CHUNK, EXPERT_CHUNK),
                                             x_sems.at[slot]))

    start = lambda cp: cp.start()
    wait = lambda cp: cp.wait()

    def tail_copies(do):
        for k in range(N_EXPERTS):
            chunk = used_ref[0] + k

            @pl.when(chunk < n_chunks)
            def _(chunk=chunk):
                do(pltpu.make_async_copy(zeros_ref, _token_rows(ys_ref, chunk * EXPERT_CHUNK, EXPERT_CHUNK),
                                         zsem))

    def weight_copies(e, slot):
        return (pltpu.make_async_copy(wg_ref.at[e], sg_buf.at[slot], w_sems.at[slot]),
                pltpu.make_async_copy(wu_ref.at[e], su_buf.at[slot], w_sems.at[slot]),
                pltpu.make_async_copy(wd_ref.at[e], sd_buf.at[slot], w_sems.at[slot]))

    def next_with_rows(e):
        return lax.while_loop(lambda k: (k < N_EXPERTS) & (tiles_ref[jnp.minimum(k, N_EXPERTS - 1)] == 0),
                              lambda k: k + 1, e + 1)

    @pl.when(t == 0)
    def _():
        first = next_with_rows(jnp.int32(-1))
        second = next_with_rows(first)
        state[0] = jnp.int32(-1)
        state[1] = jnp.int32(0)
        state[2] = jnp.int32(W_SLOTS - 1)
        state[3] = first
        state[4] = second
        for cp in weight_copies(first, 0):
            cp.start()

        @pl.when(second < N_EXPERTS)
        def _():
            for cp in weight_copies(second, 1):
                cp.start()

        tile_copies(0, start, False)

        @pl.when(nt > 1)
        def _():
            tile_copies(1, start, False)

        tail_copies(start)

    @pl.when(t + 2 < nt)
    def _():
        tile_copies(t + 2, start, False)

    @pl.when(t < nt)
    def _():
        @pl.when(state[1] == 0)
        def _():
            e = state[3]
            nxt = state[4]
            slot = lax.rem(state[2] + 1, W_SLOTS)
            after_next = next_with_rows(nxt)
            state[0] = e
            state[1] = tiles_ref[e]
            state[2] = slot
            state[3] = nxt
            state[4] = after_next
            for cp in weight_copies(e, slot):
                cp.wait()

            @pl.when(after_next < N_EXPERTS)
            def _():
                for cp in weight_copies(after_next, lax.rem(slot + 2, W_SLOTS)):
                    cp.start()

            wgb[...] = sg_buf[slot].astype(BF16)
            wub[...] = su_buf[slot].astype(BF16)
            wdb[...] = sd_buf[slot].astype(BF16)

        state[1] = state[1] - 1
        tile_copies(t, wait, False)

        @pl.when(t >= 2)
        def _():
            tile_copies(t - 2, wait, True)

        for n_chunks_here in range(1, TILE_CHUNKS + 1):
            @pl.when(chunks_ref[t] == n_chunks_here)
            def _(m=n_chunks_here * EXPERT_CHUNK):
                x = _tiles_to_rows(x_buf.at[lax.rem(t, X_SLOTS)], m).astype(BF16)
                g = _dot(x, wgb[...])
                u = _dot(x, wub[...])
                hidden = (g * jax.nn.sigmoid(g)) * u
                _rows_to_tiles(y_buf.at[lax.rem(t, 2)], _dot(hidden.astype(BF16), wdb[...]))

        tile_copies(t, start, True)

    @pl.when(t == last)
    def _():
        for back in (2, 1):
            @pl.when(nt >= back)
            def _(back=back):
                tile_copies(nt - back, wait, True)

        tail_copies(wait)


def _experts(tiles, chunk0, chunks, n_tiles, used_chunks, xs, wg, wu, wd):
    any_spec = pl.BlockSpec(memory_space=pl.ANY)
    zeros = jnp.zeros((EXPERT_CHUNK * ROW_TILE, LANES), F32)
    return pl.pallas_call(
        _expert_kernel,
        grid_spec=pltpu.PrefetchScalarGridSpec(
            num_scalar_prefetch=5,
            grid=(chunks.shape[0],),
            in_specs=[any_spec, any_spec, any_spec, any_spec, any_spec],
            out_specs=any_spec,
            scratch_shapes=[pltpu.VMEM((X_SLOTS, TM_EXPERT * ROW_TILE, LANES), F32),
                            pltpu.VMEM((2, TM_EXPERT * ROW_TILE, LANES), F32),
                            pltpu.VMEM((W_SLOTS, D_MODEL, D_EXPERT), F32),
                            pltpu.VMEM((W_SLOTS, D_MODEL, D_EXPERT), F32),
                            pltpu.VMEM((W_SLOTS, D_EXPERT, D_MODEL), F32),
                            pltpu.VMEM((D_MODEL, D_EXPERT), BF16),
                            pltpu.VMEM((D_MODEL, D_EXPERT), BF16),
                            pltpu.VMEM((D_EXPERT, D_MODEL), BF16),
                            pltpu.SMEM((5,), jnp.int32),
                            pltpu.SemaphoreType.DMA((W_SLOTS,)),
                            pltpu.SemaphoreType.DMA((X_SLOTS,)),
                            pltpu.SemaphoreType.DMA((2,)),
                            pltpu.SemaphoreType.DMA]),
        out_shape=jax.ShapeDtypeStruct(xs.shape, F32),
        compiler_params=pltpu.CompilerParams(dimension_semantics=("arbitrary",),
                                             vmem_limit_bytes=VMEM_LIMIT),
        name="expert_mlp",
    )(tiles, chunk0, chunks, n_tiles, used_chunks, xs, wg, wu, wd, zeros)


def _combine_kernel(dest_ref, rw_ref, fg_ref, h_ref, y_ref, o_ref, buf, h_buf, sems, h_sems):
    tm = TM_COMBINE
    i = pl.program_id(0)
    n_steps = pl.num_programs(0)
    n = n_steps * tm
    cur = i % 2

    def h_copy(step, half):
        return pltpu.make_async_copy(h_ref.at[pl.ds(pl.multiple_of(step * tm, tm), tm)], h_buf.at[half],
                                     h_sems.at[half])

    def fetch(step, half):
        h_copy(step, half).start(priority=1)

        def body(r, c):
            for s in range(2):
                pltpu.make_async_copy(_token_rows(y_ref, dest_ref[s * n + step * tm + r], 1),
                                      _token_rows(buf.at[half, s], r, 1),
                                      sems.at[half]).start()
            return c

        lax.fori_loop(0, tm, body, 0, unroll=8)

    @pl.when(i == 0)
    def _():
        fetch(0, 0)

    @pl.when(i + 1 < n_steps)
    def _():
        fetch(i + 1, 1 - cur)

    for s in range(2):
        pltpu.make_async_copy(_token_rows(y_ref, 0, tm), buf.at[cur, s], sems.at[cur]).wait()
    h_copy(i, cur).wait()
    rw = rw_ref[...]
    out = (h_buf[cur] + rw[:, 0:1] * _tiles_to_rows(buf.at[cur, 0], tm)
           + rw[:, 1:2] * _tiles_to_rows(buf.at[cur, 1], tm))
    o_ref[...] = _rms(out, fg_ref[...])


def _combine(dest, h, rw, final_g, ys):
    n = h.shape[0]
    return pl.pallas_call(
        _combine_kernel,
        grid_spec=pltpu.PrefetchScalarGridSpec(
            num_scalar_prefetch=1,
            grid=(n // TM_COMBINE,),
            in_specs=[pl.BlockSpec((TM_COMBINE, LANES), lambda i, d: (i, 0)),
                      pl.BlockSpec((1, D_MODEL), lambda i, d: (0, 0)),
                      pl.BlockSpec(memory_space=pl.ANY),
                      pl.BlockSpec(memory_space=pl.ANY)],
            out_specs=pl.BlockSpec((TM_COMBINE, D_MODEL), lambda i, d: (i, 0)),
            scratch_shapes=[pltpu.VMEM((2, 2, TM_COMBINE * ROW_TILE, LANES), F32),
                            pltpu.VMEM((2, TM_COMBINE, D_MODEL), F32),
                            pltpu.SemaphoreType.DMA((2,)),
                            pltpu.SemaphoreType.DMA((2,))]),
        out_shape=jax.ShapeDtypeStruct((n, D_MODEL), F32),
        compiler_params=pltpu.CompilerParams(dimension_semantics=("arbitrary",),
                                             vmem_limit_bytes=VMEM_LIMIT),
        name="combine",
    )(dest, rw, final_g, h, ys)


def _schedule(counts, max_tiles):
    chunks = (counts + EXPERT_CHUNK - 1) // EXPERT_CHUNK
    chunk_end = jnp.cumsum(chunks)
    chunk_start = chunk_end - chunks
    tiles = (chunks + TILE_CHUNKS - 1) // TILE_CHUNKS
    tile_end = jnp.cumsum(tiles)
    tile = jnp.arange(max_tiles, dtype=jnp.int32)
    owner = jnp.sum(tile[:, None] >= tile_end[None, :], axis=1)
    is_owner = owner[:, None] == jnp.arange(N_EXPERTS, dtype=jnp.int32)[None, :]
    of_owner = lambda v: jnp.sum(jnp.where(is_owner, v[None, :], 0), axis=1)
    done = (tile - of_owner(tile_end - tiles)) * TILE_CHUNKS
    tile_chunk0 = (of_owner(chunk_start) + done).astype(jnp.int32)
    tile_chunks = jnp.clip(of_owner(chunks) - done, 0, TILE_CHUNKS).astype(jnp.int32)
    return tiles, chunk_start * EXPERT_CHUNK, tile_chunk0, tile_chunks, tile_end[-1:], chunk_end[-1:]


def _layer(x, attn_g, w_in, sg_g, w_sp, b_sp, sb_g, sg_out_g, w_out, ffn_g,
           w_rg, b_rg, w_re, b_re, w_gate, w_up, w_down):
    batch, seq, _ = x.shape
    n = batch * seq
    x2 = x.reshape(n, D_MODEL)
    row = lambda v: v.reshape(1, -1)

    bsp_full = jnp.repeat(b_sp.T, HEAD_DIM, axis=1)
    qkv, sgn = _inproj(x2, row(attn_g), w_in.astype(BF16), row(sg_g), w_sp, bsp_full, row(sg_out_g))
    sb = _attention(qkv, batch, seq).reshape(n, SB_WIDTH)

    pad_lanes = lambda v, width: jnp.pad(v, [(0, 0)] * (v.ndim - 1) + [(0, width - v.shape[-1])])
    w_r = jnp.concatenate([pad_lanes(w_rg, ROUTER_LANE0),
                           jnp.transpose(w_re, (1, 0, 2)).reshape(D_MODEL, N_EXPERTS)], axis=1)
    w_r = pad_lanes(w_r, LANES)
    wr_hi = w_r.astype(BF16)
    wr_lo = (w_r - wr_hi.astype(F32)).astype(BF16)
    wr2 = jnp.concatenate([wr_hi, wr_lo], axis=1)
    b_r = pad_lanes(jnp.concatenate([pad_lanes(b_rg, ROUTER_LANE0), b_re.reshape(-1)]), LANES)

    h, lg = _mix(sb, sgn, x2, row(sb_g), w_out.astype(BF16), row(ffn_g), wr2, row(b_r))
    ri, rw, cnt = _route(lg)

    counts = cnt[:, 0].astype(jnp.int32)
    n_rows = 2 * n + N_EXPERTS * EXPERT_CHUNK
    tiles, offsets, tile_chunk0, tile_chunks, n_tiles, used_chunks = _schedule(
        counts, 2 * n // TM_EXPERT + N_EXPERTS)
    expert, rank = ri[0:2], ri[2:4]
    is_e = expert[None] == jnp.arange(N_EXPERTS, dtype=jnp.int32)[:, None, None]
    dest = (jnp.sum(jnp.where(is_e, offsets[:, None, None], 0), axis=0) + rank).reshape(-1)
    pad_start = offsets + counts
    pad_count = (-counts) % EXPERT_CHUNK

    xs = _dispatch(dest, pad_start, pad_count, used_chunks, h, row(ffn_g), n_rows)
    ys = _experts(tiles, tile_chunk0, tile_chunks, n_tiles, used_chunks, xs,
                  w_gate.reshape(N_EXPERTS, D_MODEL, D_EXPERT),
                  w_up.reshape(N_EXPERTS, D_MODEL, D_EXPERT),
                  w_down.reshape(N_EXPERTS, D_EXPERT, D_MODEL))
    return dest, h, rw, ys


def kernel(x, attn_norm_g, w_in, sg_norm_g, w_spatial, b_spatial, sb_out_norm_g, sg_out_norm_g,
           w_out, ffn_norm_g, w_router_group, b_router_group, w_router_expert, b_router_expert,
           w_gate, w_up, w_down, final_norm_g):
    assert attn_norm_g.shape[0] == 1, "single-layer problem"
    batch, seq, _ = x.shape
    dest, h, rw, ys = _layer(x, attn_norm_g[0], w_in[0], sg_norm_g[0], w_spatial[0], b_spatial[0],
                             sb_out_norm_g[0], sg_out_norm_g[0], w_out[0], ffn_norm_g[0],
                             w_router_group[0], b_router_group[0], w_router_expert[0],
                             b_router_expert[0], w_gate[0], w_up[0], w_down[0])
    out = _combine(dest, h, rw, final_norm_g.reshape(1, -1), ys)
    return out.reshape(batch, seq, D_MODEL)
```

```python
import functools
import math

import jax
import jax.numpy as jnp
from jax import lax
from jax.experimental import pallas as pl
from jax.experimental.pallas import tpu as pltpu

D_MODEL = 1024
HEAD_DIM = 64
SB_WIDTH = 512
SG_WIDTH = 512
SG_HEADS = 8
D_IN = 3 * SB_WIDTH + 2 * SG_WIDTH
CHUNK = 128
N_GROUPS = 4
EXPERTS_PER_GROUP = 8
N_EXPERTS = N_GROUPS * EXPERTS_PER_GROUP
D_EXPERT = 512
EPS = 1e-6
F32_EXP_UNDERFLOW = 110.0

LANES = 128
SUBLANES = 8
ROW_TILE = D_MODEL // LANES
assert ROW_TILE == SUBLANES
HEAD_PAIR = 2 * HEAD_DIM
ROUTER_LANE0 = SUBLANES
ROUTER_ROWS = ROUTER_LANE0 + N_EXPERTS
assert EXPERTS_PER_GROUP == SUBLANES and N_GROUPS <= ROUTER_LANE0

TM_PROJ = 1024
TQ_ATTN = 256
ATTN_BLOCKS_PER_STEP = 2
ATTN_TOP_ROWS = (160, 176)
TM_MIX = 1024
TM_ROUTE = 1024
TM_DISPATCH = 1024
TM_EXPERT = 640
EXPERT_CHUNK = 128
TM_COMBINE = 512
VMEM_LIMIT = 48 * 1024 * 1024

F32 = jnp.float32
BF16 = jnp.bfloat16


def _rms(x, g):
    return x * lax.rsqrt(jnp.mean(x * x, axis=-1, keepdims=True) + EPS) * g


def _gelu(x):
    c = math.sqrt(2.0 / math.pi)
    return x * (0.5 * (1.0 + jnp.tanh(c * (x + 0.044715 * (x * x * x)))))


def _softplus(z):
    return jnp.maximum(z, 0.0) + jnp.log(1.0 + jnp.exp(-jnp.abs(z)))


def _dot(a, b):
    return jnp.dot(a, b, preferred_element_type=F32)


def _rows_to_tiles(ref, x):
    m = x.shape[0]
    for k in range(ROW_TILE):
        ref[pl.ds(k, m, stride=ROW_TILE), :] = x[:, k * LANES:(k + 1) * LANES]


def _tiles_to_rows(ref, m):
    return jnp.concatenate([ref[pl.ds(k, m, stride=ROW_TILE), :] for k in range(ROW_TILE)], axis=1)


def _token_rows(ref, first_token, n_tokens):
    return ref.at[pl.ds(pl.multiple_of(first_token * ROW_TILE, ROW_TILE), n_tokens * ROW_TILE)]


def _split_bf16(x):
    hi = x.astype(BF16)
    lo = (x - hi.astype(F32)).astype(BF16)
    return hi, lo


def _inproj_kernel(x_ref, g_ref, w_ref, sgg_ref, wsp_ref, bsp_ref, sgog_ref, qkv_ref, sgn_ref,
                   gu_ref, vgn_ref, sg_ref):
    tm = TM_PROJ
    hb = _rms(x_ref[...], g_ref[...]).astype(BF16)
    gv = _gelu(_dot(hb, w_ref[:, 3 * SB_WIDTH + SG_WIDTH:D_IN]))
    vgn_ref[...] = _rms(gv, sgg_ref[...]).astype(BF16)
    gu_ref[...] = _gelu(_dot(hb, w_ref[:, 3 * SB_WIDTH:3 * SB_WIDTH + SG_WIDTH]))
    q = _dot(hb, w_ref[:, 0:SB_WIDTH]) * (1.0 / math.sqrt(HEAD_DIM))
    qkv_ref[:, 0:SB_WIDTH] = q.astype(BF16)
    qkv_ref[:, SB_WIDTH:2 * SB_WIDTH] = _dot(hb, w_ref[:, SB_WIDTH:2 * SB_WIDTH]).astype(BF16)

    lane = lax.broadcasted_iota(jnp.int32, (1, LANES), 1)
    first = lane < HEAD_DIM
    zero = jnp.zeros((), BF16)
    r_c = lax.broadcasted_iota(jnp.int32, (CHUNK, CHUNK), 0)
    c_c = lax.broadcasted_iota(jnp.int32, (CHUNK, CHUNK), 1)
    tril = r_c >= c_c
    n_pairs = SG_WIDTH // HEAD_PAIR
    w_pairs = []
    for p in range(n_pairs):
        w0 = jnp.where(tril, wsp_ref[2 * p], 0.0).astype(BF16)
        w1 = jnp.where(tril, wsp_ref[2 * p + 1], 0.0).astype(BF16)
        w_pairs.append(jnp.concatenate([w0, w1], axis=1))
    bsp = bsp_ref[...]
    for c in range(tm // CHUNK):
        rows = slice(c * CHUNK, (c + 1) * CHUNK)
        for p in range(n_pairs):
            cols = slice(p * HEAD_PAIR, (p + 1) * HEAD_PAIR)
            vg = vgn_ref[rows, cols]
            rhs = jnp.concatenate([jnp.where(first, vg, zero), jnp.where(first, zero, vg)], axis=0)
            mixed = _dot(w_pairs[p], rhs) + bsp[:, cols]
            sg_ref[rows, cols] = gu_ref[rows, cols] * mixed
    qkv_ref[:, 2 * SB_WIDTH:3 * SB_WIDTH] = _dot(hb, w_ref[:, 2 * SB_WIDTH:3 * SB_WIDTH]).astype(BF16)
    sgn_ref[...] = _rms(sg_ref[...], sgog_ref[...]).astype(BF16)


def _inproj(x2, attn_g, w_in_b, sg_g, wsp, bsp_full, sg_out_g):
    n = x2.shape[0]
    row = lambda i: (i, 0)
    const = lambda i: (0, 0)
    return pl.pallas_call(
        _inproj_kernel,
        grid=(n // TM_PROJ,),
        in_specs=[pl.BlockSpec((TM_PROJ, D_MODEL), row),
                  pl.BlockSpec((1, D_MODEL), const),
                  pl.BlockSpec((D_MODEL, D_IN), const),
                  pl.BlockSpec((1, SG_WIDTH), const),
                  pl.BlockSpec((SG_HEADS, CHUNK, CHUNK), lambda i: (0, 0, 0)),
                  pl.BlockSpec((CHUNK, SG_WIDTH), const),
                  pl.BlockSpec((1, SG_WIDTH), const)],
        out_specs=[pl.BlockSpec((TM_PROJ, 3 * SB_WIDTH), row),
                   pl.BlockSpec((TM_PROJ, SG_WIDTH), row)],
        out_shape=[jax.ShapeDtypeStruct((n, 3 * SB_WIDTH), BF16),
                   jax.ShapeDtypeStruct((n, SG_WIDTH), BF16)],
        scratch_shapes=[pltpu.VMEM((TM_PROJ, SG_WIDTH), F32),
                        pltpu.VMEM((TM_PROJ, SG_WIDTH), BF16),
                        pltpu.VMEM((TM_PROJ, SG_WIDTH), F32)],
        compiler_params=pltpu.CompilerParams(dimension_semantics=("arbitrary",),
                                             vmem_limit_bytes=VMEM_LIMIT),
        name="inproj",
    )(x2, attn_g, w_in_b, sg_g, wsp, bsp_full, sg_out_g)


def _attn_kernel(q_ref, k_ref, v_ref, o_ref, q2_ref, carry_ref):
    t = TQ_ATTN
    n_pairs = SB_WIDTH // HEAD_PAIR
    lane = lax.broadcasted_iota(jnp.int32, (1, HEAD_PAIR), 1)
    head_lanes = (lane < HEAD_DIM, lane >= HEAD_DIM)
    zero = jnp.zeros((), BF16)
    r_idx = lax.broadcasted_iota(jnp.int32, (t, t), 0)
    c_idx = lax.broadcasted_iota(jnp.int32, (t, t), 1)
    suffix = (r_idx > c_idx).astype(BF16)
    suffix2 = jnp.concatenate([suffix, suffix], axis=0)
    causal = c_idx < r_idx

    def one_query_block(sub, c):
        qi = pl.program_id(1) * ATTN_BLOCKS_PER_STEP + sub
        row0 = pl.multiple_of(sub * t, t)
        for p in range(n_pairs):
            qp = q_ref[0, pl.ds(row0, t), p * HEAD_PAIR:(p + 1) * HEAD_PAIR]
            for h in range(2):
                q2_ref[(2 * p + h) * t:(2 * p + h + 1) * t, :] = jnp.where(head_lanes[h], qp, zero)
        o_ref[0, pl.ds(row0, t), :] = jnp.zeros((t, SB_WIDTH), F32)
        carry_ref[...] = jnp.zeros_like(carry_ref)

        def block(j, diag, m):
            start = pl.multiple_of(j * t, t)
            mask2 = jnp.concatenate([causal, causal], axis=0) if diag else None
            st = [dict() for _ in range(n_pairs)]

            def head_rows(p):
                return [slice((2 * p + h) * t, (2 * p + h) * t + m) for h in range(2)]

            def scores(p):
                d = st[p]
                d["cols"] = slice(p * HEAD_PAIR, (p + 1) * HEAD_PAIR)
                kb = k_ref[0, pl.ds(start, t), d["cols"]]
                q2 = jnp.concatenate([q2_ref[r, :] for r in head_rows(p)], axis=0)
                z = lax.dot_general(q2, kb, (((1,), (1,)), ((), ())),
                                    preferred_element_type=F32)
                sp = _softplus(z)
                nl = jnp.where(mask2, sp, 0.0) if diag else sp
                hi, lo = _split_bf16(nl)
                d["hl"] = jnp.concatenate([hi, lo], axis=1)
                d["log_beta"] = z - sp
                d["nl0"] = nl[:, 0:1]

            def weights(p):
                d = st[p]
                hl = d["hl"]
                after = jnp.concatenate([_dot(hl[0:m], suffix2), _dot(hl[m:2 * m], suffix2)], axis=0)
                carry = jnp.concatenate([carry_ref[r, :] for r in head_rows(p)], axis=0)
                a = jnp.exp(d["log_beta"] - after - carry)
                if diag:
                    a = jnp.where(mask2, a, 0.0)
                a = a.astype(BF16)
                d["a2"] = jnp.concatenate([a[0:m], a[m:2 * m]], axis=1)
                new_carry = carry + after[:, 0:1] + d["nl0"]
                for h, r in enumerate(head_rows(p)):
                    carry_ref[r, :] = new_carry[h * m:(h + 1) * m]

            def values(p):
                d = st[p]
                vb = v_ref[0, pl.ds(start, t), d["cols"]]
                v2 = jnp.concatenate([jnp.where(head_lanes[0], vb, zero),
                                      jnp.where(head_lanes[1], vb, zero)], axis=0)
                o_ref[0, pl.ds(row0, m), d["cols"]] += _dot(d["a2"], v2)

            for step in range(n_pairs + 2):
                if step < n_pairs:
                    scores(step)
                if 0 <= step - 1 < n_pairs:
                    weights(step - 1)
                if 0 <= step - 2 < n_pairs:
                    values(step - 2)

        def flags():
            bounds = (0,) + ATTN_TOP_ROWS + (t,)
            lowest = [jnp.min(jnp.concatenate([carry_ref[hh * t + lo:hh * t + hi, :] for hh in range(2 * n_pairs)],
                                              axis=0))
                      for lo, hi in zip(bounds[:-1], bounds[1:])]
            below = [functools.reduce(jnp.minimum, lowest[k:]) for k in range(len(lowest))]
            return (below[0] < F32_EXP_UNDERFLOW,) + tuple(b >= F32_EXP_UNDERFLOW for b in below[1:])

        block(qi, True, t)

        def body(state):
            it, _, *done = state
            j = qi - 1 - it
            for k, m in enumerate(ATTN_TOP_ROWS + (t,)):
                use = done[k] if k < len(done) else True
                if k > 0:
                    use = jnp.logical_and(use, jnp.logical_not(done[k - 1]))

                @pl.when(use)
                def _(m=m):
                    block(j, False, m)

            return (it + 1,) + flags()

        lax.while_loop(lambda s: (s[0] < qi) & s[1], body, (jnp.int32(0),) + flags())
        return c

    lax.fori_loop(0, ATTN_BLOCKS_PER_STEP, one_query_block, 0)


def _attention(qkv, batch, seq):
    qkv3 = qkv.reshape(batch, seq, 3 * SB_WIDTH)
    n_heads = SB_WIDTH // HEAD_DIM
    return pl.pallas_call(
        _attn_kernel,
        grid=(batch, seq // (ATTN_BLOCKS_PER_STEP * TQ_ATTN)),
        in_specs=[pl.BlockSpec((1, ATTN_BLOCKS_PER_STEP * TQ_ATTN, SB_WIDTH), lambda b, i: (b, i, 0)),
                  pl.BlockSpec((1, seq, SB_WIDTH), lambda b, i: (b, 0, 1)),
                  pl.BlockSpec((1, seq, SB_WIDTH), lambda b, i: (b, 0, 2))],
        out_specs=pl.BlockSpec((1, ATTN_BLOCKS_PER_STEP * TQ_ATTN, SB_WIDTH), lambda b, i: (b, i, 0)),
        out_shape=jax.ShapeDtypeStruct((batch, seq, SB_WIDTH), F32),
        scratch_shapes=[pltpu.VMEM((n_heads * TQ_ATTN, HEAD_PAIR), BF16),
                        pltpu.VMEM((n_heads * TQ_ATTN, 1), F32)],
        compiler_params=pltpu.CompilerParams(dimension_semantics=("arbitrary",) * 2,
                                             vmem_limit_bytes=VMEM_LIMIT),
        name="sb_attention",
    )(qkv3, qkv3, qkv3)


def _mix_kernel(sb_ref, sgn_ref, x_ref, sbg_ref, wout_ref, ffng_ref, wr2_ref, br_ref,
                h_ref, lg_ref):
    sbn = _rms(sb_ref[...], sbg_ref[...]).astype(BF16)
    h = x_ref[...] + _dot(sbn, wout_ref[0:SB_WIDTH, :]) + _dot(sgn_ref[...], wout_ref[SB_WIDTH:, :])
    h_ref[...] = h
    hn = _rms(h, ffng_ref[...])

    hn_hi, hn_lo = _split_bf16(hn)
    both = _dot(hn_hi, wr2_ref[...])
    logits = both[:, 0:LANES] + both[:, LANES:] + _dot(hn_lo, wr2_ref[:, 0:LANES]) + br_ref[...]
    lg_ref[...] = logits.T[0:ROUTER_ROWS, :]


def _route_kernel(lg_ref, ri_ref, rw_ref, cnt_ref, count_ref):
    tr = TM_ROUTE
    i = pl.program_id(0)

    @pl.when(i == 0)
    def _():
        count_ref[...] = jnp.zeros_like(count_ref)

    neg = jnp.float32(-jnp.inf)
    row8 = lax.broadcasted_iota(jnp.int32, (SUBLANES, tr), 0)

    def top(v):
        m = jnp.max(v, axis=0, keepdims=True)
        return m, jnp.min(jnp.where(v == m, row8, SUBLANES), axis=0, keepdims=True)

    def group_rows(g):
        return lg_ref[ROUTER_LANE0 + g * EXPERTS_PER_GROUP:ROUTER_LANE0 + (g + 1) * EXPERTS_PER_GROUP, :]

    gl = jnp.where(row8 < N_GROUPS, lg_ref[0:SUBLANES, :], neg)
    gmax, gidx = top(gl)
    gweight = 1.0 / jnp.sum(jnp.exp(gl - gmax), axis=0, keepdims=True)
    el = group_rows(0)
    for g in range(1, N_GROUPS):
        el = jnp.where(gidx == g, group_rows(g), el)
    m1, i1 = top(el)
    m2, i2 = top(jnp.where(row8 == i1, neg, el))
    t21 = jnp.exp(m2 - m1)
    w1 = gweight / (1.0 + t21)
    w2 = gweight * t21 / (1.0 + t21)
    e1 = gidx * EXPERTS_PER_GROUP + i1
    e2 = gidx * EXPERTS_PER_GROUP + i2

    row_e = lax.broadcasted_iota(jnp.int32, (N_EXPERTS, tr), 0)
    sel1 = row_e == e1
    sel2 = row_e == e2
    onehot = jnp.where(sel1 | sel2, 1.0, 0.0)
    r_t = lax.broadcasted_iota(jnp.int32, (tr, tr), 0)
    c_t = lax.broadcasted_iota(jnp.int32, (tr, tr), 1)
    before = (r_t < c_t).astype(BF16)
    running = count_ref[:, 0:1] + _dot(onehot.astype(BF16), before)
    rank1 = jnp.sum(jnp.where(sel1, running, 0.0), axis=0, keepdims=True)
    rank2 = jnp.sum(jnp.where(sel2, running, 0.0), axis=0, keepdims=True)
    new_count = count_ref[:, 0:1] + jnp.sum(onehot, axis=1, keepdims=True)
    count_ref[...] = jnp.broadcast_to(new_count, count_ref.shape)
    cnt_ref[...] = jnp.broadcast_to(new_count, cnt_ref.shape)

    ri_ref[...] = jnp.where(row8 == 0, e1, jnp.where(row8 == 1, e2, jnp.where(
        row8 == 2, rank1.astype(jnp.int32), jnp.where(row8 == 3, rank2.astype(jnp.int32), 0))))
    row128 = lax.broadcasted_iota(jnp.int32, (LANES, tr), 0)
    rw_ref[...] = jnp.where(row128 == 0, w1, jnp.where(row128 == 1, w2, 0.0)).T


def _route(lg):
    n = lg.shape[1]
    return pl.pallas_call(
        _route_kernel,
        grid=(n // TM_ROUTE,),
        in_specs=[pl.BlockSpec((ROUTER_ROWS, TM_ROUTE), lambda i: (0, i))],
        out_specs=[pl.BlockSpec((SUBLANES, TM_ROUTE), lambda i: (0, i)),
                   pl.BlockSpec((TM_ROUTE, LANES), lambda i: (i, 0)),
                   pl.BlockSpec((N_EXPERTS, LANES), lambda i: (0, 0))],
        out_shape=[jax.ShapeDtypeStruct((SUBLANES, n), jnp.int32),
                   jax.ShapeDtypeStruct((n, LANES), F32),
                   jax.ShapeDtypeStruct((N_EXPERTS, LANES), F32)],
        scratch_shapes=[pltpu.VMEM((N_EXPERTS, LANES), F32)],
        compiler_params=pltpu.CompilerParams(dimension_semantics=("arbitrary",),
                                             vmem_limit_bytes=VMEM_LIMIT),
        name="route",
    )(lg)


def _mix(sb, sgn, x2, sb_g, w_out_b, ffn_g, wr2, br):
    n = x2.shape[0]
    row = lambda i: (i, 0)
    const = lambda i: (0, 0)
    return pl.pallas_call(
        _mix_kernel,
        grid=(n // TM_MIX,),
        in_specs=[pl.BlockSpec((TM_MIX, SB_WIDTH), row),
                  pl.BlockSpec((TM_MIX, SG_WIDTH), row),
                  pl.BlockSpec((TM_MIX, D_MODEL), row),
                  pl.BlockSpec((1, SB_WIDTH), const),
                  pl.BlockSpec((D_MODEL, D_MODEL), const),
                  pl.BlockSpec((1, D_MODEL), const),
                  pl.BlockSpec((D_MODEL, 2 * LANES), const),
                  pl.BlockSpec((1, LANES), const)],
        out_specs=[pl.BlockSpec((TM_MIX, D_MODEL), row),
                   pl.BlockSpec((ROUTER_ROWS, TM_MIX), lambda i: (0, i))],
        out_shape=[jax.ShapeDtypeStruct((n, D_MODEL), F32),
                   jax.ShapeDtypeStruct((ROUTER_ROWS, n), F32)],
        compiler_params=pltpu.CompilerParams(dimension_semantics=("arbitrary",),
                                             vmem_limit_bytes=VMEM_LIMIT),
        name="mix_router",
    )(sb, sgn, x2, sb_g, w_out_b, ffn_g, wr2, br)


_PAD_BITS = tuple(1 << b for b in reversed(range(EXPERT_CHUNK.bit_length() - 1)))


def _dispatch_kernel(dest_ref, pad_start_ref, pad_count_ref, used_ref, h_ref, g_ref, zeros_ref, xs_ref,
                     hn_ref, sem, zsem):
    tm = TM_DISPATCH
    i = pl.program_id(0)
    n_steps = pl.num_programs(0) - 1
    n = n_steps * tm
    base = (i - 1) * tm
    prev = hn_ref.at[lax.rem(i + 1, 2)]
    n_chunks = xs_ref.shape[0] // (EXPERT_CHUNK * ROW_TILE)

    def pad_copies(do):
        for e in range(N_EXPERTS):
            start = pad_start_ref[e]
            count = pad_count_ref[e]
            for bit in _PAD_BITS:
                @pl.when((count & bit) != 0)
                def _(start=start, bit=bit):
                    do(pltpu.make_async_copy(_token_rows(zeros_ref, 0, bit),
                                             _token_rows(xs_ref, start, bit), zsem))
                start = start + (count & bit)
        for k in range(N_EXPERTS):
            chunk = used_ref[0] + k

            @pl.when(chunk < n_chunks)
            def _(chunk=chunk):
                do(pltpu.make_async_copy(zeros_ref, _token_rows(xs_ref, chunk * EXPERT_CHUNK, EXPERT_CHUNK),
                                         zsem))

    @pl.when(i == 0)
    def _():
        pad_copies(lambda cp: cp.start())

    @pl.when(i > 0)
    def _():
        def body(r, c):
            src = _token_rows(prev, r, 1)
            for s in range(2):
                pltpu.make_async_copy(src, _token_rows(xs_ref, dest_ref[s * n + base + r], 1),
                                      sem).start(priority=s)
            return c

        lax.fori_loop(0, tm, body, 0, unroll=8)

    @pl.when(i < n_steps)
    def _():
        _rows_to_tiles(hn_ref.at[lax.rem(i, 2)], _rms(h_ref[...], g_ref[...]))

    @pl.when(i > 0)
    def _():
        for _ in range(2):
            pltpu.make_async_copy(prev, _token_rows(xs_ref, 0, tm), sem).wait()

    @pl.when(i == n_steps)
    def _():
        pad_copies(lambda cp: cp.wait())


def _dispatch(dest, pad_start, pad_count, used_chunks, h, ffn_g, n_rows):
    n_steps = h.shape[0] // TM_DISPATCH
    zeros = jnp.zeros((EXPERT_CHUNK * ROW_TILE, LANES), F32)
    return pl.pallas_call(
        _dispatch_kernel,
        grid_spec=pltpu.PrefetchScalarGridSpec(
            num_scalar_prefetch=4,
            grid=(n_steps + 1,),
            in_specs=[pl.BlockSpec((TM_DISPATCH, D_MODEL), lambda i, *_: (jnp.minimum(i, n_steps - 1), 0)),
                      pl.BlockSpec((1, D_MODEL), lambda i, *_: (0, 0)),
                      pl.BlockSpec(memory_space=pl.ANY)],
            out_specs=pl.BlockSpec(memory_space=pl.ANY),
            scratch_shapes=[pltpu.VMEM((2, TM_DISPATCH * ROW_TILE, LANES), F32),
                            pltpu.SemaphoreType.DMA, pltpu.SemaphoreType.DMA]),
        out_shape=jax.ShapeDtypeStruct((n_rows * ROW_TILE, LANES), F32),
        compiler_params=pltpu.CompilerParams(dimension_semantics=("arbitrary",),
                                             vmem_limit_bytes=VMEM_LIMIT),
        name="dispatch",
    )(dest, pad_start, pad_count, used_chunks, h, ffn_g, zeros)


X_SLOTS = 3
TILE_CHUNKS = TM_EXPERT // EXPERT_CHUNK
W_SLOTS = 3


def _expert_kernel(tiles_ref, chunk0_ref, chunks_ref, nt_ref, used_ref, xs_ref, wg_ref, wu_ref, wd_ref,
                   zeros_ref, ys_ref, x_buf, y_buf, sg_buf, su_buf, sd_buf, wgb, wub, wdb, state,
                   w_sems, x_sems, y_sems, zsem):
    t = pl.program_id(0)
    last = pl.num_programs(0) - 1
    nt = nt_ref[0]
    n_chunks = ys_ref.shape[0] // (EXPERT_CHUNK * ROW_TILE)

    def tile_copies(tile, do, out):
        for c in range(TILE_CHUNKS):
            @pl.when(c < chunks_ref[tile])
            def _(c=c):
                first = (chunk0_ref[tile] + c) * EXPERT_CHUNK
                if out:
                    slot = lax.rem(tile, 2)
                    do(pltpu.make_async_copy(_token_rows(y_buf.at[slot], c * EXPERT_CHUNK, EXPERT_CHUNK),
                                             _token_rows(ys_ref, first, EXPERT_CHUNK), y_sems.at[slot]))
                else:
                    slot = lax.rem(tile, X_SLOTS)
                    do(pltpu.make_async_copy(_token_rows(xs_ref, first, EXPERT_CHUNK),
                                             _token_rows(x_buf.at[slot], c * EXPERT_CHUNK, EXPERT_CHUNK),
                                             x_sems.at[slot]))

    start = lambda cp: cp.start()
    wait = lambda cp: cp.wait()

    def tail_copies(do):
        for k in range(N_EXPERTS):
            chunk = used_ref[0] + k

            @pl.when(chunk < n_chunks)
            def _(chunk=chunk):
                do(pltpu.make_async_copy(zeros_ref, _token_rows(ys_ref, chunk * EXPERT_CHUNK, EXPERT_CHUNK),
                                         zsem))

    def weight_copies(e, slot):
        return (pltpu.make_async_copy(wg_ref.at[e], sg_buf.at[slot], w_sems.at[slot]),
                pltpu.make_async_copy(wu_ref.at[e], su_buf.at[slot], w_sems.at[slot]),
                pltpu.make_async_copy(wd_ref.at[e], sd_buf.at[slot], w_sems.at[slot]))

    def next_with_rows(e):
        return lax.while_loop(lambda k: (k < N_EXPERTS) & (tiles_ref[jnp.minimum(k, N_EXPERTS - 1)] == 0),
                              lambda k: k + 1, e + 1)

    @pl.when(t == 0)
    def _():
        first = next_with_rows(jnp.int32(-1))
        second = next_with_rows(first)
        state[0] = jnp.int32(-1)
        state[1] = jnp.int32(0)
        state[2] = jnp.int32(W_SLOTS - 1)
        state[3] = first
        state[4] = second
        for cp in weight_copies(first, 0):
            cp.start()

        @pl.when(second < N_EXPERTS)
        def _():
            for cp in weight_copies(second, 1):
                cp.start()

        tile_copies(0, start, False)

        @pl.when(nt > 1)
        def _():
            tile_copies(1, start, False)

        tail_copies(start)

    @pl.when(t + 2 < nt)
    def _():
        tile_copies(t + 2, start, False)

    @pl.when(t < nt)
    def _():
        @pl.when(state[1] == 0)
        def _():
            e = state[3]
            nxt = state[4]
            slot = lax.rem(state[2] + 1, W_SLOTS)
            after_next = next_with_rows(nxt)
            state[0] = e
            state[1] = tiles_ref[e]
            state[2] = slot
            state[3] = nxt
            state[4] = after_next
            for cp in weight_copies(e, slot):
                cp.wait()

            @pl.when(after_next < N_EXPERTS)
            def _():
                for cp in weight_copies(after_next, lax.rem(slot + 2, W_SLOTS)):
                    cp.start()

            wgb[...] = sg_buf[slot].astype(BF16)
            wub[...] = su_buf[slot].astype(BF16)
            wdb[...] = sd_buf[slot].astype(BF16)

        state[1] = state[1] - 1
        tile_copies(t, wait, False)

        @pl.when(t >= 2)
        def _():
            tile_copies(t - 2, wait, True)

        for n_chunks_here in range(1, TILE_CHUNKS + 1):
            @pl.when(chunks_ref[t] == n_chunks_here)
            def _(m=n_chunks_here * EXPERT_CHUNK):
                x = _tiles_to_rows(x_buf.at[lax.rem(t, X_SLOTS)], m).astype(BF16)
                g = _dot(x, wgb[...])
                u = _dot(x, wub[...])
                hidden = (g * jax.nn.sigmoid(g)) * u
                _rows_to_tiles(y_buf.at[lax.rem(t, 2)], _dot(hidden.astype(BF16), wdb[...]))

        tile_copies(t, start, True)

    @pl.when(t == last)
    def _():
        for back in (2, 1):
            @pl.when(nt >= back)
            def _(back=back):
                tile_copies(nt - back, wait, True)

        tail_copies(wait)


def _experts(tiles, chunk0, chunks, n_tiles, used_chunks, xs, wg, wu, wd):
    any_spec = pl.BlockSpec(memory_space=pl.ANY)
    zeros = jnp.zeros((EXPERT_CHUNK * ROW_TILE, LANES), F32)
    return pl.pallas_call(
        _expert_kernel,
        grid_spec=pltpu.PrefetchScalarGridSpec(
            num_scalar_prefetch=5,
            grid=(chunks.shape[0],),
            in_specs=[any_spec, any_spec, any_spec, any_spec, any_spec],
            out_specs=any_spec,
            scratch_shapes=[pltpu.VMEM((X_SLOTS, TM_EXPERT * ROW_TILE, LANES), F32),
                            pltpu.VMEM((2, TM_EXPERT * ROW_TILE, LANES), F32),
                            pltpu.VMEM((W_SLOTS, D_MODEL, D_EXPERT), F32),
                            pltpu.VMEM((W_SLOTS, D_MODEL, D_EXPERT), F32),
                            pltpu.VMEM((W_SLOTS, D_EXPERT, D_MODEL), F32),
                            pltpu.VMEM((D_MODEL, D_EXPERT), BF16),
                            pltpu.VMEM((D_MODEL, D_EXPERT), BF16),
                            pltpu.VMEM((D_EXPERT, D_MODEL), BF16),
                            pltpu.SMEM((5,), jnp.int32),
                            pltpu.SemaphoreType.DMA((W_SLOTS,)),
                            pltpu.SemaphoreType.DMA((X_SLOTS,)),
                            pltpu.SemaphoreType.DMA((2,)),
                            pltpu.SemaphoreType.DMA]),
        out_shape=jax.ShapeDtypeStruct(xs.shape, F32),
        compiler_params=pltpu.CompilerParams(dimension_semantics=("arbitrary",),
                                             vmem_limit_bytes=VMEM_LIMIT),
        name="expert_mlp",
    )(tiles, chunk0, chunks, n_tiles, used_chunks, xs, wg, wu, wd, zeros)


def _combine_kernel(dest_ref, rw_ref, fg_ref, h_ref, y_ref, o_ref, buf, h_buf, sems, h_sems):
    tm = TM_COMBINE
    i = pl.program_id(0)
    n_steps = pl.num_programs(0)
    n = n_steps * tm
    cur = i % 2

    def h_copy(step, half):
        return pltpu.make_async_copy(h_ref.at[pl.ds(pl.multiple_of(step * tm, tm), tm)], h_buf.at[half],
                                     h_sems.at[half])

    def fetch(step, half):
        h_copy(step, half).start(priority=1)

        def body(r, c):
            for s in range(2):
                pltpu.make_async_copy(_token_rows(y_ref, dest_ref[s * n + step * tm + r], 1),
                                      _token_rows(buf.at[half, s], r, 1),
                                      sems.at[half]).start(priority=s)
            return c

        lax.fori_loop(0, tm, body, 0, unroll=8)

    @pl.when(i == 0)
    def _():
        fetch(0, 0)

    @pl.when(i + 1 < n_steps)
    def _():
        fetch(i + 1, 1 - cur)

    for s in range(2):
        pltpu.make_async_copy(_token_rows(y_ref, 0, tm), buf.at[cur, s], sems.at[cur]).wait()
    h_copy(i, cur).wait()
    rw = rw_ref[...]
    out = (h_buf[cur] + rw[:, 0:1] * _tiles_to_rows(buf.at[cur, 0], tm)
           + rw[:, 1:2] * _tiles_to_rows(buf.at[cur, 1], tm))
    o_ref[...] = _rms(out, fg_ref[...])


def _combine(dest, h, rw, final_g, ys):
    n = h.shape[0]
    return pl.pallas_call(
        _combine_kernel,
        grid_spec=pltpu.PrefetchScalarGridSpec(
            num_scalar_prefetch=1,
            grid=(n // TM_COMBINE,),
            in_specs=[pl.BlockSpec((TM_COMBINE, LANES), lambda i, d: (i, 0)),
                      pl.BlockSpec((1, D_MODEL), lambda i, d: (0, 0)),
                      pl.BlockSpec(memory_space=pl.ANY),
                      pl.BlockSpec(memory_space=pl.ANY)],
            out_specs=pl.BlockSpec((TM_COMBINE, D_MODEL), lambda i, d: (i, 0)),
            scratch_shapes=[pltpu.VMEM((2, 2, TM_COMBINE * ROW_TILE, LANES), F32),
                            pltpu.VMEM((2, TM_COMBINE, D_MODEL), F32),
                            pltpu.SemaphoreType.DMA((2,)),
                            pltpu.SemaphoreType.DMA((2,))]),
        out_shape=jax.ShapeDtypeStruct((n, D_MODEL), F32),
        compiler_params=pltpu.CompilerParams(dimension_semantics=("arbitrary",),
                                             vmem_limit_bytes=VMEM_LIMIT),
        name="combine",
    )(dest, rw, final_g, h, ys)


def _schedule(counts, max_tiles):
    chunks = (counts + EXPERT_CHUNK - 1) // EXPERT_CHUNK
    chunk_end = jnp.cumsum(chunks)
    chunk_start = chunk_end - chunks
    tiles = (chunks + TILE_CHUNKS - 1) // TILE_CHUNKS
    tile_end = jnp.cumsum(tiles)
    tile = jnp.arange(max_tiles, dtype=jnp.int32)
    owner = jnp.sum(tile[:, None] >= tile_end[None, :], axis=1)
    is_owner = owner[:, None] == jnp.arange(N_EXPERTS, dtype=jnp.int32)[None, :]
    of_owner = lambda v: jnp.sum(jnp.where(is_owner, v[None, :], 0), axis=1)
    done = (tile - of_owner(tile_end - tiles)) * TILE_CHUNKS
    tile_chunk0 = (of_owner(chunk_start) + done).astype(jnp.int32)
    tile_chunks = jnp.clip(of_owner(chunks) - done, 0, TILE_CHUNKS).astype(jnp.int32)
    return tiles, chunk_start * EXPERT_CHUNK, tile_chunk0, tile_chunks, tile_end[-1:], chunk_end[-1:]


def _layer(x, attn_g, w_in, sg_g, w_sp, b_sp, sb_g, sg_out_g, w_out, ffn_g,
           w_rg, b_rg, w_re, b_re, w_gate, w_up, w_down):
    batch, seq, _ = x.shape
    n = batch * seq
    x2 = x.reshape(n, D_MODEL)
    row = lambda v: v.reshape(1, -1)

    bsp_full = jnp.repeat(b_sp.T, HEAD_DIM, axis=1)
    qkv, sgn = _inproj(x2, row(attn_g), w_in.astype(BF16), row(sg_g), w_sp, bsp_full, row(sg_out_g))
    sb = _attention(qkv, batch, seq).reshape(n, SB_WIDTH)

    pad_lanes = lambda v, width: jnp.pad(v, [(0, 0)] * (v.ndim - 1) + [(0, width - v.shape[-1])])
    w_r = jnp.concatenate([pad_lanes(w_rg, ROUTER_LANE0),
                           jnp.transpose(w_re, (1, 0, 2)).reshape(D_MODEL, N_EXPERTS)], axis=1)
    w_r = pad_lanes(w_r, LANES)
    wr_hi = w_r.astype(BF16)
    wr_lo = (w_r - wr_hi.astype(F32)).astype(BF16)
    wr2 = jnp.concatenate([wr_hi, wr_lo], axis=1)
    b_r = pad_lanes(jnp.concatenate([pad_lanes(b_rg, ROUTER_LANE0), b_re.reshape(-1)]), LANES)

    h, lg = _mix(sb, sgn, x2, row(sb_g), w_out.astype(BF16), row(ffn_g), wr2, row(b_r))
    ri, rw, cnt = _route(lg)

    counts = cnt[:, 0].astype(jnp.int32)
    n_rows = 2 * n + N_EXPERTS * EXPERT_CHUNK
    tiles, offsets, tile_chunk0, tile_chunks, n_tiles, used_chunks = _schedule(
        counts, 2 * n // TM_EXPERT + N_EXPERTS)
    expert, rank = ri[0:2], ri[2:4]
    is_e = expert[None] == jnp.arange(N_EXPERTS, dtype=jnp.int32)[:, None, None]
    dest = (jnp.sum(jnp.where(is_e, offsets[:, None, None], 0), axis=0) + rank).reshape(-1)
    pad_start = offsets + counts
    pad_count = (-counts) % EXPERT_CHUNK

    xs = _dispatch(dest, pad_start, pad_count, used_chunks, h, row(ffn_g), n_rows)
    ys = _experts(tiles, tile_chunk0, tile_chunks, n_tiles, used_chunks, xs,
                  w_gate.reshape(N_EXPERTS, D_MODEL, D_EXPERT),
                  w_up.reshape(N_EXPERTS, D_MODEL, D_EXPERT),
                  w_down.reshape(N_EXPERTS, D_EXPERT, D_MODEL))
    return dest, h, rw, ys


def kernel(x, attn_norm_g, w_in, sg_norm_g, w_spatial, b_spatial, sb_out_norm_g, sg_out_norm_g,
           w_out, ffn_norm_g, w_router_group, b_router_group, w_router_expert, b_router_expert,
           w_gate, w_up, w_down, final_norm_g):
    assert attn_norm_g.shape[0] == 1, "single-layer problem"
    batch, seq, _ = x.shape
    dest, h, rw, ys = _layer(x, attn_norm_g[0], w_in[0], sg_norm_g[0], w_spatial[0], b_spatial[0],
                             sb_out_norm_g[0], sg_out_norm_g[0], w_out[0], ffn_norm_g[0],
                             w_router_group[0], b_router_group[0], w_router_expert[0],
                             b_router_expert[0], w_gate[0], w_up[0], w_down[0])
    out = _combine(dest, h, rw, final_norm_g.reshape(1, -1), ys)
    return out.reshape(batch, seq, D_MODEL)
```

```python
import functools
import math

import jax
import jax.numpy as jnp
from jax import lax
from jax.experimental import pallas as pl
from jax.experimental.pallas import tpu as pltpu

D_MODEL = 1024
HEAD_DIM = 64
SB_WIDTH = 512
SG_WIDTH = 512
SG_HEADS = 8
D_IN = 3 * SB_WIDTH + 2 * SG_WIDTH
CHUNK = 128
N_GROUPS = 4
EXPERTS_PER_GROUP = 8
N_EXPERTS = N_GROUPS * EXPERTS_PER_GROUP
D_EXPERT = 512
EPS = 1e-6
F32_EXP_UNDERFLOW = 110.0

LANES = 128
SUBLANES = 8
ROW_TILE = D_MODEL // LANES
assert ROW_TILE == SUBLANES
HEAD_PAIR = 2 * HEAD_DIM
ROUTER_LANE0 = SUBLANES
ROUTER_ROWS = ROUTER_LANE0 + N_EXPERTS
assert EXPERTS_PER_GROUP == SUBLANES and N_GROUPS <= ROUTER_LANE0

TM_PROJ = 1024
TQ_ATTN = 256
ATTN_BLOCKS_PER_STEP = 2
ATTN_TOP_ROWS = (160, 176)
TM_MIX = 1024
TM_ROUTE = 1024
TM_DISPATCH = 1024
TM_EXPERT = 640
EXPERT_CHUNK = 128
TM_COMBINE = 512
VMEM_LIMIT = 48 * 1024 * 1024

F32 = jnp.float32
BF16 = jnp.bfloat16


def _rms(x, g):
    return x * lax.rsqrt(jnp.mean(x * x, axis=-1, keepdims=True) + EPS) * g


def _gelu(x):
    c = math.sqrt(2.0 / math.pi)
    return x * (0.5 * (1.0 + jnp.tanh(c * (x + 0.044715 * (x * x * x)))))


def _softplus(z):
    return jnp.maximum(z, 0.0) + jnp.log(1.0 + jnp.exp(-jnp.abs(z)))


def _dot(a, b):
    return jnp.dot(a, b, preferred_element_type=F32)


def _rows_to_tiles(ref, x):
    m = x.shape[0]
    for k in range(ROW_TILE):
        ref[pl.ds(k, m, stride=ROW_TILE), :] = x[:, k * LANES:(k + 1) * LANES]


def _tiles_to_rows(ref, m):
    return jnp.concatenate([ref[pl.ds(k, m, stride=ROW_TILE), :] for k in range(ROW_TILE)], axis=1)


def _token_rows(ref, first_token, n_tokens):
    return ref.at[pl.ds(pl.multiple_of(first_token * ROW_TILE, ROW_TILE), n_tokens * ROW_TILE)]


def _split_bf16(x):
    hi = x.astype(BF16)
    lo = (x - hi.astype(F32)).astype(BF16)
    return hi, lo


def _inproj_kernel(x_ref, g_ref, w_ref, sgg_ref, wsp_ref, bsp_ref, sgog_ref, qkv_ref, sgn_ref,
                   gu_ref, vgn_ref, sg_ref):
    tm = TM_PROJ
    hb = _rms(x_ref[...], g_ref[...]).astype(BF16)
    gv = _gelu(_dot(hb, w_ref[:, 3 * SB_WIDTH + SG_WIDTH:D_IN]))
    vgn_ref[...] = _rms(gv, sgg_ref[...]).astype(BF16)
    gu_ref[...] = _gelu(_dot(hb, w_ref[:, 3 * SB_WIDTH:3 * SB_WIDTH + SG_WIDTH]))
    q = _dot(hb, w_ref[:, 0:SB_WIDTH]) * (1.0 / math.sqrt(HEAD_DIM))
    qkv_ref[:, 0:SB_WIDTH] = q.astype(BF16)
    qkv_ref[:, SB_WIDTH:2 * SB_WIDTH] = _dot(hb, w_ref[:, SB_WIDTH:2 * SB_WIDTH]).astype(BF16)

    lane = lax.broadcasted_iota(jnp.int32, (1, LANES), 1)
    first = lane < HEAD_DIM
    zero = jnp.zeros((), BF16)
    r_c = lax.broadcasted_iota(jnp.int32, (CHUNK, CHUNK), 0)
    c_c = lax.broadcasted_iota(jnp.int32, (CHUNK, CHUNK), 1)
    tril = r_c >= c_c
    n_pairs = SG_WIDTH // HEAD_PAIR
    w_pairs = []
    for p in range(n_pairs):
        w0 = jnp.where(tril, wsp_ref[2 * p], 0.0).astype(BF16)
        w1 = jnp.where(tril, wsp_ref[2 * p + 1], 0.0).astype(BF16)
        w_pairs.append(jnp.concatenate([w0, w1], axis=1))
    bsp = bsp_ref[...]
    for c in range(tm // CHUNK):
        rows = slice(c * CHUNK, (c + 1) * CHUNK)
        for p in range(n_pairs):
            cols = slice(p * HEAD_PAIR, (p + 1) * HEAD_PAIR)
            vg = vgn_ref[rows, cols]
            rhs = jnp.concatenate([jnp.where(first, vg, zero), jnp.where(first, zero, vg)], axis=0)
            mixed = _dot(w_pairs[p], rhs) + bsp[:, cols]
            sg_ref[rows, cols] = gu_ref[rows, cols] * mixed
    qkv_ref[:, 2 * SB_WIDTH:3 * SB_WIDTH] = _dot(hb, w_ref[:, 2 * SB_WIDTH:3 * SB_WIDTH]).astype(BF16)
    sgn_ref[...] = _rms(sg_ref[...], sgog_ref[...]).astype(BF16)


def _inproj(x2, attn_g, w_in_b, sg_g, wsp, bsp_full, sg_out_g):
    n = x2.shape[0]
    row = lambda i: (i, 0)
    const = lambda i: (0, 0)
    return pl.pallas_call(
        _inproj_kernel,
        grid=(n // TM_PROJ,),
        in_specs=[pl.BlockSpec((TM_PROJ, D_MODEL), row),
                  pl.BlockSpec((1, D_MODEL), const),
                  pl.BlockSpec((D_MODEL, D_IN), const),
                  pl.BlockSpec((1, SG_WIDTH), const),
                  pl.BlockSpec((SG_HEADS, CHUNK, CHUNK), lambda i: (0, 0, 0)),
                  pl.BlockSpec((CHUNK, SG_WIDTH), const),
                  pl.BlockSpec((1, SG_WIDTH), const)],
        out_specs=[pl.BlockSpec((TM_PROJ, 3 * SB_WIDTH), row),
                   pl.BlockSpec((TM_PROJ, SG_WIDTH), row)],
        out_shape=[jax.ShapeDtypeStruct((n, 3 * SB_WIDTH), BF16),
                   jax.ShapeDtypeStruct((n, SG_WIDTH), BF16)],
        scratch_shapes=[pltpu.VMEM((TM_PROJ, SG_WIDTH), F32),
                        pltpu.VMEM((TM_PROJ, SG_WIDTH), BF16),
                        pltpu.VMEM((TM_PROJ, SG_WIDTH), F32)],
        compiler_params=pltpu.CompilerParams(dimension_semantics=("arbitrary",),
                                             vmem_limit_bytes=VMEM_LIMIT),
        name="inproj",
    )(x2, attn_g, w_in_b, sg_g, wsp, bsp_full, sg_out_g)


def _attn_kernel(q_ref, k_ref, v_ref, o_ref, q2_ref, carry_ref):
    t = TQ_ATTN
    n_pairs = SB_WIDTH // HEAD_PAIR
    lane = lax.broadcasted_iota(jnp.int32, (1, HEAD_PAIR), 1)
    head_lanes = (lane < HEAD_DIM, lane >= HEAD_DIM)
    zero = jnp.zeros((), BF16)
    r_idx = lax.broadcasted_iota(jnp.int32, (t, t), 0)
    c_idx = lax.broadcasted_iota(jnp.int32, (t, t), 1)
    suffix = (r_idx > c_idx).astype(BF16)
    suffix2 = jnp.concatenate([suffix, suffix], axis=0)
    causal = c_idx < r_idx

    def one_query_block(sub, c):
        qi = pl.program_id(1) * ATTN_BLOCKS_PER_STEP + sub
        row0 = pl.multiple_of(sub * t, t)
        for p in range(n_pairs):
            qp = q_ref[0, pl.ds(row0, t), p * HEAD_PAIR:(p + 1) * HEAD_PAIR]
            for h in range(2):
                q2_ref[(2 * p + h) * t:(2 * p + h + 1) * t, :] = jnp.where(head_lanes[h], qp, zero)
        o_ref[0, pl.ds(row0, t), :] = jnp.zeros((t, SB_WIDTH), F32)
        carry_ref[...] = jnp.zeros_like(carry_ref)

        def block(j, diag, m):
            start = pl.multiple_of(j * t, t)
            mask2 = jnp.concatenate([causal, causal], axis=0) if diag else None
            st = [dict() for _ in range(n_pairs)]

            def head_rows(p):
                return [slice((2 * p + h) * t, (2 * p + h) * t + m) for h in range(2)]

            def scores(p):
                d = st[p]
                d["cols"] = slice(p * HEAD_PAIR, (p + 1) * HEAD_PAIR)
                kb = k_ref[0, pl.ds(start, t), d["cols"]]
                q2 = jnp.concatenate([q2_ref[r, :] for r in head_rows(p)], axis=0)
                z = lax.dot_general(q2, kb, (((1,), (1,)), ((), ())),
                                    preferred_element_type=F32)
                sp = _softplus(z)
                nl = jnp.where(mask2, sp, 0.0) if diag else sp
                hi, lo = _split_bf16(nl)
                d["hl"] = jnp.concatenate([hi, lo], axis=1)
                d["log_beta"] = z - sp
                d["nl0"] = nl[:, 0:1]

            def weights(p):
                d = st[p]
                hl = d["hl"]
                after = jnp.concatenate([_dot(hl[0:m], suffix2), _dot(hl[m:2 * m], suffix2)], axis=0)
                carry = jnp.concatenate([carry_ref[r, :] for r in head_rows(p)], axis=0)
                a = jnp.exp(d["log_beta"] - after - carry)
                if diag:
                    a = jnp.where(mask2, a, 0.0)
                a = a.astype(BF16)
                d["a2"] = jnp.concatenate([a[0:m], a[m:2 * m]], axis=1)
                new_carry = carry + after[:, 0:1] + d["nl0"]
                for h, r in enumerate(head_rows(p)):
                    carry_ref[r, :] = new_carry[h * m:(h + 1) * m]

            def values(p):
                d = st[p]
                vb = v_ref[0, pl.ds(start, t), d["cols"]]
                v2 = jnp.concatenate([jnp.where(head_lanes[0], vb, zero),
                                      jnp.where(head_lanes[1], vb, zero)], axis=0)
                o_ref[0, pl.ds(row0, m), d["cols"]] += _dot(d["a2"], v2)

            for step in range(n_pairs + 2):
                if step < n_pairs:
                    scores(step)
                if 0 <= step - 1 < n_pairs:
                    weights(step - 1)
                if 0 <= step - 2 < n_pairs:
                    values(step - 2)

        def flags():
            bounds = (0,) + ATTN_TOP_ROWS + (t,)
            lowest = [jnp.min(jnp.concatenate([carry_ref[hh * t + lo:hh * t + hi, :] for hh in range(2 * n_pairs)],
                                              axis=0))
                      for lo, hi in zip(bounds[:-1], bounds[1:])]
            below = [functools.reduce(jnp.minimum, lowest[k:]) for k in range(len(lowest))]
            return (below[0] < F32_EXP_UNDERFLOW,) + tuple(b >= F32_EXP_UNDERFLOW for b in below[1:])

        block(qi, True, t)

        def body(state):
            it, _, *done = state
            j = qi - 1 - it
            for k, m in enumerate(ATTN_TOP_ROWS + (t,)):
                use = done[k] if k < len(done) else True
                if k > 0:
                    use = jnp.logical_and(use, jnp.logical_not(done[k - 1]))

                @pl.when(use)
                def _(m=m):
                    block(j, False, m)

            return (it + 1,) + flags()

        lax.while_loop(lambda s: (s[0] < qi) & s[1], body, (jnp.int32(0),) + flags())
        return c

    lax.fori_loop(0, ATTN_BLOCKS_PER_STEP, one_query_block, 0)


def _attention(qkv, batch, seq):
    qkv3 = qkv.reshape(batch, seq, 3 * SB_WIDTH)
    n_heads = SB_WIDTH // HEAD_DIM
    return pl.pallas_call(
        _attn_kernel,
        grid=(batch, seq // (ATTN_BLOCKS_PER_STEP * TQ_ATTN)),
        in_specs=[pl.BlockSpec((1, ATTN_BLOCKS_PER_STEP * TQ_ATTN, SB_WIDTH), lambda b, i: (b, i, 0)),
                  pl.BlockSpec((1, seq, SB_WIDTH), lambda b, i: (b, 0, 1)),
                  pl.BlockSpec((1, seq, SB_WIDTH), lambda b, i: (b, 0, 2))],
        out_specs=pl.BlockSpec((1, ATTN_BLOCKS_PER_STEP * TQ_ATTN, SB_WIDTH), lambda b, i: (b, i, 0)),
        out_shape=jax.ShapeDtypeStruct((batch, seq, SB_WIDTH), F32),
        scratch_shapes=[pltpu.VMEM((n_heads * TQ_ATTN, HEAD_PAIR), BF16),
                        pltpu.VMEM((n_heads * TQ_ATTN, 1), F32)],
        compiler_params=pltpu.CompilerParams(dimension_semantics=("arbitrary",) * 2,
                                             vmem_limit_bytes=VMEM_LIMIT),
        name="sb_attention",
    )(qkv3, qkv3, qkv3)


def _mix_kernel(sb_ref, sgn_ref, x_ref, sbg_ref, wout_ref, ffng_ref, wr2_ref, br_ref,
                h_ref, lg_ref):
    sbn = _rms(sb_ref[...], sbg_ref[...]).astype(BF16)
    h = x_ref[...] + _dot(sbn, wout_ref[0:SB_WIDTH, :]) + _dot(sgn_ref[...], wout_ref[SB_WIDTH:, :])
    h_ref[...] = h
    hn = _rms(h, ffng_ref[...])

    hn_hi, hn_lo = _split_bf16(hn)
    both = _dot(hn_hi, wr2_ref[...])
    logits = both[:, 0:LANES] + both[:, LANES:] + _dot(hn_lo, wr2_ref[:, 0:LANES]) + br_ref[...]
    lg_ref[...] = logits.T[0:ROUTER_ROWS, :]


def _route_kernel(lg_ref, ri_ref, rw_ref, cnt_ref, count_ref):
    tr = TM_ROUTE
    i = pl.program_id(0)

    @pl.when(i == 0)
    def _():
        count_ref[...] = jnp.zeros_like(count_ref)

    neg = jnp.float32(-jnp.inf)
    row8 = lax.broadcasted_iota(jnp.int32, (SUBLANES, tr), 0)

    def top(v):
        m = jnp.max(v, axis=0, keepdims=True)
        return m, jnp.min(jnp.where(v == m, row8, SUBLANES), axis=0, keepdims=True)

    def group_rows(g):
        return lg_ref[ROUTER_LANE0 + g * EXPERTS_PER_GROUP:ROUTER_LANE0 + (g + 1) * EXPERTS_PER_GROUP, :]

    gl = jnp.where(row8 < N_GROUPS, lg_ref[0:SUBLANES, :], neg)
    gmax, gidx = top(gl)
    gweight = 1.0 / jnp.sum(jnp.exp(gl - gmax), axis=0, keepdims=True)
    el = group_rows(0)
    for g in range(1, N_GROUPS):
        el = jnp.where(gidx == g, group_rows(g), el)
    m1, i1 = top(el)
    m2, i2 = top(jnp.where(row8 == i1, neg, el))
    t21 = jnp.exp(m2 - m1)
    w1 = gweight / (1.0 + t21)
    w2 = gweight * t21 / (1.0 + t21)
    e1 = gidx * EXPERTS_PER_GROUP + i1
    e2 = gidx * EXPERTS_PER_GROUP + i2

    row_e = lax.broadcasted_iota(jnp.int32, (N_EXPERTS, tr), 0)
    sel1 = row_e == e1
    sel2 = row_e == e2
    onehot = jnp.where(sel1 | sel2, 1.0, 0.0)
    r_t = lax.broadcasted_iota(jnp.int32, (tr, tr), 0)
    c_t = lax.broadcasted_iota(jnp.int32, (tr, tr), 1)
    before = (r_t < c_t).astype(BF16)
    running = count_ref[:, 0:1] + _dot(onehot.astype(BF16), before)
    rank1 = jnp.sum(jnp.where(sel1, running, 0.0), axis=0, keepdims=True)
    rank2 = jnp.sum(jnp.where(sel2, running, 0.0), axis=0, keepdims=True)
    new_count = count_ref[:, 0:1] + jnp.sum(onehot, axis=1, keepdims=True)
    count_ref[...] = jnp.broadcast_to(new_count, count_ref.shape)
    cnt_ref[...] = jnp.broadcast_to(new_count, cnt_ref.shape)

    ri_ref[...] = jnp.where(row8 == 0, e1, jnp.where(row8 == 1, e2, jnp.where(
        row8 == 2, rank1.astype(jnp.int32), jnp.where(row8 == 3, rank2.astype(jnp.int32), 0))))
    row128 = lax.broadcasted_iota(jnp.int32, (LANES, tr), 0)
    rw_ref[...] = jnp.where(row128 == 0, w1, jnp.where(row128 == 1, w2, 0.0)).T


def _route(lg):
    n = lg.shape[1]
    return pl.pallas_call(
        _route_kernel,
        grid=(n // TM_ROUTE,),
        in_specs=[pl.BlockSpec((ROUTER_ROWS, TM_ROUTE), lambda i: (0, i))],
        out_specs=[pl.BlockSpec((SUBLANES, TM_ROUTE), lambda i: (0, i)),
                   pl.BlockSpec((TM_ROUTE, LANES), lambda i: (i, 0)),
                   pl.BlockSpec((N_EXPERTS, LANES), lambda i: (0, 0))],
        out_shape=[jax.ShapeDtypeStruct((SUBLANES, n), jnp.int32),
                   jax.ShapeDtypeStruct((n, LANES), F32),
                   jax.ShapeDtypeStruct((N_EXPERTS, LANES), F32)],
        scratch_shapes=[pltpu.VMEM((N_EXPERTS, LANES), F32)],
        compiler_params=pltpu.CompilerParams(dimension_semantics=("arbitrary",),
                                             vmem_limit_bytes=VMEM_LIMIT),
        name="route",
    )(lg)


def _mix(sb, sgn, x2, sb_g, w_out_b, ffn_g, wr2, br):
    n = x2.shape[0]
    row = lambda i: (i, 0)
    const = lambda i: (0, 0)
    return pl.pallas_call(
        _mix_kernel,
        grid=(n // TM_MIX,),
        in_specs=[pl.BlockSpec((TM_MIX, SB_WIDTH), row),
                  pl.BlockSpec((TM_MIX, SG_WIDTH), row),
                  pl.BlockSpec((TM_MIX, D_MODEL), row),
                  pl.BlockSpec((1, SB_WIDTH), const),
                  pl.BlockSpec((D_MODEL, D_MODEL), const),
                  pl.BlockSpec((1, D_MODEL), const),
                  pl.BlockSpec((D_MODEL, 2 * LANES), const),
                  pl.BlockSpec((1, LANES), const)],
        out_specs=[pl.BlockSpec((TM_MIX, D_MODEL), row),
                   pl.BlockSpec((ROUTER_ROWS, TM_MIX), lambda i: (0, i))],
        out_shape=[jax.ShapeDtypeStruct((n, D_MODEL), F32),
                   jax.ShapeDtypeStruct((ROUTER_ROWS, n), F32)],
        compiler_params=pltpu.CompilerParams(dimension_semantics=("arbitrary",),
                                             vmem_limit_bytes=VMEM_LIMIT),
        name="mix_router",
    )(sb, sgn, x2, sb_g, w_out_b, ffn_g, wr2, br)


_PAD_BITS = tuple(1 << b for b in reversed(range(EXPERT_CHUNK.bit_length() - 1)))


def _dispatch_kernel(dest_ref, pad_start_ref, pad_count_ref, used_ref, g_ref, h_ref, zeros_ref, xs_ref,
                     hn_ref, h_buf, sem, zsem, h_sems):
    tm = TM_DISPATCH
    i = pl.program_id(0)
    n_steps = pl.num_programs(0) - 1
    n = n_steps * tm
    base = (i - 1) * tm
    prev = hn_ref.at[lax.rem(i + 1, 2)]
    n_chunks = xs_ref.shape[0] // (EXPERT_CHUNK * ROW_TILE)

    def h_copy(step):
        half = lax.rem(step, 2)
        return pltpu.make_async_copy(h_ref.at[pl.ds(pl.multiple_of(step * tm, tm), tm)], h_buf.at[half],
                                     h_sems.at[half])

    def pad_copies(do):
        for e in range(N_EXPERTS):
            start = pad_start_ref[e]
            count = pad_count_ref[e]
            for bit in _PAD_BITS:
                @pl.when((count & bit) != 0)
                def _(start=start, bit=bit):
                    do(pltpu.make_async_copy(_token_rows(zeros_ref, 0, bit),
                                             _token_rows(xs_ref, start, bit), zsem))
                start = start + (count & bit)
        for k in range(N_EXPERTS):
            chunk = used_ref[0] + k

            @pl.when(chunk < n_chunks)
            def _(chunk=chunk):
                do(pltpu.make_async_copy(zeros_ref, _token_rows(xs_ref, chunk * EXPERT_CHUNK, EXPERT_CHUNK),
                                         zsem))

    @pl.when(i == 0)
    def _():
        h_copy(0).start(priority=1)
        pad_copies(lambda cp: cp.start())

    @pl.when(i + 1 < n_steps)
    def _():
        h_copy(i + 1).start(priority=1)

    @pl.when(i > 0)
    def _():
        def body(r, c):
            src = _token_rows(prev, r, 1)
            for s in range(2):
                pltpu.make_async_copy(src, _token_rows(xs_ref, dest_ref[s * n + base + r], 1),
                                      sem).start(priority=s)
            return c

        lax.fori_loop(0, tm, body, 0, unroll=8)

    @pl.when(i < n_steps)
    def _():
        h_copy(i).wait()
        _rows_to_tiles(hn_ref.at[lax.rem(i, 2)], _rms(h_buf[lax.rem(i, 2)], g_ref[...]))

    @pl.when(i > 0)
    def _():
        for _ in range(2):
            pltpu.make_async_copy(prev, _token_rows(xs_ref, 0, tm), sem).wait()

    @pl.when(i == n_steps)
    def _():
        pad_copies(lambda cp: cp.wait())


def _dispatch(dest, pad_start, pad_count, used_chunks, h, ffn_g, n_rows):
    n_steps = h.shape[0] // TM_DISPATCH
    zeros = jnp.zeros((EXPERT_CHUNK * ROW_TILE, LANES), F32)
    return pl.pallas_call(
        _dispatch_kernel,
        grid_spec=pltpu.PrefetchScalarGridSpec(
            num_scalar_prefetch=4,
            grid=(n_steps + 1,),
            in_specs=[pl.BlockSpec((1, D_MODEL), lambda i, *_: (0, 0)),
                      pl.BlockSpec(memory_space=pl.ANY),
                      pl.BlockSpec(memory_space=pl.ANY)],
            out_specs=pl.BlockSpec(memory_space=pl.ANY),
            scratch_shapes=[pltpu.VMEM((2, TM_DISPATCH * ROW_TILE, LANES), F32),
                            pltpu.VMEM((2, TM_DISPATCH, D_MODEL), F32),
                            pltpu.SemaphoreType.DMA, pltpu.SemaphoreType.DMA,
                            pltpu.SemaphoreType.DMA((2,))]),
        out_shape=jax.ShapeDtypeStruct((n_rows * ROW_TILE, LANES), F32),
        compiler_params=pltpu.CompilerParams(dimension_semantics=("arbitrary",),
                                             vmem_limit_bytes=VMEM_LIMIT),
        name="dispatch",
    )(dest, pad_start, pad_count, used_chunks, ffn_g, h, zeros)


X_SLOTS = 3
TILE_CHUNKS = TM_EXPERT // EXPERT_CHUNK
W_SLOTS = 3


def _expert_kernel(tiles_ref, chunk0_ref, chunks_ref, nt_ref, used_ref, xs_ref, wg_ref, wu_ref, wd_ref,
                   zeros_ref, ys_ref, x_buf, y_buf, sg_buf, su_buf, sd_buf, wgb, wub, wdb, state,
                   w_sems, x_sems, y_sems, zsem):
    t = pl.program_id(0)
    last = pl.num_programs(0) - 1
    nt = nt_ref[0]
    n_chunks = ys_ref.shape[0] // (EXPERT_CHUNK * ROW_TILE)

    def tile_copies(tile, do, out):
        for c in range(TILE_CHUNKS):
            @pl.when(c < chunks_ref[tile])
            def _(c=c):
                first = (chunk0_ref[tile] + c) * EXPERT_CHUNK
                if out:
                    slot = lax.rem(tile, 2)
                    do(pltpu.make_async_copy(_token_rows(y_buf.at[slot], c * EXPERT_CHUNK, EXPERT_CHUNK),
                                             _token_rows(ys_ref, first, EXPERT_CHUNK), y_sems.at[slot]))
                else:
                    slot = lax.rem(tile, X_SLOTS)
                    do(pltpu.make_async_copy(_token_rows(xs_ref, first, EXPERT_CHUNK),
                                             _token_rows(x_buf.at[slot], c * EXPERT_CHUNK, EXPERT_CHUNK),
                                             x_sems.at[slot]))

    start = lambda cp: cp.start()
    wait = lambda cp: cp.wait()

    def tail_copies(do):
        for k in range(N_EXPERTS):
            chunk = used_ref[0] + k

            @pl.when(chunk < n_chunks)
            def _(chunk=chunk):
                do(pltpu.make_async_copy(zeros_ref, _token_rows(ys_ref, chunk * EXPERT_CHUNK, EXPERT_CHUNK),
                                         zsem))

    def weight_copies(e, slot):
        return (pltpu.make_async_copy(wg_ref.at[e], sg_buf.at[slot], w_sems.at[slot]),
                pltpu.make_async_copy(wu_ref.at[e], su_buf.at[slot], w_sems.at[slot]),
                pltpu.make_async_copy(wd_ref.at[e], sd_buf.at[slot], w_sems.at[slot]))

    def next_with_rows(e):
        return lax.while_loop(lambda k: (k < N_EXPERTS) & (tiles_ref[jnp.minimum(k, N_EXPERTS - 1)] == 0),
                              lambda k: k + 1, e + 1)

    @pl.when(t == 0)
    def _():
        first = next_with_rows(jnp.int32(-1))
        second = next_with_rows(first)
        state[0] = jnp.int32(-1)
        state[1] = jnp.int32(0)
        state[2] = jnp.int32(W_SLOTS - 1)
        state[3] = first
        state[4] = second
        for cp in weight_copies(first, 0):
            cp.start()

        @pl.when(second < N_EXPERTS)
        def _():
            for cp in weight_copies(second, 1):
                cp.start()

        tile_copies(0, start, False)

        @pl.when(nt > 1)
        def _():
            tile_copies(1, start, False)

        tail_copies(start)

    @pl.when(t + 2 < nt)
    def _():
        tile_copies(t + 2, start, False)

    @pl.when(t < nt)
    def _():
        @pl.when(state[1] == 0)
        def _():
            e = state[3]
            nxt = state[4]
            slot = lax.rem(state[2] + 1, W_SLOTS)
            after_next = next_with_rows(nxt)
            state[0] = e
            state[1] = tiles_ref[e]
            state[2] = slot
            state[3] = nxt
            state[4] = after_next
            for cp in weight_copies(e, slot):
                cp.wait()

            @pl.when(after_next < N_EXPERTS)
            def _():
                for cp in weight_copies(after_next, lax.rem(slot + 2, W_SLOTS)):
                    cp.start()

            wgb[...] = sg_buf[slot].astype(BF16)
            wub[...] = su_buf[slot].astype(BF16)
            wdb[...] = sd_buf[slot].astype(BF16)

        state[1] = state[1] - 1
        tile_copies(t, wait, False)

        @pl.when(t >= 2)
        def _():
            tile_copies(t - 2, wait, True)

        for n_chunks_here in range(1, TILE_CHUNKS + 1):
            @pl.when(chunks_ref[t] == n_chunks_here)
            def _(m=n_chunks_here * EXPERT_CHUNK):
                x = _tiles_to_rows(x_buf.at[lax.rem(t, X_SLOTS)], m).astype(BF16)
                g = _dot(x, wgb[...])
                u = _dot(x, wub[...])
                hidden = (g * jax.nn.sigmoid(g)) * u
                _rows_to_tiles(y_buf.at[lax.rem(t, 2)], _dot(hidden.astype(BF16), wdb[...]))

        tile_copies(t, start, True)

    @pl.when(t == last)
    def _():
        for back in (2, 1):
            @pl.when(nt >= back)
            def _(back=back):
                tile_copies(nt - back, wait, True)

        tail_copies(wait)


def _experts(tiles, chunk0, chunks, n_tiles, used_chunks, xs, wg, wu, wd):
    any_spec = pl.BlockSpec(memory_space=pl.ANY)
    zeros = jnp.zeros((EXPERT_CHUNK * ROW_TILE, LANES), F32)
    return pl.pallas_call(
        _expert_kernel,
        grid_spec=pltpu.PrefetchScalarGridSpec(
            num_scalar_prefetch=5,
            grid=(chunks.shape[0],),
            in_specs=[any_spec, any_spec, any_spec, any_spec, any_spec],
            out_specs=any_spec,
            scratch_shapes=[pltpu.VMEM((X_SLOTS, TM_EXPERT * ROW_TILE, LANES), F32),
                            pltpu.VMEM((2, TM_EXPERT * ROW_TILE, LANES), F32),
                            pltpu.VMEM((W_SLOTS, D_MODEL, D_EXPERT), F32),
                            pltpu.VMEM((W_SLOTS, D_MODEL, D_EXPERT), F32),
                            pltpu.VMEM((W_SLOTS, D_EXPERT, D_MODEL), F32),
                            pltpu.VMEM((D_MODEL, D_EXPERT), BF16),
                            pltpu.VMEM((D_MODEL, D_EXPERT), BF16),
                            pltpu.VMEM((D_EXPERT, D_MODEL), BF16),
                            pltpu.SMEM((5,), jnp.int32),
                            pltpu.SemaphoreType.DMA((W_SLOTS,)),
                            pltpu.SemaphoreType.DMA((X_SLOTS,)),
                            pltpu.SemaphoreType.DMA((2,)),
                            pltpu.SemaphoreType.DMA]),
        out_shape=jax.ShapeDtypeStruct(xs.shape, F32),
        compiler_params=pltpu.CompilerParams(dimension_semantics=("arbitrary",),
                                             vmem_limit_bytes=VMEM_LIMIT),
        name="expert_mlp",
    )(tiles, chunk0, chunks, n_tiles, used_chunks, xs, wg, wu, wd, zeros)


def _combine_kernel(dest_ref, rw_ref, fg_ref, h_ref, y_ref, o_ref, buf, h_buf, sems, h_sems):
    tm = TM_COMBINE
    i = pl.program_id(0)
    n_steps = pl.num_programs(0)
    n = n_steps * tm
    cur = i % 2

    def h_copy(step, half):
        return pltpu.make_async_copy(h_ref.at[pl.ds(pl.multiple_of(step * tm, tm), tm)], h_buf.at[half],
                                     h_sems.at[half])

    def fetch(step, half):
        h_copy(step, half).start(priority=1)

        def body(r, c):
            for s in range(2):
                pltpu.make_async_copy(_token_rows(y_ref, dest_ref[s * n + step * tm + r], 1),
                                      _token_rows(buf.at[half, s], r, 1),
                                      sems.at[half]).start(priority=s)
            return c

        lax.fori_loop(0, tm, body, 0, unroll=8)

    @pl.when(i == 0)
    def _():
        fetch(0, 0)

    @pl.when(i + 1 < n_steps)
    def _():
        fetch(i + 1, 1 - cur)

    for s in range(2):
        pltpu.make_async_copy(_token_rows(y_ref, 0, tm), buf.at[cur, s], sems.at[cur]).wait()
    h_copy(i, cur).wait()
    rw = rw_ref[...]
    out = (h_buf[cur] + rw[:, 0:1] * _tiles_to_rows(buf.at[cur, 0], tm)
           + rw[:, 1:2] * _tiles_to_rows(buf.at[cur, 1], tm))
    o_ref[...] = _rms(out, fg_ref[...])


def _combine(dest, h, rw, final_g, ys):
    n = h.shape[0]
    return pl.pallas_call(
        _combine_kernel,
        grid_spec=pltpu.PrefetchScalarGridSpec(
            num_scalar_prefetch=1,
            grid=(n // TM_COMBINE,),
            in_specs=[pl.BlockSpec((TM_COMBINE, LANES), lambda i, d: (i, 0)),
                      pl.BlockSpec((1, D_MODEL), lambda i, d: (0, 0)),
                      pl.BlockSpec(memory_space=pl.ANY),
                      pl.BlockSpec(memory_space=pl.ANY)],
            out_specs=pl.BlockSpec((TM_COMBINE, D_MODEL), lambda i, d: (i, 0)),
            scratch_shapes=[pltpu.VMEM((2, 2, TM_COMBINE * ROW_TILE, LANES), F32),
                            pltpu.VMEM((2, TM_COMBINE, D_MODEL), F32),
                            pltpu.SemaphoreType.DMA((2,)),
                            pltpu.SemaphoreType.DMA((2,))]),
        out_shape=jax.ShapeDtypeStruct((n, D_MODEL), F32),
        compiler_params=pltpu.CompilerParams(dimension_semantics=("arbitrary",),
                                             vmem_limit_bytes=VMEM_LIMIT),
        name="combine",
    )(dest, rw, final_g, h, ys)


def _schedule(counts, max_tiles):
    chunks = (counts + EXPERT_CHUNK - 1) // EXPERT_CHUNK
    chunk_end = jnp.cumsum(chunks)
    chunk_start = chunk_end - chunks
    tiles = (chunks + TILE_CHUNKS - 1) // TILE_CHUNKS
    tile_end = jnp.cumsum(tiles)
    tile = jnp.arange(max_tiles, dtype=jnp.int32)
    owner = jnp.sum(tile[:, None] >= tile_end[None, :], axis=1)
    is_owner = owner[:, None] == jnp.arange(N_EXPERTS, dtype=jnp.int32)[None, :]
    of_owner = lambda v: jnp.sum(jnp.where(is_owner, v[None, :], 0), axis=1)
    done = (tile - of_owner(tile_end - tiles)) * TILE_CHUNKS
    tile_chunk0 = (of_owner(chunk_start) + done).astype(jnp.int32)
    tile_chunks = jnp.clip(of_owner(chunks) - done, 0, TILE_CHUNKS).astype(jnp.int32)
    return tiles, chunk_start * EXPERT_CHUNK, tile_chunk0, tile_chunks, tile_end[-1:], chunk_end[-1:]


def _layer(x, attn_g, w_in, sg_g, w_sp, b_sp, sb_g, sg_out_g, w_out, ffn_g,
           w_rg, b_rg, w_re, b_re, w_gate, w_up, w_down):
    batch, seq, _ = x.shape
    n = batch * seq
    x2 = x.reshape(n, D_MODEL)
    row = lambda v: v.reshape(1, -1)

    bsp_full = jnp.repeat(b_sp.T, HEAD_DIM, axis=1)
    qkv, sgn = _inproj(x2, row(attn_g), w_in.astype(BF16), row(sg_g), w_sp, bsp_full, row(sg_out_g))
    sb = _attention(qkv, batch, seq).reshape(n, SB_WIDTH)

    pad_lanes = lambda v, width: jnp.pad(v, [(0, 0)] * (v.ndim - 1) + [(0, width - v.shape[-1])])
    w_r = jnp.concatenate([pad_lanes(w_rg, ROUTER_LANE0),
                           jnp.transpose(w_re, (1, 0, 2)).reshape(D_MODEL, N_EXPERTS)], axis=1)
    w_r = pad_lanes(w_r, LANES)
    wr_hi = w_r.astype(BF16)
    wr_lo = (w_r - wr_hi.astype(F32)).astype(BF16)
    wr2 = jnp.concatenate([wr_hi, wr_lo], axis=1)
    b_r = pad_lanes(jnp.concatenate([pad_lanes(b_rg, ROUTER_LANE0), b_re.reshape(-1)]), LANES)

    h, lg = _mix(sb, sgn, x2, row(sb_g), w_out.astype(BF16), row(ffn_g), wr2, row(b_r))
    ri, rw, cnt = _route(lg)

    counts = cnt[:, 0].astype(jnp.int32)
    n_rows = 2 * n + N_EXPERTS * EXPERT_CHUNK
    tiles, offsets, tile_chunk0, tile_chunks, n_tiles, used_chunks = _schedule(
        counts, 2 * n // TM_EXPERT + N_EXPERTS)
    expert, rank = ri[0:2], ri[2:4]
    is_e = expert[None] == jnp.arange(N_EXPERTS, dtype=jnp.int32)[:, None, None]
    dest = (jnp.sum(jnp.where(is_e, offsets[:, None, None], 0), axis=0) + rank).reshape(-1)
    pad_start = offsets + counts
    pad_count = (-counts) % EXPERT_CHUNK

    xs = _dispatch(dest, pad_start, pad_count, used_chunks, h, row(ffn_g), n_rows)
    ys = _experts(tiles, tile_chunk0, tile_chunks, n_tiles, used_chunks, xs,
                  w_gate.reshape(N_EXPERTS, D_MODEL, D_EXPERT),
                  w_up.reshape(N_EXPERTS, D_MODEL, D_EXPERT),
                  w_down.reshape(N_EXPERTS, D_EXPERT, D_MODEL))
    return dest, h, rw, ys


def kernel(x, attn_norm_g, w_in, sg_norm_g, w_spatial, b_spatial, sb_out_norm_g, sg_out_norm_g,
           w_out, ffn_norm_g, w_router_group, b_router_group, w_router_expert, b_router_expert,
           w_gate, w_up, w_down, final_norm_g):
    assert attn_norm_g.shape[0] == 1, "single-layer problem"
    batch, seq, _ = x.shape
    dest, h, rw, ys = _layer(x, attn_norm_g[0], w_in[0], sg_norm_g[0], w_spatial[0], b_spatial[0],
                             sb_out_norm_g[0], sg_out_norm_g[0], w_out[0], ffn_norm_g[0],
                             w_router_group[0], b_router_group[0], w_router_expert[0],
                             b_router_expert[0], w_gate[0], w_up[0], w_down[0])
    out = _combine(dest, h, rw, final_norm_g.reshape(1, -1), ys)
    return out.reshape(batch, seq, D_MODEL)
```

```python
import functools
import math

import jax
import jax.numpy as jnp
from jax import lax
from jax.experimental import pallas as pl
from jax.experimental.pallas import tpu as pltpu

D_MODEL = 1024
HEAD_DIM = 64
SB_WIDTH = 512
SG_WIDTH = 512
SG_HEADS = 8
D_IN = 3 * SB_WIDTH + 2 * SG_WIDTH
CHUNK = 128
N_GROUPS = 4
EXPERTS_PER_GROUP = 8
N_EXPERTS = N_GROUPS * EXPERTS_PER_GROUP
D_EXPERT = 512
EPS = 1e-6
F32_EXP_UNDERFLOW = 110.0

LANES = 128
SUBLANES = 8
ROW_TILE = D_MODEL // LANES
assert ROW_TILE == SUBLANES
HEAD_PAIR = 2 * HEAD_DIM
ROUTER_LANE0 = SUBLANES
ROUTER_ROWS = ROUTER_LANE0 + N_EXPERTS
assert EXPERTS_PER_GROUP == SUBLANES and N_GROUPS <= ROUTER_LANE0

TM_PROJ = 1024
TQ_ATTN = 256
ATTN_BLOCKS_PER_STEP = 2
ATTN_TOP_ROWS = (160, 176)
TM_MIX = 1024
TM_ROUTE = 1024
TM_DISPATCH = 1024
TM_EXPERT = 640
EXPERT_CHUNK = 128
TM_COMBINE = 512
VMEM_LIMIT = 48 * 1024 * 1024

F32 = jnp.float32
BF16 = jnp.bfloat16


def _rms(x, g):
    return x * lax.rsqrt(jnp.mean(x * x, axis=-1, keepdims=True) + EPS) * g


def _gelu(x):
    c = math.sqrt(2.0 / math.pi)
    return x * (0.5 * (1.0 + jnp.tanh(c * (x + 0.044715 * (x * x * x)))))


def _softplus(z):
    return jnp.maximum(z, 0.0) + jnp.log(1.0 + jnp.exp(-jnp.abs(z)))


def _dot(a, b):
    return jnp.dot(a, b, preferred_element_type=F32)


def _rows_to_tiles(ref, x):
    m = x.shape[0]
    for k in range(ROW_TILE):
        ref[pl.ds(k, m, stride=ROW_TILE), :] = x[:, k * LANES:(k + 1) * LANES]


def _tiles_to_rows(ref, m):
    return jnp.concatenate([ref[pl.ds(k, m, stride=ROW_TILE), :] for k in range(ROW_TILE)], axis=1)


def _token_rows(ref, first_token, n_tokens):
    return ref.at[pl.ds(pl.multiple_of(first_token * ROW_TILE, ROW_TILE), n_tokens * ROW_TILE)]


def _split_bf16(x):
    hi = x.astype(BF16)
    lo = (x - hi.astype(F32)).astype(BF16)
    return hi, lo


def _inproj_kernel(x_ref, g_ref, w_ref, sgg_ref, wsp_ref, bsp_ref, sgog_ref, qkv_ref, sgn_ref,
                   gu_ref, vgn_ref, sg_ref):
    tm = TM_PROJ
    hb = _rms(x_ref[...], g_ref[...]).astype(BF16)
    gv = _gelu(_dot(hb, w_ref[:, 3 * SB_WIDTH + SG_WIDTH:D_IN]))
    vgn_ref[...] = _rms(gv, sgg_ref[...]).astype(BF16)
    gu_ref[...] = _gelu(_dot(hb, w_ref[:, 3 * SB_WIDTH:3 * SB_WIDTH + SG_WIDTH]))
    q = _dot(hb, w_ref[:, 0:SB_WIDTH]) * (1.0 / math.sqrt(HEAD_DIM))
    qkv_ref[:, 0:SB_WIDTH] = q.astype(BF16)
    qkv_ref[:, SB_WIDTH:2 * SB_WIDTH] = _dot(hb, w_ref[:, SB_WIDTH:2 * SB_WIDTH]).astype(BF16)

    lane = lax.broadcasted_iota(jnp.int32, (1, LANES), 1)
    first = lane < HEAD_DIM
    zero = jnp.zeros((), BF16)
    r_c = lax.broadcasted_iota(jnp.int32, (CHUNK, CHUNK), 0)
    c_c = lax.broadcasted_iota(jnp.int32, (CHUNK, CHUNK), 1)
    tril = r_c >= c_c
    n_pairs = SG_WIDTH // HEAD_PAIR
    w_pairs = []
    for p in range(n_pairs):
        w0 = jnp.where(tril, wsp_ref[2 * p], 0.0).astype(BF16)
        w1 = jnp.where(tril, wsp_ref[2 * p + 1], 0.0).astype(BF16)
        w_pairs.append(jnp.concatenate([w0, w1], axis=1))
    bsp = bsp_ref[...]
    for c in range(tm // CHUNK):
        rows = slice(c * CHUNK, (c + 1) * CHUNK)
        for p in range(n_pairs):
            cols = slice(p * HEAD_PAIR, (p + 1) * HEAD_PAIR)
            vg = vgn_ref[rows, cols]
            rhs = jnp.concatenate([jnp.where(first, vg, zero), jnp.where(first, zero, vg)], axis=0)
            mixed = _dot(w_pairs[p], rhs) + bsp[:, cols]
            sg_ref[rows, cols] = gu_ref[rows, cols] * mixed
    qkv_ref[:, 2 * SB_WIDTH:3 * SB_WIDTH] = _dot(hb, w_ref[:, 2 * SB_WIDTH:3 * SB_WIDTH]).astype(BF16)
    sgn_ref[...] = _rms(sg_ref[...], sgog_ref[...]).astype(BF16)


def _inproj(x2, attn_g, w_in_b, sg_g, wsp, bsp_full, sg_out_g):
    n = x2.shape[0]
    row = lambda i: (i, 0)
    const = lambda i: (0, 0)
    return pl.pallas_call(
        _inproj_kernel,
        grid=(n // TM_PROJ,),
        in_specs=[pl.BlockSpec((TM_PROJ, D_MODEL), row),
                  pl.BlockSpec((1, D_MODEL), const),
                  pl.BlockSpec((D_MODEL, D_IN), const),
                  pl.BlockSpec((1, SG_WIDTH), const),
                  pl.BlockSpec((SG_HEADS, CHUNK, CHUNK), lambda i: (0, 0, 0)),
                  pl.BlockSpec((CHUNK, SG_WIDTH), const),
                  pl.BlockSpec((1, SG_WIDTH), const)],
        out_specs=[pl.BlockSpec((TM_PROJ, 3 * SB_WIDTH), row),
                   pl.BlockSpec((TM_PROJ, SG_WIDTH), row)],
        out_shape=[jax.ShapeDtypeStruct((n, 3 * SB_WIDTH), BF16),
                   jax.ShapeDtypeStruct((n, SG_WIDTH), BF16)],
        scratch_shapes=[pltpu.VMEM((TM_PROJ, SG_WIDTH), F32),
                        pltpu.VMEM((TM_PROJ, SG_WIDTH), BF16),
                        pltpu.VMEM((TM_PROJ, SG_WIDTH), F32)],
        compiler_params=pltpu.CompilerParams(dimension_semantics=("arbitrary",),
                                             vmem_limit_bytes=VMEM_LIMIT),
        name="inproj",
    )(x2, attn_g, w_in_b, sg_g, wsp, bsp_full, sg_out_g)


def _attn_kernel(q_ref, k_ref, v_ref, o_ref, q2_ref, carry_ref):
    t = TQ_ATTN
    n_pairs = SB_WIDTH // HEAD_PAIR
    lane = lax.broadcasted_iota(jnp.int32, (1, HEAD_PAIR), 1)
    head_lanes = (lane < HEAD_DIM, lane >= HEAD_DIM)
    zero = jnp.zeros((), BF16)
    r_idx = lax.broadcasted_iota(jnp.int32, (t, t), 0)
    c_idx = lax.broadcasted_iota(jnp.int32, (t, t), 1)
    suffix = (r_idx > c_idx).astype(BF16)
    suffix2 = jnp.concatenate([suffix, suffix], axis=0)
    causal = c_idx < r_idx

    def one_query_block(sub, c):
        qi = pl.program_id(1) * ATTN_BLOCKS_PER_STEP + sub
        row0 = pl.multiple_of(sub * t, t)
        for p in range(n_pairs):
            qp = q_ref[0, pl.ds(row0, t), p * HEAD_PAIR:(p + 1) * HEAD_PAIR]
            for h in range(2):
                q2_ref[(2 * p + h) * t:(2 * p + h + 1) * t, :] = jnp.where(head_lanes[h], qp, zero)
        o_ref[0, pl.ds(row0, t), :] = jnp.zeros((t, SB_WIDTH), F32)
        carry_ref[...] = jnp.zeros_like(carry_ref)

        def block(j, diag, m):
            start = pl.multiple_of(j * t, t)
            mask2 = jnp.concatenate([causal, causal], axis=0) if diag else None
            st = [dict() for _ in range(n_pairs)]

            def head_rows(p):
                return [slice((2 * p + h) * t, (2 * p + h) * t + m) for h in range(2)]

            def scores(p):
                d = st[p]
                d["cols"] = slice(p * HEAD_PAIR, (p + 1) * HEAD_PAIR)
                kb = k_ref[0, pl.ds(start, t), d["cols"]]
                q2 = jnp.concatenate([q2_ref[r, :] for r in head_rows(p)], axis=0)
                z = lax.dot_general(q2, kb, (((1,), (1,)), ((), ())),
                                    preferred_element_type=F32)
                sp = _softplus(z)
                nl = jnp.where(mask2, sp, 0.0) if diag else sp
                hi, lo = _split_bf16(nl)
                d["hl"] = jnp.concatenate([hi, lo], axis=1)
                d["log_beta"] = z - sp
                d["nl0"] = nl[:, 0:1]

            def weights(p):
                d = st[p]
                hl = d["hl"]
                after = jnp.concatenate([_dot(hl[0:m], suffix2), _dot(hl[m:2 * m], suffix2)], axis=0)
                carry = jnp.concatenate([carry_ref[r, :] for r in head_rows(p)], axis=0)
                a = jnp.exp(d["log_beta"] - after - carry)
                if diag:
                    a = jnp.where(mask2, a, 0.0)
                a = a.astype(BF16)
                d["a2"] = jnp.concatenate([a[0:m], a[m:2 * m]], axis=1)
                new_carry = carry + after[:, 0:1] + d["nl0"]
                for h, r in enumerate(head_rows(p)):
                    carry_ref[r, :] = new_carry[h * m:(h + 1) * m]

            def values(p):
                d = st[p]
                vb = v_ref[0, pl.ds(start, t), d["cols"]]
                v2 = jnp.concatenate([jnp.where(head_lanes[0], vb, zero),
                                      jnp.where(head_lanes[1], vb, zero)], axis=0)
                o_ref[0, pl.ds(row0, m), d["cols"]] += _dot(d["a2"], v2)

            for step in range(n_pairs + 2):
                if step < n_pairs:
                    scores(step)
                if 0 <= step - 1 < n_pairs:
                    weights(step - 1)
                if 0 <= step - 2 < n_pairs:
                    values(step - 2)

        def flags():
            bounds = (0,) + ATTN_TOP_ROWS + (t,)
            lowest = [jnp.min(jnp.concatenate([carry_ref[hh * t + lo:hh * t + hi, :] for hh in range(2 * n_pairs)],
                                              axis=0))
                      for lo, hi in zip(bounds[:-1], bounds[1:])]
            below = [functools.reduce(jnp.minimum, lowest[k:]) for k in range(len(lowest))]
            return (below[0] < F32_EXP_UNDERFLOW,) + tuple(b >= F32_EXP_UNDERFLOW for b in below[1:])

        block(qi, True, t)

        def body(state):
            it, _, *done = state
            j = qi - 1 - it
            for k, m in enumerate(ATTN_TOP_ROWS + (t,)):
                use = done[k] if k < len(done) else True
                if k > 0:
                    use = jnp.logical_and(use, jnp.logical_not(done[k - 1]))

                @pl.when(use)
                def _(m=m):
                    block(j, False, m)

            return (it + 1,) + flags()

        lax.while_loop(lambda s: (s[0] < qi) & s[1], body, (jnp.int32(0),) + flags())
        return c

    lax.fori_loop(0, ATTN_BLOCKS_PER_STEP, one_query_block, 0)


def _attention(qkv, batch, seq):
    qkv3 = qkv.reshape(batch, seq, 3 * SB_WIDTH)
    n_heads = SB_WIDTH // HEAD_DIM
    return pl.pallas_call(
        _attn_kernel,
        grid=(batch, seq // (ATTN_BLOCKS_PER_STEP * TQ_ATTN)),
        in_specs=[pl.BlockSpec((1, ATTN_BLOCKS_PER_STEP * TQ_ATTN, SB_WIDTH), lambda b, i: (b, i, 0)),
                  pl.BlockSpec((1, seq, SB_WIDTH), lambda b, i: (b, 0, 1)),
                  pl.BlockSpec((1, seq, SB_WIDTH), lambda b, i: (b, 0, 2))],
        out_specs=pl.BlockSpec((1, ATTN_BLOCKS_PER_STEP * TQ_ATTN, SB_WIDTH), lambda b, i: (b, i, 0)),
        out_shape=jax.ShapeDtypeStruct((batch, seq, SB_WIDTH), F32),
        scratch_shapes=[pltpu.VMEM((n_heads * TQ_ATTN, HEAD_PAIR), BF16),
                        pltpu.VMEM((n_heads * TQ_ATTN, 1), F32)],
        compiler_params=pltpu.CompilerParams(dimension_semantics=("arbitrary",) * 2,
                                             vmem_limit_bytes=VMEM_LIMIT),
        name="sb_attention",
    )(qkv3, qkv3, qkv3)


def _mix_kernel(sb_ref, sgn_ref, x_ref, sbg_ref, wout_ref, ffng_ref, wr2_ref, br_ref,
                h_ref, lg_ref):
    sbn = _rms(sb_ref[...], sbg_ref[...]).astype(BF16)
    h = x_ref[...] + _dot(sbn, wout_ref[0:SB_WIDTH, :]) + _dot(sgn_ref[...], wout_ref[SB_WIDTH:, :])
    h_ref[...] = h
    hn = _rms(h, ffng_ref[...])

    hn_hi, hn_lo = _split_bf16(hn)
    both = _dot(hn_hi, wr2_ref[...])
    logits = both[:, 0:LANES] + both[:, LANES:] + _dot(hn_lo, wr2_ref[:, 0:LANES]) + br_ref[...]
    lg_ref[...] = logits.T[0:ROUTER_ROWS, :]


def _route_kernel(lg_ref, ri_ref, rw_ref, cnt_ref, count_ref):
    tr = TM_ROUTE
    i = pl.program_id(0)

    @pl.when(i == 0)
    def _():
        count_ref[...] = jnp.zeros_like(count_ref)

    neg = jnp.float32(-jnp.inf)
    row8 = lax.broadcasted_iota(jnp.int32, (SUBLANES, tr), 0)

    def top(v):
        m = jnp.max(v, axis=0, keepdims=True)
        return m, jnp.min(jnp.where(v == m, row8, SUBLANES), axis=0, keepdims=True)

    def group_rows(g):
        return lg_ref[ROUTER_LANE0 + g * EXPERTS_PER_GROUP:ROUTER_LANE0 + (g + 1) * EXPERTS_PER_GROUP, :]

    gl = jnp.where(row8 < N_GROUPS, lg_ref[0:SUBLANES, :], neg)
    gmax, gidx = top(gl)
    gweight = 1.0 / jnp.sum(jnp.exp(gl - gmax), axis=0, keepdims=True)
    el = group_rows(0)
    for g in range(1, N_GROUPS):
        el = jnp.where(gidx == g, group_rows(g), el)
    m1, i1 = top(el)
    m2, i2 = top(jnp.where(row8 == i1, neg, el))
    t21 = jnp.exp(m2 - m1)
    w1 = gweight / (1.0 + t21)
    w2 = gweight * t21 / (1.0 + t21)
    e1 = gidx * EXPERTS_PER_GROUP + i1
    e2 = gidx * EXPERTS_PER_GROUP + i2

    row_e = lax.broadcasted_iota(jnp.int32, (N_EXPERTS, tr), 0)
    sel1 = row_e == e1
    sel2 = row_e == e2
    onehot = jnp.where(sel1 | sel2, 1.0, 0.0)
    r_t = lax.broadcasted_iota(jnp.int32, (tr, tr), 0)
    c_t = lax.broadcasted_iota(jnp.int32, (tr, tr), 1)
    before = (r_t < c_t).astype(BF16)
    running = count_ref[:, 0:1] + _dot(onehot.astype(BF16), before)
    rank1 = jnp.sum(jnp.where(sel1, running, 0.0), axis=0, keepdims=True)
    rank2 = jnp.sum(jnp.where(sel2, running, 0.0), axis=0, keepdims=True)
    new_count = count_ref[:, 0:1] + jnp.sum(onehot, axis=1, keepdims=True)
    count_ref[...] = jnp.broadcast_to(new_count, count_ref.shape)
    cnt_ref[...] = jnp.broadcast_to(new_count, cnt_ref.shape)

    ri_ref[...] = jnp.where(row8 == 0, e1, jnp.where(row8 == 1, e2, jnp.where(
        row8 == 2, rank1.astype(jnp.int32), jnp.where(row8 == 3, rank2.astype(jnp.int32), 0))))
    row128 = lax.broadcasted_iota(jnp.int32, (LANES, tr), 0)
    rw_ref[...] = jnp.where(row128 == 0, w1, jnp.where(row128 == 1, w2, 0.0)).T


def _route(lg):
    n = lg.shape[1]
    return pl.pallas_call(
        _route_kernel,
        grid=(n // TM_ROUTE,),
        in_specs=[pl.BlockSpec((ROUTER_ROWS, TM_ROUTE), lambda i: (0, i))],
        out_specs=[pl.BlockSpec((SUBLANES, TM_ROUTE), lambda i: (0, i)),
                   pl.BlockSpec((TM_ROUTE, LANES), lambda i: (i, 0)),
                   pl.BlockSpec((N_EXPERTS, LANES), lambda i: (0, 0))],
        out_shape=[jax.ShapeDtypeStruct((SUBLANES, n), jnp.int32),
                   jax.ShapeDtypeStruct((n, LANES), F32),
                   jax.ShapeDtypeStruct((N_EXPERTS, LANES), F32)],
        scratch_shapes=[pltpu.VMEM((N_EXPERTS, LANES), F32)],
        compiler_params=pltpu.CompilerParams(dimension_semantics=("arbitrary",),
                                             vmem_limit_bytes=VMEM_LIMIT),
        name="route",
    )(lg)


def _mix(sb, sgn, x2, sb_g, w_out_b, ffn_g, wr2, br):
    n = x2.shape[0]
    row = lambda i: (i, 0)
    const = lambda i: (0, 0)
    return pl.pallas_call(
        _mix_kernel,
        grid=(n // TM_MIX,),
        in_specs=[pl.BlockSpec((TM_MIX, SB_WIDTH), row),
                  pl.BlockSpec((TM_MIX, SG_WIDTH), row),
                  pl.BlockSpec((TM_MIX, D_MODEL), row),
                  pl.BlockSpec((1, SB_WIDTH), const),
                  pl.BlockSpec((D_MODEL, D_MODEL), const),
                  pl.BlockSpec((1, D_MODEL), const),
                  pl.BlockSpec((D_MODEL, 2 * LANES), const),
                  pl.BlockSpec((1, LANES), const)],
        out_specs=[pl.BlockSpec((TM_MIX, D_MODEL), row),
                   pl.BlockSpec((ROUTER_ROWS, TM_MIX), lambda i: (0, i))],
        out_shape=[jax.ShapeDtypeStruct((n, D_MODEL), F32),
                   jax.ShapeDtypeStruct((ROUTER_ROWS, n), F32)],
        compiler_params=pltpu.CompilerParams(dimension_semantics=("arbitrary",),
                                             vmem_limit_bytes=VMEM_LIMIT),
        name="mix_router",
    )(sb, sgn, x2, sb_g, w_out_b, ffn_g, wr2, br)


_PAD_BITS = tuple(1 << b for b in reversed(range(EXPERT_CHUNK.bit_length() - 1)))


def _dispatch_kernel(dest_ref, pad_start_ref, pad_count_ref, used_ref, h_ref, g_ref, zeros_ref, xs_ref,
                     hn_ref, sem, zsem):
    tm = TM_DISPATCH
    i = pl.program_id(0)
    n_steps = pl.num_programs(0) - 1
    n = n_steps * tm
    base = (i - 1) * tm
    prev = hn_ref.at[lax.rem(i + 1, 2)]
    n_chunks = xs_ref.shape[0] // (EXPERT_CHUNK * ROW_TILE)

    def pad_copies(do):
        for e in range(N_EXPERTS):
            start = pad_start_ref[e]
            count = pad_count_ref[e]
            for bit in _PAD_BITS:
                @pl.when((count & bit) != 0)
                def _(start=start, bit=bit):
                    do(pltpu.make_async_copy(_token_rows(zeros_ref, 0, bit),
                                             _token_rows(xs_ref, start, bit), zsem))
                start = start + (count & bit)
        for k in range(N_EXPERTS):
            chunk = used_ref[0] + k

            @pl.when(chunk < n_chunks)
            def _(chunk=chunk):
                do(pltpu.make_async_copy(zeros_ref, _token_rows(xs_ref, chunk * EXPERT_CHUNK, EXPERT_CHUNK),
                                         zsem))

    @pl.when(i == 0)
    def _():
        pad_copies(lambda cp: cp.start())

    @pl.when(i > 0)
    def _():
        def body(r, c):
            src = _token_rows(prev, r, 1)
            for s in range(2):
                pltpu.make_async_copy(src, _token_rows(xs_ref, dest_ref[s * n + base + r], 1),
                                      sem).start(priority=s)
            return c

        lax.fori_loop(0, tm, body, 0, unroll=8)

    @pl.when(i < n_steps)
    def _():
        _rows_to_tiles(hn_ref.at[lax.rem(i, 2)], _rms(h_ref[...], g_ref[...]))

    @pl.when(i > 0)
    def _():
        for _ in range(2):
            pltpu.make_async_copy(prev, _token_rows(xs_ref, 0, tm), sem).wait()

    @pl.when(i == n_steps)
    def _():
        pad_copies(lambda cp: cp.wait())


def _dispatch(dest, pad_start, pad_count, used_chunks, h, ffn_g, n_rows):
    n_steps = h.shape[0] // TM_DISPATCH
    zeros = jnp.zeros((EXPERT_CHUNK * ROW_TILE, LANES), F32)
    return pl.pallas_call(
        _dispatch_kernel,
        grid_spec=pltpu.PrefetchScalarGridSpec(
            num_scalar_prefetch=4,
            grid=(n_steps + 1,),
            in_specs=[pl.BlockSpec((TM_DISPATCH, D_MODEL), lambda i, *_: (jnp.minimum(i, n_steps - 1), 0)),
                      pl.BlockSpec((1, D_MODEL), lambda i, *_: (0, 0)),
                      pl.BlockSpec(memory_space=pl.ANY)],
            out_specs=pl.BlockSpec(memory_space=pl.ANY),
            scratch_shapes=[pltpu.VMEM((2, TM_DISPATCH * ROW_TILE, LANES), F32),
                            pltpu.SemaphoreType.DMA, pltpu.SemaphoreType.DMA]),
        out_shape=jax.ShapeDtypeStruct((n_rows * ROW_TILE, LANES), F32),
        compiler_params=pltpu.CompilerParams(dimension_semantics=("arbitrary",),
                                             vmem_limit_bytes=VMEM_LIMIT),
        name="dispatch",
    )(dest, pad_start, pad_count, used_chunks, h, ffn_g, zeros)


X_SLOTS = 3
TILE_CHUNKS = TM_EXPERT // EXPERT_CHUNK
W_SLOTS = 3


def _expert_kernel(tiles_ref, chunk0_ref, chunks_ref, nt_ref, used_ref, xs_ref, wg_ref, wu_ref, wd_ref,
                   zeros_ref, ys_ref, x_buf, y_buf, sg_buf, su_buf, sd_buf, wgb, wub, wdb, state,
                   w_sems, x_sems, y_sems, zsem):
    t = pl.program_id(0)
    last = pl.num_programs(0) - 1
    nt = nt_ref[0]
    n_chunks = ys_ref.shape[0] // (EXPERT_CHUNK * ROW_TILE)

    def tile_copies(tile, do, out):
        for c in range(TILE_CHUNKS):
            @pl.when(c < chunks_ref[tile])
            def _(c=c):
                first = (chunk0_ref[tile] + c) * EXPERT_CHUNK
                if out:
                    slot = lax.rem(tile, 2)
                    do(pltpu.make_async_copy(_token_rows(y_buf.at[slot], c * EXPERT_CHUNK, EXPERT_CHUNK),
                                             _token_rows(ys_ref, first, EXPERT_CHUNK), y_sems.at[slot]))
                else:
                    slot = lax.rem(tile, X_SLOTS)
                    do(pltpu.make_async_copy(_token_rows(xs_ref, first, EXPERT_CHUNK),
                                             _token_rows(x_buf.at[slot], c * EXPERT_CHUNK, EXPERT_CHUNK),
                                             x_sems.at[slot]))

    start = lambda cp: cp.start()
    wait = lambda cp: cp.wait()

    def tail_copies(do):
        for k in range(N_EXPERTS):
            chunk = used_ref[0] + k

            @pl.when(chunk < n_chunks)
            def _(chunk=chunk):
                do(pltpu.make_async_copy(zeros_ref, _token_rows(ys_ref, chunk * EXPERT_CHUNK, EXPERT_CHUNK),
                                         zsem))

    def weight_copies(e, slot):
        return (pltpu.make_async_copy(wg_ref.at[e], sg_buf.at[slot], w_sems.at[slot]),
                pltpu.make_async_copy(wu_ref.at[e], su_buf.at[slot], w_sems.at[slot]),
                pltpu.make_async_copy(wd_ref.at[e], sd_buf.at[slot], w_sems.at[slot]))

    def next_with_rows(e):
        return lax.while_loop(lambda k: (k < N_EXPERTS) & (tiles_ref[jnp.minimum(k, N_EXPERTS - 1)] == 0),
                              lambda k: k + 1, e + 1)

    @pl.when(t == 0)
    def _():
        first = next_with_rows(jnp.int32(-1))
        second = next_with_rows(first)
        state[0] = jnp.int32(-1)
        state[1] = jnp.int32(0)
        state[2] = jnp.int32(W_SLOTS - 1)
        state[3] = first
        state[4] = second
        for cp in weight_copies(first, 0):
            cp.start()

        @pl.when(second < N_EXPERTS)
        def _():
            for cp in weight_copies(second, 1):
                cp.start()

        tile_copies(0, start, False)

        @pl.when(nt > 1)
        def _():
            tile_copies(1, start, False)

        tail_copies(start)

    @pl.when(t + 2 < nt)
    def _():
        tile_copies(t + 2, start, False)

    @pl.when(t < nt)
    def _():
        @pl.when(state[1] == 0)
        def _():
            e = state[3]
            nxt = state[4]
            slot = lax.rem(state[2] + 1, W_SLOTS)
            after_next = next_with_rows(nxt)
            state[0] = e
            state[1] = tiles_ref[e]
            state[2] = slot
            state[3] = nxt
            state[4] = after_next
            for cp in weight_copies(e, slot):
                cp.wait()

            @pl.when(after_next < N_EXPERTS)
            def _():
                for cp in weight_copies(after_next, lax.rem(slot + 2, W_SLOTS)):
                    cp.start()

            wgb[...] = sg_buf[slot].astype(BF16)
            wub[...] = su_buf[slot].astype(BF16)
            wdb[...] = sd_buf[slot].astype(BF16)

        state[1] = state[1] - 1
        tile_copies(t, wait, False)

        @pl.when(t >= 2)
        def _():
            tile_copies(t - 2, wait, True)

        for n_chunks_here in range(1, TILE_CHUNKS + 1):
            @pl.when(chunks_ref[t] == n_chunks_here)
            def _(m=n_chunks_here * EXPERT_CHUNK):
                x = _tiles_to_rows(x_buf.at[lax.rem(t, X_SLOTS)], m).astype(BF16)
                g = _dot(x, wgb[...])
                u = _dot(x, wub[...])
                hidden = (g * jax.nn.sigmoid(g)) * u
                _rows_to_tiles(y_buf.at[lax.rem(t, 2)], _dot(hidden.astype(BF16), wdb[...]))

        tile_copies(t, start, True)

    @pl.when(t == last)
    def _():
        for back in (2, 1):
            @pl.when(nt >= back)
            def _(back=back):
                tile_copies(nt - back, wait, True)

        tail_copies(wait)


def _experts(tiles, chunk0, chunks, n_tiles, used_chunks, xs, wg, wu, wd):
    any_spec = pl.BlockSpec(memory_space=pl.ANY)
    zeros = jnp.zeros((EXPERT_CHUNK * ROW_TILE, LANES), F32)
    return pl.pallas_call(
        _expert_kernel,
        grid_spec=pltpu.PrefetchScalarGridSpec(
            num_scalar_prefetch=5,
            grid=(chunks.shape[0],),
            in_specs=[any_spec, any_spec, any_spec, any_spec, any_spec],
            out_specs=any_spec,
            scratch_shapes=[pltpu.VMEM((X_SLOTS, TM_EXPERT * ROW_TILE, LANES), F32),
                            pltpu.VMEM((2, TM_EXPERT * ROW_TILE, LANES), F32),
                            pltpu.VMEM((W_SLOTS, D_MODEL, D_EXPERT), F32),
                            pltpu.VMEM((W_SLOTS, D_MODEL, D_EXPERT), F32),
                            pltpu.VMEM((W_SLOTS, D_EXPERT, D_MODEL), F32),
                            pltpu.VMEM((D_MODEL, D_EXPERT), BF16),
                            pltpu.VMEM((D_MODEL, D_EXPERT), BF16),
                            pltpu.VMEM((D_EXPERT, D_MODEL), BF16),
                            pltpu.SMEM((5,), jnp.int32),
                            pltpu.SemaphoreType.DMA((W_SLOTS,)),
                            pltpu.SemaphoreType.DMA((X_SLOTS,)),
                            pltpu.SemaphoreType.DMA((2,)),
                            pltpu.SemaphoreType.DMA]),
        out_shape=jax.ShapeDtypeStruct(xs.shape, F32),
        compiler_params=pltpu.CompilerParams(dimension_semantics=("arbitrary",),
                                             vmem_limit_bytes=VMEM_LIMIT),
        name="expert_mlp",
    )(tiles, chunk0, chunks, n_tiles, used_chunks, xs, wg, wu, wd, zeros)


def _combine_kernel(dest_ref, rw_ref, fg_ref, h_ref, y_ref, o_ref, buf, h_buf, sems, h_sems):
    tm = TM_COMBINE
    i = pl.program_id(0)
    n_steps = pl.num_programs(0)
    n = n_steps * tm
    cur = i % 2

    def h_copy(step, half):
        return pltpu.make_async_copy(h_ref.at[pl.ds(pl.multiple_of(step * tm, tm), tm)], h_buf.at[half],
                                     h_sems.at[half])

    def fetch(step, half):
        def body(r, c):
            for s in range(2):
                pltpu.make_async_copy(_token_rows(y_ref, dest_ref[s * n + step * tm + r], 1),
                                      _token_rows(buf.at[half, s], r, 1),
                                      sems.at[half]).start(priority=s)
            return c

        lax.fori_loop(0, tm, body, 0, unroll=8)
        h_copy(step, half).start(priority=1)

    @pl.when(i == 0)
    def _():
        fetch(0, 0)

    @pl.when(i + 1 < n_steps)
    def _():
        fetch(i + 1, 1 - cur)

    for s in range(2):
        pltpu.make_async_copy(_token_rows(y_ref, 0, tm), buf.at[cur, s], sems.at[cur]).wait()
    h_copy(i, cur).wait()
    rw = rw_ref[...]
    out = (h_buf[cur] + rw[:, 0:1] * _tiles_to_rows(buf.at[cur, 0], tm)
           + rw[:, 1:2] * _tiles_to_rows(buf.at[cur, 1], tm))
    o_ref[...] = _rms(out, fg_ref[...])


def _combine(dest, h, rw, final_g, ys):
    n = h.shape[0]
    return pl.pallas_call(
        _combine_kernel,
        grid_spec=pltpu.PrefetchScalarGridSpec(
            num_scalar_prefetch=1,
            grid=(n // TM_COMBINE,),
            in_specs=[pl.BlockSpec((TM_COMBINE, LANES), lambda i, d: (i, 0)),
                      pl.BlockSpec((1, D_MODEL), lambda i, d: (0, 0)),
                      pl.BlockSpec(memory_space=pl.ANY),
                      pl.BlockSpec(memory_space=pl.ANY)],
            out_specs=pl.BlockSpec((TM_COMBINE, D_MODEL), lambda i, d: (i, 0)),
            scratch_shapes=[pltpu.VMEM((2, 2, TM_COMBINE * ROW_TILE, LANES), F32),
                            pltpu.VMEM((2, TM_COMBINE, D_MODEL), F32),
                            pltpu.SemaphoreType.DMA((2,)),
                            pltpu.SemaphoreType.DMA((2,))]),
        out_shape=jax.ShapeDtypeStruct((n, D_MODEL), F32),
        compiler_params=pltpu.CompilerParams(dimension_semantics=("arbitrary",),
                                             vmem_limit_bytes=VMEM_LIMIT),
        name="combine",
    )(dest, rw, final_g, h, ys)


def _schedule(counts, max_tiles):
    chunks = (counts + EXPERT_CHUNK - 1) // EXPERT_CHUNK
    chunk_end = jnp.cumsum(chunks)
    chunk_start = chunk_end - chunks
    tiles = (chunks + TILE_CHUNKS - 1) // TILE_CHUNKS
    tile_end = jnp.cumsum(tiles)
    tile = jnp.arange(max_tiles, dtype=jnp.int32)
    owner = jnp.sum(tile[:, None] >= tile_end[None, :], axis=1)
    is_owner = owner[:, None] == jnp.arange(N_EXPERTS, dtype=jnp.int32)[None, :]
    of_owner = lambda v: jnp.sum(jnp.where(is_owner, v[None, :], 0), axis=1)
    done = (tile - of_owner(tile_end - tiles)) * TILE_CHUNKS
    tile_chunk0 = (of_owner(chunk_start) + done).astype(jnp.int32)
    tile_chunks = jnp.clip(of_owner(chunks) - done, 0, TILE_CHUNKS).astype(jnp.int32)
    return tiles, chunk_start * EXPERT_CHUNK, tile_chunk0, tile_chunks, tile_end[-1:], chunk_end[-1:]


def _layer(x, attn_g, w_in, sg_g, w_sp, b_sp, sb_g, sg_out_g, w_out, ffn_g,
           w_rg, b_rg, w_re, b_re, w_gate, w_up, w_down):
    batch, seq, _ = x.shape
    n = batch * seq
    x2 = x.reshape(n, D_MODEL)
    row = lambda v: v.reshape(1, -1)

    bsp_full = jnp.repeat(b_sp.T, HEAD_DIM, axis=1)
    qkv, sgn = _inproj(x2, row(attn_g), w_in.astype(BF16), row(sg_g), w_sp, bsp_full, row(sg_out_g))
    sb = _attention(qkv, batch, seq).reshape(n, SB_WIDTH)

    pad_lanes = lambda v, width: jnp.pad(v, [(0, 0)] * (v.ndim - 1) + [(0, width - v.shape[-1])])
    w_r = jnp.concatenate([pad_lanes(w_rg, ROUTER_LANE0),
                           jnp.transpose(w_re, (1, 0, 2)).reshape(D_MODEL, N_EXPERTS)], axis=1)
    w_r = pad_lanes(w_r, LANES)
    wr_hi = w_r.astype(BF16)
    wr_lo = (w_r - wr_hi.astype(F32)).astype(BF16)
    wr2 = jnp.concatenate([wr_hi, wr_lo], axis=1)
    b_r = pad_lanes(jnp.concatenate([pad_lanes(b_rg, ROUTER_LANE0), b_re.reshape(-1)]), LANES)

    h, lg = _mix(sb, sgn, x2, row(sb_g), w_out.astype(BF16), row(ffn_g), wr2, row(b_r))
    ri, rw, cnt = _route(lg)

    counts = cnt[:, 0].astype(jnp.int32)
    n_rows = 2 * n + N_EXPERTS * EXPERT_CHUNK
    tiles, offsets, tile_chunk0, tile_chunks, n_tiles, used_chunks = _schedule(
        counts, 2 * n // TM_EXPERT + N_EXPERTS)
    expert, rank = ri[0:2], ri[2:4]
    is_e = expert[None] == jnp.arange(N_EXPERTS, dtype=jnp.int32)[:, None, None]
    dest = (jnp.sum(jnp.where(is_e, offsets[:, None, None], 0), axis=0) + rank).reshape(-1)
    pad_start = offsets + counts
    pad_count = (-counts) % EXPERT_CHUNK

    xs = _dispatch(dest, pad_start, pad_count, used_chunks, h, row(ffn_g), n_rows)
    ys = _experts(tiles, tile_chunk0, tile_chunks, n_tiles, used_chunks, xs,
                  w_gate.reshape(N_EXPERTS, D_MODEL, D_EXPERT),
                  w_up.reshape(N_EXPERTS, D_MODEL, D_EXPERT),
                  w_down.reshape(N_EXPERTS, D_EXPERT, D_MODEL))
    return dest, h, rw, ys


def kernel(x, attn_norm_g, w_in, sg_norm_g, w_spatial, b_spatial, sb_out_norm_g, sg_out_norm_g,
           w_out, ffn_norm_g, w_router_group, b_router_group, w_router_expert, b_router_expert,
           w_gate, w_up, w_down, final_norm_g):
    assert attn_norm_g.shape[0] == 1, "single-layer problem"
    batch, seq, _ = x.shape
    dest, h, rw, ys = _layer(x, attn_norm_g[0], w_in[0], sg_norm_g[0], w_spatial[0], b_spatial[0],
                             sb_out_norm_g[0], sg_out_norm_g[0], w_out[0], ffn_norm_g[0],
                             w_router_group[0], b_router_group[0], w_router_expert[0],
                             b_router_expert[0], w_gate[0], w_up[0], w_down[0])
    out = _combine(dest, h, rw, final_norm_g.reshape(1, -1), ys)
    return out.reshape(batch, seq, D_MODEL)
```

```python
import functools
import math

import jax
import jax.numpy as jnp
from jax import lax
from jax.experimental import pallas as pl
from jax.experimental.pallas import tpu as pltpu

D_MODEL = 1024
HEAD_DIM = 64
SB_WIDTH = 512
SG_WIDTH = 512
SG_HEADS = 8
D_IN = 3 * SB_WIDTH + 2 * SG_WIDTH
CHUNK = 128
N_GROUPS = 4
EXPERTS_PER_GROUP = 8
N_EXPERTS = N_GROUPS * EXPERTS_PER_GROUP
D_EXPERT = 512
EPS = 1e-6
F32_EXP_UNDERFLOW = 110.0

LANES = 128
SUBLANES = 8
ROW_TILE = D_MODEL // LANES
assert ROW_TILE == SUBLANES
HEAD_PAIR = 2 * HEAD_DIM
ROUTER_LANE0 = SUBLANES
ROUTER_ROWS = ROUTER_LANE0 + N_EXPERTS
assert EXPERTS_PER_GROUP == SUBLANES and N_GROUPS <= ROUTER_LANE0

TM_PROJ = 1024
TQ_ATTN = 256
ATTN_BLOCKS_PER_STEP = 2
ATTN_TOP_ROWS = (160, 176)
TM_MIX = 1024
TM_ROUTE = 1024
TM_DISPATCH = 1024
TM_EXPERT = 640
EXPERT_CHUNK = 128
TM_COMBINE = 512
VMEM_LIMIT = 48 * 1024 * 1024

F32 = jnp.float32
BF16 = jnp.bfloat16


def _rms(x, g):
    return x * lax.rsqrt(jnp.mean(x * x, axis=-1, keepdims=True) + EPS) * g


def _gelu(x):
    c = math.sqrt(2.0 / math.pi)
    return x * (0.5 * (1.0 + jnp.tanh(c * (x + 0.044715 * (x * x * x)))))


def _softplus(z):
    return jnp.maximum(z, 0.0) + jnp.log(1.0 + jnp.exp(-jnp.abs(z)))


def _dot(a, b):
    return jnp.dot(a, b, preferred_element_type=F32)


def _rows_to_tiles(ref, x):
    m = x.shape[0]
    for k in range(ROW_TILE):
        ref[pl.ds(k, m, stride=ROW_TILE), :] = x[:, k * LANES:(k + 1) * LANES]


def _tiles_to_rows(ref, m):
    return jnp.concatenate([ref[pl.ds(k, m, stride=ROW_TILE), :] for k in range(ROW_TILE)], axis=1)


def _token_rows(ref, first_token, n_tokens):
    return ref.at[pl.ds(pl.multiple_of(first_token * ROW_TILE, ROW_TILE), n_tokens * ROW_TILE)]


def _split_bf16(x):
    hi = x.astype(BF16)
    lo = (x - hi.astype(F32)).astype(BF16)
    return hi, lo


def _inproj_kernel(x_ref, g_ref, w_ref, sgg_ref, wsp_ref, bsp_ref, sgog_ref, qkv_ref, sgn_ref,
                   gu_ref, vgn_ref, sg_ref):
    tm = TM_PROJ
    hb = _rms(x_ref[...], g_ref[...]).astype(BF16)
    gv = _gelu(_dot(hb, w_ref[:, 3 * SB_WIDTH + SG_WIDTH:D_IN]))
    vgn_ref[...] = _rms(gv, sgg_ref[...]).astype(BF16)
    gu_ref[...] = _gelu(_dot(hb, w_ref[:, 3 * SB_WIDTH:3 * SB_WIDTH + SG_WIDTH]))
    q = _dot(hb, w_ref[:, 0:SB_WIDTH]) * (1.0 / math.sqrt(HEAD_DIM))
    qkv_ref[:, 0:SB_WIDTH] = q.astype(BF16)
    qkv_ref[:, SB_WIDTH:2 * SB_WIDTH] = _dot(hb, w_ref[:, SB_WIDTH:2 * SB_WIDTH]).astype(BF16)

    lane = lax.broadcasted_iota(jnp.int32, (1, LANES), 1)
    first = lane < HEAD_DIM
    zero = jnp.zeros((), BF16)
    r_c = lax.broadcasted_iota(jnp.int32, (CHUNK, CHUNK), 0)
    c_c = lax.broadcasted_iota(jnp.int32, (CHUNK, CHUNK), 1)
    tril = r_c >= c_c
    n_pairs = SG_WIDTH // HEAD_PAIR
    w_pairs = []
    for p in range(n_pairs):
        w0 = jnp.where(tril, wsp_ref[2 * p], 0.0).astype(BF16)
        w1 = jnp.where(tril, wsp_ref[2 * p + 1], 0.0).astype(BF16)
        w_pairs.append(jnp.concatenate([w0, w1], axis=1))
    bsp = bsp_ref[...]
    for c in range(tm // CHUNK):
        rows = slice(c * CHUNK, (c + 1) * CHUNK)
        for p in range(n_pairs):
            cols = slice(p * HEAD_PAIR, (p + 1) * HEAD_PAIR)
            vg = vgn_ref[rows, cols]
            rhs = jnp.concatenate([jnp.where(first, vg, zero), jnp.where(first, zero, vg)], axis=0)
            mixed = _dot(w_pairs[p], rhs) + bsp[:, cols]
            sg_ref[rows, cols] = gu_ref[rows, cols] * mixed
    qkv_ref[:, 2 * SB_WIDTH:3 * SB_WIDTH] = _dot(hb, w_ref[:, 2 * SB_WIDTH:3 * SB_WIDTH]).astype(BF16)
    sgn_ref[...] = _rms(sg_ref[...], sgog_ref[...]).astype(BF16)


def _inproj(x2, attn_g, w_in_b, sg_g, wsp, bsp_full, sg_out_g):
    n = x2.shape[0]
    row = lambda i: (i, 0)
    const = lambda i: (0, 0)
    return pl.pallas_call(
        _inproj_kernel,
        grid=(n // TM_PROJ,),
        in_specs=[pl.BlockSpec((TM_PROJ, D_MODEL), row),
                  pl.BlockSpec((1, D_MODEL), const),
                  pl.BlockSpec((D_MODEL, D_IN), const),
                  pl.BlockSpec((1, SG_WIDTH), const),
                  pl.BlockSpec((SG_HEADS, CHUNK, CHUNK), lambda i: (0, 0, 0)),
                  pl.BlockSpec((CHUNK, SG_WIDTH), const),
                  pl.BlockSpec((1, SG_WIDTH), const)],
        out_specs=[pl.BlockSpec((TM_PROJ, 3 * SB_WIDTH), row),
                   pl.BlockSpec((TM_PROJ, SG_WIDTH), row)],
        out_shape=[jax.ShapeDtypeStruct((n, 3 * SB_WIDTH), BF16),
                   jax.ShapeDtypeStruct((n, SG_WIDTH), BF16)],
        scratch_shapes=[pltpu.VMEM((TM_PROJ, SG_WIDTH), F32),
                        pltpu.VMEM((TM_PROJ, SG_WIDTH), BF16),
                        pltpu.VMEM((TM_PROJ, SG_WIDTH), F32)],
        compiler_params=pltpu.CompilerParams(dimension_semantics=("arbitrary",),
                                             vmem_limit_bytes=VMEM_LIMIT),
        name="inproj",
    )(x2, attn_g, w_in_b, sg_g, wsp, bsp_full, sg_out_g)


def _attn_kernel(q_ref, k_ref, v_ref, o_ref, q2_ref, carry_ref):
    t = TQ_ATTN
    n_pairs = SB_WIDTH // HEAD_PAIR
    lane = lax.broadcasted_iota(jnp.int32, (1, HEAD_PAIR), 1)
    head_lanes = (lane < HEAD_DIM, lane >= HEAD_DIM)
    zero = jnp.zeros((), BF16)
    r_idx = lax.broadcasted_iota(jnp.int32, (t, t), 0)
    c_idx = lax.broadcasted_iota(jnp.int32, (t, t), 1)
    suffix = (r_idx > c_idx).astype(BF16)
    suffix2 = jnp.concatenate([suffix, suffix], axis=0)
    causal = c_idx < r_idx

    def one_query_block(sub, c):
        qi = pl.program_id(1) * ATTN_BLOCKS_PER_STEP + sub
        row0 = pl.multiple_of(sub * t, t)
        for p in range(n_pairs):
            qp = q_ref[0, pl.ds(row0, t), p * HEAD_PAIR:(p + 1) * HEAD_PAIR]
            for h in range(2):
                q2_ref[(2 * p + h) * t:(2 * p + h + 1) * t, :] = jnp.where(head_lanes[h], qp, zero)
        o_ref[0, pl.ds(row0, t), :] = jnp.zeros((t, SB_WIDTH), F32)
        carry_ref[...] = jnp.zeros_like(carry_ref)

        def block(j, diag, m):
            start = pl.multiple_of(j * t, t)
            mask2 = jnp.concatenate([causal, causal], axis=0) if diag else None
            st = [dict() for _ in range(n_pairs)]

            def head_rows(p):
                return [slice((2 * p + h) * t, (2 * p + h) * t + m) for h in range(2)]

            def scores(p):
                d = st[p]
                d["cols"] = slice(p * HEAD_PAIR, (p + 1) * HEAD_PAIR)
                kb = k_ref[0, pl.ds(start, t), d["cols"]]
                q2 = jnp.concatenate([q2_ref[r, :] for r in head_rows(p)], axis=0)
                z = lax.dot_general(q2, kb, (((1,), (1,)), ((), ())),
                                    preferred_element_type=F32)
                sp = _softplus(z)
                nl = jnp.where(mask2, sp, 0.0) if diag else sp
                hi, lo = _split_bf16(nl)
                d["hl"] = jnp.concatenate([hi, lo], axis=1)
                d["log_beta"] = z - sp
                d["nl0"] = nl[:, 0:1]

            def weights(p):
                d = st[p]
                hl = d["hl"]
                after = jnp.concatenate([_dot(hl[0:m], suffix2), _dot(hl[m:2 * m], suffix2)], axis=0)
                carry = jnp.concatenate([carry_ref[r, :] for r in head_rows(p)], axis=0)
                a = jnp.exp(d["log_beta"] - after - carry)
                if diag:
                    a = jnp.where(mask2, a, 0.0)
                a = a.astype(BF16)
                d["a2"] = jnp.concatenate([a[0:m], a[m:2 * m]], axis=1)
                new_carry = carry + after[:, 0:1] + d["nl0"]
                for h, r in enumerate(head_rows(p)):
                    carry_ref[r, :] = new_carry[h * m:(h + 1) * m]

            def values(p):
                d = st[p]
                vb = v_ref[0, pl.ds(start, t), d["cols"]]
                v2 = jnp.concatenate([jnp.where(head_lanes[0], vb, zero),
                                      jnp.where(head_lanes[1], vb, zero)], axis=0)
                o_ref[0, pl.ds(row0, m), d["cols"]] += _dot(d["a2"], v2)

            for step in range(n_pairs + 2):
                if step < n_pairs:
                    scores(step)
                if 0 <= step - 1 < n_pairs:
                    weights(step - 1)
                if 0 <= step - 2 < n_pairs:
                    values(step - 2)

        def flags():
            bounds = (0,) + ATTN_TOP_ROWS + (t,)
            lowest = [jnp.min(jnp.concatenate([carry_ref[hh * t + lo:hh * t + hi, :] for hh in range(2 * n_pairs)],
                                              axis=0))
                      for lo, hi in zip(bounds[:-1], bounds[1:])]
            below = [functools.reduce(jnp.minimum, lowest[k:]) for k in range(len(lowest))]
            return (below[0] < F32_EXP_UNDERFLOW,) + tuple(b >= F32_EXP_UNDERFLOW for b in below[1:])

        block(qi, True, t)

        def body(state):
            it, _, *done = state
            j = qi - 1 - it
            for k, m in enumerate(ATTN_TOP_ROWS + (t,)):
                use = done[k] if k < len(done) else True
                if k > 0:
                    use = jnp.logical_and(use, jnp.logical_not(done[k - 1]))

                @pl.when(use)
                def _(m=m):
                    block(j, False, m)

            return (it + 1,) + flags()

        lax.while_loop(lambda s: (s[0] < qi) & s[1], body, (jnp.int32(0),) + flags())
        return c

    lax.fori_loop(0, ATTN_BLOCKS_PER_STEP, one_query_block, 0)


def _attention(qkv, batch, seq):
    qkv3 = qkv.reshape(batch, seq, 3 * SB_WIDTH)
    n_heads = SB_WIDTH // HEAD_DIM
    return pl.pallas_call(
        _attn_kernel,
        grid=(batch, seq // (ATTN_BLOCKS_PER_STEP * TQ_ATTN)),
        in_specs=[pl.BlockSpec((1, ATTN_BLOCKS_PER_STEP * TQ_ATTN, SB_WIDTH), lambda b, i: (b, i, 0)),
                  pl.BlockSpec((1, seq, SB_WIDTH), lambda b, i: (b, 0, 1)),
                  pl.BlockSpec((1, seq, SB_WIDTH), lambda b, i: (b, 0, 2))],
        out_specs=pl.BlockSpec((1, ATTN_BLOCKS_PER_STEP * TQ_ATTN, SB_WIDTH), lambda b, i: (b, i, 0)),
        out_shape=jax.ShapeDtypeStruct((batch, seq, SB_WIDTH), F32),
        scratch_shapes=[pltpu.VMEM((n_heads * TQ_ATTN, HEAD_PAIR), BF16),
                        pltpu.VMEM((n_heads * TQ_ATTN, 1), F32)],
        compiler_params=pltpu.CompilerParams(dimension_semantics=("arbitrary",) * 2,
                                             vmem_limit_bytes=VMEM_LIMIT),
        name="sb_attention",
    )(qkv3, qkv3, qkv3)


def _mix_kernel(sb_ref, sgn_ref, x_ref, sbg_ref, wout_ref, ffng_ref, wr2_ref, br_ref,
                h_ref, lg_ref):
    sbn = _rms(sb_ref[...], sbg_ref[...]).astype(BF16)
    h = x_ref[...] + _dot(sbn, wout_ref[0:SB_WIDTH, :]) + _dot(sgn_ref[...], wout_ref[SB_WIDTH:, :])
    h_ref[...] = h
    hn = _rms(h, ffng_ref[...])

    hn_hi, hn_lo = _split_bf16(hn)
    both = _dot(hn_hi, wr2_ref[...])
    logits = both[:, 0:LANES] + both[:, LANES:] + _dot(hn_lo, wr2_ref[:, 0:LANES]) + br_ref[...]
    lg_ref[...] = logits.T[0:ROUTER_ROWS, :]


def _route_kernel(lg_ref, ri_ref, rw_ref, cnt_ref, count_ref):
    tr = TM_ROUTE
    i = pl.program_id(0)

    @pl.when(i == 0)
    def _():
        count_ref[...] = jnp.zeros_like(count_ref)

    neg = jnp.float32(-jnp.inf)
    row8 = lax.broadcasted_iota(jnp.int32, (SUBLANES, tr), 0)

    def top(v):
        m = jnp.max(v, axis=0, keepdims=True)
        return m, jnp.min(jnp.where(v == m, row8, SUBLANES), axis=0, keepdims=True)

    def group_rows(g):
        return lg_ref[ROUTER_LANE0 + g * EXPERTS_PER_GROUP:ROUTER_LANE0 + (g + 1) * EXPERTS_PER_GROUP, :]

    gl = jnp.where(row8 < N_GROUPS, lg_ref[0:SUBLANES, :], neg)
    gmax, gidx = top(gl)
    gweight = 1.0 / jnp.sum(jnp.exp(gl - gmax), axis=0, keepdims=True)
    el = group_rows(0)
    for g in range(1, N_GROUPS):
        el = jnp.where(gidx == g, group_rows(g), el)
    m1, i1 = top(el)
    m2, i2 = top(jnp.where(row8 == i1, neg, el))
    t21 = jnp.exp(m2 - m1)
    w1 = gweight / (1.0 + t21)
    w2 = gweight * t21 / (1.0 + t21)
    e1 = gidx * EXPERTS_PER_GROUP + i1
    e2 = gidx * EXPERTS_PER_GROUP + i2

    row_e = lax.broadcasted_iota(jnp.int32, (N_EXPERTS, tr), 0)
    sel1 = row_e == e1
    sel2 = row_e == e2
    onehot = jnp.where(sel1 | sel2, 1.0, 0.0)
    r_t = lax.broadcasted_iota(jnp.int32, (tr, tr), 0)
    c_t = lax.broadcasted_iota(jnp.int32, (tr, tr), 1)
    before = (r_t < c_t).astype(BF16)
    running = count_ref[:, 0:1] + _dot(onehot.astype(BF16), before)
    rank1 = jnp.sum(jnp.where(sel1, running, 0.0), axis=0, keepdims=True)
    rank2 = jnp.sum(jnp.where(sel2, running, 0.0), axis=0, keepdims=True)
    new_count = count_ref[:, 0:1] + jnp.sum(onehot, axis=1, keepdims=True)
    count_ref[...] = jnp.broadcast_to(new_count, count_ref.shape)
    cnt_ref[...] = jnp.broadcast_to(new_count, cnt_ref.shape)

    ri_ref[...] = jnp.where(row8 == 0, e1, jnp.where(row8 == 1, e2, jnp.where(
        row8 == 2, rank1.astype(jnp.int32), jnp.where(row8 == 3, rank2.astype(jnp.int32), 0))))
    row128 = lax.broadcasted_iota(jnp.int32, (LANES, tr), 0)
    rw_ref[...] = jnp.where(row128 == 0, w1, jnp.where(row128 == 1, w2, 0.0)).T


def _route(lg):
    n = lg.shape[1]
    return pl.pallas_call(
        _route_kernel,
        grid=(n // TM_ROUTE,),
        in_specs=[pl.BlockSpec((ROUTER_ROWS, TM_ROUTE), lambda i: (0, i))],
        out_specs=[pl.BlockSpec((SUBLANES, TM_ROUTE), lambda i: (0, i)),
                   pl.BlockSpec((TM_ROUTE, LANES), lambda i: (i, 0)),
                   pl.BlockSpec((N_EXPERTS, LANES), lambda i: (0, 0))],
        out_shape=[jax.ShapeDtypeStruct((SUBLANES, n), jnp.int32),
                   jax.ShapeDtypeStruct((n, LANES), F32),
                   jax.ShapeDtypeStruct((N_EXPERTS, LANES), F32)],
        scratch_shapes=[pltpu.VMEM((N_EXPERTS, LANES), F32)],
        compiler_params=pltpu.CompilerParams(dimension_semantics=("arbitrary",),
                                             vmem_limit_bytes=VMEM_LIMIT),
        name="route",
    )(lg)


def _mix(sb, sgn, x2, sb_g, w_out_b, ffn_g, wr2, br):
    n = x2.shape[0]
    row = lambda i: (i, 0)
    const = lambda i: (0, 0)
    return pl.pallas_call(
        _mix_kernel,
        grid=(n // TM_MIX,),
        in_specs=[pl.BlockSpec((TM_MIX, SB_WIDTH), row),
                  pl.BlockSpec((TM_MIX, SG_WIDTH), row),
                  pl.BlockSpec((TM_MIX, D_MODEL), row),
                  pl.BlockSpec((1, SB_WIDTH), const),
                  pl.BlockSpec((D_MODEL, D_MODEL), const),
                  pl.BlockSpec((1, D_MODEL), const),
                  pl.BlockSpec((D_MODEL, 2 * LANES), const),
                  pl.BlockSpec((1, LANES), const)],
        out_specs=[pl.BlockSpec((TM_MIX, D_MODEL), row),
                   pl.BlockSpec((ROUTER_ROWS, TM_MIX), lambda i: (0, i))],
        out_shape=[jax.ShapeDtypeStruct((n, D_MODEL), F32),
                   jax.ShapeDtypeStruct((ROUTER_ROWS, n), F32)],
        compiler_params=pltpu.CompilerParams(dimension_semantics=("arbitrary",),
                                             vmem_limit_bytes=VMEM_LIMIT),
        name="mix_router",
    )(sb, sgn, x2, sb_g, w_out_b, ffn_g, wr2, br)


_PAD_BITS = tuple(1 << b for b in reversed(range(EXPERT_CHUNK.bit_length() - 1)))


def _dispatch_kernel(dest_ref, pad_start_ref, pad_count_ref, used_ref, h_ref, g_ref, zeros_ref, xs_ref,
                     hn_ref, sem, zsem):
    tm = TM_DISPATCH
    i = pl.program_id(0)
    n_steps = pl.num_programs(0) - 1
    n = n_steps * tm
    base = (i - 1) * tm
    prev = hn_ref.at[lax.rem(i + 1, 2)]
    n_chunks = xs_ref.shape[0] // (EXPERT_CHUNK * ROW_TILE)

    def pad_copies(do):
        for e in range(N_EXPERTS):
            start = pad_start_ref[e]
            count = pad_count_ref[e]
            for bit in _PAD_BITS:
                @pl.when((count & bit) != 0)
                def _(start=start, bit=bit):
                    do(pltpu.make_async_copy(_token_rows(zeros_ref, 0, bit),
                                             _token_rows(xs_ref, start, bit), zsem))
                start = start + (count & bit)
        for k in range(N_EXPERTS):
            chunk = used_ref[0] + k

            @pl.when(chunk < n_chunks)
            def _(chunk=chunk):
                do(pltpu.make_async_copy(zeros_ref, _token_rows(xs_ref, chunk * EXPERT_CHUNK, EXPERT_CHUNK),
                                         zsem))

    @pl.when(i == 0)
    def _():
        pad_copies(lambda cp: cp.start())

    @pl.when(i > 0)
    def _():
        def body(r, c):
            src = _token_rows(prev, r, 1)
            for s in range(2):
                pltpu.make_async_copy(src, _token_rows(xs_ref, dest_ref[s * n + base + r], 1),
                                      sem).start(priority=s)
            return c

        lax.fori_loop(0, tm, body, 0, unroll=8)

    @pl.when(i < n_steps)
    def _():
        _rows_to_tiles(hn_ref.at[lax.rem(i, 2)], _rms(h_ref[...], g_ref[...]))

    @pl.when(i > 0)
    def _():
        for _ in range(2):
            pltpu.make_async_copy(prev, _token_rows(xs_ref, 0, tm), sem).wait()

    @pl.when(i == n_steps)
    def _():
        pad_copies(lambda cp: cp.wait())


def _dispatch(dest, pad_start, pad_count, used_chunks, h, ffn_g, n_rows):
    n_steps = h.shape[0] // TM_DISPATCH
    zeros = jnp.zeros((EXPERT_CHUNK * ROW_TILE, LANES), F32)
    return pl.pallas_call(
        _dispatch_kernel,
        grid_spec=pltpu.PrefetchScalarGridSpec(
            num_scalar_prefetch=4,
            grid=(n_steps + 1,),
            in_specs=[pl.BlockSpec((TM_DISPATCH, D_MODEL), lambda i, *_: (jnp.minimum(i, n_steps - 1), 0)),
                      pl.BlockSpec((1, D_MODEL), lambda i, *_: (0, 0)),
                      pl.BlockSpec(memory_space=pl.ANY)],
            out_specs=pl.BlockSpec(memory_space=pl.ANY),
            scratch_shapes=[pltpu.VMEM((2, TM_DISPATCH * ROW_TILE, LANES), F32),
                            pltpu.SemaphoreType.DMA, pltpu.SemaphoreType.DMA]),
        out_shape=jax.ShapeDtypeStruct((n_rows * ROW_TILE, LANES), F32),
        compiler_params=pltpu.CompilerParams(dimension_semantics=("arbitrary",),
                                             vmem_limit_bytes=VMEM_LIMIT),
        name="dispatch",
    )(dest, pad_start, pad_count, used_chunks, h, ffn_g, zeros)


X_SLOTS = 3
TILE_CHUNKS = TM_EXPERT // EXPERT_CHUNK
W_SLOTS = 3


def _expert_kernel(tiles_ref, chunk0_ref, chunks_ref, nt_ref, used_ref, xs_ref, wg_ref, wu_ref, wd_ref,
                   zeros_ref, ys_ref, x_buf, y_buf, sg_buf, su_buf, sd_buf, wgb, wub, wdb, state,
                   w_sems, x_sems, y_sems, zsem):
    t = pl.program_id(0)
    last = pl.num_programs(0) - 1
    nt = nt_ref[0]
    n_chunks = ys_ref.shape[0] // (EXPERT_CHUNK * ROW_TILE)

    def tile_copies(tile, do, out):
        for c in range(TILE_CHUNKS):
            @pl.when(c < chunks_ref[tile])
            def _(c=c):
                first = (chunk0_ref[tile] + c) * EXPERT_CHUNK
                if out:
                    slot = lax.rem(tile, 2)
                    do(pltpu.make_async_copy(_token_rows(y_buf.at[slot], c * EXPERT_CHUNK, EXPERT_CHUNK),
                                             _token_rows(ys_ref, first, EXPERT_CHUNK), y_sems.at[slot]))
                else:
                    slot = lax.rem(tile, X_SLOTS)
                    do(pltpu.make_async_copy(_token_rows(xs_ref, first, EXPERT_CHUNK),
                                             _token_rows(x_buf.at[slot], c * EXPERT_CHUNK, EXPERT_CHUNK),
                                             x_sems.at[slot]))

    start = lambda cp: cp.start()
    wait = lambda cp: cp.wait()

    def tail_copies(do):
        for k in range(N_EXPERTS):
            chunk = used_ref[0] + k

            @pl.when(chunk < n_chunks)
            def _(chunk=chunk):
                do(pltpu.make_async_copy(zeros_ref, _token_rows(ys_ref, chunk * EXPERT_CHUNK, EXPERT_CHUNK),
                                         zsem))

    def weight_copies(e, slot):
        return (pltpu.make_async_copy(wg_ref.at[e], sg_buf.at[slot], w_sems.at[slot]),
                pltpu.make_async_copy(wu_ref.at[e], su_buf.at[slot], w_sems.at[slot]),
                pltpu.make_async_copy(wd_ref.at[e], sd_buf.at[slot], w_sems.at[slot]))

    def next_with_rows(e):
        return lax.while_loop(lambda k: (k < N_EXPERTS) & (tiles_ref[jnp.minimum(k, N_EXPERTS - 1)] == 0),
                              lambda k: k + 1, e + 1)

    @pl.when(t == 0)
    def _():
        first = next_with_rows(jnp.int32(-1))
        second = next_with_rows(first)
        state[0] = jnp.int32(-1)
        state[1] = jnp.int32(0)
        state[2] = jnp.int32(W_SLOTS - 1)
        state[3] = first
        state[4] = second
        for cp in weight_copies(first, 0):
            cp.start()

        @pl.when(second < N_EXPERTS)
        def _():
            for cp in weight_copies(second, 1):
                cp.start()

        tile_copies(0, start, False)

        @pl.when(nt > 1)
        def _():
            tile_copies(1, start, False)

        tail_copies(start)

    @pl.when(t + 2 < nt)
    def _():
        tile_copies(t + 2, start, False)

    @pl.when(t < nt)
    def _():
        @pl.when(state[1] == 0)
        def _():
            e = state[3]
            nxt = state[4]
            slot = lax.rem(state[2] + 1, W_SLOTS)
            after_next = next_with_rows(nxt)
            state[0] = e
            state[1] = tiles_ref[e]
            state[2] = slot
            state[3] = nxt
            state[4] = after_next
            for cp in weight_copies(e, slot):
                cp.wait()

            @pl.when(after_next < N_EXPERTS)
            def _():
                for cp in weight_copies(after_next, lax.rem(slot + 2, W_SLOTS)):
                    cp.start()

            wgb[...] = sg_buf[slot].astype(BF16)
            wub[...] = su_buf[slot].astype(BF16)
            wdb[...] = sd_buf[slot].astype(BF16)

        state[1] = state[1] - 1
        tile_copies(t, wait, False)

        @pl.when(t >= 2)
        def _():
            tile_copies(t - 2, wait, True)

        for n_chunks_here in range(1, TILE_CHUNKS + 1):
            @pl.when(chunks_ref[t] == n_chunks_here)
            def _(m=n_chunks_here * EXPERT_CHUNK):
                x = _tiles_to_rows(x_buf.at[lax.rem(t, X_SLOTS)], m).astype(BF16)
                g = _dot(x, wgb[...])
                u = _dot(x, wub[...])
                hidden = (g * jax.nn.sigmoid(g)) * u
                _rows_to_tiles(y_buf.at[lax.rem(t, 2)], _dot(hidden.astype(BF16), wdb[...]))

        tile_copies(t, start, True)

    @pl.when(t == last)
    def _():
        for back in (2, 1):
            @pl.when(nt >= back)
            def _(back=back):
                tile_copies(nt - back, wait, True)

        tail_copies(wait)


def _experts(tiles, chunk0, chunks, n_tiles, used_chunks, xs, wg, wu, wd):
    any_spec = pl.BlockSpec(memory_space=pl.ANY)
    zeros = jnp.zeros((EXPERT_CHUNK * ROW_TILE, LANES), F32)
    return pl.pallas_call(
        _expert_kernel,
        grid_spec=pltpu.PrefetchScalarGridSpec(
            num_scalar_prefetch=5,
            grid=(chunks.shape[0],),
            in_specs=[any_spec, any_spec, any_spec, any_spec, any_spec],
            out_specs=any_spec,
            scratch_shapes=[pltpu.VMEM((X_SLOTS, TM_EXPERT * ROW_TILE, LANES), F32),
                            pltpu.VMEM((2, TM_EXPERT * ROW_TILE, LANES), F32),
                            pltpu.VMEM((W_SLOTS, D_MODEL, D_EXPERT), F32),
                            pltpu.VMEM((W_SLOTS, D_MODEL, D_EXPERT), F32),
                            pltpu.VMEM((W_SLOTS, D_EXPERT, D_MODEL), F32),
                            pltpu.VMEM((D_MODEL, D_EXPERT), BF16),
                            pltpu.VMEM((D_MODEL, D_EXPERT), BF16),
                            pltpu.VMEM((D_EXPERT, D_MODEL), BF16),
                            pltpu.SMEM((5,), jnp.int32),
                            pltpu.SemaphoreType.DMA((W_SLOTS,)),
                            pltpu.SemaphoreType.DMA((X_SLOTS,)),
                            pltpu.SemaphoreType.DMA((2,)),
                            pltpu.SemaphoreType.DMA]),
        out_shape=jax.ShapeDtypeStruct(xs.shape, F32),
        compiler_params=pltpu.CompilerParams(dimension_semantics=("arbitrary",),
                                             vmem_limit_bytes=VMEM_LIMIT),
        name="expert_mlp",
    )(tiles, chunk0, chunks, n_tiles, used_chunks, xs, wg, wu, wd, zeros)


def _combine_kernel(dest_ref, rw_ref, fg_ref, h_ref, y_ref, o_ref, buf, h_buf, o_buf, sems, h_sems, o_sems):
    tm = TM_COMBINE
    i = pl.program_id(0)
    n_steps = pl.num_programs(0)
    n = n_steps * tm
    cur = i % 2

    def h_copy(step, half):
        return pltpu.make_async_copy(h_ref.at[pl.ds(pl.multiple_of(step * tm, tm), tm)], h_buf.at[half],
                                     h_sems.at[half])

    def fetch(step, half):
        h_copy(step, half).start(priority=1)

        def body(r, c):
            for s in range(2):
                pltpu.make_async_copy(_token_rows(y_ref, dest_ref[s * n + step * tm + r], 1),
                                      _token_rows(buf.at[half, s], r, 1),
                                      sems.at[half]).start(priority=s)
            return c

        lax.fori_loop(0, tm, body, 0, unroll=8)

    @pl.when(i == 0)
    def _():
        fetch(0, 0)

    @pl.when(i + 1 < n_steps)
    def _():
        fetch(i + 1, 1 - cur)

    for s in range(2):
        pltpu.make_async_copy(_token_rows(y_ref, 0, tm), buf.at[cur, s], sems.at[cur]).wait()
    h_copy(i, cur).wait()

    def o_copy(step, half):
        return pltpu.make_async_copy(o_buf.at[half], o_ref.at[pl.ds(pl.multiple_of(step * tm, tm), tm)],
                                     o_sems.at[half])

    @pl.when(i >= 2)
    def _():
        o_copy(i - 2, cur).wait()

    rw = rw_ref[...]
    out = (h_buf[cur] + rw[:, 0:1] * _tiles_to_rows(buf.at[cur, 0], tm)
           + rw[:, 1:2] * _tiles_to_rows(buf.at[cur, 1], tm))
    o_buf[cur] = _rms(out, fg_ref[...])
    o_copy(i, cur).start(priority=1)

    @pl.when(i == n_steps - 1)
    def _():
        @pl.when(i >= 1)
        def _():
            o_copy(i - 1, 1 - cur).wait()

        o_copy(i, cur).wait()


def _combine(dest, h, rw, final_g, ys):
    n = h.shape[0]
    return pl.pallas_call(
        _combine_kernel,
        grid_spec=pltpu.PrefetchScalarGridSpec(
            num_scalar_prefetch=1,
            grid=(n // TM_COMBINE,),
            in_specs=[pl.BlockSpec((TM_COMBINE, LANES), lambda i, d: (i, 0)),
                      pl.BlockSpec((1, D_MODEL), lambda i, d: (0, 0)),
                      pl.BlockSpec(memory_space=pl.ANY),
                      pl.BlockSpec(memory_space=pl.ANY)],
            out_specs=pl.BlockSpec(memory_space=pl.ANY),
            scratch_shapes=[pltpu.VMEM((2, 2, TM_COMBINE * ROW_TILE, LANES), F32),
                            pltpu.VMEM((2, TM_COMBINE, D_MODEL), F32),
                            pltpu.VMEM((2, TM_COMBINE, D_MODEL), F32),
                            pltpu.SemaphoreType.DMA((2,)),
                            pltpu.SemaphoreType.DMA((2,)),
                            pltpu.SemaphoreType.DMA((2,))]),
        out_shape=jax.ShapeDtypeStruct((n, D_MODEL), F32),
        compiler_params=pltpu.CompilerParams(dimension_semantics=("arbitrary",),
                                             vmem_limit_bytes=VMEM_LIMIT),
        name="combine",
    )(dest, rw, final_g, h, ys)


def _schedule(counts, max_tiles):
    chunks = (counts + EXPERT_CHUNK - 1) // EXPERT_CHUNK
    chunk_end = jnp.cumsum(chunks)
    chunk_start = chunk_end - chunks
    tiles = (chunks + TILE_CHUNKS - 1) // TILE_CHUNKS
    tile_end = jnp.cumsum(tiles)
    tile = jnp.arange(max_tiles, dtype=jnp.int32)
    owner = jnp.sum(tile[:, None] >= tile_end[None, :], axis=1)
    is_owner = owner[:, None] == jnp.arange(N_EXPERTS, dtype=jnp.int32)[None, :]
    of_owner = lambda v: jnp.sum(jnp.where(is_owner, v[None, :], 0), axis=1)
    done = (tile - of_owner(tile_end - tiles)) * TILE_CHUNKS
    tile_chunk0 = (of_owner(chunk_start) + done).astype(jnp.int32)
    tile_chunks = jnp.clip(of_owner(chunks) - done, 0, TILE_CHUNKS).astype(jnp.int32)
    return tiles, chunk_start * EXPERT_CHUNK, tile_chunk0, tile_chunks, tile_end[-1:], chunk_end[-1:]


def _layer(x, attn_g, w_in, sg_g, w_sp, b_sp, sb_g, sg_out_g, w_out, ffn_g,
           w_rg, b_rg, w_re, b_re, w_gate, w_up, w_down):
    batch, seq, _ = x.shape
    n = batch * seq
    x2 = x.reshape(n, D_MODEL)
    row = lambda v: v.reshape(1, -1)

    bsp_full = jnp.repeat(b_sp.T, HEAD_DIM, axis=1)
    qkv, sgn = _inproj(x2, row(attn_g), w_in.astype(BF16), row(sg_g), w_sp, bsp_full, row(sg_out_g))
    sb = _attention(qkv, batch, seq).reshape(n, SB_WIDTH)

    pad_lanes = lambda v, width: jnp.pad(v, [(0, 0)] * (v.ndim - 1) + [(0, width - v.shape[-1])])
    w_r = jnp.concatenate([pad_lanes(w_rg, ROUTER_LANE0),
                           jnp.transpose(w_re, (1, 0, 2)).reshape(D_MODEL, N_EXPERTS)], axis=1)
    w_r = pad_lanes(w_r, LANES)
    wr_hi = w_r.astype(BF16)
    wr_lo = (w_r - wr_hi.astype(F32)).astype(BF16)
    wr2 = jnp.concatenate([wr_hi, wr_lo], axis=1)
    b_r = pad_lanes(jnp.concatenate([pad_lanes(b_rg, ROUTER_LANE0), b_re.reshape(-1)]), LANES)

    h, lg = _mix(sb, sgn, x2, row(sb_g), w_out.astype(BF16), row(ffn_g), wr2, row(b_r))
    ri, rw, cnt = _route(lg)

    counts = cnt[:, 0].astype(jnp.int32)
    n_rows = 2 * n + N_EXPERTS * EXPERT_CHUNK
    tiles, offsets, tile_chunk0, tile_chunks, n_tiles, used_chunks = _schedule(
        counts, 2 * n // TM_EXPERT + N_EXPERTS)
    expert, rank = ri[0:2], ri[2:4]
    is_e = expert[None] == jnp.arange(N_EXPERTS, dtype=jnp.int32)[:, None, None]
    dest = (jnp.sum(jnp.where(is_e, offsets[:, None, None], 0), axis=0) + rank).reshape(-1)
    pad_start = offsets + counts
    pad_count = (-counts) % EXPERT_CHUNK

    xs = _dispatch(dest, pad_start, pad_count, used_chunks, h, row(ffn_g), n_rows)
    ys = _experts(tiles, tile_chunk0, tile_chunks, n_tiles, used_chunks, xs,
                  w_gate.reshape(N_EXPERTS, D_MODEL, D_EXPERT),
                  w_up.reshape(N_EXPERTS, D_MODEL, D_EXPERT),
                  w_down.reshape(N_EXPERTS, D_EXPERT, D_MODEL))
    return dest, h, rw, ys


def kernel(x, attn_norm_g, w_in, sg_norm_g, w_spatial, b_spatial, sb_out_norm_g, sg_out_norm_g,
           w_out, ffn_norm_g, w_router_group, b_router_group, w_router_expert, b_router_expert,
           w_gate, w_up, w_down, final_norm_g):
    assert attn_norm_g.shape[0] == 1, "single-layer problem"
    batch, seq, _ = x.shape
    dest, h, rw, ys = _layer(x, attn_norm_g[0], w_in[0], sg_norm_g[0], w_spatial[0], b_spatial[0],
                             sb_out_norm_g[0], sg_out_norm_g[0], w_out[0], ffn_norm_g[0],
                             w_router_group[0], b_router_group[0], w_router_expert[0],
                             b_router_expert[0], w_gate[0], w_up[0], w_down[0])
    out = _combine(dest, h, rw, final_norm_g.reshape(1, -1), ys)
    return out.reshape(batch, seq, D_MODEL)
```

```python
import functools
import math

import jax
import jax.numpy as jnp
from jax import lax
from jax.experimental import pallas as pl
from jax.experimental.pallas import tpu as pltpu

D_MODEL = 1024
HEAD_DIM = 64
SB_WIDTH = 512
SG_WIDTH = 512
SG_HEADS = 8
D_IN = 3 * SB_WIDTH + 2 * SG_WIDTH
CHUNK = 128
N_GROUPS = 4
EXPERTS_PER_GROUP = 8
N_EXPERTS = N_GROUPS * EXPERTS_PER_GROUP
D_EXPERT = 512
EPS = 1e-6
F32_EXP_UNDERFLOW = 110.0

LANES = 128
SUBLANES = 8
ROW_TILE = D_MODEL // LANES
assert ROW_TILE == SUBLANES
HEAD_PAIR = 2 * HEAD_DIM
ROUTER_LANE0 = SUBLANES
ROUTER_ROWS = ROUTER_LANE0 + N_EXPERTS
assert EXPERTS_PER_GROUP == SUBLANES and N_GROUPS <= ROUTER_LANE0

TM_PROJ = 1024
TQ_ATTN = 256
ATTN_BLOCKS_PER_STEP = 2
ATTN_TOP_ROWS = (160, 176)
TM_MIX = 1024
TM_ROUTE = 1024
TM_DISPATCH = 1024
TM_EXPERT = 640
EXPERT_CHUNK = 128
TM_COMBINE = 1024
VMEM_LIMIT = 48 * 1024 * 1024

F32 = jnp.float32
BF16 = jnp.bfloat16


def _rms(x, g):
    return x * lax.rsqrt(jnp.mean(x * x, axis=-1, keepdims=True) + EPS) * g


def _gelu(x):
    c = math.sqrt(2.0 / math.pi)
    return x * (0.5 * (1.0 + jnp.tanh(c * (x + 0.044715 * (x * x * x)))))


def _softplus(z):
    return jnp.maximum(z, 0.0) + jnp.log(1.0 + jnp.exp(-jnp.abs(z)))


def _dot(a, b):
    return jnp.dot(a, b, preferred_element_type=F32)


def _rows_to_tiles(ref, x):
    m = x.shape[0]
    for k in range(ROW_TILE):
        ref[pl.ds(k, m, stride=ROW_TILE), :] = x[:, k * LANES:(k + 1) * LANES]


def _tiles_to_rows(ref, m):
    return jnp.concatenate([ref[pl.ds(k, m, stride=ROW_TILE), :] for k in range(ROW_TILE)], axis=1)


def _token_rows(ref, first_token, n_tokens):
    return ref.at[pl.ds(pl.multiple_of(first_token * ROW_TILE, ROW_TILE), n_tokens * ROW_TILE)]


def _split_bf16(x):
    hi = x.astype(BF16)
    lo = (x - hi.astype(F32)).astype(BF16)
    return hi, lo


def _inproj_kernel(x_ref, g_ref, w_ref, sgg_ref, wsp_ref, bsp_ref, sgog_ref, qkv_ref, sgn_ref,
                   gu_ref, vgn_ref, sg_ref):
    tm = TM_PROJ
    hb = _rms(x_ref[...], g_ref[...]).astype(BF16)
    gv = _gelu(_dot(hb, w_ref[:, 3 * SB_WIDTH + SG_WIDTH:D_IN]))
    vgn_ref[...] = _rms(gv, sgg_ref[...]).astype(BF16)
    gu_ref[...] = _gelu(_dot(hb, w_ref[:, 3 * SB_WIDTH:3 * SB_WIDTH + SG_WIDTH]))
    q = _dot(hb, w_ref[:, 0:SB_WIDTH]) * (1.0 / math.sqrt(HEAD_DIM))
    qkv_ref[:, 0:SB_WIDTH] = q.astype(BF16)
    qkv_ref[:, SB_WIDTH:2 * SB_WIDTH] = _dot(hb, w_ref[:, SB_WIDTH:2 * SB_WIDTH]).astype(BF16)

    lane = lax.broadcasted_iota(jnp.int32, (1, LANES), 1)
    first = lane < HEAD_DIM
    zero = jnp.zeros((), BF16)
    r_c = lax.broadcasted_iota(jnp.int32, (CHUNK, CHUNK), 0)
    c_c = lax.broadcasted_iota(jnp.int32, (CHUNK, CHUNK), 1)
    tril = r_c >= c_c
    n_pairs = SG_WIDTH // HEAD_PAIR
    w_pairs = []
    for p in range(n_pairs):
        w0 = jnp.where(tril, wsp_ref[2 * p], 0.0).astype(BF16)
        w1 = jnp.where(tril, wsp_ref[2 * p + 1], 0.0).astype(BF16)
        w_pairs.append(jnp.concatenate([w0, w1], axis=1))
    bsp = bsp_ref[...]
    for c in range(tm // CHUNK):
        rows = slice(c * CHUNK, (c + 1) * CHUNK)
        for p in range(n_pairs):
            cols = slice(p * HEAD_PAIR, (p + 1) * HEAD_PAIR)
            vg = vgn_ref[rows, cols]
            rhs = jnp.concatenate([jnp.where(first, vg, zero), jnp.where(first, zero, vg)], axis=0)
            mixed = _dot(w_pairs[p], rhs) + bsp[:, cols]
            sg_ref[rows, cols] = gu_ref[rows, cols] * mixed
    qkv_ref[:, 2 * SB_WIDTH:3 * SB_WIDTH] = _dot(hb, w_ref[:, 2 * SB_WIDTH:3 * SB_WIDTH]).astype(BF16)
    sgn_ref[...] = _rms(sg_ref[...], sgog_ref[...]).astype(BF16)


def _inproj(x2, attn_g, w_in_b, sg_g, wsp, bsp_full, sg_out_g):
    n = x2.shape[0]
    row = lambda i: (i, 0)
    const = lambda i: (0, 0)
    return pl.pallas_call(
        _inproj_kernel,
        grid=(n // TM_PROJ,),
        in_specs=[pl.BlockSpec((TM_PROJ, D_MODEL), row),
                  pl.BlockSpec((1, D_MODEL), const),
                  pl.BlockSpec((D_MODEL, D_IN), const),
                  pl.BlockSpec((1, SG_WIDTH), const),
                  pl.BlockSpec((SG_HEADS, CHUNK, CHUNK), lambda i: (0, 0, 0)),
                  pl.BlockSpec((CHUNK, SG_WIDTH), const),
                  pl.BlockSpec((1, SG_WIDTH), const)],
        out_specs=[pl.BlockSpec((TM_PROJ, 3 * SB_WIDTH), row),
                   pl.BlockSpec((TM_PROJ, SG_WIDTH), row)],
        out_shape=[jax.ShapeDtypeStruct((n, 3 * SB_WIDTH), BF16),
                   jax.ShapeDtypeStruct((n, SG_WIDTH), BF16)],
        scratch_shapes=[pltpu.VMEM((TM_PROJ, SG_WIDTH), F32),
                        pltpu.VMEM((TM_PROJ, SG_WIDTH), BF16),
                        pltpu.VMEM((TM_PROJ, SG_WIDTH), F32)],
        compiler_params=pltpu.CompilerParams(dimension_semantics=("arbitrary",),
                                             vmem_limit_bytes=VMEM_LIMIT),
        name="inproj",
    )(x2, attn_g, w_in_b, sg_g, wsp, bsp_full, sg_out_g)


def _attn_kernel(q_ref, k_ref, v_ref, o_ref, q2_ref, carry_ref):
    t = TQ_ATTN
    n_pairs = SB_WIDTH // HEAD_PAIR
    lane = lax.broadcasted_iota(jnp.int32, (1, HEAD_PAIR), 1)
    head_lanes = (lane < HEAD_DIM, lane >= HEAD_DIM)
    zero = jnp.zeros((), BF16)
    r_idx = lax.broadcasted_iota(jnp.int32, (t, t), 0)
    c_idx = lax.broadcasted_iota(jnp.int32, (t, t), 1)
    suffix = (r_idx > c_idx).astype(BF16)
    suffix2 = jnp.concatenate([suffix, suffix], axis=0)
    causal = c_idx < r_idx

    def one_query_block(sub, c):
        qi = pl.program_id(1) * ATTN_BLOCKS_PER_STEP + sub
        row0 = pl.multiple_of(sub * t, t)
        for p in range(n_pairs):
            qp = q_ref[0, pl.ds(row0, t), p * HEAD_PAIR:(p + 1) * HEAD_PAIR]
            for h in range(2):
                q2_ref[(2 * p + h) * t:(2 * p + h + 1) * t, :] = jnp.where(head_lanes[h], qp, zero)
        o_ref[0, pl.ds(row0, t), :] = jnp.zeros((t, SB_WIDTH), F32)
        carry_ref[...] = jnp.zeros_like(carry_ref)

        def block(j, diag, m):
            start = pl.multiple_of(j * t, t)
            mask2 = jnp.concatenate([causal, causal], axis=0) if diag else None
            st = [dict() for _ in range(n_pairs)]

            def head_rows(p):
                return [slice((2 * p + h) * t, (2 * p + h) * t + m) for h in range(2)]

            def scores(p):
                d = st[p]
                d["cols"] = slice(p * HEAD_PAIR, (p + 1) * HEAD_PAIR)
                kb = k_ref[0, pl.ds(start, t), d["cols"]]
                q2 = jnp.concatenate([q2_ref[r, :] for r in head_rows(p)], axis=0)
                z = lax.dot_general(q2, kb, (((1,), (1,)), ((), ())),
                                    preferred_element_type=F32)
                sp = _softplus(z)
                nl = jnp.where(mask2, sp, 0.0) if diag else sp
                hi, lo = _split_bf16(nl)
                d["hl"] = jnp.concatenate([hi, lo], axis=1)
                d["log_beta"] = z - sp
                d["nl0"] = nl[:, 0:1]

            def weights(p):
                d = st[p]
                hl = d["hl"]
                after = jnp.concatenate([_dot(hl[0:m], suffix2), _dot(hl[m:2 * m], suffix2)], axis=0)
                carry = jnp.concatenate([carry_ref[r, :] for r in head_rows(p)], axis=0)
                a = jnp.exp(d["log_beta"] - after - carry)
                if diag:
                    a = jnp.where(mask2, a, 0.0)
                a = a.astype(BF16)
                d["a2"] = jnp.concatenate([a[0:m], a[m:2 * m]], axis=1)
                new_carry = carry + after[:, 0:1] + d["nl0"]
                for h, r in enumerate(head_rows(p)):
                    carry_ref[r, :] = new_carry[h * m:(h + 1) * m]

            def values(p):
                d = st[p]
                vb = v_ref[0, pl.ds(start, t), d["cols"]]
                v2 = jnp.concatenate([jnp.where(head_lanes[0], vb, zero),
                                      jnp.where(head_lanes[1], vb, zero)], axis=0)
                o_ref[0, pl.ds(row0, m), d["cols"]] += _dot(d["a2"], v2)

            for step in range(n_pairs + 2):
                if step < n_pairs:
                    scores(step)
                if 0 <= step - 1 < n_pairs:
                    weights(step - 1)
                if 0 <= step - 2 < n_pairs:
                    values(step - 2)

        def flags():
            bounds = (0,) + ATTN_TOP_ROWS + (t,)
            lowest = [jnp.min(jnp.concatenate([carry_ref[hh * t + lo:hh * t + hi, :] for hh in range(2 * n_pairs)],
                                              axis=0))
                      for lo, hi in zip(bounds[:-1], bounds[1:])]
            below = [functools.reduce(jnp.minimum, lowest[k:]) for k in range(len(lowest))]
            return (below[0] < F32_EXP_UNDERFLOW,) + tuple(b >= F32_EXP_UNDERFLOW for b in below[1:])

        block(qi, True, t)

        def body(state):
            it, _, *done = state
            j = qi - 1 - it
            for k, m in enumerate(ATTN_TOP_ROWS + (t,)):
                use = done[k] if k < len(done) else True
                if k > 0:
                    use = jnp.logical_and(use, jnp.logical_not(done[k - 1]))

                @pl.when(use)
                def _(m=m):
                    block(j, False, m)

            return (it + 1,) + flags()

        lax.while_loop(lambda s: (s[0] < qi) & s[1], body, (jnp.int32(0),) + flags())
        return c

    lax.fori_loop(0, ATTN_BLOCKS_PER_STEP, one_query_block, 0)


def _attention(qkv, batch, seq):
    qkv3 = qkv.reshape(batch, seq, 3 * SB_WIDTH)
    n_heads = SB_WIDTH // HEAD_DIM
    return pl.pallas_call(
        _attn_kernel,
        grid=(batch, seq // (ATTN_BLOCKS_PER_STEP * TQ_ATTN)),
        in_specs=[pl.BlockSpec((1, ATTN_BLOCKS_PER_STEP * TQ_ATTN, SB_WIDTH), lambda b, i: (b, i, 0)),
                  pl.BlockSpec((1, seq, SB_WIDTH), lambda b, i: (b, 0, 1)),
                  pl.BlockSpec((1, seq, SB_WIDTH), lambda b, i: (b, 0, 2))],
        out_specs=pl.BlockSpec((1, ATTN_BLOCKS_PER_STEP * TQ_ATTN, SB_WIDTH), lambda b, i: (b, i, 0)),
        out_shape=jax.ShapeDtypeStruct((batch, seq, SB_WIDTH), F32),
        scratch_shapes=[pltpu.VMEM((n_heads * TQ_ATTN, HEAD_PAIR), BF16),
                        pltpu.VMEM((n_heads * TQ_ATTN, 1), F32)],
        compiler_params=pltpu.CompilerParams(dimension_semantics=("arbitrary",) * 2,
                                             vmem_limit_bytes=VMEM_LIMIT),
        name="sb_attention",
    )(qkv3, qkv3, qkv3)


def _mix_kernel(sb_ref, sgn_ref, x_ref, sbg_ref, wout_ref, ffng_ref, wr2_ref, br_ref,
                h_ref, lg_ref):
    sbn = _rms(sb_ref[...], sbg_ref[...]).astype(BF16)
    h = x_ref[...] + _dot(sbn, wout_ref[0:SB_WIDTH, :]) + _dot(sgn_ref[...], wout_ref[SB_WIDTH:, :])
    h_ref[...] = h
    hn = _rms(h, ffng_ref[...])

    hn_hi, hn_lo = _split_bf16(hn)
    both = _dot(hn_hi, wr2_ref[...])
    logits = both[:, 0:LANES] + both[:, LANES:] + _dot(hn_lo, wr2_ref[:, 0:LANES]) + br_ref[...]
    lg_ref[...] = logits.T[0:ROUTER_ROWS, :]


def _route_kernel(lg_ref, ri_ref, rw_ref, cnt_ref, count_ref):
    tr = TM_ROUTE
    i = pl.program_id(0)

    @pl.when(i == 0)
    def _():
        count_ref[...] = jnp.zeros_like(count_ref)

    neg = jnp.float32(-jnp.inf)
    row8 = lax.broadcasted_iota(jnp.int32, (SUBLANES, tr), 0)

    def top(v):
        m = jnp.max(v, axis=0, keepdims=True)
        return m, jnp.min(jnp.where(v == m, row8, SUBLANES), axis=0, keepdims=True)

    def group_rows(g):
        return lg_ref[ROUTER_LANE0 + g * EXPERTS_PER_GROUP:ROUTER_LANE0 + (g + 1) * EXPERTS_PER_GROUP, :]

    gl = jnp.where(row8 < N_GROUPS, lg_ref[0:SUBLANES, :], neg)
    gmax, gidx = top(gl)
    gweight = 1.0 / jnp.sum(jnp.exp(gl - gmax), axis=0, keepdims=True)
    el = group_rows(0)
    for g in range(1, N_GROUPS):
        el = jnp.where(gidx == g, group_rows(g), el)
    m1, i1 = top(el)
    m2, i2 = top(jnp.where(row8 == i1, neg, el))
    t21 = jnp.exp(m2 - m1)
    w1 = gweight / (1.0 + t21)
    w2 = gweight * t21 / (1.0 + t21)
    e1 = gidx * EXPERTS_PER_GROUP + i1
    e2 = gidx * EXPERTS_PER_GROUP + i2

    row_e = lax.broadcasted_iota(jnp.int32, (N_EXPERTS, tr), 0)
    sel1 = row_e == e1
    sel2 = row_e == e2
    onehot = jnp.where(sel1 | sel2, 1.0, 0.0)
    r_t = lax.broadcasted_iota(jnp.int32, (tr, tr), 0)
    c_t = lax.broadcasted_iota(jnp.int32, (tr, tr), 1)
    before = (r_t < c_t).astype(BF16)
    running = count_ref[:, 0:1] + _dot(onehot.astype(BF16), before)
    rank1 = jnp.sum(jnp.where(sel1, running, 0.0), axis=0, keepdims=True)
    rank2 = jnp.sum(jnp.where(sel2, running, 0.0), axis=0, keepdims=True)
    new_count = count_ref[:, 0:1] + jnp.sum(onehot, axis=1, keepdims=True)
    count_ref[...] = jnp.broadcast_to(new_count, count_ref.shape)
    cnt_ref[...] = jnp.broadcast_to(new_count, cnt_ref.shape)

    ri_ref[...] = jnp.where(row8 == 0, e1, jnp.where(row8 == 1, e2, jnp.where(
        row8 == 2, rank1.astype(jnp.int32), jnp.where(row8 == 3, rank2.astype(jnp.int32), 0))))
    row128 = lax.broadcasted_iota(jnp.int32, (LANES, tr), 0)
    rw_ref[...] = jnp.where(row128 == 0, w1, jnp.where(row128 == 1, w2, 0.0)).T


def _route(lg):
    n = lg.shape[1]
    return pl.pallas_call(
        _route_kernel,
        grid=(n // TM_ROUTE,),
        in_specs=[pl.BlockSpec((ROUTER_ROWS, TM_ROUTE), lambda i: (0, i))],
        out_specs=[pl.BlockSpec((SUBLANES, TM_ROUTE), lambda i: (0, i)),
                   pl.BlockSpec((TM_ROUTE, LANES), lambda i: (i, 0)),
                   pl.BlockSpec((N_EXPERTS, LANES), lambda i: (0, 0))],
        out_shape=[jax.ShapeDtypeStruct((SUBLANES, n), jnp.int32),
                   jax.ShapeDtypeStruct((n, LANES), F32),
                   jax.ShapeDtypeStruct((N_EXPERTS, LANES), F32)],
        scratch_shapes=[pltpu.VMEM((N_EXPERTS, LANES), F32)],
        compiler_params=pltpu.CompilerParams(dimension_semantics=("arbitrary",),
                                             vmem_limit_bytes=VMEM_LIMIT),
        name="route",
    )(lg)


def _mix(sb, sgn, x2, sb_g, w_out_b, ffn_g, wr2, br):
    n = x2.shape[0]
    row = lambda i: (i, 0)
    const = lambda i: (0, 0)
    return pl.pallas_call(
        _mix_kernel,
        grid=(n // TM_MIX,),
        in_specs=[pl.BlockSpec((TM_MIX, SB_WIDTH), row),
                  pl.BlockSpec((TM_MIX, SG_WIDTH), row),
                  pl.BlockSpec((TM_MIX, D_MODEL), row),
                  pl.BlockSpec((1, SB_WIDTH), const),
                  pl.BlockSpec((D_MODEL, D_MODEL), const),
                  pl.BlockSpec((1, D_MODEL), const),
                  pl.BlockSpec((D_MODEL, 2 * LANES), const),
                  pl.BlockSpec((1, LANES), const)],
        out_specs=[pl.BlockSpec((TM_MIX, D_MODEL), row),
                   pl.BlockSpec((ROUTER_ROWS, TM_MIX), lambda i: (0, i))],
        out_shape=[jax.ShapeDtypeStruct((n, D_MODEL), F32),
                   jax.ShapeDtypeStruct((ROUTER_ROWS, n), F32)],
        compiler_params=pltpu.CompilerParams(dimension_semantics=("arbitrary",),
                                             vmem_limit_bytes=VMEM_LIMIT),
        name="mix_router",
    )(sb, sgn, x2, sb_g, w_out_b, ffn_g, wr2, br)


_PAD_BITS = tuple(1 << b for b in reversed(range(EXPERT_CHUNK.bit_length() - 1)))


def _dispatch_kernel(dest_ref, pad_start_ref, pad_count_ref, used_ref, h_ref, g_ref, zeros_ref, xs_ref,
                     hn_ref, sem, zsem):
    tm = TM_DISPATCH
    i = pl.program_id(0)
    n_steps = pl.num_programs(0) - 1
    n = n_steps * tm
    base = (i - 1) * tm
    prev = hn_ref.at[lax.rem(i + 1, 2)]
    n_chunks = xs_ref.shape[0] // (EXPERT_CHUNK * ROW_TILE)

    def pad_copies(do):
        for e in range(N_EXPERTS):
            start = pad_start_ref[e]
            count = pad_count_ref[e]
            for bit in _PAD_BITS:
                @pl.when((count & bit) != 0)
                def _(start=start, bit=bit):
                    do(pltpu.make_async_copy(_token_rows(zeros_ref, 0, bit),
                                             _token_rows(xs_ref, start, bit), zsem))
                start = start + (count & bit)
        for k in range(N_EXPERTS):
            chunk = used_ref[0] + k

            @pl.when(chunk < n_chunks)
            def _(chunk=chunk):
                do(pltpu.make_async_copy(zeros_ref, _token_rows(xs_ref, chunk * EXPERT_CHUNK, EXPERT_CHUNK),
                                         zsem))

    @pl.when(i == 0)
    def _():
        pad_copies(lambda cp: cp.start())

    @pl.when(i > 0)
    def _():
        def body(r, c):
            src = _token_rows(prev, r, 1)
            for s in range(2):
                pltpu.make_async_copy(src, _token_rows(xs_ref, dest_ref[s * n + base + r], 1),
                                      sem).start(priority=s)
            return c

        lax.fori_loop(0, tm, body, 0, unroll=8)

    @pl.when(i < n_steps)
    def _():
        _rows_to_tiles(hn_ref.at[lax.rem(i, 2)], _rms(h_ref[...], g_ref[...]))

    @pl.when(i > 0)
    def _():
        for _ in range(2):
            pltpu.make_async_copy(prev, _token_rows(xs_ref, 0, tm), sem).wait()

    @pl.when(i == n_steps)
    def _():
        pad_copies(lambda cp: cp.wait())


def _dispatch(dest, pad_start, pad_count, used_chunks, h, ffn_g, n_rows):
    n_steps = h.shape[0] // TM_DISPATCH
    zeros = jnp.zeros((EXPERT_CHUNK * ROW_TILE, LANES), F32)
    return pl.pallas_call(
        _dispatch_kernel,
        grid_spec=pltpu.PrefetchScalarGridSpec(
            num_scalar_prefetch=4,
            grid=(n_steps + 1,),
            in_specs=[pl.BlockSpec((TM_DISPATCH, D_MODEL), lambda i, *_: (jnp.minimum(i, n_steps - 1), 0)),
                      pl.BlockSpec((1, D_MODEL), lambda i, *_: (0, 0)),
                      pl.BlockSpec(memory_space=pl.ANY)],
            out_specs=pl.BlockSpec(memory_space=pl.ANY),
            scratch_shapes=[pltpu.VMEM((2, TM_DISPATCH * ROW_TILE, LANES), F32),
                            pltpu.SemaphoreType.DMA, pltpu.SemaphoreType.DMA]),
        out_shape=jax.ShapeDtypeStruct((n_rows * ROW_TILE, LANES), F32),
        compiler_params=pltpu.CompilerParams(dimension_semantics=("arbitrary",),
                                             vmem_limit_bytes=VMEM_LIMIT),
        name="dispatch",
    )(dest, pad_start, pad_count, used_chunks, h, ffn_g, zeros)


X_SLOTS = 3
TILE_CHUNKS = TM_EXPERT // EXPERT_CHUNK
W_SLOTS = 3


def _expert_kernel(tiles_ref, chunk0_ref, chunks_ref, nt_ref, used_ref, xs_ref, wg_ref, wu_ref, wd_ref,
                   zeros_ref, ys_ref, x_buf, y_buf, sg_buf, su_buf, sd_buf, wgb, wub, wdb, state,
                   w_sems, x_sems, y_sems, zsem):
    t = pl.program_id(0)
    last = pl.num_programs(0) - 1
    nt = nt_ref[0]
    n_chunks = ys_ref.shape[0] // (EXPERT_CHUNK * ROW_TILE)

    def tile_copies(tile, do, out):
        for c in range(TILE_CHUNKS):
            @pl.when(c < chunks_ref[tile])
            def _(c=c):
                first = (chunk0_ref[tile] + c) * EXPERT_CHUNK
                if out:
                    slot = lax.rem(tile, 2)
                    do(pltpu.make_async_copy(_token_rows(y_buf.at[slot], c * EXPERT_CHUNK, EXPERT_CHUNK),
                                             _token_rows(ys_ref, first, EXPERT_CHUNK), y_sems.at[slot]))
                else:
                    slot = lax.rem(tile, X_SLOTS)
                    do(pltpu.make_async_copy(_token_rows(xs_ref, first, EXPERT_CHUNK),
                                             _token_rows(x_buf.at[slot], c * EXPERT_CHUNK, EXPERT_CHUNK),
                                             x_sems.at[slot]))

    start = lambda cp: cp.start()
    wait = lambda cp: cp.wait()

    def tail_copies(do):
        for k in range(N_EXPERTS):
            chunk = used_ref[0] + k

            @pl.when(chunk < n_chunks)
            def _(chunk=chunk):
                do(pltpu.make_async_copy(zeros_ref, _token_rows(ys_ref, chunk * EXPERT_CHUNK, EXPERT_CHUNK),
                                         zsem))

    def weight_copies(e, slot):
        return (pltpu.make_async_copy(wg_ref.at[e], sg_buf.at[slot], w_sems.at[slot]),
                pltpu.make_async_copy(wu_ref.at[e], su_buf.at[slot], w_sems.at[slot]),
                pltpu.make_async_copy(wd_ref.at[e], sd_buf.at[slot], w_sems.at[slot]))

    def next_with_rows(e):
        return lax.while_loop(lambda k: (k < N_EXPERTS) & (tiles_ref[jnp.minimum(k, N_EXPERTS - 1)] == 0),
                              lambda k: k + 1, e + 1)

    @pl.when(t == 0)
    def _():
        first = next_with_rows(jnp.int32(-1))
        second = next_with_rows(first)
        state[0] = jnp.int32(-1)
        state[1] = jnp.int32(0)
        state[2] = jnp.int32(W_SLOTS - 1)
        state[3] = first
        state[4] = second
        for cp in weight_copies(first, 0):
            cp.start()

        @pl.when(second < N_EXPERTS)
        def _():
            for cp in weight_copies(second, 1):
                cp.start()

        tile_copies(0, start, False)

        @pl.when(nt > 1)
        def _():
            tile_copies(1, start, False)

        tail_copies(start)

    @pl.when(t + 2 < nt)
    def _():
        tile_copies(t + 2, start, False)

    @pl.when(t < nt)
    def _():
        @pl.when(state[1] == 0)
        def _():
            e = state[3]
            nxt = state[4]
            slot = lax.rem(state[2] + 1, W_SLOTS)
            after_next = next_with_rows(nxt)
            state[0] = e
            state[1] = tiles_ref[e]
            state[2] = slot
            state[3] = nxt
            state[4] = after_next
            for cp in weight_copies(e, slot):
                cp.wait()

            @pl.when(after_next < N_EXPERTS)
            def _():
                for cp in weight_copies(after_next, lax.rem(slot + 2, W_SLOTS)):
                    cp.start()

            wgb[...] = sg_buf[slot].astype(BF16)
            wub[...] = su_buf[slot].astype(BF16)
            wdb[...] = sd_buf[slot].astype(BF16)

        state[1] = state[1] - 1
        tile_copies(t, wait, False)

        @pl.when(t >= 2)
        def _():
            tile_copies(t - 2, wait, True)

        for n_chunks_here in range(1, TILE_CHUNKS + 1):
            @pl.when(chunks_ref[t] == n_chunks_here)
            def _(m=n_chunks_here * EXPERT_CHUNK):
                x = _tiles_to_rows(x_buf.at[lax.rem(t, X_SLOTS)], m).astype(BF16)
                g = _dot(x, wgb[...])
                u = _dot(x, wub[...])
                hidden = (g * jax.nn.sigmoid(g)) * u
                _rows_to_tiles(y_buf.at[lax.rem(t, 2)], _dot(hidden.astype(BF16), wdb[...]))

        tile_copies(t, start, True)

    @pl.when(t == last)
    def _():
        for back in (2, 1):
            @pl.when(nt >= back)
            def _(back=back):
                tile_copies(nt - back, wait, True)

        tail_copies(wait)


def _experts(tiles, chunk0, chunks, n_tiles, used_chunks, xs, wg, wu, wd):
    any_spec = pl.BlockSpec(memory_space=pl.ANY)
    zeros = jnp.zeros((EXPERT_CHUNK * ROW_TILE, LANES), F32)
    return pl.pallas_call(
        _expert_kernel,
        grid_spec=pltpu.PrefetchScalarGridSpec(
            num_scalar_prefetch=5,
            grid=(chunks.shape[0],),
            in_specs=[any_spec, any_spec, any_spec, any_spec, any_spec],
            out_specs=any_spec,
            scratch_shapes=[pltpu.VMEM((X_SLOTS, TM_EXPERT * ROW_TILE, LANES), F32),
                            pltpu.VMEM((2, TM_EXPERT * ROW_TILE, LANES), F32),
                            pltpu.VMEM((W_SLOTS, D_MODEL, D_EXPERT), F32),
                            pltpu.VMEM((W_SLOTS, D_MODEL, D_EXPERT), F32),
                            pltpu.VMEM((W_SLOTS, D_EXPERT, D_MODEL), F32),
                            pltpu.VMEM((D_MODEL, D_EXPERT), BF16),
                            pltpu.VMEM((D_MODEL, D_EXPERT), BF16),
                            pltpu.VMEM((D_EXPERT, D_MODEL), BF16),
                            pltpu.SMEM((5,), jnp.int32),
                            pltpu.SemaphoreType.DMA((W_SLOTS,)),
                            pltpu.SemaphoreType.DMA((X_SLOTS,)),
                            pltpu.SemaphoreType.DMA((2,)),
                            pltpu.SemaphoreType.DMA]),
        out_shape=jax.ShapeDtypeStruct(xs.shape, F32),
        compiler_params=pltpu.CompilerParams(dimension_semantics=("arbitrary",),
                                             vmem_limit_bytes=VMEM_LIMIT),
        name="expert_mlp",
    )(tiles, chunk0, chunks, n_tiles, used_chunks, xs, wg, wu, wd, zeros)


def _combine_kernel(dest_ref, rw_ref, fg_ref, h_ref, y_ref, o_ref, buf, h_buf, sems, h_sems):
    tm = TM_COMBINE
    i = pl.program_id(0)
    n_steps = pl.num_programs(0)
    n = n_steps * tm
    cur = i % 2

    def h_copy(step, half):
        return pltpu.make_async_copy(h_ref.at[pl.ds(pl.multiple_of(step * tm, tm), tm)], h_buf.at[half],
                                     h_sems.at[half])

    def fetch(step, half):
        h_copy(step, half).start(priority=1)

        def body(r, c):
            for s in range(2):
                pltpu.make_async_copy(_token_rows(y_ref, dest_ref[s * n + step * tm + r], 1),
                                      _token_rows(buf.at[half, s], r, 1),
                                      sems.at[half]).start(priority=s)
            return c

        lax.fori_loop(0, tm, body, 0, unroll=8)

    @pl.when(i == 0)
    def _():
        fetch(0, 0)

    @pl.when(i + 1 < n_steps)
    def _():
        fetch(i + 1, 1 - cur)

    for s in range(2):
        pltpu.make_async_copy(_token_rows(y_ref, 0, tm), buf.at[cur, s], sems.at[cur]).wait()
    h_copy(i, cur).wait()
    rw = rw_ref[...]
    out = (h_buf[cur] + rw[:, 0:1] * _tiles_to_rows(buf.at[cur, 0], tm)
           + rw[:, 1:2] * _tiles_to_rows(buf.at[cur, 1], tm))
    o_ref[...] = _rms(out, fg_ref[...])


def _combine(dest, h, rw, final_g, ys):
    n = h.shape[0]
    return pl.pallas_call(
        _combine_kernel,
        grid_spec=pltpu.PrefetchScalarGridSpec(
            num_scalar_prefetch=1,
            grid=(n // TM_COMBINE,),
            in_specs=[pl.BlockSpec((TM_COMBINE, LANES), lambda i, d: (i, 0)),
                      pl.BlockSpec((1, D_MODEL), lambda i, d: (0, 0)),
                      pl.BlockSpec(memory_space=pl.ANY),
                      pl.BlockSpec(memory_space=pl.ANY)],
            out_specs=pl.BlockSpec((TM_COMBINE, D_MODEL), lambda i, d: (i, 0)),
            scratch_shapes=[pltpu.VMEM((2, 2, TM_COMBINE * ROW_TILE, LANES), F32),
                            pltpu.VMEM((2, TM_COMBINE, D_MODEL), F32),
                            pltpu.SemaphoreType.DMA((2,)),
                            pltpu.SemaphoreType.DMA((2,))]),
        out_shape=jax.ShapeDtypeStruct((n, D_MODEL), F32),
        compiler_params=pltpu.CompilerParams(dimension_semantics=("arbitrary",),
                                             vmem_limit_bytes=VMEM_LIMIT),
        name="combine",
    )(dest, rw, final_g, h, ys)


def _schedule(counts, max_tiles):
    chunks = (counts + EXPERT_CHUNK - 1) // EXPERT_CHUNK
    chunk_end = jnp.cumsum(chunks)
    chunk_start = chunk_end - chunks
    tiles = (chunks + TILE_CHUNKS - 1) // TILE_CHUNKS
    tile_end = jnp.cumsum(tiles)
    tile = jnp.arange(max_tiles, dtype=jnp.int32)
    owner = jnp.sum(tile[:, None] >= tile_end[None, :], axis=1)
    is_owner = owner[:, None] == jnp.arange(N_EXPERTS, dtype=jnp.int32)[None, :]
    of_owner = lambda v: jnp.sum(jnp.where(is_owner, v[None, :], 0), axis=1)
    done = (tile - of_owner(tile_end - tiles)) * TILE_CHUNKS
    tile_chunk0 = (of_owner(chunk_start) + done).astype(jnp.int32)
    tile_chunks = jnp.clip(of_owner(chunks) - done, 0, TILE_CHUNKS).astype(jnp.int32)
    return tiles, chunk_start * EXPERT_CHUNK, tile_chunk0, tile_chunks, tile_end[-1:], chunk_end[-1:]


def _layer(x, attn_g, w_in, sg_g, w_sp, b_sp, sb_g, sg_out_g, w_out, ffn_g,
           w_rg, b_rg, w_re, b_re, w_gate, w_up, w_down):
    batch, seq, _ = x.shape
    n = batch * seq
    x2 = x.reshape(n, D_MODEL)
    row = lambda v: v.reshape(1, -1)

    bsp_full = jnp.repeat(b_sp.T, HEAD_DIM, axis=1)
    qkv, sgn = _inproj(x2, row(attn_g), w_in.astype(BF16), row(sg_g), w_sp, bsp_full, row(sg_out_g))
    sb = _attention(qkv, batch, seq).reshape(n, SB_WIDTH)

    pad_lanes = lambda v, width: jnp.pad(v, [(0, 0)] * (v.ndim - 1) + [(0, width - v.shape[-1])])
    w_r = jnp.concatenate([pad_lanes(w_rg, ROUTER_LANE0),
                           jnp.transpose(w_re, (1, 0, 2)).reshape(D_MODEL, N_EXPERTS)], axis=1)
    w_r = pad_lanes(w_r, LANES)
    wr_hi = w_r.astype(BF16)
    wr_lo = (w_r - wr_hi.astype(F32)).astype(BF16)
    wr2 = jnp.concatenate([wr_hi, wr_lo], axis=1)
    b_r = pad_lanes(jnp.concatenate([pad_lanes(b_rg, ROUTER_LANE0), b_re.reshape(-1)]), LANES)

    h, lg = _mix(sb, sgn, x2, row(sb_g), w_out.astype(BF16), row(ffn_g), wr2, row(b_r))
    ri, rw, cnt = _route(lg)

    counts = cnt[:, 0].astype(jnp.int32)
    n_rows = 2 * n + N_EXPERTS * EXPERT_CHUNK
    tiles, offsets, tile_chunk0, tile_chunks, n_tiles, used_chunks = _schedule(
        counts, 2 * n // TM_EXPERT + N_EXPERTS)
    expert, rank = ri[0:2], ri[2:4]
    is_e = expert[None] == jnp.arange(N_EXPERTS, dtype=jnp.int32)[:, None, None]
    dest = (jnp.sum(jnp.where(is_e, offsets[:, None, None], 0), axis=0) + rank).reshape(-1)
    pad_start = offsets + counts
    pad_count = (-counts) % EXPERT_CHUNK

    xs = _dispatch(dest, pad_start, pad_count, used_chunks, h, row(ffn_g), n_rows)
    ys = _experts(tiles, tile_chunk0, tile_chunks, n_tiles, used_chunks, xs,
                  w_gate.reshape(N_EXPERTS, D_MODEL, D_EXPERT),
                  w_up.reshape(N_EXPERTS, D_MODEL, D_EXPERT),
                  w_down.reshape(N_EXPERTS, D_EXPERT, D_MODEL))
    return dest, h, rw, ys


def kernel(x, attn_norm_g, w_in, sg_norm_g, w_spatial, b_spatial, sb_out_norm_g, sg_out_norm_g,
           w_out, ffn_norm_g, w_router_group, b_router_group, w_router_expert, b_router_expert,
           w_gate, w_up, w_down, final_norm_g):
    assert attn_norm_g.shape[0] == 1, "single-layer problem"
    batch, seq, _ = x.shape
    dest, h, rw, ys = _layer(x, attn_norm_g[0], w_in[0], sg_norm_g[0], w_spatial[0], b_spatial[0],
                             sb_out_norm_g[0], sg_out_norm_g[0], w_out[0], ffn_norm_g[0],
                             w_router_group[0], b_router_group[0], w_router_expert[0],
                             b_router_expert[0], w_gate[0], w_up[0], w_down[0])
    out = _combine(dest, h, rw, final_norm_g.reshape(1, -1), ys)
    return out.reshape(batch, seq, D_MODEL)
```

```python
import functools
import math

import jax
import jax.numpy as jnp
from jax import lax
from jax.experimental import pallas as pl
from jax.experimental.pallas import tpu as pltpu

D_MODEL = 1024
HEAD_DIM = 64
SB_WIDTH = 512
SG_WIDTH = 512
SG_HEADS = 8
D_IN = 3 * SB_WIDTH + 2 * SG_WIDTH
CHUNK = 128
N_GROUPS = 4
EXPERTS_PER_GROUP = 8
N_EXPERTS = N_GROUPS * EXPERTS_PER_GROUP
D_EXPERT = 512
EPS = 1e-6
F32_EXP_UNDERFLOW = 110.0

LANES = 128
SUBLANES = 8
ROW_TILE = D_MODEL // LANES
assert ROW_TILE == SUBLANES
HEAD_PAIR = 2 * HEAD_DIM
ROUTER_LANE0 = SUBLANES
ROUTER_ROWS = ROUTER_LANE0 + N_EXPERTS
assert EXPERTS_PER_GROUP == SUBLANES and N_GROUPS <= ROUTER_LANE0

TM_PROJ = 1024
TQ_ATTN = 256
ATTN_BLOCKS_PER_STEP = 2
ATTN_TOP_ROWS = (160, 176)
TM_MIX = 1024
TM_ROUTE = 1024
TM_DISPATCH = 1024
TM_EXPERT = 640
EXPERT_CHUNK = 128
TM_COMBINE = 256
VMEM_LIMIT = 48 * 1024 * 1024

F32 = jnp.float32
BF16 = jnp.bfloat16


def _rms(x, g):
    return x * lax.rsqrt(jnp.mean(x * x, axis=-1, keepdims=True) + EPS) * g


def _gelu(x):
    c = math.sqrt(2.0 / math.pi)
    return x * (0.5 * (1.0 + jnp.tanh(c * (x + 0.044715 * (x * x * x)))))


def _softplus(z):
    return jnp.maximum(z, 0.0) + jnp.log(1.0 + jnp.exp(-jnp.abs(z)))


def _dot(a, b):
    return jnp.dot(a, b, preferred_element_type=F32)


def _rows_to_tiles(ref, x):
    m = x.shape[0]
    for k in range(ROW_TILE):
        ref[pl.ds(k, m, stride=ROW_TILE), :] = x[:, k * LANES:(k + 1) * LANES]


def _tiles_to_rows(ref, m):
    return jnp.concatenate([ref[pl.ds(k, m, stride=ROW_TILE), :] for k in range(ROW_TILE)], axis=1)


def _token_rows(ref, first_token, n_tokens):
    return ref.at[pl.ds(pl.multiple_of(first_token * ROW_TILE, ROW_TILE), n_tokens * ROW_TILE)]


def _split_bf16(x):
    hi = x.astype(BF16)
    lo = (x - hi.astype(F32)).astype(BF16)
    return hi, lo


def _inproj_kernel(x_ref, g_ref, w_ref, sgg_ref, wsp_ref, bsp_ref, sgog_ref, qkv_ref, sgn_ref,
                   gu_ref, vgn_ref, sg_ref):
    tm = TM_PROJ
    hb = _rms(x_ref[...], g_ref[...]).astype(BF16)
    gv = _gelu(_dot(hb, w_ref[:, 3 * SB_WIDTH + SG_WIDTH:D_IN]))
    vgn_ref[...] = _rms(gv, sgg_ref[...]).astype(BF16)
    gu_ref[...] = _gelu(_dot(hb, w_ref[:, 3 * SB_WIDTH:3 * SB_WIDTH + SG_WIDTH]))
    q = _dot(hb, w_ref[:, 0:SB_WIDTH]) * (1.0 / math.sqrt(HEAD_DIM))
    qkv_ref[:, 0:SB_WIDTH] = q.astype(BF16)
    qkv_ref[:, SB_WIDTH:2 * SB_WIDTH] = _dot(hb, w_ref[:, SB_WIDTH:2 * SB_WIDTH]).astype(BF16)

    lane = lax.broadcasted_iota(jnp.int32, (1, LANES), 1)
    first = lane < HEAD_DIM
    zero = jnp.zeros((), BF16)
    r_c = lax.broadcasted_iota(jnp.int32, (CHUNK, CHUNK), 0)
    c_c = lax.broadcasted_iota(jnp.int32, (CHUNK, CHUNK), 1)
    tril = r_c >= c_c
    n_pairs = SG_WIDTH // HEAD_PAIR
    w_pairs = []
    for p in range(n_pairs):
        w0 = jnp.where(tril, wsp_ref[2 * p], 0.0).astype(BF16)
        w1 = jnp.where(tril, wsp_ref[2 * p + 1], 0.0).astype(BF16)
        w_pairs.append(jnp.concatenate([w0, w1], axis=1))
    bsp = bsp_ref[...]
    for c in range(tm // CHUNK):
        rows = slice(c * CHUNK, (c + 1) * CHUNK)
        for p in range(n_pairs):
            cols = slice(p * HEAD_PAIR, (p + 1) * HEAD_PAIR)
            vg = vgn_ref[rows, cols]
            rhs = jnp.concatenate([jnp.where(first, vg, zero), jnp.where(first, zero, vg)], axis=0)
            mixed = _dot(w_pairs[p], rhs) + bsp[:, cols]
            sg_ref[rows, cols] = gu_ref[rows, cols] * mixed
    qkv_ref[:, 2 * SB_WIDTH:3 * SB_WIDTH] = _dot(hb, w_ref[:, 2 * SB_WIDTH:3 * SB_WIDTH]).astype(BF16)
    sgn_ref[...] = _rms(sg_ref[...], sgog_ref[...]).astype(BF16)


def _inproj(x2, attn_g, w_in_b, sg_g, wsp, bsp_full, sg_out_g):
    n = x2.shape[0]
    row = lambda i: (i, 0)
    const = lambda i: (0, 0)
    return pl.pallas_call(
        _inproj_kernel,
        grid=(n // TM_PROJ,),
        in_specs=[pl.BlockSpec((TM_PROJ, D_MODEL), row),
                  pl.BlockSpec((1, D_MODEL), const),
                  pl.BlockSpec((D_MODEL, D_IN), const),
                  pl.BlockSpec((1, SG_WIDTH), const),
                  pl.BlockSpec((SG_HEADS, CHUNK, CHUNK), lambda i: (0, 0, 0)),
                  pl.BlockSpec((CHUNK, SG_WIDTH), const),
                  pl.BlockSpec((1, SG_WIDTH), const)],
        out_specs=[pl.BlockSpec((TM_PROJ, 3 * SB_WIDTH), row),
                   pl.BlockSpec((TM_PROJ, SG_WIDTH), row)],
        out_shape=[jax.ShapeDtypeStruct((n, 3 * SB_WIDTH), BF16),
                   jax.ShapeDtypeStruct((n, SG_WIDTH), BF16)],
        scratch_shapes=[pltpu.VMEM((TM_PROJ, SG_WIDTH), F32),
                        pltpu.VMEM((TM_PROJ, SG_WIDTH), BF16),
                        pltpu.VMEM((TM_PROJ, SG_WIDTH), F32)],
        compiler_params=pltpu.CompilerParams(dimension_semantics=("arbitrary",),
                                             vmem_limit_bytes=VMEM_LIMIT),
        name="inproj",
    )(x2, attn_g, w_in_b, sg_g, wsp, bsp_full, sg_out_g)


def _attn_kernel(q_ref, k_ref, v_ref, o_ref, q2_ref, carry_ref):
    t = TQ_ATTN
    n_pairs = SB_WIDTH // HEAD_PAIR
    lane = lax.broadcasted_iota(jnp.int32, (1, HEAD_PAIR), 1)
    head_lanes = (lane < HEAD_DIM, lane >= HEAD_DIM)
    zero = jnp.zeros((), BF16)
    r_idx = lax.broadcasted_iota(jnp.int32, (t, t), 0)
    c_idx = lax.broadcasted_iota(jnp.int32, (t, t), 1)
    suffix = (r_idx > c_idx).astype(BF16)
    suffix2 = jnp.concatenate([suffix, suffix], axis=0)
    causal = c_idx < r_idx

    def one_query_block(sub, c):
        qi = pl.program_id(1) * ATTN_BLOCKS_PER_STEP + sub
        row0 = pl.multiple_of(sub * t, t)
        for p in range(n_pairs):
            qp = q_ref[0, pl.ds(row0, t), p * HEAD_PAIR:(p + 1) * HEAD_PAIR]
            for h in range(2):
                q2_ref[(2 * p + h) * t:(2 * p + h + 1) * t, :] = jnp.where(head_lanes[h], qp, zero)
        o_ref[0, pl.ds(row0, t), :] = jnp.zeros((t, SB_WIDTH), F32)
        carry_ref[...] = jnp.zeros_like(carry_ref)

        def block(j, diag, m):
            start = pl.multiple_of(j * t, t)
            mask2 = jnp.concatenate([causal, causal], axis=0) if diag else None
            st = [dict() for _ in range(n_pairs)]

            def head_rows(p):
                return [slice((2 * p + h) * t, (2 * p + h) * t + m) for h in range(2)]

            def scores(p):
                d = st[p]
                d["cols"] = slice(p * HEAD_PAIR, (p + 1) * HEAD_PAIR)
                kb = k_ref[0, pl.ds(start, t), d["cols"]]
                q2 = jnp.concatenate([q2_ref[r, :] for r in head_rows(p)], axis=0)
                z = lax.dot_general(q2, kb, (((1,), (1,)), ((), ())),
                                    preferred_element_type=F32)
                sp = _softplus(z)
                nl = jnp.where(mask2, sp, 0.0) if diag else sp
                hi, lo = _split_bf16(nl)
                d["hl"] = jnp.concatenate([hi, lo], axis=1)
                d["log_beta"] = z - sp
                d["nl0"] = nl[:, 0:1]

            def weights(p):
                d = st[p]
                hl = d["hl"]
                after = jnp.concatenate([_dot(hl[0:m], suffix2), _dot(hl[m:2 * m], suffix2)], axis=0)
                carry = jnp.concatenate([carry_ref[r, :] for r in head_rows(p)], axis=0)
                a = jnp.exp(d["log_beta"] - after - carry)
                if diag:
                    a = jnp.where(mask2, a, 0.0)
                a = a.astype(BF16)
                d["a2"] = jnp.concatenate([a[0:m], a[m:2 * m]], axis=1)
                new_carry = carry + after[:, 0:1] + d["nl0"]
                for h, r in enumerate(head_rows(p)):
                    carry_ref[r, :] = new_carry[h * m:(h + 1) * m]

            def values(p):
                d = st[p]
                vb = v_ref[0, pl.ds(start, t), d["cols"]]
                v2 = jnp.concatenate([jnp.where(head_lanes[0], vb, zero),
                                      jnp.where(head_lanes[1], vb, zero)], axis=0)
                o_ref[0, pl.ds(row0, m), d["cols"]] += _dot(d["a2"], v2)

            for step in range(n_pairs + 2):
                if step < n_pairs:
                    scores(step)
                if 0 <= step - 1 < n_pairs:
                    weights(step - 1)
                if 0 <= step - 2 < n_pairs:
                    values(step - 2)

        def flags():
            bounds = (0,) + ATTN_TOP_ROWS + (t,)
            lowest = [jnp.min(jnp.concatenate([carry_ref[hh * t + lo:hh * t + hi, :] for hh in range(2 * n_pairs)],
                                              axis=0))
                      for lo, hi in zip(bounds[:-1], bounds[1:])]
            below = [functools.reduce(jnp.minimum, lowest[k:]) for k in range(len(lowest))]
            return (below[0] < F32_EXP_UNDERFLOW,) + tuple(b >= F32_EXP_UNDERFLOW for b in below[1:])

        block(qi, True, t)

        def body(state):
            it, _, *done = state
            j = qi - 1 - it
            for k, m in enumerate(ATTN_TOP_ROWS + (t,)):
                use = done[k] if k < len(done) else True
                if k > 0:
                    use = jnp.logical_and(use, jnp.logical_not(done[k - 1]))

                @pl.when(use)
                def _(m=m):
                    block(j, False, m)

            return (it + 1,) + flags()

        lax.while_loop(lambda s: (s[0] < qi) & s[1], body, (jnp.int32(0),) + flags())
        return c

    lax.fori_loop(0, ATTN_BLOCKS_PER_STEP, one_query_block, 0)


def _attention(qkv, batch, seq):
    qkv3 = qkv.reshape(batch, seq, 3 * SB_WIDTH)
    n_heads = SB_WIDTH // HEAD_DIM
    return pl.pallas_call(
        _attn_kernel,
        grid=(batch, seq // (ATTN_BLOCKS_PER_STEP * TQ_ATTN)),
        in_specs=[pl.BlockSpec((1, ATTN_BLOCKS_PER_STEP * TQ_ATTN, SB_WIDTH), lambda b, i: (b, i, 0)),
                  pl.BlockSpec((1, seq, SB_WIDTH), lambda b, i: (b, 0, 1)),
                  pl.BlockSpec((1, seq, SB_WIDTH), lambda b, i: (b, 0, 2))],
        out_specs=pl.BlockSpec((1, ATTN_BLOCKS_PER_STEP * TQ_ATTN, SB_WIDTH), lambda b, i: (b, i, 0)),
        out_shape=jax.ShapeDtypeStruct((batch, seq, SB_WIDTH), F32),
        scratch_shapes=[pltpu.VMEM((n_heads * TQ_ATTN, HEAD_PAIR), BF16),
                        pltpu.VMEM((n_heads * TQ_ATTN, 1), F32)],
        compiler_params=pltpu.CompilerParams(dimension_semantics=("arbitrary",) * 2,
                                             vmem_limit_bytes=VMEM_LIMIT),
        name="sb_attention",
    )(qkv3, qkv3, qkv3)


def _mix_kernel(sb_ref, sgn_ref, x_ref, sbg_ref, wout_ref, ffng_ref, wr2_ref, br_ref,
                h_ref, lg_ref):
    sbn = _rms(sb_ref[...], sbg_ref[...]).astype(BF16)
    h = x_ref[...] + _dot(sbn, wout_ref[0:SB_WIDTH, :]) + _dot(sgn_ref[...], wout_ref[SB_WIDTH:, :])
    h_ref[...] = h
    hn = _rms(h, ffng_ref[...])

    hn_hi, hn_lo = _split_bf16(hn)
    both = _dot(hn_hi, wr2_ref[...])
    logits = both[:, 0:LANES] + both[:, LANES:] + _dot(hn_lo, wr2_ref[:, 0:LANES]) + br_ref[...]
    lg_ref[...] = logits.T[0:ROUTER_ROWS, :]


def _route_kernel(lg_ref, ri_ref, rw_ref, cnt_ref, count_ref):
    tr = TM_ROUTE
    i = pl.program_id(0)

    @pl.when(i == 0)
    def _():
        count_ref[...] = jnp.zeros_like(count_ref)

    neg = jnp.float32(-jnp.inf)
    row8 = lax.broadcasted_iota(jnp.int32, (SUBLANES, tr), 0)

    def top(v):
        m = jnp.max(v, axis=0, keepdims=True)
        return m, jnp.min(jnp.where(v == m, row8, SUBLANES), axis=0, keepdims=True)

    def group_rows(g):
        return lg_ref[ROUTER_LANE0 + g * EXPERTS_PER_GROUP:ROUTER_LANE0 + (g + 1) * EXPERTS_PER_GROUP, :]

    gl = jnp.where(row8 < N_GROUPS, lg_ref[0:SUBLANES, :], neg)
    gmax, gidx = top(gl)
    gweight = 1.0 / jnp.sum(jnp.exp(gl - gmax), axis=0, keepdims=True)
    el = group_rows(0)
    for g in range(1, N_GROUPS):
        el = jnp.where(gidx == g, group_rows(g), el)
    m1, i1 = top(el)
    m2, i2 = top(jnp.where(row8 == i1, neg, el))
    t21 = jnp.exp(m2 - m1)
    w1 = gweight / (1.0 + t21)
    w2 = gweight * t21 / (1.0 + t21)
    e1 = gidx * EXPERTS_PER_GROUP + i1
    e2 = gidx * EXPERTS_PER_GROUP + i2

    row_e = lax.broadcasted_iota(jnp.int32, (N_EXPERTS, tr), 0)
    sel1 = row_e == e1
    sel2 = row_e == e2
    onehot = jnp.where(sel1 | sel2, 1.0, 0.0)
    r_t = lax.broadcasted_iota(jnp.int32, (tr, tr), 0)
    c_t = lax.broadcasted_iota(jnp.int32, (tr, tr), 1)
    before = (r_t < c_t).astype(BF16)
    running = count_ref[:, 0:1] + _dot(onehot.astype(BF16), before)
    rank1 = jnp.sum(jnp.where(sel1, running, 0.0), axis=0, keepdims=True)
    rank2 = jnp.sum(jnp.where(sel2, running, 0.0), axis=0, keepdims=True)
    new_count = count_ref[:, 0:1] + jnp.sum(onehot, axis=1, keepdims=True)
    count_ref[...] = jnp.broadcast_to(new_count, count_ref.shape)
    cnt_ref[...] = jnp.broadcast_to(new_count, cnt_ref.shape)

    ri_ref[...] = jnp.where(row8 == 0, e1, jnp.where(row8 == 1, e2, jnp.where(
        row8 == 2, rank1.astype(jnp.int32), jnp.where(row8 == 3, rank2.astype(jnp.int32), 0))))
    row128 = lax.broadcasted_iota(jnp.int32, (LANES, tr), 0)
    rw_ref[...] = jnp.where(row128 == 0, w1, jnp.where(row128 == 1, w2, 0.0)).T


def _route(lg):
    n = lg.shape[1]
    return pl.pallas_call(
        _route_kernel,
        grid=(n // TM_ROUTE,),
        in_specs=[pl.BlockSpec((ROUTER_ROWS, TM_ROUTE), lambda i: (0, i))],
        out_specs=[pl.BlockSpec((SUBLANES, TM_ROUTE), lambda i: (0, i)),
                   pl.BlockSpec((TM_ROUTE, LANES), lambda i: (i, 0)),
                   pl.BlockSpec((N_EXPERTS, LANES), lambda i: (0, 0))],
        out_shape=[jax.ShapeDtypeStruct((SUBLANES, n), jnp.int32),
                   jax.ShapeDtypeStruct((n, LANES), F32),
                   jax.ShapeDtypeStruct((N_EXPERTS, LANES), F32)],
        scratch_shapes=[pltpu.VMEM((N_EXPERTS, LANES), F32)],
        compiler_params=pltpu.CompilerParams(dimension_semantics=("arbitrary",),
                                             vmem_limit_bytes=VMEM_LIMIT),
        name="route",
    )(lg)


def _mix(sb, sgn, x2, sb_g, w_out_b, ffn_g, wr2, br):
    n = x2.shape[0]
    row = lambda i: (i, 0)
    const = lambda i: (0, 0)
    return pl.pallas_call(
        _mix_kernel,
        grid=(n // TM_MIX,),
        in_specs=[pl.BlockSpec((TM_MIX, SB_WIDTH), row),
                  pl.BlockSpec((TM_MIX, SG_WIDTH), row),
                  pl.BlockSpec((TM_MIX, D_MODEL), row),
                  pl.BlockSpec((1, SB_WIDTH), const),
                  pl.BlockSpec((D_MODEL, D_MODEL), const),
                  pl.BlockSpec((1, D_MODEL), const),
                  pl.BlockSpec((D_MODEL, 2 * LANES), const),
                  pl.BlockSpec((1, LANES), const)],
        out_specs=[pl.BlockSpec((TM_MIX, D_MODEL), row),
                   pl.BlockSpec((ROUTER_ROWS, TM_MIX), lambda i: (0, i))],
        out_shape=[jax.ShapeDtypeStruct((n, D_MODEL), F32),
                   jax.ShapeDtypeStruct((ROUTER_ROWS, n), F32)],
        compiler_params=pltpu.CompilerParams(dimension_semantics=("arbitrary",),
                                             vmem_limit_bytes=VMEM_LIMIT),
        name="mix_router",
    )(sb, sgn, x2, sb_g, w_out_b, ffn_g, wr2, br)


_PAD_BITS = tuple(1 << b for b in reversed(range(EXPERT_CHUNK.bit_length() - 1)))


def _dispatch_kernel(dest_ref, pad_start_ref, pad_count_ref, used_ref, h_ref, g_ref, zeros_ref, xs_ref,
                     hn_ref, sem, zsem):
    tm = TM_DISPATCH
    i = pl.program_id(0)
    n_steps = pl.num_programs(0) - 1
    n = n_steps * tm
    base = (i - 1) * tm
    prev = hn_ref.at[lax.rem(i + 1, 2)]
    n_chunks = xs_ref.shape[0] // (EXPERT_CHUNK * ROW_TILE)

    def pad_copies(do):
        for e in range(N_EXPERTS):
            start = pad_start_ref[e]
            count = pad_count_ref[e]
            for bit in _PAD_BITS:
                @pl.when((count & bit) != 0)
                def _(start=start, bit=bit):
                    do(pltpu.make_async_copy(_token_rows(zeros_ref, 0, bit),
                                             _token_rows(xs_ref, start, bit), zsem))
                start = start + (count & bit)
        for k in range(N_EXPERTS):
            chunk = used_ref[0] + k

            @pl.when(chunk < n_chunks)
            def _(chunk=chunk):
                do(pltpu.make_async_copy(zeros_ref, _token_rows(xs_ref, chunk * EXPERT_CHUNK, EXPERT_CHUNK),
                                         zsem))

    @pl.when(i == 0)
    def _():
        pad_copies(lambda cp: cp.start())

    @pl.when(i > 0)
    def _():
        def body(r, c):
            src = _token_rows(prev, r, 1)
            for s in range(2):
                pltpu.make_async_copy(src, _token_rows(xs_ref, dest_ref[s * n + base + r], 1),
                                      sem).start(priority=s)
            return c

        lax.fori_loop(0, tm, body, 0, unroll=8)

    @pl.when(i < n_steps)
    def _():
        _rows_to_tiles(hn_ref.at[lax.rem(i, 2)], _rms(h_ref[...], g_ref[...]))

    @pl.when(i > 0)
    def _():
        for _ in range(2):
            pltpu.make_async_copy(prev, _token_rows(xs_ref, 0, tm), sem).wait()

    @pl.when(i == n_steps)
    def _():
        pad_copies(lambda cp: cp.wait())


def _dispatch(dest, pad_start, pad_count, used_chunks, h, ffn_g, n_rows):
    n_steps = h.shape[0] // TM_DISPATCH
    zeros = jnp.zeros((EXPERT_CHUNK * ROW_TILE, LANES), F32)
    return pl.pallas_call(
        _dispatch_kernel,
        grid_spec=pltpu.PrefetchScalarGridSpec(
            num_scalar_prefetch=4,
            grid=(n_steps + 1,),
            in_specs=[pl.BlockSpec((TM_DISPATCH, D_MODEL), lambda i, *_: (jnp.minimum(i, n_steps - 1), 0)),
                      pl.BlockSpec((1, D_MODEL), lambda i, *_: (0, 0)),
                      pl.BlockSpec(memory_space=pl.ANY)],
            out_specs=pl.BlockSpec(memory_space=pl.ANY),
            scratch_shapes=[pltpu.VMEM((2, TM_DISPATCH * ROW_TILE, LANES), F32),
                            pltpu.SemaphoreType.DMA, pltpu.SemaphoreType.DMA]),
        out_shape=jax.ShapeDtypeStruct((n_rows * ROW_TILE, LANES), F32),
        compiler_params=pltpu.CompilerParams(dimension_semantics=("arbitrary",),
                                             vmem_limit_bytes=VMEM_LIMIT),
        name="dispatch",
    )(dest, pad_start, pad_count, used_chunks, h, ffn_g, zeros)


X_SLOTS = 3
TILE_CHUNKS = TM_EXPERT // EXPERT_CHUNK
W_SLOTS = 3


def _expert_kernel(tiles_ref, chunk0_ref, chunks_ref, nt_ref, used_ref, xs_ref, wg_ref, wu_ref, wd_ref,
                   zeros_ref, ys_ref, x_buf, y_buf, sg_buf, su_buf, sd_buf, wgb, wub, wdb, state,
                   w_sems, x_sems, y_sems, zsem):
    t = pl.program_id(0)
    last = pl.num_programs(0) - 1
    nt = nt_ref[0]
    n_chunks = ys_ref.shape[0] // (EXPERT_CHUNK * ROW_TILE)

    def tile_copies(tile, do, out):
        for c in range(TILE_CHUNKS):
            @pl.when(c < chunks_ref[tile])
            def _(c=c):
                first = (chunk0_ref[tile] + c) * EXPERT_CHUNK
                if out:
                    slot = lax.rem(tile, 2)
                    do(pltpu.make_async_copy(_token_rows(y_buf.at[slot], c * EXPERT_CHUNK, EXPERT_CHUNK),
                                             _token_rows(ys_ref, first, EXPERT_CHUNK), y_sems.at[slot]))
                else:
                    slot = lax.rem(tile, X_SLOTS)
                    do(pltpu.make_async_copy(_token_rows(xs_ref, first, EXPERT_CHUNK),
                                             _token_rows(x_buf.at[slot], c * EXPERT_CHUNK, EXPERT_CHUNK),
                                             x_sems.at[slot]))

    start = lambda cp: cp.start()
    wait = lambda cp: cp.wait()

    def tail_copies(do):
        for k in range(N_EXPERTS):
            chunk = used_ref[0] + k

            @pl.when(chunk < n_chunks)
            def _(chunk=chunk):
                do(pltpu.make_async_copy(zeros_ref, _token_rows(ys_ref, chunk * EXPERT_CHUNK, EXPERT_CHUNK),
                                         zsem))

    def weight_copies(e, slot):
        return (pltpu.make_async_copy(wg_ref.at[e], sg_buf.at[slot], w_sems.at[slot]),
                pltpu.make_async_copy(wu_ref.at[e], su_buf.at[slot], w_sems.at[slot]),
                pltpu.make_async_copy(wd_ref.at[e], sd_buf.at[slot], w_sems.at[slot]))

    def next_with_rows(e):
        return lax.while_loop(lambda k: (k < N_EXPERTS) & (tiles_ref[jnp.minimum(k, N_EXPERTS - 1)] == 0),
                              lambda k: k + 1, e + 1)

    @pl.when(t == 0)
    def _():
        first = next_with_rows(jnp.int32(-1))
        second = next_with_rows(first)
        state[0] = jnp.int32(-1)
        state[1] = jnp.int32(0)
        state[2] = jnp.int32(W_SLOTS - 1)
        state[3] = first
        state[4] = second
        for cp in weight_copies(first, 0):
            cp.start()

        @pl.when(second < N_EXPERTS)
        def _():
            for cp in weight_copies(second, 1):
                cp.start()

        tile_copies(0, start, False)

        @pl.when(nt > 1)
        def _():
            tile_copies(1, start, False)

        tail_copies(start)

    @pl.when(t + 2 < nt)
    def _():
        tile_copies(t + 2, start, False)

    @pl.when(t < nt)
    def _():
        @pl.when(state[1] == 0)
        def _():
            e = state[3]
            nxt = state[4]
            slot = lax.rem(state[2] + 1, W_SLOTS)
            after_next = next_with_rows(nxt)
            state[0] = e
            state[1] = tiles_ref[e]
            state[2] = slot
            state[3] = nxt
            state[4] = after_next
            for cp in weight_copies(e, slot):
                cp.wait()

            @pl.when(after_next < N_EXPERTS)
            def _():
                for cp in weight_copies(after_next, lax.rem(slot + 2, W_SLOTS)):
                    cp.start()

            wgb[...] = sg_buf[slot].astype(BF16)
            wub[...] = su_buf[slot].astype(BF16)
            wdb[...] = sd_buf[slot].astype(BF16)

        state[1] = state[1] - 1
        tile_copies(t, wait, False)

        @pl.when(t >= 2)
        def _():
            tile_copies(t - 2, wait, True)

        for n_chunks_here in range(1, TILE_CHUNKS + 1):
            @pl.when(chunks_ref[t] == n_chunks_here)
            def _(m=n_chunks_here * EXPERT_CHUNK):
                x = _tiles_to_rows(x_buf.at[lax.rem(t, X_SLOTS)], m).astype(BF16)
                g = _dot(x, wgb[...])
                u = _dot(x, wub[...])
                hidden = (g * jax.nn.sigmoid(g)) * u
                _rows_to_tiles(y_buf.at[lax.rem(t, 2)], _dot(hidden.astype(BF16), wdb[...]))

        tile_copies(t, start, True)

    @pl.when(t == last)
    def _():
        for back in (2, 1):
            @pl.when(nt >= back)
            def _(back=back):
                tile_copies(nt - back, wait, True)

        tail_copies(wait)


def _experts(tiles, chunk0, chunks, n_tiles, used_chunks, xs, wg, wu, wd):
    any_spec = pl.BlockSpec(memory_space=pl.ANY)
    zeros = jnp.zeros((EXPERT_CHUNK * ROW_TILE, LANES), F32)
    return pl.pallas_call(
        _expert_kernel,
        grid_spec=pltpu.PrefetchScalarGridSpec(
            num_scalar_prefetch=5,
            grid=(chunks.shape[0],),
            in_specs=[any_spec, any_spec, any_spec, any_spec, any_spec],
            out_specs=any_spec,
            scratch_shapes=[pltpu.VMEM((X_SLOTS, TM_EXPERT * ROW_TILE, LANES), F32),
                            pltpu.VMEM((2, TM_EXPERT * ROW_TILE, LANES), F32),
                            pltpu.VMEM((W_SLOTS, D_MODEL, D_EXPERT), F32),
                            pltpu.VMEM((W_SLOTS, D_MODEL, D_EXPERT), F32),
                            pltpu.VMEM((W_SLOTS, D_EXPERT, D_MODEL), F32),
                            pltpu.VMEM((D_MODEL, D_EXPERT), BF16),
                            pltpu.VMEM((D_MODEL, D_EXPERT), BF16),
                            pltpu.VMEM((D_EXPERT, D_MODEL), BF16),
                            pltpu.SMEM((5,), jnp.int32),
                            pltpu.SemaphoreType.DMA((W_SLOTS,)),
                            pltpu.SemaphoreType.DMA((X_SLOTS,)),
                            pltpu.SemaphoreType.DMA((2,)),
                            pltpu.SemaphoreType.DMA]),
        out_shape=jax.ShapeDtypeStruct(xs.shape, F32),
        compiler_params=pltpu.CompilerParams(dimension_semantics=("arbitrary",),
                                             vmem_limit_bytes=VMEM_LIMIT),
        name="expert_mlp",
    )(tiles, chunk0, chunks, n_tiles, used_chunks, xs, wg, wu, wd, zeros)


def _combine_kernel(dest_ref, rw_ref, fg_ref, h_ref, y_ref, o_ref, buf, h_buf, sems, h_sems):
    tm = TM_COMBINE
    i = pl.program_id(0)
    n_steps = pl.num_programs(0)
    n = n_steps * tm
    cur = i % 2

    def h_copy(step, half):
        return pltpu.make_async_copy(h_ref.at[pl.ds(pl.multiple_of(step * tm, tm), tm)], h_buf.at[half],
                                     h_sems.at[half])

    def fetch(step, half):
        def body(r, c):
            for s in range(2):
                pltpu.make_async_copy(_token_rows(y_ref, dest_ref[s * n + step * tm + r], 1),
                                      _token_rows(buf.at[half, s], r, 1),
                                      sems.at[half]).start(priority=s)
            return c

        lax.fori_loop(0, tm, body, 0, unroll=8)
        h_copy(step, half).start(priority=1)

    @pl.when(i == 0)
    def _():
        fetch(0, 0)

    @pl.when(i + 1 < n_steps)
    def _():
        fetch(i + 1, 1 - cur)

    for s in range(2):
        pltpu.make_async_copy(_token_rows(y_ref, 0, tm), buf.at[cur, s], sems.at[cur]).wait()
    h_copy(i, cur).wait()
    rw = rw_ref[...]
    out = (h_buf[cur] + rw[:, 0:1] * _tiles_to_rows(buf.at[cur, 0], tm)
           + rw[:, 1:2] * _tiles_to_rows(buf.at[cur, 1], tm))
    o_ref[...] = _rms(out, fg_ref[...])


def _combine(dest, h, rw, final_g, ys):
    n = h.shape[0]
    return pl.pallas_call(
        _combine_kernel,
        grid_spec=pltpu.PrefetchScalarGridSpec(
            num_scalar_prefetch=1,
            grid=(n // TM_COMBINE,),
            in_specs=[pl.BlockSpec((TM_COMBINE, LANES), lambda i, d: (i, 0)),
                      pl.BlockSpec((1, D_MODEL), lambda i, d: (0, 0)),
                      pl.BlockSpec(memory_space=pl.ANY),
                      pl.BlockSpec(memory_space=pl.ANY)],
            out_specs=pl.BlockSpec((TM_COMBINE, D_MODEL), lambda i, d: (i, 0)),
            scratch_shapes=[pltpu.VMEM((2, 2, TM_COMBINE * ROW_TILE, LANES), F32),
                            pltpu.VMEM((2, TM_COMBINE, D_MODEL), F32),
                            pltpu.SemaphoreType.DMA((2,)),
                            pltpu.SemaphoreType.DMA((2,))]),
        out_shape=jax.ShapeDtypeStruct((n, D_MODEL), F32),
        compiler_params=pltpu.CompilerParams(dimension_semantics=("arbitrary",),
                                             vmem_limit_bytes=VMEM_LIMIT),
        name="combine",
    )(dest, rw, final_g, h, ys)


def _schedule(counts, max_tiles):
    chunks = (counts + EXPERT_CHUNK - 1) // EXPERT_CHUNK
    chunk_end = jnp.cumsum(chunks)
    chunk_start = chunk_end - chunks
    tiles = (chunks + TILE_CHUNKS - 1) // TILE_CHUNKS
    tile_end = jnp.cumsum(tiles)
    tile = jnp.arange(max_tiles, dtype=jnp.int32)
    owner = jnp.sum(tile[:, None] >= tile_end[None, :], axis=1)
    is_owner = owner[:, None] == jnp.arange(N_EXPERTS, dtype=jnp.int32)[None, :]
    of_owner = lambda v: jnp.sum(jnp.where(is_owner, v[None, :], 0), axis=1)
    done = (tile - of_owner(tile_end - tiles)) * TILE_CHUNKS
    tile_chunk0 = (of_owner(chunk_start) + done).astype(jnp.int32)
    tile_chunks = jnp.clip(of_owner(chunks) - done, 0, TILE_CHUNKS).astype(jnp.int32)
    return tiles, chunk_start * EXPERT_CHUNK, tile_chunk0, tile_chunks, tile_end[-1:], chunk_end[-1:]


def _layer(x, attn_g, w_in, sg_g, w_sp, b_sp, sb_g, sg_out_g, w_out, ffn_g,
           w_rg, b_rg, w_re, b_re, w_gate, w_up, w_down):
    batch, seq, _ = x.shape
    n = batch * seq
    x2 = x.reshape(n, D_MODEL)
    row = lambda v: v.reshape(1, -1)

    bsp_full = jnp.repeat(b_sp.T, HEAD_DIM, axis=1)
    qkv, sgn = _inproj(x2, row(attn_g), w_in.astype(BF16), row(sg_g), w_sp, bsp_full, row(sg_out_g))
    sb = _attention(qkv, batch, seq).reshape(n, SB_WIDTH)

    pad_lanes = lambda v, width: jnp.pad(v, [(0, 0)] * (v.ndim - 1) + [(0, width - v.shape[-1])])
    w_r = jnp.concatenate([pad_lanes(w_rg, ROUTER_LANE0),
                           jnp.transpose(w_re, (1, 0, 2)).reshape(D_MODEL, N_EXPERTS)], axis=1)
    w_r = pad_lanes(w_r, LANES)
    wr_hi = w_r.astype(BF16)
    wr_lo = (w_r - wr_hi.astype(F32)).astype(BF16)
    wr2 = jnp.concatenate([wr_hi, wr_lo], axis=1)
    b_r = pad_lanes(jnp.concatenate([pad_lanes(b_rg, ROUTER_LANE0), b_re.reshape(-1)]), LANES)

    h, lg = _mix(sb, sgn, x2, row(sb_g), w_out.astype(BF16), row(ffn_g), wr2, row(b_r))
    ri, rw, cnt = _route(lg)

    counts = cnt[:, 0].astype(jnp.int32)
    n_rows = 2 * n + N_EXPERTS * EXPERT_CHUNK
    tiles, offsets, tile_chunk0, tile_chunks, n_tiles, used_chunks = _schedule(
        counts, 2 * n // TM_EXPERT + N_EXPERTS)
    expert, rank = ri[0:2], ri[2:4]
    is_e = expert[None] == jnp.arange(N_EXPERTS, dtype=jnp.int32)[:, None, None]
    dest = (jnp.sum(jnp.where(is_e, offsets[:, None, None], 0), axis=0) + rank).reshape(-1)
    pad_start = offsets + counts
    pad_count = (-counts) % EXPERT_CHUNK

    xs = _dispatch(dest, pad_start, pad_count, used_chunks, h, row(ffn_g), n_rows)
    ys = _experts(tiles, tile_chunk0, tile_chunks, n_tiles, used_chunks, xs,
                  w_gate.reshape(N_EXPERTS, D_MODEL, D_EXPERT),
                  w_up.reshape(N_EXPERTS, D_MODEL, D_EXPERT),
                  w_down.reshape(N_EXPERTS, D_EXPERT, D_MODEL))
    return dest, h, rw, ys


def kernel(x, attn_norm_g, w_in, sg_norm_g, w_spatial, b_spatial, sb_out_norm_g, sg_out_norm_g,
           w_out, ffn_norm_g, w_router_group, b_router_group, w_router_expert, b_router_expert,
           w_gate, w_up, w_down, final_norm_g):
    assert attn_norm_g.shape[0] == 1, "single-layer problem"
    batch, seq, _ = x.shape
    dest, h, rw, ys = _layer(x, attn_norm_g[0], w_in[0], sg_norm_g[0], w_spatial[0], b_spatial[0],
                             sb_out_norm_g[0], sg_out_norm_g[0], w_out[0], ffn_norm_g[0],
                             w_router_group[0], b_router_group[0], w_router_expert[0],
                             b_router_expert[0], w_gate[0], w_up[0], w_down[0])
    out = _combine(dest, h, rw, final_norm_g.reshape(1, -1), ys)
    return out.reshape(batch, seq, D_MODEL)
```

```python
import functools
import math

import jax
import jax.numpy as jnp
from jax import lax
from jax.experimental import pallas as pl
from jax.experimental.pallas import tpu as pltpu

D_MODEL = 1024
HEAD_DIM = 64
SB_WIDTH = 512
SG_WIDTH = 512
SG_HEADS = 8
D_IN = 3 * SB_WIDTH + 2 * SG_WIDTH
CHUNK = 128
N_GROUPS = 4
EXPERTS_PER_GROUP = 8
N_EXPERTS = N_GROUPS * EXPERTS_PER_GROUP
D_EXPERT = 512
EPS = 1e-6
F32_EXP_UNDERFLOW = 110.0

LANES = 128
SUBLANES = 8
ROW_TILE = D_MODEL // LANES
assert ROW_TILE == SUBLANES
HEAD_PAIR = 2 * HEAD_DIM
ROUTER_LANE0 = SUBLANES
ROUTER_ROWS = ROUTER_LANE0 + N_EXPERTS
assert EXPERTS_PER_GROUP == SUBLANES and N_GROUPS <= ROUTER_LANE0

TM_PROJ = 1024
TQ_ATTN = 256
ATTN_BLOCKS_PER_STEP = 2
ATTN_TOP_ROWS = (160, 176)
TM_MIX = 1024
TM_ROUTE = 1024
TM_DISPATCH = 1024
TM_EXPERT = 640
EXPERT_CHUNK = 128
TM_COMBINE = 512
VMEM_LIMIT = 48 * 1024 * 1024

F32 = jnp.float32
BF16 = jnp.bfloat16


def _rms(x, g):
    return x * lax.rsqrt(jnp.mean(x * x, axis=-1, keepdims=True) + EPS) * g


def _gelu(x):
    c = math.sqrt(2.0 / math.pi)
    return x * (0.5 * (1.0 + jnp.tanh(c * (x + 0.044715 * (x * x * x)))))


def _softplus(z):
    return jnp.maximum(z, 0.0) + jnp.log(1.0 + jnp.exp(-jnp.abs(z)))


def _dot(a, b):
    return jnp.dot(a, b, preferred_element_type=F32)


def _rows_to_tiles(ref, x):
    m = x.shape[0]
    for k in range(ROW_TILE):
        ref[pl.ds(k, m, stride=ROW_TILE), :] = x[:, k * LANES:(k + 1) * LANES]


def _tiles_to_rows(ref, m):
    return jnp.concatenate([ref[pl.ds(k, m, stride=ROW_TILE), :] for k in range(ROW_TILE)], axis=1)


def _token_rows(ref, first_token, n_tokens):
    return ref.at[pl.ds(pl.multiple_of(first_token * ROW_TILE, ROW_TILE), n_tokens * ROW_TILE)]


def _split_bf16(x):
    hi = x.astype(BF16)
    lo = (x - hi.astype(F32)).astype(BF16)
    return hi, lo


def _inproj_kernel(x_ref, g_ref, w_ref, sgg_ref, wsp_ref, bsp_ref, sgog_ref, qkv_ref, sgn_ref,
                   gu_ref, vgn_ref, sg_ref):
    tm = TM_PROJ
    hb = _rms(x_ref[...], g_ref[...]).astype(BF16)
    gv = _gelu(_dot(hb, w_ref[:, 3 * SB_WIDTH + SG_WIDTH:D_IN]))
    vgn_ref[...] = _rms(gv, sgg_ref[...]).astype(BF16)
    gu_ref[...] = _gelu(_dot(hb, w_ref[:, 3 * SB_WIDTH:3 * SB_WIDTH + SG_WIDTH]))
    q = _dot(hb, w_ref[:, 0:SB_WIDTH]) * (1.0 / math.sqrt(HEAD_DIM))
    qkv_ref[:, 0:SB_WIDTH] = q.astype(BF16)
    qkv_ref[:, SB_WIDTH:2 * SB_WIDTH] = _dot(hb, w_ref[:, SB_WIDTH:2 * SB_WIDTH]).astype(BF16)

    lane = lax.broadcasted_iota(jnp.int32, (1, LANES), 1)
    first = lane < HEAD_DIM
    zero = jnp.zeros((), BF16)
    r_c = lax.broadcasted_iota(jnp.int32, (CHUNK, CHUNK), 0)
    c_c = lax.broadcasted_iota(jnp.int32, (CHUNK, CHUNK), 1)
    tril = r_c >= c_c
    n_pairs = SG_WIDTH // HEAD_PAIR
    w_pairs = []
    for p in range(n_pairs):
        w0 = jnp.where(tril, wsp_ref[2 * p], 0.0).astype(BF16)
        w1 = jnp.where(tril, wsp_ref[2 * p + 1], 0.0).astype(BF16)
        w_pairs.append(jnp.concatenate([w0, w1], axis=1))
    bsp = bsp_ref[...]
    for c in range(tm // CHUNK):
        rows = slice(c * CHUNK, (c + 1) * CHUNK)
        for p in range(n_pairs):
            cols = slice(p * HEAD_PAIR, (p + 1) * HEAD_PAIR)
            vg = vgn_ref[rows, cols]
            rhs = jnp.concatenate([jnp.where(first, vg, zero), jnp.where(first, zero, vg)], axis=0)
            mixed = _dot(w_pairs[p], rhs) + bsp[:, cols]
            sg_ref[rows, cols] = gu_ref[rows, cols] * mixed
    qkv_ref[:, 2 * SB_WIDTH:3 * SB_WIDTH] = _dot(hb, w_ref[:, 2 * SB_WIDTH:3 * SB_WIDTH]).astype(BF16)
    sgn_ref[...] = _rms(sg_ref[...], sgog_ref[...]).astype(BF16)


def _inproj(x2, attn_g, w_in_b, sg_g, wsp, bsp_full, sg_out_g):
    n = x2.shape[0]
    row = lambda i: (i, 0)
    const = lambda i: (0, 0)
    return pl.pallas_call(
        _inproj_kernel,
        grid=(n // TM_PROJ,),
        in_specs=[pl.BlockSpec((TM_PROJ, D_MODEL), row),
                  pl.BlockSpec((1, D_MODEL), const),
                  pl.BlockSpec((D_MODEL, D_IN), const),
                  pl.BlockSpec((1, SG_WIDTH), const),
                  pl.BlockSpec((SG_HEADS, CHUNK, CHUNK), lambda i: (0, 0, 0)),
                  pl.BlockSpec((CHUNK, SG_WIDTH), const),
                  pl.BlockSpec((1, SG_WIDTH), const)],
        out_specs=[pl.BlockSpec((TM_PROJ, 3 * SB_WIDTH), row),
                   pl.BlockSpec((TM_PROJ, SG_WIDTH), row)],
        out_shape=[jax.ShapeDtypeStruct((n, 3 * SB_WIDTH), BF16),
                   jax.ShapeDtypeStruct((n, SG_WIDTH), BF16)],
        scratch_shapes=[pltpu.VMEM((TM_PROJ, SG_WIDTH), F32),
                        pltpu.VMEM((TM_PROJ, SG_WIDTH), BF16),
                        pltpu.VMEM((TM_PROJ, SG_WIDTH), F32)],
        compiler_params=pltpu.CompilerParams(dimension_semantics=("arbitrary",),
                                             vmem_limit_bytes=VMEM_LIMIT),
        name="inproj",
    )(x2, attn_g, w_in_b, sg_g, wsp, bsp_full, sg_out_g)


def _attn_kernel(q_ref, k_ref, v_ref, o_ref, q2_ref, carry_ref, sfx_ref):
    t = TQ_ATTN
    n_pairs = SB_WIDTH // HEAD_PAIR
    lane = lax.broadcasted_iota(jnp.int32, (1, HEAD_PAIR), 1)
    head_lanes = (lane < HEAD_DIM, lane >= HEAD_DIM)
    zero = jnp.zeros((), BF16)
    r_idx = lax.broadcasted_iota(jnp.int32, (t, t), 0)
    c_idx = lax.broadcasted_iota(jnp.int32, (t, t), 1)
    suffix = (r_idx > c_idx).astype(BF16)
    sfx_ref[...] = jnp.concatenate([suffix, suffix], axis=0)
    causal = c_idx < r_idx

    def one_query_block(sub, c):
        qi = pl.program_id(1) * ATTN_BLOCKS_PER_STEP + sub
        row0 = pl.multiple_of(sub * t, t)
        for p in range(n_pairs):
            qp = q_ref[0, pl.ds(row0, t), p * HEAD_PAIR:(p + 1) * HEAD_PAIR]
            for h in range(2):
                q2_ref[(2 * p + h) * t:(2 * p + h + 1) * t, :] = jnp.where(head_lanes[h], qp, zero)
        o_ref[0, pl.ds(row0, t), :] = jnp.zeros((t, SB_WIDTH), F32)
        carry_ref[...] = jnp.zeros_like(carry_ref)

        def block(j, diag, m):
            start = pl.multiple_of(j * t, t)
            mask2 = jnp.concatenate([causal, causal], axis=0) if diag else None
            st = [dict() for _ in range(n_pairs)]

            def head_rows(p):
                return [slice((2 * p + h) * t, (2 * p + h) * t + m) for h in range(2)]

            def scores(p):
                d = st[p]
                d["cols"] = slice(p * HEAD_PAIR, (p + 1) * HEAD_PAIR)
                kb = k_ref[0, pl.ds(start, t), d["cols"]]
                q2 = jnp.concatenate([q2_ref[r, :] for r in head_rows(p)], axis=0)
                z = lax.dot_general(q2, kb, (((1,), (1,)), ((), ())),
                                    preferred_element_type=F32)
                sp = _softplus(z)
                nl = jnp.where(mask2, sp, 0.0) if diag else sp
                hi, lo = _split_bf16(nl)
                d["hl"] = jnp.concatenate([hi, lo], axis=1)
                d["log_beta"] = z - sp
                d["nl0"] = nl[:, 0:1]

            def weights(p):
                d = st[p]
                hl = d["hl"]
                after = jnp.concatenate([_dot(hl[0:m], sfx_ref[...]), _dot(hl[m:2 * m], sfx_ref[...])], axis=0)
                carry = jnp.concatenate([carry_ref[r, :] for r in head_rows(p)], axis=0)
                a = jnp.exp(d["log_beta"] - after - carry)
                if diag:
                    a = jnp.where(mask2, a, 0.0)
                a = a.astype(BF16)
                d["a2"] = jnp.concatenate([a[0:m], a[m:2 * m]], axis=1)
                new_carry = carry + after[:, 0:1] + d["nl0"]
                for h, r in enumerate(head_rows(p)):
                    carry_ref[r, :] = new_carry[h * m:(h + 1) * m]

            def values(p):
                d = st[p]
                vb = v_ref[0, pl.ds(start, t), d["cols"]]
                v2 = jnp.concatenate([jnp.where(head_lanes[0], vb, zero),
                                      jnp.where(head_lanes[1], vb, zero)], axis=0)
                o_ref[0, pl.ds(row0, m), d["cols"]] += _dot(d["a2"], v2)

            for step in range(n_pairs + 2):
                if step < n_pairs:
                    scores(step)
                if 0 <= step - 1 < n_pairs:
                    weights(step - 1)
                if 0 <= step - 2 < n_pairs:
                    values(step - 2)

        def flags():
            bounds = (0,) + ATTN_TOP_ROWS + (t,)
            lowest = [jnp.min(jnp.concatenate([carry_ref[hh * t + lo:hh * t + hi, :] for hh in range(2 * n_pairs)],
                                              axis=0))
                      for lo, hi in zip(bounds[:-1], bounds[1:])]
            below = [functools.reduce(jnp.minimum, lowest[k:]) for k in range(len(lowest))]
            return (below[0] < F32_EXP_UNDERFLOW,) + tuple(b >= F32_EXP_UNDERFLOW for b in below[1:])

        block(qi, True, t)

        def body(state):
            it, _, *done = state
            j = qi - 1 - it
            for k, m in enumerate(ATTN_TOP_ROWS + (t,)):
                use = done[k] if k < len(done) else True
                if k > 0:
                    use = jnp.logical_and(use, jnp.logical_not(done[k - 1]))

                @pl.when(use)
                def _(m=m):
                    block(j, False, m)

            return (it + 1,) + flags()

        lax.while_loop(lambda s: (s[0] < qi) & s[1], body, (jnp.int32(0),) + flags())
        return c

    lax.fori_loop(0, ATTN_BLOCKS_PER_STEP, one_query_block, 0)


def _attention(qkv, batch, seq):
    qkv3 = qkv.reshape(batch, seq, 3 * SB_WIDTH)
    n_heads = SB_WIDTH // HEAD_DIM
    return pl.pallas_call(
        _attn_kernel,
        grid=(batch, seq // (ATTN_BLOCKS_PER_STEP * TQ_ATTN)),
        in_specs=[pl.BlockSpec((1, ATTN_BLOCKS_PER_STEP * TQ_ATTN, SB_WIDTH), lambda b, i: (b, i, 0)),
                  pl.BlockSpec((1, seq, SB_WIDTH), lambda b, i: (b, 0, 1)),
                  pl.BlockSpec((1, seq, SB_WIDTH), lambda b, i: (b, 0, 2))],
        out_specs=pl.BlockSpec((1, ATTN_BLOCKS_PER_STEP * TQ_ATTN, SB_WIDTH), lambda b, i: (b, i, 0)),
        out_shape=jax.ShapeDtypeStruct((batch, seq, SB_WIDTH), F32),
        scratch_shapes=[pltpu.VMEM((n_heads * TQ_ATTN, HEAD_PAIR), BF16),
                        pltpu.VMEM((n_heads * TQ_ATTN, 1), F32),
                        pltpu.VMEM((2 * TQ_ATTN, TQ_ATTN), BF16)],
        compiler_params=pltpu.CompilerParams(dimension_semantics=("arbitrary",) * 2,
                                             vmem_limit_bytes=VMEM_LIMIT),
        name="sb_attention",
    )(qkv3, qkv3, qkv3)


def _mix_kernel(sb_ref, sgn_ref, x_ref, sbg_ref, wout_ref, ffng_ref, wr2_ref, br_ref,
                h_ref, lg_ref):
    sbn = _rms(sb_ref[...], sbg_ref[...]).astype(BF16)
    h = x_ref[...] + _dot(sbn, wout_ref[0:SB_WIDTH, :]) + _dot(sgn_ref[...], wout_ref[SB_WIDTH:, :])
    h_ref[...] = h
    hn = _rms(h, ffng_ref[...])

    hn_hi, hn_lo = _split_bf16(hn)
    both = _dot(hn_hi, wr2_ref[...])
    logits = both[:, 0:LANES] + both[:, LANES:] + _dot(hn_lo, wr2_ref[:, 0:LANES]) + br_ref[...]
    lg_ref[...] = logits.T[0:ROUTER_ROWS, :]


def _route_kernel(lg_ref, ri_ref, rw_ref, cnt_ref, count_ref):
    tr = TM_ROUTE
    i = pl.program_id(0)

    @pl.when(i == 0)
    def _():
        count_ref[...] = jnp.zeros_like(count_ref)

    neg = jnp.float32(-jnp.inf)
    row8 = lax.broadcasted_iota(jnp.int32, (SUBLANES, tr), 0)

    def top(v):
        m = jnp.max(v, axis=0, keepdims=True)
        return m, jnp.min(jnp.where(v == m, row8, SUBLANES), axis=0, keepdims=True)

    def group_rows(g):
        return lg_ref[ROUTER_LANE0 + g * EXPERTS_PER_GROUP:ROUTER_LANE0 + (g + 1) * EXPERTS_PER_GROUP, :]

    gl = jnp.where(row8 < N_GROUPS, lg_ref[0:SUBLANES, :], neg)
    gmax, gidx = top(gl)
    gweight = 1.0 / jnp.sum(jnp.exp(gl - gmax), axis=0, keepdims=True)
    el = group_rows(0)
    for g in range(1, N_GROUPS):
        el = jnp.where(gidx == g, group_rows(g), el)
    m1, i1 = top(el)
    m2, i2 = top(jnp.where(row8 == i1, neg, el))
    t21 = jnp.exp(m2 - m1)
    w1 = gweight / (1.0 + t21)
    w2 = gweight * t21 / (1.0 + t21)
    e1 = gidx * EXPERTS_PER_GROUP + i1
    e2 = gidx * EXPERTS_PER_GROUP + i2

    row_e = lax.broadcasted_iota(jnp.int32, (N_EXPERTS, tr), 0)
    sel1 = row_e == e1
    sel2 = row_e == e2
    onehot = jnp.where(sel1 | sel2, 1.0, 0.0)
    r_t = lax.broadcasted_iota(jnp.int32, (tr, tr), 0)
    c_t = lax.broadcasted_iota(jnp.int32, (tr, tr), 1)
    before = (r_t < c_t).astype(BF16)
    running = count_ref[:, 0:1] + _dot(onehot.astype(BF16), before)
    rank1 = jnp.sum(jnp.where(sel1, running, 0.0), axis=0, keepdims=True)
    rank2 = jnp.sum(jnp.where(sel2, running, 0.0), axis=0, keepdims=True)
    new_count = count_ref[:, 0:1] + jnp.sum(onehot, axis=1, keepdims=True)
    count_ref[...] = jnp.broadcast_to(new_count, count_ref.shape)
    cnt_ref[...] = jnp.broadcast_to(new_count, cnt_ref.shape)

    ri_ref[...] = jnp.where(row8 == 0, e1, jnp.where(row8 == 1, e2, jnp.where(
        row8 == 2, rank1.astype(jnp.int32), jnp.where(row8 == 3, rank2.astype(jnp.int32), 0))))
    row128 = lax.broadcasted_iota(jnp.int32, (LANES, tr), 0)
    rw_ref[...] = jnp.where(row128 == 0, w1, jnp.where(row128 == 1, w2, 0.0)).T


def _route(lg):
    n = lg.shape[1]
    return pl.pallas_call(
        _route_kernel,
        grid=(n // TM_ROUTE,),
        in_specs=[pl.BlockSpec((ROUTER_ROWS, TM_ROUTE), lambda i: (0, i))],
        out_specs=[pl.BlockSpec((SUBLANES, TM_ROUTE), lambda i: (0, i)),
                   pl.BlockSpec((TM_ROUTE, LANES), lambda i: (i, 0)),
                   pl.BlockSpec((N_EXPERTS, LANES), lambda i: (0, 0))],
        out_shape=[jax.ShapeDtypeStruct((SUBLANES, n), jnp.int32),
                   jax.ShapeDtypeStruct((n, LANES), F32),
                   jax.ShapeDtypeStruct((N_EXPERTS, LANES), F32)],
        scratch_shapes=[pltpu.VMEM((N_EXPERTS, LANES), F32)],
        compiler_params=pltpu.CompilerParams(dimension_semantics=("arbitrary",),
                                             vmem_limit_bytes=VMEM_LIMIT),
        name="route",
    )(lg)


def _mix(sb, sgn, x2, sb_g, w_out_b, ffn_g, wr2, br):
    n = x2.shape[0]
    row = lambda i: (i, 0)
    const = lambda i: (0, 0)
    return pl.pallas_call(
        _mix_kernel,
        grid=(n // TM_MIX,),
        in_specs=[pl.BlockSpec((TM_MIX, SB_WIDTH), row),
                  pl.BlockSpec((TM_MIX, SG_WIDTH), row),
                  pl.BlockSpec((TM_MIX, D_MODEL), row),
                  pl.BlockSpec((1, SB_WIDTH), const),
                  pl.BlockSpec((D_MODEL, D_MODEL), const),
                  pl.BlockSpec((1, D_MODEL), const),
                  pl.BlockSpec((D_MODEL, 2 * LANES), const),
                  pl.BlockSpec((1, LANES), const)],
        out_specs=[pl.BlockSpec((TM_MIX, D_MODEL), row),
                   pl.BlockSpec((ROUTER_ROWS, TM_MIX), lambda i: (0, i))],
        out_shape=[jax.ShapeDtypeStruct((n, D_MODEL), F32),
                   jax.ShapeDtypeStruct((ROUTER_ROWS, n), F32)],
        compiler_params=pltpu.CompilerParams(dimension_semantics=("arbitrary",),
                                             vmem_limit_bytes=VMEM_LIMIT),
        name="mix_router",
    )(sb, sgn, x2, sb_g, w_out_b, ffn_g, wr2, br)


_PAD_BITS = tuple(1 << b for b in reversed(range(EXPERT_CHUNK.bit_length() - 1)))


def _dispatch_kernel(dest_ref, pad_start_ref, pad_count_ref, used_ref, h_ref, g_ref, zeros_ref, xs_ref,
                     hn_ref, sem, zsem):
    tm = TM_DISPATCH
    i = pl.program_id(0)
    n_steps = pl.num_programs(0) - 1
    n = n_steps * tm
    base = (i - 1) * tm
    prev = hn_ref.at[lax.rem(i + 1, 2)]
    n_chunks = xs_ref.shape[0] // (EXPERT_CHUNK * ROW_TILE)

    def pad_copies(do):
        for e in range(N_EXPERTS):
            start = pad_start_ref[e]
            count = pad_count_ref[e]
            for bit in _PAD_BITS:
                @pl.when((count & bit) != 0)
                def _(start=start, bit=bit):
                    do(pltpu.make_async_copy(_token_rows(zeros_ref, 0, bit),
                                             _token_rows(xs_ref, start, bit), zsem))
                start = start + (count & bit)
        for k in range(N_EXPERTS):
            chunk = used_ref[0] + k

            @pl.when(chunk < n_chunks)
            def _(chunk=chunk):
                do(pltpu.make_async_copy(zeros_ref, _token_rows(xs_ref, chunk * EXPERT_CHUNK, EXPERT_CHUNK),
                                         zsem))

    @pl.when(i == 0)
    def _():
        pad_copies(lambda cp: cp.start())

    @pl.when(i > 0)
    def _():
        def body(r, c):
            src = _token_rows(prev, r, 1)
            for s in range(2):
                pltpu.make_async_copy(src, _token_rows(xs_ref, dest_ref[s * n + base + r], 1),
                                      sem).start(priority=s)
            return c

        lax.fori_loop(0, tm, body, 0, unroll=8)

    @pl.when(i < n_steps)
    def _():
        _rows_to_tiles(hn_ref.at[lax.rem(i, 2)], _rms(h_ref[...], g_ref[...]))

    @pl.when(i > 0)
    def _():
        for _ in range(2):
            pltpu.make_async_copy(prev, _token_rows(xs_ref, 0, tm), sem).wait()

    @pl.when(i == n_steps)
    def _():
        pad_copies(lambda cp: cp.wait())


def _dispatch(dest, pad_start, pad_count, used_chunks, h, ffn_g, n_rows):
    n_steps = h.shape[0] // TM_DISPATCH
    zeros = jnp.zeros((EXPERT_CHUNK * ROW_TILE, LANES), F32)
    return pl.pallas_call(
        _dispatch_kernel,
        grid_spec=pltpu.PrefetchScalarGridSpec(
            num_scalar_prefetch=4,
            grid=(n_steps + 1,),
            in_specs=[pl.BlockSpec((TM_DISPATCH, D_MODEL), lambda i, *_: (jnp.minimum(i, n_steps - 1), 0)),
                      pl.BlockSpec((1, D_MODEL), lambda i, *_: (0, 0)),
                      pl.BlockSpec(memory_space=pl.ANY)],
            out_specs=pl.BlockSpec(memory_space=pl.ANY),
            scratch_shapes=[pltpu.VMEM((2, TM_DISPATCH * ROW_TILE, LANES), F32),
                            pltpu.SemaphoreType.DMA, pltpu.SemaphoreType.DMA]),
        out_shape=jax.ShapeDtypeStruct((n_rows * ROW_TILE, LANES), F32),
        compiler_params=pltpu.CompilerParams(dimension_semantics=("arbitrary",),
                                             vmem_limit_bytes=VMEM_LIMIT),
        name="dispatch",
    )(dest, pad_start, pad_count, used_chunks, h, ffn_g, zeros)


X_SLOTS = 3
TILE_CHUNKS = TM_EXPERT // EXPERT_CHUNK
W_SLOTS = 3


def _expert_kernel(tiles_ref, chunk0_ref, chunks_ref, nt_ref, used_ref, xs_ref, wg_ref, wu_ref, wd_ref,
                   zeros_ref, ys_ref, x_buf, y_buf, sg_buf, su_buf, sd_buf, wgb, wub, wdb, state,
                   w_sems, x_sems, y_sems, zsem):
    t = pl.program_id(0)
    last = pl.num_programs(0) - 1
    nt = nt_ref[0]
    n_chunks = ys_ref.shape[0] // (EXPERT_CHUNK * ROW_TILE)

    def tile_copies(tile, do, out):
        for c in range(TILE_CHUNKS):
            @pl.when(c < chunks_ref[tile])
            def _(c=c):
                first = (chunk0_ref[tile] + c) * EXPERT_CHUNK
                if out:
                    slot = lax.rem(tile, 2)
                    do(pltpu.make_async_copy(_token_rows(y_buf.at[slot], c * EXPERT_CHUNK, EXPERT_CHUNK),
                                             _token_rows(ys_ref, first, EXPERT_CHUNK), y_sems.at[slot]))
                else:
                    slot = lax.rem(tile, X_SLOTS)
                    do(pltpu.make_async_copy(_token_rows(xs_ref, first, EXPERT_CHUNK),
                                             _token_rows(x_buf.at[slot], c * EXPERT_CHUNK, EXPERT_CHUNK),
                                             x_sems.at[slot]))

    start = lambda cp: cp.start()
    wait = lambda cp: cp.wait()

    def tail_copies(do):
        for k in range(N_EXPERTS):
            chunk = used_ref[0] + k

            @pl.when(chunk < n_chunks)
            def _(chunk=chunk):
                do(pltpu.make_async_copy(zeros_ref, _token_rows(ys_ref, chunk * EXPERT_CHUNK, EXPERT_CHUNK),
                                         zsem))

    def weight_copies(e, slot):
        return (pltpu.make_async_copy(wg_ref.at[e], sg_buf.at[slot], w_sems.at[slot]),
                pltpu.make_async_copy(wu_ref.at[e], su_buf.at[slot], w_sems.at[slot]),
                pltpu.make_async_copy(wd_ref.at[e], sd_buf.at[slot], w_sems.at[slot]))

    def next_with_rows(e):
        return lax.while_loop(lambda k: (k < N_EXPERTS) & (tiles_ref[jnp.minimum(k, N_EXPERTS - 1)] == 0),
                              lambda k: k + 1, e + 1)

    @pl.when(t == 0)
    def _():
        first = next_with_rows(jnp.int32(-1))
        second = next_with_rows(first)
        state[0] = jnp.int32(-1)
        state[1] = jnp.int32(0)
        state[2] = jnp.int32(W_SLOTS - 1)
        state[3] = first
        state[4] = second
        for cp in weight_copies(first, 0):
            cp.start()

        @pl.when(second < N_EXPERTS)
        def _():
            for cp in weight_copies(second, 1):
                cp.start()

        tile_copies(0, start, False)

        @pl.when(nt > 1)
        def _():
            tile_copies(1, start, False)

        tail_copies(start)

    @pl.when(t + 2 < nt)
    def _():
        tile_copies(t + 2, start, False)

    @pl.when(t < nt)
    def _():
        @pl.when(state[1] == 0)
        def _():
            e = state[3]
            nxt = state[4]
            slot = lax.rem(state[2] + 1, W_SLOTS)
            after_next = next_with_rows(nxt)
            state[0] = e
            state[1] = tiles_ref[e]
            state[2] = slot
            state[3] = nxt
            state[4] = after_next
            for cp in weight_copies(e, slot):
                cp.wait()

            @pl.when(after_next < N_EXPERTS)
            def _():
                for cp in weight_copies(after_next, lax.rem(slot + 2, W_SLOTS)):
                    cp.start()

            wgb[...] = sg_buf[slot].astype(BF16)
            wub[...] = su_buf[slot].astype(BF16)
            wdb[...] = sd_buf[slot].astype(BF16)

        state[1] = state[1] - 1
        tile_copies(t, wait, False)

        @pl.when(t >= 2)
        def _():
            tile_copies(t - 2, wait, True)

        for n_chunks_here in range(1, TILE_CHUNKS + 1):
            @pl.when(chunks_ref[t] == n_chunks_here)
            def _(m=n_chunks_here * EXPERT_CHUNK):
                x = _tiles_to_rows(x_buf.at[lax.rem(t, X_SLOTS)], m).astype(BF16)
                g = _dot(x, wgb[...])
                u = _dot(x, wub[...])
                hidden = (g * jax.nn.sigmoid(g)) * u
                _rows_to_tiles(y_buf.at[lax.rem(t, 2)], _dot(hidden.astype(BF16), wdb[...]))

        tile_copies(t, start, True)

    @pl.when(t == last)
    def _():
        for back in (2, 1):
            @pl.when(nt >= back)
            def _(back=back):
                tile_copies(nt - back, wait, True)

        tail_copies(wait)


def _experts(tiles, chunk0, chunks, n_tiles, used_chunks, xs, wg, wu, wd):
    any_spec = pl.BlockSpec(memory_space=pl.ANY)
    zeros = jnp.zeros((EXPERT_CHUNK * ROW_TILE, LANES), F32)
    return pl.pallas_call(
        _expert_kernel,
        grid_spec=pltpu.PrefetchScalarGridSpec(
            num_scalar_prefetch=5,
            grid=(chunks.shape[0],),
            in_specs=[any_spec, any_spec, any_spec, any_spec, any_spec],
            out_specs=any_spec,
            scratch_shapes=[pltpu.VMEM((X_SLOTS, TM_EXPERT * ROW_TILE, LANES), F32),
                            pltpu.VMEM((2, TM_EXPERT * ROW_TILE, LANES), F32),
                            pltpu.VMEM((W_SLOTS, D_MODEL, D_EXPERT), F32),
                            pltpu.VMEM((W_SLOTS, D_MODEL, D_EXPERT), F32),
                            pltpu.VMEM((W_SLOTS, D_EXPERT, D_MODEL), F32),
                            pltpu.VMEM((D_MODEL, D_EXPERT), BF16),
                            pltpu.VMEM((D_MODEL, D_EXPERT), BF16),
                            pltpu.VMEM((D_EXPERT, D_MODEL), BF16),
                            pltpu.SMEM((5,), jnp.int32),
                            pltpu.SemaphoreType.DMA((W_SLOTS,)),
                            pltpu.SemaphoreType.DMA((X_SLOTS,)),
                            pltpu.SemaphoreType.DMA((2,)),
                            pltpu.SemaphoreType.DMA]),
        out_shape=jax.ShapeDtypeStruct(xs.shape, F32),
        compiler_params=pltpu.CompilerParams(dimension_semantics=("arbitrary",),
                                             vmem_limit_bytes=VMEM_LIMIT),
        name="expert_mlp",
    )(tiles, chunk0, chunks, n_tiles, used_chunks, xs, wg, wu, wd, zeros)


def _combine_kernel(dest_ref, rw_ref, fg_ref, h_ref, y_ref, o_ref, buf, h_buf, sems, h_sems):
    tm = TM_COMBINE
    i = pl.program_id(0)
    n_steps = pl.num_programs(0)
    n = n_steps * tm
    cur = i % 2

    def h_copy(step, half):
        return pltpu.make_async_copy(h_ref.at[pl.ds(pl.multiple_of(step * tm, tm), tm)], h_buf.at[half],
                                     h_sems.at[half])

    def fetch(step, half):
        h_copy(step, half).start(priority=1)

        def body(r, c):
            for s in range(2):
                pltpu.make_async_copy(_token_rows(y_ref, dest_ref[s * n + step * tm + r], 1),
                                      _token_rows(buf.at[half, s], r, 1),
                                      sems.at[half]).start(priority=s)
            return c

        lax.fori_loop(0, tm, body, 0, unroll=8)

    @pl.when(i == 0)
    def _():
        fetch(0, 0)

    @pl.when(i + 1 < n_steps)
    def _():
        fetch(i + 1, 1 - cur)

    for s in range(2):
        pltpu.make_async_copy(_token_rows(y_ref, 0, tm), buf.at[cur, s], sems.at[cur]).wait()
    h_copy(i, cur).wait()
    rw = rw_ref[...]
    out = (h_buf[cur] + rw[:, 0:1] * _tiles_to_rows(buf.at[cur, 0], tm)
           + rw[:, 1:2] * _tiles_to_rows(buf.at[cur, 1], tm))
    o_ref[...] = _rms(out, fg_ref[...])


def _combine(dest, h, rw, final_g, ys):
    n = h.shape[0]
    return pl.pallas_call(
        _combine_kernel,
        grid_spec=pltpu.PrefetchScalarGridSpec(
            num_scalar_prefetch=1,
            grid=(n // TM_COMBINE,),
            in_specs=[pl.BlockSpec((TM_COMBINE, LANES), lambda i, d: (i, 0)),
                      pl.BlockSpec((1, D_MODEL), lambda i, d: (0, 0)),
                      pl.BlockSpec(memory_space=pl.ANY),
                      pl.BlockSpec(memory_space=pl.ANY)],
            out_specs=pl.BlockSpec((TM_COMBINE, D_MODEL), lambda i, d: (i, 0)),
            scratch_shapes=[pltpu.VMEM((2, 2, TM_COMBINE * ROW_TILE, LANES), F32),
                            pltpu.VMEM((2, TM_COMBINE, D_MODEL), F32),
                            pltpu.SemaphoreType.DMA((2,)),
                            pltpu.SemaphoreType.DMA((2,))]),
        out_shape=jax.ShapeDtypeStruct((n, D_MODEL), F32),
        compiler_params=pltpu.CompilerParams(dimension_semantics=("arbitrary",),
                                             vmem_limit_bytes=VMEM_LIMIT),
        name="combine",
    )(dest, rw, final_g, h, ys)


def _schedule(counts, max_tiles):
    chunks = (counts + EXPERT_CHUNK - 1) // EXPERT_CHUNK
    chunk_end = jnp.cumsum(chunks)
    chunk_start = chunk_end - chunks
    tiles = (chunks + TILE_CHUNKS - 1) // TILE_CHUNKS
    tile_end = jnp.cumsum(tiles)
    tile = jnp.arange(max_tiles, dtype=jnp.int32)
    owner = jnp.sum(tile[:, None] >= tile_end[None, :], axis=1)
    is_owner = owner[:, None] == jnp.arange(N_EXPERTS, dtype=jnp.int32)[None, :]
    of_owner = lambda v: jnp.sum(jnp.where(is_owner, v[None, :], 0), axis=1)
    done = (tile - of_owner(tile_end - tiles)) * TILE_CHUNKS
    tile_chunk0 = (of_owner(chunk_start) + done).astype(jnp.int32)
    tile_chunks = jnp.clip(of_owner(chunks) - done, 0, TILE_CHUNKS).astype(jnp.int32)
    return tiles, chunk_start * EXPERT_CHUNK, tile_chunk0, tile_chunks, tile_end[-1:], chunk_end[-1:]


def _layer(x, attn_g, w_in, sg_g, w_sp, b_sp, sb_g, sg_out_g, w_out, ffn_g,
           w_rg, b_rg, w_re, b_re, w_gate, w_up, w_down):
    batch, seq, _ = x.shape
    n = batch * seq
    x2 = x.reshape(n, D_MODEL)
    row = lambda v: v.reshape(1, -1)

    bsp_full = jnp.repeat(b_sp.T, HEAD_DIM, axis=1)
    qkv, sgn = _inproj(x2, row(attn_g), w_in.astype(BF16), row(sg_g), w_sp, bsp_full, row(sg_out_g))
    sb = _attention(qkv, batch, seq).reshape(n, SB_WIDTH)

    pad_lanes = lambda v, width: jnp.pad(v, [(0, 0)] * (v.ndim - 1) + [(0, width - v.shape[-1])])
    w_r = jnp.concatenate([pad_lanes(w_rg, ROUTER_LANE0),
                           jnp.transpose(w_re, (1, 0, 2)).reshape(D_MODEL, N_EXPERTS)], axis=1)
    w_r = pad_lanes(w_r, LANES)
    wr_hi = w_r.astype(BF16)
    wr_lo = (w_r - wr_hi.astype(F32)).astype(BF16)
    wr2 = jnp.concatenate([wr_hi, wr_lo], axis=1)
    b_r = pad_lanes(jnp.concatenate([pad_lanes(b_rg, ROUTER_LANE0), b_re.reshape(-1)]), LANES)

    h, lg = _mix(sb, sgn, x2, row(sb_g), w_out.astype(BF16), row(ffn_g), wr2, row(b_r))
    ri, rw, cnt = _route(lg)

    counts = cnt[:, 0].astype(jnp.int32)
    n_rows = 2 * n + N_EXPERTS * EXPERT_CHUNK
    tiles, offsets, tile_chunk0, tile_chunks, n_tiles, used_chunks = _schedule(
        counts, 2 * n // TM_EXPERT + N_EXPERTS)
    expert, rank = ri[0:2], ri[2:4]
    is_e = expert[None] == jnp.arange(N_EXPERTS, dtype=jnp.int32)[:, None, None]
    dest = (jnp.sum(jnp.where(is_e, offsets[:, None, None], 0), axis=0) + rank).reshape(-1)
    pad_start = offsets + counts
    pad_count = (-counts) % EXPERT_CHUNK

    xs = _dispatch(dest, pad_start, pad_count, used_chunks, h, row(ffn_g), n_rows)
    ys = _experts(tiles, tile_chunk0, tile_chunks, n_tiles, used_chunks, xs,
                  w_gate.reshape(N_EXPERTS, D_MODEL, D_EXPERT),
                  w_up.reshape(N_EXPERTS, D_MODEL, D_EXPERT),
                  w_down.reshape(N_EXPERTS, D_EXPERT, D_MODEL))
    return dest, h, rw, ys


def kernel(x, attn_norm_g, w_in, sg_norm_g, w_spatial, b_spatial, sb_out_norm_g, sg_out_norm_g,
           w_out, ffn_norm_g, w_router_group, b_router_group, w_router_expert, b_router_expert,
           w_gate, w_up, w_down, final_norm_g):
    assert attn_norm_g.shape[0] == 1, "single-layer problem"
    batch, seq, _ = x.shape
    dest, h, rw, ys = _layer(x, attn_norm_g[0], w_in[0], sg_norm_g[0], w_spatial[0], b_spatial[0],
                             sb_out_norm_g[0], sg_out_norm_g[0], w_out[0], ffn_norm_g[0],
                             w_router_group[0], b_router_group[0], w_router_expert[0],
                             b_router_expert[0], w_gate[0], w_up[0], w_down[0])
    out = _combine(dest, h, rw, final_norm_g.reshape(1, -1), ys)
    return out.reshape(batch, seq, D_MODEL)
```

```python
import functools
import math

import jax
import jax.numpy as jnp
from jax import lax
from jax.experimental import pallas as pl
from jax.experimental.pallas import tpu as pltpu

D_MODEL = 1024
HEAD_DIM = 64
SB_WIDTH = 512
SG_WIDTH = 512
SG_HEADS = 8
D_IN = 3 * SB_WIDTH + 2 * SG_WIDTH
CHUNK = 128
N_GROUPS = 4
EXPERTS_PER_GROUP = 8
N_EXPERTS = N_GROUPS * EXPERTS_PER_GROUP
D_EXPERT = 512
EPS = 1e-6
F32_EXP_UNDERFLOW = 110.0

LANES = 128
SUBLANES = 8
ROW_TILE = D_MODEL // LANES
assert ROW_TILE == SUBLANES
HEAD_PAIR = 2 * HEAD_DIM
ROUTER_LANE0 = SUBLANES
ROUTER_ROWS = ROUTER_LANE0 + N_EXPERTS
assert EXPERTS_PER_GROUP == SUBLANES and N_GROUPS <= ROUTER_LANE0

TM_PROJ = 1024
TQ_ATTN = 256
ATTN_BLOCKS_PER_STEP = 2
ATTN_TOP_ROWS = (160, 176)
TM_MIX = 1024
TM_ROUTE = 1024
TM_DISPATCH = 1024
TM_EXPERT = 640
EXPERT_CHUNK = 128
TM_COMBINE = 512
VMEM_LIMIT = 48 * 1024 * 1024

F32 = jnp.float32
BF16 = jnp.bfloat16


def _rms(x, g):
    return x * lax.rsqrt(jnp.mean(x * x, axis=-1, keepdims=True) + EPS) * g


def _gelu(x):
    c = math.sqrt(2.0 / math.pi)
    return x * (0.5 * (1.0 + jnp.tanh(c * (x + 0.044715 * (x * x * x)))))


def _softplus(z):
    return jnp.maximum(z, 0.0) + jnp.log(1.0 + jnp.exp(-jnp.abs(z)))


def _dot(a, b):
    return jnp.dot(a, b, preferred_element_type=F32)


def _rows_to_tiles(ref, x):
    m = x.shape[0]
    for k in range(ROW_TILE):
        ref[pl.ds(k, m, stride=ROW_TILE), :] = x[:, k * LANES:(k + 1) * LANES]


def _tiles_to_rows(ref, m):
    return jnp.concatenate([ref[pl.ds(k, m, stride=ROW_TILE), :] for k in range(ROW_TILE)], axis=1)


def _token_rows(ref, first_token, n_tokens):
    return ref.at[pl.ds(pl.multiple_of(first_token * ROW_TILE, ROW_TILE), n_tokens * ROW_TILE)]


def _split_bf16(x):
    hi = x.astype(BF16)
    lo = (x - hi.astype(F32)).astype(BF16)
    return hi, lo


def _inproj_kernel(x_ref, g_ref, w_ref, sgg_ref, wsp_ref, bsp_ref, sgog_ref, qkv_ref, sgn_ref,
                   gu_ref, vgn_ref, sg_ref):
    tm = TM_PROJ
    hb = _rms(x_ref[...], g_ref[...]).astype(BF16)
    gv = _gelu(_dot(hb, w_ref[:, 3 * SB_WIDTH + SG_WIDTH:D_IN]))
    vgn_ref[...] = _rms(gv, sgg_ref[...]).astype(BF16)
    gu_ref[...] = _gelu(_dot(hb, w_ref[:, 3 * SB_WIDTH:3 * SB_WIDTH + SG_WIDTH]))
    q = _dot(hb, w_ref[:, 0:SB_WIDTH]) * (1.0 / math.sqrt(HEAD_DIM))
    qkv_ref[:, 0:SB_WIDTH] = q.astype(BF16)
    qkv_ref[:, SB_WIDTH:2 * SB_WIDTH] = _dot(hb, w_ref[:, SB_WIDTH:2 * SB_WIDTH]).astype(BF16)

    lane = lax.broadcasted_iota(jnp.int32, (1, LANES), 1)
    first = lane < HEAD_DIM
    zero = jnp.zeros((), BF16)
    r_c = lax.broadcasted_iota(jnp.int32, (CHUNK, CHUNK), 0)
    c_c = lax.broadcasted_iota(jnp.int32, (CHUNK, CHUNK), 1)
    tril = r_c >= c_c
    n_pairs = SG_WIDTH // HEAD_PAIR
    w_pairs = []
    for p in range(n_pairs):
        w0 = jnp.where(tril, wsp_ref[2 * p], 0.0).astype(BF16)
        w1 = jnp.where(tril, wsp_ref[2 * p + 1], 0.0).astype(BF16)
        w_pairs.append(jnp.concatenate([w0, w1], axis=1))
    bsp = bsp_ref[...]
    for c in range(tm // CHUNK):
        rows = slice(c * CHUNK, (c + 1) * CHUNK)
        for p in range(n_pairs):
            cols = slice(p * HEAD_PAIR, (p + 1) * HEAD_PAIR)
            vg = vgn_ref[rows, cols]
            rhs = jnp.concatenate([jnp.where(first, vg, zero), jnp.where(first, zero, vg)], axis=0)
            mixed = _dot(w_pairs[p], rhs) + bsp[:, cols]
            sg_ref[rows, cols] = gu_ref[rows, cols] * mixed
    qkv_ref[:, 2 * SB_WIDTH:3 * SB_WIDTH] = _dot(hb, w_ref[:, 2 * SB_WIDTH:3 * SB_WIDTH]).astype(BF16)
    sgn_ref[...] = _rms(sg_ref[...], sgog_ref[...]).astype(BF16)


def _inproj(x2, attn_g, w_in_b, sg_g, wsp, bsp_full, sg_out_g):
    n = x2.shape[0]
    row = lambda i: (i, 0)
    const = lambda i: (0, 0)
    return pl.pallas_call(
        _inproj_kernel,
        grid=(n // TM_PROJ,),
        in_specs=[pl.BlockSpec((TM_PROJ, D_MODEL), row),
                  pl.BlockSpec((1, D_MODEL), const),
                  pl.BlockSpec((D_MODEL, D_IN), const),
                  pl.BlockSpec((1, SG_WIDTH), const),
                  pl.BlockSpec((SG_HEADS, CHUNK, CHUNK), lambda i: (0, 0, 0)),
                  pl.BlockSpec((CHUNK, SG_WIDTH), const),
                  pl.BlockSpec((1, SG_WIDTH), const)],
        out_specs=[pl.BlockSpec((TM_PROJ, 3 * SB_WIDTH), row),
                   pl.BlockSpec((TM_PROJ, SG_WIDTH), row)],
        out_shape=[jax.ShapeDtypeStruct((n, 3 * SB_WIDTH), BF16),
                   jax.ShapeDtypeStruct((n, SG_WIDTH), BF16)],
        scratch_shapes=[pltpu.VMEM((TM_PROJ, SG_WIDTH), F32),
                        pltpu.VMEM((TM_PROJ, SG_WIDTH), BF16),
                        pltpu.VMEM((TM_PROJ, SG_WIDTH), F32)],
        compiler_params=pltpu.CompilerParams(dimension_semantics=("arbitrary",),
                                             vmem_limit_bytes=VMEM_LIMIT),
        name="inproj",
    )(x2, attn_g, w_in_b, sg_g, wsp, bsp_full, sg_out_g)


def _attn_kernel(q_ref, k_ref, v_ref, o_ref, q2_ref, carry_ref):
    t = TQ_ATTN
    n_pairs = SB_WIDTH // HEAD_PAIR
    lane = lax.broadcasted_iota(jnp.int32, (1, HEAD_PAIR), 1)
    head_lanes = (lane < HEAD_DIM, lane >= HEAD_DIM)
    zero = jnp.zeros((), BF16)
    r_idx = lax.broadcasted_iota(jnp.int32, (t, t), 0)
    c_idx = lax.broadcasted_iota(jnp.int32, (t, t), 1)
    suffix = (r_idx > c_idx).astype(BF16)
    suffix2 = jnp.concatenate([suffix, suffix], axis=0)
    causal = c_idx < r_idx

    def one_query_block(sub, c):
        qi = pl.program_id(1) * ATTN_BLOCKS_PER_STEP + sub
        row0 = pl.multiple_of(sub * t, t)
        for p in range(n_pairs):
            qp = q_ref[0, pl.ds(row0, t), p * HEAD_PAIR:(p + 1) * HEAD_PAIR]
            for h in range(2):
                q2_ref[(2 * p + h) * t:(2 * p + h + 1) * t, :] = jnp.where(head_lanes[h], qp, zero)
        o_ref[0, pl.ds(row0, t), :] = jnp.zeros((t, SB_WIDTH), F32)
        carry_ref[...] = jnp.zeros_like(carry_ref)

        def block(j, diag, m):
            start = pl.multiple_of(j * t, t)
            mask2 = jnp.concatenate([causal, causal], axis=0) if diag else None
            st = [dict() for _ in range(n_pairs)]

            def head_rows(p):
                return [slice((2 * p + h) * t, (2 * p + h) * t + m) for h in range(2)]

            def scores(p):
                d = st[p]
                d["cols"] = slice(p * HEAD_PAIR, (p + 1) * HEAD_PAIR)
                kb = k_ref[0, pl.ds(start, t), d["cols"]]
                q2 = jnp.concatenate([q2_ref[r, :] for r in head_rows(p)], axis=0)
                z = lax.dot_general(q2, kb, (((1,), (1,)), ((), ())),
                                    preferred_element_type=F32)
                sp = _softplus(z)
                nl = jnp.where(mask2, sp, 0.0) if diag else sp
                hi, lo = _split_bf16(nl)
                d["hl"] = jnp.concatenate([hi, lo], axis=1)
                d["log_beta"] = z - sp
                d["nl0"] = nl[:, 0:1]

            def weights(p):
                d = st[p]
                hl = d["hl"]
                after = jnp.concatenate([_dot(hl[0:m], suffix2), _dot(hl[m:2 * m], suffix2)], axis=0)
                carry = jnp.concatenate([carry_ref[r, :] for r in head_rows(p)], axis=0)
                a = jnp.exp(d["log_beta"] - after - carry)
                if diag:
                    a = jnp.where(mask2, a, 0.0)
                a = a.astype(BF16)
                d["a2"] = jnp.concatenate([a[0:m], a[m:2 * m]], axis=1)
                new_carry = carry + after[:, 0:1] + d["nl0"]
                for h, r in enumerate(head_rows(p)):
                    carry_ref[r, :] = new_carry[h * m:(h + 1) * m]

            def values(p):
                d = st[p]
                vb = v_ref[0, pl.ds(start, t), d["cols"]]
                v2 = jnp.concatenate([jnp.where(head_lanes[0], vb, zero),
                                      jnp.where(head_lanes[1], vb, zero)], axis=0)
                o_ref[0, pl.ds(row0, m), d["cols"]] += _dot(d["a2"], v2)

            for step in range(n_pairs + 2):
                if step < n_pairs:
                    scores(step)
                if 0 <= step - 1 < n_pairs:
                    weights(step - 1)
                if 0 <= step - 2 < n_pairs:
                    values(step - 2)

        def flags():
            bounds = (0,) + ATTN_TOP_ROWS + (t,)
            lowest = [jnp.min(jnp.concatenate([carry_ref[hh * t + lo:hh * t + hi, :] for hh in range(2 * n_pairs)],
                                              axis=0))
                      for lo, hi in zip(bounds[:-1], bounds[1:])]
            below = [functools.reduce(jnp.minimum, lowest[k:]) for k in range(len(lowest))]
            return (below[0] < F32_EXP_UNDERFLOW,) + tuple(b >= F32_EXP_UNDERFLOW for b in below[1:])

        block(qi, True, t)

        def body(state):
            it, _, *done = state
            j = qi - 1 - it
            for k, m in enumerate(ATTN_TOP_ROWS + (t,)):
                use = done[k] if k < len(done) else True
                if k > 0:
                    use = jnp.logical_and(use, jnp.logical_not(done[k - 1]))

                @pl.when(use)
                def _(m=m):
                    block(j, False, m)

            return (it + 1,) + flags()

        lax.while_loop(lambda s: (s[0] < qi) & s[1], body, (jnp.int32(0),) + flags())
        return c

    lax.fori_loop(0, ATTN_BLOCKS_PER_STEP, one_query_block, 0)


def _attention(qkv, batch, seq):
    qkv3 = qkv.reshape(batch, seq, 3 * SB_WIDTH)
    n_heads = SB_WIDTH // HEAD_DIM
    return pl.pallas_call(
        _attn_kernel,
        grid=(batch, seq // (ATTN_BLOCKS_PER_STEP * TQ_ATTN)),
        in_specs=[pl.BlockSpec((1, ATTN_BLOCKS_PER_STEP * TQ_ATTN, SB_WIDTH), lambda b, i: (b, i, 0)),
                  pl.BlockSpec((1, seq, SB_WIDTH), lambda b, i: (b, 0, 1)),
                  pl.BlockSpec((1, seq, SB_WIDTH), lambda b, i: (b, 0, 2))],
        out_specs=pl.BlockSpec((1, ATTN_BLOCKS_PER_STEP * TQ_ATTN, SB_WIDTH), lambda b, i: (b, i, 0)),
        out_shape=jax.ShapeDtypeStruct((batch, seq, SB_WIDTH), F32),
        scratch_shapes=[pltpu.VMEM((n_heads * TQ_ATTN, HEAD_PAIR), BF16),
                        pltpu.VMEM((n_heads * TQ_ATTN, 1), F32)],
        compiler_params=pltpu.CompilerParams(dimension_semantics=("arbitrary",) * 2,
                                             vmem_limit_bytes=VMEM_LIMIT),
        name="sb_attention",
    )(qkv3, qkv3, qkv3)


def _mix_kernel(sb_ref, sgn_ref, x_ref, sbg_ref, wout_ref, ffng_ref, wr2_ref, br_ref,
                h_ref, lg_ref):
    sbn = _rms(sb_ref[...], sbg_ref[...]).astype(BF16)
    h = x_ref[...] + _dot(sbn, wout_ref[0:SB_WIDTH, :]) + _dot(sgn_ref[...], wout_ref[SB_WIDTH:, :])
    h_ref[...] = h
    hn = _rms(h, ffng_ref[...])

    hn_hi, hn_lo = _split_bf16(hn)
    both = _dot(hn_hi, wr2_ref[...])
    logits = both[:, 0:LANES] + both[:, LANES:] + _dot(hn_lo, wr2_ref[:, 0:LANES]) + br_ref[...]
    lg_ref[...] = logits.T[0:ROUTER_ROWS, :]


def _route_kernel(lg_ref, ri_ref, rw_ref, cnt_ref, count_ref):
    tr = TM_ROUTE
    i = pl.program_id(0)

    @pl.when(i == 0)
    def _():
        count_ref[...] = jnp.zeros_like(count_ref)

    neg = jnp.float32(-jnp.inf)
    row8 = lax.broadcasted_iota(jnp.int32, (SUBLANES, tr), 0)

    def top(v):
        m = jnp.max(v, axis=0, keepdims=True)
        return m, jnp.min(jnp.where(v == m, row8, SUBLANES), axis=0, keepdims=True)

    def group_rows(g):
        return lg_ref[ROUTER_LANE0 + g * EXPERTS_PER_GROUP:ROUTER_LANE0 + (g + 1) * EXPERTS_PER_GROUP, :]

    gl = jnp.where(row8 < N_GROUPS, lg_ref[0:SUBLANES, :], neg)
    gmax, gidx = top(gl)
    gweight = 1.0 / jnp.sum(jnp.exp(gl - gmax), axis=0, keepdims=True)
    el = group_rows(0)
    for g in range(1, N_GROUPS):
        el = jnp.where(gidx == g, group_rows(g), el)
    m1, i1 = top(el)
    m2, i2 = top(jnp.where(row8 == i1, neg, el))
    t21 = jnp.exp(m2 - m1)
    w1 = gweight / (1.0 + t21)
    w2 = gweight * t21 / (1.0 + t21)
    e1 = gidx * EXPERTS_PER_GROUP + i1
    e2 = gidx * EXPERTS_PER_GROUP + i2

    row_e = lax.broadcasted_iota(jnp.int32, (N_EXPERTS, tr), 0)
    sel1 = row_e == e1
    sel2 = row_e == e2
    onehot = jnp.where(sel1 | sel2, 1.0, 0.0)
    r_t = lax.broadcasted_iota(jnp.int32, (tr, tr), 0)
    c_t = lax.broadcasted_iota(jnp.int32, (tr, tr), 1)
    before = (r_t < c_t).astype(BF16)
    running = count_ref[:, 0:1] + _dot(onehot.astype(BF16), before)
    rank1 = jnp.sum(jnp.where(sel1, running, 0.0), axis=0, keepdims=True)
    rank2 = jnp.sum(jnp.where(sel2, running, 0.0), axis=0, keepdims=True)
    new_count = count_ref[:, 0:1] + jnp.sum(onehot, axis=1, keepdims=True)
    count_ref[...] = jnp.broadcast_to(new_count, count_ref.shape)
    cnt_ref[...] = jnp.broadcast_to(new_count, cnt_ref.shape)

    ri_ref[...] = jnp.where(row8 == 0, e1, jnp.where(row8 == 1, e2, jnp.where(
        row8 == 2, rank1.astype(jnp.int32), jnp.where(row8 == 3, rank2.astype(jnp.int32), 0))))
    row128 = lax.broadcasted_iota(jnp.int32, (LANES, tr), 0)
    rw_ref[...] = jnp.where(row128 == 0, w1, jnp.where(row128 == 1, w2, 0.0)).T


def _route(lg):
    n = lg.shape[1]
    return pl.pallas_call(
        _route_kernel,
        grid=(n // TM_ROUTE,),
        in_specs=[pl.BlockSpec((ROUTER_ROWS, TM_ROUTE), lambda i: (0, i))],
        out_specs=[pl.BlockSpec((SUBLANES, TM_ROUTE), lambda i: (0, i)),
                   pl.BlockSpec((TM_ROUTE, LANES), lambda i: (i, 0)),
                   pl.BlockSpec((N_EXPERTS, LANES), lambda i: (0, 0))],
        out_shape=[jax.ShapeDtypeStruct((SUBLANES, n), jnp.int32),
                   jax.ShapeDtypeStruct((n, LANES), F32),
                   jax.ShapeDtypeStruct((N_EXPERTS, LANES), F32)],
        scratch_shapes=[pltpu.VMEM((N_EXPERTS, LANES), F32)],
        compiler_params=pltpu.CompilerParams(dimension_semantics=("arbitrary",),
                                             vmem_limit_bytes=VMEM_LIMIT),
        name="route",
    )(lg)


def _mix(sb, sgn, x2, sb_g, w_out_b, ffn_g, wr2, br):
    n = x2.shape[0]
    row = lambda i: (i, 0)
    const = lambda i: (0, 0)
    return pl.pallas_call(
        _mix_kernel,
        grid=(n // TM_MIX,),
        in_specs=[pl.BlockSpec((TM_MIX, SB_WIDTH), row),
                  pl.BlockSpec((TM_MIX, SG_WIDTH), row),
                  pl.BlockSpec((TM_MIX, D_MODEL), row),
                  pl.BlockSpec((1, SB_WIDTH), const),
                  pl.BlockSpec((D_MODEL, D_MODEL), const),
                  pl.BlockSpec((1, D_MODEL), const),
                  pl.BlockSpec((D_MODEL, 2 * LANES), const),
                  pl.BlockSpec((1, LANES), const)],
        out_specs=[pl.BlockSpec((TM_MIX, D_MODEL), row),
                   pl.BlockSpec((ROUTER_ROWS, TM_MIX), lambda i: (0, i))],
        out_shape=[jax.ShapeDtypeStruct((n, D_MODEL), F32),
                   jax.ShapeDtypeStruct((ROUTER_ROWS, n), F32)],
        compiler_params=pltpu.CompilerParams(dimension_semantics=("arbitrary",),
                                             vmem_limit_bytes=VMEM_LIMIT),
        name="mix_router",
    )(sb, sgn, x2, sb_g, w_out_b, ffn_g, wr2, br)


_PAD_BITS = tuple(1 << b for b in reversed(range(EXPERT_CHUNK.bit_length() - 1)))


def _dispatch_kernel(dest_ref, pad_start_ref, pad_count_ref, used_ref, h_ref, g_ref, zeros_ref, xs_ref,
                     hn_ref, sem, zsem):
    tm = TM_DISPATCH
    i = pl.program_id(0)
    n_steps = pl.num_programs(0) - 1
    n = n_steps * tm
    base = (i - 1) * tm
    prev = hn_ref.at[lax.rem(i + 1, 2)]
    n_chunks = xs_ref.shape[0] // (EXPERT_CHUNK * ROW_TILE)

    def pad_copies(do):
        for e in range(N_EXPERTS):
            start = pad_start_ref[e]
            count = pad_count_ref[e]
            for bit in _PAD_BITS:
                @pl.when((count & bit) != 0)
                def _(start=start, bit=bit):
                    do(pltpu.make_async_copy(_token_rows(zeros_ref, 0, bit),
                                             _token_rows(xs_ref, start, bit), zsem))
                start = start + (count & bit)
        for k in range(N_EXPERTS):
            chunk = used_ref[0] + k

            @pl.when(chunk < n_chunks)
            def _(chunk=chunk):
                do(pltpu.make_async_copy(zeros_ref, _token_rows(xs_ref, chunk * EXPERT_CHUNK, EXPERT_CHUNK),
                                         zsem))

    @pl.when(i == 0)
    def _():
        pad_copies(lambda cp: cp.start())

    @pl.when(i > 0)
    def _():
        def body(r, c):
            src = _token_rows(prev, r, 1)
            for s in range(2):
                pltpu.make_async_copy(src, _token_rows(xs_ref, dest_ref[s * n + base + r], 1),
                                      sem).start(priority=s)
            return c

        lax.fori_loop(0, tm, body, 0, unroll=16)

    @pl.when(i < n_steps)
    def _():
        _rows_to_tiles(hn_ref.at[lax.rem(i, 2)], _rms(h_ref[...], g_ref[...]))

    @pl.when(i > 0)
    def _():
        for _ in range(2):
            pltpu.make_async_copy(prev, _token_rows(xs_ref, 0, tm), sem).wait()

    @pl.when(i == n_steps)
    def _():
        pad_copies(lambda cp: cp.wait())


def _dispatch(dest, pad_start, pad_count, used_chunks, h, ffn_g, n_rows):
    n_steps = h.shape[0] // TM_DISPATCH
    zeros = jnp.zeros((EXPERT_CHUNK * ROW_TILE, LANES), F32)
    return pl.pallas_call(
        _dispatch_kernel,
        grid_spec=pltpu.PrefetchScalarGridSpec(
            num_scalar_prefetch=4,
            grid=(n_steps + 1,),
            in_specs=[pl.BlockSpec((TM_DISPATCH, D_MODEL), lambda i, *_: (jnp.minimum(i, n_steps - 1), 0)),
                      pl.BlockSpec((1, D_MODEL), lambda i, *_: (0, 0)),
                      pl.BlockSpec(memory_space=pl.ANY)],
            out_specs=pl.BlockSpec(memory_space=pl.ANY),
            scratch_shapes=[pltpu.VMEM((2, TM_DISPATCH * ROW_TILE, LANES), F32),
                            pltpu.SemaphoreType.DMA, pltpu.SemaphoreType.DMA]),
        out_shape=jax.ShapeDtypeStruct((n_rows * ROW_TILE, LANES), F32),
        compiler_params=pltpu.CompilerParams(dimension_semantics=("arbitrary",),
                                             vmem_limit_bytes=VMEM_LIMIT),
        name="dispatch",
    )(dest, pad_start, pad_count, used_chunks, h, ffn_g, zeros)


X_SLOTS = 3
TILE_CHUNKS = TM_EXPERT // EXPERT_CHUNK
W_SLOTS = 3


def _expert_kernel(tiles_ref, chunk0_ref, chunks_ref, nt_ref, used_ref, xs_ref, wg_ref, wu_ref, wd_ref,
                   zeros_ref, ys_ref, x_buf, y_buf, sg_buf, su_buf, sd_buf, wgb, wub, wdb, state,
                   w_sems, x_sems, y_sems, zsem):
    t = pl.program_id(0)
    last = pl.num_programs(0) - 1
    nt = nt_ref[0]
    n_chunks = ys_ref.shape[0] // (EXPERT_CHUNK * ROW_TILE)

    def tile_copies(tile, do, out):
        for c in range(TILE_CHUNKS):
            @pl.when(c < chunks_ref[tile])
            def _(c=c):
                first = (chunk0_ref[tile] + c) * EXPERT_CHUNK
                if out:
                    slot = lax.rem(tile, 2)
                    do(pltpu.make_async_copy(_token_rows(y_buf.at[slot], c * EXPERT_CHUNK, EXPERT_CHUNK),
                                             _token_rows(ys_ref, first, EXPERT_CHUNK), y_sems.at[slot]))
                else:
                    slot = lax.rem(tile, X_SLOTS)
                    do(pltpu.make_async_copy(_token_rows(xs_ref, first, EXPERT_CHUNK),
                                             _token_rows(x_buf.at[slot], c * EXPERT_CHUNK, EXPERT_CHUNK),
                                             x_sems.at[slot]))

    start = lambda cp: cp.start()
    wait = lambda cp: cp.wait()

    def tail_copies(do):
        for k in range(N_EXPERTS):
            chunk = used_ref[0] + k

            @pl.when(chunk < n_chunks)
            def _(chunk=chunk):
                do(pltpu.make_async_copy(zeros_ref, _token_rows(ys_ref, chunk * EXPERT_CHUNK, EXPERT_CHUNK),
                                         zsem))

    def weight_copies(e, slot):
        return (pltpu.make_async_copy(wg_ref.at[e], sg_buf.at[slot], w_sems.at[slot]),
                pltpu.make_async_copy(wu_ref.at[e], su_buf.at[slot], w_sems.at[slot]),
                pltpu.make_async_copy(wd_ref.at[e], sd_buf.at[slot], w_sems.at[slot]))

    def next_with_rows(e):
        return lax.while_loop(lambda k: (k < N_EXPERTS) & (tiles_ref[jnp.minimum(k, N_EXPERTS - 1)] == 0),
                              lambda k: k + 1, e + 1)

    @pl.when(t == 0)
    def _():
        first = next_with_rows(jnp.int32(-1))
        second = next_with_rows(first)
        state[0] = jnp.int32(-1)
        state[1] = jnp.int32(0)
        state[2] = jnp.int32(W_SLOTS - 1)
        state[3] = first
        state[4] = second
        for cp in weight_copies(first, 0):
            cp.start()

        @pl.when(second < N_EXPERTS)
        def _():
            for cp in weight_copies(second, 1):
                cp.start()

        tile_copies(0, start, False)

        @pl.when(nt > 1)
        def _():
            tile_copies(1, start, False)

        tail_copies(start)

    @pl.when(t + 2 < nt)
    def _():
        tile_copies(t + 2, start, False)

    @pl.when(t < nt)
    def _():
        @pl.when(state[1] == 0)
        def _():
            e = state[3]
            nxt = state[4]
            slot = lax.rem(state[2] + 1, W_SLOTS)
            after_next = next_with_rows(nxt)
            state[0] = e
            state[1] = tiles_ref[e]
            state[2] = slot
            state[3] = nxt
            state[4] = after_next
            for cp in weight_copies(e, slot):
                cp.wait()

            @pl.when(after_next < N_EXPERTS)
            def _():
                for cp in weight_copies(after_next, lax.rem(slot + 2, W_SLOTS)):
                    cp.start()

            wgb[...] = sg_buf[slot].astype(BF16)
            wub[...] = su_buf[slot].astype(BF16)
            wdb[...] = sd_buf[slot].astype(BF16)

        state[1] = state[1] - 1
        tile_copies(t, wait, False)

        @pl.when(t >= 2)
        def _():
            tile_copies(t - 2, wait, True)

        for n_chunks_here in range(1, TILE_CHUNKS + 1):
            @pl.when(chunks_ref[t] == n_chunks_here)
            def _(m=n_chunks_here * EXPERT_CHUNK):
                x = _tiles_to_rows(x_buf.at[lax.rem(t, X_SLOTS)], m).astype(BF16)
                g = _dot(x, wgb[...])
                u = _dot(x, wub[...])
                hidden = (g * jax.nn.sigmoid(g)) * u
                _rows_to_tiles(y_buf.at[lax.rem(t, 2)], _dot(hidden.astype(BF16), wdb[...]))

        tile_copies(t, start, True)

    @pl.when(t == last)
    def _():
        for back in (2, 1):
            @pl.when(nt >= back)
            def _(back=back):
                tile_copies(nt - back, wait, True)

        tail_copies(wait)


def _experts(tiles, chunk0, chunks, n_tiles, used_chunks, xs, wg, wu, wd):
    any_spec = pl.BlockSpec(memory_space=pl.ANY)
    zeros = jnp.zeros((EXPERT_CHUNK * ROW_TILE, LANES), F32)
    return pl.pallas_call(
        _expert_kernel,
        grid_spec=pltpu.PrefetchScalarGridSpec(
            num_scalar_prefetch=5,
            grid=(chunks.shape[0],),
            in_specs=[any_spec, any_spec, any_spec, any_spec, any_spec],
            out_specs=any_spec,
            scratch_shapes=[pltpu.VMEM((X_SLOTS, TM_EXPERT * ROW_TILE, LANES), F32),
                            pltpu.VMEM((2, TM_EXPERT * ROW_TILE, LANES), F32),
                            pltpu.VMEM((W_SLOTS, D_MODEL, D_EXPERT), F32),
                            pltpu.VMEM((W_SLOTS, D_MODEL, D_EXPERT), F32),
                            pltpu.VMEM((W_SLOTS, D_EXPERT, D_MODEL), F32),
                            pltpu.VMEM((D_MODEL, D_EXPERT), BF16),
                            pltpu.VMEM((D_MODEL, D_EXPERT), BF16),
                            pltpu.VMEM((D_EXPERT, D_MODEL), BF16),
                            pltpu.SMEM((5,), jnp.int32),
                            pltpu.SemaphoreType.DMA((W_SLOTS,)),
                            pltpu.SemaphoreType.DMA((X_SLOTS,)),
                            pltpu.SemaphoreType.DMA((2,)),
                            pltpu.SemaphoreType.DMA]),
        out_shape=jax.ShapeDtypeStruct(xs.shape, F32),
        compiler_params=pltpu.CompilerParams(dimension_semantics=("arbitrary",),
                                             vmem_limit_bytes=VMEM_LIMIT),
        name="expert_mlp",
    )(tiles, chunk0, chunks, n_tiles, used_chunks, xs, wg, wu, wd, zeros)


def _combine_kernel(dest_ref, rw_ref, fg_ref, h_ref, y_ref, o_ref, buf, h_buf, sems, h_sems):
    tm = TM_COMBINE
    i = pl.program_id(0)
    n_steps = pl.num_programs(0)
    n = n_steps * tm
    cur = i % 2

    def h_copy(step, half):
        return pltpu.make_async_copy(h_ref.at[pl.ds(pl.multiple_of(step * tm, tm), tm)], h_buf.at[half],
                                     h_sems.at[half])

    def fetch(step, half):
        h_copy(step, half).start(priority=1)

        def body(r, c):
            for s in range(2):
                pltpu.make_async_copy(_token_rows(y_ref, dest_ref[s * n + step * tm + r], 1),
                                      _token_rows(buf.at[half, s], r, 1),
                                      sems.at[half]).start(priority=s)
            return c

        lax.fori_loop(0, tm, body, 0, unroll=16)

    @pl.when(i == 0)
    def _():
        fetch(0, 0)

    @pl.when(i + 1 < n_steps)
    def _():
        fetch(i + 1, 1 - cur)

    for s in range(2):
        pltpu.make_async_copy(_token_rows(y_ref, 0, tm), buf.at[cur, s], sems.at[cur]).wait()
    h_copy(i, cur).wait()
    rw = rw_ref[...]
    out = (h_buf[cur] + rw[:, 0:1] * _tiles_to_rows(buf.at[cur, 0], tm)
           + rw[:, 1:2] * _tiles_to_rows(buf.at[cur, 1], tm))
    o_ref[...] = _rms(out, fg_ref[...])


def _combine(dest, h, rw, final_g, ys):
    n = h.shape[0]
    return pl.pallas_call(
        _combine_kernel,
        grid_spec=pltpu.PrefetchScalarGridSpec(
            num_scalar_prefetch=1,
            grid=(n // TM_COMBINE,),
            in_specs=[pl.BlockSpec((TM_COMBINE, LANES), lambda i, d: (i, 0)),
                      pl.BlockSpec((1, D_MODEL), lambda i, d: (0, 0)),
                      pl.BlockSpec(memory_space=pl.ANY),
                      pl.BlockSpec(memory_space=pl.ANY)],
            out_specs=pl.BlockSpec((TM_COMBINE, D_MODEL), lambda i, d: (i, 0)),
            scratch_shapes=[pltpu.VMEM((2, 2, TM_COMBINE * ROW_TILE, LANES), F32),
                            pltpu.VMEM((2, TM_COMBINE, D_MODEL), F32),
                            pltpu.SemaphoreType.DMA((2,)),
                            pltpu.SemaphoreType.DMA((2,))]),
        out_shape=jax.ShapeDtypeStruct((n, D_MODEL), F32),
        compiler_params=pltpu.CompilerParams(dimension_semantics=("arbitrary",),
                                             vmem_limit_bytes=VMEM_LIMIT),
        name="combine",
    )(dest, rw, final_g, h, ys)


def _schedule(counts, max_tiles):
    chunks = (counts + EXPERT_CHUNK - 1) // EXPERT_CHUNK
    chunk_end = jnp.cumsum(chunks)
    chunk_start = chunk_end - chunks
    tiles = (chunks + TILE_CHUNKS - 1) // TILE_CHUNKS
    tile_end = jnp.cumsum(tiles)
    tile = jnp.arange(max_tiles, dtype=jnp.int32)
    owner = jnp.sum(tile[:, None] >= tile_end[None, :], axis=1)
    is_owner = owner[:, None] == jnp.arange(N_EXPERTS, dtype=jnp.int32)[None, :]
    of_owner = lambda v: jnp.sum(jnp.where(is_owner, v[None, :], 0), axis=1)
    done = (tile - of_owner(tile_end - tiles)) * TILE_CHUNKS
    tile_chunk0 = (of_owner(chunk_start) + done).astype(jnp.int32)
    tile_chunks = jnp.clip(of_owner(chunks) - done, 0, TILE_CHUNKS).astype(jnp.int32)
    return tiles, chunk_start * EXPERT_CHUNK, tile_chunk0, tile_chunks, tile_end[-1:], chunk_end[-1:]


def _layer(x, attn_g, w_in, sg_g, w_sp, b_sp, sb_g, sg_out_g, w_out, ffn_g,
           w_rg, b_rg, w_re, b_re, w_gate, w_up, w_down):
    batch, seq, _ = x.shape
    n = batch * seq
    x2 = x.reshape(n, D_MODEL)
    row = lambda v: v.reshape(1, -1)

    bsp_full = jnp.repeat(b_sp.T, HEAD_DIM, axis=1)
    qkv, sgn = _inproj(x2, row(attn_g), w_in.astype(BF16), row(sg_g), w_sp, bsp_full, row(sg_out_g))
    sb = _attention(qkv, batch, seq).reshape(n, SB_WIDTH)

    pad_lanes = lambda v, width: jnp.pad(v, [(0, 0)] * (v.ndim - 1) + [(0, width - v.shape[-1])])
    w_r = jnp.concatenate([pad_lanes(w_rg, ROUTER_LANE0),
                           jnp.transpose(w_re, (1, 0, 2)).reshape(D_MODEL, N_EXPERTS)], axis=1)
    w_r = pad_lanes(w_r, LANES)
    wr_hi = w_r.astype(BF16)
    wr_lo = (w_r - wr_hi.astype(F32)).astype(BF16)
    wr2 = jnp.concatenate([wr_hi, wr_lo], axis=1)
    b_r = pad_lanes(jnp.concatenate([pad_lanes(b_rg, ROUTER_LANE0), b_re.reshape(-1)]), LANES)

    h, lg = _mix(sb, sgn, x2, row(sb_g), w_out.astype(BF16), row(ffn_g), wr2, row(b_r))
    ri, rw, cnt = _route(lg)

    counts = cnt[:, 0].astype(jnp.int32)
    n_rows = 2 * n + N_EXPERTS * EXPERT_CHUNK
    tiles, offsets, tile_chunk0, tile_chunks, n_tiles, used_chunks = _schedule(
        counts, 2 * n // TM_EXPERT + N_EXPERTS)
    expert, rank = ri[0:2], ri[2:4]
    is_e = expert[None] == jnp.arange(N_EXPERTS, dtype=jnp.int32)[:, None, None]
    dest = (jnp.sum(jnp.where(is_e, offsets[:, None, None], 0), axis=0) + rank).reshape(-1)
    pad_start = offsets + counts
    pad_count = (-counts) % EXPERT_CHUNK

    xs = _dispatch(dest, pad_start, pad_count, used_chunks, h, row(ffn_g), n_rows)
    ys = _experts(tiles, tile_chunk0, tile_chunks, n_tiles, used_chunks, xs,
                  w_gate.reshape(N_EXPERTS, D_MODEL, D_EXPERT),
                  w_up.reshape(N_EXPERTS, D_MODEL, D_EXPERT),
                  w_down.reshape(N_EXPERTS, D_EXPERT, D_MODEL))
    return dest, h, rw, ys


def kernel(x, attn_norm_g, w_in, sg_norm_g, w_spatial, b_spatial, sb_out_norm_g, sg_out_norm_g,
           w_out, ffn_norm_g, w_router_group, b_router_group, w_router_expert, b_router_expert,
           w_gate, w_up, w_down, final_norm_g):
    assert attn_norm_g.shape[0] == 1, "single-layer problem"
    batch, seq, _ = x.shape
    dest, h, rw, ys = _layer(x, attn_norm_g[0], w_in[0], sg_norm_g[0], w_spatial[0], b_spatial[0],
                             sb_out_norm_g[0], sg_out_norm_g[0], w_out[0], ffn_norm_g[0],
                             w_router_group[0], b_router_group[0], w_router_expert[0],
                             b_router_expert[0], w_gate[0], w_up[0], w_down[0])
    out = _combine(dest, h, rw, final_norm_g.reshape(1, -1), ys)
    return out.reshape(batch, seq, D_MODEL)
```
